```python
import jax, jax.numpy as jnp
from jax import lax
import numpy as np

D_MODEL = 2048
BATCH = 8
SEQ = 2048
DEPTH = 1

N_META = 16
D_MIX = D_MODEL
D_LRU = D_MIX // 2
D_CONF = D_MIX - D_LRU
LRU_HEADS = 16
LRU_HEAD_DIM = D_LRU // LRU_HEADS
CONF_GROUPS = 16
LRU_CONV_WIDTH = 4
CONF_KERNEL = 31
LRU_C = 8.0
EPS = 1e-6
IN_COLS = 2 * D_LRU + 3 * D_CONF

kernel_name = "hymba_style_rglru_conformer_hybrid"


def rms_norm(x, w):
    xf = x.astype(jnp.float32)
    var = jnp.mean(xf * xf, axis=-1, keepdims=True)
    return (xf * lax.rsqrt(var + EPS) * w.astype(jnp.float32)).astype(x.dtype)


def layer_norm(x, w, b):
    xf = x.astype(jnp.float32)
    mu = jnp.mean(xf, axis=-1, keepdims=True)
    var = jnp.mean(jnp.square(xf - mu), axis=-1, keepdims=True)
    y = (xf - mu) * lax.rsqrt(var + EPS) * w.astype(jnp.float32) + b.astype(jnp.float32)
    return y.astype(x.dtype)


def causal_depthwise_conv(x, w, b):
    k, c = w.shape
    y = lax.conv_general_dilated(
        x, w.reshape(k, 1, c).astype(x.dtype),
        window_strides=(1,), padding=[(k - 1, 0)],
        dimension_numbers=("NWC", "WIO", "NWC"),
        feature_group_count=c)
    return y + b.astype(x.dtype)


def rg_lru(x, w_a, b_a, w_x, b_x, lam):
    bsz, t, _ = x.shape
    xh = x.reshape(bsz, t, LRU_HEADS, LRU_HEAD_DIM)
    r = jax.nn.sigmoid(jnp.einsum("bthi,hij->bthj", xh, w_a).reshape(bsz, t, D_LRU) + b_a)
    i = jax.nn.sigmoid(jnp.einsum("bthi,hij->bthj", xh, w_x).reshape(bsz, t, D_LRU) + b_x)
    log_a = -LRU_C * r.astype(jnp.float32) * jax.nn.softplus(-lam.astype(jnp.float32))
    a = jnp.exp(log_a)
    mult = jnp.sqrt(-jnp.expm1(2.0 * log_a))
    u = mult * (i * x).astype(jnp.float32)

    def combine(left, right):
        a1, b1 = left
        a2, b2 = right
        return a1 * a2, a2 * b1 + b2

    _, h = lax.associative_scan(combine, (a, u), axis=1)
    return h.astype(x.dtype)


def conformer_conv(u, dw_w, dw_b, ln_w, ln_b, pw_w, pw_b):
    v = u[..., :D_CONF] * jax.nn.sigmoid(u[..., D_CONF:])
    v = causal_depthwise_conv(v, dw_w, dw_b)
    v = jax.nn.silu(layer_norm(v, ln_w, ln_b))
    return v @ pw_w + pw_b


def _fwd_setup_inputs(seed: int = 0) -> dict:
    key = jax.random.key(seed)
    ks = jax.random.split(key, 24)
    f32 = jnp.float32
    n = lambda k, shape, s: jax.random.normal(k, shape, f32) * s
    x = n(ks[0], (BATCH, SEQ, D_MODEL), 1.0)
    meta_tokens = n(ks[1], (N_META, D_MODEL), 1.0)
    pre_norm_w = 1.0 + n(ks[2], (DEPTH, D_MODEL), 0.02)
    post_norm_w = 1.0 + n(ks[3], (DEPTH, D_MODEL), 0.02)
    w_in = n(ks[4], (DEPTH, D_MODEL, IN_COLS), D_MODEL ** -0.5)
    b_in = n(ks[5], (DEPTH, IN_COLS), 0.01)
    lru_conv_w = n(ks[6], (DEPTH, LRU_CONV_WIDTH, D_LRU), LRU_CONV_WIDTH ** -0.5)
    lru_conv_b = n(ks[7], (DEPTH, D_LRU), 0.01)
    w_gate_a = n(ks[8], (DEPTH, LRU_HEADS, LRU_HEAD_DIM, LRU_HEAD_DIM), LRU_HEAD_DIM ** -0.5)
    b_gate_a = n(ks[9], (DEPTH, D_LRU), 0.01)
    w_gate_x = n(ks[10], (DEPTH, LRU_HEADS, LRU_HEAD_DIM, LRU_HEAD_DIM), LRU_HEAD_DIM ** -0.5)
    b_gate_x = n(ks[11], (DEPTH, D_LRU), 0.01)
    a_c = jax.random.uniform(ks[12], (DEPTH, D_LRU), f32, 0.9, 0.999)
    a0 = a_c ** (1.0 / LRU_C)
    lru_lambda = jnp.log(a0) - jnp.log1p(-a0)
    conf_dw_w = n(ks[13], (DEPTH, CONF_KERNEL, D_CONF), CONF_KERNEL ** -0.5)
    conf_dw_b = n(ks[14], (DEPTH, D_CONF), 0.01)
    conf_ln_w = 1.0 + n(ks[15], (DEPTH, D_CONF), 0.02)
    conf_ln_b = n(ks[16], (DEPTH, D_CONF), 0.01)
    conf_pw_w = n(ks[17], (DEPTH, D_CONF, D_CONF), D_CONF ** -0.5)
    conf_pw_b = n(ks[18], (DEPTH, D_CONF), 0.01)
    w_out = n(ks[19], (DEPTH, D_MIX, D_MODEL), D_MIX ** -0.5)
    return {
        "x": x, "meta_tokens": meta_tokens,
        "pre_norm_w": pre_norm_w, "post_norm_w": post_norm_w,
        "w_in": w_in, "b_in": b_in,
        "lru_conv_w": lru_conv_w, "lru_conv_b": lru_conv_b,
        "w_gate_a": w_gate_a, "b_gate_a": b_gate_a,
        "w_gate_x": w_gate_x, "b_gate_x": b_gate_x,
        "lru_lambda": lru_lambda,
        "conf_dw_w": conf_dw_w, "conf_dw_b": conf_dw_b,
        "conf_ln_w": conf_ln_w, "conf_ln_b": conf_ln_b,
        "conf_pw_w": conf_pw_w, "conf_pw_b": conf_pw_b,
        "w_out": w_out,
    }


def _fwd_reference(x, meta_tokens, pre_norm_w, post_norm_w, w_in, b_in,
              lru_conv_w, lru_conv_b, w_gate_a, b_gate_a, w_gate_x, b_gate_x,
              lru_lambda, conf_dw_w, conf_dw_b, conf_ln_w, conf_ln_b,
              conf_pw_w, conf_pw_b, w_out):
    bsz = x.shape[0]
    meta = jnp.broadcast_to(meta_tokens.astype(x.dtype)[None], (bsz, N_META, D_MODEL))
    h = jnp.concatenate([meta, x], axis=1)
    for l in range(DEPTH):
        hn = rms_norm(h, pre_norm_w[l])
        z = hn @ w_in[l] + b_in[l]
        x_lru = z[..., :D_LRU]
        g_lru = z[..., D_LRU:2 * D_LRU]
        u_conf = z[..., 2 * D_LRU:2 * D_LRU + 2 * D_CONF]
        g_conf = z[..., 2 * D_LRU + 2 * D_CONF:]
        xc = causal_depthwise_conv(x_lru, lru_conv_w[l], lru_conv_b[l])
        y_lru = rg_lru(xc, w_gate_a[l], b_gate_a[l], w_gate_x[l], b_gate_x[l],
                       lru_lambda[l]) * jax.nn.silu(g_lru)
        y_conf = conformer_conv(u_conf, conf_dw_w[l], conf_dw_b[l], conf_ln_w[l],
                                conf_ln_b[l], conf_pw_w[l], conf_pw_b[l]) * jax.nn.silu(g_conf)
        y = jnp.concatenate([y_lru, y_conf], axis=-1) @ w_out[l]
        h = h + rms_norm(y, post_norm_w[l])
    return h[:, N_META:]


import jax as _jax
import jax.numpy as _jnp

TWIN_FORMAT = 'train_step'
FWD_PARAMS = ['x', 'meta_tokens', 'pre_norm_w', 'post_norm_w', 'w_in', 'b_in', 'lru_conv_w', 'lru_conv_b', 'w_gate_a', 'b_gate_a', 'w_gate_x', 'b_gate_x', 'lru_lambda', 'conf_dw_w', 'conf_dw_b', 'conf_ln_w', 'conf_ln_b', 'conf_pw_w', 'conf_pw_b', 'w_out']
TWIN_WEIGHTS = ['meta_tokens', 'pre_norm_w', 'post_norm_w', 'w_in', 'b_in', 'lru_conv_w', 'lru_conv_b', 'w_gate_a', 'b_gate_a', 'w_gate_x', 'b_gate_x', 'lru_lambda', 'conf_dw_w', 'conf_dw_b', 'conf_ln_w', 'conf_ln_b', 'conf_pw_w', 'conf_pw_b', 'w_out']
TWIN_DIFF_INPUT = 'x'
TWIN_INPUTS = ['x', 'meta_tokens', 'pre_norm_w', 'post_norm_w', 'w_in', 'b_in', 'lru_conv_w', 'lru_conv_b', 'w_gate_a', 'b_gate_a', 'w_gate_x', 'b_gate_x', 'lru_lambda', 'conf_dw_w', 'conf_dw_b', 'conf_ln_w', 'conf_ln_b', 'conf_pw_w', 'conf_pw_b', 'w_out', 'loss_target', 'm_meta_tokens', 'm_pre_norm_w', 'm_post_norm_w', 'm_w_in', 'm_b_in', 'm_lru_conv_w', 'm_lru_conv_b', 'm_w_gate_a', 'm_b_gate_a', 'm_w_gate_x', 'm_b_gate_x', 'm_lru_lambda', 'm_conf_dw_w', 'm_conf_dw_b', 'm_conf_ln_w', 'm_conf_ln_b', 'm_conf_pw_w', 'm_conf_pw_b', 'm_w_out', 'v_meta_tokens', 'v_pre_norm_w', 'v_post_norm_w', 'v_w_in', 'v_b_in', 'v_lru_conv_w', 'v_lru_conv_b', 'v_w_gate_a', 'v_b_gate_a', 'v_w_gate_x', 'v_b_gate_x', 'v_lru_lambda', 'v_conf_dw_w', 'v_conf_dw_b', 'v_conf_ln_w', 'v_conf_ln_b', 'v_conf_pw_w', 'v_conf_pw_b', 'v_w_out']
TWIN_OUTPUTS = ['loss', 'grad_x', 'grad_meta_tokens', 'grad_pre_norm_w', 'grad_post_norm_w', 'grad_w_in', 'grad_b_in', 'grad_lru_conv_w', 'grad_lru_conv_b', 'grad_w_gate_a', 'grad_b_gate_a', 'grad_w_gate_x', 'grad_b_gate_x', 'grad_lru_lambda', 'grad_conf_dw_w', 'grad_conf_dw_b', 'grad_conf_ln_w', 'grad_conf_ln_b', 'grad_conf_pw_w', 'grad_conf_pw_b', 'grad_w_out', 'delta_meta_tokens', 'delta_pre_norm_w', 'delta_post_norm_w', 'delta_w_in', 'delta_b_in', 'delta_lru_conv_w', 'delta_lru_conv_b', 'delta_w_gate_a', 'delta_b_gate_a', 'delta_w_gate_x', 'delta_b_gate_x', 'delta_lru_lambda', 'delta_conf_dw_w', 'delta_conf_dw_b', 'delta_conf_ln_w', 'delta_conf_ln_b', 'delta_conf_pw_w', 'delta_conf_pw_b', 'delta_w_out', 'new_m_meta_tokens', 'new_m_pre_norm_w', 'new_m_post_norm_w', 'new_m_w_in', 'new_m_b_in', 'new_m_lru_conv_w', 'new_m_lru_conv_b', 'new_m_w_gate_a', 'new_m_b_gate_a', 'new_m_w_gate_x', 'new_m_b_gate_x', 'new_m_lru_lambda', 'new_m_conf_dw_w', 'new_m_conf_dw_b', 'new_m_conf_ln_w', 'new_m_conf_ln_b', 'new_m_conf_pw_w', 'new_m_conf_pw_b', 'new_m_w_out', 'new_v_meta_tokens', 'new_v_pre_norm_w', 'new_v_post_norm_w', 'new_v_w_in', 'new_v_b_in', 'new_v_lru_conv_w', 'new_v_lru_conv_b', 'new_v_w_gate_a', 'new_v_b_gate_a', 'new_v_w_gate_x', 'new_v_b_gate_x', 'new_v_lru_lambda', 'new_v_conf_dw_w', 'new_v_conf_dw_b', 'new_v_conf_ln_w', 'new_v_conf_ln_b', 'new_v_conf_pw_w', 'new_v_conf_pw_b', 'new_v_w_out']
TWIN_LEAF_KINDS = {'loss': 'loss', 'grad_x': 'grad_x', 'grad_meta_tokens': 'grad_w', 'grad_pre_norm_w': 'grad_w', 'grad_post_norm_w': 'grad_w', 'grad_w_in': 'grad_w', 'grad_b_in': 'grad_w', 'grad_lru_conv_w': 'grad_w', 'grad_lru_conv_b': 'grad_w', 'grad_w_gate_a': 'grad_w', 'grad_b_gate_a': 'grad_w', 'grad_w_gate_x': 'grad_w', 'grad_b_gate_x': 'grad_w', 'grad_lru_lambda': 'grad_w', 'grad_conf_dw_w': 'grad_w', 'grad_conf_dw_b': 'grad_w', 'grad_conf_ln_w': 'grad_w', 'grad_conf_ln_b': 'grad_w', 'grad_conf_pw_w': 'grad_w', 'grad_conf_pw_b': 'grad_w', 'grad_w_out': 'grad_w', 'delta_meta_tokens': 'delta_w', 'delta_pre_norm_w': 'delta_w', 'delta_post_norm_w': 'delta_w', 'delta_w_in': 'delta_w', 'delta_b_in': 'delta_w', 'delta_lru_conv_w': 'delta_w', 'delta_lru_conv_b': 'delta_w', 'delta_w_gate_a': 'delta_w', 'delta_b_gate_a': 'delta_w', 'delta_w_gate_x': 'delta_w', 'delta_b_gate_x': 'delta_w', 'delta_lru_lambda': 'delta_w', 'delta_conf_dw_w': 'delta_w', 'delta_conf_dw_b': 'delta_w', 'delta_conf_ln_w': 'delta_w', 'delta_conf_ln_b': 'delta_w', 'delta_conf_pw_w': 'delta_w', 'delta_conf_pw_b': 'delta_w', 'delta_w_out': 'delta_w', 'new_m_meta_tokens': 'new_m', 'new_m_pre_norm_w': 'new_m', 'new_m_post_norm_w': 'new_m', 'new_m_w_in': 'new_m', 'new_m_b_in': 'new_m', 'new_m_lru_conv_w': 'new_m', 'new_m_lru_conv_b': 'new_m', 'new_m_w_gate_a': 'new_m', 'new_m_b_gate_a': 'new_m', 'new_m_w_gate_x': 'new_m', 'new_m_b_gate_x': 'new_m', 'new_m_lru_lambda': 'new_m', 'new_m_conf_dw_w': 'new_m', 'new_m_conf_dw_b': 'new_m', 'new_m_conf_ln_w': 'new_m', 'new_m_conf_ln_b': 'new_m', 'new_m_conf_pw_w': 'new_m', 'new_m_conf_pw_b': 'new_m', 'new_m_w_out': 'new_m', 'new_v_meta_tokens': 'new_v', 'new_v_pre_norm_w': 'new_v', 'new_v_post_norm_w': 'new_v', 'new_v_w_in': 'new_v', 'new_v_b_in': 'new_v', 'new_v_lru_conv_w': 'new_v', 'new_v_lru_conv_b': 'new_v', 'new_v_w_gate_a': 'new_v', 'new_v_b_gate_a': 'new_v', 'new_v_w_gate_x': 'new_v', 'new_v_b_gate_x': 'new_v', 'new_v_lru_lambda': 'new_v', 'new_v_conf_dw_w': 'new_v', 'new_v_conf_dw_b': 'new_v', 'new_v_conf_ln_w': 'new_v', 'new_v_conf_ln_b': 'new_v', 'new_v_conf_pw_w': 'new_v', 'new_v_conf_pw_b': 'new_v', 'new_v_w_out': 'new_v'}


def _forward(args):
    return _fwd_reference(*[args[k] for k in FWD_PARAMS])


def _output_shape():
    out = _jax.eval_shape(lambda: _forward(_fwd_setup_inputs(0)))
    return out.shape, out.dtype

N_MICROBATCH = 1
ADAM_LR = 0.001
ADAM_B1 = 0.9
ADAM_B2 = 0.999
ADAM_EPS = 1e-08
ADAM_WD = 0.01
ADAM_STEP = 10
PER_EXAMPLE_BATCH_AXIS = {'x': 0, 'loss_target': 0}
SHARED_INPUTS = []
_WEIGHT_DTYPES = {'meta_tokens': _jnp.float32, 'pre_norm_w': _jnp.float32, 'post_norm_w': _jnp.float32, 'w_in': _jnp.float32, 'b_in': _jnp.float32, 'lru_conv_w': _jnp.float32, 'lru_conv_b': _jnp.float32, 'w_gate_a': _jnp.float32, 'b_gate_a': _jnp.float32, 'w_gate_x': _jnp.float32, 'b_gate_x': _jnp.float32, 'lru_lambda': _jnp.float32, 'conf_dw_w': _jnp.float32, 'conf_dw_b': _jnp.float32, 'conf_ln_w': _jnp.float32, 'conf_ln_b': _jnp.float32, 'conf_pw_w': _jnp.float32, 'conf_pw_b': _jnp.float32, 'w_out': _jnp.float32}
MOMENT_SCALE = {'meta_tokens': 6.026236e-03, 'pre_norm_w': 1.306829e-01, 'post_norm_w': 8.009388e+00, 'w_in': 8.495274e-02, 'b_in': 6.829076e-01, 'lru_conv_w': 1.095767e-01, 'lru_conv_b': 1.530970e+00, 'w_gate_a': 4.895680e-02, 'b_gate_a': 3.095945e-02, 'w_gate_x': 9.087593e-02, 'b_gate_x': 3.505140e-02, 'lru_lambda': 4.507614e-02, 'conf_dw_w': 9.338075e-02, 'conf_dw_b': 1.976784e-01, 'conf_ln_w': 1.072278e-01, 'conf_ln_b': 1.089497e-01, 'conf_pw_w': 9.109271e-02, 'conf_pw_b': 2.191899e-01, 'w_out': 1.022746e-01}


def _to_microbatches(a, axis):
    t = _jnp.moveaxis(a, axis, 0)
    t = t.reshape((N_MICROBATCH, t.shape[0] // N_MICROBATCH) + t.shape[1:])
    return _jnp.moveaxis(t, 1, axis + 1)


def setup_inputs(seed: int = 0) -> dict:
    inp = _fwd_setup_inputs(seed)
    key = _jax.random.fold_in(_jax.random.key(seed), 7919)
    shape, _ = _output_shape()
    out = dict(inp)
    out["loss_target"] = _jax.random.normal(_jax.random.fold_in(key, 0), shape, _jnp.float32)
    for i, name in enumerate(TWIN_WEIGHTS):
        w = inp[name].astype(_jnp.float32)
        if MOMENT_SCALE is None:
            s = _jnp.sqrt(_jnp.mean(_jnp.square(w)) + 1e-30)
        else:
            s = MOMENT_SCALE[name]
        km, kv = _jax.random.split(_jax.random.fold_in(key, i + 1))
        out[name] = w
        out["m_" + name] = s * _jax.random.normal(km, w.shape, _jnp.float32)
        out["v_" + name] = (s * s) * _jax.random.uniform(kv, w.shape, _jnp.float32, 0.5, 1.5)
    if N_MICROBATCH > 1:
        for name, axis in PER_EXAMPLE_BATCH_AXIS.items():
            out[name] = _to_microbatches(out[name], axis)
    return {'x': out['x'], 'meta_tokens': out['meta_tokens'], 'pre_norm_w': out['pre_norm_w'], 'post_norm_w': out['post_norm_w'], 'w_in': out['w_in'], 'b_in': out['b_in'], 'lru_conv_w': out['lru_conv_w'], 'lru_conv_b': out['lru_conv_b'], 'w_gate_a': out['w_gate_a'], 'b_gate_a': out['b_gate_a'], 'w_gate_x': out['w_gate_x'], 'b_gate_x': out['b_gate_x'], 'lru_lambda': out['lru_lambda'], 'conf_dw_w': out['conf_dw_w'], 'conf_dw_b': out['conf_dw_b'], 'conf_ln_w': out['conf_ln_w'], 'conf_ln_b': out['conf_ln_b'], 'conf_pw_w': out['conf_pw_w'], 'conf_pw_b': out['conf_pw_b'], 'w_out': out['w_out'], 'loss_target': out['loss_target'], 'm_meta_tokens': out['m_meta_tokens'], 'm_pre_norm_w': out['m_pre_norm_w'], 'm_post_norm_w': out['m_post_norm_w'], 'm_w_in': out['m_w_in'], 'm_b_in': out['m_b_in'], 'm_lru_conv_w': out['m_lru_conv_w'], 'm_lru_conv_b': out['m_lru_conv_b'], 'm_w_gate_a': out['m_w_gate_a'], 'm_b_gate_a': out['m_b_gate_a'], 'm_w_gate_x': out['m_w_gate_x'], 'm_b_gate_x': out['m_b_gate_x'], 'm_lru_lambda': out['m_lru_lambda'], 'm_conf_dw_w': out['m_conf_dw_w'], 'm_conf_dw_b': out['m_conf_dw_b'], 'm_conf_ln_w': out['m_conf_ln_w'], 'm_conf_ln_b': out['m_conf_ln_b'], 'm_conf_pw_w': out['m_conf_pw_w'], 'm_conf_pw_b': out['m_conf_pw_b'], 'm_w_out': out['m_w_out'], 'v_meta_tokens': out['v_meta_tokens'], 'v_pre_norm_w': out['v_pre_norm_w'], 'v_post_norm_w': out['v_post_norm_w'], 'v_w_in': out['v_w_in'], 'v_b_in': out['v_b_in'], 'v_lru_conv_w': out['v_lru_conv_w'], 'v_lru_conv_b': out['v_lru_conv_b'], 'v_w_gate_a': out['v_w_gate_a'], 'v_b_gate_a': out['v_b_gate_a'], 'v_w_gate_x': out['v_w_gate_x'], 'v_b_gate_x': out['v_b_gate_x'], 'v_lru_lambda': out['v_lru_lambda'], 'v_conf_dw_w': out['v_conf_dw_w'], 'v_conf_dw_b': out['v_conf_dw_b'], 'v_conf_ln_w': out['v_conf_ln_w'], 'v_conf_ln_b': out['v_conf_ln_b'], 'v_conf_pw_w': out['v_conf_pw_w'], 'v_conf_pw_b': out['v_conf_pw_b'], 'v_w_out': out['v_w_out']}


def _loss(weights, diff, rest, loss_target):
    with _jax.named_scope("forward"):
        args = {**rest, TWIN_DIFF_INPUT: diff, **{k: w.astype(_WEIGHT_DTYPES[k]) for k, w in weights.items()}}
        y = _forward(args)
    with _jax.named_scope("loss_head"):
        err = _jnp.square(y.astype(_jnp.float32) - loss_target)
        return 0.5 * _jnp.sum(_jnp.mean(err, axis=-1)) if err.ndim else 0.5 * err


def _adamw(w, g, m, v):
    m = ADAM_B1 * m + (1.0 - ADAM_B1) * g
    v = ADAM_B2 * v + (1.0 - ADAM_B2) * _jnp.square(g)
    m_hat = m / (1.0 - ADAM_B1 ** ADAM_STEP)
    v_hat = v / (1.0 - ADAM_B2 ** ADAM_STEP)
    delta = -ADAM_LR * (m_hat / (_jnp.sqrt(v_hat) + ADAM_EPS) + ADAM_WD * w)
    return delta, m, v


def reference(x, meta_tokens, pre_norm_w, post_norm_w, w_in, b_in, lru_conv_w, lru_conv_b, w_gate_a, b_gate_a, w_gate_x, b_gate_x, lru_lambda, conf_dw_w, conf_dw_b, conf_ln_w, conf_ln_b, conf_pw_w, conf_pw_b, w_out, loss_target, m_meta_tokens, m_pre_norm_w, m_post_norm_w, m_w_in, m_b_in, m_lru_conv_w, m_lru_conv_b, m_w_gate_a, m_b_gate_a, m_w_gate_x, m_b_gate_x, m_lru_lambda, m_conf_dw_w, m_conf_dw_b, m_conf_ln_w, m_conf_ln_b, m_conf_pw_w, m_conf_pw_b, m_w_out, v_meta_tokens, v_pre_norm_w, v_post_norm_w, v_w_in, v_b_in, v_lru_conv_w, v_lru_conv_b, v_w_gate_a, v_b_gate_a, v_w_gate_x, v_b_gate_x, v_lru_lambda, v_conf_dw_w, v_conf_dw_b, v_conf_ln_w, v_conf_ln_b, v_conf_pw_w, v_conf_pw_b, v_w_out):
    given = dict(x=x, meta_tokens=meta_tokens, pre_norm_w=pre_norm_w, post_norm_w=post_norm_w, w_in=w_in, b_in=b_in, lru_conv_w=lru_conv_w, lru_conv_b=lru_conv_b, w_gate_a=w_gate_a, b_gate_a=b_gate_a, w_gate_x=w_gate_x, b_gate_x=b_gate_x, lru_lambda=lru_lambda, conf_dw_w=conf_dw_w, conf_dw_b=conf_dw_b, conf_ln_w=conf_ln_w, conf_ln_b=conf_ln_b, conf_pw_w=conf_pw_w, conf_pw_b=conf_pw_b, w_out=w_out, loss_target=loss_target, m_meta_tokens=m_meta_tokens, m_pre_norm_w=m_pre_norm_w, m_post_norm_w=m_post_norm_w, m_w_in=m_w_in, m_b_in=m_b_in, m_lru_conv_w=m_lru_conv_w, m_lru_conv_b=m_lru_conv_b, m_w_gate_a=m_w_gate_a, m_b_gate_a=m_b_gate_a, m_w_gate_x=m_w_gate_x, m_b_gate_x=m_b_gate_x, m_lru_lambda=m_lru_lambda, m_conf_dw_w=m_conf_dw_w, m_conf_dw_b=m_conf_dw_b, m_conf_ln_w=m_conf_ln_w, m_conf_ln_b=m_conf_ln_b, m_conf_pw_w=m_conf_pw_w, m_conf_pw_b=m_conf_pw_b, m_w_out=m_w_out, v_meta_tokens=v_meta_tokens, v_pre_norm_w=v_pre_norm_w, v_post_norm_w=v_post_norm_w, v_w_in=v_w_in, v_b_in=v_b_in, v_lru_conv_w=v_lru_conv_w, v_lru_conv_b=v_lru_conv_b, v_w_gate_a=v_w_gate_a, v_b_gate_a=v_b_gate_a, v_w_gate_x=v_w_gate_x, v_b_gate_x=v_b_gate_x, v_lru_lambda=v_lru_lambda, v_conf_dw_w=v_conf_dw_w, v_conf_dw_b=v_conf_dw_b, v_conf_ln_w=v_conf_ln_w, v_conf_ln_b=v_conf_ln_b, v_conf_pw_w=v_conf_pw_w, v_conf_pw_b=v_conf_pw_b, v_w_out=v_w_out)
    weights = {n: given[n] for n in TWIN_WEIGHTS}
    shared = {n: given[n] for n in SHARED_INPUTS}
    per_example = {n: given[n] for n in ['x']}
    grad_fn = _jax.value_and_grad(_loss, argnums=(0, 1))

    def one_microbatch(ex, loss_target):
        ex = dict(ex)
        diff = ex.pop(TWIN_DIFF_INPUT)
        return grad_fn(weights, diff, {**shared, **ex}, loss_target)

    if N_MICROBATCH == 1:
        loss, (grad_w, grad_x) = one_microbatch(per_example, given["loss_target"])
    else:
        def body(carry, xs):
            loss_sum, grad_sum = carry
            l_k, (gw_k, gx_k) = one_microbatch(xs[0], xs[1])
            with _jax.named_scope("update"):
                return (loss_sum + l_k, _jax.tree.map(_jnp.add, grad_sum, gw_k)), gx_k

        init = (_jnp.zeros((), _jnp.float32), _jax.tree.map(_jnp.zeros_like, weights))
        (loss, grad_w), grad_x = _jax.lax.scan(body, init, (per_example, given["loss_target"]))
    with _jax.named_scope("update"):
        delta_w, new_m, new_v = {}, {}, {}
        for n in TWIN_WEIGHTS:
            delta_w[n], new_m[n], new_v[n] = _adamw(weights[n], grad_w[n], given["m_" + n], given["v_" + n])
    return (loss, grad_x, *[grad_w[n] for n in TWIN_WEIGHTS], *[delta_w[n] for n in TWIN_WEIGHTS],
            *[new_m[n] for n in TWIN_WEIGHTS], *[new_v[n] for n in TWIN_WEIGHTS])
```

```python
import functools

import jax
import jax.numpy as jnp
from jax import lax
from jax.experimental import pallas as pl
from jax.experimental.pallas import tpu as pltpu

F32 = jnp.float32
BF16 = jnp.bfloat16

D = 2048
DL = 1024
DC = 1024
NIN = 5120
NMETA = 16
SEQ = 2048
T = NMETA + SEQ
TP = 2176
TM = 544
CB = 256
NCB = DL // CB
R = 16
KW = 31
KWP = 32
LW = 4
LRU_C = 8.0
EPS = 1e-6
NDEV = 8

ADAM_LR = 0.001
ADAM_B1 = 0.9
ADAM_B2 = 0.999
ADAM_EPS = 1e-08
ADAM_WD = 0.01
ADAM_STEP = 10

VMEM_LIMIT = 56 * 1024 * 1024


def _cparams():
    return pltpu.CompilerParams(vmem_limit_bytes=VMEM_LIMIT)


def _sig(x):
    return 1.0 / (1.0 + jnp.exp(-x))


def _expm1_neg(y):
    poly = y * (1.0 + y * (0.5 + y * (1.0 / 6.0 + y * (1.0 / 24.0 + y * (1.0 / 120.0)))))
    return jnp.where(y > -0.1, poly, jnp.exp(y) - 1.0)


def _softplus(x):
    e = jnp.exp(-jnp.abs(x))
    w = 1.0 + e
    l1p = jnp.where(w == 1.0, e, jnp.log(w) * e / (w - 1.0))
    return jnp.maximum(x, 0.0) + l1p


def _row_iota(shape):
    return lax.broadcasted_iota(jnp.int32, shape, 0)


def _fold8(v):
    return v[0:8, :] + v[8:16, :]


def _exchange(name, items):
    n = len(items)
    flips = [(k >> 2 & 1, k >> 1 & 1, k & 1) for k in range(1, NDEV)]

    def body(*refs):
        ins = refs[:n]
        outs = refs[n:2 * n]
        send_sems, recv_sems, local_sems = refs[2 * n:]
        x, y, c = lax.axis_index("x"), lax.axis_index("y"), lax.axis_index("c")
        me = 4 * x + 2 * y + c

        local = [pltpu.make_async_copy(items[a][2](ins[a], me), items[a][3](outs[a], me), local_sems.at[a])
                 for a in range(n)]
        for cp in local:
            cp.start()
        remote = []
        for k, (dx, dy, dc) in enumerate(flips):
            px = 1 - x if dx else x
            py = 1 - y if dy else y
            pc = 1 - c if dc else c
            peer = 4 * px + 2 * py + pc
            for a in range(n):
                start = pltpu.make_async_remote_copy(
                    src_ref=items[a][2](ins[a], peer), dst_ref=items[a][3](outs[a], me),
                    send_sem=send_sems.at[a, k], recv_sem=recv_sems.at[a, k],
                    device_id=(px, py, pc), device_id_type=pl.DeviceIdType.MESH)
                wait = pltpu.make_async_remote_copy(
                    src_ref=items[a][2](ins[a], peer), dst_ref=items[a][3](outs[a], peer),
                    send_sem=send_sems.at[a, k], recv_sem=recv_sems.at[a, k],
                    device_id=(px, py, pc), device_id_type=pl.DeviceIdType.MESH)
                start.start()
                remote.append(wait)
        for cp in remote:
            cp.wait()
        for cp in local:
            cp.wait()

    any_spec = pl.BlockSpec(memory_space=pl.ANY)
    return pl.pallas_call(
        body, name=name,
        out_shape=[it[1] for it in items],
        in_specs=[any_spec] * n, out_specs=[any_spec] * n,
        scratch_shapes=[pltpu.SemaphoreType.DMA((n, NDEV - 1)), pltpu.SemaphoreType.DMA((n, NDEV - 1)),
                        pltpu.SemaphoreType.DMA((n,))],
    )(*[it[0] for it in items])


def _whole(ref, l):
    return ref


def _slot(ref, l):
    return ref.at[l]


def _cols(width):
    def at(ref, l):
        return ref.at[:, pl.ds(pl.multiple_of(l * width, 128), width)]
    return at


def _rows(height):
    def at(ref, l):
        return ref.at[pl.ds(pl.multiple_of(l * height, 8), height), :]
    return at


def _prenorm_inproj(h, pre_w, w_in, b_in):
    nsec = NIN // 1024

    def body(h_ref, pw_ref, w_ref, b_ref, z_ref, hn_ref):
        @pl.when(pl.program_id(1) == 0)
        def _():
            pw = pw_ref[...]

            def chunk(ci, carry):
                r0 = pl.multiple_of(ci * R, R)
                xv = h_ref[pl.ds(r0, R), :]
                ms = jnp.mean(xv * xv, axis=-1, keepdims=True)
                hn_ref[pl.ds(r0, R), :] = (xv * lax.rsqrt(ms + EPS) * pw).astype(BF16)
                return carry
            lax.fori_loop(0, TM // R, chunk, 0)

        z_ref[...] = jnp.dot(hn_ref[...], w_ref[...], preferred_element_type=F32) + b_ref[...]

    return pl.pallas_call(
        body, name="prenorm_inproj",
        grid=(TP // TM, nsec),
        in_specs=[pl.BlockSpec((TM, D), lambda i, n: (i, 0)),
                  pl.BlockSpec((1, D), lambda i, n: (0, 0)),
                  pl.BlockSpec((D, 1024), lambda i, n: (0, n)),
                  pl.BlockSpec((1, 1024), lambda i, n: (0, n))],
        out_specs=[pl.BlockSpec((TM, 1024), lambda i, n: (i, n)),
                   pl.BlockSpec((TM, D), lambda i, n: (i, 0))],
        out_shape=[jax.ShapeDtypeStruct((TP, NIN), F32), jax.ShapeDtypeStruct((TP, D), BF16)],
        compiler_params=_cparams(),
    )(h, pre_w, w_in, b_in)


def _gate_values(ga, gx, xc, sp8):
    r = _sig(ga)
    i = _sig(gx)
    log_a = -(r * sp8)
    a = jnp.exp(log_a)
    mult = jnp.sqrt(-_expm1_neg(2.0 * log_a))
    return r, i, a, mult


def _lru_fwd(z, conv_w, conv_b, wa_g, b_a, wx_g, b_x, lam):
    def body(x_ref, g_ref, cw_ref, cb_ref, wa_ref, ba_ref, wx_ref, bx_ref, lam_ref,
             y_ref, xc_ref, hs_ref, ga_s, gx_s):
        taps = [cw_ref[k:k + 1, :] for k in range(LW)]
        cb = cb_ref[...]

        def conv_chunk(ci, carry):
            r0 = pl.multiple_of(ci * R, R)
            cur = x_ref[pl.ds(r0, R), :]
            p0 = pl.multiple_of(jnp.maximum(r0 - 8, 0), 8)
            prev = jnp.where(ci > 0, x_ref[pl.ds(p0, 8), :], 0.0)
            buf = jnp.concatenate([prev, cur], axis=0)
            acc = cur * taps[LW - 1] + cb
            for s in range(1, LW):
                acc = acc + pltpu.roll(buf, s, 0)[8:8 + R, :] * taps[LW - 1 - s]
            xc_ref[pl.ds(r0, R), :] = acc
            return carry
        lax.fori_loop(0, TP // R, conv_chunk, 0)

        def gate_chunk(ci, carry):
            r0 = pl.multiple_of(ci * TM, TM)
            xb = xc_ref[pl.ds(r0, TM), :].astype(BF16)
            ga_s[pl.ds(r0, TM), :] = jnp.dot(xb, wa_ref[...], preferred_element_type=F32) + ba_ref[...]
            gx_s[pl.ds(r0, TM), :] = jnp.dot(xb, wx_ref[...], preferred_element_type=F32) + bx_ref[...]
            return carry
        lax.fori_loop(0, TP // TM, gate_chunk, 0)

        sp8 = LRU_C * _softplus(-lam_ref[...])
        row = _row_iota((R, CB))

        def scan_chunk(ci, hprev):
            r0 = pl.multiple_of(ci * R, R)
            xc = xc_ref[pl.ds(r0, R), :]
            _, i, a, mult = _gate_values(ga_s[pl.ds(r0, R), :], gx_s[pl.ds(r0, R), :], xc, sp8)
            u = mult * (i * xc)
            k = 1
            while k < R:
                m = row >= k
                u = jnp.where(m, a * pltpu.roll(u, k, 0) + u, u)
                a = jnp.where(m, a * pltpu.roll(a, k, 0), a)
                k *= 2
            hv = u + a * hprev
            hs_ref[pl.ds(r0, R), :] = hv
            g = g_ref[pl.ds(r0, R), :]
            y_ref[pl.ds(r0, R), :] = (hv * (g * _sig(g))).astype(BF16)
            return jnp.sum(jnp.where(row == R - 1, hv, 0.0), axis=0, keepdims=True)
        lax.fori_loop(0, TP // R, scan_chunk, jnp.zeros((1, CB), F32))

    col = lambda off: pl.BlockSpec((TP, CB), lambda j: (0, off + j))
    vec = pl.BlockSpec((1, CB), lambda j: (0, j))
    wsp = pl.BlockSpec((None, CB, CB), lambda j: (j, 0, 0))
    return pl.pallas_call(
        body, name="lru_fwd",
        grid=(NCB,),
        in_specs=[col(0), col(NCB), pl.BlockSpec((LW, CB), lambda j: (0, j)), vec, wsp, vec, wsp, vec, vec],
        out_specs=[col(0), col(0), col(0)],
        out_shape=[jax.ShapeDtypeStruct((TP, DL), BF16), jax.ShapeDtypeStruct((TP, DL), F32),
                   jax.ShapeDtypeStruct((TP, DL), F32)],
        scratch_shapes=[pltpu.VMEM((TP, CB), F32), pltpu.VMEM((TP, CB), F32)],
        compiler_params=_cparams(),
    )(z, z, conv_w, conv_b, wa_g, b_a, wx_g, b_x, lam)


def _conf_fwd_conv(z, dw_w, dw_b):
    def body(u1_ref, u2_ref, w_ref, b_ref, vc_ref, vs):
        vs[pl.ds(0, KWP), :] = jnp.zeros((KWP, CB), F32)

        def glu_chunk(ci, carry):
            r0 = pl.multiple_of(ci * R, R)
            vs[pl.ds(KWP + r0, R), :] = u1_ref[pl.ds(r0, R), :] * _sig(u2_ref[pl.ds(r0, R), :])
            return carry
        lax.fori_loop(0, TP // R, glu_chunk, 0)

        bias = b_ref[...]

        def conv_chunk(ci, carry):
            r0 = pl.multiple_of(ci * R, R)
            buf = vs[pl.ds(r0, KWP + R), :]
            acc = jnp.zeros((R, CB), F32) + bias
            for rr in range(8):
                rolled = buf if rr == 0 else pltpu.roll(buf, rr, 0)
                for q in range(4):
                    s = 8 * q + rr
                    if s > KW - 1:
                        continue
                    k = KW - 1 - s
                    acc = acc + rolled[KWP - 8 * q:KWP - 8 * q + R, :] * w_ref[k:k + 1, :]
            vc_ref[pl.ds(r0, R), :] = acc
            return carry
        lax.fori_loop(0, TP // R, conv_chunk, 0)

    return pl.pallas_call(
        body, name="conf_fwd_conv",
        grid=(NCB,),
        in_specs=[pl.BlockSpec((TP, CB), lambda j: (0, 2 * NCB + j)),
                  pl.BlockSpec((TP, CB), lambda j: (0, 3 * NCB + j)),
                  pl.BlockSpec((KWP, CB), lambda j: (0, j)),
                  pl.BlockSpec((1, CB), lambda j: (0, j))],
        out_specs=pl.BlockSpec((TP, CB), lambda j: (0, j)),
        out_shape=jax.ShapeDtypeStruct((TP, DC), F32),
        scratch_shapes=[pltpu.VMEM((TP + KWP, CB), F32)],
        compiler_params=_cparams(),
    )(z, z, dw_w, dw_b)


def _ln_chunk(vc, lw, lb):
    mu = jnp.mean(vc, axis=-1, keepdims=True)
    xm = vc - mu
    var = jnp.mean(xm * xm, axis=-1, keepdims=True)
    rstd = lax.rsqrt(var + EPS)
    xhat = xm * rstd
    return xhat, rstd, xhat * lw + lb


def _conf_fwd_proj(vc, z, ln_w, ln_b, pw_w, pw_b):
    def body(vc_ref, g_ref, lw_ref, lb_ref, w_ref, b_ref, y_ref, p_ref, s_s):
        lw, lb = lw_ref[...], lb_ref[...]

        def ln_chunk(ci, carry):
            r0 = pl.multiple_of(ci * R, R)
            for half in range(2):
                rr = r0 + 8 * half
                _, _, ln = _ln_chunk(vc_ref[pl.ds(rr, 8), :], lw, lb)
                p_ref[pl.ds(rr, 8), :] = ln * _sig(ln)
            s_s[pl.ds(r0, R), :] = p_ref[pl.ds(r0, R), :].astype(BF16)
            return carry
        lax.fori_loop(0, TM // R, ln_chunk, 0)

        p_ref[...] = jnp.dot(s_s[...], w_ref[...], preferred_element_type=F32) + b_ref[...]

        def out_chunk(ci, carry):
            r0 = pl.multiple_of(ci * R, R)
            g = g_ref[pl.ds(r0, R), :]
            y_ref[pl.ds(r0, R), :] = (p_ref[pl.ds(r0, R), :] * (g * _sig(g))).astype(BF16)
            return carry
        lax.fori_loop(0, TM // R, out_chunk, 0)

    row = pl.BlockSpec((TM, DC), lambda i: (i, 0))
    vec = pl.BlockSpec((1, DC), lambda i: (0, 0))
    return pl.pallas_call(
        body, name="conf_fwd_proj",
        grid=(TP // TM,),
        in_specs=[row, pl.BlockSpec((TM, DC), lambda i: (i, 4)), vec, vec,
                  pl.BlockSpec((DC, DC), lambda i: (0, 0)), vec],
        out_specs=[row, row],
        out_shape=[jax.ShapeDtypeStruct((TP, DC), BF16), jax.ShapeDtypeStruct((TP, DC), F32)],
        scratch_shapes=[pltpu.VMEM((TM, DC), BF16)],
        compiler_params=_cparams(),
    )(vc, z, ln_w, ln_b, pw_w, pw_b)


def _outproj_loss(ylru, yconf, w_out, h, tgt, post_w):
    def body(yl_ref, yc_ref, w_ref, h_ref, t_ref, pw_ref, dout_ref, dy_ref, loss_ref, dpw_ref, y_s):
        i = pl.program_id(0)
        k = pl.program_id(1)

        @pl.when(k == 0)
        def _():
            y_s[...] = jnp.dot(yl_ref[...], w_ref[...], preferred_element_type=F32)

        @pl.when(k == 1)
        def _():
            y_s[...] += jnp.dot(yc_ref[...], w_ref[...], preferred_element_type=F32)

        @pl.when(jnp.logical_and(i == 0, k == 1))
        def _():
            loss_ref[...] = jnp.zeros_like(loss_ref)
            dpw_ref[...] = jnp.zeros_like(dpw_ref)

        @pl.when(k == 1)
        def _():
            pw = pw_ref[...]
            row = _row_iota((8, D))

            def chunk(ci, carry):
                r0 = pl.multiple_of(ci * 8, 8)
                yv = y_s[pl.ds(r0, 8), :]
                rs = lax.rsqrt(jnp.mean(yv * yv, axis=-1, keepdims=True) + EPS)
                grow = row + (i * TM + r0)
                valid = jnp.logical_and(grow >= NMETA, grow < T)
                yn = yv * rs
                err = jnp.where(valid, h_ref[pl.ds(r0, 8), :] + yn * pw - t_ref[pl.ds(r0, 8), :], 0.0)
                loss_ref[...] += err * err
                d_rn = err * (1.0 / D)
                dout_ref[pl.ds(r0, 8), :] = d_rn
                dpw_ref[...] += d_rn * yn
                gw = d_rn * pw
                dot = jnp.mean(gw * yv, axis=-1, keepdims=True)
                dy_ref[pl.ds(r0, 8), :] = (rs * gw - yv * (rs * rs * rs * dot)).astype(BF16)
                return carry
            lax.fori_loop(0, TM // 8, chunk, 0)

    row = pl.BlockSpec((TM, D), lambda i, k: (i, 0))
    half = pl.BlockSpec((TM, DL), lambda i, k: (i, 0))
    acc = pl.BlockSpec((8, D), lambda i, k: (0, 0))
    return pl.pallas_call(
        body, name="outproj_loss",
        grid=(TP // TM, 2),
        in_specs=[half, half, pl.BlockSpec((DL, D), lambda i, k: (k, 0)), row, row,
                  pl.BlockSpec((1, D), lambda i, k: (0, 0))],
        out_specs=[row, row, acc, acc],
        out_shape=[jax.ShapeDtypeStruct((TP, D), F32), jax.ShapeDtypeStruct((TP, D), BF16),
                   jax.ShapeDtypeStruct((8, D), F32), jax.ShapeDtypeStruct((8, D), F32)],
        scratch_shapes=[pltpu.VMEM((TM, D), F32)],
        compiler_params=_cparams(),
    )(ylru, yconf, w_out, h, tgt, post_w)


_NT = (((1,), (1,)), ((), ()))
_TN = (((0,), (0,)), ((), ()))


def _outproj_bwd(dy, ylru, yconf, w_out):
    def body(dy_ref, yl_ref, yc_ref, w_ref, dycat_ref, dw_ref):
        j = pl.program_id(0)
        dyv = dy_ref[...]
        dycat_ref[...] = lax.dot_general(dyv, w_ref[...], _NT, preferred_element_type=F32)

        @pl.when(j < NCB)
        def _():
            dw_ref[...] = lax.dot_general(yl_ref[...], dyv, _TN, preferred_element_type=F32).astype(BF16)

        @pl.when(j >= NCB)
        def _():
            dw_ref[...] = lax.dot_general(yc_ref[...], dyv, _TN, preferred_element_type=F32).astype(BF16)

    return pl.pallas_call(
        body, name="outproj_bwd",
        grid=(2 * NCB,),
        in_specs=[pl.BlockSpec((TP, D), lambda j: (0, 0)),
                  pl.BlockSpec((TP, CB), lambda j: (0, jnp.minimum(j, NCB - 1))),
                  pl.BlockSpec((TP, CB), lambda j: (0, jnp.maximum(j - NCB, 0))),
                  pl.BlockSpec((CB, D), lambda j: (j, 0))],
        out_specs=[pl.BlockSpec((TP, CB), lambda j: (0, j)), pl.BlockSpec((CB, D), lambda j: (j, 0))],
        out_shape=[jax.ShapeDtypeStruct((TP, D), F32), jax.ShapeDtypeStruct((D, D), BF16)],
        compiler_params=_cparams(),
    )(dy, ylru, yconf, w_out)


def _conf_bwd_proj(dycat, p, z, vc, ln_w, ln_b, pw_w):
    def body(dy_ref, p_ref, g_ref, vc_ref, lw_ref, lb_ref, w_ref,
             dvc_ref, dgc_ref, dpw_ref, vecs_ref, dp_s, s_s, ds_s):
        i = pl.program_id(0)
        lw, lb = lw_ref[...], lb_ref[...]

        @pl.when(i == 0)
        def _():
            dpw_ref[...] = jnp.zeros_like(dpw_ref)
            vecs_ref[...] = jnp.zeros_like(vecs_ref)

        def pre_chunk(ci, carry):
            r0 = pl.multiple_of(ci * R, R)
            for half in range(2):
                rr = r0 + 8 * half
                dyv = dy_ref[pl.ds(rr, 8), :]
                g = g_ref[pl.ds(rr, 8), :]
                sg = _sig(g)
                dp = dyv * (g * sg)
                dg = dyv * p_ref[pl.ds(rr, 8), :] * (sg * (1.0 + g * (1.0 - sg)))
                vecs_ref[0:8, :] += dp
                vecs_ref[8:16, :] += dg
                ds_s[pl.ds(rr, 8), :] = dp
                dvc_ref[pl.ds(rr, 8), :] = dg
            dp_s[pl.ds(r0, R), :] = ds_s[pl.ds(r0, R), :].astype(BF16)
            dgc_ref[pl.ds(r0, R), :] = dvc_ref[pl.ds(r0, R), :].astype(BF16)
            for half in range(2):
                rr = r0 + 8 * half
                _, _, ln = _ln_chunk(vc_ref[pl.ds(rr, 8), :], lw, lb)
                ds_s[pl.ds(rr, 8), :] = ln * _sig(ln)
            s_s[pl.ds(r0, R), :] = ds_s[pl.ds(r0, R), :].astype(BF16)
            return carry
        lax.fori_loop(0, TM // R, pre_chunk, 0)

        dpb = dp_s[...]
        ds_s[...] = lax.dot_general(dpb, w_ref[...], _NT, preferred_element_type=F32)
        dpw_ref[...] += lax.dot_general(s_s[...], dpb, _TN, preferred_element_type=F32)

        def post_chunk(ci, carry):
            r0 = pl.multiple_of(ci * 8, 8)
            xhat, rstd, ln = _ln_chunk(vc_ref[pl.ds(r0, 8), :], lw, lb)
            sl = _sig(ln)
            dln = ds_s[pl.ds(r0, 8), :] * (sl * (1.0 + ln * (1.0 - sl)))
            vecs_ref[16:24, :] += dln * xhat
            vecs_ref[24:32, :] += dln
            dxh = dln * lw
            m1 = jnp.mean(dxh, axis=-1, keepdims=True)
            m2 = jnp.mean(dxh * xhat, axis=-1, keepdims=True)
            dvc_ref[pl.ds(r0, 8), :] = rstd * (dxh - m1 - xhat * m2)
            return carry
        lax.fori_loop(0, TM // 8, post_chunk, 0)

    row = pl.BlockSpec((TM, DC), lambda i: (i, 0))
    vec = pl.BlockSpec((1, DC), lambda i: (0, 0))
    return pl.pallas_call(
        body, name="conf_bwd_proj",
        grid=(TP // TM,),
        in_specs=[pl.BlockSpec((TM, DC), lambda i: (i, 1)), row, pl.BlockSpec((TM, DC), lambda i: (i, 4)), row,
                  vec, vec, pl.BlockSpec((DC, DC), lambda i: (0, 0))],
        out_specs=[row, row, pl.BlockSpec((DC, DC), lambda i: (0, 0)), pl.BlockSpec((32, DC), lambda i: (0, 0))],
        out_shape=[jax.ShapeDtypeStruct((TP, DC), F32), jax.ShapeDtypeStruct((TP, DC), BF16),
                   jax.ShapeDtypeStruct((DC, DC), F32), jax.ShapeDtypeStruct((32, DC), F32)],
        scratch_shapes=[pltpu.VMEM((TM, DC), BF16), pltpu.VMEM((TM, DC), BF16), pltpu.VMEM((TM, DC), F32)],
        compiler_params=_cparams(),
    )(dycat, p, z, vc, ln_w, ln_b, pw_w)


def _conf_bwd_conv(dvc, z, dw_w):
    def body(dvc_ref, u1_ref, u2_ref, w_ref, du1_ref, du2_ref, dw_ref, vecs_ref, vs, dvs):
        vs[pl.ds(0, KWP), :] = jnp.zeros((KWP, CB), F32)
        dvs[pl.ds(TP, KWP), :] = jnp.zeros((KWP, CB), F32)
        dw_ref[...] = jnp.zeros_like(dw_ref)
        vecs_ref[...] = jnp.zeros_like(vecs_ref)

        def fill_chunk(ci, carry):
            r0 = pl.multiple_of(ci * R, R)
            vs[pl.ds(KWP + r0, R), :] = u1_ref[pl.ds(r0, R), :] * _sig(u2_ref[pl.ds(r0, R), :])
            dv = dvc_ref[pl.ds(r0, R), :]
            dvs[pl.ds(r0, R), :] = dv
            vecs_ref[0:8, :] += _fold8(dv)
            return carry
        lax.fori_loop(0, TP // R, fill_chunk, 0)

        def conv_chunk(ci, carry):
            r0 = pl.multiple_of(ci * R, R)
            vbuf = vs[pl.ds(r0, KWP + R), :]
            dbuf = dvs[pl.ds(r0, KWP + R), :]
            dcur = dbuf[0:R, :]
            dv = jnp.zeros((R, CB), F32)
            for rr in range(8):
                vroll = vbuf if rr == 0 else pltpu.roll(vbuf, rr, 0)
                droll = dbuf if rr == 0 else pltpu.roll(dbuf, KWP + R - rr, 0)
                for q in range(4):
                    s = 8 * q + rr
                    if s > KW - 1:
                        continue
                    k = KW - 1 - s
                    dv = dv + droll[8 * q:8 * q + R, :] * w_ref[k:k + 1, :]
                    dw_ref[8 * k:8 * k + 8, :] += _fold8(dcur * vroll[KWP - 8 * q:KWP - 8 * q + R, :])
            u1 = u1_ref[pl.ds(r0, R), :]
            sg = _sig(u2_ref[pl.ds(r0, R), :])
            du1 = dv * sg
            du2 = dv * u1 * (sg * (1.0 - sg))
            du1_ref[pl.ds(r0, R), :] = du1.astype(BF16)
            du2_ref[pl.ds(r0, R), :] = du2.astype(BF16)
            vecs_ref[8:16, :] += _fold8(du1)
            vecs_ref[16:24, :] += _fold8(du2)
            return carry
        lax.fori_loop(0, TP // R, conv_chunk, 0)

    blk = pl.BlockSpec((TP, CB), lambda j: (0, j))
    return pl.pallas_call(
        body, name="conf_bwd_conv",
        grid=(NCB,),
        in_specs=[blk, pl.BlockSpec((TP, CB), lambda j: (0, 2 * NCB + j)),
                  pl.BlockSpec((TP, CB), lambda j: (0, 3 * NCB + j)), pl.BlockSpec((KWP, CB), lambda j: (0, j))],
        out_specs=[blk, blk, pl.BlockSpec((8 * KWP, CB), lambda j: (0, j)), pl.BlockSpec((24, CB), lambda j: (0, j))],
        out_shape=[jax.ShapeDtypeStruct((TP, DC), BF16), jax.ShapeDtypeStruct((TP, DC), BF16),
                   jax.ShapeDtypeStruct((8 * KWP, DC), F32), jax.ShapeDtypeStruct((24, DC), F32)],
        scratch_shapes=[pltpu.VMEM((TP + KWP, CB), F32), pltpu.VMEM((TP + KWP, CB), F32)],
        compiler_params=_cparams(),
    )(dvc, z, z, dw_w)


def _lru_bwd(dycat, z, xc, hs, conv_w, wa_g, b_a, wx_g, b_x, lam):
    NV = 6

    def body(dy_ref, x_ref, g_ref, xc_ref, hs_ref, cw_ref, wa_ref, ba_ref, wx_ref, bx_ref, lam_ref,
             dxl_ref, dgl_ref, dwa_ref, dwx_ref, dcw_ref, vecs_ref, ga_s, gx_s, dxc_s):
        vecs_ref[...] = jnp.zeros_like(vecs_ref)
        dcw_ref[...] = jnp.zeros_like(dcw_ref)
        dxc_s[pl.ds(TP, 8), :] = jnp.zeros((8, CB), F32)

        def gate_chunk(ci, carry):
            r0 = pl.multiple_of(ci * TM, TM)
            xb = xc_ref[pl.ds(r0, TM), :].astype(BF16)
            ga_s[pl.ds(r0, TM), :] = jnp.dot(xb, wa_ref[...], preferred_element_type=F32) + ba_ref[...]
            gx_s[pl.ds(r0, TM), :] = jnp.dot(xb, wx_ref[...], preferred_element_type=F32) + bx_ref[...]
            return carry
        lax.fori_loop(0, TP // TM, gate_chunk, 0)

        sp8 = LRU_C * _softplus(-lam_ref[...])
        row = _row_iota((R, CB))
        nchunk = TP // R

        def scan_chunk(cj, carry):
            a_next, lam_next = carry
            ci = nchunk - 1 - cj
            r0 = pl.multiple_of(ci * R, R)
            dyv = dy_ref[pl.ds(r0, R), :]
            g = g_ref[pl.ds(r0, R), :]
            hv = hs_ref[pl.ds(r0, R), :]
            xc = xc_ref[pl.ds(r0, R), :]
            sg = _sig(g)
            dgl = dyv * hv * (sg * (1.0 + g * (1.0 - sg)))
            dgl_ref[pl.ds(r0, R), :] = dgl.astype(BF16)
            vecs_ref[0:8, :] += _fold8(dgl)
            dhs = dyv * (g * sg)
            r, i, a, mult = _gate_values(ga_s[pl.ds(r0, R), :], gx_s[pl.ds(r0, R), :], xc, sp8)
            b = jnp.where(row == R - 1, a_next, pltpu.roll(a, R - 1, 0))
            lv = dhs
            k = 1
            while k < R:
                m = row < R - k
                lv = jnp.where(m, lv + b * pltpu.roll(lv, R - k, 0), lv)
                b = jnp.where(m, b * pltpu.roll(b, R - k, 0), b)
                k *= 2
            lv = lv + b * lam_next
            p0 = pl.multiple_of(jnp.maximum(r0 - 8, 0), 8)
            hprev8 = jnp.where(ci > 0, hs_ref[pl.ds(p0, 8), :], 0.0)
            hprev = pltpu.roll(jnp.concatenate([hprev8, hv], axis=0), 1, 0)[8:8 + R, :]
            da = lv * hprev
            ixc = i * xc
            dmult = lv * ixc
            di = lv * mult * xc
            dxc_s[pl.ds(r0, R), :] = lv * mult * i
            a2 = a * a
            dlog_a = da * a - dmult * a2 / mult
            vecs_ref[32:40, :] += _fold8(dlog_a * r)
            dga = -(dlog_a * sp8) * r * (1.0 - r)
            dgx = di * i * (1.0 - i)
            ga_s[pl.ds(r0, R), :] = dga
            gx_s[pl.ds(r0, R), :] = dgx
            vecs_ref[16:24, :] += _fold8(dga)
            vecs_ref[24:32, :] += _fold8(dgx)
            a_first = jnp.sum(jnp.where(row == 0, a, 0.0), axis=0, keepdims=True)
            l_first = jnp.sum(jnp.where(row == 0, lv, 0.0), axis=0, keepdims=True)
            return a_first, l_first
        lax.fori_loop(0, nchunk, scan_chunk, (jnp.zeros((1, CB), F32), jnp.zeros((1, CB), F32)))

        dwa_ref[...] = jnp.zeros_like(dwa_ref)
        dwx_ref[...] = jnp.zeros_like(dwx_ref)

        def mm_chunk(ci, carry):
            r0 = pl.multiple_of(ci * TM, TM)
            xb = xc_ref[pl.ds(r0, TM), :].astype(BF16)
            dgab = ga_s[pl.ds(r0, TM), :].astype(BF16)
            dgxb = gx_s[pl.ds(r0, TM), :].astype(BF16)
            dxc_s[pl.ds(r0, TM), :] += (lax.dot_general(dgab, wa_ref[...], _NT, preferred_element_type=F32)
                                        + lax.dot_general(dgxb, wx_ref[...], _NT, preferred_element_type=F32))
            dwa_ref[...] += lax.dot_general(xb, dgab, _TN, preferred_element_type=F32)
            dwx_ref[...] += lax.dot_general(xb, dgxb, _TN, preferred_element_type=F32)
            return carry
        lax.fori_loop(0, TP // TM, mm_chunk, 0)

        taps = [cw_ref[k:k + 1, :] for k in range(LW)]

        def conv_chunk(ci, carry):
            r0 = pl.multiple_of(ci * R, R)
            dbuf = dxc_s[pl.ds(r0, R + 8), :]
            dcur = dbuf[0:R, :]
            p0 = pl.multiple_of(jnp.maximum(r0 - 8, 0), 8)
            xprev = jnp.where(ci > 0, x_ref[pl.ds(p0, 8), :], 0.0)
            xbuf = jnp.concatenate([xprev, x_ref[pl.ds(r0, R), :]], axis=0)
            dxl = dcur * taps[LW - 1]
            dcw_ref[8 * (LW - 1):8 * LW, :] += _fold8(dcur * xbuf[8:8 + R, :])
            for s in range(1, LW):
                k = LW - 1 - s
                dxl = dxl + pltpu.roll(dbuf, R + 8 - s, 0)[0:R, :] * taps[k]
                dcw_ref[8 * k:8 * k + 8, :] += _fold8(dcur * pltpu.roll(xbuf, s, 0)[8:8 + R, :])
            dxl_ref[pl.ds(r0, R), :] = dxl.astype(BF16)
            vecs_ref[8:16, :] += _fold8(dxl)
            vecs_ref[40:48, :] += _fold8(dcur)
            return carry
        lax.fori_loop(0, TP // R, conv_chunk, 0)
        vecs_ref[32:40, :] = vecs_ref[32:40, :] * (LRU_C * _sig(-lam_ref[...]))

    col = lambda off: pl.BlockSpec((TP, CB), lambda j: (0, off + j))
    vec = pl.BlockSpec((1, CB), lambda j: (0, j))
    wsp = pl.BlockSpec((None, CB, CB), lambda j: (j, 0, 0))
    return pl.pallas_call(
        body, name="lru_bwd",
        grid=(NCB,),
        in_specs=[col(0), col(0), col(NCB), col(0), col(0), pl.BlockSpec((LW, CB), lambda j: (0, j)),
                  wsp, vec, wsp, vec, vec],
        out_specs=[col(0), col(0), wsp, wsp, pl.BlockSpec((8 * LW, CB), lambda j: (0, j)),
                   pl.BlockSpec((8 * NV, CB), lambda j: (0, j))],
        out_shape=[jax.ShapeDtypeStruct((TP, DL), BF16), jax.ShapeDtypeStruct((TP, DL), BF16),
                   jax.ShapeDtypeStruct((NCB, CB, CB), F32), jax.ShapeDtypeStruct((NCB, CB, CB), F32),
                   jax.ShapeDtypeStruct((8 * LW, DL), F32), jax.ShapeDtypeStruct((8 * NV, DL), F32)],
        scratch_shapes=[pltpu.VMEM((TP, CB), F32), pltpu.VMEM((TP, CB), F32), pltpu.VMEM((TP + 8, CB), F32)],
        compiler_params=_cparams(),
    )(dycat, z, z, xc, hs, conv_w, wa_g, b_a, wx_g, b_x, lam)


def _inproj_wgrad(hn, dz):
    KB = 512

    def body(hn_ref, dz_ref, dw_ref):
        dw_ref[...] = lax.dot_general(hn_ref[...], dz_ref[...], _TN, preferred_element_type=F32).astype(BF16)

    return pl.pallas_call(
        body, name="inproj_wgrad",
        grid=(NIN // 1024, D // KB),
        in_specs=[pl.BlockSpec((TP, KB), lambda n, kb: (0, kb)), pl.BlockSpec((TP, 1024), lambda n, kb: (0, n))],
        out_specs=pl.BlockSpec((KB, 1024), lambda n, kb: (kb, n)),
        out_shape=jax.ShapeDtypeStruct((D, NIN), BF16),
        compiler_params=_cparams(),
    )(hn, dz)


def _inproj_bwd(dz, w_in, h, dout, pre_w):
    nsec = NIN // 1024

    def body(dz_ref, w_ref, h_ref, dout_ref, pw_ref, dh_ref, dpw_ref, acc_s):
        i = pl.program_id(0)
        s = pl.program_id(1)
        part = lax.dot_general(dz_ref[...], w_ref[...], _NT, preferred_element_type=F32)

        @pl.when(s == 0)
        def _():
            acc_s[...] = part

        @pl.when(s > 0)
        def _():
            acc_s[...] += part

        @pl.when(jnp.logical_and(i == 0, s == nsec - 1))
        def _():
            dpw_ref[...] = jnp.zeros_like(dpw_ref)

        @pl.when(s == nsec - 1)
        def _():
            pw = pw_ref[...]

            def chunk(ci, carry):
                r0 = pl.multiple_of(ci * 8, 8)
                hv = h_ref[pl.ds(r0, 8), :]
                dhn = acc_s[pl.ds(r0, 8), :]
                rs = lax.rsqrt(jnp.mean(hv * hv, axis=-1, keepdims=True) + EPS)
                dpw_ref[...] += dhn * (hv * rs)
                gw = dhn * pw
                dot = jnp.mean(gw * hv, axis=-1, keepdims=True)
                dh_ref[pl.ds(r0, 8), :] = rs * gw - hv * (rs * rs * rs * dot) + dout_ref[pl.ds(r0, 8), :]
                return carry
            lax.fori_loop(0, TM // 8, chunk, 0)

    row = pl.BlockSpec((TM, D), lambda i, s: (i, 0))
    return pl.pallas_call(
        body, name="inproj_bwd",
        grid=(TP // TM, nsec),
        in_specs=[pl.BlockSpec((TM, 1024), lambda i, s: (i, s)), pl.BlockSpec((D, 1024), lambda i, s: (0, s)),
                  row, row, pl.BlockSpec((1, D), lambda i, s: (0, 0))],
        out_specs=[row, pl.BlockSpec((8, D), lambda i, s: (0, 0))],
        out_shape=[jax.ShapeDtypeStruct((TP, D), F32), jax.ShapeDtypeStruct((8, D), F32)],
        scratch_shapes=[pltpu.VMEM((TM, D), F32)],
        compiler_params=_cparams(),
    )(dz, w_in, h, dout, pre_w)


def _adamw(name, parts, w, m, v, block_rows):
    rows, cols = w.shape
    cw = cols if cols <= 640 else 512
    c1 = 1.0 / (1.0 - ADAM_B1 ** ADAM_STEP)
    c2 = 1.0 / (1.0 - ADAM_B2 ** ADAM_STEP)

    def body(p_ref, w_ref, m_ref, v_ref, g_ref, d_ref, nm_ref, nv_ref):
        def chunk(ci, carry):
            r0 = pl.multiple_of(ci * R, R)
            for c0 in range(0, cols, cw):
                g = p_ref[0, pl.ds(r0, R), c0:c0 + cw].astype(F32)
                for sidx in range(1, NDEV):
                    g = g + p_ref[sidx, pl.ds(r0, R), c0:c0 + cw].astype(F32)
                wv = w_ref[pl.ds(r0, R), c0:c0 + cw]
                mv = ADAM_B1 * m_ref[pl.ds(r0, R), c0:c0 + cw] + (1.0 - ADAM_B1) * g
                vv = ADAM_B2 * v_ref[pl.ds(r0, R), c0:c0 + cw] + (1.0 - ADAM_B2) * (g * g)
                g_ref[pl.ds(r0, R), c0:c0 + cw] = g
                nm_ref[pl.ds(r0, R), c0:c0 + cw] = mv
                nv_ref[pl.ds(r0, R), c0:c0 + cw] = vv
                upd = (mv * c1) / (jnp.sqrt(vv * c2) + ADAM_EPS) + ADAM_WD * wv
                d_ref[pl.ds(r0, R), c0:c0 + cw] = -ADAM_LR * upd
            return carry
        lax.fori_loop(0, block_rows // R, chunk, 0)

    blk = pl.BlockSpec((block_rows, cols), lambda i: (i, 0))
    shp = jax.ShapeDtypeStruct((rows, cols), F32)
    return pl.pallas_call(
        body, name=name,
        grid=(rows // block_rows,),
        in_specs=[pl.BlockSpec((NDEV, block_rows, cols), lambda i: (0, i, 0)), blk, blk, blk],
        out_specs=[blk, blk, blk, blk],
        out_shape=[shp, shp, shp, shp],
        compiler_params=_cparams(),
    )(parts, w, m, v)


_REP = ["pre_norm_w", "post_norm_w", "b_in", "lru_conv_b", "w_gate_a", "b_gate_a", "w_gate_x", "b_gate_x",
        "lru_lambda", "conf_dw_b", "conf_ln_w", "conf_ln_b", "conf_pw_b"]
_REP_SIZE = {"pre_norm_w": D, "post_norm_w": D, "b_in": NIN, "w_gate_a": 16 * 64 * 64, "w_gate_x": 16 * 64 * 64}
_REP_ROWS = 1168
_SM_ROWS = 64


def _pack_rep(d):
    flat = jnp.concatenate([d[n].reshape(-1) for n in _REP])
    return jnp.pad(flat, (0, _REP_ROWS * 128 - flat.shape[0])).reshape(_REP_ROWS, 128)


def _unpack_rep(vec, shapes):
    flat = vec.reshape(-1)
    out, off = {}, 0
    for n in _REP:
        size = _REP_SIZE.get(n, 1024)
        out[n] = flat[off:off + size].reshape(shapes[n])
        off += size
    return out


def _pack_small(lru_cw, dw_w, meta):
    buf = jnp.zeros((_SM_ROWS, 256), F32)
    buf = buf.at[0:LW, 0:128].set(lru_cw)
    buf = buf.at[8:8 + dw_w.shape[0], 0:128].set(dw_w)
    return buf.at[40:56, :].set(meta)


def _block_diag4(w):
    w4 = w.reshape(NCB, 4, 64, 64)
    eye = jnp.eye(4, dtype=w.dtype)
    return jnp.einsum("ghij,hk->ghikj", w4, eye).reshape(NCB, CB, CB)


def _diag_blocks(g):
    g5 = g.reshape(NCB, 4, 64, 4, 64)
    return jnp.stack([g5[:, hh, :, hh, :] for hh in range(4)], axis=1).reshape(16, 64, 64)


def _local_step(x, target, meta_full, win_full, wout_full, pw_full, lru_cw_full, dw_w_full, W):
    h = jnp.concatenate([meta_full, x, jnp.zeros((TP - T, D), F32)], axis=0)
    tgt = jnp.concatenate([jnp.zeros((NMETA, D), F32), target, jnp.zeros((TP - T, D), F32)], axis=0)
    wa_g = _block_diag4(W["w_gate_a"][0]).astype(BF16)
    wx_g = _block_diag4(W["w_gate_x"][0]).astype(BF16)

    z, hn = _prenorm_inproj(h, W["pre_norm_w"], win_full, W["b_in"])
    ylru, xc, hs = _lru_fwd(z, lru_cw_full, W["lru_conv_b"], wa_g, W["b_gate_a"], wx_g, W["b_gate_x"],
                            W["lru_lambda"])
    vc = _conf_fwd_conv(z, dw_w_full, W["conf_dw_b"])
    yconf, p = _conf_fwd_proj(vc, z, W["conf_ln_w"], W["conf_ln_b"], pw_full, W["conf_pw_b"])
    dout, dy, loss_acc, dpostw_acc = _outproj_loss(ylru, yconf, wout_full, h, tgt, W["post_norm_w"])
    loss_local = 0.5 / D * jnp.sum(loss_acc)

    dycat, dwout_part = _outproj_bwd(dy, ylru, yconf, wout_full)
    dvc, dgc, dpw_part, cvecs = _conf_bwd_proj(dycat, p, z, vc, W["conf_ln_w"], W["conf_ln_b"], pw_full)
    du1, du2, ddw_acc, kvecs = _conf_bwd_conv(dvc, z, dw_w_full)
    dxl, dgl, dwa_g, dwx_g, dcw_acc, lvecs = _lru_bwd(dycat, z, xc, hs, lru_cw_full, wa_g, W["b_gate_a"], wx_g,
                                                      W["b_gate_x"], W["lru_lambda"])
    dz = jnp.concatenate([dxl, dgl, du1, du2, dgc], axis=1)
    dwin_part = _inproj_wgrad(hn, dz)
    dh, dprew_acc = _inproj_bwd(dz, win_full, h, dout, W["pre_norm_w"])

    sum8 = lambda a: jnp.sum(a.reshape(-1, 8, a.shape[-1]), axis=1)
    cv, kv, lv = sum8(cvecs), sum8(kvecs), sum8(lvecs)
    rep_part = dict(
        pre_norm_w=jnp.sum(dprew_acc, axis=0), post_norm_w=jnp.sum(dpostw_acc, axis=0),
        b_in=jnp.concatenate([lv[1], lv[0], kv[1], kv[2], cv[1]]),
        lru_conv_b=lv[5], w_gate_a=_diag_blocks(dwa_g), b_gate_a=lv[2], w_gate_x=_diag_blocks(dwx_g),
        b_gate_x=lv[3], lru_lambda=lv[4],
        conf_dw_b=kv[0], conf_ln_w=cv[2], conf_ln_b=cv[3], conf_pw_b=cv[0])
    return loss_local, dh, dwin_part, dwout_part, dpw_part, sum8(dcw_acc), sum8(ddw_acc), rep_part


def kernel(x, meta_tokens, pre_norm_w, post_norm_w, w_in, b_in, lru_conv_w, lru_conv_b, w_gate_a, b_gate_a, w_gate_x, b_gate_x, lru_lambda, conf_dw_w, conf_dw_b, conf_ln_w, conf_ln_b, conf_pw_w, conf_pw_b, w_out, loss_target, m_meta_tokens, m_pre_norm_w, m_post_norm_w, m_w_in, m_b_in, m_lru_conv_w, m_lru_conv_b, m_w_gate_a, m_b_gate_a, m_w_gate_x, m_b_gate_x, m_lru_lambda, m_conf_dw_w, m_conf_dw_b, m_conf_ln_w, m_conf_ln_b, m_conf_pw_w, m_conf_pw_b, m_w_out, v_meta_tokens, v_pre_norm_w, v_post_norm_w, v_w_in, v_b_in, v_lru_conv_w, v_lru_conv_b, v_w_gate_a, v_b_gate_a, v_w_gate_x, v_b_gate_x, v_lru_lambda, v_conf_dw_w, v_conf_dw_b, v_conf_ln_w, v_conf_ln_b, v_conf_pw_w, v_conf_pw_b, v_w_out):
    W = dict(meta_tokens=meta_tokens, pre_norm_w=pre_norm_w, post_norm_w=post_norm_w, w_in=w_in, b_in=b_in,
             lru_conv_w=lru_conv_w, lru_conv_b=lru_conv_b, w_gate_a=w_gate_a, b_gate_a=b_gate_a,
             w_gate_x=w_gate_x, b_gate_x=b_gate_x, lru_lambda=lru_lambda, conf_dw_w=conf_dw_w,
             conf_dw_b=conf_dw_b, conf_ln_w=conf_ln_w, conf_ln_b=conf_ln_b, conf_pw_w=conf_pw_w,
             conf_pw_b=conf_pw_b, w_out=w_out)
    M = dict(meta_tokens=m_meta_tokens, pre_norm_w=m_pre_norm_w, post_norm_w=m_post_norm_w, w_in=m_w_in,
             b_in=m_b_in, lru_conv_w=m_lru_conv_w, lru_conv_b=m_lru_conv_b, w_gate_a=m_w_gate_a,
             b_gate_a=m_b_gate_a, w_gate_x=m_w_gate_x, b_gate_x=m_b_gate_x, lru_lambda=m_lru_lambda,
             conf_dw_w=m_conf_dw_w, conf_dw_b=m_conf_dw_b, conf_ln_w=m_conf_ln_w, conf_ln_b=m_conf_ln_b,
             conf_pw_w=m_conf_pw_w, conf_pw_b=m_conf_pw_b, w_out=m_w_out)
    V = dict(meta_tokens=v_meta_tokens, pre_norm_w=v_pre_norm_w, post_norm_w=v_post_norm_w, w_in=v_w_in,
             b_in=v_b_in, lru_conv_w=v_lru_conv_w, lru_conv_b=v_lru_conv_b, w_gate_a=v_w_gate_a,
             b_gate_a=v_b_gate_a, w_gate_x=v_w_gate_x, b_gate_x=v_b_gate_x, lru_lambda=v_lru_lambda,
             conf_dw_w=v_conf_dw_w, conf_dw_b=v_conf_dw_b, conf_ln_w=v_conf_ln_w, conf_ln_b=v_conf_ln_b,
             conf_pw_w=v_conf_pw_w, conf_pw_b=v_conf_pw_b, w_out=v_w_out)
    names = list(W.keys())
    shapes = {n: W[n].shape for n in names}

    small = _pack_small(lru_conv_w[0], conf_dw_w[0], meta_tokens)
    win_full, wout_full, pw_full, small_all = _exchange("gather_weights", [
        (w_in[0].astype(BF16), jax.ShapeDtypeStruct((D, NIN), BF16), _whole, _cols(NIN // NDEV)),
        (w_out[0].astype(BF16), jax.ShapeDtypeStruct((D, D), BF16), _whole, _rows(D // NDEV)),
        (conf_pw_w[0].astype(BF16), jax.ShapeDtypeStruct((DC, DC), BF16), _whole, _rows(DC // NDEV)),
        (small, jax.ShapeDtypeStruct((NDEV, _SM_ROWS, 256), F32), _whole, _slot),
    ])
    unshard = lambda a: jnp.transpose(a, (1, 0, 2)).reshape(a.shape[1], -1)
    lru_cw_full = unshard(small_all[:, 0:LW, 0:128])
    dw_w_full = unshard(small_all[:, 8:8 + KWP, 0:128])
    meta_full = unshard(small_all[:, 40:56, :])

    loss_local, dh, dwin_part, dwout_part, dpw_part, dcw_part, ddw_part, rep_part = _local_step(
        x[0], loss_target[0], meta_full, win_full, wout_full, pw_full, lru_cw_full, dw_w_full, W)
    loss = lax.psum(loss_local, ("x", "y", "c"))
    grad_x = dh[NMETA:T][None]
    dmeta_part = dh[0:NMETA]
    shard = lambda a, wdt: jnp.transpose(a.reshape(a.shape[0], NDEV, wdt), (1, 0, 2))
    small_part = jnp.zeros((NDEV, _SM_ROWS, 256), F32)
    small_part = small_part.at[:, 0:LW, 0:128].set(shard(dcw_part, 128))
    small_part = small_part.at[:, 8:8 + KWP, 0:128].set(shard(ddw_part, 128))
    small_part = small_part.at[:, 40:56, :].set(shard(dmeta_part, 256))

    win_parts, wout_parts, pw_parts, small_parts, rep_parts = _exchange("scatter_grads", [
        (dwin_part, jax.ShapeDtypeStruct((NDEV, D, NIN // NDEV), BF16), _cols(NIN // NDEV), _slot),
        (dwout_part, jax.ShapeDtypeStruct((NDEV, D // NDEV, D), BF16), _rows(D // NDEV), _slot),
        (dpw_part.astype(BF16), jax.ShapeDtypeStruct((NDEV, DC // NDEV, DC), BF16), _rows(DC // NDEV), _slot),
        (small_part, jax.ShapeDtypeStruct((NDEV, _SM_ROWS, 256), F32), _slot, _slot),
        (_pack_rep(rep_part), jax.ShapeDtypeStruct((NDEV, _REP_ROWS, 128), F32), _whole, _slot),
    ])

    G, DW, NM, NV = {}, {}, {}, {}
    G["w_in"], DW["w_in"], NM["w_in"], NV["w_in"] = _adamw("adamw_w_in", win_parts, w_in[0], m_w_in[0], v_w_in[0], 256)
    G["w_out"], DW["w_out"], NM["w_out"], NV["w_out"] = _adamw("adamw_w_out", wout_parts, w_out[0], m_w_out[0], v_w_out[0], 64)
    G["conf_pw_w"], DW["conf_pw_w"], NM["conf_pw_w"], NV["conf_pw_w"] = _adamw(
        "adamw_pw", pw_parts, conf_pw_w[0], m_conf_pw_w[0], v_conf_pw_w[0], 128)
    sm = _adamw("adamw_small", small_parts, small,
                _pack_small(m_lru_conv_w[0], m_conf_dw_w[0], m_meta_tokens),
                _pack_small(v_lru_conv_w[0], v_conf_dw_w[0], v_meta_tokens), _SM_ROWS)
    rp = _adamw("adamw_rep", rep_parts, _pack_rep(W), _pack_rep(M), _pack_rep(V), _REP_ROWS)
    for dst, s_arr, r_arr in zip((G, DW, NM, NV), sm, rp):
        dst["lru_conv_w"] = s_arr[0:LW, 0:128][None]
        dst["conf_dw_w"] = s_arr[8:8 + KW, 0:128][None]
        dst["meta_tokens"] = s_arr[40:56, :]
        dst.update(_unpack_rep(r_arr, shapes))
        for n in ("w_in", "w_out", "conf_pw_w"):
            dst[n] = dst[n].reshape(shapes[n])

    return (loss, grad_x, *[G[n] for n in names], *[DW[n] for n in names],
            *[NM[n] for n in names], *[NV[n] for n in names])
```

```python
import functools

import jax
import jax.numpy as jnp
from jax import lax
from jax.experimental import pallas as pl
from jax.experimental.pallas import tpu as pltpu

F32 = jnp.float32
BF16 = jnp.bfloat16

D = 2048
DL = 1024
DC = 1024
NIN = 5120
NMETA = 16
SEQ = 2048
T = NMETA + SEQ
TP = 2176
TM = 544
CB = 256
NCB = DL // CB
R = 16
KW = 31
KWP = 32
LW = 4
LRU_C = 8.0
EPS = 1e-6
NDEV = 8

ADAM_LR = 0.001
ADAM_B1 = 0.9
ADAM_B2 = 0.999
ADAM_EPS = 1e-08
ADAM_WD = 0.01
ADAM_STEP = 10

VMEM_LIMIT = 56 * 1024 * 1024


def _cparams():
    return pltpu.CompilerParams(vmem_limit_bytes=VMEM_LIMIT)


def _sig(x):
    return 1.0 / (1.0 + jnp.exp(-x))


def _expm1_neg(y):
    poly = y * (1.0 + y * (0.5 + y * (1.0 / 6.0 + y * (1.0 / 24.0 + y * (1.0 / 120.0)))))
    return jnp.where(y > -0.1, poly, jnp.exp(y) - 1.0)


def _softplus(x):
    e = jnp.exp(-jnp.abs(x))
    w = 1.0 + e
    l1p = jnp.where(w == 1.0, e, jnp.log(w) * e / (w - 1.0))
    return jnp.maximum(x, 0.0) + l1p


def _row_iota(shape):
    return lax.broadcasted_iota(jnp.int32, shape, 0)


def _fold8(v):
    return v[0:8, :] + v[8:16, :]


_FLIPS = [(k >> 2 & 1, k >> 1 & 1, k & 1) for k in range(1, NDEV)]
_HBM = pl.BlockSpec(memory_space=pltpu.HBM)
_SEM = pl.BlockSpec(memory_space=pltpu.SEMAPHORE)


def _peers():
    x, y, c = lax.axis_index("x"), lax.axis_index("y"), lax.axis_index("c")
    out = []
    for dx, dy, dc in _FLIPS:
        px = 1 - x if dx else x
        py = 1 - y if dy else y
        pc = 1 - c if dc else c
        out.append(((px, py, pc), 4 * px + 2 * py + pc))
    return 4 * x + 2 * y + c, out


def _exchange_start(name, items):
    n = len(items)

    def body(*refs):
        srcs, lands = refs[:n], refs[n:2 * n]
        outs = refs[2 * n:]
        send_sems, recv_sems, local_sems = outs[:n], outs[n:2 * n], outs[2 * n:3 * n]
        token = outs[-1]
        me, peers = _peers()
        for a in range(n):
            src_at, dst_at = items[a][2], items[a][3]
            pltpu.make_async_copy(src_at(srcs[a], me), dst_at(lands[a], me), local_sems[a]).start()
        for a in range(n):
            src_at, dst_at = items[a][2], items[a][3]
            for k, (pos, peer) in enumerate(peers):
                pltpu.make_async_remote_copy(
                    src_ref=src_at(srcs[a], peer), dst_ref=dst_at(lands[a], me),
                    send_sem=send_sems[a].at[k], recv_sem=recv_sems[a].at[k],
                    device_id=pos, device_id_type=pl.DeviceIdType.MESH).start()
        token[...] = jnp.zeros_like(token)

    srcs = [pltpu.with_memory_space_constraint(it[0], pltpu.HBM) for it in items]
    lands = [pltpu.with_memory_space_constraint(lax.empty(it[1].shape, it[1].dtype), pltpu.HBM) for it in items]
    sem7 = pltpu.SemaphoreType.DMA((NDEV - 1,))
    res = pl.pallas_call(
        body, name=name,
        out_shape=([sem7] * (2 * n) + [pltpu.SemaphoreType.DMA(())] * n
                   + [pltpu.HBM(a.shape, a.dtype) for a in srcs] + [pltpu.HBM(a.shape, a.dtype) for a in lands]
                   + [jax.ShapeDtypeStruct((8, 128), F32)]),
        in_specs=[_HBM] * (2 * n),
        out_specs=[_SEM] * (3 * n) + [_HBM] * (2 * n) + [pl.BlockSpec(memory_space=pltpu.VMEM)],
        input_output_aliases={i: 3 * n + i for i in range(2 * n)},
        compiler_params=pltpu.CompilerParams(has_side_effects=pltpu.SideEffectType.DATAFLOW_SIDE_EFFECTING),
    )(*srcs, *lands)
    handles = [dict(send=res[a], recv=res[n + a], local=res[2 * n + a], src=res[3 * n + a], land=res[4 * n + a],
                    src_at=items[a][2], dst_at=items[a][3]) for a in range(n)]
    return handles, res[-1]


def _exchange_wait(name, handles, after):
    n = len(handles)

    def body(*refs):
        srcs, lands = refs[:n], refs[n:2 * n]
        send_sems, recv_sems, local_sems = refs[2 * n:3 * n], refs[3 * n:4 * n], refs[4 * n:5 * n]
        me, peers = _peers()
        for a in range(n):
            src_at, dst_at = handles[a]["src_at"], handles[a]["dst_at"]
            for k, (pos, peer) in enumerate(peers):
                cp = pltpu.make_async_remote_copy(
                    src_ref=src_at(srcs[a], peer), dst_ref=dst_at(lands[a], peer),
                    send_sem=send_sems[a].at[k], recv_sem=recv_sems[a].at[k],
                    device_id=pos, device_id_type=pl.DeviceIdType.MESH)
                cp.wait_send()
                cp.wait_recv()
            pltpu.make_async_copy(src_at(srcs[a], me), dst_at(lands[a], me), local_sems[a]).wait()

    srcs = [hd["src"] for hd in handles]
    lands = [hd["land"] for hd in handles]
    res = pl.pallas_call(
        body, name=name,
        out_shape=[pltpu.HBM(a.shape, a.dtype) for a in srcs] + [pltpu.HBM(a.shape, a.dtype) for a in lands],
        in_specs=[_HBM] * (2 * n) + [_SEM] * (3 * n) + [pl.BlockSpec(memory_space=pl.ANY)],
        out_specs=[_HBM] * (2 * n),
        input_output_aliases={i: i for i in range(2 * n)},
        compiler_params=pltpu.CompilerParams(has_side_effects=pltpu.SideEffectType.DATAFLOW_SIDE_EFFECTING),
    )(*srcs, *lands, *[hd["send"] for hd in handles], *[hd["recv"] for hd in handles],
      *[hd["local"] for hd in handles], after)
    return list(res[n:])


def _whole(ref, l):
    return ref


def _slot(ref, l):
    return ref.at[l]


def _cols(width):
    def at(ref, l):
        return ref.at[:, pl.ds(pl.multiple_of(l * width, 128), width)]
    return at


def _rows(height):
    def at(ref, l):
        return ref.at[pl.ds(pl.multiple_of(l * height, 8), height), :]
    return at


def _prenorm_inproj(h, pre_w, w_in, b_in):
    nsec = NIN // 1024

    def body(h_ref, pw_ref, w_ref, b_ref, z_ref, hn_ref):
        @pl.when(pl.program_id(1) == 0)
        def _():
            pw = pw_ref[...]

            def chunk(ci, carry):
                r0 = pl.multiple_of(ci * R, R)
                xv = h_ref[pl.ds(r0, R), :]
                ms = jnp.mean(xv * xv, axis=-1, keepdims=True)
                hn_ref[pl.ds(r0, R), :] = (xv * lax.rsqrt(ms + EPS) * pw).astype(BF16)
                return carry
            lax.fori_loop(0, TM // R, chunk, 0)

        z_ref[...] = jnp.dot(hn_ref[...], w_ref[...], preferred_element_type=F32) + b_ref[...]

    return pl.pallas_call(
        body, name="prenorm_inproj",
        grid=(TP // TM, nsec),
        in_specs=[pl.BlockSpec((TM, D), lambda i, n: (i, 0)),
                  pl.BlockSpec((1, D), lambda i, n: (0, 0)),
                  pl.BlockSpec((D, 1024), lambda i, n: (0, n)),
                  pl.BlockSpec((1, 1024), lambda i, n: (0, n))],
        out_specs=[pl.BlockSpec((TM, 1024), lambda i, n: (i, n)),
                   pl.BlockSpec((TM, D), lambda i, n: (i, 0))],
        out_shape=[jax.ShapeDtypeStruct((TP, NIN), F32), jax.ShapeDtypeStruct((TP, D), BF16)],
        compiler_params=_cparams(),
    )(h, pre_w, w_in, b_in)


def _gate_values(ga, gx, xc, sp8):
    r = _sig(ga)
    i = _sig(gx)
    log_a = -(r * sp8)
    a = jnp.exp(log_a)
    mult = jnp.sqrt(-_expm1_neg(2.0 * log_a))
    return r, i, a, mult


def _lru_fwd(z, conv_w, conv_b, wa_g, b_a, wx_g, b_x, lam):
    def body(x_ref, g_ref, cw_ref, cb_ref, wa_ref, ba_ref, wx_ref, bx_ref, lam_ref,
             y_ref, xc_ref, hs_ref, ga_s, gx_s):
        taps = [cw_ref[k:k + 1, :] for k in range(LW)]
        cb = cb_ref[...]

        def conv_chunk(ci, carry):
            r0 = pl.multiple_of(ci * R, R)
            cur = x_ref[pl.ds(r0, R), :]
            p0 = pl.multiple_of(jnp.maximum(r0 - 8, 0), 8)
            prev = jnp.where(ci > 0, x_ref[pl.ds(p0, 8), :], 0.0)
            buf = jnp.concatenate([prev, cur], axis=0)
            acc = cur * taps[LW - 1] + cb
            for s in range(1, LW):
                acc = acc + pltpu.roll(buf, s, 0)[8:8 + R, :] * taps[LW - 1 - s]
            xc_ref[pl.ds(r0, R), :] = acc
            return carry
        lax.fori_loop(0, TP // R, conv_chunk, 0)

        def gate_chunk(ci, carry):
            r0 = pl.multiple_of(ci * TM, TM)
            xb = xc_ref[pl.ds(r0, TM), :].astype(BF16)
            ga_s[pl.ds(r0, TM), :] = jnp.dot(xb, wa_ref[...], preferred_element_type=F32) + ba_ref[...]
            gx_s[pl.ds(r0, TM), :] = jnp.dot(xb, wx_ref[...], preferred_element_type=F32) + bx_ref[...]
            return carry
        lax.fori_loop(0, TP // TM, gate_chunk, 0)

        sp8 = LRU_C * _softplus(-lam_ref[...])
        row = _row_iota((R, CB))

        def scan_chunk(ci, hprev):
            r0 = pl.multiple_of(ci * R, R)
            xc = xc_ref[pl.ds(r0, R), :]
            _, i, a, mult = _gate_values(ga_s[pl.ds(r0, R), :], gx_s[pl.ds(r0, R), :], xc, sp8)
            u = mult * (i * xc)
            k = 1
            while k < R:
                m = row >= k
                u = jnp.where(m, a * pltpu.roll(u, k, 0) + u, u)
                a = jnp.where(m, a * pltpu.roll(a, k, 0), a)
                k *= 2
            hv = u + a * hprev
            hs_ref[pl.ds(r0, R), :] = hv
            g = g_ref[pl.ds(r0, R), :]
            y_ref[pl.ds(r0, R), :] = (hv * (g * _sig(g))).astype(BF16)
            return jnp.sum(jnp.where(row == R - 1, hv, 0.0), axis=0, keepdims=True)
        lax.fori_loop(0, TP // R, scan_chunk, jnp.zeros((1, CB), F32))

    col = lambda off: pl.BlockSpec((TP, CB), lambda j: (0, off + j))
    vec = pl.BlockSpec((1, CB), lambda j: (0, j))
    wsp = pl.BlockSpec((None, CB, CB), lambda j: (j, 0, 0))
    return pl.pallas_call(
        body, name="lru_fwd",
        grid=(NCB,),
        in_specs=[col(0), col(NCB), pl.BlockSpec((LW, CB), lambda j: (0, j)), vec, wsp, vec, wsp, vec, vec],
        out_specs=[col(0), col(0), col(0)],
        out_shape=[jax.ShapeDtypeStruct((TP, DL), BF16), jax.ShapeDtypeStruct((TP, DL), F32),
                   jax.ShapeDtypeStruct((TP, DL), F32)],
        scratch_shapes=[pltpu.VMEM((TP, CB), F32), pltpu.VMEM((TP, CB), F32)],
        compiler_params=_cparams(),
    )(z, z, conv_w, conv_b, wa_g, b_a, wx_g, b_x, lam)


def _conf_fwd_conv(z, dw_w, dw_b):
    def body(u1_ref, u2_ref, w_ref, b_ref, vc_ref, vs):
        vs[pl.ds(0, KWP), :] = jnp.zeros((KWP, CB), F32)

        def glu_chunk(ci, carry):
            r0 = pl.multiple_of(ci * R, R)
            vs[pl.ds(KWP + r0, R), :] = u1_ref[pl.ds(r0, R), :] * _sig(u2_ref[pl.ds(r0, R), :])
            return carry
        lax.fori_loop(0, TP // R, glu_chunk, 0)

        bias = b_ref[...]

        def conv_chunk(ci, carry):
            r0 = pl.multiple_of(ci * R, R)
            buf = vs[pl.ds(r0, KWP + R), :]
            acc = jnp.zeros((R, CB), F32) + bias
            for rr in range(8):
                rolled = buf if rr == 0 else pltpu.roll(buf, rr, 0)
                for q in range(4):
                    s = 8 * q + rr
                    if s > KW - 1:
                        continue
                    k = KW - 1 - s
                    acc = acc + rolled[KWP - 8 * q:KWP - 8 * q + R, :] * w_ref[k:k + 1, :]
            vc_ref[pl.ds(r0, R), :] = acc
            return carry
        lax.fori_loop(0, TP // R, conv_chunk, 0)

    return pl.pallas_call(
        body, name="conf_fwd_conv",
        grid=(NCB,),
        in_specs=[pl.BlockSpec((TP, CB), lambda j: (0, 2 * NCB + j)),
                  pl.BlockSpec((TP, CB), lambda j: (0, 3 * NCB + j)),
                  pl.BlockSpec((KWP, CB), lambda j: (0, j)),
                  pl.BlockSpec((1, CB), lambda j: (0, j))],
        out_specs=pl.BlockSpec((TP, CB), lambda j: (0, j)),
        out_shape=jax.ShapeDtypeStruct((TP, DC), F32),
        scratch_shapes=[pltpu.VMEM((TP + KWP, CB), F32)],
        compiler_params=_cparams(),
    )(z, z, dw_w, dw_b)


def _ln_chunk(vc, lw, lb):
    mu = jnp.mean(vc, axis=-1, keepdims=True)
    xm = vc - mu
    var = jnp.mean(xm * xm, axis=-1, keepdims=True)
    rstd = lax.rsqrt(var + EPS)
    xhat = xm * rstd
    return xhat, rstd, xhat * lw + lb


def _conf_fwd_proj(vc, z, ln_w, ln_b, pw_w, pw_b):
    def body(vc_ref, g_ref, lw_ref, lb_ref, w_ref, b_ref, y_ref, p_ref, s_s):
        lw, lb = lw_ref[...], lb_ref[...]

        def ln_chunk(ci, carry):
            r0 = pl.multiple_of(ci * R, R)
            for half in range(2):
                rr = r0 + 8 * half
                _, _, ln = _ln_chunk(vc_ref[pl.ds(rr, 8), :], lw, lb)
                p_ref[pl.ds(rr, 8), :] = ln * _sig(ln)
            s_s[pl.ds(r0, R), :] = p_ref[pl.ds(r0, R), :].astype(BF16)
            return carry
        lax.fori_loop(0, TM // R, ln_chunk, 0)

        p_ref[...] = jnp.dot(s_s[...], w_ref[...], preferred_element_type=F32) + b_ref[...]

        def out_chunk(ci, carry):
            r0 = pl.multiple_of(ci * R, R)
            g = g_ref[pl.ds(r0, R), :]
            y_ref[pl.ds(r0, R), :] = (p_ref[pl.ds(r0, R), :] * (g * _sig(g))).astype(BF16)
            return carry
        lax.fori_loop(0, TM // R, out_chunk, 0)

    row = pl.BlockSpec((TM, DC), lambda i: (i, 0))
    vec = pl.BlockSpec((1, DC), lambda i: (0, 0))
    return pl.pallas_call(
        body, name="conf_fwd_proj",
        grid=(TP // TM,),
        in_specs=[row, pl.BlockSpec((TM, DC), lambda i: (i, 4)), vec, vec,
                  pl.BlockSpec((DC, DC), lambda i: (0, 0)), vec],
        out_specs=[row, row],
        out_shape=[jax.ShapeDtypeStruct((TP, DC), BF16), jax.ShapeDtypeStruct((TP, DC), F32)],
        scratch_shapes=[pltpu.VMEM((TM, DC), BF16)],
        compiler_params=_cparams(),
    )(vc, z, ln_w, ln_b, pw_w, pw_b)


def _outproj_loss(ylru, yconf, w_out, h, tgt, post_w):
    def body(yl_ref, yc_ref, w_ref, h_ref, t_ref, pw_ref, dout_ref, dy_ref, loss_ref, dpw_ref, y_s):
        i = pl.program_id(0)
        k = pl.program_id(1)

        @pl.when(k == 0)
        def _():
            y_s[...] = jnp.dot(yl_ref[...], w_ref[...], preferred_element_type=F32)

        @pl.when(k == 1)
        def _():
            y_s[...] += jnp.dot(yc_ref[...], w_ref[...], preferred_element_type=F32)

        @pl.when(jnp.logical_and(i == 0, k == 1))
        def _():
            loss_ref[...] = jnp.zeros_like(loss_ref)
            dpw_ref[...] = jnp.zeros_like(dpw_ref)

        @pl.when(k == 1)
        def _():
            pw = pw_ref[...]
            row = _row_iota((8, D))

            def chunk(ci, carry):
                r0 = pl.multiple_of(ci * 8, 8)
                yv = y_s[pl.ds(r0, 8), :]
                rs = lax.rsqrt(jnp.mean(yv * yv, axis=-1, keepdims=True) + EPS)
                grow = row + (i * TM + r0)
                valid = jnp.logical_and(grow >= NMETA, grow < T)
                yn = yv * rs
                err = jnp.where(valid, h_ref[pl.ds(r0, 8), :] + yn * pw - t_ref[pl.ds(r0, 8), :], 0.0)
                loss_ref[...] += err * err
                d_rn = err * (1.0 / D)
                dout_ref[pl.ds(r0, 8), :] = d_rn
                dpw_ref[...] += d_rn * yn
                gw = d_rn * pw
                dot = jnp.mean(gw * yv, axis=-1, keepdims=True)
                dy_ref[pl.ds(r0, 8), :] = (rs * gw - yv * (rs * rs * rs * dot)).astype(BF16)
                return carry
            lax.fori_loop(0, TM // 8, chunk, 0)

    row = pl.BlockSpec((TM, D), lambda i, k: (i, 0))
    half = pl.BlockSpec((TM, DL), lambda i, k: (i, 0))
    acc = pl.BlockSpec((8, D), lambda i, k: (0, 0))
    return pl.pallas_call(
        body, name="outproj_loss",
        grid=(TP // TM, 2),
        in_specs=[half, half, pl.BlockSpec((DL, D), lambda i, k: (k, 0)), row, row,
                  pl.BlockSpec((1, D), lambda i, k: (0, 0))],
        out_specs=[row, row, acc, acc],
        out_shape=[jax.ShapeDtypeStruct((TP, D), F32), jax.ShapeDtypeStruct((TP, D), BF16),
                   jax.ShapeDtypeStruct((8, D), F32), jax.ShapeDtypeStruct((8, D), F32)],
        scratch_shapes=[pltpu.VMEM((TM, D), F32)],
        compiler_params=_cparams(),
    )(ylru, yconf, w_out, h, tgt, post_w)


_NT = (((1,), (1,)), ((), ()))
_TN = (((0,), (0,)), ((), ()))


def _outproj_bwd(dy, ylru, yconf, w_out):
    def body(dy_ref, yl_ref, yc_ref, w_ref, dycat_ref, dw_ref):
        j = pl.program_id(0)
        dyv = dy_ref[...]
        dycat_ref[...] = lax.dot_general(dyv, w_ref[...], _NT, preferred_element_type=F32)

        @pl.when(j < NCB)
        def _():
            dw_ref[...] = lax.dot_general(yl_ref[...], dyv, _TN, preferred_element_type=F32).astype(BF16)

        @pl.when(j >= NCB)
        def _():
            dw_ref[...] = lax.dot_general(yc_ref[...], dyv, _TN, preferred_element_type=F32).astype(BF16)

    return pl.pallas_call(
        body, name="outproj_bwd",
        grid=(2 * NCB,),
        in_specs=[pl.BlockSpec((TP, D), lambda j: (0, 0)),
                  pl.BlockSpec((TP, CB), lambda j: (0, jnp.minimum(j, NCB - 1))),
                  pl.BlockSpec((TP, CB), lambda j: (0, jnp.maximum(j - NCB, 0))),
                  pl.BlockSpec((CB, D), lambda j: (j, 0))],
        out_specs=[pl.BlockSpec((TP, CB), lambda j: (0, j)), pl.BlockSpec((CB, D), lambda j: (j, 0))],
        out_shape=[jax.ShapeDtypeStruct((TP, D), F32), jax.ShapeDtypeStruct((D, D), BF16)],
        compiler_params=_cparams(),
    )(dy, ylru, yconf, w_out)


def _conf_bwd_proj(dycat, p, z, vc, ln_w, ln_b, pw_w):
    def body(dy_ref, p_ref, g_ref, vc_ref, lw_ref, lb_ref, w_ref,
             dvc_ref, dgc_ref, dpw_ref, vecs_ref, dp_s, s_s, ds_s):
        i = pl.program_id(0)
        lw, lb = lw_ref[...], lb_ref[...]

        @pl.when(i == 0)
        def _():
            dpw_ref[...] = jnp.zeros_like(dpw_ref)
            vecs_ref[...] = jnp.zeros_like(vecs_ref)

        def pre_chunk(ci, carry):
            r0 = pl.multiple_of(ci * R, R)
            for half in range(2):
                rr = r0 + 8 * half
                dyv = dy_ref[pl.ds(rr, 8), :]
                g = g_ref[pl.ds(rr, 8), :]
                sg = _sig(g)
                dp = dyv * (g * sg)
                dg = dyv * p_ref[pl.ds(rr, 8), :] * (sg * (1.0 + g * (1.0 - sg)))
                vecs_ref[0:8, :] += dp
                vecs_ref[8:16, :] += dg
                ds_s[pl.ds(rr, 8), :] = dp
                dvc_ref[pl.ds(rr, 8), :] = dg
            dp_s[pl.ds(r0, R), :] = ds_s[pl.ds(r0, R), :].astype(BF16)
            dgc_ref[pl.ds(r0, R), :] = dvc_ref[pl.ds(r0, R), :].astype(BF16)
            for half in range(2):
                rr = r0 + 8 * half
                _, _, ln = _ln_chunk(vc_ref[pl.ds(rr, 8), :], lw, lb)
                ds_s[pl.ds(rr, 8), :] = ln * _sig(ln)
            s_s[pl.ds(r0, R), :] = ds_s[pl.ds(r0, R), :].astype(BF16)
            return carry
        lax.fori_loop(0, TM // R, pre_chunk, 0)

        dpb = dp_s[...]
        ds_s[...] = lax.dot_general(dpb, w_ref[...], _NT, preferred_element_type=F32)
        dpw_ref[...] += lax.dot_general(s_s[...], dpb, _TN, preferred_element_type=F32)

        def post_chunk(ci, carry):
            r0 = pl.multiple_of(ci * 8, 8)
            xhat, rstd, ln = _ln_chunk(vc_ref[pl.ds(r0, 8), :], lw, lb)
            sl = _sig(ln)
            dln = ds_s[pl.ds(r0, 8), :] * (sl * (1.0 + ln * (1.0 - sl)))
            vecs_ref[16:24, :] += dln * xhat
            vecs_ref[24:32, :] += dln
            dxh = dln * lw
            m1 = jnp.mean(dxh, axis=-1, keepdims=True)
            m2 = jnp.mean(dxh * xhat, axis=-1, keepdims=True)
            dvc_ref[pl.ds(r0, 8), :] = rstd * (dxh - m1 - xhat * m2)
            return carry
        lax.fori_loop(0, TM // 8, post_chunk, 0)

    row = pl.BlockSpec((TM, DC), lambda i: (i, 0))
    vec = pl.BlockSpec((1, DC), lambda i: (0, 0))
    return pl.pallas_call(
        body, name="conf_bwd_proj",
        grid=(TP // TM,),
        in_specs=[pl.BlockSpec((TM, DC), lambda i: (i, 1)), row, pl.BlockSpec((TM, DC), lambda i: (i, 4)), row,
                  vec, vec, pl.BlockSpec((DC, DC), lambda i: (0, 0))],
        out_specs=[row, row, pl.BlockSpec((DC, DC), lambda i: (0, 0)), pl.BlockSpec((32, DC), lambda i: (0, 0))],
        out_shape=[jax.ShapeDtypeStruct((TP, DC), F32), jax.ShapeDtypeStruct((TP, DC), BF16),
                   jax.ShapeDtypeStruct((DC, DC), F32), jax.ShapeDtypeStruct((32, DC), F32)],
        scratch_shapes=[pltpu.VMEM((TM, DC), BF16), pltpu.VMEM((TM, DC), BF16), pltpu.VMEM((TM, DC), F32)],
        compiler_params=_cparams(),
    )(dycat, p, z, vc, ln_w, ln_b, pw_w)


def _conf_bwd_conv(dvc, z, dw_w):
    def body(dvc_ref, u1_ref, u2_ref, w_ref, du1_ref, du2_ref, dw_ref, vecs_ref, vs, dvs):
        vs[pl.ds(0, KWP), :] = jnp.zeros((KWP, CB), F32)
        dvs[pl.ds(TP, KWP), :] = jnp.zeros((KWP, CB), F32)
        dw_ref[...] = jnp.zeros_like(dw_ref)
        vecs_ref[...] = jnp.zeros_like(vecs_ref)

        def fill_chunk(ci, carry):
            r0 = pl.multiple_of(ci * R, R)
            vs[pl.ds(KWP + r0, R), :] = u1_ref[pl.ds(r0, R), :] * _sig(u2_ref[pl.ds(r0, R), :])
            dv = dvc_ref[pl.ds(r0, R), :]
            dvs[pl.ds(r0, R), :] = dv
            vecs_ref[0:8, :] += _fold8(dv)
            return carry
        lax.fori_loop(0, TP // R, fill_chunk, 0)

        def conv_chunk(ci, carry):
            r0 = pl.multiple_of(ci * R, R)
            vbuf = vs[pl.ds(r0, KWP + R), :]
            dbuf = dvs[pl.ds(r0, KWP + R), :]
            dcur = dbuf[0:R, :]
            dv = jnp.zeros((R, CB), F32)
            for rr in range(8):
                vroll = vbuf if rr == 0 else pltpu.roll(vbuf, rr, 0)
                droll = dbuf if rr == 0 else pltpu.roll(dbuf, KWP + R - rr, 0)
                for q in range(4):
                    s = 8 * q + rr
                    if s > KW - 1:
                        continue
                    k = KW - 1 - s
                    dv = dv + droll[8 * q:8 * q + R, :] * w_ref[k:k + 1, :]
                    dw_ref[8 * k:8 * k + 8, :] += _fold8(dcur * vroll[KWP - 8 * q:KWP - 8 * q + R, :])
            u1 = u1_ref[pl.ds(r0, R), :]
            sg = _sig(u2_ref[pl.ds(r0, R), :])
            du1 = dv * sg
            du2 = dv * u1 * (sg * (1.0 - sg))
            du1_ref[pl.ds(r0, R), :] = du1.astype(BF16)
            du2_ref[pl.ds(r0, R), :] = du2.astype(BF16)
            vecs_ref[8:16, :] += _fold8(du1)
            vecs_ref[16:24, :] += _fold8(du2)
            return carry
        lax.fori_loop(0, TP // R, conv_chunk, 0)

    blk = pl.BlockSpec((TP, CB), lambda j: (0, j))
    return pl.pallas_call(
        body, name="conf_bwd_conv",
        grid=(NCB,),
        in_specs=[blk, pl.BlockSpec((TP, CB), lambda j: (0, 2 * NCB + j)),
                  pl.BlockSpec((TP, CB), lambda j: (0, 3 * NCB + j)), pl.BlockSpec((KWP, CB), lambda j: (0, j))],
        out_specs=[blk, blk, pl.BlockSpec((8 * KWP, CB), lambda j: (0, j)), pl.BlockSpec((24, CB), lambda j: (0, j))],
        out_shape=[jax.ShapeDtypeStruct((TP, DC), BF16), jax.ShapeDtypeStruct((TP, DC), BF16),
                   jax.ShapeDtypeStruct((8 * KWP, DC), F32), jax.ShapeDtypeStruct((24, DC), F32)],
        scratch_shapes=[pltpu.VMEM((TP + KWP, CB), F32), pltpu.VMEM((TP + KWP, CB), F32)],
        compiler_params=_cparams(),
    )(dvc, z, z, dw_w)


def _lru_bwd(dycat, z, xc, hs, conv_w, wa_g, b_a, wx_g, b_x, lam):
    NV = 6

    def body(dy_ref, x_ref, g_ref, xc_ref, hs_ref, cw_ref, wa_ref, ba_ref, wx_ref, bx_ref, lam_ref,
             dxl_ref, dgl_ref, dwa_ref, dwx_ref, dcw_ref, vecs_ref, ga_s, gx_s, dxc_s):
        vecs_ref[...] = jnp.zeros_like(vecs_ref)
        dcw_ref[...] = jnp.zeros_like(dcw_ref)
        dxc_s[pl.ds(TP, 8), :] = jnp.zeros((8, CB), F32)

        def gate_chunk(ci, carry):
            r0 = pl.multiple_of(ci * TM, TM)
            xb = xc_ref[pl.ds(r0, TM), :].astype(BF16)
            ga_s[pl.ds(r0, TM), :] = jnp.dot(xb, wa_ref[...], preferred_element_type=F32) + ba_ref[...]
            gx_s[pl.ds(r0, TM), :] = jnp.dot(xb, wx_ref[...], preferred_element_type=F32) + bx_ref[...]
            return carry
        lax.fori_loop(0, TP // TM, gate_chunk, 0)

        sp8 = LRU_C * _softplus(-lam_ref[...])
        row = _row_iota((R, CB))
        nchunk = TP // R

        def scan_chunk(cj, carry):
            a_next, lam_next = carry
            ci = nchunk - 1 - cj
            r0 = pl.multiple_of(ci * R, R)
            dyv = dy_ref[pl.ds(r0, R), :]
            g = g_ref[pl.ds(r0, R), :]
            hv = hs_ref[pl.ds(r0, R), :]
            xc = xc_ref[pl.ds(r0, R), :]
            sg = _sig(g)
            dgl = dyv * hv * (sg * (1.0 + g * (1.0 - sg)))
            dgl_ref[pl.ds(r0, R), :] = dgl.astype(BF16)
            vecs_ref[0:8, :] += _fold8(dgl)
            dhs = dyv * (g * sg)
            r, i, a, mult = _gate_values(ga_s[pl.ds(r0, R), :], gx_s[pl.ds(r0, R), :], xc, sp8)
            b = jnp.where(row == R - 1, a_next, pltpu.roll(a, R - 1, 0))
            lv = dhs
            k = 1
            while k < R:
                m = row < R - k
                lv = jnp.where(m, lv + b * pltpu.roll(lv, R - k, 0), lv)
                b = jnp.where(m, b * pltpu.roll(b, R - k, 0), b)
                k *= 2
            lv = lv + b * lam_next
            p0 = pl.multiple_of(jnp.maximum(r0 - 8, 0), 8)
            hprev8 = jnp.where(ci > 0, hs_ref[pl.ds(p0, 8), :], 0.0)
            hprev = pltpu.roll(jnp.concatenate([hprev8, hv], axis=0), 1, 0)[8:8 + R, :]
            da = lv * hprev
            ixc = i * xc
            dmult = lv * ixc
            di = lv * mult * xc
            dxc_s[pl.ds(r0, R), :] = lv * mult * i
            a2 = a * a
            dlog_a = da * a - dmult * a2 / mult
            vecs_ref[32:40, :] += _fold8(dlog_a * r)
            dga = -(dlog_a * sp8) * r * (1.0 - r)
            dgx = di * i * (1.0 - i)
            ga_s[pl.ds(r0, R), :] = dga
            gx_s[pl.ds(r0, R), :] = dgx
            vecs_ref[16:24, :] += _fold8(dga)
            vecs_ref[24:32, :] += _fold8(dgx)
            a_first = jnp.sum(jnp.where(row == 0, a, 0.0), axis=0, keepdims=True)
            l_first = jnp.sum(jnp.where(row == 0, lv, 0.0), axis=0, keepdims=True)
            return a_first, l_first
        lax.fori_loop(0, nchunk, scan_chunk, (jnp.zeros((1, CB), F32), jnp.zeros((1, CB), F32)))

        dwa_ref[...] = jnp.zeros_like(dwa_ref)
        dwx_ref[...] = jnp.zeros_like(dwx_ref)

        def mm_chunk(ci, carry):
            r0 = pl.multiple_of(ci * TM, TM)
            xb = xc_ref[pl.ds(r0, TM), :].astype(BF16)
            dgab = ga_s[pl.ds(r0, TM), :].astype(BF16)
            dgxb = gx_s[pl.ds(r0, TM), :].astype(BF16)
            dxc_s[pl.ds(r0, TM), :] += (lax.dot_general(dgab, wa_ref[...], _NT, preferred_element_type=F32)
                                        + lax.dot_general(dgxb, wx_ref[...], _NT, preferred_element_type=F32))
            dwa_ref[...] += lax.dot_general(xb, dgab, _TN, preferred_element_type=F32)
            dwx_ref[...] += lax.dot_general(xb, dgxb, _TN, preferred_element_type=F32)
            return carry
        lax.fori_loop(0, TP // TM, mm_chunk, 0)

        taps = [cw_ref[k:k + 1, :] for k in range(LW)]

        def conv_chunk(ci, carry):
            r0 = pl.multiple_of(ci * R, R)
            dbuf = dxc_s[pl.ds(r0, R + 8), :]
            dcur = dbuf[0:R, :]
            p0 = pl.multiple_of(jnp.maximum(r0 - 8, 0), 8)
            xprev = jnp.where(ci > 0, x_ref[pl.ds(p0, 8), :], 0.0)
            xbuf = jnp.concatenate([xprev, x_ref[pl.ds(r0, R), :]], axis=0)
            dxl = dcur * taps[LW - 1]
            dcw_ref[8 * (LW - 1):8 * LW, :] += _fold8(dcur * xbuf[8:8 + R, :])
            for s in range(1, LW):
                k = LW - 1 - s
                dxl = dxl + pltpu.roll(dbuf, R + 8 - s, 0)[0:R, :] * taps[k]
                dcw_ref[8 * k:8 * k + 8, :] += _fold8(dcur * pltpu.roll(xbuf, s, 0)[8:8 + R, :])
            dxl_ref[pl.ds(r0, R), :] = dxl.astype(BF16)
            vecs_ref[8:16, :] += _fold8(dxl)
            vecs_ref[40:48, :] += _fold8(dcur)
            return carry
        lax.fori_loop(0, TP // R, conv_chunk, 0)
        vecs_ref[32:40, :] = vecs_ref[32:40, :] * (LRU_C * _sig(-lam_ref[...]))

    col = lambda off: pl.BlockSpec((TP, CB), lambda j: (0, off + j))
    vec = pl.BlockSpec((1, CB), lambda j: (0, j))
    wsp = pl.BlockSpec((None, CB, CB), lambda j: (j, 0, 0))
    return pl.pallas_call(
        body, name="lru_bwd",
        grid=(NCB,),
        in_specs=[col(0), col(0), col(NCB), col(0), col(0), pl.BlockSpec((LW, CB), lambda j: (0, j)),
                  wsp, vec, wsp, vec, vec],
        out_specs=[col(0), col(0), wsp, wsp, pl.BlockSpec((8 * LW, CB), lambda j: (0, j)),
                   pl.BlockSpec((8 * NV, CB), lambda j: (0, j))],
        out_shape=[jax.ShapeDtypeStruct((TP, DL), BF16), jax.ShapeDtypeStruct((TP, DL), BF16),
                   jax.ShapeDtypeStruct((NCB, CB, CB), F32), jax.ShapeDtypeStruct((NCB, CB, CB), F32),
                   jax.ShapeDtypeStruct((8 * LW, DL), F32), jax.ShapeDtypeStruct((8 * NV, DL), F32)],
        scratch_shapes=[pltpu.VMEM((TP, CB), F32), pltpu.VMEM((TP, CB), F32), pltpu.VMEM((TP + 8, CB), F32)],
        compiler_params=_cparams(),
    )(dycat, z, z, xc, hs, conv_w, wa_g, b_a, wx_g, b_x, lam)


def _inproj_wgrad(hn, dz):
    KB = 512

    def body(hn_ref, dz_ref, dw_ref):
        dw_ref[...] = lax.dot_general(hn_ref[...], dz_ref[...], _TN, preferred_element_type=F32).astype(BF16)

    return pl.pallas_call(
        body, name="inproj_wgrad",
        grid=(NIN // 1024, D // KB),
        in_specs=[pl.BlockSpec((TP, KB), lambda n, kb: (0, kb)), pl.BlockSpec((TP, 1024), lambda n, kb: (0, n))],
        out_specs=pl.BlockSpec((KB, 1024), lambda n, kb: (kb, n)),
        out_shape=jax.ShapeDtypeStruct((D, NIN), BF16),
        compiler_params=_cparams(),
    )(hn, dz)


def _inproj_bwd(dz, w_in, h, dout, pre_w):
    nsec = NIN // 1024

    def body(dz_ref, w_ref, h_ref, dout_ref, pw_ref, dh_ref, dpw_ref, acc_s):
        i = pl.program_id(0)
        s = pl.program_id(1)
        part = lax.dot_general(dz_ref[...], w_ref[...], _NT, preferred_element_type=F32)

        @pl.when(s == 0)
        def _():
            acc_s[...] = part

        @pl.when(s > 0)
        def _():
            acc_s[...] += part

        @pl.when(jnp.logical_and(i == 0, s == nsec - 1))
        def _():
            dpw_ref[...] = jnp.zeros_like(dpw_ref)

        @pl.when(s == nsec - 1)
        def _():
            pw = pw_ref[...]

            def chunk(ci, carry):
                r0 = pl.multiple_of(ci * 8, 8)
                hv = h_ref[pl.ds(r0, 8), :]
                dhn = acc_s[pl.ds(r0, 8), :]
                rs = lax.rsqrt(jnp.mean(hv * hv, axis=-1, keepdims=True) + EPS)
                dpw_ref[...] += dhn * (hv * rs)
                gw = dhn * pw
                dot = jnp.mean(gw * hv, axis=-1, keepdims=True)
                dh_ref[pl.ds(r0, 8), :] = rs * gw - hv * (rs * rs * rs * dot) + dout_ref[pl.ds(r0, 8), :]
                return carry
            lax.fori_loop(0, TM // 8, chunk, 0)

    row = pl.BlockSpec((TM, D), lambda i, s: (i, 0))
    return pl.pallas_call(
        body, name="inproj_bwd",
        grid=(TP // TM, nsec),
        in_specs=[pl.BlockSpec((TM, 1024), lambda i, s: (i, s)), pl.BlockSpec((D, 1024), lambda i, s: (0, s)),
                  row, row, pl.BlockSpec((1, D), lambda i, s: (0, 0))],
        out_specs=[row, pl.BlockSpec((8, D), lambda i, s: (0, 0))],
        out_shape=[jax.ShapeDtypeStruct((TP, D), F32), jax.ShapeDtypeStruct((8, D), F32)],
        scratch_shapes=[pltpu.VMEM((TM, D), F32)],
        compiler_params=_cparams(),
    )(dz, w_in, h, dout, pre_w)


def _adamw(name, parts, w, m, v, block_rows):
    rows, cols = w.shape
    cw = cols if cols <= 640 else 512
    c1 = 1.0 / (1.0 - ADAM_B1 ** ADAM_STEP)
    c2 = 1.0 / (1.0 - ADAM_B2 ** ADAM_STEP)

    def body(p_ref, w_ref, m_ref, v_ref, g_ref, d_ref, nm_ref, nv_ref):
        def chunk(ci, carry):
            r0 = pl.multiple_of(ci * R, R)
            for c0 in range(0, cols, cw):
                g = p_ref[0, pl.ds(r0, R), c0:c0 + cw].astype(F32)
                for sidx in range(1, NDEV):
                    g = g + p_ref[sidx, pl.ds(r0, R), c0:c0 + cw].astype(F32)
                wv = w_ref[pl.ds(r0, R), c0:c0 + cw]
                mv = ADAM_B1 * m_ref[pl.ds(r0, R), c0:c0 + cw] + (1.0 - ADAM_B1) * g
                vv = ADAM_B2 * v_ref[pl.ds(r0, R), c0:c0 + cw] + (1.0 - ADAM_B2) * (g * g)
                g_ref[pl.ds(r0, R), c0:c0 + cw] = g
                nm_ref[pl.ds(r0, R), c0:c0 + cw] = mv
                nv_ref[pl.ds(r0, R), c0:c0 + cw] = vv
                upd = (mv * c1) / (jnp.sqrt(vv * c2) + ADAM_EPS) + ADAM_WD * wv
                d_ref[pl.ds(r0, R), c0:c0 + cw] = -ADAM_LR * upd
            return carry
        lax.fori_loop(0, block_rows // R, chunk, 0)

    blk = pl.BlockSpec((block_rows, cols), lambda i: (i, 0))
    shp = jax.ShapeDtypeStruct((rows, cols), F32)
    return pl.pallas_call(
        body, name=name,
        grid=(rows // block_rows,),
        in_specs=[pl.BlockSpec((NDEV, block_rows, cols), lambda i: (0, i, 0)), blk, blk, blk],
        out_specs=[blk, blk, blk, blk],
        out_shape=[shp, shp, shp, shp],
        compiler_params=_cparams(),
    )(parts, w, m, v)


_REP = ["pre_norm_w", "post_norm_w", "b_in", "lru_conv_b", "w_gate_a", "b_gate_a", "w_gate_x", "b_gate_x",
        "lru_lambda", "conf_dw_b", "conf_ln_w", "conf_ln_b", "conf_pw_b"]
_REP_SIZE = {"pre_norm_w": D, "post_norm_w": D, "b_in": NIN, "w_gate_a": 16 * 64 * 64, "w_gate_x": 16 * 64 * 64}
_REP_ROWS = 1168
_SM_ROWS = 64


def _pack_rep(d):
    flat = jnp.concatenate([d[n].reshape(-1) for n in _REP])
    return jnp.pad(flat, (0, _REP_ROWS * 128 - flat.shape[0])).reshape(_REP_ROWS, 128)


def _unpack_rep(vec, shapes):
    flat = vec.reshape(-1)
    out, off = {}, 0
    for n in _REP:
        size = _REP_SIZE.get(n, 1024)
        out[n] = flat[off:off + size].reshape(shapes[n])
        off += size
    return out


def _pack_small(lru_cw, dw_w, meta):
    buf = jnp.zeros((_SM_ROWS, 256), F32)
    buf = buf.at[0:LW, 0:128].set(lru_cw)
    buf = buf.at[8:8 + dw_w.shape[0], 0:128].set(dw_w)
    return buf.at[40:56, :].set(meta)


def _block_diag4(w):
    w4 = w.reshape(NCB, 4, 64, 64)
    eye = jnp.eye(4, dtype=w.dtype)
    return jnp.einsum("ghij,hk->ghikj", w4, eye).reshape(NCB, CB, CB)


def _diag_blocks(g):
    g5 = g.reshape(NCB, 4, 64, 4, 64)
    return jnp.stack([g5[:, hh, :, hh, :] for hh in range(4)], axis=1).reshape(16, 64, 64)


def _local_step(x, target, meta_full, win_full, out_weights, lru_cw_full, dw_w_full, W, send):
    h = jnp.concatenate([meta_full, x, jnp.zeros((TP - T, D), F32)], axis=0)
    tgt = jnp.concatenate([jnp.zeros((NMETA, D), F32), target, jnp.zeros((TP - T, D), F32)], axis=0)
    wa_g = _block_diag4(W["w_gate_a"][0]).astype(BF16)
    wx_g = _block_diag4(W["w_gate_x"][0]).astype(BF16)

    z, hn = _prenorm_inproj(h, W["pre_norm_w"], win_full, W["b_in"])
    ylru, xc, hs = _lru_fwd(z, lru_cw_full, W["lru_conv_b"], wa_g, W["b_gate_a"], wx_g, W["b_gate_x"],
                            W["lru_lambda"])
    vc = _conf_fwd_conv(z, dw_w_full, W["conf_dw_b"])
    wout_full, pw_full = out_weights(vc)
    yconf, p = _conf_fwd_proj(vc, z, W["conf_ln_w"], W["conf_ln_b"], pw_full, W["conf_pw_b"])
    dout, dy, loss_acc, dpostw_acc = _outproj_loss(ylru, yconf, wout_full, h, tgt, W["post_norm_w"])
    loss_local = 0.5 / D * jnp.sum(loss_acc)

    dycat, dwout_part = _outproj_bwd(dy, ylru, yconf, wout_full)
    tok = send("w_out", dwout_part)
    dvc, dgc, dpw_part, cvecs = _conf_bwd_proj(dycat, p, z, vc, W["conf_ln_w"] + tok, W["conf_ln_b"], pw_full)
    tok = send("conf_pw_w", dpw_part)
    du1, du2, ddw_acc, kvecs = _conf_bwd_conv(dvc, z, dw_w_full + tok)
    dxl, dgl, dwa_g, dwx_g, dcw_acc, lvecs = _lru_bwd(dycat, z, xc, hs, lru_cw_full, wa_g, W["b_gate_a"], wx_g,
                                                      W["b_gate_x"], W["lru_lambda"])
    dz = jnp.concatenate([dxl, dgl, du1, du2, dgc], axis=1)
    tok = send("w_in", _inproj_wgrad(hn, dz))
    dh, dprew_acc = _inproj_bwd(dz, win_full, h, dout, W["pre_norm_w"] + tok)

    sum8 = lambda a: jnp.sum(a.reshape(-1, 8, a.shape[-1]), axis=1)
    cv, kv, lv = sum8(cvecs), sum8(kvecs), sum8(lvecs)
    rep_part = dict(
        pre_norm_w=jnp.sum(dprew_acc, axis=0), post_norm_w=jnp.sum(dpostw_acc, axis=0),
        b_in=jnp.concatenate([lv[1], lv[0], kv[1], kv[2], cv[1]]),
        lru_conv_b=lv[5], w_gate_a=_diag_blocks(dwa_g), b_gate_a=lv[2], w_gate_x=_diag_blocks(dwx_g),
        b_gate_x=lv[3], lru_lambda=lv[4],
        conf_dw_b=kv[0], conf_ln_w=cv[2], conf_ln_b=cv[3], conf_pw_b=cv[0])
    return loss_local, dh, sum8(dcw_acc), sum8(ddw_acc), rep_part


def kernel(x, meta_tokens, pre_norm_w, post_norm_w, w_in, b_in, lru_conv_w, lru_conv_b, w_gate_a, b_gate_a, w_gate_x, b_gate_x, lru_lambda, conf_dw_w, conf_dw_b, conf_ln_w, conf_ln_b, conf_pw_w, conf_pw_b, w_out, loss_target, m_meta_tokens, m_pre_norm_w, m_post_norm_w, m_w_in, m_b_in, m_lru_conv_w, m_lru_conv_b, m_w_gate_a, m_b_gate_a, m_w_gate_x, m_b_gate_x, m_lru_lambda, m_conf_dw_w, m_conf_dw_b, m_conf_ln_w, m_conf_ln_b, m_conf_pw_w, m_conf_pw_b, m_w_out, v_meta_tokens, v_pre_norm_w, v_post_norm_w, v_w_in, v_b_in, v_lru_conv_w, v_lru_conv_b, v_w_gate_a, v_b_gate_a, v_w_gate_x, v_b_gate_x, v_lru_lambda, v_conf_dw_w, v_conf_dw_b, v_conf_ln_w, v_conf_ln_b, v_conf_pw_w, v_conf_pw_b, v_w_out):
    W = dict(meta_tokens=meta_tokens, pre_norm_w=pre_norm_w, post_norm_w=post_norm_w, w_in=w_in, b_in=b_in,
             lru_conv_w=lru_conv_w, lru_conv_b=lru_conv_b, w_gate_a=w_gate_a, b_gate_a=b_gate_a,
             w_gate_x=w_gate_x, b_gate_x=b_gate_x, lru_lambda=lru_lambda, conf_dw_w=conf_dw_w,
             conf_dw_b=conf_dw_b, conf_ln_w=conf_ln_w, conf_ln_b=conf_ln_b, conf_pw_w=conf_pw_w,
             conf_pw_b=conf_pw_b, w_out=w_out)
    M = dict(meta_tokens=m_meta_tokens, pre_norm_w=m_pre_norm_w, post_norm_w=m_post_norm_w, w_in=m_w_in,
             b_in=m_b_in, lru_conv_w=m_lru_conv_w, lru_conv_b=m_lru_conv_b, w_gate_a=m_w_gate_a,
             b_gate_a=m_b_gate_a, w_gate_x=m_w_gate_x, b_gate_x=m_b_gate_x, lru_lambda=m_lru_lambda,
             conf_dw_w=m_conf_dw_w, conf_dw_b=m_conf_dw_b, conf_ln_w=m_conf_ln_w, conf_ln_b=m_conf_ln_b,
             conf_pw_w=m_conf_pw_w, conf_pw_b=m_conf_pw_b, w_out=m_w_out)
    V = dict(meta_tokens=v_meta_tokens, pre_norm_w=v_pre_norm_w, post_norm_w=v_post_norm_w, w_in=v_w_in,
             b_in=v_b_in, lru_conv_w=v_lru_conv_w, lru_conv_b=v_lru_conv_b, w_gate_a=v_w_gate_a,
             b_gate_a=v_b_gate_a, w_gate_x=v_w_gate_x, b_gate_x=v_b_gate_x, lru_lambda=v_lru_lambda,
             conf_dw_w=v_conf_dw_w, conf_dw_b=v_conf_dw_b, conf_ln_w=v_conf_ln_w, conf_ln_b=v_conf_ln_b,
             conf_pw_w=v_conf_pw_w, conf_pw_b=v_conf_pw_b, w_out=v_w_out)
    names = list(W.keys())
    shapes = {n: W[n].shape for n in names}

    small = _pack_small(lru_conv_w[0], conf_dw_w[0], meta_tokens)
    gathered, _ = _exchange_start("gather_start", [
        (small, jax.ShapeDtypeStruct((NDEV, _SM_ROWS, 256), F32), _whole, _slot),
        (w_in[0].astype(BF16), jax.ShapeDtypeStruct((D, NIN), BF16), _whole, _cols(NIN // NDEV)),
        (w_out[0].astype(BF16), jax.ShapeDtypeStruct((D, D), BF16), _whole, _rows(D // NDEV)),
        (conf_pw_w[0].astype(BF16), jax.ShapeDtypeStruct((DC, DC), BF16), _whole, _rows(DC // NDEV)),
    ])
    small_all, win_full = _exchange_wait("gather_wait_in", gathered[0:2], x)
    unshard = lambda a: jnp.transpose(a, (1, 0, 2)).reshape(a.shape[1], -1)
    lru_cw_full = unshard(small_all[:, 0:LW, 0:128])
    dw_w_full = unshard(small_all[:, 8:8 + KWP, 0:128])
    meta_full = unshard(small_all[:, 40:56, :])

    def out_weights(after):
        return _exchange_wait("gather_wait_out", gathered[2:4], after)

    piece = {"w_in": (jax.ShapeDtypeStruct((NDEV, D, NIN // NDEV), BF16), _cols(NIN // NDEV)),
             "w_out": (jax.ShapeDtypeStruct((NDEV, D // NDEV, D), BF16), _rows(D // NDEV)),
             "conf_pw_w": (jax.ShapeDtypeStruct((NDEV, DC // NDEV, DC), BF16), _rows(DC // NDEV))}
    sent = {}

    def send(name, part):
        handles, token = _exchange_start("scatter_" + name + "_start",
                                         [(part.astype(BF16), piece[name][0], piece[name][1], _slot)])
        sent[name] = handles
        return token[0, 0]

    loss_local, dh, dcw_part, ddw_part, rep_part = _local_step(
        x[0], loss_target[0], meta_full, win_full, out_weights, lru_cw_full, dw_w_full, W, send)
    loss = lax.psum(loss_local, ("x", "y", "c"))
    grad_x = dh[NMETA:T][None]
    dmeta_part = dh[0:NMETA]
    shard = lambda a, wdt: jnp.transpose(a.reshape(a.shape[0], NDEV, wdt), (1, 0, 2))
    small_part = jnp.zeros((NDEV, _SM_ROWS, 256), F32)
    small_part = small_part.at[:, 0:LW, 0:128].set(shard(dcw_part, 128))
    small_part = small_part.at[:, 8:8 + KWP, 0:128].set(shard(ddw_part, 128))
    small_part = small_part.at[:, 40:56, :].set(shard(dmeta_part, 256))

    rest, _ = _exchange_start("scatter_rest_start", [
        (small_part, jax.ShapeDtypeStruct((NDEV, _SM_ROWS, 256), F32), _slot, _slot),
        (_pack_rep(rep_part), jax.ShapeDtypeStruct((NDEV, _REP_ROWS, 128), F32), _whole, _slot),
    ])

    G, DW, NM, NV = {}, {}, {}, {}
    (wout_parts,) = _exchange_wait("scatter_w_out_wait", sent["w_out"], dh)
    G["w_out"], DW["w_out"], NM["w_out"], NV["w_out"] = _adamw("adamw_w_out", wout_parts, w_out[0], m_w_out[0], v_w_out[0], 64)
    (pw_parts,) = _exchange_wait("scatter_conf_pw_w_wait", sent["conf_pw_w"], G["w_out"])
    G["conf_pw_w"], DW["conf_pw_w"], NM["conf_pw_w"], NV["conf_pw_w"] = _adamw(
        "adamw_pw", pw_parts, conf_pw_w[0], m_conf_pw_w[0], v_conf_pw_w[0], 128)
    (win_parts,) = _exchange_wait("scatter_w_in_wait", sent["w_in"], G["conf_pw_w"])
    G["w_in"], DW["w_in"], NM["w_in"], NV["w_in"] = _adamw("adamw_w_in", win_parts, w_in[0], m_w_in[0], v_w_in[0], 256)
    small_parts, rep_parts = _exchange_wait("scatter_rest_wait", rest, G["w_in"])
    sm = _adamw("adamw_small", small_parts, small,
                _pack_small(m_lru_conv_w[0], m_conf_dw_w[0], m_meta_tokens),
                _pack_small(v_lru_conv_w[0], v_conf_dw_w[0], v_meta_tokens), _SM_ROWS)
    rp = _adamw("adamw_rep", rep_parts, _pack_rep(W), _pack_rep(M), _pack_rep(V), _REP_ROWS)
    for dst, s_arr, r_arr in zip((G, DW, NM, NV), sm, rp):
        dst["lru_conv_w"] = s_arr[0:LW, 0:128][None]
        dst["conf_dw_w"] = s_arr[8:8 + KW, 0:128][None]
        dst["meta_tokens"] = s_arr[40:56, :]
        dst.update(_unpack_rep(r_arr, shapes))
        for n in ("w_in", "w_out", "conf_pw_w"):
            dst[n] = dst[n].reshape(shapes[n])

    return (loss, grad_x, *[G[n] for n in names], *[DW[n] for n in names],
            *[NM[n] for n in names], *[NV[n] for n in names])
```

```python
import functools

import jax
import jax.numpy as jnp
from jax import lax
from jax.experimental import pallas as pl
from jax.experimental.pallas import tpu as pltpu

F32 = jnp.float32
BF16 = jnp.bfloat16

D = 2048
DL = 1024
DC = 1024
NIN = 5120
NMETA = 16
SEQ = 2048
T = NMETA + SEQ
TP = 2176
TM = 544
CB = 256
NCB = DL // CB
R = 16
KW = 31
KWP = 32
LW = 4
LRU_C = 8.0
EPS = 1e-6
NDEV = 8

ADAM_LR = 0.001
ADAM_B1 = 0.9
ADAM_B2 = 0.999
ADAM_EPS = 1e-08
ADAM_WD = 0.01
ADAM_STEP = 10

VMEM_LIMIT = 56 * 1024 * 1024


def _cparams():
    return pltpu.CompilerParams(vmem_limit_bytes=VMEM_LIMIT)


def _sig(x):
    return 1.0 / (1.0 + jnp.exp(-x))


def _expm1_neg(y):
    poly = y * (1.0 + y * (0.5 + y * (1.0 / 6.0 + y * (1.0 / 24.0 + y * (1.0 / 120.0)))))
    return jnp.where(y > -0.1, poly, jnp.exp(y) - 1.0)


def _softplus(x):
    e = jnp.exp(-jnp.abs(x))
    w = 1.0 + e
    l1p = jnp.where(w == 1.0, e, jnp.log(w) * e / (w - 1.0))
    return jnp.maximum(x, 0.0) + l1p


def _row_iota(shape):
    return lax.broadcasted_iota(jnp.int32, shape, 0)


def _fold8(v):
    return v[0:8, :] + v[8:16, :]


_FLIPS = [(k >> 2 & 1, k >> 1 & 1, k & 1) for k in range(1, NDEV)]
_HBM = pl.BlockSpec(memory_space=pltpu.HBM)
_SEM = pl.BlockSpec(memory_space=pltpu.SEMAPHORE)


def _peers():
    x, y, c = lax.axis_index("x"), lax.axis_index("y"), lax.axis_index("c")
    out = []
    for dx, dy, dc in _FLIPS:
        px = 1 - x if dx else x
        py = 1 - y if dy else y
        pc = 1 - c if dc else c
        out.append(((px, py, pc), 4 * px + 2 * py + pc))
    return 4 * x + 2 * y + c, out


def _exchange_start(name, items):
    n = len(items)

    def body(*refs):
        srcs, lands = refs[:n], refs[n:2 * n]
        outs = refs[2 * n:]
        send_sems, recv_sems, local_sems = outs[:n], outs[n:2 * n], outs[2 * n:3 * n]
        token = outs[-1]
        me, peers = _peers()
        for a in range(n):
            src_at, dst_at = items[a][2], items[a][3]
            pltpu.make_async_copy(src_at(srcs[a], me), dst_at(lands[a], me), local_sems[a]).start()
        for a in range(n):
            src_at, dst_at = items[a][2], items[a][3]
            for k, (pos, peer) in enumerate(peers):
                pltpu.make_async_remote_copy(
                    src_ref=src_at(srcs[a], peer), dst_ref=dst_at(lands[a], me),
                    send_sem=send_sems[a].at[k], recv_sem=recv_sems[a].at[k],
                    device_id=pos, device_id_type=pl.DeviceIdType.MESH).start()
        token[...] = jnp.zeros_like(token)

    srcs = [pltpu.with_memory_space_constraint(it[0], pltpu.HBM) for it in items]
    lands = [pltpu.with_memory_space_constraint(lax.empty(it[1].shape, it[1].dtype), pltpu.HBM) for it in items]
    sem7 = pltpu.SemaphoreType.DMA((NDEV - 1,))
    res = pl.pallas_call(
        body, name=name,
        out_shape=([sem7] * (2 * n) + [pltpu.SemaphoreType.DMA(())] * n
                   + [pltpu.HBM(a.shape, a.dtype) for a in srcs] + [pltpu.HBM(a.shape, a.dtype) for a in lands]
                   + [jax.ShapeDtypeStruct((8, 128), F32)]),
        in_specs=[_HBM] * (2 * n),
        out_specs=[_SEM] * (3 * n) + [_HBM] * (2 * n) + [pl.BlockSpec(memory_space=pltpu.VMEM)],
        input_output_aliases={i: 3 * n + i for i in range(2 * n)},
        compiler_params=pltpu.CompilerParams(has_side_effects=pltpu.SideEffectType.DATAFLOW_SIDE_EFFECTING),
    )(*srcs, *lands)
    handles = [dict(send=res[a], recv=res[n + a], local=res[2 * n + a], src=res[3 * n + a], land=res[4 * n + a],
                    src_at=items[a][2], dst_at=items[a][3]) for a in range(n)]
    return handles, res[-1]


def _exchange_wait(name, handles, after):
    n = len(handles)

    def body(*refs):
        srcs, lands = refs[:n], refs[n:2 * n]
        send_sems, recv_sems, local_sems = refs[2 * n:3 * n], refs[3 * n:4 * n], refs[4 * n:5 * n]
        me, peers = _peers()
        for a in range(n):
            src_at, dst_at = handles[a]["src_at"], handles[a]["dst_at"]
            for k, (pos, peer) in enumerate(peers):
                cp = pltpu.make_async_remote_copy(
                    src_ref=src_at(srcs[a], peer), dst_ref=dst_at(lands[a], peer),
                    send_sem=send_sems[a].at[k], recv_sem=recv_sems[a].at[k],
                    device_id=pos, device_id_type=pl.DeviceIdType.MESH)
                cp.wait_send()
                cp.wait_recv()
            pltpu.make_async_copy(src_at(srcs[a], me), dst_at(lands[a], me), local_sems[a]).wait()

    srcs = [hd["src"] for hd in handles]
    lands = [hd["land"] for hd in handles]
    res = pl.pallas_call(
        body, name=name,
        out_shape=[pltpu.HBM(a.shape, a.dtype) for a in srcs] + [pltpu.HBM(a.shape, a.dtype) for a in lands],
        in_specs=[_HBM] * (2 * n) + [_SEM] * (3 * n) + [pl.BlockSpec(memory_space=pl.ANY)],
        out_specs=[_HBM] * (2 * n),
        input_output_aliases={i: i for i in range(2 * n)},
        compiler_params=pltpu.CompilerParams(has_side_effects=pltpu.SideEffectType.DATAFLOW_SIDE_EFFECTING),
    )(*srcs, *lands, *[hd["send"] for hd in handles], *[hd["recv"] for hd in handles],
      *[hd["local"] for hd in handles], after)
    return list(res[n:])


def _whole(ref, l):
    return ref


def _slot(ref, l):
    return ref.at[l]


def _cols(width):
    def at(ref, l):
        return ref.at[:, pl.ds(pl.multiple_of(l * width, 128), width)]
    return at


def _rows(height):
    def at(ref, l):
        return ref.at[pl.ds(pl.multiple_of(l * height, 8), height), :]
    return at


def _prenorm_inproj(h, pre_w, w_in, b_in):
    nsec = NIN // 1024

    def body(h_ref, pw_ref, w_ref, b_ref, z_ref, hn_ref):
        @pl.when(pl.program_id(1) == 0)
        def _():
            pw = pw_ref[...]

            def chunk(ci, carry):
                r0 = pl.multiple_of(ci * R, R)
                xv = h_ref[pl.ds(r0, R), :]
                ms = jnp.mean(xv * xv, axis=-1, keepdims=True)
                hn_ref[pl.ds(r0, R), :] = (xv * lax.rsqrt(ms + EPS) * pw).astype(BF16)
                return carry
            lax.fori_loop(0, TM // R, chunk, 0, unroll=2)

        z_ref[...] = jnp.dot(hn_ref[...], w_ref[...], preferred_element_type=F32) + b_ref[...]

    return pl.pallas_call(
        body, name="prenorm_inproj",
        grid=(TP // TM, nsec),
        in_specs=[pl.BlockSpec((TM, D), lambda i, n: (i, 0)),
                  pl.BlockSpec((1, D), lambda i, n: (0, 0)),
                  pl.BlockSpec((D, 1024), lambda i, n: (0, n)),
                  pl.BlockSpec((1, 1024), lambda i, n: (0, n))],
        out_specs=[pl.BlockSpec((TM, 1024), lambda i, n: (i, n)),
                   pl.BlockSpec((TM, D), lambda i, n: (i, 0))],
        out_shape=[jax.ShapeDtypeStruct((TP, NIN), F32), jax.ShapeDtypeStruct((TP, D), BF16)],
        compiler_params=_cparams(),
    )(h, pre_w, w_in, b_in)


def _gate_values(ga, gx, xc, sp8):
    r = _sig(ga)
    i = _sig(gx)
    log_a = -(r * sp8)
    a = jnp.exp(log_a)
    mult = jnp.sqrt(-_expm1_neg(2.0 * log_a))
    return r, i, a, mult


def _lru_fwd(z, conv_w, conv_b, wa_g, b_a, wx_g, b_x, lam):
    def body(x_ref, g_ref, cw_ref, cb_ref, wa_ref, ba_ref, wx_ref, bx_ref, lam_ref,
             y_ref, xc_ref, hs_ref, ga_s, gx_s):
        taps = [cw_ref[k:k + 1, :] for k in range(LW)]
        cb = cb_ref[...]

        def conv_chunk(ci, carry):
            r0 = pl.multiple_of(ci * R, R)
            cur = x_ref[pl.ds(r0, R), :]
            p0 = pl.multiple_of(jnp.maximum(r0 - 8, 0), 8)
            prev = jnp.where(ci > 0, x_ref[pl.ds(p0, 8), :], 0.0)
            buf = jnp.concatenate([prev, cur], axis=0)
            acc = cur * taps[LW - 1] + cb
            for s in range(1, LW):
                acc = acc + pltpu.roll(buf, s, 0)[8:8 + R, :] * taps[LW - 1 - s]
            xc_ref[pl.ds(r0, R), :] = acc
            return carry
        lax.fori_loop(0, TP // R, conv_chunk, 0)

        def gate_chunk(ci, carry):
            r0 = pl.multiple_of(ci * TM, TM)
            xb = xc_ref[pl.ds(r0, TM), :].astype(BF16)
            ga_s[pl.ds(r0, TM), :] = jnp.dot(xb, wa_ref[...], preferred_element_type=F32) + ba_ref[...]
            gx_s[pl.ds(r0, TM), :] = jnp.dot(xb, wx_ref[...], preferred_element_type=F32) + bx_ref[...]
            return carry
        lax.fori_loop(0, TP // TM, gate_chunk, 0)

        sp8 = LRU_C * _softplus(-lam_ref[...])
        row = _row_iota((R, CB))

        def scan_chunk(ci, hprev):
            r0 = pl.multiple_of(ci * R, R)
            xc = xc_ref[pl.ds(r0, R), :]
            _, i, a, mult = _gate_values(ga_s[pl.ds(r0, R), :], gx_s[pl.ds(r0, R), :], xc, sp8)
            u = mult * (i * xc)
            k = 1
            while k < R:
                m = row >= k
                u = jnp.where(m, a * pltpu.roll(u, k, 0) + u, u)
                a = jnp.where(m, a * pltpu.roll(a, k, 0), a)
                k *= 2
            hv = u + a * hprev
            hs_ref[pl.ds(r0, R), :] = hv
            g = g_ref[pl.ds(r0, R), :]
            y_ref[pl.ds(r0, R), :] = (hv * (g * _sig(g))).astype(BF16)
            return jnp.sum(jnp.where(row == R - 1, hv, 0.0), axis=0, keepdims=True)
        lax.fori_loop(0, TP // R, scan_chunk, jnp.zeros((1, CB), F32))

    col = lambda off: pl.BlockSpec((TP, CB), lambda j: (0, off + j))
    vec = pl.BlockSpec((1, CB), lambda j: (0, j))
    wsp = pl.BlockSpec((None, CB, CB), lambda j: (j, 0, 0))
    return pl.pallas_call(
        body, name="lru_fwd",
        grid=(NCB,),
        in_specs=[col(0), col(NCB), pl.BlockSpec((LW, CB), lambda j: (0, j)), vec, wsp, vec, wsp, vec, vec],
        out_specs=[col(0), col(0), col(0)],
        out_shape=[jax.ShapeDtypeStruct((TP, DL), BF16), jax.ShapeDtypeStruct((TP, DL), F32),
                   jax.ShapeDtypeStruct((TP, DL), F32)],
        scratch_shapes=[pltpu.VMEM((TP, CB), F32), pltpu.VMEM((TP, CB), F32)],
        compiler_params=_cparams(),
    )(z, z, conv_w, conv_b, wa_g, b_a, wx_g, b_x, lam)


def _conf_fwd_conv(z, dw_w, dw_b):
    def body(u1_ref, u2_ref, w_ref, b_ref, vc_ref, vs):
        vs[pl.ds(0, KWP), :] = jnp.zeros((KWP, CB), F32)

        def glu_chunk(ci, carry):
            r0 = pl.multiple_of(ci * R, R)
            vs[pl.ds(KWP + r0, R), :] = u1_ref[pl.ds(r0, R), :] * _sig(u2_ref[pl.ds(r0, R), :])
            return carry
        lax.fori_loop(0, TP // R, glu_chunk, 0)

        bias = b_ref[...]

        def conv_chunk(ci, carry):
            r0 = pl.multiple_of(ci * R, R)
            buf = vs[pl.ds(r0, KWP + R), :]
            acc = jnp.zeros((R, CB), F32) + bias
            for rr in range(8):
                rolled = buf if rr == 0 else pltpu.roll(buf, rr, 0)
                for q in range(4):
                    s = 8 * q + rr
                    if s > KW - 1:
                        continue
                    k = KW - 1 - s
                    acc = acc + rolled[KWP - 8 * q:KWP - 8 * q + R, :] * w_ref[k:k + 1, :]
            vc_ref[pl.ds(r0, R), :] = acc
            return carry
        lax.fori_loop(0, TP // R, conv_chunk, 0)

    return pl.pallas_call(
        body, name="conf_fwd_conv",
        grid=(NCB,),
        in_specs=[pl.BlockSpec((TP, CB), lambda j: (0, 2 * NCB + j)),
                  pl.BlockSpec((TP, CB), lambda j: (0, 3 * NCB + j)),
                  pl.BlockSpec((KWP, CB), lambda j: (0, j)),
                  pl.BlockSpec((1, CB), lambda j: (0, j))],
        out_specs=pl.BlockSpec((TP, CB), lambda j: (0, j)),
        out_shape=jax.ShapeDtypeStruct((TP, DC), F32),
        scratch_shapes=[pltpu.VMEM((TP + KWP, CB), F32)],
        compiler_params=_cparams(),
    )(z, z, dw_w, dw_b)


def _ln_chunk(vc, lw, lb):
    mu = jnp.mean(vc, axis=-1, keepdims=True)
    xm = vc - mu
    var = jnp.mean(xm * xm, axis=-1, keepdims=True)
    rstd = lax.rsqrt(var + EPS)
    xhat = xm * rstd
    return xhat, rstd, xhat * lw + lb


def _conf_fwd_proj(vc, z, ln_w, ln_b, pw_w, pw_b):
    def body(vc_ref, g_ref, lw_ref, lb_ref, w_ref, b_ref, y_ref, p_ref, s_s):
        lw, lb = lw_ref[...], lb_ref[...]

        def ln_chunk(ci, carry):
            r0 = pl.multiple_of(ci * R, R)
            for half in range(2):
                rr = r0 + 8 * half
                _, _, ln = _ln_chunk(vc_ref[pl.ds(rr, 8), :], lw, lb)
                p_ref[pl.ds(rr, 8), :] = ln * _sig(ln)
            s_s[pl.ds(r0, R), :] = p_ref[pl.ds(r0, R), :].astype(BF16)
            return carry
        lax.fori_loop(0, TM // R, ln_chunk, 0, unroll=2)

        p_ref[...] = jnp.dot(s_s[...], w_ref[...], preferred_element_type=F32) + b_ref[...]

        def out_chunk(ci, carry):
            r0 = pl.multiple_of(ci * R, R)
            g = g_ref[pl.ds(r0, R), :]
            y_ref[pl.ds(r0, R), :] = (p_ref[pl.ds(r0, R), :] * (g * _sig(g))).astype(BF16)
            return carry
        lax.fori_loop(0, TM // R, out_chunk, 0)

    row = pl.BlockSpec((TM, DC), lambda i: (i, 0))
    vec = pl.BlockSpec((1, DC), lambda i: (0, 0))
    return pl.pallas_call(
        body, name="conf_fwd_proj",
        grid=(TP // TM,),
        in_specs=[row, pl.BlockSpec((TM, DC), lambda i: (i, 4)), vec, vec,
                  pl.BlockSpec((DC, DC), lambda i: (0, 0)), vec],
        out_specs=[row, row],
        out_shape=[jax.ShapeDtypeStruct((TP, DC), BF16), jax.ShapeDtypeStruct((TP, DC), F32)],
        scratch_shapes=[pltpu.VMEM((TM, DC), BF16)],
        compiler_params=_cparams(),
    )(vc, z, ln_w, ln_b, pw_w, pw_b)


def _outproj_loss(ylru, yconf, w_out, h, tgt, post_w):
    def body(yl_ref, yc_ref, w_ref, h_ref, t_ref, pw_ref, dout_ref, dy_ref, loss_ref, dpw_ref, y_s):
        i = pl.program_id(0)
        k = pl.program_id(1)

        @pl.when(k == 0)
        def _():
            y_s[...] = jnp.dot(yl_ref[...], w_ref[...], preferred_element_type=F32)

        @pl.when(k == 1)
        def _():
            y_s[...] += jnp.dot(yc_ref[...], w_ref[...], preferred_element_type=F32)

        @pl.when(jnp.logical_and(i == 0, k == 1))
        def _():
            loss_ref[...] = jnp.zeros_like(loss_ref)
            dpw_ref[...] = jnp.zeros_like(dpw_ref)

        @pl.when(k == 1)
        def _():
            pw = pw_ref[...]
            row = _row_iota((8, D))

            def chunk(ci, carry):
                r0 = pl.multiple_of(ci * 8, 8)
                yv = y_s[pl.ds(r0, 8), :]
                rs = lax.rsqrt(jnp.mean(yv * yv, axis=-1, keepdims=True) + EPS)
                grow = row + (i * TM + r0)
                valid = jnp.logical_and(grow >= NMETA, grow < T)
                yn = yv * rs
                err = jnp.where(valid, h_ref[pl.ds(r0, 8), :] + yn * pw - t_ref[pl.ds(r0, 8), :], 0.0)
                loss_ref[...] += err * err
                d_rn = err * (1.0 / D)
                dout_ref[pl.ds(r0, 8), :] = d_rn
                dpw_ref[...] += d_rn * yn
                gw = d_rn * pw
                dot = jnp.mean(gw * yv, axis=-1, keepdims=True)
                dy_ref[pl.ds(r0, 8), :] = (rs * gw - yv * (rs * rs * rs * dot)).astype(BF16)
                return carry
            lax.fori_loop(0, TM // 8, chunk, 0, unroll=4)

    row = pl.BlockSpec((TM, D), lambda i, k: (i, 0))
    half = pl.BlockSpec((TM, DL), lambda i, k: (i, 0))
    acc = pl.BlockSpec((8, D), lambda i, k: (0, 0))
    return pl.pallas_call(
        body, name="outproj_loss",
        grid=(TP // TM, 2),
        in_specs=[half, half, pl.BlockSpec((DL, D), lambda i, k: (k, 0)), row, row,
                  pl.BlockSpec((1, D), lambda i, k: (0, 0))],
        out_specs=[row, row, acc, acc],
        out_shape=[jax.ShapeDtypeStruct((TP, D), F32), jax.ShapeDtypeStruct((TP, D), BF16),
                   jax.ShapeDtypeStruct((8, D), F32), jax.ShapeDtypeStruct((8, D), F32)],
        scratch_shapes=[pltpu.VMEM((TM, D), F32)],
        compiler_params=_cparams(),
    )(ylru, yconf, w_out, h, tgt, post_w)


_NT = (((1,), (1,)), ((), ()))
_TN = (((0,), (0,)), ((), ()))


def _outproj_bwd(dy, ylru, yconf, w_out):
    def body(dy_ref, yl_ref, yc_ref, w_ref, dycat_ref, dw_ref):
        j = pl.program_id(0)
        dyv = dy_ref[...]
        dycat_ref[...] = lax.dot_general(dyv, w_ref[...], _NT, preferred_element_type=F32)

        @pl.when(j < NCB)
        def _():
            dw_ref[...] = lax.dot_general(yl_ref[...], dyv, _TN, preferred_element_type=F32).astype(BF16)

        @pl.when(j >= NCB)
        def _():
            dw_ref[...] = lax.dot_general(yc_ref[...], dyv, _TN, preferred_element_type=F32).astype(BF16)

    return pl.pallas_call(
        body, name="outproj_bwd",
        grid=(2 * NCB,),
        in_specs=[pl.BlockSpec((TP, D), lambda j: (0, 0)),
                  pl.BlockSpec((TP, CB), lambda j: (0, jnp.minimum(j, NCB - 1))),
                  pl.BlockSpec((TP, CB), lambda j: (0, jnp.maximum(j - NCB, 0))),
                  pl.BlockSpec((CB, D), lambda j: (j, 0))],
        out_specs=[pl.BlockSpec((TP, CB), lambda j: (0, j)), pl.BlockSpec((CB, D), lambda j: (j, 0))],
        out_shape=[jax.ShapeDtypeStruct((TP, D), F32), jax.ShapeDtypeStruct((D, D), BF16)],
        compiler_params=_cparams(),
    )(dy, ylru, yconf, w_out)


def _conf_bwd_proj(dycat, p, z, vc, ln_w, ln_b, pw_w):
    def body(dy_ref, p_ref, g_ref, vc_ref, lw_ref, lb_ref, w_ref,
             dvc_ref, dgc_ref, dpw_ref, vecs_ref, dp_s, s_s, ds_s):
        i = pl.program_id(0)
        lw, lb = lw_ref[...], lb_ref[...]

        @pl.when(i == 0)
        def _():
            dpw_ref[...] = jnp.zeros_like(dpw_ref)
            vecs_ref[...] = jnp.zeros_like(vecs_ref)

        def pre_chunk(ci, carry):
            r0 = pl.multiple_of(ci * R, R)
            for half in range(2):
                rr = r0 + 8 * half
                dyv = dy_ref[pl.ds(rr, 8), :]
                g = g_ref[pl.ds(rr, 8), :]
                sg = _sig(g)
                dp = dyv * (g * sg)
                dg = dyv * p_ref[pl.ds(rr, 8), :] * (sg * (1.0 + g * (1.0 - sg)))
                vecs_ref[0:8, :] += dp
                vecs_ref[8:16, :] += dg
                ds_s[pl.ds(rr, 8), :] = dp
                dvc_ref[pl.ds(rr, 8), :] = dg
            dp_s[pl.ds(r0, R), :] = ds_s[pl.ds(r0, R), :].astype(BF16)
            dgc_ref[pl.ds(r0, R), :] = dvc_ref[pl.ds(r0, R), :].astype(BF16)
            for half in range(2):
                rr = r0 + 8 * half
                _, _, ln = _ln_chunk(vc_ref[pl.ds(rr, 8), :], lw, lb)
                ds_s[pl.ds(rr, 8), :] = ln * _sig(ln)
            s_s[pl.ds(r0, R), :] = ds_s[pl.ds(r0, R), :].astype(BF16)
            return carry
        lax.fori_loop(0, TM // R, pre_chunk, 0, unroll=2)

        dpb = dp_s[...]
        ds_s[...] = lax.dot_general(dpb, w_ref[...], _NT, preferred_element_type=F32)
        dpw_ref[...] += lax.dot_general(s_s[...], dpb, _TN, preferred_element_type=F32)

        def post_chunk(ci, carry):
            r0 = pl.multiple_of(ci * 8, 8)
            xhat, rstd, ln = _ln_chunk(vc_ref[pl.ds(r0, 8), :], lw, lb)
            sl = _sig(ln)
            dln = ds_s[pl.ds(r0, 8), :] * (sl * (1.0 + ln * (1.0 - sl)))
            vecs_ref[16:24, :] += dln * xhat
            vecs_ref[24:32, :] += dln
            dxh = dln * lw
            m1 = jnp.mean(dxh, axis=-1, keepdims=True)
            m2 = jnp.mean(dxh * xhat, axis=-1, keepdims=True)
            dvc_ref[pl.ds(r0, 8), :] = rstd * (dxh - m1 - xhat * m2)
            return carry
        lax.fori_loop(0, TM // 8, post_chunk, 0, unroll=4)

    row = pl.BlockSpec((TM, DC), lambda i: (i, 0))
    vec = pl.BlockSpec((1, DC), lambda i: (0, 0))
    return pl.pallas_call(
        body, name="conf_bwd_proj",
        grid=(TP // TM,),
        in_specs=[pl.BlockSpec((TM, DC), lambda i: (i, 1)), row, pl.BlockSpec((TM, DC), lambda i: (i, 4)), row,
                  vec, vec, pl.BlockSpec((DC, DC), lambda i: (0, 0))],
        out_specs=[row, row, pl.BlockSpec((DC, DC), lambda i: (0, 0)), pl.BlockSpec((32, DC), lambda i: (0, 0))],
        out_shape=[jax.ShapeDtypeStruct((TP, DC), F32), jax.ShapeDtypeStruct((TP, DC), BF16),
                   jax.ShapeDtypeStruct((DC, DC), F32), jax.ShapeDtypeStruct((32, DC), F32)],
        scratch_shapes=[pltpu.VMEM((TM, DC), BF16), pltpu.VMEM((TM, DC), BF16), pltpu.VMEM((TM, DC), F32)],
        compiler_params=_cparams(),
    )(dycat, p, z, vc, ln_w, ln_b, pw_w)


def _conf_bwd_conv(dvc, z, dw_w):
    def body(dvc_ref, u1_ref, u2_ref, w_ref, du_ref, dw_ref, vecs_ref, vs, dvs):
        vs[pl.ds(0, KWP), :] = jnp.zeros((KWP, CB), F32)
        dvs[pl.ds(TP, KWP), :] = jnp.zeros((KWP, CB), F32)
        dw_ref[...] = jnp.zeros_like(dw_ref)
        vecs_ref[...] = jnp.zeros_like(vecs_ref)

        def fill_chunk(ci, carry):
            r0 = pl.multiple_of(ci * R, R)
            vs[pl.ds(KWP + r0, R), :] = u1_ref[pl.ds(r0, R), :] * _sig(u2_ref[pl.ds(r0, R), :])
            dv = dvc_ref[pl.ds(r0, R), :]
            dvs[pl.ds(r0, R), :] = dv
            vecs_ref[0:8, :] += _fold8(dv)
            return carry
        lax.fori_loop(0, TP // R, fill_chunk, 0)

        def conv_chunk(ci, carry):
            r0 = pl.multiple_of(ci * R, R)
            vbuf = vs[pl.ds(r0, KWP + R), :]
            dbuf = dvs[pl.ds(r0, KWP + R), :]
            dcur = dbuf[0:R, :]
            dv = jnp.zeros((R, CB), F32)
            for rr in range(8):
                vroll = vbuf if rr == 0 else pltpu.roll(vbuf, rr, 0)
                droll = dbuf if rr == 0 else pltpu.roll(dbuf, KWP + R - rr, 0)
                for q in range(4):
                    s = 8 * q + rr
                    if s > KW - 1:
                        continue
                    k = KW - 1 - s
                    dv = dv + droll[8 * q:8 * q + R, :] * w_ref[k:k + 1, :]
                    dw_ref[8 * k:8 * k + 8, :] += _fold8(dcur * vroll[KWP - 8 * q:KWP - 8 * q + R, :])
            u1 = u1_ref[pl.ds(r0, R), :]
            sg = _sig(u2_ref[pl.ds(r0, R), :])
            du1 = dv * sg
            du2 = dv * u1 * (sg * (1.0 - sg))
            du_ref[0, pl.ds(r0, R), :] = du1.astype(BF16)
            du_ref[1, pl.ds(r0, R), :] = du2.astype(BF16)
            vecs_ref[8:16, :] += _fold8(du1)
            vecs_ref[16:24, :] += _fold8(du2)
            return carry
        lax.fori_loop(0, TP // R, conv_chunk, 0)

    blk = pl.BlockSpec((TP, CB), lambda j: (0, j))
    return pl.pallas_call(
        body, name="conf_bwd_conv",
        grid=(NCB,),
        in_specs=[blk, pl.BlockSpec((TP, CB), lambda j: (0, 2 * NCB + j)),
                  pl.BlockSpec((TP, CB), lambda j: (0, 3 * NCB + j)), pl.BlockSpec((KWP, CB), lambda j: (0, j))],
        out_specs=[pl.BlockSpec((2, TP, CB), lambda j: (0, 0, j)), pl.BlockSpec((8 * KWP, CB), lambda j: (0, j)),
                   pl.BlockSpec((24, CB), lambda j: (0, j))],
        out_shape=[jax.ShapeDtypeStruct((2, TP, DC), BF16),
                   jax.ShapeDtypeStruct((8 * KWP, DC), F32), jax.ShapeDtypeStruct((24, DC), F32)],
        scratch_shapes=[pltpu.VMEM((TP + KWP, CB), F32), pltpu.VMEM((TP + KWP, CB), F32)],
        compiler_params=_cparams(),
    )(dvc, z, z, dw_w)


def _lru_bwd(dycat, z, xc, hs, conv_w, wa_g, b_a, wx_g, b_x, lam):
    NV = 6

    def body(dy_ref, x_ref, g_ref, xc_ref, hs_ref, cw_ref, wa_ref, ba_ref, wx_ref, bx_ref, lam_ref,
             dzl_ref, dwa_ref, dwx_ref, dcw_ref, vecs_ref, ga_s, gx_s, dxc_s):
        vecs_ref[...] = jnp.zeros_like(vecs_ref)
        dcw_ref[...] = jnp.zeros_like(dcw_ref)
        dxc_s[pl.ds(TP, 8), :] = jnp.zeros((8, CB), F32)

        def gate_chunk(ci, carry):
            r0 = pl.multiple_of(ci * TM, TM)
            xb = xc_ref[pl.ds(r0, TM), :].astype(BF16)
            ga_s[pl.ds(r0, TM), :] = jnp.dot(xb, wa_ref[...], preferred_element_type=F32) + ba_ref[...]
            gx_s[pl.ds(r0, TM), :] = jnp.dot(xb, wx_ref[...], preferred_element_type=F32) + bx_ref[...]
            return carry
        lax.fori_loop(0, TP // TM, gate_chunk, 0)

        sp8 = LRU_C * _softplus(-lam_ref[...])
        row = _row_iota((R, CB))
        nchunk = TP // R

        def scan_chunk(cj, carry):
            a_next, lam_next = carry
            ci = nchunk - 1 - cj
            r0 = pl.multiple_of(ci * R, R)
            dyv = dy_ref[pl.ds(r0, R), :]
            g = g_ref[pl.ds(r0, R), :]
            hv = hs_ref[pl.ds(r0, R), :]
            xc = xc_ref[pl.ds(r0, R), :]
            sg = _sig(g)
            dgl = dyv * hv * (sg * (1.0 + g * (1.0 - sg)))
            dzl_ref[1, pl.ds(r0, R), :] = dgl.astype(BF16)
            vecs_ref[0:8, :] += _fold8(dgl)
            dhs = dyv * (g * sg)
            r, i, a, mult = _gate_values(ga_s[pl.ds(r0, R), :], gx_s[pl.ds(r0, R), :], xc, sp8)
            b = jnp.where(row == R - 1, a_next, pltpu.roll(a, R - 1, 0))
            lv = dhs
            k = 1
            while k < R:
                m = row < R - k
                lv = jnp.where(m, lv + b * pltpu.roll(lv, R - k, 0), lv)
                b = jnp.where(m, b * pltpu.roll(b, R - k, 0), b)
                k *= 2
            lv = lv + b * lam_next
            p0 = pl.multiple_of(jnp.maximum(r0 - 8, 0), 8)
            hprev8 = jnp.where(ci > 0, hs_ref[pl.ds(p0, 8), :], 0.0)
            hprev = pltpu.roll(jnp.concatenate([hprev8, hv], axis=0), 1, 0)[8:8 + R, :]
            da = lv * hprev
            ixc = i * xc
            dmult = lv * ixc
            di = lv * mult * xc
            dxc_s[pl.ds(r0, R), :] = lv * mult * i
            a2 = a * a
            dlog_a = da * a - dmult * a2 / mult
            vecs_ref[32:40, :] += _fold8(dlog_a * r)
            dga = -(dlog_a * sp8) * r * (1.0 - r)
            dgx = di * i * (1.0 - i)
            ga_s[pl.ds(r0, R), :] = dga
            gx_s[pl.ds(r0, R), :] = dgx
            vecs_ref[16:24, :] += _fold8(dga)
            vecs_ref[24:32, :] += _fold8(dgx)
            a_first = jnp.sum(jnp.where(row == 0, a, 0.0), axis=0, keepdims=True)
            l_first = jnp.sum(jnp.where(row == 0, lv, 0.0), axis=0, keepdims=True)
            return a_first, l_first
        lax.fori_loop(0, nchunk, scan_chunk, (jnp.zeros((1, CB), F32), jnp.zeros((1, CB), F32)))

        dwa_ref[...] = jnp.zeros_like(dwa_ref)
        dwx_ref[...] = jnp.zeros_like(dwx_ref)

        def mm_chunk(ci, carry):
            r0 = pl.multiple_of(ci * TM, TM)
            xb = xc_ref[pl.ds(r0, TM), :].astype(BF16)
            dgab = ga_s[pl.ds(r0, TM), :].astype(BF16)
            dgxb = gx_s[pl.ds(r0, TM), :].astype(BF16)
            dxc_s[pl.ds(r0, TM), :] += (lax.dot_general(dgab, wa_ref[...], _NT, preferred_element_type=F32)
                                        + lax.dot_general(dgxb, wx_ref[...], _NT, preferred_element_type=F32))
            dwa_ref[...] += lax.dot_general(xb, dgab, _TN, preferred_element_type=F32)
            dwx_ref[...] += lax.dot_general(xb, dgxb, _TN, preferred_element_type=F32)
            return carry
        lax.fori_loop(0, TP // TM, mm_chunk, 0)

        taps = [cw_ref[k:k + 1, :] for k in range(LW)]

        def conv_chunk(ci, carry):
            r0 = pl.multiple_of(ci * R, R)
            dbuf = dxc_s[pl.ds(r0, R + 8), :]
            dcur = dbuf[0:R, :]
            p0 = pl.multiple_of(jnp.maximum(r0 - 8, 0), 8)
            xprev = jnp.where(ci > 0, x_ref[pl.ds(p0, 8), :], 0.0)
            xbuf = jnp.concatenate([xprev, x_ref[pl.ds(r0, R), :]], axis=0)
            dxl = dcur * taps[LW - 1]
            dcw_ref[8 * (LW - 1):8 * LW, :] += _fold8(dcur * xbuf[8:8 + R, :])
            for s in range(1, LW):
                k = LW - 1 - s
                dxl = dxl + pltpu.roll(dbuf, R + 8 - s, 0)[0:R, :] * taps[k]
                dcw_ref[8 * k:8 * k + 8, :] += _fold8(dcur * pltpu.roll(xbuf, s, 0)[8:8 + R, :])
            dzl_ref[0, pl.ds(r0, R), :] = dxl.astype(BF16)
            vecs_ref[8:16, :] += _fold8(dxl)
            vecs_ref[40:48, :] += _fold8(dcur)
            return carry
        lax.fori_loop(0, TP // R, conv_chunk, 0)
        vecs_ref[32:40, :] = vecs_ref[32:40, :] * (LRU_C * _sig(-lam_ref[...]))

    col = lambda off: pl.BlockSpec((TP, CB), lambda j: (0, off + j))
    vec = pl.BlockSpec((1, CB), lambda j: (0, j))
    wsp = pl.BlockSpec((None, CB, CB), lambda j: (j, 0, 0))
    return pl.pallas_call(
        body, name="lru_bwd",
        grid=(NCB,),
        in_specs=[col(0), col(0), col(NCB), col(0), col(0), pl.BlockSpec((LW, CB), lambda j: (0, j)),
                  wsp, vec, wsp, vec, vec],
        out_specs=[pl.BlockSpec((2, TP, CB), lambda j: (0, 0, j)), wsp, wsp,
                   pl.BlockSpec((8 * LW, CB), lambda j: (0, j)), pl.BlockSpec((8 * NV, CB), lambda j: (0, j))],
        out_shape=[jax.ShapeDtypeStruct((2, TP, DL), BF16),
                   jax.ShapeDtypeStruct((NCB, CB, CB), F32), jax.ShapeDtypeStruct((NCB, CB, CB), F32),
                   jax.ShapeDtypeStruct((8 * LW, DL), F32), jax.ShapeDtypeStruct((8 * NV, DL), F32)],
        scratch_shapes=[pltpu.VMEM((TP, CB), F32), pltpu.VMEM((TP, CB), F32), pltpu.VMEM((TP + 8, CB), F32)],
        compiler_params=_cparams(),
    )(dycat, z, z, xc, hs, conv_w, wa_g, b_a, wx_g, b_x, lam)


def _dz_section(sec, dzl_ref, dzc_ref, dgc_ref, use):
    @pl.when(sec < 2)
    def _():
        use(dzl_ref)

    @pl.when(jnp.logical_and(sec >= 2, sec < 4))
    def _():
        use(dzc_ref)

    @pl.when(sec == 4)
    def _():
        use(dgc_ref)


def _dz_specs(rows, index):
    return [pl.BlockSpec((None, rows, 1024), lambda a, b: (jnp.minimum(index(a, b)[1], 1), index(a, b)[0], 0)),
            pl.BlockSpec((None, rows, 1024), lambda a, b: (jnp.clip(index(a, b)[1] - 2, 0, 1), index(a, b)[0], 0)),
            pl.BlockSpec((rows, 1024), lambda a, b: (index(a, b)[0], 0))]


def _inproj_wgrad(hn, dzl, dzc, dgc):
    KB = 512

    def body(hn_ref, dzl_ref, dzc_ref, dgc_ref, dw_ref):
        def use(dz_ref):
            dw_ref[...] = lax.dot_general(hn_ref[...], dz_ref[...], _TN, preferred_element_type=F32).astype(BF16)
        _dz_section(pl.program_id(0), dzl_ref, dzc_ref, dgc_ref, use)

    return pl.pallas_call(
        body, name="inproj_wgrad",
        grid=(NIN // 1024, D // KB),
        in_specs=[pl.BlockSpec((TP, KB), lambda n, kb: (0, kb))] + _dz_specs(TP, lambda n, kb: (0, n)),
        out_specs=pl.BlockSpec((KB, 1024), lambda n, kb: (kb, n)),
        out_shape=jax.ShapeDtypeStruct((D, NIN), BF16),
        compiler_params=_cparams(),
    )(hn, dzl, dzc, dgc)


def _inproj_bwd(dzl, dzc, dgc, w_in, h, dout, pre_w):
    nsec = NIN // 1024

    def body(dzl_ref, dzc_ref, dgc_ref, w_ref, h_ref, dout_ref, pw_ref, dh_ref, dpw_ref, acc_s):
        i = pl.program_id(0)
        s = pl.program_id(1)

        @pl.when(s == 0)
        def _():
            acc_s[...] = jnp.zeros_like(acc_s)

        def use(dz_ref):
            acc_s[...] += lax.dot_general(dz_ref[...], w_ref[...], _NT, preferred_element_type=F32)
        _dz_section(s, dzl_ref, dzc_ref, dgc_ref, use)

        @pl.when(jnp.logical_and(i == 0, s == nsec - 1))
        def _():
            dpw_ref[...] = jnp.zeros_like(dpw_ref)

        @pl.when(s == nsec - 1)
        def _():
            pw = pw_ref[...]

            def chunk(ci, carry):
                r0 = pl.multiple_of(ci * 8, 8)
                hv = h_ref[pl.ds(r0, 8), :]
                dhn = acc_s[pl.ds(r0, 8), :]
                rs = lax.rsqrt(jnp.mean(hv * hv, axis=-1, keepdims=True) + EPS)
                dpw_ref[...] += dhn * (hv * rs)
                gw = dhn * pw
                dot = jnp.mean(gw * hv, axis=-1, keepdims=True)
                dh_ref[pl.ds(r0, 8), :] = rs * gw - hv * (rs * rs * rs * dot) + dout_ref[pl.ds(r0, 8), :]
                return carry
            lax.fori_loop(0, TM // 8, chunk, 0, unroll=4)

    row = pl.BlockSpec((TM, D), lambda i, s: (i, 0))
    return pl.pallas_call(
        body, name="inproj_bwd",
        grid=(TP // TM, nsec),
        in_specs=_dz_specs(TM, lambda i, s: (i, s)) + [
            pl.BlockSpec((D, 1024), lambda i, s: (0, s)), row, row, pl.BlockSpec((1, D), lambda i, s: (0, 0))],
        out_specs=[row, pl.BlockSpec((8, D), lambda i, s: (0, 0))],
        out_shape=[jax.ShapeDtypeStruct((TP, D), F32), jax.ShapeDtypeStruct((8, D), F32)],
        scratch_shapes=[pltpu.VMEM((TM, D), F32)],
        compiler_params=_cparams(),
    )(dzl, dzc, dgc, w_in, h, dout, pre_w)


def _adamw(name, parts, w, m, v, block_rows):
    rows, cols = w.shape
    cw = cols if cols <= 640 else 512
    c1 = 1.0 / (1.0 - ADAM_B1 ** ADAM_STEP)
    c2 = 1.0 / (1.0 - ADAM_B2 ** ADAM_STEP)

    def body(p_ref, w_ref, m_ref, v_ref, g_ref, d_ref, nm_ref, nv_ref):
        def chunk(ci, carry):
            r0 = pl.multiple_of(ci * R, R)
            for c0 in range(0, cols, cw):
                g = p_ref[0, pl.ds(r0, R), c0:c0 + cw].astype(F32)
                for sidx in range(1, NDEV):
                    g = g + p_ref[sidx, pl.ds(r0, R), c0:c0 + cw].astype(F32)
                wv = w_ref[pl.ds(r0, R), c0:c0 + cw]
                mv = ADAM_B1 * m_ref[pl.ds(r0, R), c0:c0 + cw] + (1.0 - ADAM_B1) * g
                vv = ADAM_B2 * v_ref[pl.ds(r0, R), c0:c0 + cw] + (1.0 - ADAM_B2) * (g * g)
                g_ref[pl.ds(r0, R), c0:c0 + cw] = g
                nm_ref[pl.ds(r0, R), c0:c0 + cw] = mv
                nv_ref[pl.ds(r0, R), c0:c0 + cw] = vv
                upd = (mv * c1) / (jnp.sqrt(vv * c2) + ADAM_EPS) + ADAM_WD * wv
                d_ref[pl.ds(r0, R), c0:c0 + cw] = -ADAM_LR * upd
            return carry
        lax.fori_loop(0, block_rows // R, chunk, 0)

    blk = pl.BlockSpec((block_rows, cols), lambda i: (i, 0))
    shp = jax.ShapeDtypeStruct((rows, cols), F32)
    return pl.pallas_call(
        body, name=name,
        grid=(rows // block_rows,),
        in_specs=[pl.BlockSpec((NDEV, block_rows, cols), lambda i: (0, i, 0)), blk, blk, blk],
        out_specs=[blk, blk, blk, blk],
        out_shape=[shp, shp, shp, shp],
        compiler_params=_cparams(),
    )(parts, w, m, v)


def _adam_math(g, w, m, v):
    c1 = 1.0 / (1.0 - ADAM_B1 ** ADAM_STEP)
    c2 = 1.0 / (1.0 - ADAM_B2 ** ADAM_STEP)
    mv = ADAM_B1 * m + (1.0 - ADAM_B1) * g
    vv = ADAM_B2 * v + (1.0 - ADAM_B2) * (g * g)
    upd = (mv * c1) / (jnp.sqrt(vv * c2) + ADAM_EPS) + ADAM_WD * w
    return -ADAM_LR * upd, mv, vv


_VEC = [("pre_norm_w", 2), ("post_norm_w", 2), ("b_in", 5), ("lru_conv_b", 1), ("b_gate_a", 1), ("b_gate_x", 1),
        ("lru_lambda", 1), ("conf_dw_b", 1), ("conf_ln_w", 1), ("conf_ln_b", 1), ("conf_pw_b", 1)]
_VEC_ROWS = 24
_LOSS_ROW = 17
_SM_ROWS = 64


def _pack_grads(dprew_acc, dpostw_acc, cvecs, kvecs, lvecs, dcw_acc, ddw_acc, dh, loss_acc):
    def body(pre_ref, post_ref, c_ref, k_ref, l_ref, dcw_ref, ddw_ref, dh_ref, loss_ref, vec_ref, small_ref, tmp):
        s8 = lambda ref, r: jnp.sum(ref[8 * r:8 * r + 8, :], axis=0, keepdims=True)
        vec_ref[...] = jnp.zeros_like(vec_ref)
        pre, post = s8(pre_ref, 0), s8(post_ref, 0)
        rows = [pre[:, 0:1024], pre[:, 1024:2048], post[:, 0:1024], post[:, 1024:2048],
                s8(l_ref, 1), s8(l_ref, 0), s8(k_ref, 1), s8(k_ref, 2), s8(c_ref, 1),
                s8(l_ref, 5), s8(l_ref, 2), s8(l_ref, 3), s8(l_ref, 4),
                s8(k_ref, 0), s8(c_ref, 2), s8(c_ref, 3), s8(c_ref, 0)]
        for r, val in enumerate(rows):
            vec_ref[r:r + 1, :] = val
        vec_ref[_LOSS_ROW:_LOSS_ROW + 1, :] = jnp.zeros((1, 1024), F32) + (0.5 / D) * jnp.sum(loss_ref[...])

        small_ref[...] = jnp.zeros_like(small_ref)
        for k in range(LW):
            tmp[k:k + 1, :] = s8(dcw_ref, k)
        for k in range(KW):
            tmp[8 + k:9 + k, :] = s8(ddw_ref, k)
        for d in range(NDEV):
            small_ref[d, 0:LW, 0:128] = tmp[0:LW, 128 * d:128 * d + 128]
            small_ref[d, 8:8 + KW, 0:128] = tmp[8:8 + KW, 128 * d:128 * d + 128]
            small_ref[d, 40:56, :] = dh_ref[:, 256 * d:256 * d + 256]

    full = lambda a: pl.BlockSpec(a.shape, lambda i: (0,) * a.ndim)
    ins = [dprew_acc, dpostw_acc, cvecs, kvecs, lvecs, dcw_acc, ddw_acc]
    return pl.pallas_call(
        body, name="pack_grads",
        grid=(1,),
        in_specs=[full(a) for a in ins] + [pl.BlockSpec((NMETA, D), lambda i: (0, 0)), full(loss_acc)],
        out_specs=[pl.BlockSpec((_VEC_ROWS, 1024), lambda i: (0, 0)),
                   pl.BlockSpec((NDEV, _SM_ROWS, 256), lambda i: (0, 0, 0))],
        out_shape=[jax.ShapeDtypeStruct((_VEC_ROWS, 1024), F32), jax.ShapeDtypeStruct((NDEV, _SM_ROWS, 256), F32)],
        scratch_shapes=[pltpu.VMEM((40, 1024), F32)],
        compiler_params=_cparams(),
    )(*ins, dh, loss_acc)


def _adamw_vec(parts, W, M, V):
    nv = len(_VEC)

    def body(*refs):
        p_ref = refs[0]
        w_refs, m_refs, v_refs = refs[1:1 + nv], refs[1 + nv:1 + 2 * nv], refs[1 + 2 * nv:1 + 3 * nv]
        outs = refs[1 + 3 * nv:]

        def total(r):
            acc = p_ref[0, r:r + 1, :]
            for sidx in range(1, NDEV):
                acc = acc + p_ref[sidx, r:r + 1, :]
            return acc

        row = 0
        for idx, (_, nrows) in enumerate(_VEC):
            for part in range(nrows):
                cols = slice(1024 * part, 1024 * part + 1024)
                g = total(row + part)
                delta, mv, vv = _adam_math(g, w_refs[idx][:, cols], m_refs[idx][:, cols], v_refs[idx][:, cols])
                for o, val in zip(outs[4 * idx:4 * idx + 4], (g, delta, mv, vv)):
                    o[:, cols] = val
            row += nrows
        outs[-1][...] = total(_LOSS_ROW)[:, 0:128]

    names = [n for n, _ in _VEC]
    flat = lambda d: [d[n].reshape(1, -1) for n in names]
    ws, ms, vs = flat(W), flat(M), flat(V)
    res = pl.pallas_call(
        body, name="adamw_vec",
        out_shape=[jax.ShapeDtypeStruct(w.shape, F32) for w in ws for _ in range(4)]
        + [jax.ShapeDtypeStruct((1, 128), F32)],
        compiler_params=_cparams(),
    )(parts, *ws, *ms, *vs)
    return {n: tuple(res[4 * i:4 * i + 4]) for i, n in enumerate(names)}, res[-1]


def _adamw_small(parts, W, M, V):
    where = {"lru_conv_w": (slice(0, LW), slice(0, 128)), "conf_dw_w": (slice(8, 8 + KW), slice(0, 128)),
             "meta_tokens": (slice(40, 56), slice(0, 256))}
    names = list(where)

    def body(*refs):
        p_ref = refs[0]
        outs = refs[10:]
        for idx, n in enumerate(names):
            rs, cs = where[n]
            g = p_ref[0, rs, cs]
            for sidx in range(1, NDEV):
                g = g + p_ref[sidx, rs, cs]
            delta, mv, vv = _adam_math(g, refs[1 + idx][...], refs[4 + idx][...], refs[7 + idx][...])
            for o, val in zip(outs[4 * idx:4 * idx + 4], (g, delta, mv, vv)):
                o[...] = val

    two_d = lambda a: a.reshape(a.shape[-2:])
    ws, ms, vs = ([two_d(d[n]) for n in names] for d in (W, M, V))
    res = pl.pallas_call(
        body, name="adamw_small",
        out_shape=[jax.ShapeDtypeStruct(w.shape, F32) for w in ws for _ in range(4)],
        compiler_params=_cparams(),
    )(parts, *ws, *ms, *vs)
    return {n: tuple(res[4 * i:4 * i + 4]) for i, n in enumerate(names)}


def _pack_small(lru_cw, dw_w, meta):
    buf = jnp.zeros((_SM_ROWS, 256), F32)
    buf = buf.at[0:LW, 0:128].set(lru_cw)
    buf = buf.at[8:8 + dw_w.shape[0], 0:128].set(dw_w)
    return buf.at[40:56, :].set(meta)


def _block_diag4(w):
    w4 = w.reshape(NCB, 4, 64, 64)
    eye = jnp.eye(4, dtype=w.dtype)
    return jnp.einsum("ghij,hk->ghikj", w4, eye).reshape(NCB, CB, CB)


def _diag_blocks(g):
    g5 = g.reshape(NCB, 4, 64, 4, 64)
    return jnp.stack([g5[:, hh, :, hh, :] for hh in range(4)], axis=1).reshape(16, 64, 64)


def _local_step(x, target, meta_full, win_full, out_weights, lru_cw_full, dw_w_full, W, send):
    h = jnp.concatenate([meta_full, x, jnp.zeros((TP - T, D), F32)], axis=0)
    tgt = jnp.concatenate([jnp.zeros((NMETA, D), F32), target, jnp.zeros((TP - T, D), F32)], axis=0)
    wa_g = _block_diag4(W["w_gate_a"][0]).astype(BF16)
    wx_g = _block_diag4(W["w_gate_x"][0]).astype(BF16)

    z, hn = _prenorm_inproj(h, W["pre_norm_w"], win_full, W["b_in"])
    ylru, xc, hs = _lru_fwd(z, lru_cw_full, W["lru_conv_b"], wa_g, W["b_gate_a"], wx_g, W["b_gate_x"],
                            W["lru_lambda"])
    vc = _conf_fwd_conv(z, dw_w_full, W["conf_dw_b"])
    wout_full, pw_full = out_weights(vc)
    yconf, p = _conf_fwd_proj(vc, z, W["conf_ln_w"], W["conf_ln_b"], pw_full, W["conf_pw_b"])
    dout, dy, loss_acc, dpostw_acc = _outproj_loss(ylru, yconf, wout_full, h, tgt, W["post_norm_w"])

    dycat, dwout_part = _outproj_bwd(dy, ylru, yconf, wout_full)
    tok = send("w_out", dwout_part)
    dvc, dgc, dpw_part, cvecs = _conf_bwd_proj(dycat, p, z, vc, W["conf_ln_w"] + tok, W["conf_ln_b"], pw_full)
    tok = send("conf_pw_w", dpw_part)
    dzc, ddw_acc, kvecs = _conf_bwd_conv(dvc, z, dw_w_full + tok)
    dzl, dwa_g, dwx_g, dcw_acc, lvecs = _lru_bwd(dycat, z, xc, hs, lru_cw_full, wa_g, W["b_gate_a"], wx_g,
                                                 W["b_gate_x"], W["lru_lambda"])
    tok = send("w_in", _inproj_wgrad(hn, dzl, dzc, dgc))
    dh, dprew_acc = _inproj_bwd(dzl, dzc, dgc, win_full, h, dout, W["pre_norm_w"] + tok)

    vec_pack, small_part = _pack_grads(dprew_acc, dpostw_acc, cvecs, kvecs, lvecs, dcw_acc, ddw_acc, dh, loss_acc)
    return dh, vec_pack, small_part, _diag_blocks(dwa_g), _diag_blocks(dwx_g)


def kernel(x, meta_tokens, pre_norm_w, post_norm_w, w_in, b_in, lru_conv_w, lru_conv_b, w_gate_a, b_gate_a, w_gate_x, b_gate_x, lru_lambda, conf_dw_w, conf_dw_b, conf_ln_w, conf_ln_b, conf_pw_w, conf_pw_b, w_out, loss_target, m_meta_tokens, m_pre_norm_w, m_post_norm_w, m_w_in, m_b_in, m_lru_conv_w, m_lru_conv_b, m_w_gate_a, m_b_gate_a, m_w_gate_x, m_b_gate_x, m_lru_lambda, m_conf_dw_w, m_conf_dw_b, m_conf_ln_w, m_conf_ln_b, m_conf_pw_w, m_conf_pw_b, m_w_out, v_meta_tokens, v_pre_norm_w, v_post_norm_w, v_w_in, v_b_in, v_lru_conv_w, v_lru_conv_b, v_w_gate_a, v_b_gate_a, v_w_gate_x, v_b_gate_x, v_lru_lambda, v_conf_dw_w, v_conf_dw_b, v_conf_ln_w, v_conf_ln_b, v_conf_pw_w, v_conf_pw_b, v_w_out):
    W = dict(meta_tokens=meta_tokens, pre_norm_w=pre_norm_w, post_norm_w=post_norm_w, w_in=w_in, b_in=b_in,
             lru_conv_w=lru_conv_w, lru_conv_b=lru_conv_b, w_gate_a=w_gate_a, b_gate_a=b_gate_a,
             w_gate_x=w_gate_x, b_gate_x=b_gate_x, lru_lambda=lru_lambda, conf_dw_w=conf_dw_w,
             conf_dw_b=conf_dw_b, conf_ln_w=conf_ln_w, conf_ln_b=conf_ln_b, conf_pw_w=conf_pw_w,
             conf_pw_b=conf_pw_b, w_out=w_out)
    M = dict(meta_tokens=m_meta_tokens, pre_norm_w=m_pre_norm_w, post_norm_w=m_post_norm_w, w_in=m_w_in,
             b_in=m_b_in, lru_conv_w=m_lru_conv_w, lru_conv_b=m_lru_conv_b, w_gate_a=m_w_gate_a,
             b_gate_a=m_b_gate_a, w_gate_x=m_w_gate_x, b_gate_x=m_b_gate_x, lru_lambda=m_lru_lambda,
             conf_dw_w=m_conf_dw_w, conf_dw_b=m_conf_dw_b, conf_ln_w=m_conf_ln_w, conf_ln_b=m_conf_ln_b,
             conf_pw_w=m_conf_pw_w, conf_pw_b=m_conf_pw_b, w_out=m_w_out)
    V = dict(meta_tokens=v_meta_tokens, pre_norm_w=v_pre_norm_w, post_norm_w=v_post_norm_w, w_in=v_w_in,
             b_in=v_b_in, lru_conv_w=v_lru_conv_w, lru_conv_b=v_lru_conv_b, w_gate_a=v_w_gate_a,
             b_gate_a=v_b_gate_a, w_gate_x=v_w_gate_x, b_gate_x=v_b_gate_x, lru_lambda=v_lru_lambda,
             conf_dw_w=v_conf_dw_w, conf_dw_b=v_conf_dw_b, conf_ln_w=v_conf_ln_w, conf_ln_b=v_conf_ln_b,
             conf_pw_w=v_conf_pw_w, conf_pw_b=v_conf_pw_b, w_out=v_w_out)
    names = list(W.keys())
    shapes = {n: W[n].shape for n in names}

    small = _pack_small(lru_conv_w[0], conf_dw_w[0], meta_tokens)
    gathered, _ = _exchange_start("gather_start", [
        (small, jax.ShapeDtypeStruct((NDEV, _SM_ROWS, 256), F32), _whole, _slot),
        (w_in[0].astype(BF16), jax.ShapeDtypeStruct((D, NIN), BF16), _whole, _cols(NIN // NDEV)),
        (w_out[0].astype(BF16), jax.ShapeDtypeStruct((D, D), BF16), _whole, _rows(D // NDEV)),
        (conf_pw_w[0].astype(BF16), jax.ShapeDtypeStruct((DC, DC), BF16), _whole, _rows(DC // NDEV)),
    ])
    small_all, win_full = _exchange_wait("gather_wait_in", gathered[0:2], x)
    unshard = lambda a: jnp.transpose(a, (1, 0, 2)).reshape(a.shape[1], -1)
    lru_cw_full = unshard(small_all[:, 0:LW, 0:128])
    dw_w_full = unshard(small_all[:, 8:8 + KWP, 0:128])
    meta_full = unshard(small_all[:, 40:56, :])

    def out_weights(after):
        return _exchange_wait("gather_wait_out", gathered[2:4], after)

    piece = {"w_in": (jax.ShapeDtypeStruct((NDEV, D, NIN // NDEV), BF16), _cols(NIN // NDEV)),
             "w_out": (jax.ShapeDtypeStruct((NDEV, D // NDEV, D), BF16), _rows(D // NDEV)),
             "conf_pw_w": (jax.ShapeDtypeStruct((NDEV, DC // NDEV, DC), BF16), _rows(DC // NDEV))}
    sent = {}

    def send(name, part):
        handles, token = _exchange_start("scatter_" + name + "_start",
                                         [(part.astype(BF16), piece[name][0], piece[name][1], _slot)])
        sent[name] = handles
        return token[0, 0]

    dh, vec_pack, small_part, dwa, dwx = _local_step(
        x[0], loss_target[0], meta_full, win_full, out_weights, lru_cw_full, dw_w_full, W, send)
    grad_x = dh[NMETA:T][None]
    gate = jax.ShapeDtypeStruct((NDEV, 16 * 64, 64), F32)
    rest, _ = _exchange_start("scatter_rest_start", [
        (small_part, jax.ShapeDtypeStruct((NDEV, _SM_ROWS, 256), F32), _slot, _slot),
        (vec_pack, jax.ShapeDtypeStruct((NDEV, _VEC_ROWS, 1024), F32), _whole, _slot),
        (dwa.reshape(16 * 64, 64), gate, _whole, _slot),
        (dwx.reshape(16 * 64, 64), gate, _whole, _slot),
    ])

    G, DW, NM, NV = {}, {}, {}, {}
    (wout_parts,) = _exchange_wait("scatter_w_out_wait", sent["w_out"], dh)
    G["w_out"], DW["w_out"], NM["w_out"], NV["w_out"] = _adamw("adamw_w_out", wout_parts, w_out[0], m_w_out[0], v_w_out[0], 64)
    (pw_parts,) = _exchange_wait("scatter_conf_pw_w_wait", sent["conf_pw_w"], G["w_out"])
    G["conf_pw_w"], DW["conf_pw_w"], NM["conf_pw_w"], NV["conf_pw_w"] = _adamw(
        "adamw_pw", pw_parts, conf_pw_w[0], m_conf_pw_w[0], v_conf_pw_w[0], 128)
    (win_parts,) = _exchange_wait("scatter_w_in_wait", sent["w_in"], G["conf_pw_w"])
    G["w_in"], DW["w_in"], NM["w_in"], NV["w_in"] = _adamw("adamw_w_in", win_parts, w_in[0], m_w_in[0], v_w_in[0], 256)
    small_parts, vec_parts, wa_parts, wx_parts = _exchange_wait("scatter_rest_wait", rest, G["w_in"])
    res = dict(_adamw_small(small_parts, W, M, V))
    vec_res, loss_row = _adamw_vec(vec_parts, W, M, V)
    res.update(vec_res)
    for n, parts in (("w_gate_a", wa_parts), ("w_gate_x", wx_parts)):
        res[n] = _adamw("adamw_" + n, parts, *[d[n].reshape(16 * 64, 64) for d in (W, M, V)], 16 * 64)
    for n, vals in res.items():
        for dst, val in zip((G, DW, NM, NV), vals):
            dst[n] = val
    for dst in (G, DW, NM, NV):
        for n in names:
            dst[n] = dst[n].reshape(shapes[n])
    loss = loss_row[0, 0]

    return (loss, grad_x, *[G[n] for n in names], *[DW[n] for n in names],
            *[NM[n] for n in names], *[NV[n] for n in names])
```

```python
import functools

import jax
import jax.numpy as jnp
from jax import lax
from jax.experimental import pallas as pl
from jax.experimental.pallas import tpu as pltpu

F32 = jnp.float32
BF16 = jnp.bfloat16

D = 2048
DL = 1024
DC = 1024
NIN = 5120
NMETA = 16
SEQ = 2048
T = NMETA + SEQ
TP = 2176
TM = 544
CB = 256
NCB = DL // CB
R = 16
KW = 31
KWP = 32
LW = 4
LRU_C = 8.0
EPS = 1e-6
NDEV = 8

ADAM_LR = 0.001
ADAM_B1 = 0.9
ADAM_B2 = 0.999
ADAM_EPS = 1e-08
ADAM_WD = 0.01
ADAM_STEP = 10

VMEM_LIMIT = 56 * 1024 * 1024


def _cparams():
    return pltpu.CompilerParams(vmem_limit_bytes=VMEM_LIMIT)


def _sig(x):
    return 1.0 / (1.0 + jnp.exp(-x))


def _expm1_neg(y):
    poly = y * (1.0 + y * (0.5 + y * (1.0 / 6.0 + y * (1.0 / 24.0 + y * (1.0 / 120.0)))))
    return jnp.where(y > -0.1, poly, jnp.exp(y) - 1.0)


def _softplus(x):
    e = jnp.exp(-jnp.abs(x))
    w = 1.0 + e
    l1p = jnp.where(w == 1.0, e, jnp.log(w) * e / (w - 1.0))
    return jnp.maximum(x, 0.0) + l1p


def _row_iota(shape):
    return lax.broadcasted_iota(jnp.int32, shape, 0)


def _fold8(v):
    return v[0:8, :] + v[8:16, :]


_FLIPS = [(k >> 2 & 1, k >> 1 & 1, k & 1) for k in range(1, NDEV)]
_HBM = pl.BlockSpec(memory_space=pltpu.HBM)
_SEM = pl.BlockSpec(memory_space=pltpu.SEMAPHORE)


def _peers():
    x, y, c = lax.axis_index("x"), lax.axis_index("y"), lax.axis_index("c")
    out = []
    for dx, dy, dc in _FLIPS:
        px = 1 - x if dx else x
        py = 1 - y if dy else y
        pc = 1 - c if dc else c
        out.append(((px, py, pc), 4 * px + 2 * py + pc))
    return 4 * x + 2 * y + c, out


def _exchange_start(name, items):
    n = len(items)

    def body(*refs):
        srcs, lands = refs[:n], refs[n:2 * n]
        outs = refs[2 * n:]
        send_sems, recv_sems, local_sems = outs[:n], outs[n:2 * n], outs[2 * n:3 * n]
        token = outs[-1]
        me, peers = _peers()
        for a in range(n):
            src_at, dst_at = items[a][2], items[a][3]
            pltpu.make_async_copy(src_at(srcs[a], me), dst_at(lands[a], me), local_sems[a]).start()
        for a in range(n):
            src_at, dst_at = items[a][2], items[a][3]
            for k, (pos, peer) in enumerate(peers):
                pltpu.make_async_remote_copy(
                    src_ref=src_at(srcs[a], peer), dst_ref=dst_at(lands[a], me),
                    send_sem=send_sems[a].at[k], recv_sem=recv_sems[a].at[k],
                    device_id=pos, device_id_type=pl.DeviceIdType.MESH).start()
        token[...] = jnp.zeros_like(token)

    srcs = [pltpu.with_memory_space_constraint(it[0], pltpu.HBM) for it in items]
    lands = [pltpu.with_memory_space_constraint(lax.empty(it[1].shape, it[1].dtype), pltpu.HBM) for it in items]
    sem7 = pltpu.SemaphoreType.DMA((NDEV - 1,))
    res = pl.pallas_call(
        body, name=name,
        out_shape=([sem7] * (2 * n) + [pltpu.SemaphoreType.DMA(())] * n
                   + [pltpu.HBM(a.shape, a.dtype) for a in srcs] + [pltpu.HBM(a.shape, a.dtype) for a in lands]
                   + [jax.ShapeDtypeStruct((8, 128), F32)]),
        in_specs=[_HBM] * (2 * n),
        out_specs=[_SEM] * (3 * n) + [_HBM] * (2 * n) + [pl.BlockSpec(memory_space=pltpu.VMEM)],
        input_output_aliases={i: 3 * n + i for i in range(2 * n)},
        compiler_params=pltpu.CompilerParams(has_side_effects=pltpu.SideEffectType.DATAFLOW_SIDE_EFFECTING),
    )(*srcs, *lands)
    handles = [dict(send=res[a], recv=res[n + a], local=res[2 * n + a], src=res[3 * n + a], land=res[4 * n + a],
                    src_at=items[a][2], dst_at=items[a][3]) for a in range(n)]
    return handles, res[-1]


def _exchange_wait(name, handles, after):
    n = len(handles)

    def body(*refs):
        srcs, lands = refs[:n], refs[n:2 * n]
        send_sems, recv_sems, local_sems = refs[2 * n:3 * n], refs[3 * n:4 * n], refs[4 * n:5 * n]
        me, peers = _peers()
        for a in range(n):
            src_at, dst_at = handles[a]["src_at"], handles[a]["dst_at"]
            for k, (pos, peer) in enumerate(peers):
                cp = pltpu.make_async_remote_copy(
                    src_ref=src_at(srcs[a], peer), dst_ref=dst_at(lands[a], peer),
                    send_sem=send_sems[a].at[k], recv_sem=recv_sems[a].at[k],
                    device_id=pos, device_id_type=pl.DeviceIdType.MESH)
                cp.wait_send()
                cp.wait_recv()
            pltpu.make_async_copy(src_at(srcs[a], me), dst_at(lands[a], me), local_sems[a]).wait()

    srcs = [hd["src"] for hd in handles]
    lands = [hd["land"] for hd in handles]
    res = pl.pallas_call(
        body, name=name,
        out_shape=[pltpu.HBM(a.shape, a.dtype) for a in srcs] + [pltpu.HBM(a.shape, a.dtype) for a in lands],
        in_specs=[_HBM] * (2 * n) + [_SEM] * (3 * n) + [pl.BlockSpec(memory_space=pl.ANY)],
        out_specs=[_HBM] * (2 * n),
        input_output_aliases={i: i for i in range(2 * n)},
        compiler_params=pltpu.CompilerParams(has_side_effects=pltpu.SideEffectType.DATAFLOW_SIDE_EFFECTING),
    )(*srcs, *lands, *[hd["send"] for hd in handles], *[hd["recv"] for hd in handles],
      *[hd["local"] for hd in handles], after)
    return list(res[n:])


def _whole(ref, l):
    return ref


def _slot(ref, l):
    return ref.at[l]


def _cols(width):
    def at(ref, l):
        return ref.at[:, pl.ds(pl.multiple_of(l * width, 128), width)]
    return at


def _rows(height):
    def at(ref, l):
        return ref.at[pl.ds(pl.multiple_of(l * height, 8), height), :]
    return at


def _prenorm_inproj(h, pre_w, w_in, b_in):
    nsec = NIN // 1024

    def body(h_ref, pw_ref, w_ref, b_ref, z_ref, hn_ref):
        @pl.when(pl.program_id(1) == 0)
        def _():
            pw = pw_ref[...]

            def chunk(ci, carry):
                r0 = pl.multiple_of(ci * R, R)
                xv = h_ref[pl.ds(r0, R), :]
                ms = jnp.mean(xv * xv, axis=-1, keepdims=True)
                hn_ref[pl.ds(r0, R), :] = (xv * lax.rsqrt(ms + EPS) * pw).astype(BF16)
                return carry
            lax.fori_loop(0, TM // R, chunk, 0, unroll=2)

        z_ref[...] = jnp.dot(hn_ref[...], w_ref[...], preferred_element_type=F32) + b_ref[...]

    return pl.pallas_call(
        body, name="prenorm_inproj",
        grid=(TP // TM, nsec),
        in_specs=[pl.BlockSpec((TM, D), lambda i, n: (i, 0)),
                  pl.BlockSpec((1, D), lambda i, n: (0, 0)),
                  pl.BlockSpec((D, 1024), lambda i, n: (0, n)),
                  pl.BlockSpec((1, 1024), lambda i, n: (0, n))],
        out_specs=[pl.BlockSpec((TM, 1024), lambda i, n: (i, n)),
                   pl.BlockSpec((TM, D), lambda i, n: (i, 0))],
        out_shape=[jax.ShapeDtypeStruct((TP, NIN), F32), jax.ShapeDtypeStruct((TP, D), BF16)],
        compiler_params=_cparams(),
    )(h, pre_w, w_in, b_in)


def _gate_values(ga, gx, xc, sp8):
    r = _sig(ga)
    i = _sig(gx)
    log_a = -(r * sp8)
    a = jnp.exp(log_a)
    mult = jnp.sqrt(-_expm1_neg(2.0 * log_a))
    return r, i, a, mult


def _lru_fwd(z, conv_w, conv_b, wa_g, b_a, wx_g, b_x, lam):
    def body(x_ref, g_ref, cw_ref, cb_ref, wa_ref, ba_ref, wx_ref, bx_ref, lam_ref,
             y_ref, xc_ref, hs_ref, ga_s, gx_s):
        taps = [cw_ref[k:k + 1, :] for k in range(LW)]
        cb = cb_ref[...]

        def conv_chunk(ci, carry):
            r0 = pl.multiple_of(ci * R, R)
            cur = x_ref[pl.ds(r0, R), :]
            p0 = pl.multiple_of(jnp.maximum(r0 - 8, 0), 8)
            prev = jnp.where(ci > 0, x_ref[pl.ds(p0, 8), :], 0.0)
            buf = jnp.concatenate([prev, cur], axis=0)
            acc = cur * taps[LW - 1] + cb
            for s in range(1, LW):
                acc = acc + pltpu.roll(buf, s, 0)[8:8 + R, :] * taps[LW - 1 - s]
            xc_ref[pl.ds(r0, R), :] = acc
            return carry
        lax.fori_loop(0, TP // R, conv_chunk, 0)

        def gate_chunk(ci, carry):
            r0 = pl.multiple_of(ci * TM, TM)
            xb = xc_ref[pl.ds(r0, TM), :].astype(BF16)
            ga_s[pl.ds(r0, TM), :] = jnp.dot(xb, wa_ref[...], preferred_element_type=F32) + ba_ref[...]
            gx_s[pl.ds(r0, TM), :] = jnp.dot(xb, wx_ref[...], preferred_element_type=F32) + bx_ref[...]
            return carry
        lax.fori_loop(0, TP // TM, gate_chunk, 0)

        sp8 = LRU_C * _softplus(-lam_ref[...])
        row = _row_iota((R, CB))

        def scan_chunk(ci, hprev):
            r0 = pl.multiple_of(ci * R, R)
            xc = xc_ref[pl.ds(r0, R), :]
            _, i, a, mult = _gate_values(ga_s[pl.ds(r0, R), :], gx_s[pl.ds(r0, R), :], xc, sp8)
            u = mult * (i * xc)
            k = 1
            while k < R:
                m = row >= k
                u = jnp.where(m, a * pltpu.roll(u, k, 0) + u, u)
                a = jnp.where(m, a * pltpu.roll(a, k, 0), a)
                k *= 2
            hv = u + a * hprev
            hs_ref[pl.ds(r0, R), :] = hv
            g = g_ref[pl.ds(r0, R), :]
            y_ref[pl.ds(r0, R), :] = (hv * (g * _sig(g))).astype(BF16)
            return jnp.sum(jnp.where(row == R - 1, hv, 0.0), axis=0, keepdims=True)
        lax.fori_loop(0, TP // R, scan_chunk, jnp.zeros((1, CB), F32))

    col = lambda off: pl.BlockSpec((TP, CB), lambda j: (0, off + j))
    vec = pl.BlockSpec((1, CB), lambda j: (0, j))
    wsp = pl.BlockSpec((None, CB, CB), lambda j: (j, 0, 0))
    return pl.pallas_call(
        body, name="lru_fwd",
        grid=(NCB,),
        in_specs=[col(0), col(NCB), pl.BlockSpec((LW, CB), lambda j: (0, j)), vec, wsp, vec, wsp, vec, vec],
        out_specs=[col(0), col(0), col(0)],
        out_shape=[jax.ShapeDtypeStruct((TP, DL), BF16), jax.ShapeDtypeStruct((TP, DL), F32),
                   jax.ShapeDtypeStruct((TP, DL), F32)],
        scratch_shapes=[pltpu.VMEM((TP, CB), F32), pltpu.VMEM((TP, CB), F32)],
        compiler_params=_cparams(),
    )(z, z, conv_w, conv_b, wa_g, b_a, wx_g, b_x, lam)


def _conf_fwd_conv(z, dw_w, dw_b):
    def body(u1_ref, u2_ref, w_ref, b_ref, vc_ref, vs):
        vs[pl.ds(0, KWP), :] = jnp.zeros((KWP, CB), F32)

        def glu_chunk(ci, carry):
            r0 = pl.multiple_of(ci * R, R)
            vs[pl.ds(KWP + r0, R), :] = u1_ref[pl.ds(r0, R), :] * _sig(u2_ref[pl.ds(r0, R), :])
            return carry
        lax.fori_loop(0, TP // R, glu_chunk, 0)

        bias = b_ref[...]

        def conv_chunk(ci, carry):
            r0 = pl.multiple_of(ci * R, R)
            buf = vs[pl.ds(r0, KWP + R), :]
            acc = jnp.zeros((R, CB), F32) + bias
            for rr in range(8):
                rolled = buf if rr == 0 else pltpu.roll(buf, rr, 0)
                for q in range(4):
                    s = 8 * q + rr
                    if s > KW - 1:
                        continue
                    k = KW - 1 - s
                    acc = acc + rolled[KWP - 8 * q:KWP - 8 * q + R, :] * w_ref[k:k + 1, :]
            vc_ref[pl.ds(r0, R), :] = acc
            return carry
        lax.fori_loop(0, TP // R, conv_chunk, 0)

    return pl.pallas_call(
        body, name="conf_fwd_conv",
        grid=(NCB,),
        in_specs=[pl.BlockSpec((TP, CB), lambda j: (0, 2 * NCB + j)),
                  pl.BlockSpec((TP, CB), lambda j: (0, 3 * NCB + j)),
                  pl.BlockSpec((KWP, CB), lambda j: (0, j)),
                  pl.BlockSpec((1, CB), lambda j: (0, j))],
        out_specs=pl.BlockSpec((TP, CB), lambda j: (0, j)),
        out_shape=jax.ShapeDtypeStruct((TP, DC), F32),
        scratch_shapes=[pltpu.VMEM((TP + KWP, CB), F32)],
        compiler_params=_cparams(),
    )(z, z, dw_w, dw_b)


def _ln_chunk(vc, lw, lb):
    mu = jnp.mean(vc, axis=-1, keepdims=True)
    xm = vc - mu
    var = jnp.mean(xm * xm, axis=-1, keepdims=True)
    rstd = lax.rsqrt(var + EPS)
    xhat = xm * rstd
    return xhat, rstd, xhat * lw + lb


def _conf_fwd_proj(vc, z, ln_w, ln_b, pw_w, pw_b):
    def body(vc_ref, g_ref, lw_ref, lb_ref, w_ref, b_ref, y_ref, p_ref, s_s):
        lw, lb = lw_ref[...], lb_ref[...]

        def ln_chunk(ci, carry):
            r0 = pl.multiple_of(ci * R, R)
            for half in range(2):
                rr = r0 + 8 * half
                _, _, ln = _ln_chunk(vc_ref[pl.ds(rr, 8), :], lw, lb)
                p_ref[pl.ds(rr, 8), :] = ln * _sig(ln)
            s_s[pl.ds(r0, R), :] = p_ref[pl.ds(r0, R), :].astype(BF16)
            return carry
        lax.fori_loop(0, TM // R, ln_chunk, 0, unroll=2)

        p_ref[...] = jnp.dot(s_s[...], w_ref[...], preferred_element_type=F32) + b_ref[...]

        def out_chunk(ci, carry):
            r0 = pl.multiple_of(ci * R, R)
            g = g_ref[pl.ds(r0, R), :]
            y_ref[pl.ds(r0, R), :] = (p_ref[pl.ds(r0, R), :] * (g * _sig(g))).astype(BF16)
            return carry
        lax.fori_loop(0, TM // R, out_chunk, 0)

    row = pl.BlockSpec((TM, DC), lambda i: (i, 0))
    vec = pl.BlockSpec((1, DC), lambda i: (0, 0))
    return pl.pallas_call(
        body, name="conf_fwd_proj",
        grid=(TP // TM,),
        in_specs=[row, pl.BlockSpec((TM, DC), lambda i: (i, 4)), vec, vec,
                  pl.BlockSpec((DC, DC), lambda i: (0, 0)), vec],
        out_specs=[row, row],
        out_shape=[jax.ShapeDtypeStruct((TP, DC), BF16), jax.ShapeDtypeStruct((TP, DC), F32)],
        scratch_shapes=[pltpu.VMEM((TM, DC), BF16)],
        compiler_params=_cparams(),
    )(vc, z, ln_w, ln_b, pw_w, pw_b)


def _outproj_loss(ylru, yconf, w_out, h, tgt, post_w):
    def body(yl_ref, yc_ref, w_ref, h_ref, t_ref, pw_ref, dout_ref, dy_ref, loss_ref, dpw_ref, y_s):
        i = pl.program_id(0)
        k = pl.program_id(1)

        @pl.when(k == 0)
        def _():
            y_s[...] = jnp.dot(yl_ref[...], w_ref[...], preferred_element_type=F32)

        @pl.when(k == 1)
        def _():
            y_s[...] += jnp.dot(yc_ref[...], w_ref[...], preferred_element_type=F32)

        @pl.when(jnp.logical_and(i == 0, k == 1))
        def _():
            loss_ref[...] = jnp.zeros_like(loss_ref)
            dpw_ref[...] = jnp.zeros_like(dpw_ref)

        @pl.when(k == 1)
        def _():
            pw = pw_ref[...]
            row = _row_iota((8, D))

            def chunk(ci, carry):
                r0 = pl.multiple_of(ci * 8, 8)
                yv = y_s[pl.ds(r0, 8), :]
                rs = lax.rsqrt(jnp.mean(yv * yv, axis=-1, keepdims=True) + EPS)
                grow = row + (i * TM + r0)
                valid = jnp.logical_and(grow >= NMETA, grow < T)
                yn = yv * rs
                err = jnp.where(valid, h_ref[pl.ds(r0, 8), :] + yn * pw - t_ref[pl.ds(r0, 8), :], 0.0)
                loss_ref[...] += err * err
                d_rn = err * (1.0 / D)
                dout_ref[pl.ds(r0, 8), :] = d_rn
                dpw_ref[...] += d_rn * yn
                gw = d_rn * pw
                dot = jnp.mean(gw * yv, axis=-1, keepdims=True)
                dy_ref[pl.ds(r0, 8), :] = (rs * gw - yv * (rs * rs * rs * dot)).astype(BF16)
                return carry
            lax.fori_loop(0, TM // 8, chunk, 0, unroll=4)

    row = pl.BlockSpec((TM, D), lambda i, k: (i, 0))
    half = pl.BlockSpec((TM, DL), lambda i, k: (i, 0))
    acc = pl.BlockSpec((8, D), lambda i, k: (0, 0))
    return pl.pallas_call(
        body, name="outproj_loss",
        grid=(TP // TM, 2),
        in_specs=[half, half, pl.BlockSpec((DL, D), lambda i, k: (k, 0)), row, row,
                  pl.BlockSpec((1, D), lambda i, k: (0, 0))],
        out_specs=[row, row, acc, acc],
        out_shape=[jax.ShapeDtypeStruct((TP, D), F32), jax.ShapeDtypeStruct((TP, D), BF16),
                   jax.ShapeDtypeStruct((8, D), F32), jax.ShapeDtypeStruct((8, D), F32)],
        scratch_shapes=[pltpu.VMEM((TM, D), F32)],
        compiler_params=_cparams(),
    )(ylru, yconf, w_out, h, tgt, post_w)


_NT = (((1,), (1,)), ((), ()))
_TN = (((0,), (0,)), ((), ()))


def _outproj_bwd(dy, ylru, yconf, w_out):
    def body(dy_ref, yl_ref, yc_ref, w_ref, dycat_ref, dw_ref):
        j = pl.program_id(0)
        dyv = dy_ref[...]
        dycat_ref[...] = lax.dot_general(dyv, w_ref[...], _NT, preferred_element_type=F32)

        @pl.when(j < NCB)
        def _():
            dw_ref[...] = lax.dot_general(yl_ref[...], dyv, _TN, preferred_element_type=F32).astype(BF16)

        @pl.when(j >= NCB)
        def _():
            dw_ref[...] = lax.dot_general(yc_ref[...], dyv, _TN, preferred_element_type=F32).astype(BF16)

    return pl.pallas_call(
        body, name="outproj_bwd",
        grid=(2 * NCB,),
        in_specs=[pl.BlockSpec((TP, D), lambda j: (0, 0)),
                  pl.BlockSpec((TP, CB), lambda j: (0, jnp.minimum(j, NCB - 1))),
                  pl.BlockSpec((TP, CB), lambda j: (0, jnp.maximum(j - NCB, 0))),
                  pl.BlockSpec((CB, D), lambda j: (j, 0))],
        out_specs=[pl.BlockSpec((TP, CB), lambda j: (0, j)), pl.BlockSpec((CB, D), lambda j: (j, 0))],
        out_shape=[jax.ShapeDtypeStruct((TP, D), F32), jax.ShapeDtypeStruct((D, D), BF16)],
        compiler_params=_cparams(),
    )(dy, ylru, yconf, w_out)


def _conf_bwd_proj(dycat, p, z, vc, ln_w, ln_b, pw_w):
    def body(dy_ref, p_ref, g_ref, vc_ref, lw_ref, lb_ref, w_ref,
             dvc_ref, dgc_ref, dpw_ref, vecs_ref, dp_s, s_s, ds_s):
        i = pl.program_id(0)
        lw, lb = lw_ref[...], lb_ref[...]

        @pl.when(i == 0)
        def _():
            dpw_ref[...] = jnp.zeros_like(dpw_ref)
            vecs_ref[...] = jnp.zeros_like(vecs_ref)

        def pre_chunk(ci, carry):
            r0 = pl.multiple_of(ci * R, R)
            for half in range(2):
                rr = r0 + 8 * half
                dyv = dy_ref[pl.ds(rr, 8), :]
                g = g_ref[pl.ds(rr, 8), :]
                sg = _sig(g)
                dp = dyv * (g * sg)
                dg = dyv * p_ref[pl.ds(rr, 8), :] * (sg * (1.0 + g * (1.0 - sg)))
                vecs_ref[0:8, :] += dp
                vecs_ref[8:16, :] += dg
                ds_s[pl.ds(rr, 8), :] = dp
                dvc_ref[pl.ds(rr, 8), :] = dg
            dp_s[pl.ds(r0, R), :] = ds_s[pl.ds(r0, R), :].astype(BF16)
            dgc_ref[pl.ds(r0, R), :] = dvc_ref[pl.ds(r0, R), :].astype(BF16)
            for half in range(2):
                rr = r0 + 8 * half
                _, _, ln = _ln_chunk(vc_ref[pl.ds(rr, 8), :], lw, lb)
                ds_s[pl.ds(rr, 8), :] = ln * _sig(ln)
            s_s[pl.ds(r0, R), :] = ds_s[pl.ds(r0, R), :].astype(BF16)
            return carry
        lax.fori_loop(0, TM // R, pre_chunk, 0, unroll=2)

        dpb = dp_s[...]
        ds_s[...] = lax.dot_general(dpb, w_ref[...], _NT, preferred_element_type=F32)
        dpw_ref[...] += lax.dot_general(s_s[...], dpb, _TN, preferred_element_type=F32)

        def post_chunk(ci, carry):
            r0 = pl.multiple_of(ci * 8, 8)
            xhat, rstd, ln = _ln_chunk(vc_ref[pl.ds(r0, 8), :], lw, lb)
            sl = _sig(ln)
            dln = ds_s[pl.ds(r0, 8), :] * (sl * (1.0 + ln * (1.0 - sl)))
            vecs_ref[16:24, :] += dln * xhat
            vecs_ref[24:32, :] += dln
            dxh = dln * lw
            m1 = jnp.mean(dxh, axis=-1, keepdims=True)
            m2 = jnp.mean(dxh * xhat, axis=-1, keepdims=True)
            dvc_ref[pl.ds(r0, 8), :] = rstd * (dxh - m1 - xhat * m2)
            return carry
        lax.fori_loop(0, TM // 8, post_chunk, 0, unroll=4)

    row = pl.BlockSpec((TM, DC), lambda i: (i, 0))
    vec = pl.BlockSpec((1, DC), lambda i: (0, 0))
    return pl.pallas_call(
        body, name="conf_bwd_proj",
        grid=(TP // TM,),
        in_specs=[pl.BlockSpec((TM, DC), lambda i: (i, 1)), row, pl.BlockSpec((TM, DC), lambda i: (i, 4)), row,
                  vec, vec, pl.BlockSpec((DC, DC), lambda i: (0, 0))],
        out_specs=[row, row, pl.BlockSpec((DC, DC), lambda i: (0, 0)), pl.BlockSpec((32, DC), lambda i: (0, 0))],
        out_shape=[jax.ShapeDtypeStruct((TP, DC), F32), jax.ShapeDtypeStruct((TP, DC), BF16),
                   jax.ShapeDtypeStruct((DC, DC), F32), jax.ShapeDtypeStruct((32, DC), F32)],
        scratch_shapes=[pltpu.VMEM((TM, DC), BF16), pltpu.VMEM((TM, DC), BF16), pltpu.VMEM((TM, DC), F32)],
        compiler_params=_cparams(),
    )(dycat, p, z, vc, ln_w, ln_b, pw_w)


def _conf_bwd_conv(dvc, z, dw_w):
    def body(dvc_ref, u1_ref, u2_ref, w_ref, du_ref, dw_ref, vecs_ref, vs, dvs):
        vs[pl.ds(0, KWP), :] = jnp.zeros((KWP, CB), F32)
        dvs[pl.ds(TP, KWP), :] = jnp.zeros((KWP, CB), F32)
        dw_ref[...] = jnp.zeros_like(dw_ref)
        vecs_ref[...] = jnp.zeros_like(vecs_ref)

        def fill_chunk(ci, carry):
            r0 = pl.multiple_of(ci * R, R)
            vs[pl.ds(KWP + r0, R), :] = u1_ref[pl.ds(r0, R), :] * _sig(u2_ref[pl.ds(r0, R), :])
            dv = dvc_ref[pl.ds(r0, R), :]
            dvs[pl.ds(r0, R), :] = dv
            vecs_ref[0:8, :] += _fold8(dv)
            return carry
        lax.fori_loop(0, TP // R, fill_chunk, 0)

        def conv_chunk(ci, carry):
            r0 = pl.multiple_of(ci * R, R)
            vbuf = vs[pl.ds(r0, KWP + R), :]
            dbuf = dvs[pl.ds(r0, KWP + R), :]
            dcur = dbuf[0:R, :]
            dv = jnp.zeros((R, CB), F32)
            for rr in range(8):
                vroll = vbuf if rr == 0 else pltpu.roll(vbuf, rr, 0)
                droll = dbuf if rr == 0 else pltpu.roll(dbuf, KWP + R - rr, 0)
                for q in range(4):
                    s = 8 * q + rr
                    if s > KW - 1:
                        continue
                    k = KW - 1 - s
                    dv = dv + droll[8 * q:8 * q + R, :] * w_ref[k:k + 1, :]
                    dw_ref[8 * k:8 * k + 8, :] += _fold8(dcur * vroll[KWP - 8 * q:KWP - 8 * q + R, :])
            u1 = u1_ref[pl.ds(r0, R), :]
            sg = _sig(u2_ref[pl.ds(r0, R), :])
            du1 = dv * sg
            du2 = dv * u1 * (sg * (1.0 - sg))
            du_ref[0, pl.ds(r0, R), :] = du1.astype(BF16)
            du_ref[1, pl.ds(r0, R), :] = du2.astype(BF16)
            vecs_ref[8:16, :] += _fold8(du1)
            vecs_ref[16:24, :] += _fold8(du2)
            return carry
        lax.fori_loop(0, TP // R, conv_chunk, 0)

    blk = pl.BlockSpec((TP, CB), lambda j: (0, j))
    return pl.pallas_call(
        body, name="conf_bwd_conv",
        grid=(NCB,),
        in_specs=[blk, pl.BlockSpec((TP, CB), lambda j: (0, 2 * NCB + j)),
                  pl.BlockSpec((TP, CB), lambda j: (0, 3 * NCB + j)), pl.BlockSpec((KWP, CB), lambda j: (0, j))],
        out_specs=[pl.BlockSpec((2, TP, CB), lambda j: (0, 0, j)), pl.BlockSpec((8 * KWP, CB), lambda j: (0, j)),
                   pl.BlockSpec((24, CB), lambda j: (0, j))],
        out_shape=[jax.ShapeDtypeStruct((2, TP, DC), BF16),
                   jax.ShapeDtypeStruct((8 * KWP, DC), F32), jax.ShapeDtypeStruct((24, DC), F32)],
        scratch_shapes=[pltpu.VMEM((TP + KWP, CB), F32), pltpu.VMEM((TP + KWP, CB), F32)],
        compiler_params=_cparams(),
    )(dvc, z, z, dw_w)


def _lru_bwd(dycat, z, xc, hs, conv_w, wa_g, b_a, wx_g, b_x, lam):
    NV = 6

    def body(dy_ref, x_ref, g_ref, xc_ref, hs_ref, cw_ref, wa_ref, ba_ref, wx_ref, bx_ref, lam_ref,
             dzl_ref, dwa_ref, dwx_ref, dcw_ref, vecs_ref, ga_s, gx_s, dxc_s):
        vecs_ref[...] = jnp.zeros_like(vecs_ref)
        dcw_ref[...] = jnp.zeros_like(dcw_ref)
        dxc_s[pl.ds(TP, 8), :] = jnp.zeros((8, CB), F32)

        def gate_chunk(ci, carry):
            r0 = pl.multiple_of(ci * TM, TM)
            xb = xc_ref[pl.ds(r0, TM), :].astype(BF16)
            ga_s[pl.ds(r0, TM), :] = jnp.dot(xb, wa_ref[...], preferred_element_type=F32) + ba_ref[...]
            gx_s[pl.ds(r0, TM), :] = jnp.dot(xb, wx_ref[...], preferred_element_type=F32) + bx_ref[...]
            return carry
        lax.fori_loop(0, TP // TM, gate_chunk, 0)

        sp8 = LRU_C * _softplus(-lam_ref[...])
        row = _row_iota((R, CB))
        nchunk = TP // R

        def scan_chunk(cj, carry):
            a_next, lam_next = carry
            ci = nchunk - 1 - cj
            r0 = pl.multiple_of(ci * R, R)
            dyv = dy_ref[pl.ds(r0, R), :]
            g = g_ref[pl.ds(r0, R), :]
            hv = hs_ref[pl.ds(r0, R), :]
            xc = xc_ref[pl.ds(r0, R), :]
            sg = _sig(g)
            dgl = dyv * hv * (sg * (1.0 + g * (1.0 - sg)))
            dzl_ref[1, pl.ds(r0, R), :] = dgl.astype(BF16)
            vecs_ref[0:8, :] += _fold8(dgl)
            dhs = dyv * (g * sg)
            r, i, a, mult = _gate_values(ga_s[pl.ds(r0, R), :], gx_s[pl.ds(r0, R), :], xc, sp8)
            b = jnp.where(row == R - 1, a_next, pltpu.roll(a, R - 1, 0))
            lv = dhs
            k = 1
            while k < R:
                m = row < R - k
                lv = jnp.where(m, lv + b * pltpu.roll(lv, R - k, 0), lv)
                b = jnp.where(m, b * pltpu.roll(b, R - k, 0), b)
                k *= 2
            lv = lv + b * lam_next
            p0 = pl.multiple_of(jnp.maximum(r0 - 8, 0), 8)
            hprev8 = jnp.where(ci > 0, hs_ref[pl.ds(p0, 8), :], 0.0)
            hprev = pltpu.roll(jnp.concatenate([hprev8, hv], axis=0), 1, 0)[8:8 + R, :]
            da = lv * hprev
            ixc = i * xc
            dmult = lv * ixc
            di = lv * mult * xc
            dxc_s[pl.ds(r0, R), :] = lv * mult * i
            a2 = a * a
            dlog_a = da * a - dmult * a2 / mult
            vecs_ref[32:40, :] += _fold8(dlog_a * r)
            dga = -(dlog_a * sp8) * r * (1.0 - r)
            dgx = di * i * (1.0 - i)
            ga_s[pl.ds(r0, R), :] = dga
            gx_s[pl.ds(r0, R), :] = dgx
            vecs_ref[16:24, :] += _fold8(dga)
            vecs_ref[24:32, :] += _fold8(dgx)
            a_first = jnp.sum(jnp.where(row == 0, a, 0.0), axis=0, keepdims=True)
            l_first = jnp.sum(jnp.where(row == 0, lv, 0.0), axis=0, keepdims=True)
            return a_first, l_first
        lax.fori_loop(0, nchunk, scan_chunk, (jnp.zeros((1, CB), F32), jnp.zeros((1, CB), F32)))

        dwa_ref[...] = jnp.zeros_like(dwa_ref)
        dwx_ref[...] = jnp.zeros_like(dwx_ref)

        def mm_chunk(ci, carry):
            r0 = pl.multiple_of(ci * TM, TM)
            xb = xc_ref[pl.ds(r0, TM), :].astype(BF16)
            dgab = ga_s[pl.ds(r0, TM), :].astype(BF16)
            dgxb = gx_s[pl.ds(r0, TM), :].astype(BF16)
            dxc_s[pl.ds(r0, TM), :] += (lax.dot_general(dgab, wa_ref[...], _NT, preferred_element_type=F32)
                                        + lax.dot_general(dgxb, wx_ref[...], _NT, preferred_element_type=F32))
            dwa_ref[...] += lax.dot_general(xb, dgab, _TN, preferred_element_type=F32)
            dwx_ref[...] += lax.dot_general(xb, dgxb, _TN, preferred_element_type=F32)
            return carry
        lax.fori_loop(0, TP // TM, mm_chunk, 0)

        taps = [cw_ref[k:k + 1, :] for k in range(LW)]

        def conv_chunk(ci, carry):
            r0 = pl.multiple_of(ci * R, R)
            dbuf = dxc_s[pl.ds(r0, R + 8), :]
            dcur = dbuf[0:R, :]
            p0 = pl.multiple_of(jnp.maximum(r0 - 8, 0), 8)
            xprev = jnp.where(ci > 0, x_ref[pl.ds(p0, 8), :], 0.0)
            xbuf = jnp.concatenate([xprev, x_ref[pl.ds(r0, R), :]], axis=0)
            dxl = dcur * taps[LW - 1]
            dcw_ref[8 * (LW - 1):8 * LW, :] += _fold8(dcur * xbuf[8:8 + R, :])
            for s in range(1, LW):
                k = LW - 1 - s
                dxl = dxl + pltpu.roll(dbuf, R + 8 - s, 0)[0:R, :] * taps[k]
                dcw_ref[8 * k:8 * k + 8, :] += _fold8(dcur * pltpu.roll(xbuf, s, 0)[8:8 + R, :])
            dzl_ref[0, pl.ds(r0, R), :] = dxl.astype(BF16)
            vecs_ref[8:16, :] += _fold8(dxl)
            vecs_ref[40:48, :] += _fold8(dcur)
            return carry
        lax.fori_loop(0, TP // R, conv_chunk, 0)
        vecs_ref[32:40, :] = vecs_ref[32:40, :] * (LRU_C * _sig(-lam_ref[...]))

    col = lambda off: pl.BlockSpec((TP, CB), lambda j: (0, off + j))
    vec = pl.BlockSpec((1, CB), lambda j: (0, j))
    wsp = pl.BlockSpec((None, CB, CB), lambda j: (j, 0, 0))
    return pl.pallas_call(
        body, name="lru_bwd",
        grid=(NCB,),
        in_specs=[col(0), col(0), col(NCB), col(0), col(0), pl.BlockSpec((LW, CB), lambda j: (0, j)),
                  wsp, vec, wsp, vec, vec],
        out_specs=[pl.BlockSpec((2, TP, CB), lambda j: (0, 0, j)), wsp, wsp,
                   pl.BlockSpec((8 * LW, CB), lambda j: (0, j)), pl.BlockSpec((8 * NV, CB), lambda j: (0, j))],
        out_shape=[jax.ShapeDtypeStruct((2, TP, DL), BF16),
                   jax.ShapeDtypeStruct((NCB, CB, CB), F32), jax.ShapeDtypeStruct((NCB, CB, CB), F32),
                   jax.ShapeDtypeStruct((8 * LW, DL), F32), jax.ShapeDtypeStruct((8 * NV, DL), F32)],
        scratch_shapes=[pltpu.VMEM((TP, CB), F32), pltpu.VMEM((TP, CB), F32), pltpu.VMEM((TP + 8, CB), F32)],
        compiler_params=_cparams(),
    )(dycat, z, z, xc, hs, conv_w, wa_g, b_a, wx_g, b_x, lam)


def _dz_section(sec, dzl_ref, dzc_ref, dgc_ref, use):
    @pl.when(sec < 2)
    def _():
        use(dzl_ref)

    @pl.when(jnp.logical_and(sec >= 2, sec < 4))
    def _():
        use(dzc_ref)

    @pl.when(sec == 4)
    def _():
        use(dgc_ref)


def _dz_specs(rows, index):
    return [pl.BlockSpec((None, rows, 1024), lambda a, b: (jnp.minimum(index(a, b)[1], 1), index(a, b)[0], 0)),
            pl.BlockSpec((None, rows, 1024), lambda a, b: (jnp.clip(index(a, b)[1] - 2, 0, 1), index(a, b)[0], 0)),
            pl.BlockSpec((rows, 1024), lambda a, b: (index(a, b)[0], 0))]


def _inproj_wgrad(name, hn, dzs):
    KB = 512
    nsec = dzs.shape[0]

    def body(hn_ref, dz_ref, dw_ref):
        dw_ref[...] = lax.dot_general(hn_ref[...], dz_ref[...], _TN, preferred_element_type=F32).astype(BF16)

    return pl.pallas_call(
        body, name=name,
        grid=(nsec, D // KB),
        in_specs=[pl.BlockSpec((TP, KB), lambda n, kb: (0, kb)),
                  pl.BlockSpec((None, TP, 1024), lambda n, kb: (n, 0, 0))],
        out_specs=pl.BlockSpec((KB, 1024), lambda n, kb: (kb, n)),
        out_shape=jax.ShapeDtypeStruct((D, nsec * 1024), BF16),
        compiler_params=_cparams(),
    )(hn, dzs)


def _sum_win_parts(parts_a, parts_b, parts_c):
    RB = 64

    def body(a_ref, b_ref, c_ref, o_ref):
        def chunk(ci, carry):
            r0 = pl.multiple_of(ci * R, R)
            for ref, base, ncol in ((a_ref, 0, 2048), (b_ref, 2048, 2048), (c_ref, 4096, 1024)):
                for c0 in range(0, ncol, 512):
                    acc = ref[0, pl.ds(r0, R), c0:c0 + 512].astype(F32)
                    for sidx in range(1, NDEV):
                        acc = acc + ref[sidx, pl.ds(r0, R), c0:c0 + 512].astype(F32)
                    o_ref[pl.ds(r0, R), base + c0:base + c0 + 512] = acc.astype(BF16)
            return carry
        lax.fori_loop(0, RB // R, chunk, 0)

    spec = lambda ncol: pl.BlockSpec((NDEV, RB, ncol), lambda i: (0, i, 0))
    return pl.pallas_call(
        body, name="sum_win_parts",
        grid=(D // NDEV // RB,),
        in_specs=[spec(2048), spec(2048), spec(1024)],
        out_specs=pl.BlockSpec((RB, NIN), lambda i: (i, 0)),
        out_shape=jax.ShapeDtypeStruct((D // NDEV, NIN), BF16),
        compiler_params=_cparams(),
    )(parts_a, parts_b, parts_c)


def _inproj_bwd(dzl, dzc, dgc, w_in, h, dout, pre_w):
    nsec = NIN // 1024

    def body(dzl_ref, dzc_ref, dgc_ref, w_ref, h_ref, dout_ref, pw_ref, dh_ref, dpw_ref, acc_s):
        i = pl.program_id(0)
        s = pl.program_id(1)

        @pl.when(s == 0)
        def _():
            acc_s[...] = jnp.zeros_like(acc_s)

        def use(dz_ref):
            acc_s[...] += lax.dot_general(dz_ref[...], w_ref[...], _NT, preferred_element_type=F32)
        _dz_section(s, dzl_ref, dzc_ref, dgc_ref, use)

        @pl.when(jnp.logical_and(i == 0, s == nsec - 1))
        def _():
            dpw_ref[...] = jnp.zeros_like(dpw_ref)

        @pl.when(s == nsec - 1)
        def _():
            pw = pw_ref[...]

            def chunk(ci, carry):
                r0 = pl.multiple_of(ci * 8, 8)
                hv = h_ref[pl.ds(r0, 8), :]
                dhn = acc_s[pl.ds(r0, 8), :]
                rs = lax.rsqrt(jnp.mean(hv * hv, axis=-1, keepdims=True) + EPS)
                dpw_ref[...] += dhn * (hv * rs)
                gw = dhn * pw
                dot = jnp.mean(gw * hv, axis=-1, keepdims=True)
                dh_ref[pl.ds(r0, 8), :] = rs * gw - hv * (rs * rs * rs * dot) + dout_ref[pl.ds(r0, 8), :]
                return carry
            lax.fori_loop(0, TM // 8, chunk, 0, unroll=4)

    row = pl.BlockSpec((TM, D), lambda i, s: (i, 0))
    return pl.pallas_call(
        body, name="inproj_bwd",
        grid=(TP // TM, nsec),
        in_specs=_dz_specs(TM, lambda i, s: (i, s)) + [
            pl.BlockSpec((D, 1024), lambda i, s: (0, s)), row, row, pl.BlockSpec((1, D), lambda i, s: (0, 0))],
        out_specs=[row, pl.BlockSpec((8, D), lambda i, s: (0, 0))],
        out_shape=[jax.ShapeDtypeStruct((TP, D), F32), jax.ShapeDtypeStruct((8, D), F32)],
        scratch_shapes=[pltpu.VMEM((TM, D), F32)],
        compiler_params=_cparams(),
    )(dzl, dzc, dgc, w_in, h, dout, pre_w)


def _adamw(name, parts, w, m, v, block_rows):
    rows, cols = w.shape
    nparts = parts.shape[0]
    cw = cols if cols <= 640 else 512

    def body(p_ref, w_ref, m_ref, v_ref, g_ref, d_ref, nm_ref, nv_ref):
        def chunk(ci, carry):
            r0 = pl.multiple_of(ci * R, R)
            for c0 in range(0, cols, cw):
                at = (pl.ds(r0, R), slice(c0, c0 + cw))
                g = p_ref[(0,) + at].astype(F32)
                for sidx in range(1, nparts):
                    g = g + p_ref[(sidx,) + at].astype(F32)
                delta, mv, vv = _adam_math(g, w_ref[at], m_ref[at], v_ref[at])
                g_ref[at] = g
                nm_ref[at] = mv
                nv_ref[at] = vv
                d_ref[at] = delta
            return carry
        lax.fori_loop(0, block_rows // R, chunk, 0)

    blk = pl.BlockSpec((block_rows, cols), lambda i: (i, 0))
    shp = jax.ShapeDtypeStruct((rows, cols), F32)
    return pl.pallas_call(
        body, name=name,
        grid=(rows // block_rows,),
        in_specs=[pl.BlockSpec((nparts, block_rows, cols), lambda i: (0, i, 0)), blk, blk, blk],
        out_specs=[blk, blk, blk, blk],
        out_shape=[shp, shp, shp, shp],
        compiler_params=_cparams(),
    )(parts, w, m, v)


def _adam_math(g, w, m, v):
    c1 = 1.0 / (1.0 - ADAM_B1 ** ADAM_STEP)
    c2 = 1.0 / (1.0 - ADAM_B2 ** ADAM_STEP)
    mv = ADAM_B1 * m + (1.0 - ADAM_B1) * g
    vv = ADAM_B2 * v + (1.0 - ADAM_B2) * (g * g)
    upd = (mv * c1) / (jnp.sqrt(vv * c2) + ADAM_EPS) + ADAM_WD * w
    return -ADAM_LR * upd, mv, vv


_VEC = [("pre_norm_w", 2), ("post_norm_w", 2), ("b_in", 5), ("lru_conv_b", 1), ("b_gate_a", 1), ("b_gate_x", 1),
        ("lru_lambda", 1), ("conf_dw_b", 1), ("conf_ln_w", 1), ("conf_ln_b", 1), ("conf_pw_b", 1)]
_VEC_ROWS = 24
_LOSS_ROW = 17
_SM_ROWS = 64


def _pack_grads(dprew_acc, dpostw_acc, cvecs, kvecs, lvecs, dcw_acc, ddw_acc, dh, loss_acc):
    def body(pre_ref, post_ref, c_ref, k_ref, l_ref, dcw_ref, ddw_ref, dh_ref, loss_ref, vec_ref, small_ref, tmp):
        s8 = lambda ref, r: jnp.sum(ref[8 * r:8 * r + 8, :], axis=0, keepdims=True)
        vec_ref[...] = jnp.zeros_like(vec_ref)
        pre, post = s8(pre_ref, 0), s8(post_ref, 0)
        rows = [pre[:, 0:1024], pre[:, 1024:2048], post[:, 0:1024], post[:, 1024:2048],
                s8(l_ref, 1), s8(l_ref, 0), s8(k_ref, 1), s8(k_ref, 2), s8(c_ref, 1),
                s8(l_ref, 5), s8(l_ref, 2), s8(l_ref, 3), s8(l_ref, 4),
                s8(k_ref, 0), s8(c_ref, 2), s8(c_ref, 3), s8(c_ref, 0)]
        for r, val in enumerate(rows):
            vec_ref[r:r + 1, :] = val
        vec_ref[_LOSS_ROW:_LOSS_ROW + 1, :] = jnp.zeros((1, 1024), F32) + (0.5 / D) * jnp.sum(loss_ref[...])

        small_ref[...] = jnp.zeros_like(small_ref)
        for k in range(LW):
            tmp[k:k + 1, :] = s8(dcw_ref, k)
        for k in range(KW):
            tmp[8 + k:9 + k, :] = s8(ddw_ref, k)
        for d in range(NDEV):
            small_ref[d, 0:LW, 0:128] = tmp[0:LW, 128 * d:128 * d + 128]
            small_ref[d, 8:8 + KW, 0:128] = tmp[8:8 + KW, 128 * d:128 * d + 128]
            small_ref[d, 40:56, :] = dh_ref[:, 256 * d:256 * d + 256]

    full = lambda a: pl.BlockSpec(a.shape, lambda i: (0,) * a.ndim)
    ins = [dprew_acc, dpostw_acc, cvecs, kvecs, lvecs, dcw_acc, ddw_acc]
    return pl.pallas_call(
        body, name="pack_grads",
        grid=(1,),
        in_specs=[full(a) for a in ins] + [pl.BlockSpec((NMETA, D), lambda i: (0, 0)), full(loss_acc)],
        out_specs=[pl.BlockSpec((_VEC_ROWS, 1024), lambda i: (0, 0)),
                   pl.BlockSpec((NDEV, _SM_ROWS, 256), lambda i: (0, 0, 0))],
        out_shape=[jax.ShapeDtypeStruct((_VEC_ROWS, 1024), F32), jax.ShapeDtypeStruct((NDEV, _SM_ROWS, 256), F32)],
        scratch_shapes=[pltpu.VMEM((40, 1024), F32)],
        compiler_params=_cparams(),
    )(*ins, dh, loss_acc)


def _adamw_vec(parts, W, M, V):
    nv = len(_VEC)

    def body(*refs):
        p_ref = refs[0]
        w_refs, m_refs, v_refs = refs[1:1 + nv], refs[1 + nv:1 + 2 * nv], refs[1 + 2 * nv:1 + 3 * nv]
        outs = refs[1 + 3 * nv:]

        def total(r):
            acc = p_ref[0, r:r + 1, :]
            for sidx in range(1, NDEV):
                acc = acc + p_ref[sidx, r:r + 1, :]
            return acc

        row = 0
        for idx, (_, nrows) in enumerate(_VEC):
            for part in range(nrows):
                cols = slice(1024 * part, 1024 * part + 1024)
                g = total(row + part)
                delta, mv, vv = _adam_math(g, w_refs[idx][:, cols], m_refs[idx][:, cols], v_refs[idx][:, cols])
                for o, val in zip(outs[4 * idx:4 * idx + 4], (g, delta, mv, vv)):
                    o[:, cols] = val
            row += nrows
        outs[-1][...] = total(_LOSS_ROW)[:, 0:128]

    names = [n for n, _ in _VEC]
    flat = lambda d: [d[n].reshape(1, -1) for n in names]
    ws, ms, vs = flat(W), flat(M), flat(V)
    res = pl.pallas_call(
        body, name="adamw_vec",
        out_shape=[jax.ShapeDtypeStruct(w.shape, F32) for w in ws for _ in range(4)]
        + [jax.ShapeDtypeStruct((1, 128), F32)],
        compiler_params=_cparams(),
    )(parts, *ws, *ms, *vs)
    return {n: tuple(res[4 * i:4 * i + 4]) for i, n in enumerate(names)}, res[-1]


def _adamw_small(parts, W, M, V):
    where = {"lru_conv_w": (slice(0, LW), slice(0, 128)), "conf_dw_w": (slice(8, 8 + KW), slice(0, 128)),
             "meta_tokens": (slice(40, 56), slice(0, 256))}
    names = list(where)

    def body(*refs):
        p_ref = refs[0]
        outs = refs[10:]
        for idx, n in enumerate(names):
            rs, cs = where[n]
            g = p_ref[0, rs, cs]
            for sidx in range(1, NDEV):
                g = g + p_ref[sidx, rs, cs]
            delta, mv, vv = _adam_math(g, refs[1 + idx][...], refs[4 + idx][...], refs[7 + idx][...])
            for o, val in zip(outs[4 * idx:4 * idx + 4], (g, delta, mv, vv)):
                o[...] = val

    two_d = lambda a: a.reshape(a.shape[-2:])
    ws, ms, vs = ([two_d(d[n]) for n in names] for d in (W, M, V))
    res = pl.pallas_call(
        body, name="adamw_small",
        out_shape=[jax.ShapeDtypeStruct(w.shape, F32) for w in ws for _ in range(4)],
        compiler_params=_cparams(),
    )(parts, *ws, *ms, *vs)
    return {n: tuple(res[4 * i:4 * i + 4]) for i, n in enumerate(names)}


def _pack_small(lru_cw, dw_w, meta):
    buf = jnp.zeros((_SM_ROWS, 256), F32)
    buf = buf.at[0:LW, 0:128].set(lru_cw)
    buf = buf.at[8:8 + dw_w.shape[0], 0:128].set(dw_w)
    return buf.at[40:56, :].set(meta)


def _block_diag4(w):
    w4 = w.reshape(NCB, 4, 64, 64)
    eye = jnp.eye(4, dtype=w.dtype)
    return jnp.einsum("ghij,hk->ghikj", w4, eye).reshape(NCB, CB, CB)


def _diag_blocks(g):
    g5 = g.reshape(NCB, 4, 64, 4, 64)
    return jnp.stack([g5[:, hh, :, hh, :] for hh in range(4)], axis=1).reshape(16, 64, 64)


def _local_step(x, target, meta_full, win_full, out_weights, lru_cw_full, dw_w_full, W, send):
    h = jnp.concatenate([meta_full, x, jnp.zeros((TP - T, D), F32)], axis=0)
    tgt = jnp.concatenate([jnp.zeros((NMETA, D), F32), target, jnp.zeros((TP - T, D), F32)], axis=0)
    wa_g = _block_diag4(W["w_gate_a"][0]).astype(BF16)
    wx_g = _block_diag4(W["w_gate_x"][0]).astype(BF16)

    z, hn = _prenorm_inproj(h, W["pre_norm_w"], win_full, W["b_in"])
    ylru, xc, hs = _lru_fwd(z, lru_cw_full, W["lru_conv_b"], wa_g, W["b_gate_a"], wx_g, W["b_gate_x"],
                            W["lru_lambda"])
    vc = _conf_fwd_conv(z, dw_w_full, W["conf_dw_b"])
    wout_full, pw_full = out_weights(vc)
    yconf, p = _conf_fwd_proj(vc, z, W["conf_ln_w"], W["conf_ln_b"], pw_full, W["conf_pw_b"])
    dout, dy, loss_acc, dpostw_acc = _outproj_loss(ylru, yconf, wout_full, h, tgt, W["post_norm_w"])

    dycat, dwout_part = _outproj_bwd(dy, ylru, yconf, wout_full)
    tok = send("w_out", dwout_part)
    dvc, dgc, dpw_part, cvecs = _conf_bwd_proj(dycat, p, z, vc, W["conf_ln_w"] + tok, W["conf_ln_b"], pw_full)
    tok = send("conf_pw_w", dpw_part)
    tok = tok + send("w_in_c", _inproj_wgrad("inproj_wgrad_c", hn, dgc[None]))
    dzc, ddw_acc, kvecs = _conf_bwd_conv(dvc, z, dw_w_full + tok)
    tok = send("w_in_b", _inproj_wgrad("inproj_wgrad_b", hn, dzc))
    dzl, dwa_g, dwx_g, dcw_acc, lvecs = _lru_bwd(dycat, z, xc, hs, lru_cw_full, wa_g, W["b_gate_a"] + tok, wx_g,
                                                 W["b_gate_x"], W["lru_lambda"])
    tok = send("w_in_a", _inproj_wgrad("inproj_wgrad_a", hn, dzl))
    dh, dprew_acc = _inproj_bwd(dzl, dzc, dgc, win_full, h, dout, W["pre_norm_w"] + tok)

    vec_pack, small_part = _pack_grads(dprew_acc, dpostw_acc, cvecs, kvecs, lvecs, dcw_acc, ddw_acc, dh, loss_acc)
    return dh, vec_pack, small_part, _diag_blocks(dwa_g), _diag_blocks(dwx_g)


def kernel(x, meta_tokens, pre_norm_w, post_norm_w, w_in, b_in, lru_conv_w, lru_conv_b, w_gate_a, b_gate_a, w_gate_x, b_gate_x, lru_lambda, conf_dw_w, conf_dw_b, conf_ln_w, conf_ln_b, conf_pw_w, conf_pw_b, w_out, loss_target, m_meta_tokens, m_pre_norm_w, m_post_norm_w, m_w_in, m_b_in, m_lru_conv_w, m_lru_conv_b, m_w_gate_a, m_b_gate_a, m_w_gate_x, m_b_gate_x, m_lru_lambda, m_conf_dw_w, m_conf_dw_b, m_conf_ln_w, m_conf_ln_b, m_conf_pw_w, m_conf_pw_b, m_w_out, v_meta_tokens, v_pre_norm_w, v_post_norm_w, v_w_in, v_b_in, v_lru_conv_w, v_lru_conv_b, v_w_gate_a, v_b_gate_a, v_w_gate_x, v_b_gate_x, v_lru_lambda, v_conf_dw_w, v_conf_dw_b, v_conf_ln_w, v_conf_ln_b, v_conf_pw_w, v_conf_pw_b, v_w_out):
    W = dict(meta_tokens=meta_tokens, pre_norm_w=pre_norm_w, post_norm_w=post_norm_w, w_in=w_in, b_in=b_in,
             lru_conv_w=lru_conv_w, lru_conv_b=lru_conv_b, w_gate_a=w_gate_a, b_gate_a=b_gate_a,
             w_gate_x=w_gate_x, b_gate_x=b_gate_x, lru_lambda=lru_lambda, conf_dw_w=conf_dw_w,
             conf_dw_b=conf_dw_b, conf_ln_w=conf_ln_w, conf_ln_b=conf_ln_b, conf_pw_w=conf_pw_w,
             conf_pw_b=conf_pw_b, w_out=w_out)
    M = dict(meta_tokens=m_meta_tokens, pre_norm_w=m_pre_norm_w, post_norm_w=m_post_norm_w, w_in=m_w_in,
             b_in=m_b_in, lru_conv_w=m_lru_conv_w, lru_conv_b=m_lru_conv_b, w_gate_a=m_w_gate_a,
             b_gate_a=m_b_gate_a, w_gate_x=m_w_gate_x, b_gate_x=m_b_gate_x, lru_lambda=m_lru_lambda,
             conf_dw_w=m_conf_dw_w, conf_dw_b=m_conf_dw_b, conf_ln_w=m_conf_ln_w, conf_ln_b=m_conf_ln_b,
             conf_pw_w=m_conf_pw_w, conf_pw_b=m_conf_pw_b, w_out=m_w_out)
    V = dict(meta_tokens=v_meta_tokens, pre_norm_w=v_pre_norm_w, post_norm_w=v_post_norm_w, w_in=v_w_in,
             b_in=v_b_in, lru_conv_w=v_lru_conv_w, lru_conv_b=v_lru_conv_b, w_gate_a=v_w_gate_a,
             b_gate_a=v_b_gate_a, w_gate_x=v_w_gate_x, b_gate_x=v_b_gate_x, lru_lambda=v_lru_lambda,
             conf_dw_w=v_conf_dw_w, conf_dw_b=v_conf_dw_b, conf_ln_w=v_conf_ln_w, conf_ln_b=v_conf_ln_b,
             conf_pw_w=v_conf_pw_w, conf_pw_b=v_conf_pw_b, w_out=v_w_out)
    names = list(W.keys())
    shapes = {n: W[n].shape for n in names}

    small = _pack_small(lru_conv_w[0], conf_dw_w[0], meta_tokens)
    gathered, _ = _exchange_start("gather_start", [
        (small, jax.ShapeDtypeStruct((NDEV, _SM_ROWS, 256), F32), _whole, _slot),
        (w_in[0].astype(BF16), jax.ShapeDtypeStruct((D, NIN), BF16), _whole, _cols(NIN // NDEV)),
        (w_out[0].astype(BF16), jax.ShapeDtypeStruct((D, D), BF16), _whole, _rows(D // NDEV)),
        (conf_pw_w[0].astype(BF16), jax.ShapeDtypeStruct((DC, DC), BF16), _whole, _rows(DC // NDEV)),
    ])
    small_all, win_full = _exchange_wait("gather_wait_in", gathered[0:2], x)
    unshard = lambda a: jnp.transpose(a, (1, 0, 2)).reshape(a.shape[1], -1)
    lru_cw_full = unshard(small_all[:, 0:LW, 0:128])
    dw_w_full = unshard(small_all[:, 8:8 + KWP, 0:128])
    meta_full = unshard(small_all[:, 40:56, :])

    def out_weights(after):
        return _exchange_wait("gather_wait_out", gathered[2:4], after)

    row_stage = lambda ncol: (jax.ShapeDtypeStruct((NDEV, D // NDEV, ncol), BF16), _rows(D // NDEV))
    piece = {"w_in_a": row_stage(2048), "w_in_b": row_stage(2048), "w_in_c": row_stage(1024),
             "w_out": row_stage(D),
             "conf_pw_w": (jax.ShapeDtypeStruct((NDEV, DC // NDEV, DC), BF16), _rows(DC // NDEV))}
    sent = {}

    def send(name, part):
        handles, token = _exchange_start("scatter_" + name + "_start",
                                         [(part.astype(BF16), piece[name][0], piece[name][1], _slot)])
        sent[name] = handles
        return token[0, 0]

    dh, vec_pack, small_part, dwa, dwx = _local_step(
        x[0], loss_target[0], meta_full, win_full, out_weights, lru_cw_full, dw_w_full, W, send)
    grad_x = dh[NMETA:T][None]
    gate = jax.ShapeDtypeStruct((NDEV, 16 * 64, 64), BF16)
    rest, _ = _exchange_start("scatter_rest_start", [
        (small_part, jax.ShapeDtypeStruct((NDEV, _SM_ROWS, 256), F32), _slot, _slot),
        (vec_pack, jax.ShapeDtypeStruct((NDEV, _VEC_ROWS, 1024), F32), _whole, _slot),
        (dwa.reshape(16 * 64, 64).astype(BF16), gate, _whole, _slot),
        (dwx.reshape(16 * 64, 64).astype(BF16), gate, _whole, _slot),
    ])

    G, DW, NM, NV = {}, {}, {}, {}
    (wout_parts,) = _exchange_wait("scatter_w_out_wait", sent["w_out"], dh)
    G["w_out"], DW["w_out"], NM["w_out"], NV["w_out"] = _adamw("adamw_w_out", wout_parts, w_out[0], m_w_out[0], v_w_out[0], 64)
    (pw_parts,) = _exchange_wait("scatter_conf_pw_w_wait", sent["conf_pw_w"], G["w_out"])
    G["conf_pw_w"], DW["conf_pw_w"], NM["conf_pw_w"], NV["conf_pw_w"] = _adamw(
        "adamw_pw", pw_parts, conf_pw_w[0], m_conf_pw_w[0], v_conf_pw_w[0], 128)
    (parts_c,) = _exchange_wait("scatter_w_in_c_wait", sent["w_in_c"], G["conf_pw_w"])
    (parts_b,) = _exchange_wait("scatter_w_in_b_wait", sent["w_in_b"], parts_c)
    (parts_a,) = _exchange_wait("scatter_w_in_a_wait", sent["w_in_a"], parts_b)
    win_rows = _sum_win_parts(parts_a, parts_b, parts_c)
    win_stage2, _ = _exchange_start("scatter_w_in_stage2_start", [
        (win_rows, jax.ShapeDtypeStruct((NDEV, D // NDEV, NIN // NDEV), BF16), _cols(NIN // NDEV), _slot)])
    small_parts, vec_parts, wa_parts, wx_parts = _exchange_wait("scatter_rest_wait", rest, win_rows)
    res = dict(_adamw_small(small_parts, W, M, V))
    vec_res, loss_row = _adamw_vec(vec_parts, W, M, V)
    res.update(vec_res)
    for n, parts in (("w_gate_a", wa_parts), ("w_gate_x", wx_parts)):
        res[n] = _adamw("adamw_" + n, parts, *[d[n].reshape(16 * 64, 64) for d in (W, M, V)], 16 * 64)
    (win_sum,) = _exchange_wait("scatter_w_in_stage2_wait", win_stage2, res["w_gate_x"][0])
    res["w_in"] = _adamw("adamw_w_in", win_sum.reshape(1, D, NIN // NDEV), w_in[0], m_w_in[0], v_w_in[0], 256)
    for n, vals in res.items():
        for dst, val in zip((G, DW, NM, NV), vals):
            dst[n] = val
    for dst in (G, DW, NM, NV):
        for n in names:
            dst[n] = dst[n].reshape(shapes[n])
    loss = loss_row[0, 0]

    return (loss, grad_x, *[G[n] for n in names], *[DW[n] for n in names],
            *[NM[n] for n in names], *[NV[n] for n in names])
```

```python
import functools

import jax
import jax.numpy as jnp
from jax import lax
from jax.experimental import pallas as pl
from jax.experimental.pallas import tpu as pltpu

F32 = jnp.float32
BF16 = jnp.bfloat16

D = 2048
DL = 1024
DC = 1024
NIN = 5120
NMETA = 16
SEQ = 2048
T = NMETA + SEQ
TP = 2176
TM = 544
CB = 256
NCB = DL // CB
R = 16
KW = 31
KWP = 32
LW = 4
LRU_C = 8.0
EPS = 1e-6
NDEV = 8

ADAM_LR = 0.001
ADAM_B1 = 0.9
ADAM_B2 = 0.999
ADAM_EPS = 1e-08
ADAM_WD = 0.01
ADAM_STEP = 10

VMEM_LIMIT = 56 * 1024 * 1024


def _cparams():
    return pltpu.CompilerParams(vmem_limit_bytes=VMEM_LIMIT)


def _sig(x):
    return 1.0 / (1.0 + jnp.exp(-x))


def _expm1_neg(y):
    poly = y * (1.0 + y * (0.5 + y * (1.0 / 6.0 + y * (1.0 / 24.0 + y * (1.0 / 120.0)))))
    return jnp.where(y > -0.1, poly, jnp.exp(y) - 1.0)


def _softplus(x):
    e = jnp.exp(-jnp.abs(x))
    w = 1.0 + e
    l1p = jnp.where(w == 1.0, e, jnp.log(w) * e / (w - 1.0))
    return jnp.maximum(x, 0.0) + l1p


def _row_iota(shape):
    return lax.broadcasted_iota(jnp.int32, shape, 0)


def _fold8(v):
    return v[0:8, :] + v[8:16, :]


_FLIPS = [(k >> 2 & 1, k >> 1 & 1, k & 1) for k in range(1, NDEV)]
_HBM = pl.BlockSpec(memory_space=pltpu.HBM)
_SEM = pl.BlockSpec(memory_space=pltpu.SEMAPHORE)


def _peers():
    x, y, c = lax.axis_index("x"), lax.axis_index("y"), lax.axis_index("c")
    out = []
    for dx, dy, dc in _FLIPS:
        px = 1 - x if dx else x
        py = 1 - y if dy else y
        pc = 1 - c if dc else c
        out.append(((px, py, pc), 4 * px + 2 * py + pc))
    return 4 * x + 2 * y + c, out


def _exchange_start(name, items):
    n = len(items)

    def body(*refs):
        srcs, lands = refs[:n], refs[n:2 * n]
        outs = refs[2 * n:]
        send_sems, recv_sems, local_sems = outs[:n], outs[n:2 * n], outs[2 * n:3 * n]
        token = outs[-1]
        me, peers = _peers()
        for a in range(n):
            src_at, dst_at = items[a][2], items[a][3]
            pltpu.make_async_copy(src_at(srcs[a], me), dst_at(lands[a], me), local_sems[a]).start()
        for a in range(n):
            src_at, dst_at = items[a][2], items[a][3]
            for k, (pos, peer) in enumerate(peers):
                pltpu.make_async_remote_copy(
                    src_ref=src_at(srcs[a], peer), dst_ref=dst_at(lands[a], me),
                    send_sem=send_sems[a].at[k], recv_sem=recv_sems[a].at[k],
                    device_id=pos, device_id_type=pl.DeviceIdType.MESH).start()
        token[...] = jnp.zeros_like(token)

    srcs = [pltpu.with_memory_space_constraint(it[0], pltpu.HBM) for it in items]
    lands = [pltpu.with_memory_space_constraint(lax.empty(it[1].shape, it[1].dtype), pltpu.HBM) for it in items]
    sem7 = pltpu.SemaphoreType.DMA((NDEV - 1,))
    res = pl.pallas_call(
        body, name=name,
        out_shape=([sem7] * (2 * n) + [pltpu.SemaphoreType.DMA(())] * n
                   + [pltpu.HBM(a.shape, a.dtype) for a in srcs] + [pltpu.HBM(a.shape, a.dtype) for a in lands]
                   + [jax.ShapeDtypeStruct((8, 128), F32)]),
        in_specs=[_HBM] * (2 * n),
        out_specs=[_SEM] * (3 * n) + [_HBM] * (2 * n) + [pl.BlockSpec(memory_space=pltpu.VMEM)],
        input_output_aliases={i: 3 * n + i for i in range(2 * n)},
        compiler_params=pltpu.CompilerParams(has_side_effects=pltpu.SideEffectType.DATAFLOW_SIDE_EFFECTING),
    )(*srcs, *lands)
    handles = [dict(send=res[a], recv=res[n + a], local=res[2 * n + a], src=res[3 * n + a], land=res[4 * n + a],
                    src_at=items[a][2], dst_at=items[a][3]) for a in range(n)]
    return handles, res[-1]


def _exchange_wait(name, handles, after):
    n = len(handles)

    def body(*refs):
        srcs, lands = refs[:n], refs[n:2 * n]
        send_sems, recv_sems, local_sems = refs[2 * n:3 * n], refs[3 * n:4 * n], refs[4 * n:5 * n]
        me, peers = _peers()
        for a in range(n):
            src_at, dst_at = handles[a]["src_at"], handles[a]["dst_at"]
            for k, (pos, peer) in enumerate(peers):
                cp = pltpu.make_async_remote_copy(
                    src_ref=src_at(srcs[a], peer), dst_ref=dst_at(lands[a], peer),
                    send_sem=send_sems[a].at[k], recv_sem=recv_sems[a].at[k],
                    device_id=pos, device_id_type=pl.DeviceIdType.MESH)
                cp.wait_send()
                cp.wait_recv()
            pltpu.make_async_copy(src_at(srcs[a], me), dst_at(lands[a], me), local_sems[a]).wait()

    srcs = [hd["src"] for hd in handles]
    lands = [hd["land"] for hd in handles]
    res = pl.pallas_call(
        body, name=name,
        out_shape=[pltpu.HBM(a.shape, a.dtype) for a in srcs] + [pltpu.HBM(a.shape, a.dtype) for a in lands],
        in_specs=[_HBM] * (2 * n) + [_SEM] * (3 * n) + [pl.BlockSpec(memory_space=pl.ANY)],
        out_specs=[_HBM] * (2 * n),
        input_output_aliases={i: i for i in range(2 * n)},
        compiler_params=pltpu.CompilerParams(has_side_effects=pltpu.SideEffectType.DATAFLOW_SIDE_EFFECTING),
    )(*srcs, *lands, *[hd["send"] for hd in handles], *[hd["recv"] for hd in handles],
      *[hd["local"] for hd in handles], after)
    return list(res[n:])


_SIDE = pltpu.SideEffectType.DATAFLOW_SIDE_EFFECTING
_WCOLS = NIN // NDEV


def _win_cols(ref, l):
    return ref.at[:, pl.ds(pl.multiple_of(l * _WCOLS, 128), _WCOLS)]


def _win_routes():
    x, y, c = lax.axis_index("x"), lax.axis_index("y"), lax.axis_index("c")
    pos = [(x, y, 1 - c), (1 - x, y, c), (x, 1 - y, c), (1 - x, 1 - y, c)]
    return 4 * x + 2 * y + c, [(p, 4 * p[0] + 2 * p[1] + p[2]) for p in pos]


def _win_gather_start(shard):
    def body(src, land, send_sems, recv_sems, local_sem, src_thru, land_thru, token):
        me, routes = _win_routes()
        pltpu.make_async_copy(src, _win_cols(land, me), local_sem).start()
        for k, (pos, _) in enumerate(routes):
            pltpu.make_async_remote_copy(src_ref=src, dst_ref=_win_cols(land, me), send_sem=send_sems.at[k],
                                         recv_sem=recv_sems.at[k], device_id=pos,
                                         device_id_type=pl.DeviceIdType.MESH).start()
        token[...] = jnp.zeros_like(token)

    src = pltpu.with_memory_space_constraint(shard, pltpu.HBM)
    land = pltpu.with_memory_space_constraint(lax.empty((D, NIN), BF16), pltpu.HBM)
    sem4 = pltpu.SemaphoreType.DMA((4,))
    res = pl.pallas_call(
        body, name="win_gather_start",
        out_shape=[sem4, sem4, pltpu.SemaphoreType.DMA(()), pltpu.HBM(src.shape, BF16), pltpu.HBM(land.shape, BF16),
                   jax.ShapeDtypeStruct((8, 128), F32)],
        in_specs=[_HBM, _HBM],
        out_specs=[_SEM, _SEM, _SEM, _HBM, _HBM, pl.BlockSpec(memory_space=pltpu.VMEM)],
        input_output_aliases={0: 3, 1: 4},
        compiler_params=pltpu.CompilerParams(has_side_effects=_SIDE),
    )(src, land)
    return dict(send=res[0], recv=res[1], local=res[2], src=res[3], land=res[4]), res[5]


def _win_gather_forward(hd, after):
    def body(land, recv_sems, after_ref, land_thru, fsend_sems, frecv_sems):
        me, routes = _win_routes()
        sibling = routes[0][0]
        for k in (1, 2, 3):
            pos, peer = routes[k]
            piece = _win_cols(land, peer)
            pltpu.make_async_remote_copy(src_ref=piece, dst_ref=piece, send_sem=fsend_sems.at[k - 1],
                                         recv_sem=recv_sems.at[k], device_id=pos,
                                         device_id_type=pl.DeviceIdType.MESH).wait_recv()
            pltpu.make_async_remote_copy(src_ref=piece, dst_ref=piece, send_sem=fsend_sems.at[k - 1],
                                         recv_sem=frecv_sems.at[k - 1], device_id=sibling,
                                         device_id_type=pl.DeviceIdType.MESH).start()

    sem3 = pltpu.SemaphoreType.DMA((3,))
    res = pl.pallas_call(
        body, name="win_gather_forward",
        out_shape=[pltpu.HBM(hd["land"].shape, BF16), sem3, sem3],
        in_specs=[_HBM, _SEM, pl.BlockSpec(memory_space=pl.ANY)],
        out_specs=[_HBM, _SEM, _SEM],
        input_output_aliases={0: 0},
        compiler_params=pltpu.CompilerParams(has_side_effects=_SIDE),
    )(hd["land"], hd["recv"], after)
    return dict(hd, land=res[0], fsend=res[1], frecv=res[2])


def _win_gather_wait(hd):
    def body(src, land, send_sems, recv_sems, local_sem, fsend_sems, frecv_sems, src_thru, land_thru):
        me, routes = _win_routes()
        sib_pos, sibling = routes[0]
        for k, (pos, peer) in enumerate(routes):
            cp = pltpu.make_async_remote_copy(src_ref=src, dst_ref=_win_cols(land, peer), send_sem=send_sems.at[k],
                                              recv_sem=recv_sems.at[k], device_id=pos,
                                              device_id_type=pl.DeviceIdType.MESH)
            cp.wait_send()
            if k == 0:
                cp.wait_recv()
        pltpu.make_async_copy(src, _win_cols(land, me), local_sem).wait()
        for k in (1, 2, 3):
            mine = _win_cols(land, routes[k][1])
            theirs = _win_cols(land, 4 * routes[k][0][0] + 2 * routes[k][0][1] + sib_pos[2])
            cp = pltpu.make_async_remote_copy(src_ref=mine, dst_ref=theirs, send_sem=fsend_sems.at[k - 1],
                                              recv_sem=frecv_sems.at[k - 1], device_id=sib_pos,
                                              device_id_type=pl.DeviceIdType.MESH)
            cp.wait_send()
            cp.wait_recv()

    res = pl.pallas_call(
        body, name="win_gather_wait",
        out_shape=[pltpu.HBM(hd["src"].shape, BF16), pltpu.HBM(hd["land"].shape, BF16)],
        in_specs=[_HBM, _HBM] + [_SEM] * 5,
        out_specs=[_HBM, _HBM],
        input_output_aliases={0: 0, 1: 1},
        compiler_params=pltpu.CompilerParams(has_side_effects=_SIDE),
    )(hd["src"], hd["land"], hd["send"], hd["recv"], hd["local"], hd["fsend"], hd["frecv"])
    return res[1]


def _whole(ref, l):
    return ref


def _slot(ref, l):
    return ref.at[l]


def _cols(width):
    def at(ref, l):
        return ref.at[:, pl.ds(pl.multiple_of(l * width, 128), width)]
    return at


def _rows(height):
    def at(ref, l):
        return ref.at[pl.ds(pl.multiple_of(l * height, 8), height), :]
    return at


def _prenorm_inproj(h, pre_w, w_in, b_in):
    nsec = NIN // 1024

    def body(h_ref, pw_ref, w_ref, b_ref, z_ref, hn_ref):
        @pl.when(pl.program_id(1) == 0)
        def _():
            pw = pw_ref[...]

            def chunk(ci, carry):
                r0 = pl.multiple_of(ci * R, R)
                xv = h_ref[pl.ds(r0, R), :]
                ms = jnp.mean(xv * xv, axis=-1, keepdims=True)
                hn_ref[pl.ds(r0, R), :] = (xv * lax.rsqrt(ms + EPS) * pw).astype(BF16)
                return carry
            lax.fori_loop(0, TM // R, chunk, 0, unroll=2)

        z_ref[...] = jnp.dot(hn_ref[...], w_ref[...], preferred_element_type=F32) + b_ref[...]

    return pl.pallas_call(
        body, name="prenorm_inproj",
        grid=(TP // TM, nsec),
        in_specs=[pl.BlockSpec((TM, D), lambda i, n: (i, 0)),
                  pl.BlockSpec((1, D), lambda i, n: (0, 0)),
                  pl.BlockSpec((D, 1024), lambda i, n: (0, n)),
                  pl.BlockSpec((1, 1024), lambda i, n: (0, n))],
        out_specs=[pl.BlockSpec((TM, 1024), lambda i, n: (i, n)),
                   pl.BlockSpec((TM, D), lambda i, n: (i, 0))],
        out_shape=[jax.ShapeDtypeStruct((TP, NIN), F32), jax.ShapeDtypeStruct((TP, D), BF16)],
        compiler_params=_cparams(),
    )(h, pre_w, w_in, b_in)


def _gate_values(ga, gx, xc, sp8):
    r = _sig(ga)
    i = _sig(gx)
    log_a = -(r * sp8)
    a = jnp.exp(log_a)
    mult = jnp.sqrt(-_expm1_neg(2.0 * log_a))
    return r, i, a, mult


def _lru_fwd(z, conv_w, conv_b, wa_g, b_a, wx_g, b_x, lam):
    def body(x_ref, g_ref, cw_ref, cb_ref, wa_ref, ba_ref, wx_ref, bx_ref, lam_ref,
             y_ref, xc_ref, hs_ref, ga_s, gx_s):
        taps = [cw_ref[k:k + 1, :] for k in range(LW)]
        cb = cb_ref[...]

        def conv_chunk(ci, carry):
            r0 = pl.multiple_of(ci * R, R)
            cur = x_ref[pl.ds(r0, R), :]
            p0 = pl.multiple_of(jnp.maximum(r0 - 8, 0), 8)
            prev = jnp.where(ci > 0, x_ref[pl.ds(p0, 8), :], 0.0)
            buf = jnp.concatenate([prev, cur], axis=0)
            acc = cur * taps[LW - 1] + cb
            for s in range(1, LW):
                acc = acc + pltpu.roll(buf, s, 0)[8:8 + R, :] * taps[LW - 1 - s]
            xc_ref[pl.ds(r0, R), :] = acc
            return carry
        lax.fori_loop(0, TP // R, conv_chunk, 0)

        def gate_chunk(ci, carry):
            r0 = pl.multiple_of(ci * TM, TM)
            xb = xc_ref[pl.ds(r0, TM), :].astype(BF16)
            ga_s[pl.ds(r0, TM), :] = jnp.dot(xb, wa_ref[...], preferred_element_type=F32) + ba_ref[...]
            gx_s[pl.ds(r0, TM), :] = jnp.dot(xb, wx_ref[...], preferred_element_type=F32) + bx_ref[...]
            return carry
        lax.fori_loop(0, TP // TM, gate_chunk, 0)

        sp8 = LRU_C * _softplus(-lam_ref[...])
        row = _row_iota((R, CB))

        def scan_chunk(ci, hprev):
            r0 = pl.multiple_of(ci * R, R)
            xc = xc_ref[pl.ds(r0, R), :]
            _, i, a, mult = _gate_values(ga_s[pl.ds(r0, R), :], gx_s[pl.ds(r0, R), :], xc, sp8)
            u = mult * (i * xc)
            k = 1
            while k < R:
                m = row >= k
                u = jnp.where(m, a * pltpu.roll(u, k, 0) + u, u)
                a = jnp.where(m, a * pltpu.roll(a, k, 0), a)
                k *= 2
            hv = u + a * hprev
            hs_ref[pl.ds(r0, R), :] = hv
            g = g_ref[pl.ds(r0, R), :]
            y_ref[pl.ds(r0, R), :] = (hv * (g * _sig(g))).astype(BF16)
            return jnp.sum(jnp.where(row == R - 1, hv, 0.0), axis=0, keepdims=True)
        lax.fori_loop(0, TP // R, scan_chunk, jnp.zeros((1, CB), F32))

    col = lambda off: pl.BlockSpec((TP, CB), lambda j: (0, off + j))
    vec = pl.BlockSpec((1, CB), lambda j: (0, j))
    wsp = pl.BlockSpec((None, CB, CB), lambda j: (j, 0, 0))
    return pl.pallas_call(
        body, name="lru_fwd",
        grid=(NCB,),
        in_specs=[col(0), col(NCB), pl.BlockSpec((LW, CB), lambda j: (0, j)), vec, wsp, vec, wsp, vec, vec],
        out_specs=[col(0), col(0), col(0)],
        out_shape=[jax.ShapeDtypeStruct((TP, DL), BF16), jax.ShapeDtypeStruct((TP, DL), F32),
                   jax.ShapeDtypeStruct((TP, DL), F32)],
        scratch_shapes=[pltpu.VMEM((TP, CB), F32), pltpu.VMEM((TP, CB), F32)],
        compiler_params=_cparams(),
    )(z, z, conv_w, conv_b, wa_g, b_a, wx_g, b_x, lam)


def _conf_fwd_conv(z, dw_w, dw_b):
    def body(u1_ref, u2_ref, w_ref, b_ref, vc_ref, vs):
        vs[pl.ds(0, KWP), :] = jnp.zeros((KWP, CB), F32)

        def glu_chunk(ci, carry):
            r0 = pl.multiple_of(ci * R, R)
            vs[pl.ds(KWP + r0, R), :] = u1_ref[pl.ds(r0, R), :] * _sig(u2_ref[pl.ds(r0, R), :])
            return carry
        lax.fori_loop(0, TP // R, glu_chunk, 0)

        bias = b_ref[...]

        def conv_chunk(ci, carry):
            r0 = pl.multiple_of(ci * R, R)
            buf = vs[pl.ds(r0, KWP + R), :]
            acc = jnp.zeros((R, CB), F32) + bias
            for rr in range(8):
                rolled = buf if rr == 0 else pltpu.roll(buf, rr, 0)
                for q in range(4):
                    s = 8 * q + rr
                    if s > KW - 1:
                        continue
                    k = KW - 1 - s
                    acc = acc + rolled[KWP - 8 * q:KWP - 8 * q + R, :] * w_ref[k:k + 1, :]
            vc_ref[pl.ds(r0, R), :] = acc
            return carry
        lax.fori_loop(0, TP // R, conv_chunk, 0)

    return pl.pallas_call(
        body, name="conf_fwd_conv",
        grid=(NCB,),
        in_specs=[pl.BlockSpec((TP, CB), lambda j: (0, 2 * NCB + j)),
                  pl.BlockSpec((TP, CB), lambda j: (0, 3 * NCB + j)),
                  pl.BlockSpec((KWP, CB), lambda j: (0, j)),
                  pl.BlockSpec((1, CB), lambda j: (0, j))],
        out_specs=pl.BlockSpec((TP, CB), lambda j: (0, j)),
        out_shape=jax.ShapeDtypeStruct((TP, DC), F32),
        scratch_shapes=[pltpu.VMEM((TP + KWP, CB), F32)],
        compiler_params=_cparams(),
    )(z, z, dw_w, dw_b)


def _ln_chunk(vc, lw, lb):
    mu = jnp.mean(vc, axis=-1, keepdims=True)
    xm = vc - mu
    var = jnp.mean(xm * xm, axis=-1, keepdims=True)
    rstd = lax.rsqrt(var + EPS)
    xhat = xm * rstd
    return xhat, rstd, xhat * lw + lb


def _conf_fwd_proj(vc, z, ln_w, ln_b, pw_w, pw_b):
    def body(vc_ref, g_ref, lw_ref, lb_ref, w_ref, b_ref, y_ref, p_ref, s_s):
        lw, lb = lw_ref[...], lb_ref[...]

        def ln_chunk(ci, carry):
            r0 = pl.multiple_of(ci * R, R)
            for half in range(2):
                rr = r0 + 8 * half
                _, _, ln = _ln_chunk(vc_ref[pl.ds(rr, 8), :], lw, lb)
                p_ref[pl.ds(rr, 8), :] = ln * _sig(ln)
            s_s[pl.ds(r0, R), :] = p_ref[pl.ds(r0, R), :].astype(BF16)
            return carry
        lax.fori_loop(0, TM // R, ln_chunk, 0, unroll=2)

        p_ref[...] = jnp.dot(s_s[...], w_ref[...], preferred_element_type=F32) + b_ref[...]

        def out_chunk(ci, carry):
            r0 = pl.multiple_of(ci * R, R)
            g = g_ref[pl.ds(r0, R), :]
            y_ref[pl.ds(r0, R), :] = (p_ref[pl.ds(r0, R), :] * (g * _sig(g))).astype(BF16)
            return carry
        lax.fori_loop(0, TM // R, out_chunk, 0)

    row = pl.BlockSpec((TM, DC), lambda i: (i, 0))
    vec = pl.BlockSpec((1, DC), lambda i: (0, 0))
    return pl.pallas_call(
        body, name="conf_fwd_proj",
        grid=(TP // TM,),
        in_specs=[row, pl.BlockSpec((TM, DC), lambda i: (i, 4)), vec, vec,
                  pl.BlockSpec((DC, DC), lambda i: (0, 0)), vec],
        out_specs=[row, row],
        out_shape=[jax.ShapeDtypeStruct((TP, DC), BF16), jax.ShapeDtypeStruct((TP, DC), F32)],
        scratch_shapes=[pltpu.VMEM((TM, DC), BF16)],
        compiler_params=_cparams(),
    )(vc, z, ln_w, ln_b, pw_w, pw_b)


def _outproj_loss(ylru, yconf, w_out, h, tgt, post_w):
    def body(yl_ref, yc_ref, w_ref, h_ref, t_ref, pw_ref, dout_ref, dy_ref, loss_ref, dpw_ref, y_s):
        i = pl.program_id(0)
        k = pl.program_id(1)

        @pl.when(k == 0)
        def _():
            y_s[...] = jnp.dot(yl_ref[...], w_ref[...], preferred_element_type=F32)

        @pl.when(k == 1)
        def _():
            y_s[...] += jnp.dot(yc_ref[...], w_ref[...], preferred_element_type=F32)

        @pl.when(jnp.logical_and(i == 0, k == 1))
        def _():
            loss_ref[...] = jnp.zeros_like(loss_ref)
            dpw_ref[...] = jnp.zeros_like(dpw_ref)

        @pl.when(k == 1)
        def _():
            pw = pw_ref[...]
            row = _row_iota((8, D))

            def chunk(ci, carry):
                r0 = pl.multiple_of(ci * 8, 8)
                yv = y_s[pl.ds(r0, 8), :]
                rs = lax.rsqrt(jnp.mean(yv * yv, axis=-1, keepdims=True) + EPS)
                grow = row + (i * TM + r0)
                valid = jnp.logical_and(grow >= NMETA, grow < T)
                yn = yv * rs
                err = jnp.where(valid, h_ref[pl.ds(r0, 8), :] + yn * pw - t_ref[pl.ds(r0, 8), :], 0.0)
                loss_ref[...] += err * err
                d_rn = err * (1.0 / D)
                dout_ref[pl.ds(r0, 8), :] = d_rn
                dpw_ref[...] += d_rn * yn
                gw = d_rn * pw
                dot = jnp.mean(gw * yv, axis=-1, keepdims=True)
                dy_ref[pl.ds(r0, 8), :] = (rs * gw - yv * (rs * rs * rs * dot)).astype(BF16)
                return carry
            lax.fori_loop(0, TM // 8, chunk, 0, unroll=4)

    row = pl.BlockSpec((TM, D), lambda i, k: (i, 0))
    half = pl.BlockSpec((TM, DL), lambda i, k: (i, 0))
    acc = pl.BlockSpec((8, D), lambda i, k: (0, 0))
    return pl.pallas_call(
        body, name="outproj_loss",
        grid=(TP // TM, 2),
        in_specs=[half, half, pl.BlockSpec((DL, D), lambda i, k: (k, 0)), row, row,
                  pl.BlockSpec((1, D), lambda i, k: (0, 0))],
        out_specs=[row, row, acc, acc],
        out_shape=[jax.ShapeDtypeStruct((TP, D), F32), jax.ShapeDtypeStruct((TP, D), BF16),
                   jax.ShapeDtypeStruct((8, D), F32), jax.ShapeDtypeStruct((8, D), F32)],
        scratch_shapes=[pltpu.VMEM((TM, D), F32)],
        compiler_params=_cparams(),
    )(ylru, yconf, w_out, h, tgt, post_w)


_NT = (((1,), (1,)), ((), ()))
_TN = (((0,), (0,)), ((), ()))


def _outproj_bwd(dy, ylru, yconf, w_out):
    def body(dy_ref, yl_ref, yc_ref, w_ref, dycat_ref, dw_ref):
        j = pl.program_id(0)
        dyv = dy_ref[...]
        dycat_ref[...] = lax.dot_general(dyv, w_ref[...], _NT, preferred_element_type=F32)

        @pl.when(j < NCB)
        def _():
            dw_ref[...] = lax.dot_general(yl_ref[...], dyv, _TN, preferred_element_type=F32).astype(BF16)

        @pl.when(j >= NCB)
        def _():
            dw_ref[...] = lax.dot_general(yc_ref[...], dyv, _TN, preferred_element_type=F32).astype(BF16)

    return pl.pallas_call(
        body, name="outproj_bwd",
        grid=(2 * NCB,),
        in_specs=[pl.BlockSpec((TP, D), lambda j: (0, 0)),
                  pl.BlockSpec((TP, CB), lambda j: (0, jnp.minimum(j, NCB - 1))),
                  pl.BlockSpec((TP, CB), lambda j: (0, jnp.maximum(j - NCB, 0))),
                  pl.BlockSpec((CB, D), lambda j: (j, 0))],
        out_specs=[pl.BlockSpec((TP, CB), lambda j: (0, j)), pl.BlockSpec((CB, D), lambda j: (j, 0))],
        out_shape=[jax.ShapeDtypeStruct((TP, D), F32), jax.ShapeDtypeStruct((D, D), BF16)],
        compiler_params=_cparams(),
    )(dy, ylru, yconf, w_out)


def _conf_bwd_proj(dycat, p, z, vc, ln_w, ln_b, pw_w):
    def body(dy_ref, p_ref, g_ref, vc_ref, lw_ref, lb_ref, w_ref,
             dvc_ref, dgc_ref, dpw_ref, vecs_ref, dp_s, s_s, ds_s):
        i = pl.program_id(0)
        lw, lb = lw_ref[...], lb_ref[...]

        @pl.when(i == 0)
        def _():
            dpw_ref[...] = jnp.zeros_like(dpw_ref)
            vecs_ref[...] = jnp.zeros_like(vecs_ref)

        def pre_chunk(ci, carry):
            r0 = pl.multiple_of(ci * R, R)
            for half in range(2):
                rr = r0 + 8 * half
                dyv = dy_ref[pl.ds(rr, 8), :]
                g = g_ref[pl.ds(rr, 8), :]
                sg = _sig(g)
                dp = dyv * (g * sg)
                dg = dyv * p_ref[pl.ds(rr, 8), :] * (sg * (1.0 + g * (1.0 - sg)))
                vecs_ref[0:8, :] += dp
                vecs_ref[8:16, :] += dg
                ds_s[pl.ds(rr, 8), :] = dp
                dvc_ref[pl.ds(rr, 8), :] = dg
            dp_s[pl.ds(r0, R), :] = ds_s[pl.ds(r0, R), :].astype(BF16)
            dgc_ref[pl.ds(r0, R), :] = dvc_ref[pl.ds(r0, R), :].astype(BF16)
            for half in range(2):
                rr = r0 + 8 * half
                _, _, ln = _ln_chunk(vc_ref[pl.ds(rr, 8), :], lw, lb)
                ds_s[pl.ds(rr, 8), :] = ln * _sig(ln)
            s_s[pl.ds(r0, R), :] = ds_s[pl.ds(r0, R), :].astype(BF16)
            return carry
        lax.fori_loop(0, TM // R, pre_chunk, 0, unroll=2)

        dpb = dp_s[...]
        ds_s[...] = lax.dot_general(dpb, w_ref[...], _NT, preferred_element_type=F32)
        dpw_ref[...] += lax.dot_general(s_s[...], dpb, _TN, preferred_element_type=F32)

        def post_chunk(ci, carry):
            r0 = pl.multiple_of(ci * 8, 8)
            xhat, rstd, ln = _ln_chunk(vc_ref[pl.ds(r0, 8), :], lw, lb)
            sl = _sig(ln)
            dln = ds_s[pl.ds(r0, 8), :] * (sl * (1.0 + ln * (1.0 - sl)))
            vecs_ref[16:24, :] += dln * xhat
            vecs_ref[24:32, :] += dln
            dxh = dln * lw
            m1 = jnp.mean(dxh, axis=-1, keepdims=True)
            m2 = jnp.mean(dxh * xhat, axis=-1, keepdims=True)
            dvc_ref[pl.ds(r0, 8), :] = rstd * (dxh - m1 - xhat * m2)
            return carry
        lax.fori_loop(0, TM // 8, post_chunk, 0, unroll=4)

    row = pl.BlockSpec((TM, DC), lambda i: (i, 0))
    vec = pl.BlockSpec((1, DC), lambda i: (0, 0))
    return pl.pallas_call(
        body, name="conf_bwd_proj",
        grid=(TP // TM,),
        in_specs=[pl.BlockSpec((TM, DC), lambda i: (i, 1)), row, pl.BlockSpec((TM, DC), lambda i: (i, 4)), row,
                  vec, vec, pl.BlockSpec((DC, DC), lambda i: (0, 0))],
        out_specs=[row, row, pl.BlockSpec((DC, DC), lambda i: (0, 0)), pl.BlockSpec((32, DC), lambda i: (0, 0))],
        out_shape=[jax.ShapeDtypeStruct((TP, DC), F32), jax.ShapeDtypeStruct((TP, DC), BF16),
                   jax.ShapeDtypeStruct((DC, DC), F32), jax.ShapeDtypeStruct((32, DC), F32)],
        scratch_shapes=[pltpu.VMEM((TM, DC), BF16), pltpu.VMEM((TM, DC), BF16), pltpu.VMEM((TM, DC), F32)],
        compiler_params=_cparams(),
    )(dycat, p, z, vc, ln_w, ln_b, pw_w)


def _conf_bwd_conv(dvc, z, dw_w):
    def body(dvc_ref, u1_ref, u2_ref, w_ref, du_ref, dw_ref, vecs_ref, vs, dvs):
        vs[pl.ds(0, KWP), :] = jnp.zeros((KWP, CB), F32)
        dvs[pl.ds(TP, KWP), :] = jnp.zeros((KWP, CB), F32)
        dw_ref[...] = jnp.zeros_like(dw_ref)
        vecs_ref[...] = jnp.zeros_like(vecs_ref)

        def fill_chunk(ci, carry):
            r0 = pl.multiple_of(ci * R, R)
            vs[pl.ds(KWP + r0, R), :] = u1_ref[pl.ds(r0, R), :] * _sig(u2_ref[pl.ds(r0, R), :])
            dv = dvc_ref[pl.ds(r0, R), :]
            dvs[pl.ds(r0, R), :] = dv
            vecs_ref[0:8, :] += _fold8(dv)
            return carry
        lax.fori_loop(0, TP // R, fill_chunk, 0)

        def conv_chunk(ci, carry):
            r0 = pl.multiple_of(ci * R, R)
            vbuf = vs[pl.ds(r0, KWP + R), :]
            dbuf = dvs[pl.ds(r0, KWP + R), :]
            dcur = dbuf[0:R, :]
            dv = jnp.zeros((R, CB), F32)
            for rr in range(8):
                vroll = vbuf if rr == 0 else pltpu.roll(vbuf, rr, 0)
                droll = dbuf if rr == 0 else pltpu.roll(dbuf, KWP + R - rr, 0)
                for q in range(4):
                    s = 8 * q + rr
                    if s > KW - 1:
                        continue
                    k = KW - 1 - s
                    dv = dv + droll[8 * q:8 * q + R, :] * w_ref[k:k + 1, :]
                    dw_ref[8 * k:8 * k + 8, :] += _fold8(dcur * vroll[KWP - 8 * q:KWP - 8 * q + R, :])
            u1 = u1_ref[pl.ds(r0, R), :]
            sg = _sig(u2_ref[pl.ds(r0, R), :])
            du1 = dv * sg
            du2 = dv * u1 * (sg * (1.0 - sg))
            du_ref[0, pl.ds(r0, R), :] = du1.astype(BF16)
            du_ref[1, pl.ds(r0, R), :] = du2.astype(BF16)
            vecs_ref[8:16, :] += _fold8(du1)
            vecs_ref[16:24, :] += _fold8(du2)
            return carry
        lax.fori_loop(0, TP // R, conv_chunk, 0)

    blk = pl.BlockSpec((TP, CB), lambda j: (0, j))
    return pl.pallas_call(
        body, name="conf_bwd_conv",
        grid=(NCB,),
        in_specs=[blk, pl.BlockSpec((TP, CB), lambda j: (0, 2 * NCB + j)),
                  pl.BlockSpec((TP, CB), lambda j: (0, 3 * NCB + j)), pl.BlockSpec((KWP, CB), lambda j: (0, j))],
        out_specs=[pl.BlockSpec((2, TP, CB), lambda j: (0, 0, j)), pl.BlockSpec((8 * KWP, CB), lambda j: (0, j)),
                   pl.BlockSpec((24, CB), lambda j: (0, j))],
        out_shape=[jax.ShapeDtypeStruct((2, TP, DC), BF16),
                   jax.ShapeDtypeStruct((8 * KWP, DC), F32), jax.ShapeDtypeStruct((24, DC), F32)],
        scratch_shapes=[pltpu.VMEM((TP + KWP, CB), F32), pltpu.VMEM((TP + KWP, CB), F32)],
        compiler_params=_cparams(),
    )(dvc, z, z, dw_w)


def _lru_bwd(dycat, z, xc, hs, conv_w, wa_g, b_a, wx_g, b_x, lam):
    NV = 6

    def body(dy_ref, x_ref, g_ref, xc_ref, hs_ref, cw_ref, wa_ref, ba_ref, wx_ref, bx_ref, lam_ref,
             dzl_ref, dwa_ref, dwx_ref, dcw_ref, vecs_ref, ga_s, gx_s, dxc_s):
        vecs_ref[...] = jnp.zeros_like(vecs_ref)
        dcw_ref[...] = jnp.zeros_like(dcw_ref)
        dxc_s[pl.ds(TP, 8), :] = jnp.zeros((8, CB), F32)

        def gate_chunk(ci, carry):
            r0 = pl.multiple_of(ci * TM, TM)
            xb = xc_ref[pl.ds(r0, TM), :].astype(BF16)
            ga_s[pl.ds(r0, TM), :] = jnp.dot(xb, wa_ref[...], preferred_element_type=F32) + ba_ref[...]
            gx_s[pl.ds(r0, TM), :] = jnp.dot(xb, wx_ref[...], preferred_element_type=F32) + bx_ref[...]
            return carry
        lax.fori_loop(0, TP // TM, gate_chunk, 0)

        sp8 = LRU_C * _softplus(-lam_ref[...])
        row = _row_iota((R, CB))
        nchunk = TP // R

        def scan_chunk(cj, carry):
            a_next, lam_next = carry
            ci = nchunk - 1 - cj
            r0 = pl.multiple_of(ci * R, R)
            dyv = dy_ref[pl.ds(r0, R), :]
            g = g_ref[pl.ds(r0, R), :]
            hv = hs_ref[pl.ds(r0, R), :]
            xc = xc_ref[pl.ds(r0, R), :]
            sg = _sig(g)
            dgl = dyv * hv * (sg * (1.0 + g * (1.0 - sg)))
            dzl_ref[1, pl.ds(r0, R), :] = dgl.astype(BF16)
            vecs_ref[0:8, :] += _fold8(dgl)
            dhs = dyv * (g * sg)
            r, i, a, mult = _gate_values(ga_s[pl.ds(r0, R), :], gx_s[pl.ds(r0, R), :], xc, sp8)
            b = jnp.where(row == R - 1, a_next, pltpu.roll(a, R - 1, 0))
            lv = dhs
            k = 1
            while k < R:
                m = row < R - k
                lv = jnp.where(m, lv + b * pltpu.roll(lv, R - k, 0), lv)
                b = jnp.where(m, b * pltpu.roll(b, R - k, 0), b)
                k *= 2
            lv = lv + b * lam_next
            p0 = pl.multiple_of(jnp.maximum(r0 - 8, 0), 8)
            hprev8 = jnp.where(ci > 0, hs_ref[pl.ds(p0, 8), :], 0.0)
            hprev = pltpu.roll(jnp.concatenate([hprev8, hv], axis=0), 1, 0)[8:8 + R, :]
            da = lv * hprev
            ixc = i * xc
            dmult = lv * ixc
            di = lv * mult * xc
            dxc_s[pl.ds(r0, R), :] = lv * mult * i
            a2 = a * a
            dlog_a = da * a - dmult * a2 / mult
            vecs_ref[32:40, :] += _fold8(dlog_a * r)
            dga = -(dlog_a * sp8) * r * (1.0 - r)
            dgx = di * i * (1.0 - i)
            ga_s[pl.ds(r0, R), :] = dga
            gx_s[pl.ds(r0, R), :] = dgx
            vecs_ref[16:24, :] += _fold8(dga)
            vecs_ref[24:32, :] += _fold8(dgx)
            a_first = jnp.sum(jnp.where(row == 0, a, 0.0), axis=0, keepdims=True)
            l_first = jnp.sum(jnp.where(row == 0, lv, 0.0), axis=0, keepdims=True)
            return a_first, l_first
        lax.fori_loop(0, nchunk, scan_chunk, (jnp.zeros((1, CB), F32), jnp.zeros((1, CB), F32)))

        dwa_ref[...] = jnp.zeros_like(dwa_ref)
        dwx_ref[...] = jnp.zeros_like(dwx_ref)

        def mm_chunk(ci, carry):
            r0 = pl.multiple_of(ci * TM, TM)
            xb = xc_ref[pl.ds(r0, TM), :].astype(BF16)
            dgab = ga_s[pl.ds(r0, TM), :].astype(BF16)
            dgxb = gx_s[pl.ds(r0, TM), :].astype(BF16)
            dxc_s[pl.ds(r0, TM), :] += (lax.dot_general(dgab, wa_ref[...], _NT, preferred_element_type=F32)
                                        + lax.dot_general(dgxb, wx_ref[...], _NT, preferred_element_type=F32))
            dwa_ref[...] += lax.dot_general(xb, dgab, _TN, preferred_element_type=F32)
            dwx_ref[...] += lax.dot_general(xb, dgxb, _TN, preferred_element_type=F32)
            return carry
        lax.fori_loop(0, TP // TM, mm_chunk, 0)

        taps = [cw_ref[k:k + 1, :] for k in range(LW)]

        def conv_chunk(ci, carry):
            r0 = pl.multiple_of(ci * R, R)
            dbuf = dxc_s[pl.ds(r0, R + 8), :]
            dcur = dbuf[0:R, :]
            p0 = pl.multiple_of(jnp.maximum(r0 - 8, 0), 8)
            xprev = jnp.where(ci > 0, x_ref[pl.ds(p0, 8), :], 0.0)
            xbuf = jnp.concatenate([xprev, x_ref[pl.ds(r0, R), :]], axis=0)
            dxl = dcur * taps[LW - 1]
            dcw_ref[8 * (LW - 1):8 * LW, :] += _fold8(dcur * xbuf[8:8 + R, :])
            for s in range(1, LW):
                k = LW - 1 - s
                dxl = dxl + pltpu.roll(dbuf, R + 8 - s, 0)[0:R, :] * taps[k]
                dcw_ref[8 * k:8 * k + 8, :] += _fold8(dcur * pltpu.roll(xbuf, s, 0)[8:8 + R, :])
            dzl_ref[0, pl.ds(r0, R), :] = dxl.astype(BF16)
            vecs_ref[8:16, :] += _fold8(dxl)
            vecs_ref[40:48, :] += _fold8(dcur)
            return carry
        lax.fori_loop(0, TP // R, conv_chunk, 0)
        vecs_ref[32:40, :] = vecs_ref[32:40, :] * (LRU_C * _sig(-lam_ref[...]))

    col = lambda off: pl.BlockSpec((TP, CB), lambda j: (0, off + j))
    vec = pl.BlockSpec((1, CB), lambda j: (0, j))
    wsp = pl.BlockSpec((None, CB, CB), lambda j: (j, 0, 0))
    return pl.pallas_call(
        body, name="lru_bwd",
        grid=(NCB,),
        in_specs=[col(0), col(0), col(NCB), col(0), col(0), pl.BlockSpec((LW, CB), lambda j: (0, j)),
                  wsp, vec, wsp, vec, vec],
        out_specs=[pl.BlockSpec((2, TP, CB), lambda j: (0, 0, j)), wsp, wsp,
                   pl.BlockSpec((8 * LW, CB), lambda j: (0, j)), pl.BlockSpec((8 * NV, CB), lambda j: (0, j))],
        out_shape=[jax.ShapeDtypeStruct((2, TP, DL), BF16),
                   jax.ShapeDtypeStruct((NCB, CB, CB), F32), jax.ShapeDtypeStruct((NCB, CB, CB), F32),
                   jax.ShapeDtypeStruct((8 * LW, DL), F32), jax.ShapeDtypeStruct((8 * NV, DL), F32)],
        scratch_shapes=[pltpu.VMEM((TP, CB), F32), pltpu.VMEM((TP, CB), F32), pltpu.VMEM((TP + 8, CB), F32)],
        compiler_params=_cparams(),
    )(dycat, z, z, xc, hs, conv_w, wa_g, b_a, wx_g, b_x, lam)


def _dz_section(sec, dzl_ref, dzc_ref, dgc_ref, use):
    @pl.when(sec < 2)
    def _():
        use(dzl_ref)

    @pl.when(jnp.logical_and(sec >= 2, sec < 4))
    def _():
        use(dzc_ref)

    @pl.when(sec == 4)
    def _():
        use(dgc_ref)


def _dz_specs(rows, index):
    return [pl.BlockSpec((None, rows, 1024), lambda a, b: (jnp.minimum(index(a, b)[1], 1), index(a, b)[0], 0)),
            pl.BlockSpec((None, rows, 1024), lambda a, b: (jnp.clip(index(a, b)[1] - 2, 0, 1), index(a, b)[0], 0)),
            pl.BlockSpec((rows, 1024), lambda a, b: (index(a, b)[0], 0))]


def _inproj_wgrad(name, hn, dzs):
    KB = 512
    nsec = dzs.shape[0]

    def body(hn_ref, dz_ref, dw_ref):
        dw_ref[...] = lax.dot_general(hn_ref[...], dz_ref[...], _TN, preferred_element_type=F32).astype(BF16)

    return pl.pallas_call(
        body, name=name,
        grid=(nsec, D // KB),
        in_specs=[pl.BlockSpec((TP, KB), lambda n, kb: (0, kb)),
                  pl.BlockSpec((None, TP, 1024), lambda n, kb: (n, 0, 0))],
        out_specs=pl.BlockSpec((KB, 1024), lambda n, kb: (kb, n)),
        out_shape=jax.ShapeDtypeStruct((D, nsec * 1024), BF16),
        compiler_params=_cparams(),
    )(hn, dzs)


def _sum_win_parts(parts_a, parts_b, parts_c):
    RB = 64

    def body(a_ref, b_ref, c_ref, o_ref):
        def chunk(ci, carry):
            r0 = pl.multiple_of(ci * R, R)
            for ref, base, ncol in ((a_ref, 0, 2048), (b_ref, 2048, 2048), (c_ref, 4096, 1024)):
                for c0 in range(0, ncol, 512):
                    acc = ref[0, pl.ds(r0, R), c0:c0 + 512].astype(F32)
                    for sidx in range(1, NDEV):
                        acc = acc + ref[sidx, pl.ds(r0, R), c0:c0 + 512].astype(F32)
                    o_ref[pl.ds(r0, R), base + c0:base + c0 + 512] = acc.astype(BF16)
            return carry
        lax.fori_loop(0, RB // R, chunk, 0)

    spec = lambda ncol: pl.BlockSpec((NDEV, RB, ncol), lambda i: (0, i, 0))
    return pl.pallas_call(
        body, name="sum_win_parts",
        grid=(D // NDEV // RB,),
        in_specs=[spec(2048), spec(2048), spec(1024)],
        out_specs=pl.BlockSpec((RB, NIN), lambda i: (i, 0)),
        out_shape=jax.ShapeDtypeStruct((D // NDEV, NIN), BF16),
        compiler_params=_cparams(),
    )(parts_a, parts_b, parts_c)


def _inproj_bwd(dzl, dzc, dgc, w_in, h, dout, pre_w):
    nsec = NIN // 1024

    def body(dzl_ref, dzc_ref, dgc_ref, w_ref, h_ref, dout_ref, pw_ref, dh_ref, dpw_ref, acc_s):
        i = pl.program_id(0)
        s = pl.program_id(1)

        @pl.when(s == 0)
        def _():
            acc_s[...] = jnp.zeros_like(acc_s)

        def use(dz_ref):
            acc_s[...] += lax.dot_general(dz_ref[...], w_ref[...], _NT, preferred_element_type=F32)
        _dz_section(s, dzl_ref, dzc_ref, dgc_ref, use)

        @pl.when(jnp.logical_and(i == 0, s == nsec - 1))
        def _():
            dpw_ref[...] = jnp.zeros_like(dpw_ref)

        @pl.when(s == nsec - 1)
        def _():
            pw = pw_ref[...]

            def chunk(ci, carry):
                r0 = pl.multiple_of(ci * 8, 8)
                hv = h_ref[pl.ds(r0, 8), :]
                dhn = acc_s[pl.ds(r0, 8), :]
                rs = lax.rsqrt(jnp.mean(hv * hv, axis=-1, keepdims=True) + EPS)
                dpw_ref[...] += dhn * (hv * rs)
                gw = dhn * pw
                dot = jnp.mean(gw * hv, axis=-1, keepdims=True)
                dh_ref[pl.ds(r0, 8), :] = rs * gw - hv * (rs * rs * rs * dot) + dout_ref[pl.ds(r0, 8), :]
                return carry
            lax.fori_loop(0, TM // 8, chunk, 0, unroll=4)

    row = pl.BlockSpec((TM, D), lambda i, s: (i, 0))
    return pl.pallas_call(
        body, name="inproj_bwd",
        grid=(TP // TM, nsec),
        in_specs=_dz_specs(TM, lambda i, s: (i, s)) + [
            pl.BlockSpec((D, 1024), lambda i, s: (0, s)), row, row, pl.BlockSpec((1, D), lambda i, s: (0, 0))],
        out_specs=[row, pl.BlockSpec((8, D), lambda i, s: (0, 0))],
        out_shape=[jax.ShapeDtypeStruct((TP, D), F32), jax.ShapeDtypeStruct((8, D), F32)],
        scratch_shapes=[pltpu.VMEM((TM, D), F32)],
        compiler_params=_cparams(),
    )(dzl, dzc, dgc, w_in, h, dout, pre_w)


def _adamw(name, parts, w, m, v, block_rows):
    rows, cols = w.shape
    nparts = parts.shape[0]
    cw = cols if cols <= 640 else 512

    def body(p_ref, w_ref, m_ref, v_ref, g_ref, d_ref, nm_ref, nv_ref):
        def chunk(ci, carry):
            r0 = pl.multiple_of(ci * R, R)
            for c0 in range(0, cols, cw):
                at = (pl.ds(r0, R), slice(c0, c0 + cw))
                g = p_ref[(0,) + at].astype(F32)
                for sidx in range(1, nparts):
                    g = g + p_ref[(sidx,) + at].astype(F32)
                delta, mv, vv = _adam_math(g, w_ref[at], m_ref[at], v_ref[at])
                g_ref[at] = g
                nm_ref[at] = mv
                nv_ref[at] = vv
                d_ref[at] = delta
            return carry
        lax.fori_loop(0, block_rows // R, chunk, 0)

    blk = pl.BlockSpec((block_rows, cols), lambda i: (i, 0))
    shp = jax.ShapeDtypeStruct((rows, cols), F32)
    return pl.pallas_call(
        body, name=name,
        grid=(rows // block_rows,),
        in_specs=[pl.BlockSpec((nparts, block_rows, cols), lambda i: (0, i, 0)), blk, blk, blk],
        out_specs=[blk, blk, blk, blk],
        out_shape=[shp, shp, shp, shp],
        compiler_params=_cparams(),
    )(parts, w, m, v)


def _adam_math(g, w, m, v):
    c1 = 1.0 / (1.0 - ADAM_B1 ** ADAM_STEP)
    c2 = 1.0 / (1.0 - ADAM_B2 ** ADAM_STEP)
    mv = ADAM_B1 * m + (1.0 - ADAM_B1) * g
    vv = ADAM_B2 * v + (1.0 - ADAM_B2) * (g * g)
    upd = (mv * c1) / (jnp.sqrt(vv * c2) + ADAM_EPS) + ADAM_WD * w
    return -ADAM_LR * upd, mv, vv


_VEC = [("pre_norm_w", 2), ("post_norm_w", 2), ("b_in", 5), ("lru_conv_b", 1), ("b_gate_a", 1), ("b_gate_x", 1),
        ("lru_lambda", 1), ("conf_dw_b", 1), ("conf_ln_w", 1), ("conf_ln_b", 1), ("conf_pw_b", 1)]
_VEC_ROWS = 24
_LOSS_ROW = 17
_SM_ROWS = 64


def _pack_grads(dprew_acc, dpostw_acc, cvecs, kvecs, lvecs, dcw_acc, ddw_acc, dh, loss_acc):
    def body(pre_ref, post_ref, c_ref, k_ref, l_ref, dcw_ref, ddw_ref, dh_ref, loss_ref, vec_ref, small_ref, tmp):
        s8 = lambda ref, r: jnp.sum(ref[8 * r:8 * r + 8, :], axis=0, keepdims=True)
        vec_ref[...] = jnp.zeros_like(vec_ref)
        pre, post = s8(pre_ref, 0), s8(post_ref, 0)
        rows = [pre[:, 0:1024], pre[:, 1024:2048], post[:, 0:1024], post[:, 1024:2048],
                s8(l_ref, 1), s8(l_ref, 0), s8(k_ref, 1), s8(k_ref, 2), s8(c_ref, 1),
                s8(l_ref, 5), s8(l_ref, 2), s8(l_ref, 3), s8(l_ref, 4),
                s8(k_ref, 0), s8(c_ref, 2), s8(c_ref, 3), s8(c_ref, 0)]
        for r, val in enumerate(rows):
            vec_ref[r:r + 1, :] = val
        vec_ref[_LOSS_ROW:_LOSS_ROW + 1, :] = jnp.zeros((1, 1024), F32) + (0.5 / D) * jnp.sum(loss_ref[...])

        small_ref[...] = jnp.zeros_like(small_ref)
        for k in range(LW):
            tmp[k:k + 1, :] = s8(dcw_ref, k)
        for k in range(KW):
            tmp[8 + k:9 + k, :] = s8(ddw_ref, k)
        for d in range(NDEV):
            small_ref[d, 0:LW, 0:128] = tmp[0:LW, 128 * d:128 * d + 128]
            small_ref[d, 8:8 + KW, 0:128] = tmp[8:8 + KW, 128 * d:128 * d + 128]
            small_ref[d, 40:56, :] = dh_ref[:, 256 * d:256 * d + 256]

    full = lambda a: pl.BlockSpec(a.shape, lambda i: (0,) * a.ndim)
    ins = [dprew_acc, dpostw_acc, cvecs, kvecs, lvecs, dcw_acc, ddw_acc]
    return pl.pallas_call(
        body, name="pack_grads",
        grid=(1,),
        in_specs=[full(a) for a in ins] + [pl.BlockSpec((NMETA, D), lambda i: (0, 0)), full(loss_acc)],
        out_specs=[pl.BlockSpec((_VEC_ROWS, 1024), lambda i: (0, 0)),
                   pl.BlockSpec((NDEV, _SM_ROWS, 256), lambda i: (0, 0, 0))],
        out_shape=[jax.ShapeDtypeStruct((_VEC_ROWS, 1024), F32), jax.ShapeDtypeStruct((NDEV, _SM_ROWS, 256), F32)],
        scratch_shapes=[pltpu.VMEM((40, 1024), F32)],
        compiler_params=_cparams(),
    )(*ins, dh, loss_acc)


def _adamw_vec(parts, W, M, V):
    nv = len(_VEC)

    def body(*refs):
        p_ref = refs[0]
        w_refs, m_refs, v_refs = refs[1:1 + nv], refs[1 + nv:1 + 2 * nv], refs[1 + 2 * nv:1 + 3 * nv]
        outs = refs[1 + 3 * nv:]

        def total(r):
            acc = p_ref[0, r:r + 1, :]
            for sidx in range(1, NDEV):
                acc = acc + p_ref[sidx, r:r + 1, :]
            return acc

        row = 0
        for idx, (_, nrows) in enumerate(_VEC):
            for part in range(nrows):
                cols = slice(1024 * part, 1024 * part + 1024)
                g = total(row + part)
                delta, mv, vv = _adam_math(g, w_refs[idx][:, cols], m_refs[idx][:, cols], v_refs[idx][:, cols])
                for o, val in zip(outs[4 * idx:4 * idx + 4], (g, delta, mv, vv)):
                    o[:, cols] = val
            row += nrows
        outs[-1][...] = total(_LOSS_ROW)[:, 0:128]

    names = [n for n, _ in _VEC]
    flat = lambda d: [d[n].reshape(1, -1) for n in names]
    ws, ms, vs = flat(W), flat(M), flat(V)
    res = pl.pallas_call(
        body, name="adamw_vec",
        out_shape=[jax.ShapeDtypeStruct(w.shape, F32) for w in ws for _ in range(4)]
        + [jax.ShapeDtypeStruct((1, 128), F32)],
        compiler_params=_cparams(),
    )(parts, *ws, *ms, *vs)
    return {n: tuple(res[4 * i:4 * i + 4]) for i, n in enumerate(names)}, res[-1]


def _adamw_small(parts, W, M, V):
    where = {"lru_conv_w": (slice(0, LW), slice(0, 128)), "conf_dw_w": (slice(8, 8 + KW), slice(0, 128)),
             "meta_tokens": (slice(40, 56), slice(0, 256))}
    names = list(where)

    def body(*refs):
        p_ref = refs[0]
        outs = refs[10:]
        for idx, n in enumerate(names):
            rs, cs = where[n]
            g = p_ref[0, rs, cs]
            for sidx in range(1, NDEV):
                g = g + p_ref[sidx, rs, cs]
            delta, mv, vv = _adam_math(g, refs[1 + idx][...], refs[4 + idx][...], refs[7 + idx][...])
            for o, val in zip(outs[4 * idx:4 * idx + 4], (g, delta, mv, vv)):
                o[...] = val

    two_d = lambda a: a.reshape(a.shape[-2:])
    ws, ms, vs = ([two_d(d[n]) for n in names] for d in (W, M, V))
    res = pl.pallas_call(
        body, name="adamw_small",
        out_shape=[jax.ShapeDtypeStruct(w.shape, F32) for w in ws for _ in range(4)],
        compiler_params=_cparams(),
    )(parts, *ws, *ms, *vs)
    return {n: tuple(res[4 * i:4 * i + 4]) for i, n in enumerate(names)}


def _pack_small(lru_cw, dw_w, meta):
    buf = jnp.zeros((_SM_ROWS, 256), F32)
    buf = buf.at[0:LW, 0:128].set(lru_cw)
    buf = buf.at[8:8 + dw_w.shape[0], 0:128].set(dw_w)
    return buf.at[40:56, :].set(meta)


def _block_diag4(w):
    w4 = w.reshape(NCB, 4, 64, 64)
    eye = jnp.eye(4, dtype=w.dtype)
    return jnp.einsum("ghij,hk->ghikj", w4, eye).reshape(NCB, CB, CB)


def _diag_blocks(g):
    g5 = g.reshape(NCB, 4, 64, 4, 64)
    return jnp.stack([g5[:, hh, :, hh, :] for hh in range(4)], axis=1).reshape(16, 64, 64)


def _local_step(x, target, meta_full, win_full, out_weights, lru_cw_full, dw_w_full, W, send):
    h = jnp.concatenate([meta_full, x, jnp.zeros((TP - T, D), F32)], axis=0)
    tgt = jnp.concatenate([jnp.zeros((NMETA, D), F32), target, jnp.zeros((TP - T, D), F32)], axis=0)
    wa_g = _block_diag4(W["w_gate_a"][0]).astype(BF16)
    wx_g = _block_diag4(W["w_gate_x"][0]).astype(BF16)

    z, hn = _prenorm_inproj(h, W["pre_norm_w"], win_full, W["b_in"])
    ylru, xc, hs = _lru_fwd(z, lru_cw_full, W["lru_conv_b"], wa_g, W["b_gate_a"], wx_g, W["b_gate_x"],
                            W["lru_lambda"])
    vc = _conf_fwd_conv(z, dw_w_full, W["conf_dw_b"])
    wout_full, pw_full = out_weights(vc)
    yconf, p = _conf_fwd_proj(vc, z, W["conf_ln_w"], W["conf_ln_b"], pw_full, W["conf_pw_b"])
    dout, dy, loss_acc, dpostw_acc = _outproj_loss(ylru, yconf, wout_full, h, tgt, W["post_norm_w"])

    dycat, dwout_part = _outproj_bwd(dy, ylru, yconf, wout_full)
    tok = send("w_out", dwout_part)
    dvc, dgc, dpw_part, cvecs = _conf_bwd_proj(dycat, p, z, vc, W["conf_ln_w"] + tok, W["conf_ln_b"], pw_full)
    tok = send("conf_pw_w", dpw_part)
    tok = tok + send("w_in_c", _inproj_wgrad("inproj_wgrad_c", hn, dgc[None]))
    dzc, ddw_acc, kvecs = _conf_bwd_conv(dvc, z, dw_w_full + tok)
    tok = send("w_in_b", _inproj_wgrad("inproj_wgrad_b", hn, dzc))
    dzl, dwa_g, dwx_g, dcw_acc, lvecs = _lru_bwd(dycat, z, xc, hs, lru_cw_full, wa_g, W["b_gate_a"] + tok, wx_g,
                                                 W["b_gate_x"], W["lru_lambda"])
    tok = send("w_in_a", _inproj_wgrad("inproj_wgrad_a", hn, dzl))
    dh, dprew_acc = _inproj_bwd(dzl, dzc, dgc, win_full, h, dout, W["pre_norm_w"] + tok)

    vec_pack, small_part = _pack_grads(dprew_acc, dpostw_acc, cvecs, kvecs, lvecs, dcw_acc, ddw_acc, dh, loss_acc)
    return dh, vec_pack, small_part, _diag_blocks(dwa_g), _diag_blocks(dwx_g)


def kernel(x, meta_tokens, pre_norm_w, post_norm_w, w_in, b_in, lru_conv_w, lru_conv_b, w_gate_a, b_gate_a, w_gate_x, b_gate_x, lru_lambda, conf_dw_w, conf_dw_b, conf_ln_w, conf_ln_b, conf_pw_w, conf_pw_b, w_out, loss_target, m_meta_tokens, m_pre_norm_w, m_post_norm_w, m_w_in, m_b_in, m_lru_conv_w, m_lru_conv_b, m_w_gate_a, m_b_gate_a, m_w_gate_x, m_b_gate_x, m_lru_lambda, m_conf_dw_w, m_conf_dw_b, m_conf_ln_w, m_conf_ln_b, m_conf_pw_w, m_conf_pw_b, m_w_out, v_meta_tokens, v_pre_norm_w, v_post_norm_w, v_w_in, v_b_in, v_lru_conv_w, v_lru_conv_b, v_w_gate_a, v_b_gate_a, v_w_gate_x, v_b_gate_x, v_lru_lambda, v_conf_dw_w, v_conf_dw_b, v_conf_ln_w, v_conf_ln_b, v_conf_pw_w, v_conf_pw_b, v_w_out):
    W = dict(meta_tokens=meta_tokens, pre_norm_w=pre_norm_w, post_norm_w=post_norm_w, w_in=w_in, b_in=b_in,
             lru_conv_w=lru_conv_w, lru_conv_b=lru_conv_b, w_gate_a=w_gate_a, b_gate_a=b_gate_a,
             w_gate_x=w_gate_x, b_gate_x=b_gate_x, lru_lambda=lru_lambda, conf_dw_w=conf_dw_w,
             conf_dw_b=conf_dw_b, conf_ln_w=conf_ln_w, conf_ln_b=conf_ln_b, conf_pw_w=conf_pw_w,
             conf_pw_b=conf_pw_b, w_out=w_out)
    M = dict(meta_tokens=m_meta_tokens, pre_norm_w=m_pre_norm_w, post_norm_w=m_post_norm_w, w_in=m_w_in,
             b_in=m_b_in, lru_conv_w=m_lru_conv_w, lru_conv_b=m_lru_conv_b, w_gate_a=m_w_gate_a,
             b_gate_a=m_b_gate_a, w_gate_x=m_w_gate_x, b_gate_x=m_b_gate_x, lru_lambda=m_lru_lambda,
             conf_dw_w=m_conf_dw_w, conf_dw_b=m_conf_dw_b, conf_ln_w=m_conf_ln_w, conf_ln_b=m_conf_ln_b,
             conf_pw_w=m_conf_pw_w, conf_pw_b=m_conf_pw_b, w_out=m_w_out)
    V = dict(meta_tokens=v_meta_tokens, pre_norm_w=v_pre_norm_w, post_norm_w=v_post_norm_w, w_in=v_w_in,
             b_in=v_b_in, lru_conv_w=v_lru_conv_w, lru_conv_b=v_lru_conv_b, w_gate_a=v_w_gate_a,
             b_gate_a=v_b_gate_a, w_gate_x=v_w_gate_x, b_gate_x=v_b_gate_x, lru_lambda=v_lru_lambda,
             conf_dw_w=v_conf_dw_w, conf_dw_b=v_conf_dw_b, conf_ln_w=v_conf_ln_w, conf_ln_b=v_conf_ln_b,
             conf_pw_w=v_conf_pw_w, conf_pw_b=v_conf_pw_b, w_out=v_w_out)
    names = list(W.keys())
    shapes = {n: W[n].shape for n in names}

    small = _pack_small(lru_conv_w[0], conf_dw_w[0], meta_tokens)
    win_flight, tok = _win_gather_start(w_in[0].astype(BF16))
    gathered, tok = _exchange_start("gather_start", [
        (small + tok[0, 0], jax.ShapeDtypeStruct((NDEV, _SM_ROWS, 256), F32), _whole, _slot),
        (w_out[0].astype(BF16), jax.ShapeDtypeStruct((D, D), BF16), _whole, _rows(D // NDEV)),
        (conf_pw_w[0].astype(BF16), jax.ShapeDtypeStruct((DC, DC), BF16), _whole, _rows(DC // NDEV)),
    ])
    win_flight = _win_gather_forward(win_flight, tok)
    (small_all,) = _exchange_wait("gather_wait_small", gathered[0:1], x)
    win_full = _win_gather_wait(win_flight)
    unshard = lambda a: jnp.transpose(a, (1, 0, 2)).reshape(a.shape[1], -1)
    lru_cw_full = unshard(small_all[:, 0:LW, 0:128])
    dw_w_full = unshard(small_all[:, 8:8 + KWP, 0:128])
    meta_full = unshard(small_all[:, 40:56, :])

    def out_weights(after):
        return _exchange_wait("gather_wait_out", gathered[1:3], after)

    row_stage = lambda ncol: (jax.ShapeDtypeStruct((NDEV, D // NDEV, ncol), BF16), _rows(D // NDEV))
    piece = {"w_in_a": row_stage(2048), "w_in_b": row_stage(2048), "w_in_c": row_stage(1024),
             "w_out": row_stage(D),
             "conf_pw_w": (jax.ShapeDtypeStruct((NDEV, DC // NDEV, DC), BF16), _rows(DC // NDEV))}
    sent = {}

    def send(name, part):
        handles, token = _exchange_start("scatter_" + name + "_start",
                                         [(part.astype(BF16), piece[name][0], piece[name][1], _slot)])
        sent[name] = handles
        return token[0, 0]

    dh, vec_pack, small_part, dwa, dwx = _local_step(
        x[0], loss_target[0], meta_full, win_full, out_weights, lru_cw_full, dw_w_full, W, send)
    grad_x = dh[NMETA:T][None]
    gate = jax.ShapeDtypeStruct((NDEV, 16 * 64, 64), BF16)
    rest, _ = _exchange_start("scatter_rest_start", [
        (small_part, jax.ShapeDtypeStruct((NDEV, _SM_ROWS, 256), F32), _slot, _slot),
        (vec_pack, jax.ShapeDtypeStruct((NDEV, _VEC_ROWS, 1024), F32), _whole, _slot),
        (dwa.reshape(16 * 64, 64).astype(BF16), gate, _whole, _slot),
        (dwx.reshape(16 * 64, 64).astype(BF16), gate, _whole, _slot),
    ])

    G, DW, NM, NV = {}, {}, {}, {}
    (wout_parts,) = _exchange_wait("scatter_w_out_wait", sent["w_out"], dh)
    G["w_out"], DW["w_out"], NM["w_out"], NV["w_out"] = _adamw("adamw_w_out", wout_parts, w_out[0], m_w_out[0], v_w_out[0], 64)
    (pw_parts,) = _exchange_wait("scatter_conf_pw_w_wait", sent["conf_pw_w"], G["w_out"])
    G["conf_pw_w"], DW["conf_pw_w"], NM["conf_pw_w"], NV["conf_pw_w"] = _adamw(
        "adamw_pw", pw_parts, conf_pw_w[0], m_conf_pw_w[0], v_conf_pw_w[0], 128)
    (parts_c,) = _exchange_wait("scatter_w_in_c_wait", sent["w_in_c"], G["conf_pw_w"])
    (parts_b,) = _exchange_wait("scatter_w_in_b_wait", sent["w_in_b"], parts_c)
    (parts_a,) = _exchange_wait("scatter_w_in_a_wait", sent["w_in_a"], parts_b)
    win_rows = _sum_win_parts(parts_a, parts_b, parts_c)
    win_stage2, _ = _exchange_start("scatter_w_in_stage2_start", [
        (win_rows, jax.ShapeDtypeStruct((NDEV, D // NDEV, NIN // NDEV), BF16), _cols(NIN // NDEV), _slot)])
    small_parts, vec_parts, wa_parts, wx_parts = _exchange_wait("scatter_rest_wait", rest, win_rows)
    res = dict(_adamw_small(small_parts, W, M, V))
    vec_res, loss_row = _adamw_vec(vec_parts, W, M, V)
    res.update(vec_res)
    for n, parts in (("w_gate_a", wa_parts), ("w_gate_x", wx_parts)):
        res[n] = _adamw("adamw_" + n, parts, *[d[n].reshape(16 * 64, 64) for d in (W, M, V)], 16 * 64)
    (win_sum,) = _exchange_wait("scatter_w_in_stage2_wait", win_stage2, res["w_gate_x"][0])
    res["w_in"] = _adamw("adamw_w_in", win_sum.reshape(1, D, NIN // NDEV), w_in[0], m_w_in[0], v_w_in[0], 256)
    for n, vals in res.items():
        for dst, val in zip((G, DW, NM, NV), vals):
            dst[n] = val
    for dst in (G, DW, NM, NV):
        for n in names:
            dst[n] = dst[n].reshape(shapes[n])
    loss = loss_row[0, 0]

    return (loss, grad_x, *[G[n] for n in names], *[DW[n] for n in names],
            *[NM[n] for n in names], *[NV[n] for n in names])
```

```python
import functools

import jax
import jax.numpy as jnp
from jax import lax
from jax.experimental import pallas as pl
from jax.experimental.pallas import tpu as pltpu

F32 = jnp.float32
BF16 = jnp.bfloat16

D = 2048
DL = 1024
DC = 1024
NIN = 5120
NMETA = 16
SEQ = 2048
T = NMETA + SEQ
TP = 2176
TM = 544
CB = 256
NCB = DL // CB
R = 16
KW = 31
KWP = 32
LW = 4
LRU_C = 8.0
EPS = 1e-6
NDEV = 8

ADAM_LR = 0.001
ADAM_B1 = 0.9
ADAM_B2 = 0.999
ADAM_EPS = 1e-08
ADAM_WD = 0.01
ADAM_STEP = 10

VMEM_LIMIT = 56 * 1024 * 1024


def _cparams():
    return pltpu.CompilerParams(vmem_limit_bytes=VMEM_LIMIT)


def _sig(x):
    return 1.0 / (1.0 + jnp.exp(-x))


def _expm1_neg(y):
    poly = y * (1.0 + y * (0.5 + y * (1.0 / 6.0 + y * (1.0 / 24.0 + y * (1.0 / 120.0)))))
    return jnp.where(y > -0.1, poly, jnp.exp(y) - 1.0)


def _softplus(x):
    e = jnp.exp(-jnp.abs(x))
    w = 1.0 + e
    l1p = jnp.where(w == 1.0, e, jnp.log(w) * e / (w - 1.0))
    return jnp.maximum(x, 0.0) + l1p


def _row_iota(shape):
    return lax.broadcasted_iota(jnp.int32, shape, 0)


def _fold8(v):
    return v[0:8, :] + v[8:16, :]


_FLIPS = [(k >> 2 & 1, k >> 1 & 1, k & 1) for k in range(1, NDEV)]
_HBM = pl.BlockSpec(memory_space=pltpu.HBM)
_SEM = pl.BlockSpec(memory_space=pltpu.SEMAPHORE)


def _peers():
    x, y, c = lax.axis_index("x"), lax.axis_index("y"), lax.axis_index("c")
    out = []
    for dx, dy, dc in _FLIPS:
        px = 1 - x if dx else x
        py = 1 - y if dy else y
        pc = 1 - c if dc else c
        out.append(((px, py, pc), 4 * px + 2 * py + pc))
    return 4 * x + 2 * y + c, out


def _exchange_start(name, items):
    n = len(items)

    def body(*refs):
        srcs, lands = refs[:n], refs[n:2 * n]
        outs = refs[2 * n:]
        send_sems, recv_sems, local_sems = outs[:n], outs[n:2 * n], outs[2 * n:3 * n]
        token = outs[-1]
        me, peers = _peers()
        for a in range(n):
            src_at, dst_at = items[a][2], items[a][3]
            pltpu.make_async_copy(src_at(srcs[a], me), dst_at(lands[a], me), local_sems[a]).start()
        for a in range(n):
            src_at, dst_at = items[a][2], items[a][3]
            for k, (pos, peer) in enumerate(peers):
                pltpu.make_async_remote_copy(
                    src_ref=src_at(srcs[a], peer), dst_ref=dst_at(lands[a], me),
                    send_sem=send_sems[a].at[k], recv_sem=recv_sems[a].at[k],
                    device_id=pos, device_id_type=pl.DeviceIdType.MESH).start()
        token[...] = jnp.zeros_like(token)

    srcs = [pltpu.with_memory_space_constraint(it[0], pltpu.HBM) for it in items]
    lands = [pltpu.with_memory_space_constraint(lax.empty(it[1].shape, it[1].dtype), pltpu.HBM) for it in items]
    sem7 = pltpu.SemaphoreType.DMA((NDEV - 1,))
    res = pl.pallas_call(
        body, name=name,
        out_shape=([sem7] * (2 * n) + [pltpu.SemaphoreType.DMA(())] * n
                   + [pltpu.HBM(a.shape, a.dtype) for a in srcs] + [pltpu.HBM(a.shape, a.dtype) for a in lands]
                   + [jax.ShapeDtypeStruct((8, 128), F32)]),
        in_specs=[_HBM] * (2 * n),
        out_specs=[_SEM] * (3 * n) + [_HBM] * (2 * n) + [pl.BlockSpec(memory_space=pltpu.VMEM)],
        input_output_aliases={i: 3 * n + i for i in range(2 * n)},
        compiler_params=pltpu.CompilerParams(has_side_effects=pltpu.SideEffectType.DATAFLOW_SIDE_EFFECTING),
    )(*srcs, *lands)
    handles = [dict(send=res[a], recv=res[n + a], local=res[2 * n + a], src=res[3 * n + a], land=res[4 * n + a],
                    src_at=items[a][2], dst_at=items[a][3]) for a in range(n)]
    return handles, res[-1]


def _exchange_wait(name, handles, after):
    n = len(handles)

    def body(*refs):
        srcs, lands = refs[:n], refs[n:2 * n]
        send_sems, recv_sems, local_sems = refs[2 * n:3 * n], refs[3 * n:4 * n], refs[4 * n:5 * n]
        me, peers = _peers()
        for a in range(n):
            src_at, dst_at = handles[a]["src_at"], handles[a]["dst_at"]
            for k, (pos, peer) in enumerate(peers):
                cp = pltpu.make_async_remote_copy(
                    src_ref=src_at(srcs[a], peer), dst_ref=dst_at(lands[a], peer),
                    send_sem=send_sems[a].at[k], recv_sem=recv_sems[a].at[k],
                    device_id=pos, device_id_type=pl.DeviceIdType.MESH)
                cp.wait_send()
                cp.wait_recv()
            pltpu.make_async_copy(src_at(srcs[a], me), dst_at(lands[a], me), local_sems[a]).wait()

    srcs = [hd["src"] for hd in handles]
    lands = [hd["land"] for hd in handles]
    res = pl.pallas_call(
        body, name=name,
        out_shape=[pltpu.HBM(a.shape, a.dtype) for a in srcs] + [pltpu.HBM(a.shape, a.dtype) for a in lands],
        in_specs=[_HBM] * (2 * n) + [_SEM] * (3 * n) + [pl.BlockSpec(memory_space=pl.ANY)],
        out_specs=[_HBM] * (2 * n),
        input_output_aliases={i: i for i in range(2 * n)},
        compiler_params=pltpu.CompilerParams(has_side_effects=pltpu.SideEffectType.DATAFLOW_SIDE_EFFECTING),
    )(*srcs, *lands, *[hd["send"] for hd in handles], *[hd["recv"] for hd in handles],
      *[hd["local"] for hd in handles], after)
    return list(res[n:])


_SIDE = pltpu.SideEffectType.DATAFLOW_SIDE_EFFECTING
_WCOLS = NIN // NDEV


def _win_cols(ref, l):
    return ref.at[:, pl.ds(pl.multiple_of(l * _WCOLS, 128), _WCOLS)]


def _win_routes():
    x, y, c = lax.axis_index("x"), lax.axis_index("y"), lax.axis_index("c")
    pos = [(x, y, 1 - c), (1 - x, y, c), (x, 1 - y, c), (1 - x, 1 - y, c)]
    return 4 * x + 2 * y + c, [(p, 4 * p[0] + 2 * p[1] + p[2]) for p in pos]


def _win_gather_start(shard):
    def body(src, land, send_sems, recv_sems, local_sem, src_thru, land_thru, token):
        me, routes = _win_routes()
        pltpu.make_async_copy(src, _win_cols(land, me), local_sem).start()
        for k, (pos, _) in enumerate(routes):
            pltpu.make_async_remote_copy(src_ref=src, dst_ref=_win_cols(land, me), send_sem=send_sems.at[k],
                                         recv_sem=recv_sems.at[k], device_id=pos,
                                         device_id_type=pl.DeviceIdType.MESH).start()
        token[...] = jnp.zeros_like(token)

    src = pltpu.with_memory_space_constraint(shard, pltpu.HBM)
    land = pltpu.with_memory_space_constraint(lax.empty((D, NIN), BF16), pltpu.HBM)
    sem4 = pltpu.SemaphoreType.DMA((4,))
    res = pl.pallas_call(
        body, name="win_gather_start",
        out_shape=[sem4, sem4, pltpu.SemaphoreType.DMA(()), pltpu.HBM(src.shape, BF16), pltpu.HBM(land.shape, BF16),
                   jax.ShapeDtypeStruct((8, 128), F32)],
        in_specs=[_HBM, _HBM],
        out_specs=[_SEM, _SEM, _SEM, _HBM, _HBM, pl.BlockSpec(memory_space=pltpu.VMEM)],
        input_output_aliases={0: 3, 1: 4},
        compiler_params=pltpu.CompilerParams(has_side_effects=_SIDE),
    )(src, land)
    return dict(send=res[0], recv=res[1], local=res[2], src=res[3], land=res[4]), res[5]


def _win_gather_forward(hd, after):
    def body(land, recv_sems, after_ref, land_thru, fsend_sems, frecv_sems):
        me, routes = _win_routes()
        sibling = routes[0][0]
        for k in (1, 2, 3):
            pos, peer = routes[k]
            piece = _win_cols(land, peer)
            pltpu.make_async_remote_copy(src_ref=piece, dst_ref=piece, send_sem=fsend_sems.at[k - 1],
                                         recv_sem=recv_sems.at[k], device_id=pos,
                                         device_id_type=pl.DeviceIdType.MESH).wait_recv()
            pltpu.make_async_remote_copy(src_ref=piece, dst_ref=piece, send_sem=fsend_sems.at[k - 1],
                                         recv_sem=frecv_sems.at[k - 1], device_id=sibling,
                                         device_id_type=pl.DeviceIdType.MESH).start()

    sem3 = pltpu.SemaphoreType.DMA((3,))
    res = pl.pallas_call(
        body, name="win_gather_forward",
        out_shape=[pltpu.HBM(hd["land"].shape, BF16), sem3, sem3],
        in_specs=[_HBM, _SEM, pl.BlockSpec(memory_space=pl.ANY)],
        out_specs=[_HBM, _SEM, _SEM],
        input_output_aliases={0: 0},
        compiler_params=pltpu.CompilerParams(has_side_effects=_SIDE),
    )(hd["land"], hd["recv"], after)
    return dict(hd, land=res[0], fsend=res[1], frecv=res[2])


def _win_gather_wait(hd):
    def body(src, land, send_sems, recv_sems, local_sem, fsend_sems, frecv_sems, src_thru, land_thru):
        me, routes = _win_routes()
        sib_pos, sibling = routes[0]
        for k, (pos, peer) in enumerate(routes):
            cp = pltpu.make_async_remote_copy(src_ref=src, dst_ref=_win_cols(land, peer), send_sem=send_sems.at[k],
                                              recv_sem=recv_sems.at[k], device_id=pos,
                                              device_id_type=pl.DeviceIdType.MESH)
            cp.wait_send()
            if k == 0:
                cp.wait_recv()
        pltpu.make_async_copy(src, _win_cols(land, me), local_sem).wait()
        for k in (1, 2, 3):
            mine = _win_cols(land, routes[k][1])
            theirs = _win_cols(land, 4 * routes[k][0][0] + 2 * routes[k][0][1] + sib_pos[2])
            cp = pltpu.make_async_remote_copy(src_ref=mine, dst_ref=theirs, send_sem=fsend_sems.at[k - 1],
                                              recv_sem=frecv_sems.at[k - 1], device_id=sib_pos,
                                              device_id_type=pl.DeviceIdType.MESH)
            cp.wait_send()
            cp.wait_recv()

    res = pl.pallas_call(
        body, name="win_gather_wait",
        out_shape=[pltpu.HBM(hd["src"].shape, BF16), pltpu.HBM(hd["land"].shape, BF16)],
        in_specs=[_HBM, _HBM] + [_SEM] * 5,
        out_specs=[_HBM, _HBM],
        input_output_aliases={0: 0, 1: 1},
        compiler_params=pltpu.CompilerParams(has_side_effects=_SIDE),
    )(hd["src"], hd["land"], hd["send"], hd["recv"], hd["local"], hd["fsend"], hd["frecv"])
    return res[1]


def _whole(ref, l):
    return ref


def _slot(ref, l):
    return ref.at[l]


def _cols(width):
    def at(ref, l):
        return ref.at[:, pl.ds(pl.multiple_of(l * width, 128), width)]
    return at


def _rows(height):
    def at(ref, l):
        return ref.at[pl.ds(pl.multiple_of(l * height, 8), height), :]
    return at


def _prenorm_inproj(h, pre_w, w_in, b_in):
    nsec = NIN // 1024

    def body(h_ref, pw_ref, w_ref, b_ref, z_ref, hn_ref):
        @pl.when(pl.program_id(1) == 0)
        def _():
            pw = pw_ref[...]

            def chunk(ci, carry):
                r0 = pl.multiple_of(ci * R, R)
                xv = h_ref[pl.ds(r0, R), :]
                ms = jnp.mean(xv * xv, axis=-1, keepdims=True)
                hn_ref[pl.ds(r0, R), :] = (xv * lax.rsqrt(ms + EPS) * pw).astype(BF16)
                return carry
            lax.fori_loop(0, TM // R, chunk, 0, unroll=2)

        z_ref[...] = jnp.dot(hn_ref[...], w_ref[...], preferred_element_type=F32) + b_ref[...]

    return pl.pallas_call(
        body, name="prenorm_inproj",
        grid=(TP // TM, nsec),
        in_specs=[pl.BlockSpec((TM, D), lambda i, n: (i, 0)),
                  pl.BlockSpec((1, D), lambda i, n: (0, 0)),
                  pl.BlockSpec((D, 1024), lambda i, n: (0, n)),
                  pl.BlockSpec((1, 1024), lambda i, n: (0, n))],
        out_specs=[pl.BlockSpec((TM, 1024), lambda i, n: (i, n)),
                   pl.BlockSpec((TM, D), lambda i, n: (i, 0))],
        out_shape=[jax.ShapeDtypeStruct((TP, NIN), F32), jax.ShapeDtypeStruct((TP, D), BF16)],
        compiler_params=_cparams(),
    )(h, pre_w, w_in, b_in)


def _gate_values(ga, gx, xc, sp8):
    r = _sig(ga)
    i = _sig(gx)
    log_a = -(r * sp8)
    a = jnp.exp(log_a)
    mult = jnp.sqrt(-_expm1_neg(2.0 * log_a))
    return r, i, a, mult


def _lru_fwd(z, conv_w, conv_b, wa_g, b_a, wx_g, b_x, lam):
    def body(x_ref, g_ref, cw_ref, cb_ref, wa_ref, ba_ref, wx_ref, bx_ref, lam_ref,
             y_ref, xc_ref, hs_ref, ga_s, gx_s):
        taps = [cw_ref[k:k + 1, :] for k in range(LW)]
        cb = cb_ref[...]

        def conv_chunk(ci, carry):
            r0 = pl.multiple_of(ci * R, R)
            cur = x_ref[pl.ds(r0, R), :]
            p0 = pl.multiple_of(jnp.maximum(r0 - 8, 0), 8)
            prev = jnp.where(ci > 0, x_ref[pl.ds(p0, 8), :], 0.0)
            buf = jnp.concatenate([prev, cur], axis=0)
            acc = cur * taps[LW - 1] + cb
            for s in range(1, LW):
                acc = acc + pltpu.roll(buf, s, 0)[8:8 + R, :] * taps[LW - 1 - s]
            xc_ref[pl.ds(r0, R), :] = acc
            return carry
        lax.fori_loop(0, TP // R, conv_chunk, 0)

        def gate_chunk(ci, carry):
            r0 = pl.multiple_of(ci * TM, TM)
            xb = xc_ref[pl.ds(r0, TM), :].astype(BF16)
            ga_s[pl.ds(r0, TM), :] = jnp.dot(xb, wa_ref[...], preferred_element_type=F32) + ba_ref[...]
            gx_s[pl.ds(r0, TM), :] = jnp.dot(xb, wx_ref[...], preferred_element_type=F32) + bx_ref[...]
            return carry
        lax.fori_loop(0, TP // TM, gate_chunk, 0)

        sp8 = LRU_C * _softplus(-lam_ref[...])
        row = _row_iota((R, CB))

        def scan_chunk(ci, hprev):
            r0 = pl.multiple_of(ci * R, R)
            xc = xc_ref[pl.ds(r0, R), :]
            _, i, a, mult = _gate_values(ga_s[pl.ds(r0, R), :], gx_s[pl.ds(r0, R), :], xc, sp8)
            u = mult * (i * xc)
            k = 1
            while k < R:
                m = row >= k
                u = jnp.where(m, a * pltpu.roll(u, k, 0) + u, u)
                a = jnp.where(m, a * pltpu.roll(a, k, 0), a)
                k *= 2
            hv = u + a * hprev
            hs_ref[pl.ds(r0, R), :] = hv
            g = g_ref[pl.ds(r0, R), :]
            y_ref[pl.ds(r0, R), :] = (hv * (g * _sig(g))).astype(BF16)
            return jnp.sum(jnp.where(row == R - 1, hv, 0.0), axis=0, keepdims=True)
        lax.fori_loop(0, TP // R, scan_chunk, jnp.zeros((1, CB), F32))

    col = lambda off: pl.BlockSpec((TP, CB), lambda j: (0, off + j))
    vec = pl.BlockSpec((1, CB), lambda j: (0, j))
    wsp = pl.BlockSpec((None, CB, CB), lambda j: (j, 0, 0))
    return pl.pallas_call(
        body, name="lru_fwd",
        grid=(NCB,),
        in_specs=[col(0), col(NCB), pl.BlockSpec((LW, CB), lambda j: (0, j)), vec, wsp, vec, wsp, vec, vec],
        out_specs=[col(0), col(0), col(0)],
        out_shape=[jax.ShapeDtypeStruct((TP, DL), BF16), jax.ShapeDtypeStruct((TP, DL), F32),
                   jax.ShapeDtypeStruct((TP, DL), F32)],
        scratch_shapes=[pltpu.VMEM((TP, CB), F32), pltpu.VMEM((TP, CB), F32)],
        compiler_params=_cparams(),
    )(z, z, conv_w, conv_b, wa_g, b_a, wx_g, b_x, lam)


def _conf_fwd_conv(z, dw_w, dw_b):
    def body(u1_ref, u2_ref, w_ref, b_ref, vc_ref, vs):
        vs[pl.ds(0, KWP), :] = jnp.zeros((KWP, CB), F32)

        def glu_chunk(ci, carry):
            r0 = pl.multiple_of(ci * R, R)
            vs[pl.ds(KWP + r0, R), :] = u1_ref[pl.ds(r0, R), :] * _sig(u2_ref[pl.ds(r0, R), :])
            return carry
        lax.fori_loop(0, TP // R, glu_chunk, 0)

        bias = b_ref[...]

        def conv_chunk(ci, carry):
            r0 = pl.multiple_of(ci * R, R)
            buf = vs[pl.ds(r0, KWP + R), :]
            acc = jnp.zeros((R, CB), F32) + bias
            for rr in range(8):
                rolled = buf if rr == 0 else pltpu.roll(buf, rr, 0)
                for q in range(4):
                    s = 8 * q + rr
                    if s > KW - 1:
                        continue
                    k = KW - 1 - s
                    acc = acc + rolled[KWP - 8 * q:KWP - 8 * q + R, :] * w_ref[k:k + 1, :]
            vc_ref[pl.ds(r0, R), :] = acc
            return carry
        lax.fori_loop(0, TP // R, conv_chunk, 0)

    return pl.pallas_call(
        body, name="conf_fwd_conv",
        grid=(NCB,),
        in_specs=[pl.BlockSpec((TP, CB), lambda j: (0, 2 * NCB + j)),
                  pl.BlockSpec((TP, CB), lambda j: (0, 3 * NCB + j)),
                  pl.BlockSpec((KWP, CB), lambda j: (0, j)),
                  pl.BlockSpec((1, CB), lambda j: (0, j))],
        out_specs=pl.BlockSpec((TP, CB), lambda j: (0, j)),
        out_shape=jax.ShapeDtypeStruct((TP, DC), F32),
        scratch_shapes=[pltpu.VMEM((TP + KWP, CB), F32)],
        compiler_params=_cparams(),
    )(z, z, dw_w, dw_b)


def _ln_chunk(vc, lw, lb):
    mu = jnp.mean(vc, axis=-1, keepdims=True)
    xm = vc - mu
    var = jnp.mean(xm * xm, axis=-1, keepdims=True)
    rstd = lax.rsqrt(var + EPS)
    xhat = xm * rstd
    return xhat, rstd, xhat * lw + lb


def _conf_fwd_proj(vc, z, ln_w, ln_b, pw_w, pw_b):
    def body(vc_ref, g_ref, lw_ref, lb_ref, w_ref, b_ref, y_ref, p_ref, s_s):
        lw, lb = lw_ref[...], lb_ref[...]

        def ln_chunk(ci, carry):
            r0 = pl.multiple_of(ci * R, R)
            for half in range(2):
                rr = r0 + 8 * half
                _, _, ln = _ln_chunk(vc_ref[pl.ds(rr, 8), :], lw, lb)
                p_ref[pl.ds(rr, 8), :] = ln * _sig(ln)
            s_s[pl.ds(r0, R), :] = p_ref[pl.ds(r0, R), :].astype(BF16)
            return carry
        lax.fori_loop(0, TM // R, ln_chunk, 0, unroll=2)

        p_ref[...] = jnp.dot(s_s[...], w_ref[...], preferred_element_type=F32) + b_ref[...]

        def out_chunk(ci, carry):
            r0 = pl.multiple_of(ci * R, R)
            g = g_ref[pl.ds(r0, R), :]
            y_ref[pl.ds(r0, R), :] = (p_ref[pl.ds(r0, R), :] * (g * _sig(g))).astype(BF16)
            return carry
        lax.fori_loop(0, TM // R, out_chunk, 0)

    row = pl.BlockSpec((TM, DC), lambda i: (i, 0))
    vec = pl.BlockSpec((1, DC), lambda i: (0, 0))
    return pl.pallas_call(
        body, name="conf_fwd_proj",
        grid=(TP // TM,),
        in_specs=[row, pl.BlockSpec((TM, DC), lambda i: (i, 4)), vec, vec,
                  pl.BlockSpec((DC, DC), lambda i: (0, 0)), vec],
        out_specs=[row, row],
        out_shape=[jax.ShapeDtypeStruct((TP, DC), BF16), jax.ShapeDtypeStruct((TP, DC), F32)],
        scratch_shapes=[pltpu.VMEM((TM, DC), BF16)],
        compiler_params=_cparams(),
    )(vc, z, ln_w, ln_b, pw_w, pw_b)


def _outproj_loss(ylru, yconf, w_out, h, tgt, post_w):
    def body(yl_ref, yc_ref, w_ref, h_ref, t_ref, pw_ref, dout_ref, dy_ref, loss_ref, dpw_ref, y_s):
        i = pl.program_id(0)
        k = pl.program_id(1)

        @pl.when(k == 0)
        def _():
            y_s[...] = jnp.dot(yl_ref[...], w_ref[...], preferred_element_type=F32)

        @pl.when(k == 1)
        def _():
            y_s[...] += jnp.dot(yc_ref[...], w_ref[...], preferred_element_type=F32)

        @pl.when(jnp.logical_and(i == 0, k == 1))
        def _():
            loss_ref[...] = jnp.zeros_like(loss_ref)
            dpw_ref[...] = jnp.zeros_like(dpw_ref)

        @pl.when(k == 1)
        def _():
            pw = pw_ref[...]
            row = _row_iota((8, D))

            def chunk(ci, carry):
                r0 = pl.multiple_of(ci * 8, 8)
                yv = y_s[pl.ds(r0, 8), :]
                rs = lax.rsqrt(jnp.mean(yv * yv, axis=-1, keepdims=True) + EPS)
                grow = row + (i * TM + r0)
                valid = jnp.logical_and(grow >= NMETA, grow < T)
                yn = yv * rs
                err = jnp.where(valid, h_ref[pl.ds(r0, 8), :] + yn * pw - t_ref[pl.ds(r0, 8), :], 0.0)
                loss_ref[...] += err * err
                d_rn = err * (1.0 / D)
                dout_ref[pl.ds(r0, 8), :] = d_rn
                dpw_ref[...] += d_rn * yn
                gw = d_rn * pw
                dot = jnp.mean(gw * yv, axis=-1, keepdims=True)
                dy_ref[pl.ds(r0, 8), :] = (rs * gw - yv * (rs * rs * rs * dot)).astype(BF16)
                return carry
            lax.fori_loop(0, TM // 8, chunk, 0, unroll=4)

    row = pl.BlockSpec((TM, D), lambda i, k: (i, 0))
    half = pl.BlockSpec((TM, DL), lambda i, k: (i, 0))
    acc = pl.BlockSpec((8, D), lambda i, k: (0, 0))
    return pl.pallas_call(
        body, name="outproj_loss",
        grid=(TP // TM, 2),
        in_specs=[half, half, pl.BlockSpec((DL, D), lambda i, k: (k, 0)), row, row,
                  pl.BlockSpec((1, D), lambda i, k: (0, 0))],
        out_specs=[row, row, acc, acc],
        out_shape=[jax.ShapeDtypeStruct((TP, D), F32), jax.ShapeDtypeStruct((TP, D), BF16),
                   jax.ShapeDtypeStruct((8, D), F32), jax.ShapeDtypeStruct((8, D), F32)],
        scratch_shapes=[pltpu.VMEM((TM, D), F32)],
        compiler_params=_cparams(),
    )(ylru, yconf, w_out, h, tgt, post_w)


_NT = (((1,), (1,)), ((), ()))
_TN = (((0,), (0,)), ((), ()))


def _outproj_bwd(dy, ylru, yconf, w_out):
    def body(dy_ref, yl_ref, yc_ref, w_ref, dycat_ref, dw_ref):
        j = pl.program_id(0)
        dyv = dy_ref[...]
        dycat_ref[...] = lax.dot_general(dyv, w_ref[...], _NT, preferred_element_type=F32)

        @pl.when(j < NCB)
        def _():
            dw_ref[...] = lax.dot_general(yl_ref[...], dyv, _TN, preferred_element_type=F32).astype(BF16)

        @pl.when(j >= NCB)
        def _():
            dw_ref[...] = lax.dot_general(yc_ref[...], dyv, _TN, preferred_element_type=F32).astype(BF16)

    return pl.pallas_call(
        body, name="outproj_bwd",
        grid=(2 * NCB,),
        in_specs=[pl.BlockSpec((TP, D), lambda j: (0, 0)),
                  pl.BlockSpec((TP, CB), lambda j: (0, jnp.minimum(j, NCB - 1))),
                  pl.BlockSpec((TP, CB), lambda j: (0, jnp.maximum(j - NCB, 0))),
                  pl.BlockSpec((CB, D), lambda j: (j, 0))],
        out_specs=[pl.BlockSpec((TP, CB), lambda j: (0, j)), pl.BlockSpec((CB, D), lambda j: (j, 0))],
        out_shape=[jax.ShapeDtypeStruct((TP, D), F32), jax.ShapeDtypeStruct((D, D), BF16)],
        compiler_params=_cparams(),
    )(dy, ylru, yconf, w_out)


def _conf_bwd_proj(dycat, p, z, vc, ln_w, ln_b, pw_w):
    def body(dy_ref, p_ref, g_ref, vc_ref, lw_ref, lb_ref, w_ref,
             dvc_ref, dgc_ref, dpw_ref, vecs_ref, dp_s, s_s, ds_s):
        i = pl.program_id(0)
        lw, lb = lw_ref[...], lb_ref[...]

        @pl.when(i == 0)
        def _():
            dpw_ref[...] = jnp.zeros_like(dpw_ref)
            vecs_ref[...] = jnp.zeros_like(vecs_ref)

        def pre_chunk(ci, carry):
            r0 = pl.multiple_of(ci * R, R)
            for half in range(2):
                rr = r0 + 8 * half
                dyv = dy_ref[pl.ds(rr, 8), :]
                g = g_ref[pl.ds(rr, 8), :]
                sg = _sig(g)
                dp = dyv * (g * sg)
                dg = dyv * p_ref[pl.ds(rr, 8), :] * (sg * (1.0 + g * (1.0 - sg)))
                vecs_ref[0:8, :] += dp
                vecs_ref[8:16, :] += dg
                ds_s[pl.ds(rr, 8), :] = dp
                dvc_ref[pl.ds(rr, 8), :] = dg
            dp_s[pl.ds(r0, R), :] = ds_s[pl.ds(r0, R), :].astype(BF16)
            dgc_ref[pl.ds(r0, R), :] = dvc_ref[pl.ds(r0, R), :].astype(BF16)
            for half in range(2):
                rr = r0 + 8 * half
                _, _, ln = _ln_chunk(vc_ref[pl.ds(rr, 8), :], lw, lb)
                ds_s[pl.ds(rr, 8), :] = ln * _sig(ln)
            s_s[pl.ds(r0, R), :] = ds_s[pl.ds(r0, R), :].astype(BF16)
            return carry
        lax.fori_loop(0, TM // R, pre_chunk, 0, unroll=2)

        dpb = dp_s[...]
        ds_s[...] = lax.dot_general(dpb, w_ref[...], _NT, preferred_element_type=F32)
        dpw_ref[...] += lax.dot_general(s_s[...], dpb, _TN, preferred_element_type=F32)

        def post_chunk(ci, carry):
            r0 = pl.multiple_of(ci * 8, 8)
            xhat, rstd, ln = _ln_chunk(vc_ref[pl.ds(r0, 8), :], lw, lb)
            sl = _sig(ln)
            dln = ds_s[pl.ds(r0, 8), :] * (sl * (1.0 + ln * (1.0 - sl)))
            vecs_ref[16:24, :] += dln * xhat
            vecs_ref[24:32, :] += dln
            dxh = dln * lw
            m1 = jnp.mean(dxh, axis=-1, keepdims=True)
            m2 = jnp.mean(dxh * xhat, axis=-1, keepdims=True)
            dvc_ref[pl.ds(r0, 8), :] = rstd * (dxh - m1 - xhat * m2)
            return carry
        lax.fori_loop(0, TM // 8, post_chunk, 0, unroll=4)

    row = pl.BlockSpec((TM, DC), lambda i: (i, 0))
    vec = pl.BlockSpec((1, DC), lambda i: (0, 0))
    return pl.pallas_call(
        body, name="conf_bwd_proj",
        grid=(TP // TM,),
        in_specs=[pl.BlockSpec((TM, DC), lambda i: (i, 1)), row, pl.BlockSpec((TM, DC), lambda i: (i, 4)), row,
                  vec, vec, pl.BlockSpec((DC, DC), lambda i: (0, 0))],
        out_specs=[row, row, pl.BlockSpec((DC, DC), lambda i: (0, 0)), pl.BlockSpec((32, DC), lambda i: (0, 0))],
        out_shape=[jax.ShapeDtypeStruct((TP, DC), F32), jax.ShapeDtypeStruct((TP, DC), BF16),
                   jax.ShapeDtypeStruct((DC, DC), F32), jax.ShapeDtypeStruct((32, DC), F32)],
        scratch_shapes=[pltpu.VMEM((TM, DC), BF16), pltpu.VMEM((TM, DC), BF16), pltpu.VMEM((TM, DC), F32)],
        compiler_params=_cparams(),
    )(dycat, p, z, vc, ln_w, ln_b, pw_w)


def _conf_bwd_conv(dvc, z, dw_w):
    def body(dvc_ref, u1_ref, u2_ref, w_ref, du_ref, dw_ref, vecs_ref, vs, dvs):
        vs[pl.ds(0, KWP), :] = jnp.zeros((KWP, CB), F32)
        dvs[pl.ds(TP, KWP), :] = jnp.zeros((KWP, CB), F32)
        dw_ref[...] = jnp.zeros_like(dw_ref)
        vecs_ref[...] = jnp.zeros_like(vecs_ref)

        def fill_chunk(ci, carry):
            r0 = pl.multiple_of(ci * R, R)
            vs[pl.ds(KWP + r0, R), :] = u1_ref[pl.ds(r0, R), :] * _sig(u2_ref[pl.ds(r0, R), :])
            dv = dvc_ref[pl.ds(r0, R), :]
            dvs[pl.ds(r0, R), :] = dv
            vecs_ref[0:8, :] += _fold8(dv)
            return carry
        lax.fori_loop(0, TP // R, fill_chunk, 0)

        def conv_chunk(ci, carry):
            r0 = pl.multiple_of(ci * R, R)
            vbuf = vs[pl.ds(r0, KWP + R), :]
            dbuf = dvs[pl.ds(r0, KWP + R), :]
            dcur = dbuf[0:R, :]
            dv = jnp.zeros((R, CB), F32)
            for rr in range(8):
                vroll = vbuf if rr == 0 else pltpu.roll(vbuf, rr, 0)
                droll = dbuf if rr == 0 else pltpu.roll(dbuf, KWP + R - rr, 0)
                for q in range(4):
                    s = 8 * q + rr
                    if s > KW - 1:
                        continue
                    k = KW - 1 - s
                    dv = dv + droll[8 * q:8 * q + R, :] * w_ref[k:k + 1, :]
                    dw_ref[8 * k:8 * k + 8, :] += _fold8(dcur * vroll[KWP - 8 * q:KWP - 8 * q + R, :])
            u1 = u1_ref[pl.ds(r0, R), :]
            sg = _sig(u2_ref[pl.ds(r0, R), :])
            du1 = dv * sg
            du2 = dv * u1 * (sg * (1.0 - sg))
            du_ref[0, pl.ds(r0, R), :] = du1.astype(BF16)
            du_ref[1, pl.ds(r0, R), :] = du2.astype(BF16)
            vecs_ref[8:16, :] += _fold8(du1)
            vecs_ref[16:24, :] += _fold8(du2)
            return carry
        lax.fori_loop(0, TP // R, conv_chunk, 0)

    blk = pl.BlockSpec((TP, CB), lambda j: (0, j))
    return pl.pallas_call(
        body, name="conf_bwd_conv",
        grid=(NCB,),
        in_specs=[blk, pl.BlockSpec((TP, CB), lambda j: (0, 2 * NCB + j)),
                  pl.BlockSpec((TP, CB), lambda j: (0, 3 * NCB + j)), pl.BlockSpec((KWP, CB), lambda j: (0, j))],
        out_specs=[pl.BlockSpec((2, TP, CB), lambda j: (0, 0, j)), pl.BlockSpec((8 * KWP, CB), lambda j: (0, j)),
                   pl.BlockSpec((24, CB), lambda j: (0, j))],
        out_shape=[jax.ShapeDtypeStruct((2, TP, DC), BF16),
                   jax.ShapeDtypeStruct((8 * KWP, DC), F32), jax.ShapeDtypeStruct((24, DC), F32)],
        scratch_shapes=[pltpu.VMEM((TP + KWP, CB), F32), pltpu.VMEM((TP + KWP, CB), F32)],
        compiler_params=_cparams(),
    )(dvc, z, z, dw_w)


def _lru_bwd(dycat, z, xc, hs, conv_w, wa_g, b_a, wx_g, b_x, lam):
    NV = 6

    def body(dy_ref, x_ref, g_ref, xc_ref, hs_ref, cw_ref, wa_ref, ba_ref, wx_ref, bx_ref, lam_ref,
             dzl_ref, dwa_ref, dwx_ref, dcw_ref, vecs_ref, ga_s, gx_s, dxc_s):
        vecs_ref[...] = jnp.zeros_like(vecs_ref)
        dcw_ref[...] = jnp.zeros_like(dcw_ref)
        dxc_s[pl.ds(TP, 8), :] = jnp.zeros((8, CB), F32)

        def gate_chunk(ci, carry):
            r0 = pl.multiple_of(ci * TM, TM)
            xb = xc_ref[pl.ds(r0, TM), :].astype(BF16)
            ga_s[pl.ds(r0, TM), :] = jnp.dot(xb, wa_ref[...], preferred_element_type=F32) + ba_ref[...]
            gx_s[pl.ds(r0, TM), :] = jnp.dot(xb, wx_ref[...], preferred_element_type=F32) + bx_ref[...]
            return carry
        lax.fori_loop(0, TP // TM, gate_chunk, 0)

        sp8 = LRU_C * _softplus(-lam_ref[...])
        row = _row_iota((R, CB))
        nchunk = TP // R

        def scan_chunk(cj, carry):
            a_next, lam_next = carry
            ci = nchunk - 1 - cj
            r0 = pl.multiple_of(ci * R, R)
            dyv = dy_ref[pl.ds(r0, R), :]
            g = g_ref[pl.ds(r0, R), :]
            hv = hs_ref[pl.ds(r0, R), :]
            xc = xc_ref[pl.ds(r0, R), :]
            sg = _sig(g)
            dgl = dyv * hv * (sg * (1.0 + g * (1.0 - sg)))
            dzl_ref[1, pl.ds(r0, R), :] = dgl.astype(BF16)
            vecs_ref[0:8, :] += _fold8(dgl)
            dhs = dyv * (g * sg)
            r, i, a, mult = _gate_values(ga_s[pl.ds(r0, R), :], gx_s[pl.ds(r0, R), :], xc, sp8)
            b = jnp.where(row == R - 1, a_next, pltpu.roll(a, R - 1, 0))
            lv = dhs
            k = 1
            while k < R:
                m = row < R - k
                lv = jnp.where(m, lv + b * pltpu.roll(lv, R - k, 0), lv)
                b = jnp.where(m, b * pltpu.roll(b, R - k, 0), b)
                k *= 2
            lv = lv + b * lam_next
            p0 = pl.multiple_of(jnp.maximum(r0 - 8, 0), 8)
            hprev8 = jnp.where(ci > 0, hs_ref[pl.ds(p0, 8), :], 0.0)
            hprev = pltpu.roll(jnp.concatenate([hprev8, hv], axis=0), 1, 0)[8:8 + R, :]
            da = lv * hprev
            ixc = i * xc
            dmult = lv * ixc
            di = lv * mult * xc
            dxc_s[pl.ds(r0, R), :] = lv * mult * i
            a2 = a * a
            dlog_a = da * a - dmult * a2 / mult
            vecs_ref[32:40, :] += _fold8(dlog_a * r)
            dga = -(dlog_a * sp8) * r * (1.0 - r)
            dgx = di * i * (1.0 - i)
            ga_s[pl.ds(r0, R), :] = dga
            gx_s[pl.ds(r0, R), :] = dgx
            vecs_ref[16:24, :] += _fold8(dga)
            vecs_ref[24:32, :] += _fold8(dgx)
            a_first = jnp.sum(jnp.where(row == 0, a, 0.0), axis=0, keepdims=True)
            l_first = jnp.sum(jnp.where(row == 0, lv, 0.0), axis=0, keepdims=True)
            return a_first, l_first
        lax.fori_loop(0, nchunk, scan_chunk, (jnp.zeros((1, CB), F32), jnp.zeros((1, CB), F32)))

        dwa_ref[...] = jnp.zeros_like(dwa_ref)
        dwx_ref[...] = jnp.zeros_like(dwx_ref)

        def mm_chunk(ci, carry):
            r0 = pl.multiple_of(ci * TM, TM)
            xb = xc_ref[pl.ds(r0, TM), :].astype(BF16)
            dgab = ga_s[pl.ds(r0, TM), :].astype(BF16)
            dgxb = gx_s[pl.ds(r0, TM), :].astype(BF16)
            dxc_s[pl.ds(r0, TM), :] += (lax.dot_general(dgab, wa_ref[...], _NT, preferred_element_type=F32)
                                        + lax.dot_general(dgxb, wx_ref[...], _NT, preferred_element_type=F32))
            dwa_ref[...] += lax.dot_general(xb, dgab, _TN, preferred_element_type=F32)
            dwx_ref[...] += lax.dot_general(xb, dgxb, _TN, preferred_element_type=F32)
            return carry
        lax.fori_loop(0, TP // TM, mm_chunk, 0)

        taps = [cw_ref[k:k + 1, :] for k in range(LW)]

        def conv_chunk(ci, carry):
            r0 = pl.multiple_of(ci * R, R)
            dbuf = dxc_s[pl.ds(r0, R + 8), :]
            dcur = dbuf[0:R, :]
            p0 = pl.multiple_of(jnp.maximum(r0 - 8, 0), 8)
            xprev = jnp.where(ci > 0, x_ref[pl.ds(p0, 8), :], 0.0)
            xbuf = jnp.concatenate([xprev, x_ref[pl.ds(r0, R), :]], axis=0)
            dxl = dcur * taps[LW - 1]
            dcw_ref[8 * (LW - 1):8 * LW, :] += _fold8(dcur * xbuf[8:8 + R, :])
            for s in range(1, LW):
                k = LW - 1 - s
                dxl = dxl + pltpu.roll(dbuf, R + 8 - s, 0)[0:R, :] * taps[k]
                dcw_ref[8 * k:8 * k + 8, :] += _fold8(dcur * pltpu.roll(xbuf, s, 0)[8:8 + R, :])
            dzl_ref[0, pl.ds(r0, R), :] = dxl.astype(BF16)
            vecs_ref[8:16, :] += _fold8(dxl)
            vecs_ref[40:48, :] += _fold8(dcur)
            return carry
        lax.fori_loop(0, TP // R, conv_chunk, 0)
        vecs_ref[32:40, :] = vecs_ref[32:40, :] * (LRU_C * _sig(-lam_ref[...]))

    col = lambda off: pl.BlockSpec((TP, CB), lambda j: (0, off + j))
    vec = pl.BlockSpec((1, CB), lambda j: (0, j))
    wsp = pl.BlockSpec((None, CB, CB), lambda j: (j, 0, 0))
    return pl.pallas_call(
        body, name="lru_bwd",
        grid=(NCB,),
        in_specs=[col(0), col(0), col(NCB), col(0), col(0), pl.BlockSpec((LW, CB), lambda j: (0, j)),
                  wsp, vec, wsp, vec, vec],
        out_specs=[pl.BlockSpec((2, TP, CB), lambda j: (0, 0, j)), wsp, wsp,
                   pl.BlockSpec((8 * LW, CB), lambda j: (0, j)), pl.BlockSpec((8 * NV, CB), lambda j: (0, j))],
        out_shape=[jax.ShapeDtypeStruct((2, TP, DL), BF16),
                   jax.ShapeDtypeStruct((NCB, CB, CB), F32), jax.ShapeDtypeStruct((NCB, CB, CB), F32),
                   jax.ShapeDtypeStruct((8 * LW, DL), F32), jax.ShapeDtypeStruct((8 * NV, DL), F32)],
        scratch_shapes=[pltpu.VMEM((TP, CB), F32), pltpu.VMEM((TP, CB), F32), pltpu.VMEM((TP + 8, CB), F32)],
        compiler_params=_cparams(),
    )(dycat, z, z, xc, hs, conv_w, wa_g, b_a, wx_g, b_x, lam)


def _dz_section(sec, dzl_ref, dzc_ref, dgc_ref, use):
    @pl.when(sec < 2)
    def _():
        use(dzl_ref)

    @pl.when(jnp.logical_and(sec >= 2, sec < 4))
    def _():
        use(dzc_ref)

    @pl.when(sec == 4)
    def _():
        use(dgc_ref)


def _dz_specs(rows, index):
    return [pl.BlockSpec((None, rows, 1024), lambda a, b: (jnp.minimum(index(a, b)[1], 1), index(a, b)[0], 0)),
            pl.BlockSpec((None, rows, 1024), lambda a, b: (jnp.clip(index(a, b)[1] - 2, 0, 1), index(a, b)[0], 0)),
            pl.BlockSpec((rows, 1024), lambda a, b: (index(a, b)[0], 0))]


def _inproj_wgrad(name, hn, dzs):
    KB = 512
    nsec = dzs.shape[0]

    def body(hn_ref, dz_ref, dw_ref):
        dw_ref[...] = lax.dot_general(hn_ref[...], dz_ref[...], _TN, preferred_element_type=F32).astype(BF16)

    return pl.pallas_call(
        body, name=name,
        grid=(nsec, D // KB),
        in_specs=[pl.BlockSpec((TP, KB), lambda n, kb: (0, kb)),
                  pl.BlockSpec((None, TP, 1024), lambda n, kb: (n, 0, 0))],
        out_specs=pl.BlockSpec((KB, 1024), lambda n, kb: (kb, n)),
        out_shape=jax.ShapeDtypeStruct((D, nsec * 1024), BF16),
        compiler_params=_cparams(),
    )(hn, dzs)


def _sum_win_parts(parts_a, parts_b, parts_c):
    RB = 64

    def body(a_ref, b_ref, c_ref, o_ref):
        def chunk(ci, carry):
            r0 = pl.multiple_of(ci * R, R)
            for ref, base, ncol in ((a_ref, 0, 2048), (b_ref, 2048, 2048), (c_ref, 4096, 1024)):
                for c0 in range(0, ncol, 512):
                    acc = ref[0, pl.ds(r0, R), c0:c0 + 512].astype(F32)
                    for sidx in range(1, NDEV):
                        acc = acc + ref[sidx, pl.ds(r0, R), c0:c0 + 512].astype(F32)
                    o_ref[pl.ds(r0, R), base + c0:base + c0 + 512] = acc.astype(BF16)
            return carry
        lax.fori_loop(0, RB // R, chunk, 0)

    spec = lambda ncol: pl.BlockSpec((NDEV, RB, ncol), lambda i: (0, i, 0))
    return pl.pallas_call(
        body, name="sum_win_parts",
        grid=(D // NDEV // RB,),
        in_specs=[spec(2048), spec(2048), spec(1024)],
        out_specs=pl.BlockSpec((RB, NIN), lambda i: (i, 0)),
        out_shape=jax.ShapeDtypeStruct((D // NDEV, NIN), BF16),
        compiler_params=_cparams(),
    )(parts_a, parts_b, parts_c)


def _inproj_bwd(dzl, dzc, dgc, w_in, h, dout, pre_w):
    nsec = NIN // 1024

    def body(dzl_ref, dzc_ref, dgc_ref, w_ref, h_ref, dout_ref, pw_ref, dh_ref, dpw_ref, acc_s):
        i = pl.program_id(0)
        s = pl.program_id(1)

        @pl.when(s == 0)
        def _():
            acc_s[...] = jnp.zeros_like(acc_s)

        def use(dz_ref):
            acc_s[...] += lax.dot_general(dz_ref[...], w_ref[...], _NT, preferred_element_type=F32)
        _dz_section(s, dzl_ref, dzc_ref, dgc_ref, use)

        @pl.when(jnp.logical_and(i == 0, s == nsec - 1))
        def _():
            dpw_ref[...] = jnp.zeros_like(dpw_ref)

        @pl.when(s == nsec - 1)
        def _():
            pw = pw_ref[...]

            def chunk(ci, carry):
                r0 = pl.multiple_of(ci * 8, 8)
                hv = h_ref[pl.ds(r0, 8), :]
                dhn = acc_s[pl.ds(r0, 8), :]
                rs = lax.rsqrt(jnp.mean(hv * hv, axis=-1, keepdims=True) + EPS)
                dpw_ref[...] += dhn * (hv * rs)
                gw = dhn * pw
                dot = jnp.mean(gw * hv, axis=-1, keepdims=True)
                dh_ref[pl.ds(r0, 8), :] = rs * gw - hv * (rs * rs * rs * dot) + dout_ref[pl.ds(r0, 8), :]
                return carry
            lax.fori_loop(0, TM // 8, chunk, 0, unroll=4)

    row = pl.BlockSpec((TM, D), lambda i, s: (i, 0))
    return pl.pallas_call(
        body, name="inproj_bwd",
        grid=(TP // TM, nsec),
        in_specs=_dz_specs(TM, lambda i, s: (i, s)) + [
            pl.BlockSpec((D, 1024), lambda i, s: (0, s)), row, row, pl.BlockSpec((1, D), lambda i, s: (0, 0))],
        out_specs=[row, pl.BlockSpec((8, D), lambda i, s: (0, 0))],
        out_shape=[jax.ShapeDtypeStruct((TP, D), F32), jax.ShapeDtypeStruct((8, D), F32)],
        scratch_shapes=[pltpu.VMEM((TM, D), F32)],
        compiler_params=_cparams(),
    )(dzl, dzc, dgc, w_in, h, dout, pre_w)


def _adamw(name, parts, w, m, v, block_rows):
    rows, cols = w.shape
    nparts = parts.shape[0]
    cw = cols if cols <= 640 else 512

    def body(p_ref, w_ref, m_ref, v_ref, g_ref, d_ref, nm_ref, nv_ref):
        def chunk(ci, carry):
            r0 = pl.multiple_of(ci * R, R)
            for c0 in range(0, cols, cw):
                at = (pl.ds(r0, R), slice(c0, c0 + cw))
                g = p_ref[(0,) + at].astype(F32)
                for sidx in range(1, nparts):
                    g = g + p_ref[(sidx,) + at].astype(F32)
                delta, mv, vv = _adam_math(g, w_ref[at], m_ref[at], v_ref[at])
                g_ref[at] = g
                nm_ref[at] = mv
                nv_ref[at] = vv
                d_ref[at] = delta
            return carry
        lax.fori_loop(0, block_rows // R, chunk, 0)

    blk = pl.BlockSpec((block_rows, cols), lambda i: (i, 0))
    shp = jax.ShapeDtypeStruct((rows, cols), F32)
    return pl.pallas_call(
        body, name=name,
        grid=(rows // block_rows,),
        in_specs=[pl.BlockSpec((nparts, block_rows, cols), lambda i: (0, i, 0)), blk, blk, blk],
        out_specs=[blk, blk, blk, blk],
        out_shape=[shp, shp, shp, shp],
        compiler_params=_cparams(),
    )(parts, w, m, v)


def _adam_math(g, w, m, v):
    c1 = 1.0 / (1.0 - ADAM_B1 ** ADAM_STEP)
    c2 = 1.0 / (1.0 - ADAM_B2 ** ADAM_STEP)
    mv = ADAM_B1 * m + (1.0 - ADAM_B1) * g
    vv = ADAM_B2 * v + (1.0 - ADAM_B2) * (g * g)
    upd = (mv * c1) / (jnp.sqrt(vv * c2) + ADAM_EPS) + ADAM_WD * w
    return -ADAM_LR * upd, mv, vv


_VEC = [("pre_norm_w", 2), ("post_norm_w", 2), ("b_in", 5), ("lru_conv_b", 1), ("b_gate_a", 1), ("b_gate_x", 1),
        ("lru_lambda", 1), ("conf_dw_b", 1), ("conf_ln_w", 1), ("conf_ln_b", 1), ("conf_pw_b", 1)]
_VEC_ROWS = 24
_LOSS_ROW = 17
_SM_ROWS = 64


def _pack_grads(dprew_acc, dpostw_acc, cvecs, kvecs, lvecs, dcw_acc, ddw_acc, dh, loss_acc):
    def body(pre_ref, post_ref, c_ref, k_ref, l_ref, dcw_ref, ddw_ref, dh_ref, loss_ref, vec_ref, small_ref, tmp):
        s8 = lambda ref, r: jnp.sum(ref[8 * r:8 * r + 8, :], axis=0, keepdims=True)
        vec_ref[...] = jnp.zeros_like(vec_ref)
        pre, post = s8(pre_ref, 0), s8(post_ref, 0)
        rows = [pre[:, 0:1024], pre[:, 1024:2048], post[:, 0:1024], post[:, 1024:2048],
                s8(l_ref, 1), s8(l_ref, 0), s8(k_ref, 1), s8(k_ref, 2), s8(c_ref, 1),
                s8(l_ref, 5), s8(l_ref, 2), s8(l_ref, 3), s8(l_ref, 4),
                s8(k_ref, 0), s8(c_ref, 2), s8(c_ref, 3), s8(c_ref, 0)]
        for r, val in enumerate(rows):
            vec_ref[r:r + 1, :] = val
        vec_ref[_LOSS_ROW:_LOSS_ROW + 1, :] = jnp.zeros((1, 1024), F32) + (0.5 / D) * jnp.sum(loss_ref[...])

        small_ref[...] = jnp.zeros_like(small_ref)
        for k in range(LW):
            tmp[k:k + 1, :] = s8(dcw_ref, k)
        for k in range(KW):
            tmp[8 + k:9 + k, :] = s8(ddw_ref, k)
        for d in range(NDEV):
            small_ref[d, 0:LW, 0:128] = tmp[0:LW, 128 * d:128 * d + 128]
            small_ref[d, 8:8 + KW, 0:128] = tmp[8:8 + KW, 128 * d:128 * d + 128]
            small_ref[d, 40:56, :] = dh_ref[:, 256 * d:256 * d + 256]

    full = lambda a: pl.BlockSpec(a.shape, lambda i: (0,) * a.ndim)
    ins = [dprew_acc, dpostw_acc, cvecs, kvecs, lvecs, dcw_acc, ddw_acc]
    return pl.pallas_call(
        body, name="pack_grads",
        grid=(1,),
        in_specs=[full(a) for a in ins] + [pl.BlockSpec((NMETA, D), lambda i: (0, 0)), full(loss_acc)],
        out_specs=[pl.BlockSpec((_VEC_ROWS, 1024), lambda i: (0, 0)),
                   pl.BlockSpec((NDEV, _SM_ROWS, 256), lambda i: (0, 0, 0))],
        out_shape=[jax.ShapeDtypeStruct((_VEC_ROWS, 1024), F32), jax.ShapeDtypeStruct((NDEV, _SM_ROWS, 256), F32)],
        scratch_shapes=[pltpu.VMEM((40, 1024), F32)],
        compiler_params=_cparams(),
    )(*ins, dh, loss_acc)


def _adamw_vec(parts, W, M, V):
    nv = len(_VEC)

    def body(*refs):
        p_ref = refs[0]
        w_refs, m_refs, v_refs = refs[1:1 + nv], refs[1 + nv:1 + 2 * nv], refs[1 + 2 * nv:1 + 3 * nv]
        outs = refs[1 + 3 * nv:]

        def total(r):
            acc = p_ref[0, r:r + 1, :]
            for sidx in range(1, NDEV):
                acc = acc + p_ref[sidx, r:r + 1, :]
            return acc

        row = 0
        for idx, (_, nrows) in enumerate(_VEC):
            for part in range(nrows):
                cols = slice(1024 * part, 1024 * part + 1024)
                g = total(row + part)
                delta, mv, vv = _adam_math(g, w_refs[idx][:, cols], m_refs[idx][:, cols], v_refs[idx][:, cols])
                for o, val in zip(outs[4 * idx:4 * idx + 4], (g, delta, mv, vv)):
                    o[:, cols] = val
            row += nrows
        outs[-1][...] = total(_LOSS_ROW)[:, 0:128]

    names = [n for n, _ in _VEC]
    flat = lambda d: [d[n].reshape(1, -1) for n in names]
    ws, ms, vs = flat(W), flat(M), flat(V)
    res = pl.pallas_call(
        body, name="adamw_vec",
        out_shape=[jax.ShapeDtypeStruct(w.shape, F32) for w in ws for _ in range(4)]
        + [jax.ShapeDtypeStruct((1, 128), F32)],
        compiler_params=_cparams(),
    )(parts, *ws, *ms, *vs)
    return {n: tuple(res[4 * i:4 * i + 4]) for i, n in enumerate(names)}, res[-1]


def _adamw_small(parts, W, M, V):
    where = {"lru_conv_w": (slice(0, LW), slice(0, 128)), "conf_dw_w": (slice(8, 8 + KW), slice(0, 128)),
             "meta_tokens": (slice(40, 56), slice(0, 256))}
    names = list(where)

    def body(*refs):
        p_ref = refs[0]
        outs = refs[10:]
        for idx, n in enumerate(names):
            rs, cs = where[n]
            g = p_ref[0, rs, cs]
            for sidx in range(1, NDEV):
                g = g + p_ref[sidx, rs, cs]
            delta, mv, vv = _adam_math(g, refs[1 + idx][...], refs[4 + idx][...], refs[7 + idx][...])
            for o, val in zip(outs[4 * idx:4 * idx + 4], (g, delta, mv, vv)):
                o[...] = val

    two_d = lambda a: a.reshape(a.shape[-2:])
    ws, ms, vs = ([two_d(d[n]) for n in names] for d in (W, M, V))
    res = pl.pallas_call(
        body, name="adamw_small",
        out_shape=[jax.ShapeDtypeStruct(w.shape, F32) for w in ws for _ in range(4)],
        compiler_params=_cparams(),
    )(parts, *ws, *ms, *vs)
    return {n: tuple(res[4 * i:4 * i + 4]) for i, n in enumerate(names)}


def _pack_small(lru_cw, dw_w, meta):
    buf = jnp.zeros((_SM_ROWS, 256), F32)
    buf = buf.at[0:LW, 0:128].set(lru_cw)
    buf = buf.at[8:8 + dw_w.shape[0], 0:128].set(dw_w)
    return buf.at[40:56, :].set(meta)


def _block_diag4(w):
    w4 = w.reshape(NCB, 4, 64, 64)
    eye = jnp.eye(4, dtype=w.dtype)
    return jnp.einsum("ghij,hk->ghikj", w4, eye).reshape(NCB, CB, CB)


def _diag_blocks(g):
    g5 = g.reshape(NCB, 4, 64, 4, 64)
    return jnp.stack([g5[:, hh, :, hh, :] for hh in range(4)], axis=1).reshape(16, 64, 64)


def _local_step(x, target, meta_full, win_full, out_weights, lru_cw_full, dw_w_full, W, send):
    h = jnp.concatenate([meta_full, x, jnp.zeros((TP - T, D), F32)], axis=0)
    tgt = jnp.concatenate([jnp.zeros((NMETA, D), F32), target, jnp.zeros((TP - T, D), F32)], axis=0)
    wa_g = _block_diag4(W["w_gate_a"][0]).astype(BF16)
    wx_g = _block_diag4(W["w_gate_x"][0]).astype(BF16)

    z, hn = _prenorm_inproj(h, W["pre_norm_w"], win_full, W["b_in"])
    ylru, xc, hs = _lru_fwd(z, lru_cw_full, W["lru_conv_b"], wa_g, W["b_gate_a"], wx_g, W["b_gate_x"],
                            W["lru_lambda"])
    vc = _conf_fwd_conv(z, dw_w_full, W["conf_dw_b"])
    wout_full, pw_full = out_weights(vc)
    yconf, p = _conf_fwd_proj(vc, z, W["conf_ln_w"], W["conf_ln_b"], pw_full, W["conf_pw_b"])
    dout, dy, loss_acc, dpostw_acc = _outproj_loss(ylru, yconf, wout_full, h, tgt, W["post_norm_w"])

    dycat, dwout_part = _outproj_bwd(dy, ylru, yconf, wout_full)
    tok = send("w_out", dwout_part)
    dvc, dgc, dpw_part, cvecs = _conf_bwd_proj(dycat, p, z, vc, W["conf_ln_w"] + tok, W["conf_ln_b"], pw_full)
    tok = send("conf_pw_w", dpw_part)
    tok = tok + send("w_in_c", _inproj_wgrad("inproj_wgrad_c", hn, dgc[None]))
    dzc, ddw_acc, kvecs = _conf_bwd_conv(dvc, z, dw_w_full + tok)
    tok = send("w_in_b", _inproj_wgrad("inproj_wgrad_b", hn, dzc))
    dzl, dwa_g, dwx_g, dcw_acc, lvecs = _lru_bwd(dycat, z, xc, hs, lru_cw_full, wa_g, W["b_gate_a"] + tok, wx_g,
                                                 W["b_gate_x"], W["lru_lambda"])
    tok = send("w_in_a", _inproj_wgrad("inproj_wgrad_a", hn, dzl))
    tok = tok + send("w_gates", _diag_blocks(dwa_g).reshape(16 * 64, 64), _diag_blocks(dwx_g).reshape(16 * 64, 64))
    dh, dprew_acc = _inproj_bwd(dzl, dzc, dgc, win_full, h, dout, W["pre_norm_w"] + tok)

    vec_pack, small_part = _pack_grads(dprew_acc, dpostw_acc, cvecs, kvecs, lvecs, dcw_acc, ddw_acc, dh, loss_acc)
    return dh, vec_pack, small_part


def kernel(x, meta_tokens, pre_norm_w, post_norm_w, w_in, b_in, lru_conv_w, lru_conv_b, w_gate_a, b_gate_a, w_gate_x, b_gate_x, lru_lambda, conf_dw_w, conf_dw_b, conf_ln_w, conf_ln_b, conf_pw_w, conf_pw_b, w_out, loss_target, m_meta_tokens, m_pre_norm_w, m_post_norm_w, m_w_in, m_b_in, m_lru_conv_w, m_lru_conv_b, m_w_gate_a, m_b_gate_a, m_w_gate_x, m_b_gate_x, m_lru_lambda, m_conf_dw_w, m_conf_dw_b, m_conf_ln_w, m_conf_ln_b, m_conf_pw_w, m_conf_pw_b, m_w_out, v_meta_tokens, v_pre_norm_w, v_post_norm_w, v_w_in, v_b_in, v_lru_conv_w, v_lru_conv_b, v_w_gate_a, v_b_gate_a, v_w_gate_x, v_b_gate_x, v_lru_lambda, v_conf_dw_w, v_conf_dw_b, v_conf_ln_w, v_conf_ln_b, v_conf_pw_w, v_conf_pw_b, v_w_out):
    W = dict(meta_tokens=meta_tokens, pre_norm_w=pre_norm_w, post_norm_w=post_norm_w, w_in=w_in, b_in=b_in,
             lru_conv_w=lru_conv_w, lru_conv_b=lru_conv_b, w_gate_a=w_gate_a, b_gate_a=b_gate_a,
             w_gate_x=w_gate_x, b_gate_x=b_gate_x, lru_lambda=lru_lambda, conf_dw_w=conf_dw_w,
             conf_dw_b=conf_dw_b, conf_ln_w=conf_ln_w, conf_ln_b=conf_ln_b, conf_pw_w=conf_pw_w,
             conf_pw_b=conf_pw_b, w_out=w_out)
    M = dict(meta_tokens=m_meta_tokens, pre_norm_w=m_pre_norm_w, post_norm_w=m_post_norm_w, w_in=m_w_in,
             b_in=m_b_in, lru_conv_w=m_lru_conv_w, lru_conv_b=m_lru_conv_b, w_gate_a=m_w_gate_a,
             b_gate_a=m_b_gate_a, w_gate_x=m_w_gate_x, b_gate_x=m_b_gate_x, lru_lambda=m_lru_lambda,
             conf_dw_w=m_conf_dw_w, conf_dw_b=m_conf_dw_b, conf_ln_w=m_conf_ln_w, conf_ln_b=m_conf_ln_b,
             conf_pw_w=m_conf_pw_w, conf_pw_b=m_conf_pw_b, w_out=m_w_out)
    V = dict(meta_tokens=v_meta_tokens, pre_norm_w=v_pre_norm_w, post_norm_w=v_post_norm_w, w_in=v_w_in,
             b_in=v_b_in, lru_conv_w=v_lru_conv_w, lru_conv_b=v_lru_conv_b, w_gate_a=v_w_gate_a,
             b_gate_a=v_b_gate_a, w_gate_x=v_w_gate_x, b_gate_x=v_b_gate_x, lru_lambda=v_lru_lambda,
             conf_dw_w=v_conf_dw_w, conf_dw_b=v_conf_dw_b, conf_ln_w=v_conf_ln_w, conf_ln_b=v_conf_ln_b,
             conf_pw_w=v_conf_pw_w, conf_pw_b=v_conf_pw_b, w_out=v_w_out)
    names = list(W.keys())
    shapes = {n: W[n].shape for n in names}

    small = _pack_small(lru_conv_w[0], conf_dw_w[0], meta_tokens)
    win_flight, tok = _win_gather_start(w_in[0].astype(BF16))
    gathered, tok = _exchange_start("gather_start", [
        (small + tok[0, 0], jax.ShapeDtypeStruct((NDEV, _SM_ROWS, 256), F32), _whole, _slot),
        (w_out[0].astype(BF16), jax.ShapeDtypeStruct((D, D), BF16), _whole, _rows(D // NDEV)),
        (conf_pw_w[0].astype(BF16), jax.ShapeDtypeStruct((DC, DC), BF16), _whole, _rows(DC // NDEV)),
    ])
    win_flight = _win_gather_forward(win_flight, tok)
    (small_all,) = _exchange_wait("gather_wait_small", gathered[0:1], x)
    win_full = _win_gather_wait(win_flight)
    unshard = lambda a: jnp.transpose(a, (1, 0, 2)).reshape(a.shape[1], -1)
    lru_cw_full = unshard(small_all[:, 0:LW, 0:128])
    dw_w_full = unshard(small_all[:, 8:8 + KWP, 0:128])
    meta_full = unshard(small_all[:, 40:56, :])

    def out_weights(after):
        return _exchange_wait("gather_wait_out", gathered[1:3], after)

    row_stage = lambda ncol: (jax.ShapeDtypeStruct((NDEV, D // NDEV, ncol), BF16), _rows(D // NDEV))
    piece = {"w_in_a": row_stage(2048), "w_in_b": row_stage(2048), "w_in_c": row_stage(1024),
             "w_out": row_stage(D),
             "conf_pw_w": (jax.ShapeDtypeStruct((NDEV, DC // NDEV, DC), BF16), _rows(DC // NDEV)),
             "w_gates": (jax.ShapeDtypeStruct((NDEV, 16 * 64, 64), BF16), _whole)}
    sent = {}

    def send(name, *parts):
        handles, token = _exchange_start(
            "scatter_" + name + "_start",
            [(part.astype(BF16), piece[name][0], piece[name][1], _slot) for part in parts])
        sent[name] = handles
        return token[0, 0]

    dh, vec_pack, small_part = _local_step(
        x[0], loss_target[0], meta_full, win_full, out_weights, lru_cw_full, dw_w_full, W, send)
    grad_x = dh[NMETA:T][None]

    (parts_c,) = _exchange_wait("scatter_w_in_c_wait", sent["w_in_c"], dh)
    (parts_b,) = _exchange_wait("scatter_w_in_b_wait", sent["w_in_b"], parts_c)
    (parts_a,) = _exchange_wait("scatter_w_in_a_wait", sent["w_in_a"], parts_b)
    win_rows = _sum_win_parts(parts_a, parts_b, parts_c)
    win_stage2, tok = _exchange_start("scatter_w_in_stage2_start", [
        (win_rows, jax.ShapeDtypeStruct((NDEV, D // NDEV, NIN // NDEV), BF16), _cols(NIN // NDEV), _slot)])
    rest, _ = _exchange_start("scatter_rest_start", [
        (small_part, jax.ShapeDtypeStruct((NDEV, _SM_ROWS, 256), F32), _slot, _slot),
        (vec_pack + tok[0, 0], jax.ShapeDtypeStruct((NDEV, _VEC_ROWS, 1024), F32), _whole, _slot),
    ])

    G, DW, NM, NV = {}, {}, {}, {}
    (wout_parts,) = _exchange_wait("scatter_w_out_wait", sent["w_out"], win_rows)
    G["w_out"], DW["w_out"], NM["w_out"], NV["w_out"] = _adamw("adamw_w_out", wout_parts, w_out[0], m_w_out[0], v_w_out[0], 64)
    (pw_parts,) = _exchange_wait("scatter_conf_pw_w_wait", sent["conf_pw_w"], G["w_out"])
    G["conf_pw_w"], DW["conf_pw_w"], NM["conf_pw_w"], NV["conf_pw_w"] = _adamw(
        "adamw_pw", pw_parts, conf_pw_w[0], m_conf_pw_w[0], v_conf_pw_w[0], 128)
    res = {}
    wa_parts, wx_parts = _exchange_wait("scatter_w_gates_wait", sent["w_gates"], G["conf_pw_w"])
    for n, parts in (("w_gate_a", wa_parts), ("w_gate_x", wx_parts)):
        res[n] = _adamw("adamw_" + n, parts, *[d[n].reshape(16 * 64, 64) for d in (W, M, V)], 16 * 64)
    small_parts, vec_parts = _exchange_wait("scatter_rest_wait", rest, res["w_gate_x"][0])
    res.update(_adamw_small(small_parts, W, M, V))
    vec_res, loss_row = _adamw_vec(vec_parts, W, M, V)
    res.update(vec_res)
    (win_sum,) = _exchange_wait("scatter_w_in_stage2_wait", win_stage2, loss_row)
    res["w_in"] = _adamw("adamw_w_in", win_sum.reshape(1, D, NIN // NDEV), w_in[0], m_w_in[0], v_w_in[0], 256)
    for n, vals in res.items():
        for dst, val in zip((G, DW, NM, NV), vals):
            dst[n] = val
    for dst in (G, DW, NM, NV):
        for n in names:
            dst[n] = dst[n].reshape(shapes[n])
    loss = loss_row[0, 0]

    return (loss, grad_x, *[G[n] for n in names], *[DW[n] for n in names],
            *[NM[n] for n in names], *[NV[n] for n in names])
```

```python
import functools

import jax
import jax.numpy as jnp
from jax import lax
from jax.experimental import pallas as pl
from jax.experimental.pallas import tpu as pltpu

F32 = jnp.float32
BF16 = jnp.bfloat16

D = 2048
DL = 1024
DC = 1024
NIN = 5120
NMETA = 16
SEQ = 2048
T = NMETA + SEQ
TP = 2176
TM = 544
CB = 256
NCB = DL // CB
R = 16
KW = 31
KWP = 32
LW = 4
LRU_C = 8.0
EPS = 1e-6
NDEV = 8

ADAM_LR = 0.001
ADAM_B1 = 0.9
ADAM_B2 = 0.999
ADAM_EPS = 1e-08
ADAM_WD = 0.01
ADAM_STEP = 10

VMEM_LIMIT = 56 * 1024 * 1024


def _cparams():
    return pltpu.CompilerParams(vmem_limit_bytes=VMEM_LIMIT)


def _sig(x):
    return 1.0 / (1.0 + jnp.exp(-x))


def _expm1_neg(y):
    poly = y * (1.0 + y * (0.5 + y * (1.0 / 6.0 + y * (1.0 / 24.0 + y * (1.0 / 120.0)))))
    return jnp.where(y > -0.1, poly, jnp.exp(y) - 1.0)


def _softplus(x):
    e = jnp.exp(-jnp.abs(x))
    w = 1.0 + e
    l1p = jnp.where(w == 1.0, e, jnp.log(w) * e / (w - 1.0))
    return jnp.maximum(x, 0.0) + l1p


def _row_iota(shape):
    return lax.broadcasted_iota(jnp.int32, shape, 0)


def _fold8(v):
    return v[0:8, :] + v[8:16, :]


_FLIPS = [(k >> 2 & 1, k >> 1 & 1, k & 1) for k in range(1, NDEV)]
_HBM = pl.BlockSpec(memory_space=pltpu.HBM)
_SEM = pl.BlockSpec(memory_space=pltpu.SEMAPHORE)


def _peers():
    x, y, c = lax.axis_index("x"), lax.axis_index("y"), lax.axis_index("c")
    out = []
    for dx, dy, dc in _FLIPS:
        px = 1 - x if dx else x
        py = 1 - y if dy else y
        pc = 1 - c if dc else c
        out.append(((px, py, pc), 4 * px + 2 * py + pc))
    return 4 * x + 2 * y + c, out


def _exchange_start(name, items):
    n = len(items)

    def body(*refs):
        srcs, lands = refs[:n], refs[n:2 * n]
        outs = refs[2 * n:]
        send_sems, recv_sems, local_sems = outs[:n], outs[n:2 * n], outs[2 * n:3 * n]
        token = outs[-1]
        me, peers = _peers()
        for a in range(n):
            src_at, dst_at = items[a][2], items[a][3]
            pltpu.make_async_copy(src_at(srcs[a], me), dst_at(lands[a], me), local_sems[a]).start()
        for a in range(n):
            src_at, dst_at = items[a][2], items[a][3]
            for k, (pos, peer) in enumerate(peers):
                pltpu.make_async_remote_copy(
                    src_ref=src_at(srcs[a], peer), dst_ref=dst_at(lands[a], me),
                    send_sem=send_sems[a].at[k], recv_sem=recv_sems[a].at[k],
                    device_id=pos, device_id_type=pl.DeviceIdType.MESH).start()
        token[...] = jnp.zeros_like(token)

    srcs = [pltpu.with_memory_space_constraint(it[0], pltpu.HBM) for it in items]
    lands = [pltpu.with_memory_space_constraint(lax.empty(it[1].shape, it[1].dtype), pltpu.HBM) for it in items]
    sem7 = pltpu.SemaphoreType.DMA((NDEV - 1,))
    res = pl.pallas_call(
        body, name=name,
        out_shape=([sem7] * (2 * n) + [pltpu.SemaphoreType.DMA(())] * n
                   + [pltpu.HBM(a.shape, a.dtype) for a in srcs] + [pltpu.HBM(a.shape, a.dtype) for a in lands]
                   + [jax.ShapeDtypeStruct((8, 128), F32)]),
        in_specs=[_HBM] * (2 * n),
        out_specs=[_SEM] * (3 * n) + [_HBM] * (2 * n) + [pl.BlockSpec(memory_space=pltpu.VMEM)],
        input_output_aliases={i: 3 * n + i for i in range(2 * n)},
        compiler_params=pltpu.CompilerParams(has_side_effects=pltpu.SideEffectType.DATAFLOW_SIDE_EFFECTING),
    )(*srcs, *lands)
    handles = [dict(send=res[a], recv=res[n + a], local=res[2 * n + a], src=res[3 * n + a], land=res[4 * n + a],
                    src_at=items[a][2], dst_at=items[a][3]) for a in range(n)]
    return handles, res[-1]


def _exchange_wait(name, handles, after):
    n = len(handles)

    def body(*refs):
        srcs, lands = refs[:n], refs[n:2 * n]
        send_sems, recv_sems, local_sems = refs[2 * n:3 * n], refs[3 * n:4 * n], refs[4 * n:5 * n]
        me, peers = _peers()
        for a in range(n):
            src_at, dst_at = handles[a]["src_at"], handles[a]["dst_at"]
            for k, (pos, peer) in enumerate(peers):
                cp = pltpu.make_async_remote_copy(
                    src_ref=src_at(srcs[a], peer), dst_ref=dst_at(lands[a], peer),
                    send_sem=send_sems[a].at[k], recv_sem=recv_sems[a].at[k],
                    device_id=pos, device_id_type=pl.DeviceIdType.MESH)
                cp.wait_send()
                cp.wait_recv()
            pltpu.make_async_copy(src_at(srcs[a], me), dst_at(lands[a], me), local_sems[a]).wait()

    srcs = [hd["src"] for hd in handles]
    lands = [hd["land"] for hd in handles]
    res = pl.pallas_call(
        body, name=name,
        out_shape=[pltpu.HBM(a.shape, a.dtype) for a in srcs] + [pltpu.HBM(a.shape, a.dtype) for a in lands],
        in_specs=[_HBM] * (2 * n) + [_SEM] * (3 * n) + [pl.BlockSpec(memory_space=pl.ANY)],
        out_specs=[_HBM] * (2 * n),
        input_output_aliases={i: i for i in range(2 * n)},
        compiler_params=pltpu.CompilerParams(has_side_effects=pltpu.SideEffectType.DATAFLOW_SIDE_EFFECTING),
    )(*srcs, *lands, *[hd["send"] for hd in handles], *[hd["recv"] for hd in handles],
      *[hd["local"] for hd in handles], after)
    return list(res[n:])


_SIDE = pltpu.SideEffectType.DATAFLOW_SIDE_EFFECTING
_WCOLS = NIN // NDEV


def _win_cols(ref, l):
    return ref.at[:, pl.ds(pl.multiple_of(l * _WCOLS, 128), _WCOLS)]


def _win_routes():
    x, y, c = lax.axis_index("x"), lax.axis_index("y"), lax.axis_index("c")
    pos = [(x, y, 1 - c), (1 - x, y, c), (x, 1 - y, c), (1 - x, 1 - y, c)]
    return 4 * x + 2 * y + c, [(p, 4 * p[0] + 2 * p[1] + p[2]) for p in pos]


def _win_gather_start(shard):
    def body(src, land, send_sems, recv_sems, local_sem, src_thru, land_thru, token):
        me, routes = _win_routes()
        pltpu.make_async_copy(src, _win_cols(land, me), local_sem).start()
        for k, (pos, _) in enumerate(routes):
            pltpu.make_async_remote_copy(src_ref=src, dst_ref=_win_cols(land, me), send_sem=send_sems.at[k],
                                         recv_sem=recv_sems.at[k], device_id=pos,
                                         device_id_type=pl.DeviceIdType.MESH).start()
        token[...] = jnp.zeros_like(token)

    src = pltpu.with_memory_space_constraint(shard, pltpu.HBM)
    land = pltpu.with_memory_space_constraint(lax.empty((D, NIN), BF16), pltpu.HBM)
    sem4 = pltpu.SemaphoreType.DMA((4,))
    res = pl.pallas_call(
        body, name="win_gather_start",
        out_shape=[sem4, sem4, pltpu.SemaphoreType.DMA(()), pltpu.HBM(src.shape, BF16), pltpu.HBM(land.shape, BF16),
                   jax.ShapeDtypeStruct((8, 128), F32)],
        in_specs=[_HBM, _HBM],
        out_specs=[_SEM, _SEM, _SEM, _HBM, _HBM, pl.BlockSpec(memory_space=pltpu.VMEM)],
        input_output_aliases={0: 3, 1: 4},
        compiler_params=pltpu.CompilerParams(has_side_effects=_SIDE),
    )(src, land)
    return dict(send=res[0], recv=res[1], local=res[2], src=res[3], land=res[4]), res[5]


def _win_gather_forward(hd, after):
    def body(land, recv_sems, after_ref, land_thru, fsend_sems, frecv_sems):
        me, routes = _win_routes()
        sibling = routes[0][0]
        for k in (1, 2, 3):
            pos, peer = routes[k]
            piece = _win_cols(land, peer)
            pltpu.make_async_remote_copy(src_ref=piece, dst_ref=piece, send_sem=fsend_sems.at[k - 1],
                                         recv_sem=recv_sems.at[k], device_id=pos,
                                         device_id_type=pl.DeviceIdType.MESH).wait_recv()
            pltpu.make_async_remote_copy(src_ref=piece, dst_ref=piece, send_sem=fsend_sems.at[k - 1],
                                         recv_sem=frecv_sems.at[k - 1], device_id=sibling,
                                         device_id_type=pl.DeviceIdType.MESH).start()

    sem3 = pltpu.SemaphoreType.DMA((3,))
    res = pl.pallas_call(
        body, name="win_gather_forward",
        out_shape=[pltpu.HBM(hd["land"].shape, BF16), sem3, sem3],
        in_specs=[_HBM, _SEM, pl.BlockSpec(memory_space=pl.ANY)],
        out_specs=[_HBM, _SEM, _SEM],
        input_output_aliases={0: 0},
        compiler_params=pltpu.CompilerParams(has_side_effects=_SIDE),
    )(hd["land"], hd["recv"], after)
    return dict(hd, land=res[0], fsend=res[1], frecv=res[2])


def _win_gather_early(hd):
    def body(src, land, recv_sems, local_sem, src_thru, land_thru):
        me, routes = _win_routes()
        pos, sibling = routes[0]
        pltpu.make_async_remote_copy(src_ref=src, dst_ref=_win_cols(land, sibling), send_sem=local_sem,
                                     recv_sem=recv_sems.at[0], device_id=pos,
                                     device_id_type=pl.DeviceIdType.MESH).wait_recv()
        pltpu.make_async_copy(src, _win_cols(land, me), local_sem).wait()

    res = pl.pallas_call(
        body, name="win_gather_early",
        out_shape=[pltpu.HBM(hd["src"].shape, BF16), pltpu.HBM(hd["land"].shape, BF16)],
        in_specs=[_HBM, _HBM, _SEM, _SEM],
        out_specs=[_HBM, _HBM],
        input_output_aliases={0: 0, 1: 1},
        compiler_params=pltpu.CompilerParams(has_side_effects=_SIDE),
    )(hd["src"], hd["land"], hd["recv"], hd["local"])
    return dict(hd, src=res[0], land=res[1])


def _win_gather_wait(hd):
    def body(src, land, send_sems, fsend_sems, frecv_sems, src_thru, land_thru):
        me, routes = _win_routes()
        sib_pos, sibling = routes[0]
        for k, (pos, peer) in enumerate(routes):
            pltpu.make_async_remote_copy(src_ref=src, dst_ref=_win_cols(land, peer), send_sem=send_sems.at[k],
                                         recv_sem=frecv_sems.at[0], device_id=pos,
                                         device_id_type=pl.DeviceIdType.MESH).wait_send()
        for k in (1, 2, 3):
            mine = _win_cols(land, routes[k][1])
            theirs = _win_cols(land, 4 * routes[k][0][0] + 2 * routes[k][0][1] + sib_pos[2])
            cp = pltpu.make_async_remote_copy(src_ref=mine, dst_ref=theirs, send_sem=fsend_sems.at[k - 1],
                                              recv_sem=frecv_sems.at[k - 1], device_id=sib_pos,
                                              device_id_type=pl.DeviceIdType.MESH)
            cp.wait_send()
            cp.wait_recv()

    res = pl.pallas_call(
        body, name="win_gather_wait",
        out_shape=[pltpu.HBM(hd["src"].shape, BF16), pltpu.HBM(hd["land"].shape, BF16)],
        in_specs=[_HBM, _HBM] + [_SEM] * 3,
        out_specs=[_HBM, _HBM],
        input_output_aliases={0: 0, 1: 1},
        compiler_params=pltpu.CompilerParams(has_side_effects=_SIDE),
    )(hd["src"], hd["land"], hd["send"], hd["fsend"], hd["frecv"])
    return res[1]


def _whole(ref, l):
    return ref


def _slot(ref, l):
    return ref.at[l]


def _cols(width):
    def at(ref, l):
        return ref.at[:, pl.ds(pl.multiple_of(l * width, 128), width)]
    return at


def _rows(height):
    def at(ref, l):
        return ref.at[pl.ds(pl.multiple_of(l * height, 8), height), :]
    return at


def _prenorm(h, pre_w):
    def body(h_ref, pw_ref, hn_ref):
        pw = pw_ref[...]

        def chunk(ci, carry):
            r0 = pl.multiple_of(ci * R, R)
            xv = h_ref[pl.ds(r0, R), :]
            ms = jnp.mean(xv * xv, axis=-1, keepdims=True)
            hn_ref[pl.ds(r0, R), :] = (xv * lax.rsqrt(ms + EPS) * pw).astype(BF16)
            return carry
        lax.fori_loop(0, TM // R, chunk, 0, unroll=2)

    row = pl.BlockSpec((TM, D), lambda i: (i, 0))
    return pl.pallas_call(
        body, name="prenorm",
        grid=(TP // TM,),
        in_specs=[row, pl.BlockSpec((1, D), lambda i: (0, 0))],
        out_specs=row,
        out_shape=jax.ShapeDtypeStruct((TP, D), BF16),
        compiler_params=_cparams(),
    )(h, pre_w)


def _inproj_cols(name, shards, hn, w_land, b_in, z_prev):
    nsh = shards.shape[0]

    def body(idx_ref, hn_ref, w_ref, b_ref, *rest):
        z_ref = rest[-2]
        z_ref[...] = jnp.dot(hn_ref[...], w_ref[...], preferred_element_type=F32) + b_ref[...]

    any_spec = pl.BlockSpec(memory_space=pl.ANY)
    in_specs = [pl.BlockSpec((TM, D), lambda j, i, idx: (i, 0)),
                pl.BlockSpec((D, _WCOLS), lambda j, i, idx: (0, idx[j])),
                pl.BlockSpec((1, _WCOLS), lambda j, i, idx: (0, idx[j]))]
    operands = [hn, w_land, b_in]
    aliases = {2: 1}
    if z_prev is not None:
        in_specs.append(any_spec)
        operands.append(z_prev)
        aliases[4] = 0
    return pl.pallas_call(
        body, name=name,
        grid_spec=pltpu.PrefetchScalarGridSpec(
            num_scalar_prefetch=1, grid=(nsh, TP // TM), in_specs=in_specs,
            out_specs=[pl.BlockSpec((TM, _WCOLS), lambda j, i, idx: (i, idx[j])), any_spec]),
        out_shape=[jax.ShapeDtypeStruct((TP, NIN), F32), jax.ShapeDtypeStruct(w_land.shape, w_land.dtype)],
        input_output_aliases=aliases,
        compiler_params=_cparams(),
    )(shards, *operands)


def _gate_values(ga, gx, xc, sp8):
    r = _sig(ga)
    i = _sig(gx)
    log_a = -(r * sp8)
    a = jnp.exp(log_a)
    mult = jnp.sqrt(-_expm1_neg(2.0 * log_a))
    return r, i, a, mult


def _lru_fwd(z, conv_w, conv_b, wa_g, b_a, wx_g, b_x, lam):
    def body(x_ref, g_ref, cw_ref, cb_ref, wa_ref, ba_ref, wx_ref, bx_ref, lam_ref,
             y_ref, xc_ref, hs_ref, ga_s, gx_s):
        taps = [cw_ref[k:k + 1, :] for k in range(LW)]
        cb = cb_ref[...]

        def conv_chunk(ci, carry):
            r0 = pl.multiple_of(ci * R, R)
            cur = x_ref[pl.ds(r0, R), :]
            p0 = pl.multiple_of(jnp.maximum(r0 - 8, 0), 8)
            prev = jnp.where(ci > 0, x_ref[pl.ds(p0, 8), :], 0.0)
            buf = jnp.concatenate([prev, cur], axis=0)
            acc = cur * taps[LW - 1] + cb
            for s in range(1, LW):
                acc = acc + pltpu.roll(buf, s, 0)[8:8 + R, :] * taps[LW - 1 - s]
            xc_ref[pl.ds(r0, R), :] = acc
            return carry
        lax.fori_loop(0, TP // R, conv_chunk, 0)

        def gate_chunk(ci, carry):
            r0 = pl.multiple_of(ci * TM, TM)
            xb = xc_ref[pl.ds(r0, TM), :].astype(BF16)
            ga_s[pl.ds(r0, TM), :] = jnp.dot(xb, wa_ref[...], preferred_element_type=F32) + ba_ref[...]
            gx_s[pl.ds(r0, TM), :] = jnp.dot(xb, wx_ref[...], preferred_element_type=F32) + bx_ref[...]
            return carry
        lax.fori_loop(0, TP // TM, gate_chunk, 0)

        sp8 = LRU_C * _softplus(-lam_ref[...])
        row = _row_iota((R, CB))

        def scan_chunk(ci, hprev):
            r0 = pl.multiple_of(ci * R, R)
            xc = xc_ref[pl.ds(r0, R), :]
            _, i, a, mult = _gate_values(ga_s[pl.ds(r0, R), :], gx_s[pl.ds(r0, R), :], xc, sp8)
            u = mult * (i * xc)
            k = 1
            while k < R:
                m = row >= k
                u = jnp.where(m, a * pltpu.roll(u, k, 0) + u, u)
                a = jnp.where(m, a * pltpu.roll(a, k, 0), a)
                k *= 2
            hv = u + a * hprev
            hs_ref[pl.ds(r0, R), :] = hv
            g = g_ref[pl.ds(r0, R), :]
            y_ref[pl.ds(r0, R), :] = (hv * (g * _sig(g))).astype(BF16)
            return jnp.sum(jnp.where(row == R - 1, hv, 0.0), axis=0, keepdims=True)
        lax.fori_loop(0, TP // R, scan_chunk, jnp.zeros((1, CB), F32))

    col = lambda off: pl.BlockSpec((TP, CB), lambda j: (0, off + j))
    vec = pl.BlockSpec((1, CB), lambda j: (0, j))
    wsp = pl.BlockSpec((None, CB, CB), lambda j: (j, 0, 0))
    return pl.pallas_call(
        body, name="lru_fwd",
        grid=(NCB,),
        in_specs=[col(0), col(NCB), pl.BlockSpec((LW, CB), lambda j: (0, j)), vec, wsp, vec, wsp, vec, vec],
        out_specs=[col(0), col(0), col(0)],
        out_shape=[jax.ShapeDtypeStruct((TP, DL), BF16), jax.ShapeDtypeStruct((TP, DL), F32),
                   jax.ShapeDtypeStruct((TP, DL), F32)],
        scratch_shapes=[pltpu.VMEM((TP, CB), F32), pltpu.VMEM((TP, CB), F32)],
        compiler_params=_cparams(),
    )(z, z, conv_w, conv_b, wa_g, b_a, wx_g, b_x, lam)


def _conf_fwd_conv(z, dw_w, dw_b):
    def body(u1_ref, u2_ref, w_ref, b_ref, vc_ref, vs):
        vs[pl.ds(0, KWP), :] = jnp.zeros((KWP, CB), F32)

        def glu_chunk(ci, carry):
            r0 = pl.multiple_of(ci * R, R)
            vs[pl.ds(KWP + r0, R), :] = u1_ref[pl.ds(r0, R), :] * _sig(u2_ref[pl.ds(r0, R), :])
            return carry
        lax.fori_loop(0, TP // R, glu_chunk, 0)

        bias = b_ref[...]

        def conv_chunk(ci, carry):
            r0 = pl.multiple_of(ci * R, R)
            buf = vs[pl.ds(r0, KWP + R), :]
            acc = jnp.zeros((R, CB), F32) + bias
            for rr in range(8):
                rolled = buf if rr == 0 else pltpu.roll(buf, rr, 0)
                for q in range(4):
                    s = 8 * q + rr
                    if s > KW - 1:
                        continue
                    k = KW - 1 - s
                    acc = acc + rolled[KWP - 8 * q:KWP - 8 * q + R, :] * w_ref[k:k + 1, :]
            vc_ref[pl.ds(r0, R), :] = acc
            return carry
        lax.fori_loop(0, TP // R, conv_chunk, 0)

    return pl.pallas_call(
        body, name="conf_fwd_conv",
        grid=(NCB,),
        in_specs=[pl.BlockSpec((TP, CB), lambda j: (0, 2 * NCB + j)),
                  pl.BlockSpec((TP, CB), lambda j: (0, 3 * NCB + j)),
                  pl.BlockSpec((KWP, CB), lambda j: (0, j)),
                  pl.BlockSpec((1, CB), lambda j: (0, j))],
        out_specs=pl.BlockSpec((TP, CB), lambda j: (0, j)),
        out_shape=jax.ShapeDtypeStruct((TP, DC), F32),
        scratch_shapes=[pltpu.VMEM((TP + KWP, CB), F32)],
        compiler_params=_cparams(),
    )(z, z, dw_w, dw_b)


def _ln_chunk(vc, lw, lb):
    mu = jnp.mean(vc, axis=-1, keepdims=True)
    xm = vc - mu
    var = jnp.mean(xm * xm, axis=-1, keepdims=True)
    rstd = lax.rsqrt(var + EPS)
    xhat = xm * rstd
    return xhat, rstd, xhat * lw + lb


def _conf_fwd_proj(vc, z, ln_w, ln_b, pw_w, pw_b):
    def body(vc_ref, g_ref, lw_ref, lb_ref, w_ref, b_ref, y_ref, p_ref, s_s):
        lw, lb = lw_ref[...], lb_ref[...]

        def ln_chunk(ci, carry):
            r0 = pl.multiple_of(ci * R, R)
            for half in range(2):
                rr = r0 + 8 * half
                _, _, ln = _ln_chunk(vc_ref[pl.ds(rr, 8), :], lw, lb)
                p_ref[pl.ds(rr, 8), :] = ln * _sig(ln)
            s_s[pl.ds(r0, R), :] = p_ref[pl.ds(r0, R), :].astype(BF16)
            return carry
        lax.fori_loop(0, TM // R, ln_chunk, 0, unroll=2)

        p_ref[...] = jnp.dot(s_s[...], w_ref[...], preferred_element_type=F32) + b_ref[...]

        def out_chunk(ci, carry):
            r0 = pl.multiple_of(ci * R, R)
            g = g_ref[pl.ds(r0, R), :]
            y_ref[pl.ds(r0, R), :] = (p_ref[pl.ds(r0, R), :] * (g * _sig(g))).astype(BF16)
            return carry
        lax.fori_loop(0, TM // R, out_chunk, 0)

    row = pl.BlockSpec((TM, DC), lambda i: (i, 0))
    vec = pl.BlockSpec((1, DC), lambda i: (0, 0))
    return pl.pallas_call(
        body, name="conf_fwd_proj",
        grid=(TP // TM,),
        in_specs=[row, pl.BlockSpec((TM, DC), lambda i: (i, 4)), vec, vec,
                  pl.BlockSpec((DC, DC), lambda i: (0, 0)), vec],
        out_specs=[row, row],
        out_shape=[jax.ShapeDtypeStruct((TP, DC), BF16), jax.ShapeDtypeStruct((TP, DC), F32)],
        scratch_shapes=[pltpu.VMEM((TM, DC), BF16)],
        compiler_params=_cparams(),
    )(vc, z, ln_w, ln_b, pw_w, pw_b)


def _outproj_loss(ylru, yconf, w_out, h, tgt, post_w):
    def body(yl_ref, yc_ref, w_ref, h_ref, t_ref, pw_ref, dout_ref, dy_ref, loss_ref, dpw_ref, y_s):
        i = pl.program_id(0)
        k = pl.program_id(1)

        @pl.when(k == 0)
        def _():
            y_s[...] = jnp.dot(yl_ref[...], w_ref[...], preferred_element_type=F32)

        @pl.when(k == 1)
        def _():
            y_s[...] += jnp.dot(yc_ref[...], w_ref[...], preferred_element_type=F32)

        @pl.when(jnp.logical_and(i == 0, k == 1))
        def _():
            loss_ref[...] = jnp.zeros_like(loss_ref)
            dpw_ref[...] = jnp.zeros_like(dpw_ref)

        @pl.when(k == 1)
        def _():
            pw = pw_ref[...]
            row = _row_iota((8, D))

            def chunk(ci, carry):
                r0 = pl.multiple_of(ci * 8, 8)
                yv = y_s[pl.ds(r0, 8), :]
                rs = lax.rsqrt(jnp.mean(yv * yv, axis=-1, keepdims=True) + EPS)
                grow = row + (i * TM + r0)
                valid = jnp.logical_and(grow >= NMETA, grow < T)
                yn = yv * rs
                err = jnp.where(valid, h_ref[pl.ds(r0, 8), :] + yn * pw - t_ref[pl.ds(r0, 8), :], 0.0)
                loss_ref[...] += err * err
                d_rn = err * (1.0 / D)
                dout_ref[pl.ds(r0, 8), :] = d_rn
                dpw_ref[...] += d_rn * yn
                gw = d_rn * pw
                dot = jnp.mean(gw * yv, axis=-1, keepdims=True)
                dy_ref[pl.ds(r0, 8), :] = (rs * gw - yv * (rs * rs * rs * dot)).astype(BF16)
                return carry
            lax.fori_loop(0, TM // 8, chunk, 0, unroll=4)

    row = pl.BlockSpec((TM, D), lambda i, k: (i, 0))
    half = pl.BlockSpec((TM, DL), lambda i, k: (i, 0))
    acc = pl.BlockSpec((8, D), lambda i, k: (0, 0))
    return pl.pallas_call(
        body, name="outproj_loss",
        grid=(TP // TM, 2),
        in_specs=[half, half, pl.BlockSpec((DL, D), lambda i, k: (k, 0)), row, row,
                  pl.BlockSpec((1, D), lambda i, k: (0, 0))],
        out_specs=[row, row, acc, acc],
        out_shape=[jax.ShapeDtypeStruct((TP, D), F32), jax.ShapeDtypeStruct((TP, D), BF16),
                   jax.ShapeDtypeStruct((8, D), F32), jax.ShapeDtypeStruct((8, D), F32)],
        scratch_shapes=[pltpu.VMEM((TM, D), F32)],
        compiler_params=_cparams(),
    )(ylru, yconf, w_out, h, tgt, post_w)


_NT = (((1,), (1,)), ((), ()))
_TN = (((0,), (0,)), ((), ()))


def _outproj_bwd(dy, ylru, yconf, w_out):
    def body(dy_ref, yl_ref, yc_ref, w_ref, dycat_ref, dw_ref):
        j = pl.program_id(0)
        dyv = dy_ref[...]
        dycat_ref[...] = lax.dot_general(dyv, w_ref[...], _NT, preferred_element_type=F32)

        @pl.when(j < NCB)
        def _():
            dw_ref[...] = lax.dot_general(yl_ref[...], dyv, _TN, preferred_element_type=F32).astype(BF16)

        @pl.when(j >= NCB)
        def _():
            dw_ref[...] = lax.dot_general(yc_ref[...], dyv, _TN, preferred_element_type=F32).astype(BF16)

    return pl.pallas_call(
        body, name="outproj_bwd",
        grid=(2 * NCB,),
        in_specs=[pl.BlockSpec((TP, D), lambda j: (0, 0)),
                  pl.BlockSpec((TP, CB), lambda j: (0, jnp.minimum(j, NCB - 1))),
                  pl.BlockSpec((TP, CB), lambda j: (0, jnp.maximum(j - NCB, 0))),
                  pl.BlockSpec((CB, D), lambda j: (j, 0))],
        out_specs=[pl.BlockSpec((TP, CB), lambda j: (0, j)), pl.BlockSpec((CB, D), lambda j: (j, 0))],
        out_shape=[jax.ShapeDtypeStruct((TP, D), F32), jax.ShapeDtypeStruct((D, D), BF16)],
        compiler_params=_cparams(),
    )(dy, ylru, yconf, w_out)


def _conf_bwd_proj(dycat, p, z, vc, ln_w, ln_b, pw_w):
    def body(dy_ref, p_ref, g_ref, vc_ref, lw_ref, lb_ref, w_ref,
             dvc_ref, dgc_ref, dpw_ref, vecs_ref, dp_s, s_s, ds_s):
        i = pl.program_id(0)
        lw, lb = lw_ref[...], lb_ref[...]

        @pl.when(i == 0)
        def _():
            dpw_ref[...] = jnp.zeros_like(dpw_ref)
            vecs_ref[...] = jnp.zeros_like(vecs_ref)

        def pre_chunk(ci, carry):
            r0 = pl.multiple_of(ci * R, R)
            for half in range(2):
                rr = r0 + 8 * half
                dyv = dy_ref[pl.ds(rr, 8), :]
                g = g_ref[pl.ds(rr, 8), :]
                sg = _sig(g)
                dp = dyv * (g * sg)
                dg = dyv * p_ref[pl.ds(rr, 8), :] * (sg * (1.0 + g * (1.0 - sg)))
                vecs_ref[0:8, :] += dp
                vecs_ref[8:16, :] += dg
                ds_s[pl.ds(rr, 8), :] = dp
                dvc_ref[pl.ds(rr, 8), :] = dg
            dp_s[pl.ds(r0, R), :] = ds_s[pl.ds(r0, R), :].astype(BF16)
            dgc_ref[pl.ds(r0, R), :] = dvc_ref[pl.ds(r0, R), :].astype(BF16)
            for half in range(2):
                rr = r0 + 8 * half
                _, _, ln = _ln_chunk(vc_ref[pl.ds(rr, 8), :], lw, lb)
                ds_s[pl.ds(rr, 8), :] = ln * _sig(ln)
            s_s[pl.ds(r0, R), :] = ds_s[pl.ds(r0, R), :].astype(BF16)
            return carry
        lax.fori_loop(0, TM // R, pre_chunk, 0, unroll=2)

        dpb = dp_s[...]
        ds_s[...] = lax.dot_general(dpb, w_ref[...], _NT, preferred_element_type=F32)
        dpw_ref[...] += lax.dot_general(s_s[...], dpb, _TN, preferred_element_type=F32)

        def post_chunk(ci, carry):
            r0 = pl.multiple_of(ci * 8, 8)
            xhat, rstd, ln = _ln_chunk(vc_ref[pl.ds(r0, 8), :], lw, lb)
            sl = _sig(ln)
            dln = ds_s[pl.ds(r0, 8), :] * (sl * (1.0 + ln * (1.0 - sl)))
            vecs_ref[16:24, :] += dln * xhat
            vecs_ref[24:32, :] += dln
            dxh = dln * lw
            m1 = jnp.mean(dxh, axis=-1, keepdims=True)
            m2 = jnp.mean(dxh * xhat, axis=-1, keepdims=True)
            dvc_ref[pl.ds(r0, 8), :] = rstd * (dxh - m1 - xhat * m2)
            return carry
        lax.fori_loop(0, TM // 8, post_chunk, 0, unroll=4)

    row = pl.BlockSpec((TM, DC), lambda i: (i, 0))
    vec = pl.BlockSpec((1, DC), lambda i: (0, 0))
    return pl.pallas_call(
        body, name="conf_bwd_proj",
        grid=(TP // TM,),
        in_specs=[pl.BlockSpec((TM, DC), lambda i: (i, 1)), row, pl.BlockSpec((TM, DC), lambda i: (i, 4)), row,
                  vec, vec, pl.BlockSpec((DC, DC), lambda i: (0, 0))],
        out_specs=[row, row, pl.BlockSpec((DC, DC), lambda i: (0, 0)), pl.BlockSpec((32, DC), lambda i: (0, 0))],
        out_shape=[jax.ShapeDtypeStruct((TP, DC), F32), jax.ShapeDtypeStruct((TP, DC), BF16),
                   jax.ShapeDtypeStruct((DC, DC), F32), jax.ShapeDtypeStruct((32, DC), F32)],
        scratch_shapes=[pltpu.VMEM((TM, DC), BF16), pltpu.VMEM((TM, DC), BF16), pltpu.VMEM((TM, DC), F32)],
        compiler_params=_cparams(),
    )(dycat, p, z, vc, ln_w, ln_b, pw_w)


def _conf_bwd_conv(dvc, z, dw_w):
    def body(dvc_ref, u1_ref, u2_ref, w_ref, du_ref, dw_ref, vecs_ref, vs, dvs):
        vs[pl.ds(0, KWP), :] = jnp.zeros((KWP, CB), F32)
        dvs[pl.ds(TP, KWP), :] = jnp.zeros((KWP, CB), F32)
        dw_ref[...] = jnp.zeros_like(dw_ref)
        vecs_ref[...] = jnp.zeros_like(vecs_ref)

        def fill_chunk(ci, carry):
            r0 = pl.multiple_of(ci * R, R)
            vs[pl.ds(KWP + r0, R), :] = u1_ref[pl.ds(r0, R), :] * _sig(u2_ref[pl.ds(r0, R), :])
            dv = dvc_ref[pl.ds(r0, R), :]
            dvs[pl.ds(r0, R), :] = dv
            vecs_ref[0:8, :] += _fold8(dv)
            return carry
        lax.fori_loop(0, TP // R, fill_chunk, 0)

        def conv_chunk(ci, carry):
            r0 = pl.multiple_of(ci * R, R)
            vbuf = vs[pl.ds(r0, KWP + R), :]
            dbuf = dvs[pl.ds(r0, KWP + R), :]
            dcur = dbuf[0:R, :]
            dv = jnp.zeros((R, CB), F32)
            for rr in range(8):
                vroll = vbuf if rr == 0 else pltpu.roll(vbuf, rr, 0)
                droll = dbuf if rr == 0 else pltpu.roll(dbuf, KWP + R - rr, 0)
                for q in range(4):
                    s = 8 * q + rr
                    if s > KW - 1:
                        continue
                    k = KW - 1 - s
                    dv = dv + droll[8 * q:8 * q + R, :] * w_ref[k:k + 1, :]
                    dw_ref[8 * k:8 * k + 8, :] += _fold8(dcur * vroll[KWP - 8 * q:KWP - 8 * q + R, :])
            u1 = u1_ref[pl.ds(r0, R), :]
            sg = _sig(u2_ref[pl.ds(r0, R), :])
            du1 = dv * sg
            du2 = dv * u1 * (sg * (1.0 - sg))
            du_ref[0, pl.ds(r0, R), :] = du1.astype(BF16)
            du_ref[1, pl.ds(r0, R), :] = du2.astype(BF16)
            vecs_ref[8:16, :] += _fold8(du1)
            vecs_ref[16:24, :] += _fold8(du2)
            return carry
        lax.fori_loop(0, TP // R, conv_chunk, 0)

    blk = pl.BlockSpec((TP, CB), lambda j: (0, j))
    return pl.pallas_call(
        body, name="conf_bwd_conv",
        grid=(NCB,),
        in_specs=[blk, pl.BlockSpec((TP, CB), lambda j: (0, 2 * NCB + j)),
                  pl.BlockSpec((TP, CB), lambda j: (0, 3 * NCB + j)), pl.BlockSpec((KWP, CB), lambda j: (0, j))],
        out_specs=[pl.BlockSpec((2, TP, CB), lambda j: (0, 0, j)), pl.BlockSpec((8 * KWP, CB), lambda j: (0, j)),
                   pl.BlockSpec((24, CB), lambda j: (0, j))],
        out_shape=[jax.ShapeDtypeStruct((2, TP, DC), BF16),
                   jax.ShapeDtypeStruct((8 * KWP, DC), F32), jax.ShapeDtypeStruct((24, DC), F32)],
        scratch_shapes=[pltpu.VMEM((TP + KWP, CB), F32), pltpu.VMEM((TP + KWP, CB), F32)],
        compiler_params=_cparams(),
    )(dvc, z, z, dw_w)


def _lru_bwd(dycat, z, xc, hs, conv_w, wa_g, b_a, wx_g, b_x, lam):
    NV = 6

    def body(dy_ref, x_ref, g_ref, xc_ref, hs_ref, cw_ref, wa_ref, ba_ref, wx_ref, bx_ref, lam_ref,
             dzl_ref, dwa_ref, dwx_ref, dcw_ref, vecs_ref, ga_s, gx_s, dxc_s):
        vecs_ref[...] = jnp.zeros_like(vecs_ref)
        dcw_ref[...] = jnp.zeros_like(dcw_ref)
        dxc_s[pl.ds(TP, 8), :] = jnp.zeros((8, CB), F32)

        def gate_chunk(ci, carry):
            r0 = pl.multiple_of(ci * TM, TM)
            xb = xc_ref[pl.ds(r0, TM), :].astype(BF16)
            ga_s[pl.ds(r0, TM), :] = jnp.dot(xb, wa_ref[...], preferred_element_type=F32) + ba_ref[...]
            gx_s[pl.ds(r0, TM), :] = jnp.dot(xb, wx_ref[...], preferred_element_type=F32) + bx_ref[...]
            return carry
        lax.fori_loop(0, TP // TM, gate_chunk, 0)

        sp8 = LRU_C * _softplus(-lam_ref[...])
        row = _row_iota((R, CB))
        nchunk = TP // R

        def scan_chunk(cj, carry):
            a_next, lam_next = carry
            ci = nchunk - 1 - cj
            r0 = pl.multiple_of(ci * R, R)
            dyv = dy_ref[pl.ds(r0, R), :]
            g = g_ref[pl.ds(r0, R), :]
            hv = hs_ref[pl.ds(r0, R), :]
            xc = xc_ref[pl.ds(r0, R), :]
            sg = _sig(g)
            dgl = dyv * hv * (sg * (1.0 + g * (1.0 - sg)))
            dzl_ref[1, pl.ds(r0, R), :] = dgl.astype(BF16)
            vecs_ref[0:8, :] += _fold8(dgl)
            dhs = dyv * (g * sg)
            r, i, a, mult = _gate_values(ga_s[pl.ds(r0, R), :], gx_s[pl.ds(r0, R), :], xc, sp8)
            b = jnp.where(row == R - 1, a_next, pltpu.roll(a, R - 1, 0))
            lv = dhs
            k = 1
            while k < R:
                m = row < R - k
                lv = jnp.where(m, lv + b * pltpu.roll(lv, R - k, 0), lv)
                b = jnp.where(m, b * pltpu.roll(b, R - k, 0), b)
                k *= 2
            lv = lv + b * lam_next
            p0 = pl.multiple_of(jnp.maximum(r0 - 8, 0), 8)
            hprev8 = jnp.where(ci > 0, hs_ref[pl.ds(p0, 8), :], 0.0)
            hprev = pltpu.roll(jnp.concatenate([hprev8, hv], axis=0), 1, 0)[8:8 + R, :]
            da = lv * hprev
            ixc = i * xc
            dmult = lv * ixc
            di = lv * mult * xc
            dxc_s[pl.ds(r0, R), :] = lv * mult * i
            a2 = a * a
            dlog_a = da * a - dmult * a2 / mult
            vecs_ref[32:40, :] += _fold8(dlog_a * r)
            dga = -(dlog_a * sp8) * r * (1.0 - r)
            dgx = di * i * (1.0 - i)
            ga_s[pl.ds(r0, R), :] = dga
            gx_s[pl.ds(r0, R), :] = dgx
            vecs_ref[16:24, :] += _fold8(dga)
            vecs_ref[24:32, :] += _fold8(dgx)
            a_first = jnp.sum(jnp.where(row == 0, a, 0.0), axis=0, keepdims=True)
            l_first = jnp.sum(jnp.where(row == 0, lv, 0.0), axis=0, keepdims=True)
            return a_first, l_first
        lax.fori_loop(0, nchunk, scan_chunk, (jnp.zeros((1, CB), F32), jnp.zeros((1, CB), F32)))

        dwa_ref[...] = jnp.zeros_like(dwa_ref)
        dwx_ref[...] = jnp.zeros_like(dwx_ref)

        def mm_chunk(ci, carry):
            r0 = pl.multiple_of(ci * TM, TM)
            xb = xc_ref[pl.ds(r0, TM), :].astype(BF16)
            dgab = ga_s[pl.ds(r0, TM), :].astype(BF16)
            dgxb = gx_s[pl.ds(r0, TM), :].astype(BF16)
            dxc_s[pl.ds(r0, TM), :] += (lax.dot_general(dgab, wa_ref[...], _NT, preferred_element_type=F32)
                                        + lax.dot_general(dgxb, wx_ref[...], _NT, preferred_element_type=F32))
            dwa_ref[...] += lax.dot_general(xb, dgab, _TN, preferred_element_type=F32)
            dwx_ref[...] += lax.dot_general(xb, dgxb, _TN, preferred_element_type=F32)
            return carry
        lax.fori_loop(0, TP // TM, mm_chunk, 0)

        taps = [cw_ref[k:k + 1, :] for k in range(LW)]

        def conv_chunk(ci, carry):
            r0 = pl.multiple_of(ci * R, R)
            dbuf = dxc_s[pl.ds(r0, R + 8), :]
            dcur = dbuf[0:R, :]
            p0 = pl.multiple_of(jnp.maximum(r0 - 8, 0), 8)
            xprev = jnp.where(ci > 0, x_ref[pl.ds(p0, 8), :], 0.0)
            xbuf = jnp.concatenate([xprev, x_ref[pl.ds(r0, R), :]], axis=0)
            dxl = dcur * taps[LW - 1]
            dcw_ref[8 * (LW - 1):8 * LW, :] += _fold8(dcur * xbuf[8:8 + R, :])
            for s in range(1, LW):
                k = LW - 1 - s
                dxl = dxl + pltpu.roll(dbuf, R + 8 - s, 0)[0:R, :] * taps[k]
                dcw_ref[8 * k:8 * k + 8, :] += _fold8(dcur * pltpu.roll(xbuf, s, 0)[8:8 + R, :])
            dzl_ref[0, pl.ds(r0, R), :] = dxl.astype(BF16)
            vecs_ref[8:16, :] += _fold8(dxl)
            vecs_ref[40:48, :] += _fold8(dcur)
            return carry
        lax.fori_loop(0, TP // R, conv_chunk, 0)
        vecs_ref[32:40, :] = vecs_ref[32:40, :] * (LRU_C * _sig(-lam_ref[...]))

    col = lambda off: pl.BlockSpec((TP, CB), lambda j: (0, off + j))
    vec = pl.BlockSpec((1, CB), lambda j: (0, j))
    wsp = pl.BlockSpec((None, CB, CB), lambda j: (j, 0, 0))
    return pl.pallas_call(
        body, name="lru_bwd",
        grid=(NCB,),
        in_specs=[col(0), col(0), col(NCB), col(0), col(0), pl.BlockSpec((LW, CB), lambda j: (0, j)),
                  wsp, vec, wsp, vec, vec],
        out_specs=[pl.BlockSpec((2, TP, CB), lambda j: (0, 0, j)), wsp, wsp,
                   pl.BlockSpec((8 * LW, CB), lambda j: (0, j)), pl.BlockSpec((8 * NV, CB), lambda j: (0, j))],
        out_shape=[jax.ShapeDtypeStruct((2, TP, DL), BF16),
                   jax.ShapeDtypeStruct((NCB, CB, CB), F32), jax.ShapeDtypeStruct((NCB, CB, CB), F32),
                   jax.ShapeDtypeStruct((8 * LW, DL), F32), jax.ShapeDtypeStruct((8 * NV, DL), F32)],
        scratch_shapes=[pltpu.VMEM((TP, CB), F32), pltpu.VMEM((TP, CB), F32), pltpu.VMEM((TP + 8, CB), F32)],
        compiler_params=_cparams(),
    )(dycat, z, z, xc, hs, conv_w, wa_g, b_a, wx_g, b_x, lam)


def _dz_section(sec, dzl_ref, dzc_ref, dgc_ref, use):
    @pl.when(sec < 2)
    def _():
        use(dzl_ref)

    @pl.when(jnp.logical_and(sec >= 2, sec < 4))
    def _():
        use(dzc_ref)

    @pl.when(sec == 4)
    def _():
        use(dgc_ref)


def _dz_specs(rows, index):
    return [pl.BlockSpec((None, rows, 1024), lambda a, b: (jnp.minimum(index(a, b)[1], 1), index(a, b)[0], 0)),
            pl.BlockSpec((None, rows, 1024), lambda a, b: (jnp.clip(index(a, b)[1] - 2, 0, 1), index(a, b)[0], 0)),
            pl.BlockSpec((rows, 1024), lambda a, b: (index(a, b)[0], 0))]


def _inproj_wgrad(name, hn, dzs):
    KB = 512
    nsec = dzs.shape[0]

    def body(hn_ref, dz_ref, dw_ref):
        dw_ref[...] = lax.dot_general(hn_ref[...], dz_ref[...], _TN, preferred_element_type=F32).astype(BF16)

    return pl.pallas_call(
        body, name=name,
        grid=(nsec, D // KB),
        in_specs=[pl.BlockSpec((TP, KB), lambda n, kb: (0, kb)),
                  pl.BlockSpec((None, TP, 1024), lambda n, kb: (n, 0, 0))],
        out_specs=pl.BlockSpec((KB, 1024), lambda n, kb: (kb, n)),
        out_shape=jax.ShapeDtypeStruct((D, nsec * 1024), BF16),
        compiler_params=_cparams(),
    )(hn, dzs)


def _sum_win_parts(parts_a, parts_b, parts_c):
    RB = 64

    def body(a_ref, b_ref, c_ref, o_ref):
        def chunk(ci, carry):
            r0 = pl.multiple_of(ci * R, R)
            for ref, base, ncol in ((a_ref, 0, 2048), (b_ref, 2048, 2048), (c_ref, 4096, 1024)):
                for c0 in range(0, ncol, 512):
                    acc = ref[0, pl.ds(r0, R), c0:c0 + 512].astype(F32)
                    for sidx in range(1, NDEV):
                        acc = acc + ref[sidx, pl.ds(r0, R), c0:c0 + 512].astype(F32)
                    o_ref[pl.ds(r0, R), base + c0:base + c0 + 512] = acc.astype(BF16)
            return carry
        lax.fori_loop(0, RB // R, chunk, 0)

    spec = lambda ncol: pl.BlockSpec((NDEV, RB, ncol), lambda i: (0, i, 0))
    return pl.pallas_call(
        body, name="sum_win_parts",
        grid=(D // NDEV // RB,),
        in_specs=[spec(2048), spec(2048), spec(1024)],
        out_specs=pl.BlockSpec((RB, NIN), lambda i: (i, 0)),
        out_shape=jax.ShapeDtypeStruct((D // NDEV, NIN), BF16),
        compiler_params=_cparams(),
    )(parts_a, parts_b, parts_c)


def _inproj_bwd(dzl, dzc, dgc, w_in, h, dout, pre_w):
    nsec = NIN // 1024

    def body(dzl_ref, dzc_ref, dgc_ref, w_ref, h_ref, dout_ref, pw_ref, dh_ref, dpw_ref, acc_s):
        i = pl.program_id(0)
        s = pl.program_id(1)

        @pl.when(s == 0)
        def _():
            acc_s[...] = jnp.zeros_like(acc_s)

        def use(dz_ref):
            acc_s[...] += lax.dot_general(dz_ref[...], w_ref[...], _NT, preferred_element_type=F32)
        _dz_section(s, dzl_ref, dzc_ref, dgc_ref, use)

        @pl.when(jnp.logical_and(i == 0, s == nsec - 1))
        def _():
            dpw_ref[...] = jnp.zeros_like(dpw_ref)

        @pl.when(s == nsec - 1)
        def _():
            pw = pw_ref[...]

            def chunk(ci, carry):
                r0 = pl.multiple_of(ci * 8, 8)
                hv = h_ref[pl.ds(r0, 8), :]
                dhn = acc_s[pl.ds(r0, 8), :]
                rs = lax.rsqrt(jnp.mean(hv * hv, axis=-1, keepdims=True) + EPS)
                dpw_ref[...] += dhn * (hv * rs)
                gw = dhn * pw
                dot = jnp.mean(gw * hv, axis=-1, keepdims=True)
                dh_ref[pl.ds(r0, 8), :] = rs * gw - hv * (rs * rs * rs * dot) + dout_ref[pl.ds(r0, 8), :]
                return carry
            lax.fori_loop(0, TM // 8, chunk, 0, unroll=4)

    row = pl.BlockSpec((TM, D), lambda i, s: (i, 0))
    return pl.pallas_call(
        body, name="inproj_bwd",
        grid=(TP // TM, nsec),
        in_specs=_dz_specs(TM, lambda i, s: (i, s)) + [
            pl.BlockSpec((D, 1024), lambda i, s: (0, s)), row, row, pl.BlockSpec((1, D), lambda i, s: (0, 0))],
        out_specs=[row, pl.BlockSpec((8, D), lambda i, s: (0, 0))],
        out_shape=[jax.ShapeDtypeStruct((TP, D), F32), jax.ShapeDtypeStruct((8, D), F32)],
        scratch_shapes=[pltpu.VMEM((TM, D), F32)],
        compiler_params=_cparams(),
    )(dzl, dzc, dgc, w_in, h, dout, pre_w)


def _adamw(name, parts, w, m, v, block_rows):
    rows, cols = w.shape
    nparts = parts.shape[0]
    cw = cols if cols <= 640 else 512

    def body(p_ref, w_ref, m_ref, v_ref, g_ref, d_ref, nm_ref, nv_ref):
        def chunk(ci, carry):
            r0 = pl.multiple_of(ci * R, R)
            for c0 in range(0, cols, cw):
                at = (pl.ds(r0, R), slice(c0, c0 + cw))
                g = p_ref[(0,) + at].astype(F32)
                for sidx in range(1, nparts):
                    g = g + p_ref[(sidx,) + at].astype(F32)
                delta, mv, vv = _adam_math(g, w_ref[at], m_ref[at], v_ref[at])
                g_ref[at] = g
                nm_ref[at] = mv
                nv_ref[at] = vv
                d_ref[at] = delta
            return carry
        lax.fori_loop(0, block_rows // R, chunk, 0)

    blk = pl.BlockSpec((block_rows, cols), lambda i: (i, 0))
    shp = jax.ShapeDtypeStruct((rows, cols), F32)
    return pl.pallas_call(
        body, name=name,
        grid=(rows // block_rows,),
        in_specs=[pl.BlockSpec((nparts, block_rows, cols), lambda i: (0, i, 0)), blk, blk, blk],
        out_specs=[blk, blk, blk, blk],
        out_shape=[shp, shp, shp, shp],
        compiler_params=_cparams(),
    )(parts, w, m, v)


def _adam_math(g, w, m, v):
    c1 = 1.0 / (1.0 - ADAM_B1 ** ADAM_STEP)
    c2 = 1.0 / (1.0 - ADAM_B2 ** ADAM_STEP)
    mv = ADAM_B1 * m + (1.0 - ADAM_B1) * g
    vv = ADAM_B2 * v + (1.0 - ADAM_B2) * (g * g)
    upd = (mv * c1) / (jnp.sqrt(vv * c2) + ADAM_EPS) + ADAM_WD * w
    return -ADAM_LR * upd, mv, vv


_VEC = [("pre_norm_w", 2), ("post_norm_w", 2), ("b_in", 5), ("lru_conv_b", 1), ("b_gate_a", 1), ("b_gate_x", 1),
        ("lru_lambda", 1), ("conf_dw_b", 1), ("conf_ln_w", 1), ("conf_ln_b", 1), ("conf_pw_b", 1)]
_VEC_ROWS = 24
_LOSS_ROW = 17
_SM_ROWS = 64


def _pack_grads(dprew_acc, dpostw_acc, cvecs, kvecs, lvecs, dcw_acc, ddw_acc, dh, loss_acc):
    def body(pre_ref, post_ref, c_ref, k_ref, l_ref, dcw_ref, ddw_ref, dh_ref, loss_ref, vec_ref, small_ref, tmp):
        s8 = lambda ref, r: jnp.sum(ref[8 * r:8 * r + 8, :], axis=0, keepdims=True)
        vec_ref[...] = jnp.zeros_like(vec_ref)
        pre, post = s8(pre_ref, 0), s8(post_ref, 0)
        rows = [pre[:, 0:1024], pre[:, 1024:2048], post[:, 0:1024], post[:, 1024:2048],
                s8(l_ref, 1), s8(l_ref, 0), s8(k_ref, 1), s8(k_ref, 2), s8(c_ref, 1),
                s8(l_ref, 5), s8(l_ref, 2), s8(l_ref, 3), s8(l_ref, 4),
                s8(k_ref, 0), s8(c_ref, 2), s8(c_ref, 3), s8(c_ref, 0)]
        for r, val in enumerate(rows):
            vec_ref[r:r + 1, :] = val
        vec_ref[_LOSS_ROW:_LOSS_ROW + 1, :] = jnp.zeros((1, 1024), F32) + (0.5 / D) * jnp.sum(loss_ref[...])

        small_ref[...] = jnp.zeros_like(small_ref)
        for k in range(LW):
            tmp[k:k + 1, :] = s8(dcw_ref, k)
        for k in range(KW):
            tmp[8 + k:9 + k, :] = s8(ddw_ref, k)
        for d in range(NDEV):
            small_ref[d, 0:LW, 0:128] = tmp[0:LW, 128 * d:128 * d + 128]
            small_ref[d, 8:8 + KW, 0:128] = tmp[8:8 + KW, 128 * d:128 * d + 128]
            small_ref[d, 40:56, :] = dh_ref[:, 256 * d:256 * d + 256]

    full = lambda a: pl.BlockSpec(a.shape, lambda i: (0,) * a.ndim)
    ins = [dprew_acc, dpostw_acc, cvecs, kvecs, lvecs, dcw_acc, ddw_acc]
    return pl.pallas_call(
        body, name="pack_grads",
        grid=(1,),
        in_specs=[full(a) for a in ins] + [pl.BlockSpec((NMETA, D), lambda i: (0, 0)), full(loss_acc)],
        out_specs=[pl.BlockSpec((_VEC_ROWS, 1024), lambda i: (0, 0)),
                   pl.BlockSpec((NDEV, _SM_ROWS, 256), lambda i: (0, 0, 0))],
        out_shape=[jax.ShapeDtypeStruct((_VEC_ROWS, 1024), F32), jax.ShapeDtypeStruct((NDEV, _SM_ROWS, 256), F32)],
        scratch_shapes=[pltpu.VMEM((40, 1024), F32)],
        compiler_params=_cparams(),
    )(*ins, dh, loss_acc)


def _adamw_vec(parts, W, M, V):
    nv = len(_VEC)

    def body(*refs):
        p_ref = refs[0]
        w_refs, m_refs, v_refs = refs[1:1 + nv], refs[1 + nv:1 + 2 * nv], refs[1 + 2 * nv:1 + 3 * nv]
        outs = refs[1 + 3 * nv:]

        def total(r):
            acc = p_ref[0, r:r + 1, :]
            for sidx in range(1, NDEV):
                acc = acc + p_ref[sidx, r:r + 1, :]
            return acc

        row = 0
        for idx, (_, nrows) in enumerate(_VEC):
            for part in range(nrows):
                cols = slice(1024 * part, 1024 * part + 1024)
                g = total(row + part)
                delta, mv, vv = _adam_math(g, w_refs[idx][:, cols], m_refs[idx][:, cols], v_refs[idx][:, cols])
                for o, val in zip(outs[4 * idx:4 * idx + 4], (g, delta, mv, vv)):
                    o[:, cols] = val
            row += nrows
        outs[-1][...] = total(_LOSS_ROW)[:, 0:128]

    names = [n for n, _ in _VEC]
    flat = lambda d: [d[n].reshape(1, -1) for n in names]
    ws, ms, vs = flat(W), flat(M), flat(V)
    res = pl.pallas_call(
        body, name="adamw_vec",
        out_shape=[jax.ShapeDtypeStruct(w.shape, F32) for w in ws for _ in range(4)]
        + [jax.ShapeDtypeStruct((1, 128), F32)],
        compiler_params=_cparams(),
    )(parts, *ws, *ms, *vs)
    return {n: tuple(res[4 * i:4 * i + 4]) for i, n in enumerate(names)}, res[-1]


def _adamw_small(parts, W, M, V):
    where = {"lru_conv_w": (slice(0, LW), slice(0, 128)), "conf_dw_w": (slice(8, 8 + KW), slice(0, 128)),
             "meta_tokens": (slice(40, 56), slice(0, 256))}
    names = list(where)

    def body(*refs):
        p_ref = refs[0]
        outs = refs[10:]
        for idx, n in enumerate(names):
            rs, cs = where[n]
            g = p_ref[0, rs, cs]
            for sidx in range(1, NDEV):
                g = g + p_ref[sidx, rs, cs]
            delta, mv, vv = _adam_math(g, refs[1 + idx][...], refs[4 + idx][...], refs[7 + idx][...])
            for o, val in zip(outs[4 * idx:4 * idx + 4], (g, delta, mv, vv)):
                o[...] = val

    two_d = lambda a: a.reshape(a.shape[-2:])
    ws, ms, vs = ([two_d(d[n]) for n in names] for d in (W, M, V))
    res = pl.pallas_call(
        body, name="adamw_small",
        out_shape=[jax.ShapeDtypeStruct(w.shape, F32) for w in ws for _ in range(4)],
        compiler_params=_cparams(),
    )(parts, *ws, *ms, *vs)
    return {n: tuple(res[4 * i:4 * i + 4]) for i, n in enumerate(names)}


def _pack_small(lru_cw, dw_w, meta):
    buf = jnp.zeros((_SM_ROWS, 256), F32)
    buf = buf.at[0:LW, 0:128].set(lru_cw)
    buf = buf.at[8:8 + dw_w.shape[0], 0:128].set(dw_w)
    return buf.at[40:56, :].set(meta)


def _block_diag4(w):
    w4 = w.reshape(NCB, 4, 64, 64)
    eye = jnp.eye(4, dtype=w.dtype)
    return jnp.einsum("ghij,hk->ghikj", w4, eye).reshape(NCB, CB, CB)


def _diag_blocks(g):
    g5 = g.reshape(NCB, 4, 64, 4, 64)
    return jnp.stack([g5[:, hh, :, hh, :] for hh in range(4)], axis=1).reshape(16, 64, 64)


def _local_step(x, target, meta_full, inproj, out_weights, lru_cw_full, dw_w_full, W, send):
    h = jnp.concatenate([meta_full, x, jnp.zeros((TP - T, D), F32)], axis=0)
    tgt = jnp.concatenate([jnp.zeros((NMETA, D), F32), target, jnp.zeros((TP - T, D), F32)], axis=0)
    wa_g = _block_diag4(W["w_gate_a"][0]).astype(BF16)
    wx_g = _block_diag4(W["w_gate_x"][0]).astype(BF16)

    hn = _prenorm(h, W["pre_norm_w"])
    z, win_full = inproj(hn)
    ylru, xc, hs = _lru_fwd(z, lru_cw_full, W["lru_conv_b"], wa_g, W["b_gate_a"], wx_g, W["b_gate_x"],
                            W["lru_lambda"])
    vc = _conf_fwd_conv(z, dw_w_full, W["conf_dw_b"])
    wout_full, pw_full = out_weights(vc)
    yconf, p = _conf_fwd_proj(vc, z, W["conf_ln_w"], W["conf_ln_b"], pw_full, W["conf_pw_b"])
    dout, dy, loss_acc, dpostw_acc = _outproj_loss(ylru, yconf, wout_full, h, tgt, W["post_norm_w"])

    dycat, dwout_part = _outproj_bwd(dy, ylru, yconf, wout_full)
    tok = send("w_out", dwout_part)
    dvc, dgc, dpw_part, cvecs = _conf_bwd_proj(dycat, p, z, vc, W["conf_ln_w"] + tok, W["conf_ln_b"], pw_full)
    tok = send("conf_pw_w", dpw_part)
    tok = tok + send("w_in_c", _inproj_wgrad("inproj_wgrad_c", hn, dgc[None]))
    dzc, ddw_acc, kvecs = _conf_bwd_conv(dvc, z, dw_w_full + tok)
    tok = send("w_in_b", _inproj_wgrad("inproj_wgrad_b", hn, dzc))
    dzl, dwa_g, dwx_g, dcw_acc, lvecs = _lru_bwd(dycat, z, xc, hs, lru_cw_full, wa_g, W["b_gate_a"] + tok, wx_g,
                                                 W["b_gate_x"], W["lru_lambda"])
    tok = send("w_in_a", _inproj_wgrad("inproj_wgrad_a", hn, dzl))
    tok = tok + send("w_gates", _diag_blocks(dwa_g).reshape(16 * 64, 64), _diag_blocks(dwx_g).reshape(16 * 64, 64))
    dh, dprew_acc = _inproj_bwd(dzl, dzc, dgc, win_full, h, dout, W["pre_norm_w"] + tok)

    vec_pack, small_part = _pack_grads(dprew_acc, dpostw_acc, cvecs, kvecs, lvecs, dcw_acc, ddw_acc, dh, loss_acc)
    return dh, vec_pack, small_part


def kernel(x, meta_tokens, pre_norm_w, post_norm_w, w_in, b_in, lru_conv_w, lru_conv_b, w_gate_a, b_gate_a, w_gate_x, b_gate_x, lru_lambda, conf_dw_w, conf_dw_b, conf_ln_w, conf_ln_b, conf_pw_w, conf_pw_b, w_out, loss_target, m_meta_tokens, m_pre_norm_w, m_post_norm_w, m_w_in, m_b_in, m_lru_conv_w, m_lru_conv_b, m_w_gate_a, m_b_gate_a, m_w_gate_x, m_b_gate_x, m_lru_lambda, m_conf_dw_w, m_conf_dw_b, m_conf_ln_w, m_conf_ln_b, m_conf_pw_w, m_conf_pw_b, m_w_out, v_meta_tokens, v_pre_norm_w, v_post_norm_w, v_w_in, v_b_in, v_lru_conv_w, v_lru_conv_b, v_w_gate_a, v_b_gate_a, v_w_gate_x, v_b_gate_x, v_lru_lambda, v_conf_dw_w, v_conf_dw_b, v_conf_ln_w, v_conf_ln_b, v_conf_pw_w, v_conf_pw_b, v_w_out):
    W = dict(meta_tokens=meta_tokens, pre_norm_w=pre_norm_w, post_norm_w=post_norm_w, w_in=w_in, b_in=b_in,
             lru_conv_w=lru_conv_w, lru_conv_b=lru_conv_b, w_gate_a=w_gate_a, b_gate_a=b_gate_a,
             w_gate_x=w_gate_x, b_gate_x=b_gate_x, lru_lambda=lru_lambda, conf_dw_w=conf_dw_w,
             conf_dw_b=conf_dw_b, conf_ln_w=conf_ln_w, conf_ln_b=conf_ln_b, conf_pw_w=conf_pw_w,
             conf_pw_b=conf_pw_b, w_out=w_out)
    M = dict(meta_tokens=m_meta_tokens, pre_norm_w=m_pre_norm_w, post_norm_w=m_post_norm_w, w_in=m_w_in,
             b_in=m_b_in, lru_conv_w=m_lru_conv_w, lru_conv_b=m_lru_conv_b, w_gate_a=m_w_gate_a,
             b_gate_a=m_b_gate_a, w_gate_x=m_w_gate_x, b_gate_x=m_b_gate_x, lru_lambda=m_lru_lambda,
             conf_dw_w=m_conf_dw_w, conf_dw_b=m_conf_dw_b, conf_ln_w=m_conf_ln_w, conf_ln_b=m_conf_ln_b,
             conf_pw_w=m_conf_pw_w, conf_pw_b=m_conf_pw_b, w_out=m_w_out)
    V = dict(meta_tokens=v_meta_tokens, pre_norm_w=v_pre_norm_w, post_norm_w=v_post_norm_w, w_in=v_w_in,
             b_in=v_b_in, lru_conv_w=v_lru_conv_w, lru_conv_b=v_lru_conv_b, w_gate_a=v_w_gate_a,
             b_gate_a=v_b_gate_a, w_gate_x=v_w_gate_x, b_gate_x=v_b_gate_x, lru_lambda=v_lru_lambda,
             conf_dw_w=v_conf_dw_w, conf_dw_b=v_conf_dw_b, conf_ln_w=v_conf_ln_w, conf_ln_b=v_conf_ln_b,
             conf_pw_w=v_conf_pw_w, conf_pw_b=v_conf_pw_b, w_out=v_w_out)
    names = list(W.keys())
    shapes = {n: W[n].shape for n in names}

    small = _pack_small(lru_conv_w[0], conf_dw_w[0], meta_tokens)
    (small_flight,), tok = _exchange_start("gather_small_start", [
        (small, jax.ShapeDtypeStruct((NDEV, _SM_ROWS, 256), F32), _whole, _slot)])
    win_flight, tok = _win_gather_start(w_in[0].astype(BF16) + tok[0, 0].astype(BF16))
    gathered, tok = _exchange_start("gather_out_start", [
        (w_out[0].astype(BF16) + tok[0, 0].astype(BF16), jax.ShapeDtypeStruct((D, D), BF16), _whole,
         _rows(D // NDEV)),
        (conf_pw_w[0].astype(BF16), jax.ShapeDtypeStruct((DC, DC), BF16), _whole, _rows(DC // NDEV)),
    ])
    (small_all,) = _exchange_wait("gather_small_wait", [small_flight], tok)
    unshard = lambda a: jnp.transpose(a, (1, 0, 2)).reshape(a.shape[1], -1)
    lru_cw_full = unshard(small_all[:, 0:LW, 0:128])
    dw_w_full = unshard(small_all[:, 8:8 + KWP, 0:128])
    meta_full = unshard(small_all[:, 40:56, :])

    def out_weights(after):
        return _exchange_wait("gather_out_wait", gathered, after)

    def inproj(hn):
        xi, yi, ci = lax.axis_index("x"), lax.axis_index("y"), lax.axis_index("c")
        shard = lambda px, py, pc: (4 * px + 2 * py + pc).astype(jnp.int32)
        here = jnp.stack([shard(xi, yi, ci), shard(xi, yi, 1 - ci)])
        over_links = jnp.stack([shard(1 - xi, yi, ci), shard(xi, 1 - yi, ci), shard(1 - xi, 1 - yi, ci)])
        flight = _win_gather_early(win_flight)
        z, land = _inproj_cols("inproj_here", here, hn, flight["land"], b_in, None)
        flight = _win_gather_forward(dict(flight, land=land), z)
        z, land = _inproj_cols("inproj_links", over_links, hn, flight["land"], b_in, z)
        land = _win_gather_wait(dict(flight, land=land))
        return _inproj_cols("inproj_sibling", over_links + 1 - 2 * ci, hn, land, b_in, z)

    row_stage = lambda ncol: (jax.ShapeDtypeStruct((NDEV, D // NDEV, ncol), BF16), _rows(D // NDEV))
    piece = {"w_in_a": row_stage(2048), "w_in_b": row_stage(2048), "w_in_c": row_stage(1024),
             "w_out": row_stage(D),
             "conf_pw_w": (jax.ShapeDtypeStruct((NDEV, DC // NDEV, DC), BF16), _rows(DC // NDEV)),
             "w_gates": (jax.ShapeDtypeStruct((NDEV, 16 * 64, 64), BF16), _whole)}
    sent = {}

    def send(name, *parts):
        handles, token = _exchange_start(
            "scatter_" + name + "_start",
            [(part.astype(BF16), piece[name][0], piece[name][1], _slot) for part in parts])
        sent[name] = handles
        return token[0, 0]

    dh, vec_pack, small_part = _local_step(
        x[0], loss_target[0], meta_full, inproj, out_weights, lru_cw_full, dw_w_full, W, send)
    grad_x = dh[NMETA:T][None]

    (parts_c,) = _exchange_wait("scatter_w_in_c_wait", sent["w_in_c"], dh)
    (parts_b,) = _exchange_wait("scatter_w_in_b_wait", sent["w_in_b"], parts_c)
    (parts_a,) = _exchange_wait("scatter_w_in_a_wait", sent["w_in_a"], parts_b)
    win_rows = _sum_win_parts(parts_a, parts_b, parts_c)
    win_stage2, tok = _exchange_start("scatter_w_in_stage2_start", [
        (win_rows, jax.ShapeDtypeStruct((NDEV, D // NDEV, NIN // NDEV), BF16), _cols(NIN // NDEV), _slot)])
    rest, _ = _exchange_start("scatter_rest_start", [
        (small_part, jax.ShapeDtypeStruct((NDEV, _SM_ROWS, 256), F32), _slot, _slot),
        (vec_pack + tok[0, 0], jax.ShapeDtypeStruct((NDEV, _VEC_ROWS, 1024), F32), _whole, _slot),
    ])

    G, DW, NM, NV = {}, {}, {}, {}
    (wout_parts,) = _exchange_wait("scatter_w_out_wait", sent["w_out"], win_rows)
    G["w_out"], DW["w_out"], NM["w_out"], NV["w_out"] = _adamw("adamw_w_out", wout_parts, w_out[0], m_w_out[0], v_w_out[0], 64)
    (pw_parts,) = _exchange_wait("scatter_conf_pw_w_wait", sent["conf_pw_w"], G["w_out"])
    G["conf_pw_w"], DW["conf_pw_w"], NM["conf_pw_w"], NV["conf_pw_w"] = _adamw(
        "adamw_pw", pw_parts, conf_pw_w[0], m_conf_pw_w[0], v_conf_pw_w[0], 128)
    res = {}
    wa_parts, wx_parts = _exchange_wait("scatter_w_gates_wait", sent["w_gates"], G["conf_pw_w"])
    for n, parts in (("w_gate_a", wa_parts), ("w_gate_x", wx_parts)):
        res[n] = _adamw("adamw_" + n, parts, *[d[n].reshape(16 * 64, 64) for d in (W, M, V)], 16 * 64)
    small_parts, vec_parts = _exchange_wait("scatter_rest_wait", rest, res["w_gate_x"][0])
    res.update(_adamw_small(small_parts, W, M, V))
    vec_res, loss_row = _adamw_vec(vec_parts, W, M, V)
    res.update(vec_res)
    (win_sum,) = _exchange_wait("scatter_w_in_stage2_wait", win_stage2, loss_row)
    res["w_in"] = _adamw("adamw_w_in", win_sum.reshape(1, D, NIN // NDEV), w_in[0], m_w_in[0], v_w_in[0], 256)
    for n, vals in res.items():
        for dst, val in zip((G, DW, NM, NV), vals):
            dst[n] = val
    for dst in (G, DW, NM, NV):
        for n in names:
            dst[n] = dst[n].reshape(shapes[n])
    loss = loss_row[0, 0]

    return (loss, grad_x, *[G[n] for n in names], *[DW[n] for n in names],
            *[NM[n] for n in names], *[NV[n] for n in names])
```

```python
import functools

import jax
import jax.numpy as jnp
from jax import lax
from jax.experimental import pallas as pl
from jax.experimental.pallas import tpu as pltpu

F32 = jnp.float32
BF16 = jnp.bfloat16

D = 2048
DL = 1024
DC = 1024
NIN = 5120
NMETA = 16
SEQ = 2048
T = NMETA + SEQ
TP = 2176
TM = 544
CB = 256
NCB = DL // CB
R = 16
KW = 31
KWP = 32
LW = 4
LRU_C = 8.0
EPS = 1e-6
NDEV = 8

ADAM_LR = 0.001
ADAM_B1 = 0.9
ADAM_B2 = 0.999
ADAM_EPS = 1e-08
ADAM_WD = 0.01
ADAM_STEP = 10

VMEM_LIMIT = 56 * 1024 * 1024


def _cparams():
    return pltpu.CompilerParams(vmem_limit_bytes=VMEM_LIMIT)


def _sig(x):
    return 1.0 / (1.0 + jnp.exp(-x))


def _expm1_neg(y):
    poly = y * (1.0 + y * (0.5 + y * (1.0 / 6.0 + y * (1.0 / 24.0 + y * (1.0 / 120.0)))))
    return jnp.where(y > -0.1, poly, jnp.exp(y) - 1.0)


def _softplus(x):
    e = jnp.exp(-jnp.abs(x))
    w = 1.0 + e
    l1p = jnp.where(w == 1.0, e, jnp.log(w) * e / (w - 1.0))
    return jnp.maximum(x, 0.0) + l1p


def _row_iota(shape):
    return lax.broadcasted_iota(jnp.int32, shape, 0)


def _fold8(v):
    return v[0:8, :] + v[8:16, :]


_FLIPS = [(k >> 2 & 1, k >> 1 & 1, k & 1) for k in range(1, NDEV)]
_HBM = pl.BlockSpec(memory_space=pltpu.HBM)
_SEM = pl.BlockSpec(memory_space=pltpu.SEMAPHORE)


def _peers():
    x, y, c = lax.axis_index("x"), lax.axis_index("y"), lax.axis_index("c")
    out = []
    for dx, dy, dc in _FLIPS:
        px = 1 - x if dx else x
        py = 1 - y if dy else y
        pc = 1 - c if dc else c
        out.append(((px, py, pc), 4 * px + 2 * py + pc))
    return 4 * x + 2 * y + c, out


def _exchange_start(name, items):
    n = len(items)

    def body(*refs):
        srcs, lands = refs[:n], refs[n:2 * n]
        outs = refs[2 * n:]
        send_sems, recv_sems, local_sems = outs[:n], outs[n:2 * n], outs[2 * n:3 * n]
        token = outs[-1]
        me, peers = _peers()
        for a in range(n):
            src_at, dst_at = items[a][2], items[a][3]
            pltpu.make_async_copy(src_at(srcs[a], me), dst_at(lands[a], me), local_sems[a]).start()
        for a in range(n):
            src_at, dst_at = items[a][2], items[a][3]
            for k, (pos, peer) in enumerate(peers):
                pltpu.make_async_remote_copy(
                    src_ref=src_at(srcs[a], peer), dst_ref=dst_at(lands[a], me),
                    send_sem=send_sems[a].at[k], recv_sem=recv_sems[a].at[k],
                    device_id=pos, device_id_type=pl.DeviceIdType.MESH).start()
        token[...] = jnp.zeros_like(token)

    srcs = [pltpu.with_memory_space_constraint(it[0], pltpu.HBM) for it in items]
    lands = [pltpu.with_memory_space_constraint(lax.empty(it[1].shape, it[1].dtype), pltpu.HBM) for it in items]
    sem7 = pltpu.SemaphoreType.DMA((NDEV - 1,))
    res = pl.pallas_call(
        body, name=name,
        out_shape=([sem7] * (2 * n) + [pltpu.SemaphoreType.DMA(())] * n
                   + [pltpu.HBM(a.shape, a.dtype) for a in srcs] + [pltpu.HBM(a.shape, a.dtype) for a in lands]
                   + [jax.ShapeDtypeStruct((8, 128), F32)]),
        in_specs=[_HBM] * (2 * n),
        out_specs=[_SEM] * (3 * n) + [_HBM] * (2 * n) + [pl.BlockSpec(memory_space=pltpu.VMEM)],
        input_output_aliases={i: 3 * n + i for i in range(2 * n)},
        compiler_params=pltpu.CompilerParams(has_side_effects=pltpu.SideEffectType.DATAFLOW_SIDE_EFFECTING),
    )(*srcs, *lands)
    handles = [dict(send=res[a], recv=res[n + a], local=res[2 * n + a], src=res[3 * n + a], land=res[4 * n + a],
                    src_at=items[a][2], dst_at=items[a][3]) for a in range(n)]
    return handles, res[-1]


def _exchange_wait(name, handles, after):
    n = len(handles)

    def body(*refs):
        srcs, lands = refs[:n], refs[n:2 * n]
        send_sems, recv_sems, local_sems = refs[2 * n:3 * n], refs[3 * n:4 * n], refs[4 * n:5 * n]
        me, peers = _peers()
        for a in range(n):
            src_at, dst_at = handles[a]["src_at"], handles[a]["dst_at"]
            for k, (pos, peer) in enumerate(peers):
                cp = pltpu.make_async_remote_copy(
                    src_ref=src_at(srcs[a], peer), dst_ref=dst_at(lands[a], peer),
                    send_sem=send_sems[a].at[k], recv_sem=recv_sems[a].at[k],
                    device_id=pos, device_id_type=pl.DeviceIdType.MESH)
                cp.wait_send()
                cp.wait_recv()
            pltpu.make_async_copy(src_at(srcs[a], me), dst_at(lands[a], me), local_sems[a]).wait()

    srcs = [hd["src"] for hd in handles]
    lands = [hd["land"] for hd in handles]
    res = pl.pallas_call(
        body, name=name,
        out_shape=[pltpu.HBM(a.shape, a.dtype) for a in srcs] + [pltpu.HBM(a.shape, a.dtype) for a in lands],
        in_specs=[_HBM] * (2 * n) + [_SEM] * (3 * n) + [pl.BlockSpec(memory_space=pl.ANY)],
        out_specs=[_HBM] * (2 * n),
        input_output_aliases={i: i for i in range(2 * n)},
        compiler_params=pltpu.CompilerParams(has_side_effects=pltpu.SideEffectType.DATAFLOW_SIDE_EFFECTING),
    )(*srcs, *lands, *[hd["send"] for hd in handles], *[hd["recv"] for hd in handles],
      *[hd["local"] for hd in handles], after)
    return list(res[n:])


_SIDE = pltpu.SideEffectType.DATAFLOW_SIDE_EFFECTING
_WCOLS = NIN // NDEV


def _win_cols(ref, l):
    return ref.at[:, pl.ds(pl.multiple_of(l * _WCOLS, 128), _WCOLS)]


def _win_routes():
    x, y, c = lax.axis_index("x"), lax.axis_index("y"), lax.axis_index("c")
    pos = [(x, y, 1 - c), (1 - x, y, c), (x, 1 - y, c), (1 - x, 1 - y, c)]
    return 4 * x + 2 * y + c, [(p, 4 * p[0] + 2 * p[1] + p[2]) for p in pos]


def _win_gather_start(shard):
    def body(src, land, send_sems, recv_sems, local_sem, src_thru, land_thru, token):
        me, routes = _win_routes()
        pltpu.make_async_copy(src, _win_cols(land, me), local_sem).start()
        for k, (pos, _) in enumerate(routes):
            pltpu.make_async_remote_copy(src_ref=src, dst_ref=_win_cols(land, me), send_sem=send_sems.at[k],
                                         recv_sem=recv_sems.at[k], device_id=pos,
                                         device_id_type=pl.DeviceIdType.MESH).start()
        token[...] = jnp.zeros_like(token)

    src = pltpu.with_memory_space_constraint(shard, pltpu.HBM)
    land = pltpu.with_memory_space_constraint(lax.empty((D, NIN), BF16), pltpu.HBM)
    sem4 = pltpu.SemaphoreType.DMA((4,))
    res = pl.pallas_call(
        body, name="win_gather_start",
        out_shape=[sem4, sem4, pltpu.SemaphoreType.DMA(()), pltpu.HBM(src.shape, BF16), pltpu.HBM(land.shape, BF16),
                   jax.ShapeDtypeStruct((8, 128), F32)],
        in_specs=[_HBM, _HBM],
        out_specs=[_SEM, _SEM, _SEM, _HBM, _HBM, pl.BlockSpec(memory_space=pltpu.VMEM)],
        input_output_aliases={0: 3, 1: 4},
        compiler_params=pltpu.CompilerParams(has_side_effects=_SIDE),
    )(src, land)
    return dict(send=res[0], recv=res[1], local=res[2], src=res[3], land=res[4]), res[5]


def _win_gather_forward(hd, after):
    def body(land, recv_sems, after_ref, land_thru, fsend_sems, frecv_sems):
        me, routes = _win_routes()
        sibling = routes[0][0]
        for k in (1, 2, 3):
            pos, peer = routes[k]
            piece = _win_cols(land, peer)
            pltpu.make_async_remote_copy(src_ref=piece, dst_ref=piece, send_sem=fsend_sems.at[k - 1],
                                         recv_sem=recv_sems.at[k], device_id=pos,
                                         device_id_type=pl.DeviceIdType.MESH).wait_recv()
            pltpu.make_async_remote_copy(src_ref=piece, dst_ref=piece, send_sem=fsend_sems.at[k - 1],
                                         recv_sem=frecv_sems.at[k - 1], device_id=sibling,
                                         device_id_type=pl.DeviceIdType.MESH).start()

    sem3 = pltpu.SemaphoreType.DMA((3,))
    res = pl.pallas_call(
        body, name="win_gather_forward",
        out_shape=[pltpu.HBM(hd["land"].shape, BF16), sem3, sem3],
        in_specs=[_HBM, _SEM, pl.BlockSpec(memory_space=pl.ANY)],
        out_specs=[_HBM, _SEM, _SEM],
        input_output_aliases={0: 0},
        compiler_params=pltpu.CompilerParams(has_side_effects=_SIDE),
    )(hd["land"], hd["recv"], after)
    return dict(hd, land=res[0], fsend=res[1], frecv=res[2])


def _win_gather_early(hd):
    def body(src, land, recv_sems, local_sem, src_thru, land_thru):
        me, routes = _win_routes()
        pos, sibling = routes[0]
        pltpu.make_async_remote_copy(src_ref=src, dst_ref=_win_cols(land, sibling), send_sem=local_sem,
                                     recv_sem=recv_sems.at[0], device_id=pos,
                                     device_id_type=pl.DeviceIdType.MESH).wait_recv()
        pltpu.make_async_copy(src, _win_cols(land, me), local_sem).wait()

    res = pl.pallas_call(
        body, name="win_gather_early",
        out_shape=[pltpu.HBM(hd["src"].shape, BF16), pltpu.HBM(hd["land"].shape, BF16)],
        in_specs=[_HBM, _HBM, _SEM, _SEM],
        out_specs=[_HBM, _HBM],
        input_output_aliases={0: 0, 1: 1},
        compiler_params=pltpu.CompilerParams(has_side_effects=_SIDE),
    )(hd["src"], hd["land"], hd["recv"], hd["local"])
    return dict(hd, src=res[0], land=res[1])


def _win_gather_wait(hd):
    def body(src, land, send_sems, fsend_sems, frecv_sems, src_thru, land_thru):
        me, routes = _win_routes()
        sib_pos, sibling = routes[0]
        for k, (pos, peer) in enumerate(routes):
            pltpu.make_async_remote_copy(src_ref=src, dst_ref=_win_cols(land, peer), send_sem=send_sems.at[k],
                                         recv_sem=frecv_sems.at[0], device_id=pos,
                                         device_id_type=pl.DeviceIdType.MESH).wait_send()
        for k in (1, 2, 3):
            mine = _win_cols(land, routes[k][1])
            theirs = _win_cols(land, 4 * routes[k][0][0] + 2 * routes[k][0][1] + sib_pos[2])
            cp = pltpu.make_async_remote_copy(src_ref=mine, dst_ref=theirs, send_sem=fsend_sems.at[k - 1],
                                              recv_sem=frecv_sems.at[k - 1], device_id=sib_pos,
                                              device_id_type=pl.DeviceIdType.MESH)
            cp.wait_send()
            cp.wait_recv()

    res = pl.pallas_call(
        body, name="win_gather_wait",
        out_shape=[pltpu.HBM(hd["src"].shape, BF16), pltpu.HBM(hd["land"].shape, BF16)],
        in_specs=[_HBM, _HBM] + [_SEM] * 3,
        out_specs=[_HBM, _HBM],
        input_output_aliases={0: 0, 1: 1},
        compiler_params=pltpu.CompilerParams(has_side_effects=_SIDE),
    )(hd["src"], hd["land"], hd["send"], hd["fsend"], hd["frecv"])
    return res[1]


def _whole(ref, l):
    return ref


def _slot(ref, l):
    return ref.at[l]


def _cols(width):
    def at(ref, l):
        return ref.at[:, pl.ds(pl.multiple_of(l * width, 128), width)]
    return at


def _rows(height):
    def at(ref, l):
        return ref.at[pl.ds(pl.multiple_of(l * height, 8), height), :]
    return at


NTILE = TP // TM


def _tile_rows(t):
    lo = max(t * TM - NMETA, 0)
    hi = min((t + 1) * TM - NMETA, SEQ)
    return lo, hi - lo, lo + NMETA - t * TM


def _for_tile(t, fn):
    for static_t in range(NTILE):
        pl.when(t == static_t)(functools.partial(fn, static_t))


def _token_tile_copy(hbm_ref, buf, sem, t):
    lo, n, off = _tile_rows(t)
    return pltpu.make_async_copy(hbm_ref.at[pl.ds(lo, n)], buf.at[pl.ds(off, n)], sem)


def _prenorm(x, meta_full, pre_w):
    def body(x_ref, meta_ref, pw_ref, h_ref, hn_ref, xbuf, sems):
        i = pl.program_id(0)
        slot = i % 2

        def start(t):
            _token_tile_copy(x_ref, xbuf.at[t % 2], sems.at[t % 2], t).start()

        @pl.when(i == 0)
        def _():
            start(0)
        _for_tile(i + 1, start)
        _for_tile(i, lambda t: _token_tile_copy(x_ref, xbuf.at[t % 2], sems.at[t % 2], t).wait())

        @pl.when(i == 0)
        def _():
            xbuf[0, 0:NMETA, :] = meta_ref[...]

        @pl.when(i == NTILE - 1)
        def _():
            last = _tile_rows(NTILE - 1)[1]
            xbuf[(NTILE - 1) % 2, last:TM, :] = jnp.zeros((TM - last, D), F32)

        pw = pw_ref[...]

        def chunk(ci, carry):
            r0 = pl.multiple_of(ci * R, R)
            xv = xbuf[slot, pl.ds(r0, R), :]
            h_ref[pl.ds(r0, R), :] = xv
            ms = jnp.mean(xv * xv, axis=-1, keepdims=True)
            hn_ref[pl.ds(r0, R), :] = (xv * lax.rsqrt(ms + EPS) * pw).astype(BF16)
            return carry
        lax.fori_loop(0, TM // R, chunk, 0, unroll=2)

    row = pl.BlockSpec((TM, D), lambda i: (i, 0))
    return pl.pallas_call(
        body, name="prenorm",
        grid=(NTILE,),
        in_specs=[pl.BlockSpec(memory_space=pl.ANY), pl.BlockSpec((NMETA, D), lambda i: (0, 0)),
                  pl.BlockSpec((1, D), lambda i: (0, 0))],
        out_specs=[row, row],
        out_shape=[jax.ShapeDtypeStruct((TP, D), F32), jax.ShapeDtypeStruct((TP, D), BF16)],
        scratch_shapes=[pltpu.VMEM((2, TM, D), F32), pltpu.SemaphoreType.DMA((2,))],
        compiler_params=_cparams(),
    )(x, meta_full, pre_w)


def _inproj_cols(name, shards, hn, w_land, b_in, z_prev):
    nsh = shards.shape[0]

    def body(idx_ref, hn_ref, w_ref, b_ref, *rest):
        z_ref = rest[-2]
        z_ref[...] = jnp.dot(hn_ref[...], w_ref[...], preferred_element_type=F32) + b_ref[...]

    any_spec = pl.BlockSpec(memory_space=pl.ANY)
    in_specs = [pl.BlockSpec((TM, D), lambda j, i, idx: (i, 0)),
                pl.BlockSpec((D, _WCOLS), lambda j, i, idx: (0, idx[j])),
                pl.BlockSpec((1, _WCOLS), lambda j, i, idx: (0, idx[j]))]
    operands = [hn, w_land, b_in]
    aliases = {2: 1}
    if z_prev is not None:
        in_specs.append(any_spec)
        operands.append(z_prev)
        aliases[4] = 0
    return pl.pallas_call(
        body, name=name,
        grid_spec=pltpu.PrefetchScalarGridSpec(
            num_scalar_prefetch=1, grid=(nsh, TP // TM), in_specs=in_specs,
            out_specs=[pl.BlockSpec((TM, _WCOLS), lambda j, i, idx: (i, idx[j])), any_spec]),
        out_shape=[jax.ShapeDtypeStruct((TP, NIN), F32), jax.ShapeDtypeStruct(w_land.shape, w_land.dtype)],
        input_output_aliases=aliases,
        compiler_params=_cparams(),
    )(shards, *operands)


def _gate_values(ga, gx, xc, sp8):
    r = _sig(ga)
    i = _sig(gx)
    log_a = -(r * sp8)
    a = jnp.exp(log_a)
    mult = jnp.sqrt(-_expm1_neg(2.0 * log_a))
    return r, i, a, mult


def _lru_fwd(z, conv_w, conv_b, wa_g, b_a, wx_g, b_x, lam):
    def body(x_ref, g_ref, cw_ref, cb_ref, wa_ref, ba_ref, wx_ref, bx_ref, lam_ref,
             y_ref, xc_ref, hs_ref, ga_s, gx_s):
        taps = [cw_ref[k:k + 1, :] for k in range(LW)]
        cb = cb_ref[...]

        def conv_chunk(ci, carry):
            r0 = pl.multiple_of(ci * R, R)
            cur = x_ref[pl.ds(r0, R), :]
            p0 = pl.multiple_of(jnp.maximum(r0 - 8, 0), 8)
            prev = jnp.where(ci > 0, x_ref[pl.ds(p0, 8), :], 0.0)
            buf = jnp.concatenate([prev, cur], axis=0)
            acc = cur * taps[LW - 1] + cb
            for s in range(1, LW):
                acc = acc + pltpu.roll(buf, s, 0)[8:8 + R, :] * taps[LW - 1 - s]
            xc_ref[pl.ds(r0, R), :] = acc
            return carry
        lax.fori_loop(0, TP // R, conv_chunk, 0)

        def gate_chunk(ci, carry):
            r0 = pl.multiple_of(ci * TM, TM)
            xb = xc_ref[pl.ds(r0, TM), :].astype(BF16)
            ga_s[pl.ds(r0, TM), :] = jnp.dot(xb, wa_ref[...], preferred_element_type=F32) + ba_ref[...]
            gx_s[pl.ds(r0, TM), :] = jnp.dot(xb, wx_ref[...], preferred_element_type=F32) + bx_ref[...]
            return carry
        lax.fori_loop(0, TP // TM, gate_chunk, 0)

        sp8 = LRU_C * _softplus(-lam_ref[...])
        row = _row_iota((R, CB))

        def scan_chunk(ci, hprev):
            r0 = pl.multiple_of(ci * R, R)
            xc = xc_ref[pl.ds(r0, R), :]
            _, i, a, mult = _gate_values(ga_s[pl.ds(r0, R), :], gx_s[pl.ds(r0, R), :], xc, sp8)
            u = mult * (i * xc)
            k = 1
            while k < R:
                m = row >= k
                u = jnp.where(m, a * pltpu.roll(u, k, 0) + u, u)
                a = jnp.where(m, a * pltpu.roll(a, k, 0), a)
                k *= 2
            hv = u + a * hprev
            hs_ref[pl.ds(r0, R), :] = hv
            g = g_ref[pl.ds(r0, R), :]
            y_ref[pl.ds(r0, R), :] = (hv * (g * _sig(g))).astype(BF16)
            return jnp.sum(jnp.where(row == R - 1, hv, 0.0), axis=0, keepdims=True)
        lax.fori_loop(0, TP // R, scan_chunk, jnp.zeros((1, CB), F32))

    col = lambda off: pl.BlockSpec((TP, CB), lambda j: (0, off + j))
    vec = pl.BlockSpec((1, CB), lambda j: (0, j))
    wsp = pl.BlockSpec((None, CB, CB), lambda j: (j, 0, 0))
    return pl.pallas_call(
        body, name="lru_fwd",
        grid=(NCB,),
        in_specs=[col(0), col(NCB), pl.BlockSpec((LW, CB), lambda j: (0, j)), vec, wsp, vec, wsp, vec, vec],
        out_specs=[col(0), col(0), col(0)],
        out_shape=[jax.ShapeDtypeStruct((TP, DL), BF16), jax.ShapeDtypeStruct((TP, DL), F32),
                   jax.ShapeDtypeStruct((TP, DL), F32)],
        scratch_shapes=[pltpu.VMEM((TP, CB), F32), pltpu.VMEM((TP, CB), F32)],
        compiler_params=_cparams(),
    )(z, z, conv_w, conv_b, wa_g, b_a, wx_g, b_x, lam)


def _conf_fwd_conv(z, dw_w, dw_b):
    def body(u1_ref, u2_ref, w_ref, b_ref, vc_ref, vs):
        vs[pl.ds(0, KWP), :] = jnp.zeros((KWP, CB), F32)

        def glu_chunk(ci, carry):
            r0 = pl.multiple_of(ci * R, R)
            vs[pl.ds(KWP + r0, R), :] = u1_ref[pl.ds(r0, R), :] * _sig(u2_ref[pl.ds(r0, R), :])
            return carry
        lax.fori_loop(0, TP // R, glu_chunk, 0)

        bias = b_ref[...]

        def conv_chunk(ci, carry):
            r0 = pl.multiple_of(ci * R, R)
            buf = vs[pl.ds(r0, KWP + R), :]
            acc = jnp.zeros((R, CB), F32) + bias
            for rr in range(8):
                rolled = buf if rr == 0 else pltpu.roll(buf, rr, 0)
                for q in range(4):
                    s = 8 * q + rr
                    if s > KW - 1:
                        continue
                    k = KW - 1 - s
                    acc = acc + rolled[KWP - 8 * q:KWP - 8 * q + R, :] * w_ref[k:k + 1, :]
            vc_ref[pl.ds(r0, R), :] = acc
            return carry
        lax.fori_loop(0, TP // R, conv_chunk, 0)

    return pl.pallas_call(
        body, name="conf_fwd_conv",
        grid=(NCB,),
        in_specs=[pl.BlockSpec((TP, CB), lambda j: (0, 2 * NCB + j)),
                  pl.BlockSpec((TP, CB), lambda j: (0, 3 * NCB + j)),
                  pl.BlockSpec((KWP, CB), lambda j: (0, j)),
                  pl.BlockSpec((1, CB), lambda j: (0, j))],
        out_specs=pl.BlockSpec((TP, CB), lambda j: (0, j)),
        out_shape=jax.ShapeDtypeStruct((TP, DC), F32),
        scratch_shapes=[pltpu.VMEM((TP + KWP, CB), F32)],
        compiler_params=_cparams(),
    )(z, z, dw_w, dw_b)


def _ln_chunk(vc, lw, lb):
    mu = jnp.mean(vc, axis=-1, keepdims=True)
    xm = vc - mu
    var = jnp.mean(xm * xm, axis=-1, keepdims=True)
    rstd = lax.rsqrt(var + EPS)
    xhat = xm * rstd
    return xhat, rstd, xhat * lw + lb


def _conf_fwd_proj(vc, z, ln_w, ln_b, pw_w, pw_b):
    def body(vc_ref, g_ref, lw_ref, lb_ref, w_ref, b_ref, y_ref, p_ref, s_s):
        lw, lb = lw_ref[...], lb_ref[...]

        def ln_chunk(ci, carry):
            r0 = pl.multiple_of(ci * R, R)
            for half in range(2):
                rr = r0 + 8 * half
                _, _, ln = _ln_chunk(vc_ref[pl.ds(rr, 8), :], lw, lb)
                p_ref[pl.ds(rr, 8), :] = ln * _sig(ln)
            s_s[pl.ds(r0, R), :] = p_ref[pl.ds(r0, R), :].astype(BF16)
            return carry
        lax.fori_loop(0, TM // R, ln_chunk, 0, unroll=2)

        p_ref[...] = jnp.dot(s_s[...], w_ref[...], preferred_element_type=F32) + b_ref[...]

        def out_chunk(ci, carry):
            r0 = pl.multiple_of(ci * R, R)
            g = g_ref[pl.ds(r0, R), :]
            y_ref[pl.ds(r0, R), :] = (p_ref[pl.ds(r0, R), :] * (g * _sig(g))).astype(BF16)
            return carry
        lax.fori_loop(0, TM // R, out_chunk, 0)

    row = pl.BlockSpec((TM, DC), lambda i: (i, 0))
    vec = pl.BlockSpec((1, DC), lambda i: (0, 0))
    return pl.pallas_call(
        body, name="conf_fwd_proj",
        grid=(TP // TM,),
        in_specs=[row, pl.BlockSpec((TM, DC), lambda i: (i, 4)), vec, vec,
                  pl.BlockSpec((DC, DC), lambda i: (0, 0)), vec],
        out_specs=[row, row],
        out_shape=[jax.ShapeDtypeStruct((TP, DC), BF16), jax.ShapeDtypeStruct((TP, DC), F32)],
        scratch_shapes=[pltpu.VMEM((TM, DC), BF16)],
        compiler_params=_cparams(),
    )(vc, z, ln_w, ln_b, pw_w, pw_b)


def _outproj_loss(ylru, yconf, w_out, h, target, post_w):
    def body(yl_ref, yc_ref, w_ref, h_ref, tgt_hbm, pw_ref, dout_ref, dy_ref, loss_ref, dpw_ref, y_s, t_ref, sem):
        i = pl.program_id(0)
        k = pl.program_id(1)

        @pl.when(k == 0)
        def _():
            _for_tile(i, lambda t: _token_tile_copy(tgt_hbm, t_ref, sem, t).start())
            y_s[...] = jnp.dot(yl_ref[...], w_ref[...], preferred_element_type=F32)

        @pl.when(k == 1)
        def _():
            y_s[...] += jnp.dot(yc_ref[...], w_ref[...], preferred_element_type=F32)

        @pl.when(jnp.logical_and(i == 0, k == 1))
        def _():
            loss_ref[...] = jnp.zeros_like(loss_ref)
            dpw_ref[...] = jnp.zeros_like(dpw_ref)

        @pl.when(k == 1)
        def _():
            _for_tile(i, lambda t: _token_tile_copy(tgt_hbm, t_ref, sem, t).wait())

            @pl.when(i == 0)
            def _():
                t_ref[0:NMETA, :] = jnp.zeros((NMETA, D), F32)

            @pl.when(i == NTILE - 1)
            def _():
                last = _tile_rows(NTILE - 1)[1]
                t_ref[last:TM, :] = jnp.zeros((TM - last, D), F32)

            pw = pw_ref[...]
            row = _row_iota((8, D))

            def chunk(ci, carry):
                r0 = pl.multiple_of(ci * 8, 8)
                yv = y_s[pl.ds(r0, 8), :]
                rs = lax.rsqrt(jnp.mean(yv * yv, axis=-1, keepdims=True) + EPS)
                grow = row + (i * TM + r0)
                valid = jnp.logical_and(grow >= NMETA, grow < T)
                yn = yv * rs
                err = jnp.where(valid, h_ref[pl.ds(r0, 8), :] + yn * pw - t_ref[pl.ds(r0, 8), :], 0.0)
                loss_ref[...] += err * err
                d_rn = err * (1.0 / D)
                dout_ref[pl.ds(r0, 8), :] = d_rn
                dpw_ref[...] += d_rn * yn
                gw = d_rn * pw
                dot = jnp.mean(gw * yv, axis=-1, keepdims=True)
                dy_ref[pl.ds(r0, 8), :] = (rs * gw - yv * (rs * rs * rs * dot)).astype(BF16)
                return carry
            lax.fori_loop(0, TM // 8, chunk, 0, unroll=4)

    row = pl.BlockSpec((TM, D), lambda i, k: (i, 0))
    half = pl.BlockSpec((TM, DL), lambda i, k: (i, 0))
    acc = pl.BlockSpec((8, D), lambda i, k: (0, 0))
    return pl.pallas_call(
        body, name="outproj_loss",
        grid=(TP // TM, 2),
        in_specs=[half, half, pl.BlockSpec((DL, D), lambda i, k: (k, 0)), row, pl.BlockSpec(memory_space=pl.ANY),
                  pl.BlockSpec((1, D), lambda i, k: (0, 0))],
        out_specs=[row, row, acc, acc],
        out_shape=[jax.ShapeDtypeStruct((TP, D), F32), jax.ShapeDtypeStruct((TP, D), BF16),
                   jax.ShapeDtypeStruct((8, D), F32), jax.ShapeDtypeStruct((8, D), F32)],
        scratch_shapes=[pltpu.VMEM((TM, D), F32), pltpu.VMEM((TM, D), F32), pltpu.SemaphoreType.DMA(())],
        compiler_params=_cparams(),
    )(ylru, yconf, w_out, h, target, post_w)


_NT = (((1,), (1,)), ((), ()))
_TN = (((0,), (0,)), ((), ()))


def _outproj_bwd(dy, ylru, yconf, w_out):
    def body(dy_ref, yl_ref, yc_ref, w_ref, dycat_ref, dw_ref):
        j = pl.program_id(0)
        dyv = dy_ref[...]
        dycat_ref[...] = lax.dot_general(dyv, w_ref[...], _NT, preferred_element_type=F32)

        @pl.when(j < NCB)
        def _():
            dw_ref[...] = lax.dot_general(yl_ref[...], dyv, _TN, preferred_element_type=F32).astype(BF16)

        @pl.when(j >= NCB)
        def _():
            dw_ref[...] = lax.dot_general(yc_ref[...], dyv, _TN, preferred_element_type=F32).astype(BF16)

    return pl.pallas_call(
        body, name="outproj_bwd",
        grid=(2 * NCB,),
        in_specs=[pl.BlockSpec((TP, D), lambda j: (0, 0)),
                  pl.BlockSpec((TP, CB), lambda j: (0, jnp.minimum(j, NCB - 1))),
                  pl.BlockSpec((TP, CB), lambda j: (0, jnp.maximum(j - NCB, 0))),
                  pl.BlockSpec((CB, D), lambda j: (j, 0))],
        out_specs=[pl.BlockSpec((TP, CB), lambda j: (0, j)), pl.BlockSpec((CB, D), lambda j: (j, 0))],
        out_shape=[jax.ShapeDtypeStruct((TP, D), F32), jax.ShapeDtypeStruct((D, D), BF16)],
        compiler_params=_cparams(),
    )(dy, ylru, yconf, w_out)


def _conf_bwd_proj(dycat, p, z, vc, ln_w, ln_b, pw_w):
    def body(dy_ref, p_ref, g_ref, vc_ref, lw_ref, lb_ref, w_ref,
             dvc_ref, dgc_ref, dpw_ref, vecs_ref, dp_s, s_s, ds_s):
        i = pl.program_id(0)
        lw, lb = lw_ref[...], lb_ref[...]

        @pl.when(i == 0)
        def _():
            dpw_ref[...] = jnp.zeros_like(dpw_ref)
            vecs_ref[...] = jnp.zeros_like(vecs_ref)

        def pre_chunk(ci, carry):
            r0 = pl.multiple_of(ci * R, R)
            for half in range(2):
                rr = r0 + 8 * half
                dyv = dy_ref[pl.ds(rr, 8), :]
                g = g_ref[pl.ds(rr, 8), :]
                sg = _sig(g)
                dp = dyv * (g * sg)
                dg = dyv * p_ref[pl.ds(rr, 8), :] * (sg * (1.0 + g * (1.0 - sg)))
                vecs_ref[0:8, :] += dp
                vecs_ref[8:16, :] += dg
                ds_s[pl.ds(rr, 8), :] = dp
                dvc_ref[pl.ds(rr, 8), :] = dg
            dp_s[pl.ds(r0, R), :] = ds_s[pl.ds(r0, R), :].astype(BF16)
            dgc_ref[pl.ds(r0, R), :] = dvc_ref[pl.ds(r0, R), :].astype(BF16)
            for half in range(2):
                rr = r0 + 8 * half
                _, _, ln = _ln_chunk(vc_ref[pl.ds(rr, 8), :], lw, lb)
                ds_s[pl.ds(rr, 8), :] = ln * _sig(ln)
            s_s[pl.ds(r0, R), :] = ds_s[pl.ds(r0, R), :].astype(BF16)
            return carry
        lax.fori_loop(0, TM // R, pre_chunk, 0, unroll=2)

        dpb = dp_s[...]
        ds_s[...] = lax.dot_general(dpb, w_ref[...], _NT, preferred_element_type=F32)
        dpw_ref[...] += lax.dot_general(s_s[...], dpb, _TN, preferred_element_type=F32)

        def post_chunk(ci, carry):
            r0 = pl.multiple_of(ci * 8, 8)
            xhat, rstd, ln = _ln_chunk(vc_ref[pl.ds(r0, 8), :], lw, lb)
            sl = _sig(ln)
            dln = ds_s[pl.ds(r0, 8), :] * (sl * (1.0 + ln * (1.0 - sl)))
            vecs_ref[16:24, :] += dln * xhat
            vecs_ref[24:32, :] += dln
            dxh = dln * lw
            m1 = jnp.mean(dxh, axis=-1, keepdims=True)
            m2 = jnp.mean(dxh * xhat, axis=-1, keepdims=True)
            dvc_ref[pl.ds(r0, 8), :] = rstd * (dxh - m1 - xhat * m2)
            return carry
        lax.fori_loop(0, TM // 8, post_chunk, 0, unroll=4)

    row = pl.BlockSpec((TM, DC), lambda i: (i, 0))
    vec = pl.BlockSpec((1, DC), lambda i: (0, 0))
    return pl.pallas_call(
        body, name="conf_bwd_proj",
        grid=(TP // TM,),
        in_specs=[pl.BlockSpec((TM, DC), lambda i: (i, 1)), row, pl.BlockSpec((TM, DC), lambda i: (i, 4)), row,
                  vec, vec, pl.BlockSpec((DC, DC), lambda i: (0, 0))],
        out_specs=[row, row, pl.BlockSpec((DC, DC), lambda i: (0, 0)), pl.BlockSpec((32, DC), lambda i: (0, 0))],
        out_shape=[jax.ShapeDtypeStruct((TP, DC), F32), jax.ShapeDtypeStruct((TP, DC), BF16),
                   jax.ShapeDtypeStruct((DC, DC), F32), jax.ShapeDtypeStruct((32, DC), F32)],
        scratch_shapes=[pltpu.VMEM((TM, DC), BF16), pltpu.VMEM((TM, DC), BF16), pltpu.VMEM((TM, DC), F32)],
        compiler_params=_cparams(),
    )(dycat, p, z, vc, ln_w, ln_b, pw_w)


def _conf_bwd_conv(dvc, z, dw_w):
    def body(dvc_ref, u1_ref, u2_ref, w_ref, du_ref, dw_ref, vecs_ref, vs, dvs):
        vs[pl.ds(0, KWP), :] = jnp.zeros((KWP, CB), F32)
        dvs[pl.ds(TP, KWP), :] = jnp.zeros((KWP, CB), F32)
        dw_ref[...] = jnp.zeros_like(dw_ref)
        vecs_ref[...] = jnp.zeros_like(vecs_ref)

        def fill_chunk(ci, carry):
            r0 = pl.multiple_of(ci * R, R)
            vs[pl.ds(KWP + r0, R), :] = u1_ref[pl.ds(r0, R), :] * _sig(u2_ref[pl.ds(r0, R), :])
            dv = dvc_ref[pl.ds(r0, R), :]
            dvs[pl.ds(r0, R), :] = dv
            vecs_ref[0:8, :] += _fold8(dv)
            return carry
        lax.fori_loop(0, TP // R, fill_chunk, 0)

        def conv_chunk(ci, carry):
            r0 = pl.multiple_of(ci * R, R)
            vbuf = vs[pl.ds(r0, KWP + R), :]
            dbuf = dvs[pl.ds(r0, KWP + R), :]
            dcur = dbuf[0:R, :]
            dv = jnp.zeros((R, CB), F32)
            for rr in range(8):
                vroll = vbuf if rr == 0 else pltpu.roll(vbuf, rr, 0)
                droll = dbuf if rr == 0 else pltpu.roll(dbuf, KWP + R - rr, 0)
                for q in range(4):
                    s = 8 * q + rr
                    if s > KW - 1:
                        continue
                    k = KW - 1 - s
                    dv = dv + droll[8 * q:8 * q + R, :] * w_ref[k:k + 1, :]
                    dw_ref[8 * k:8 * k + 8, :] += _fold8(dcur * vroll[KWP - 8 * q:KWP - 8 * q + R, :])
            u1 = u1_ref[pl.ds(r0, R), :]
            sg = _sig(u2_ref[pl.ds(r0, R), :])
            du1 = dv * sg
            du2 = dv * u1 * (sg * (1.0 - sg))
            du_ref[0, pl.ds(r0, R), :] = du1.astype(BF16)
            du_ref[1, pl.ds(r0, R), :] = du2.astype(BF16)
            vecs_ref[8:16, :] += _fold8(du1)
            vecs_ref[16:24, :] += _fold8(du2)
            return carry
        lax.fori_loop(0, TP // R, conv_chunk, 0)

    blk = pl.BlockSpec((TP, CB), lambda j: (0, j))
    return pl.pallas_call(
        body, name="conf_bwd_conv",
        grid=(NCB,),
        in_specs=[blk, pl.BlockSpec((TP, CB), lambda j: (0, 2 * NCB + j)),
                  pl.BlockSpec((TP, CB), lambda j: (0, 3 * NCB + j)), pl.BlockSpec((KWP, CB), lambda j: (0, j))],
        out_specs=[pl.BlockSpec((2, TP, CB), lambda j: (0, 0, j)), pl.BlockSpec((8 * KWP, CB), lambda j: (0, j)),
                   pl.BlockSpec((24, CB), lambda j: (0, j))],
        out_shape=[jax.ShapeDtypeStruct((2, TP, DC), BF16),
                   jax.ShapeDtypeStruct((8 * KWP, DC), F32), jax.ShapeDtypeStruct((24, DC), F32)],
        scratch_shapes=[pltpu.VMEM((TP + KWP, CB), F32), pltpu.VMEM((TP + KWP, CB), F32)],
        compiler_params=_cparams(),
    )(dvc, z, z, dw_w)


def _lru_bwd(dycat, z, xc, hs, conv_w, wa_g, b_a, wx_g, b_x, lam):
    NV = 6

    def body(dy_ref, x_ref, g_ref, xc_ref, hs_ref, cw_ref, wa_ref, ba_ref, wx_ref, bx_ref, lam_ref,
             dzl_ref, dwa_ref, dwx_ref, dcw_ref, vecs_ref, ga_s, gx_s, dxc_s):
        vecs_ref[...] = jnp.zeros_like(vecs_ref)
        dcw_ref[...] = jnp.zeros_like(dcw_ref)
        dxc_s[pl.ds(TP, 8), :] = jnp.zeros((8, CB), F32)

        def gate_chunk(ci, carry):
            r0 = pl.multiple_of(ci * TM, TM)
            xb = xc_ref[pl.ds(r0, TM), :].astype(BF16)
            ga_s[pl.ds(r0, TM), :] = jnp.dot(xb, wa_ref[...], preferred_element_type=F32) + ba_ref[...]
            gx_s[pl.ds(r0, TM), :] = jnp.dot(xb, wx_ref[...], preferred_element_type=F32) + bx_ref[...]
            return carry
        lax.fori_loop(0, TP // TM, gate_chunk, 0)

        sp8 = LRU_C * _softplus(-lam_ref[...])
        row = _row_iota((R, CB))
        nchunk = TP // R

        def scan_chunk(cj, carry):
            a_next, lam_next = carry
            ci = nchunk - 1 - cj
            r0 = pl.multiple_of(ci * R, R)
            dyv = dy_ref[pl.ds(r0, R), :]
            g = g_ref[pl.ds(r0, R), :]
            hv = hs_ref[pl.ds(r0, R), :]
            xc = xc_ref[pl.ds(r0, R), :]
            sg = _sig(g)
            dgl = dyv * hv * (sg * (1.0 + g * (1.0 - sg)))
            dzl_ref[1, pl.ds(r0, R), :] = dgl.astype(BF16)
            vecs_ref[0:8, :] += _fold8(dgl)
            dhs = dyv * (g * sg)
            r, i, a, mult = _gate_values(ga_s[pl.ds(r0, R), :], gx_s[pl.ds(r0, R), :], xc, sp8)
            b = jnp.where(row == R - 1, a_next, pltpu.roll(a, R - 1, 0))
            lv = dhs
            k = 1
            while k < R:
                m = row < R - k
                lv = jnp.where(m, lv + b * pltpu.roll(lv, R - k, 0), lv)
                b = jnp.where(m, b * pltpu.roll(b, R - k, 0), b)
                k *= 2
            lv = lv + b * lam_next
            p0 = pl.multiple_of(jnp.maximum(r0 - 8, 0), 8)
            hprev8 = jnp.where(ci > 0, hs_ref[pl.ds(p0, 8), :], 0.0)
            hprev = pltpu.roll(jnp.concatenate([hprev8, hv], axis=0), 1, 0)[8:8 + R, :]
            da = lv * hprev
            ixc = i * xc
            dmult = lv * ixc
            di = lv * mult * xc
            dxc_s[pl.ds(r0, R), :] = lv * mult * i
            a2 = a * a
            dlog_a = da * a - dmult * a2 / mult
            vecs_ref[32:40, :] += _fold8(dlog_a * r)
            dga = -(dlog_a * sp8) * r * (1.0 - r)
            dgx = di * i * (1.0 - i)
            ga_s[pl.ds(r0, R), :] = dga
            gx_s[pl.ds(r0, R), :] = dgx
            vecs_ref[16:24, :] += _fold8(dga)
            vecs_ref[24:32, :] += _fold8(dgx)
            a_first = jnp.sum(jnp.where(row == 0, a, 0.0), axis=0, keepdims=True)
            l_first = jnp.sum(jnp.where(row == 0, lv, 0.0), axis=0, keepdims=True)
            return a_first, l_first
        lax.fori_loop(0, nchunk, scan_chunk, (jnp.zeros((1, CB), F32), jnp.zeros((1, CB), F32)))

        dwa_ref[...] = jnp.zeros_like(dwa_ref)
        dwx_ref[...] = jnp.zeros_like(dwx_ref)

        def mm_chunk(ci, carry):
            r0 = pl.multiple_of(ci * TM, TM)
            xb = xc_ref[pl.ds(r0, TM), :].astype(BF16)
            dgab = ga_s[pl.ds(r0, TM), :].astype(BF16)
            dgxb = gx_s[pl.ds(r0, TM), :].astype(BF16)
            dxc_s[pl.ds(r0, TM), :] += (lax.dot_general(dgab, wa_ref[...], _NT, preferred_element_type=F32)
                                        + lax.dot_general(dgxb, wx_ref[...], _NT, preferred_element_type=F32))
            dwa_ref[...] += lax.dot_general(xb, dgab, _TN, preferred_element_type=F32)
            dwx_ref[...] += lax.dot_general(xb, dgxb, _TN, preferred_element_type=F32)
            return carry
        lax.fori_loop(0, TP // TM, mm_chunk, 0)

        taps = [cw_ref[k:k + 1, :] for k in range(LW)]

        def conv_chunk(ci, carry):
            r0 = pl.multiple_of(ci * R, R)
            dbuf = dxc_s[pl.ds(r0, R + 8), :]
            dcur = dbuf[0:R, :]
            p0 = pl.multiple_of(jnp.maximum(r0 - 8, 0), 8)
            xprev = jnp.where(ci > 0, x_ref[pl.ds(p0, 8), :], 0.0)
            xbuf = jnp.concatenate([xprev, x_ref[pl.ds(r0, R), :]], axis=0)
            dxl = dcur * taps[LW - 1]
            dcw_ref[8 * (LW - 1):8 * LW, :] += _fold8(dcur * xbuf[8:8 + R, :])
            for s in range(1, LW):
                k = LW - 1 - s
                dxl = dxl + pltpu.roll(dbuf, R + 8 - s, 0)[0:R, :] * taps[k]
                dcw_ref[8 * k:8 * k + 8, :] += _fold8(dcur * pltpu.roll(xbuf, s, 0)[8:8 + R, :])
            dzl_ref[0, pl.ds(r0, R), :] = dxl.astype(BF16)
            vecs_ref[8:16, :] += _fold8(dxl)
            vecs_ref[40:48, :] += _fold8(dcur)
            return carry
        lax.fori_loop(0, TP // R, conv_chunk, 0)
        vecs_ref[32:40, :] = vecs_ref[32:40, :] * (LRU_C * _sig(-lam_ref[...]))

    col = lambda off: pl.BlockSpec((TP, CB), lambda j: (0, off + j))
    vec = pl.BlockSpec((1, CB), lambda j: (0, j))
    wsp = pl.BlockSpec((None, CB, CB), lambda j: (j, 0, 0))
    return pl.pallas_call(
        body, name="lru_bwd",
        grid=(NCB,),
        in_specs=[col(0), col(0), col(NCB), col(0), col(0), pl.BlockSpec((LW, CB), lambda j: (0, j)),
                  wsp, vec, wsp, vec, vec],
        out_specs=[pl.BlockSpec((2, TP, CB), lambda j: (0, 0, j)), wsp, wsp,
                   pl.BlockSpec((8 * LW, CB), lambda j: (0, j)), pl.BlockSpec((8 * NV, CB), lambda j: (0, j))],
        out_shape=[jax.ShapeDtypeStruct((2, TP, DL), BF16),
                   jax.ShapeDtypeStruct((NCB, CB, CB), F32), jax.ShapeDtypeStruct((NCB, CB, CB), F32),
                   jax.ShapeDtypeStruct((8 * LW, DL), F32), jax.ShapeDtypeStruct((8 * NV, DL), F32)],
        scratch_shapes=[pltpu.VMEM((TP, CB), F32), pltpu.VMEM((TP, CB), F32), pltpu.VMEM((TP + 8, CB), F32)],
        compiler_params=_cparams(),
    )(dycat, z, z, xc, hs, conv_w, wa_g, b_a, wx_g, b_x, lam)


def _dz_section(sec, dzl_ref, dzc_ref, dgc_ref, use):
    @pl.when(sec < 2)
    def _():
        use(dzl_ref)

    @pl.when(jnp.logical_and(sec >= 2, sec < 4))
    def _():
        use(dzc_ref)

    @pl.when(sec == 4)
    def _():
        use(dgc_ref)


def _dz_specs(rows, index):
    return [pl.BlockSpec((None, rows, 1024), lambda a, b: (jnp.minimum(index(a, b)[1], 1), index(a, b)[0], 0)),
            pl.BlockSpec((None, rows, 1024), lambda a, b: (jnp.clip(index(a, b)[1] - 2, 0, 1), index(a, b)[0], 0)),
            pl.BlockSpec((rows, 1024), lambda a, b: (index(a, b)[0], 0))]


def _inproj_wgrad(name, hn, dzs):
    KB = 512
    nsec = dzs.shape[0]

    def body(hn_ref, dz_ref, dw_ref):
        dw_ref[...] = lax.dot_general(hn_ref[...], dz_ref[...], _TN, preferred_element_type=F32).astype(BF16)

    return pl.pallas_call(
        body, name=name,
        grid=(nsec, D // KB),
        in_specs=[pl.BlockSpec((TP, KB), lambda n, kb: (0, kb)),
                  pl.BlockSpec((None, TP, 1024), lambda n, kb: (n, 0, 0))],
        out_specs=pl.BlockSpec((KB, 1024), lambda n, kb: (kb, n)),
        out_shape=jax.ShapeDtypeStruct((D, nsec * 1024), BF16),
        compiler_params=_cparams(),
    )(hn, dzs)


def _sum_win_parts(parts_a, parts_b, parts_c):
    RB = 64

    def body(a_ref, b_ref, c_ref, o_ref):
        def chunk(ci, carry):
            r0 = pl.multiple_of(ci * R, R)
            for ref, base, ncol in ((a_ref, 0, 2048), (b_ref, 2048, 2048), (c_ref, 4096, 1024)):
                for c0 in range(0, ncol, 512):
                    acc = ref[0, pl.ds(r0, R), c0:c0 + 512].astype(F32)
                    for sidx in range(1, NDEV):
                        acc = acc + ref[sidx, pl.ds(r0, R), c0:c0 + 512].astype(F32)
                    o_ref[pl.ds(r0, R), base + c0:base + c0 + 512] = acc.astype(BF16)
            return carry
        lax.fori_loop(0, RB // R, chunk, 0)

    spec = lambda ncol: pl.BlockSpec((NDEV, RB, ncol), lambda i: (0, i, 0))
    return pl.pallas_call(
        body, name="sum_win_parts",
        grid=(D // NDEV // RB,),
        in_specs=[spec(2048), spec(2048), spec(1024)],
        out_specs=pl.BlockSpec((RB, NIN), lambda i: (i, 0)),
        out_shape=jax.ShapeDtypeStruct((D // NDEV, NIN), BF16),
        compiler_params=_cparams(),
    )(parts_a, parts_b, parts_c)


def _inproj_bwd(dzl, dzc, dgc, w_in, h, dout, pre_w):
    nsec = NIN // 1024

    def body(dzl_ref, dzc_ref, dgc_ref, w_ref, h_ref, dout_ref, pw_ref, gx_hbm, dmeta_ref, dpw_ref, acc_s, dh_s, sem):
        i = pl.program_id(0)
        s = pl.program_id(1)

        def gx_copy(t):
            lo, n, off = _tile_rows(t)
            return pltpu.make_async_copy(dh_s.at[pl.ds(off, n)], gx_hbm.at[pl.ds(lo, n)], sem)

        @pl.when(s == 0)
        def _():
            acc_s[...] = jnp.zeros_like(acc_s)

        def use(dz_ref):
            acc_s[...] += lax.dot_general(dz_ref[...], w_ref[...], _NT, preferred_element_type=F32)
        _dz_section(s, dzl_ref, dzc_ref, dgc_ref, use)

        @pl.when(jnp.logical_and(i == 0, s == nsec - 1))
        def _():
            dpw_ref[...] = jnp.zeros_like(dpw_ref)

        @pl.when(s == nsec - 1)
        def _():
            _for_tile(i - 1, lambda t: gx_copy(t).wait())
            pw = pw_ref[...]

            def chunk(ci, carry):
                r0 = pl.multiple_of(ci * 8, 8)
                hv = h_ref[pl.ds(r0, 8), :]
                dhn = acc_s[pl.ds(r0, 8), :]
                rs = lax.rsqrt(jnp.mean(hv * hv, axis=-1, keepdims=True) + EPS)
                dpw_ref[...] += dhn * (hv * rs)
                gw = dhn * pw
                dot = jnp.mean(gw * hv, axis=-1, keepdims=True)
                dh_s[pl.ds(r0, 8), :] = rs * gw - hv * (rs * rs * rs * dot) + dout_ref[pl.ds(r0, 8), :]
                return carry
            lax.fori_loop(0, TM // 8, chunk, 0, unroll=4)
            _for_tile(i, lambda t: gx_copy(t).start())

            @pl.when(i == 0)
            def _():
                dmeta_ref[...] = dh_s[0:NMETA, :]

            @pl.when(i == NTILE - 1)
            def _():
                gx_copy(NTILE - 1).wait()

    row = pl.BlockSpec((TM, D), lambda i, s: (i, 0))
    return pl.pallas_call(
        body, name="inproj_bwd",
        grid=(TP // TM, nsec),
        in_specs=_dz_specs(TM, lambda i, s: (i, s)) + [
            pl.BlockSpec((D, 1024), lambda i, s: (0, s)), row, row, pl.BlockSpec((1, D), lambda i, s: (0, 0))],
        out_specs=[pl.BlockSpec(memory_space=pl.ANY), pl.BlockSpec((NMETA, D), lambda i, s: (0, 0)),
                   pl.BlockSpec((8, D), lambda i, s: (0, 0))],
        out_shape=[jax.ShapeDtypeStruct((SEQ, D), F32), jax.ShapeDtypeStruct((NMETA, D), F32),
                   jax.ShapeDtypeStruct((8, D), F32)],
        scratch_shapes=[pltpu.VMEM((TM, D), F32), pltpu.VMEM((TM, D), F32), pltpu.SemaphoreType.DMA(())],
        compiler_params=_cparams(),
    )(dzl, dzc, dgc, w_in, h, dout, pre_w)


def _adamw(name, parts, w, m, v, block_rows):
    rows, cols = w.shape
    nparts = parts.shape[0]
    cw = cols if cols <= 640 else 512

    def body(p_ref, w_ref, m_ref, v_ref, g_ref, d_ref, nm_ref, nv_ref):
        def chunk(ci, carry):
            r0 = pl.multiple_of(ci * R, R)
            for c0 in range(0, cols, cw):
                at = (pl.ds(r0, R), slice(c0, c0 + cw))
                g = p_ref[(0,) + at].astype(F32)
                for sidx in range(1, nparts):
                    g = g + p_ref[(sidx,) + at].astype(F32)
                delta, mv, vv = _adam_math(g, w_ref[at], m_ref[at], v_ref[at])
                g_ref[at] = g
                nm_ref[at] = mv
                nv_ref[at] = vv
                d_ref[at] = delta
            return carry
        lax.fori_loop(0, block_rows // R, chunk, 0)

    blk = pl.BlockSpec((block_rows, cols), lambda i: (i, 0))
    shp = jax.ShapeDtypeStruct((rows, cols), F32)
    return pl.pallas_call(
        body, name=name,
        grid=(rows // block_rows,),
        in_specs=[pl.BlockSpec((nparts, block_rows, cols), lambda i: (0, i, 0)), blk, blk, blk],
        out_specs=[blk, blk, blk, blk],
        out_shape=[shp, shp, shp, shp],
        compiler_params=_cparams(),
    )(parts, w, m, v)


def _adam_math(g, w, m, v):
    c1 = 1.0 / (1.0 - ADAM_B1 ** ADAM_STEP)
    c2 = 1.0 / (1.0 - ADAM_B2 ** ADAM_STEP)
    mv = ADAM_B1 * m + (1.0 - ADAM_B1) * g
    vv = ADAM_B2 * v + (1.0 - ADAM_B2) * (g * g)
    upd = (mv * c1) / (jnp.sqrt(vv * c2) + ADAM_EPS) + ADAM_WD * w
    return -ADAM_LR * upd, mv, vv


_VEC = [("pre_norm_w", 2), ("post_norm_w", 2), ("b_in", 5), ("lru_conv_b", 1), ("b_gate_a", 1), ("b_gate_x", 1),
        ("lru_lambda", 1), ("conf_dw_b", 1), ("conf_ln_w", 1), ("conf_ln_b", 1), ("conf_pw_b", 1)]
_VEC_ROWS = 24
_LOSS_ROW = 17
_SM_ROWS = 64


def _pack_grads(dprew_acc, dpostw_acc, cvecs, kvecs, lvecs, dcw_acc, ddw_acc, dh, loss_acc):
    def body(pre_ref, post_ref, c_ref, k_ref, l_ref, dcw_ref, ddw_ref, dh_ref, loss_ref, vec_ref, small_ref, tmp):
        s8 = lambda ref, r: jnp.sum(ref[8 * r:8 * r + 8, :], axis=0, keepdims=True)
        vec_ref[...] = jnp.zeros_like(vec_ref)
        pre, post = s8(pre_ref, 0), s8(post_ref, 0)
        rows = [pre[:, 0:1024], pre[:, 1024:2048], post[:, 0:1024], post[:, 1024:2048],
                s8(l_ref, 1), s8(l_ref, 0), s8(k_ref, 1), s8(k_ref, 2), s8(c_ref, 1),
                s8(l_ref, 5), s8(l_ref, 2), s8(l_ref, 3), s8(l_ref, 4),
                s8(k_ref, 0), s8(c_ref, 2), s8(c_ref, 3), s8(c_ref, 0)]
        for r, val in enumerate(rows):
            vec_ref[r:r + 1, :] = val
        vec_ref[_LOSS_ROW:_LOSS_ROW + 1, :] = jnp.zeros((1, 1024), F32) + (0.5 / D) * jnp.sum(loss_ref[...])

        small_ref[...] = jnp.zeros_like(small_ref)
        for k in range(LW):
            tmp[k:k + 1, :] = s8(dcw_ref, k)
        for k in range(KW):
            tmp[8 + k:9 + k, :] = s8(ddw_ref, k)
        for d in range(NDEV):
            small_ref[d, 0:LW, 0:128] = tmp[0:LW, 128 * d:128 * d + 128]
            small_ref[d, 8:8 + KW, 0:128] = tmp[8:8 + KW, 128 * d:128 * d + 128]
            small_ref[d, 40:56, :] = dh_ref[:, 256 * d:256 * d + 256]

    full = lambda a: pl.BlockSpec(a.shape, lambda i: (0,) * a.ndim)
    ins = [dprew_acc, dpostw_acc, cvecs, kvecs, lvecs, dcw_acc, ddw_acc]
    return pl.pallas_call(
        body, name="pack_grads",
        grid=(1,),
        in_specs=[full(a) for a in ins] + [full(dh), full(loss_acc)],
        out_specs=[pl.BlockSpec((_VEC_ROWS, 1024), lambda i: (0, 0)),
                   pl.BlockSpec((NDEV, _SM_ROWS, 256), lambda i: (0, 0, 0))],
        out_shape=[jax.ShapeDtypeStruct((_VEC_ROWS, 1024), F32), jax.ShapeDtypeStruct((NDEV, _SM_ROWS, 256), F32)],
        scratch_shapes=[pltpu.VMEM((40, 1024), F32)],
        compiler_params=_cparams(),
    )(*ins, dh, loss_acc)


def _adamw_vec(parts, W, M, V):
    nv = len(_VEC)

    def body(*refs):
        p_ref = refs[0]
        w_refs, m_refs, v_refs = refs[1:1 + nv], refs[1 + nv:1 + 2 * nv], refs[1 + 2 * nv:1 + 3 * nv]
        outs = refs[1 + 3 * nv:]

        def total(r):
            acc = p_ref[0, r:r + 1, :]
            for sidx in range(1, NDEV):
                acc = acc + p_ref[sidx, r:r + 1, :]
            return acc

        row = 0
        for idx, (_, nrows) in enumerate(_VEC):
            for part in range(nrows):
                cols = slice(1024 * part, 1024 * part + 1024)
                g = total(row + part)
                delta, mv, vv = _adam_math(g, w_refs[idx][:, cols], m_refs[idx][:, cols], v_refs[idx][:, cols])
                for o, val in zip(outs[4 * idx:4 * idx + 4], (g, delta, mv, vv)):
                    o[:, cols] = val
            row += nrows
        outs[-1][...] = total(_LOSS_ROW)[:, 0:128]

    names = [n for n, _ in _VEC]
    flat = lambda d: [d[n].reshape(1, -1) for n in names]
    ws, ms, vs = flat(W), flat(M), flat(V)
    res = pl.pallas_call(
        body, name="adamw_vec",
        out_shape=[jax.ShapeDtypeStruct(w.shape, F32) for w in ws for _ in range(4)]
        + [jax.ShapeDtypeStruct((1, 128), F32)],
        compiler_params=_cparams(),
    )(parts, *ws, *ms, *vs)
    return {n: tuple(res[4 * i:4 * i + 4]) for i, n in enumerate(names)}, res[-1]


def _adamw_small(parts, W, M, V):
    where = {"lru_conv_w": (slice(0, LW), slice(0, 128)), "conf_dw_w": (slice(8, 8 + KW), slice(0, 128)),
             "meta_tokens": (slice(40, 56), slice(0, 256))}
    names = list(where)

    def body(*refs):
        p_ref = refs[0]
        outs = refs[10:]
        for idx, n in enumerate(names):
            rs, cs = where[n]
            g = p_ref[0, rs, cs]
            for sidx in range(1, NDEV):
                g = g + p_ref[sidx, rs, cs]
            delta, mv, vv = _adam_math(g, refs[1 + idx][...], refs[4 + idx][...], refs[7 + idx][...])
            for o, val in zip(outs[4 * idx:4 * idx + 4], (g, delta, mv, vv)):
                o[...] = val

    two_d = lambda a: a.reshape(a.shape[-2:])
    ws, ms, vs = ([two_d(d[n]) for n in names] for d in (W, M, V))
    res = pl.pallas_call(
        body, name="adamw_small",
        out_shape=[jax.ShapeDtypeStruct(w.shape, F32) for w in ws for _ in range(4)],
        compiler_params=_cparams(),
    )(parts, *ws, *ms, *vs)
    return {n: tuple(res[4 * i:4 * i + 4]) for i, n in enumerate(names)}


def _pack_small(lru_cw, dw_w, meta):
    buf = jnp.zeros((_SM_ROWS, 256), F32)
    buf = buf.at[0:LW, 0:128].set(lru_cw)
    buf = buf.at[8:8 + dw_w.shape[0], 0:128].set(dw_w)
    return buf.at[40:56, :].set(meta)


def _block_diag4(w):
    w4 = w.reshape(NCB, 4, 64, 64)
    eye = jnp.eye(4, dtype=w.dtype)
    return jnp.einsum("ghij,hk->ghikj", w4, eye).reshape(NCB, CB, CB)


def _diag_blocks(g):
    g5 = g.reshape(NCB, 4, 64, 4, 64)
    return jnp.stack([g5[:, hh, :, hh, :] for hh in range(4)], axis=1).reshape(16, 64, 64)


def _local_step(x, target, meta_full, inproj, out_weights, lru_cw_full, dw_w_full, W, send):
    wa_g = _block_diag4(W["w_gate_a"][0]).astype(BF16)
    wx_g = _block_diag4(W["w_gate_x"][0]).astype(BF16)

    h, hn = _prenorm(x, meta_full, W["pre_norm_w"])
    z, win_full = inproj(hn)
    ylru, xc, hs = _lru_fwd(z, lru_cw_full, W["lru_conv_b"], wa_g, W["b_gate_a"], wx_g, W["b_gate_x"],
                            W["lru_lambda"])
    vc = _conf_fwd_conv(z, dw_w_full, W["conf_dw_b"])
    wout_full, pw_full = out_weights(vc)
    yconf, p = _conf_fwd_proj(vc, z, W["conf_ln_w"], W["conf_ln_b"], pw_full, W["conf_pw_b"])
    dout, dy, loss_acc, dpostw_acc = _outproj_loss(ylru, yconf, wout_full, h, target, W["post_norm_w"])

    dycat, dwout_part = _outproj_bwd(dy, ylru, yconf, wout_full)
    tok = send("w_out", dwout_part)
    dvc, dgc, dpw_part, cvecs = _conf_bwd_proj(dycat, p, z, vc, W["conf_ln_w"] + tok, W["conf_ln_b"], pw_full)
    tok = send("conf_pw_w", dpw_part)
    tok = tok + send("w_in_c", _inproj_wgrad("inproj_wgrad_c", hn, dgc[None]))
    dzc, ddw_acc, kvecs = _conf_bwd_conv(dvc, z, dw_w_full + tok)
    tok = send("w_in_b", _inproj_wgrad("inproj_wgrad_b", hn, dzc))
    dzl, dwa_g, dwx_g, dcw_acc, lvecs = _lru_bwd(dycat, z, xc, hs, lru_cw_full, wa_g, W["b_gate_a"] + tok, wx_g,
                                                 W["b_gate_x"], W["lru_lambda"])
    tok = send("w_in_a", _inproj_wgrad("inproj_wgrad_a", hn, dzl))
    tok = tok + send("w_gates", _diag_blocks(dwa_g).reshape(16 * 64, 64), _diag_blocks(dwx_g).reshape(16 * 64, 64))
    grad_x, dmeta, dprew_acc = _inproj_bwd(dzl, dzc, dgc, win_full, h, dout, W["pre_norm_w"] + tok)

    vec_pack, small_part = _pack_grads(dprew_acc, dpostw_acc, cvecs, kvecs, lvecs, dcw_acc, ddw_acc, dmeta, loss_acc)
    return grad_x, vec_pack, small_part


def kernel(x, meta_tokens, pre_norm_w, post_norm_w, w_in, b_in, lru_conv_w, lru_conv_b, w_gate_a, b_gate_a, w_gate_x, b_gate_x, lru_lambda, conf_dw_w, conf_dw_b, conf_ln_w, conf_ln_b, conf_pw_w, conf_pw_b, w_out, loss_target, m_meta_tokens, m_pre_norm_w, m_post_norm_w, m_w_in, m_b_in, m_lru_conv_w, m_lru_conv_b, m_w_gate_a, m_b_gate_a, m_w_gate_x, m_b_gate_x, m_lru_lambda, m_conf_dw_w, m_conf_dw_b, m_conf_ln_w, m_conf_ln_b, m_conf_pw_w, m_conf_pw_b, m_w_out, v_meta_tokens, v_pre_norm_w, v_post_norm_w, v_w_in, v_b_in, v_lru_conv_w, v_lru_conv_b, v_w_gate_a, v_b_gate_a, v_w_gate_x, v_b_gate_x, v_lru_lambda, v_conf_dw_w, v_conf_dw_b, v_conf_ln_w, v_conf_ln_b, v_conf_pw_w, v_conf_pw_b, v_w_out):
    W = dict(meta_tokens=meta_tokens, pre_norm_w=pre_norm_w, post_norm_w=post_norm_w, w_in=w_in, b_in=b_in,
             lru_conv_w=lru_conv_w, lru_conv_b=lru_conv_b, w_gate_a=w_gate_a, b_gate_a=b_gate_a,
             w_gate_x=w_gate_x, b_gate_x=b_gate_x, lru_lambda=lru_lambda, conf_dw_w=conf_dw_w,
             conf_dw_b=conf_dw_b, conf_ln_w=conf_ln_w, conf_ln_b=conf_ln_b, conf_pw_w=conf_pw_w,
             conf_pw_b=conf_pw_b, w_out=w_out)
    M = dict(meta_tokens=m_meta_tokens, pre_norm_w=m_pre_norm_w, post_norm_w=m_post_norm_w, w_in=m_w_in,
             b_in=m_b_in, lru_conv_w=m_lru_conv_w, lru_conv_b=m_lru_conv_b, w_gate_a=m_w_gate_a,
             b_gate_a=m_b_gate_a, w_gate_x=m_w_gate_x, b_gate_x=m_b_gate_x, lru_lambda=m_lru_lambda,
             conf_dw_w=m_conf_dw_w, conf_dw_b=m_conf_dw_b, conf_ln_w=m_conf_ln_w, conf_ln_b=m_conf_ln_b,
             conf_pw_w=m_conf_pw_w, conf_pw_b=m_conf_pw_b, w_out=m_w_out)
    V = dict(meta_tokens=v_meta_tokens, pre_norm_w=v_pre_norm_w, post_norm_w=v_post_norm_w, w_in=v_w_in,
             b_in=v_b_in, lru_conv_w=v_lru_conv_w, lru_conv_b=v_lru_conv_b, w_gate_a=v_w_gate_a,
             b_gate_a=v_b_gate_a, w_gate_x=v_w_gate_x, b_gate_x=v_b_gate_x, lru_lambda=v_lru_lambda,
             conf_dw_w=v_conf_dw_w, conf_dw_b=v_conf_dw_b, conf_ln_w=v_conf_ln_w, conf_ln_b=v_conf_ln_b,
             conf_pw_w=v_conf_pw_w, conf_pw_b=v_conf_pw_b, w_out=v_w_out)
    names = list(W.keys())
    shapes = {n: W[n].shape for n in names}

    small = _pack_small(lru_conv_w[0], conf_dw_w[0], meta_tokens)
    (small_flight,), tok = _exchange_start("gather_small_start", [
        (small, jax.ShapeDtypeStruct((NDEV, _SM_ROWS, 256), F32), _whole, _slot)])
    win_flight, tok = _win_gather_start(w_in[0].astype(BF16) + tok[0, 0].astype(BF16))
    gathered, tok = _exchange_start("gather_out_start", [
        (w_out[0].astype(BF16) + tok[0, 0].astype(BF16), jax.ShapeDtypeStruct((D, D), BF16), _whole,
         _rows(D // NDEV)),
        (conf_pw_w[0].astype(BF16), jax.ShapeDtypeStruct((DC, DC), BF16), _whole, _rows(DC // NDEV)),
    ])
    (small_all,) = _exchange_wait("gather_small_wait", [small_flight], tok)
    unshard = lambda a: jnp.transpose(a, (1, 0, 2)).reshape(a.shape[1], -1)
    lru_cw_full = unshard(small_all[:, 0:LW, 0:128])
    dw_w_full = unshard(small_all[:, 8:8 + KWP, 0:128])
    meta_full = unshard(small_all[:, 40:56, :])

    def out_weights(after):
        return _exchange_wait("gather_out_wait", gathered, after)

    def inproj(hn):
        xi, yi, ci = lax.axis_index("x"), lax.axis_index("y"), lax.axis_index("c")
        shard = lambda px, py, pc: (4 * px + 2 * py + pc).astype(jnp.int32)
        here = jnp.stack([shard(xi, yi, ci), shard(xi, yi, 1 - ci)])
        over_links = jnp.stack([shard(1 - xi, yi, ci), shard(xi, 1 - yi, ci), shard(1 - xi, 1 - yi, ci)])
        flight = _win_gather_early(win_flight)
        z, land = _inproj_cols("inproj_here", here, hn, flight["land"], b_in, None)
        flight = _win_gather_forward(dict(flight, land=land), z)
        z, land = _inproj_cols("inproj_links", over_links, hn, flight["land"], b_in, z)
        land = _win_gather_wait(dict(flight, land=land))
        return _inproj_cols("inproj_sibling", over_links + 1 - 2 * ci, hn, land, b_in, z)

    row_stage = lambda ncol: (jax.ShapeDtypeStruct((NDEV, D // NDEV, ncol), BF16), _rows(D // NDEV))
    piece = {"w_in_a": row_stage(2048), "w_in_b": row_stage(2048), "w_in_c": row_stage(1024),
             "w_out": row_stage(D),
             "conf_pw_w": (jax.ShapeDtypeStruct((NDEV, DC // NDEV, DC), BF16), _rows(DC // NDEV)),
             "w_gates": (jax.ShapeDtypeStruct((NDEV, 16 * 64, 64), BF16), _whole)}
    sent = {}

    def send(name, *parts):
        handles, token = _exchange_start(
            "scatter_" + name + "_start",
            [(part.astype(BF16), piece[name][0], piece[name][1], _slot) for part in parts])
        sent[name] = handles
        return token[0, 0]

    grad_x, vec_pack, small_part = _local_step(
        x[0], loss_target[0], meta_full, inproj, out_weights, lru_cw_full, dw_w_full, W, send)
    grad_x = grad_x[None]

    (parts_c,) = _exchange_wait("scatter_w_in_c_wait", sent["w_in_c"], vec_pack)
    (parts_b,) = _exchange_wait("scatter_w_in_b_wait", sent["w_in_b"], parts_c)
    (parts_a,) = _exchange_wait("scatter_w_in_a_wait", sent["w_in_a"], parts_b)
    win_rows = _sum_win_parts(parts_a, parts_b, parts_c)
    win_stage2, tok = _exchange_start("scatter_w_in_stage2_start", [
        (win_rows, jax.ShapeDtypeStruct((NDEV, D // NDEV, NIN // NDEV), BF16), _cols(NIN // NDEV), _slot)])
    rest, _ = _exchange_start("scatter_rest_start", [
        (small_part, jax.ShapeDtypeStruct((NDEV, _SM_ROWS, 256), F32), _slot, _slot),
        (vec_pack + tok[0, 0], jax.ShapeDtypeStruct((NDEV, _VEC_ROWS, 1024), F32), _whole, _slot),
    ])

    G, DW, NM, NV = {}, {}, {}, {}
    (wout_parts,) = _exchange_wait("scatter_w_out_wait", sent["w_out"], win_rows)
    G["w_out"], DW["w_out"], NM["w_out"], NV["w_out"] = _adamw("adamw_w_out", wout_parts, w_out[0], m_w_out[0], v_w_out[0], 64)
    (pw_parts,) = _exchange_wait("scatter_conf_pw_w_wait", sent["conf_pw_w"], G["w_out"])
    G["conf_pw_w"], DW["conf_pw_w"], NM["conf_pw_w"], NV["conf_pw_w"] = _adamw(
        "adamw_pw", pw_parts, conf_pw_w[0], m_conf_pw_w[0], v_conf_pw_w[0], 128)
    res = {}
    wa_parts, wx_parts = _exchange_wait("scatter_w_gates_wait", sent["w_gates"], G["conf_pw_w"])
    for n, parts in (("w_gate_a", wa_parts), ("w_gate_x", wx_parts)):
        res[n] = _adamw("adamw_" + n, parts, *[d[n].reshape(16 * 64, 64) for d in (W, M, V)], 16 * 64)
    small_parts, vec_parts = _exchange_wait("scatter_rest_wait", rest, res["w_gate_x"][0])
    res.update(_adamw_small(small_parts, W, M, V))
    vec_res, loss_row = _adamw_vec(vec_parts, W, M, V)
    res.update(vec_res)
    (win_sum,) = _exchange_wait("scatter_w_in_stage2_wait", win_stage2, loss_row)
    res["w_in"] = _adamw("adamw_w_in", win_sum.reshape(1, D, NIN // NDEV), w_in[0], m_w_in[0], v_w_in[0], 256)
    for n, vals in res.items():
        for dst, val in zip((G, DW, NM, NV), vals):
            dst[n] = val
    for dst in (G, DW, NM, NV):
        for n in names:
            dst[n] = dst[n].reshape(shapes[n])
    loss = loss_row[0, 0]

    return (loss, grad_x, *[G[n] for n in names], *[DW[n] for n in names],
            *[NM[n] for n in names], *[NV[n] for n in names])
```

```python
import functools

import jax
import jax.numpy as jnp
from jax import lax
from jax.experimental import pallas as pl
from jax.experimental.pallas import tpu as pltpu

F32 = jnp.float32
BF16 = jnp.bfloat16

D = 2048
DL = 1024
DC = 1024
NIN = 5120
NMETA = 16
SEQ = 2048
T = NMETA + SEQ
TP = 2176
TM = 544
CB = 256
NCB = DL // CB
R = 16
KW = 31
KWP = 32
LW = 4
LRU_C = 8.0
EPS = 1e-6
NDEV = 8

ADAM_LR = 0.001
ADAM_B1 = 0.9
ADAM_B2 = 0.999
ADAM_EPS = 1e-08
ADAM_WD = 0.01
ADAM_STEP = 10

VMEM_LIMIT = 56 * 1024 * 1024


def _cparams():
    return pltpu.CompilerParams(vmem_limit_bytes=VMEM_LIMIT)


def _sig(x):
    return 1.0 / (1.0 + jnp.exp(-x))


def _expm1_neg(y):
    poly = y * (1.0 + y * (0.5 + y * (1.0 / 6.0 + y * (1.0 / 24.0 + y * (1.0 / 120.0)))))
    return jnp.where(y > -0.1, poly, jnp.exp(y) - 1.0)


def _softplus(x):
    e = jnp.exp(-jnp.abs(x))
    w = 1.0 + e
    l1p = jnp.where(w == 1.0, e, jnp.log(w) * e / (w - 1.0))
    return jnp.maximum(x, 0.0) + l1p


def _row_iota(shape):
    return lax.broadcasted_iota(jnp.int32, shape, 0)


def _fold8(v):
    return v[0:8, :] + v[8:16, :]


_FLIPS = [(k >> 2 & 1, k >> 1 & 1, k & 1) for k in range(1, NDEV)]
_HBM = pl.BlockSpec(memory_space=pltpu.HBM)
_SEM = pl.BlockSpec(memory_space=pltpu.SEMAPHORE)


def _peers():
    x, y, c = lax.axis_index("x"), lax.axis_index("y"), lax.axis_index("c")
    out = []
    for dx, dy, dc in _FLIPS:
        px = 1 - x if dx else x
        py = 1 - y if dy else y
        pc = 1 - c if dc else c
        out.append(((px, py, pc), 4 * px + 2 * py + pc))
    return 4 * x + 2 * y + c, out


def _exchange_start(name, items):
    n = len(items)

    def body(*refs):
        srcs, lands = refs[:n], refs[n:2 * n]
        outs = refs[2 * n:]
        send_sems, recv_sems, local_sems = outs[:n], outs[n:2 * n], outs[2 * n:3 * n]
        token = outs[-1]
        me, peers = _peers()
        for a in range(n):
            src_at, dst_at = items[a][2], items[a][3]
            pltpu.make_async_copy(src_at(srcs[a], me), dst_at(lands[a], me), local_sems[a]).start()
        for a in range(n):
            src_at, dst_at = items[a][2], items[a][3]
            for k, (pos, peer) in enumerate(peers):
                pltpu.make_async_remote_copy(
                    src_ref=src_at(srcs[a], peer), dst_ref=dst_at(lands[a], me),
                    send_sem=send_sems[a].at[k], recv_sem=recv_sems[a].at[k],
                    device_id=pos, device_id_type=pl.DeviceIdType.MESH).start()
        token[...] = jnp.zeros_like(token)

    srcs = [pltpu.with_memory_space_constraint(it[0], pltpu.HBM) for it in items]
    lands = [pltpu.with_memory_space_constraint(lax.empty(it[1].shape, it[1].dtype), pltpu.HBM) for it in items]
    sem7 = pltpu.SemaphoreType.DMA((NDEV - 1,))
    res = pl.pallas_call(
        body, name=name,
        out_shape=([sem7] * (2 * n) + [pltpu.SemaphoreType.DMA(())] * n
                   + [pltpu.HBM(a.shape, a.dtype) for a in srcs] + [pltpu.HBM(a.shape, a.dtype) for a in lands]
                   + [jax.ShapeDtypeStruct((8, 128), F32)]),
        in_specs=[_HBM] * (2 * n),
        out_specs=[_SEM] * (3 * n) + [_HBM] * (2 * n) + [pl.BlockSpec(memory_space=pltpu.VMEM)],
        input_output_aliases={i: 3 * n + i for i in range(2 * n)},
        compiler_params=pltpu.CompilerParams(has_side_effects=pltpu.SideEffectType.DATAFLOW_SIDE_EFFECTING),
    )(*srcs, *lands)
    handles = [dict(send=res[a], recv=res[n + a], local=res[2 * n + a], src=res[3 * n + a], land=res[4 * n + a],
                    src_at=items[a][2], dst_at=items[a][3]) for a in range(n)]
    return handles, res[-1]


def _exchange_wait(name, handles, after):
    n = len(handles)

    def body(*refs):
        srcs, lands = refs[:n], refs[n:2 * n]
        send_sems, recv_sems, local_sems = refs[2 * n:3 * n], refs[3 * n:4 * n], refs[4 * n:5 * n]
        me, peers = _peers()
        for a in range(n):
            src_at, dst_at = handles[a]["src_at"], handles[a]["dst_at"]
            for k, (pos, peer) in enumerate(peers):
                cp = pltpu.make_async_remote_copy(
                    src_ref=src_at(srcs[a], peer), dst_ref=dst_at(lands[a], peer),
                    send_sem=send_sems[a].at[k], recv_sem=recv_sems[a].at[k],
                    device_id=pos, device_id_type=pl.DeviceIdType.MESH)
                cp.wait_send()
                cp.wait_recv()
            pltpu.make_async_copy(src_at(srcs[a], me), dst_at(lands[a], me), local_sems[a]).wait()

    srcs = [hd["src"] for hd in handles]
    lands = [hd["land"] for hd in handles]
    res = pl.pallas_call(
        body, name=name,
        out_shape=[pltpu.HBM(a.shape, a.dtype) for a in srcs] + [pltpu.HBM(a.shape, a.dtype) for a in lands],
        in_specs=[_HBM] * (2 * n) + [_SEM] * (3 * n) + [pl.BlockSpec(memory_space=pl.ANY)],
        out_specs=[_HBM] * (2 * n),
        input_output_aliases={i: i for i in range(2 * n)},
        compiler_params=pltpu.CompilerParams(has_side_effects=pltpu.SideEffectType.DATAFLOW_SIDE_EFFECTING),
    )(*srcs, *lands, *[hd["send"] for hd in handles], *[hd["recv"] for hd in handles],
      *[hd["local"] for hd in handles], after)
    return list(res[n:])


_SIDE = pltpu.SideEffectType.DATAFLOW_SIDE_EFFECTING
_WCOLS = NIN // NDEV


def _win_cols(ref, l):
    return ref.at[:, pl.ds(pl.multiple_of(l * _WCOLS, 128), _WCOLS)]


def _win_routes():
    x, y, c = lax.axis_index("x"), lax.axis_index("y"), lax.axis_index("c")
    pos = [(x, y, 1 - c), (1 - x, y, c), (x, 1 - y, c), (1 - x, 1 - y, c)]
    return 4 * x + 2 * y + c, [(p, 4 * p[0] + 2 * p[1] + p[2]) for p in pos]


def _win_gather_start(shard):
    def body(src, land, send_sem, recv_sem, local_sem, src_thru, land_thru, token):
        me, routes = _win_routes()
        pltpu.make_async_copy(src, _win_cols(land, me), local_sem).start()
        pltpu.make_async_remote_copy(src_ref=src, dst_ref=_win_cols(land, me), send_sem=send_sem, recv_sem=recv_sem,
                                     device_id=routes[0][0], device_id_type=pl.DeviceIdType.MESH).start()
        token[...] = jnp.zeros_like(token)

    src = pltpu.with_memory_space_constraint(shard, pltpu.HBM)
    land = pltpu.with_memory_space_constraint(lax.empty((D, NIN), BF16), pltpu.HBM)
    sem = pltpu.SemaphoreType.DMA(())
    res = pl.pallas_call(
        body, name="win_gather_start",
        out_shape=[sem, sem, sem, pltpu.HBM(src.shape, BF16), pltpu.HBM(land.shape, BF16),
                   jax.ShapeDtypeStruct((8, 128), F32)],
        in_specs=[_HBM, _HBM],
        out_specs=[_SEM, _SEM, _SEM, _HBM, _HBM, pl.BlockSpec(memory_space=pltpu.VMEM)],
        input_output_aliases={0: 3, 1: 4},
        compiler_params=pltpu.CompilerParams(has_side_effects=_SIDE),
    )(src, land)
    return dict(send0=res[0], recv0=res[1], local=res[2], src=res[3], land=res[4]), res[5]


def _win_gather_links(hd, after):
    def body(src, land, after_ref, send_sems, recv_sems, src_thru, land_thru, token):
        me, routes = _win_routes()
        for k in (1, 2, 3):
            pltpu.make_async_remote_copy(src_ref=src, dst_ref=_win_cols(land, me), send_sem=send_sems.at[k - 1],
                                         recv_sem=recv_sems.at[k - 1], device_id=routes[k][0],
                                         device_id_type=pl.DeviceIdType.MESH).start()
        token[...] = jnp.zeros_like(token)

    sem3 = pltpu.SemaphoreType.DMA((3,))
    res = pl.pallas_call(
        body, name="win_gather_links",
        out_shape=[sem3, sem3, pltpu.HBM(hd["src"].shape, BF16), pltpu.HBM(hd["land"].shape, BF16),
                   jax.ShapeDtypeStruct((8, 128), F32)],
        in_specs=[_HBM, _HBM, pl.BlockSpec(memory_space=pl.ANY)],
        out_specs=[_SEM, _SEM, _HBM, _HBM, pl.BlockSpec(memory_space=pltpu.VMEM)],
        input_output_aliases={0: 2, 1: 3},
        compiler_params=pltpu.CompilerParams(has_side_effects=_SIDE),
    )(hd["src"], hd["land"], after)
    return dict(hd, send=res[0], recv=res[1], src=res[2], land=res[3]), res[4]


def _win_gather_forward(hd, after):
    def body(land, recv_sems, after_ref, land_thru, fsend_sems, frecv_sems):
        me, routes = _win_routes()
        sibling = routes[0][0]
        for k in (1, 2, 3):
            pos, peer = routes[k]
            piece = _win_cols(land, peer)
            pltpu.make_async_remote_copy(src_ref=piece, dst_ref=piece, send_sem=fsend_sems.at[k - 1],
                                         recv_sem=recv_sems.at[k - 1], device_id=pos,
                                         device_id_type=pl.DeviceIdType.MESH).wait_recv()
            pltpu.make_async_remote_copy(src_ref=piece, dst_ref=piece, send_sem=fsend_sems.at[k - 1],
                                         recv_sem=frecv_sems.at[k - 1], device_id=sibling,
                                         device_id_type=pl.DeviceIdType.MESH).start()

    sem3 = pltpu.SemaphoreType.DMA((3,))
    res = pl.pallas_call(
        body, name="win_gather_forward",
        out_shape=[pltpu.HBM(hd["land"].shape, BF16), sem3, sem3],
        in_specs=[_HBM, _SEM, pl.BlockSpec(memory_space=pl.ANY)],
        out_specs=[_HBM, _SEM, _SEM],
        input_output_aliases={0: 0},
        compiler_params=pltpu.CompilerParams(has_side_effects=_SIDE),
    )(hd["land"], hd["recv"], after)
    return dict(hd, land=res[0], fsend=res[1], frecv=res[2])


def _win_gather_early(hd):
    def body(src, land, recv_sem, local_sem, src_thru, land_thru):
        me, routes = _win_routes()
        pos, sibling = routes[0]
        pltpu.make_async_remote_copy(src_ref=src, dst_ref=_win_cols(land, sibling), send_sem=local_sem,
                                     recv_sem=recv_sem, device_id=pos,
                                     device_id_type=pl.DeviceIdType.MESH).wait_recv()
        pltpu.make_async_copy(src, _win_cols(land, me), local_sem).wait()

    res = pl.pallas_call(
        body, name="win_gather_early",
        out_shape=[pltpu.HBM(hd["src"].shape, BF16), pltpu.HBM(hd["land"].shape, BF16)],
        in_specs=[_HBM, _HBM, _SEM, _SEM],
        out_specs=[_HBM, _HBM],
        input_output_aliases={0: 0, 1: 1},
        compiler_params=pltpu.CompilerParams(has_side_effects=_SIDE),
    )(hd["src"], hd["land"], hd["recv0"], hd["local"])
    return dict(hd, src=res[0], land=res[1])


def _win_gather_wait(hd):
    def body(src, land, send0_sem, send_sems, fsend_sems, frecv_sems, src_thru, land_thru):
        me, routes = _win_routes()
        sib_pos, sibling = routes[0]
        for k, (pos, peer) in enumerate(routes):
            pltpu.make_async_remote_copy(src_ref=src, dst_ref=_win_cols(land, peer),
                                         send_sem=send0_sem if k == 0 else send_sems.at[k - 1],
                                         recv_sem=frecv_sems.at[0], device_id=pos,
                                         device_id_type=pl.DeviceIdType.MESH).wait_send()
        for k in (1, 2, 3):
            mine = _win_cols(land, routes[k][1])
            theirs = _win_cols(land, 4 * routes[k][0][0] + 2 * routes[k][0][1] + sib_pos[2])
            cp = pltpu.make_async_remote_copy(src_ref=mine, dst_ref=theirs, send_sem=fsend_sems.at[k - 1],
                                              recv_sem=frecv_sems.at[k - 1], device_id=sib_pos,
                                              device_id_type=pl.DeviceIdType.MESH)
            cp.wait_send()
            cp.wait_recv()

    res = pl.pallas_call(
        body, name="win_gather_wait",
        out_shape=[pltpu.HBM(hd["src"].shape, BF16), pltpu.HBM(hd["land"].shape, BF16)],
        in_specs=[_HBM, _HBM] + [_SEM] * 4,
        out_specs=[_HBM, _HBM],
        input_output_aliases={0: 0, 1: 1},
        compiler_params=pltpu.CompilerParams(has_side_effects=_SIDE),
    )(hd["src"], hd["land"], hd["send0"], hd["send"], hd["fsend"], hd["frecv"])
    return res[1]


def _whole(ref, l):
    return ref


def _slot(ref, l):
    return ref.at[l]


def _cols(width):
    def at(ref, l):
        return ref.at[:, pl.ds(pl.multiple_of(l * width, 128), width)]
    return at


def _rows(height):
    def at(ref, l):
        return ref.at[pl.ds(pl.multiple_of(l * height, 8), height), :]
    return at


NTILE = TP // TM


def _tile_rows(t):
    lo = max(t * TM - NMETA, 0)
    hi = min((t + 1) * TM - NMETA, SEQ)
    return lo, hi - lo, lo + NMETA - t * TM


def _for_tile(t, fn):
    for static_t in range(NTILE):
        pl.when(t == static_t)(functools.partial(fn, static_t))


def _token_tile_copy(hbm_ref, buf, sem, t):
    lo, n, off = _tile_rows(t)
    return pltpu.make_async_copy(hbm_ref.at[pl.ds(lo, n)], buf.at[pl.ds(off, n)], sem)


def _prenorm(x, meta_full, pre_w):
    def body(x_ref, meta_ref, pw_ref, h_ref, hn_ref, xbuf, sems):
        i = pl.program_id(0)
        slot = i % 2

        def start(t):
            _token_tile_copy(x_ref, xbuf.at[t % 2], sems.at[t % 2], t).start()

        @pl.when(i == 0)
        def _():
            start(0)
        _for_tile(i + 1, start)
        _for_tile(i, lambda t: _token_tile_copy(x_ref, xbuf.at[t % 2], sems.at[t % 2], t).wait())

        @pl.when(i == 0)
        def _():
            xbuf[0, 0:NMETA, :] = meta_ref[...]

        @pl.when(i == NTILE - 1)
        def _():
            last = _tile_rows(NTILE - 1)[1]
            xbuf[(NTILE - 1) % 2, last:TM, :] = jnp.zeros((TM - last, D), F32)

        pw = pw_ref[...]

        def chunk(ci, carry):
            r0 = pl.multiple_of(ci * R, R)
            xv = xbuf[slot, pl.ds(r0, R), :]
            h_ref[pl.ds(r0, R), :] = xv
            ms = jnp.mean(xv * xv, axis=-1, keepdims=True)
            hn_ref[pl.ds(r0, R), :] = (xv * lax.rsqrt(ms + EPS) * pw).astype(BF16)
            return carry
        lax.fori_loop(0, TM // R, chunk, 0, unroll=2)

    row = pl.BlockSpec((TM, D), lambda i: (i, 0))
    return pl.pallas_call(
        body, name="prenorm",
        grid=(NTILE,),
        in_specs=[pl.BlockSpec(memory_space=pl.ANY), pl.BlockSpec((NMETA, D), lambda i: (0, 0)),
                  pl.BlockSpec((1, D), lambda i: (0, 0))],
        out_specs=[row, row],
        out_shape=[jax.ShapeDtypeStruct((TP, D), F32), jax.ShapeDtypeStruct((TP, D), BF16)],
        scratch_shapes=[pltpu.VMEM((2, TM, D), F32), pltpu.SemaphoreType.DMA((2,))],
        compiler_params=_cparams(),
    )(x, meta_full, pre_w)


def _inproj_cols(name, shards, hn, w_land, b_in, z_prev):
    nsh = shards.shape[0]

    def body(idx_ref, hn_ref, w_ref, b_ref, *rest):
        z_ref = rest[-2]
        z_ref[...] = jnp.dot(hn_ref[...], w_ref[...], preferred_element_type=F32) + b_ref[...]

    any_spec = pl.BlockSpec(memory_space=pl.ANY)
    in_specs = [pl.BlockSpec((TM, D), lambda j, i, idx: (i, 0)),
                pl.BlockSpec((D, _WCOLS), lambda j, i, idx: (0, idx[j])),
                pl.BlockSpec((1, _WCOLS), lambda j, i, idx: (0, idx[j]))]
    operands = [hn, w_land, b_in]
    aliases = {2: 1}
    if z_prev is not None:
        in_specs.append(any_spec)
        operands.append(z_prev)
        aliases[4] = 0
    return pl.pallas_call(
        body, name=name,
        grid_spec=pltpu.PrefetchScalarGridSpec(
            num_scalar_prefetch=1, grid=(nsh, TP // TM), in_specs=in_specs,
            out_specs=[pl.BlockSpec((TM, _WCOLS), lambda j, i, idx: (i, idx[j])), any_spec]),
        out_shape=[jax.ShapeDtypeStruct((TP, NIN), F32), jax.ShapeDtypeStruct(w_land.shape, w_land.dtype)],
        input_output_aliases=aliases,
        compiler_params=_cparams(),
    )(shards, *operands)


def _gate_values(ga, gx, xc, sp8):
    r = _sig(ga)
    i = _sig(gx)
    log_a = -(r * sp8)
    a = jnp.exp(log_a)
    mult = jnp.sqrt(-_expm1_neg(2.0 * log_a))
    return r, i, a, mult


def _lru_fwd(z, conv_w, conv_b, wa_g, b_a, wx_g, b_x, lam):
    def body(x_ref, g_ref, cw_ref, cb_ref, wa_ref, ba_ref, wx_ref, bx_ref, lam_ref,
             y_ref, xc_ref, hs_ref, ga_s, gx_s):
        taps = [cw_ref[k:k + 1, :] for k in range(LW)]
        cb = cb_ref[...]

        def conv_chunk(ci, carry):
            r0 = pl.multiple_of(ci * R, R)
            cur = x_ref[pl.ds(r0, R), :]
            p0 = pl.multiple_of(jnp.maximum(r0 - 8, 0), 8)
            prev = jnp.where(ci > 0, x_ref[pl.ds(p0, 8), :], 0.0)
            buf = jnp.concatenate([prev, cur], axis=0)
            acc = cur * taps[LW - 1] + cb
            for s in range(1, LW):
                acc = acc + pltpu.roll(buf, s, 0)[8:8 + R, :] * taps[LW - 1 - s]
            xc_ref[pl.ds(r0, R), :] = acc
            return carry
        lax.fori_loop(0, TP // R, conv_chunk, 0)

        def gate_chunk(ci, carry):
            r0 = pl.multiple_of(ci * TM, TM)
            xb = xc_ref[pl.ds(r0, TM), :].astype(BF16)
            ga_s[pl.ds(r0, TM), :] = jnp.dot(xb, wa_ref[...], preferred_element_type=F32) + ba_ref[...]
            gx_s[pl.ds(r0, TM), :] = jnp.dot(xb, wx_ref[...], preferred_element_type=F32) + bx_ref[...]
            return carry
        lax.fori_loop(0, TP // TM, gate_chunk, 0)

        sp8 = LRU_C * _softplus(-lam_ref[...])
        row = _row_iota((R, CB))

        def scan_chunk(ci, hprev):
            r0 = pl.multiple_of(ci * R, R)
            xc = xc_ref[pl.ds(r0, R), :]
            _, i, a, mult = _gate_values(ga_s[pl.ds(r0, R), :], gx_s[pl.ds(r0, R), :], xc, sp8)
            u = mult * (i * xc)
            k = 1
            while k < R:
                m = row >= k
                u = jnp.where(m, a * pltpu.roll(u, k, 0) + u, u)
                a = jnp.where(m, a * pltpu.roll(a, k, 0), a)
                k *= 2
            hv = u + a * hprev
            hs_ref[pl.ds(r0, R), :] = hv
            g = g_ref[pl.ds(r0, R), :]
            y_ref[pl.ds(r0, R), :] = (hv * (g * _sig(g))).astype(BF16)
            return jnp.sum(jnp.where(row == R - 1, hv, 0.0), axis=0, keepdims=True)
        lax.fori_loop(0, TP // R, scan_chunk, jnp.zeros((1, CB), F32))

    col = lambda off: pl.BlockSpec((TP, CB), lambda j: (0, off + j))
    vec = pl.BlockSpec((1, CB), lambda j: (0, j))
    wsp = pl.BlockSpec((None, CB, CB), lambda j: (j, 0, 0))
    return pl.pallas_call(
        body, name="lru_fwd",
        grid=(NCB,),
        in_specs=[col(0), col(NCB), pl.BlockSpec((LW, CB), lambda j: (0, j)), vec, wsp, vec, wsp, vec, vec],
        out_specs=[col(0), col(0), col(0)],
        out_shape=[jax.ShapeDtypeStruct((TP, DL), BF16), jax.ShapeDtypeStruct((TP, DL), F32),
                   jax.ShapeDtypeStruct((TP, DL), F32)],
        scratch_shapes=[pltpu.VMEM((TP, CB), F32), pltpu.VMEM((TP, CB), F32)],
        compiler_params=_cparams(),
    )(z, z, conv_w, conv_b, wa_g, b_a, wx_g, b_x, lam)


def _conf_fwd_conv(z, dw_w, dw_b):
    def body(u1_ref, u2_ref, w_ref, b_ref, vc_ref, vs):
        vs[pl.ds(0, KWP), :] = jnp.zeros((KWP, CB), F32)

        def glu_chunk(ci, carry):
            r0 = pl.multiple_of(ci * R, R)
            vs[pl.ds(KWP + r0, R), :] = u1_ref[pl.ds(r0, R), :] * _sig(u2_ref[pl.ds(r0, R), :])
            return carry
        lax.fori_loop(0, TP // R, glu_chunk, 0)

        bias = b_ref[...]

        def conv_chunk(ci, carry):
            r0 = pl.multiple_of(ci * R, R)
            buf = vs[pl.ds(r0, KWP + R), :]
            acc = jnp.zeros((R, CB), F32) + bias
            for rr in range(8):
                rolled = buf if rr == 0 else pltpu.roll(buf, rr, 0)
                for q in range(4):
                    s = 8 * q + rr
                    if s > KW - 1:
                        continue
                    k = KW - 1 - s
                    acc = acc + rolled[KWP - 8 * q:KWP - 8 * q + R, :] * w_ref[k:k + 1, :]
            vc_ref[pl.ds(r0, R), :] = acc
            return carry
        lax.fori_loop(0, TP // R, conv_chunk, 0)

    return pl.pallas_call(
        body, name="conf_fwd_conv",
        grid=(NCB,),
        in_specs=[pl.BlockSpec((TP, CB), lambda j: (0, 2 * NCB + j)),
                  pl.BlockSpec((TP, CB), lambda j: (0, 3 * NCB + j)),
                  pl.BlockSpec((KWP, CB), lambda j: (0, j)),
                  pl.BlockSpec((1, CB), lambda j: (0, j))],
        out_specs=pl.BlockSpec((TP, CB), lambda j: (0, j)),
        out_shape=jax.ShapeDtypeStruct((TP, DC), F32),
        scratch_shapes=[pltpu.VMEM((TP + KWP, CB), F32)],
        compiler_params=_cparams(),
    )(z, z, dw_w, dw_b)


def _ln_chunk(vc, lw, lb):
    mu = jnp.mean(vc, axis=-1, keepdims=True)
    xm = vc - mu
    var = jnp.mean(xm * xm, axis=-1, keepdims=True)
    rstd = lax.rsqrt(var + EPS)
    xhat = xm * rstd
    return xhat, rstd, xhat * lw + lb


def _conf_fwd_proj(vc, z, ln_w, ln_b, pw_w, pw_b):
    def body(vc_ref, g_ref, lw_ref, lb_ref, w_ref, b_ref, y_ref, p_ref, s_s):
        lw, lb = lw_ref[...], lb_ref[...]

        def ln_chunk(ci, carry):
            r0 = pl.multiple_of(ci * R, R)
            for half in range(2):
                rr = r0 + 8 * half
                _, _, ln = _ln_chunk(vc_ref[pl.ds(rr, 8), :], lw, lb)
                p_ref[pl.ds(rr, 8), :] = ln * _sig(ln)
            s_s[pl.ds(r0, R), :] = p_ref[pl.ds(r0, R), :].astype(BF16)
            return carry
        lax.fori_loop(0, TM // R, ln_chunk, 0, unroll=2)

        p_ref[...] = jnp.dot(s_s[...], w_ref[...], preferred_element_type=F32) + b_ref[...]

        def out_chunk(ci, carry):
            r0 = pl.multiple_of(ci * R, R)
            g = g_ref[pl.ds(r0, R), :]
            y_ref[pl.ds(r0, R), :] = (p_ref[pl.ds(r0, R), :] * (g * _sig(g))).astype(BF16)
            return carry
        lax.fori_loop(0, TM // R, out_chunk, 0)

    row = pl.BlockSpec((TM, DC), lambda i: (i, 0))
    vec = pl.BlockSpec((1, DC), lambda i: (0, 0))
    return pl.pallas_call(
        body, name="conf_fwd_proj",
        grid=(TP // TM,),
        in_specs=[row, pl.BlockSpec((TM, DC), lambda i: (i, 4)), vec, vec,
                  pl.BlockSpec((DC, DC), lambda i: (0, 0)), vec],
        out_specs=[row, row],
        out_shape=[jax.ShapeDtypeStruct((TP, DC), BF16), jax.ShapeDtypeStruct((TP, DC), F32)],
        scratch_shapes=[pltpu.VMEM((TM, DC), BF16)],
        compiler_params=_cparams(),
    )(vc, z, ln_w, ln_b, pw_w, pw_b)


def _outproj_loss(ylru, yconf, w_out, h, target, post_w):
    def body(yl_ref, yc_ref, w_ref, h_ref, tgt_hbm, pw_ref, dout_ref, dy_ref, loss_ref, dpw_ref, y_s, t_ref, sem):
        i = pl.program_id(0)
        k = pl.program_id(1)

        @pl.when(k == 0)
        def _():
            _for_tile(i, lambda t: _token_tile_copy(tgt_hbm, t_ref, sem, t).start())
            y_s[...] = jnp.dot(yl_ref[...], w_ref[...], preferred_element_type=F32)

        @pl.when(k == 1)
        def _():
            y_s[...] += jnp.dot(yc_ref[...], w_ref[...], preferred_element_type=F32)

        @pl.when(jnp.logical_and(i == 0, k == 1))
        def _():
            loss_ref[...] = jnp.zeros_like(loss_ref)
            dpw_ref[...] = jnp.zeros_like(dpw_ref)

        @pl.when(k == 1)
        def _():
            _for_tile(i, lambda t: _token_tile_copy(tgt_hbm, t_ref, sem, t).wait())

            @pl.when(i == 0)
            def _():
                t_ref[0:NMETA, :] = jnp.zeros((NMETA, D), F32)

            @pl.when(i == NTILE - 1)
            def _():
                last = _tile_rows(NTILE - 1)[1]
                t_ref[last:TM, :] = jnp.zeros((TM - last, D), F32)

            pw = pw_ref[...]
            row = _row_iota((8, D))

            def chunk(ci, carry):
                r0 = pl.multiple_of(ci * 8, 8)
                yv = y_s[pl.ds(r0, 8), :]
                rs = lax.rsqrt(jnp.mean(yv * yv, axis=-1, keepdims=True) + EPS)
                grow = row + (i * TM + r0)
                valid = jnp.logical_and(grow >= NMETA, grow < T)
                yn = yv * rs
                err = jnp.where(valid, h_ref[pl.ds(r0, 8), :] + yn * pw - t_ref[pl.ds(r0, 8), :], 0.0)
                loss_ref[...] += err * err
                d_rn = err * (1.0 / D)
                dout_ref[pl.ds(r0, 8), :] = d_rn
                dpw_ref[...] += d_rn * yn
                gw = d_rn * pw
                dot = jnp.mean(gw * yv, axis=-1, keepdims=True)
                dy_ref[pl.ds(r0, 8), :] = (rs * gw - yv * (rs * rs * rs * dot)).astype(BF16)
                return carry
            lax.fori_loop(0, TM // 8, chunk, 0, unroll=4)

    row = pl.BlockSpec((TM, D), lambda i, k: (i, 0))
    half = pl.BlockSpec((TM, DL), lambda i, k: (i, 0))
    acc = pl.BlockSpec((8, D), lambda i, k: (0, 0))
    return pl.pallas_call(
        body, name="outproj_loss",
        grid=(TP // TM, 2),
        in_specs=[half, half, pl.BlockSpec((DL, D), lambda i, k: (k, 0)), row, pl.BlockSpec(memory_space=pl.ANY),
                  pl.BlockSpec((1, D), lambda i, k: (0, 0))],
        out_specs=[row, row, acc, acc],
        out_shape=[jax.ShapeDtypeStruct((TP, D), F32), jax.ShapeDtypeStruct((TP, D), BF16),
                   jax.ShapeDtypeStruct((8, D), F32), jax.ShapeDtypeStruct((8, D), F32)],
        scratch_shapes=[pltpu.VMEM((TM, D), F32), pltpu.VMEM((TM, D), F32), pltpu.SemaphoreType.DMA(())],
        compiler_params=_cparams(),
    )(ylru, yconf, w_out, h, target, post_w)


_NT = (((1,), (1,)), ((), ()))
_TN = (((0,), (0,)), ((), ()))


def _outproj_bwd(dy, ylru, yconf, w_out):
    def body(dy_ref, yl_ref, yc_ref, w_ref, dycat_ref, dw_ref):
        j = pl.program_id(0)
        dyv = dy_ref[...]
        dycat_ref[...] = lax.dot_general(dyv, w_ref[...], _NT, preferred_element_type=F32)

        @pl.when(j < NCB)
        def _():
            dw_ref[...] = lax.dot_general(yl_ref[...], dyv, _TN, preferred_element_type=F32).astype(BF16)

        @pl.when(j >= NCB)
        def _():
            dw_ref[...] = lax.dot_general(yc_ref[...], dyv, _TN, preferred_element_type=F32).astype(BF16)

    return pl.pallas_call(
        body, name="outproj_bwd",
        grid=(2 * NCB,),
        in_specs=[pl.BlockSpec((TP, D), lambda j: (0, 0)),
                  pl.BlockSpec((TP, CB), lambda j: (0, jnp.minimum(j, NCB - 1))),
                  pl.BlockSpec((TP, CB), lambda j: (0, jnp.maximum(j - NCB, 0))),
                  pl.BlockSpec((CB, D), lambda j: (j, 0))],
        out_specs=[pl.BlockSpec((TP, CB), lambda j: (0, j)), pl.BlockSpec((CB, D), lambda j: (j, 0))],
        out_shape=[jax.ShapeDtypeStruct((TP, D), F32), jax.ShapeDtypeStruct((D, D), BF16)],
        compiler_params=_cparams(),
    )(dy, ylru, yconf, w_out)


def _conf_bwd_proj(dycat, p, z, vc, ln_w, ln_b, pw_w):
    def body(dy_ref, p_ref, g_ref, vc_ref, lw_ref, lb_ref, w_ref,
             dvc_ref, dgc_ref, dpw_ref, vecs_ref, dp_s, s_s, ds_s):
        i = pl.program_id(0)
        lw, lb = lw_ref[...], lb_ref[...]

        @pl.when(i == 0)
        def _():
            dpw_ref[...] = jnp.zeros_like(dpw_ref)
            vecs_ref[...] = jnp.zeros_like(vecs_ref)

        def pre_chunk(ci, carry):
            r0 = pl.multiple_of(ci * R, R)
            for half in range(2):
                rr = r0 + 8 * half
                dyv = dy_ref[pl.ds(rr, 8), :]
                g = g_ref[pl.ds(rr, 8), :]
                sg = _sig(g)
                dp = dyv * (g * sg)
                dg = dyv * p_ref[pl.ds(rr, 8), :] * (sg * (1.0 + g * (1.0 - sg)))
                vecs_ref[0:8, :] += dp
                vecs_ref[8:16, :] += dg
                ds_s[pl.ds(rr, 8), :] = dp
                dvc_ref[pl.ds(rr, 8), :] = dg
            dp_s[pl.ds(r0, R), :] = ds_s[pl.ds(r0, R), :].astype(BF16)
            dgc_ref[pl.ds(r0, R), :] = dvc_ref[pl.ds(r0, R), :].astype(BF16)
            for half in range(2):
                rr = r0 + 8 * half
                _, _, ln = _ln_chunk(vc_ref[pl.ds(rr, 8), :], lw, lb)
                ds_s[pl.ds(rr, 8), :] = ln * _sig(ln)
            s_s[pl.ds(r0, R), :] = ds_s[pl.ds(r0, R), :].astype(BF16)
            return carry
        lax.fori_loop(0, TM // R, pre_chunk, 0, unroll=2)

        dpb = dp_s[...]
        ds_s[...] = lax.dot_general(dpb, w_ref[...], _NT, preferred_element_type=F32)
        dpw_ref[...] += lax.dot_general(s_s[...], dpb, _TN, preferred_element_type=F32)

        def post_chunk(ci, carry):
            r0 = pl.multiple_of(ci * 8, 8)
            xhat, rstd, ln = _ln_chunk(vc_ref[pl.ds(r0, 8), :], lw, lb)
            sl = _sig(ln)
            dln = ds_s[pl.ds(r0, 8), :] * (sl * (1.0 + ln * (1.0 - sl)))
            vecs_ref[16:24, :] += dln * xhat
            vecs_ref[24:32, :] += dln
            dxh = dln * lw
            m1 = jnp.mean(dxh, axis=-1, keepdims=True)
            m2 = jnp.mean(dxh * xhat, axis=-1, keepdims=True)
            dvc_ref[pl.ds(r0, 8), :] = rstd * (dxh - m1 - xhat * m2)
            return carry
        lax.fori_loop(0, TM // 8, post_chunk, 0, unroll=4)

    row = pl.BlockSpec((TM, DC), lambda i: (i, 0))
    vec = pl.BlockSpec((1, DC), lambda i: (0, 0))
    return pl.pallas_call(
        body, name="conf_bwd_proj",
        grid=(TP // TM,),
        in_specs=[pl.BlockSpec((TM, DC), lambda i: (i, 1)), row, pl.BlockSpec((TM, DC), lambda i: (i, 4)), row,
                  vec, vec, pl.BlockSpec((DC, DC), lambda i: (0, 0))],
        out_specs=[row, row, pl.BlockSpec((DC, DC), lambda i: (0, 0)), pl.BlockSpec((32, DC), lambda i: (0, 0))],
        out_shape=[jax.ShapeDtypeStruct((TP, DC), F32), jax.ShapeDtypeStruct((TP, DC), BF16),
                   jax.ShapeDtypeStruct((DC, DC), F32), jax.ShapeDtypeStruct((32, DC), F32)],
        scratch_shapes=[pltpu.VMEM((TM, DC), BF16), pltpu.VMEM((TM, DC), BF16), pltpu.VMEM((TM, DC), F32)],
        compiler_params=_cparams(),
    )(dycat, p, z, vc, ln_w, ln_b, pw_w)


def _conf_bwd_conv(dvc, z, dw_w):
    def body(dvc_ref, u1_ref, u2_ref, w_ref, du_ref, dw_ref, vecs_ref, vs, dvs):
        vs[pl.ds(0, KWP), :] = jnp.zeros((KWP, CB), F32)
        dvs[pl.ds(TP, KWP), :] = jnp.zeros((KWP, CB), F32)
        dw_ref[...] = jnp.zeros_like(dw_ref)
        vecs_ref[...] = jnp.zeros_like(vecs_ref)

        def fill_chunk(ci, carry):
            r0 = pl.multiple_of(ci * R, R)
            vs[pl.ds(KWP + r0, R), :] = u1_ref[pl.ds(r0, R), :] * _sig(u2_ref[pl.ds(r0, R), :])
            dv = dvc_ref[pl.ds(r0, R), :]
            dvs[pl.ds(r0, R), :] = dv
            vecs_ref[0:8, :] += _fold8(dv)
            return carry
        lax.fori_loop(0, TP // R, fill_chunk, 0)

        def conv_chunk(ci, carry):
            r0 = pl.multiple_of(ci * R, R)
            vbuf = vs[pl.ds(r0, KWP + R), :]
            dbuf = dvs[pl.ds(r0, KWP + R), :]
            dcur = dbuf[0:R, :]
            dv = jnp.zeros((R, CB), F32)
            for rr in range(8):
                vroll = vbuf if rr == 0 else pltpu.roll(vbuf, rr, 0)
                droll = dbuf if rr == 0 else pltpu.roll(dbuf, KWP + R - rr, 0)
                for q in range(4):
                    s = 8 * q + rr
                    if s > KW - 1:
                        continue
                    k = KW - 1 - s
                    dv = dv + droll[8 * q:8 * q + R, :] * w_ref[k:k + 1, :]
                    dw_ref[8 * k:8 * k + 8, :] += _fold8(dcur * vroll[KWP - 8 * q:KWP - 8 * q + R, :])
            u1 = u1_ref[pl.ds(r0, R), :]
            sg = _sig(u2_ref[pl.ds(r0, R), :])
            du1 = dv * sg
            du2 = dv * u1 * (sg * (1.0 - sg))
            du_ref[0, pl.ds(r0, R), :] = du1.astype(BF16)
            du_ref[1, pl.ds(r0, R), :] = du2.astype(BF16)
            vecs_ref[8:16, :] += _fold8(du1)
            vecs_ref[16:24, :] += _fold8(du2)
            return carry
        lax.fori_loop(0, TP // R, conv_chunk, 0)

    blk = pl.BlockSpec((TP, CB), lambda j: (0, j))
    return pl.pallas_call(
        body, name="conf_bwd_conv",
        grid=(NCB,),
        in_specs=[blk, pl.BlockSpec((TP, CB), lambda j: (0, 2 * NCB + j)),
                  pl.BlockSpec((TP, CB), lambda j: (0, 3 * NCB + j)), pl.BlockSpec((KWP, CB), lambda j: (0, j))],
        out_specs=[pl.BlockSpec((2, TP, CB), lambda j: (0, 0, j)), pl.BlockSpec((8 * KWP, CB), lambda j: (0, j)),
                   pl.BlockSpec((24, CB), lambda j: (0, j))],
        out_shape=[jax.ShapeDtypeStruct((2, TP, DC), BF16),
                   jax.ShapeDtypeStruct((8 * KWP, DC), F32), jax.ShapeDtypeStruct((24, DC), F32)],
        scratch_shapes=[pltpu.VMEM((TP + KWP, CB), F32), pltpu.VMEM((TP + KWP, CB), F32)],
        compiler_params=_cparams(),
    )(dvc, z, z, dw_w)


def _lru_bwd(dycat, z, xc, hs, conv_w, wa_g, b_a, wx_g, b_x, lam):
    NV = 6

    def body(dy_ref, x_ref, g_ref, xc_ref, hs_ref, cw_ref, wa_ref, ba_ref, wx_ref, bx_ref, lam_ref,
             dzl_ref, dwa_ref, dwx_ref, dcw_ref, vecs_ref, ga_s, gx_s, dxc_s):
        vecs_ref[...] = jnp.zeros_like(vecs_ref)
        dcw_ref[...] = jnp.zeros_like(dcw_ref)
        dxc_s[pl.ds(TP, 8), :] = jnp.zeros((8, CB), F32)

        def gate_chunk(ci, carry):
            r0 = pl.multiple_of(ci * TM, TM)
            xb = xc_ref[pl.ds(r0, TM), :].astype(BF16)
            ga_s[pl.ds(r0, TM), :] = jnp.dot(xb, wa_ref[...], preferred_element_type=F32) + ba_ref[...]
            gx_s[pl.ds(r0, TM), :] = jnp.dot(xb, wx_ref[...], preferred_element_type=F32) + bx_ref[...]
            return carry
        lax.fori_loop(0, TP // TM, gate_chunk, 0)

        sp8 = LRU_C * _softplus(-lam_ref[...])
        row = _row_iota((R, CB))
        nchunk = TP // R

        def scan_chunk(cj, carry):
            a_next, lam_next = carry
            ci = nchunk - 1 - cj
            r0 = pl.multiple_of(ci * R, R)
            dyv = dy_ref[pl.ds(r0, R), :]
            g = g_ref[pl.ds(r0, R), :]
            hv = hs_ref[pl.ds(r0, R), :]
            xc = xc_ref[pl.ds(r0, R), :]
            sg = _sig(g)
            dgl = dyv * hv * (sg * (1.0 + g * (1.0 - sg)))
            dzl_ref[1, pl.ds(r0, R), :] = dgl.astype(BF16)
            vecs_ref[0:8, :] += _fold8(dgl)
            dhs = dyv * (g * sg)
            r, i, a, mult = _gate_values(ga_s[pl.ds(r0, R), :], gx_s[pl.ds(r0, R), :], xc, sp8)
            b = jnp.where(row == R - 1, a_next, pltpu.roll(a, R - 1, 0))
            lv = dhs
            k = 1
            while k < R:
                m = row < R - k
                lv = jnp.where(m, lv + b * pltpu.roll(lv, R - k, 0), lv)
                b = jnp.where(m, b * pltpu.roll(b, R - k, 0), b)
                k *= 2
            lv = lv + b * lam_next
            p0 = pl.multiple_of(jnp.maximum(r0 - 8, 0), 8)
            hprev8 = jnp.where(ci > 0, hs_ref[pl.ds(p0, 8), :], 0.0)
            hprev = pltpu.roll(jnp.concatenate([hprev8, hv], axis=0), 1, 0)[8:8 + R, :]
            da = lv * hprev
            ixc = i * xc
            dmult = lv * ixc
            di = lv * mult * xc
            dxc_s[pl.ds(r0, R), :] = lv * mult * i
            a2 = a * a
            dlog_a = da * a - dmult * a2 / mult
            vecs_ref[32:40, :] += _fold8(dlog_a * r)
            dga = -(dlog_a * sp8) * r * (1.0 - r)
            dgx = di * i * (1.0 - i)
            ga_s[pl.ds(r0, R), :] = dga
            gx_s[pl.ds(r0, R), :] = dgx
            vecs_ref[16:24, :] += _fold8(dga)
            vecs_ref[24:32, :] += _fold8(dgx)
            a_first = jnp.sum(jnp.where(row == 0, a, 0.0), axis=0, keepdims=True)
            l_first = jnp.sum(jnp.where(row == 0, lv, 0.0), axis=0, keepdims=True)
            return a_first, l_first
        lax.fori_loop(0, nchunk, scan_chunk, (jnp.zeros((1, CB), F32), jnp.zeros((1, CB), F32)))

        dwa_ref[...] = jnp.zeros_like(dwa_ref)
        dwx_ref[...] = jnp.zeros_like(dwx_ref)

        def mm_chunk(ci, carry):
            r0 = pl.multiple_of(ci * TM, TM)
            xb = xc_ref[pl.ds(r0, TM), :].astype(BF16)
            dgab = ga_s[pl.ds(r0, TM), :].astype(BF16)
            dgxb = gx_s[pl.ds(r0, TM), :].astype(BF16)
            dxc_s[pl.ds(r0, TM), :] += (lax.dot_general(dgab, wa_ref[...], _NT, preferred_element_type=F32)
                                        + lax.dot_general(dgxb, wx_ref[...], _NT, preferred_element_type=F32))
            dwa_ref[...] += lax.dot_general(xb, dgab, _TN, preferred_element_type=F32)
            dwx_ref[...] += lax.dot_general(xb, dgxb, _TN, preferred_element_type=F32)
            return carry
        lax.fori_loop(0, TP // TM, mm_chunk, 0)

        taps = [cw_ref[k:k + 1, :] for k in range(LW)]

        def conv_chunk(ci, carry):
            r0 = pl.multiple_of(ci * R, R)
            dbuf = dxc_s[pl.ds(r0, R + 8), :]
            dcur = dbuf[0:R, :]
            p0 = pl.multiple_of(jnp.maximum(r0 - 8, 0), 8)
            xprev = jnp.where(ci > 0, x_ref[pl.ds(p0, 8), :], 0.0)
            xbuf = jnp.concatenate([xprev, x_ref[pl.ds(r0, R), :]], axis=0)
            dxl = dcur * taps[LW - 1]
            dcw_ref[8 * (LW - 1):8 * LW, :] += _fold8(dcur * xbuf[8:8 + R, :])
            for s in range(1, LW):
                k = LW - 1 - s
                dxl = dxl + pltpu.roll(dbuf, R + 8 - s, 0)[0:R, :] * taps[k]
                dcw_ref[8 * k:8 * k + 8, :] += _fold8(dcur * pltpu.roll(xbuf, s, 0)[8:8 + R, :])
            dzl_ref[0, pl.ds(r0, R), :] = dxl.astype(BF16)
            vecs_ref[8:16, :] += _fold8(dxl)
            vecs_ref[40:48, :] += _fold8(dcur)
            return carry
        lax.fori_loop(0, TP // R, conv_chunk, 0)
        vecs_ref[32:40, :] = vecs_ref[32:40, :] * (LRU_C * _sig(-lam_ref[...]))

    col = lambda off: pl.BlockSpec((TP, CB), lambda j: (0, off + j))
    vec = pl.BlockSpec((1, CB), lambda j: (0, j))
    wsp = pl.BlockSpec((None, CB, CB), lambda j: (j, 0, 0))
    return pl.pallas_call(
        body, name="lru_bwd",
        grid=(NCB,),
        in_specs=[col(0), col(0), col(NCB), col(0), col(0), pl.BlockSpec((LW, CB), lambda j: (0, j)),
                  wsp, vec, wsp, vec, vec],
        out_specs=[pl.BlockSpec((2, TP, CB), lambda j: (0, 0, j)), wsp, wsp,
                   pl.BlockSpec((8 * LW, CB), lambda j: (0, j)), pl.BlockSpec((8 * NV, CB), lambda j: (0, j))],
        out_shape=[jax.ShapeDtypeStruct((2, TP, DL), BF16),
                   jax.ShapeDtypeStruct((NCB, CB, CB), F32), jax.ShapeDtypeStruct((NCB, CB, CB), F32),
                   jax.ShapeDtypeStruct((8 * LW, DL), F32), jax.ShapeDtypeStruct((8 * NV, DL), F32)],
        scratch_shapes=[pltpu.VMEM((TP, CB), F32), pltpu.VMEM((TP, CB), F32), pltpu.VMEM((TP + 8, CB), F32)],
        compiler_params=_cparams(),
    )(dycat, z, z, xc, hs, conv_w, wa_g, b_a, wx_g, b_x, lam)


def _dz_section(sec, dzl_ref, dzc_ref, dgc_ref, use):
    @pl.when(sec < 2)
    def _():
        use(dzl_ref)

    @pl.when(jnp.logical_and(sec >= 2, sec < 4))
    def _():
        use(dzc_ref)

    @pl.when(sec == 4)
    def _():
        use(dgc_ref)


def _dz_specs(rows, index):
    return [pl.BlockSpec((None, rows, 1024), lambda a, b: (jnp.minimum(index(a, b)[1], 1), index(a, b)[0], 0)),
            pl.BlockSpec((None, rows, 1024), lambda a, b: (jnp.clip(index(a, b)[1] - 2, 0, 1), index(a, b)[0], 0)),
            pl.BlockSpec((rows, 1024), lambda a, b: (index(a, b)[0], 0))]


def _inproj_wgrad(name, hn, dzs):
    KB = 512
    nsec = dzs.shape[0]

    def body(hn_ref, dz_ref, dw_ref):
        dw_ref[...] = lax.dot_general(hn_ref[...], dz_ref[...], _TN, preferred_element_type=F32).astype(BF16)

    return pl.pallas_call(
        body, name=name,
        grid=(nsec, D // KB),
        in_specs=[pl.BlockSpec((TP, KB), lambda n, kb: (0, kb)),
                  pl.BlockSpec((None, TP, 1024), lambda n, kb: (n, 0, 0))],
        out_specs=pl.BlockSpec((KB, 1024), lambda n, kb: (kb, n)),
        out_shape=jax.ShapeDtypeStruct((D, nsec * 1024), BF16),
        compiler_params=_cparams(),
    )(hn, dzs)


def _sum_win_parts(parts_a, parts_b, parts_c):
    RB = 64

    def body(a_ref, b_ref, c_ref, o_ref):
        def chunk(ci, carry):
            r0 = pl.multiple_of(ci * R, R)
            for ref, base, ncol in ((a_ref, 0, 2048), (b_ref, 2048, 2048), (c_ref, 4096, 1024)):
                for c0 in range(0, ncol, 512):
                    acc = ref[0, pl.ds(r0, R), c0:c0 + 512].astype(F32)
                    for sidx in range(1, NDEV):
                        acc = acc + ref[sidx, pl.ds(r0, R), c0:c0 + 512].astype(F32)
                    o_ref[pl.ds(r0, R), base + c0:base + c0 + 512] = acc.astype(BF16)
            return carry
        lax.fori_loop(0, RB // R, chunk, 0)

    spec = lambda ncol: pl.BlockSpec((NDEV, RB, ncol), lambda i: (0, i, 0))
    return pl.pallas_call(
        body, name="sum_win_parts",
        grid=(D // NDEV // RB,),
        in_specs=[spec(2048), spec(2048), spec(1024)],
        out_specs=pl.BlockSpec((RB, NIN), lambda i: (i, 0)),
        out_shape=jax.ShapeDtypeStruct((D // NDEV, NIN), BF16),
        compiler_params=_cparams(),
    )(parts_a, parts_b, parts_c)


def _inproj_bwd(dzl, dzc, dgc, w_in, h, dout, pre_w):
    nsec = NIN // 1024

    def body(dzl_ref, dzc_ref, dgc_ref, w_ref, h_ref, dout_ref, pw_ref, gx_hbm, dmeta_ref, dpw_ref, acc_s, dh_s, sem):
        i = pl.program_id(0)
        s = pl.program_id(1)

        def gx_copy(t):
            lo, n, off = _tile_rows(t)
            return pltpu.make_async_copy(dh_s.at[pl.ds(off, n)], gx_hbm.at[pl.ds(lo, n)], sem)

        @pl.when(s == 0)
        def _():
            acc_s[...] = jnp.zeros_like(acc_s)

        def use(dz_ref):
            acc_s[...] += lax.dot_general(dz_ref[...], w_ref[...], _NT, preferred_element_type=F32)
        _dz_section(s, dzl_ref, dzc_ref, dgc_ref, use)

        @pl.when(jnp.logical_and(i == 0, s == nsec - 1))
        def _():
            dpw_ref[...] = jnp.zeros_like(dpw_ref)

        @pl.when(s == nsec - 1)
        def _():
            _for_tile(i - 1, lambda t: gx_copy(t).wait())
            pw = pw_ref[...]

            def chunk(ci, carry):
                r0 = pl.multiple_of(ci * 8, 8)
                hv = h_ref[pl.ds(r0, 8), :]
                dhn = acc_s[pl.ds(r0, 8), :]
                rs = lax.rsqrt(jnp.mean(hv * hv, axis=-1, keepdims=True) + EPS)
                dpw_ref[...] += dhn * (hv * rs)
                gw = dhn * pw
                dot = jnp.mean(gw * hv, axis=-1, keepdims=True)
                dh_s[pl.ds(r0, 8), :] = rs * gw - hv * (rs * rs * rs * dot) + dout_ref[pl.ds(r0, 8), :]
                return carry
            lax.fori_loop(0, TM // 8, chunk, 0, unroll=4)
            _for_tile(i, lambda t: gx_copy(t).start())

            @pl.when(i == 0)
            def _():
                dmeta_ref[...] = dh_s[0:NMETA, :]

            @pl.when(i == NTILE - 1)
            def _():
                gx_copy(NTILE - 1).wait()

    row = pl.BlockSpec((TM, D), lambda i, s: (i, 0))
    return pl.pallas_call(
        body, name="inproj_bwd",
        grid=(TP // TM, nsec),
        in_specs=_dz_specs(TM, lambda i, s: (i, s)) + [
            pl.BlockSpec((D, 1024), lambda i, s: (0, s)), row, row, pl.BlockSpec((1, D), lambda i, s: (0, 0))],
        out_specs=[pl.BlockSpec(memory_space=pl.ANY), pl.BlockSpec((NMETA, D), lambda i, s: (0, 0)),
                   pl.BlockSpec((8, D), lambda i, s: (0, 0))],
        out_shape=[jax.ShapeDtypeStruct((SEQ, D), F32), jax.ShapeDtypeStruct((NMETA, D), F32),
                   jax.ShapeDtypeStruct((8, D), F32)],
        scratch_shapes=[pltpu.VMEM((TM, D), F32), pltpu.VMEM((TM, D), F32), pltpu.SemaphoreType.DMA(())],
        compiler_params=_cparams(),
    )(dzl, dzc, dgc, w_in, h, dout, pre_w)


def _adamw(name, parts, w, m, v, block_rows):
    rows, cols = w.shape
    nparts = parts.shape[0]
    cw = cols if cols <= 640 else 512

    def body(p_ref, w_ref, m_ref, v_ref, g_ref, d_ref, nm_ref, nv_ref):
        def chunk(ci, carry):
            r0 = pl.multiple_of(ci * R, R)
            for c0 in range(0, cols, cw):
                at = (pl.ds(r0, R), slice(c0, c0 + cw))
                g = p_ref[(0,) + at].astype(F32)
                for sidx in range(1, nparts):
                    g = g + p_ref[(sidx,) + at].astype(F32)
                delta, mv, vv = _adam_math(g, w_ref[at], m_ref[at], v_ref[at])
                g_ref[at] = g
                nm_ref[at] = mv
                nv_ref[at] = vv
                d_ref[at] = delta
            return carry
        lax.fori_loop(0, block_rows // R, chunk, 0)

    blk = pl.BlockSpec((block_rows, cols), lambda i: (i, 0))
    shp = jax.ShapeDtypeStruct((rows, cols), F32)
    return pl.pallas_call(
        body, name=name,
        grid=(rows // block_rows,),
        in_specs=[pl.BlockSpec((nparts, block_rows, cols), lambda i: (0, i, 0)), blk, blk, blk],
        out_specs=[blk, blk, blk, blk],
        out_shape=[shp, shp, shp, shp],
        compiler_params=_cparams(),
    )(parts, w, m, v)


def _adam_math(g, w, m, v):
    c1 = 1.0 / (1.0 - ADAM_B1 ** ADAM_STEP)
    c2 = 1.0 / (1.0 - ADAM_B2 ** ADAM_STEP)
    mv = ADAM_B1 * m + (1.0 - ADAM_B1) * g
    vv = ADAM_B2 * v + (1.0 - ADAM_B2) * (g * g)
    upd = (mv * c1) / (jnp.sqrt(vv * c2) + ADAM_EPS) + ADAM_WD * w
    return -ADAM_LR * upd, mv, vv


_VEC = [("pre_norm_w", 2), ("post_norm_w", 2), ("b_in", 5), ("lru_conv_b", 1), ("b_gate_a", 1), ("b_gate_x", 1),
        ("lru_lambda", 1), ("conf_dw_b", 1), ("conf_ln_w", 1), ("conf_ln_b", 1), ("conf_pw_b", 1)]
_VEC_ROWS = 24
_LOSS_ROW = 17
_SM_ROWS = 64


def _pack_grads(dprew_acc, dpostw_acc, cvecs, kvecs, lvecs, dcw_acc, ddw_acc, dh, loss_acc):
    def body(pre_ref, post_ref, c_ref, k_ref, l_ref, dcw_ref, ddw_ref, dh_ref, loss_ref, vec_ref, small_ref, tmp):
        s8 = lambda ref, r: jnp.sum(ref[8 * r:8 * r + 8, :], axis=0, keepdims=True)
        vec_ref[...] = jnp.zeros_like(vec_ref)
        pre, post = s8(pre_ref, 0), s8(post_ref, 0)
        rows = [pre[:, 0:1024], pre[:, 1024:2048], post[:, 0:1024], post[:, 1024:2048],
                s8(l_ref, 1), s8(l_ref, 0), s8(k_ref, 1), s8(k_ref, 2), s8(c_ref, 1),
                s8(l_ref, 5), s8(l_ref, 2), s8(l_ref, 3), s8(l_ref, 4),
                s8(k_ref, 0), s8(c_ref, 2), s8(c_ref, 3), s8(c_ref, 0)]
        for r, val in enumerate(rows):
            vec_ref[r:r + 1, :] = val
        vec_ref[_LOSS_ROW:_LOSS_ROW + 1, :] = jnp.zeros((1, 1024), F32) + (0.5 / D) * jnp.sum(loss_ref[...])

        small_ref[...] = jnp.zeros_like(small_ref)
        for k in range(LW):
            tmp[k:k + 1, :] = s8(dcw_ref, k)
        for k in range(KW):
            tmp[8 + k:9 + k, :] = s8(ddw_ref, k)
        for d in range(NDEV):
            small_ref[d, 0:LW, 0:128] = tmp[0:LW, 128 * d:128 * d + 128]
            small_ref[d, 8:8 + KW, 0:128] = tmp[8:8 + KW, 128 * d:128 * d + 128]
            small_ref[d, 40:56, :] = dh_ref[:, 256 * d:256 * d + 256]

    full = lambda a: pl.BlockSpec(a.shape, lambda i: (0,) * a.ndim)
    ins = [dprew_acc, dpostw_acc, cvecs, kvecs, lvecs, dcw_acc, ddw_acc]
    return pl.pallas_call(
        body, name="pack_grads",
        grid=(1,),
        in_specs=[full(a) for a in ins] + [full(dh), full(loss_acc)],
        out_specs=[pl.BlockSpec((_VEC_ROWS, 1024), lambda i: (0, 0)),
                   pl.BlockSpec((NDEV, _SM_ROWS, 256), lambda i: (0, 0, 0))],
        out_shape=[jax.ShapeDtypeStruct((_VEC_ROWS, 1024), F32), jax.ShapeDtypeStruct((NDEV, _SM_ROWS, 256), F32)],
        scratch_shapes=[pltpu.VMEM((40, 1024), F32)],
        compiler_params=_cparams(),
    )(*ins, dh, loss_acc)


def _adamw_vec(parts, W, M, V):
    nv = len(_VEC)

    def body(*refs):
        p_ref = refs[0]
        w_refs, m_refs, v_refs = refs[1:1 + nv], refs[1 + nv:1 + 2 * nv], refs[1 + 2 * nv:1 + 3 * nv]
        outs = refs[1 + 3 * nv:]

        def total(r):
            acc = p_ref[0, r:r + 1, :]
            for sidx in range(1, NDEV):
                acc = acc + p_ref[sidx, r:r + 1, :]
            return acc

        row = 0
        for idx, (_, nrows) in enumerate(_VEC):
            for part in range(nrows):
                cols = slice(1024 * part, 1024 * part + 1024)
                g = total(row + part)
                delta, mv, vv = _adam_math(g, w_refs[idx][:, cols], m_refs[idx][:, cols], v_refs[idx][:, cols])
                for o, val in zip(outs[4 * idx:4 * idx + 4], (g, delta, mv, vv)):
                    o[:, cols] = val
            row += nrows
        outs[-1][...] = total(_LOSS_ROW)[:, 0:128]

    names = [n for n, _ in _VEC]
    flat = lambda d: [d[n].reshape(1, -1) for n in names]
    ws, ms, vs = flat(W), flat(M), flat(V)
    res = pl.pallas_call(
        body, name="adamw_vec",
        out_shape=[jax.ShapeDtypeStruct(w.shape, F32) for w in ws for _ in range(4)]
        + [jax.ShapeDtypeStruct((1, 128), F32)],
        compiler_params=_cparams(),
    )(parts, *ws, *ms, *vs)
    return {n: tuple(res[4 * i:4 * i + 4]) for i, n in enumerate(names)}, res[-1]


def _adamw_small(parts, W, M, V):
    where = {"lru_conv_w": (slice(0, LW), slice(0, 128)), "conf_dw_w": (slice(8, 8 + KW), slice(0, 128)),
             "meta_tokens": (slice(40, 56), slice(0, 256))}
    names = list(where)

    def body(*refs):
        p_ref = refs[0]
        outs = refs[10:]
        for idx, n in enumerate(names):
            rs, cs = where[n]
            g = p_ref[0, rs, cs]
            for sidx in range(1, NDEV):
                g = g + p_ref[sidx, rs, cs]
            delta, mv, vv = _adam_math(g, refs[1 + idx][...], refs[4 + idx][...], refs[7 + idx][...])
            for o, val in zip(outs[4 * idx:4 * idx + 4], (g, delta, mv, vv)):
                o[...] = val

    two_d = lambda a: a.reshape(a.shape[-2:])
    ws, ms, vs = ([two_d(d[n]) for n in names] for d in (W, M, V))
    res = pl.pallas_call(
        body, name="adamw_small",
        out_shape=[jax.ShapeDtypeStruct(w.shape, F32) for w in ws for _ in range(4)],
        compiler_params=_cparams(),
    )(parts, *ws, *ms, *vs)
    return {n: tuple(res[4 * i:4 * i + 4]) for i, n in enumerate(names)}


def _pack_small(lru_cw, dw_w, meta):
    buf = jnp.zeros((_SM_ROWS, 256), F32)
    buf = buf.at[0:LW, 0:128].set(lru_cw)
    buf = buf.at[8:8 + dw_w.shape[0], 0:128].set(dw_w)
    return buf.at[40:56, :].set(meta)


def _block_diag4(w):
    w4 = w.reshape(NCB, 4, 64, 64)
    eye = jnp.eye(4, dtype=w.dtype)
    return jnp.einsum("ghij,hk->ghikj", w4, eye).reshape(NCB, CB, CB)


def _diag_blocks(g):
    g5 = g.reshape(NCB, 4, 64, 4, 64)
    return jnp.stack([g5[:, hh, :, hh, :] for hh in range(4)], axis=1).reshape(16, 64, 64)


def _local_step(x, target, meta_full, inproj, out_weights, lru_cw_full, dw_w_full, W, send):
    wa_g = _block_diag4(W["w_gate_a"][0]).astype(BF16)
    wx_g = _block_diag4(W["w_gate_x"][0]).astype(BF16)

    h, hn = _prenorm(x, meta_full, W["pre_norm_w"])
    z, win_full = inproj(hn)
    ylru, xc, hs = _lru_fwd(z, lru_cw_full, W["lru_conv_b"], wa_g, W["b_gate_a"], wx_g, W["b_gate_x"],
                            W["lru_lambda"])
    vc = _conf_fwd_conv(z, dw_w_full, W["conf_dw_b"])
    wout_full, pw_full = out_weights(vc)
    yconf, p = _conf_fwd_proj(vc, z, W["conf_ln_w"], W["conf_ln_b"], pw_full, W["conf_pw_b"])
    dout, dy, loss_acc, dpostw_acc = _outproj_loss(ylru, yconf, wout_full, h, target, W["post_norm_w"])

    dycat, dwout_part = _outproj_bwd(dy, ylru, yconf, wout_full)
    tok = send("w_out", dwout_part)
    dvc, dgc, dpw_part, cvecs = _conf_bwd_proj(dycat, p, z, vc, W["conf_ln_w"] + tok, W["conf_ln_b"], pw_full)
    tok = send("conf_pw_w", dpw_part)
    tok = tok + send("w_in_c", _inproj_wgrad("inproj_wgrad_c", hn, dgc[None]))
    dzc, ddw_acc, kvecs = _conf_bwd_conv(dvc, z, dw_w_full + tok)
    tok = send("w_in_b", _inproj_wgrad("inproj_wgrad_b", hn, dzc))
    dzl, dwa_g, dwx_g, dcw_acc, lvecs = _lru_bwd(dycat, z, xc, hs, lru_cw_full, wa_g, W["b_gate_a"] + tok, wx_g,
                                                 W["b_gate_x"], W["lru_lambda"])
    tok = send("w_in_a", _inproj_wgrad("inproj_wgrad_a", hn, dzl))
    tok = tok + send("w_gates", _diag_blocks(dwa_g).reshape(16 * 64, 64), _diag_blocks(dwx_g).reshape(16 * 64, 64))
    grad_x, dmeta, dprew_acc = _inproj_bwd(dzl, dzc, dgc, win_full, h, dout, W["pre_norm_w"] + tok)

    vec_pack, small_part = _pack_grads(dprew_acc, dpostw_acc, cvecs, kvecs, lvecs, dcw_acc, ddw_acc, dmeta, loss_acc)
    return grad_x, vec_pack, small_part


def kernel(x, meta_tokens, pre_norm_w, post_norm_w, w_in, b_in, lru_conv_w, lru_conv_b, w_gate_a, b_gate_a, w_gate_x, b_gate_x, lru_lambda, conf_dw_w, conf_dw_b, conf_ln_w, conf_ln_b, conf_pw_w, conf_pw_b, w_out, loss_target, m_meta_tokens, m_pre_norm_w, m_post_norm_w, m_w_in, m_b_in, m_lru_conv_w, m_lru_conv_b, m_w_gate_a, m_b_gate_a, m_w_gate_x, m_b_gate_x, m_lru_lambda, m_conf_dw_w, m_conf_dw_b, m_conf_ln_w, m_conf_ln_b, m_conf_pw_w, m_conf_pw_b, m_w_out, v_meta_tokens, v_pre_norm_w, v_post_norm_w, v_w_in, v_b_in, v_lru_conv_w, v_lru_conv_b, v_w_gate_a, v_b_gate_a, v_w_gate_x, v_b_gate_x, v_lru_lambda, v_conf_dw_w, v_conf_dw_b, v_conf_ln_w, v_conf_ln_b, v_conf_pw_w, v_conf_pw_b, v_w_out):
    W = dict(meta_tokens=meta_tokens, pre_norm_w=pre_norm_w, post_norm_w=post_norm_w, w_in=w_in, b_in=b_in,
             lru_conv_w=lru_conv_w, lru_conv_b=lru_conv_b, w_gate_a=w_gate_a, b_gate_a=b_gate_a,
             w_gate_x=w_gate_x, b_gate_x=b_gate_x, lru_lambda=lru_lambda, conf_dw_w=conf_dw_w,
             conf_dw_b=conf_dw_b, conf_ln_w=conf_ln_w, conf_ln_b=conf_ln_b, conf_pw_w=conf_pw_w,
             conf_pw_b=conf_pw_b, w_out=w_out)
    M = dict(meta_tokens=m_meta_tokens, pre_norm_w=m_pre_norm_w, post_norm_w=m_post_norm_w, w_in=m_w_in,
             b_in=m_b_in, lru_conv_w=m_lru_conv_w, lru_conv_b=m_lru_conv_b, w_gate_a=m_w_gate_a,
             b_gate_a=m_b_gate_a, w_gate_x=m_w_gate_x, b_gate_x=m_b_gate_x, lru_lambda=m_lru_lambda,
             conf_dw_w=m_conf_dw_w, conf_dw_b=m_conf_dw_b, conf_ln_w=m_conf_ln_w, conf_ln_b=m_conf_ln_b,
             conf_pw_w=m_conf_pw_w, conf_pw_b=m_conf_pw_b, w_out=m_w_out)
    V = dict(meta_tokens=v_meta_tokens, pre_norm_w=v_pre_norm_w, post_norm_w=v_post_norm_w, w_in=v_w_in,
             b_in=v_b_in, lru_conv_w=v_lru_conv_w, lru_conv_b=v_lru_conv_b, w_gate_a=v_w_gate_a,
             b_gate_a=v_b_gate_a, w_gate_x=v_w_gate_x, b_gate_x=v_b_gate_x, lru_lambda=v_lru_lambda,
             conf_dw_w=v_conf_dw_w, conf_dw_b=v_conf_dw_b, conf_ln_w=v_conf_ln_w, conf_ln_b=v_conf_ln_b,
             conf_pw_w=v_conf_pw_w, conf_pw_b=v_conf_pw_b, w_out=v_w_out)
    names = list(W.keys())
    shapes = {n: W[n].shape for n in names}

    small = _pack_small(lru_conv_w[0], conf_dw_w[0], meta_tokens)
    win_flight, tok = _win_gather_start(w_in[0].astype(BF16))
    (small_flight,), tok = _exchange_start("gather_small_start", [
        (small + tok[0, 0], jax.ShapeDtypeStruct((NDEV, _SM_ROWS, 256), F32), _whole, _slot)])
    win_flight, tok = _win_gather_links(win_flight, tok)
    gathered, tok = _exchange_start("gather_out_start", [
        (w_out[0].astype(BF16) + tok[0, 0].astype(BF16), jax.ShapeDtypeStruct((D, D), BF16), _whole,
         _rows(D // NDEV)),
        (conf_pw_w[0].astype(BF16), jax.ShapeDtypeStruct((DC, DC), BF16), _whole, _rows(DC // NDEV)),
    ])
    (small_all,) = _exchange_wait("gather_small_wait", [small_flight], tok)
    unshard = lambda a: jnp.transpose(a, (1, 0, 2)).reshape(a.shape[1], -1)
    lru_cw_full = unshard(small_all[:, 0:LW, 0:128])
    dw_w_full = unshard(small_all[:, 8:8 + KWP, 0:128])
    meta_full = unshard(small_all[:, 40:56, :])

    def out_weights(after):
        return _exchange_wait("gather_out_wait", gathered, after)

    def inproj(hn):
        xi, yi, ci = lax.axis_index("x"), lax.axis_index("y"), lax.axis_index("c")
        shard = lambda px, py, pc: (4 * px + 2 * py + pc).astype(jnp.int32)
        here = jnp.stack([shard(xi, yi, ci), shard(xi, yi, 1 - ci)])
        over_links = jnp.stack([shard(1 - xi, yi, ci), shard(xi, 1 - yi, ci), shard(1 - xi, 1 - yi, ci)])
        flight = _win_gather_early(win_flight)
        z, land = _inproj_cols("inproj_here", here, hn, flight["land"], b_in, None)
        flight = _win_gather_forward(dict(flight, land=land), z)
        z, land = _inproj_cols("inproj_links", over_links, hn, flight["land"], b_in, z)
        land = _win_gather_wait(dict(flight, land=land))
        return _inproj_cols("inproj_sibling", over_links + 1 - 2 * ci, hn, land, b_in, z)

    row_stage = lambda ncol: (jax.ShapeDtypeStruct((NDEV, D // NDEV, ncol), BF16), _rows(D // NDEV))
    piece = {"w_in_a": row_stage(2048), "w_in_b": row_stage(2048), "w_in_c": row_stage(1024),
             "w_out": row_stage(D),
             "conf_pw_w": (jax.ShapeDtypeStruct((NDEV, DC // NDEV, DC), BF16), _rows(DC // NDEV)),
             "w_gates": (jax.ShapeDtypeStruct((NDEV, 16 * 64, 64), BF16), _whole)}
    sent = {}

    def send(name, *parts):
        handles, token = _exchange_start(
            "scatter_" + name + "_start",
            [(part.astype(BF16), piece[name][0], piece[name][1], _slot) for part in parts])
        sent[name] = handles
        return token[0, 0]

    grad_x, vec_pack, small_part = _local_step(
        x[0], loss_target[0], meta_full, inproj, out_weights, lru_cw_full, dw_w_full, W, send)
    grad_x = grad_x[None]

    (parts_c,) = _exchange_wait("scatter_w_in_c_wait", sent["w_in_c"], vec_pack)
    (parts_b,) = _exchange_wait("scatter_w_in_b_wait", sent["w_in_b"], parts_c)
    (parts_a,) = _exchange_wait("scatter_w_in_a_wait", sent["w_in_a"], parts_b)
    win_rows = _sum_win_parts(parts_a, parts_b, parts_c)
    win_stage2, tok = _exchange_start("scatter_w_in_stage2_start", [
        (win_rows, jax.ShapeDtypeStruct((NDEV, D // NDEV, NIN // NDEV), BF16), _cols(NIN // NDEV), _slot)])
    rest, _ = _exchange_start("scatter_rest_start", [
        (small_part, jax.ShapeDtypeStruct((NDEV, _SM_ROWS, 256), F32), _slot, _slot),
        (vec_pack + tok[0, 0], jax.ShapeDtypeStruct((NDEV, _VEC_ROWS, 1024), F32), _whole, _slot),
    ])

    G, DW, NM, NV = {}, {}, {}, {}
    (wout_parts,) = _exchange_wait("scatter_w_out_wait", sent["w_out"], win_rows)
    G["w_out"], DW["w_out"], NM["w_out"], NV["w_out"] = _adamw("adamw_w_out", wout_parts, w_out[0], m_w_out[0], v_w_out[0], 64)
    (pw_parts,) = _exchange_wait("scatter_conf_pw_w_wait", sent["conf_pw_w"], G["w_out"])
    G["conf_pw_w"], DW["conf_pw_w"], NM["conf_pw_w"], NV["conf_pw_w"] = _adamw(
        "adamw_pw", pw_parts, conf_pw_w[0], m_conf_pw_w[0], v_conf_pw_w[0], 128)
    res = {}
    wa_parts, wx_parts = _exchange_wait("scatter_w_gates_wait", sent["w_gates"], G["conf_pw_w"])
    for n, parts in (("w_gate_a", wa_parts), ("w_gate_x", wx_parts)):
        res[n] = _adamw("adamw_" + n, parts, *[d[n].reshape(16 * 64, 64) for d in (W, M, V)], 16 * 64)
    small_parts, vec_parts = _exchange_wait("scatter_rest_wait", rest, res["w_gate_x"][0])
    res.update(_adamw_small(small_parts, W, M, V))
    vec_res, loss_row = _adamw_vec(vec_parts, W, M, V)
    res.update(vec_res)
    (win_sum,) = _exchange_wait("scatter_w_in_stage2_wait", win_stage2, loss_row)
    res["w_in"] = _adamw("adamw_w_in", win_sum.reshape(1, D, NIN // NDEV), w_in[0], m_w_in[0], v_w_in[0], 256)
    for n, vals in res.items():
        for dst, val in zip((G, DW, NM, NV), vals):
            dst[n] = val
    for dst in (G, DW, NM, NV):
        for n in names:
            dst[n] = dst[n].reshape(shapes[n])
    loss = loss_row[0, 0]

    return (loss, grad_x, *[G[n] for n in names], *[DW[n] for n in names],
            *[NM[n] for n in names], *[NV[n] for n in names])
```

```python
import functools

import jax
import jax.numpy as jnp
from jax import lax
from jax.experimental import pallas as pl
from jax.experimental.pallas import tpu as pltpu

F32 = jnp.float32
BF16 = jnp.bfloat16

D = 2048
DL = 1024
DC = 1024
NIN = 5120
NMETA = 16
SEQ = 2048
T = NMETA + SEQ
TP = 2176
TM = 544
CB = 256
NCB = DL // CB
R = 16
KW = 31
KWP = 32
LW = 4
LRU_C = 8.0
EPS = 1e-6
NDEV = 8

ADAM_LR = 0.001
ADAM_B1 = 0.9
ADAM_B2 = 0.999
ADAM_EPS = 1e-08
ADAM_WD = 0.01
ADAM_STEP = 10

VMEM_LIMIT = 56 * 1024 * 1024


def _cparams():
    return pltpu.CompilerParams(vmem_limit_bytes=VMEM_LIMIT)


def _sig(x):
    return 1.0 / (1.0 + jnp.exp(-x))


def _expm1_neg(y):
    poly = y * (1.0 + y * (0.5 + y * (1.0 / 6.0 + y * (1.0 / 24.0 + y * (1.0 / 120.0)))))
    return jnp.where(y > -0.1, poly, jnp.exp(y) - 1.0)


def _softplus(x):
    e = jnp.exp(-jnp.abs(x))
    w = 1.0 + e
    l1p = jnp.where(w == 1.0, e, jnp.log(w) * e / (w - 1.0))
    return jnp.maximum(x, 0.0) + l1p


def _row_iota(shape):
    return lax.broadcasted_iota(jnp.int32, shape, 0)


def _fold8(v):
    return v[0:8, :] + v[8:16, :]


_FLIPS = [(k >> 2 & 1, k >> 1 & 1, k & 1) for k in range(1, NDEV)]
_HBM = pl.BlockSpec(memory_space=pltpu.HBM)
_SEM = pl.BlockSpec(memory_space=pltpu.SEMAPHORE)


def _peers():
    x, y, c = lax.axis_index("x"), lax.axis_index("y"), lax.axis_index("c")
    out = []
    for dx, dy, dc in _FLIPS:
        px = 1 - x if dx else x
        py = 1 - y if dy else y
        pc = 1 - c if dc else c
        out.append(((px, py, pc), 4 * px + 2 * py + pc))
    return 4 * x + 2 * y + c, out


def _exchange_start(name, items):
    n = len(items)

    def body(*refs):
        srcs, lands = refs[:n], refs[n:2 * n]
        outs = refs[2 * n:]
        send_sems, recv_sems, local_sems = outs[:n], outs[n:2 * n], outs[2 * n:3 * n]
        token = outs[-1]
        me, peers = _peers()
        for a in range(n):
            src_at, dst_at = items[a][2], items[a][3]
            pltpu.make_async_copy(src_at(srcs[a], me), dst_at(lands[a], me), local_sems[a]).start()
        for a in range(n):
            src_at, dst_at = items[a][2], items[a][3]
            for k, (pos, peer) in enumerate(peers):
                pltpu.make_async_remote_copy(
                    src_ref=src_at(srcs[a], peer), dst_ref=dst_at(lands[a], me),
                    send_sem=send_sems[a].at[k], recv_sem=recv_sems[a].at[k],
                    device_id=pos, device_id_type=pl.DeviceIdType.MESH).start()
        token[...] = jnp.zeros_like(token)

    srcs = [pltpu.with_memory_space_constraint(it[0], pltpu.HBM) for it in items]
    lands = [pltpu.with_memory_space_constraint(lax.empty(it[1].shape, it[1].dtype), pltpu.HBM) for it in items]
    sem7 = pltpu.SemaphoreType.DMA((NDEV - 1,))
    res = pl.pallas_call(
        body, name=name,
        out_shape=([sem7] * (2 * n) + [pltpu.SemaphoreType.DMA(())] * n
                   + [pltpu.HBM(a.shape, a.dtype) for a in srcs] + [pltpu.HBM(a.shape, a.dtype) for a in lands]
                   + [jax.ShapeDtypeStruct((8, 128), F32)]),
        in_specs=[_HBM] * (2 * n),
        out_specs=[_SEM] * (3 * n) + [_HBM] * (2 * n) + [pl.BlockSpec(memory_space=pltpu.VMEM)],
        input_output_aliases={i: 3 * n + i for i in range(2 * n)},
        compiler_params=pltpu.CompilerParams(has_side_effects=pltpu.SideEffectType.DATAFLOW_SIDE_EFFECTING),
    )(*srcs, *lands)
    handles = [dict(send=res[a], recv=res[n + a], local=res[2 * n + a], src=res[3 * n + a], land=res[4 * n + a],
                    src_at=items[a][2], dst_at=items[a][3]) for a in range(n)]
    return handles, res[-1]


def _wait_bytes(piece, sem):
    pltpu.make_async_copy(piece, piece, sem).wait()


def _exchange_wait(name, handles, after):
    n = len(handles)

    def body(*refs):
        srcs, lands = refs[:n], refs[n:2 * n]
        send_sems, recv_sems, local_sems = refs[2 * n:3 * n], refs[3 * n:4 * n], refs[4 * n:5 * n]
        me, peers = _peers()
        for a in range(n):
            src_at, dst_at = handles[a]["src_at"], handles[a]["dst_at"]
            for k, (pos, peer) in enumerate(peers):
                _wait_bytes(src_at(srcs[a], peer), send_sems[a].at[k])
                _wait_bytes(dst_at(lands[a], peer), recv_sems[a].at[k])
            pltpu.make_async_copy(src_at(srcs[a], me), dst_at(lands[a], me), local_sems[a]).wait()

    srcs = [hd["src"] for hd in handles]
    lands = [hd["land"] for hd in handles]
    res = pl.pallas_call(
        body, name=name,
        out_shape=[pltpu.HBM(a.shape, a.dtype) for a in srcs] + [pltpu.HBM(a.shape, a.dtype) for a in lands],
        in_specs=[_HBM] * (2 * n) + [_SEM] * (3 * n) + [pl.BlockSpec(memory_space=pl.ANY)],
        out_specs=[_HBM] * (2 * n),
        input_output_aliases={i: i for i in range(2 * n)},
        compiler_params=pltpu.CompilerParams(has_side_effects=pltpu.SideEffectType.DATAFLOW_SIDE_EFFECTING),
    )(*srcs, *lands, *[hd["send"] for hd in handles], *[hd["recv"] for hd in handles],
      *[hd["local"] for hd in handles], after)
    return list(res[n:])


_SIDE = pltpu.SideEffectType.DATAFLOW_SIDE_EFFECTING
_WCOLS = NIN // NDEV


def _win_cols(ref, l):
    return ref.at[:, pl.ds(pl.multiple_of(l * _WCOLS, 128), _WCOLS)]


def _win_routes():
    x, y, c = lax.axis_index("x"), lax.axis_index("y"), lax.axis_index("c")
    pos = [(x, y, 1 - c), (1 - x, y, c), (x, 1 - y, c), (1 - x, 1 - y, c)]
    return 4 * x + 2 * y + c, [(p, 4 * p[0] + 2 * p[1] + p[2]) for p in pos]


def _win_gather_start(shard):
    def body(src, land, send_sem, recv_sem, local_sem, src_thru, land_thru, token):
        me, routes = _win_routes()
        pltpu.make_async_copy(src, _win_cols(land, me), local_sem).start()
        pltpu.make_async_remote_copy(src_ref=src, dst_ref=_win_cols(land, me), send_sem=send_sem, recv_sem=recv_sem,
                                     device_id=routes[0][0], device_id_type=pl.DeviceIdType.MESH).start()
        token[...] = jnp.zeros_like(token)

    src = pltpu.with_memory_space_constraint(shard, pltpu.HBM)
    land = pltpu.with_memory_space_constraint(lax.empty((D, NIN), BF16), pltpu.HBM)
    sem = pltpu.SemaphoreType.DMA(())
    res = pl.pallas_call(
        body, name="win_gather_start",
        out_shape=[sem, sem, sem, pltpu.HBM(src.shape, BF16), pltpu.HBM(land.shape, BF16),
                   jax.ShapeDtypeStruct((8, 128), F32)],
        in_specs=[_HBM, _HBM],
        out_specs=[_SEM, _SEM, _SEM, _HBM, _HBM, pl.BlockSpec(memory_space=pltpu.VMEM)],
        input_output_aliases={0: 3, 1: 4},
        compiler_params=pltpu.CompilerParams(has_side_effects=_SIDE),
    )(src, land)
    return dict(send0=res[0], recv0=res[1], local=res[2], src=res[3], land=res[4]), res[5]


def _win_gather_links(hd, after):
    def body(src, land, after_ref, send_sems, recv_sems, src_thru, land_thru, token):
        me, routes = _win_routes()
        for k in (1, 2, 3):
            pltpu.make_async_remote_copy(src_ref=src, dst_ref=_win_cols(land, me), send_sem=send_sems.at[k - 1],
                                         recv_sem=recv_sems.at[k - 1], device_id=routes[k][0],
                                         device_id_type=pl.DeviceIdType.MESH).start()
        token[...] = jnp.zeros_like(token)

    sem3 = pltpu.SemaphoreType.DMA((3,))
    res = pl.pallas_call(
        body, name="win_gather_links",
        out_shape=[sem3, sem3, pltpu.HBM(hd["src"].shape, BF16), pltpu.HBM(hd["land"].shape, BF16),
                   jax.ShapeDtypeStruct((8, 128), F32)],
        in_specs=[_HBM, _HBM, pl.BlockSpec(memory_space=pl.ANY)],
        out_specs=[_SEM, _SEM, _HBM, _HBM, pl.BlockSpec(memory_space=pltpu.VMEM)],
        input_output_aliases={0: 2, 1: 3},
        compiler_params=pltpu.CompilerParams(has_side_effects=_SIDE),
    )(hd["src"], hd["land"], after)
    return dict(hd, send=res[0], recv=res[1], src=res[2], land=res[3]), res[4]


def _win_gather_forward(hd, after):
    def body(land, recv_sems, after_ref, land_thru, fsend_sems, frecv_sems):
        me, routes = _win_routes()
        sibling = routes[0][0]
        for k in (1, 2, 3):
            pos, peer = routes[k]
            piece = _win_cols(land, peer)
            pltpu.make_async_remote_copy(src_ref=piece, dst_ref=piece, send_sem=fsend_sems.at[k - 1],
                                         recv_sem=recv_sems.at[k - 1], device_id=pos,
                                         device_id_type=pl.DeviceIdType.MESH).wait_recv()
            pltpu.make_async_remote_copy(src_ref=piece, dst_ref=piece, send_sem=fsend_sems.at[k - 1],
                                         recv_sem=frecv_sems.at[k - 1], device_id=sibling,
                                         device_id_type=pl.DeviceIdType.MESH).start()

    sem3 = pltpu.SemaphoreType.DMA((3,))
    res = pl.pallas_call(
        body, name="win_gather_forward",
        out_shape=[pltpu.HBM(hd["land"].shape, BF16), sem3, sem3],
        in_specs=[_HBM, _SEM, pl.BlockSpec(memory_space=pl.ANY)],
        out_specs=[_HBM, _SEM, _SEM],
        input_output_aliases={0: 0},
        compiler_params=pltpu.CompilerParams(has_side_effects=_SIDE),
    )(hd["land"], hd["recv"], after)
    return dict(hd, land=res[0], fsend=res[1], frecv=res[2])


def _win_gather_early(hd):
    def body(src, land, recv_sem, local_sem, src_thru, land_thru):
        me, routes = _win_routes()
        _wait_bytes(_win_cols(land, routes[0][1]), recv_sem)
        pltpu.make_async_copy(src, _win_cols(land, me), local_sem).wait()

    res = pl.pallas_call(
        body, name="win_gather_early",
        out_shape=[pltpu.HBM(hd["src"].shape, BF16), pltpu.HBM(hd["land"].shape, BF16)],
        in_specs=[_HBM, _HBM, _SEM, _SEM],
        out_specs=[_HBM, _HBM],
        input_output_aliases={0: 0, 1: 1},
        compiler_params=pltpu.CompilerParams(has_side_effects=_SIDE),
    )(hd["src"], hd["land"], hd["recv0"], hd["local"])
    return dict(hd, src=res[0], land=res[1])


def _win_gather_wait(hd):
    def body(src, land, send0_sem, send_sems, fsend_sems, frecv_sems, src_thru, land_thru):
        me, routes = _win_routes()
        sib_pos, sibling = routes[0]
        for k in range(4):
            _wait_bytes(src, send0_sem if k == 0 else send_sems.at[k - 1])
        for k in (1, 2, 3):
            _wait_bytes(_win_cols(land, routes[k][1]), fsend_sems.at[k - 1])
            _wait_bytes(_win_cols(land, 4 * routes[k][0][0] + 2 * routes[k][0][1] + sib_pos[2]), frecv_sems.at[k - 1])

    res = pl.pallas_call(
        body, name="win_gather_wait",
        out_shape=[pltpu.HBM(hd["src"].shape, BF16), pltpu.HBM(hd["land"].shape, BF16)],
        in_specs=[_HBM, _HBM] + [_SEM] * 4,
        out_specs=[_HBM, _HBM],
        input_output_aliases={0: 0, 1: 1},
        compiler_params=pltpu.CompilerParams(has_side_effects=_SIDE),
    )(hd["src"], hd["land"], hd["send0"], hd["send"], hd["fsend"], hd["frecv"])
    return res[1]


def _whole(ref, l):
    return ref


def _slot(ref, l):
    return ref.at[l]


def _cols(width):
    def at(ref, l):
        return ref.at[:, pl.ds(pl.multiple_of(l * width, 128), width)]
    return at


def _rows(height):
    def at(ref, l):
        return ref.at[pl.ds(pl.multiple_of(l * height, 8), height), :]
    return at


NTILE = TP // TM


def _tile_rows(t):
    lo = max(t * TM - NMETA, 0)
    hi = min((t + 1) * TM - NMETA, SEQ)
    return lo, hi - lo, lo + NMETA - t * TM


def _for_tile(t, fn):
    for static_t in range(NTILE):
        pl.when(t == static_t)(functools.partial(fn, static_t))


def _token_tile_copy(hbm_ref, buf, sem, t):
    lo, n, off = _tile_rows(t)
    return pltpu.make_async_copy(hbm_ref.at[pl.ds(lo, n)], buf.at[pl.ds(off, n)], sem)


def _prenorm(x, meta_full, pre_w):
    def body(x_ref, meta_ref, pw_ref, h_ref, hn_ref, xbuf, sems):
        i = pl.program_id(0)
        slot = i % 2

        def start(t):
            _token_tile_copy(x_ref, xbuf.at[t % 2], sems.at[t % 2], t).start()

        @pl.when(i == 0)
        def _():
            start(0)
        _for_tile(i + 1, start)
        _for_tile(i, lambda t: _token_tile_copy(x_ref, xbuf.at[t % 2], sems.at[t % 2], t).wait())

        @pl.when(i == 0)
        def _():
            xbuf[0, 0:NMETA, :] = meta_ref[...]

        @pl.when(i == NTILE - 1)
        def _():
            last = _tile_rows(NTILE - 1)[1]
            xbuf[(NTILE - 1) % 2, last:TM, :] = jnp.zeros((TM - last, D), F32)

        pw = pw_ref[...]

        def chunk(ci, carry):
            r0 = pl.multiple_of(ci * R, R)
            xv = xbuf[slot, pl.ds(r0, R), :]
            h_ref[pl.ds(r0, R), :] = xv
            ms = jnp.mean(xv * xv, axis=-1, keepdims=True)
            hn_ref[pl.ds(r0, R), :] = (xv * lax.rsqrt(ms + EPS) * pw).astype(BF16)
            return carry
        lax.fori_loop(0, TM // R, chunk, 0, unroll=2)

    row = pl.BlockSpec((TM, D), lambda i: (i, 0))
    return pl.pallas_call(
        body, name="prenorm",
        grid=(NTILE,),
        in_specs=[pl.BlockSpec(memory_space=pl.ANY), pl.BlockSpec((NMETA, D), lambda i: (0, 0)),
                  pl.BlockSpec((1, D), lambda i: (0, 0))],
        out_specs=[row, row],
        out_shape=[jax.ShapeDtypeStruct((TP, D), F32), jax.ShapeDtypeStruct((TP, D), BF16)],
        scratch_shapes=[pltpu.VMEM((2, TM, D), F32), pltpu.SemaphoreType.DMA((2,))],
        compiler_params=_cparams(),
    )(x, meta_full, pre_w)


def _inproj_cols(name, shards, hn, w_land, b_in, z_prev):
    nsh = shards.shape[0]

    def body(idx_ref, hn_ref, w_ref, b_ref, *rest):
        z_ref = rest[-2]
        z_ref[...] = jnp.dot(hn_ref[...], w_ref[...], preferred_element_type=F32) + b_ref[...]

    any_spec = pl.BlockSpec(memory_space=pl.ANY)
    in_specs = [pl.BlockSpec((TM, D), lambda j, i, idx: (i, 0)),
                pl.BlockSpec((D, _WCOLS), lambda j, i, idx: (0, idx[j])),
                pl.BlockSpec((1, _WCOLS), lambda j, i, idx: (0, idx[j]))]
    operands = [hn, w_land, b_in]
    aliases = {2: 1}
    if z_prev is not None:
        in_specs.append(any_spec)
        operands.append(z_prev)
        aliases[4] = 0
    return pl.pallas_call(
        body, name=name,
        grid_spec=pltpu.PrefetchScalarGridSpec(
            num_scalar_prefetch=1, grid=(nsh, TP // TM), in_specs=in_specs,
            out_specs=[pl.BlockSpec((TM, _WCOLS), lambda j, i, idx: (i, idx[j])), any_spec]),
        out_shape=[jax.ShapeDtypeStruct((TP, NIN), F32), jax.ShapeDtypeStruct(w_land.shape, w_land.dtype)],
        input_output_aliases=aliases,
        compiler_params=_cparams(),
    )(shards, *operands)


def _gate_values(ga, gx, xc, sp8):
    r = _sig(ga)
    i = _sig(gx)
    log_a = -(r * sp8)
    a = jnp.exp(log_a)
    mult = jnp.sqrt(-_expm1_neg(2.0 * log_a))
    return r, i, a, mult


def _lru_fwd(z, conv_w, conv_b, wa_g, b_a, wx_g, b_x, lam):
    def body(x_ref, g_ref, cw_ref, cb_ref, wa_ref, ba_ref, wx_ref, bx_ref, lam_ref,
             y_ref, xc_ref, hs_ref, ga_s, gx_s):
        taps = [cw_ref[k:k + 1, :] for k in range(LW)]
        cb = cb_ref[...]

        def conv_chunk(ci, carry):
            r0 = pl.multiple_of(ci * R, R)
            cur = x_ref[pl.ds(r0, R), :]
            p0 = pl.multiple_of(jnp.maximum(r0 - 8, 0), 8)
            prev = jnp.where(ci > 0, x_ref[pl.ds(p0, 8), :], 0.0)
            buf = jnp.concatenate([prev, cur], axis=0)
            acc = cur * taps[LW - 1] + cb
            for s in range(1, LW):
                acc = acc + pltpu.roll(buf, s, 0)[8:8 + R, :] * taps[LW - 1 - s]
            xc_ref[pl.ds(r0, R), :] = acc
            return carry
        lax.fori_loop(0, TP // R, conv_chunk, 0)

        def gate_chunk(ci, carry):
            r0 = pl.multiple_of(ci * TM, TM)
            xb = xc_ref[pl.ds(r0, TM), :].astype(BF16)
            ga_s[pl.ds(r0, TM), :] = jnp.dot(xb, wa_ref[...], preferred_element_type=F32) + ba_ref[...]
            gx_s[pl.ds(r0, TM), :] = jnp.dot(xb, wx_ref[...], preferred_element_type=F32) + bx_ref[...]
            return carry
        lax.fori_loop(0, TP // TM, gate_chunk, 0)

        sp8 = LRU_C * _softplus(-lam_ref[...])
        row = _row_iota((R, CB))

        def scan_chunk(ci, hprev):
            r0 = pl.multiple_of(ci * R, R)
            xc = xc_ref[pl.ds(r0, R), :]
            _, i, a, mult = _gate_values(ga_s[pl.ds(r0, R), :], gx_s[pl.ds(r0, R), :], xc, sp8)
            u = mult * (i * xc)
            k = 1
            while k < R:
                m = row >= k
                u = jnp.where(m, a * pltpu.roll(u, k, 0) + u, u)
                a = jnp.where(m, a * pltpu.roll(a, k, 0), a)
                k *= 2
            hv = u + a * hprev
            hs_ref[pl.ds(r0, R), :] = hv
            g = g_ref[pl.ds(r0, R), :]
            y_ref[pl.ds(r0, R), :] = (hv * (g * _sig(g))).astype(BF16)
            return jnp.sum(jnp.where(row == R - 1, hv, 0.0), axis=0, keepdims=True)
        lax.fori_loop(0, TP // R, scan_chunk, jnp.zeros((1, CB), F32))

    col = lambda off: pl.BlockSpec((TP, CB), lambda j: (0, off + j))
    vec = pl.BlockSpec((1, CB), lambda j: (0, j))
    wsp = pl.BlockSpec((None, CB, CB), lambda j: (j, 0, 0))
    return pl.pallas_call(
        body, name="lru_fwd",
        grid=(NCB,),
        in_specs=[col(0), col(NCB), pl.BlockSpec((LW, CB), lambda j: (0, j)), vec, wsp, vec, wsp, vec, vec],
        out_specs=[col(0), col(0), col(0)],
        out_shape=[jax.ShapeDtypeStruct((TP, DL), BF16), jax.ShapeDtypeStruct((TP, DL), F32),
                   jax.ShapeDtypeStruct((TP, DL), F32)],
        scratch_shapes=[pltpu.VMEM((TP, CB), F32), pltpu.VMEM((TP, CB), F32)],
        compiler_params=_cparams(),
    )(z, z, conv_w, conv_b, wa_g, b_a, wx_g, b_x, lam)


def _conf_fwd_conv(z, dw_w, dw_b):
    def body(u1_ref, u2_ref, w_ref, b_ref, vc_ref, vs):
        vs[pl.ds(0, KWP), :] = jnp.zeros((KWP, CB), F32)

        def glu_chunk(ci, carry):
            r0 = pl.multiple_of(ci * R, R)
            vs[pl.ds(KWP + r0, R), :] = u1_ref[pl.ds(r0, R), :] * _sig(u2_ref[pl.ds(r0, R), :])
            return carry
        lax.fori_loop(0, TP // R, glu_chunk, 0)

        bias = b_ref[...]

        def conv_chunk(ci, carry):
            r0 = pl.multiple_of(ci * R, R)
            buf = vs[pl.ds(r0, KWP + R), :]
            acc = jnp.zeros((R, CB), F32) + bias
            for rr in range(8):
                rolled = buf if rr == 0 else pltpu.roll(buf, rr, 0)
                for q in range(4):
                    s = 8 * q + rr
                    if s > KW - 1:
                        continue
                    k = KW - 1 - s
                    acc = acc + rolled[KWP - 8 * q:KWP - 8 * q + R, :] * w_ref[k:k + 1, :]
            vc_ref[pl.ds(r0, R), :] = acc
            return carry
        lax.fori_loop(0, TP // R, conv_chunk, 0)

    return pl.pallas_call(
        body, name="conf_fwd_conv",
        grid=(NCB,),
        in_specs=[pl.BlockSpec((TP, CB), lambda j: (0, 2 * NCB + j)),
                  pl.BlockSpec((TP, CB), lambda j: (0, 3 * NCB + j)),
                  pl.BlockSpec((KWP, CB), lambda j: (0, j)),
                  pl.BlockSpec((1, CB), lambda j: (0, j))],
        out_specs=pl.BlockSpec((TP, CB), lambda j: (0, j)),
        out_shape=jax.ShapeDtypeStruct((TP, DC), F32),
        scratch_shapes=[pltpu.VMEM((TP + KWP, CB), F32)],
        compiler_params=_cparams(),
    )(z, z, dw_w, dw_b)


def _ln_chunk(vc, lw, lb):
    mu = jnp.mean(vc, axis=-1, keepdims=True)
    xm = vc - mu
    var = jnp.mean(xm * xm, axis=-1, keepdims=True)
    rstd = lax.rsqrt(var + EPS)
    xhat = xm * rstd
    return xhat, rstd, xhat * lw + lb


def _conf_fwd_proj(vc, z, ln_w, ln_b, pw_w, pw_b):
    def body(vc_ref, g_ref, lw_ref, lb_ref, w_ref, b_ref, y_ref, p_ref, s_s):
        lw, lb = lw_ref[...], lb_ref[...]

        def ln_chunk(ci, carry):
            r0 = pl.multiple_of(ci * R, R)
            for half in range(2):
                rr = r0 + 8 * half
                _, _, ln = _ln_chunk(vc_ref[pl.ds(rr, 8), :], lw, lb)
                p_ref[pl.ds(rr, 8), :] = ln * _sig(ln)
            s_s[pl.ds(r0, R), :] = p_ref[pl.ds(r0, R), :].astype(BF16)
            return carry
        lax.fori_loop(0, TM // R, ln_chunk, 0, unroll=2)

        p_ref[...] = jnp.dot(s_s[...], w_ref[...], preferred_element_type=F32) + b_ref[...]

        def out_chunk(ci, carry):
            r0 = pl.multiple_of(ci * R, R)
            g = g_ref[pl.ds(r0, R), :]
            y_ref[pl.ds(r0, R), :] = (p_ref[pl.ds(r0, R), :] * (g * _sig(g))).astype(BF16)
            return carry
        lax.fori_loop(0, TM // R, out_chunk, 0)

    row = pl.BlockSpec((TM, DC), lambda i: (i, 0))
    vec = pl.BlockSpec((1, DC), lambda i: (0, 0))
    return pl.pallas_call(
        body, name="conf_fwd_proj",
        grid=(TP // TM,),
        in_specs=[row, pl.BlockSpec((TM, DC), lambda i: (i, 4)), vec, vec,
                  pl.BlockSpec((DC, DC), lambda i: (0, 0)), vec],
        out_specs=[row, row],
        out_shape=[jax.ShapeDtypeStruct((TP, DC), BF16), jax.ShapeDtypeStruct((TP, DC), F32)],
        scratch_shapes=[pltpu.VMEM((TM, DC), BF16)],
        compiler_params=_cparams(),
    )(vc, z, ln_w, ln_b, pw_w, pw_b)


def _outproj_loss(ylru, yconf, w_out, h, target, post_w):
    def body(yl_ref, yc_ref, w_ref, h_ref, tgt_hbm, pw_ref, dout_ref, dy_ref, loss_ref, dpw_ref, y_s, t_ref, sem):
        i = pl.program_id(0)
        k = pl.program_id(1)

        @pl.when(k == 0)
        def _():
            _for_tile(i, lambda t: _token_tile_copy(tgt_hbm, t_ref, sem, t).start())
            y_s[...] = jnp.dot(yl_ref[...], w_ref[...], preferred_element_type=F32)

        @pl.when(k == 1)
        def _():
            y_s[...] += jnp.dot(yc_ref[...], w_ref[...], preferred_element_type=F32)

        @pl.when(jnp.logical_and(i == 0, k == 1))
        def _():
            loss_ref[...] = jnp.zeros_like(loss_ref)
            dpw_ref[...] = jnp.zeros_like(dpw_ref)

        @pl.when(k == 1)
        def _():
            _for_tile(i, lambda t: _token_tile_copy(tgt_hbm, t_ref, sem, t).wait())

            @pl.when(i == 0)
            def _():
                t_ref[0:NMETA, :] = jnp.zeros((NMETA, D), F32)

            @pl.when(i == NTILE - 1)
            def _():
                last = _tile_rows(NTILE - 1)[1]
                t_ref[last:TM, :] = jnp.zeros((TM - last, D), F32)

            pw = pw_ref[...]
            row = _row_iota((8, D))

            def chunk(ci, carry):
                r0 = pl.multiple_of(ci * 8, 8)
                yv = y_s[pl.ds(r0, 8), :]
                rs = lax.rsqrt(jnp.mean(yv * yv, axis=-1, keepdims=True) + EPS)
                grow = row + (i * TM + r0)
                valid = jnp.logical_and(grow >= NMETA, grow < T)
                yn = yv * rs
                err = jnp.where(valid, h_ref[pl.ds(r0, 8), :] + yn * pw - t_ref[pl.ds(r0, 8), :], 0.0)
                loss_ref[...] += err * err
                d_rn = err * (1.0 / D)
                dout_ref[pl.ds(r0, 8), :] = d_rn
                dpw_ref[...] += d_rn * yn
                gw = d_rn * pw
                dot = jnp.mean(gw * yv, axis=-1, keepdims=True)
                dy_ref[pl.ds(r0, 8), :] = (rs * gw - yv * (rs * rs * rs * dot)).astype(BF16)
                return carry
            lax.fori_loop(0, TM // 8, chunk, 0, unroll=4)

    row = pl.BlockSpec((TM, D), lambda i, k: (i, 0))
    half = pl.BlockSpec((TM, DL), lambda i, k: (i, 0))
    acc = pl.BlockSpec((8, D), lambda i, k: (0, 0))
    return pl.pallas_call(
        body, name="outproj_loss",
        grid=(TP // TM, 2),
        in_specs=[half, half, pl.BlockSpec((DL, D), lambda i, k: (k, 0)), row, pl.BlockSpec(memory_space=pl.ANY),
                  pl.BlockSpec((1, D), lambda i, k: (0, 0))],
        out_specs=[row, row, acc, acc],
        out_shape=[jax.ShapeDtypeStruct((TP, D), F32), jax.ShapeDtypeStruct((TP, D), BF16),
                   jax.ShapeDtypeStruct((8, D), F32), jax.ShapeDtypeStruct((8, D), F32)],
        scratch_shapes=[pltpu.VMEM((TM, D), F32), pltpu.VMEM((TM, D), F32), pltpu.SemaphoreType.DMA(())],
        compiler_params=_cparams(),
    )(ylru, yconf, w_out, h, target, post_w)


_NT = (((1,), (1,)), ((), ()))
_TN = (((0,), (0,)), ((), ()))


def _outproj_bwd(dy, ylru, yconf, w_out):
    def body(dy_ref, yl_ref, yc_ref, w_ref, dycat_ref, dw_ref):
        j = pl.program_id(0)
        dyv = dy_ref[...]
        dycat_ref[...] = lax.dot_general(dyv, w_ref[...], _NT, preferred_element_type=F32)

        @pl.when(j < NCB)
        def _():
            dw_ref[...] = lax.dot_general(yl_ref[...], dyv, _TN, preferred_element_type=F32).astype(BF16)

        @pl.when(j >= NCB)
        def _():
            dw_ref[...] = lax.dot_general(yc_ref[...], dyv, _TN, preferred_element_type=F32).astype(BF16)

    return pl.pallas_call(
        body, name="outproj_bwd",
        grid=(2 * NCB,),
        in_specs=[pl.BlockSpec((TP, D), lambda j: (0, 0)),
                  pl.BlockSpec((TP, CB), lambda j: (0, jnp.minimum(j, NCB - 1))),
                  pl.BlockSpec((TP, CB), lambda j: (0, jnp.maximum(j - NCB, 0))),
                  pl.BlockSpec((CB, D), lambda j: (j, 0))],
        out_specs=[pl.BlockSpec((TP, CB), lambda j: (0, j)), pl.BlockSpec((CB, D), lambda j: (j, 0))],
        out_shape=[jax.ShapeDtypeStruct((TP, D), F32), jax.ShapeDtypeStruct((D, D), BF16)],
        compiler_params=_cparams(),
    )(dy, ylru, yconf, w_out)


def _conf_bwd_proj(dycat, p, z, vc, ln_w, ln_b, pw_w):
    def body(dy_ref, p_ref, g_ref, vc_ref, lw_ref, lb_ref, w_ref,
             dvc_ref, dgc_ref, dpw_ref, vecs_ref, dp_s, s_s, ds_s):
        i = pl.program_id(0)
        lw, lb = lw_ref[...], lb_ref[...]

        @pl.when(i == 0)
        def _():
            dpw_ref[...] = jnp.zeros_like(dpw_ref)
            vecs_ref[...] = jnp.zeros_like(vecs_ref)

        def pre_chunk(ci, carry):
            r0 = pl.multiple_of(ci * R, R)
            for half in range(2):
                rr = r0 + 8 * half
                dyv = dy_ref[pl.ds(rr, 8), :]
                g = g_ref[pl.ds(rr, 8), :]
                sg = _sig(g)
                dp = dyv * (g * sg)
                dg = dyv * p_ref[pl.ds(rr, 8), :] * (sg * (1.0 + g * (1.0 - sg)))
                vecs_ref[0:8, :] += dp
                vecs_ref[8:16, :] += dg
                ds_s[pl.ds(rr, 8), :] = dp
                dvc_ref[pl.ds(rr, 8), :] = dg
            dp_s[pl.ds(r0, R), :] = ds_s[pl.ds(r0, R), :].astype(BF16)
            dgc_ref[pl.ds(r0, R), :] = dvc_ref[pl.ds(r0, R), :].astype(BF16)
            for half in range(2):
                rr = r0 + 8 * half
                _, _, ln = _ln_chunk(vc_ref[pl.ds(rr, 8), :], lw, lb)
                ds_s[pl.ds(rr, 8), :] = ln * _sig(ln)
            s_s[pl.ds(r0, R), :] = ds_s[pl.ds(r0, R), :].astype(BF16)
            return carry
        lax.fori_loop(0, TM // R, pre_chunk, 0, unroll=2)

        dpb = dp_s[...]
        ds_s[...] = lax.dot_general(dpb, w_ref[...], _NT, preferred_element_type=F32)
        dpw_ref[...] += lax.dot_general(s_s[...], dpb, _TN, preferred_element_type=F32)

        def post_chunk(ci, carry):
            r0 = pl.multiple_of(ci * 8, 8)
            xhat, rstd, ln = _ln_chunk(vc_ref[pl.ds(r0, 8), :], lw, lb)
            sl = _sig(ln)
            dln = ds_s[pl.ds(r0, 8), :] * (sl * (1.0 + ln * (1.0 - sl)))
            vecs_ref[16:24, :] += dln * xhat
            vecs_ref[24:32, :] += dln
            dxh = dln * lw
            m1 = jnp.mean(dxh, axis=-1, keepdims=True)
            m2 = jnp.mean(dxh * xhat, axis=-1, keepdims=True)
            dvc_ref[pl.ds(r0, 8), :] = rstd * (dxh - m1 - xhat * m2)
            return carry
        lax.fori_loop(0, TM // 8, post_chunk, 0, unroll=4)

    row = pl.BlockSpec((TM, DC), lambda i: (i, 0))
    vec = pl.BlockSpec((1, DC), lambda i: (0, 0))
    return pl.pallas_call(
        body, name="conf_bwd_proj",
        grid=(TP // TM,),
        in_specs=[pl.BlockSpec((TM, DC), lambda i: (i, 1)), row, pl.BlockSpec((TM, DC), lambda i: (i, 4)), row,
                  vec, vec, pl.BlockSpec((DC, DC), lambda i: (0, 0))],
        out_specs=[row, row, pl.BlockSpec((DC, DC), lambda i: (0, 0)), pl.BlockSpec((32, DC), lambda i: (0, 0))],
        out_shape=[jax.ShapeDtypeStruct((TP, DC), F32), jax.ShapeDtypeStruct((TP, DC), BF16),
                   jax.ShapeDtypeStruct((DC, DC), F32), jax.ShapeDtypeStruct((32, DC), F32)],
        scratch_shapes=[pltpu.VMEM((TM, DC), BF16), pltpu.VMEM((TM, DC), BF16), pltpu.VMEM((TM, DC), F32)],
        compiler_params=_cparams(),
    )(dycat, p, z, vc, ln_w, ln_b, pw_w)


def _conf_bwd_conv(dvc, z, dw_w):
    def body(dvc_ref, u1_ref, u2_ref, w_ref, du_ref, dw_ref, vecs_ref, vs, dvs):
        vs[pl.ds(0, KWP), :] = jnp.zeros((KWP, CB), F32)
        dvs[pl.ds(TP, KWP), :] = jnp.zeros((KWP, CB), F32)
        dw_ref[...] = jnp.zeros_like(dw_ref)
        vecs_ref[...] = jnp.zeros_like(vecs_ref)

        def fill_chunk(ci, carry):
            r0 = pl.multiple_of(ci * R, R)
            vs[pl.ds(KWP + r0, R), :] = u1_ref[pl.ds(r0, R), :] * _sig(u2_ref[pl.ds(r0, R), :])
            dv = dvc_ref[pl.ds(r0, R), :]
            dvs[pl.ds(r0, R), :] = dv
            vecs_ref[0:8, :] += _fold8(dv)
            return carry
        lax.fori_loop(0, TP // R, fill_chunk, 0)

        def conv_chunk(ci, carry):
            r0 = pl.multiple_of(ci * R, R)
            vbuf = vs[pl.ds(r0, KWP + R), :]
            dbuf = dvs[pl.ds(r0, KWP + R), :]
            dcur = dbuf[0:R, :]
            dv = jnp.zeros((R, CB), F32)
            for rr in range(8):
                vroll = vbuf if rr == 0 else pltpu.roll(vbuf, rr, 0)
                droll = dbuf if rr == 0 else pltpu.roll(dbuf, KWP + R - rr, 0)
                for q in range(4):
                    s = 8 * q + rr
                    if s > KW - 1:
                        continue
                    k = KW - 1 - s
                    dv = dv + droll[8 * q:8 * q + R, :] * w_ref[k:k + 1, :]
                    dw_ref[8 * k:8 * k + 8, :] += _fold8(dcur * vroll[KWP - 8 * q:KWP - 8 * q + R, :])
            u1 = u1_ref[pl.ds(r0, R), :]
            sg = _sig(u2_ref[pl.ds(r0, R), :])
            du1 = dv * sg
            du2 = dv * u1 * (sg * (1.0 - sg))
            du_ref[0, pl.ds(r0, R), :] = du1.astype(BF16)
            du_ref[1, pl.ds(r0, R), :] = du2.astype(BF16)
            vecs_ref[8:16, :] += _fold8(du1)
            vecs_ref[16:24, :] += _fold8(du2)
            return carry
        lax.fori_loop(0, TP // R, conv_chunk, 0)

    blk = pl.BlockSpec((TP, CB), lambda j: (0, j))
    return pl.pallas_call(
        body, name="conf_bwd_conv",
        grid=(NCB,),
        in_specs=[blk, pl.BlockSpec((TP, CB), lambda j: (0, 2 * NCB + j)),
                  pl.BlockSpec((TP, CB), lambda j: (0, 3 * NCB + j)), pl.BlockSpec((KWP, CB), lambda j: (0, j))],
        out_specs=[pl.BlockSpec((2, TP, CB), lambda j: (0, 0, j)), pl.BlockSpec((8 * KWP, CB), lambda j: (0, j)),
                   pl.BlockSpec((24, CB), lambda j: (0, j))],
        out_shape=[jax.ShapeDtypeStruct((2, TP, DC), BF16),
                   jax.ShapeDtypeStruct((8 * KWP, DC), F32), jax.ShapeDtypeStruct((24, DC), F32)],
        scratch_shapes=[pltpu.VMEM((TP + KWP, CB), F32), pltpu.VMEM((TP + KWP, CB), F32)],
        compiler_params=_cparams(),
    )(dvc, z, z, dw_w)


def _lru_bwd(dycat, z, xc, hs, conv_w, wa_g, b_a, wx_g, b_x, lam):
    NV = 6

    def body(dy_ref, x_ref, g_ref, xc_ref, hs_ref, cw_ref, wa_ref, ba_ref, wx_ref, bx_ref, lam_ref,
             dzl_ref, dwa_ref, dwx_ref, dcw_ref, vecs_ref, ga_s, gx_s, dxc_s):
        vecs_ref[...] = jnp.zeros_like(vecs_ref)
        dcw_ref[...] = jnp.zeros_like(dcw_ref)
        dxc_s[pl.ds(TP, 8), :] = jnp.zeros((8, CB), F32)

        def gate_chunk(ci, carry):
            r0 = pl.multiple_of(ci * TM, TM)
            xb = xc_ref[pl.ds(r0, TM), :].astype(BF16)
            ga_s[pl.ds(r0, TM), :] = jnp.dot(xb, wa_ref[...], preferred_element_type=F32) + ba_ref[...]
            gx_s[pl.ds(r0, TM), :] = jnp.dot(xb, wx_ref[...], preferred_element_type=F32) + bx_ref[...]
            return carry
        lax.fori_loop(0, TP // TM, gate_chunk, 0)

        sp8 = LRU_C * _softplus(-lam_ref[...])
        row = _row_iota((R, CB))
        nchunk = TP // R

        def scan_chunk(cj, carry):
            a_next, lam_next = carry
            ci = nchunk - 1 - cj
            r0 = pl.multiple_of(ci * R, R)
            dyv = dy_ref[pl.ds(r0, R), :]
            g = g_ref[pl.ds(r0, R), :]
            hv = hs_ref[pl.ds(r0, R), :]
            xc = xc_ref[pl.ds(r0, R), :]
            sg = _sig(g)
            dgl = dyv * hv * (sg * (1.0 + g * (1.0 - sg)))
            dzl_ref[1, pl.ds(r0, R), :] = dgl.astype(BF16)
            vecs_ref[0:8, :] += _fold8(dgl)
            dhs = dyv * (g * sg)
            r, i, a, mult = _gate_values(ga_s[pl.ds(r0, R), :], gx_s[pl.ds(r0, R), :], xc, sp8)
            b = jnp.where(row == R - 1, a_next, pltpu.roll(a, R - 1, 0))
            lv = dhs
            k = 1
            while k < R:
                m = row < R - k
                lv = jnp.where(m, lv + b * pltpu.roll(lv, R - k, 0), lv)
                b = jnp.where(m, b * pltpu.roll(b, R - k, 0), b)
                k *= 2
            lv = lv + b * lam_next
            p0 = pl.multiple_of(jnp.maximum(r0 - 8, 0), 8)
            hprev8 = jnp.where(ci > 0, hs_ref[pl.ds(p0, 8), :], 0.0)
            hprev = pltpu.roll(jnp.concatenate([hprev8, hv], axis=0), 1, 0)[8:8 + R, :]
            da = lv * hprev
            ixc = i * xc
            dmult = lv * ixc
            di = lv * mult * xc
            dxc_s[pl.ds(r0, R), :] = lv * mult * i
            a2 = a * a
            dlog_a = da * a - dmult * a2 / mult
            vecs_ref[32:40, :] += _fold8(dlog_a * r)
            dga = -(dlog_a * sp8) * r * (1.0 - r)
            dgx = di * i * (1.0 - i)
            ga_s[pl.ds(r0, R), :] = dga
            gx_s[pl.ds(r0, R), :] = dgx
            vecs_ref[16:24, :] += _fold8(dga)
            vecs_ref[24:32, :] += _fold8(dgx)
            a_first = jnp.sum(jnp.where(row == 0, a, 0.0), axis=0, keepdims=True)
            l_first = jnp.sum(jnp.where(row == 0, lv, 0.0), axis=0, keepdims=True)
            return a_first, l_first
        lax.fori_loop(0, nchunk, scan_chunk, (jnp.zeros((1, CB), F32), jnp.zeros((1, CB), F32)))

        dwa_ref[...] = jnp.zeros_like(dwa_ref)
        dwx_ref[...] = jnp.zeros_like(dwx_ref)

        def mm_chunk(ci, carry):
            r0 = pl.multiple_of(ci * TM, TM)
            xb = xc_ref[pl.ds(r0, TM), :].astype(BF16)
            dgab = ga_s[pl.ds(r0, TM), :].astype(BF16)
            dgxb = gx_s[pl.ds(r0, TM), :].astype(BF16)
            dxc_s[pl.ds(r0, TM), :] += (lax.dot_general(dgab, wa_ref[...], _NT, preferred_element_type=F32)
                                        + lax.dot_general(dgxb, wx_ref[...], _NT, preferred_element_type=F32))
            dwa_ref[...] += lax.dot_general(xb, dgab, _TN, preferred_element_type=F32)
            dwx_ref[...] += lax.dot_general(xb, dgxb, _TN, preferred_element_type=F32)
            return carry
        lax.fori_loop(0, TP // TM, mm_chunk, 0)

        taps = [cw_ref[k:k + 1, :] for k in range(LW)]

        def conv_chunk(ci, carry):
            r0 = pl.multiple_of(ci * R, R)
            dbuf = dxc_s[pl.ds(r0, R + 8), :]
            dcur = dbuf[0:R, :]
            p0 = pl.multiple_of(jnp.maximum(r0 - 8, 0), 8)
            xprev = jnp.where(ci > 0, x_ref[pl.ds(p0, 8), :], 0.0)
            xbuf = jnp.concatenate([xprev, x_ref[pl.ds(r0, R), :]], axis=0)
            dxl = dcur * taps[LW - 1]
            dcw_ref[8 * (LW - 1):8 * LW, :] += _fold8(dcur * xbuf[8:8 + R, :])
            for s in range(1, LW):
                k = LW - 1 - s
                dxl = dxl + pltpu.roll(dbuf, R + 8 - s, 0)[0:R, :] * taps[k]
                dcw_ref[8 * k:8 * k + 8, :] += _fold8(dcur * pltpu.roll(xbuf, s, 0)[8:8 + R, :])
            dzl_ref[0, pl.ds(r0, R), :] = dxl.astype(BF16)
            vecs_ref[8:16, :] += _fold8(dxl)
            vecs_ref[40:48, :] += _fold8(dcur)
            return carry
        lax.fori_loop(0, TP // R, conv_chunk, 0)
        vecs_ref[32:40, :] = vecs_ref[32:40, :] * (LRU_C * _sig(-lam_ref[...]))

    col = lambda off: pl.BlockSpec((TP, CB), lambda j: (0, off + j))
    vec = pl.BlockSpec((1, CB), lambda j: (0, j))
    wsp = pl.BlockSpec((None, CB, CB), lambda j: (j, 0, 0))
    return pl.pallas_call(
        body, name="lru_bwd",
        grid=(NCB,),
        in_specs=[col(0), col(0), col(NCB), col(0), col(0), pl.BlockSpec((LW, CB), lambda j: (0, j)),
                  wsp, vec, wsp, vec, vec],
        out_specs=[pl.BlockSpec((2, TP, CB), lambda j: (0, 0, j)), wsp, wsp,
                   pl.BlockSpec((8 * LW, CB), lambda j: (0, j)), pl.BlockSpec((8 * NV, CB), lambda j: (0, j))],
        out_shape=[jax.ShapeDtypeStruct((2, TP, DL), BF16),
                   jax.ShapeDtypeStruct((NCB, CB, CB), F32), jax.ShapeDtypeStruct((NCB, CB, CB), F32),
                   jax.ShapeDtypeStruct((8 * LW, DL), F32), jax.ShapeDtypeStruct((8 * NV, DL), F32)],
        scratch_shapes=[pltpu.VMEM((TP, CB), F32), pltpu.VMEM((TP, CB), F32), pltpu.VMEM((TP + 8, CB), F32)],
        compiler_params=_cparams(),
    )(dycat, z, z, xc, hs, conv_w, wa_g, b_a, wx_g, b_x, lam)


def _dz_section(sec, dzl_ref, dzc_ref, dgc_ref, use):
    @pl.when(sec < 2)
    def _():
        use(dzl_ref)

    @pl.when(jnp.logical_and(sec >= 2, sec < 4))
    def _():
        use(dzc_ref)

    @pl.when(sec == 4)
    def _():
        use(dgc_ref)


def _dz_specs(rows, index):
    return [pl.BlockSpec((None, rows, 1024), lambda a, b: (jnp.minimum(index(a, b)[1], 1), index(a, b)[0], 0)),
            pl.BlockSpec((None, rows, 1024), lambda a, b: (jnp.clip(index(a, b)[1] - 2, 0, 1), index(a, b)[0], 0)),
            pl.BlockSpec((rows, 1024), lambda a, b: (index(a, b)[0], 0))]


def _inproj_wgrad(name, hn, dzs):
    KB = 512
    nsec = dzs.shape[0]

    def body(hn_ref, dz_ref, dw_ref):
        dw_ref[...] = lax.dot_general(hn_ref[...], dz_ref[...], _TN, preferred_element_type=F32).astype(BF16)

    return pl.pallas_call(
        body, name=name,
        grid=(nsec, D // KB),
        in_specs=[pl.BlockSpec((TP, KB), lambda n, kb: (0, kb)),
                  pl.BlockSpec((None, TP, 1024), lambda n, kb: (n, 0, 0))],
        out_specs=pl.BlockSpec((KB, 1024), lambda n, kb: (kb, n)),
        out_shape=jax.ShapeDtypeStruct((D, nsec * 1024), BF16),
        compiler_params=_cparams(),
    )(hn, dzs)


def _sum_win_parts(parts_a, parts_b, parts_c):
    RB = 64

    def body(a_ref, b_ref, c_ref, o_ref):
        def chunk(ci, carry):
            r0 = pl.multiple_of(ci * R, R)
            for ref, base, ncol in ((a_ref, 0, 2048), (b_ref, 2048, 2048), (c_ref, 4096, 1024)):
                for c0 in range(0, ncol, 512):
                    acc = ref[0, pl.ds(r0, R), c0:c0 + 512].astype(F32)
                    for sidx in range(1, NDEV):
                        acc = acc + ref[sidx, pl.ds(r0, R), c0:c0 + 512].astype(F32)
                    o_ref[pl.ds(r0, R), base + c0:base + c0 + 512] = acc.astype(BF16)
            return carry
        lax.fori_loop(0, RB // R, chunk, 0)

    spec = lambda ncol: pl.BlockSpec((NDEV, RB, ncol), lambda i: (0, i, 0))
    return pl.pallas_call(
        body, name="sum_win_parts",
        grid=(D // NDEV // RB,),
        in_specs=[spec(2048), spec(2048), spec(1024)],
        out_specs=pl.BlockSpec((RB, NIN), lambda i: (i, 0)),
        out_shape=jax.ShapeDtypeStruct((D // NDEV, NIN), BF16),
        compiler_params=_cparams(),
    )(parts_a, parts_b, parts_c)


def _inproj_bwd(dzl, dzc, dgc, w_in, h, dout, pre_w):
    nsec = NIN // 1024

    def body(dzl_ref, dzc_ref, dgc_ref, w_ref, h_ref, dout_ref, pw_ref, gx_hbm, dmeta_ref, dpw_ref, acc_s, dh_s, sem):
        i = pl.program_id(0)
        s = pl.program_id(1)

        def gx_copy(t):
            lo, n, off = _tile_rows(t)
            return pltpu.make_async_copy(dh_s.at[pl.ds(off, n)], gx_hbm.at[pl.ds(lo, n)], sem)

        @pl.when(s == 0)
        def _():
            acc_s[...] = jnp.zeros_like(acc_s)

        def use(dz_ref):
            acc_s[...] += lax.dot_general(dz_ref[...], w_ref[...], _NT, preferred_element_type=F32)
        _dz_section(s, dzl_ref, dzc_ref, dgc_ref, use)

        @pl.when(jnp.logical_and(i == 0, s == nsec - 1))
        def _():
            dpw_ref[...] = jnp.zeros_like(dpw_ref)

        @pl.when(s == nsec - 1)
        def _():
            _for_tile(i - 1, lambda t: gx_copy(t).wait())
            pw = pw_ref[...]

            def chunk(ci, carry):
                r0 = pl.multiple_of(ci * 8, 8)
                hv = h_ref[pl.ds(r0, 8), :]
                dhn = acc_s[pl.ds(r0, 8), :]
                rs = lax.rsqrt(jnp.mean(hv * hv, axis=-1, keepdims=True) + EPS)
                dpw_ref[...] += dhn * (hv * rs)
                gw = dhn * pw
                dot = jnp.mean(gw * hv, axis=-1, keepdims=True)
                dh_s[pl.ds(r0, 8), :] = rs * gw - hv * (rs * rs * rs * dot) + dout_ref[pl.ds(r0, 8), :]
                return carry
            lax.fori_loop(0, TM // 8, chunk, 0, unroll=4)
            _for_tile(i, lambda t: gx_copy(t).start())

            @pl.when(i == 0)
            def _():
                dmeta_ref[...] = dh_s[0:NMETA, :]

            @pl.when(i == NTILE - 1)
            def _():
                gx_copy(NTILE - 1).wait()

    row = pl.BlockSpec((TM, D), lambda i, s: (i, 0))
    return pl.pallas_call(
        body, name="inproj_bwd",
        grid=(TP // TM, nsec),
        in_specs=_dz_specs(TM, lambda i, s: (i, s)) + [
            pl.BlockSpec((D, 1024), lambda i, s: (0, s)), row, row, pl.BlockSpec((1, D), lambda i, s: (0, 0))],
        out_specs=[pl.BlockSpec(memory_space=pl.ANY), pl.BlockSpec((NMETA, D), lambda i, s: (0, 0)),
                   pl.BlockSpec((8, D), lambda i, s: (0, 0))],
        out_shape=[jax.ShapeDtypeStruct((SEQ, D), F32), jax.ShapeDtypeStruct((NMETA, D), F32),
                   jax.ShapeDtypeStruct((8, D), F32)],
        scratch_shapes=[pltpu.VMEM((TM, D), F32), pltpu.VMEM((TM, D), F32), pltpu.SemaphoreType.DMA(())],
        compiler_params=_cparams(),
    )(dzl, dzc, dgc, w_in, h, dout, pre_w)


def _adamw(name, parts, w, m, v, block_rows):
    rows, cols = w.shape
    nparts = parts.shape[0]
    cw = cols if cols <= 640 else 512

    def body(p_ref, w_ref, m_ref, v_ref, g_ref, d_ref, nm_ref, nv_ref):
        def chunk(ci, carry):
            r0 = pl.multiple_of(ci * R, R)
            for c0 in range(0, cols, cw):
                at = (pl.ds(r0, R), slice(c0, c0 + cw))
                g = p_ref[(0,) + at].astype(F32)
                for sidx in range(1, nparts):
                    g = g + p_ref[(sidx,) + at].astype(F32)
                delta, mv, vv = _adam_math(g, w_ref[at], m_ref[at], v_ref[at])
                g_ref[at] = g
                nm_ref[at] = mv
                nv_ref[at] = vv
                d_ref[at] = delta
            return carry
        lax.fori_loop(0, block_rows // R, chunk, 0)

    blk = pl.BlockSpec((block_rows, cols), lambda i: (i, 0))
    shp = jax.ShapeDtypeStruct((rows, cols), F32)
    return pl.pallas_call(
        body, name=name,
        grid=(rows // block_rows,),
        in_specs=[pl.BlockSpec((nparts, block_rows, cols), lambda i: (0, i, 0)), blk, blk, blk],
        out_specs=[blk, blk, blk, blk],
        out_shape=[shp, shp, shp, shp],
        compiler_params=_cparams(),
    )(parts, w, m, v)


def _adam_math(g, w, m, v):
    c1 = 1.0 / (1.0 - ADAM_B1 ** ADAM_STEP)
    c2 = 1.0 / (1.0 - ADAM_B2 ** ADAM_STEP)
    mv = ADAM_B1 * m + (1.0 - ADAM_B1) * g
    vv = ADAM_B2 * v + (1.0 - ADAM_B2) * (g * g)
    upd = (mv * c1) / (jnp.sqrt(vv * c2) + ADAM_EPS) + ADAM_WD * w
    return -ADAM_LR * upd, mv, vv


_VEC = [("pre_norm_w", 2), ("post_norm_w", 2), ("b_in", 5), ("lru_conv_b", 1), ("b_gate_a", 1), ("b_gate_x", 1),
        ("lru_lambda", 1), ("conf_dw_b", 1), ("conf_ln_w", 1), ("conf_ln_b", 1), ("conf_pw_b", 1)]
_VEC_ROWS = 24
_LOSS_ROW = 17
_SM_ROWS = 64


def _pack_grads(dprew_acc, dpostw_acc, cvecs, kvecs, lvecs, dcw_acc, ddw_acc, dh, loss_acc):
    def body(pre_ref, post_ref, c_ref, k_ref, l_ref, dcw_ref, ddw_ref, dh_ref, loss_ref, vec_ref, small_ref, tmp):
        s8 = lambda ref, r: jnp.sum(ref[8 * r:8 * r + 8, :], axis=0, keepdims=True)
        vec_ref[...] = jnp.zeros_like(vec_ref)
        pre, post = s8(pre_ref, 0), s8(post_ref, 0)
        rows = [pre[:, 0:1024], pre[:, 1024:2048], post[:, 0:1024], post[:, 1024:2048],
                s8(l_ref, 1), s8(l_ref, 0), s8(k_ref, 1), s8(k_ref, 2), s8(c_ref, 1),
                s8(l_ref, 5), s8(l_ref, 2), s8(l_ref, 3), s8(l_ref, 4),
                s8(k_ref, 0), s8(c_ref, 2), s8(c_ref, 3), s8(c_ref, 0)]
        for r, val in enumerate(rows):
            vec_ref[r:r + 1, :] = val
        vec_ref[_LOSS_ROW:_LOSS_ROW + 1, :] = jnp.zeros((1, 1024), F32) + (0.5 / D) * jnp.sum(loss_ref[...])

        small_ref[...] = jnp.zeros_like(small_ref)
        for k in range(LW):
            tmp[k:k + 1, :] = s8(dcw_ref, k)
        for k in range(KW):
            tmp[8 + k:9 + k, :] = s8(ddw_ref, k)
        for d in range(NDEV):
            small_ref[d, 0:LW, 0:128] = tmp[0:LW, 128 * d:128 * d + 128]
            small_ref[d, 8:8 + KW, 0:128] = tmp[8:8 + KW, 128 * d:128 * d + 128]
            small_ref[d, 40:56, :] = dh_ref[:, 256 * d:256 * d + 256]

    full = lambda a: pl.BlockSpec(a.shape, lambda i: (0,) * a.ndim)
    ins = [dprew_acc, dpostw_acc, cvecs, kvecs, lvecs, dcw_acc, ddw_acc]
    return pl.pallas_call(
        body, name="pack_grads",
        grid=(1,),
        in_specs=[full(a) for a in ins] + [full(dh), full(loss_acc)],
        out_specs=[pl.BlockSpec((_VEC_ROWS, 1024), lambda i: (0, 0)),
                   pl.BlockSpec((NDEV, _SM_ROWS, 256), lambda i: (0, 0, 0))],
        out_shape=[jax.ShapeDtypeStruct((_VEC_ROWS, 1024), F32), jax.ShapeDtypeStruct((NDEV, _SM_ROWS, 256), F32)],
        scratch_shapes=[pltpu.VMEM((40, 1024), F32)],
        compiler_params=_cparams(),
    )(*ins, dh, loss_acc)


def _adamw_vec(parts, W, M, V):
    nv = len(_VEC)

    def body(*refs):
        p_ref = refs[0]
        w_refs, m_refs, v_refs = refs[1:1 + nv], refs[1 + nv:1 + 2 * nv], refs[1 + 2 * nv:1 + 3 * nv]
        outs = refs[1 + 3 * nv:]

        def total(r):
            acc = p_ref[0, r:r + 1, :]
            for sidx in range(1, NDEV):
                acc = acc + p_ref[sidx, r:r + 1, :]
            return acc

        row = 0
        for idx, (_, nrows) in enumerate(_VEC):
            for part in range(nrows):
                cols = slice(1024 * part, 1024 * part + 1024)
                g = total(row + part)
                delta, mv, vv = _adam_math(g, w_refs[idx][:, cols], m_refs[idx][:, cols], v_refs[idx][:, cols])
                for o, val in zip(outs[4 * idx:4 * idx + 4], (g, delta, mv, vv)):
                    o[:, cols] = val
            row += nrows
        outs[-1][...] = total(_LOSS_ROW)[:, 0:128]

    names = [n for n, _ in _VEC]
    flat = lambda d: [d[n].reshape(1, -1) for n in names]
    ws, ms, vs = flat(W), flat(M), flat(V)
    res = pl.pallas_call(
        body, name="adamw_vec",
        out_shape=[jax.ShapeDtypeStruct(w.shape, F32) for w in ws for _ in range(4)]
        + [jax.ShapeDtypeStruct((1, 128), F32)],
        compiler_params=_cparams(),
    )(parts, *ws, *ms, *vs)
    return {n: tuple(res[4 * i:4 * i + 4]) for i, n in enumerate(names)}, res[-1]


def _adamw_small(parts, W, M, V):
    where = {"lru_conv_w": (slice(0, LW), slice(0, 128)), "conf_dw_w": (slice(8, 8 + KW), slice(0, 128)),
             "meta_tokens": (slice(40, 56), slice(0, 256))}
    names = list(where)

    def body(*refs):
        p_ref = refs[0]
        outs = refs[10:]
        for idx, n in enumerate(names):
            rs, cs = where[n]
            g = p_ref[0, rs, cs]
            for sidx in range(1, NDEV):
                g = g + p_ref[sidx, rs, cs]
            delta, mv, vv = _adam_math(g, refs[1 + idx][...], refs[4 + idx][...], refs[7 + idx][...])
            for o, val in zip(outs[4 * idx:4 * idx + 4], (g, delta, mv, vv)):
                o[...] = val

    two_d = lambda a: a.reshape(a.shape[-2:])
    ws, ms, vs = ([two_d(d[n]) for n in names] for d in (W, M, V))
    res = pl.pallas_call(
        body, name="adamw_small",
        out_shape=[jax.ShapeDtypeStruct(w.shape, F32) for w in ws for _ in range(4)],
        compiler_params=_cparams(),
    )(parts, *ws, *ms, *vs)
    return {n: tuple(res[4 * i:4 * i + 4]) for i, n in enumerate(names)}


def _pack_small(lru_cw, dw_w, meta):
    buf = jnp.zeros((_SM_ROWS, 256), F32)
    buf = buf.at[0:LW, 0:128].set(lru_cw)
    buf = buf.at[8:8 + dw_w.shape[0], 0:128].set(dw_w)
    return buf.at[40:56, :].set(meta)


def _block_diag4(w):
    w4 = w.reshape(NCB, 4, 64, 64)
    eye = jnp.eye(4, dtype=w.dtype)
    return jnp.einsum("ghij,hk->ghikj", w4, eye).reshape(NCB, CB, CB)


def _diag_blocks(g):
    g5 = g.reshape(NCB, 4, 64, 4, 64)
    return jnp.stack([g5[:, hh, :, hh, :] for hh in range(4)], axis=1).reshape(16, 64, 64)


def _local_step(x, target, meta_full, inproj, out_weights, lru_cw_full, dw_w_full, W, send):
    wa_g = _block_diag4(W["w_gate_a"][0]).astype(BF16)
    wx_g = _block_diag4(W["w_gate_x"][0]).astype(BF16)

    h, hn = _prenorm(x, meta_full, W["pre_norm_w"])
    z, win_full = inproj(hn)
    ylru, xc, hs = _lru_fwd(z, lru_cw_full, W["lru_conv_b"], wa_g, W["b_gate_a"], wx_g, W["b_gate_x"],
                            W["lru_lambda"])
    vc = _conf_fwd_conv(z, dw_w_full, W["conf_dw_b"])
    wout_full, pw_full = out_weights(vc)
    yconf, p = _conf_fwd_proj(vc, z, W["conf_ln_w"], W["conf_ln_b"], pw_full, W["conf_pw_b"])
    dout, dy, loss_acc, dpostw_acc = _outproj_loss(ylru, yconf, wout_full, h, target, W["post_norm_w"])

    dycat, dwout_part = _outproj_bwd(dy, ylru, yconf, wout_full)
    tok = send("w_out", dwout_part)
    dvc, dgc, dpw_part, cvecs = _conf_bwd_proj(dycat, p, z, vc, W["conf_ln_w"] + tok, W["conf_ln_b"], pw_full)
    tok = send("conf_pw_w", dpw_part)
    tok = tok + send("w_in_c", _inproj_wgrad("inproj_wgrad_c", hn, dgc[None]))
    dzc, ddw_acc, kvecs = _conf_bwd_conv(dvc, z, dw_w_full + tok)
    tok = send("w_in_b", _inproj_wgrad("inproj_wgrad_b", hn, dzc))
    dzl, dwa_g, dwx_g, dcw_acc, lvecs = _lru_bwd(dycat, z, xc, hs, lru_cw_full, wa_g, W["b_gate_a"] + tok, wx_g,
                                                 W["b_gate_x"], W["lru_lambda"])
    tok = send("w_in_a", _inproj_wgrad("inproj_wgrad_a", hn, dzl))
    tok = tok + send("w_gates", _diag_blocks(dwa_g).reshape(16 * 64, 64), _diag_blocks(dwx_g).reshape(16 * 64, 64))
    grad_x, dmeta, dprew_acc = _inproj_bwd(dzl, dzc, dgc, win_full, h, dout, W["pre_norm_w"] + tok)

    vec_pack, small_part = _pack_grads(dprew_acc, dpostw_acc, cvecs, kvecs, lvecs, dcw_acc, ddw_acc, dmeta, loss_acc)
    return grad_x, vec_pack, small_part


def kernel(x, meta_tokens, pre_norm_w, post_norm_w, w_in, b_in, lru_conv_w, lru_conv_b, w_gate_a, b_gate_a, w_gate_x, b_gate_x, lru_lambda, conf_dw_w, conf_dw_b, conf_ln_w, conf_ln_b, conf_pw_w, conf_pw_b, w_out, loss_target, m_meta_tokens, m_pre_norm_w, m_post_norm_w, m_w_in, m_b_in, m_lru_conv_w, m_lru_conv_b, m_w_gate_a, m_b_gate_a, m_w_gate_x, m_b_gate_x, m_lru_lambda, m_conf_dw_w, m_conf_dw_b, m_conf_ln_w, m_conf_ln_b, m_conf_pw_w, m_conf_pw_b, m_w_out, v_meta_tokens, v_pre_norm_w, v_post_norm_w, v_w_in, v_b_in, v_lru_conv_w, v_lru_conv_b, v_w_gate_a, v_b_gate_a, v_w_gate_x, v_b_gate_x, v_lru_lambda, v_conf_dw_w, v_conf_dw_b, v_conf_ln_w, v_conf_ln_b, v_conf_pw_w, v_conf_pw_b, v_w_out):
    W = dict(meta_tokens=meta_tokens, pre_norm_w=pre_norm_w, post_norm_w=post_norm_w, w_in=w_in, b_in=b_in,
             lru_conv_w=lru_conv_w, lru_conv_b=lru_conv_b, w_gate_a=w_gate_a, b_gate_a=b_gate_a,
             w_gate_x=w_gate_x, b_gate_x=b_gate_x, lru_lambda=lru_lambda, conf_dw_w=conf_dw_w,
             conf_dw_b=conf_dw_b, conf_ln_w=conf_ln_w, conf_ln_b=conf_ln_b, conf_pw_w=conf_pw_w,
             conf_pw_b=conf_pw_b, w_out=w_out)
    M = dict(meta_tokens=m_meta_tokens, pre_norm_w=m_pre_norm_w, post_norm_w=m_post_norm_w, w_in=m_w_in,
             b_in=m_b_in, lru_conv_w=m_lru_conv_w, lru_conv_b=m_lru_conv_b, w_gate_a=m_w_gate_a,
             b_gate_a=m_b_gate_a, w_gate_x=m_w_gate_x, b_gate_x=m_b_gate_x, lru_lambda=m_lru_lambda,
             conf_dw_w=m_conf_dw_w, conf_dw_b=m_conf_dw_b, conf_ln_w=m_conf_ln_w, conf_ln_b=m_conf_ln_b,
             conf_pw_w=m_conf_pw_w, conf_pw_b=m_conf_pw_b, w_out=m_w_out)
    V = dict(meta_tokens=v_meta_tokens, pre_norm_w=v_pre_norm_w, post_norm_w=v_post_norm_w, w_in=v_w_in,
             b_in=v_b_in, lru_conv_w=v_lru_conv_w, lru_conv_b=v_lru_conv_b, w_gate_a=v_w_gate_a,
             b_gate_a=v_b_gate_a, w_gate_x=v_w_gate_x, b_gate_x=v_b_gate_x, lru_lambda=v_lru_lambda,
             conf_dw_w=v_conf_dw_w, conf_dw_b=v_conf_dw_b, conf_ln_w=v_conf_ln_w, conf_ln_b=v_conf_ln_b,
             conf_pw_w=v_conf_pw_w, conf_pw_b=v_conf_pw_b, w_out=v_w_out)
    names = list(W.keys())
    shapes = {n: W[n].shape for n in names}

    small = _pack_small(lru_conv_w[0], conf_dw_w[0], meta_tokens)
    win_flight, tok = _win_gather_start(w_in[0].astype(BF16))
    (small_flight,), tok = _exchange_start("gather_small_start", [
        (small + tok[0, 0], jax.ShapeDtypeStruct((NDEV, _SM_ROWS, 256), F32), _whole, _slot)])
    win_flight, tok = _win_gather_links(win_flight, tok)
    gathered, tok = _exchange_start("gather_out_start", [
        (w_out[0].astype(BF16) + tok[0, 0].astype(BF16), jax.ShapeDtypeStruct((D, D), BF16), _whole,
         _rows(D // NDEV)),
        (conf_pw_w[0].astype(BF16), jax.ShapeDtypeStruct((DC, DC), BF16), _whole, _rows(DC // NDEV)),
    ])
    (small_all,) = _exchange_wait("gather_small_wait", [small_flight], tok)
    unshard = lambda a: jnp.transpose(a, (1, 0, 2)).reshape(a.shape[1], -1)
    lru_cw_full = unshard(small_all[:, 0:LW, 0:128])
    dw_w_full = unshard(small_all[:, 8:8 + KWP, 0:128])
    meta_full = unshard(small_all[:, 40:56, :])

    def out_weights(after):
        return _exchange_wait("gather_out_wait", gathered, after)

    def inproj(hn):
        xi, yi, ci = lax.axis_index("x"), lax.axis_index("y"), lax.axis_index("c")
        shard = lambda px, py, pc: (4 * px + 2 * py + pc).astype(jnp.int32)
        here = jnp.stack([shard(xi, yi, ci), shard(xi, yi, 1 - ci)])
        over_links = jnp.stack([shard(1 - xi, yi, ci), shard(xi, 1 - yi, ci), shard(1 - xi, 1 - yi, ci)])
        flight = _win_gather_early(win_flight)
        z, land = _inproj_cols("inproj_here", here, hn, flight["land"], b_in, None)
        flight = _win_gather_forward(dict(flight, land=land), z)
        z, land = _inproj_cols("inproj_links", over_links, hn, flight["land"], b_in, z)
        land = _win_gather_wait(dict(flight, land=land))
        return _inproj_cols("inproj_sibling", over_links + 1 - 2 * ci, hn, land, b_in, z)

    row_stage = lambda ncol: (jax.ShapeDtypeStruct((NDEV, D // NDEV, ncol), BF16), _rows(D // NDEV))
    piece = {"w_in_a": row_stage(2048), "w_in_b": row_stage(2048), "w_in_c": row_stage(1024),
             "w_out": row_stage(D),
             "conf_pw_w": (jax.ShapeDtypeStruct((NDEV, DC // NDEV, DC), BF16), _rows(DC // NDEV)),
             "w_gates": (jax.ShapeDtypeStruct((NDEV, 16 * 64, 64), BF16), _whole)}
    sent = {}

    def send(name, *parts):
        handles, token = _exchange_start(
            "scatter_" + name + "_start",
            [(part.astype(BF16), piece[name][0], piece[name][1], _slot) for part in parts])
        sent[name] = handles
        return token[0, 0]

    grad_x, vec_pack, small_part = _local_step(
        x[0], loss_target[0], meta_full, inproj, out_weights, lru_cw_full, dw_w_full, W, send)
    grad_x = grad_x[None]

    (parts_c,) = _exchange_wait("scatter_w_in_c_wait", sent["w_in_c"], vec_pack)
    (parts_b,) = _exchange_wait("scatter_w_in_b_wait", sent["w_in_b"], parts_c)
    (parts_a,) = _exchange_wait("scatter_w_in_a_wait", sent["w_in_a"], parts_b)
    win_rows = _sum_win_parts(parts_a, parts_b, parts_c)
    win_stage2, tok = _exchange_start("scatter_w_in_stage2_start", [
        (win_rows, jax.ShapeDtypeStruct((NDEV, D // NDEV, NIN // NDEV), BF16), _cols(NIN // NDEV), _slot)])
    rest, _ = _exchange_start("scatter_rest_start", [
        (small_part, jax.ShapeDtypeStruct((NDEV, _SM_ROWS, 256), F32), _slot, _slot),
        (vec_pack + tok[0, 0], jax.ShapeDtypeStruct((NDEV, _VEC_ROWS, 1024), F32), _whole, _slot),
    ])

    G, DW, NM, NV = {}, {}, {}, {}
    (wout_parts,) = _exchange_wait("scatter_w_out_wait", sent["w_out"], win_rows)
    G["w_out"], DW["w_out"], NM["w_out"], NV["w_out"] = _adamw("adamw_w_out", wout_parts, w_out[0], m_w_out[0], v_w_out[0], 64)
    (pw_parts,) = _exchange_wait("scatter_conf_pw_w_wait", sent["conf_pw_w"], G["w_out"])
    G["conf_pw_w"], DW["conf_pw_w"], NM["conf_pw_w"], NV["conf_pw_w"] = _adamw(
        "adamw_pw", pw_parts, conf_pw_w[0], m_conf_pw_w[0], v_conf_pw_w[0], 128)
    res = {}
    wa_parts, wx_parts = _exchange_wait("scatter_w_gates_wait", sent["w_gates"], G["conf_pw_w"])
    for n, parts in (("w_gate_a", wa_parts), ("w_gate_x", wx_parts)):
        res[n] = _adamw("adamw_" + n, parts, *[d[n].reshape(16 * 64, 64) for d in (W, M, V)], 16 * 64)
    small_parts, vec_parts = _exchange_wait("scatter_rest_wait", rest, res["w_gate_x"][0])
    res.update(_adamw_small(small_parts, W, M, V))
    vec_res, loss_row = _adamw_vec(vec_parts, W, M, V)
    res.update(vec_res)
    (win_sum,) = _exchange_wait("scatter_w_in_stage2_wait", win_stage2, loss_row)
    res["w_in"] = _adamw("adamw_w_in", win_sum.reshape(1, D, NIN // NDEV), w_in[0], m_w_in[0], v_w_in[0], 256)
    for n, vals in res.items():
        for dst, val in zip((G, DW, NM, NV), vals):
            dst[n] = val
    for dst in (G, DW, NM, NV):
        for n in names:
            dst[n] = dst[n].reshape(shapes[n])
    loss = loss_row[0, 0]

    return (loss, grad_x, *[G[n] for n in names], *[DW[n] for n in names],
            *[NM[n] for n in names], *[NV[n] for n in names])
```

```python
import functools

import jax
import jax.numpy as jnp
from jax import lax
from jax.experimental import pallas as pl
from jax.experimental.pallas import tpu as pltpu

F32 = jnp.float32
BF16 = jnp.bfloat16

D = 2048
DL = 1024
DC = 1024
NIN = 5120
NMETA = 16
SEQ = 2048
T = NMETA + SEQ
TP = 2176
TM = 544
CB = 256
NCB = DL // CB
R = 16
KW = 31
KWP = 32
LW = 4
LRU_C = 8.0
EPS = 1e-6
NDEV = 8

ADAM_LR = 0.001
ADAM_B1 = 0.9
ADAM_B2 = 0.999
ADAM_EPS = 1e-08
ADAM_WD = 0.01
ADAM_STEP = 10

VMEM_LIMIT = 56 * 1024 * 1024


def _cparams():
    return pltpu.CompilerParams(vmem_limit_bytes=VMEM_LIMIT)


def _sig(x):
    return 1.0 / (1.0 + jnp.exp(-x))


def _expm1_neg(y):
    poly = y * (1.0 + y * (0.5 + y * (1.0 / 6.0 + y * (1.0 / 24.0 + y * (1.0 / 120.0)))))
    return jnp.where(y > -0.1, poly, jnp.exp(y) - 1.0)


def _softplus(x):
    e = jnp.exp(-jnp.abs(x))
    w = 1.0 + e
    l1p = jnp.where(w == 1.0, e, jnp.log(w) * e / (w - 1.0))
    return jnp.maximum(x, 0.0) + l1p


def _row_iota(shape):
    return lax.broadcasted_iota(jnp.int32, shape, 0)


def _fold8(v):
    return v[0:8, :] + v[8:16, :]


_FLIPS = [(k >> 2 & 1, k >> 1 & 1, k & 1) for k in range(1, NDEV)]
_HBM = pl.BlockSpec(memory_space=pltpu.HBM)
_SEM = pl.BlockSpec(memory_space=pltpu.SEMAPHORE)


def _peers():
    x, y, c = lax.axis_index("x"), lax.axis_index("y"), lax.axis_index("c")
    out = []
    for dx, dy, dc in _FLIPS:
        px = 1 - x if dx else x
        py = 1 - y if dy else y
        pc = 1 - c if dc else c
        out.append(((px, py, pc), 4 * px + 2 * py + pc))
    return 4 * x + 2 * y + c, out


def _exchange_start(name, items):
    n = len(items)

    def body(*refs):
        srcs, lands = refs[:n], refs[n:2 * n]
        outs = refs[2 * n:]
        send_sems, recv_sems, local_sems = outs[:n], outs[n:2 * n], outs[2 * n:3 * n]
        token = outs[-1]
        me, peers = _peers()
        for a in range(n):
            src_at, dst_at = items[a][2], items[a][3]
            pltpu.make_async_copy(src_at(srcs[a], me), dst_at(lands[a], me), local_sems[a]).start()
        for a in range(n):
            src_at, dst_at = items[a][2], items[a][3]
            for k, (pos, peer) in enumerate(peers):
                pltpu.make_async_remote_copy(
                    src_ref=src_at(srcs[a], peer), dst_ref=dst_at(lands[a], me),
                    send_sem=send_sems[a].at[k], recv_sem=recv_sems[a].at[k],
                    device_id=pos, device_id_type=pl.DeviceIdType.MESH).start()
        token[...] = jnp.zeros_like(token)

    srcs = [pltpu.with_memory_space_constraint(it[0], pltpu.HBM) for it in items]
    lands = [pltpu.with_memory_space_constraint(lax.empty(it[1].shape, it[1].dtype), pltpu.HBM) for it in items]
    sem7 = pltpu.SemaphoreType.DMA((NDEV - 1,))
    res = pl.pallas_call(
        body, name=name,
        out_shape=([sem7] * (2 * n) + [pltpu.SemaphoreType.DMA(())] * n
                   + [pltpu.HBM(a.shape, a.dtype) for a in srcs] + [pltpu.HBM(a.shape, a.dtype) for a in lands]
                   + [jax.ShapeDtypeStruct((8, 128), F32)]),
        in_specs=[_HBM] * (2 * n),
        out_specs=[_SEM] * (3 * n) + [_HBM] * (2 * n) + [pl.BlockSpec(memory_space=pltpu.VMEM)],
        input_output_aliases={i: 3 * n + i for i in range(2 * n)},
        compiler_params=pltpu.CompilerParams(has_side_effects=pltpu.SideEffectType.DATAFLOW_SIDE_EFFECTING),
    )(*srcs, *lands)
    handles = [dict(send=res[a], recv=res[n + a], local=res[2 * n + a], src=res[3 * n + a], land=res[4 * n + a],
                    src_at=items[a][2], dst_at=items[a][3]) for a in range(n)]
    return handles, res[-1]


def _wait_bytes(piece, sem):
    pltpu.make_async_copy(piece, piece, sem).wait()


def _exchange_wait(name, handles, after):
    n = len(handles)

    def body(*refs):
        srcs, lands = refs[:n], refs[n:2 * n]
        send_sems, recv_sems, local_sems = refs[2 * n:3 * n], refs[3 * n:4 * n], refs[4 * n:5 * n]
        me, peers = _peers()
        for a in range(n):
            src_at, dst_at = handles[a]["src_at"], handles[a]["dst_at"]
            for k, (pos, peer) in enumerate(peers):
                _wait_bytes(src_at(srcs[a], peer), send_sems[a].at[k])
                _wait_bytes(dst_at(lands[a], peer), recv_sems[a].at[k])
            pltpu.make_async_copy(src_at(srcs[a], me), dst_at(lands[a], me), local_sems[a]).wait()

    srcs = [hd["src"] for hd in handles]
    lands = [hd["land"] for hd in handles]
    res = pl.pallas_call(
        body, name=name,
        out_shape=[pltpu.HBM(a.shape, a.dtype) for a in srcs] + [pltpu.HBM(a.shape, a.dtype) for a in lands],
        in_specs=[_HBM] * (2 * n) + [_SEM] * (3 * n) + [pl.BlockSpec(memory_space=pl.ANY)],
        out_specs=[_HBM] * (2 * n),
        input_output_aliases={i: i for i in range(2 * n)},
        compiler_params=pltpu.CompilerParams(has_side_effects=pltpu.SideEffectType.DATAFLOW_SIDE_EFFECTING),
    )(*srcs, *lands, *[hd["send"] for hd in handles], *[hd["recv"] for hd in handles],
      *[hd["local"] for hd in handles], after)
    return list(res[n:])


_SIDE = pltpu.SideEffectType.DATAFLOW_SIDE_EFFECTING
_WCOLS = NIN // NDEV


def _win_cols(ref, l):
    return ref.at[:, pl.ds(pl.multiple_of(l * _WCOLS, 128), _WCOLS)]


def _win_routes():
    x, y, c = lax.axis_index("x"), lax.axis_index("y"), lax.axis_index("c")
    pos = [(x, y, 1 - c), (1 - x, y, c), (x, 1 - y, c), (1 - x, 1 - y, c)]
    return 4 * x + 2 * y + c, [(p, 4 * p[0] + 2 * p[1] + p[2]) for p in pos]


def _win_gather_start(shard):
    def body(src, land, send_sem, recv_sem, local_sem, src_thru, land_thru, token):
        me, routes = _win_routes()
        pltpu.make_async_copy(src, _win_cols(land, me), local_sem).start()
        pltpu.make_async_remote_copy(src_ref=src, dst_ref=_win_cols(land, me), send_sem=send_sem, recv_sem=recv_sem,
                                     device_id=routes[0][0], device_id_type=pl.DeviceIdType.MESH).start()
        token[...] = jnp.zeros_like(token)

    src = pltpu.with_memory_space_constraint(shard, pltpu.HBM)
    land = pltpu.with_memory_space_constraint(lax.empty((D, NIN), BF16), pltpu.HBM)
    sem = pltpu.SemaphoreType.DMA(())
    res = pl.pallas_call(
        body, name="win_gather_start",
        out_shape=[sem, sem, sem, pltpu.HBM(src.shape, BF16), pltpu.HBM(land.shape, BF16),
                   jax.ShapeDtypeStruct((8, 128), F32)],
        in_specs=[_HBM, _HBM],
        out_specs=[_SEM, _SEM, _SEM, _HBM, _HBM, pl.BlockSpec(memory_space=pltpu.VMEM)],
        input_output_aliases={0: 3, 1: 4},
        compiler_params=pltpu.CompilerParams(has_side_effects=_SIDE),
    )(src, land)
    return dict(send0=res[0], recv0=res[1], local=res[2], src=res[3], land=res[4]), res[5]


def _win_gather_links(hd, after):
    def body(src, land, after_ref, send_sems, recv_sems, src_thru, land_thru, token):
        me, routes = _win_routes()
        for k in (1, 2, 3):
            pltpu.make_async_remote_copy(src_ref=src, dst_ref=_win_cols(land, me), send_sem=send_sems.at[k - 1],
                                         recv_sem=recv_sems.at[k - 1], device_id=routes[k][0],
                                         device_id_type=pl.DeviceIdType.MESH).start()
        token[...] = jnp.zeros_like(token)

    sem3 = pltpu.SemaphoreType.DMA((3,))
    res = pl.pallas_call(
        body, name="win_gather_links",
        out_shape=[sem3, sem3, pltpu.HBM(hd["src"].shape, BF16), pltpu.HBM(hd["land"].shape, BF16),
                   jax.ShapeDtypeStruct((8, 128), F32)],
        in_specs=[_HBM, _HBM, pl.BlockSpec(memory_space=pl.ANY)],
        out_specs=[_SEM, _SEM, _HBM, _HBM, pl.BlockSpec(memory_space=pltpu.VMEM)],
        input_output_aliases={0: 2, 1: 3},
        compiler_params=pltpu.CompilerParams(has_side_effects=_SIDE),
    )(hd["src"], hd["land"], after)
    return dict(hd, send=res[0], recv=res[1], src=res[2], land=res[3]), res[4]


def _win_gather_forward(hd, after):
    def body(land, recv_sems, after_ref, land_thru, fsend_sems, frecv_sems):
        me, routes = _win_routes()
        sibling = routes[0][0]
        for k in (1, 2, 3):
            pos, peer = routes[k]
            piece = _win_cols(land, peer)
            pltpu.make_async_remote_copy(src_ref=piece, dst_ref=piece, send_sem=fsend_sems.at[k - 1],
                                         recv_sem=recv_sems.at[k - 1], device_id=pos,
                                         device_id_type=pl.DeviceIdType.MESH).wait_recv()
            pltpu.make_async_remote_copy(src_ref=piece, dst_ref=piece, send_sem=fsend_sems.at[k - 1],
                                         recv_sem=frecv_sems.at[k - 1], device_id=sibling,
                                         device_id_type=pl.DeviceIdType.MESH).start()

    sem3 = pltpu.SemaphoreType.DMA((3,))
    res = pl.pallas_call(
        body, name="win_gather_forward",
        out_shape=[pltpu.HBM(hd["land"].shape, BF16), sem3, sem3],
        in_specs=[_HBM, _SEM, pl.BlockSpec(memory_space=pl.ANY)],
        out_specs=[_HBM, _SEM, _SEM],
        input_output_aliases={0: 0},
        compiler_params=pltpu.CompilerParams(has_side_effects=_SIDE),
    )(hd["land"], hd["recv"], after)
    return dict(hd, land=res[0], fsend=res[1], frecv=res[2])


def _win_gather_early(hd):
    def body(src, land, recv_sem, local_sem, src_thru, land_thru):
        me, routes = _win_routes()
        _wait_bytes(_win_cols(land, routes[0][1]), recv_sem)
        pltpu.make_async_copy(src, _win_cols(land, me), local_sem).wait()

    res = pl.pallas_call(
        body, name="win_gather_early",
        out_shape=[pltpu.HBM(hd["src"].shape, BF16), pltpu.HBM(hd["land"].shape, BF16)],
        in_specs=[_HBM, _HBM, _SEM, _SEM],
        out_specs=[_HBM, _HBM],
        input_output_aliases={0: 0, 1: 1},
        compiler_params=pltpu.CompilerParams(has_side_effects=_SIDE),
    )(hd["src"], hd["land"], hd["recv0"], hd["local"])
    return dict(hd, src=res[0], land=res[1])


def _win_gather_wait(hd):
    def body(src, land, send0_sem, send_sems, fsend_sems, frecv_sems, src_thru, land_thru):
        me, routes = _win_routes()
        sib_pos, sibling = routes[0]
        for k in range(4):
            _wait_bytes(src, send0_sem if k == 0 else send_sems.at[k - 1])
        for k in (1, 2, 3):
            _wait_bytes(_win_cols(land, routes[k][1]), fsend_sems.at[k - 1])
            _wait_bytes(_win_cols(land, 4 * routes[k][0][0] + 2 * routes[k][0][1] + sib_pos[2]), frecv_sems.at[k - 1])

    res = pl.pallas_call(
        body, name="win_gather_wait",
        out_shape=[pltpu.HBM(hd["src"].shape, BF16), pltpu.HBM(hd["land"].shape, BF16)],
        in_specs=[_HBM, _HBM] + [_SEM] * 4,
        out_specs=[_HBM, _HBM],
        input_output_aliases={0: 0, 1: 1},
        compiler_params=pltpu.CompilerParams(has_side_effects=_SIDE),
    )(hd["src"], hd["land"], hd["send0"], hd["send"], hd["fsend"], hd["frecv"])
    return res[1]


def _whole(ref, l):
    return ref


def _slot(ref, l):
    return ref.at[l]


def _cols(width):
    def at(ref, l):
        return ref.at[:, pl.ds(pl.multiple_of(l * width, 128), width)]
    return at


def _rows(height):
    def at(ref, l):
        return ref.at[pl.ds(pl.multiple_of(l * height, 8), height), :]
    return at


NTILE = TP // TM


def _tile_rows(t):
    lo = max(t * TM - NMETA, 0)
    hi = min((t + 1) * TM - NMETA, SEQ)
    return lo, hi - lo, lo + NMETA - t * TM


def _for_tile(t, fn):
    for static_t in range(NTILE):
        pl.when(t == static_t)(functools.partial(fn, static_t))


def _token_tile_copy(hbm_ref, buf, sem, t):
    lo, n, off = _tile_rows(t)
    return pltpu.make_async_copy(hbm_ref.at[pl.ds(lo, n)], buf.at[pl.ds(off, n)], sem)


def _prenorm(x, meta_full, pre_w):
    def body(x_ref, meta_ref, pw_ref, h_ref, hn_ref, xbuf, sems):
        i = pl.program_id(0)
        slot = i % 2

        def start(t):
            _token_tile_copy(x_ref, xbuf.at[t % 2], sems.at[t % 2], t).start()

        @pl.when(i == 0)
        def _():
            start(0)
        _for_tile(i + 1, start)
        _for_tile(i, lambda t: _token_tile_copy(x_ref, xbuf.at[t % 2], sems.at[t % 2], t).wait())

        @pl.when(i == 0)
        def _():
            xbuf[0, 0:NMETA, :] = meta_ref[...]

        @pl.when(i == NTILE - 1)
        def _():
            last = _tile_rows(NTILE - 1)[1]
            xbuf[(NTILE - 1) % 2, last:TM, :] = jnp.zeros((TM - last, D), F32)

        pw = pw_ref[...]

        def chunk(ci, carry):
            r0 = pl.multiple_of(ci * R, R)
            xv = xbuf[slot, pl.ds(r0, R), :]
            h_ref[pl.ds(r0, R), :] = xv
            ms = jnp.mean(xv * xv, axis=-1, keepdims=True)
            hn_ref[pl.ds(r0, R), :] = (xv * lax.rsqrt(ms + EPS) * pw).astype(BF16)
            return carry
        lax.fori_loop(0, TM // R, chunk, 0, unroll=2)

    row = pl.BlockSpec((TM, D), lambda i: (i, 0))
    return pl.pallas_call(
        body, name="prenorm",
        grid=(NTILE,),
        in_specs=[pl.BlockSpec(memory_space=pl.ANY), pl.BlockSpec((NMETA, D), lambda i: (0, 0)),
                  pl.BlockSpec((1, D), lambda i: (0, 0))],
        out_specs=[row, row],
        out_shape=[jax.ShapeDtypeStruct((TP, D), F32), jax.ShapeDtypeStruct((TP, D), BF16)],
        scratch_shapes=[pltpu.VMEM((2, TM, D), F32), pltpu.SemaphoreType.DMA((2,))],
        compiler_params=_cparams(),
    )(x, meta_full, pre_w)


def _inproj_cols(name, shards, hn, w_land, b_in, z_prev):
    nsh = shards.shape[0]

    def body(idx_ref, hn_ref, w_ref, b_ref, *rest):
        z_ref = rest[-2]
        z_ref[...] = jnp.dot(hn_ref[...], w_ref[...], preferred_element_type=F32) + b_ref[...]

    any_spec = pl.BlockSpec(memory_space=pl.ANY)
    in_specs = [pl.BlockSpec((TM, D), lambda j, i, idx: (i, 0)),
                pl.BlockSpec((D, _WCOLS), lambda j, i, idx: (0, idx[j])),
                pl.BlockSpec((1, _WCOLS), lambda j, i, idx: (0, idx[j]))]
    operands = [hn, w_land, b_in]
    aliases = {2: 1}
    if z_prev is not None:
        in_specs.append(any_spec)
        operands.append(z_prev)
        aliases[4] = 0
    return pl.pallas_call(
        body, name=name,
        grid_spec=pltpu.PrefetchScalarGridSpec(
            num_scalar_prefetch=1, grid=(nsh, TP // TM), in_specs=in_specs,
            out_specs=[pl.BlockSpec((TM, _WCOLS), lambda j, i, idx: (i, idx[j])), any_spec]),
        out_shape=[jax.ShapeDtypeStruct((TP, NIN), F32), jax.ShapeDtypeStruct(w_land.shape, w_land.dtype)],
        input_output_aliases=aliases,
        compiler_params=_cparams(),
    )(shards, *operands)


def _gate_values(ga, gx, xc, sp8):
    r = _sig(ga)
    i = _sig(gx)
    log_a = -(r * sp8)
    a = jnp.exp(log_a)
    mult = jnp.sqrt(-_expm1_neg(2.0 * log_a))
    return r, i, a, mult


def _lru_fwd(z, conv_w, conv_b, wa_g, b_a, wx_g, b_x, lam):
    def body(x_ref, g_ref, cw_ref, cb_ref, wa_ref, ba_ref, wx_ref, bx_ref, lam_ref,
             y_ref, xc_ref, hs_ref, ga_s, gx_s):
        taps = [cw_ref[k:k + 1, :] for k in range(LW)]
        cb = cb_ref[...]

        def conv_chunk(ci, carry):
            r0 = pl.multiple_of(ci * R, R)
            cur = x_ref[pl.ds(r0, R), :]
            p0 = pl.multiple_of(jnp.maximum(r0 - 8, 0), 8)
            prev = jnp.where(ci > 0, x_ref[pl.ds(p0, 8), :], 0.0)
            buf = jnp.concatenate([prev, cur], axis=0)
            acc = cur * taps[LW - 1] + cb
            for s in range(1, LW):
                acc = acc + pltpu.roll(buf, s, 0)[8:8 + R, :] * taps[LW - 1 - s]
            xc_ref[pl.ds(r0, R), :] = acc
            return carry
        lax.fori_loop(0, TP // R, conv_chunk, 0)

        def gate_chunk(ci, carry):
            r0 = pl.multiple_of(ci * TM, TM)
            xb = xc_ref[pl.ds(r0, TM), :].astype(BF16)
            ga_s[pl.ds(r0, TM), :] = jnp.dot(xb, wa_ref[...], preferred_element_type=F32) + ba_ref[...]
            gx_s[pl.ds(r0, TM), :] = jnp.dot(xb, wx_ref[...], preferred_element_type=F32) + bx_ref[...]
            return carry
        lax.fori_loop(0, TP // TM, gate_chunk, 0)

        sp8 = LRU_C * _softplus(-lam_ref[...])
        row = _row_iota((R, CB))

        def scan_chunk(ci, hprev):
            r0 = pl.multiple_of(ci * R, R)
            xc = xc_ref[pl.ds(r0, R), :]
            _, i, a, mult = _gate_values(ga_s[pl.ds(r0, R), :], gx_s[pl.ds(r0, R), :], xc, sp8)
            u = mult * (i * xc)
            k = 1
            while k < R:
                m = row >= k
                u = jnp.where(m, a * pltpu.roll(u, k, 0) + u, u)
                a = jnp.where(m, a * pltpu.roll(a, k, 0), a)
                k *= 2
            hv = u + a * hprev
            hs_ref[pl.ds(r0, R), :] = hv
            g = g_ref[pl.ds(r0, R), :]
            y_ref[pl.ds(r0, R), :] = (hv * (g * _sig(g))).astype(BF16)
            return jnp.sum(jnp.where(row == R - 1, hv, 0.0), axis=0, keepdims=True)
        lax.fori_loop(0, TP // R, scan_chunk, jnp.zeros((1, CB), F32))

    col = lambda off: pl.BlockSpec((TP, CB), lambda j: (0, off + j))
    vec = pl.BlockSpec((1, CB), lambda j: (0, j))
    wsp = pl.BlockSpec((None, CB, CB), lambda j: (j, 0, 0))
    return pl.pallas_call(
        body, name="lru_fwd",
        grid=(NCB,),
        in_specs=[col(0), col(NCB), pl.BlockSpec((LW, CB), lambda j: (0, j)), vec, wsp, vec, wsp, vec, vec],
        out_specs=[col(0), col(0), col(0)],
        out_shape=[jax.ShapeDtypeStruct((TP, DL), BF16), jax.ShapeDtypeStruct((TP, DL), F32),
                   jax.ShapeDtypeStruct((TP, DL), F32)],
        scratch_shapes=[pltpu.VMEM((TP, CB), F32), pltpu.VMEM((TP, CB), F32)],
        compiler_params=_cparams(),
    )(z, z, conv_w, conv_b, wa_g, b_a, wx_g, b_x, lam)


def _conf_fwd_conv(z, dw_w, dw_b):
    def body(u1_ref, u2_ref, w_ref, b_ref, vc_ref, vs):
        vs[pl.ds(0, KWP), :] = jnp.zeros((KWP, CB), F32)

        def glu_chunk(ci, carry):
            r0 = pl.multiple_of(ci * R, R)
            vs[pl.ds(KWP + r0, R), :] = u1_ref[pl.ds(r0, R), :] * _sig(u2_ref[pl.ds(r0, R), :])
            return carry
        lax.fori_loop(0, TP // R, glu_chunk, 0)

        bias = b_ref[...]

        def conv_chunk(ci, carry):
            r0 = pl.multiple_of(ci * R, R)
            buf = vs[pl.ds(r0, KWP + R), :]
            acc = jnp.zeros((R, CB), F32) + bias
            for rr in range(8):
                rolled = buf if rr == 0 else pltpu.roll(buf, rr, 0)
                for q in range(4):
                    s = 8 * q + rr
                    if s > KW - 1:
                        continue
                    k = KW - 1 - s
                    acc = acc + rolled[KWP - 8 * q:KWP - 8 * q + R, :] * w_ref[k:k + 1, :]
            vc_ref[pl.ds(r0, R), :] = acc
            return carry
        lax.fori_loop(0, TP // R, conv_chunk, 0)

    return pl.pallas_call(
        body, name="conf_fwd_conv",
        grid=(NCB,),
        in_specs=[pl.BlockSpec((TP, CB), lambda j: (0, 2 * NCB + j)),
                  pl.BlockSpec((TP, CB), lambda j: (0, 3 * NCB + j)),
                  pl.BlockSpec((KWP, CB), lambda j: (0, j)),
                  pl.BlockSpec((1, CB), lambda j: (0, j))],
        out_specs=pl.BlockSpec((TP, CB), lambda j: (0, j)),
        out_shape=jax.ShapeDtypeStruct((TP, DC), F32),
        scratch_shapes=[pltpu.VMEM((TP + KWP, CB), F32)],
        compiler_params=_cparams(),
    )(z, z, dw_w, dw_b)


def _ln_chunk(vc, lw, lb):
    mu = jnp.mean(vc, axis=-1, keepdims=True)
    xm = vc - mu
    var = jnp.mean(xm * xm, axis=-1, keepdims=True)
    rstd = lax.rsqrt(var + EPS)
    xhat = xm * rstd
    return xhat, rstd, xhat * lw + lb


def _conf_fwd_proj(vc, z, ln_w, ln_b, pw_w, pw_b):
    def body(vc_ref, g_ref, lw_ref, lb_ref, w_ref, b_ref, y_ref, p_ref, s_s):
        lw, lb = lw_ref[...], lb_ref[...]

        def ln_chunk(ci, carry):
            r0 = pl.multiple_of(ci * R, R)
            for half in range(2):
                rr = r0 + 8 * half
                _, _, ln = _ln_chunk(vc_ref[pl.ds(rr, 8), :], lw, lb)
                p_ref[pl.ds(rr, 8), :] = ln * _sig(ln)
            s_s[pl.ds(r0, R), :] = p_ref[pl.ds(r0, R), :].astype(BF16)
            return carry
        lax.fori_loop(0, TM // R, ln_chunk, 0, unroll=2)

        p_ref[...] = jnp.dot(s_s[...], w_ref[...], preferred_element_type=F32) + b_ref[...]

        def out_chunk(ci, carry):
            r0 = pl.multiple_of(ci * R, R)
            g = g_ref[pl.ds(r0, R), :]
            y_ref[pl.ds(r0, R), :] = (p_ref[pl.ds(r0, R), :] * (g * _sig(g))).astype(BF16)
            return carry
        lax.fori_loop(0, TM // R, out_chunk, 0)

    row = pl.BlockSpec((TM, DC), lambda i: (i, 0))
    vec = pl.BlockSpec((1, DC), lambda i: (0, 0))
    return pl.pallas_call(
        body, name="conf_fwd_proj",
        grid=(TP // TM,),
        in_specs=[row, pl.BlockSpec((TM, DC), lambda i: (i, 4)), vec, vec,
                  pl.BlockSpec((DC, DC), lambda i: (0, 0)), vec],
        out_specs=[row, row],
        out_shape=[jax.ShapeDtypeStruct((TP, DC), BF16), jax.ShapeDtypeStruct((TP, DC), F32)],
        scratch_shapes=[pltpu.VMEM((TM, DC), BF16)],
        compiler_params=_cparams(),
    )(vc, z, ln_w, ln_b, pw_w, pw_b)


def _outproj_loss(ylru, yconf, w_out, h, target, post_w):
    def body(yl_ref, yc_ref, w_ref, h_ref, tgt_hbm, pw_ref, dout_ref, dy_ref, loss_ref, dpw_ref, y_s, t_ref, sem):
        i = pl.program_id(0)
        k = pl.program_id(1)

        @pl.when(k == 0)
        def _():
            _for_tile(i, lambda t: _token_tile_copy(tgt_hbm, t_ref, sem, t).start())
            y_s[...] = jnp.dot(yl_ref[...], w_ref[...], preferred_element_type=F32)

        @pl.when(k == 1)
        def _():
            y_s[...] += jnp.dot(yc_ref[...], w_ref[...], preferred_element_type=F32)

        @pl.when(jnp.logical_and(i == 0, k == 1))
        def _():
            loss_ref[...] = jnp.zeros_like(loss_ref)
            dpw_ref[...] = jnp.zeros_like(dpw_ref)

        @pl.when(k == 1)
        def _():
            _for_tile(i, lambda t: _token_tile_copy(tgt_hbm, t_ref, sem, t).wait())

            @pl.when(i == 0)
            def _():
                t_ref[0:NMETA, :] = jnp.zeros((NMETA, D), F32)

            @pl.when(i == NTILE - 1)
            def _():
                last = _tile_rows(NTILE - 1)[1]
                t_ref[last:TM, :] = jnp.zeros((TM - last, D), F32)

            pw = pw_ref[...]
            row = _row_iota((8, D))

            def chunk(ci, carry):
                r0 = pl.multiple_of(ci * 8, 8)
                yv = y_s[pl.ds(r0, 8), :]
                rs = lax.rsqrt(jnp.mean(yv * yv, axis=-1, keepdims=True) + EPS)
                grow = row + (i * TM + r0)
                valid = jnp.logical_and(grow >= NMETA, grow < T)
                yn = yv * rs
                err = jnp.where(valid, h_ref[pl.ds(r0, 8), :] + yn * pw - t_ref[pl.ds(r0, 8), :], 0.0)
                loss_ref[...] += err * err
                d_rn = err * (1.0 / D)
                dout_ref[pl.ds(r0, 8), :] = d_rn
                dpw_ref[...] += d_rn * yn
                gw = d_rn * pw
                dot = jnp.mean(gw * yv, axis=-1, keepdims=True)
                dy_ref[pl.ds(r0, 8), :] = (rs * gw - yv * (rs * rs * rs * dot)).astype(BF16)
                return carry
            lax.fori_loop(0, TM // 8, chunk, 0, unroll=4)

    row = pl.BlockSpec((TM, D), lambda i, k: (i, 0))
    half = pl.BlockSpec((TM, DL), lambda i, k: (i, 0))
    acc = pl.BlockSpec((8, D), lambda i, k: (0, 0))
    return pl.pallas_call(
        body, name="outproj_loss",
        grid=(TP // TM, 2),
        in_specs=[half, half, pl.BlockSpec((DL, D), lambda i, k: (k, 0)), row, pl.BlockSpec(memory_space=pl.ANY),
                  pl.BlockSpec((1, D), lambda i, k: (0, 0))],
        out_specs=[row, row, acc, acc],
        out_shape=[jax.ShapeDtypeStruct((TP, D), F32), jax.ShapeDtypeStruct((TP, D), BF16),
                   jax.ShapeDtypeStruct((8, D), F32), jax.ShapeDtypeStruct((8, D), F32)],
        scratch_shapes=[pltpu.VMEM((TM, D), F32), pltpu.VMEM((TM, D), F32), pltpu.SemaphoreType.DMA(())],
        compiler_params=_cparams(),
    )(ylru, yconf, w_out, h, target, post_w)


_NT = (((1,), (1,)), ((), ()))
_TN = (((0,), (0,)), ((), ()))


def _outproj_bwd(dy, ylru, yconf, w_out):
    def body(dy_ref, yl_ref, yc_ref, w_ref, dycat_ref, dw_ref):
        j = pl.program_id(0)
        dyv = dy_ref[...]
        dycat_ref[...] = lax.dot_general(dyv, w_ref[...], _NT, preferred_element_type=F32)

        @pl.when(j < NCB)
        def _():
            dw_ref[...] = lax.dot_general(yl_ref[...], dyv, _TN, preferred_element_type=F32).astype(BF16)

        @pl.when(j >= NCB)
        def _():
            dw_ref[...] = lax.dot_general(yc_ref[...], dyv, _TN, preferred_element_type=F32).astype(BF16)

    return pl.pallas_call(
        body, name="outproj_bwd",
        grid=(2 * NCB,),
        in_specs=[pl.BlockSpec((TP, D), lambda j: (0, 0)),
                  pl.BlockSpec((TP, CB), lambda j: (0, jnp.minimum(j, NCB - 1))),
                  pl.BlockSpec((TP, CB), lambda j: (0, jnp.maximum(j - NCB, 0))),
                  pl.BlockSpec((CB, D), lambda j: (j, 0))],
        out_specs=[pl.BlockSpec((TP, CB), lambda j: (0, j)), pl.BlockSpec((CB, D), lambda j: (j, 0))],
        out_shape=[jax.ShapeDtypeStruct((TP, D), F32), jax.ShapeDtypeStruct((D, D), BF16)],
        compiler_params=_cparams(),
    )(dy, ylru, yconf, w_out)


def _conf_bwd_proj(dycat, p, z, vc, ln_w, ln_b, pw_w):
    def body(dy_ref, p_ref, g_ref, vc_ref, lw_ref, lb_ref, w_ref,
             dvc_ref, dgc_ref, dpw_ref, vecs_ref, dp_s, s_s, ds_s):
        i = pl.program_id(0)
        lw, lb = lw_ref[...], lb_ref[...]

        @pl.when(i == 0)
        def _():
            dpw_ref[...] = jnp.zeros_like(dpw_ref)
            vecs_ref[...] = jnp.zeros_like(vecs_ref)

        def pre_chunk(ci, carry):
            r0 = pl.multiple_of(ci * R, R)
            for half in range(2):
                rr = r0 + 8 * half
                dyv = dy_ref[pl.ds(rr, 8), :]
                g = g_ref[pl.ds(rr, 8), :]
                sg = _sig(g)
                dp = dyv * (g * sg)
                dg = dyv * p_ref[pl.ds(rr, 8), :] * (sg * (1.0 + g * (1.0 - sg)))
                vecs_ref[0:8, :] += dp
                vecs_ref[8:16, :] += dg
                ds_s[pl.ds(rr, 8), :] = dp
                dvc_ref[pl.ds(rr, 8), :] = dg
            dp_s[pl.ds(r0, R), :] = ds_s[pl.ds(r0, R), :].astype(BF16)
            dgc_ref[pl.ds(r0, R), :] = dvc_ref[pl.ds(r0, R), :].astype(BF16)
            for half in range(2):
                rr = r0 + 8 * half
                _, _, ln = _ln_chunk(vc_ref[pl.ds(rr, 8), :], lw, lb)
                ds_s[pl.ds(rr, 8), :] = ln * _sig(ln)
            s_s[pl.ds(r0, R), :] = ds_s[pl.ds(r0, R), :].astype(BF16)
            return carry
        lax.fori_loop(0, TM // R, pre_chunk, 0, unroll=2)

        dpb = dp_s[...]
        ds_s[...] = lax.dot_general(dpb, w_ref[...], _NT, preferred_element_type=F32)
        dpw_ref[...] += lax.dot_general(s_s[...], dpb, _TN, preferred_element_type=F32)

        def post_chunk(ci, carry):
            r0 = pl.multiple_of(ci * 8, 8)
            xhat, rstd, ln = _ln_chunk(vc_ref[pl.ds(r0, 8), :], lw, lb)
            sl = _sig(ln)
            dln = ds_s[pl.ds(r0, 8), :] * (sl * (1.0 + ln * (1.0 - sl)))
            vecs_ref[16:24, :] += dln * xhat
            vecs_ref[24:32, :] += dln
            dxh = dln * lw
            m1 = jnp.mean(dxh, axis=-1, keepdims=True)
            m2 = jnp.mean(dxh * xhat, axis=-1, keepdims=True)
            dvc_ref[pl.ds(r0, 8), :] = rstd * (dxh - m1 - xhat * m2)
            return carry
        lax.fori_loop(0, TM // 8, post_chunk, 0, unroll=4)

    row = pl.BlockSpec((TM, DC), lambda i: (i, 0))
    vec = pl.BlockSpec((1, DC), lambda i: (0, 0))
    return pl.pallas_call(
        body, name="conf_bwd_proj",
        grid=(TP // TM,),
        in_specs=[pl.BlockSpec((TM, DC), lambda i: (i, 1)), row, pl.BlockSpec((TM, DC), lambda i: (i, 4)), row,
                  vec, vec, pl.BlockSpec((DC, DC), lambda i: (0, 0))],
        out_specs=[row, row, pl.BlockSpec((DC, DC), lambda i: (0, 0)), pl.BlockSpec((32, DC), lambda i: (0, 0))],
        out_shape=[jax.ShapeDtypeStruct((TP, DC), F32), jax.ShapeDtypeStruct((TP, DC), BF16),
                   jax.ShapeDtypeStruct((DC, DC), F32), jax.ShapeDtypeStruct((32, DC), F32)],
        scratch_shapes=[pltpu.VMEM((TM, DC), BF16), pltpu.VMEM((TM, DC), BF16), pltpu.VMEM((TM, DC), F32)],
        compiler_params=_cparams(),
    )(dycat, p, z, vc, ln_w, ln_b, pw_w)


def _conf_bwd_conv(dvc, z, dw_w):
    def body(dvc_ref, u1_ref, u2_ref, w_ref, du_ref, dw_ref, vecs_ref, vs, dvs):
        vs[pl.ds(0, KWP), :] = jnp.zeros((KWP, CB), F32)
        dvs[pl.ds(TP, KWP), :] = jnp.zeros((KWP, CB), F32)
        dw_ref[...] = jnp.zeros_like(dw_ref)
        vecs_ref[...] = jnp.zeros_like(vecs_ref)

        def fill_chunk(ci, carry):
            r0 = pl.multiple_of(ci * R, R)
            vs[pl.ds(KWP + r0, R), :] = u1_ref[pl.ds(r0, R), :] * _sig(u2_ref[pl.ds(r0, R), :])
            dv = dvc_ref[pl.ds(r0, R), :]
            dvs[pl.ds(r0, R), :] = dv
            vecs_ref[0:8, :] += _fold8(dv)
            return carry
        lax.fori_loop(0, TP // R, fill_chunk, 0)

        def conv_chunk(ci, carry):
            r0 = pl.multiple_of(ci * R, R)
            vbuf = vs[pl.ds(r0, KWP + R), :]
            dbuf = dvs[pl.ds(r0, KWP + R), :]
            dcur = dbuf[0:R, :]
            dv = jnp.zeros((R, CB), F32)
            for rr in range(8):
                vroll = vbuf if rr == 0 else pltpu.roll(vbuf, rr, 0)
                droll = dbuf if rr == 0 else pltpu.roll(dbuf, KWP + R - rr, 0)
                for q in range(4):
                    s = 8 * q + rr
                    if s > KW - 1:
                        continue
                    k = KW - 1 - s
                    dv = dv + droll[8 * q:8 * q + R, :] * w_ref[k:k + 1, :]
                    dw_ref[8 * k:8 * k + 8, :] += _fold8(dcur * vroll[KWP - 8 * q:KWP - 8 * q + R, :])
            u1 = u1_ref[pl.ds(r0, R), :]
            sg = _sig(u2_ref[pl.ds(r0, R), :])
            du1 = dv * sg
            du2 = dv * u1 * (sg * (1.0 - sg))
            du_ref[0, pl.ds(r0, R), :] = du1.astype(BF16)
            du_ref[1, pl.ds(r0, R), :] = du2.astype(BF16)
            vecs_ref[8:16, :] += _fold8(du1)
            vecs_ref[16:24, :] += _fold8(du2)
            return carry
        lax.fori_loop(0, TP // R, conv_chunk, 0)

    blk = pl.BlockSpec((TP, CB), lambda j: (0, j))
    return pl.pallas_call(
        body, name="conf_bwd_conv",
        grid=(NCB,),
        in_specs=[blk, pl.BlockSpec((TP, CB), lambda j: (0, 2 * NCB + j)),
                  pl.BlockSpec((TP, CB), lambda j: (0, 3 * NCB + j)), pl.BlockSpec((KWP, CB), lambda j: (0, j))],
        out_specs=[pl.BlockSpec((2, TP, CB), lambda j: (0, 0, j)), pl.BlockSpec((8 * KWP, CB), lambda j: (0, j)),
                   pl.BlockSpec((24, CB), lambda j: (0, j))],
        out_shape=[jax.ShapeDtypeStruct((2, TP, DC), BF16),
                   jax.ShapeDtypeStruct((8 * KWP, DC), F32), jax.ShapeDtypeStruct((24, DC), F32)],
        scratch_shapes=[pltpu.VMEM((TP + KWP, CB), F32), pltpu.VMEM((TP + KWP, CB), F32)],
        compiler_params=_cparams(),
    )(dvc, z, z, dw_w)


def _lru_bwd(dycat, z, xc, hs, conv_w, wa_g, b_a, wx_g, b_x, lam):
    NV = 6

    def body(dy_ref, x_ref, g_ref, xc_ref, hs_ref, cw_ref, wa_ref, ba_ref, wx_ref, bx_ref, lam_ref,
             dzl_ref, dwa_ref, dwx_ref, dcw_ref, vecs_ref, ga_s, gx_s, dxc_s):
        vecs_ref[...] = jnp.zeros_like(vecs_ref)
        dcw_ref[...] = jnp.zeros_like(dcw_ref)
        dxc_s[pl.ds(TP, 8), :] = jnp.zeros((8, CB), F32)

        def gate_chunk(ci, carry):
            r0 = pl.multiple_of(ci * TM, TM)
            xb = xc_ref[pl.ds(r0, TM), :].astype(BF16)
            ga_s[pl.ds(r0, TM), :] = jnp.dot(xb, wa_ref[...], preferred_element_type=F32) + ba_ref[...]
            gx_s[pl.ds(r0, TM), :] = jnp.dot(xb, wx_ref[...], preferred_element_type=F32) + bx_ref[...]
            return carry
        lax.fori_loop(0, TP // TM, gate_chunk, 0)

        sp8 = LRU_C * _softplus(-lam_ref[...])
        row = _row_iota((R, CB))
        nchunk = TP // R

        def scan_chunk(cj, carry):
            a_next, lam_next = carry
            ci = nchunk - 1 - cj
            r0 = pl.multiple_of(ci * R, R)
            dyv = dy_ref[pl.ds(r0, R), :]
            g = g_ref[pl.ds(r0, R), :]
            hv = hs_ref[pl.ds(r0, R), :]
            xc = xc_ref[pl.ds(r0, R), :]
            sg = _sig(g)
            dgl = dyv * hv * (sg * (1.0 + g * (1.0 - sg)))
            dzl_ref[1, pl.ds(r0, R), :] = dgl.astype(BF16)
            vecs_ref[0:8, :] += _fold8(dgl)
            dhs = dyv * (g * sg)
            r, i, a, mult = _gate_values(ga_s[pl.ds(r0, R), :], gx_s[pl.ds(r0, R), :], xc, sp8)
            b = jnp.where(row == R - 1, a_next, pltpu.roll(a, R - 1, 0))
            lv = dhs
            k = 1
            while k < R:
                m = row < R - k
                lv = jnp.where(m, lv + b * pltpu.roll(lv, R - k, 0), lv)
                b = jnp.where(m, b * pltpu.roll(b, R - k, 0), b)
                k *= 2
            lv = lv + b * lam_next
            p0 = pl.multiple_of(jnp.maximum(r0 - 8, 0), 8)
            hprev8 = jnp.where(ci > 0, hs_ref[pl.ds(p0, 8), :], 0.0)
            hprev = pltpu.roll(jnp.concatenate([hprev8, hv], axis=0), 1, 0)[8:8 + R, :]
            da = lv * hprev
            ixc = i * xc
            dmult = lv * ixc
            di = lv * mult * xc
            dxc_s[pl.ds(r0, R), :] = lv * mult * i
            a2 = a * a
            dlog_a = da * a - dmult * a2 / mult
            vecs_ref[32:40, :] += _fold8(dlog_a * r)
            dga = -(dlog_a * sp8) * r * (1.0 - r)
            dgx = di * i * (1.0 - i)
            ga_s[pl.ds(r0, R), :] = dga
            gx_s[pl.ds(r0, R), :] = dgx
            vecs_ref[16:24, :] += _fold8(dga)
            vecs_ref[24:32, :] += _fold8(dgx)
            a_first = jnp.sum(jnp.where(row == 0, a, 0.0), axis=0, keepdims=True)
            l_first = jnp.sum(jnp.where(row == 0, lv, 0.0), axis=0, keepdims=True)
            return a_first, l_first
        lax.fori_loop(0, nchunk, scan_chunk, (jnp.zeros((1, CB), F32), jnp.zeros((1, CB), F32)))

        dwa_ref[...] = jnp.zeros_like(dwa_ref)
        dwx_ref[...] = jnp.zeros_like(dwx_ref)

        def mm_chunk(ci, carry):
            r0 = pl.multiple_of(ci * TM, TM)
            xb = xc_ref[pl.ds(r0, TM), :].astype(BF16)
            dgab = ga_s[pl.ds(r0, TM), :].astype(BF16)
            dgxb = gx_s[pl.ds(r0, TM), :].astype(BF16)
            dxc_s[pl.ds(r0, TM), :] += (lax.dot_general(dgab, wa_ref[...], _NT, preferred_element_type=F32)
                                        + lax.dot_general(dgxb, wx_ref[...], _NT, preferred_element_type=F32))
            dwa_ref[...] += lax.dot_general(xb, dgab, _TN, preferred_element_type=F32)
            dwx_ref[...] += lax.dot_general(xb, dgxb, _TN, preferred_element_type=F32)
            return carry
        lax.fori_loop(0, TP // TM, mm_chunk, 0)

        taps = [cw_ref[k:k + 1, :] for k in range(LW)]

        def conv_chunk(ci, carry):
            r0 = pl.multiple_of(ci * R, R)
            dbuf = dxc_s[pl.ds(r0, R + 8), :]
            dcur = dbuf[0:R, :]
            p0 = pl.multiple_of(jnp.maximum(r0 - 8, 0), 8)
            xprev = jnp.where(ci > 0, x_ref[pl.ds(p0, 8), :], 0.0)
            xbuf = jnp.concatenate([xprev, x_ref[pl.ds(r0, R), :]], axis=0)
            dxl = dcur * taps[LW - 1]
            dcw_ref[8 * (LW - 1):8 * LW, :] += _fold8(dcur * xbuf[8:8 + R, :])
            for s in range(1, LW):
                k = LW - 1 - s
                dxl = dxl + pltpu.roll(dbuf, R + 8 - s, 0)[0:R, :] * taps[k]
                dcw_ref[8 * k:8 * k + 8, :] += _fold8(dcur * pltpu.roll(xbuf, s, 0)[8:8 + R, :])
            dzl_ref[0, pl.ds(r0, R), :] = dxl.astype(BF16)
            vecs_ref[8:16, :] += _fold8(dxl)
            vecs_ref[40:48, :] += _fold8(dcur)
            return carry
        lax.fori_loop(0, TP // R, conv_chunk, 0)
        vecs_ref[32:40, :] = vecs_ref[32:40, :] * (LRU_C * _sig(-lam_ref[...]))

    col = lambda off: pl.BlockSpec((TP, CB), lambda j: (0, off + j))
    vec = pl.BlockSpec((1, CB), lambda j: (0, j))
    wsp = pl.BlockSpec((None, CB, CB), lambda j: (j, 0, 0))
    return pl.pallas_call(
        body, name="lru_bwd",
        grid=(NCB,),
        in_specs=[col(0), col(0), col(NCB), col(0), col(0), pl.BlockSpec((LW, CB), lambda j: (0, j)),
                  wsp, vec, wsp, vec, vec],
        out_specs=[pl.BlockSpec((2, TP, CB), lambda j: (0, 0, j)), wsp, wsp,
                   pl.BlockSpec((8 * LW, CB), lambda j: (0, j)), pl.BlockSpec((8 * NV, CB), lambda j: (0, j))],
        out_shape=[jax.ShapeDtypeStruct((2, TP, DL), BF16),
                   jax.ShapeDtypeStruct((NCB, CB, CB), F32), jax.ShapeDtypeStruct((NCB, CB, CB), F32),
                   jax.ShapeDtypeStruct((8 * LW, DL), F32), jax.ShapeDtypeStruct((8 * NV, DL), F32)],
        scratch_shapes=[pltpu.VMEM((TP, CB), F32), pltpu.VMEM((TP, CB), F32), pltpu.VMEM((TP + 8, CB), F32)],
        compiler_params=_cparams(),
    )(dycat, z, z, xc, hs, conv_w, wa_g, b_a, wx_g, b_x, lam)


def _dz_section(sec, dzl_ref, dzc_ref, dgc_ref, use):
    @pl.when(sec < 2)
    def _():
        use(dzl_ref)

    @pl.when(jnp.logical_and(sec >= 2, sec < 4))
    def _():
        use(dzc_ref)

    @pl.when(sec == 4)
    def _():
        use(dgc_ref)


def _dz_specs(rows, index):
    return [pl.BlockSpec((None, rows, 1024), lambda a, b: (jnp.minimum(index(a, b)[1], 1), index(a, b)[0], 0)),
            pl.BlockSpec((None, rows, 1024), lambda a, b: (jnp.clip(index(a, b)[1] - 2, 0, 1), index(a, b)[0], 0)),
            pl.BlockSpec((rows, 1024), lambda a, b: (index(a, b)[0], 0))]


def _inproj_wgrad(name, hn, dzs):
    KB = 512
    nsec = dzs.shape[0]

    def body(hn_ref, dz_ref, dw_ref):
        dw_ref[...] = lax.dot_general(hn_ref[...], dz_ref[...], _TN, preferred_element_type=F32).astype(BF16)

    return pl.pallas_call(
        body, name=name,
        grid=(nsec, D // KB),
        in_specs=[pl.BlockSpec((TP, KB), lambda n, kb: (0, kb)),
                  pl.BlockSpec((None, TP, 1024), lambda n, kb: (n, 0, 0))],
        out_specs=pl.BlockSpec((KB, 1024), lambda n, kb: (kb, n)),
        out_shape=jax.ShapeDtypeStruct((D, nsec * 1024), BF16),
        compiler_params=_cparams(),
    )(hn, dzs)


def _sum_win_parts(parts_a, parts_b, parts_c):
    RB = 64

    def body(a_ref, b_ref, c_ref, o_ref):
        def chunk(ci, carry):
            r0 = pl.multiple_of(ci * R, R)
            for ref, base, ncol in ((a_ref, 0, 2048), (b_ref, 2048, 2048), (c_ref, 4096, 1024)):
                for c0 in range(0, ncol, 512):
                    acc = ref[0, pl.ds(r0, R), c0:c0 + 512].astype(F32)
                    for sidx in range(1, NDEV):
                        acc = acc + ref[sidx, pl.ds(r0, R), c0:c0 + 512].astype(F32)
                    o_ref[pl.ds(r0, R), base + c0:base + c0 + 512] = acc.astype(BF16)
            return carry
        lax.fori_loop(0, RB // R, chunk, 0)

    spec = lambda ncol: pl.BlockSpec((NDEV, RB, ncol), lambda i: (0, i, 0))
    return pl.pallas_call(
        body, name="sum_win_parts",
        grid=(D // NDEV // RB,),
        in_specs=[spec(2048), spec(2048), spec(1024)],
        out_specs=pl.BlockSpec((RB, NIN), lambda i: (i, 0)),
        out_shape=jax.ShapeDtypeStruct((D // NDEV, NIN), BF16),
        compiler_params=_cparams(),
    )(parts_a, parts_b, parts_c)


def _inproj_bwd(dzl, dzc, dgc, w_in, h, dout, pre_w):
    nsec = NIN // 1024

    def body(dzl_ref, dzc_ref, dgc_ref, w_ref, h_ref, dout_ref, pw_ref, gx_hbm, dmeta_ref, dpw_ref, acc_s, dh_s, sem):
        i = pl.program_id(0)
        s = pl.program_id(1)

        def gx_copy(t):
            lo, n, off = _tile_rows(t)
            return pltpu.make_async_copy(dh_s.at[pl.ds(off, n)], gx_hbm.at[pl.ds(lo, n)], sem)

        @pl.when(s == 0)
        def _():
            acc_s[...] = jnp.zeros_like(acc_s)

        def use(dz_ref):
            acc_s[...] += lax.dot_general(dz_ref[...], w_ref[...], _NT, preferred_element_type=F32)
        _dz_section(s, dzl_ref, dzc_ref, dgc_ref, use)

        @pl.when(jnp.logical_and(i == 0, s == nsec - 1))
        def _():
            dpw_ref[...] = jnp.zeros_like(dpw_ref)

        @pl.when(s == nsec - 1)
        def _():
            _for_tile(i - 1, lambda t: gx_copy(t).wait())
            pw = pw_ref[...]

            def chunk(ci, carry):
                r0 = pl.multiple_of(ci * 8, 8)
                hv = h_ref[pl.ds(r0, 8), :]
                dhn = acc_s[pl.ds(r0, 8), :]
                rs = lax.rsqrt(jnp.mean(hv * hv, axis=-1, keepdims=True) + EPS)
                dpw_ref[...] += dhn * (hv * rs)
                gw = dhn * pw
                dot = jnp.mean(gw * hv, axis=-1, keepdims=True)
                dh_s[pl.ds(r0, 8), :] = rs * gw - hv * (rs * rs * rs * dot) + dout_ref[pl.ds(r0, 8), :]
                return carry
            lax.fori_loop(0, TM // 8, chunk, 0, unroll=4)
            _for_tile(i, lambda t: gx_copy(t).start())

            @pl.when(i == 0)
            def _():
                dmeta_ref[...] = dh_s[0:NMETA, :]

            @pl.when(i == NTILE - 1)
            def _():
                gx_copy(NTILE - 1).wait()

    row = pl.BlockSpec((TM, D), lambda i, s: (i, 0))
    return pl.pallas_call(
        body, name="inproj_bwd",
        grid=(TP // TM, nsec),
        in_specs=_dz_specs(TM, lambda i, s: (i, s)) + [
            pl.BlockSpec((D, 1024), lambda i, s: (0, s)), row, row, pl.BlockSpec((1, D), lambda i, s: (0, 0))],
        out_specs=[pl.BlockSpec(memory_space=pl.ANY), pl.BlockSpec((NMETA, D), lambda i, s: (0, 0)),
                   pl.BlockSpec((8, D), lambda i, s: (0, 0))],
        out_shape=[jax.ShapeDtypeStruct((SEQ, D), F32), jax.ShapeDtypeStruct((NMETA, D), F32),
                   jax.ShapeDtypeStruct((8, D), F32)],
        scratch_shapes=[pltpu.VMEM((TM, D), F32), pltpu.VMEM((TM, D), F32), pltpu.SemaphoreType.DMA(())],
        compiler_params=_cparams(),
    )(dzl, dzc, dgc, w_in, h, dout, pre_w)


def _adamw(name, parts, w, m, v, block_rows):
    rows, cols = w.shape
    nparts = parts.shape[0]
    cw = cols if cols <= 640 else 512

    def body(p_ref, w_ref, m_ref, v_ref, g_ref, d_ref, nm_ref, nv_ref):
        def chunk(ci, carry):
            r0 = pl.multiple_of(ci * R, R)
            for c0 in range(0, cols, cw):
                at = (pl.ds(r0, R), slice(c0, c0 + cw))
                g = p_ref[(0,) + at].astype(F32)
                for sidx in range(1, nparts):
                    g = g + p_ref[(sidx,) + at].astype(F32)
                delta, mv, vv = _adam_math(g, w_ref[at], m_ref[at], v_ref[at])
                g_ref[at] = g
                nm_ref[at] = mv
                nv_ref[at] = vv
                d_ref[at] = delta
            return carry
        lax.fori_loop(0, block_rows // R, chunk, 0)

    blk = pl.BlockSpec((block_rows, cols), lambda i: (i, 0))
    shp = jax.ShapeDtypeStruct((rows, cols), F32)
    return pl.pallas_call(
        body, name=name,
        grid=(rows // block_rows,),
        in_specs=[pl.BlockSpec((nparts, block_rows, cols), lambda i: (0, i, 0)), blk, blk, blk],
        out_specs=[blk, blk, blk, blk],
        out_shape=[shp, shp, shp, shp],
        compiler_params=_cparams(),
    )(parts, w, m, v)


def _adam_math(g, w, m, v):
    c1 = 1.0 / (1.0 - ADAM_B1 ** ADAM_STEP)
    c2 = 1.0 / (1.0 - ADAM_B2 ** ADAM_STEP)
    mv = ADAM_B1 * m + (1.0 - ADAM_B1) * g
    vv = ADAM_B2 * v + (1.0 - ADAM_B2) * (g * g)
    upd = (mv * c1) / (jnp.sqrt(vv * c2) + ADAM_EPS) + ADAM_WD * w
    return -ADAM_LR * upd, mv, vv


_VEC = [("pre_norm_w", 2), ("post_norm_w", 2), ("b_in", 5), ("lru_conv_b", 1), ("b_gate_a", 1), ("b_gate_x", 1),
        ("lru_lambda", 1), ("conf_dw_b", 1), ("conf_ln_w", 1), ("conf_ln_b", 1), ("conf_pw_b", 1)]
_VEC_ROWS = 24
_LOSS_ROW = 17
_SM_ROWS = 64


def _pack_grads(dprew_acc, dpostw_acc, cvecs, kvecs, lvecs, dcw_acc, ddw_acc, dh, loss_acc):
    def body(pre_ref, post_ref, c_ref, k_ref, l_ref, dcw_ref, ddw_ref, dh_ref, loss_ref, vec_ref, small_ref, tmp):
        s8 = lambda ref, r: jnp.sum(ref[8 * r:8 * r + 8, :], axis=0, keepdims=True)
        vec_ref[...] = jnp.zeros_like(vec_ref)
        pre, post = s8(pre_ref, 0), s8(post_ref, 0)
        rows = [pre[:, 0:1024], pre[:, 1024:2048], post[:, 0:1024], post[:, 1024:2048],
                s8(l_ref, 1), s8(l_ref, 0), s8(k_ref, 1), s8(k_ref, 2), s8(c_ref, 1),
                s8(l_ref, 5), s8(l_ref, 2), s8(l_ref, 3), s8(l_ref, 4),
                s8(k_ref, 0), s8(c_ref, 2), s8(c_ref, 3), s8(c_ref, 0)]
        for r, val in enumerate(rows):
            vec_ref[r:r + 1, :] = val
        vec_ref[_LOSS_ROW:_LOSS_ROW + 1, :] = jnp.zeros((1, 1024), F32) + (0.5 / D) * jnp.sum(loss_ref[...])

        small_ref[...] = jnp.zeros_like(small_ref)
        for k in range(LW):
            tmp[k:k + 1, :] = s8(dcw_ref, k)
        for k in range(KW):
            tmp[8 + k:9 + k, :] = s8(ddw_ref, k)
        for d in range(NDEV):
            small_ref[d, 0:LW, 0:128] = tmp[0:LW, 128 * d:128 * d + 128]
            small_ref[d, 8:8 + KW, 0:128] = tmp[8:8 + KW, 128 * d:128 * d + 128]
            small_ref[d, 40:56, :] = dh_ref[:, 256 * d:256 * d + 256]

    full = lambda a: pl.BlockSpec(a.shape, lambda i: (0,) * a.ndim)
    ins = [dprew_acc, dpostw_acc, cvecs, kvecs, lvecs, dcw_acc, ddw_acc]
    return pl.pallas_call(
        body, name="pack_grads",
        grid=(1,),
        in_specs=[full(a) for a in ins] + [full(dh), full(loss_acc)],
        out_specs=[pl.BlockSpec((_VEC_ROWS, 1024), lambda i: (0, 0)),
                   pl.BlockSpec((NDEV, _SM_ROWS, 256), lambda i: (0, 0, 0))],
        out_shape=[jax.ShapeDtypeStruct((_VEC_ROWS, 1024), F32), jax.ShapeDtypeStruct((NDEV, _SM_ROWS, 256), F32)],
        scratch_shapes=[pltpu.VMEM((40, 1024), F32)],
        compiler_params=_cparams(),
    )(*ins, dh, loss_acc)


def _adamw_vec(parts, W, M, V):
    nv = len(_VEC)

    def body(*refs):
        p_ref = refs[0]
        w_refs, m_refs, v_refs = refs[1:1 + nv], refs[1 + nv:1 + 2 * nv], refs[1 + 2 * nv:1 + 3 * nv]
        outs = refs[1 + 3 * nv:]

        def total(r):
            acc = p_ref[0, r:r + 1, :]
            for sidx in range(1, NDEV):
                acc = acc + p_ref[sidx, r:r + 1, :]
            return acc

        row = 0
        for idx, (_, nrows) in enumerate(_VEC):
            for part in range(nrows):
                cols = slice(1024 * part, 1024 * part + 1024)
                g = total(row + part)
                delta, mv, vv = _adam_math(g, w_refs[idx][:, cols], m_refs[idx][:, cols], v_refs[idx][:, cols])
                for o, val in zip(outs[4 * idx:4 * idx + 4], (g, delta, mv, vv)):
                    o[:, cols] = val
            row += nrows
        outs[-1][...] = total(_LOSS_ROW)[:, 0:128]

    names = [n for n, _ in _VEC]
    flat = lambda d: [d[n].reshape(1, -1) for n in names]
    ws, ms, vs = flat(W), flat(M), flat(V)
    res = pl.pallas_call(
        body, name="adamw_vec",
        out_shape=[jax.ShapeDtypeStruct(w.shape, F32) for w in ws for _ in range(4)]
        + [jax.ShapeDtypeStruct((1, 128), F32)],
        compiler_params=_cparams(),
    )(parts, *ws, *ms, *vs)
    return {n: tuple(res[4 * i:4 * i + 4]) for i, n in enumerate(names)}, res[-1]


def _adamw_small(parts, W, M, V):
    where = {"lru_conv_w": (slice(0, LW), slice(0, 128)), "conf_dw_w": (slice(8, 8 + KW), slice(0, 128)),
             "meta_tokens": (slice(40, 56), slice(0, 256))}
    names = list(where)

    def body(*refs):
        p_ref = refs[0]
        outs = refs[10:]
        for idx, n in enumerate(names):
            rs, cs = where[n]
            g = p_ref[0, rs, cs]
            for sidx in range(1, NDEV):
                g = g + p_ref[sidx, rs, cs]
            delta, mv, vv = _adam_math(g, refs[1 + idx][...], refs[4 + idx][...], refs[7 + idx][...])
            for o, val in zip(outs[4 * idx:4 * idx + 4], (g, delta, mv, vv)):
                o[...] = val

    two_d = lambda a: a.reshape(a.shape[-2:])
    ws, ms, vs = ([two_d(d[n]) for n in names] for d in (W, M, V))
    res = pl.pallas_call(
        body, name="adamw_small",
        out_shape=[jax.ShapeDtypeStruct(w.shape, F32) for w in ws for _ in range(4)],
        compiler_params=_cparams(),
    )(parts, *ws, *ms, *vs)
    return {n: tuple(res[4 * i:4 * i + 4]) for i, n in enumerate(names)}


def _pack_small(lru_cw, dw_w, meta):
    buf = jnp.zeros((_SM_ROWS, 256), F32)
    buf = buf.at[0:LW, 0:128].set(lru_cw)
    buf = buf.at[8:8 + dw_w.shape[0], 0:128].set(dw_w)
    return buf.at[40:56, :].set(meta)


def _block_diag4(w):
    w4 = w.reshape(NCB, 4, 64, 64)
    eye = jnp.eye(4, dtype=w.dtype)
    return jnp.einsum("ghij,hk->ghikj", w4, eye).reshape(NCB, CB, CB)


def _diag_blocks(g):
    g5 = g.reshape(NCB, 4, 64, 4, 64)
    return jnp.stack([g5[:, hh, :, hh, :] for hh in range(4)], axis=1).reshape(16, 64, 64)


def _local_step(x, target, meta_full, inproj, out_weights, lru_cw_full, dw_w_full, W, send):
    wa_g = _block_diag4(W["w_gate_a"][0]).astype(BF16)
    wx_g = _block_diag4(W["w_gate_x"][0]).astype(BF16)

    h, hn = _prenorm(x, meta_full, W["pre_norm_w"])
    z, win_full = inproj(hn)
    ylru, xc, hs = _lru_fwd(z, lru_cw_full, W["lru_conv_b"], wa_g, W["b_gate_a"], wx_g, W["b_gate_x"],
                            W["lru_lambda"])
    vc = _conf_fwd_conv(z, dw_w_full, W["conf_dw_b"])
    wout_full, pw_full = out_weights(vc)
    yconf, p = _conf_fwd_proj(vc, z, W["conf_ln_w"], W["conf_ln_b"], pw_full, W["conf_pw_b"])
    dout, dy, loss_acc, dpostw_acc = _outproj_loss(ylru, yconf, wout_full, h, target, W["post_norm_w"])

    dycat, dwout_part = _outproj_bwd(dy, ylru, yconf, wout_full)
    tok = send("w_out", dwout_part)
    dvc, dgc, dpw_part, cvecs = _conf_bwd_proj(dycat, p, z, vc, W["conf_ln_w"] + tok, W["conf_ln_b"], pw_full)
    tok = send("conf_pw_w", dpw_part)
    tok = tok + send("w_in_c", _inproj_wgrad("inproj_wgrad_c", hn, dgc[None]))
    dzc, ddw_acc, kvecs = _conf_bwd_conv(dvc, z, dw_w_full + tok)
    tok = send("w_in_b", _inproj_wgrad("inproj_wgrad_b", hn, dzc))
    dzl, dwa_g, dwx_g, dcw_acc, lvecs = _lru_bwd(dycat, z, xc, hs, lru_cw_full, wa_g, W["b_gate_a"] + tok, wx_g,
                                                 W["b_gate_x"], W["lru_lambda"])
    tok = send("w_in_a", _inproj_wgrad("inproj_wgrad_a", hn, dzl))
    tok = tok + send("w_gates", _diag_blocks(dwa_g).reshape(16 * 64, 64), _diag_blocks(dwx_g).reshape(16 * 64, 64))
    grad_x, dmeta, dprew_acc = _inproj_bwd(dzl, dzc, dgc, win_full, h, dout, W["pre_norm_w"] + tok)

    vec_pack, small_part = _pack_grads(dprew_acc, dpostw_acc, cvecs, kvecs, lvecs, dcw_acc, ddw_acc, dmeta, loss_acc)
    return grad_x, vec_pack, small_part


def kernel(x, meta_tokens, pre_norm_w, post_norm_w, w_in, b_in, lru_conv_w, lru_conv_b, w_gate_a, b_gate_a, w_gate_x, b_gate_x, lru_lambda, conf_dw_w, conf_dw_b, conf_ln_w, conf_ln_b, conf_pw_w, conf_pw_b, w_out, loss_target, m_meta_tokens, m_pre_norm_w, m_post_norm_w, m_w_in, m_b_in, m_lru_conv_w, m_lru_conv_b, m_w_gate_a, m_b_gate_a, m_w_gate_x, m_b_gate_x, m_lru_lambda, m_conf_dw_w, m_conf_dw_b, m_conf_ln_w, m_conf_ln_b, m_conf_pw_w, m_conf_pw_b, m_w_out, v_meta_tokens, v_pre_norm_w, v_post_norm_w, v_w_in, v_b_in, v_lru_conv_w, v_lru_conv_b, v_w_gate_a, v_b_gate_a, v_w_gate_x, v_b_gate_x, v_lru_lambda, v_conf_dw_w, v_conf_dw_b, v_conf_ln_w, v_conf_ln_b, v_conf_pw_w, v_conf_pw_b, v_w_out):
    W = dict(meta_tokens=meta_tokens, pre_norm_w=pre_norm_w, post_norm_w=post_norm_w, w_in=w_in, b_in=b_in,
             lru_conv_w=lru_conv_w, lru_conv_b=lru_conv_b, w_gate_a=w_gate_a, b_gate_a=b_gate_a,
             w_gate_x=w_gate_x, b_gate_x=b_gate_x, lru_lambda=lru_lambda, conf_dw_w=conf_dw_w,
             conf_dw_b=conf_dw_b, conf_ln_w=conf_ln_w, conf_ln_b=conf_ln_b, conf_pw_w=conf_pw_w,
             conf_pw_b=conf_pw_b, w_out=w_out)
    M = dict(meta_tokens=m_meta_tokens, pre_norm_w=m_pre_norm_w, post_norm_w=m_post_norm_w, w_in=m_w_in,
             b_in=m_b_in, lru_conv_w=m_lru_conv_w, lru_conv_b=m_lru_conv_b, w_gate_a=m_w_gate_a,
             b_gate_a=m_b_gate_a, w_gate_x=m_w_gate_x, b_gate_x=m_b_gate_x, lru_lambda=m_lru_lambda,
             conf_dw_w=m_conf_dw_w, conf_dw_b=m_conf_dw_b, conf_ln_w=m_conf_ln_w, conf_ln_b=m_conf_ln_b,
             conf_pw_w=m_conf_pw_w, conf_pw_b=m_conf_pw_b, w_out=m_w_out)
    V = dict(meta_tokens=v_meta_tokens, pre_norm_w=v_pre_norm_w, post_norm_w=v_post_norm_w, w_in=v_w_in,
             b_in=v_b_in, lru_conv_w=v_lru_conv_w, lru_conv_b=v_lru_conv_b, w_gate_a=v_w_gate_a,
             b_gate_a=v_b_gate_a, w_gate_x=v_w_gate_x, b_gate_x=v_b_gate_x, lru_lambda=v_lru_lambda,
             conf_dw_w=v_conf_dw_w, conf_dw_b=v_conf_dw_b, conf_ln_w=v_conf_ln_w, conf_ln_b=v_conf_ln_b,
             conf_pw_w=v_conf_pw_w, conf_pw_b=v_conf_pw_b, w_out=v_w_out)
    names = list(W.keys())
    shapes = {n: W[n].shape for n in names}

    small = _pack_small(lru_conv_w[0], conf_dw_w[0], meta_tokens)
    (small_flight,), tok = _exchange_start("gather_small_start", [
        (small, jax.ShapeDtypeStruct((NDEV, _SM_ROWS, 256), F32), _whole, _slot)])
    win_flight, tok = _win_gather_start(w_in[0].astype(BF16) + tok[0, 0].astype(BF16))
    win_flight, tok = _win_gather_links(win_flight, tok)
    gathered, tok = _exchange_start("gather_out_start", [
        (w_out[0].astype(BF16) + tok[0, 0].astype(BF16), jax.ShapeDtypeStruct((D, D), BF16), _whole,
         _rows(D // NDEV)),
        (conf_pw_w[0].astype(BF16), jax.ShapeDtypeStruct((DC, DC), BF16), _whole, _rows(DC // NDEV)),
    ])
    (small_all,) = _exchange_wait("gather_small_wait", [small_flight], tok)
    unshard = lambda a: jnp.transpose(a, (1, 0, 2)).reshape(a.shape[1], -1)
    lru_cw_full = unshard(small_all[:, 0:LW, 0:128])
    dw_w_full = unshard(small_all[:, 8:8 + KWP, 0:128])
    meta_full = unshard(small_all[:, 40:56, :])

    def out_weights(after):
        return _exchange_wait("gather_out_wait", gathered, after)

    def inproj(hn):
        xi, yi, ci = lax.axis_index("x"), lax.axis_index("y"), lax.axis_index("c")
        shard = lambda px, py, pc: (4 * px + 2 * py + pc).astype(jnp.int32)
        here = jnp.stack([shard(xi, yi, ci), shard(xi, yi, 1 - ci)])
        over_links = jnp.stack([shard(1 - xi, yi, ci), shard(xi, 1 - yi, ci), shard(1 - xi, 1 - yi, ci)])
        flight = _win_gather_early(win_flight)
        z, land = _inproj_cols("inproj_here", here, hn, flight["land"], b_in, None)
        flight = _win_gather_forward(dict(flight, land=land), z)
        z, land = _inproj_cols("inproj_links", over_links, hn, flight["land"], b_in, z)
        land = _win_gather_wait(dict(flight, land=land))
        return _inproj_cols("inproj_sibling", over_links + 1 - 2 * ci, hn, land, b_in, z)

    row_stage = lambda ncol: (jax.ShapeDtypeStruct((NDEV, D // NDEV, ncol), BF16), _rows(D // NDEV))
    piece = {"w_in_a": row_stage(2048), "w_in_b": row_stage(2048), "w_in_c": row_stage(1024),
             "w_out": row_stage(D),
             "conf_pw_w": (jax.ShapeDtypeStruct((NDEV, DC // NDEV, DC), BF16), _rows(DC // NDEV)),
             "w_gates": (jax.ShapeDtypeStruct((NDEV, 16 * 64, 64), BF16), _whole)}
    sent = {}

    def send(name, *parts):
        handles, token = _exchange_start(
            "scatter_" + name + "_start",
            [(part.astype(BF16), piece[name][0], piece[name][1], _slot) for part in parts])
        sent[name] = handles
        return token[0, 0]

    grad_x, vec_pack, small_part = _local_step(
        x[0], loss_target[0], meta_full, inproj, out_weights, lru_cw_full, dw_w_full, W, send)
    grad_x = grad_x[None]

    rest, tok = _exchange_start("scatter_rest_start", [
        (small_part, jax.ShapeDtypeStruct((NDEV, _SM_ROWS, 256), F32), _slot, _slot),
        (vec_pack, jax.ShapeDtypeStruct((NDEV, _VEC_ROWS, 1024), F32), _whole, _slot),
    ])
    (parts_c,) = _exchange_wait("scatter_w_in_c_wait", sent["w_in_c"], tok)
    (parts_b,) = _exchange_wait("scatter_w_in_b_wait", sent["w_in_b"], parts_c)
    (parts_a,) = _exchange_wait("scatter_w_in_a_wait", sent["w_in_a"], parts_b)
    win_rows = _sum_win_parts(parts_a, parts_b, parts_c)
    win_stage2, _ = _exchange_start("scatter_w_in_stage2_start", [
        (win_rows, jax.ShapeDtypeStruct((NDEV, D // NDEV, NIN // NDEV), BF16), _cols(NIN // NDEV), _slot)])

    G, DW, NM, NV = {}, {}, {}, {}
    (wout_parts,) = _exchange_wait("scatter_w_out_wait", sent["w_out"], win_rows)
    G["w_out"], DW["w_out"], NM["w_out"], NV["w_out"] = _adamw("adamw_w_out", wout_parts, w_out[0], m_w_out[0], v_w_out[0], 64)
    (pw_parts,) = _exchange_wait("scatter_conf_pw_w_wait", sent["conf_pw_w"], G["w_out"])
    G["conf_pw_w"], DW["conf_pw_w"], NM["conf_pw_w"], NV["conf_pw_w"] = _adamw(
        "adamw_pw", pw_parts, conf_pw_w[0], m_conf_pw_w[0], v_conf_pw_w[0], 128)
    res = {}
    wa_parts, wx_parts = _exchange_wait("scatter_w_gates_wait", sent["w_gates"], G["conf_pw_w"])
    for n, parts in (("w_gate_a", wa_parts), ("w_gate_x", wx_parts)):
        res[n] = _adamw("adamw_" + n, parts, *[d[n].reshape(16 * 64, 64) for d in (W, M, V)], 16 * 64)
    small_parts, vec_parts = _exchange_wait("scatter_rest_wait", rest, res["w_gate_x"][0])
    res.update(_adamw_small(small_parts, W, M, V))
    vec_res, loss_row = _adamw_vec(vec_parts, W, M, V)
    res.update(vec_res)
    (win_sum,) = _exchange_wait("scatter_w_in_stage2_wait", win_stage2, loss_row)
    res["w_in"] = _adamw("adamw_w_in", win_sum.reshape(1, D, NIN // NDEV), w_in[0], m_w_in[0], v_w_in[0], 256)
    for n, vals in res.items():
        for dst, val in zip((G, DW, NM, NV), vals):
            dst[n] = val
    for dst in (G, DW, NM, NV):
        for n in names:
            dst[n] = dst[n].reshape(shapes[n])
    loss = loss_row[0, 0]

    return (loss, grad_x, *[G[n] for n in names], *[DW[n] for n in names],
            *[NM[n] for n in names], *[NV[n] for n in names])
```

```python
import functools

import jax
import jax.numpy as jnp
from jax import lax
from jax.experimental import pallas as pl
from jax.experimental.pallas import tpu as pltpu

F32 = jnp.float32
BF16 = jnp.bfloat16

D = 2048
DL = 1024
DC = 1024
NIN = 5120
NMETA = 16
SEQ = 2048
T = NMETA + SEQ
TP = 2176
TM = 544
CB = 256
NCB = DL // CB
R = 16
KW = 31
KWP = 32
LW = 4
LRU_C = 8.0
EPS = 1e-6
NDEV = 8

ADAM_LR = 0.001
ADAM_B1 = 0.9
ADAM_B2 = 0.999
ADAM_EPS = 1e-08
ADAM_WD = 0.01
ADAM_STEP = 10

VMEM_LIMIT = 56 * 1024 * 1024


def _cparams():
    return pltpu.CompilerParams(vmem_limit_bytes=VMEM_LIMIT)


def _sig(x):
    return 1.0 / (1.0 + jnp.exp(-x))


def _expm1_neg(y):
    poly = y * (1.0 + y * (0.5 + y * (1.0 / 6.0 + y * (1.0 / 24.0 + y * (1.0 / 120.0)))))
    return jnp.where(y > -0.1, poly, jnp.exp(y) - 1.0)


def _softplus(x):
    e = jnp.exp(-jnp.abs(x))
    w = 1.0 + e
    l1p = jnp.where(w == 1.0, e, jnp.log(w) * e / (w - 1.0))
    return jnp.maximum(x, 0.0) + l1p


def _row_iota(shape):
    return lax.broadcasted_iota(jnp.int32, shape, 0)


def _fold8(v):
    return v[0:8, :] + v[8:16, :]


_FLIPS = [(k >> 2 & 1, k >> 1 & 1, k & 1) for k in range(1, NDEV)]
_HBM = pl.BlockSpec(memory_space=pltpu.HBM)
_SEM = pl.BlockSpec(memory_space=pltpu.SEMAPHORE)


def _peers():
    x, y, c = lax.axis_index("x"), lax.axis_index("y"), lax.axis_index("c")
    out = []
    for dx, dy, dc in _FLIPS:
        px = 1 - x if dx else x
        py = 1 - y if dy else y
        pc = 1 - c if dc else c
        out.append(((px, py, pc), 4 * px + 2 * py + pc))
    return 4 * x + 2 * y + c, out


def _exchange_start(name, items):
    n = len(items)

    def body(*refs):
        srcs, lands = refs[:n], refs[n:2 * n]
        outs = refs[2 * n:]
        send_sems, recv_sems, local_sems = outs[:n], outs[n:2 * n], outs[2 * n:3 * n]
        token = outs[-1]
        me, peers = _peers()
        for a in range(n):
            src_at, dst_at = items[a][2], items[a][3]
            pltpu.make_async_copy(src_at(srcs[a], me), dst_at(lands[a], me), local_sems[a]).start()
        for a in range(n):
            src_at, dst_at = items[a][2], items[a][3]
            for k, (pos, peer) in enumerate(peers):
                pltpu.make_async_remote_copy(
                    src_ref=src_at(srcs[a], peer), dst_ref=dst_at(lands[a], me),
                    send_sem=send_sems[a].at[k], recv_sem=recv_sems[a].at[k],
                    device_id=pos, device_id_type=pl.DeviceIdType.MESH).start()
        token[...] = jnp.zeros_like(token)

    srcs = [pltpu.with_memory_space_constraint(it[0], pltpu.HBM) for it in items]
    lands = [pltpu.with_memory_space_constraint(lax.empty(it[1].shape, it[1].dtype), pltpu.HBM) for it in items]
    sem7 = pltpu.SemaphoreType.DMA((NDEV - 1,))
    res = pl.pallas_call(
        body, name=name,
        out_shape=([sem7] * (2 * n) + [pltpu.SemaphoreType.DMA(())] * n
                   + [pltpu.HBM(a.shape, a.dtype) for a in srcs] + [pltpu.HBM(a.shape, a.dtype) for a in lands]
                   + [jax.ShapeDtypeStruct((8, 128), F32)]),
        in_specs=[_HBM] * (2 * n),
        out_specs=[_SEM] * (3 * n) + [_HBM] * (2 * n) + [pl.BlockSpec(memory_space=pltpu.VMEM)],
        input_output_aliases={i: 3 * n + i for i in range(2 * n)},
        compiler_params=pltpu.CompilerParams(has_side_effects=pltpu.SideEffectType.DATAFLOW_SIDE_EFFECTING),
    )(*srcs, *lands)
    handles = [dict(send=res[a], recv=res[n + a], local=res[2 * n + a], src=res[3 * n + a], land=res[4 * n + a],
                    src_at=items[a][2], dst_at=items[a][3]) for a in range(n)]
    return handles, res[-1]


def _wait_bytes(piece, sem):
    pltpu.make_async_copy(piece, piece, sem).wait()


def _exchange_wait(name, handles, after):
    n = len(handles)

    def body(*refs):
        srcs, lands = refs[:n], refs[n:2 * n]
        send_sems, recv_sems, local_sems = refs[2 * n:3 * n], refs[3 * n:4 * n], refs[4 * n:5 * n]
        me, peers = _peers()
        for a in range(n):
            src_at, dst_at = handles[a]["src_at"], handles[a]["dst_at"]
            for k, (pos, peer) in enumerate(peers):
                _wait_bytes(src_at(srcs[a], peer), send_sems[a].at[k])
                _wait_bytes(dst_at(lands[a], peer), recv_sems[a].at[k])
            pltpu.make_async_copy(src_at(srcs[a], me), dst_at(lands[a], me), local_sems[a]).wait()

    srcs = [hd["src"] for hd in handles]
    lands = [hd["land"] for hd in handles]
    res = pl.pallas_call(
        body, name=name,
        out_shape=[pltpu.HBM(a.shape, a.dtype) for a in srcs] + [pltpu.HBM(a.shape, a.dtype) for a in lands],
        in_specs=[_HBM] * (2 * n) + [_SEM] * (3 * n) + [pl.BlockSpec(memory_space=pl.ANY)],
        out_specs=[_HBM] * (2 * n),
        input_output_aliases={i: i for i in range(2 * n)},
        compiler_params=pltpu.CompilerParams(has_side_effects=pltpu.SideEffectType.DATAFLOW_SIDE_EFFECTING),
    )(*srcs, *lands, *[hd["send"] for hd in handles], *[hd["recv"] for hd in handles],
      *[hd["local"] for hd in handles], after)
    return list(res[n:])


_SIDE = pltpu.SideEffectType.DATAFLOW_SIDE_EFFECTING
_WCOLS = NIN // NDEV


def _win_cols(ref, l):
    return ref.at[:, pl.ds(pl.multiple_of(l * _WCOLS, 128), _WCOLS)]


def _win_routes():
    x, y, c = lax.axis_index("x"), lax.axis_index("y"), lax.axis_index("c")
    pos = [(x, y, 1 - c), (1 - x, y, c), (x, 1 - y, c), (1 - x, 1 - y, c)]
    return 4 * x + 2 * y + c, [(p, 4 * p[0] + 2 * p[1] + p[2]) for p in pos]


def _win_gather_start(shard):
    def body(src, land, send_sem, recv_sem, local_sem, src_thru, land_thru, token):
        me, routes = _win_routes()
        pltpu.make_async_copy(src, _win_cols(land, me), local_sem).start()
        pltpu.make_async_remote_copy(src_ref=src, dst_ref=_win_cols(land, me), send_sem=send_sem, recv_sem=recv_sem,
                                     device_id=routes[0][0], device_id_type=pl.DeviceIdType.MESH).start()
        token[...] = jnp.zeros_like(token)

    src = pltpu.with_memory_space_constraint(shard, pltpu.HBM)
    land = pltpu.with_memory_space_constraint(lax.empty((D, NIN), BF16), pltpu.HBM)
    sem = pltpu.SemaphoreType.DMA(())
    res = pl.pallas_call(
        body, name="win_gather_start",
        out_shape=[sem, sem, sem, pltpu.HBM(src.shape, BF16), pltpu.HBM(land.shape, BF16),
                   jax.ShapeDtypeStruct((8, 128), F32)],
        in_specs=[_HBM, _HBM],
        out_specs=[_SEM, _SEM, _SEM, _HBM, _HBM, pl.BlockSpec(memory_space=pltpu.VMEM)],
        input_output_aliases={0: 3, 1: 4},
        compiler_params=pltpu.CompilerParams(has_side_effects=_SIDE),
    )(src, land)
    return dict(send0=res[0], recv0=res[1], local=res[2], src=res[3], land=res[4]), res[5]


def _win_gather_links(hd, after):
    def body(src, land, after_ref, send_sems, recv_sems, src_thru, land_thru, token):
        me, routes = _win_routes()
        for k in (1, 2, 3):
            pltpu.make_async_remote_copy(src_ref=src, dst_ref=_win_cols(land, me), send_sem=send_sems.at[k - 1],
                                         recv_sem=recv_sems.at[k - 1], device_id=routes[k][0],
                                         device_id_type=pl.DeviceIdType.MESH).start()
        token[...] = jnp.zeros_like(token)

    sem3 = pltpu.SemaphoreType.DMA((3,))
    res = pl.pallas_call(
        body, name="win_gather_links",
        out_shape=[sem3, sem3, pltpu.HBM(hd["src"].shape, BF16), pltpu.HBM(hd["land"].shape, BF16),
                   jax.ShapeDtypeStruct((8, 128), F32)],
        in_specs=[_HBM, _HBM, pl.BlockSpec(memory_space=pl.ANY)],
        out_specs=[_SEM, _SEM, _HBM, _HBM, pl.BlockSpec(memory_space=pltpu.VMEM)],
        input_output_aliases={0: 2, 1: 3},
        compiler_params=pltpu.CompilerParams(has_side_effects=_SIDE),
    )(hd["src"], hd["land"], after)
    return dict(hd, send=res[0], recv=res[1], src=res[2], land=res[3]), res[4]


def _win_gather_forward(hd, after):
    def body(land, recv_sems, after_ref, land_thru, fsend_sems, frecv_sems):
        me, routes = _win_routes()
        sibling = routes[0][0]
        for k in (1, 2, 3):
            pos, peer = routes[k]
            piece = _win_cols(land, peer)
            pltpu.make_async_remote_copy(src_ref=piece, dst_ref=piece, send_sem=fsend_sems.at[k - 1],
                                         recv_sem=recv_sems.at[k - 1], device_id=pos,
                                         device_id_type=pl.DeviceIdType.MESH).wait_recv()
            pltpu.make_async_remote_copy(src_ref=piece, dst_ref=piece, send_sem=fsend_sems.at[k - 1],
                                         recv_sem=frecv_sems.at[k - 1], device_id=sibling,
                                         device_id_type=pl.DeviceIdType.MESH).start()

    sem3 = pltpu.SemaphoreType.DMA((3,))
    res = pl.pallas_call(
        body, name="win_gather_forward",
        out_shape=[pltpu.HBM(hd["land"].shape, BF16), sem3, sem3],
        in_specs=[_HBM, _SEM, pl.BlockSpec(memory_space=pl.ANY)],
        out_specs=[_HBM, _SEM, _SEM],
        input_output_aliases={0: 0},
        compiler_params=pltpu.CompilerParams(has_side_effects=_SIDE),
    )(hd["land"], hd["recv"], after)
    return dict(hd, land=res[0], fsend=res[1], frecv=res[2])


def _win_gather_early(hd):
    def body(src, land, recv_sem, local_sem, src_thru, land_thru):
        me, routes = _win_routes()
        _wait_bytes(_win_cols(land, routes[0][1]), recv_sem)
        pltpu.make_async_copy(src, _win_cols(land, me), local_sem).wait()

    res = pl.pallas_call(
        body, name="win_gather_early",
        out_shape=[pltpu.HBM(hd["src"].shape, BF16), pltpu.HBM(hd["land"].shape, BF16)],
        in_specs=[_HBM, _HBM, _SEM, _SEM],
        out_specs=[_HBM, _HBM],
        input_output_aliases={0: 0, 1: 1},
        compiler_params=pltpu.CompilerParams(has_side_effects=_SIDE),
    )(hd["src"], hd["land"], hd["recv0"], hd["local"])
    return dict(hd, src=res[0], land=res[1])


def _win_gather_wait(hd):
    def body(src, land, send0_sem, send_sems, fsend_sems, frecv_sems, src_thru, land_thru):
        me, routes = _win_routes()
        sib_pos, sibling = routes[0]
        for k in range(4):
            _wait_bytes(src, send0_sem if k == 0 else send_sems.at[k - 1])
        for k in (1, 2, 3):
            _wait_bytes(_win_cols(land, routes[k][1]), fsend_sems.at[k - 1])
            _wait_bytes(_win_cols(land, 4 * routes[k][0][0] + 2 * routes[k][0][1] + sib_pos[2]), frecv_sems.at[k - 1])

    res = pl.pallas_call(
        body, name="win_gather_wait",
        out_shape=[pltpu.HBM(hd["src"].shape, BF16), pltpu.HBM(hd["land"].shape, BF16)],
        in_specs=[_HBM, _HBM] + [_SEM] * 4,
        out_specs=[_HBM, _HBM],
        input_output_aliases={0: 0, 1: 1},
        compiler_params=pltpu.CompilerParams(has_side_effects=_SIDE),
    )(hd["src"], hd["land"], hd["send0"], hd["send"], hd["fsend"], hd["frecv"])
    return res[1]


def _whole(ref, l):
    return ref


def _slot(ref, l):
    return ref.at[l]


def _cols(width):
    def at(ref, l):
        return ref.at[:, pl.ds(pl.multiple_of(l * width, 128), width)]
    return at


def _rows(height):
    def at(ref, l):
        return ref.at[pl.ds(pl.multiple_of(l * height, 8), height), :]
    return at


NTILE = TP // TM


def _tile_rows(t):
    lo = max(t * TM - NMETA, 0)
    hi = min((t + 1) * TM - NMETA, SEQ)
    return lo, hi - lo, lo + NMETA - t * TM


def _for_tile(t, fn):
    for static_t in range(NTILE):
        pl.when(t == static_t)(functools.partial(fn, static_t))


def _token_tile_copy(hbm_ref, buf, sem, t):
    lo, n, off = _tile_rows(t)
    return pltpu.make_async_copy(hbm_ref.at[pl.ds(lo, n)], buf.at[pl.ds(off, n)], sem)


def _prenorm(x, meta_full, pre_w):
    def body(x_ref, meta_ref, pw_ref, h_ref, hn_ref, xbuf, sems):
        i = pl.program_id(0)
        slot = i % 2

        def start(t):
            _token_tile_copy(x_ref, xbuf.at[t % 2], sems.at[t % 2], t).start()

        @pl.when(i == 0)
        def _():
            start(0)
        _for_tile(i + 1, start)
        _for_tile(i, lambda t: _token_tile_copy(x_ref, xbuf.at[t % 2], sems.at[t % 2], t).wait())

        @pl.when(i == 0)
        def _():
            xbuf[0, 0:NMETA, :] = meta_ref[...]

        @pl.when(i == NTILE - 1)
        def _():
            last = _tile_rows(NTILE - 1)[1]
            xbuf[(NTILE - 1) % 2, last:TM, :] = jnp.zeros((TM - last, D), F32)

        pw = pw_ref[...]

        def chunk(ci, carry):
            r0 = pl.multiple_of(ci * R, R)
            xv = xbuf[slot, pl.ds(r0, R), :]
            h_ref[pl.ds(r0, R), :] = xv
            ms = jnp.mean(xv * xv, axis=-1, keepdims=True)
            hn_ref[pl.ds(r0, R), :] = (xv * lax.rsqrt(ms + EPS) * pw).astype(BF16)
            return carry
        lax.fori_loop(0, TM // R, chunk, 0, unroll=2)

    row = pl.BlockSpec((TM, D), lambda i: (i, 0))
    return pl.pallas_call(
        body, name="prenorm",
        grid=(NTILE,),
        in_specs=[pl.BlockSpec(memory_space=pl.ANY), pl.BlockSpec((NMETA, D), lambda i: (0, 0)),
                  pl.BlockSpec((1, D), lambda i: (0, 0))],
        out_specs=[row, row],
        out_shape=[jax.ShapeDtypeStruct((TP, D), F32), jax.ShapeDtypeStruct((TP, D), BF16)],
        scratch_shapes=[pltpu.VMEM((2, TM, D), F32), pltpu.SemaphoreType.DMA((2,))],
        compiler_params=_cparams(),
    )(x, meta_full, pre_w)


def _inproj_cols(name, shards, hn, w_land, b_in, z_prev):
    nsh = shards.shape[0]

    def body(idx_ref, hn_ref, w_ref, b_ref, *rest):
        z_ref = rest[-2]
        z_ref[...] = jnp.dot(hn_ref[...], w_ref[...], preferred_element_type=F32) + b_ref[...]

    any_spec = pl.BlockSpec(memory_space=pl.ANY)
    in_specs = [pl.BlockSpec((TM, D), lambda j, i, idx: (i, 0)),
                pl.BlockSpec((D, _WCOLS), lambda j, i, idx: (0, idx[j])),
                pl.BlockSpec((1, _WCOLS), lambda j, i, idx: (0, idx[j]))]
    operands = [hn, w_land, b_in]
    aliases = {2: 1}
    if z_prev is not None:
        in_specs.append(any_spec)
        operands.append(z_prev)
        aliases[4] = 0
    return pl.pallas_call(
        body, name=name,
        grid_spec=pltpu.PrefetchScalarGridSpec(
            num_scalar_prefetch=1, grid=(nsh, TP // TM), in_specs=in_specs,
            out_specs=[pl.BlockSpec((TM, _WCOLS), lambda j, i, idx: (i, idx[j])), any_spec]),
        out_shape=[jax.ShapeDtypeStruct((TP, NIN), F32), jax.ShapeDtypeStruct(w_land.shape, w_land.dtype)],
        input_output_aliases=aliases,
        compiler_params=_cparams(),
    )(shards, *operands)


def _gate_values(ga, gx, xc, sp8):
    r = _sig(ga)
    i = _sig(gx)
    log_a = -(r * sp8)
    a = jnp.exp(log_a)
    mult = jnp.sqrt(-_expm1_neg(2.0 * log_a))
    return r, i, a, mult


def _lru_fwd(z, conv_w, conv_b, wa_g, b_a, wx_g, b_x, lam):
    def body(x_ref, g_ref, cw_ref, cb_ref, wa_ref, ba_ref, wx_ref, bx_ref, lam_ref,
             y_ref, xc_ref, hs_ref, ga_s, gx_s):
        taps = [cw_ref[k:k + 1, :] for k in range(LW)]
        cb = cb_ref[...]

        def conv_chunk(ci, carry):
            r0 = pl.multiple_of(ci * R, R)
            cur = x_ref[pl.ds(r0, R), :]
            p0 = pl.multiple_of(jnp.maximum(r0 - 8, 0), 8)
            prev = jnp.where(ci > 0, x_ref[pl.ds(p0, 8), :], 0.0)
            buf = jnp.concatenate([prev, cur], axis=0)
            acc = cur * taps[LW - 1] + cb
            for s in range(1, LW):
                acc = acc + pltpu.roll(buf, s, 0)[8:8 + R, :] * taps[LW - 1 - s]
            xc_ref[pl.ds(r0, R), :] = acc
            return carry
        lax.fori_loop(0, TP // R, conv_chunk, 0)

        def gate_chunk(ci, carry):
            r0 = pl.multiple_of(ci * TM, TM)
            xb = xc_ref[pl.ds(r0, TM), :].astype(BF16)
            ga_s[pl.ds(r0, TM), :] = jnp.dot(xb, wa_ref[...], preferred_element_type=F32) + ba_ref[...]
            gx_s[pl.ds(r0, TM), :] = jnp.dot(xb, wx_ref[...], preferred_element_type=F32) + bx_ref[...]
            return carry
        lax.fori_loop(0, TP // TM, gate_chunk, 0)

        sp8 = LRU_C * _softplus(-lam_ref[...])
        row = _row_iota((R, CB))

        def scan_chunk(ci, hprev):
            r0 = pl.multiple_of(ci * R, R)
            xc = xc_ref[pl.ds(r0, R), :]
            _, i, a, mult = _gate_values(ga_s[pl.ds(r0, R), :], gx_s[pl.ds(r0, R), :], xc, sp8)
            u = mult * (i * xc)
            k = 1
            while k < R:
                m = row >= k
                u = jnp.where(m, a * pltpu.roll(u, k, 0) + u, u)
                a = jnp.where(m, a * pltpu.roll(a, k, 0), a)
                k *= 2
            hv = u + a * hprev
            hs_ref[pl.ds(r0, R), :] = hv
            g = g_ref[pl.ds(r0, R), :]
            y_ref[pl.ds(r0, R), :] = (hv * (g * _sig(g))).astype(BF16)
            return jnp.sum(jnp.where(row == R - 1, hv, 0.0), axis=0, keepdims=True)
        lax.fori_loop(0, TP // R, scan_chunk, jnp.zeros((1, CB), F32))

    col = lambda off: pl.BlockSpec((TP, CB), lambda j: (0, off + j))
    vec = pl.BlockSpec((1, CB), lambda j: (0, j))
    wsp = pl.BlockSpec((None, CB, CB), lambda j: (j, 0, 0))
    return pl.pallas_call(
        body, name="lru_fwd",
        grid=(NCB,),
        in_specs=[col(0), col(NCB), pl.BlockSpec((LW, CB), lambda j: (0, j)), vec, wsp, vec, wsp, vec, vec],
        out_specs=[col(0), col(0), col(0)],
        out_shape=[jax.ShapeDtypeStruct((TP, DL), BF16), jax.ShapeDtypeStruct((TP, DL), F32),
                   jax.ShapeDtypeStruct((TP, DL), F32)],
        scratch_shapes=[pltpu.VMEM((TP, CB), F32), pltpu.VMEM((TP, CB), F32)],
        compiler_params=_cparams(),
    )(z, z, conv_w, conv_b, wa_g, b_a, wx_g, b_x, lam)


def _conf_fwd_conv(z, dw_w, dw_b):
    def body(u1_ref, u2_ref, w_ref, b_ref, vc_ref, vs):
        vs[pl.ds(0, KWP), :] = jnp.zeros((KWP, CB), F32)

        def glu_chunk(ci, carry):
            r0 = pl.multiple_of(ci * R, R)
            vs[pl.ds(KWP + r0, R), :] = u1_ref[pl.ds(r0, R), :] * _sig(u2_ref[pl.ds(r0, R), :])
            return carry
        lax.fori_loop(0, TP // R, glu_chunk, 0)

        bias = b_ref[...]

        def conv_chunk(ci, carry):
            r0 = pl.multiple_of(ci * R, R)
            buf = vs[pl.ds(r0, KWP + R), :]
            acc = jnp.zeros((R, CB), F32) + bias
            for rr in range(8):
                rolled = buf if rr == 0 else pltpu.roll(buf, rr, 0)
                for q in range(4):
                    s = 8 * q + rr
                    if s > KW - 1:
                        continue
                    k = KW - 1 - s
                    acc = acc + rolled[KWP - 8 * q:KWP - 8 * q + R, :] * w_ref[k:k + 1, :]
            vc_ref[pl.ds(r0, R), :] = acc
            return carry
        lax.fori_loop(0, TP // R, conv_chunk, 0)

    return pl.pallas_call(
        body, name="conf_fwd_conv",
        grid=(NCB,),
        in_specs=[pl.BlockSpec((TP, CB), lambda j: (0, 2 * NCB + j)),
                  pl.BlockSpec((TP, CB), lambda j: (0, 3 * NCB + j)),
                  pl.BlockSpec((KWP, CB), lambda j: (0, j)),
                  pl.BlockSpec((1, CB), lambda j: (0, j))],
        out_specs=pl.BlockSpec((TP, CB), lambda j: (0, j)),
        out_shape=jax.ShapeDtypeStruct((TP, DC), F32),
        scratch_shapes=[pltpu.VMEM((TP + KWP, CB), F32)],
        compiler_params=_cparams(),
    )(z, z, dw_w, dw_b)


def _ln_chunk(vc, lw, lb):
    mu = jnp.mean(vc, axis=-1, keepdims=True)
    xm = vc - mu
    var = jnp.mean(xm * xm, axis=-1, keepdims=True)
    rstd = lax.rsqrt(var + EPS)
    xhat = xm * rstd
    return xhat, rstd, xhat * lw + lb


def _conf_fwd_proj(vc, z, ln_w, ln_b, pw_w, pw_b):
    def body(vc_ref, g_ref, lw_ref, lb_ref, w_ref, b_ref, y_ref, p_ref, s_s):
        lw, lb = lw_ref[...], lb_ref[...]

        def ln_chunk(ci, carry):
            r0 = pl.multiple_of(ci * R, R)
            for half in range(2):
                rr = r0 + 8 * half
                _, _, ln = _ln_chunk(vc_ref[pl.ds(rr, 8), :], lw, lb)
                p_ref[pl.ds(rr, 8), :] = ln * _sig(ln)
            s_s[pl.ds(r0, R), :] = p_ref[pl.ds(r0, R), :].astype(BF16)
            return carry
        lax.fori_loop(0, TM // R, ln_chunk, 0, unroll=2)

        p_ref[...] = jnp.dot(s_s[...], w_ref[...], preferred_element_type=F32) + b_ref[...]

        def out_chunk(ci, carry):
            r0 = pl.multiple_of(ci * R, R)
            g = g_ref[pl.ds(r0, R), :]
            y_ref[pl.ds(r0, R), :] = (p_ref[pl.ds(r0, R), :] * (g * _sig(g))).astype(BF16)
            return carry
        lax.fori_loop(0, TM // R, out_chunk, 0)

    row = pl.BlockSpec((TM, DC), lambda i: (i, 0))
    vec = pl.BlockSpec((1, DC), lambda i: (0, 0))
    return pl.pallas_call(
        body, name="conf_fwd_proj",
        grid=(TP // TM,),
        in_specs=[row, pl.BlockSpec((TM, DC), lambda i: (i, 4)), vec, vec,
                  pl.BlockSpec((DC, DC), lambda i: (0, 0)), vec],
        out_specs=[row, row],
        out_shape=[jax.ShapeDtypeStruct((TP, DC), BF16), jax.ShapeDtypeStruct((TP, DC), F32)],
        scratch_shapes=[pltpu.VMEM((TM, DC), BF16)],
        compiler_params=_cparams(),
    )(vc, z, ln_w, ln_b, pw_w, pw_b)


def _outproj_loss(ylru, yconf, w_out, h, target, post_w):
    def body(yl_ref, yc_ref, w_ref, h_ref, tgt_hbm, pw_ref, dout_ref, dy_ref, loss_ref, dpw_ref, y_s, t_ref, sem):
        i = pl.program_id(0)
        k = pl.program_id(1)

        @pl.when(k == 0)
        def _():
            _for_tile(i, lambda t: _token_tile_copy(tgt_hbm, t_ref, sem, t).start())
            y_s[...] = jnp.dot(yl_ref[...], w_ref[...], preferred_element_type=F32)

        @pl.when(k == 1)
        def _():
            y_s[...] += jnp.dot(yc_ref[...], w_ref[...], preferred_element_type=F32)

        @pl.when(jnp.logical_and(i == 0, k == 1))
        def _():
            loss_ref[...] = jnp.zeros_like(loss_ref)
            dpw_ref[...] = jnp.zeros_like(dpw_ref)

        @pl.when(k == 1)
        def _():
            _for_tile(i, lambda t: _token_tile_copy(tgt_hbm, t_ref, sem, t).wait())

            @pl.when(i == 0)
            def _():
                t_ref[0:NMETA, :] = jnp.zeros((NMETA, D), F32)

            @pl.when(i == NTILE - 1)
            def _():
                last = _tile_rows(NTILE - 1)[1]
                t_ref[last:TM, :] = jnp.zeros((TM - last, D), F32)

            pw = pw_ref[...]
            row = _row_iota((8, D))

            def chunk(ci, carry):
                r0 = pl.multiple_of(ci * 8, 8)
                yv = y_s[pl.ds(r0, 8), :]
                rs = lax.rsqrt(jnp.mean(yv * yv, axis=-1, keepdims=True) + EPS)
                grow = row + (i * TM + r0)
                valid = jnp.logical_and(grow >= NMETA, grow < T)
                yn = yv * rs
                err = jnp.where(valid, h_ref[pl.ds(r0, 8), :] + yn * pw - t_ref[pl.ds(r0, 8), :], 0.0)
                loss_ref[...] += err * err
                d_rn = err * (1.0 / D)
                dout_ref[pl.ds(r0, 8), :] = d_rn
                dpw_ref[...] += d_rn * yn
                gw = d_rn * pw
                dot = jnp.mean(gw * yv, axis=-1, keepdims=True)
                dy_ref[pl.ds(r0, 8), :] = (rs * gw - yv * (rs * rs * rs * dot)).astype(BF16)
                return carry
            lax.fori_loop(0, TM // 8, chunk, 0, unroll=4)

    row = pl.BlockSpec((TM, D), lambda i, k: (i, 0))
    half = pl.BlockSpec((TM, DL), lambda i, k: (i, 0))
    acc = pl.BlockSpec((8, D), lambda i, k: (0, 0))
    return pl.pallas_call(
        body, name="outproj_loss",
        grid=(TP // TM, 2),
        in_specs=[half, half, pl.BlockSpec((DL, D), lambda i, k: (k, 0)), row, pl.BlockSpec(memory_space=pl.ANY),
                  pl.BlockSpec((1, D), lambda i, k: (0, 0))],
        out_specs=[row, row, acc, acc],
        out_shape=[jax.ShapeDtypeStruct((TP, D), F32), jax.ShapeDtypeStruct((TP, D), BF16),
                   jax.ShapeDtypeStruct((8, D), F32), jax.ShapeDtypeStruct((8, D), F32)],
        scratch_shapes=[pltpu.VMEM((TM, D), F32), pltpu.VMEM((TM, D), F32), pltpu.SemaphoreType.DMA(())],
        compiler_params=_cparams(),
    )(ylru, yconf, w_out, h, target, post_w)


_NT = (((1,), (1,)), ((), ()))
_TN = (((0,), (0,)), ((), ()))


def _outproj_bwd(dy, ylru, yconf, w_out):
    def body(dy_ref, yl_ref, yc_ref, w_ref, dycat_ref, dw_ref):
        j = pl.program_id(0)
        dyv = dy_ref[...]
        dycat_ref[...] = lax.dot_general(dyv, w_ref[...], _NT, preferred_element_type=F32)

        @pl.when(j < NCB)
        def _():
            dw_ref[...] = lax.dot_general(yl_ref[...], dyv, _TN, preferred_element_type=F32).astype(BF16)

        @pl.when(j >= NCB)
        def _():
            dw_ref[...] = lax.dot_general(yc_ref[...], dyv, _TN, preferred_element_type=F32).astype(BF16)

    return pl.pallas_call(
        body, name="outproj_bwd",
        grid=(2 * NCB,),
        in_specs=[pl.BlockSpec((TP, D), lambda j: (0, 0)),
                  pl.BlockSpec((TP, CB), lambda j: (0, jnp.minimum(j, NCB - 1))),
                  pl.BlockSpec((TP, CB), lambda j: (0, jnp.maximum(j - NCB, 0))),
                  pl.BlockSpec((CB, D), lambda j: (j, 0))],
        out_specs=[pl.BlockSpec((TP, CB), lambda j: (0, j)), pl.BlockSpec((CB, D), lambda j: (j, 0))],
        out_shape=[jax.ShapeDtypeStruct((TP, D), F32), jax.ShapeDtypeStruct((D, D), BF16)],
        compiler_params=_cparams(),
    )(dy, ylru, yconf, w_out)


def _conf_bwd_proj(dycat, p, z, vc, ln_w, ln_b, pw_w):
    def body(dy_ref, p_ref, g_ref, vc_ref, lw_ref, lb_ref, w_ref,
             dvc_ref, dgc_ref, dpw_ref, vecs_ref, dp_s, s_s, ds_s):
        i = pl.program_id(0)
        lw, lb = lw_ref[...], lb_ref[...]

        @pl.when(i == 0)
        def _():
            dpw_ref[...] = jnp.zeros_like(dpw_ref)
            vecs_ref[...] = jnp.zeros_like(vecs_ref)

        def pre_chunk(ci, carry):
            r0 = pl.multiple_of(ci * R, R)
            for half in range(2):
                rr = r0 + 8 * half
                dyv = dy_ref[pl.ds(rr, 8), :]
                g = g_ref[pl.ds(rr, 8), :]
                sg = _sig(g)
                dp = dyv * (g * sg)
                dg = dyv * p_ref[pl.ds(rr, 8), :] * (sg * (1.0 + g * (1.0 - sg)))
                vecs_ref[0:8, :] += dp
                vecs_ref[8:16, :] += dg
                ds_s[pl.ds(rr, 8), :] = dp
                dvc_ref[pl.ds(rr, 8), :] = dg
            dp_s[pl.ds(r0, R), :] = ds_s[pl.ds(r0, R), :].astype(BF16)
            dgc_ref[pl.ds(r0, R), :] = dvc_ref[pl.ds(r0, R), :].astype(BF16)
            for half in range(2):
                rr = r0 + 8 * half
                _, _, ln = _ln_chunk(vc_ref[pl.ds(rr, 8), :], lw, lb)
                ds_s[pl.ds(rr, 8), :] = ln * _sig(ln)
            s_s[pl.ds(r0, R), :] = ds_s[pl.ds(r0, R), :].astype(BF16)
            return carry
        lax.fori_loop(0, TM // R, pre_chunk, 0, unroll=2)

        dpb = dp_s[...]
        ds_s[...] = lax.dot_general(dpb, w_ref[...], _NT, preferred_element_type=F32)
        dpw_ref[...] += lax.dot_general(s_s[...], dpb, _TN, preferred_element_type=F32)

        def post_chunk(ci, carry):
            r0 = pl.multiple_of(ci * 8, 8)
            xhat, rstd, ln = _ln_chunk(vc_ref[pl.ds(r0, 8), :], lw, lb)
            sl = _sig(ln)
            dln = ds_s[pl.ds(r0, 8), :] * (sl * (1.0 + ln * (1.0 - sl)))
            vecs_ref[16:24, :] += dln * xhat
            vecs_ref[24:32, :] += dln
            dxh = dln * lw
            m1 = jnp.mean(dxh, axis=-1, keepdims=True)
            m2 = jnp.mean(dxh * xhat, axis=-1, keepdims=True)
            dvc_ref[pl.ds(r0, 8), :] = rstd * (dxh - m1 - xhat * m2)
            return carry
        lax.fori_loop(0, TM // 8, post_chunk, 0, unroll=4)

    row = pl.BlockSpec((TM, DC), lambda i: (i, 0))
    vec = pl.BlockSpec((1, DC), lambda i: (0, 0))
    return pl.pallas_call(
        body, name="conf_bwd_proj",
        grid=(TP // TM,),
        in_specs=[pl.BlockSpec((TM, DC), lambda i: (i, 1)), row, pl.BlockSpec((TM, DC), lambda i: (i, 4)), row,
                  vec, vec, pl.BlockSpec((DC, DC), lambda i: (0, 0))],
        out_specs=[row, row, pl.BlockSpec((DC, DC), lambda i: (0, 0)), pl.BlockSpec((32, DC), lambda i: (0, 0))],
        out_shape=[jax.ShapeDtypeStruct((TP, DC), F32), jax.ShapeDtypeStruct((TP, DC), BF16),
                   jax.ShapeDtypeStruct((DC, DC), F32), jax.ShapeDtypeStruct((32, DC), F32)],
        scratch_shapes=[pltpu.VMEM((TM, DC), BF16), pltpu.VMEM((TM, DC), BF16), pltpu.VMEM((TM, DC), F32)],
        compiler_params=_cparams(),
    )(dycat, p, z, vc, ln_w, ln_b, pw_w)


def _conf_bwd_conv(dvc, z, dw_w):
    def body(dvc_ref, u1_ref, u2_ref, w_ref, du_ref, dw_ref, vecs_ref, vs, dvs):
        vs[pl.ds(0, KWP), :] = jnp.zeros((KWP, CB), F32)
        dvs[pl.ds(TP, KWP), :] = jnp.zeros((KWP, CB), F32)
        dw_ref[...] = jnp.zeros_like(dw_ref)
        vecs_ref[...] = jnp.zeros_like(vecs_ref)

        def fill_chunk(ci, carry):
            r0 = pl.multiple_of(ci * R, R)
            vs[pl.ds(KWP + r0, R), :] = u1_ref[pl.ds(r0, R), :] * _sig(u2_ref[pl.ds(r0, R), :])
            dv = dvc_ref[pl.ds(r0, R), :]
            dvs[pl.ds(r0, R), :] = dv
            vecs_ref[0:8, :] += _fold8(dv)
            return carry
        lax.fori_loop(0, TP // R, fill_chunk, 0)

        def conv_chunk(ci, carry):
            r0 = pl.multiple_of(ci * R, R)
            vbuf = vs[pl.ds(r0, KWP + R), :]
            dbuf = dvs[pl.ds(r0, KWP + R), :]
            dcur = dbuf[0:R, :]
            dv = jnp.zeros((R, CB), F32)
            for rr in range(8):
                vroll = vbuf if rr == 0 else pltpu.roll(vbuf, rr, 0)
                droll = dbuf if rr == 0 else pltpu.roll(dbuf, KWP + R - rr, 0)
                for q in range(4):
                    s = 8 * q + rr
                    if s > KW - 1:
                        continue
                    k = KW - 1 - s
                    dv = dv + droll[8 * q:8 * q + R, :] * w_ref[k:k + 1, :]
                    dw_ref[8 * k:8 * k + 8, :] += _fold8(dcur * vroll[KWP - 8 * q:KWP - 8 * q + R, :])
            u1 = u1_ref[pl.ds(r0, R), :]
            sg = _sig(u2_ref[pl.ds(r0, R), :])
            du1 = dv * sg
            du2 = dv * u1 * (sg * (1.0 - sg))
            du_ref[0, pl.ds(r0, R), :] = du1.astype(BF16)
            du_ref[1, pl.ds(r0, R), :] = du2.astype(BF16)
            vecs_ref[8:16, :] += _fold8(du1)
            vecs_ref[16:24, :] += _fold8(du2)
            return carry
        lax.fori_loop(0, TP // R, conv_chunk, 0)

    blk = pl.BlockSpec((TP, CB), lambda j: (0, j))
    return pl.pallas_call(
        body, name="conf_bwd_conv",
        grid=(NCB,),
        in_specs=[blk, pl.BlockSpec((TP, CB), lambda j: (0, 2 * NCB + j)),
                  pl.BlockSpec((TP, CB), lambda j: (0, 3 * NCB + j)), pl.BlockSpec((KWP, CB), lambda j: (0, j))],
        out_specs=[pl.BlockSpec((2, TP, CB), lambda j: (0, 0, j)), pl.BlockSpec((8 * KWP, CB), lambda j: (0, j)),
                   pl.BlockSpec((24, CB), lambda j: (0, j))],
        out_shape=[jax.ShapeDtypeStruct((2, TP, DC), BF16),
                   jax.ShapeDtypeStruct((8 * KWP, DC), F32), jax.ShapeDtypeStruct((24, DC), F32)],
        scratch_shapes=[pltpu.VMEM((TP + KWP, CB), F32), pltpu.VMEM((TP + KWP, CB), F32)],
        compiler_params=_cparams(),
    )(dvc, z, z, dw_w)


def _lru_bwd(dycat, z, xc, hs, conv_w, wa_g, b_a, wx_g, b_x, lam):
    NV = 6

    def body(dy_ref, x_ref, g_ref, xc_ref, hs_ref, cw_ref, wa_ref, ba_ref, wx_ref, bx_ref, lam_ref,
             dzl_ref, dwa_ref, dwx_ref, dcw_ref, vecs_ref, ga_s, gx_s, dxc_s):
        vecs_ref[...] = jnp.zeros_like(vecs_ref)
        dcw_ref[...] = jnp.zeros_like(dcw_ref)
        dxc_s[pl.ds(TP, 8), :] = jnp.zeros((8, CB), F32)

        def gate_chunk(ci, carry):
            r0 = pl.multiple_of(ci * TM, TM)
            xb = xc_ref[pl.ds(r0, TM), :].astype(BF16)
            ga_s[pl.ds(r0, TM), :] = jnp.dot(xb, wa_ref[...], preferred_element_type=F32) + ba_ref[...]
            gx_s[pl.ds(r0, TM), :] = jnp.dot(xb, wx_ref[...], preferred_element_type=F32) + bx_ref[...]
            return carry
        lax.fori_loop(0, TP // TM, gate_chunk, 0)

        sp8 = LRU_C * _softplus(-lam_ref[...])
        row = _row_iota((R, CB))
        nchunk = TP // R

        def scan_chunk(cj, carry):
            a_next, lam_next = carry
            ci = nchunk - 1 - cj
            r0 = pl.multiple_of(ci * R, R)
            dyv = dy_ref[pl.ds(r0, R), :]
            g = g_ref[pl.ds(r0, R), :]
            hv = hs_ref[pl.ds(r0, R), :]
            xc = xc_ref[pl.ds(r0, R), :]
            sg = _sig(g)
            dgl = dyv * hv * (sg * (1.0 + g * (1.0 - sg)))
            dzl_ref[1, pl.ds(r0, R), :] = dgl.astype(BF16)
            vecs_ref[0:8, :] += _fold8(dgl)
            dhs = dyv * (g * sg)
            r, i, a, mult = _gate_values(ga_s[pl.ds(r0, R), :], gx_s[pl.ds(r0, R), :], xc, sp8)
            b = jnp.where(row == R - 1, a_next, pltpu.roll(a, R - 1, 0))
            lv = dhs
            k = 1
            while k < R:
                m = row < R - k
                lv = jnp.where(m, lv + b * pltpu.roll(lv, R - k, 0), lv)
                b = jnp.where(m, b * pltpu.roll(b, R - k, 0), b)
                k *= 2
            lv = lv + b * lam_next
            p0 = pl.multiple_of(jnp.maximum(r0 - 8, 0), 8)
            hprev8 = jnp.where(ci > 0, hs_ref[pl.ds(p0, 8), :], 0.0)
            hprev = pltpu.roll(jnp.concatenate([hprev8, hv], axis=0), 1, 0)[8:8 + R, :]
            da = lv * hprev
            ixc = i * xc
            dmult = lv * ixc
            di = lv * mult * xc
            dxc_s[pl.ds(r0, R), :] = lv * mult * i
            a2 = a * a
            dlog_a = da * a - dmult * a2 / mult
            vecs_ref[32:40, :] += _fold8(dlog_a * r)
            dga = -(dlog_a * sp8) * r * (1.0 - r)
            dgx = di * i * (1.0 - i)
            ga_s[pl.ds(r0, R), :] = dga
            gx_s[pl.ds(r0, R), :] = dgx
            vecs_ref[16:24, :] += _fold8(dga)
            vecs_ref[24:32, :] += _fold8(dgx)
            a_first = jnp.sum(jnp.where(row == 0, a, 0.0), axis=0, keepdims=True)
            l_first = jnp.sum(jnp.where(row == 0, lv, 0.0), axis=0, keepdims=True)
            return a_first, l_first
        lax.fori_loop(0, nchunk, scan_chunk, (jnp.zeros((1, CB), F32), jnp.zeros((1, CB), F32)))

        dwa_ref[...] = jnp.zeros_like(dwa_ref)
        dwx_ref[...] = jnp.zeros_like(dwx_ref)

        def mm_chunk(ci, carry):
            r0 = pl.multiple_of(ci * TM, TM)
            xb = xc_ref[pl.ds(r0, TM), :].astype(BF16)
            dgab = ga_s[pl.ds(r0, TM), :].astype(BF16)
            dgxb = gx_s[pl.ds(r0, TM), :].astype(BF16)
            dxc_s[pl.ds(r0, TM), :] += (lax.dot_general(dgab, wa_ref[...], _NT, preferred_element_type=F32)
                                        + lax.dot_general(dgxb, wx_ref[...], _NT, preferred_element_type=F32))
            dwa_ref[...] += lax.dot_general(xb, dgab, _TN, preferred_element_type=F32)
            dwx_ref[...] += lax.dot_general(xb, dgxb, _TN, preferred_element_type=F32)
            return carry
        lax.fori_loop(0, TP // TM, mm_chunk, 0)

        taps = [cw_ref[k:k + 1, :] for k in range(LW)]

        def conv_chunk(ci, carry):
            r0 = pl.multiple_of(ci * R, R)
            dbuf = dxc_s[pl.ds(r0, R + 8), :]
            dcur = dbuf[0:R, :]
            p0 = pl.multiple_of(jnp.maximum(r0 - 8, 0), 8)
            xprev = jnp.where(ci > 0, x_ref[pl.ds(p0, 8), :], 0.0)
            xbuf = jnp.concatenate([xprev, x_ref[pl.ds(r0, R), :]], axis=0)
            dxl = dcur * taps[LW - 1]
            dcw_ref[8 * (LW - 1):8 * LW, :] += _fold8(dcur * xbuf[8:8 + R, :])
            for s in range(1, LW):
                k = LW - 1 - s
                dxl = dxl + pltpu.roll(dbuf, R + 8 - s, 0)[0:R, :] * taps[k]
                dcw_ref[8 * k:8 * k + 8, :] += _fold8(dcur * pltpu.roll(xbuf, s, 0)[8:8 + R, :])
            dzl_ref[0, pl.ds(r0, R), :] = dxl.astype(BF16)
            vecs_ref[8:16, :] += _fold8(dxl)
            vecs_ref[40:48, :] += _fold8(dcur)
            return carry
        lax.fori_loop(0, TP // R, conv_chunk, 0)
        vecs_ref[32:40, :] = vecs_ref[32:40, :] * (LRU_C * _sig(-lam_ref[...]))

    col = lambda off: pl.BlockSpec((TP, CB), lambda j: (0, off + j))
    vec = pl.BlockSpec((1, CB), lambda j: (0, j))
    wsp = pl.BlockSpec((None, CB, CB), lambda j: (j, 0, 0))
    return pl.pallas_call(
        body, name="lru_bwd",
        grid=(NCB,),
        in_specs=[col(0), col(0), col(NCB), col(0), col(0), pl.BlockSpec((LW, CB), lambda j: (0, j)),
                  wsp, vec, wsp, vec, vec],
        out_specs=[pl.BlockSpec((2, TP, CB), lambda j: (0, 0, j)), wsp, wsp,
                   pl.BlockSpec((8 * LW, CB), lambda j: (0, j)), pl.BlockSpec((8 * NV, CB), lambda j: (0, j))],
        out_shape=[jax.ShapeDtypeStruct((2, TP, DL), BF16),
                   jax.ShapeDtypeStruct((NCB, CB, CB), F32), jax.ShapeDtypeStruct((NCB, CB, CB), F32),
                   jax.ShapeDtypeStruct((8 * LW, DL), F32), jax.ShapeDtypeStruct((8 * NV, DL), F32)],
        scratch_shapes=[pltpu.VMEM((TP, CB), F32), pltpu.VMEM((TP, CB), F32), pltpu.VMEM((TP + 8, CB), F32)],
        compiler_params=_cparams(),
    )(dycat, z, z, xc, hs, conv_w, wa_g, b_a, wx_g, b_x, lam)


def _dz_section(sec, dzl_ref, dzc_ref, dgc_ref, use):
    @pl.when(sec < 2)
    def _():
        use(dzl_ref)

    @pl.when(jnp.logical_and(sec >= 2, sec < 4))
    def _():
        use(dzc_ref)

    @pl.when(sec == 4)
    def _():
        use(dgc_ref)


def _dz_specs(rows, index):
    return [pl.BlockSpec((None, rows, 1024), lambda a, b: (jnp.minimum(index(a, b)[1], 1), index(a, b)[0], 0)),
            pl.BlockSpec((None, rows, 1024), lambda a, b: (jnp.clip(index(a, b)[1] - 2, 0, 1), index(a, b)[0], 0)),
            pl.BlockSpec((rows, 1024), lambda a, b: (index(a, b)[0], 0))]


def _inproj_wgrad(name, hn, dzs):
    KB = 512
    nsec = dzs.shape[0]

    def body(hn_ref, dz_ref, dw_ref):
        dw_ref[...] = lax.dot_general(hn_ref[...], dz_ref[...], _TN, preferred_element_type=F32).astype(BF16)

    return pl.pallas_call(
        body, name=name,
        grid=(nsec, D // KB),
        in_specs=[pl.BlockSpec((TP, KB), lambda n, kb: (0, kb)),
                  pl.BlockSpec((None, TP, 1024), lambda n, kb: (n, 0, 0))],
        out_specs=pl.BlockSpec((KB, 1024), lambda n, kb: (kb, n)),
        out_shape=jax.ShapeDtypeStruct((D, nsec * 1024), BF16),
        compiler_params=_cparams(),
    )(hn, dzs)


def _sum_win_parts(parts_a, parts_b, parts_c):
    RB = 64

    def body(a_ref, b_ref, c_ref, o_ref):
        def chunk(ci, carry):
            r0 = pl.multiple_of(ci * R, R)
            for ref, base, ncol in ((a_ref, 0, 2048), (b_ref, 2048, 2048), (c_ref, 4096, 1024)):
                for c0 in range(0, ncol, 512):
                    acc = ref[0, pl.ds(r0, R), c0:c0 + 512].astype(F32)
                    for sidx in range(1, NDEV):
                        acc = acc + ref[sidx, pl.ds(r0, R), c0:c0 + 512].astype(F32)
                    o_ref[pl.ds(r0, R), base + c0:base + c0 + 512] = acc.astype(BF16)
            return carry
        lax.fori_loop(0, RB // R, chunk, 0)

    spec = lambda ncol: pl.BlockSpec((NDEV, RB, ncol), lambda i: (0, i, 0))
    return pl.pallas_call(
        body, name="sum_win_parts",
        grid=(D // NDEV // RB,),
        in_specs=[spec(2048), spec(2048), spec(1024)],
        out_specs=pl.BlockSpec((RB, NIN), lambda i: (i, 0)),
        out_shape=jax.ShapeDtypeStruct((D // NDEV, NIN), BF16),
        compiler_params=_cparams(),
    )(parts_a, parts_b, parts_c)


def _inproj_bwd(dzl, dzc, dgc, w_in, h, dout, pre_w):
    nsec = NIN // 1024

    def body(dzl_ref, dzc_ref, dgc_ref, w_ref, h_ref, dout_ref, pw_ref, gx_hbm, dmeta_ref, dpw_ref, acc_s, dh_s, sem):
        i = pl.program_id(0)
        s = pl.program_id(1)

        def gx_copy(t):
            lo, n, off = _tile_rows(t)
            return pltpu.make_async_copy(dh_s.at[pl.ds(off, n)], gx_hbm.at[pl.ds(lo, n)], sem)

        @pl.when(s == 0)
        def _():
            acc_s[...] = jnp.zeros_like(acc_s)

        def use(dz_ref):
            acc_s[...] += lax.dot_general(dz_ref[...], w_ref[...], _NT, preferred_element_type=F32)
        _dz_section(s, dzl_ref, dzc_ref, dgc_ref, use)

        @pl.when(jnp.logical_and(i == 0, s == nsec - 1))
        def _():
            dpw_ref[...] = jnp.zeros_like(dpw_ref)

        @pl.when(s == nsec - 1)
        def _():
            _for_tile(i - 1, lambda t: gx_copy(t).wait())
            pw = pw_ref[...]

            def chunk(ci, carry):
                r0 = pl.multiple_of(ci * 8, 8)
                hv = h_ref[pl.ds(r0, 8), :]
                dhn = acc_s[pl.ds(r0, 8), :]
                rs = lax.rsqrt(jnp.mean(hv * hv, axis=-1, keepdims=True) + EPS)
                dpw_ref[...] += dhn * (hv * rs)
                gw = dhn * pw
                dot = jnp.mean(gw * hv, axis=-1, keepdims=True)
                dh_s[pl.ds(r0, 8), :] = rs * gw - hv * (rs * rs * rs * dot) + dout_ref[pl.ds(r0, 8), :]
                return carry
            lax.fori_loop(0, TM // 8, chunk, 0, unroll=4)
            _for_tile(i, lambda t: gx_copy(t).start())

            @pl.when(i == 0)
            def _():
                dmeta_ref[...] = dh_s[0:NMETA, :]

            @pl.when(i == NTILE - 1)
            def _():
                gx_copy(NTILE - 1).wait()

    row = pl.BlockSpec((TM, D), lambda i, s: (i, 0))
    return pl.pallas_call(
        body, name="inproj_bwd",
        grid=(TP // TM, nsec),
        in_specs=_dz_specs(TM, lambda i, s: (i, s)) + [
            pl.BlockSpec((D, 1024), lambda i, s: (0, s)), row, row, pl.BlockSpec((1, D), lambda i, s: (0, 0))],
        out_specs=[pl.BlockSpec(memory_space=pl.ANY), pl.BlockSpec((NMETA, D), lambda i, s: (0, 0)),
                   pl.BlockSpec((8, D), lambda i, s: (0, 0))],
        out_shape=[jax.ShapeDtypeStruct((SEQ, D), F32), jax.ShapeDtypeStruct((NMETA, D), F32),
                   jax.ShapeDtypeStruct((8, D), F32)],
        scratch_shapes=[pltpu.VMEM((TM, D), F32), pltpu.VMEM((TM, D), F32), pltpu.SemaphoreType.DMA(())],
        compiler_params=_cparams(),
    )(dzl, dzc, dgc, w_in, h, dout, pre_w)


def _adamw(name, parts, w, m, v, block_rows):
    rows, cols = w.shape
    nparts = parts.shape[0]
    cw = cols if cols <= 640 else 512

    def body(p_ref, w_ref, m_ref, v_ref, g_ref, d_ref, nm_ref, nv_ref):
        def chunk(ci, carry):
            r0 = pl.multiple_of(ci * R, R)
            for c0 in range(0, cols, cw):
                at = (pl.ds(r0, R), slice(c0, c0 + cw))
                g = p_ref[(0,) + at].astype(F32)
                for sidx in range(1, nparts):
                    g = g + p_ref[(sidx,) + at].astype(F32)
                delta, mv, vv = _adam_math(g, w_ref[at], m_ref[at], v_ref[at])
                g_ref[at] = g
                nm_ref[at] = mv
                nv_ref[at] = vv
                d_ref[at] = delta
            return carry
        lax.fori_loop(0, block_rows // R, chunk, 0)

    blk = pl.BlockSpec((block_rows, cols), lambda i: (i, 0))
    shp = jax.ShapeDtypeStruct((rows, cols), F32)
    return pl.pallas_call(
        body, name=name,
        grid=(rows // block_rows,),
        in_specs=[pl.BlockSpec((nparts, block_rows, cols), lambda i: (0, i, 0)), blk, blk, blk],
        out_specs=[blk, blk, blk, blk],
        out_shape=[shp, shp, shp, shp],
        compiler_params=_cparams(),
    )(parts, w, m, v)


def _adam_math(g, w, m, v):
    c1 = 1.0 / (1.0 - ADAM_B1 ** ADAM_STEP)
    c2 = 1.0 / (1.0 - ADAM_B2 ** ADAM_STEP)
    mv = ADAM_B1 * m + (1.0 - ADAM_B1) * g
    vv = ADAM_B2 * v + (1.0 - ADAM_B2) * (g * g)
    upd = (mv * c1) / (jnp.sqrt(vv * c2) + ADAM_EPS) + ADAM_WD * w
    return -ADAM_LR * upd, mv, vv


_VEC = [("pre_norm_w", 2), ("post_norm_w", 2), ("b_in", 5), ("lru_conv_b", 1), ("b_gate_a", 1), ("b_gate_x", 1),
        ("lru_lambda", 1), ("conf_dw_b", 1), ("conf_ln_w", 1), ("conf_ln_b", 1), ("conf_pw_b", 1)]
_VEC_ROWS = 24
_LOSS_ROW = 17
_SM_ROWS = 64


def _pack_grads(dprew_acc, dpostw_acc, cvecs, kvecs, lvecs, dcw_acc, ddw_acc, dh, loss_acc):
    def body(pre_ref, post_ref, c_ref, k_ref, l_ref, dcw_ref, ddw_ref, dh_ref, loss_ref, vec_ref, small_ref, tmp):
        s8 = lambda ref, r: jnp.sum(ref[8 * r:8 * r + 8, :], axis=0, keepdims=True)
        vec_ref[...] = jnp.zeros_like(vec_ref)
        pre, post = s8(pre_ref, 0), s8(post_ref, 0)
        rows = [pre[:, 0:1024], pre[:, 1024:2048], post[:, 0:1024], post[:, 1024:2048],
                s8(l_ref, 1), s8(l_ref, 0), s8(k_ref, 1), s8(k_ref, 2), s8(c_ref, 1),
                s8(l_ref, 5), s8(l_ref, 2), s8(l_ref, 3), s8(l_ref, 4),
                s8(k_ref, 0), s8(c_ref, 2), s8(c_ref, 3), s8(c_ref, 0)]
        for r, val in enumerate(rows):
            vec_ref[r:r + 1, :] = val
        vec_ref[_LOSS_ROW:_LOSS_ROW + 1, :] = jnp.zeros((1, 1024), F32) + (0.5 / D) * jnp.sum(loss_ref[...])

        small_ref[...] = jnp.zeros_like(small_ref)
        for k in range(LW):
            tmp[k:k + 1, :] = s8(dcw_ref, k)
        for k in range(KW):
            tmp[8 + k:9 + k, :] = s8(ddw_ref, k)
        for d in range(NDEV):
            small_ref[d, 0:LW, 0:128] = tmp[0:LW, 128 * d:128 * d + 128]
            small_ref[d, 8:8 + KW, 0:128] = tmp[8:8 + KW, 128 * d:128 * d + 128]
            small_ref[d, 40:56, :] = dh_ref[:, 256 * d:256 * d + 256]

    full = lambda a: pl.BlockSpec(a.shape, lambda i: (0,) * a.ndim)
    ins = [dprew_acc, dpostw_acc, cvecs, kvecs, lvecs, dcw_acc, ddw_acc]
    return pl.pallas_call(
        body, name="pack_grads",
        grid=(1,),
        in_specs=[full(a) for a in ins] + [full(dh), full(loss_acc)],
        out_specs=[pl.BlockSpec((_VEC_ROWS, 1024), lambda i: (0, 0)),
                   pl.BlockSpec((NDEV, _SM_ROWS, 256), lambda i: (0, 0, 0))],
        out_shape=[jax.ShapeDtypeStruct((_VEC_ROWS, 1024), F32), jax.ShapeDtypeStruct((NDEV, _SM_ROWS, 256), F32)],
        scratch_shapes=[pltpu.VMEM((40, 1024), F32)],
        compiler_params=_cparams(),
    )(*ins, dh, loss_acc)


def _adamw_vec(parts, W, M, V):
    nv = len(_VEC)

    def body(*refs):
        p_ref = refs[0]
        w_refs, m_refs, v_refs = refs[1:1 + nv], refs[1 + nv:1 + 2 * nv], refs[1 + 2 * nv:1 + 3 * nv]
        outs = refs[1 + 3 * nv:]

        def total(r):
            acc = p_ref[0, r:r + 1, :]
            for sidx in range(1, NDEV):
                acc = acc + p_ref[sidx, r:r + 1, :]
            return acc

        row = 0
        for idx, (_, nrows) in enumerate(_VEC):
            for part in range(nrows):
                cols = slice(1024 * part, 1024 * part + 1024)
                g = total(row + part)
                delta, mv, vv = _adam_math(g, w_refs[idx][:, cols], m_refs[idx][:, cols], v_refs[idx][:, cols])
                for o, val in zip(outs[4 * idx:4 * idx + 4], (g, delta, mv, vv)):
                    o[:, cols] = val
            row += nrows
        outs[-1][...] = total(_LOSS_ROW)[:, 0:128]

    names = [n for n, _ in _VEC]
    flat = lambda d: [d[n].reshape(1, -1) for n in names]
    ws, ms, vs = flat(W), flat(M), flat(V)
    res = pl.pallas_call(
        body, name="adamw_vec",
        out_shape=[jax.ShapeDtypeStruct(w.shape, F32) for w in ws for _ in range(4)]
        + [jax.ShapeDtypeStruct((1, 128), F32)],
        compiler_params=_cparams(),
    )(parts, *ws, *ms, *vs)
    return {n: tuple(res[4 * i:4 * i + 4]) for i, n in enumerate(names)}, res[-1]


def _adamw_small(parts, W, M, V):
    where = {"lru_conv_w": (slice(0, LW), slice(0, 128)), "conf_dw_w": (slice(8, 8 + KW), slice(0, 128)),
             "meta_tokens": (slice(40, 56), slice(0, 256))}
    names = list(where)

    def body(*refs):
        p_ref = refs[0]
        outs = refs[10:]
        for idx, n in enumerate(names):
            rs, cs = where[n]
            g = p_ref[0, rs, cs]
            for sidx in range(1, NDEV):
                g = g + p_ref[sidx, rs, cs]
            delta, mv, vv = _adam_math(g, refs[1 + idx][...], refs[4 + idx][...], refs[7 + idx][...])
            for o, val in zip(outs[4 * idx:4 * idx + 4], (g, delta, mv, vv)):
                o[...] = val

    two_d = lambda a: a.reshape(a.shape[-2:])
    ws, ms, vs = ([two_d(d[n]) for n in names] for d in (W, M, V))
    res = pl.pallas_call(
        body, name="adamw_small",
        out_shape=[jax.ShapeDtypeStruct(w.shape, F32) for w in ws for _ in range(4)],
        compiler_params=_cparams(),
    )(parts, *ws, *ms, *vs)
    return {n: tuple(res[4 * i:4 * i + 4]) for i, n in enumerate(names)}


def _pack_small(lru_cw, dw_w, meta):
    buf = jnp.zeros((_SM_ROWS, 256), F32)
    buf = buf.at[0:LW, 0:128].set(lru_cw)
    buf = buf.at[8:8 + dw_w.shape[0], 0:128].set(dw_w)
    return buf.at[40:56, :].set(meta)


def _block_diag4(w):
    w4 = w.reshape(NCB, 4, 64, 64)
    eye = jnp.eye(4, dtype=w.dtype)
    return jnp.einsum("ghij,hk->ghikj", w4, eye).reshape(NCB, CB, CB)


def _diag_blocks(g):
    g5 = g.reshape(NCB, 4, 64, 4, 64)
    return jnp.stack([g5[:, hh, :, hh, :] for hh in range(4)], axis=1).reshape(16, 64, 64)


def _gate_mats(W):
    return _block_diag4(W["w_gate_a"][0]).astype(BF16), _block_diag4(W["w_gate_x"][0]).astype(BF16)


def _local_step(x, target, meta_full, inproj, out_weights, lru_cw_full, dw_w_full, W, gate_mats, send):
    wa_g, wx_g = gate_mats

    h, hn = _prenorm(x, meta_full, W["pre_norm_w"])
    z, win_full = inproj(hn)
    ylru, xc, hs = _lru_fwd(z, lru_cw_full, W["lru_conv_b"], wa_g, W["b_gate_a"], wx_g, W["b_gate_x"],
                            W["lru_lambda"])
    vc = _conf_fwd_conv(z, dw_w_full, W["conf_dw_b"])
    wout_full, pw_full = out_weights(vc)
    yconf, p = _conf_fwd_proj(vc, z, W["conf_ln_w"], W["conf_ln_b"], pw_full, W["conf_pw_b"])
    dout, dy, loss_acc, dpostw_acc = _outproj_loss(ylru, yconf, wout_full, h, target, W["post_norm_w"])

    dycat, dwout_part = _outproj_bwd(dy, ylru, yconf, wout_full)
    tok = send("w_out", dwout_part)
    dvc, dgc, dpw_part, cvecs = _conf_bwd_proj(dycat, p, z, vc, W["conf_ln_w"] + tok, W["conf_ln_b"], pw_full)
    tok = send("conf_pw_w", dpw_part)
    tok = tok + send("w_in_c", _inproj_wgrad("inproj_wgrad_c", hn, dgc[None]))
    dzc, ddw_acc, kvecs = _conf_bwd_conv(dvc, z, dw_w_full + tok)
    tok = send("w_in_b", _inproj_wgrad("inproj_wgrad_b", hn, dzc))
    dzl, dwa_g, dwx_g, dcw_acc, lvecs = _lru_bwd(dycat, z, xc, hs, lru_cw_full, wa_g, W["b_gate_a"] + tok, wx_g,
                                                 W["b_gate_x"], W["lru_lambda"])
    tok = send("w_in_a", _inproj_wgrad("inproj_wgrad_a", hn, dzl))
    tok = tok + send("w_gates", _diag_blocks(dwa_g).reshape(16 * 64, 64), _diag_blocks(dwx_g).reshape(16 * 64, 64))
    grad_x, dmeta, dprew_acc = _inproj_bwd(dzl, dzc, dgc, win_full, h, dout, W["pre_norm_w"] + tok)

    vec_pack, small_part = _pack_grads(dprew_acc, dpostw_acc, cvecs, kvecs, lvecs, dcw_acc, ddw_acc, dmeta, loss_acc)
    return grad_x, vec_pack, small_part


def kernel(x, meta_tokens, pre_norm_w, post_norm_w, w_in, b_in, lru_conv_w, lru_conv_b, w_gate_a, b_gate_a, w_gate_x, b_gate_x, lru_lambda, conf_dw_w, conf_dw_b, conf_ln_w, conf_ln_b, conf_pw_w, conf_pw_b, w_out, loss_target, m_meta_tokens, m_pre_norm_w, m_post_norm_w, m_w_in, m_b_in, m_lru_conv_w, m_lru_conv_b, m_w_gate_a, m_b_gate_a, m_w_gate_x, m_b_gate_x, m_lru_lambda, m_conf_dw_w, m_conf_dw_b, m_conf_ln_w, m_conf_ln_b, m_conf_pw_w, m_conf_pw_b, m_w_out, v_meta_tokens, v_pre_norm_w, v_post_norm_w, v_w_in, v_b_in, v_lru_conv_w, v_lru_conv_b, v_w_gate_a, v_b_gate_a, v_w_gate_x, v_b_gate_x, v_lru_lambda, v_conf_dw_w, v_conf_dw_b, v_conf_ln_w, v_conf_ln_b, v_conf_pw_w, v_conf_pw_b, v_w_out):
    W = dict(meta_tokens=meta_tokens, pre_norm_w=pre_norm_w, post_norm_w=post_norm_w, w_in=w_in, b_in=b_in,
             lru_conv_w=lru_conv_w, lru_conv_b=lru_conv_b, w_gate_a=w_gate_a, b_gate_a=b_gate_a,
             w_gate_x=w_gate_x, b_gate_x=b_gate_x, lru_lambda=lru_lambda, conf_dw_w=conf_dw_w,
             conf_dw_b=conf_dw_b, conf_ln_w=conf_ln_w, conf_ln_b=conf_ln_b, conf_pw_w=conf_pw_w,
             conf_pw_b=conf_pw_b, w_out=w_out)
    M = dict(meta_tokens=m_meta_tokens, pre_norm_w=m_pre_norm_w, post_norm_w=m_post_norm_w, w_in=m_w_in,
             b_in=m_b_in, lru_conv_w=m_lru_conv_w, lru_conv_b=m_lru_conv_b, w_gate_a=m_w_gate_a,
             b_gate_a=m_b_gate_a, w_gate_x=m_w_gate_x, b_gate_x=m_b_gate_x, lru_lambda=m_lru_lambda,
             conf_dw_w=m_conf_dw_w, conf_dw_b=m_conf_dw_b, conf_ln_w=m_conf_ln_w, conf_ln_b=m_conf_ln_b,
             conf_pw_w=m_conf_pw_w, conf_pw_b=m_conf_pw_b, w_out=m_w_out)
    V = dict(meta_tokens=v_meta_tokens, pre_norm_w=v_pre_norm_w, post_norm_w=v_post_norm_w, w_in=v_w_in,
             b_in=v_b_in, lru_conv_w=v_lru_conv_w, lru_conv_b=v_lru_conv_b, w_gate_a=v_w_gate_a,
             b_gate_a=v_b_gate_a, w_gate_x=v_w_gate_x, b_gate_x=v_b_gate_x, lru_lambda=v_lru_lambda,
             conf_dw_w=v_conf_dw_w, conf_dw_b=v_conf_dw_b, conf_ln_w=v_conf_ln_w, conf_ln_b=v_conf_ln_b,
             conf_pw_w=v_conf_pw_w, conf_pw_b=v_conf_pw_b, w_out=v_w_out)
    names = list(W.keys())
    shapes = {n: W[n].shape for n in names}

    small = _pack_small(lru_conv_w[0], conf_dw_w[0], meta_tokens)
    (small_flight,), tok = _exchange_start("gather_small_start", [
        (small, jax.ShapeDtypeStruct((NDEV, _SM_ROWS, 256), F32), _whole, _slot)])
    win_flight, tok = _win_gather_start(w_in[0].astype(BF16) + tok[0, 0].astype(BF16))
    gate_mats = _gate_mats(W)
    wout_shard = w_out[0].astype(BF16) + tok[0, 0].astype(BF16)
    pw_shard = conf_pw_w[0].astype(BF16)
    cast_done = (gate_mats[0][0, 0:8, 0:128] + gate_mats[1][0, 0:8, 0:128]
                 + wout_shard[0:8, 0:128] + pw_shard[0:8, 0:128])
    win_flight, tok = _win_gather_links(win_flight, cast_done)
    gathered, tok = _exchange_start("gather_out_start", [
        (wout_shard + tok[0, 0].astype(BF16), jax.ShapeDtypeStruct((D, D), BF16), _whole, _rows(D // NDEV)),
        (pw_shard, jax.ShapeDtypeStruct((DC, DC), BF16), _whole, _rows(DC // NDEV)),
    ])
    (small_all,) = _exchange_wait("gather_small_wait", [small_flight], tok)
    unshard = lambda a: jnp.transpose(a, (1, 0, 2)).reshape(a.shape[1], -1)
    lru_cw_full = unshard(small_all[:, 0:LW, 0:128])
    dw_w_full = unshard(small_all[:, 8:8 + KWP, 0:128])
    meta_full = unshard(small_all[:, 40:56, :])

    def out_weights(after):
        return _exchange_wait("gather_out_wait", gathered, after)

    def inproj(hn):
        xi, yi, ci = lax.axis_index("x"), lax.axis_index("y"), lax.axis_index("c")
        shard = lambda px, py, pc: (4 * px + 2 * py + pc).astype(jnp.int32)
        here = jnp.stack([shard(xi, yi, ci), shard(xi, yi, 1 - ci)])
        over_links = jnp.stack([shard(1 - xi, yi, ci), shard(xi, 1 - yi, ci), shard(1 - xi, 1 - yi, ci)])
        flight = _win_gather_early(win_flight)
        z, land = _inproj_cols("inproj_here", here, hn, flight["land"], b_in, None)
        flight = _win_gather_forward(dict(flight, land=land), z)
        z, land = _inproj_cols("inproj_links", over_links, hn, flight["land"], b_in, z)
        land = _win_gather_wait(dict(flight, land=land))
        return _inproj_cols("inproj_sibling", over_links + 1 - 2 * ci, hn, land, b_in, z)

    row_stage = lambda ncol: (jax.ShapeDtypeStruct((NDEV, D // NDEV, ncol), BF16), _rows(D // NDEV))
    piece = {"w_in_a": row_stage(2048), "w_in_b": row_stage(2048), "w_in_c": row_stage(1024),
             "w_out": row_stage(D),
             "conf_pw_w": (jax.ShapeDtypeStruct((NDEV, DC // NDEV, DC), BF16), _rows(DC // NDEV)),
             "w_gates": (jax.ShapeDtypeStruct((NDEV, 16 * 64, 64), BF16), _whole)}
    sent = {}

    def send(name, *parts):
        handles, token = _exchange_start(
            "scatter_" + name + "_start",
            [(part.astype(BF16), piece[name][0], piece[name][1], _slot) for part in parts])
        sent[name] = handles
        return token[0, 0]

    grad_x, vec_pack, small_part = _local_step(
        x[0], loss_target[0], meta_full, inproj, out_weights, lru_cw_full, dw_w_full, W, gate_mats, send)
    grad_x = grad_x[None]

    rest, tok = _exchange_start("scatter_rest_start", [
        (small_part, jax.ShapeDtypeStruct((NDEV, _SM_ROWS, 256), F32), _slot, _slot),
        (vec_pack, jax.ShapeDtypeStruct((NDEV, _VEC_ROWS, 1024), F32), _whole, _slot),
    ])
    (parts_c,) = _exchange_wait("scatter_w_in_c_wait", sent["w_in_c"], tok)
    (parts_b,) = _exchange_wait("scatter_w_in_b_wait", sent["w_in_b"], parts_c)
    (parts_a,) = _exchange_wait("scatter_w_in_a_wait", sent["w_in_a"], parts_b)
    win_rows = _sum_win_parts(parts_a, parts_b, parts_c)
    win_stage2, tok = _exchange_start("scatter_w_in_stage2_start", [
        (win_rows, jax.ShapeDtypeStruct((NDEV, D // NDEV, NIN // NDEV), BF16), _cols(NIN // NDEV), _slot)])

    G, DW, NM, NV = {}, {}, {}, {}
    (wout_parts,) = _exchange_wait("scatter_w_out_wait", sent["w_out"], tok)
    G["w_out"], DW["w_out"], NM["w_out"], NV["w_out"] = _adamw("adamw_w_out", wout_parts, w_out[0], m_w_out[0], v_w_out[0], 64)
    (pw_parts,) = _exchange_wait("scatter_conf_pw_w_wait", sent["conf_pw_w"], G["w_out"])
    G["conf_pw_w"], DW["conf_pw_w"], NM["conf_pw_w"], NV["conf_pw_w"] = _adamw(
        "adamw_pw", pw_parts, conf_pw_w[0], m_conf_pw_w[0], v_conf_pw_w[0], 128)
    res = {}
    wa_parts, wx_parts = _exchange_wait("scatter_w_gates_wait", sent["w_gates"], G["conf_pw_w"])
    for n, parts in (("w_gate_a", wa_parts), ("w_gate_x", wx_parts)):
        res[n] = _adamw("adamw_" + n, parts, *[d[n].reshape(16 * 64, 64) for d in (W, M, V)], 16 * 64)
    small_parts, vec_parts = _exchange_wait("scatter_rest_wait", rest, res["w_gate_x"][0])
    res.update(_adamw_small(small_parts, W, M, V))
    vec_res, loss_row = _adamw_vec(vec_parts, W, M, V)
    res.update(vec_res)
    (win_sum,) = _exchange_wait("scatter_w_in_stage2_wait", win_stage2, loss_row)
    res["w_in"] = _adamw("adamw_w_in", win_sum.reshape(1, D, NIN // NDEV), w_in[0], m_w_in[0], v_w_in[0], 256)
    for n, vals in res.items():
        for dst, val in zip((G, DW, NM, NV), vals):
            dst[n] = val
    for dst in (G, DW, NM, NV):
        for n in names:
            dst[n] = dst[n].reshape(shapes[n])
    loss = loss_row[0, 0]

    return (loss, grad_x, *[G[n] for n in names], *[DW[n] for n in names],
            *[NM[n] for n in names], *[NV[n] for n in names])
```

```python
import functools

import jax
import jax.numpy as jnp
from jax import lax
from jax.experimental import pallas as pl
from jax.experimental.pallas import tpu as pltpu

F32 = jnp.float32
BF16 = jnp.bfloat16

D = 2048
DL = 1024
DC = 1024
NIN = 5120
NMETA = 16
SEQ = 2048
T = NMETA + SEQ
TP = 2176
TM = 544
CB = 256
NCB = DL // CB
R = 16
KW = 31
KWP = 32
LW = 4
LRU_C = 8.0
EPS = 1e-6
NDEV = 8

ADAM_LR = 0.001
ADAM_B1 = 0.9
ADAM_B2 = 0.999
ADAM_EPS = 1e-08
ADAM_WD = 0.01
ADAM_STEP = 10

VMEM_LIMIT = 56 * 1024 * 1024


def _cparams():
    return pltpu.CompilerParams(vmem_limit_bytes=VMEM_LIMIT)


def _sig(x):
    return 1.0 / (1.0 + jnp.exp(-x))


def _expm1_neg(y):
    poly = y * (1.0 + y * (0.5 + y * (1.0 / 6.0 + y * (1.0 / 24.0 + y * (1.0 / 120.0)))))
    return jnp.where(y > -0.1, poly, jnp.exp(y) - 1.0)


def _softplus(x):
    e = jnp.exp(-jnp.abs(x))
    w = 1.0 + e
    l1p = jnp.where(w == 1.0, e, jnp.log(w) * e / (w - 1.0))
    return jnp.maximum(x, 0.0) + l1p


def _row_iota(shape):
    return lax.broadcasted_iota(jnp.int32, shape, 0)


def _fold8(v):
    return v[0:8, :] + v[8:16, :]


_FLIPS = [(k >> 2 & 1, k >> 1 & 1, k & 1) for k in range(1, NDEV)]
_HBM = pl.BlockSpec(memory_space=pltpu.HBM)
_SEM = pl.BlockSpec(memory_space=pltpu.SEMAPHORE)


def _peers():
    x, y, c = lax.axis_index("x"), lax.axis_index("y"), lax.axis_index("c")
    out = []
    for dx, dy, dc in _FLIPS:
        px = 1 - x if dx else x
        py = 1 - y if dy else y
        pc = 1 - c if dc else c
        out.append(((px, py, pc), 4 * px + 2 * py + pc))
    return 4 * x + 2 * y + c, out


def _exchange_start(name, items):
    n = len(items)

    def body(*refs):
        srcs, lands = refs[:n], refs[n:2 * n]
        outs = refs[2 * n:]
        send_sems, recv_sems, local_sems = outs[:n], outs[n:2 * n], outs[2 * n:3 * n]
        token = outs[-1]
        me, peers = _peers()
        for a in range(n):
            src_at, dst_at = items[a][2], items[a][3]
            pltpu.make_async_copy(src_at(srcs[a], me), dst_at(lands[a], me), local_sems[a]).start()
        for a in range(n):
            src_at, dst_at = items[a][2], items[a][3]
            for k, (pos, peer) in enumerate(peers):
                pltpu.make_async_remote_copy(
                    src_ref=src_at(srcs[a], peer), dst_ref=dst_at(lands[a], me),
                    send_sem=send_sems[a].at[k], recv_sem=recv_sems[a].at[k],
                    device_id=pos, device_id_type=pl.DeviceIdType.MESH).start()
        token[...] = jnp.zeros_like(token)

    srcs = [pltpu.with_memory_space_constraint(it[0], pltpu.HBM) for it in items]
    lands = [pltpu.with_memory_space_constraint(lax.empty(it[1].shape, it[1].dtype), pltpu.HBM) for it in items]
    sem7 = pltpu.SemaphoreType.DMA((NDEV - 1,))
    res = pl.pallas_call(
        body, name=name,
        out_shape=([sem7] * (2 * n) + [pltpu.SemaphoreType.DMA(())] * n
                   + [pltpu.HBM(a.shape, a.dtype) for a in srcs] + [pltpu.HBM(a.shape, a.dtype) for a in lands]
                   + [jax.ShapeDtypeStruct((8, 128), F32)]),
        in_specs=[_HBM] * (2 * n),
        out_specs=[_SEM] * (3 * n) + [_HBM] * (2 * n) + [pl.BlockSpec(memory_space=pltpu.VMEM)],
        input_output_aliases={i: 3 * n + i for i in range(2 * n)},
        compiler_params=pltpu.CompilerParams(has_side_effects=pltpu.SideEffectType.DATAFLOW_SIDE_EFFECTING),
    )(*srcs, *lands)
    handles = [dict(send=res[a], recv=res[n + a], local=res[2 * n + a], src=res[3 * n + a], land=res[4 * n + a],
                    src_at=items[a][2], dst_at=items[a][3]) for a in range(n)]
    return handles, res[-1]


def _wait_bytes(piece, sem):
    pltpu.make_async_copy(piece, piece, sem).wait()


def _exchange_wait(name, handles, after):
    n = len(handles)

    def body(*refs):
        srcs, lands = refs[:n], refs[n:2 * n]
        send_sems, recv_sems, local_sems = refs[2 * n:3 * n], refs[3 * n:4 * n], refs[4 * n:5 * n]
        me, peers = _peers()
        for a in range(n):
            src_at, dst_at = handles[a]["src_at"], handles[a]["dst_at"]
            for k, (pos, peer) in enumerate(peers):
                _wait_bytes(src_at(srcs[a], peer), send_sems[a].at[k])
                _wait_bytes(dst_at(lands[a], peer), recv_sems[a].at[k])
            pltpu.make_async_copy(src_at(srcs[a], me), dst_at(lands[a], me), local_sems[a]).wait()

    srcs = [hd["src"] for hd in handles]
    lands = [hd["land"] for hd in handles]
    res = pl.pallas_call(
        body, name=name,
        out_shape=[pltpu.HBM(a.shape, a.dtype) for a in srcs] + [pltpu.HBM(a.shape, a.dtype) for a in lands],
        in_specs=[_HBM] * (2 * n) + [_SEM] * (3 * n) + [pl.BlockSpec(memory_space=pl.ANY)],
        out_specs=[_HBM] * (2 * n),
        input_output_aliases={i: i for i in range(2 * n)},
        compiler_params=pltpu.CompilerParams(has_side_effects=pltpu.SideEffectType.DATAFLOW_SIDE_EFFECTING),
    )(*srcs, *lands, *[hd["send"] for hd in handles], *[hd["recv"] for hd in handles],
      *[hd["local"] for hd in handles], after)
    return list(res[n:])


_SIDE = pltpu.SideEffectType.DATAFLOW_SIDE_EFFECTING
_WCOLS = NIN // NDEV


def _win_cols(ref, l):
    return ref.at[:, pl.ds(pl.multiple_of(l * _WCOLS, 128), _WCOLS)]


def _win_routes():
    x, y, c = lax.axis_index("x"), lax.axis_index("y"), lax.axis_index("c")
    pos = [(x, y, 1 - c), (1 - x, y, c), (x, 1 - y, c), (1 - x, 1 - y, c)]
    return 4 * x + 2 * y + c, [(p, 4 * p[0] + 2 * p[1] + p[2]) for p in pos]


def _win_gather_start(shard):
    def body(src, land, send_sem, recv_sem, local_sem, src_thru, land_thru, token):
        me, routes = _win_routes()
        pltpu.make_async_copy(src, _win_cols(land, me), local_sem).start()
        pltpu.make_async_remote_copy(src_ref=src, dst_ref=_win_cols(land, me), send_sem=send_sem, recv_sem=recv_sem,
                                     device_id=routes[0][0], device_id_type=pl.DeviceIdType.MESH).start()
        token[...] = jnp.zeros_like(token)

    src = pltpu.with_memory_space_constraint(shard, pltpu.HBM)
    land = pltpu.with_memory_space_constraint(lax.empty((D, NIN), BF16), pltpu.HBM)
    sem = pltpu.SemaphoreType.DMA(())
    res = pl.pallas_call(
        body, name="win_gather_start",
        out_shape=[sem, sem, sem, pltpu.HBM(src.shape, BF16), pltpu.HBM(land.shape, BF16),
                   jax.ShapeDtypeStruct((8, 128), F32)],
        in_specs=[_HBM, _HBM],
        out_specs=[_SEM, _SEM, _SEM, _HBM, _HBM, pl.BlockSpec(memory_space=pltpu.VMEM)],
        input_output_aliases={0: 3, 1: 4},
        compiler_params=pltpu.CompilerParams(has_side_effects=_SIDE),
    )(src, land)
    return dict(send0=res[0], recv0=res[1], local=res[2], src=res[3], land=res[4]), res[5]


def _win_gather_links(hd, after):
    def body(src, land, after_ref, send_sems, recv_sems, src_thru, land_thru, token):
        me, routes = _win_routes()
        for k in (1, 2, 3):
            pltpu.make_async_remote_copy(src_ref=src, dst_ref=_win_cols(land, me), send_sem=send_sems.at[k - 1],
                                         recv_sem=recv_sems.at[k - 1], device_id=routes[k][0],
                                         device_id_type=pl.DeviceIdType.MESH).start()
        token[...] = jnp.zeros_like(token)

    sem3 = pltpu.SemaphoreType.DMA((3,))
    res = pl.pallas_call(
        body, name="win_gather_links",
        out_shape=[sem3, sem3, pltpu.HBM(hd["src"].shape, BF16), pltpu.HBM(hd["land"].shape, BF16),
                   jax.ShapeDtypeStruct((8, 128), F32)],
        in_specs=[_HBM, _HBM, pl.BlockSpec(memory_space=pl.ANY)],
        out_specs=[_SEM, _SEM, _HBM, _HBM, pl.BlockSpec(memory_space=pltpu.VMEM)],
        input_output_aliases={0: 2, 1: 3},
        compiler_params=pltpu.CompilerParams(has_side_effects=_SIDE),
    )(hd["src"], hd["land"], after)
    return dict(hd, send=res[0], recv=res[1], src=res[2], land=res[3]), res[4]


def _win_gather_forward(hd, after):
    def body(land, recv_sems, after_ref, land_thru, fsend_sems, frecv_sems):
        me, routes = _win_routes()
        sibling = routes[0][0]
        for k in (1, 2, 3):
            pos, peer = routes[k]
            piece = _win_cols(land, peer)
            pltpu.make_async_remote_copy(src_ref=piece, dst_ref=piece, send_sem=fsend_sems.at[k - 1],
                                         recv_sem=recv_sems.at[k - 1], device_id=pos,
                                         device_id_type=pl.DeviceIdType.MESH).wait_recv()
            pltpu.make_async_remote_copy(src_ref=piece, dst_ref=piece, send_sem=fsend_sems.at[k - 1],
                                         recv_sem=frecv_sems.at[k - 1], device_id=sibling,
                                         device_id_type=pl.DeviceIdType.MESH).start()

    sem3 = pltpu.SemaphoreType.DMA((3,))
    res = pl.pallas_call(
        body, name="win_gather_forward",
        out_shape=[pltpu.HBM(hd["land"].shape, BF16), sem3, sem3],
        in_specs=[_HBM, _SEM, pl.BlockSpec(memory_space=pl.ANY)],
        out_specs=[_HBM, _SEM, _SEM],
        input_output_aliases={0: 0},
        compiler_params=pltpu.CompilerParams(has_side_effects=_SIDE),
    )(hd["land"], hd["recv"], after)
    return dict(hd, land=res[0], fsend=res[1], frecv=res[2])


def _win_gather_early(hd):
    def body(src, land, recv_sem, local_sem, src_thru, land_thru):
        me, routes = _win_routes()
        _wait_bytes(_win_cols(land, routes[0][1]), recv_sem)
        pltpu.make_async_copy(src, _win_cols(land, me), local_sem).wait()

    res = pl.pallas_call(
        body, name="win_gather_early",
        out_shape=[pltpu.HBM(hd["src"].shape, BF16), pltpu.HBM(hd["land"].shape, BF16)],
        in_specs=[_HBM, _HBM, _SEM, _SEM],
        out_specs=[_HBM, _HBM],
        input_output_aliases={0: 0, 1: 1},
        compiler_params=pltpu.CompilerParams(has_side_effects=_SIDE),
    )(hd["src"], hd["land"], hd["recv0"], hd["local"])
    return dict(hd, src=res[0], land=res[1])


def _win_gather_wait(hd):
    def body(src, land, send0_sem, send_sems, fsend_sems, frecv_sems, src_thru, land_thru):
        me, routes = _win_routes()
        sib_pos, sibling = routes[0]
        for k in range(4):
            _wait_bytes(src, send0_sem if k == 0 else send_sems.at[k - 1])
        for k in (1, 2, 3):
            _wait_bytes(_win_cols(land, routes[k][1]), fsend_sems.at[k - 1])
            _wait_bytes(_win_cols(land, 4 * routes[k][0][0] + 2 * routes[k][0][1] + sib_pos[2]), frecv_sems.at[k - 1])

    res = pl.pallas_call(
        body, name="win_gather_wait",
        out_shape=[pltpu.HBM(hd["src"].shape, BF16), pltpu.HBM(hd["land"].shape, BF16)],
        in_specs=[_HBM, _HBM] + [_SEM] * 4,
        out_specs=[_HBM, _HBM],
        input_output_aliases={0: 0, 1: 1},
        compiler_params=pltpu.CompilerParams(has_side_effects=_SIDE),
    )(hd["src"], hd["land"], hd["send0"], hd["send"], hd["fsend"], hd["frecv"])
    return res[1]


def _whole(ref, l):
    return ref


def _slot(ref, l):
    return ref.at[l]


def _cols(width):
    def at(ref, l):
        return ref.at[:, pl.ds(pl.multiple_of(l * width, 128), width)]
    return at


def _rows(height):
    def at(ref, l):
        return ref.at[pl.ds(pl.multiple_of(l * height, 8), height), :]
    return at


NTILE = TP // TM


def _tile_rows(t):
    lo = max(t * TM - NMETA, 0)
    hi = min((t + 1) * TM - NMETA, SEQ)
    return lo, hi - lo, lo + NMETA - t * TM


def _for_tile(t, fn):
    for static_t in range(NTILE):
        pl.when(t == static_t)(functools.partial(fn, static_t))


def _token_tile_copy(hbm_ref, buf, sem, t):
    lo, n, off = _tile_rows(t)
    return pltpu.make_async_copy(hbm_ref.at[pl.ds(lo, n)], buf.at[pl.ds(off, n)], sem)


def _prenorm(x, meta_full, pre_w):
    def body(x_ref, meta_ref, pw_ref, h_ref, hn_ref, xbuf, sems):
        i = pl.program_id(0)
        slot = i % 2

        def start(t):
            _token_tile_copy(x_ref, xbuf.at[t % 2], sems.at[t % 2], t).start()

        @pl.when(i == 0)
        def _():
            start(0)
        _for_tile(i + 1, start)
        _for_tile(i, lambda t: _token_tile_copy(x_ref, xbuf.at[t % 2], sems.at[t % 2], t).wait())

        @pl.when(i == 0)
        def _():
            xbuf[0, 0:NMETA, :] = meta_ref[...]

        @pl.when(i == NTILE - 1)
        def _():
            last = _tile_rows(NTILE - 1)[1]
            xbuf[(NTILE - 1) % 2, last:TM, :] = jnp.zeros((TM - last, D), F32)

        pw = pw_ref[...]

        def chunk(ci, carry):
            r0 = pl.multiple_of(ci * R, R)
            xv = xbuf[slot, pl.ds(r0, R), :]
            h_ref[pl.ds(r0, R), :] = xv
            ms = jnp.mean(xv * xv, axis=-1, keepdims=True)
            hn_ref[pl.ds(r0, R), :] = (xv * lax.rsqrt(ms + EPS) * pw).astype(BF16)
            return carry
        lax.fori_loop(0, TM // R, chunk, 0, unroll=2)

    row = pl.BlockSpec((TM, D), lambda i: (i, 0))
    return pl.pallas_call(
        body, name="prenorm",
        grid=(NTILE,),
        in_specs=[pl.BlockSpec(memory_space=pl.ANY), pl.BlockSpec((NMETA, D), lambda i: (0, 0)),
                  pl.BlockSpec((1, D), lambda i: (0, 0))],
        out_specs=[row, row],
        out_shape=[jax.ShapeDtypeStruct((TP, D), F32), jax.ShapeDtypeStruct((TP, D), BF16)],
        scratch_shapes=[pltpu.VMEM((2, TM, D), F32), pltpu.SemaphoreType.DMA((2,))],
        compiler_params=_cparams(),
    )(x, meta_full, pre_w)


def _inproj_cols(name, shards, hn, w_land, b_in, z_prev):
    nsh = shards.shape[0]

    def body(idx_ref, hn_ref, w_ref, b_ref, *rest):
        z_ref = rest[-2]
        z_ref[...] = jnp.dot(hn_ref[...], w_ref[...], preferred_element_type=F32) + b_ref[...]

    any_spec = pl.BlockSpec(memory_space=pl.ANY)
    in_specs = [pl.BlockSpec((TM, D), lambda j, i, idx: (i, 0)),
                pl.BlockSpec((D, _WCOLS), lambda j, i, idx: (0, idx[j])),
                pl.BlockSpec((1, _WCOLS), lambda j, i, idx: (0, idx[j]))]
    operands = [hn, w_land, b_in]
    aliases = {2: 1}
    if z_prev is not None:
        in_specs.append(any_spec)
        operands.append(z_prev)
        aliases[4] = 0
    return pl.pallas_call(
        body, name=name,
        grid_spec=pltpu.PrefetchScalarGridSpec(
            num_scalar_prefetch=1, grid=(nsh, TP // TM), in_specs=in_specs,
            out_specs=[pl.BlockSpec((TM, _WCOLS), lambda j, i, idx: (i, idx[j])), any_spec]),
        out_shape=[jax.ShapeDtypeStruct((TP, NIN), F32), jax.ShapeDtypeStruct(w_land.shape, w_land.dtype)],
        input_output_aliases=aliases,
        compiler_params=_cparams(),
    )(shards, *operands)


def _gate_values(ga, gx, xc, sp8):
    r = _sig(ga)
    i = _sig(gx)
    log_a = -(r * sp8)
    a = jnp.exp(log_a)
    mult = jnp.sqrt(-_expm1_neg(2.0 * log_a))
    return r, i, a, mult


def _lru_fwd(z, conv_w, conv_b, wa_g, b_a, wx_g, b_x, lam):
    def body(x_ref, g_ref, cw_ref, cb_ref, wa_ref, ba_ref, wx_ref, bx_ref, lam_ref,
             y_ref, xc_ref, hs_ref, ga_s, gx_s):
        taps = [cw_ref[k:k + 1, :] for k in range(LW)]
        cb = cb_ref[...]

        def conv_chunk(ci, carry):
            r0 = pl.multiple_of(ci * R, R)
            cur = x_ref[pl.ds(r0, R), :]
            p0 = pl.multiple_of(jnp.maximum(r0 - 8, 0), 8)
            prev = jnp.where(ci > 0, x_ref[pl.ds(p0, 8), :], 0.0)
            buf = jnp.concatenate([prev, cur], axis=0)
            acc = cur * taps[LW - 1] + cb
            for s in range(1, LW):
                acc = acc + pltpu.roll(buf, s, 0)[8:8 + R, :] * taps[LW - 1 - s]
            xc_ref[pl.ds(r0, R), :] = acc
            return carry
        lax.fori_loop(0, TP // R, conv_chunk, 0)

        def gate_chunk(ci, carry):
            r0 = pl.multiple_of(ci * TM, TM)
            xb = xc_ref[pl.ds(r0, TM), :].astype(BF16)
            ga_s[pl.ds(r0, TM), :] = jnp.dot(xb, wa_ref[...], preferred_element_type=F32) + ba_ref[...]
            gx_s[pl.ds(r0, TM), :] = jnp.dot(xb, wx_ref[...], preferred_element_type=F32) + bx_ref[...]
            return carry
        lax.fori_loop(0, TP // TM, gate_chunk, 0)

        sp8 = LRU_C * _softplus(-lam_ref[...])
        row = _row_iota((R, CB))

        def scan_chunk(ci, hprev):
            r0 = pl.multiple_of(ci * R, R)
            xc = xc_ref[pl.ds(r0, R), :]
            _, i, a, mult = _gate_values(ga_s[pl.ds(r0, R), :], gx_s[pl.ds(r0, R), :], xc, sp8)
            u = mult * (i * xc)
            k = 1
            while k < R:
                m = row >= k
                u = jnp.where(m, a * pltpu.roll(u, k, 0) + u, u)
                a = jnp.where(m, a * pltpu.roll(a, k, 0), a)
                k *= 2
            hv = u + a * hprev
            hs_ref[pl.ds(r0, R), :] = hv
            g = g_ref[pl.ds(r0, R), :]
            y_ref[pl.ds(r0, R), :] = (hv * (g * _sig(g))).astype(BF16)
            return jnp.sum(jnp.where(row == R - 1, hv, 0.0), axis=0, keepdims=True)
        lax.fori_loop(0, TP // R, scan_chunk, jnp.zeros((1, CB), F32))

    col = lambda off: pl.BlockSpec((TP, CB), lambda j: (0, off + j))
    vec = pl.BlockSpec((1, CB), lambda j: (0, j))
    wsp = pl.BlockSpec((None, CB, CB), lambda j: (j, 0, 0))
    return pl.pallas_call(
        body, name="lru_fwd",
        grid=(NCB,),
        in_specs=[col(0), col(NCB), pl.BlockSpec((LW, CB), lambda j: (0, j)), vec, wsp, vec, wsp, vec, vec],
        out_specs=[col(0), col(0), col(0)],
        out_shape=[jax.ShapeDtypeStruct((TP, DL), BF16), jax.ShapeDtypeStruct((TP, DL), F32),
                   jax.ShapeDtypeStruct((TP, DL), F32)],
        scratch_shapes=[pltpu.VMEM((TP, CB), F32), pltpu.VMEM((TP, CB), F32)],
        compiler_params=_cparams(),
    )(z, z, conv_w, conv_b, wa_g, b_a, wx_g, b_x, lam)


CBC = 128
NCBC = DC // CBC
RC = 64


def _fold_rows(v):
    acc = v[0:8, :]
    for r in range(8, v.shape[0], 8):
        acc = acc + v[r:r + 8, :]
    return acc


def _conf_fwd_conv(z, dw_w, dw_b):
    def body(u1_ref, u2_ref, w_ref, b_ref, vc_ref, vs):
        vs[pl.ds(0, KWP), :] = jnp.zeros((KWP, CBC), F32)

        def glu_chunk(ci, carry):
            r0 = pl.multiple_of(ci * RC, RC)
            vs[pl.ds(KWP + r0, RC), :] = u1_ref[pl.ds(r0, RC), :] * _sig(u2_ref[pl.ds(r0, RC), :])
            return carry
        lax.fori_loop(0, TP // RC, glu_chunk, 0)

        bias = b_ref[...]

        def conv_chunk(ci, carry):
            r0 = pl.multiple_of(ci * RC, RC)
            buf = vs[pl.ds(r0, KWP + RC), :]
            acc = jnp.zeros((RC, CBC), F32) + bias
            for rr in range(8):
                rolled = buf if rr == 0 else pltpu.roll(buf, rr, 0)
                for q in range(4):
                    s = 8 * q + rr
                    if s > KW - 1:
                        continue
                    k = KW - 1 - s
                    acc = acc + rolled[KWP - 8 * q:KWP - 8 * q + RC, :] * w_ref[k:k + 1, :]
            vc_ref[pl.ds(r0, RC), :] = acc
            return carry
        lax.fori_loop(0, TP // RC, conv_chunk, 0)

    return pl.pallas_call(
        body, name="conf_fwd_conv",
        grid=(NCBC,),
        in_specs=[pl.BlockSpec((TP, CBC), lambda j: (0, 2 * NCBC + j)),
                  pl.BlockSpec((TP, CBC), lambda j: (0, 3 * NCBC + j)),
                  pl.BlockSpec((KWP, CBC), lambda j: (0, j)),
                  pl.BlockSpec((1, CBC), lambda j: (0, j))],
        out_specs=pl.BlockSpec((TP, CBC), lambda j: (0, j)),
        out_shape=jax.ShapeDtypeStruct((TP, DC), F32),
        scratch_shapes=[pltpu.VMEM((TP + KWP, CBC), F32)],
        compiler_params=_cparams(),
    )(z, z, dw_w, dw_b)


def _ln_chunk(vc, lw, lb):
    mu = jnp.mean(vc, axis=-1, keepdims=True)
    xm = vc - mu
    var = jnp.mean(xm * xm, axis=-1, keepdims=True)
    rstd = lax.rsqrt(var + EPS)
    xhat = xm * rstd
    return xhat, rstd, xhat * lw + lb


def _conf_fwd_proj(vc, z, ln_w, ln_b, pw_w, pw_b):
    def body(vc_ref, g_ref, lw_ref, lb_ref, w_ref, b_ref, y_ref, p_ref, s_s):
        lw, lb = lw_ref[...], lb_ref[...]

        def ln_chunk(ci, carry):
            r0 = pl.multiple_of(ci * R, R)
            for half in range(2):
                rr = r0 + 8 * half
                _, _, ln = _ln_chunk(vc_ref[pl.ds(rr, 8), :], lw, lb)
                p_ref[pl.ds(rr, 8), :] = ln * _sig(ln)
            s_s[pl.ds(r0, R), :] = p_ref[pl.ds(r0, R), :].astype(BF16)
            return carry
        lax.fori_loop(0, TM // R, ln_chunk, 0, unroll=2)

        p_ref[...] = jnp.dot(s_s[...], w_ref[...], preferred_element_type=F32) + b_ref[...]

        def out_chunk(ci, carry):
            r0 = pl.multiple_of(ci * R, R)
            g = g_ref[pl.ds(r0, R), :]
            y_ref[pl.ds(r0, R), :] = (p_ref[pl.ds(r0, R), :] * (g * _sig(g))).astype(BF16)
            return carry
        lax.fori_loop(0, TM // R, out_chunk, 0)

    row = pl.BlockSpec((TM, DC), lambda i: (i, 0))
    vec = pl.BlockSpec((1, DC), lambda i: (0, 0))
    return pl.pallas_call(
        body, name="conf_fwd_proj",
        grid=(TP // TM,),
        in_specs=[row, pl.BlockSpec((TM, DC), lambda i: (i, 4)), vec, vec,
                  pl.BlockSpec((DC, DC), lambda i: (0, 0)), vec],
        out_specs=[row, row],
        out_shape=[jax.ShapeDtypeStruct((TP, DC), BF16), jax.ShapeDtypeStruct((TP, DC), F32)],
        scratch_shapes=[pltpu.VMEM((TM, DC), BF16)],
        compiler_params=_cparams(),
    )(vc, z, ln_w, ln_b, pw_w, pw_b)


def _outproj_loss(ylru, yconf, w_out, h, target, post_w):
    def body(yl_ref, yc_ref, w_ref, h_ref, tgt_hbm, pw_ref, dout_ref, dy_ref, loss_ref, dpw_ref, y_s, t_ref, sem):
        i = pl.program_id(0)
        k = pl.program_id(1)

        @pl.when(k == 0)
        def _():
            _for_tile(i, lambda t: _token_tile_copy(tgt_hbm, t_ref, sem, t).start())
            y_s[...] = jnp.dot(yl_ref[...], w_ref[...], preferred_element_type=F32)

        @pl.when(k == 1)
        def _():
            y_s[...] += jnp.dot(yc_ref[...], w_ref[...], preferred_element_type=F32)

        @pl.when(jnp.logical_and(i == 0, k == 1))
        def _():
            loss_ref[...] = jnp.zeros_like(loss_ref)
            dpw_ref[...] = jnp.zeros_like(dpw_ref)

        @pl.when(k == 1)
        def _():
            _for_tile(i, lambda t: _token_tile_copy(tgt_hbm, t_ref, sem, t).wait())

            @pl.when(i == 0)
            def _():
                t_ref[0:NMETA, :] = jnp.zeros((NMETA, D), F32)

            @pl.when(i == NTILE - 1)
            def _():
                last = _tile_rows(NTILE - 1)[1]
                t_ref[last:TM, :] = jnp.zeros((TM - last, D), F32)

            pw = pw_ref[...]
            row = _row_iota((8, D))

            def chunk(ci, carry):
                r0 = pl.multiple_of(ci * 8, 8)
                yv = y_s[pl.ds(r0, 8), :]
                rs = lax.rsqrt(jnp.mean(yv * yv, axis=-1, keepdims=True) + EPS)
                grow = row + (i * TM + r0)
                valid = jnp.logical_and(grow >= NMETA, grow < T)
                yn = yv * rs
                err = jnp.where(valid, h_ref[pl.ds(r0, 8), :] + yn * pw - t_ref[pl.ds(r0, 8), :], 0.0)
                loss_ref[...] += err * err
                d_rn = err * (1.0 / D)
                dout_ref[pl.ds(r0, 8), :] = d_rn
                dpw_ref[...] += d_rn * yn
                gw = d_rn * pw
                dot = jnp.mean(gw * yv, axis=-1, keepdims=True)
                dy_ref[pl.ds(r0, 8), :] = (rs * gw - yv * (rs * rs * rs * dot)).astype(BF16)
                return carry
            lax.fori_loop(0, TM // 8, chunk, 0, unroll=4)

    row = pl.BlockSpec((TM, D), lambda i, k: (i, 0))
    half = pl.BlockSpec((TM, DL), lambda i, k: (i, 0))
    acc = pl.BlockSpec((8, D), lambda i, k: (0, 0))
    return pl.pallas_call(
        body, name="outproj_loss",
        grid=(TP // TM, 2),
        in_specs=[half, half, pl.BlockSpec((DL, D), lambda i, k: (k, 0)), row, pl.BlockSpec(memory_space=pl.ANY),
                  pl.BlockSpec((1, D), lambda i, k: (0, 0))],
        out_specs=[row, row, acc, acc],
        out_shape=[jax.ShapeDtypeStruct((TP, D), F32), jax.ShapeDtypeStruct((TP, D), BF16),
                   jax.ShapeDtypeStruct((8, D), F32), jax.ShapeDtypeStruct((8, D), F32)],
        scratch_shapes=[pltpu.VMEM((TM, D), F32), pltpu.VMEM((TM, D), F32), pltpu.SemaphoreType.DMA(())],
        compiler_params=_cparams(),
    )(ylru, yconf, w_out, h, target, post_w)


_NT = (((1,), (1,)), ((), ()))
_TN = (((0,), (0,)), ((), ()))


def _outproj_bwd(dy, ylru, yconf, w_out):
    def body(dy_ref, yl_ref, yc_ref, w_ref, dycat_ref, dw_ref):
        j = pl.program_id(0)
        dyv = dy_ref[...]
        dycat_ref[...] = lax.dot_general(dyv, w_ref[...], _NT, preferred_element_type=F32)

        @pl.when(j < NCB)
        def _():
            dw_ref[...] = lax.dot_general(yl_ref[...], dyv, _TN, preferred_element_type=F32).astype(BF16)

        @pl.when(j >= NCB)
        def _():
            dw_ref[...] = lax.dot_general(yc_ref[...], dyv, _TN, preferred_element_type=F32).astype(BF16)

    return pl.pallas_call(
        body, name="outproj_bwd",
        grid=(2 * NCB,),
        in_specs=[pl.BlockSpec((TP, D), lambda j: (0, 0)),
                  pl.BlockSpec((TP, CB), lambda j: (0, jnp.minimum(j, NCB - 1))),
                  pl.BlockSpec((TP, CB), lambda j: (0, jnp.maximum(j - NCB, 0))),
                  pl.BlockSpec((CB, D), lambda j: (j, 0))],
        out_specs=[pl.BlockSpec((TP, CB), lambda j: (0, j)), pl.BlockSpec((CB, D), lambda j: (j, 0))],
        out_shape=[jax.ShapeDtypeStruct((TP, D), F32), jax.ShapeDtypeStruct((D, D), BF16)],
        compiler_params=_cparams(),
    )(dy, ylru, yconf, w_out)


def _conf_bwd_proj(dycat, p, z, vc, ln_w, ln_b, pw_w):
    def body(dy_ref, p_ref, g_ref, vc_ref, lw_ref, lb_ref, w_ref,
             dvc_ref, dgc_ref, dpw_ref, vecs_ref, dp_s, s_s, ds_s):
        i = pl.program_id(0)
        lw, lb = lw_ref[...], lb_ref[...]

        @pl.when(i == 0)
        def _():
            dpw_ref[...] = jnp.zeros_like(dpw_ref)
            vecs_ref[...] = jnp.zeros_like(vecs_ref)

        def pre_chunk(ci, carry):
            r0 = pl.multiple_of(ci * R, R)
            for half in range(2):
                rr = r0 + 8 * half
                dyv = dy_ref[pl.ds(rr, 8), :]
                g = g_ref[pl.ds(rr, 8), :]
                sg = _sig(g)
                dp = dyv * (g * sg)
                dg = dyv * p_ref[pl.ds(rr, 8), :] * (sg * (1.0 + g * (1.0 - sg)))
                vecs_ref[0:8, :] += dp
                vecs_ref[8:16, :] += dg
                ds_s[pl.ds(rr, 8), :] = dp
                dvc_ref[pl.ds(rr, 8), :] = dg
            dp_s[pl.ds(r0, R), :] = ds_s[pl.ds(r0, R), :].astype(BF16)
            dgc_ref[pl.ds(r0, R), :] = dvc_ref[pl.ds(r0, R), :].astype(BF16)
            for half in range(2):
                rr = r0 + 8 * half
                _, _, ln = _ln_chunk(vc_ref[pl.ds(rr, 8), :], lw, lb)
                ds_s[pl.ds(rr, 8), :] = ln * _sig(ln)
            s_s[pl.ds(r0, R), :] = ds_s[pl.ds(r0, R), :].astype(BF16)
            return carry
        lax.fori_loop(0, TM // R, pre_chunk, 0, unroll=2)

        dpb = dp_s[...]
        ds_s[...] = lax.dot_general(dpb, w_ref[...], _NT, preferred_element_type=F32)
        dpw_ref[...] += lax.dot_general(s_s[...], dpb, _TN, preferred_element_type=F32)

        def post_chunk(ci, carry):
            r0 = pl.multiple_of(ci * 8, 8)
            xhat, rstd, ln = _ln_chunk(vc_ref[pl.ds(r0, 8), :], lw, lb)
            sl = _sig(ln)
            dln = ds_s[pl.ds(r0, 8), :] * (sl * (1.0 + ln * (1.0 - sl)))
            vecs_ref[16:24, :] += dln * xhat
            vecs_ref[24:32, :] += dln
            dxh = dln * lw
            m1 = jnp.mean(dxh, axis=-1, keepdims=True)
            m2 = jnp.mean(dxh * xhat, axis=-1, keepdims=True)
            dvc_ref[pl.ds(r0, 8), :] = rstd * (dxh - m1 - xhat * m2)
            return carry
        lax.fori_loop(0, TM // 8, post_chunk, 0, unroll=4)

    row = pl.BlockSpec((TM, DC), lambda i: (i, 0))
    vec = pl.BlockSpec((1, DC), lambda i: (0, 0))
    return pl.pallas_call(
        body, name="conf_bwd_proj",
        grid=(TP // TM,),
        in_specs=[pl.BlockSpec((TM, DC), lambda i: (i, 1)), row, pl.BlockSpec((TM, DC), lambda i: (i, 4)), row,
                  vec, vec, pl.BlockSpec((DC, DC), lambda i: (0, 0))],
        out_specs=[row, row, pl.BlockSpec((DC, DC), lambda i: (0, 0)), pl.BlockSpec((32, DC), lambda i: (0, 0))],
        out_shape=[jax.ShapeDtypeStruct((TP, DC), F32), jax.ShapeDtypeStruct((TP, DC), BF16),
                   jax.ShapeDtypeStruct((DC, DC), F32), jax.ShapeDtypeStruct((32, DC), F32)],
        scratch_shapes=[pltpu.VMEM((TM, DC), BF16), pltpu.VMEM((TM, DC), BF16), pltpu.VMEM((TM, DC), F32)],
        compiler_params=_cparams(),
    )(dycat, p, z, vc, ln_w, ln_b, pw_w)


def _conf_bwd_conv(dvc, z, dw_w):
    def body(dvc_ref, u1_ref, u2_ref, w_ref, du_ref, dw_ref, vecs_ref, vs, dvs):
        vs[pl.ds(0, KWP), :] = jnp.zeros((KWP, CBC), F32)
        dvs[pl.ds(TP, KWP), :] = jnp.zeros((KWP, CBC), F32)
        dw_ref[...] = jnp.zeros_like(dw_ref)
        vecs_ref[...] = jnp.zeros_like(vecs_ref)

        def fill_chunk(ci, carry):
            r0 = pl.multiple_of(ci * RC, RC)
            vs[pl.ds(KWP + r0, RC), :] = u1_ref[pl.ds(r0, RC), :] * _sig(u2_ref[pl.ds(r0, RC), :])
            dv = dvc_ref[pl.ds(r0, RC), :]
            dvs[pl.ds(r0, RC), :] = dv
            vecs_ref[0:8, :] += _fold_rows(dv)
            return carry
        lax.fori_loop(0, TP // RC, fill_chunk, 0)

        def conv_chunk(ci, carry):
            r0 = pl.multiple_of(ci * RC, RC)
            vbuf = vs[pl.ds(r0, KWP + RC), :]
            dbuf = dvs[pl.ds(r0, KWP + RC), :]
            dcur = dbuf[0:RC, :]
            dv = jnp.zeros((RC, CBC), F32)
            for rr in range(8):
                vroll = vbuf if rr == 0 else pltpu.roll(vbuf, rr, 0)
                droll = dbuf if rr == 0 else pltpu.roll(dbuf, KWP + RC - rr, 0)
                for q in range(4):
                    s = 8 * q + rr
                    if s > KW - 1:
                        continue
                    k = KW - 1 - s
                    dv = dv + droll[8 * q:8 * q + RC, :] * w_ref[k:k + 1, :]
                    dw_ref[8 * k:8 * k + 8, :] += _fold_rows(dcur * vroll[KWP - 8 * q:KWP - 8 * q + RC, :])
            u1 = u1_ref[pl.ds(r0, RC), :]
            sg = _sig(u2_ref[pl.ds(r0, RC), :])
            du1 = dv * sg
            du2 = dv * u1 * (sg * (1.0 - sg))
            du_ref[0, pl.ds(r0, RC), :] = du1.astype(BF16)
            du_ref[1, pl.ds(r0, RC), :] = du2.astype(BF16)
            vecs_ref[8:16, :] += _fold_rows(du1)
            vecs_ref[16:24, :] += _fold_rows(du2)
            return carry
        lax.fori_loop(0, TP // RC, conv_chunk, 0)

    blk = pl.BlockSpec((TP, CBC), lambda j: (0, j))
    return pl.pallas_call(
        body, name="conf_bwd_conv",
        grid=(NCBC,),
        in_specs=[blk, pl.BlockSpec((TP, CBC), lambda j: (0, 2 * NCBC + j)),
                  pl.BlockSpec((TP, CBC), lambda j: (0, 3 * NCBC + j)), pl.BlockSpec((KWP, CBC), lambda j: (0, j))],
        out_specs=[pl.BlockSpec((2, TP, CBC), lambda j: (0, 0, j)), pl.BlockSpec((8 * KWP, CBC), lambda j: (0, j)),
                   pl.BlockSpec((24, CBC), lambda j: (0, j))],
        out_shape=[jax.ShapeDtypeStruct((2, TP, DC), BF16),
                   jax.ShapeDtypeStruct((8 * KWP, DC), F32), jax.ShapeDtypeStruct((24, DC), F32)],
        scratch_shapes=[pltpu.VMEM((TP + KWP, CBC), F32), pltpu.VMEM((TP + KWP, CBC), F32)],
        compiler_params=_cparams(),
    )(dvc, z, z, dw_w)


def _lru_bwd(dycat, z, xc, hs, conv_w, wa_g, b_a, wx_g, b_x, lam):
    NV = 6

    def body(dy_ref, x_ref, g_ref, xc_ref, hs_ref, cw_ref, wa_ref, ba_ref, wx_ref, bx_ref, lam_ref,
             dzl_ref, dwa_ref, dwx_ref, dcw_ref, vecs_ref, ga_s, gx_s, dxc_s):
        vecs_ref[...] = jnp.zeros_like(vecs_ref)
        dcw_ref[...] = jnp.zeros_like(dcw_ref)
        dxc_s[pl.ds(TP, 8), :] = jnp.zeros((8, CB), F32)

        def gate_chunk(ci, carry):
            r0 = pl.multiple_of(ci * TM, TM)
            xb = xc_ref[pl.ds(r0, TM), :].astype(BF16)
            ga_s[pl.ds(r0, TM), :] = jnp.dot(xb, wa_ref[...], preferred_element_type=F32) + ba_ref[...]
            gx_s[pl.ds(r0, TM), :] = jnp.dot(xb, wx_ref[...], preferred_element_type=F32) + bx_ref[...]
            return carry
        lax.fori_loop(0, TP // TM, gate_chunk, 0)

        sp8 = LRU_C * _softplus(-lam_ref[...])
        row = _row_iota((R, CB))
        nchunk = TP // R

        def scan_chunk(cj, carry):
            a_next, lam_next = carry
            ci = nchunk - 1 - cj
            r0 = pl.multiple_of(ci * R, R)
            dyv = dy_ref[pl.ds(r0, R), :]
            g = g_ref[pl.ds(r0, R), :]
            hv = hs_ref[pl.ds(r0, R), :]
            xc = xc_ref[pl.ds(r0, R), :]
            sg = _sig(g)
            dgl = dyv * hv * (sg * (1.0 + g * (1.0 - sg)))
            dzl_ref[1, pl.ds(r0, R), :] = dgl.astype(BF16)
            vecs_ref[0:8, :] += _fold8(dgl)
            dhs = dyv * (g * sg)
            r, i, a, mult = _gate_values(ga_s[pl.ds(r0, R), :], gx_s[pl.ds(r0, R), :], xc, sp8)
            b = jnp.where(row == R - 1, a_next, pltpu.roll(a, R - 1, 0))
            lv = dhs
            k = 1
            while k < R:
                m = row < R - k
                lv = jnp.where(m, lv + b * pltpu.roll(lv, R - k, 0), lv)
                b = jnp.where(m, b * pltpu.roll(b, R - k, 0), b)
                k *= 2
            lv = lv + b * lam_next
            p0 = pl.multiple_of(jnp.maximum(r0 - 8, 0), 8)
            hprev8 = jnp.where(ci > 0, hs_ref[pl.ds(p0, 8), :], 0.0)
            hprev = pltpu.roll(jnp.concatenate([hprev8, hv], axis=0), 1, 0)[8:8 + R, :]
            da = lv * hprev
            ixc = i * xc
            dmult = lv * ixc
            di = lv * mult * xc
            dxc_s[pl.ds(r0, R), :] = lv * mult * i
            a2 = a * a
            dlog_a = da * a - dmult * a2 / mult
            vecs_ref[32:40, :] += _fold8(dlog_a * r)
            dga = -(dlog_a * sp8) * r * (1.0 - r)
            dgx = di * i * (1.0 - i)
            ga_s[pl.ds(r0, R), :] = dga
            gx_s[pl.ds(r0, R), :] = dgx
            vecs_ref[16:24, :] += _fold8(dga)
            vecs_ref[24:32, :] += _fold8(dgx)
            a_first = jnp.sum(jnp.where(row == 0, a, 0.0), axis=0, keepdims=True)
            l_first = jnp.sum(jnp.where(row == 0, lv, 0.0), axis=0, keepdims=True)
            return a_first, l_first
        lax.fori_loop(0, nchunk, scan_chunk, (jnp.zeros((1, CB), F32), jnp.zeros((1, CB), F32)))

        dwa_ref[...] = jnp.zeros_like(dwa_ref)
        dwx_ref[...] = jnp.zeros_like(dwx_ref)

        def mm_chunk(ci, carry):
            r0 = pl.multiple_of(ci * TM, TM)
            xb = xc_ref[pl.ds(r0, TM), :].astype(BF16)
            dgab = ga_s[pl.ds(r0, TM), :].astype(BF16)
            dgxb = gx_s[pl.ds(r0, TM), :].astype(BF16)
            dxc_s[pl.ds(r0, TM), :] += (lax.dot_general(dgab, wa_ref[...], _NT, preferred_element_type=F32)
                                        + lax.dot_general(dgxb, wx_ref[...], _NT, preferred_element_type=F32))
            dwa_ref[...] += lax.dot_general(xb, dgab, _TN, preferred_element_type=F32)
            dwx_ref[...] += lax.dot_general(xb, dgxb, _TN, preferred_element_type=F32)
            return carry
        lax.fori_loop(0, TP // TM, mm_chunk, 0)

        taps = [cw_ref[k:k + 1, :] for k in range(LW)]

        def conv_chunk(ci, carry):
            r0 = pl.multiple_of(ci * R, R)
            dbuf = dxc_s[pl.ds(r0, R + 8), :]
            dcur = dbuf[0:R, :]
            p0 = pl.multiple_of(jnp.maximum(r0 - 8, 0), 8)
            xprev = jnp.where(ci > 0, x_ref[pl.ds(p0, 8), :], 0.0)
            xbuf = jnp.concatenate([xprev, x_ref[pl.ds(r0, R), :]], axis=0)
            dxl = dcur * taps[LW - 1]
            dcw_ref[8 * (LW - 1):8 * LW, :] += _fold8(dcur * xbuf[8:8 + R, :])
            for s in range(1, LW):
                k = LW - 1 - s
                dxl = dxl + pltpu.roll(dbuf, R + 8 - s, 0)[0:R, :] * taps[k]
                dcw_ref[8 * k:8 * k + 8, :] += _fold8(dcur * pltpu.roll(xbuf, s, 0)[8:8 + R, :])
            dzl_ref[0, pl.ds(r0, R), :] = dxl.astype(BF16)
            vecs_ref[8:16, :] += _fold8(dxl)
            vecs_ref[40:48, :] += _fold8(dcur)
            return carry
        lax.fori_loop(0, TP // R, conv_chunk, 0)
        vecs_ref[32:40, :] = vecs_ref[32:40, :] * (LRU_C * _sig(-lam_ref[...]))

    col = lambda off: pl.BlockSpec((TP, CB), lambda j: (0, off + j))
    vec = pl.BlockSpec((1, CB), lambda j: (0, j))
    wsp = pl.BlockSpec((None, CB, CB), lambda j: (j, 0, 0))
    return pl.pallas_call(
        body, name="lru_bwd",
        grid=(NCB,),
        in_specs=[col(0), col(0), col(NCB), col(0), col(0), pl.BlockSpec((LW, CB), lambda j: (0, j)),
                  wsp, vec, wsp, vec, vec],
        out_specs=[pl.BlockSpec((2, TP, CB), lambda j: (0, 0, j)), wsp, wsp,
                   pl.BlockSpec((8 * LW, CB), lambda j: (0, j)), pl.BlockSpec((8 * NV, CB), lambda j: (0, j))],
        out_shape=[jax.ShapeDtypeStruct((2, TP, DL), BF16),
                   jax.ShapeDtypeStruct((NCB, CB, CB), F32), jax.ShapeDtypeStruct((NCB, CB, CB), F32),
                   jax.ShapeDtypeStruct((8 * LW, DL), F32), jax.ShapeDtypeStruct((8 * NV, DL), F32)],
        scratch_shapes=[pltpu.VMEM((TP, CB), F32), pltpu.VMEM((TP, CB), F32), pltpu.VMEM((TP + 8, CB), F32)],
        compiler_params=_cparams(),
    )(dycat, z, z, xc, hs, conv_w, wa_g, b_a, wx_g, b_x, lam)


def _dz_section(sec, dzl_ref, dzc_ref, dgc_ref, use):
    @pl.when(sec < 2)
    def _():
        use(dzl_ref)

    @pl.when(jnp.logical_and(sec >= 2, sec < 4))
    def _():
        use(dzc_ref)

    @pl.when(sec == 4)
    def _():
        use(dgc_ref)


def _dz_specs(rows, index):
    return [pl.BlockSpec((None, rows, 1024), lambda a, b: (jnp.minimum(index(a, b)[1], 1), index(a, b)[0], 0)),
            pl.BlockSpec((None, rows, 1024), lambda a, b: (jnp.clip(index(a, b)[1] - 2, 0, 1), index(a, b)[0], 0)),
            pl.BlockSpec((rows, 1024), lambda a, b: (index(a, b)[0], 0))]


def _inproj_wgrad(name, hn, dzs):
    KB = 512
    nsec = dzs.shape[0]

    def body(hn_ref, dz_ref, dw_ref):
        dw_ref[...] = lax.dot_general(hn_ref[...], dz_ref[...], _TN, preferred_element_type=F32).astype(BF16)

    return pl.pallas_call(
        body, name=name,
        grid=(nsec, D // KB),
        in_specs=[pl.BlockSpec((TP, KB), lambda n, kb: (0, kb)),
                  pl.BlockSpec((None, TP, 1024), lambda n, kb: (n, 0, 0))],
        out_specs=pl.BlockSpec((KB, 1024), lambda n, kb: (kb, n)),
        out_shape=jax.ShapeDtypeStruct((D, nsec * 1024), BF16),
        compiler_params=_cparams(),
    )(hn, dzs)


def _sum_win_parts(parts_a, parts_b, parts_c):
    RB = 64

    def body(a_ref, b_ref, c_ref, o_ref):
        def chunk(ci, carry):
            r0 = pl.multiple_of(ci * R, R)
            for ref, base, ncol in ((a_ref, 0, 2048), (b_ref, 2048, 2048), (c_ref, 4096, 1024)):
                for c0 in range(0, ncol, 512):
                    acc = ref[0, pl.ds(r0, R), c0:c0 + 512].astype(F32)
                    for sidx in range(1, NDEV):
                        acc = acc + ref[sidx, pl.ds(r0, R), c0:c0 + 512].astype(F32)
                    o_ref[pl.ds(r0, R), base + c0:base + c0 + 512] = acc.astype(BF16)
            return carry
        lax.fori_loop(0, RB // R, chunk, 0)

    spec = lambda ncol: pl.BlockSpec((NDEV, RB, ncol), lambda i: (0, i, 0))
    return pl.pallas_call(
        body, name="sum_win_parts",
        grid=(D // NDEV // RB,),
        in_specs=[spec(2048), spec(2048), spec(1024)],
        out_specs=pl.BlockSpec((RB, NIN), lambda i: (i, 0)),
        out_shape=jax.ShapeDtypeStruct((D // NDEV, NIN), BF16),
        compiler_params=_cparams(),
    )(parts_a, parts_b, parts_c)


def _inproj_bwd(dzl, dzc, dgc, w_in, h, dout, pre_w):
    nsec = NIN // 1024

    def body(dzl_ref, dzc_ref, dgc_ref, w_ref, h_ref, dout_ref, pw_ref, gx_hbm, dmeta_ref, dpw_ref, acc_s, dh_s, sem):
        i = pl.program_id(0)
        s = pl.program_id(1)

        def gx_copy(t):
            lo, n, off = _tile_rows(t)
            return pltpu.make_async_copy(dh_s.at[pl.ds(off, n)], gx_hbm.at[pl.ds(lo, n)], sem)

        @pl.when(s == 0)
        def _():
            acc_s[...] = jnp.zeros_like(acc_s)

        def use(dz_ref):
            acc_s[...] += lax.dot_general(dz_ref[...], w_ref[...], _NT, preferred_element_type=F32)
        _dz_section(s, dzl_ref, dzc_ref, dgc_ref, use)

        @pl.when(jnp.logical_and(i == 0, s == nsec - 1))
        def _():
            dpw_ref[...] = jnp.zeros_like(dpw_ref)

        @pl.when(s == nsec - 1)
        def _():
            _for_tile(i - 1, lambda t: gx_copy(t).wait())
            pw = pw_ref[...]

            def chunk(ci, carry):
                r0 = pl.multiple_of(ci * 8, 8)
                hv = h_ref[pl.ds(r0, 8), :]
                dhn = acc_s[pl.ds(r0, 8), :]
                rs = lax.rsqrt(jnp.mean(hv * hv, axis=-1, keepdims=True) + EPS)
                dpw_ref[...] += dhn * (hv * rs)
                gw = dhn * pw
                dot = jnp.mean(gw * hv, axis=-1, keepdims=True)
                dh_s[pl.ds(r0, 8), :] = rs * gw - hv * (rs * rs * rs * dot) + dout_ref[pl.ds(r0, 8), :]
                return carry
            lax.fori_loop(0, TM // 8, chunk, 0, unroll=4)
            _for_tile(i, lambda t: gx_copy(t).start())

            @pl.when(i == 0)
            def _():
                dmeta_ref[...] = dh_s[0:NMETA, :]

            @pl.when(i == NTILE - 1)
            def _():
                gx_copy(NTILE - 1).wait()

    row = pl.BlockSpec((TM, D), lambda i, s: (i, 0))
    return pl.pallas_call(
        body, name="inproj_bwd",
        grid=(TP // TM, nsec),
        in_specs=_dz_specs(TM, lambda i, s: (i, s)) + [
            pl.BlockSpec((D, 1024), lambda i, s: (0, s)), row, row, pl.BlockSpec((1, D), lambda i, s: (0, 0))],
        out_specs=[pl.BlockSpec(memory_space=pl.ANY), pl.BlockSpec((NMETA, D), lambda i, s: (0, 0)),
                   pl.BlockSpec((8, D), lambda i, s: (0, 0))],
        out_shape=[jax.ShapeDtypeStruct((SEQ, D), F32), jax.ShapeDtypeStruct((NMETA, D), F32),
                   jax.ShapeDtypeStruct((8, D), F32)],
        scratch_shapes=[pltpu.VMEM((TM, D), F32), pltpu.VMEM((TM, D), F32), pltpu.SemaphoreType.DMA(())],
        compiler_params=_cparams(),
    )(dzl, dzc, dgc, w_in, h, dout, pre_w)


def _adamw(name, parts, w, m, v, block_rows):
    rows, cols = w.shape
    nparts = parts.shape[0]
    cw = cols if cols <= 640 else 512

    def body(p_ref, w_ref, m_ref, v_ref, g_ref, d_ref, nm_ref, nv_ref):
        def chunk(ci, carry):
            r0 = pl.multiple_of(ci * R, R)
            for c0 in range(0, cols, cw):
                at = (pl.ds(r0, R), slice(c0, c0 + cw))
                g = p_ref[(0,) + at].astype(F32)
                for sidx in range(1, nparts):
                    g = g + p_ref[(sidx,) + at].astype(F32)
                delta, mv, vv = _adam_math(g, w_ref[at], m_ref[at], v_ref[at])
                g_ref[at] = g
                nm_ref[at] = mv
                nv_ref[at] = vv
                d_ref[at] = delta
            return carry
        lax.fori_loop(0, block_rows // R, chunk, 0)

    blk = pl.BlockSpec((block_rows, cols), lambda i: (i, 0))
    shp = jax.ShapeDtypeStruct((rows, cols), F32)
    return pl.pallas_call(
        body, name=name,
        grid=(rows // block_rows,),
        in_specs=[pl.BlockSpec((nparts, block_rows, cols), lambda i: (0, i, 0)), blk, blk, blk],
        out_specs=[blk, blk, blk, blk],
        out_shape=[shp, shp, shp, shp],
        compiler_params=_cparams(),
    )(parts, w, m, v)


def _adam_math(g, w, m, v):
    c1 = 1.0 / (1.0 - ADAM_B1 ** ADAM_STEP)
    c2 = 1.0 / (1.0 - ADAM_B2 ** ADAM_STEP)
    mv = ADAM_B1 * m + (1.0 - ADAM_B1) * g
    vv = ADAM_B2 * v + (1.0 - ADAM_B2) * (g * g)
    upd = (mv * c1) / (jnp.sqrt(vv * c2) + ADAM_EPS) + ADAM_WD * w
    return -ADAM_LR * upd, mv, vv


_VEC = [("pre_norm_w", 2), ("post_norm_w", 2), ("b_in", 5), ("lru_conv_b", 1), ("b_gate_a", 1), ("b_gate_x", 1),
        ("lru_lambda", 1), ("conf_dw_b", 1), ("conf_ln_w", 1), ("conf_ln_b", 1), ("conf_pw_b", 1)]
_VEC_ROWS = 24
_LOSS_ROW = 17
_SM_ROWS = 64


def _pack_grads(dprew_acc, dpostw_acc, cvecs, kvecs, lvecs, dcw_acc, ddw_acc, dh, loss_acc):
    def body(pre_ref, post_ref, c_ref, k_ref, l_ref, dcw_ref, ddw_ref, dh_ref, loss_ref, vec_ref, small_ref, tmp):
        s8 = lambda ref, r: jnp.sum(ref[8 * r:8 * r + 8, :], axis=0, keepdims=True)
        vec_ref[...] = jnp.zeros_like(vec_ref)
        pre, post = s8(pre_ref, 0), s8(post_ref, 0)
        rows = [pre[:, 0:1024], pre[:, 1024:2048], post[:, 0:1024], post[:, 1024:2048],
                s8(l_ref, 1), s8(l_ref, 0), s8(k_ref, 1), s8(k_ref, 2), s8(c_ref, 1),
                s8(l_ref, 5), s8(l_ref, 2), s8(l_ref, 3), s8(l_ref, 4),
                s8(k_ref, 0), s8(c_ref, 2), s8(c_ref, 3), s8(c_ref, 0)]
        for r, val in enumerate(rows):
            vec_ref[r:r + 1, :] = val
        vec_ref[_LOSS_ROW:_LOSS_ROW + 1, :] = jnp.zeros((1, 1024), F32) + (0.5 / D) * jnp.sum(loss_ref[...])

        small_ref[...] = jnp.zeros_like(small_ref)
        for k in range(LW):
            tmp[k:k + 1, :] = s8(dcw_ref, k)
        for k in range(KW):
            tmp[8 + k:9 + k, :] = s8(ddw_ref, k)
        for d in range(NDEV):
            small_ref[d, 0:LW, 0:128] = tmp[0:LW, 128 * d:128 * d + 128]
            small_ref[d, 8:8 + KW, 0:128] = tmp[8:8 + KW, 128 * d:128 * d + 128]
            small_ref[d, 40:56, :] = dh_ref[:, 256 * d:256 * d + 256]

    full = lambda a: pl.BlockSpec(a.shape, lambda i: (0,) * a.ndim)
    ins = [dprew_acc, dpostw_acc, cvecs, kvecs, lvecs, dcw_acc, ddw_acc]
    return pl.pallas_call(
        body, name="pack_grads",
        grid=(1,),
        in_specs=[full(a) for a in ins] + [full(dh), full(loss_acc)],
        out_specs=[pl.BlockSpec((_VEC_ROWS, 1024), lambda i: (0, 0)),
                   pl.BlockSpec((NDEV, _SM_ROWS, 256), lambda i: (0, 0, 0))],
        out_shape=[jax.ShapeDtypeStruct((_VEC_ROWS, 1024), F32), jax.ShapeDtypeStruct((NDEV, _SM_ROWS, 256), F32)],
        scratch_shapes=[pltpu.VMEM((40, 1024), F32)],
        compiler_params=_cparams(),
    )(*ins, dh, loss_acc)


def _adamw_vec(parts, W, M, V):
    nv = len(_VEC)

    def body(*refs):
        p_ref = refs[0]
        w_refs, m_refs, v_refs = refs[1:1 + nv], refs[1 + nv:1 + 2 * nv], refs[1 + 2 * nv:1 + 3 * nv]
        outs = refs[1 + 3 * nv:]

        def total(r):
            acc = p_ref[0, r:r + 1, :]
            for sidx in range(1, NDEV):
                acc = acc + p_ref[sidx, r:r + 1, :]
            return acc

        row = 0
        for idx, (_, nrows) in enumerate(_VEC):
            for part in range(nrows):
                cols = slice(1024 * part, 1024 * part + 1024)
                g = total(row + part)
                delta, mv, vv = _adam_math(g, w_refs[idx][:, cols], m_refs[idx][:, cols], v_refs[idx][:, cols])
                for o, val in zip(outs[4 * idx:4 * idx + 4], (g, delta, mv, vv)):
                    o[:, cols] = val
            row += nrows
        outs[-1][...] = total(_LOSS_ROW)[:, 0:128]

    names = [n for n, _ in _VEC]
    flat = lambda d: [d[n].reshape(1, -1) for n in names]
    ws, ms, vs = flat(W), flat(M), flat(V)
    res = pl.pallas_call(
        body, name="adamw_vec",
        out_shape=[jax.ShapeDtypeStruct(w.shape, F32) for w in ws for _ in range(4)]
        + [jax.ShapeDtypeStruct((1, 128), F32)],
        compiler_params=_cparams(),
    )(parts, *ws, *ms, *vs)
    return {n: tuple(res[4 * i:4 * i + 4]) for i, n in enumerate(names)}, res[-1]


def _adamw_small(parts, W, M, V):
    where = {"lru_conv_w": (slice(0, LW), slice(0, 128)), "conf_dw_w": (slice(8, 8 + KW), slice(0, 128)),
             "meta_tokens": (slice(40, 56), slice(0, 256))}
    names = list(where)

    def body(*refs):
        p_ref = refs[0]
        outs = refs[10:]
        for idx, n in enumerate(names):
            rs, cs = where[n]
            g = p_ref[0, rs, cs]
            for sidx in range(1, NDEV):
                g = g + p_ref[sidx, rs, cs]
            delta, mv, vv = _adam_math(g, refs[1 + idx][...], refs[4 + idx][...], refs[7 + idx][...])
            for o, val in zip(outs[4 * idx:4 * idx + 4], (g, delta, mv, vv)):
                o[...] = val

    two_d = lambda a: a.reshape(a.shape[-2:])
    ws, ms, vs = ([two_d(d[n]) for n in names] for d in (W, M, V))
    res = pl.pallas_call(
        body, name="adamw_small",
        out_shape=[jax.ShapeDtypeStruct(w.shape, F32) for w in ws for _ in range(4)],
        compiler_params=_cparams(),
    )(parts, *ws, *ms, *vs)
    return {n: tuple(res[4 * i:4 * i + 4]) for i, n in enumerate(names)}


def _pack_small(lru_cw, dw_w, meta):
    buf = jnp.zeros((_SM_ROWS, 256), F32)
    buf = buf.at[0:LW, 0:128].set(lru_cw)
    buf = buf.at[8:8 + dw_w.shape[0], 0:128].set(dw_w)
    return buf.at[40:56, :].set(meta)


def _block_diag4(w):
    w4 = w.reshape(NCB, 4, 64, 64)
    eye = jnp.eye(4, dtype=w.dtype)
    return jnp.einsum("ghij,hk->ghikj", w4, eye).reshape(NCB, CB, CB)


def _diag_blocks(g):
    g5 = g.reshape(NCB, 4, 64, 4, 64)
    return jnp.stack([g5[:, hh, :, hh, :] for hh in range(4)], axis=1).reshape(16, 64, 64)


def _gate_mats(W):
    return _block_diag4(W["w_gate_a"][0]).astype(BF16), _block_diag4(W["w_gate_x"][0]).astype(BF16)


def _local_step(x, target, meta_full, inproj, out_weights, lru_cw_full, dw_w_full, W, gate_mats, send):
    wa_g, wx_g = gate_mats

    h, hn = _prenorm(x, meta_full, W["pre_norm_w"])
    z, win_full = inproj(hn)
    ylru, xc, hs = _lru_fwd(z, lru_cw_full, W["lru_conv_b"], wa_g, W["b_gate_a"], wx_g, W["b_gate_x"],
                            W["lru_lambda"])
    vc = _conf_fwd_conv(z, dw_w_full, W["conf_dw_b"])
    wout_full, pw_full = out_weights(vc)
    yconf, p = _conf_fwd_proj(vc, z, W["conf_ln_w"], W["conf_ln_b"], pw_full, W["conf_pw_b"])
    dout, dy, loss_acc, dpostw_acc = _outproj_loss(ylru, yconf, wout_full, h, target, W["post_norm_w"])

    dycat, dwout_part = _outproj_bwd(dy, ylru, yconf, wout_full)
    tok = send("w_out", dwout_part)
    dvc, dgc, dpw_part, cvecs = _conf_bwd_proj(dycat, p, z, vc, W["conf_ln_w"] + tok, W["conf_ln_b"], pw_full)
    tok = send("conf_pw_w", dpw_part)
    tok = tok + send("w_in_c", _inproj_wgrad("inproj_wgrad_c", hn, dgc[None]))
    dzc, ddw_acc, kvecs = _conf_bwd_conv(dvc, z, dw_w_full + tok)
    tok = send("w_in_b", _inproj_wgrad("inproj_wgrad_b", hn, dzc))
    dzl, dwa_g, dwx_g, dcw_acc, lvecs = _lru_bwd(dycat, z, xc, hs, lru_cw_full, wa_g, W["b_gate_a"] + tok, wx_g,
                                                 W["b_gate_x"], W["lru_lambda"])
    tok = send("w_in_a", _inproj_wgrad("inproj_wgrad_a", hn, dzl))
    tok = tok + send("w_gates", _diag_blocks(dwa_g).reshape(16 * 64, 64), _diag_blocks(dwx_g).reshape(16 * 64, 64))
    grad_x, dmeta, dprew_acc = _inproj_bwd(dzl, dzc, dgc, win_full, h, dout, W["pre_norm_w"] + tok)

    vec_pack, small_part = _pack_grads(dprew_acc, dpostw_acc, cvecs, kvecs, lvecs, dcw_acc, ddw_acc, dmeta, loss_acc)
    return grad_x, vec_pack, small_part


def kernel(x, meta_tokens, pre_norm_w, post_norm_w, w_in, b_in, lru_conv_w, lru_conv_b, w_gate_a, b_gate_a, w_gate_x, b_gate_x, lru_lambda, conf_dw_w, conf_dw_b, conf_ln_w, conf_ln_b, conf_pw_w, conf_pw_b, w_out, loss_target, m_meta_tokens, m_pre_norm_w, m_post_norm_w, m_w_in, m_b_in, m_lru_conv_w, m_lru_conv_b, m_w_gate_a, m_b_gate_a, m_w_gate_x, m_b_gate_x, m_lru_lambda, m_conf_dw_w, m_conf_dw_b, m_conf_ln_w, m_conf_ln_b, m_conf_pw_w, m_conf_pw_b, m_w_out, v_meta_tokens, v_pre_norm_w, v_post_norm_w, v_w_in, v_b_in, v_lru_conv_w, v_lru_conv_b, v_w_gate_a, v_b_gate_a, v_w_gate_x, v_b_gate_x, v_lru_lambda, v_conf_dw_w, v_conf_dw_b, v_conf_ln_w, v_conf_ln_b, v_conf_pw_w, v_conf_pw_b, v_w_out):
    W = dict(meta_tokens=meta_tokens, pre_norm_w=pre_norm_w, post_norm_w=post_norm_w, w_in=w_in, b_in=b_in,
             lru_conv_w=lru_conv_w, lru_conv_b=lru_conv_b, w_gate_a=w_gate_a, b_gate_a=b_gate_a,
             w_gate_x=w_gate_x, b_gate_x=b_gate_x, lru_lambda=lru_lambda, conf_dw_w=conf_dw_w,
             conf_dw_b=conf_dw_b, conf_ln_w=conf_ln_w, conf_ln_b=conf_ln_b, conf_pw_w=conf_pw_w,
             conf_pw_b=conf_pw_b, w_out=w_out)
    M = dict(meta_tokens=m_meta_tokens, pre_norm_w=m_pre_norm_w, post_norm_w=m_post_norm_w, w_in=m_w_in,
             b_in=m_b_in, lru_conv_w=m_lru_conv_w, lru_conv_b=m_lru_conv_b, w_gate_a=m_w_gate_a,
             b_gate_a=m_b_gate_a, w_gate_x=m_w_gate_x, b_gate_x=m_b_gate_x, lru_lambda=m_lru_lambda,
             conf_dw_w=m_conf_dw_w, conf_dw_b=m_conf_dw_b, conf_ln_w=m_conf_ln_w, conf_ln_b=m_conf_ln_b,
             conf_pw_w=m_conf_pw_w, conf_pw_b=m_conf_pw_b, w_out=m_w_out)
    V = dict(meta_tokens=v_meta_tokens, pre_norm_w=v_pre_norm_w, post_norm_w=v_post_norm_w, w_in=v_w_in,
             b_in=v_b_in, lru_conv_w=v_lru_conv_w, lru_conv_b=v_lru_conv_b, w_gate_a=v_w_gate_a,
             b_gate_a=v_b_gate_a, w_gate_x=v_w_gate_x, b_gate_x=v_b_gate_x, lru_lambda=v_lru_lambda,
             conf_dw_w=v_conf_dw_w, conf_dw_b=v_conf_dw_b, conf_ln_w=v_conf_ln_w, conf_ln_b=v_conf_ln_b,
             conf_pw_w=v_conf_pw_w, conf_pw_b=v_conf_pw_b, w_out=v_w_out)
    names = list(W.keys())
    shapes = {n: W[n].shape for n in names}

    small = _pack_small(lru_conv_w[0], conf_dw_w[0], meta_tokens)
    (small_flight,), tok = _exchange_start("gather_small_start", [
        (small, jax.ShapeDtypeStruct((NDEV, _SM_ROWS, 256), F32), _whole, _slot)])
    win_flight, tok = _win_gather_start(w_in[0].astype(BF16) + tok[0, 0].astype(BF16))
    gate_mats = _gate_mats(W)
    wout_shard = w_out[0].astype(BF16) + tok[0, 0].astype(BF16)
    pw_shard = conf_pw_w[0].astype(BF16)
    cast_done = (gate_mats[0][0, 0:8, 0:128] + gate_mats[1][0, 0:8, 0:128]
                 + wout_shard[0:8, 0:128] + pw_shard[0:8, 0:128])
    win_flight, tok = _win_gather_links(win_flight, cast_done)
    gathered, tok = _exchange_start("gather_out_start", [
        (wout_shard + tok[0, 0].astype(BF16), jax.ShapeDtypeStruct((D, D), BF16), _whole, _rows(D // NDEV)),
        (pw_shard, jax.ShapeDtypeStruct((DC, DC), BF16), _whole, _rows(DC // NDEV)),
    ])
    (small_all,) = _exchange_wait("gather_small_wait", [small_flight], tok)
    unshard = lambda a: jnp.transpose(a, (1, 0, 2)).reshape(a.shape[1], -1)
    lru_cw_full = unshard(small_all[:, 0:LW, 0:128])
    dw_w_full = unshard(small_all[:, 8:8 + KWP, 0:128])
    meta_full = unshard(small_all[:, 40:56, :])

    def out_weights(after):
        return _exchange_wait("gather_out_wait", gathered, after)

    def inproj(hn):
        xi, yi, ci = lax.axis_index("x"), lax.axis_index("y"), lax.axis_index("c")
        shard = lambda px, py, pc: (4 * px + 2 * py + pc).astype(jnp.int32)
        here = jnp.stack([shard(xi, yi, ci), shard(xi, yi, 1 - ci)])
        over_links = jnp.stack([shard(1 - xi, yi, ci), shard(xi, 1 - yi, ci), shard(1 - xi, 1 - yi, ci)])
        flight = _win_gather_early(win_flight)
        z, land = _inproj_cols("inproj_here", here, hn, flight["land"], b_in, None)
        flight = _win_gather_forward(dict(flight, land=land), z)
        z, land = _inproj_cols("inproj_links", over_links, hn, flight["land"], b_in, z)
        land = _win_gather_wait(dict(flight, land=land))
        return _inproj_cols("inproj_sibling", over_links + 1 - 2 * ci, hn, land, b_in, z)

    row_stage = lambda ncol: (jax.ShapeDtypeStruct((NDEV, D // NDEV, ncol), BF16), _rows(D // NDEV))
    piece = {"w_in_a": row_stage(2048), "w_in_b": row_stage(2048), "w_in_c": row_stage(1024),
             "w_out": row_stage(D),
             "conf_pw_w": (jax.ShapeDtypeStruct((NDEV, DC // NDEV, DC), BF16), _rows(DC // NDEV)),
             "w_gates": (jax.ShapeDtypeStruct((NDEV, 16 * 64, 64), BF16), _whole)}
    sent = {}

    def send(name, *parts):
        handles, token = _exchange_start(
            "scatter_" + name + "_start",
            [(part.astype(BF16), piece[name][0], piece[name][1], _slot) for part in parts])
        sent[name] = handles
        return token[0, 0]

    grad_x, vec_pack, small_part = _local_step(
        x[0], loss_target[0], meta_full, inproj, out_weights, lru_cw_full, dw_w_full, W, gate_mats, send)
    grad_x = grad_x[None]

    rest, tok = _exchange_start("scatter_rest_start", [
        (small_part, jax.ShapeDtypeStruct((NDEV, _SM_ROWS, 256), F32), _slot, _slot),
        (vec_pack, jax.ShapeDtypeStruct((NDEV, _VEC_ROWS, 1024), F32), _whole, _slot),
    ])
    (parts_c,) = _exchange_wait("scatter_w_in_c_wait", sent["w_in_c"], tok)
    (parts_b,) = _exchange_wait("scatter_w_in_b_wait", sent["w_in_b"], parts_c)
    (parts_a,) = _exchange_wait("scatter_w_in_a_wait", sent["w_in_a"], parts_b)
    win_rows = _sum_win_parts(parts_a, parts_b, parts_c)
    win_stage2, tok = _exchange_start("scatter_w_in_stage2_start", [
        (win_rows, jax.ShapeDtypeStruct((NDEV, D // NDEV, NIN // NDEV), BF16), _cols(NIN // NDEV), _slot)])

    G, DW, NM, NV = {}, {}, {}, {}
    (wout_parts,) = _exchange_wait("scatter_w_out_wait", sent["w_out"], tok)
    G["w_out"], DW["w_out"], NM["w_out"], NV["w_out"] = _adamw("adamw_w_out", wout_parts, w_out[0], m_w_out[0], v_w_out[0], 64)
    (pw_parts,) = _exchange_wait("scatter_conf_pw_w_wait", sent["conf_pw_w"], G["w_out"])
    G["conf_pw_w"], DW["conf_pw_w"], NM["conf_pw_w"], NV["conf_pw_w"] = _adamw(
        "adamw_pw", pw_parts, conf_pw_w[0], m_conf_pw_w[0], v_conf_pw_w[0], 128)
    res = {}
    wa_parts, wx_parts = _exchange_wait("scatter_w_gates_wait", sent["w_gates"], G["conf_pw_w"])
    for n, parts in (("w_gate_a", wa_parts), ("w_gate_x", wx_parts)):
        res[n] = _adamw("adamw_" + n, parts, *[d[n].reshape(16 * 64, 64) for d in (W, M, V)], 16 * 64)
    small_parts, vec_parts = _exchange_wait("scatter_rest_wait", rest, res["w_gate_x"][0])
    res.update(_adamw_small(small_parts, W, M, V))
    vec_res, loss_row = _adamw_vec(vec_parts, W, M, V)
    res.update(vec_res)
    (win_sum,) = _exchange_wait("scatter_w_in_stage2_wait", win_stage2, loss_row)
    res["w_in"] = _adamw("adamw_w_in", win_sum.reshape(1, D, NIN // NDEV), w_in[0], m_w_in[0], v_w_in[0], 256)
    for n, vals in res.items():
        for dst, val in zip((G, DW, NM, NV), vals):
            dst[n] = val
    for dst in (G, DW, NM, NV):
        for n in names:
            dst[n] = dst[n].reshape(shapes[n])
    loss = loss_row[0, 0]

    return (loss, grad_x, *[G[n] for n in names], *[DW[n] for n in names],
            *[NM[n] for n in names], *[NV[n] for n in names])
```

```python
import functools

import jax
import jax.numpy as jnp
from jax import lax
from jax.experimental import pallas as pl
from jax.experimental.pallas import tpu as pltpu

F32 = jnp.float32
BF16 = jnp.bfloat16

D = 2048
DL = 1024
DC = 1024
NIN = 5120
NMETA = 16
SEQ = 2048
T = NMETA + SEQ
TP = 2176
TM = 544
CB = 256
NCB = DL // CB
R = 16
KW = 31
KWP = 32
LW = 4
LRU_C = 8.0
EPS = 1e-6
NDEV = 8

ADAM_LR = 0.001
ADAM_B1 = 0.9
ADAM_B2 = 0.999
ADAM_EPS = 1e-08
ADAM_WD = 0.01
ADAM_STEP = 10

VMEM_LIMIT = 56 * 1024 * 1024


def _cparams():
    return pltpu.CompilerParams(vmem_limit_bytes=VMEM_LIMIT)


def _sig(x):
    return 1.0 / (1.0 + jnp.exp(-x))


def _expm1_neg(y):
    poly = y * (1.0 + y * (0.5 + y * (1.0 / 6.0 + y * (1.0 / 24.0 + y * (1.0 / 120.0)))))
    return jnp.where(y > -0.1, poly, jnp.exp(y) - 1.0)


def _softplus(x):
    e = jnp.exp(-jnp.abs(x))
    w = 1.0 + e
    l1p = jnp.where(w == 1.0, e, jnp.log(w) * e / (w - 1.0))
    return jnp.maximum(x, 0.0) + l1p


def _row_iota(shape):
    return lax.broadcasted_iota(jnp.int32, shape, 0)


def _fold8(v):
    return v[0:8, :] + v[8:16, :]


_FLIPS = [(k >> 2 & 1, k >> 1 & 1, k & 1) for k in range(1, NDEV)]
_HBM = pl.BlockSpec(memory_space=pltpu.HBM)
_SEM = pl.BlockSpec(memory_space=pltpu.SEMAPHORE)


def _peers():
    x, y, c = lax.axis_index("x"), lax.axis_index("y"), lax.axis_index("c")
    out = []
    for dx, dy, dc in _FLIPS:
        px = 1 - x if dx else x
        py = 1 - y if dy else y
        pc = 1 - c if dc else c
        out.append(((px, py, pc), 4 * px + 2 * py + pc))
    return 4 * x + 2 * y + c, out


def _exchange_start(name, items):
    n = len(items)

    def body(*refs):
        srcs, lands = refs[:n], refs[n:2 * n]
        outs = refs[2 * n:]
        send_sems, recv_sems, local_sems = outs[:n], outs[n:2 * n], outs[2 * n:3 * n]
        token = outs[-1]
        me, peers = _peers()
        for a in range(n):
            src_at, dst_at = items[a][2], items[a][3]
            pltpu.make_async_copy(src_at(srcs[a], me), dst_at(lands[a], me), local_sems[a]).start()
        for a in range(n):
            src_at, dst_at = items[a][2], items[a][3]
            for k, (pos, peer) in enumerate(peers):
                pltpu.make_async_remote_copy(
                    src_ref=src_at(srcs[a], peer), dst_ref=dst_at(lands[a], me),
                    send_sem=send_sems[a].at[k], recv_sem=recv_sems[a].at[k],
                    device_id=pos, device_id_type=pl.DeviceIdType.MESH).start()
        token[...] = jnp.zeros_like(token)

    srcs = [pltpu.with_memory_space_constraint(it[0], pltpu.HBM) for it in items]
    lands = [pltpu.with_memory_space_constraint(lax.empty(it[1].shape, it[1].dtype), pltpu.HBM) for it in items]
    sem7 = pltpu.SemaphoreType.DMA((NDEV - 1,))
    res = pl.pallas_call(
        body, name=name,
        out_shape=([sem7] * (2 * n) + [pltpu.SemaphoreType.DMA(())] * n
                   + [pltpu.HBM(a.shape, a.dtype) for a in srcs] + [pltpu.HBM(a.shape, a.dtype) for a in lands]
                   + [jax.ShapeDtypeStruct((8, 128), F32)]),
        in_specs=[_HBM] * (2 * n),
        out_specs=[_SEM] * (3 * n) + [_HBM] * (2 * n) + [pl.BlockSpec(memory_space=pltpu.VMEM)],
        input_output_aliases={i: 3 * n + i for i in range(2 * n)},
        compiler_params=pltpu.CompilerParams(has_side_effects=pltpu.SideEffectType.DATAFLOW_SIDE_EFFECTING),
    )(*srcs, *lands)
    handles = [dict(send=res[a], recv=res[n + a], local=res[2 * n + a], src=res[3 * n + a], land=res[4 * n + a],
                    src_at=items[a][2], dst_at=items[a][3]) for a in range(n)]
    return handles, res[-1]


def _wait_bytes(piece, sem):
    pltpu.make_async_copy(piece, piece, sem).wait()


def _exchange_wait(name, handles, after):
    n = len(handles)

    def body(*refs):
        srcs, lands = refs[:n], refs[n:2 * n]
        send_sems, recv_sems, local_sems = refs[2 * n:3 * n], refs[3 * n:4 * n], refs[4 * n:5 * n]
        me, peers = _peers()
        for a in range(n):
            src_at, dst_at = handles[a]["src_at"], handles[a]["dst_at"]
            for k, (pos, peer) in enumerate(peers):
                _wait_bytes(src_at(srcs[a], peer), send_sems[a].at[k])
                _wait_bytes(dst_at(lands[a], peer), recv_sems[a].at[k])
            pltpu.make_async_copy(src_at(srcs[a], me), dst_at(lands[a], me), local_sems[a]).wait()

    srcs = [hd["src"] for hd in handles]
    lands = [hd["land"] for hd in handles]
    res = pl.pallas_call(
        body, name=name,
        out_shape=[pltpu.HBM(a.shape, a.dtype) for a in srcs] + [pltpu.HBM(a.shape, a.dtype) for a in lands],
        in_specs=[_HBM] * (2 * n) + [_SEM] * (3 * n) + [pl.BlockSpec(memory_space=pl.ANY)],
        out_specs=[_HBM] * (2 * n),
        input_output_aliases={i: i for i in range(2 * n)},
        compiler_params=pltpu.CompilerParams(has_side_effects=pltpu.SideEffectType.DATAFLOW_SIDE_EFFECTING),
    )(*srcs, *lands, *[hd["send"] for hd in handles], *[hd["recv"] for hd in handles],
      *[hd["local"] for hd in handles], after)
    return list(res[n:])


_SIDE = pltpu.SideEffectType.DATAFLOW_SIDE_EFFECTING
_WCOLS = NIN // NDEV


def _win_cols(ref, l):
    return ref.at[:, pl.ds(pl.multiple_of(l * _WCOLS, 128), _WCOLS)]


def _win_routes():
    x, y, c = lax.axis_index("x"), lax.axis_index("y"), lax.axis_index("c")
    pos = [(x, y, 1 - c), (1 - x, y, c), (x, 1 - y, c), (1 - x, 1 - y, c)]
    return 4 * x + 2 * y + c, [(p, 4 * p[0] + 2 * p[1] + p[2]) for p in pos]


def _win_gather_start(shard):
    def body(src, land, send_sem, recv_sem, local_sem, src_thru, land_thru, token):
        me, routes = _win_routes()
        pltpu.make_async_copy(src, _win_cols(land, me), local_sem).start()
        pltpu.make_async_remote_copy(src_ref=src, dst_ref=_win_cols(land, me), send_sem=send_sem, recv_sem=recv_sem,
                                     device_id=routes[0][0], device_id_type=pl.DeviceIdType.MESH).start()
        token[...] = jnp.zeros_like(token)

    src = pltpu.with_memory_space_constraint(shard, pltpu.HBM)
    land = pltpu.with_memory_space_constraint(lax.empty((D, NIN), BF16), pltpu.HBM)
    sem = pltpu.SemaphoreType.DMA(())
    res = pl.pallas_call(
        body, name="win_gather_start",
        out_shape=[sem, sem, sem, pltpu.HBM(src.shape, BF16), pltpu.HBM(land.shape, BF16),
                   jax.ShapeDtypeStruct((8, 128), F32)],
        in_specs=[_HBM, _HBM],
        out_specs=[_SEM, _SEM, _SEM, _HBM, _HBM, pl.BlockSpec(memory_space=pltpu.VMEM)],
        input_output_aliases={0: 3, 1: 4},
        compiler_params=pltpu.CompilerParams(has_side_effects=_SIDE),
    )(src, land)
    return dict(send0=res[0], recv0=res[1], local=res[2], src=res[3], land=res[4]), res[5]


def _win_gather_links(hd, after):
    def body(src, land, after_ref, send_sems, recv_sems, src_thru, land_thru, token):
        me, routes = _win_routes()
        for k in (1, 2, 3):
            pltpu.make_async_remote_copy(src_ref=src, dst_ref=_win_cols(land, me), send_sem=send_sems.at[k - 1],
                                         recv_sem=recv_sems.at[k - 1], device_id=routes[k][0],
                                         device_id_type=pl.DeviceIdType.MESH).start()
        token[...] = jnp.zeros_like(token)

    sem3 = pltpu.SemaphoreType.DMA((3,))
    res = pl.pallas_call(
        body, name="win_gather_links",
        out_shape=[sem3, sem3, pltpu.HBM(hd["src"].shape, BF16), pltpu.HBM(hd["land"].shape, BF16),
                   jax.ShapeDtypeStruct((8, 128), F32)],
        in_specs=[_HBM, _HBM, pl.BlockSpec(memory_space=pl.ANY)],
        out_specs=[_SEM, _SEM, _HBM, _HBM, pl.BlockSpec(memory_space=pltpu.VMEM)],
        input_output_aliases={0: 2, 1: 3},
        compiler_params=pltpu.CompilerParams(has_side_effects=_SIDE),
    )(hd["src"], hd["land"], after)
    return dict(hd, send=res[0], recv=res[1], src=res[2], land=res[3]), res[4]


def _win_gather_forward(hd, after):
    def body(land, recv_sems, after_ref, land_thru, fsend_sems, frecv_sems):
        me, routes = _win_routes()
        sibling = routes[0][0]
        for k in (1, 2, 3):
            pos, peer = routes[k]
            piece = _win_cols(land, peer)
            pltpu.make_async_remote_copy(src_ref=piece, dst_ref=piece, send_sem=fsend_sems.at[k - 1],
                                         recv_sem=recv_sems.at[k - 1], device_id=pos,
                                         device_id_type=pl.DeviceIdType.MESH).wait_recv()
            pltpu.make_async_remote_copy(src_ref=piece, dst_ref=piece, send_sem=fsend_sems.at[k - 1],
                                         recv_sem=frecv_sems.at[k - 1], device_id=sibling,
                                         device_id_type=pl.DeviceIdType.MESH).start()

    sem3 = pltpu.SemaphoreType.DMA((3,))
    res = pl.pallas_call(
        body, name="win_gather_forward",
        out_shape=[pltpu.HBM(hd["land"].shape, BF16), sem3, sem3],
        in_specs=[_HBM, _SEM, pl.BlockSpec(memory_space=pl.ANY)],
        out_specs=[_HBM, _SEM, _SEM],
        input_output_aliases={0: 0},
        compiler_params=pltpu.CompilerParams(has_side_effects=_SIDE),
    )(hd["land"], hd["recv"], after)
    return dict(hd, land=res[0], fsend=res[1], frecv=res[2])


def _win_gather_early(hd):
    def body(src, land, recv_sem, local_sem, src_thru, land_thru):
        me, routes = _win_routes()
        _wait_bytes(_win_cols(land, routes[0][1]), recv_sem)
        pltpu.make_async_copy(src, _win_cols(land, me), local_sem).wait()

    res = pl.pallas_call(
        body, name="win_gather_early",
        out_shape=[pltpu.HBM(hd["src"].shape, BF16), pltpu.HBM(hd["land"].shape, BF16)],
        in_specs=[_HBM, _HBM, _SEM, _SEM],
        out_specs=[_HBM, _HBM],
        input_output_aliases={0: 0, 1: 1},
        compiler_params=pltpu.CompilerParams(has_side_effects=_SIDE),
    )(hd["src"], hd["land"], hd["recv0"], hd["local"])
    return dict(hd, src=res[0], land=res[1])


def _win_gather_wait(hd):
    def body(src, land, send0_sem, send_sems, fsend_sems, frecv_sems, src_thru, land_thru):
        me, routes = _win_routes()
        sib_pos, sibling = routes[0]
        for k in range(4):
            _wait_bytes(src, send0_sem if k == 0 else send_sems.at[k - 1])
        for k in (1, 2, 3):
            _wait_bytes(_win_cols(land, routes[k][1]), fsend_sems.at[k - 1])
            _wait_bytes(_win_cols(land, 4 * routes[k][0][0] + 2 * routes[k][0][1] + sib_pos[2]), frecv_sems.at[k - 1])

    res = pl.pallas_call(
        body, name="win_gather_wait",
        out_shape=[pltpu.HBM(hd["src"].shape, BF16), pltpu.HBM(hd["land"].shape, BF16)],
        in_specs=[_HBM, _HBM] + [_SEM] * 4,
        out_specs=[_HBM, _HBM],
        input_output_aliases={0: 0, 1: 1},
        compiler_params=pltpu.CompilerParams(has_side_effects=_SIDE),
    )(hd["src"], hd["land"], hd["send0"], hd["send"], hd["fsend"], hd["frecv"])
    return res[1]


def _whole(ref, l):
    return ref


def _slot(ref, l):
    return ref.at[l]


def _cols(width):
    def at(ref, l):
        return ref.at[:, pl.ds(pl.multiple_of(l * width, 128), width)]
    return at


def _rows(height):
    def at(ref, l):
        return ref.at[pl.ds(pl.multiple_of(l * height, 8), height), :]
    return at


NTILE = TP // TM


def _tile_rows(t):
    lo = max(t * TM - NMETA, 0)
    hi = min((t + 1) * TM - NMETA, SEQ)
    return lo, hi - lo, lo + NMETA - t * TM


def _for_tile(t, fn):
    for static_t in range(NTILE):
        pl.when(t == static_t)(functools.partial(fn, static_t))


def _token_tile_copy(hbm_ref, buf, sem, t):
    lo, n, off = _tile_rows(t)
    return pltpu.make_async_copy(hbm_ref.at[pl.ds(lo, n)], buf.at[pl.ds(off, n)], sem)


def _prenorm(x, meta_full, pre_w):
    def body(x_ref, meta_ref, pw_ref, h_ref, hn_ref, xbuf, sems):
        i = pl.program_id(0)
        slot = i % 2

        def start(t):
            _token_tile_copy(x_ref, xbuf.at[t % 2], sems.at[t % 2], t).start()

        @pl.when(i == 0)
        def _():
            start(0)
        _for_tile(i + 1, start)
        _for_tile(i, lambda t: _token_tile_copy(x_ref, xbuf.at[t % 2], sems.at[t % 2], t).wait())

        @pl.when(i == 0)
        def _():
            xbuf[0, 0:NMETA, :] = meta_ref[...]

        @pl.when(i == NTILE - 1)
        def _():
            last = _tile_rows(NTILE - 1)[1]
            xbuf[(NTILE - 1) % 2, last:TM, :] = jnp.zeros((TM - last, D), F32)

        pw = pw_ref[...]

        def chunk(ci, carry):
            r0 = pl.multiple_of(ci * R, R)
            xv = xbuf[slot, pl.ds(r0, R), :]
            h_ref[pl.ds(r0, R), :] = xv
            ms = jnp.mean(xv * xv, axis=-1, keepdims=True)
            hn_ref[pl.ds(r0, R), :] = (xv * lax.rsqrt(ms + EPS) * pw).astype(BF16)
            return carry
        lax.fori_loop(0, TM // R, chunk, 0, unroll=2)

    row = pl.BlockSpec((TM, D), lambda i: (i, 0))
    return pl.pallas_call(
        body, name="prenorm",
        grid=(NTILE,),
        in_specs=[pl.BlockSpec(memory_space=pl.ANY), pl.BlockSpec((NMETA, D), lambda i: (0, 0)),
                  pl.BlockSpec((1, D), lambda i: (0, 0))],
        out_specs=[row, row],
        out_shape=[jax.ShapeDtypeStruct((TP, D), F32), jax.ShapeDtypeStruct((TP, D), BF16)],
        scratch_shapes=[pltpu.VMEM((2, TM, D), F32), pltpu.SemaphoreType.DMA((2,))],
        compiler_params=_cparams(),
    )(x, meta_full, pre_w)


def _inproj_cols(name, shards, hn, w_land, b_in, z_prev):
    nsh = shards.shape[0]

    def body(idx_ref, hn_ref, w_ref, b_ref, *rest):
        z_ref = rest[-2]
        z_ref[...] = jnp.dot(hn_ref[...], w_ref[...], preferred_element_type=F32) + b_ref[...]

    any_spec = pl.BlockSpec(memory_space=pl.ANY)
    in_specs = [pl.BlockSpec((TM, D), lambda j, i, idx: (i, 0)),
                pl.BlockSpec((D, _WCOLS), lambda j, i, idx: (0, idx[j])),
                pl.BlockSpec((1, _WCOLS), lambda j, i, idx: (0, idx[j]))]
    operands = [hn, w_land, b_in]
    aliases = {2: 1}
    if z_prev is not None:
        in_specs.append(any_spec)
        operands.append(z_prev)
        aliases[4] = 0
    return pl.pallas_call(
        body, name=name,
        grid_spec=pltpu.PrefetchScalarGridSpec(
            num_scalar_prefetch=1, grid=(nsh, TP // TM), in_specs=in_specs,
            out_specs=[pl.BlockSpec((TM, _WCOLS), lambda j, i, idx: (i, idx[j])), any_spec]),
        out_shape=[jax.ShapeDtypeStruct((TP, NIN), F32), jax.ShapeDtypeStruct(w_land.shape, w_land.dtype)],
        input_output_aliases=aliases,
        compiler_params=_cparams(),
    )(shards, *operands)


def _gate_values(ga, gx, xc, sp8):
    r = _sig(ga)
    i = _sig(gx)
    log_a = -(r * sp8)
    a = jnp.exp(log_a)
    mult = jnp.sqrt(-_expm1_neg(2.0 * log_a))
    return r, i, a, mult


def _lru_fwd(z, conv_w, conv_b, wa_g, b_a, wx_g, b_x, lam):
    def body(x_ref, g_ref, cw_ref, cb_ref, wa_ref, ba_ref, wx_ref, bx_ref, lam_ref,
             y_ref, xc_ref, hs_ref, ga_s, gx_s):
        taps = [cw_ref[k:k + 1, :] for k in range(LW)]
        cb = cb_ref[...]

        def conv_chunk(ci, carry):
            r0 = pl.multiple_of(ci * R, R)
            cur = x_ref[pl.ds(r0, R), :]
            p0 = pl.multiple_of(jnp.maximum(r0 - 8, 0), 8)
            prev = jnp.where(ci > 0, x_ref[pl.ds(p0, 8), :], 0.0)
            buf = jnp.concatenate([prev, cur], axis=0)
            acc = cur * taps[LW - 1] + cb
            for s in range(1, LW):
                acc = acc + pltpu.roll(buf, s, 0)[8:8 + R, :] * taps[LW - 1 - s]
            xc_ref[pl.ds(r0, R), :] = acc
            return carry
        lax.fori_loop(0, TP // R, conv_chunk, 0)

        def gate_chunk(ci, carry):
            r0 = pl.multiple_of(ci * TM, TM)
            xb = xc_ref[pl.ds(r0, TM), :].astype(BF16)
            ga_s[pl.ds(r0, TM), :] = jnp.dot(xb, wa_ref[...], preferred_element_type=F32) + ba_ref[...]
            gx_s[pl.ds(r0, TM), :] = jnp.dot(xb, wx_ref[...], preferred_element_type=F32) + bx_ref[...]
            return carry
        lax.fori_loop(0, TP // TM, gate_chunk, 0)

        sp8 = LRU_C * _softplus(-lam_ref[...])
        row = _row_iota((R, CB))

        def scan_chunk(ci, hprev):
            r0 = pl.multiple_of(ci * R, R)
            xc = xc_ref[pl.ds(r0, R), :]
            _, i, a, mult = _gate_values(ga_s[pl.ds(r0, R), :], gx_s[pl.ds(r0, R), :], xc, sp8)
            u = mult * (i * xc)
            k = 1
            while k < R:
                m = row >= k
                u = jnp.where(m, a * pltpu.roll(u, k, 0) + u, u)
                a = jnp.where(m, a * pltpu.roll(a, k, 0), a)
                k *= 2
            hv = u + a * hprev
            hs_ref[pl.ds(r0, R), :] = hv
            g = g_ref[pl.ds(r0, R), :]
            y_ref[pl.ds(r0, R), :] = (hv * (g * _sig(g))).astype(BF16)
            return jnp.sum(jnp.where(row == R - 1, hv, 0.0), axis=0, keepdims=True)
        lax.fori_loop(0, TP // R, scan_chunk, jnp.zeros((1, CB), F32))

    col = lambda off: pl.BlockSpec((TP, CB), lambda j: (0, off + j))
    vec = pl.BlockSpec((1, CB), lambda j: (0, j))
    wsp = pl.BlockSpec((None, CB, CB), lambda j: (j, 0, 0))
    return pl.pallas_call(
        body, name="lru_fwd",
        grid=(NCB,),
        in_specs=[col(0), col(NCB), pl.BlockSpec((LW, CB), lambda j: (0, j)), vec, wsp, vec, wsp, vec, vec],
        out_specs=[col(0), col(0), col(0)],
        out_shape=[jax.ShapeDtypeStruct((TP, DL), BF16), jax.ShapeDtypeStruct((TP, DL), F32),
                   jax.ShapeDtypeStruct((TP, DL), F32)],
        scratch_shapes=[pltpu.VMEM((TP, CB), F32), pltpu.VMEM((TP, CB), F32)],
        compiler_params=_cparams(),
    )(z, z, conv_w, conv_b, wa_g, b_a, wx_g, b_x, lam)


CBC = 128
NCBC = DC // CBC
RC = 64


def _fold_rows(v):
    acc = v[0:8, :]
    for r in range(8, v.shape[0], 8):
        acc = acc + v[r:r + 8, :]
    return acc


def _conf_fwd_conv(z, dw_w, dw_b):
    def body(u1_ref, u2_ref, w_ref, b_ref, vc_ref, vs):
        vs[pl.ds(0, KWP), :] = jnp.zeros((KWP, CBC), F32)

        def glu_chunk(ci, carry):
            r0 = pl.multiple_of(ci * RC, RC)
            vs[pl.ds(KWP + r0, RC), :] = u1_ref[pl.ds(r0, RC), :] * _sig(u2_ref[pl.ds(r0, RC), :])
            return carry
        lax.fori_loop(0, TP // RC, glu_chunk, 0)

        bias = b_ref[...]

        def conv_chunk(ci, carry):
            r0 = pl.multiple_of(ci * RC, RC)
            buf = vs[pl.ds(r0, KWP + RC), :]
            acc = jnp.zeros((RC, CBC), F32) + bias
            for rr in range(8):
                rolled = buf if rr == 0 else pltpu.roll(buf, rr, 0)
                for q in range(4):
                    s = 8 * q + rr
                    if s > KW - 1:
                        continue
                    k = KW - 1 - s
                    acc = acc + rolled[KWP - 8 * q:KWP - 8 * q + RC, :] * w_ref[k:k + 1, :]
            vc_ref[pl.ds(r0, RC), :] = acc
            return carry
        lax.fori_loop(0, TP // RC, conv_chunk, 0)

    return pl.pallas_call(
        body, name="conf_fwd_conv",
        grid=(NCBC,),
        in_specs=[pl.BlockSpec((TP, CBC), lambda j: (0, 2 * NCBC + j)),
                  pl.BlockSpec((TP, CBC), lambda j: (0, 3 * NCBC + j)),
                  pl.BlockSpec((KWP, CBC), lambda j: (0, j)),
                  pl.BlockSpec((1, CBC), lambda j: (0, j))],
        out_specs=pl.BlockSpec((TP, CBC), lambda j: (0, j)),
        out_shape=jax.ShapeDtypeStruct((TP, DC), F32),
        scratch_shapes=[pltpu.VMEM((TP + KWP, CBC), F32)],
        compiler_params=_cparams(),
    )(z, z, dw_w, dw_b)


def _ln_chunk(vc, lw, lb):
    mu = jnp.mean(vc, axis=-1, keepdims=True)
    xm = vc - mu
    var = jnp.mean(xm * xm, axis=-1, keepdims=True)
    rstd = lax.rsqrt(var + EPS)
    xhat = xm * rstd
    return xhat, rstd, xhat * lw + lb


def _conf_fwd_proj(vc, z, ln_w, ln_b, pw_w, pw_b):
    def body(vc_ref, g_ref, lw_ref, lb_ref, w_ref, b_ref, y_ref, p_ref, s_s):
        lw, lb = lw_ref[...], lb_ref[...]

        def ln_chunk(ci, carry):
            r0 = pl.multiple_of(ci * R, R)
            for half in range(2):
                rr = r0 + 8 * half
                _, _, ln = _ln_chunk(vc_ref[pl.ds(rr, 8), :], lw, lb)
                p_ref[pl.ds(rr, 8), :] = ln * _sig(ln)
            s_s[pl.ds(r0, R), :] = p_ref[pl.ds(r0, R), :].astype(BF16)
            return carry
        lax.fori_loop(0, TM // R, ln_chunk, 0, unroll=2)

        p_ref[...] = jnp.dot(s_s[...], w_ref[...], preferred_element_type=F32) + b_ref[...]

        def out_chunk(ci, carry):
            r0 = pl.multiple_of(ci * R, R)
            g = g_ref[pl.ds(r0, R), :]
            y_ref[pl.ds(r0, R), :] = (p_ref[pl.ds(r0, R), :] * (g * _sig(g))).astype(BF16)
            return carry
        lax.fori_loop(0, TM // R, out_chunk, 0)

    row = pl.BlockSpec((TM, DC), lambda i: (i, 0))
    vec = pl.BlockSpec((1, DC), lambda i: (0, 0))
    return pl.pallas_call(
        body, name="conf_fwd_proj",
        grid=(TP // TM,),
        in_specs=[row, pl.BlockSpec((TM, DC), lambda i: (i, 4)), vec, vec,
                  pl.BlockSpec((DC, DC), lambda i: (0, 0)), vec],
        out_specs=[row, row],
        out_shape=[jax.ShapeDtypeStruct((TP, DC), BF16), jax.ShapeDtypeStruct((TP, DC), F32)],
        scratch_shapes=[pltpu.VMEM((TM, DC), BF16)],
        compiler_params=_cparams(),
    )(vc, z, ln_w, ln_b, pw_w, pw_b)


def _outproj_loss(ylru, yconf, w_out, h, target, post_w):
    def body(yl_ref, yc_ref, w_ref, h_ref, tgt_hbm, pw_ref, dout_ref, dy_ref, loss_ref, dpw_ref, y_s, t_ref, sem):
        i = pl.program_id(0)
        k = pl.program_id(1)

        @pl.when(k == 0)
        def _():
            _for_tile(i, lambda t: _token_tile_copy(tgt_hbm, t_ref, sem, t).start())
            y_s[...] = jnp.dot(yl_ref[...], w_ref[...], preferred_element_type=F32)

        @pl.when(k == 1)
        def _():
            y_s[...] += jnp.dot(yc_ref[...], w_ref[...], preferred_element_type=F32)

        @pl.when(jnp.logical_and(i == 0, k == 1))
        def _():
            loss_ref[...] = jnp.zeros_like(loss_ref)
            dpw_ref[...] = jnp.zeros_like(dpw_ref)

        @pl.when(k == 1)
        def _():
            _for_tile(i, lambda t: _token_tile_copy(tgt_hbm, t_ref, sem, t).wait())

            @pl.when(i == 0)
            def _():
                t_ref[0:NMETA, :] = jnp.zeros((NMETA, D), F32)

            @pl.when(i == NTILE - 1)
            def _():
                last = _tile_rows(NTILE - 1)[1]
                t_ref[last:TM, :] = jnp.zeros((TM - last, D), F32)

            pw = pw_ref[...]
            row = _row_iota((8, D))

            def chunk(ci, carry):
                r0 = pl.multiple_of(ci * 8, 8)
                yv = y_s[pl.ds(r0, 8), :]
                rs = lax.rsqrt(jnp.mean(yv * yv, axis=-1, keepdims=True) + EPS)
                grow = row + (i * TM + r0)
                valid = jnp.logical_and(grow >= NMETA, grow < T)
                yn = yv * rs
                err = jnp.where(valid, h_ref[pl.ds(r0, 8), :] + yn * pw - t_ref[pl.ds(r0, 8), :], 0.0)
                loss_ref[...] += err * err
                d_rn = err * (1.0 / D)
                dout_ref[pl.ds(r0, 8), :] = d_rn
                dpw_ref[...] += d_rn * yn
                gw = d_rn * pw
                dot = jnp.mean(gw * yv, axis=-1, keepdims=True)
                dy_ref[pl.ds(r0, 8), :] = (rs * gw - yv * (rs * rs * rs * dot)).astype(BF16)
                return carry
            lax.fori_loop(0, TM // 8, chunk, 0, unroll=4)

    row = pl.BlockSpec((TM, D), lambda i, k: (i, 0))
    half = pl.BlockSpec((TM, DL), lambda i, k: (i, 0))
    acc = pl.BlockSpec((8, D), lambda i, k: (0, 0))
    return pl.pallas_call(
        body, name="outproj_loss",
        grid=(TP // TM, 2),
        in_specs=[half, half, pl.BlockSpec((DL, D), lambda i, k: (k, 0)), row, pl.BlockSpec(memory_space=pl.ANY),
                  pl.BlockSpec((1, D), lambda i, k: (0, 0))],
        out_specs=[row, row, acc, acc],
        out_shape=[jax.ShapeDtypeStruct((TP, D), F32), jax.ShapeDtypeStruct((TP, D), BF16),
                   jax.ShapeDtypeStruct((8, D), F32), jax.ShapeDtypeStruct((8, D), F32)],
        scratch_shapes=[pltpu.VMEM((TM, D), F32), pltpu.VMEM((TM, D), F32), pltpu.SemaphoreType.DMA(())],
        compiler_params=_cparams(),
    )(ylru, yconf, w_out, h, target, post_w)


_NT = (((1,), (1,)), ((), ()))
_TN = (((0,), (0,)), ((), ()))


def _outproj_bwd(dy, ylru, yconf, w_out):
    def body(dy_ref, yl_ref, yc_ref, w_ref, dycat_ref, dw_ref):
        j = pl.program_id(0)
        dyv = dy_ref[...]
        dycat_ref[...] = lax.dot_general(dyv, w_ref[...], _NT, preferred_element_type=F32)

        @pl.when(j < NCB)
        def _():
            dw_ref[...] = lax.dot_general(yl_ref[...], dyv, _TN, preferred_element_type=F32).astype(BF16)

        @pl.when(j >= NCB)
        def _():
            dw_ref[...] = lax.dot_general(yc_ref[...], dyv, _TN, preferred_element_type=F32).astype(BF16)

    return pl.pallas_call(
        body, name="outproj_bwd",
        grid=(2 * NCB,),
        in_specs=[pl.BlockSpec((TP, D), lambda j: (0, 0)),
                  pl.BlockSpec((TP, CB), lambda j: (0, jnp.minimum(j, NCB - 1))),
                  pl.BlockSpec((TP, CB), lambda j: (0, jnp.maximum(j - NCB, 0))),
                  pl.BlockSpec((CB, D), lambda j: (j, 0))],
        out_specs=[pl.BlockSpec((TP, CB), lambda j: (0, j)), pl.BlockSpec((CB, D), lambda j: (j, 0))],
        out_shape=[jax.ShapeDtypeStruct((TP, D), F32), jax.ShapeDtypeStruct((D, D), BF16)],
        compiler_params=_cparams(),
    )(dy, ylru, yconf, w_out)


_AFTER = pl.BlockSpec(memory_space=pl.ANY)


def _conf_bwd_proj(dycat, p, z, vc, ln_w, ln_b, pw_w, after):
    def body(dy_ref, p_ref, g_ref, vc_ref, lw_ref, lb_ref, w_ref, after_ref,
             dvc_ref, dgc_ref, dpw_ref, vecs_ref, dp_s, s_s, ds_s):
        i = pl.program_id(0)
        lw, lb = lw_ref[...], lb_ref[...]

        @pl.when(i == 0)
        def _():
            dpw_ref[...] = jnp.zeros_like(dpw_ref)
            vecs_ref[...] = jnp.zeros_like(vecs_ref)

        def pre_chunk(ci, carry):
            r0 = pl.multiple_of(ci * R, R)
            for half in range(2):
                rr = r0 + 8 * half
                dyv = dy_ref[pl.ds(rr, 8), :]
                g = g_ref[pl.ds(rr, 8), :]
                sg = _sig(g)
                dp = dyv * (g * sg)
                dg = dyv * p_ref[pl.ds(rr, 8), :] * (sg * (1.0 + g * (1.0 - sg)))
                vecs_ref[0:8, :] += dp
                vecs_ref[8:16, :] += dg
                ds_s[pl.ds(rr, 8), :] = dp
                dvc_ref[pl.ds(rr, 8), :] = dg
            dp_s[pl.ds(r0, R), :] = ds_s[pl.ds(r0, R), :].astype(BF16)
            dgc_ref[pl.ds(r0, R), :] = dvc_ref[pl.ds(r0, R), :].astype(BF16)
            for half in range(2):
                rr = r0 + 8 * half
                _, _, ln = _ln_chunk(vc_ref[pl.ds(rr, 8), :], lw, lb)
                ds_s[pl.ds(rr, 8), :] = ln * _sig(ln)
            s_s[pl.ds(r0, R), :] = ds_s[pl.ds(r0, R), :].astype(BF16)
            return carry
        lax.fori_loop(0, TM // R, pre_chunk, 0, unroll=2)

        dpb = dp_s[...]
        ds_s[...] = lax.dot_general(dpb, w_ref[...], _NT, preferred_element_type=F32)
        dpw_ref[...] += lax.dot_general(s_s[...], dpb, _TN, preferred_element_type=F32)

        def post_chunk(ci, carry):
            r0 = pl.multiple_of(ci * 8, 8)
            xhat, rstd, ln = _ln_chunk(vc_ref[pl.ds(r0, 8), :], lw, lb)
            sl = _sig(ln)
            dln = ds_s[pl.ds(r0, 8), :] * (sl * (1.0 + ln * (1.0 - sl)))
            vecs_ref[16:24, :] += dln * xhat
            vecs_ref[24:32, :] += dln
            dxh = dln * lw
            m1 = jnp.mean(dxh, axis=-1, keepdims=True)
            m2 = jnp.mean(dxh * xhat, axis=-1, keepdims=True)
            dvc_ref[pl.ds(r0, 8), :] = rstd * (dxh - m1 - xhat * m2)
            return carry
        lax.fori_loop(0, TM // 8, post_chunk, 0, unroll=4)

    row = pl.BlockSpec((TM, DC), lambda i: (i, 0))
    vec = pl.BlockSpec((1, DC), lambda i: (0, 0))
    return pl.pallas_call(
        body, name="conf_bwd_proj",
        grid=(TP // TM,),
        in_specs=[pl.BlockSpec((TM, DC), lambda i: (i, 1)), row, pl.BlockSpec((TM, DC), lambda i: (i, 4)), row,
                  vec, vec, pl.BlockSpec((DC, DC), lambda i: (0, 0)), _AFTER],
        out_specs=[row, row, pl.BlockSpec((DC, DC), lambda i: (0, 0)), pl.BlockSpec((32, DC), lambda i: (0, 0))],
        out_shape=[jax.ShapeDtypeStruct((TP, DC), F32), jax.ShapeDtypeStruct((TP, DC), BF16),
                   jax.ShapeDtypeStruct((DC, DC), F32), jax.ShapeDtypeStruct((32, DC), F32)],
        scratch_shapes=[pltpu.VMEM((TM, DC), BF16), pltpu.VMEM((TM, DC), BF16), pltpu.VMEM((TM, DC), F32)],
        compiler_params=_cparams(),
    )(dycat, p, z, vc, ln_w, ln_b, pw_w, after)


def _conf_bwd_conv(dvc, z, dw_w, after):
    def body(dvc_ref, u1_ref, u2_ref, w_ref, after_ref, du_ref, dw_ref, vecs_ref, vs, dvs):
        vs[pl.ds(0, KWP), :] = jnp.zeros((KWP, CBC), F32)
        dvs[pl.ds(TP, KWP), :] = jnp.zeros((KWP, CBC), F32)
        dw_ref[...] = jnp.zeros_like(dw_ref)
        vecs_ref[...] = jnp.zeros_like(vecs_ref)

        def fill_chunk(ci, carry):
            r0 = pl.multiple_of(ci * RC, RC)
            vs[pl.ds(KWP + r0, RC), :] = u1_ref[pl.ds(r0, RC), :] * _sig(u2_ref[pl.ds(r0, RC), :])
            dv = dvc_ref[pl.ds(r0, RC), :]
            dvs[pl.ds(r0, RC), :] = dv
            vecs_ref[0:8, :] += _fold_rows(dv)
            return carry
        lax.fori_loop(0, TP // RC, fill_chunk, 0)

        def conv_chunk(ci, carry):
            r0 = pl.multiple_of(ci * RC, RC)
            vbuf = vs[pl.ds(r0, KWP + RC), :]
            dbuf = dvs[pl.ds(r0, KWP + RC), :]
            dcur = dbuf[0:RC, :]
            dv = jnp.zeros((RC, CBC), F32)
            for rr in range(8):
                vroll = vbuf if rr == 0 else pltpu.roll(vbuf, rr, 0)
                droll = dbuf if rr == 0 else pltpu.roll(dbuf, KWP + RC - rr, 0)
                for q in range(4):
                    s = 8 * q + rr
                    if s > KW - 1:
                        continue
                    k = KW - 1 - s
                    dv = dv + droll[8 * q:8 * q + RC, :] * w_ref[k:k + 1, :]
                    dw_ref[8 * k:8 * k + 8, :] += _fold_rows(dcur * vroll[KWP - 8 * q:KWP - 8 * q + RC, :])
            u1 = u1_ref[pl.ds(r0, RC), :]
            sg = _sig(u2_ref[pl.ds(r0, RC), :])
            du1 = dv * sg
            du2 = dv * u1 * (sg * (1.0 - sg))
            du_ref[0, pl.ds(r0, RC), :] = du1.astype(BF16)
            du_ref[1, pl.ds(r0, RC), :] = du2.astype(BF16)
            vecs_ref[8:16, :] += _fold_rows(du1)
            vecs_ref[16:24, :] += _fold_rows(du2)
            return carry
        lax.fori_loop(0, TP // RC, conv_chunk, 0)

    blk = pl.BlockSpec((TP, CBC), lambda j: (0, j))
    return pl.pallas_call(
        body, name="conf_bwd_conv",
        grid=(NCBC,),
        in_specs=[blk, pl.BlockSpec((TP, CBC), lambda j: (0, 2 * NCBC + j)),
                  pl.BlockSpec((TP, CBC), lambda j: (0, 3 * NCBC + j)), pl.BlockSpec((KWP, CBC), lambda j: (0, j)),
                  _AFTER],
        out_specs=[pl.BlockSpec((2, TP, CBC), lambda j: (0, 0, j)), pl.BlockSpec((8 * KWP, CBC), lambda j: (0, j)),
                   pl.BlockSpec((24, CBC), lambda j: (0, j))],
        out_shape=[jax.ShapeDtypeStruct((2, TP, DC), BF16),
                   jax.ShapeDtypeStruct((8 * KWP, DC), F32), jax.ShapeDtypeStruct((24, DC), F32)],
        scratch_shapes=[pltpu.VMEM((TP + KWP, CBC), F32), pltpu.VMEM((TP + KWP, CBC), F32)],
        compiler_params=_cparams(),
    )(dvc, z, z, dw_w, after)


def _lru_bwd(dycat, z, xc, hs, conv_w, wa_g, b_a, wx_g, b_x, lam, after):
    NV = 6

    def body(dy_ref, x_ref, g_ref, xc_ref, hs_ref, cw_ref, wa_ref, ba_ref, wx_ref, bx_ref, lam_ref, after_ref,
             dzl_ref, dwa_ref, dwx_ref, dcw_ref, vecs_ref, ga_s, gx_s, dxc_s):
        vecs_ref[...] = jnp.zeros_like(vecs_ref)
        dcw_ref[...] = jnp.zeros_like(dcw_ref)
        dxc_s[pl.ds(TP, 8), :] = jnp.zeros((8, CB), F32)

        def gate_chunk(ci, carry):
            r0 = pl.multiple_of(ci * TM, TM)
            xb = xc_ref[pl.ds(r0, TM), :].astype(BF16)
            ga_s[pl.ds(r0, TM), :] = jnp.dot(xb, wa_ref[...], preferred_element_type=F32) + ba_ref[...]
            gx_s[pl.ds(r0, TM), :] = jnp.dot(xb, wx_ref[...], preferred_element_type=F32) + bx_ref[...]
            return carry
        lax.fori_loop(0, TP // TM, gate_chunk, 0)

        sp8 = LRU_C * _softplus(-lam_ref[...])
        row = _row_iota((R, CB))
        nchunk = TP // R

        def scan_chunk(cj, carry):
            a_next, lam_next = carry
            ci = nchunk - 1 - cj
            r0 = pl.multiple_of(ci * R, R)
            dyv = dy_ref[pl.ds(r0, R), :]
            g = g_ref[pl.ds(r0, R), :]
            hv = hs_ref[pl.ds(r0, R), :]
            xc = xc_ref[pl.ds(r0, R), :]
            sg = _sig(g)
            dgl = dyv * hv * (sg * (1.0 + g * (1.0 - sg)))
            dzl_ref[1, pl.ds(r0, R), :] = dgl.astype(BF16)
            vecs_ref[0:8, :] += _fold8(dgl)
            dhs = dyv * (g * sg)
            r, i, a, mult = _gate_values(ga_s[pl.ds(r0, R), :], gx_s[pl.ds(r0, R), :], xc, sp8)
            b = jnp.where(row == R - 1, a_next, pltpu.roll(a, R - 1, 0))
            lv = dhs
            k = 1
            while k < R:
                m = row < R - k
                lv = jnp.where(m, lv + b * pltpu.roll(lv, R - k, 0), lv)
                b = jnp.where(m, b * pltpu.roll(b, R - k, 0), b)
                k *= 2
            lv = lv + b * lam_next
            p0 = pl.multiple_of(jnp.maximum(r0 - 8, 0), 8)
            hprev8 = jnp.where(ci > 0, hs_ref[pl.ds(p0, 8), :], 0.0)
            hprev = pltpu.roll(jnp.concatenate([hprev8, hv], axis=0), 1, 0)[8:8 + R, :]
            da = lv * hprev
            ixc = i * xc
            dmult = lv * ixc
            di = lv * mult * xc
            dxc_s[pl.ds(r0, R), :] = lv * mult * i
            a2 = a * a
            dlog_a = da * a - dmult * a2 / mult
            vecs_ref[32:40, :] += _fold8(dlog_a * r)
            dga = -(dlog_a * sp8) * r * (1.0 - r)
            dgx = di * i * (1.0 - i)
            ga_s[pl.ds(r0, R), :] = dga
            gx_s[pl.ds(r0, R), :] = dgx
            vecs_ref[16:24, :] += _fold8(dga)
            vecs_ref[24:32, :] += _fold8(dgx)
            a_first = jnp.sum(jnp.where(row == 0, a, 0.0), axis=0, keepdims=True)
            l_first = jnp.sum(jnp.where(row == 0, lv, 0.0), axis=0, keepdims=True)
            return a_first, l_first
        lax.fori_loop(0, nchunk, scan_chunk, (jnp.zeros((1, CB), F32), jnp.zeros((1, CB), F32)))

        dwa_ref[...] = jnp.zeros_like(dwa_ref)
        dwx_ref[...] = jnp.zeros_like(dwx_ref)

        def mm_chunk(ci, carry):
            r0 = pl.multiple_of(ci * TM, TM)
            xb = xc_ref[pl.ds(r0, TM), :].astype(BF16)
            dgab = ga_s[pl.ds(r0, TM), :].astype(BF16)
            dgxb = gx_s[pl.ds(r0, TM), :].astype(BF16)
            dxc_s[pl.ds(r0, TM), :] += (lax.dot_general(dgab, wa_ref[...], _NT, preferred_element_type=F32)
                                        + lax.dot_general(dgxb, wx_ref[...], _NT, preferred_element_type=F32))
            dwa_ref[...] += lax.dot_general(xb, dgab, _TN, preferred_element_type=F32)
            dwx_ref[...] += lax.dot_general(xb, dgxb, _TN, preferred_element_type=F32)
            return carry
        lax.fori_loop(0, TP // TM, mm_chunk, 0)

        taps = [cw_ref[k:k + 1, :] for k in range(LW)]

        def conv_chunk(ci, carry):
            r0 = pl.multiple_of(ci * R, R)
            dbuf = dxc_s[pl.ds(r0, R + 8), :]
            dcur = dbuf[0:R, :]
            p0 = pl.multiple_of(jnp.maximum(r0 - 8, 0), 8)
            xprev = jnp.where(ci > 0, x_ref[pl.ds(p0, 8), :], 0.0)
            xbuf = jnp.concatenate([xprev, x_ref[pl.ds(r0, R), :]], axis=0)
            dxl = dcur * taps[LW - 1]
            dcw_ref[8 * (LW - 1):8 * LW, :] += _fold8(dcur * xbuf[8:8 + R, :])
            for s in range(1, LW):
                k = LW - 1 - s
                dxl = dxl + pltpu.roll(dbuf, R + 8 - s, 0)[0:R, :] * taps[k]
                dcw_ref[8 * k:8 * k + 8, :] += _fold8(dcur * pltpu.roll(xbuf, s, 0)[8:8 + R, :])
            dzl_ref[0, pl.ds(r0, R), :] = dxl.astype(BF16)
            vecs_ref[8:16, :] += _fold8(dxl)
            vecs_ref[40:48, :] += _fold8(dcur)
            return carry
        lax.fori_loop(0, TP // R, conv_chunk, 0)
        vecs_ref[32:40, :] = vecs_ref[32:40, :] * (LRU_C * _sig(-lam_ref[...]))

    col = lambda off: pl.BlockSpec((TP, CB), lambda j: (0, off + j))
    vec = pl.BlockSpec((1, CB), lambda j: (0, j))
    wsp = pl.BlockSpec((None, CB, CB), lambda j: (j, 0, 0))
    return pl.pallas_call(
        body, name="lru_bwd",
        grid=(NCB,),
        in_specs=[col(0), col(0), col(NCB), col(0), col(0), pl.BlockSpec((LW, CB), lambda j: (0, j)),
                  wsp, vec, wsp, vec, vec, _AFTER],
        out_specs=[pl.BlockSpec((2, TP, CB), lambda j: (0, 0, j)), wsp, wsp,
                   pl.BlockSpec((8 * LW, CB), lambda j: (0, j)), pl.BlockSpec((8 * NV, CB), lambda j: (0, j))],
        out_shape=[jax.ShapeDtypeStruct((2, TP, DL), BF16),
                   jax.ShapeDtypeStruct((NCB, CB, CB), F32), jax.ShapeDtypeStruct((NCB, CB, CB), F32),
                   jax.ShapeDtypeStruct((8 * LW, DL), F32), jax.ShapeDtypeStruct((8 * NV, DL), F32)],
        scratch_shapes=[pltpu.VMEM((TP, CB), F32), pltpu.VMEM((TP, CB), F32), pltpu.VMEM((TP + 8, CB), F32)],
        compiler_params=_cparams(),
    )(dycat, z, z, xc, hs, conv_w, wa_g, b_a, wx_g, b_x, lam, after)


def _dz_section(sec, dzl_ref, dzc_ref, dgc_ref, use):
    @pl.when(sec < 2)
    def _():
        use(dzl_ref)

    @pl.when(jnp.logical_and(sec >= 2, sec < 4))
    def _():
        use(dzc_ref)

    @pl.when(sec == 4)
    def _():
        use(dgc_ref)


def _dz_specs(rows, index):
    return [pl.BlockSpec((None, rows, 1024), lambda a, b: (jnp.minimum(index(a, b)[1], 1), index(a, b)[0], 0)),
            pl.BlockSpec((None, rows, 1024), lambda a, b: (jnp.clip(index(a, b)[1] - 2, 0, 1), index(a, b)[0], 0)),
            pl.BlockSpec((rows, 1024), lambda a, b: (index(a, b)[0], 0))]


def _inproj_wgrad(name, hn, dzs):
    KB = 512
    nsec = dzs.shape[0]

    def body(hn_ref, dz_ref, dw_ref):
        dw_ref[...] = lax.dot_general(hn_ref[...], dz_ref[...], _TN, preferred_element_type=F32).astype(BF16)

    return pl.pallas_call(
        body, name=name,
        grid=(nsec, D // KB),
        in_specs=[pl.BlockSpec((TP, KB), lambda n, kb: (0, kb)),
                  pl.BlockSpec((None, TP, 1024), lambda n, kb: (n, 0, 0))],
        out_specs=pl.BlockSpec((KB, 1024), lambda n, kb: (kb, n)),
        out_shape=jax.ShapeDtypeStruct((D, nsec * 1024), BF16),
        compiler_params=_cparams(),
    )(hn, dzs)


def _sum_win_parts(parts_a, parts_b, parts_c):
    RB = 64

    def body(a_ref, b_ref, c_ref, o_ref):
        def chunk(ci, carry):
            r0 = pl.multiple_of(ci * R, R)
            for ref, base, ncol in ((a_ref, 0, 2048), (b_ref, 2048, 2048), (c_ref, 4096, 1024)):
                for c0 in range(0, ncol, 512):
                    acc = ref[0, pl.ds(r0, R), c0:c0 + 512].astype(F32)
                    for sidx in range(1, NDEV):
                        acc = acc + ref[sidx, pl.ds(r0, R), c0:c0 + 512].astype(F32)
                    o_ref[pl.ds(r0, R), base + c0:base + c0 + 512] = acc.astype(BF16)
            return carry
        lax.fori_loop(0, RB // R, chunk, 0)

    spec = lambda ncol: pl.BlockSpec((NDEV, RB, ncol), lambda i: (0, i, 0))
    return pl.pallas_call(
        body, name="sum_win_parts",
        grid=(D // NDEV // RB,),
        in_specs=[spec(2048), spec(2048), spec(1024)],
        out_specs=pl.BlockSpec((RB, NIN), lambda i: (i, 0)),
        out_shape=jax.ShapeDtypeStruct((D // NDEV, NIN), BF16),
        compiler_params=_cparams(),
    )(parts_a, parts_b, parts_c)


def _inproj_bwd(dzl, dzc, dgc, w_in, h, dout, pre_w, after):
    nsec = NIN // 1024

    def body(dzl_ref, dzc_ref, dgc_ref, w_ref, h_ref, dout_ref, pw_ref, after_ref, gx_hbm, dmeta_ref, dpw_ref,
             acc_s, dh_s, sem):
        i = pl.program_id(0)
        s = pl.program_id(1)

        def gx_copy(t):
            lo, n, off = _tile_rows(t)
            return pltpu.make_async_copy(dh_s.at[pl.ds(off, n)], gx_hbm.at[pl.ds(lo, n)], sem)

        @pl.when(s == 0)
        def _():
            acc_s[...] = jnp.zeros_like(acc_s)

        def use(dz_ref):
            acc_s[...] += lax.dot_general(dz_ref[...], w_ref[...], _NT, preferred_element_type=F32)
        _dz_section(s, dzl_ref, dzc_ref, dgc_ref, use)

        @pl.when(jnp.logical_and(i == 0, s == nsec - 1))
        def _():
            dpw_ref[...] = jnp.zeros_like(dpw_ref)

        @pl.when(s == nsec - 1)
        def _():
            _for_tile(i - 1, lambda t: gx_copy(t).wait())
            pw = pw_ref[...]

            def chunk(ci, carry):
                r0 = pl.multiple_of(ci * 8, 8)
                hv = h_ref[pl.ds(r0, 8), :]
                dhn = acc_s[pl.ds(r0, 8), :]
                rs = lax.rsqrt(jnp.mean(hv * hv, axis=-1, keepdims=True) + EPS)
                dpw_ref[...] += dhn * (hv * rs)
                gw = dhn * pw
                dot = jnp.mean(gw * hv, axis=-1, keepdims=True)
                dh_s[pl.ds(r0, 8), :] = rs * gw - hv * (rs * rs * rs * dot) + dout_ref[pl.ds(r0, 8), :]
                return carry
            lax.fori_loop(0, TM // 8, chunk, 0, unroll=4)
            _for_tile(i, lambda t: gx_copy(t).start())

            @pl.when(i == 0)
            def _():
                dmeta_ref[...] = dh_s[0:NMETA, :]

            @pl.when(i == NTILE - 1)
            def _():
                gx_copy(NTILE - 1).wait()

    row = pl.BlockSpec((TM, D), lambda i, s: (i, 0))
    return pl.pallas_call(
        body, name="inproj_bwd",
        grid=(TP // TM, nsec),
        in_specs=_dz_specs(TM, lambda i, s: (i, s)) + [
            pl.BlockSpec((D, 1024), lambda i, s: (0, s)), row, row, pl.BlockSpec((1, D), lambda i, s: (0, 0)),
            _AFTER],
        out_specs=[pl.BlockSpec(memory_space=pl.ANY), pl.BlockSpec((NMETA, D), lambda i, s: (0, 0)),
                   pl.BlockSpec((8, D), lambda i, s: (0, 0))],
        out_shape=[jax.ShapeDtypeStruct((SEQ, D), F32), jax.ShapeDtypeStruct((NMETA, D), F32),
                   jax.ShapeDtypeStruct((8, D), F32)],
        scratch_shapes=[pltpu.VMEM((TM, D), F32), pltpu.VMEM((TM, D), F32), pltpu.SemaphoreType.DMA(())],
        compiler_params=_cparams(),
    )(dzl, dzc, dgc, w_in, h, dout, pre_w, after)


def _adamw(name, parts, w, m, v, block_rows):
    rows, cols = w.shape
    nparts = parts.shape[0]
    cw = cols if cols <= 640 else 512

    def body(p_ref, w_ref, m_ref, v_ref, g_ref, d_ref, nm_ref, nv_ref):
        def chunk(ci, carry):
            r0 = pl.multiple_of(ci * R, R)
            for c0 in range(0, cols, cw):
                at = (pl.ds(r0, R), slice(c0, c0 + cw))
                g = p_ref[(0,) + at].astype(F32)
                for sidx in range(1, nparts):
                    g = g + p_ref[(sidx,) + at].astype(F32)
                delta, mv, vv = _adam_math(g, w_ref[at], m_ref[at], v_ref[at])
                g_ref[at] = g
                nm_ref[at] = mv
                nv_ref[at] = vv
                d_ref[at] = delta
            return carry
        lax.fori_loop(0, block_rows // R, chunk, 0)

    blk = pl.BlockSpec((block_rows, cols), lambda i: (i, 0))
    shp = jax.ShapeDtypeStruct((rows, cols), F32)
    return pl.pallas_call(
        body, name=name,
        grid=(rows // block_rows,),
        in_specs=[pl.BlockSpec((nparts, block_rows, cols), lambda i: (0, i, 0)), blk, blk, blk],
        out_specs=[blk, blk, blk, blk],
        out_shape=[shp, shp, shp, shp],
        compiler_params=_cparams(),
    )(parts, w, m, v)


def _adam_math(g, w, m, v):
    c1 = 1.0 / (1.0 - ADAM_B1 ** ADAM_STEP)
    c2 = 1.0 / (1.0 - ADAM_B2 ** ADAM_STEP)
    mv = ADAM_B1 * m + (1.0 - ADAM_B1) * g
    vv = ADAM_B2 * v + (1.0 - ADAM_B2) * (g * g)
    upd = (mv * c1) / (jnp.sqrt(vv * c2) + ADAM_EPS) + ADAM_WD * w
    return -ADAM_LR * upd, mv, vv


_VEC = [("pre_norm_w", 2), ("post_norm_w", 2), ("b_in", 5), ("lru_conv_b", 1), ("b_gate_a", 1), ("b_gate_x", 1),
        ("lru_lambda", 1), ("conf_dw_b", 1), ("conf_ln_w", 1), ("conf_ln_b", 1), ("conf_pw_b", 1)]
_VEC_ROWS = 24
_LOSS_ROW = 17
_SM_ROWS = 64


def _pack_grads(dprew_acc, dpostw_acc, cvecs, kvecs, lvecs, dcw_acc, ddw_acc, dh, loss_acc):
    def body(pre_ref, post_ref, c_ref, k_ref, l_ref, dcw_ref, ddw_ref, dh_ref, loss_ref, vec_ref, small_ref, tmp):
        s8 = lambda ref, r: jnp.sum(ref[8 * r:8 * r + 8, :], axis=0, keepdims=True)
        vec_ref[...] = jnp.zeros_like(vec_ref)
        pre, post = s8(pre_ref, 0), s8(post_ref, 0)
        rows = [pre[:, 0:1024], pre[:, 1024:2048], post[:, 0:1024], post[:, 1024:2048],
                s8(l_ref, 1), s8(l_ref, 0), s8(k_ref, 1), s8(k_ref, 2), s8(c_ref, 1),
                s8(l_ref, 5), s8(l_ref, 2), s8(l_ref, 3), s8(l_ref, 4),
                s8(k_ref, 0), s8(c_ref, 2), s8(c_ref, 3), s8(c_ref, 0)]
        for r, val in enumerate(rows):
            vec_ref[r:r + 1, :] = val
        vec_ref[_LOSS_ROW:_LOSS_ROW + 1, :] = jnp.zeros((1, 1024), F32) + (0.5 / D) * jnp.sum(loss_ref[...])

        small_ref[...] = jnp.zeros_like(small_ref)
        for k in range(LW):
            tmp[k:k + 1, :] = s8(dcw_ref, k)
        for k in range(KW):
            tmp[8 + k:9 + k, :] = s8(ddw_ref, k)
        for d in range(NDEV):
            small_ref[d, 0:LW, 0:128] = tmp[0:LW, 128 * d:128 * d + 128]
            small_ref[d, 8:8 + KW, 0:128] = tmp[8:8 + KW, 128 * d:128 * d + 128]
            small_ref[d, 40:56, :] = dh_ref[:, 256 * d:256 * d + 256]

    full = lambda a: pl.BlockSpec(a.shape, lambda i: (0,) * a.ndim)
    ins = [dprew_acc, dpostw_acc, cvecs, kvecs, lvecs, dcw_acc, ddw_acc]
    return pl.pallas_call(
        body, name="pack_grads",
        grid=(1,),
        in_specs=[full(a) for a in ins] + [full(dh), full(loss_acc)],
        out_specs=[pl.BlockSpec((_VEC_ROWS, 1024), lambda i: (0, 0)),
                   pl.BlockSpec((NDEV, _SM_ROWS, 256), lambda i: (0, 0, 0))],
        out_shape=[jax.ShapeDtypeStruct((_VEC_ROWS, 1024), F32), jax.ShapeDtypeStruct((NDEV, _SM_ROWS, 256), F32)],
        scratch_shapes=[pltpu.VMEM((40, 1024), F32)],
        compiler_params=_cparams(),
    )(*ins, dh, loss_acc)


def _adamw_vec(parts, W, M, V):
    nv = len(_VEC)

    def body(*refs):
        p_ref = refs[0]
        w_refs, m_refs, v_refs = refs[1:1 + nv], refs[1 + nv:1 + 2 * nv], refs[1 + 2 * nv:1 + 3 * nv]
        outs = refs[1 + 3 * nv:]

        def total(r):
            acc = p_ref[0, r:r + 1, :]
            for sidx in range(1, NDEV):
                acc = acc + p_ref[sidx, r:r + 1, :]
            return acc

        row = 0
        for idx, (_, nrows) in enumerate(_VEC):
            for part in range(nrows):
                cols = slice(1024 * part, 1024 * part + 1024)
                g = total(row + part)
                delta, mv, vv = _adam_math(g, w_refs[idx][:, cols], m_refs[idx][:, cols], v_refs[idx][:, cols])
                for o, val in zip(outs[4 * idx:4 * idx + 4], (g, delta, mv, vv)):
                    o[:, cols] = val
            row += nrows
        outs[-1][...] = total(_LOSS_ROW)[:, 0:128]

    names = [n for n, _ in _VEC]
    flat = lambda d: [d[n].reshape(1, -1) for n in names]
    ws, ms, vs = flat(W), flat(M), flat(V)
    res = pl.pallas_call(
        body, name="adamw_vec",
        out_shape=[jax.ShapeDtypeStruct(w.shape, F32) for w in ws for _ in range(4)]
        + [jax.ShapeDtypeStruct((1, 128), F32)],
        compiler_params=_cparams(),
    )(parts, *ws, *ms, *vs)
    return {n: tuple(res[4 * i:4 * i + 4]) for i, n in enumerate(names)}, res[-1]


def _adamw_small(parts, W, M, V):
    where = {"lru_conv_w": (slice(0, LW), slice(0, 128)), "conf_dw_w": (slice(8, 8 + KW), slice(0, 128)),
             "meta_tokens": (slice(40, 56), slice(0, 256))}
    names = list(where)

    def body(*refs):
        p_ref = refs[0]
        outs = refs[10:]
        for idx, n in enumerate(names):
            rs, cs = where[n]
            g = p_ref[0, rs, cs]
            for sidx in range(1, NDEV):
                g = g + p_ref[sidx, rs, cs]
            delta, mv, vv = _adam_math(g, refs[1 + idx][...], refs[4 + idx][...], refs[7 + idx][...])
            for o, val in zip(outs[4 * idx:4 * idx + 4], (g, delta, mv, vv)):
                o[...] = val

    two_d = lambda a: a.reshape(a.shape[-2:])
    ws, ms, vs = ([two_d(d[n]) for n in names] for d in (W, M, V))
    res = pl.pallas_call(
        body, name="adamw_small",
        out_shape=[jax.ShapeDtypeStruct(w.shape, F32) for w in ws for _ in range(4)],
        compiler_params=_cparams(),
    )(parts, *ws, *ms, *vs)
    return {n: tuple(res[4 * i:4 * i + 4]) for i, n in enumerate(names)}


def _pack_small(lru_cw, dw_w, meta):
    buf = jnp.zeros((_SM_ROWS, 256), F32)
    buf = buf.at[0:LW, 0:128].set(lru_cw)
    buf = buf.at[8:8 + dw_w.shape[0], 0:128].set(dw_w)
    return buf.at[40:56, :].set(meta)


def _block_diag4(w):
    w4 = w.reshape(NCB, 4, 64, 64)
    eye = jnp.eye(4, dtype=w.dtype)
    return jnp.einsum("ghij,hk->ghikj", w4, eye).reshape(NCB, CB, CB)


def _diag_blocks(g):
    g5 = g.reshape(NCB, 4, 64, 4, 64)
    return jnp.stack([g5[:, hh, :, hh, :] for hh in range(4)], axis=1).reshape(16, 64, 64)


def _gate_mats(W):
    return _block_diag4(W["w_gate_a"][0]).astype(BF16), _block_diag4(W["w_gate_x"][0]).astype(BF16)


def _local_step(x, target, meta_full, inproj, out_weights, lru_cw_full, dw_w_full, W, gate_mats, send):
    wa_g, wx_g = gate_mats

    h, hn = _prenorm(x, meta_full, W["pre_norm_w"])
    z, win_full = inproj(hn)
    ylru, xc, hs = _lru_fwd(z, lru_cw_full, W["lru_conv_b"], wa_g, W["b_gate_a"], wx_g, W["b_gate_x"],
                            W["lru_lambda"])
    vc = _conf_fwd_conv(z, dw_w_full, W["conf_dw_b"])
    wout_full, pw_full = out_weights(vc)
    yconf, p = _conf_fwd_proj(vc, z, W["conf_ln_w"], W["conf_ln_b"], pw_full, W["conf_pw_b"])
    dout, dy, loss_acc, dpostw_acc = _outproj_loss(ylru, yconf, wout_full, h, target, W["post_norm_w"])

    dycat, dwout_part = _outproj_bwd(dy, ylru, yconf, wout_full)
    tok = send("w_out", ("w_out", dwout_part))
    dvc, dgc, dpw_part, cvecs = _conf_bwd_proj(dycat, p, z, vc, W["conf_ln_w"], W["conf_ln_b"], pw_full, tok)
    tok = send("w_in_c", ("conf_pw_w", dpw_part), ("w_in_c", _inproj_wgrad("inproj_wgrad_c", hn, dgc[None])))
    dzc, ddw_acc, kvecs = _conf_bwd_conv(dvc, z, dw_w_full, tok)
    tok = send("w_in_b", ("w_in_b", _inproj_wgrad("inproj_wgrad_b", hn, dzc)))
    dzl, dwa_g, dwx_g, dcw_acc, lvecs = _lru_bwd(dycat, z, xc, hs, lru_cw_full, wa_g, W["b_gate_a"], wx_g,
                                                 W["b_gate_x"], W["lru_lambda"], tok)
    tok = send("w_in_a", ("w_in_a", _inproj_wgrad("inproj_wgrad_a", hn, dzl)),
               ("w_gate_a", _diag_blocks(dwa_g).reshape(16 * 64, 64)),
               ("w_gate_x", _diag_blocks(dwx_g).reshape(16 * 64, 64)))
    grad_x, dmeta, dprew_acc = _inproj_bwd(dzl, dzc, dgc, win_full, h, dout, W["pre_norm_w"], tok)

    vec_pack, small_part = _pack_grads(dprew_acc, dpostw_acc, cvecs, kvecs, lvecs, dcw_acc, ddw_acc, dmeta, loss_acc)
    return grad_x, vec_pack, small_part


def kernel(x, meta_tokens, pre_norm_w, post_norm_w, w_in, b_in, lru_conv_w, lru_conv_b, w_gate_a, b_gate_a, w_gate_x, b_gate_x, lru_lambda, conf_dw_w, conf_dw_b, conf_ln_w, conf_ln_b, conf_pw_w, conf_pw_b, w_out, loss_target, m_meta_tokens, m_pre_norm_w, m_post_norm_w, m_w_in, m_b_in, m_lru_conv_w, m_lru_conv_b, m_w_gate_a, m_b_gate_a, m_w_gate_x, m_b_gate_x, m_lru_lambda, m_conf_dw_w, m_conf_dw_b, m_conf_ln_w, m_conf_ln_b, m_conf_pw_w, m_conf_pw_b, m_w_out, v_meta_tokens, v_pre_norm_w, v_post_norm_w, v_w_in, v_b_in, v_lru_conv_w, v_lru_conv_b, v_w_gate_a, v_b_gate_a, v_w_gate_x, v_b_gate_x, v_lru_lambda, v_conf_dw_w, v_conf_dw_b, v_conf_ln_w, v_conf_ln_b, v_conf_pw_w, v_conf_pw_b, v_w_out):
    W = dict(meta_tokens=meta_tokens, pre_norm_w=pre_norm_w, post_norm_w=post_norm_w, w_in=w_in, b_in=b_in,
             lru_conv_w=lru_conv_w, lru_conv_b=lru_conv_b, w_gate_a=w_gate_a, b_gate_a=b_gate_a,
             w_gate_x=w_gate_x, b_gate_x=b_gate_x, lru_lambda=lru_lambda, conf_dw_w=conf_dw_w,
             conf_dw_b=conf_dw_b, conf_ln_w=conf_ln_w, conf_ln_b=conf_ln_b, conf_pw_w=conf_pw_w,
             conf_pw_b=conf_pw_b, w_out=w_out)
    M = dict(meta_tokens=m_meta_tokens, pre_norm_w=m_pre_norm_w, post_norm_w=m_post_norm_w, w_in=m_w_in,
             b_in=m_b_in, lru_conv_w=m_lru_conv_w, lru_conv_b=m_lru_conv_b, w_gate_a=m_w_gate_a,
             b_gate_a=m_b_gate_a, w_gate_x=m_w_gate_x, b_gate_x=m_b_gate_x, lru_lambda=m_lru_lambda,
             conf_dw_w=m_conf_dw_w, conf_dw_b=m_conf_dw_b, conf_ln_w=m_conf_ln_w, conf_ln_b=m_conf_ln_b,
             conf_pw_w=m_conf_pw_w, conf_pw_b=m_conf_pw_b, w_out=m_w_out)
    V = dict(meta_tokens=v_meta_tokens, pre_norm_w=v_pre_norm_w, post_norm_w=v_post_norm_w, w_in=v_w_in,
             b_in=v_b_in, lru_conv_w=v_lru_conv_w, lru_conv_b=v_lru_conv_b, w_gate_a=v_w_gate_a,
             b_gate_a=v_b_gate_a, w_gate_x=v_w_gate_x, b_gate_x=v_b_gate_x, lru_lambda=v_lru_lambda,
             conf_dw_w=v_conf_dw_w, conf_dw_b=v_conf_dw_b, conf_ln_w=v_conf_ln_w, conf_ln_b=v_conf_ln_b,
             conf_pw_w=v_conf_pw_w, conf_pw_b=v_conf_pw_b, w_out=v_w_out)
    names = list(W.keys())
    shapes = {n: W[n].shape for n in names}

    small = _pack_small(lru_conv_w[0], conf_dw_w[0], meta_tokens)
    (small_flight,), tok = _exchange_start("gather_small_start", [
        (small, jax.ShapeDtypeStruct((NDEV, _SM_ROWS, 256), F32), _whole, _slot)])
    win_flight, tok = _win_gather_start(w_in[0].astype(BF16) + tok[0, 0].astype(BF16))
    gate_mats = _gate_mats(W)
    wout_shard = w_out[0].astype(BF16) + tok[0, 0].astype(BF16)
    pw_shard = conf_pw_w[0].astype(BF16)
    cast_done = (gate_mats[0][0, 0:8, 0:128] + gate_mats[1][0, 0:8, 0:128]
                 + wout_shard[0:8, 0:128] + pw_shard[0:8, 0:128])
    win_flight, tok = _win_gather_links(win_flight, cast_done)
    gathered, tok = _exchange_start("gather_out_start", [
        (wout_shard + tok[0, 0].astype(BF16), jax.ShapeDtypeStruct((D, D), BF16), _whole, _rows(D // NDEV)),
        (pw_shard, jax.ShapeDtypeStruct((DC, DC), BF16), _whole, _rows(DC // NDEV)),
    ])
    (small_all,) = _exchange_wait("gather_small_wait", [small_flight], tok)
    unshard = lambda a: jnp.transpose(a, (1, 0, 2)).reshape(a.shape[1], -1)
    lru_cw_full = unshard(small_all[:, 0:LW, 0:128])
    dw_w_full = unshard(small_all[:, 8:8 + KWP, 0:128])
    meta_full = unshard(small_all[:, 40:56, :])

    def out_weights(after):
        return _exchange_wait("gather_out_wait", gathered, after)

    def inproj(hn):
        xi, yi, ci = lax.axis_index("x"), lax.axis_index("y"), lax.axis_index("c")
        shard = lambda px, py, pc: (4 * px + 2 * py + pc).astype(jnp.int32)
        here = jnp.stack([shard(xi, yi, ci), shard(xi, yi, 1 - ci)])
        over_links = jnp.stack([shard(1 - xi, yi, ci), shard(xi, 1 - yi, ci), shard(1 - xi, 1 - yi, ci)])
        flight = _win_gather_early(win_flight)
        z, land = _inproj_cols("inproj_here", here, hn, flight["land"], b_in, None)
        flight = _win_gather_forward(dict(flight, land=land), z)
        z, land = _inproj_cols("inproj_links", over_links, hn, flight["land"], b_in, z)
        land = _win_gather_wait(dict(flight, land=land))
        return _inproj_cols("inproj_sibling", over_links + 1 - 2 * ci, hn, land, b_in, z)

    row_stage = lambda ncol: (jax.ShapeDtypeStruct((NDEV, D // NDEV, ncol), BF16), _rows(D // NDEV))
    piece = {"w_in_a": row_stage(2048), "w_in_b": row_stage(2048), "w_in_c": row_stage(1024),
             "w_out": row_stage(D),
             "conf_pw_w": (jax.ShapeDtypeStruct((NDEV, DC // NDEV, DC), BF16), _rows(DC // NDEV)),
             "w_gate_a": (jax.ShapeDtypeStruct((NDEV, 16 * 64, 64), BF16), _whole),
             "w_gate_x": (jax.ShapeDtypeStruct((NDEV, 16 * 64, 64), BF16), _whole)}
    sent = {}

    def send(call, *named_parts):
        handles, token = _exchange_start(
            "scatter_" + call + "_start",
            [(part.astype(BF16), piece[name][0], piece[name][1], _slot) for name, part in named_parts])
        for (name, _), handle in zip(named_parts, handles):
            sent[name] = [handle]
        return token

    grad_x, vec_pack, small_part = _local_step(
        x[0], loss_target[0], meta_full, inproj, out_weights, lru_cw_full, dw_w_full, W, gate_mats, send)
    grad_x = grad_x[None]

    rest, tok = _exchange_start("scatter_rest_start", [
        (small_part, jax.ShapeDtypeStruct((NDEV, _SM_ROWS, 256), F32), _slot, _slot),
        (vec_pack, jax.ShapeDtypeStruct((NDEV, _VEC_ROWS, 1024), F32), _whole, _slot),
    ])
    (parts_c,) = _exchange_wait("scatter_w_in_c_wait", sent["w_in_c"], tok)
    (parts_b,) = _exchange_wait("scatter_w_in_b_wait", sent["w_in_b"], parts_c)
    (parts_a,) = _exchange_wait("scatter_w_in_a_wait", sent["w_in_a"], parts_b)
    win_rows = _sum_win_parts(parts_a, parts_b, parts_c)
    win_stage2, tok = _exchange_start("scatter_w_in_stage2_start", [
        (win_rows, jax.ShapeDtypeStruct((NDEV, D // NDEV, NIN // NDEV), BF16), _cols(NIN // NDEV), _slot)])

    G, DW, NM, NV = {}, {}, {}, {}
    (wout_parts,) = _exchange_wait("scatter_w_out_wait", sent["w_out"], tok)
    G["w_out"], DW["w_out"], NM["w_out"], NV["w_out"] = _adamw("adamw_w_out", wout_parts, w_out[0], m_w_out[0], v_w_out[0], 64)
    (pw_parts,) = _exchange_wait("scatter_conf_pw_w_wait", sent["conf_pw_w"], G["w_out"])
    G["conf_pw_w"], DW["conf_pw_w"], NM["conf_pw_w"], NV["conf_pw_w"] = _adamw(
        "adamw_pw", pw_parts, conf_pw_w[0], m_conf_pw_w[0], v_conf_pw_w[0], 128)
    res = {}
    wa_parts, wx_parts = _exchange_wait("scatter_w_gates_wait", sent["w_gate_a"] + sent["w_gate_x"], G["conf_pw_w"])
    for n, parts in (("w_gate_a", wa_parts), ("w_gate_x", wx_parts)):
        res[n] = _adamw("adamw_" + n, parts, *[d[n].reshape(16 * 64, 64) for d in (W, M, V)], 16 * 64)
    small_parts, vec_parts = _exchange_wait("scatter_rest_wait", rest, res["w_gate_x"][0])
    res.update(_adamw_small(small_parts, W, M, V))
    vec_res, loss_row = _adamw_vec(vec_parts, W, M, V)
    res.update(vec_res)
    (win_sum,) = _exchange_wait("scatter_w_in_stage2_wait", win_stage2, loss_row)
    res["w_in"] = _adamw("adamw_w_in", win_sum.reshape(1, D, NIN // NDEV), w_in[0], m_w_in[0], v_w_in[0], 256)
    for n, vals in res.items():
        for dst, val in zip((G, DW, NM, NV), vals):
            dst[n] = val
    for dst in (G, DW, NM, NV):
        for n in names:
            dst[n] = dst[n].reshape(shapes[n])
    loss = loss_row[0, 0]

    return (loss, grad_x, *[G[n] for n in names], *[DW[n] for n in names],
            *[NM[n] for n in names], *[NV[n] for n in names])
```

```python
import functools

import jax
import jax.numpy as jnp
from jax import lax
from jax.experimental import pallas as pl
from jax.experimental.pallas import tpu as pltpu

F32 = jnp.float32
BF16 = jnp.bfloat16

D = 2048
DL = 1024
DC = 1024
NIN = 5120
NMETA = 16
SEQ = 2048
T = NMETA + SEQ
TP = 2176
TM = 544
CB = 256
NCB = DL // CB
R = 16
KW = 31
KWP = 32
LW = 4
LRU_C = 8.0
EPS = 1e-6
NDEV = 8

ADAM_LR = 0.001
ADAM_B1 = 0.9
ADAM_B2 = 0.999
ADAM_EPS = 1e-08
ADAM_WD = 0.01
ADAM_STEP = 10

VMEM_LIMIT = 56 * 1024 * 1024


def _cparams():
    return pltpu.CompilerParams(vmem_limit_bytes=VMEM_LIMIT)


def _sig(x):
    return 1.0 / (1.0 + jnp.exp(-x))


def _expm1_neg(y):
    poly = y * (1.0 + y * (0.5 + y * (1.0 / 6.0 + y * (1.0 / 24.0 + y * (1.0 / 120.0)))))
    return jnp.where(y > -0.1, poly, jnp.exp(y) - 1.0)


def _softplus(x):
    e = jnp.exp(-jnp.abs(x))
    w = 1.0 + e
    l1p = jnp.where(w == 1.0, e, jnp.log(w) * e / (w - 1.0))
    return jnp.maximum(x, 0.0) + l1p


def _row_iota(shape):
    return lax.broadcasted_iota(jnp.int32, shape, 0)


def _fold8(v):
    return v[0:8, :] + v[8:16, :]


_FLIPS = [(k >> 2 & 1, k >> 1 & 1, k & 1) for k in range(1, NDEV)]
_HBM = pl.BlockSpec(memory_space=pltpu.HBM)
_SEM = pl.BlockSpec(memory_space=pltpu.SEMAPHORE)


def _peers():
    x, y, c = lax.axis_index("x"), lax.axis_index("y"), lax.axis_index("c")
    out = []
    for dx, dy, dc in _FLIPS:
        px = 1 - x if dx else x
        py = 1 - y if dy else y
        pc = 1 - c if dc else c
        out.append(((px, py, pc), 4 * px + 2 * py + pc))
    return 4 * x + 2 * y + c, out


def _exchange_start(name, items):
    n = len(items)

    def body(*refs):
        srcs, lands = refs[:n], refs[n:2 * n]
        outs = refs[2 * n:]
        send_sems, recv_sems, local_sems = outs[:n], outs[n:2 * n], outs[2 * n:3 * n]
        token = outs[-1]
        me, peers = _peers()
        for a in range(n):
            src_at, dst_at = items[a][2], items[a][3]
            pltpu.make_async_copy(src_at(srcs[a], me), dst_at(lands[a], me), local_sems[a]).start()
        for a in range(n):
            src_at, dst_at = items[a][2], items[a][3]
            for k, (pos, peer) in enumerate(peers):
                pltpu.make_async_remote_copy(
                    src_ref=src_at(srcs[a], peer), dst_ref=dst_at(lands[a], me),
                    send_sem=send_sems[a].at[k], recv_sem=recv_sems[a].at[k],
                    device_id=pos, device_id_type=pl.DeviceIdType.MESH).start()
        token[...] = jnp.zeros_like(token)

    srcs = [pltpu.with_memory_space_constraint(it[0], pltpu.HBM) for it in items]
    lands = [pltpu.with_memory_space_constraint(lax.empty(it[1].shape, it[1].dtype), pltpu.HBM) for it in items]
    sem7 = pltpu.SemaphoreType.DMA((NDEV - 1,))
    res = pl.pallas_call(
        body, name=name,
        out_shape=([sem7] * (2 * n) + [pltpu.SemaphoreType.DMA(())] * n
                   + [pltpu.HBM(a.shape, a.dtype) for a in srcs] + [pltpu.HBM(a.shape, a.dtype) for a in lands]
                   + [jax.ShapeDtypeStruct((8, 128), F32)]),
        in_specs=[_HBM] * (2 * n),
        out_specs=[_SEM] * (3 * n) + [_HBM] * (2 * n) + [pl.BlockSpec(memory_space=pltpu.VMEM)],
        input_output_aliases={i: 3 * n + i for i in range(2 * n)},
        compiler_params=pltpu.CompilerParams(has_side_effects=pltpu.SideEffectType.DATAFLOW_SIDE_EFFECTING),
    )(*srcs, *lands)
    handles = [dict(send=res[a], recv=res[n + a], local=res[2 * n + a], src=res[3 * n + a], land=res[4 * n + a],
                    src_at=items[a][2], dst_at=items[a][3]) for a in range(n)]
    return handles, res[-1]


def _wait_bytes(piece, sem):
    pltpu.make_async_copy(piece, piece, sem).wait()


def _exchange_wait(name, handles, after):
    n = len(handles)

    def body(*refs):
        srcs, lands = refs[:n], refs[n:2 * n]
        send_sems, recv_sems, local_sems = refs[2 * n:3 * n], refs[3 * n:4 * n], refs[4 * n:5 * n]
        me, peers = _peers()
        for a in range(n):
            src_at, dst_at = handles[a]["src_at"], handles[a]["dst_at"]
            for k, (pos, peer) in enumerate(peers):
                _wait_bytes(src_at(srcs[a], peer), send_sems[a].at[k])
                _wait_bytes(dst_at(lands[a], peer), recv_sems[a].at[k])
            pltpu.make_async_copy(src_at(srcs[a], me), dst_at(lands[a], me), local_sems[a]).wait()

    srcs = [hd["src"] for hd in handles]
    lands = [hd["land"] for hd in handles]
    res = pl.pallas_call(
        body, name=name,
        out_shape=[pltpu.HBM(a.shape, a.dtype) for a in srcs] + [pltpu.HBM(a.shape, a.dtype) for a in lands],
        in_specs=[_HBM] * (2 * n) + [_SEM] * (3 * n) + [pl.BlockSpec(memory_space=pl.ANY)],
        out_specs=[_HBM] * (2 * n),
        input_output_aliases={i: i for i in range(2 * n)},
        compiler_params=pltpu.CompilerParams(has_side_effects=pltpu.SideEffectType.DATAFLOW_SIDE_EFFECTING),
    )(*srcs, *lands, *[hd["send"] for hd in handles], *[hd["recv"] for hd in handles],
      *[hd["local"] for hd in handles], after)
    return list(res[n:])


_SIDE = pltpu.SideEffectType.DATAFLOW_SIDE_EFFECTING
_WCOLS = NIN // NDEV


def _win_cols(ref, l):
    return ref.at[:, pl.ds(pl.multiple_of(l * _WCOLS, 128), _WCOLS)]


def _win_routes():
    x, y, c = lax.axis_index("x"), lax.axis_index("y"), lax.axis_index("c")
    pos = [(x, y, 1 - c), (1 - x, y, c), (x, 1 - y, c), (1 - x, 1 - y, c)]
    return 4 * x + 2 * y + c, [(p, 4 * p[0] + 2 * p[1] + p[2]) for p in pos]


def _win_gather_start(shard):
    def body(src, land, send_sem, recv_sem, local_sem, src_thru, land_thru, token):
        me, routes = _win_routes()
        pltpu.make_async_copy(src, _win_cols(land, me), local_sem).start()
        pltpu.make_async_remote_copy(src_ref=src, dst_ref=_win_cols(land, me), send_sem=send_sem, recv_sem=recv_sem,
                                     device_id=routes[0][0], device_id_type=pl.DeviceIdType.MESH).start()
        token[...] = jnp.zeros_like(token)

    src = pltpu.with_memory_space_constraint(shard, pltpu.HBM)
    land = pltpu.with_memory_space_constraint(lax.empty((D, NIN), BF16), pltpu.HBM)
    sem = pltpu.SemaphoreType.DMA(())
    res = pl.pallas_call(
        body, name="win_gather_start",
        out_shape=[sem, sem, sem, pltpu.HBM(src.shape, BF16), pltpu.HBM(land.shape, BF16),
                   jax.ShapeDtypeStruct((8, 128), F32)],
        in_specs=[_HBM, _HBM],
        out_specs=[_SEM, _SEM, _SEM, _HBM, _HBM, pl.BlockSpec(memory_space=pltpu.VMEM)],
        input_output_aliases={0: 3, 1: 4},
        compiler_params=pltpu.CompilerParams(has_side_effects=_SIDE),
    )(src, land)
    return dict(send0=res[0], recv0=res[1], local=res[2], src=res[3], land=res[4]), res[5]


def _win_gather_links(hd, after):
    def body(src, land, after_ref, send_sems, recv_sems, src_thru, land_thru, token):
        me, routes = _win_routes()
        for k in (1, 2, 3):
            pltpu.make_async_remote_copy(src_ref=src, dst_ref=_win_cols(land, me), send_sem=send_sems.at[k - 1],
                                         recv_sem=recv_sems.at[k - 1], device_id=routes[k][0],
                                         device_id_type=pl.DeviceIdType.MESH).start()
        token[...] = jnp.zeros_like(token)

    sem3 = pltpu.SemaphoreType.DMA((3,))
    res = pl.pallas_call(
        body, name="win_gather_links",
        out_shape=[sem3, sem3, pltpu.HBM(hd["src"].shape, BF16), pltpu.HBM(hd["land"].shape, BF16),
                   jax.ShapeDtypeStruct((8, 128), F32)],
        in_specs=[_HBM, _HBM, pl.BlockSpec(memory_space=pl.ANY)],
        out_specs=[_SEM, _SEM, _HBM, _HBM, pl.BlockSpec(memory_space=pltpu.VMEM)],
        input_output_aliases={0: 2, 1: 3},
        compiler_params=pltpu.CompilerParams(has_side_effects=_SIDE),
    )(hd["src"], hd["land"], after)
    return dict(hd, send=res[0], recv=res[1], src=res[2], land=res[3]), res[4]


def _win_gather_forward(hd, after):
    def body(land, recv_sems, after_ref, land_thru, fsend_sems, frecv_sems):
        me, routes = _win_routes()
        sibling = routes[0][0]
        for k in (1, 2, 3):
            pos, peer = routes[k]
            piece = _win_cols(land, peer)
            pltpu.make_async_remote_copy(src_ref=piece, dst_ref=piece, send_sem=fsend_sems.at[k - 1],
                                         recv_sem=recv_sems.at[k - 1], device_id=pos,
                                         device_id_type=pl.DeviceIdType.MESH).wait_recv()
            pltpu.make_async_remote_copy(src_ref=piece, dst_ref=piece, send_sem=fsend_sems.at[k - 1],
                                         recv_sem=frecv_sems.at[k - 1], device_id=sibling,
                                         device_id_type=pl.DeviceIdType.MESH).start()

    sem3 = pltpu.SemaphoreType.DMA((3,))
    res = pl.pallas_call(
        body, name="win_gather_forward",
        out_shape=[pltpu.HBM(hd["land"].shape, BF16), sem3, sem3],
        in_specs=[_HBM, _SEM, pl.BlockSpec(memory_space=pl.ANY)],
        out_specs=[_HBM, _SEM, _SEM],
        input_output_aliases={0: 0},
        compiler_params=pltpu.CompilerParams(has_side_effects=_SIDE),
    )(hd["land"], hd["recv"], after)
    return dict(hd, land=res[0], fsend=res[1], frecv=res[2])


def _win_gather_early(hd):
    def body(src, land, recv_sem, local_sem, src_thru, land_thru):
        me, routes = _win_routes()
        _wait_bytes(_win_cols(land, routes[0][1]), recv_sem)
        pltpu.make_async_copy(src, _win_cols(land, me), local_sem).wait()

    res = pl.pallas_call(
        body, name="win_gather_early",
        out_shape=[pltpu.HBM(hd["src"].shape, BF16), pltpu.HBM(hd["land"].shape, BF16)],
        in_specs=[_HBM, _HBM, _SEM, _SEM],
        out_specs=[_HBM, _HBM],
        input_output_aliases={0: 0, 1: 1},
        compiler_params=pltpu.CompilerParams(has_side_effects=_SIDE),
    )(hd["src"], hd["land"], hd["recv0"], hd["local"])
    return dict(hd, src=res[0], land=res[1])


def _win_gather_wait(hd):
    def body(src, land, send0_sem, send_sems, fsend_sems, frecv_sems, src_thru, land_thru):
        me, routes = _win_routes()
        sib_pos, sibling = routes[0]
        for k in range(4):
            _wait_bytes(src, send0_sem if k == 0 else send_sems.at[k - 1])
        for k in (1, 2, 3):
            _wait_bytes(_win_cols(land, routes[k][1]), fsend_sems.at[k - 1])
            _wait_bytes(_win_cols(land, 4 * routes[k][0][0] + 2 * routes[k][0][1] + sib_pos[2]), frecv_sems.at[k - 1])

    res = pl.pallas_call(
        body, name="win_gather_wait",
        out_shape=[pltpu.HBM(hd["src"].shape, BF16), pltpu.HBM(hd["land"].shape, BF16)],
        in_specs=[_HBM, _HBM] + [_SEM] * 4,
        out_specs=[_HBM, _HBM],
        input_output_aliases={0: 0, 1: 1},
        compiler_params=pltpu.CompilerParams(has_side_effects=_SIDE),
    )(hd["src"], hd["land"], hd["send0"], hd["send"], hd["fsend"], hd["frecv"])
    return res[1]


def _whole(ref, l):
    return ref


def _slot(ref, l):
    return ref.at[l]


def _cols(width):
    def at(ref, l):
        return ref.at[:, pl.ds(pl.multiple_of(l * width, 128), width)]
    return at


def _rows(height):
    def at(ref, l):
        return ref.at[pl.ds(pl.multiple_of(l * height, 8), height), :]
    return at


NTILE = TP // TM


def _tile_rows(t):
    lo = max(t * TM - NMETA, 0)
    hi = min((t + 1) * TM - NMETA, SEQ)
    return lo, hi - lo, lo + NMETA - t * TM


def _for_tile(t, fn):
    for static_t in range(NTILE):
        pl.when(t == static_t)(functools.partial(fn, static_t))


def _token_tile_copy(hbm_ref, buf, sem, t):
    lo, n, off = _tile_rows(t)
    return pltpu.make_async_copy(hbm_ref.at[pl.ds(lo, n)], buf.at[pl.ds(off, n)], sem)


def _prenorm(x, meta_full, pre_w):
    def body(x_ref, meta_ref, pw_ref, h_ref, hn_ref, xbuf, sems):
        i = pl.program_id(0)
        slot = i % 2

        def start(t):
            _token_tile_copy(x_ref, xbuf.at[t % 2], sems.at[t % 2], t).start()

        @pl.when(i == 0)
        def _():
            start(0)
        _for_tile(i + 1, start)
        _for_tile(i, lambda t: _token_tile_copy(x_ref, xbuf.at[t % 2], sems.at[t % 2], t).wait())

        @pl.when(i == 0)
        def _():
            xbuf[0, 0:NMETA, :] = meta_ref[...]

        @pl.when(i == NTILE - 1)
        def _():
            last = _tile_rows(NTILE - 1)[1]
            xbuf[(NTILE - 1) % 2, last:TM, :] = jnp.zeros((TM - last, D), F32)

        pw = pw_ref[...]

        def chunk(ci, carry):
            r0 = pl.multiple_of(ci * R, R)
            xv = xbuf[slot, pl.ds(r0, R), :]
            h_ref[pl.ds(r0, R), :] = xv
            ms = jnp.mean(xv * xv, axis=-1, keepdims=True)
            hn_ref[pl.ds(r0, R), :] = (xv * lax.rsqrt(ms + EPS) * pw).astype(BF16)
            return carry
        lax.fori_loop(0, TM // R, chunk, 0, unroll=2)

    row = pl.BlockSpec((TM, D), lambda i: (i, 0))
    return pl.pallas_call(
        body, name="prenorm",
        grid=(NTILE,),
        in_specs=[pl.BlockSpec(memory_space=pl.ANY), pl.BlockSpec((NMETA, D), lambda i: (0, 0)),
                  pl.BlockSpec((1, D), lambda i: (0, 0))],
        out_specs=[row, row],
        out_shape=[jax.ShapeDtypeStruct((TP, D), F32), jax.ShapeDtypeStruct((TP, D), BF16)],
        scratch_shapes=[pltpu.VMEM((2, TM, D), F32), pltpu.SemaphoreType.DMA((2,))],
        compiler_params=_cparams(),
    )(x, meta_full, pre_w)


def _inproj_cols(name, shards, hn, w_land, b_in, z_prev):
    nsh = shards.shape[0]

    def body(idx_ref, hn_ref, w_ref, b_ref, *rest):
        z_ref = rest[-2]
        z_ref[...] = jnp.dot(hn_ref[...], w_ref[...], preferred_element_type=F32) + b_ref[...]

    any_spec = pl.BlockSpec(memory_space=pl.ANY)
    in_specs = [pl.BlockSpec((TM, D), lambda j, i, idx: (i, 0)),
                pl.BlockSpec((D, _WCOLS), lambda j, i, idx: (0, idx[j])),
                pl.BlockSpec((1, _WCOLS), lambda j, i, idx: (0, idx[j]))]
    operands = [hn, w_land, b_in]
    aliases = {2: 1}
    if z_prev is not None:
        in_specs.append(any_spec)
        operands.append(z_prev)
        aliases[4] = 0
    return pl.pallas_call(
        body, name=name,
        grid_spec=pltpu.PrefetchScalarGridSpec(
            num_scalar_prefetch=1, grid=(nsh, TP // TM), in_specs=in_specs,
            out_specs=[pl.BlockSpec((TM, _WCOLS), lambda j, i, idx: (i, idx[j])), any_spec]),
        out_shape=[jax.ShapeDtypeStruct((TP, NIN), F32), jax.ShapeDtypeStruct(w_land.shape, w_land.dtype)],
        input_output_aliases=aliases,
        compiler_params=_cparams(),
    )(shards, *operands)


def _gate_values(ga, gx, xc, sp8):
    r = _sig(ga)
    i = _sig(gx)
    log_a = -(r * sp8)
    a = jnp.exp(log_a)
    mult = jnp.sqrt(-_expm1_neg(2.0 * log_a))
    return r, i, a, mult


def _lru_fwd(z, conv_w, conv_b, wa_g, b_a, wx_g, b_x, lam):
    def body(x_ref, g_ref, cw_ref, cb_ref, wa_ref, ba_ref, wx_ref, bx_ref, lam_ref,
             y_ref, xc_ref, hs_ref, ga_s, gx_s):
        taps = [cw_ref[k:k + 1, :] for k in range(LW)]
        cb = cb_ref[...]

        def conv_chunk(ci, carry):
            r0 = pl.multiple_of(ci * R, R)
            cur = x_ref[pl.ds(r0, R), :]
            p0 = pl.multiple_of(jnp.maximum(r0 - 8, 0), 8)
            prev = jnp.where(ci > 0, x_ref[pl.ds(p0, 8), :], 0.0)
            buf = jnp.concatenate([prev, cur], axis=0)
            acc = cur * taps[LW - 1] + cb
            for s in range(1, LW):
                acc = acc + pltpu.roll(buf, s, 0)[8:8 + R, :] * taps[LW - 1 - s]
            xc_ref[pl.ds(r0, R), :] = acc
            return carry
        lax.fori_loop(0, TP // R, conv_chunk, 0)

        def gate_chunk(ci, carry):
            r0 = pl.multiple_of(ci * TM, TM)
            xb = xc_ref[pl.ds(r0, TM), :].astype(BF16)
            ga_s[pl.ds(r0, TM), :] = jnp.dot(xb, wa_ref[...], preferred_element_type=F32) + ba_ref[...]
            gx_s[pl.ds(r0, TM), :] = jnp.dot(xb, wx_ref[...], preferred_element_type=F32) + bx_ref[...]
            return carry
        lax.fori_loop(0, TP // TM, gate_chunk, 0)

        sp8 = LRU_C * _softplus(-lam_ref[...])
        row = _row_iota((R, CB))

        def scan_chunk(ci, hprev):
            r0 = pl.multiple_of(ci * R, R)
            xc = xc_ref[pl.ds(r0, R), :]
            _, i, a, mult = _gate_values(ga_s[pl.ds(r0, R), :], gx_s[pl.ds(r0, R), :], xc, sp8)
            u = mult * (i * xc)
            k = 1
            while k < R:
                m = row >= k
                u = jnp.where(m, a * pltpu.roll(u, k, 0) + u, u)
                a = jnp.where(m, a * pltpu.roll(a, k, 0), a)
                k *= 2
            hv = u + a * hprev
            hs_ref[pl.ds(r0, R), :] = hv
            g = g_ref[pl.ds(r0, R), :]
            y_ref[pl.ds(r0, R), :] = (hv * (g * _sig(g))).astype(BF16)
            return jnp.sum(jnp.where(row == R - 1, hv, 0.0), axis=0, keepdims=True)
        lax.fori_loop(0, TP // R, scan_chunk, jnp.zeros((1, CB), F32))

    col = lambda off: pl.BlockSpec((TP, CB), lambda j: (0, off + j))
    vec = pl.BlockSpec((1, CB), lambda j: (0, j))
    wsp = pl.BlockSpec((None, CB, CB), lambda j: (j, 0, 0))
    return pl.pallas_call(
        body, name="lru_fwd",
        grid=(NCB,),
        in_specs=[col(0), col(NCB), pl.BlockSpec((LW, CB), lambda j: (0, j)), vec, wsp, vec, wsp, vec, vec],
        out_specs=[col(0), col(0), col(0)],
        out_shape=[jax.ShapeDtypeStruct((TP, DL), BF16), jax.ShapeDtypeStruct((TP, DL), F32),
                   jax.ShapeDtypeStruct((TP, DL), F32)],
        scratch_shapes=[pltpu.VMEM((TP, CB), F32), pltpu.VMEM((TP, CB), F32)],
        compiler_params=_cparams(),
    )(z, z, conv_w, conv_b, wa_g, b_a, wx_g, b_x, lam)


CBC = 128
NCBC = DC // CBC
RC = 64


def _fold_rows(v):
    acc = v[0:8, :]
    for r in range(8, v.shape[0], 8):
        acc = acc + v[r:r + 8, :]
    return acc


def _conf_fwd_conv(z, dw_w, dw_b):
    def body(u1_ref, u2_ref, w_ref, b_ref, vc_ref, vs):
        vs[pl.ds(0, KWP), :] = jnp.zeros((KWP, CBC), F32)

        def glu_chunk(ci, carry):
            r0 = pl.multiple_of(ci * RC, RC)
            vs[pl.ds(KWP + r0, RC), :] = u1_ref[pl.ds(r0, RC), :] * _sig(u2_ref[pl.ds(r0, RC), :])
            return carry
        lax.fori_loop(0, TP // RC, glu_chunk, 0)

        bias = b_ref[...]

        def conv_chunk(ci, carry):
            r0 = pl.multiple_of(ci * RC, RC)
            buf = vs[pl.ds(r0, KWP + RC), :]
            acc = jnp.zeros((RC, CBC), F32) + bias
            for rr in range(8):
                rolled = buf if rr == 0 else pltpu.roll(buf, rr, 0)
                for q in range(4):
                    s = 8 * q + rr
                    if s > KW - 1:
                        continue
                    k = KW - 1 - s
                    acc = acc + rolled[KWP - 8 * q:KWP - 8 * q + RC, :] * w_ref[k:k + 1, :]
            vc_ref[pl.ds(r0, RC), :] = acc
            return carry
        lax.fori_loop(0, TP // RC, conv_chunk, 0)

    return pl.pallas_call(
        body, name="conf_fwd_conv",
        grid=(NCBC,),
        in_specs=[pl.BlockSpec((TP, CBC), lambda j: (0, 2 * NCBC + j)),
                  pl.BlockSpec((TP, CBC), lambda j: (0, 3 * NCBC + j)),
                  pl.BlockSpec((KWP, CBC), lambda j: (0, j)),
                  pl.BlockSpec((1, CBC), lambda j: (0, j))],
        out_specs=pl.BlockSpec((TP, CBC), lambda j: (0, j)),
        out_shape=jax.ShapeDtypeStruct((TP, DC), F32),
        scratch_shapes=[pltpu.VMEM((TP + KWP, CBC), F32)],
        compiler_params=_cparams(),
    )(z, z, dw_w, dw_b)


def _ln_chunk(vc, lw, lb):
    mu = jnp.mean(vc, axis=-1, keepdims=True)
    xm = vc - mu
    var = jnp.mean(xm * xm, axis=-1, keepdims=True)
    rstd = lax.rsqrt(var + EPS)
    xhat = xm * rstd
    return xhat, rstd, xhat * lw + lb


def _conf_fwd_proj(vc, z, ln_w, ln_b, pw_w, pw_b):
    def body(vc_ref, g_ref, lw_ref, lb_ref, w_ref, b_ref, y_ref, p_ref, s_s):
        lw, lb = lw_ref[...], lb_ref[...]

        def ln_chunk(ci, carry):
            r0 = pl.multiple_of(ci * R, R)
            for half in range(2):
                rr = r0 + 8 * half
                _, _, ln = _ln_chunk(vc_ref[pl.ds(rr, 8), :], lw, lb)
                p_ref[pl.ds(rr, 8), :] = ln * _sig(ln)
            s_s[pl.ds(r0, R), :] = p_ref[pl.ds(r0, R), :].astype(BF16)
            return carry
        lax.fori_loop(0, TM // R, ln_chunk, 0, unroll=2)

        p_ref[...] = jnp.dot(s_s[...], w_ref[...], preferred_element_type=F32) + b_ref[...]

        def out_chunk(ci, carry):
            r0 = pl.multiple_of(ci * R, R)
            g = g_ref[pl.ds(r0, R), :]
            y_ref[pl.ds(r0, R), :] = (p_ref[pl.ds(r0, R), :] * (g * _sig(g))).astype(BF16)
            return carry
        lax.fori_loop(0, TM // R, out_chunk, 0)

    row = pl.BlockSpec((TM, DC), lambda i: (i, 0))
    vec = pl.BlockSpec((1, DC), lambda i: (0, 0))
    return pl.pallas_call(
        body, name="conf_fwd_proj",
        grid=(TP // TM,),
        in_specs=[row, pl.BlockSpec((TM, DC), lambda i: (i, 4)), vec, vec,
                  pl.BlockSpec((DC, DC), lambda i: (0, 0)), vec],
        out_specs=[row, row],
        out_shape=[jax.ShapeDtypeStruct((TP, DC), BF16), jax.ShapeDtypeStruct((TP, DC), F32)],
        scratch_shapes=[pltpu.VMEM((TM, DC), BF16)],
        compiler_params=_cparams(),
    )(vc, z, ln_w, ln_b, pw_w, pw_b)


def _outproj_loss(ylru, yconf, w_out, h, target, post_w):
    def body(yl_ref, yc_ref, w_ref, h_ref, tgt_hbm, pw_ref, dout_ref, dy_ref, loss_ref, dpw_ref, y_s, t_ref, sem):
        i = pl.program_id(0)
        k = pl.program_id(1)

        @pl.when(k == 0)
        def _():
            _for_tile(i, lambda t: _token_tile_copy(tgt_hbm, t_ref, sem, t).start())
            y_s[...] = jnp.dot(yl_ref[...], w_ref[...], preferred_element_type=F32)

        @pl.when(k == 1)
        def _():
            y_s[...] += jnp.dot(yc_ref[...], w_ref[...], preferred_element_type=F32)

        @pl.when(jnp.logical_and(i == 0, k == 1))
        def _():
            loss_ref[...] = jnp.zeros_like(loss_ref)
            dpw_ref[...] = jnp.zeros_like(dpw_ref)

        @pl.when(k == 1)
        def _():
            _for_tile(i, lambda t: _token_tile_copy(tgt_hbm, t_ref, sem, t).wait())

            @pl.when(i == 0)
            def _():
                t_ref[0:NMETA, :] = jnp.zeros((NMETA, D), F32)

            @pl.when(i == NTILE - 1)
            def _():
                last = _tile_rows(NTILE - 1)[1]
                t_ref[last:TM, :] = jnp.zeros((TM - last, D), F32)

            pw = pw_ref[...]
            row = _row_iota((8, D))

            def chunk(ci, carry):
                r0 = pl.multiple_of(ci * 8, 8)
                yv = y_s[pl.ds(r0, 8), :]
                rs = lax.rsqrt(jnp.mean(yv * yv, axis=-1, keepdims=True) + EPS)
                grow = row + (i * TM + r0)
                valid = jnp.logical_and(grow >= NMETA, grow < T)
                yn = yv * rs
                err = jnp.where(valid, h_ref[pl.ds(r0, 8), :] + yn * pw - t_ref[pl.ds(r0, 8), :], 0.0)
                loss_ref[...] += err * err
                d_rn = err * (1.0 / D)
                dout_ref[pl.ds(r0, 8), :] = d_rn
                dpw_ref[...] += d_rn * yn
                gw = d_rn * pw
                dot = jnp.mean(gw * yv, axis=-1, keepdims=True)
                dy_ref[pl.ds(r0, 8), :] = (rs * gw - yv * (rs * rs * rs * dot)).astype(BF16)
                return carry
            lax.fori_loop(0, TM // 8, chunk, 0, unroll=4)

    row = pl.BlockSpec((TM, D), lambda i, k: (i, 0))
    half = pl.BlockSpec((TM, DL), lambda i, k: (i, 0))
    acc = pl.BlockSpec((8, D), lambda i, k: (0, 0))
    return pl.pallas_call(
        body, name="outproj_loss",
        grid=(TP // TM, 2),
        in_specs=[half, half, pl.BlockSpec((DL, D), lambda i, k: (k, 0)), row, pl.BlockSpec(memory_space=pl.ANY),
                  pl.BlockSpec((1, D), lambda i, k: (0, 0))],
        out_specs=[row, row, acc, acc],
        out_shape=[jax.ShapeDtypeStruct((TP, D), F32), jax.ShapeDtypeStruct((TP, D), BF16),
                   jax.ShapeDtypeStruct((8, D), F32), jax.ShapeDtypeStruct((8, D), F32)],
        scratch_shapes=[pltpu.VMEM((TM, D), F32), pltpu.VMEM((TM, D), F32), pltpu.SemaphoreType.DMA(())],
        compiler_params=_cparams(),
    )(ylru, yconf, w_out, h, target, post_w)


_NT = (((1,), (1,)), ((), ()))
_TN = (((0,), (0,)), ((), ()))


def _outproj_bwd(dy, ylru, yconf, w_out):
    def body(dy_ref, yl_ref, yc_ref, w_ref, dycat_ref, dw_ref):
        j = pl.program_id(0)
        dyv = dy_ref[...]
        dycat_ref[...] = lax.dot_general(dyv, w_ref[...], _NT, preferred_element_type=F32)

        @pl.when(j < NCB)
        def _():
            dw_ref[...] = lax.dot_general(yl_ref[...], dyv, _TN, preferred_element_type=F32).astype(BF16)

        @pl.when(j >= NCB)
        def _():
            dw_ref[...] = lax.dot_general(yc_ref[...], dyv, _TN, preferred_element_type=F32).astype(BF16)

    return pl.pallas_call(
        body, name="outproj_bwd",
        grid=(2 * NCB,),
        in_specs=[pl.BlockSpec((TP, D), lambda j: (0, 0)),
                  pl.BlockSpec((TP, CB), lambda j: (0, jnp.minimum(j, NCB - 1))),
                  pl.BlockSpec((TP, CB), lambda j: (0, jnp.maximum(j - NCB, 0))),
                  pl.BlockSpec((CB, D), lambda j: (j, 0))],
        out_specs=[pl.BlockSpec((TP, CB), lambda j: (0, j)), pl.BlockSpec((CB, D), lambda j: (j, 0))],
        out_shape=[jax.ShapeDtypeStruct((TP, D), F32), jax.ShapeDtypeStruct((D, D), BF16)],
        compiler_params=_cparams(),
    )(dy, ylru, yconf, w_out)


_AFTER = pl.BlockSpec(memory_space=pl.ANY)


def _conf_bwd_proj(dycat, p, z, vc, ln_w, ln_b, pw_w, after):
    def body(dy_ref, p_ref, g_ref, vc_ref, lw_ref, lb_ref, w_ref, after_ref,
             dvc_ref, dgc_ref, dpw_ref, vecs_ref, dp_s, s_s, ds_s):
        i = pl.program_id(0)
        lw, lb = lw_ref[...], lb_ref[...]

        @pl.when(i == 0)
        def _():
            dpw_ref[...] = jnp.zeros_like(dpw_ref)
            vecs_ref[...] = jnp.zeros_like(vecs_ref)

        def pre_chunk(ci, carry):
            r0 = pl.multiple_of(ci * R, R)
            for half in range(2):
                rr = r0 + 8 * half
                dyv = dy_ref[pl.ds(rr, 8), :]
                g = g_ref[pl.ds(rr, 8), :]
                sg = _sig(g)
                dp = dyv * (g * sg)
                dg = dyv * p_ref[pl.ds(rr, 8), :] * (sg * (1.0 + g * (1.0 - sg)))
                vecs_ref[0:8, :] += dp
                vecs_ref[8:16, :] += dg
                ds_s[pl.ds(rr, 8), :] = dp
                dvc_ref[pl.ds(rr, 8), :] = dg
            dp_s[pl.ds(r0, R), :] = ds_s[pl.ds(r0, R), :].astype(BF16)
            dgc_ref[pl.ds(r0, R), :] = dvc_ref[pl.ds(r0, R), :].astype(BF16)
            for half in range(2):
                rr = r0 + 8 * half
                _, _, ln = _ln_chunk(vc_ref[pl.ds(rr, 8), :], lw, lb)
                ds_s[pl.ds(rr, 8), :] = ln * _sig(ln)
            s_s[pl.ds(r0, R), :] = ds_s[pl.ds(r0, R), :].astype(BF16)
            return carry
        lax.fori_loop(0, TM // R, pre_chunk, 0, unroll=2)

        dpb = dp_s[...]
        ds_s[...] = lax.dot_general(dpb, w_ref[...], _NT, preferred_element_type=F32)
        dpw_ref[...] += lax.dot_general(s_s[...], dpb, _TN, preferred_element_type=F32)

        def post_chunk(ci, carry):
            r0 = pl.multiple_of(ci * 8, 8)
            xhat, rstd, ln = _ln_chunk(vc_ref[pl.ds(r0, 8), :], lw, lb)
            sl = _sig(ln)
            dln = ds_s[pl.ds(r0, 8), :] * (sl * (1.0 + ln * (1.0 - sl)))
            vecs_ref[16:24, :] += dln * xhat
            vecs_ref[24:32, :] += dln
            dxh = dln * lw
            m1 = jnp.mean(dxh, axis=-1, keepdims=True)
            m2 = jnp.mean(dxh * xhat, axis=-1, keepdims=True)
            dvc_ref[pl.ds(r0, 8), :] = rstd * (dxh - m1 - xhat * m2)
            return carry
        lax.fori_loop(0, TM // 8, post_chunk, 0, unroll=4)

    row = pl.BlockSpec((TM, DC), lambda i: (i, 0))
    vec = pl.BlockSpec((1, DC), lambda i: (0, 0))
    return pl.pallas_call(
        body, name="conf_bwd_proj",
        grid=(TP // TM,),
        in_specs=[pl.BlockSpec((TM, DC), lambda i: (i, 1)), row, pl.BlockSpec((TM, DC), lambda i: (i, 4)), row,
                  vec, vec, pl.BlockSpec((DC, DC), lambda i: (0, 0)), _AFTER],
        out_specs=[row, row, pl.BlockSpec((DC, DC), lambda i: (0, 0)), pl.BlockSpec((32, DC), lambda i: (0, 0))],
        out_shape=[jax.ShapeDtypeStruct((TP, DC), F32), jax.ShapeDtypeStruct((TP, DC), BF16),
                   jax.ShapeDtypeStruct((DC, DC), F32), jax.ShapeDtypeStruct((32, DC), F32)],
        scratch_shapes=[pltpu.VMEM((TM, DC), BF16), pltpu.VMEM((TM, DC), BF16), pltpu.VMEM((TM, DC), F32)],
        compiler_params=_cparams(),
    )(dycat, p, z, vc, ln_w, ln_b, pw_w, after)


def _conf_bwd_conv(dvc, z, dw_w, after):
    def body(dvc_ref, u1_ref, u2_ref, w_ref, after_ref, du_ref, dw_ref, vecs_ref, vs, dvs):
        vs[pl.ds(0, KWP), :] = jnp.zeros((KWP, CBC), F32)
        dvs[pl.ds(TP, KWP), :] = jnp.zeros((KWP, CBC), F32)
        dw_ref[...] = jnp.zeros_like(dw_ref)
        vecs_ref[...] = jnp.zeros_like(vecs_ref)

        def fill_chunk(ci, carry):
            r0 = pl.multiple_of(ci * RC, RC)
            vs[pl.ds(KWP + r0, RC), :] = u1_ref[pl.ds(r0, RC), :] * _sig(u2_ref[pl.ds(r0, RC), :])
            dv = dvc_ref[pl.ds(r0, RC), :]
            dvs[pl.ds(r0, RC), :] = dv
            vecs_ref[0:8, :] += _fold_rows(dv)
            return carry
        lax.fori_loop(0, TP // RC, fill_chunk, 0)

        def conv_chunk(ci, carry):
            r0 = pl.multiple_of(ci * RC, RC)
            vbuf = vs[pl.ds(r0, KWP + RC), :]
            dbuf = dvs[pl.ds(r0, KWP + RC), :]
            dcur = dbuf[0:RC, :]
            dv = jnp.zeros((RC, CBC), F32)
            for rr in range(8):
                vroll = vbuf if rr == 0 else pltpu.roll(vbuf, rr, 0)
                droll = dbuf if rr == 0 else pltpu.roll(dbuf, KWP + RC - rr, 0)
                for q in range(4):
                    s = 8 * q + rr
                    if s > KW - 1:
                        continue
                    k = KW - 1 - s
                    dv = dv + droll[8 * q:8 * q + RC, :] * w_ref[k:k + 1, :]
                    dw_ref[8 * k:8 * k + 8, :] += _fold_rows(dcur * vroll[KWP - 8 * q:KWP - 8 * q + RC, :])
            u1 = u1_ref[pl.ds(r0, RC), :]
            sg = _sig(u2_ref[pl.ds(r0, RC), :])
            du1 = dv * sg
            du2 = dv * u1 * (sg * (1.0 - sg))
            du_ref[0, pl.ds(r0, RC), :] = du1.astype(BF16)
            du_ref[1, pl.ds(r0, RC), :] = du2.astype(BF16)
            vecs_ref[8:16, :] += _fold_rows(du1)
            vecs_ref[16:24, :] += _fold_rows(du2)
            return carry
        lax.fori_loop(0, TP // RC, conv_chunk, 0)

    blk = pl.BlockSpec((TP, CBC), lambda j: (0, j))
    return pl.pallas_call(
        body, name="conf_bwd_conv",
        grid=(NCBC,),
        in_specs=[blk, pl.BlockSpec((TP, CBC), lambda j: (0, 2 * NCBC + j)),
                  pl.BlockSpec((TP, CBC), lambda j: (0, 3 * NCBC + j)), pl.BlockSpec((KWP, CBC), lambda j: (0, j)),
                  _AFTER],
        out_specs=[pl.BlockSpec((2, TP, CBC), lambda j: (0, 0, j)), pl.BlockSpec((8 * KWP, CBC), lambda j: (0, j)),
                   pl.BlockSpec((24, CBC), lambda j: (0, j))],
        out_shape=[jax.ShapeDtypeStruct((2, TP, DC), BF16),
                   jax.ShapeDtypeStruct((8 * KWP, DC), F32), jax.ShapeDtypeStruct((24, DC), F32)],
        scratch_shapes=[pltpu.VMEM((TP + KWP, CBC), F32), pltpu.VMEM((TP + KWP, CBC), F32)],
        compiler_params=_cparams(),
    )(dvc, z, z, dw_w, after)


def _lru_bwd(dycat, z, xc, hs, conv_w, wa_g, b_a, wx_g, b_x, lam, after):
    NV = 6

    def body(dy_ref, x_ref, g_ref, xc_ref, hs_ref, cw_ref, wa_ref, ba_ref, wx_ref, bx_ref, lam_ref, after_ref,
             dzl_ref, dwa_ref, dwx_ref, dcw_ref, vecs_ref, ga_s, gx_s, dxc_s):
        vecs_ref[...] = jnp.zeros_like(vecs_ref)
        dcw_ref[...] = jnp.zeros_like(dcw_ref)
        dxc_s[pl.ds(TP, 8), :] = jnp.zeros((8, CB), F32)

        def gate_chunk(ci, carry):
            r0 = pl.multiple_of(ci * TM, TM)
            xb = xc_ref[pl.ds(r0, TM), :].astype(BF16)
            ga_s[pl.ds(r0, TM), :] = jnp.dot(xb, wa_ref[...], preferred_element_type=F32) + ba_ref[...]
            gx_s[pl.ds(r0, TM), :] = jnp.dot(xb, wx_ref[...], preferred_element_type=F32) + bx_ref[...]
            return carry
        lax.fori_loop(0, TP // TM, gate_chunk, 0)

        sp8 = LRU_C * _softplus(-lam_ref[...])
        row = _row_iota((R, CB))
        nchunk = TP // R

        def scan_chunk(cj, carry):
            a_next, lam_next = carry
            ci = nchunk - 1 - cj
            r0 = pl.multiple_of(ci * R, R)
            dyv = dy_ref[pl.ds(r0, R), :]
            g = g_ref[pl.ds(r0, R), :]
            hv = hs_ref[pl.ds(r0, R), :]
            xc = xc_ref[pl.ds(r0, R), :]
            sg = _sig(g)
            dgl = dyv * hv * (sg * (1.0 + g * (1.0 - sg)))
            dzl_ref[1, pl.ds(r0, R), :] = dgl.astype(BF16)
            vecs_ref[0:8, :] += _fold8(dgl)
            dhs = dyv * (g * sg)
            r, i, a, mult = _gate_values(ga_s[pl.ds(r0, R), :], gx_s[pl.ds(r0, R), :], xc, sp8)
            b = jnp.where(row == R - 1, a_next, pltpu.roll(a, R - 1, 0))
            lv = dhs
            k = 1
            while k < R:
                m = row < R - k
                lv = jnp.where(m, lv + b * pltpu.roll(lv, R - k, 0), lv)
                b = jnp.where(m, b * pltpu.roll(b, R - k, 0), b)
                k *= 2
            lv = lv + b * lam_next
            p0 = pl.multiple_of(jnp.maximum(r0 - 8, 0), 8)
            hprev8 = jnp.where(ci > 0, hs_ref[pl.ds(p0, 8), :], 0.0)
            hprev = pltpu.roll(jnp.concatenate([hprev8, hv], axis=0), 1, 0)[8:8 + R, :]
            da = lv * hprev
            ixc = i * xc
            dmult = lv * ixc
            di = lv * mult * xc
            dxc_s[pl.ds(r0, R), :] = lv * mult * i
            a2 = a * a
            dlog_a = da * a - dmult * a2 / mult
            vecs_ref[32:40, :] += _fold8(dlog_a * r)
            dga = -(dlog_a * sp8) * r * (1.0 - r)
            dgx = di * i * (1.0 - i)
            ga_s[pl.ds(r0, R), :] = dga
            gx_s[pl.ds(r0, R), :] = dgx
            vecs_ref[16:24, :] += _fold8(dga)
            vecs_ref[24:32, :] += _fold8(dgx)
            a_first = jnp.sum(jnp.where(row == 0, a, 0.0), axis=0, keepdims=True)
            l_first = jnp.sum(jnp.where(row == 0, lv, 0.0), axis=0, keepdims=True)
            return a_first, l_first
        lax.fori_loop(0, nchunk, scan_chunk, (jnp.zeros((1, CB), F32), jnp.zeros((1, CB), F32)))

        dwa_ref[...] = jnp.zeros_like(dwa_ref)
        dwx_ref[...] = jnp.zeros_like(dwx_ref)

        def mm_chunk(ci, carry):
            r0 = pl.multiple_of(ci * TM, TM)
            xb = xc_ref[pl.ds(r0, TM), :].astype(BF16)
            dgab = ga_s[pl.ds(r0, TM), :].astype(BF16)
            dgxb = gx_s[pl.ds(r0, TM), :].astype(BF16)
            dxc_s[pl.ds(r0, TM), :] += (lax.dot_general(dgab, wa_ref[...], _NT, preferred_element_type=F32)
                                        + lax.dot_general(dgxb, wx_ref[...], _NT, preferred_element_type=F32))
            dwa_ref[...] += lax.dot_general(xb, dgab, _TN, preferred_element_type=F32)
            dwx_ref[...] += lax.dot_general(xb, dgxb, _TN, preferred_element_type=F32)
            return carry
        lax.fori_loop(0, TP // TM, mm_chunk, 0)

        taps = [cw_ref[k:k + 1, :] for k in range(LW)]

        def conv_chunk(ci, carry):
            r0 = pl.multiple_of(ci * R, R)
            dbuf = dxc_s[pl.ds(r0, R + 8), :]
            dcur = dbuf[0:R, :]
            p0 = pl.multiple_of(jnp.maximum(r0 - 8, 0), 8)
            xprev = jnp.where(ci > 0, x_ref[pl.ds(p0, 8), :], 0.0)
            xbuf = jnp.concatenate([xprev, x_ref[pl.ds(r0, R), :]], axis=0)
            dxl = dcur * taps[LW - 1]
            dcw_ref[8 * (LW - 1):8 * LW, :] += _fold8(dcur * xbuf[8:8 + R, :])
            for s in range(1, LW):
                k = LW - 1 - s
                dxl = dxl + pltpu.roll(dbuf, R + 8 - s, 0)[0:R, :] * taps[k]
                dcw_ref[8 * k:8 * k + 8, :] += _fold8(dcur * pltpu.roll(xbuf, s, 0)[8:8 + R, :])
            dzl_ref[0, pl.ds(r0, R), :] = dxl.astype(BF16)
            vecs_ref[8:16, :] += _fold8(dxl)
            vecs_ref[40:48, :] += _fold8(dcur)
            return carry
        lax.fori_loop(0, TP // R, conv_chunk, 0)
        vecs_ref[32:40, :] = vecs_ref[32:40, :] * (LRU_C * _sig(-lam_ref[...]))

    col = lambda off: pl.BlockSpec((TP, CB), lambda j: (0, off + j))
    vec = pl.BlockSpec((1, CB), lambda j: (0, j))
    wsp = pl.BlockSpec((None, CB, CB), lambda j: (j, 0, 0))
    return pl.pallas_call(
        body, name="lru_bwd",
        grid=(NCB,),
        in_specs=[col(0), col(0), col(NCB), col(0), col(0), pl.BlockSpec((LW, CB), lambda j: (0, j)),
                  wsp, vec, wsp, vec, vec, _AFTER],
        out_specs=[pl.BlockSpec((2, TP, CB), lambda j: (0, 0, j)), wsp, wsp,
                   pl.BlockSpec((8 * LW, CB), lambda j: (0, j)), pl.BlockSpec((8 * NV, CB), lambda j: (0, j))],
        out_shape=[jax.ShapeDtypeStruct((2, TP, DL), BF16),
                   jax.ShapeDtypeStruct((NCB, CB, CB), F32), jax.ShapeDtypeStruct((NCB, CB, CB), F32),
                   jax.ShapeDtypeStruct((8 * LW, DL), F32), jax.ShapeDtypeStruct((8 * NV, DL), F32)],
        scratch_shapes=[pltpu.VMEM((TP, CB), F32), pltpu.VMEM((TP, CB), F32), pltpu.VMEM((TP + 8, CB), F32)],
        compiler_params=_cparams(),
    )(dycat, z, z, xc, hs, conv_w, wa_g, b_a, wx_g, b_x, lam, after)


def _dz_section(sec, dzl_ref, dzc_ref, dgc_ref, use):
    @pl.when(sec < 2)
    def _():
        use(dzl_ref)

    @pl.when(jnp.logical_and(sec >= 2, sec < 4))
    def _():
        use(dzc_ref)

    @pl.when(sec == 4)
    def _():
        use(dgc_ref)


def _dz_specs(rows, index):
    return [pl.BlockSpec((None, rows, 1024), lambda a, b: (jnp.minimum(index(a, b)[1], 1), index(a, b)[0], 0)),
            pl.BlockSpec((None, rows, 1024), lambda a, b: (jnp.clip(index(a, b)[1] - 2, 0, 1), index(a, b)[0], 0)),
            pl.BlockSpec((rows, 1024), lambda a, b: (index(a, b)[0], 0))]


def _inproj_wgrad(name, hn, dzs, after):
    KB = 512
    nsec = dzs.shape[0]

    def body(hn_ref, dz_ref, after_ref, dw_ref):
        dw_ref[...] = lax.dot_general(hn_ref[...], dz_ref[...], _TN, preferred_element_type=F32).astype(BF16)

    return pl.pallas_call(
        body, name=name,
        grid=(nsec, D // KB),
        in_specs=[pl.BlockSpec((TP, KB), lambda n, kb: (0, kb)),
                  pl.BlockSpec((None, TP, 1024), lambda n, kb: (n, 0, 0)), _AFTER],
        out_specs=pl.BlockSpec((KB, 1024), lambda n, kb: (kb, n)),
        out_shape=jax.ShapeDtypeStruct((D, nsec * 1024), BF16),
        compiler_params=_cparams(),
    )(hn, dzs, after)


def _sum_win_parts(parts_a, parts_b, parts_c):
    RB = 64

    def body(a_ref, b_ref, c_ref, o_ref):
        def chunk(ci, carry):
            r0 = pl.multiple_of(ci * R, R)
            for ref, base, ncol in ((a_ref, 0, 2048), (b_ref, 2048, 2048), (c_ref, 4096, 1024)):
                for c0 in range(0, ncol, 512):
                    acc = ref[0, pl.ds(r0, R), c0:c0 + 512].astype(F32)
                    for sidx in range(1, NDEV):
                        acc = acc + ref[sidx, pl.ds(r0, R), c0:c0 + 512].astype(F32)
                    o_ref[pl.ds(r0, R), base + c0:base + c0 + 512] = acc.astype(BF16)
            return carry
        lax.fori_loop(0, RB // R, chunk, 0)

    spec = lambda ncol: pl.BlockSpec((NDEV, RB, ncol), lambda i: (0, i, 0))
    return pl.pallas_call(
        body, name="sum_win_parts",
        grid=(D // NDEV // RB,),
        in_specs=[spec(2048), spec(2048), spec(1024)],
        out_specs=pl.BlockSpec((RB, NIN), lambda i: (i, 0)),
        out_shape=jax.ShapeDtypeStruct((D // NDEV, NIN), BF16),
        compiler_params=_cparams(),
    )(parts_a, parts_b, parts_c)


def _inproj_bwd(dzl, dzc, dgc, w_in, h, dout, pre_w, after):
    nsec = NIN // 1024

    def body(dzl_ref, dzc_ref, dgc_ref, w_ref, h_ref, dout_ref, pw_ref, after_ref, gx_hbm, dmeta_ref, dpw_ref,
             acc_s, dh_s, sem):
        i = pl.program_id(0)
        s = pl.program_id(1)

        def gx_copy(t):
            lo, n, off = _tile_rows(t)
            return pltpu.make_async_copy(dh_s.at[pl.ds(off, n)], gx_hbm.at[pl.ds(lo, n)], sem)

        @pl.when(s == 0)
        def _():
            acc_s[...] = jnp.zeros_like(acc_s)

        def use(dz_ref):
            acc_s[...] += lax.dot_general(dz_ref[...], w_ref[...], _NT, preferred_element_type=F32)
        _dz_section(s, dzl_ref, dzc_ref, dgc_ref, use)

        @pl.when(jnp.logical_and(i == 0, s == nsec - 1))
        def _():
            dpw_ref[...] = jnp.zeros_like(dpw_ref)

        @pl.when(s == nsec - 1)
        def _():
            _for_tile(i - 1, lambda t: gx_copy(t).wait())
            pw = pw_ref[...]

            def chunk(ci, carry):
                r0 = pl.multiple_of(ci * 8, 8)
                hv = h_ref[pl.ds(r0, 8), :]
                dhn = acc_s[pl.ds(r0, 8), :]
                rs = lax.rsqrt(jnp.mean(hv * hv, axis=-1, keepdims=True) + EPS)
                dpw_ref[...] += dhn * (hv * rs)
                gw = dhn * pw
                dot = jnp.mean(gw * hv, axis=-1, keepdims=True)
                dh_s[pl.ds(r0, 8), :] = rs * gw - hv * (rs * rs * rs * dot) + dout_ref[pl.ds(r0, 8), :]
                return carry
            lax.fori_loop(0, TM // 8, chunk, 0, unroll=4)
            _for_tile(i, lambda t: gx_copy(t).start())

            @pl.when(i == 0)
            def _():
                dmeta_ref[...] = dh_s[0:NMETA, :]

            @pl.when(i == NTILE - 1)
            def _():
                gx_copy(NTILE - 1).wait()

    row = pl.BlockSpec((TM, D), lambda i, s: (i, 0))
    return pl.pallas_call(
        body, name="inproj_bwd",
        grid=(TP // TM, nsec),
        in_specs=_dz_specs(TM, lambda i, s: (i, s)) + [
            pl.BlockSpec((D, 1024), lambda i, s: (0, s)), row, row, pl.BlockSpec((1, D), lambda i, s: (0, 0)),
            _AFTER],
        out_specs=[pl.BlockSpec(memory_space=pl.ANY), pl.BlockSpec((NMETA, D), lambda i, s: (0, 0)),
                   pl.BlockSpec((8, D), lambda i, s: (0, 0))],
        out_shape=[jax.ShapeDtypeStruct((SEQ, D), F32), jax.ShapeDtypeStruct((NMETA, D), F32),
                   jax.ShapeDtypeStruct((8, D), F32)],
        scratch_shapes=[pltpu.VMEM((TM, D), F32), pltpu.VMEM((TM, D), F32), pltpu.SemaphoreType.DMA(())],
        compiler_params=_cparams(),
    )(dzl, dzc, dgc, w_in, h, dout, pre_w, after)


def _adamw(name, parts, w, m, v, block_rows):
    rows, cols = w.shape
    nparts = parts.shape[0]
    cw = cols if cols <= 640 else 512

    def body(p_ref, w_ref, m_ref, v_ref, g_ref, d_ref, nm_ref, nv_ref):
        def chunk(ci, carry):
            r0 = pl.multiple_of(ci * R, R)
            for c0 in range(0, cols, cw):
                at = (pl.ds(r0, R), slice(c0, c0 + cw))
                g = p_ref[(0,) + at].astype(F32)
                for sidx in range(1, nparts):
                    g = g + p_ref[(sidx,) + at].astype(F32)
                delta, mv, vv = _adam_math(g, w_ref[at], m_ref[at], v_ref[at])
                g_ref[at] = g
                nm_ref[at] = mv
                nv_ref[at] = vv
                d_ref[at] = delta
            return carry
        lax.fori_loop(0, block_rows // R, chunk, 0)

    blk = pl.BlockSpec((block_rows, cols), lambda i: (i, 0))
    shp = jax.ShapeDtypeStruct((rows, cols), F32)
    return pl.pallas_call(
        body, name=name,
        grid=(rows // block_rows,),
        in_specs=[pl.BlockSpec((nparts, block_rows, cols), lambda i: (0, i, 0)), blk, blk, blk],
        out_specs=[blk, blk, blk, blk],
        out_shape=[shp, shp, shp, shp],
        compiler_params=_cparams(),
    )(parts, w, m, v)


def _adam_math(g, w, m, v):
    c1 = 1.0 / (1.0 - ADAM_B1 ** ADAM_STEP)
    c2 = 1.0 / (1.0 - ADAM_B2 ** ADAM_STEP)
    mv = ADAM_B1 * m + (1.0 - ADAM_B1) * g
    vv = ADAM_B2 * v + (1.0 - ADAM_B2) * (g * g)
    upd = (mv * c1) / (jnp.sqrt(vv * c2) + ADAM_EPS) + ADAM_WD * w
    return -ADAM_LR * upd, mv, vv


_VEC = [("pre_norm_w", 2), ("post_norm_w", 2), ("b_in", 5), ("lru_conv_b", 1), ("b_gate_a", 1), ("b_gate_x", 1),
        ("lru_lambda", 1), ("conf_dw_b", 1), ("conf_ln_w", 1), ("conf_ln_b", 1), ("conf_pw_b", 1)]
_VEC_ROWS = 24
_LOSS_ROW = 17
_SM_ROWS = 64


def _pack_grads(dprew_acc, dpostw_acc, cvecs, kvecs, lvecs, dcw_acc, ddw_acc, dh, loss_acc):
    def body(pre_ref, post_ref, c_ref, k_ref, l_ref, dcw_ref, ddw_ref, dh_ref, loss_ref, vec_ref, small_ref, tmp):
        s8 = lambda ref, r: jnp.sum(ref[8 * r:8 * r + 8, :], axis=0, keepdims=True)
        vec_ref[...] = jnp.zeros_like(vec_ref)
        pre, post = s8(pre_ref, 0), s8(post_ref, 0)
        rows = [pre[:, 0:1024], pre[:, 1024:2048], post[:, 0:1024], post[:, 1024:2048],
                s8(l_ref, 1), s8(l_ref, 0), s8(k_ref, 1), s8(k_ref, 2), s8(c_ref, 1),
                s8(l_ref, 5), s8(l_ref, 2), s8(l_ref, 3), s8(l_ref, 4),
                s8(k_ref, 0), s8(c_ref, 2), s8(c_ref, 3), s8(c_ref, 0)]
        for r, val in enumerate(rows):
            vec_ref[r:r + 1, :] = val
        vec_ref[_LOSS_ROW:_LOSS_ROW + 1, :] = jnp.zeros((1, 1024), F32) + (0.5 / D) * jnp.sum(loss_ref[...])

        small_ref[...] = jnp.zeros_like(small_ref)
        for k in range(LW):
            tmp[k:k + 1, :] = s8(dcw_ref, k)
        for k in range(KW):
            tmp[8 + k:9 + k, :] = s8(ddw_ref, k)
        for d in range(NDEV):
            small_ref[d, 0:LW, 0:128] = tmp[0:LW, 128 * d:128 * d + 128]
            small_ref[d, 8:8 + KW, 0:128] = tmp[8:8 + KW, 128 * d:128 * d + 128]
            small_ref[d, 40:56, :] = dh_ref[:, 256 * d:256 * d + 256]

    full = lambda a: pl.BlockSpec(a.shape, lambda i: (0,) * a.ndim)
    ins = [dprew_acc, dpostw_acc, cvecs, kvecs, lvecs, dcw_acc, ddw_acc]
    return pl.pallas_call(
        body, name="pack_grads",
        grid=(1,),
        in_specs=[full(a) for a in ins] + [full(dh), full(loss_acc)],
        out_specs=[pl.BlockSpec((_VEC_ROWS, 1024), lambda i: (0, 0)),
                   pl.BlockSpec((NDEV, _SM_ROWS, 256), lambda i: (0, 0, 0))],
        out_shape=[jax.ShapeDtypeStruct((_VEC_ROWS, 1024), F32), jax.ShapeDtypeStruct((NDEV, _SM_ROWS, 256), F32)],
        scratch_shapes=[pltpu.VMEM((40, 1024), F32)],
        compiler_params=_cparams(),
    )(*ins, dh, loss_acc)


def _adamw_vec(parts, W, M, V):
    nv = len(_VEC)

    def body(*refs):
        p_ref = refs[0]
        w_refs, m_refs, v_refs = refs[1:1 + nv], refs[1 + nv:1 + 2 * nv], refs[1 + 2 * nv:1 + 3 * nv]
        outs = refs[1 + 3 * nv:]

        def total(r):
            acc = p_ref[0, r:r + 1, :]
            for sidx in range(1, NDEV):
                acc = acc + p_ref[sidx, r:r + 1, :]
            return acc

        row = 0
        for idx, (_, nrows) in enumerate(_VEC):
            for part in range(nrows):
                cols = slice(1024 * part, 1024 * part + 1024)
                g = total(row + part)
                delta, mv, vv = _adam_math(g, w_refs[idx][:, cols], m_refs[idx][:, cols], v_refs[idx][:, cols])
                for o, val in zip(outs[4 * idx:4 * idx + 4], (g, delta, mv, vv)):
                    o[:, cols] = val
            row += nrows
        outs[-1][...] = total(_LOSS_ROW)[:, 0:128]

    names = [n for n, _ in _VEC]
    flat = lambda d: [d[n].reshape(1, -1) for n in names]
    ws, ms, vs = flat(W), flat(M), flat(V)
    res = pl.pallas_call(
        body, name="adamw_vec",
        out_shape=[jax.ShapeDtypeStruct(w.shape, F32) for w in ws for _ in range(4)]
        + [jax.ShapeDtypeStruct((1, 128), F32)],
        compiler_params=_cparams(),
    )(parts, *ws, *ms, *vs)
    return {n: tuple(res[4 * i:4 * i + 4]) for i, n in enumerate(names)}, res[-1]


def _adamw_small(parts, W, M, V):
    where = {"lru_conv_w": (slice(0, LW), slice(0, 128)), "conf_dw_w": (slice(8, 8 + KW), slice(0, 128)),
             "meta_tokens": (slice(40, 56), slice(0, 256))}
    names = list(where)

    def body(*refs):
        p_ref = refs[0]
        outs = refs[10:]
        for idx, n in enumerate(names):
            rs, cs = where[n]
            g = p_ref[0, rs, cs]
            for sidx in range(1, NDEV):
                g = g + p_ref[sidx, rs, cs]
            delta, mv, vv = _adam_math(g, refs[1 + idx][...], refs[4 + idx][...], refs[7 + idx][...])
            for o, val in zip(outs[4 * idx:4 * idx + 4], (g, delta, mv, vv)):
                o[...] = val

    two_d = lambda a: a.reshape(a.shape[-2:])
    ws, ms, vs = ([two_d(d[n]) for n in names] for d in (W, M, V))
    res = pl.pallas_call(
        body, name="adamw_small",
        out_shape=[jax.ShapeDtypeStruct(w.shape, F32) for w in ws for _ in range(4)],
        compiler_params=_cparams(),
    )(parts, *ws, *ms, *vs)
    return {n: tuple(res[4 * i:4 * i + 4]) for i, n in enumerate(names)}


def _pack_small(lru_cw, dw_w, meta):
    buf = jnp.zeros((_SM_ROWS, 256), F32)
    buf = buf.at[0:LW, 0:128].set(lru_cw)
    buf = buf.at[8:8 + dw_w.shape[0], 0:128].set(dw_w)
    return buf.at[40:56, :].set(meta)


def _block_diag4(w):
    w4 = w.reshape(NCB, 4, 64, 64)
    eye = jnp.eye(4, dtype=w.dtype)
    return jnp.einsum("ghij,hk->ghikj", w4, eye).reshape(NCB, CB, CB)


def _diag_blocks(g):
    g5 = g.reshape(NCB, 4, 64, 4, 64)
    return jnp.stack([g5[:, hh, :, hh, :] for hh in range(4)], axis=1).reshape(16, 64, 64)


def _gate_mats(W):
    return _block_diag4(W["w_gate_a"][0]).astype(BF16), _block_diag4(W["w_gate_x"][0]).astype(BF16)


def _local_step(x, target, meta_full, inproj, out_weights, lru_cw_full, dw_w_full, W, gate_mats, send):
    wa_g, wx_g = gate_mats

    h, hn = _prenorm(x, meta_full, W["pre_norm_w"])
    z, win_full = inproj(hn)
    ylru, xc, hs = _lru_fwd(z, lru_cw_full, W["lru_conv_b"], wa_g, W["b_gate_a"], wx_g, W["b_gate_x"],
                            W["lru_lambda"])
    vc = _conf_fwd_conv(z, dw_w_full, W["conf_dw_b"])
    wout_full, pw_full = out_weights(vc)
    yconf, p = _conf_fwd_proj(vc, z, W["conf_ln_w"], W["conf_ln_b"], pw_full, W["conf_pw_b"])
    dout, dy, loss_acc, dpostw_acc = _outproj_loss(ylru, yconf, wout_full, h, target, W["post_norm_w"])

    dycat, dwout_part = _outproj_bwd(dy, ylru, yconf, wout_full)
    tok = send("w_out", ("w_out", dwout_part))
    dvc, dgc, dpw_part, cvecs = _conf_bwd_proj(dycat, p, z, vc, W["conf_ln_w"], W["conf_ln_b"], pw_full, tok)
    tok = send("w_in_c", ("conf_pw_w", dpw_part), ("w_in_c", _inproj_wgrad("inproj_wgrad_c", hn, dgc[None], dgc)))
    dzc, ddw_acc, kvecs = _conf_bwd_conv(dvc, z, dw_w_full, tok)
    tok = send("w_in_b", ("w_in_b", _inproj_wgrad("inproj_wgrad_b", hn, dzc, dzc)))
    dzl, dwa_g, dwx_g, dcw_acc, lvecs = _lru_bwd(dycat, z, xc, hs, lru_cw_full, wa_g, W["b_gate_a"], wx_g,
                                                 W["b_gate_x"], W["lru_lambda"], tok)
    tok = send("w_gates", ("w_gate_a", _diag_blocks(dwa_g).reshape(16 * 64, 64)),
               ("w_gate_x", _diag_blocks(dwx_g).reshape(16 * 64, 64)))
    tok = send("w_in_a", ("w_in_a", _inproj_wgrad("inproj_wgrad_a", hn, dzl, tok)))
    grad_x, dmeta, dprew_acc = _inproj_bwd(dzl, dzc, dgc, win_full, h, dout, W["pre_norm_w"], tok)

    vec_pack, small_part = _pack_grads(dprew_acc, dpostw_acc, cvecs, kvecs, lvecs, dcw_acc, ddw_acc, dmeta, loss_acc)
    return grad_x, vec_pack, small_part


def kernel(x, meta_tokens, pre_norm_w, post_norm_w, w_in, b_in, lru_conv_w, lru_conv_b, w_gate_a, b_gate_a, w_gate_x, b_gate_x, lru_lambda, conf_dw_w, conf_dw_b, conf_ln_w, conf_ln_b, conf_pw_w, conf_pw_b, w_out, loss_target, m_meta_tokens, m_pre_norm_w, m_post_norm_w, m_w_in, m_b_in, m_lru_conv_w, m_lru_conv_b, m_w_gate_a, m_b_gate_a, m_w_gate_x, m_b_gate_x, m_lru_lambda, m_conf_dw_w, m_conf_dw_b, m_conf_ln_w, m_conf_ln_b, m_conf_pw_w, m_conf_pw_b, m_w_out, v_meta_tokens, v_pre_norm_w, v_post_norm_w, v_w_in, v_b_in, v_lru_conv_w, v_lru_conv_b, v_w_gate_a, v_b_gate_a, v_w_gate_x, v_b_gate_x, v_lru_lambda, v_conf_dw_w, v_conf_dw_b, v_conf_ln_w, v_conf_ln_b, v_conf_pw_w, v_conf_pw_b, v_w_out):
    W = dict(meta_tokens=meta_tokens, pre_norm_w=pre_norm_w, post_norm_w=post_norm_w, w_in=w_in, b_in=b_in,
             lru_conv_w=lru_conv_w, lru_conv_b=lru_conv_b, w_gate_a=w_gate_a, b_gate_a=b_gate_a,
             w_gate_x=w_gate_x, b_gate_x=b_gate_x, lru_lambda=lru_lambda, conf_dw_w=conf_dw_w,
             conf_dw_b=conf_dw_b, conf_ln_w=conf_ln_w, conf_ln_b=conf_ln_b, conf_pw_w=conf_pw_w,
             conf_pw_b=conf_pw_b, w_out=w_out)
    M = dict(meta_tokens=m_meta_tokens, pre_norm_w=m_pre_norm_w, post_norm_w=m_post_norm_w, w_in=m_w_in,
             b_in=m_b_in, lru_conv_w=m_lru_conv_w, lru_conv_b=m_lru_conv_b, w_gate_a=m_w_gate_a,
             b_gate_a=m_b_gate_a, w_gate_x=m_w_gate_x, b_gate_x=m_b_gate_x, lru_lambda=m_lru_lambda,
             conf_dw_w=m_conf_dw_w, conf_dw_b=m_conf_dw_b, conf_ln_w=m_conf_ln_w, conf_ln_b=m_conf_ln_b,
             conf_pw_w=m_conf_pw_w, conf_pw_b=m_conf_pw_b, w_out=m_w_out)
    V = dict(meta_tokens=v_meta_tokens, pre_norm_w=v_pre_norm_w, post_norm_w=v_post_norm_w, w_in=v_w_in,
             b_in=v_b_in, lru_conv_w=v_lru_conv_w, lru_conv_b=v_lru_conv_b, w_gate_a=v_w_gate_a,
             b_gate_a=v_b_gate_a, w_gate_x=v_w_gate_x, b_gate_x=v_b_gate_x, lru_lambda=v_lru_lambda,
             conf_dw_w=v_conf_dw_w, conf_dw_b=v_conf_dw_b, conf_ln_w=v_conf_ln_w, conf_ln_b=v_conf_ln_b,
             conf_pw_w=v_conf_pw_w, conf_pw_b=v_conf_pw_b, w_out=v_w_out)
    names = list(W.keys())
    shapes = {n: W[n].shape for n in names}

    small = _pack_small(lru_conv_w[0], conf_dw_w[0], meta_tokens)
    (small_flight,), tok = _exchange_start("gather_small_start", [
        (small, jax.ShapeDtypeStruct((NDEV, _SM_ROWS, 256), F32), _whole, _slot)])
    win_flight, tok = _win_gather_start(w_in[0].astype(BF16) + tok[0, 0].astype(BF16))
    gate_mats = _gate_mats(W)
    wout_shard = w_out[0].astype(BF16) + tok[0, 0].astype(BF16)
    pw_shard = conf_pw_w[0].astype(BF16)
    cast_done = (gate_mats[0][0, 0:8, 0:128] + gate_mats[1][0, 0:8, 0:128]
                 + wout_shard[0:8, 0:128] + pw_shard[0:8, 0:128])
    win_flight, tok = _win_gather_links(win_flight, cast_done)
    gathered, tok = _exchange_start("gather_out_start", [
        (wout_shard + tok[0, 0].astype(BF16), jax.ShapeDtypeStruct((D, D), BF16), _whole, _rows(D // NDEV)),
        (pw_shard, jax.ShapeDtypeStruct((DC, DC), BF16), _whole, _rows(DC // NDEV)),
    ])
    (small_all,) = _exchange_wait("gather_small_wait", [small_flight], tok)
    unshard = lambda a: jnp.transpose(a, (1, 0, 2)).reshape(a.shape[1], -1)
    lru_cw_full = unshard(small_all[:, 0:LW, 0:128])
    dw_w_full = unshard(small_all[:, 8:8 + KWP, 0:128])
    meta_full = unshard(small_all[:, 40:56, :])

    def out_weights(after):
        return _exchange_wait("gather_out_wait", gathered, after)

    def inproj(hn):
        xi, yi, ci = lax.axis_index("x"), lax.axis_index("y"), lax.axis_index("c")
        shard = lambda px, py, pc: (4 * px + 2 * py + pc).astype(jnp.int32)
        here = jnp.stack([shard(xi, yi, ci), shard(xi, yi, 1 - ci)])
        over_links = jnp.stack([shard(1 - xi, yi, ci), shard(xi, 1 - yi, ci), shard(1 - xi, 1 - yi, ci)])
        flight = _win_gather_early(win_flight)
        z, land = _inproj_cols("inproj_here", here, hn, flight["land"], b_in, None)
        flight = _win_gather_forward(dict(flight, land=land), z)
        z, land = _inproj_cols("inproj_links", over_links, hn, flight["land"], b_in, z)
        land = _win_gather_wait(dict(flight, land=land))
        return _inproj_cols("inproj_sibling", over_links + 1 - 2 * ci, hn, land, b_in, z)

    row_stage = lambda ncol: (jax.ShapeDtypeStruct((NDEV, D // NDEV, ncol), BF16), _rows(D // NDEV))
    piece = {"w_in_a": row_stage(2048), "w_in_b": row_stage(2048), "w_in_c": row_stage(1024),
             "w_out": row_stage(D),
             "conf_pw_w": (jax.ShapeDtypeStruct((NDEV, DC // NDEV, DC), BF16), _rows(DC // NDEV)),
             "w_gate_a": (jax.ShapeDtypeStruct((NDEV, 16 * 64, 64), BF16), _whole),
             "w_gate_x": (jax.ShapeDtypeStruct((NDEV, 16 * 64, 64), BF16), _whole)}
    sent = {}

    def send(call, *named_parts):
        handles, token = _exchange_start(
            "scatter_" + call + "_start",
            [(part.astype(BF16), piece[name][0], piece[name][1], _slot) for name, part in named_parts])
        for (name, _), handle in zip(named_parts, handles):
            sent[name] = [handle]
        return token

    grad_x, vec_pack, small_part = _local_step(
        x[0], loss_target[0], meta_full, inproj, out_weights, lru_cw_full, dw_w_full, W, gate_mats, send)
    grad_x = grad_x[None]

    rest, tok = _exchange_start("scatter_rest_start", [
        (small_part, jax.ShapeDtypeStruct((NDEV, _SM_ROWS, 256), F32), _slot, _slot),
        (vec_pack, jax.ShapeDtypeStruct((NDEV, _VEC_ROWS, 1024), F32), _whole, _slot),
    ])
    (parts_c,) = _exchange_wait("scatter_w_in_c_wait", sent["w_in_c"], tok)
    (parts_b,) = _exchange_wait("scatter_w_in_b_wait", sent["w_in_b"], parts_c)
    (parts_a,) = _exchange_wait("scatter_w_in_a_wait", sent["w_in_a"], parts_b)
    win_rows = _sum_win_parts(parts_a, parts_b, parts_c)
    win_stage2, tok = _exchange_start("scatter_w_in_stage2_start", [
        (win_rows, jax.ShapeDtypeStruct((NDEV, D // NDEV, NIN // NDEV), BF16), _cols(NIN // NDEV), _slot)])

    G, DW, NM, NV = {}, {}, {}, {}
    (wout_parts,) = _exchange_wait("scatter_w_out_wait", sent["w_out"], tok)
    G["w_out"], DW["w_out"], NM["w_out"], NV["w_out"] = _adamw("adamw_w_out", wout_parts, w_out[0], m_w_out[0], v_w_out[0], 64)
    (pw_parts,) = _exchange_wait("scatter_conf_pw_w_wait", sent["conf_pw_w"], G["w_out"])
    G["conf_pw_w"], DW["conf_pw_w"], NM["conf_pw_w"], NV["conf_pw_w"] = _adamw(
        "adamw_pw", pw_parts, conf_pw_w[0], m_conf_pw_w[0], v_conf_pw_w[0], 128)
    res = {}
    wa_parts, wx_parts = _exchange_wait("scatter_w_gates_wait", sent["w_gate_a"] + sent["w_gate_x"], G["conf_pw_w"])
    for n, parts in (("w_gate_a", wa_parts), ("w_gate_x", wx_parts)):
        res[n] = _adamw("adamw_" + n, parts, *[d[n].reshape(16 * 64, 64) for d in (W, M, V)], 16 * 64)
    small_parts, vec_parts = _exchange_wait("scatter_rest_wait", rest, res["w_gate_x"][0])
    res.update(_adamw_small(small_parts, W, M, V))
    vec_res, loss_row = _adamw_vec(vec_parts, W, M, V)
    res.update(vec_res)
    (win_sum,) = _exchange_wait("scatter_w_in_stage2_wait", win_stage2, loss_row)
    res["w_in"] = _adamw("adamw_w_in", win_sum.reshape(1, D, NIN // NDEV), w_in[0], m_w_in[0], v_w_in[0], 256)
    for n, vals in res.items():
        for dst, val in zip((G, DW, NM, NV), vals):
            dst[n] = val
    for dst in (G, DW, NM, NV):
        for n in names:
            dst[n] = dst[n].reshape(shapes[n])
    loss = loss_row[0, 0]

    return (loss, grad_x, *[G[n] for n in names], *[DW[n] for n in names],
            *[NM[n] for n in names], *[NV[n] for n in names])
```

```python
import functools

import jax
import jax.numpy as jnp
from jax import lax
from jax.experimental import pallas as pl
from jax.experimental.pallas import tpu as pltpu

F32 = jnp.float32
BF16 = jnp.bfloat16

D = 2048
DL = 1024
DC = 1024
NIN = 5120
NMETA = 16
SEQ = 2048
T = NMETA + SEQ
TP = 2176
TM = 544
CB = 256
NCB = DL // CB
R = 16
KW = 31
KWP = 32
LW = 4
LRU_C = 8.0
EPS = 1e-6
NDEV = 8

ADAM_LR = 0.001
ADAM_B1 = 0.9
ADAM_B2 = 0.999
ADAM_EPS = 1e-08
ADAM_WD = 0.01
ADAM_STEP = 10

VMEM_LIMIT = 56 * 1024 * 1024


def _cparams():
    return pltpu.CompilerParams(vmem_limit_bytes=VMEM_LIMIT)


def _sig(x):
    return 1.0 / (1.0 + jnp.exp(-x))


def _expm1_neg(y):
    poly = y * (1.0 + y * (0.5 + y * (1.0 / 6.0 + y * (1.0 / 24.0 + y * (1.0 / 120.0)))))
    return jnp.where(y > -0.1, poly, jnp.exp(y) - 1.0)


def _softplus(x):
    e = jnp.exp(-jnp.abs(x))
    w = 1.0 + e
    l1p = jnp.where(w == 1.0, e, jnp.log(w) * e / (w - 1.0))
    return jnp.maximum(x, 0.0) + l1p


def _row_iota(shape):
    return lax.broadcasted_iota(jnp.int32, shape, 0)


def _fold8(v):
    return v[0:8, :] + v[8:16, :]


_FLIPS = [(k >> 2 & 1, k >> 1 & 1, k & 1) for k in range(1, NDEV)]
_HBM = pl.BlockSpec(memory_space=pltpu.HBM)
_SEM = pl.BlockSpec(memory_space=pltpu.SEMAPHORE)


def _peers():
    x, y, c = lax.axis_index("x"), lax.axis_index("y"), lax.axis_index("c")
    out = []
    for dx, dy, dc in _FLIPS:
        px = 1 - x if dx else x
        py = 1 - y if dy else y
        pc = 1 - c if dc else c
        out.append(((px, py, pc), 4 * px + 2 * py + pc))
    return 4 * x + 2 * y + c, out


def _exchange_start(name, items):
    n = len(items)

    def body(*refs):
        srcs, lands = refs[:n], refs[n:2 * n]
        outs = refs[2 * n:]
        send_sems, recv_sems, local_sems = outs[:n], outs[n:2 * n], outs[2 * n:3 * n]
        token = outs[-1]
        me, peers = _peers()
        for a in range(n):
            src_at, dst_at = items[a][2], items[a][3]
            pltpu.make_async_copy(src_at(srcs[a], me), dst_at(lands[a], me), local_sems[a]).start()
        for a in range(n):
            src_at, dst_at = items[a][2], items[a][3]
            for k, (pos, peer) in enumerate(peers):
                pltpu.make_async_remote_copy(
                    src_ref=src_at(srcs[a], peer), dst_ref=dst_at(lands[a], me),
                    send_sem=send_sems[a].at[k], recv_sem=recv_sems[a].at[k],
                    device_id=pos, device_id_type=pl.DeviceIdType.MESH).start()
        token[...] = jnp.zeros_like(token)

    srcs = [pltpu.with_memory_space_constraint(it[0], pltpu.HBM) for it in items]
    lands = [pltpu.with_memory_space_constraint(lax.empty(it[1].shape, it[1].dtype), pltpu.HBM) for it in items]
    sem7 = pltpu.SemaphoreType.DMA((NDEV - 1,))
    res = pl.pallas_call(
        body, name=name,
        out_shape=([sem7] * (2 * n) + [pltpu.SemaphoreType.DMA(())] * n
                   + [pltpu.HBM(a.shape, a.dtype) for a in srcs] + [pltpu.HBM(a.shape, a.dtype) for a in lands]
                   + [jax.ShapeDtypeStruct((8, 128), F32)]),
        in_specs=[_HBM] * (2 * n),
        out_specs=[_SEM] * (3 * n) + [_HBM] * (2 * n) + [pl.BlockSpec(memory_space=pltpu.VMEM)],
        input_output_aliases={i: 3 * n + i for i in range(2 * n)},
        compiler_params=pltpu.CompilerParams(has_side_effects=pltpu.SideEffectType.DATAFLOW_SIDE_EFFECTING),
    )(*srcs, *lands)
    handles = [dict(send=res[a], recv=res[n + a], local=res[2 * n + a], src=res[3 * n + a], land=res[4 * n + a],
                    src_at=items[a][2], dst_at=items[a][3]) for a in range(n)]
    return handles, res[-1]


def _wait_bytes(piece, sem):
    pltpu.make_async_copy(piece, piece, sem).wait()


def _exchange_wait(name, handles, after):
    n = len(handles)

    def body(*refs):
        srcs, lands = refs[:n], refs[n:2 * n]
        send_sems, recv_sems, local_sems = refs[2 * n:3 * n], refs[3 * n:4 * n], refs[4 * n:5 * n]
        me, peers = _peers()
        for a in range(n):
            src_at, dst_at = handles[a]["src_at"], handles[a]["dst_at"]
            for k, (pos, peer) in enumerate(peers):
                _wait_bytes(src_at(srcs[a], peer), send_sems[a].at[k])
                _wait_bytes(dst_at(lands[a], peer), recv_sems[a].at[k])
            pltpu.make_async_copy(src_at(srcs[a], me), dst_at(lands[a], me), local_sems[a]).wait()

    srcs = [hd["src"] for hd in handles]
    lands = [hd["land"] for hd in handles]
    res = pl.pallas_call(
        body, name=name,
        out_shape=[pltpu.HBM(a.shape, a.dtype) for a in srcs] + [pltpu.HBM(a.shape, a.dtype) for a in lands],
        in_specs=[_HBM] * (2 * n) + [_SEM] * (3 * n) + [pl.BlockSpec(memory_space=pl.ANY)],
        out_specs=[_HBM] * (2 * n),
        input_output_aliases={i: i for i in range(2 * n)},
        compiler_params=pltpu.CompilerParams(has_side_effects=pltpu.SideEffectType.DATAFLOW_SIDE_EFFECTING),
    )(*srcs, *lands, *[hd["send"] for hd in handles], *[hd["recv"] for hd in handles],
      *[hd["local"] for hd in handles], after)
    return list(res[n:])


_SIDE = pltpu.SideEffectType.DATAFLOW_SIDE_EFFECTING
_WCOLS = NIN // NDEV


def _win_cols(ref, l):
    return ref.at[:, pl.ds(pl.multiple_of(l * _WCOLS, 128), _WCOLS)]


def _win_routes():
    x, y, c = lax.axis_index("x"), lax.axis_index("y"), lax.axis_index("c")
    pos = [(x, y, 1 - c), (1 - x, y, c), (x, 1 - y, c), (1 - x, 1 - y, c)]
    return 4 * x + 2 * y + c, [(p, 4 * p[0] + 2 * p[1] + p[2]) for p in pos]


def _win_gather_start(shard):
    def body(src, land, send_sem, recv_sem, local_sem, src_thru, land_thru, token):
        me, routes = _win_routes()
        pltpu.make_async_copy(src, _win_cols(land, me), local_sem).start()
        pltpu.make_async_remote_copy(src_ref=src, dst_ref=_win_cols(land, me), send_sem=send_sem, recv_sem=recv_sem,
                                     device_id=routes[0][0], device_id_type=pl.DeviceIdType.MESH).start()
        token[...] = jnp.zeros_like(token)

    src = pltpu.with_memory_space_constraint(shard, pltpu.HBM)
    land = pltpu.with_memory_space_constraint(lax.empty((D, NIN), BF16), pltpu.HBM)
    sem = pltpu.SemaphoreType.DMA(())
    res = pl.pallas_call(
        body, name="win_gather_start",
        out_shape=[sem, sem, sem, pltpu.HBM(src.shape, BF16), pltpu.HBM(land.shape, BF16),
                   jax.ShapeDtypeStruct((8, 128), F32)],
        in_specs=[_HBM, _HBM],
        out_specs=[_SEM, _SEM, _SEM, _HBM, _HBM, pl.BlockSpec(memory_space=pltpu.VMEM)],
        input_output_aliases={0: 3, 1: 4},
        compiler_params=pltpu.CompilerParams(has_side_effects=_SIDE),
    )(src, land)
    return dict(send0=res[0], recv0=res[1], local=res[2], src=res[3], land=res[4]), res[5]


def _win_gather_links(hd, after):
    def body(src, land, after_ref, send_sems, recv_sems, src_thru, land_thru, token):
        me, routes = _win_routes()
        for k in (1, 2, 3):
            pltpu.make_async_remote_copy(src_ref=src, dst_ref=_win_cols(land, me), send_sem=send_sems.at[k - 1],
                                         recv_sem=recv_sems.at[k - 1], device_id=routes[k][0],
                                         device_id_type=pl.DeviceIdType.MESH).start()
        token[...] = jnp.zeros_like(token)

    sem3 = pltpu.SemaphoreType.DMA((3,))
    res = pl.pallas_call(
        body, name="win_gather_links",
        out_shape=[sem3, sem3, pltpu.HBM(hd["src"].shape, BF16), pltpu.HBM(hd["land"].shape, BF16),
                   jax.ShapeDtypeStruct((8, 128), F32)],
        in_specs=[_HBM, _HBM, pl.BlockSpec(memory_space=pl.ANY)],
        out_specs=[_SEM, _SEM, _HBM, _HBM, pl.BlockSpec(memory_space=pltpu.VMEM)],
        input_output_aliases={0: 2, 1: 3},
        compiler_params=pltpu.CompilerParams(has_side_effects=_SIDE),
    )(hd["src"], hd["land"], after)
    return dict(hd, send=res[0], recv=res[1], src=res[2], land=res[3]), res[4]


def _win_gather_forward(hd, after):
    def body(land, recv_sems, after_ref, land_thru, fsend_sems, frecv_sems):
        me, routes = _win_routes()
        sibling = routes[0][0]
        for k in (1, 2, 3):
            pos, peer = routes[k]
            piece = _win_cols(land, peer)
            pltpu.make_async_remote_copy(src_ref=piece, dst_ref=piece, send_sem=fsend_sems.at[k - 1],
                                         recv_sem=recv_sems.at[k - 1], device_id=pos,
                                         device_id_type=pl.DeviceIdType.MESH).wait_recv()
            pltpu.make_async_remote_copy(src_ref=piece, dst_ref=piece, send_sem=fsend_sems.at[k - 1],
                                         recv_sem=frecv_sems.at[k - 1], device_id=sibling,
                                         device_id_type=pl.DeviceIdType.MESH).start()

    sem3 = pltpu.SemaphoreType.DMA((3,))
    res = pl.pallas_call(
        body, name="win_gather_forward",
        out_shape=[pltpu.HBM(hd["land"].shape, BF16), sem3, sem3],
        in_specs=[_HBM, _SEM, pl.BlockSpec(memory_space=pl.ANY)],
        out_specs=[_HBM, _SEM, _SEM],
        input_output_aliases={0: 0},
        compiler_params=pltpu.CompilerParams(has_side_effects=_SIDE),
    )(hd["land"], hd["recv"], after)
    return dict(hd, land=res[0], fsend=res[1], frecv=res[2])


def _win_gather_early(hd):
    def body(src, land, recv_sem, local_sem, src_thru, land_thru):
        me, routes = _win_routes()
        _wait_bytes(_win_cols(land, routes[0][1]), recv_sem)
        pltpu.make_async_copy(src, _win_cols(land, me), local_sem).wait()

    res = pl.pallas_call(
        body, name="win_gather_early",
        out_shape=[pltpu.HBM(hd["src"].shape, BF16), pltpu.HBM(hd["land"].shape, BF16)],
        in_specs=[_HBM, _HBM, _SEM, _SEM],
        out_specs=[_HBM, _HBM],
        input_output_aliases={0: 0, 1: 1},
        compiler_params=pltpu.CompilerParams(has_side_effects=_SIDE),
    )(hd["src"], hd["land"], hd["recv0"], hd["local"])
    return dict(hd, src=res[0], land=res[1])


def _win_gather_wait(hd):
    def body(src, land, send0_sem, send_sems, fsend_sems, frecv_sems, src_thru, land_thru):
        me, routes = _win_routes()
        sib_pos, sibling = routes[0]
        for k in range(4):
            _wait_bytes(src, send0_sem if k == 0 else send_sems.at[k - 1])
        for k in (1, 2, 3):
            _wait_bytes(_win_cols(land, routes[k][1]), fsend_sems.at[k - 1])
            _wait_bytes(_win_cols(land, 4 * routes[k][0][0] + 2 * routes[k][0][1] + sib_pos[2]), frecv_sems.at[k - 1])

    res = pl.pallas_call(
        body, name="win_gather_wait",
        out_shape=[pltpu.HBM(hd["src"].shape, BF16), pltpu.HBM(hd["land"].shape, BF16)],
        in_specs=[_HBM, _HBM] + [_SEM] * 4,
        out_specs=[_HBM, _HBM],
        input_output_aliases={0: 0, 1: 1},
        compiler_params=pltpu.CompilerParams(has_side_effects=_SIDE),
    )(hd["src"], hd["land"], hd["send0"], hd["send"], hd["fsend"], hd["frecv"])
    return res[1]


def _whole(ref, l):
    return ref


def _slot(ref, l):
    return ref.at[l]


def _cols(width):
    def at(ref, l):
        return ref.at[:, pl.ds(pl.multiple_of(l * width, 128), width)]
    return at


def _rows(height):
    def at(ref, l):
        return ref.at[pl.ds(pl.multiple_of(l * height, 8), height), :]
    return at


NTILE = TP // TM


def _tile_rows(t):
    lo = max(t * TM - NMETA, 0)
    hi = min((t + 1) * TM - NMETA, SEQ)
    return lo, hi - lo, lo + NMETA - t * TM


def _for_tile(t, fn):
    for static_t in range(NTILE):
        pl.when(t == static_t)(functools.partial(fn, static_t))


def _token_tile_copy(hbm_ref, buf, sem, t):
    lo, n, off = _tile_rows(t)
    return pltpu.make_async_copy(hbm_ref.at[pl.ds(lo, n)], buf.at[pl.ds(off, n)], sem)


def _prenorm(x, meta_full, pre_w):
    def body(x_ref, meta_ref, pw_ref, h_ref, hn_ref, xbuf, sems):
        i = pl.program_id(0)
        slot = i % 2

        def start(t):
            _token_tile_copy(x_ref, xbuf.at[t % 2], sems.at[t % 2], t).start()

        @pl.when(i == 0)
        def _():
            start(0)
        _for_tile(i + 1, start)
        _for_tile(i, lambda t: _token_tile_copy(x_ref, xbuf.at[t % 2], sems.at[t % 2], t).wait())

        @pl.when(i == 0)
        def _():
            xbuf[0, 0:NMETA, :] = meta_ref[...]

        @pl.when(i == NTILE - 1)
        def _():
            last = _tile_rows(NTILE - 1)[1]
            xbuf[(NTILE - 1) % 2, last:TM, :] = jnp.zeros((TM - last, D), F32)

        pw = pw_ref[...]

        def chunk(ci, carry):
            r0 = pl.multiple_of(ci * R, R)
            xv = xbuf[slot, pl.ds(r0, R), :]
            h_ref[pl.ds(r0, R), :] = xv
            ms = jnp.mean(xv * xv, axis=-1, keepdims=True)
            hn_ref[pl.ds(r0, R), :] = (xv * lax.rsqrt(ms + EPS) * pw).astype(BF16)
            return carry
        lax.fori_loop(0, TM // R, chunk, 0, unroll=2)

    row = pl.BlockSpec((TM, D), lambda i: (i, 0))
    return pl.pallas_call(
        body, name="prenorm",
        grid=(NTILE,),
        in_specs=[pl.BlockSpec(memory_space=pl.ANY), pl.BlockSpec((NMETA, D), lambda i: (0, 0)),
                  pl.BlockSpec((1, D), lambda i: (0, 0))],
        out_specs=[row, row],
        out_shape=[jax.ShapeDtypeStruct((TP, D), F32), jax.ShapeDtypeStruct((TP, D), BF16)],
        scratch_shapes=[pltpu.VMEM((2, TM, D), F32), pltpu.SemaphoreType.DMA((2,))],
        compiler_params=_cparams(),
    )(x, meta_full, pre_w)


def _inproj_cols(name, shards, hn, w_land, b_in, z_prev):
    nsh = shards.shape[0]
    one_shard = w_land.shape[1] == _WCOLS

    def body(idx_ref, hn_ref, w_ref, b_ref, *rest):
        z_ref = rest[-2]
        z_ref[...] = jnp.dot(hn_ref[...], w_ref[...], preferred_element_type=F32) + b_ref[...]

    any_spec = pl.BlockSpec(memory_space=pl.ANY)
    in_specs = [pl.BlockSpec((TM, D), lambda j, i, idx: (i, 0)),
                pl.BlockSpec((D, _WCOLS), lambda j, i, idx: (0, 0 if one_shard else idx[j])),
                pl.BlockSpec((1, _WCOLS), lambda j, i, idx: (0, idx[j]))]
    operands = [hn, w_land, b_in]
    aliases = {2: 1}
    if z_prev is not None:
        in_specs.append(any_spec)
        operands.append(z_prev)
        aliases[4] = 0
    return pl.pallas_call(
        body, name=name,
        grid_spec=pltpu.PrefetchScalarGridSpec(
            num_scalar_prefetch=1, grid=(nsh, TP // TM), in_specs=in_specs,
            out_specs=[pl.BlockSpec((TM, _WCOLS), lambda j, i, idx: (i, idx[j])), any_spec]),
        out_shape=[jax.ShapeDtypeStruct((TP, NIN), F32), jax.ShapeDtypeStruct(w_land.shape, w_land.dtype)],
        input_output_aliases=aliases,
        compiler_params=_cparams(),
    )(shards, *operands)


def _gate_values(ga, gx, xc, sp8):
    r = _sig(ga)
    i = _sig(gx)
    log_a = -(r * sp8)
    a = jnp.exp(log_a)
    mult = jnp.sqrt(-_expm1_neg(2.0 * log_a))
    return r, i, a, mult


def _lru_fwd(z, conv_w, conv_b, wa_g, b_a, wx_g, b_x, lam):
    def body(x_ref, g_ref, cw_ref, cb_ref, wa_ref, ba_ref, wx_ref, bx_ref, lam_ref,
             y_ref, xc_ref, hs_ref, ga_s, gx_s):
        taps = [cw_ref[k:k + 1, :] for k in range(LW)]
        cb = cb_ref[...]

        def conv_chunk(ci, carry):
            r0 = pl.multiple_of(ci * R, R)
            cur = x_ref[pl.ds(r0, R), :]
            p0 = pl.multiple_of(jnp.maximum(r0 - 8, 0), 8)
            prev = jnp.where(ci > 0, x_ref[pl.ds(p0, 8), :], 0.0)
            buf = jnp.concatenate([prev, cur], axis=0)
            acc = cur * taps[LW - 1] + cb
            for s in range(1, LW):
                acc = acc + pltpu.roll(buf, s, 0)[8:8 + R, :] * taps[LW - 1 - s]
            xc_ref[pl.ds(r0, R), :] = acc
            return carry
        lax.fori_loop(0, TP // R, conv_chunk, 0)

        def gate_chunk(ci, carry):
            r0 = pl.multiple_of(ci * TM, TM)
            xb = xc_ref[pl.ds(r0, TM), :].astype(BF16)
            ga_s[pl.ds(r0, TM), :] = jnp.dot(xb, wa_ref[...], preferred_element_type=F32) + ba_ref[...]
            gx_s[pl.ds(r0, TM), :] = jnp.dot(xb, wx_ref[...], preferred_element_type=F32) + bx_ref[...]
            return carry
        lax.fori_loop(0, TP // TM, gate_chunk, 0)

        sp8 = LRU_C * _softplus(-lam_ref[...])
        row = _row_iota((R, CB))

        def scan_chunk(ci, hprev):
            r0 = pl.multiple_of(ci * R, R)
            xc = xc_ref[pl.ds(r0, R), :]
            _, i, a, mult = _gate_values(ga_s[pl.ds(r0, R), :], gx_s[pl.ds(r0, R), :], xc, sp8)
            u = mult * (i * xc)
            k = 1
            while k < R:
                m = row >= k
                u = jnp.where(m, a * pltpu.roll(u, k, 0) + u, u)
                a = jnp.where(m, a * pltpu.roll(a, k, 0), a)
                k *= 2
            hv = u + a * hprev
            hs_ref[pl.ds(r0, R), :] = hv
            g = g_ref[pl.ds(r0, R), :]
            y_ref[pl.ds(r0, R), :] = (hv * (g * _sig(g))).astype(BF16)
            return jnp.sum(jnp.where(row == R - 1, hv, 0.0), axis=0, keepdims=True)
        lax.fori_loop(0, TP // R, scan_chunk, jnp.zeros((1, CB), F32))

    col = lambda off: pl.BlockSpec((TP, CB), lambda j: (0, off + j))
    vec = pl.BlockSpec((1, CB), lambda j: (0, j))
    wsp = pl.BlockSpec((None, CB, CB), lambda j: (j, 0, 0))
    return pl.pallas_call(
        body, name="lru_fwd",
        grid=(NCB,),
        in_specs=[col(0), col(NCB), pl.BlockSpec((LW, CB), lambda j: (0, j)), vec, wsp, vec, wsp, vec, vec],
        out_specs=[col(0), col(0), col(0)],
        out_shape=[jax.ShapeDtypeStruct((TP, DL), BF16), jax.ShapeDtypeStruct((TP, DL), F32),
                   jax.ShapeDtypeStruct((TP, DL), F32)],
        scratch_shapes=[pltpu.VMEM((TP, CB), F32), pltpu.VMEM((TP, CB), F32)],
        compiler_params=_cparams(),
    )(z, z, conv_w, conv_b, wa_g, b_a, wx_g, b_x, lam)


CBC = 128
NCBC = DC // CBC
RC = 64


def _fold_rows(v):
    acc = v[0:8, :]
    for r in range(8, v.shape[0], 8):
        acc = acc + v[r:r + 8, :]
    return acc


def _conf_fwd_conv(z, dw_w, dw_b):
    def body(u1_ref, u2_ref, w_ref, b_ref, vc_ref, vs):
        vs[pl.ds(0, KWP), :] = jnp.zeros((KWP, CBC), F32)

        def glu_chunk(ci, carry):
            r0 = pl.multiple_of(ci * RC, RC)
            vs[pl.ds(KWP + r0, RC), :] = u1_ref[pl.ds(r0, RC), :] * _sig(u2_ref[pl.ds(r0, RC), :])
            return carry
        lax.fori_loop(0, TP // RC, glu_chunk, 0)

        bias = b_ref[...]

        def conv_chunk(ci, carry):
            r0 = pl.multiple_of(ci * RC, RC)
            buf = vs[pl.ds(r0, KWP + RC), :]
            acc = jnp.zeros((RC, CBC), F32) + bias
            for rr in range(8):
                rolled = buf if rr == 0 else pltpu.roll(buf, rr, 0)
                for q in range(4):
                    s = 8 * q + rr
                    if s > KW - 1:
                        continue
                    k = KW - 1 - s
                    acc = acc + rolled[KWP - 8 * q:KWP - 8 * q + RC, :] * w_ref[k:k + 1, :]
            vc_ref[pl.ds(r0, RC), :] = acc
            return carry
        lax.fori_loop(0, TP // RC, conv_chunk, 0)

    return pl.pallas_call(
        body, name="conf_fwd_conv",
        grid=(NCBC,),
        in_specs=[pl.BlockSpec((TP, CBC), lambda j: (0, 2 * NCBC + j)),
                  pl.BlockSpec((TP, CBC), lambda j: (0, 3 * NCBC + j)),
                  pl.BlockSpec((KWP, CBC), lambda j: (0, j)),
                  pl.BlockSpec((1, CBC), lambda j: (0, j))],
        out_specs=pl.BlockSpec((TP, CBC), lambda j: (0, j)),
        out_shape=jax.ShapeDtypeStruct((TP, DC), F32),
        scratch_shapes=[pltpu.VMEM((TP + KWP, CBC), F32)],
        compiler_params=_cparams(),
    )(z, z, dw_w, dw_b)


def _ln_chunk(vc, lw, lb):
    mu = jnp.mean(vc, axis=-1, keepdims=True)
    xm = vc - mu
    var = jnp.mean(xm * xm, axis=-1, keepdims=True)
    rstd = lax.rsqrt(var + EPS)
    xhat = xm * rstd
    return xhat, rstd, xhat * lw + lb


def _conf_fwd_proj(vc, z, ln_w, ln_b, pw_w, pw_b):
    def body(vc_ref, g_ref, lw_ref, lb_ref, w_ref, b_ref, y_ref, p_ref, s_s):
        lw, lb = lw_ref[...], lb_ref[...]

        def ln_chunk(ci, carry):
            r0 = pl.multiple_of(ci * R, R)
            for half in range(2):
                rr = r0 + 8 * half
                _, _, ln = _ln_chunk(vc_ref[pl.ds(rr, 8), :], lw, lb)
                p_ref[pl.ds(rr, 8), :] = ln * _sig(ln)
            s_s[pl.ds(r0, R), :] = p_ref[pl.ds(r0, R), :].astype(BF16)
            return carry
        lax.fori_loop(0, TM // R, ln_chunk, 0, unroll=2)

        p_ref[...] = jnp.dot(s_s[...], w_ref[...], preferred_element_type=F32) + b_ref[...]

        def out_chunk(ci, carry):
            r0 = pl.multiple_of(ci * R, R)
            g = g_ref[pl.ds(r0, R), :]
            y_ref[pl.ds(r0, R), :] = (p_ref[pl.ds(r0, R), :] * (g * _sig(g))).astype(BF16)
            return carry
        lax.fori_loop(0, TM // R, out_chunk, 0)

    row = pl.BlockSpec((TM, DC), lambda i: (i, 0))
    vec = pl.BlockSpec((1, DC), lambda i: (0, 0))
    return pl.pallas_call(
        body, name="conf_fwd_proj",
        grid=(TP // TM,),
        in_specs=[row, pl.BlockSpec((TM, DC), lambda i: (i, 4)), vec, vec,
                  pl.BlockSpec((DC, DC), lambda i: (0, 0)), vec],
        out_specs=[row, row],
        out_shape=[jax.ShapeDtypeStruct((TP, DC), BF16), jax.ShapeDtypeStruct((TP, DC), F32)],
        scratch_shapes=[pltpu.VMEM((TM, DC), BF16)],
        compiler_params=_cparams(),
    )(vc, z, ln_w, ln_b, pw_w, pw_b)


def _outproj_loss(ylru, yconf, w_out, h, target, post_w):
    def body(yl_ref, yc_ref, w_ref, h_ref, tgt_hbm, pw_ref, dout_ref, dy_ref, loss_ref, dpw_ref, y_s, t_ref, sem):
        i = pl.program_id(0)
        k = pl.program_id(1)

        @pl.when(k == 0)
        def _():
            _for_tile(i, lambda t: _token_tile_copy(tgt_hbm, t_ref, sem, t).start())
            y_s[...] = jnp.dot(yl_ref[...], w_ref[...], preferred_element_type=F32)

        @pl.when(k == 1)
        def _():
            y_s[...] += jnp.dot(yc_ref[...], w_ref[...], preferred_element_type=F32)

        @pl.when(jnp.logical_and(i == 0, k == 1))
        def _():
            loss_ref[...] = jnp.zeros_like(loss_ref)
            dpw_ref[...] = jnp.zeros_like(dpw_ref)

        @pl.when(k == 1)
        def _():
            _for_tile(i, lambda t: _token_tile_copy(tgt_hbm, t_ref, sem, t).wait())

            @pl.when(i == 0)
            def _():
                t_ref[0:NMETA, :] = jnp.zeros((NMETA, D), F32)

            @pl.when(i == NTILE - 1)
            def _():
                last = _tile_rows(NTILE - 1)[1]
                t_ref[last:TM, :] = jnp.zeros((TM - last, D), F32)

            pw = pw_ref[...]
            row = _row_iota((8, D))

            def chunk(ci, carry):
                r0 = pl.multiple_of(ci * 8, 8)
                yv = y_s[pl.ds(r0, 8), :]
                rs = lax.rsqrt(jnp.mean(yv * yv, axis=-1, keepdims=True) + EPS)
                grow = row + (i * TM + r0)
                valid = jnp.logical_and(grow >= NMETA, grow < T)
                yn = yv * rs
                err = jnp.where(valid, h_ref[pl.ds(r0, 8), :] + yn * pw - t_ref[pl.ds(r0, 8), :], 0.0)
                loss_ref[...] += err * err
                d_rn = err * (1.0 / D)
                dout_ref[pl.ds(r0, 8), :] = d_rn
                dpw_ref[...] += d_rn * yn
                gw = d_rn * pw
                dot = jnp.mean(gw * yv, axis=-1, keepdims=True)
                dy_ref[pl.ds(r0, 8), :] = (rs * gw - yv * (rs * rs * rs * dot)).astype(BF16)
                return carry
            lax.fori_loop(0, TM // 8, chunk, 0, unroll=4)

    row = pl.BlockSpec((TM, D), lambda i, k: (i, 0))
    half = pl.BlockSpec((TM, DL), lambda i, k: (i, 0))
    acc = pl.BlockSpec((8, D), lambda i, k: (0, 0))
    return pl.pallas_call(
        body, name="outproj_loss",
        grid=(TP // TM, 2),
        in_specs=[half, half, pl.BlockSpec((DL, D), lambda i, k: (k, 0)), row, pl.BlockSpec(memory_space=pl.ANY),
                  pl.BlockSpec((1, D), lambda i, k: (0, 0))],
        out_specs=[row, row, acc, acc],
        out_shape=[jax.ShapeDtypeStruct((TP, D), F32), jax.ShapeDtypeStruct((TP, D), BF16),
                   jax.ShapeDtypeStruct((8, D), F32), jax.ShapeDtypeStruct((8, D), F32)],
        scratch_shapes=[pltpu.VMEM((TM, D), F32), pltpu.VMEM((TM, D), F32), pltpu.SemaphoreType.DMA(())],
        compiler_params=_cparams(),
    )(ylru, yconf, w_out, h, target, post_w)


_NT = (((1,), (1,)), ((), ()))
_TN = (((0,), (0,)), ((), ()))


def _outproj_bwd(dy, ylru, yconf, w_out):
    def body(dy_ref, yl_ref, yc_ref, w_ref, dycat_ref, dw_ref):
        j = pl.program_id(0)
        dyv = dy_ref[...]
        dycat_ref[...] = lax.dot_general(dyv, w_ref[...], _NT, preferred_element_type=F32)

        @pl.when(j < NCB)
        def _():
            dw_ref[...] = lax.dot_general(yl_ref[...], dyv, _TN, preferred_element_type=F32).astype(BF16)

        @pl.when(j >= NCB)
        def _():
            dw_ref[...] = lax.dot_general(yc_ref[...], dyv, _TN, preferred_element_type=F32).astype(BF16)

    return pl.pallas_call(
        body, name="outproj_bwd",
        grid=(2 * NCB,),
        in_specs=[pl.BlockSpec((TP, D), lambda j: (0, 0)),
                  pl.BlockSpec((TP, CB), lambda j: (0, jnp.minimum(j, NCB - 1))),
                  pl.BlockSpec((TP, CB), lambda j: (0, jnp.maximum(j - NCB, 0))),
                  pl.BlockSpec((CB, D), lambda j: (j, 0))],
        out_specs=[pl.BlockSpec((TP, CB), lambda j: (0, j)), pl.BlockSpec((CB, D), lambda j: (j, 0))],
        out_shape=[jax.ShapeDtypeStruct((TP, D), F32), jax.ShapeDtypeStruct((D, D), BF16)],
        compiler_params=_cparams(),
    )(dy, ylru, yconf, w_out)


_AFTER = pl.BlockSpec(memory_space=pl.ANY)


def _conf_bwd_proj(dycat, p, z, vc, ln_w, ln_b, pw_w, after):
    def body(dy_ref, p_ref, g_ref, vc_ref, lw_ref, lb_ref, w_ref, after_ref,
             dvc_ref, dgc_ref, dpw_ref, vecs_ref, dp_s, s_s, ds_s):
        i = pl.program_id(0)
        lw, lb = lw_ref[...], lb_ref[...]

        @pl.when(i == 0)
        def _():
            dpw_ref[...] = jnp.zeros_like(dpw_ref)
            vecs_ref[...] = jnp.zeros_like(vecs_ref)

        def pre_chunk(ci, carry):
            r0 = pl.multiple_of(ci * R, R)
            for half in range(2):
                rr = r0 + 8 * half
                dyv = dy_ref[pl.ds(rr, 8), :]
                g = g_ref[pl.ds(rr, 8), :]
                sg = _sig(g)
                dp = dyv * (g * sg)
                dg = dyv * p_ref[pl.ds(rr, 8), :] * (sg * (1.0 + g * (1.0 - sg)))
                vecs_ref[0:8, :] += dp
                vecs_ref[8:16, :] += dg
                ds_s[pl.ds(rr, 8), :] = dp
                dvc_ref[pl.ds(rr, 8), :] = dg
            dp_s[pl.ds(r0, R), :] = ds_s[pl.ds(r0, R), :].astype(BF16)
            dgc_ref[pl.ds(r0, R), :] = dvc_ref[pl.ds(r0, R), :].astype(BF16)
            for half in range(2):
                rr = r0 + 8 * half
                _, _, ln = _ln_chunk(vc_ref[pl.ds(rr, 8), :], lw, lb)
                ds_s[pl.ds(rr, 8), :] = ln * _sig(ln)
            s_s[pl.ds(r0, R), :] = ds_s[pl.ds(r0, R), :].astype(BF16)
            return carry
        lax.fori_loop(0, TM // R, pre_chunk, 0, unroll=2)

        dpb = dp_s[...]
        ds_s[...] = lax.dot_general(dpb, w_ref[...], _NT, preferred_element_type=F32)
        dpw_ref[...] += lax.dot_general(s_s[...], dpb, _TN, preferred_element_type=F32)

        def post_chunk(ci, carry):
            r0 = pl.multiple_of(ci * 8, 8)
            xhat, rstd, ln = _ln_chunk(vc_ref[pl.ds(r0, 8), :], lw, lb)
            sl = _sig(ln)
            dln = ds_s[pl.ds(r0, 8), :] * (sl * (1.0 + ln * (1.0 - sl)))
            vecs_ref[16:24, :] += dln * xhat
            vecs_ref[24:32, :] += dln
            dxh = dln * lw
            m1 = jnp.mean(dxh, axis=-1, keepdims=True)
            m2 = jnp.mean(dxh * xhat, axis=-1, keepdims=True)
            dvc_ref[pl.ds(r0, 8), :] = rstd * (dxh - m1 - xhat * m2)
            return carry
        lax.fori_loop(0, TM // 8, post_chunk, 0, unroll=4)

    row = pl.BlockSpec((TM, DC), lambda i: (i, 0))
    vec = pl.BlockSpec((1, DC), lambda i: (0, 0))
    return pl.pallas_call(
        body, name="conf_bwd_proj",
        grid=(TP // TM,),
        in_specs=[pl.BlockSpec((TM, DC), lambda i: (i, 1)), row, pl.BlockSpec((TM, DC), lambda i: (i, 4)), row,
                  vec, vec, pl.BlockSpec((DC, DC), lambda i: (0, 0)), _AFTER],
        out_specs=[row, row, pl.BlockSpec((DC, DC), lambda i: (0, 0)), pl.BlockSpec((32, DC), lambda i: (0, 0))],
        out_shape=[jax.ShapeDtypeStruct((TP, DC), F32), jax.ShapeDtypeStruct((TP, DC), BF16),
                   jax.ShapeDtypeStruct((DC, DC), F32), jax.ShapeDtypeStruct((32, DC), F32)],
        scratch_shapes=[pltpu.VMEM((TM, DC), BF16), pltpu.VMEM((TM, DC), BF16), pltpu.VMEM((TM, DC), F32)],
        compiler_params=_cparams(),
    )(dycat, p, z, vc, ln_w, ln_b, pw_w, after)


def _conf_bwd_conv(dvc, z, dw_w, after):
    def body(dvc_ref, u1_ref, u2_ref, w_ref, after_ref, du_ref, dw_ref, vecs_ref, vs, dvs):
        vs[pl.ds(0, KWP), :] = jnp.zeros((KWP, CBC), F32)
        dvs[pl.ds(TP, KWP), :] = jnp.zeros((KWP, CBC), F32)
        dw_ref[...] = jnp.zeros_like(dw_ref)
        vecs_ref[...] = jnp.zeros_like(vecs_ref)

        def fill_chunk(ci, carry):
            r0 = pl.multiple_of(ci * RC, RC)
            vs[pl.ds(KWP + r0, RC), :] = u1_ref[pl.ds(r0, RC), :] * _sig(u2_ref[pl.ds(r0, RC), :])
            dv = dvc_ref[pl.ds(r0, RC), :]
            dvs[pl.ds(r0, RC), :] = dv
            vecs_ref[0:8, :] += _fold_rows(dv)
            return carry
        lax.fori_loop(0, TP // RC, fill_chunk, 0)

        def conv_chunk(ci, carry):
            r0 = pl.multiple_of(ci * RC, RC)
            vbuf = vs[pl.ds(r0, KWP + RC), :]
            dbuf = dvs[pl.ds(r0, KWP + RC), :]
            dcur = dbuf[0:RC, :]
            dv = jnp.zeros((RC, CBC), F32)
            for rr in range(8):
                vroll = vbuf if rr == 0 else pltpu.roll(vbuf, rr, 0)
                droll = dbuf if rr == 0 else pltpu.roll(dbuf, KWP + RC - rr, 0)
                for q in range(4):
                    s = 8 * q + rr
                    if s > KW - 1:
                        continue
                    k = KW - 1 - s
                    dv = dv + droll[8 * q:8 * q + RC, :] * w_ref[k:k + 1, :]
                    dw_ref[8 * k:8 * k + 8, :] += _fold_rows(dcur * vroll[KWP - 8 * q:KWP - 8 * q + RC, :])
            u1 = u1_ref[pl.ds(r0, RC), :]
            sg = _sig(u2_ref[pl.ds(r0, RC), :])
            du1 = dv * sg
            du2 = dv * u1 * (sg * (1.0 - sg))
            du_ref[0, pl.ds(r0, RC), :] = du1.astype(BF16)
            du_ref[1, pl.ds(r0, RC), :] = du2.astype(BF16)
            vecs_ref[8:16, :] += _fold_rows(du1)
            vecs_ref[16:24, :] += _fold_rows(du2)
            return carry
        lax.fori_loop(0, TP // RC, conv_chunk, 0)

    blk = pl.BlockSpec((TP, CBC), lambda j: (0, j))
    return pl.pallas_call(
        body, name="conf_bwd_conv",
        grid=(NCBC,),
        in_specs=[blk, pl.BlockSpec((TP, CBC), lambda j: (0, 2 * NCBC + j)),
                  pl.BlockSpec((TP, CBC), lambda j: (0, 3 * NCBC + j)), pl.BlockSpec((KWP, CBC), lambda j: (0, j)),
                  _AFTER],
        out_specs=[pl.BlockSpec((2, TP, CBC), lambda j: (0, 0, j)), pl.BlockSpec((8 * KWP, CBC), lambda j: (0, j)),
                   pl.BlockSpec((24, CBC), lambda j: (0, j))],
        out_shape=[jax.ShapeDtypeStruct((2, TP, DC), BF16),
                   jax.ShapeDtypeStruct((8 * KWP, DC), F32), jax.ShapeDtypeStruct((24, DC), F32)],
        scratch_shapes=[pltpu.VMEM((TP + KWP, CBC), F32), pltpu.VMEM((TP + KWP, CBC), F32)],
        compiler_params=_cparams(),
    )(dvc, z, z, dw_w, after)


def _lru_bwd(dycat, z, xc, hs, conv_w, wa_g, b_a, wx_g, b_x, lam, after):
    NV = 6

    def body(dy_ref, x_ref, g_ref, xc_ref, hs_ref, cw_ref, wa_ref, ba_ref, wx_ref, bx_ref, lam_ref, after_ref,
             dzl_ref, dwa_ref, dwx_ref, dcw_ref, vecs_ref, ga_s, gx_s, dxc_s):
        vecs_ref[...] = jnp.zeros_like(vecs_ref)
        dcw_ref[...] = jnp.zeros_like(dcw_ref)
        dxc_s[pl.ds(TP, 8), :] = jnp.zeros((8, CB), F32)

        def gate_chunk(ci, carry):
            r0 = pl.multiple_of(ci * TM, TM)
            xb = xc_ref[pl.ds(r0, TM), :].astype(BF16)
            ga_s[pl.ds(r0, TM), :] = jnp.dot(xb, wa_ref[...], preferred_element_type=F32) + ba_ref[...]
            gx_s[pl.ds(r0, TM), :] = jnp.dot(xb, wx_ref[...], preferred_element_type=F32) + bx_ref[...]
            return carry
        lax.fori_loop(0, TP // TM, gate_chunk, 0)

        sp8 = LRU_C * _softplus(-lam_ref[...])
        row = _row_iota((R, CB))
        nchunk = TP // R

        def scan_chunk(cj, carry):
            a_next, lam_next = carry
            ci = nchunk - 1 - cj
            r0 = pl.multiple_of(ci * R, R)
            dyv = dy_ref[pl.ds(r0, R), :]
            g = g_ref[pl.ds(r0, R), :]
            hv = hs_ref[pl.ds(r0, R), :]
            xc = xc_ref[pl.ds(r0, R), :]
            sg = _sig(g)
            dgl = dyv * hv * (sg * (1.0 + g * (1.0 - sg)))
            dzl_ref[1, pl.ds(r0, R), :] = dgl.astype(BF16)
            vecs_ref[0:8, :] += _fold8(dgl)
            dhs = dyv * (g * sg)
            r, i, a, mult = _gate_values(ga_s[pl.ds(r0, R), :], gx_s[pl.ds(r0, R), :], xc, sp8)
            b = jnp.where(row == R - 1, a_next, pltpu.roll(a, R - 1, 0))
            lv = dhs
            k = 1
            while k < R:
                m = row < R - k
                lv = jnp.where(m, lv + b * pltpu.roll(lv, R - k, 0), lv)
                b = jnp.where(m, b * pltpu.roll(b, R - k, 0), b)
                k *= 2
            lv = lv + b * lam_next
            p0 = pl.multiple_of(jnp.maximum(r0 - 8, 0), 8)
            hprev8 = jnp.where(ci > 0, hs_ref[pl.ds(p0, 8), :], 0.0)
            hprev = pltpu.roll(jnp.concatenate([hprev8, hv], axis=0), 1, 0)[8:8 + R, :]
            da = lv * hprev
            ixc = i * xc
            dmult = lv * ixc
            di = lv * mult * xc
            dxc_s[pl.ds(r0, R), :] = lv * mult * i
            a2 = a * a
            dlog_a = da * a - dmult * a2 / mult
            vecs_ref[32:40, :] += _fold8(dlog_a * r)
            dga = -(dlog_a * sp8) * r * (1.0 - r)
            dgx = di * i * (1.0 - i)
            ga_s[pl.ds(r0, R), :] = dga
            gx_s[pl.ds(r0, R), :] = dgx
            vecs_ref[16:24, :] += _fold8(dga)
            vecs_ref[24:32, :] += _fold8(dgx)
            a_first = jnp.sum(jnp.where(row == 0, a, 0.0), axis=0, keepdims=True)
            l_first = jnp.sum(jnp.where(row == 0, lv, 0.0), axis=0, keepdims=True)
            return a_first, l_first
        lax.fori_loop(0, nchunk, scan_chunk, (jnp.zeros((1, CB), F32), jnp.zeros((1, CB), F32)))

        dwa_ref[...] = jnp.zeros_like(dwa_ref)
        dwx_ref[...] = jnp.zeros_like(dwx_ref)

        def mm_chunk(ci, carry):
            r0 = pl.multiple_of(ci * TM, TM)
            xb = xc_ref[pl.ds(r0, TM), :].astype(BF16)
            dgab = ga_s[pl.ds(r0, TM), :].astype(BF16)
            dgxb = gx_s[pl.ds(r0, TM), :].astype(BF16)
            dxc_s[pl.ds(r0, TM), :] += (lax.dot_general(dgab, wa_ref[...], _NT, preferred_element_type=F32)
                                        + lax.dot_general(dgxb, wx_ref[...], _NT, preferred_element_type=F32))
            dwa_ref[...] += lax.dot_general(xb, dgab, _TN, preferred_element_type=F32)
            dwx_ref[...] += lax.dot_general(xb, dgxb, _TN, preferred_element_type=F32)
            return carry
        lax.fori_loop(0, TP // TM, mm_chunk, 0)

        taps = [cw_ref[k:k + 1, :] for k in range(LW)]

        def conv_chunk(ci, carry):
            r0 = pl.multiple_of(ci * R, R)
            dbuf = dxc_s[pl.ds(r0, R + 8), :]
            dcur = dbuf[0:R, :]
            p0 = pl.multiple_of(jnp.maximum(r0 - 8, 0), 8)
            xprev = jnp.where(ci > 0, x_ref[pl.ds(p0, 8), :], 0.0)
            xbuf = jnp.concatenate([xprev, x_ref[pl.ds(r0, R), :]], axis=0)
            dxl = dcur * taps[LW - 1]
            dcw_ref[8 * (LW - 1):8 * LW, :] += _fold8(dcur * xbuf[8:8 + R, :])
            for s in range(1, LW):
                k = LW - 1 - s
                dxl = dxl + pltpu.roll(dbuf, R + 8 - s, 0)[0:R, :] * taps[k]
                dcw_ref[8 * k:8 * k + 8, :] += _fold8(dcur * pltpu.roll(xbuf, s, 0)[8:8 + R, :])
            dzl_ref[0, pl.ds(r0, R), :] = dxl.astype(BF16)
            vecs_ref[8:16, :] += _fold8(dxl)
            vecs_ref[40:48, :] += _fold8(dcur)
            return carry
        lax.fori_loop(0, TP // R, conv_chunk, 0)
        vecs_ref[32:40, :] = vecs_ref[32:40, :] * (LRU_C * _sig(-lam_ref[...]))

    col = lambda off: pl.BlockSpec((TP, CB), lambda j: (0, off + j))
    vec = pl.BlockSpec((1, CB), lambda j: (0, j))
    wsp = pl.BlockSpec((None, CB, CB), lambda j: (j, 0, 0))
    return pl.pallas_call(
        body, name="lru_bwd",
        grid=(NCB,),
        in_specs=[col(0), col(0), col(NCB), col(0), col(0), pl.BlockSpec((LW, CB), lambda j: (0, j)),
                  wsp, vec, wsp, vec, vec, _AFTER],
        out_specs=[pl.BlockSpec((2, TP, CB), lambda j: (0, 0, j)), wsp, wsp,
                   pl.BlockSpec((8 * LW, CB), lambda j: (0, j)), pl.BlockSpec((8 * NV, CB), lambda j: (0, j))],
        out_shape=[jax.ShapeDtypeStruct((2, TP, DL), BF16),
                   jax.ShapeDtypeStruct((NCB, CB, CB), F32), jax.ShapeDtypeStruct((NCB, CB, CB), F32),
                   jax.ShapeDtypeStruct((8 * LW, DL), F32), jax.ShapeDtypeStruct((8 * NV, DL), F32)],
        scratch_shapes=[pltpu.VMEM((TP, CB), F32), pltpu.VMEM((TP, CB), F32), pltpu.VMEM((TP + 8, CB), F32)],
        compiler_params=_cparams(),
    )(dycat, z, z, xc, hs, conv_w, wa_g, b_a, wx_g, b_x, lam, after)


def _dz_section(sec, dzl_ref, dzc_ref, dgc_ref, use):
    @pl.when(sec < 2)
    def _():
        use(dzl_ref)

    @pl.when(jnp.logical_and(sec >= 2, sec < 4))
    def _():
        use(dzc_ref)

    @pl.when(sec == 4)
    def _():
        use(dgc_ref)


def _dz_specs(rows, index):
    return [pl.BlockSpec((None, rows, 1024), lambda a, b: (jnp.minimum(index(a, b)[1], 1), index(a, b)[0], 0)),
            pl.BlockSpec((None, rows, 1024), lambda a, b: (jnp.clip(index(a, b)[1] - 2, 0, 1), index(a, b)[0], 0)),
            pl.BlockSpec((rows, 1024), lambda a, b: (index(a, b)[0], 0))]


def _inproj_wgrad(name, hn, dzs, after):
    KB = 512
    nsec = dzs.shape[0]

    def body(hn_ref, dz_ref, after_ref, dw_ref):
        dw_ref[...] = lax.dot_general(hn_ref[...], dz_ref[...], _TN, preferred_element_type=F32).astype(BF16)

    return pl.pallas_call(
        body, name=name,
        grid=(nsec, D // KB),
        in_specs=[pl.BlockSpec((TP, KB), lambda n, kb: (0, kb)),
                  pl.BlockSpec((None, TP, 1024), lambda n, kb: (n, 0, 0)), _AFTER],
        out_specs=pl.BlockSpec((KB, 1024), lambda n, kb: (kb, n)),
        out_shape=jax.ShapeDtypeStruct((D, nsec * 1024), BF16),
        compiler_params=_cparams(),
    )(hn, dzs, after)


def _sum_win_parts(parts_a, parts_b, parts_c):
    RB = 64

    def body(a_ref, b_ref, c_ref, o_ref):
        def chunk(ci, carry):
            r0 = pl.multiple_of(ci * R, R)
            for ref, base, ncol in ((a_ref, 0, 2048), (b_ref, 2048, 2048), (c_ref, 4096, 1024)):
                for c0 in range(0, ncol, 512):
                    acc = ref[0, pl.ds(r0, R), c0:c0 + 512].astype(F32)
                    for sidx in range(1, NDEV):
                        acc = acc + ref[sidx, pl.ds(r0, R), c0:c0 + 512].astype(F32)
                    o_ref[pl.ds(r0, R), base + c0:base + c0 + 512] = acc.astype(BF16)
            return carry
        lax.fori_loop(0, RB // R, chunk, 0)

    spec = lambda ncol: pl.BlockSpec((NDEV, RB, ncol), lambda i: (0, i, 0))
    return pl.pallas_call(
        body, name="sum_win_parts",
        grid=(D // NDEV // RB,),
        in_specs=[spec(2048), spec(2048), spec(1024)],
        out_specs=pl.BlockSpec((RB, NIN), lambda i: (i, 0)),
        out_shape=jax.ShapeDtypeStruct((D // NDEV, NIN), BF16),
        compiler_params=_cparams(),
    )(parts_a, parts_b, parts_c)


def _inproj_bwd(dzl, dzc, dgc, w_in, h, dout, pre_w, after):
    nsec = NIN // 1024

    def body(dzl_ref, dzc_ref, dgc_ref, w_ref, h_ref, dout_ref, pw_ref, after_ref, gx_hbm, dmeta_ref, dpw_ref,
             acc_s, dh_s, sem):
        i = pl.program_id(0)
        s = pl.program_id(1)

        def gx_copy(t):
            lo, n, off = _tile_rows(t)
            return pltpu.make_async_copy(dh_s.at[pl.ds(off, n)], gx_hbm.at[pl.ds(lo, n)], sem)

        @pl.when(s == 0)
        def _():
            acc_s[...] = jnp.zeros_like(acc_s)

        def use(dz_ref):
            acc_s[...] += lax.dot_general(dz_ref[...], w_ref[...], _NT, preferred_element_type=F32)
        _dz_section(s, dzl_ref, dzc_ref, dgc_ref, use)

        @pl.when(jnp.logical_and(i == 0, s == nsec - 1))
        def _():
            dpw_ref[...] = jnp.zeros_like(dpw_ref)

        @pl.when(s == nsec - 1)
        def _():
            _for_tile(i - 1, lambda t: gx_copy(t).wait())
            pw = pw_ref[...]

            def chunk(ci, carry):
                r0 = pl.multiple_of(ci * 8, 8)
                hv = h_ref[pl.ds(r0, 8), :]
                dhn = acc_s[pl.ds(r0, 8), :]
                rs = lax.rsqrt(jnp.mean(hv * hv, axis=-1, keepdims=True) + EPS)
                dpw_ref[...] += dhn * (hv * rs)
                gw = dhn * pw
                dot = jnp.mean(gw * hv, axis=-1, keepdims=True)
                dh_s[pl.ds(r0, 8), :] = rs * gw - hv * (rs * rs * rs * dot) + dout_ref[pl.ds(r0, 8), :]
                return carry
            lax.fori_loop(0, TM // 8, chunk, 0, unroll=4)
            _for_tile(i, lambda t: gx_copy(t).start())

            @pl.when(i == 0)
            def _():
                dmeta_ref[...] = dh_s[0:NMETA, :]

            @pl.when(i == NTILE - 1)
            def _():
                gx_copy(NTILE - 1).wait()

    row = pl.BlockSpec((TM, D), lambda i, s: (i, 0))
    return pl.pallas_call(
        body, name="inproj_bwd",
        grid=(TP // TM, nsec),
        in_specs=_dz_specs(TM, lambda i, s: (i, s)) + [
            pl.BlockSpec((D, 1024), lambda i, s: (0, s)), row, row, pl.BlockSpec((1, D), lambda i, s: (0, 0)),
            _AFTER],
        out_specs=[pl.BlockSpec(memory_space=pl.ANY), pl.BlockSpec((NMETA, D), lambda i, s: (0, 0)),
                   pl.BlockSpec((8, D), lambda i, s: (0, 0))],
        out_shape=[jax.ShapeDtypeStruct((SEQ, D), F32), jax.ShapeDtypeStruct((NMETA, D), F32),
                   jax.ShapeDtypeStruct((8, D), F32)],
        scratch_shapes=[pltpu.VMEM((TM, D), F32), pltpu.VMEM((TM, D), F32), pltpu.SemaphoreType.DMA(())],
        compiler_params=_cparams(),
    )(dzl, dzc, dgc, w_in, h, dout, pre_w, after)


def _adamw(name, parts, w, m, v, block_rows):
    rows, cols = w.shape
    nparts = parts.shape[0]
    cw = cols if cols <= 640 else 512

    def body(p_ref, w_ref, m_ref, v_ref, g_ref, d_ref, nm_ref, nv_ref):
        def chunk(ci, carry):
            r0 = pl.multiple_of(ci * R, R)
            for c0 in range(0, cols, cw):
                at = (pl.ds(r0, R), slice(c0, c0 + cw))
                g = p_ref[(0,) + at].astype(F32)
                for sidx in range(1, nparts):
                    g = g + p_ref[(sidx,) + at].astype(F32)
                delta, mv, vv = _adam_math(g, w_ref[at], m_ref[at], v_ref[at])
                g_ref[at] = g
                nm_ref[at] = mv
                nv_ref[at] = vv
                d_ref[at] = delta
            return carry
        lax.fori_loop(0, block_rows // R, chunk, 0)

    blk = pl.BlockSpec((block_rows, cols), lambda i: (i, 0))
    shp = jax.ShapeDtypeStruct((rows, cols), F32)
    return pl.pallas_call(
        body, name=name,
        grid=(rows // block_rows,),
        in_specs=[pl.BlockSpec((nparts, block_rows, cols), lambda i: (0, i, 0)), blk, blk, blk],
        out_specs=[blk, blk, blk, blk],
        out_shape=[shp, shp, shp, shp],
        compiler_params=_cparams(),
    )(parts, w, m, v)


def _adam_math(g, w, m, v):
    c1 = 1.0 / (1.0 - ADAM_B1 ** ADAM_STEP)
    c2 = 1.0 / (1.0 - ADAM_B2 ** ADAM_STEP)
    mv = ADAM_B1 * m + (1.0 - ADAM_B1) * g
    vv = ADAM_B2 * v + (1.0 - ADAM_B2) * (g * g)
    upd = (mv * c1) / (jnp.sqrt(vv * c2) + ADAM_EPS) + ADAM_WD * w
    return -ADAM_LR * upd, mv, vv


_VEC = [("pre_norm_w", 2), ("post_norm_w", 2), ("b_in", 5), ("lru_conv_b", 1), ("b_gate_a", 1), ("b_gate_x", 1),
        ("lru_lambda", 1), ("conf_dw_b", 1), ("conf_ln_w", 1), ("conf_ln_b", 1), ("conf_pw_b", 1)]
_VEC_ROWS = 24
_LOSS_ROW = 17
_SM_ROWS = 64


def _pack_grads(dprew_acc, dpostw_acc, cvecs, kvecs, lvecs, dcw_acc, ddw_acc, dh, loss_acc):
    def body(pre_ref, post_ref, c_ref, k_ref, l_ref, dcw_ref, ddw_ref, dh_ref, loss_ref, vec_ref, small_ref, tmp):
        s8 = lambda ref, r: jnp.sum(ref[8 * r:8 * r + 8, :], axis=0, keepdims=True)
        vec_ref[...] = jnp.zeros_like(vec_ref)
        pre, post = s8(pre_ref, 0), s8(post_ref, 0)
        rows = [pre[:, 0:1024], pre[:, 1024:2048], post[:, 0:1024], post[:, 1024:2048],
                s8(l_ref, 1), s8(l_ref, 0), s8(k_ref, 1), s8(k_ref, 2), s8(c_ref, 1),
                s8(l_ref, 5), s8(l_ref, 2), s8(l_ref, 3), s8(l_ref, 4),
                s8(k_ref, 0), s8(c_ref, 2), s8(c_ref, 3), s8(c_ref, 0)]
        for r, val in enumerate(rows):
            vec_ref[r:r + 1, :] = val
        vec_ref[_LOSS_ROW:_LOSS_ROW + 1, :] = jnp.zeros((1, 1024), F32) + (0.5 / D) * jnp.sum(loss_ref[...])

        small_ref[...] = jnp.zeros_like(small_ref)
        for k in range(LW):
            tmp[k:k + 1, :] = s8(dcw_ref, k)
        for k in range(KW):
            tmp[8 + k:9 + k, :] = s8(ddw_ref, k)
        for d in range(NDEV):
            small_ref[d, 0:LW, 0:128] = tmp[0:LW, 128 * d:128 * d + 128]
            small_ref[d, 8:8 + KW, 0:128] = tmp[8:8 + KW, 128 * d:128 * d + 128]
            small_ref[d, 40:56, :] = dh_ref[:, 256 * d:256 * d + 256]

    full = lambda a: pl.BlockSpec(a.shape, lambda i: (0,) * a.ndim)
    ins = [dprew_acc, dpostw_acc, cvecs, kvecs, lvecs, dcw_acc, ddw_acc]
    return pl.pallas_call(
        body, name="pack_grads",
        grid=(1,),
        in_specs=[full(a) for a in ins] + [full(dh), full(loss_acc)],
        out_specs=[pl.BlockSpec((_VEC_ROWS, 1024), lambda i: (0, 0)),
                   pl.BlockSpec((NDEV, _SM_ROWS, 256), lambda i: (0, 0, 0))],
        out_shape=[jax.ShapeDtypeStruct((_VEC_ROWS, 1024), F32), jax.ShapeDtypeStruct((NDEV, _SM_ROWS, 256), F32)],
        scratch_shapes=[pltpu.VMEM((40, 1024), F32)],
        compiler_params=_cparams(),
    )(*ins, dh, loss_acc)


def _adamw_vec(parts, W, M, V):
    nv = len(_VEC)

    def body(*refs):
        p_ref = refs[0]
        w_refs, m_refs, v_refs = refs[1:1 + nv], refs[1 + nv:1 + 2 * nv], refs[1 + 2 * nv:1 + 3 * nv]
        outs = refs[1 + 3 * nv:]

        def total(r):
            acc = p_ref[0, r:r + 1, :]
            for sidx in range(1, NDEV):
                acc = acc + p_ref[sidx, r:r + 1, :]
            return acc

        row = 0
        for idx, (_, nrows) in enumerate(_VEC):
            for part in range(nrows):
                cols = slice(1024 * part, 1024 * part + 1024)
                g = total(row + part)
                delta, mv, vv = _adam_math(g, w_refs[idx][:, cols], m_refs[idx][:, cols], v_refs[idx][:, cols])
                for o, val in zip(outs[4 * idx:4 * idx + 4], (g, delta, mv, vv)):
                    o[:, cols] = val
            row += nrows
        outs[-1][...] = total(_LOSS_ROW)[:, 0:128]

    names = [n for n, _ in _VEC]
    flat = lambda d: [d[n].reshape(1, -1) for n in names]
    ws, ms, vs = flat(W), flat(M), flat(V)
    res = pl.pallas_call(
        body, name="adamw_vec",
        out_shape=[jax.ShapeDtypeStruct(w.shape, F32) for w in ws for _ in range(4)]
        + [jax.ShapeDtypeStruct((1, 128), F32)],
        compiler_params=_cparams(),
    )(parts, *ws, *ms, *vs)
    return {n: tuple(res[4 * i:4 * i + 4]) for i, n in enumerate(names)}, res[-1]


def _adamw_small(parts, W, M, V):
    where = {"lru_conv_w": (slice(0, LW), slice(0, 128)), "conf_dw_w": (slice(8, 8 + KW), slice(0, 128)),
             "meta_tokens": (slice(40, 56), slice(0, 256))}
    names = list(where)

    def body(*refs):
        p_ref = refs[0]
        outs = refs[10:]
        for idx, n in enumerate(names):
            rs, cs = where[n]
            g = p_ref[0, rs, cs]
            for sidx in range(1, NDEV):
                g = g + p_ref[sidx, rs, cs]
            delta, mv, vv = _adam_math(g, refs[1 + idx][...], refs[4 + idx][...], refs[7 + idx][...])
            for o, val in zip(outs[4 * idx:4 * idx + 4], (g, delta, mv, vv)):
                o[...] = val

    two_d = lambda a: a.reshape(a.shape[-2:])
    ws, ms, vs = ([two_d(d[n]) for n in names] for d in (W, M, V))
    res = pl.pallas_call(
        body, name="adamw_small",
        out_shape=[jax.ShapeDtypeStruct(w.shape, F32) for w in ws for _ in range(4)],
        compiler_params=_cparams(),
    )(parts, *ws, *ms, *vs)
    return {n: tuple(res[4 * i:4 * i + 4]) for i, n in enumerate(names)}


def _pack_small(lru_cw, dw_w, meta):
    buf = jnp.zeros((_SM_ROWS, 256), F32)
    buf = buf.at[0:LW, 0:128].set(lru_cw)
    buf = buf.at[8:8 + dw_w.shape[0], 0:128].set(dw_w)
    return buf.at[40:56, :].set(meta)


def _block_diag4(w):
    w4 = w.reshape(NCB, 4, 64, 64)
    eye = jnp.eye(4, dtype=w.dtype)
    return jnp.einsum("ghij,hk->ghikj", w4, eye).reshape(NCB, CB, CB)


def _diag_blocks(g):
    g5 = g.reshape(NCB, 4, 64, 4, 64)
    return jnp.stack([g5[:, hh, :, hh, :] for hh in range(4)], axis=1).reshape(16, 64, 64)


def _gate_mats(W):
    return _block_diag4(W["w_gate_a"][0]).astype(BF16), _block_diag4(W["w_gate_x"][0]).astype(BF16)


def _local_step(x, target, meta_full, inproj, out_weights, lru_cw_full, dw_w_full, W, gate_mats, send):
    wa_g, wx_g = gate_mats

    h, hn = _prenorm(x, meta_full, W["pre_norm_w"])
    z, win_full = inproj(hn)
    ylru, xc, hs = _lru_fwd(z, lru_cw_full, W["lru_conv_b"], wa_g, W["b_gate_a"], wx_g, W["b_gate_x"],
                            W["lru_lambda"])
    vc = _conf_fwd_conv(z, dw_w_full, W["conf_dw_b"])
    wout_full, pw_full = out_weights(vc)
    yconf, p = _conf_fwd_proj(vc, z, W["conf_ln_w"], W["conf_ln_b"], pw_full, W["conf_pw_b"])
    dout, dy, loss_acc, dpostw_acc = _outproj_loss(ylru, yconf, wout_full, h, target, W["post_norm_w"])

    dycat, dwout_part = _outproj_bwd(dy, ylru, yconf, wout_full)
    tok = send("w_out", ("w_out", dwout_part))
    dvc, dgc, dpw_part, cvecs = _conf_bwd_proj(dycat, p, z, vc, W["conf_ln_w"], W["conf_ln_b"], pw_full, tok)
    tok = send("w_in_c", ("conf_pw_w", dpw_part), ("w_in_c", _inproj_wgrad("inproj_wgrad_c", hn, dgc[None], dgc)))
    dzc, ddw_acc, kvecs = _conf_bwd_conv(dvc, z, dw_w_full, tok)
    tok = send("w_in_b", ("w_in_b", _inproj_wgrad("inproj_wgrad_b", hn, dzc, dzc)))
    dzl, dwa_g, dwx_g, dcw_acc, lvecs = _lru_bwd(dycat, z, xc, hs, lru_cw_full, wa_g, W["b_gate_a"], wx_g,
                                                 W["b_gate_x"], W["lru_lambda"], tok)
    tok = send("w_gates", ("w_gate_a", _diag_blocks(dwa_g).reshape(16 * 64, 64)),
               ("w_gate_x", _diag_blocks(dwx_g).reshape(16 * 64, 64)))
    tok = send("w_in_a", ("w_in_a", _inproj_wgrad("inproj_wgrad_a", hn, dzl, tok)))
    grad_x, dmeta, dprew_acc = _inproj_bwd(dzl, dzc, dgc, win_full, h, dout, W["pre_norm_w"], tok)

    vec_pack, small_part = _pack_grads(dprew_acc, dpostw_acc, cvecs, kvecs, lvecs, dcw_acc, ddw_acc, dmeta, loss_acc)
    return grad_x, vec_pack, small_part


def kernel(x, meta_tokens, pre_norm_w, post_norm_w, w_in, b_in, lru_conv_w, lru_conv_b, w_gate_a, b_gate_a, w_gate_x, b_gate_x, lru_lambda, conf_dw_w, conf_dw_b, conf_ln_w, conf_ln_b, conf_pw_w, conf_pw_b, w_out, loss_target, m_meta_tokens, m_pre_norm_w, m_post_norm_w, m_w_in, m_b_in, m_lru_conv_w, m_lru_conv_b, m_w_gate_a, m_b_gate_a, m_w_gate_x, m_b_gate_x, m_lru_lambda, m_conf_dw_w, m_conf_dw_b, m_conf_ln_w, m_conf_ln_b, m_conf_pw_w, m_conf_pw_b, m_w_out, v_meta_tokens, v_pre_norm_w, v_post_norm_w, v_w_in, v_b_in, v_lru_conv_w, v_lru_conv_b, v_w_gate_a, v_b_gate_a, v_w_gate_x, v_b_gate_x, v_lru_lambda, v_conf_dw_w, v_conf_dw_b, v_conf_ln_w, v_conf_ln_b, v_conf_pw_w, v_conf_pw_b, v_w_out):
    W = dict(meta_tokens=meta_tokens, pre_norm_w=pre_norm_w, post_norm_w=post_norm_w, w_in=w_in, b_in=b_in,
             lru_conv_w=lru_conv_w, lru_conv_b=lru_conv_b, w_gate_a=w_gate_a, b_gate_a=b_gate_a,
             w_gate_x=w_gate_x, b_gate_x=b_gate_x, lru_lambda=lru_lambda, conf_dw_w=conf_dw_w,
             conf_dw_b=conf_dw_b, conf_ln_w=conf_ln_w, conf_ln_b=conf_ln_b, conf_pw_w=conf_pw_w,
             conf_pw_b=conf_pw_b, w_out=w_out)
    M = dict(meta_tokens=m_meta_tokens, pre_norm_w=m_pre_norm_w, post_norm_w=m_post_norm_w, w_in=m_w_in,
             b_in=m_b_in, lru_conv_w=m_lru_conv_w, lru_conv_b=m_lru_conv_b, w_gate_a=m_w_gate_a,
             b_gate_a=m_b_gate_a, w_gate_x=m_w_gate_x, b_gate_x=m_b_gate_x, lru_lambda=m_lru_lambda,
             conf_dw_w=m_conf_dw_w, conf_dw_b=m_conf_dw_b, conf_ln_w=m_conf_ln_w, conf_ln_b=m_conf_ln_b,
             conf_pw_w=m_conf_pw_w, conf_pw_b=m_conf_pw_b, w_out=m_w_out)
    V = dict(meta_tokens=v_meta_tokens, pre_norm_w=v_pre_norm_w, post_norm_w=v_post_norm_w, w_in=v_w_in,
             b_in=v_b_in, lru_conv_w=v_lru_conv_w, lru_conv_b=v_lru_conv_b, w_gate_a=v_w_gate_a,
             b_gate_a=v_b_gate_a, w_gate_x=v_w_gate_x, b_gate_x=v_b_gate_x, lru_lambda=v_lru_lambda,
             conf_dw_w=v_conf_dw_w, conf_dw_b=v_conf_dw_b, conf_ln_w=v_conf_ln_w, conf_ln_b=v_conf_ln_b,
             conf_pw_w=v_conf_pw_w, conf_pw_b=v_conf_pw_b, w_out=v_w_out)
    names = list(W.keys())
    shapes = {n: W[n].shape for n in names}

    small = _pack_small(lru_conv_w[0], conf_dw_w[0], meta_tokens)
    (small_flight,), tok = _exchange_start("gather_small_start", [
        (small, jax.ShapeDtypeStruct((NDEV, _SM_ROWS, 256), F32), _whole, _slot)])
    win_flight, tok = _win_gather_start(w_in[0].astype(BF16) + tok[0, 0].astype(BF16))
    gate_mats = _gate_mats(W)
    wout_shard = w_out[0].astype(BF16) + tok[0, 0].astype(BF16)
    pw_shard = conf_pw_w[0].astype(BF16)
    cast_done = (gate_mats[0][0, 0:8, 0:128] + gate_mats[1][0, 0:8, 0:128]
                 + wout_shard[0:8, 0:128] + pw_shard[0:8, 0:128])
    win_flight, tok = _win_gather_links(win_flight, cast_done)
    gathered, tok = _exchange_start("gather_out_start", [
        (wout_shard + tok[0, 0].astype(BF16), jax.ShapeDtypeStruct((D, D), BF16), _whole, _rows(D // NDEV)),
        (pw_shard, jax.ShapeDtypeStruct((DC, DC), BF16), _whole, _rows(DC // NDEV)),
    ])
    (small_all,) = _exchange_wait("gather_small_wait", [small_flight], tok)
    unshard = lambda a: jnp.transpose(a, (1, 0, 2)).reshape(a.shape[1], -1)
    lru_cw_full = unshard(small_all[:, 0:LW, 0:128])
    dw_w_full = unshard(small_all[:, 8:8 + KWP, 0:128])
    meta_full = unshard(small_all[:, 40:56, :])

    def out_weights(after):
        return _exchange_wait("gather_out_wait", gathered, after)

    def inproj(hn):
        xi, yi, ci = lax.axis_index("x"), lax.axis_index("y"), lax.axis_index("c")
        shard = lambda px, py, pc: (4 * px + 2 * py + pc).astype(jnp.int32)
        over_links = jnp.stack([shard(1 - xi, yi, ci), shard(xi, 1 - yi, ci), shard(1 - xi, 1 - yi, ci)])
        z, src = _inproj_cols("inproj_own", jnp.stack([shard(xi, yi, ci)]), hn, win_flight["src"], b_in, None)
        flight = _win_gather_early(dict(win_flight, src=src))
        z, land = _inproj_cols("inproj_here", jnp.stack([shard(xi, yi, 1 - ci)]), hn, flight["land"], b_in, z)
        flight = _win_gather_forward(dict(flight, land=land), z)
        z, land = _inproj_cols("inproj_links", over_links, hn, flight["land"], b_in, z)
        land = _win_gather_wait(dict(flight, land=land))
        return _inproj_cols("inproj_sibling", over_links + 1 - 2 * ci, hn, land, b_in, z)

    row_stage = lambda ncol: (jax.ShapeDtypeStruct((NDEV, D // NDEV, ncol), BF16), _rows(D // NDEV))
    piece = {"w_in_a": row_stage(2048), "w_in_b": row_stage(2048), "w_in_c": row_stage(1024),
             "w_out": row_stage(D),
             "conf_pw_w": (jax.ShapeDtypeStruct((NDEV, DC // NDEV, DC), BF16), _rows(DC // NDEV)),
             "w_gate_a": (jax.ShapeDtypeStruct((NDEV, 16 * 64, 64), BF16), _whole),
             "w_gate_x": (jax.ShapeDtypeStruct((NDEV, 16 * 64, 64), BF16), _whole)}
    sent = {}

    def send(call, *named_parts):
        handles, token = _exchange_start(
            "scatter_" + call + "_start",
            [(part.astype(BF16), piece[name][0], piece[name][1], _slot) for name, part in named_parts])
        for (name, _), handle in zip(named_parts, handles):
            sent[name] = [handle]
        return token

    grad_x, vec_pack, small_part = _local_step(
        x[0], loss_target[0], meta_full, inproj, out_weights, lru_cw_full, dw_w_full, W, gate_mats, send)
    grad_x = grad_x[None]

    rest, tok = _exchange_start("scatter_rest_start", [
        (small_part, jax.ShapeDtypeStruct((NDEV, _SM_ROWS, 256), F32), _slot, _slot),
        (vec_pack, jax.ShapeDtypeStruct((NDEV, _VEC_ROWS, 1024), F32), _whole, _slot),
    ])
    (parts_c,) = _exchange_wait("scatter_w_in_c_wait", sent["w_in_c"], tok)
    (parts_b,) = _exchange_wait("scatter_w_in_b_wait", sent["w_in_b"], parts_c)
    (parts_a,) = _exchange_wait("scatter_w_in_a_wait", sent["w_in_a"], parts_b)
    win_rows = _sum_win_parts(parts_a, parts_b, parts_c)
    win_stage2, tok = _exchange_start("scatter_w_in_stage2_start", [
        (win_rows, jax.ShapeDtypeStruct((NDEV, D // NDEV, NIN // NDEV), BF16), _cols(NIN // NDEV), _slot)])

    G, DW, NM, NV = {}, {}, {}, {}
    (wout_parts,) = _exchange_wait("scatter_w_out_wait", sent["w_out"], tok)
    G["w_out"], DW["w_out"], NM["w_out"], NV["w_out"] = _adamw("adamw_w_out", wout_parts, w_out[0], m_w_out[0], v_w_out[0], 64)
    (pw_parts,) = _exchange_wait("scatter_conf_pw_w_wait", sent["conf_pw_w"], G["w_out"])
    G["conf_pw_w"], DW["conf_pw_w"], NM["conf_pw_w"], NV["conf_pw_w"] = _adamw(
        "adamw_pw", pw_parts, conf_pw_w[0], m_conf_pw_w[0], v_conf_pw_w[0], 128)
    res = {}
    wa_parts, wx_parts = _exchange_wait("scatter_w_gates_wait", sent["w_gate_a"] + sent["w_gate_x"], G["conf_pw_w"])
    for n, parts in (("w_gate_a", wa_parts), ("w_gate_x", wx_parts)):
        res[n] = _adamw("adamw_" + n, parts, *[d[n].reshape(16 * 64, 64) for d in (W, M, V)], 16 * 64)
    small_parts, vec_parts = _exchange_wait("scatter_rest_wait", rest, res["w_gate_x"][0])
    res.update(_adamw_small(small_parts, W, M, V))
    vec_res, loss_row = _adamw_vec(vec_parts, W, M, V)
    res.update(vec_res)
    (win_sum,) = _exchange_wait("scatter_w_in_stage2_wait", win_stage2, loss_row)
    res["w_in"] = _adamw("adamw_w_in", win_sum.reshape(1, D, NIN // NDEV), w_in[0], m_w_in[0], v_w_in[0], 256)
    for n, vals in res.items():
        for dst, val in zip((G, DW, NM, NV), vals):
            dst[n] = val
    for dst in (G, DW, NM, NV):
        for n in names:
            dst[n] = dst[n].reshape(shapes[n])
    loss = loss_row[0, 0]

    return (loss, grad_x, *[G[n] for n in names], *[DW[n] for n in names],
            *[NM[n] for n in names], *[NV[n] for n in names])
```

```python
import functools

import jax
import jax.numpy as jnp
from jax import lax
from jax.experimental import pallas as pl
from jax.experimental.pallas import tpu as pltpu

F32 = jnp.float32
BF16 = jnp.bfloat16

D = 2048
DL = 1024
DC = 1024
NIN = 5120
NMETA = 16
SEQ = 2048
T = NMETA + SEQ
TP = 2176
TM = 544
CB = 256
NCB = DL // CB
R = 16
KW = 31
KWP = 32
LW = 4
LRU_C = 8.0
EPS = 1e-6
NDEV = 8

ADAM_LR = 0.001
ADAM_B1 = 0.9
ADAM_B2 = 0.999
ADAM_EPS = 1e-08
ADAM_WD = 0.01
ADAM_STEP = 10

VMEM_LIMIT = 56 * 1024 * 1024


def _cparams():
    return pltpu.CompilerParams(vmem_limit_bytes=VMEM_LIMIT)


def _sig(x):
    return 1.0 / (1.0 + jnp.exp(-x))


def _expm1_neg(y):
    poly = y * (1.0 + y * (0.5 + y * (1.0 / 6.0 + y * (1.0 / 24.0 + y * (1.0 / 120.0)))))
    return jnp.where(y > -0.1, poly, jnp.exp(y) - 1.0)


def _softplus(x):
    e = jnp.exp(-jnp.abs(x))
    w = 1.0 + e
    l1p = jnp.where(w == 1.0, e, jnp.log(w) * e / (w - 1.0))
    return jnp.maximum(x, 0.0) + l1p


def _row_iota(shape):
    return lax.broadcasted_iota(jnp.int32, shape, 0)


def _fold8(v):
    return v[0:8, :] + v[8:16, :]


_FLIPS = [(k >> 2 & 1, k >> 1 & 1, k & 1) for k in range(1, NDEV)]
_HBM = pl.BlockSpec(memory_space=pltpu.HBM)
_SEM = pl.BlockSpec(memory_space=pltpu.SEMAPHORE)


def _peers():
    x, y, c = lax.axis_index("x"), lax.axis_index("y"), lax.axis_index("c")
    out = []
    for dx, dy, dc in _FLIPS:
        px = 1 - x if dx else x
        py = 1 - y if dy else y
        pc = 1 - c if dc else c
        out.append(((px, py, pc), 4 * px + 2 * py + pc))
    return 4 * x + 2 * y + c, out


def _exchange_start(name, items):
    n = len(items)

    def body(*refs):
        srcs, lands = refs[:n], refs[n:2 * n]
        outs = refs[2 * n:]
        send_sems, recv_sems, local_sems = outs[:n], outs[n:2 * n], outs[2 * n:3 * n]
        token = outs[-1]
        me, peers = _peers()
        for a in range(n):
            src_at, dst_at = items[a][2], items[a][3]
            pltpu.make_async_copy(src_at(srcs[a], me), dst_at(lands[a], me), local_sems[a]).start()
        for a in range(n):
            src_at, dst_at = items[a][2], items[a][3]
            for k, (pos, peer) in enumerate(peers):
                pltpu.make_async_remote_copy(
                    src_ref=src_at(srcs[a], peer), dst_ref=dst_at(lands[a], me),
                    send_sem=send_sems[a].at[k], recv_sem=recv_sems[a].at[k],
                    device_id=pos, device_id_type=pl.DeviceIdType.MESH).start()
        token[...] = jnp.zeros_like(token)

    srcs = [pltpu.with_memory_space_constraint(it[0], pltpu.HBM) for it in items]
    lands = [pltpu.with_memory_space_constraint(lax.empty(it[1].shape, it[1].dtype), pltpu.HBM) for it in items]
    sem7 = pltpu.SemaphoreType.DMA((NDEV - 1,))
    res = pl.pallas_call(
        body, name=name,
        out_shape=([sem7] * (2 * n) + [pltpu.SemaphoreType.DMA(())] * n
                   + [pltpu.HBM(a.shape, a.dtype) for a in srcs] + [pltpu.HBM(a.shape, a.dtype) for a in lands]
                   + [jax.ShapeDtypeStruct((8, 128), F32)]),
        in_specs=[_HBM] * (2 * n),
        out_specs=[_SEM] * (3 * n) + [_HBM] * (2 * n) + [pl.BlockSpec(memory_space=pltpu.VMEM)],
        input_output_aliases={i: 3 * n + i for i in range(2 * n)},
        compiler_params=pltpu.CompilerParams(has_side_effects=pltpu.SideEffectType.DATAFLOW_SIDE_EFFECTING),
    )(*srcs, *lands)
    handles = [dict(send=res[a], recv=res[n + a], local=res[2 * n + a], src=res[3 * n + a], land=res[4 * n + a],
                    src_at=items[a][2], dst_at=items[a][3]) for a in range(n)]
    return handles, res[-1]


def _wait_bytes(piece, sem):
    pltpu.make_async_copy(piece, piece, sem).wait()


def _exchange_wait(name, handles, after):
    n = len(handles)

    def body(*refs):
        srcs, lands = refs[:n], refs[n:2 * n]
        send_sems, recv_sems, local_sems = refs[2 * n:3 * n], refs[3 * n:4 * n], refs[4 * n:5 * n]
        me, peers = _peers()
        for a in range(n):
            src_at, dst_at = handles[a]["src_at"], handles[a]["dst_at"]
            for k, (pos, peer) in enumerate(peers):
                _wait_bytes(src_at(srcs[a], peer), send_sems[a].at[k])
                _wait_bytes(dst_at(lands[a], peer), recv_sems[a].at[k])
            pltpu.make_async_copy(src_at(srcs[a], me), dst_at(lands[a], me), local_sems[a]).wait()

    srcs = [hd["src"] for hd in handles]
    lands = [hd["land"] for hd in handles]
    res = pl.pallas_call(
        body, name=name,
        out_shape=[pltpu.HBM(a.shape, a.dtype) for a in srcs] + [pltpu.HBM(a.shape, a.dtype) for a in lands],
        in_specs=[_HBM] * (2 * n) + [_SEM] * (3 * n) + [pl.BlockSpec(memory_space=pl.ANY)],
        out_specs=[_HBM] * (2 * n),
        input_output_aliases={i: i for i in range(2 * n)},
        compiler_params=pltpu.CompilerParams(has_side_effects=pltpu.SideEffectType.DATAFLOW_SIDE_EFFECTING),
    )(*srcs, *lands, *[hd["send"] for hd in handles], *[hd["recv"] for hd in handles],
      *[hd["local"] for hd in handles], after)
    return list(res[n:])


_SIDE = pltpu.SideEffectType.DATAFLOW_SIDE_EFFECTING
_WCOLS = NIN // NDEV


def _win_cols(ref, l):
    return ref.at[:, pl.ds(pl.multiple_of(l * _WCOLS, 128), _WCOLS)]


def _win_routes():
    x, y, c = lax.axis_index("x"), lax.axis_index("y"), lax.axis_index("c")
    pos = [(x, y, 1 - c), (1 - x, y, c), (x, 1 - y, c), (1 - x, 1 - y, c)]
    return 4 * x + 2 * y + c, [(p, 4 * p[0] + 2 * p[1] + p[2]) for p in pos]


def _win_gather_start(shard):
    def body(src, land, send_sem, recv_sem, local_sem, src_thru, land_thru, token):
        me, routes = _win_routes()
        pltpu.make_async_copy(src, _win_cols(land, me), local_sem).start()
        pltpu.make_async_remote_copy(src_ref=src, dst_ref=_win_cols(land, me), send_sem=send_sem, recv_sem=recv_sem,
                                     device_id=routes[0][0], device_id_type=pl.DeviceIdType.MESH).start()
        token[...] = jnp.zeros_like(token)

    src = pltpu.with_memory_space_constraint(shard, pltpu.HBM)
    land = pltpu.with_memory_space_constraint(lax.empty((D, NIN), BF16), pltpu.HBM)
    sem = pltpu.SemaphoreType.DMA(())
    res = pl.pallas_call(
        body, name="win_gather_start",
        out_shape=[sem, sem, sem, pltpu.HBM(src.shape, BF16), pltpu.HBM(land.shape, BF16),
                   jax.ShapeDtypeStruct((8, 128), F32)],
        in_specs=[_HBM, _HBM],
        out_specs=[_SEM, _SEM, _SEM, _HBM, _HBM, pl.BlockSpec(memory_space=pltpu.VMEM)],
        input_output_aliases={0: 3, 1: 4},
        compiler_params=pltpu.CompilerParams(has_side_effects=_SIDE),
    )(src, land)
    return dict(send0=res[0], recv0=res[1], local=res[2], src=res[3], land=res[4]), res[5]


def _win_gather_links(hd, after):
    def body(src, land, after_ref, send_sems, recv_sems, src_thru, land_thru, token):
        me, routes = _win_routes()
        for k in (1, 2, 3):
            pltpu.make_async_remote_copy(src_ref=src, dst_ref=_win_cols(land, me), send_sem=send_sems.at[k - 1],
                                         recv_sem=recv_sems.at[k - 1], device_id=routes[k][0],
                                         device_id_type=pl.DeviceIdType.MESH).start()
        token[...] = jnp.zeros_like(token)

    sem3 = pltpu.SemaphoreType.DMA((3,))
    res = pl.pallas_call(
        body, name="win_gather_links",
        out_shape=[sem3, sem3, pltpu.HBM(hd["src"].shape, BF16), pltpu.HBM(hd["land"].shape, BF16),
                   jax.ShapeDtypeStruct((8, 128), F32)],
        in_specs=[_HBM, _HBM, pl.BlockSpec(memory_space=pl.ANY)],
        out_specs=[_SEM, _SEM, _HBM, _HBM, pl.BlockSpec(memory_space=pltpu.VMEM)],
        input_output_aliases={0: 2, 1: 3},
        compiler_params=pltpu.CompilerParams(has_side_effects=_SIDE),
    )(hd["src"], hd["land"], after)
    return dict(hd, send=res[0], recv=res[1], src=res[2], land=res[3]), res[4]


def _win_gather_forward(hd, after):
    def body(land, recv_sems, after_ref, land_thru, fsend_sems, frecv_sems):
        me, routes = _win_routes()
        sibling = routes[0][0]
        for k in (1, 2, 3):
            pos, peer = routes[k]
            piece = _win_cols(land, peer)
            pltpu.make_async_remote_copy(src_ref=piece, dst_ref=piece, send_sem=fsend_sems.at[k - 1],
                                         recv_sem=recv_sems.at[k - 1], device_id=pos,
                                         device_id_type=pl.DeviceIdType.MESH).wait_recv()
            pltpu.make_async_remote_copy(src_ref=piece, dst_ref=piece, send_sem=fsend_sems.at[k - 1],
                                         recv_sem=frecv_sems.at[k - 1], device_id=sibling,
                                         device_id_type=pl.DeviceIdType.MESH).start()

    sem3 = pltpu.SemaphoreType.DMA((3,))
    res = pl.pallas_call(
        body, name="win_gather_forward",
        out_shape=[pltpu.HBM(hd["land"].shape, BF16), sem3, sem3],
        in_specs=[_HBM, _SEM, pl.BlockSpec(memory_space=pl.ANY)],
        out_specs=[_HBM, _SEM, _SEM],
        input_output_aliases={0: 0},
        compiler_params=pltpu.CompilerParams(has_side_effects=_SIDE),
    )(hd["land"], hd["recv"], after)
    return dict(hd, land=res[0], fsend=res[1], frecv=res[2])


def _win_gather_early(hd):
    def body(src, land, recv_sem, local_sem, src_thru, land_thru):
        me, routes = _win_routes()
        _wait_bytes(_win_cols(land, routes[0][1]), recv_sem)
        pltpu.make_async_copy(src, _win_cols(land, me), local_sem).wait()

    res = pl.pallas_call(
        body, name="win_gather_early",
        out_shape=[pltpu.HBM(hd["src"].shape, BF16), pltpu.HBM(hd["land"].shape, BF16)],
        in_specs=[_HBM, _HBM, _SEM, _SEM],
        out_specs=[_HBM, _HBM],
        input_output_aliases={0: 0, 1: 1},
        compiler_params=pltpu.CompilerParams(has_side_effects=_SIDE),
    )(hd["src"], hd["land"], hd["recv0"], hd["local"])
    return dict(hd, src=res[0], land=res[1])


def _win_gather_wait(hd):
    def body(src, land, send0_sem, send_sems, fsend_sems, frecv_sems, src_thru, land_thru):
        me, routes = _win_routes()
        sib_pos, sibling = routes[0]
        for k in range(4):
            _wait_bytes(src, send0_sem if k == 0 else send_sems.at[k - 1])
        for k in (1, 2, 3):
            _wait_bytes(_win_cols(land, routes[k][1]), fsend_sems.at[k - 1])
            _wait_bytes(_win_cols(land, 4 * routes[k][0][0] + 2 * routes[k][0][1] + sib_pos[2]), frecv_sems.at[k - 1])

    res = pl.pallas_call(
        body, name="win_gather_wait",
        out_shape=[pltpu.HBM(hd["src"].shape, BF16), pltpu.HBM(hd["land"].shape, BF16)],
        in_specs=[_HBM, _HBM] + [_SEM] * 4,
        out_specs=[_HBM, _HBM],
        input_output_aliases={0: 0, 1: 1},
        compiler_params=pltpu.CompilerParams(has_side_effects=_SIDE),
    )(hd["src"], hd["land"], hd["send0"], hd["send"], hd["fsend"], hd["frecv"])
    return res[1]


def _whole(ref, l):
    return ref


def _slot(ref, l):
    return ref.at[l]


def _cols(width):
    def at(ref, l):
        return ref.at[:, pl.ds(pl.multiple_of(l * width, 128), width)]
    return at


def _rows(height):
    def at(ref, l):
        return ref.at[pl.ds(pl.multiple_of(l * height, 8), height), :]
    return at


NTILE = TP // TM


def _tile_rows(t):
    lo = max(t * TM - NMETA, 0)
    hi = min((t + 1) * TM - NMETA, SEQ)
    return lo, hi - lo, lo + NMETA - t * TM


def _for_tile(t, fn):
    for static_t in range(NTILE):
        pl.when(t == static_t)(functools.partial(fn, static_t))


def _token_tile_copy(hbm_ref, buf, sem, t):
    lo, n, off = _tile_rows(t)
    return pltpu.make_async_copy(hbm_ref.at[pl.ds(lo, n)], buf.at[pl.ds(off, n)], sem)


def _prenorm(x, meta_full, pre_w):
    def body(x_ref, meta_ref, pw_ref, h_ref, hn_ref, xbuf, sems):
        i = pl.program_id(0)
        slot = i % 2

        def start(t):
            _token_tile_copy(x_ref, xbuf.at[t % 2], sems.at[t % 2], t).start()

        @pl.when(i == 0)
        def _():
            start(0)
        _for_tile(i + 1, start)
        _for_tile(i, lambda t: _token_tile_copy(x_ref, xbuf.at[t % 2], sems.at[t % 2], t).wait())

        @pl.when(i == 0)
        def _():
            xbuf[0, 0:NMETA, :] = meta_ref[...]

        @pl.when(i == NTILE - 1)
        def _():
            last = _tile_rows(NTILE - 1)[1]
            xbuf[(NTILE - 1) % 2, last:TM, :] = jnp.zeros((TM - last, D), F32)

        pw = pw_ref[...]

        def chunk(ci, carry):
            r0 = pl.multiple_of(ci * R, R)
            xv = xbuf[slot, pl.ds(r0, R), :]
            h_ref[pl.ds(r0, R), :] = xv
            ms = jnp.mean(xv * xv, axis=-1, keepdims=True)
            hn_ref[pl.ds(r0, R), :] = (xv * lax.rsqrt(ms + EPS) * pw).astype(BF16)
            return carry
        lax.fori_loop(0, TM // R, chunk, 0, unroll=2)

    row = pl.BlockSpec((TM, D), lambda i: (i, 0))
    return pl.pallas_call(
        body, name="prenorm",
        grid=(NTILE,),
        in_specs=[pl.BlockSpec(memory_space=pl.ANY), pl.BlockSpec((NMETA, D), lambda i: (0, 0)),
                  pl.BlockSpec((1, D), lambda i: (0, 0))],
        out_specs=[row, row],
        out_shape=[jax.ShapeDtypeStruct((TP, D), F32), jax.ShapeDtypeStruct((TP, D), BF16)],
        scratch_shapes=[pltpu.VMEM((2, TM, D), F32), pltpu.SemaphoreType.DMA((2,))],
        compiler_params=_cparams(),
    )(x, meta_full, pre_w)


def _inproj_cols(name, shards, hn, w_land, b_in, z_prev):
    nsh = shards.shape[0]
    one_shard = w_land.shape[1] == _WCOLS

    def body(idx_ref, hn_ref, w_ref, b_ref, *rest):
        z_ref = rest[-2]
        z_ref[...] = jnp.dot(hn_ref[...], w_ref[...], preferred_element_type=F32) + b_ref[...]

    any_spec = pl.BlockSpec(memory_space=pl.ANY)
    in_specs = [pl.BlockSpec((TM, D), lambda j, i, idx: (i, 0)),
                pl.BlockSpec((D, _WCOLS), lambda j, i, idx: (0, 0 if one_shard else idx[j])),
                pl.BlockSpec((1, _WCOLS), lambda j, i, idx: (0, idx[j]))]
    operands = [hn, w_land, b_in]
    aliases = {2: 1}
    if z_prev is not None:
        in_specs.append(any_spec)
        operands.append(z_prev)
        aliases[4] = 0
    return pl.pallas_call(
        body, name=name,
        grid_spec=pltpu.PrefetchScalarGridSpec(
            num_scalar_prefetch=1, grid=(nsh, TP // TM), in_specs=in_specs,
            out_specs=[pl.BlockSpec((TM, _WCOLS), lambda j, i, idx: (i, idx[j])), any_spec]),
        out_shape=[jax.ShapeDtypeStruct((TP, NIN), F32), jax.ShapeDtypeStruct(w_land.shape, w_land.dtype)],
        input_output_aliases=aliases,
        compiler_params=_cparams(),
    )(shards, *operands)


def _gate_values(ga, gx, xc, sp8):
    r = _sig(ga)
    i = _sig(gx)
    log_a = -(r * sp8)
    a = jnp.exp(log_a)
    mult = jnp.sqrt(-_expm1_neg(2.0 * log_a))
    return r, i, a, mult


def _lru_fwd(z, conv_w, conv_b, wa_g, b_a, wx_g, b_x, lam):
    def body(x_ref, g_ref, cw_ref, cb_ref, wa_ref, ba_ref, wx_ref, bx_ref, lam_ref,
             y_ref, xc_ref, hs_ref, ga_s, gx_s):
        taps = [cw_ref[k:k + 1, :] for k in range(LW)]
        cb = cb_ref[...]

        def conv_chunk(ci, carry):
            r0 = pl.multiple_of(ci * R, R)
            cur = x_ref[pl.ds(r0, R), :]
            p0 = pl.multiple_of(jnp.maximum(r0 - 8, 0), 8)
            prev = jnp.where(ci > 0, x_ref[pl.ds(p0, 8), :], 0.0)
            buf = jnp.concatenate([prev, cur], axis=0)
            acc = cur * taps[LW - 1] + cb
            for s in range(1, LW):
                acc = acc + pltpu.roll(buf, s, 0)[8:8 + R, :] * taps[LW - 1 - s]
            xc_ref[pl.ds(r0, R), :] = acc
            return carry
        lax.fori_loop(0, TP // R, conv_chunk, 0)

        def gate_chunk(ci, carry):
            r0 = pl.multiple_of(ci * TM, TM)
            xb = xc_ref[pl.ds(r0, TM), :].astype(BF16)
            ga_s[pl.ds(r0, TM), :] = jnp.dot(xb, wa_ref[...], preferred_element_type=F32) + ba_ref[...]
            gx_s[pl.ds(r0, TM), :] = jnp.dot(xb, wx_ref[...], preferred_element_type=F32) + bx_ref[...]
            return carry
        lax.fori_loop(0, TP // TM, gate_chunk, 0)

        sp8 = LRU_C * _softplus(-lam_ref[...])
        row = _row_iota((R, CB))

        def scan_chunk(ci, hprev):
            r0 = pl.multiple_of(ci * R, R)
            xc = xc_ref[pl.ds(r0, R), :]
            _, i, a, mult = _gate_values(ga_s[pl.ds(r0, R), :], gx_s[pl.ds(r0, R), :], xc, sp8)
            u = mult * (i * xc)
            k = 1
            while k < R:
                m = row >= k
                u = jnp.where(m, a * pltpu.roll(u, k, 0) + u, u)
                a = jnp.where(m, a * pltpu.roll(a, k, 0), a)
                k *= 2
            hv = u + a * hprev
            hs_ref[pl.ds(r0, R), :] = hv
            g = g_ref[pl.ds(r0, R), :]
            y_ref[pl.ds(r0, R), :] = (hv * (g * _sig(g))).astype(BF16)
            return jnp.sum(jnp.where(row == R - 1, hv, 0.0), axis=0, keepdims=True)
        lax.fori_loop(0, TP // R // 2, lambda i, hp: scan_chunk(2 * i + 1, scan_chunk(2 * i, hp)),
                      jnp.zeros((1, CB), F32))

    col = lambda off: pl.BlockSpec((TP, CB), lambda j: (0, off + j))
    vec = pl.BlockSpec((1, CB), lambda j: (0, j))
    wsp = pl.BlockSpec((None, CB, CB), lambda j: (j, 0, 0))
    return pl.pallas_call(
        body, name="lru_fwd",
        grid=(NCB,),
        in_specs=[col(0), col(NCB), pl.BlockSpec((LW, CB), lambda j: (0, j)), vec, wsp, vec, wsp, vec, vec],
        out_specs=[col(0), col(0), col(0)],
        out_shape=[jax.ShapeDtypeStruct((TP, DL), BF16), jax.ShapeDtypeStruct((TP, DL), F32),
                   jax.ShapeDtypeStruct((TP, DL), F32)],
        scratch_shapes=[pltpu.VMEM((TP, CB), F32), pltpu.VMEM((TP, CB), F32)],
        compiler_params=_cparams(),
    )(z, z, conv_w, conv_b, wa_g, b_a, wx_g, b_x, lam)


CBC = 128
NCBC = DC // CBC
RC = 64


def _fold_rows(v):
    acc = v[0:8, :]
    for r in range(8, v.shape[0], 8):
        acc = acc + v[r:r + 8, :]
    return acc


def _conf_fwd_conv(z, dw_w, dw_b):
    def body(u1_ref, u2_ref, w_ref, b_ref, vc_ref, vs):
        vs[pl.ds(0, KWP), :] = jnp.zeros((KWP, CBC), F32)

        def glu_chunk(ci, carry):
            r0 = pl.multiple_of(ci * RC, RC)
            vs[pl.ds(KWP + r0, RC), :] = u1_ref[pl.ds(r0, RC), :] * _sig(u2_ref[pl.ds(r0, RC), :])
            return carry
        lax.fori_loop(0, TP // RC, glu_chunk, 0)

        bias = b_ref[...]

        def conv_chunk(ci, carry):
            r0 = pl.multiple_of(ci * RC, RC)
            buf = vs[pl.ds(r0, KWP + RC), :]
            acc = jnp.zeros((RC, CBC), F32) + bias
            for rr in range(8):
                rolled = buf if rr == 0 else pltpu.roll(buf, rr, 0)
                for q in range(4):
                    s = 8 * q + rr
                    if s > KW - 1:
                        continue
                    k = KW - 1 - s
                    acc = acc + rolled[KWP - 8 * q:KWP - 8 * q + RC, :] * w_ref[k:k + 1, :]
            vc_ref[pl.ds(r0, RC), :] = acc
            return carry
        lax.fori_loop(0, TP // RC, conv_chunk, 0)

    return pl.pallas_call(
        body, name="conf_fwd_conv",
        grid=(NCBC,),
        in_specs=[pl.BlockSpec((TP, CBC), lambda j: (0, 2 * NCBC + j)),
                  pl.BlockSpec((TP, CBC), lambda j: (0, 3 * NCBC + j)),
                  pl.BlockSpec((KWP, CBC), lambda j: (0, j)),
                  pl.BlockSpec((1, CBC), lambda j: (0, j))],
        out_specs=pl.BlockSpec((TP, CBC), lambda j: (0, j)),
        out_shape=jax.ShapeDtypeStruct((TP, DC), F32),
        scratch_shapes=[pltpu.VMEM((TP + KWP, CBC), F32)],
        compiler_params=_cparams(),
    )(z, z, dw_w, dw_b)


def _ln_chunk(vc, lw, lb):
    mu = jnp.mean(vc, axis=-1, keepdims=True)
    xm = vc - mu
    var = jnp.mean(xm * xm, axis=-1, keepdims=True)
    rstd = lax.rsqrt(var + EPS)
    xhat = xm * rstd
    return xhat, rstd, xhat * lw + lb


def _conf_fwd_proj(vc, z, ln_w, ln_b, pw_w, pw_b):
    def body(vc_ref, g_ref, lw_ref, lb_ref, w_ref, b_ref, y_ref, p_ref, s_s):
        lw, lb = lw_ref[...], lb_ref[...]

        def ln_chunk(ci, carry):
            r0 = pl.multiple_of(ci * R, R)
            for half in range(2):
                rr = r0 + 8 * half
                _, _, ln = _ln_chunk(vc_ref[pl.ds(rr, 8), :], lw, lb)
                p_ref[pl.ds(rr, 8), :] = ln * _sig(ln)
            s_s[pl.ds(r0, R), :] = p_ref[pl.ds(r0, R), :].astype(BF16)
            return carry
        lax.fori_loop(0, TM // R, ln_chunk, 0, unroll=2)

        p_ref[...] = jnp.dot(s_s[...], w_ref[...], preferred_element_type=F32) + b_ref[...]

        def out_chunk(ci, carry):
            r0 = pl.multiple_of(ci * R, R)
            g = g_ref[pl.ds(r0, R), :]
            y_ref[pl.ds(r0, R), :] = (p_ref[pl.ds(r0, R), :] * (g * _sig(g))).astype(BF16)
            return carry
        lax.fori_loop(0, TM // R, out_chunk, 0)

    row = pl.BlockSpec((TM, DC), lambda i: (i, 0))
    vec = pl.BlockSpec((1, DC), lambda i: (0, 0))
    return pl.pallas_call(
        body, name="conf_fwd_proj",
        grid=(TP // TM,),
        in_specs=[row, pl.BlockSpec((TM, DC), lambda i: (i, 4)), vec, vec,
                  pl.BlockSpec((DC, DC), lambda i: (0, 0)), vec],
        out_specs=[row, row],
        out_shape=[jax.ShapeDtypeStruct((TP, DC), BF16), jax.ShapeDtypeStruct((TP, DC), F32)],
        scratch_shapes=[pltpu.VMEM((TM, DC), BF16)],
        compiler_params=_cparams(),
    )(vc, z, ln_w, ln_b, pw_w, pw_b)


def _outproj_loss(ylru, yconf, w_out, h, target, post_w):
    def body(yl_ref, yc_ref, w_ref, h_ref, tgt_hbm, pw_ref, dout_ref, dy_ref, loss_ref, dpw_ref, y_s, t_ref, sem):
        i = pl.program_id(0)
        k = pl.program_id(1)

        @pl.when(k == 0)
        def _():
            _for_tile(i, lambda t: _token_tile_copy(tgt_hbm, t_ref, sem, t).start())
            y_s[...] = jnp.dot(yl_ref[...], w_ref[...], preferred_element_type=F32)

        @pl.when(k == 1)
        def _():
            y_s[...] += jnp.dot(yc_ref[...], w_ref[...], preferred_element_type=F32)

        @pl.when(jnp.logical_and(i == 0, k == 1))
        def _():
            loss_ref[...] = jnp.zeros_like(loss_ref)
            dpw_ref[...] = jnp.zeros_like(dpw_ref)

        @pl.when(k == 1)
        def _():
            _for_tile(i, lambda t: _token_tile_copy(tgt_hbm, t_ref, sem, t).wait())

            @pl.when(i == 0)
            def _():
                t_ref[0:NMETA, :] = jnp.zeros((NMETA, D), F32)

            @pl.when(i == NTILE - 1)
            def _():
                last = _tile_rows(NTILE - 1)[1]
                t_ref[last:TM, :] = jnp.zeros((TM - last, D), F32)

            pw = pw_ref[...]
            row = _row_iota((8, D))

            def chunk(ci, carry):
                r0 = pl.multiple_of(ci * 8, 8)
                yv = y_s[pl.ds(r0, 8), :]
                rs = lax.rsqrt(jnp.mean(yv * yv, axis=-1, keepdims=True) + EPS)
                grow = row + (i * TM + r0)
                valid = jnp.logical_and(grow >= NMETA, grow < T)
                yn = yv * rs
                err = jnp.where(valid, h_ref[pl.ds(r0, 8), :] + yn * pw - t_ref[pl.ds(r0, 8), :], 0.0)
                loss_ref[...] += err * err
                d_rn = err * (1.0 / D)
                dout_ref[pl.ds(r0, 8), :] = d_rn
                dpw_ref[...] += d_rn * yn
                gw = d_rn * pw
                dot = jnp.mean(gw * yv, axis=-1, keepdims=True)
                dy_ref[pl.ds(r0, 8), :] = (rs * gw - yv * (rs * rs * rs * dot)).astype(BF16)
                return carry
            lax.fori_loop(0, TM // 8, chunk, 0, unroll=4)

    row = pl.BlockSpec((TM, D), lambda i, k: (i, 0))
    half = pl.BlockSpec((TM, DL), lambda i, k: (i, 0))
    acc = pl.BlockSpec((8, D), lambda i, k: (0, 0))
    return pl.pallas_call(
        body, name="outproj_loss",
        grid=(TP // TM, 2),
        in_specs=[half, half, pl.BlockSpec((DL, D), lambda i, k: (k, 0)), row, pl.BlockSpec(memory_space=pl.ANY),
                  pl.BlockSpec((1, D), lambda i, k: (0, 0))],
        out_specs=[row, row, acc, acc],
        out_shape=[jax.ShapeDtypeStruct((TP, D), F32), jax.ShapeDtypeStruct((TP, D), BF16),
                   jax.ShapeDtypeStruct((8, D), F32), jax.ShapeDtypeStruct((8, D), F32)],
        scratch_shapes=[pltpu.VMEM((TM, D), F32), pltpu.VMEM((TM, D), F32), pltpu.SemaphoreType.DMA(())],
        compiler_params=_cparams(),
    )(ylru, yconf, w_out, h, target, post_w)


_NT = (((1,), (1,)), ((), ()))
_TN = (((0,), (0,)), ((), ()))


def _outproj_bwd(dy, ylru, yconf, w_out):
    def body(dy_ref, yl_ref, yc_ref, w_ref, dycat_ref, dw_ref):
        j = pl.program_id(0)
        dyv = dy_ref[...]
        dycat_ref[...] = lax.dot_general(dyv, w_ref[...], _NT, preferred_element_type=F32)

        @pl.when(j < NCB)
        def _():
            dw_ref[...] = lax.dot_general(yl_ref[...], dyv, _TN, preferred_element_type=F32).astype(BF16)

        @pl.when(j >= NCB)
        def _():
            dw_ref[...] = lax.dot_general(yc_ref[...], dyv, _TN, preferred_element_type=F32).astype(BF16)

    return pl.pallas_call(
        body, name="outproj_bwd",
        grid=(2 * NCB,),
        in_specs=[pl.BlockSpec((TP, D), lambda j: (0, 0)),
                  pl.BlockSpec((TP, CB), lambda j: (0, jnp.minimum(j, NCB - 1))),
                  pl.BlockSpec((TP, CB), lambda j: (0, jnp.maximum(j - NCB, 0))),
                  pl.BlockSpec((CB, D), lambda j: (j, 0))],
        out_specs=[pl.BlockSpec((TP, CB), lambda j: (0, j)), pl.BlockSpec((CB, D), lambda j: (j, 0))],
        out_shape=[jax.ShapeDtypeStruct((TP, D), F32), jax.ShapeDtypeStruct((D, D), BF16)],
        compiler_params=_cparams(),
    )(dy, ylru, yconf, w_out)


_AFTER = pl.BlockSpec(memory_space=pl.ANY)


def _conf_bwd_proj(dycat, p, z, vc, ln_w, ln_b, pw_w, after):
    def body(dy_ref, p_ref, g_ref, vc_ref, lw_ref, lb_ref, w_ref, after_ref,
             dvc_ref, dgc_ref, dpw_ref, vecs_ref, dp_s, s_s, ds_s):
        i = pl.program_id(0)
        lw, lb = lw_ref[...], lb_ref[...]

        @pl.when(i == 0)
        def _():
            dpw_ref[...] = jnp.zeros_like(dpw_ref)
            vecs_ref[...] = jnp.zeros_like(vecs_ref)

        def pre_chunk(ci, carry):
            r0 = pl.multiple_of(ci * R, R)
            for half in range(2):
                rr = r0 + 8 * half
                dyv = dy_ref[pl.ds(rr, 8), :]
                g = g_ref[pl.ds(rr, 8), :]
                sg = _sig(g)
                dp = dyv * (g * sg)
                dg = dyv * p_ref[pl.ds(rr, 8), :] * (sg * (1.0 + g * (1.0 - sg)))
                vecs_ref[0:8, :] += dp
                vecs_ref[8:16, :] += dg
                ds_s[pl.ds(rr, 8), :] = dp
                dvc_ref[pl.ds(rr, 8), :] = dg
            dp_s[pl.ds(r0, R), :] = ds_s[pl.ds(r0, R), :].astype(BF16)
            dgc_ref[pl.ds(r0, R), :] = dvc_ref[pl.ds(r0, R), :].astype(BF16)
            for half in range(2):
                rr = r0 + 8 * half
                _, _, ln = _ln_chunk(vc_ref[pl.ds(rr, 8), :], lw, lb)
                ds_s[pl.ds(rr, 8), :] = ln * _sig(ln)
            s_s[pl.ds(r0, R), :] = ds_s[pl.ds(r0, R), :].astype(BF16)
            return carry
        lax.fori_loop(0, TM // R, pre_chunk, 0, unroll=2)

        dpb = dp_s[...]
        ds_s[...] = lax.dot_general(dpb, w_ref[...], _NT, preferred_element_type=F32)
        dpw_ref[...] += lax.dot_general(s_s[...], dpb, _TN, preferred_element_type=F32)

        def post_chunk(ci, carry):
            r0 = pl.multiple_of(ci * 8, 8)
            xhat, rstd, ln = _ln_chunk(vc_ref[pl.ds(r0, 8), :], lw, lb)
            sl = _sig(ln)
            dln = ds_s[pl.ds(r0, 8), :] * (sl * (1.0 + ln * (1.0 - sl)))
            vecs_ref[16:24, :] += dln * xhat
            vecs_ref[24:32, :] += dln
            dxh = dln * lw
            m1 = jnp.mean(dxh, axis=-1, keepdims=True)
            m2 = jnp.mean(dxh * xhat, axis=-1, keepdims=True)
            dvc_ref[pl.ds(r0, 8), :] = rstd * (dxh - m1 - xhat * m2)
            return carry
        lax.fori_loop(0, TM // 8, post_chunk, 0, unroll=4)

    row = pl.BlockSpec((TM, DC), lambda i: (i, 0))
    vec = pl.BlockSpec((1, DC), lambda i: (0, 0))
    return pl.pallas_call(
        body, name="conf_bwd_proj",
        grid=(TP // TM,),
        in_specs=[pl.BlockSpec((TM, DC), lambda i: (i, 1)), row, pl.BlockSpec((TM, DC), lambda i: (i, 4)), row,
                  vec, vec, pl.BlockSpec((DC, DC), lambda i: (0, 0)), _AFTER],
        out_specs=[row, row, pl.BlockSpec((DC, DC), lambda i: (0, 0)), pl.BlockSpec((32, DC), lambda i: (0, 0))],
        out_shape=[jax.ShapeDtypeStruct((TP, DC), F32), jax.ShapeDtypeStruct((TP, DC), BF16),
                   jax.ShapeDtypeStruct((DC, DC), F32), jax.ShapeDtypeStruct((32, DC), F32)],
        scratch_shapes=[pltpu.VMEM((TM, DC), BF16), pltpu.VMEM((TM, DC), BF16), pltpu.VMEM((TM, DC), F32)],
        compiler_params=_cparams(),
    )(dycat, p, z, vc, ln_w, ln_b, pw_w, after)


def _conf_bwd_conv(dvc, z, dw_w, after):
    def body(dvc_ref, u1_ref, u2_ref, w_ref, after_ref, du_ref, dw_ref, vecs_ref, vs, dvs):
        vs[pl.ds(0, KWP), :] = jnp.zeros((KWP, CBC), F32)
        dvs[pl.ds(TP, KWP), :] = jnp.zeros((KWP, CBC), F32)
        dw_ref[...] = jnp.zeros_like(dw_ref)
        vecs_ref[...] = jnp.zeros_like(vecs_ref)

        def fill_chunk(ci, carry):
            r0 = pl.multiple_of(ci * RC, RC)
            vs[pl.ds(KWP + r0, RC), :] = u1_ref[pl.ds(r0, RC), :] * _sig(u2_ref[pl.ds(r0, RC), :])
            dv = dvc_ref[pl.ds(r0, RC), :]
            dvs[pl.ds(r0, RC), :] = dv
            vecs_ref[0:8, :] += _fold_rows(dv)
            return carry
        lax.fori_loop(0, TP // RC, fill_chunk, 0)

        def conv_chunk(ci, carry):
            r0 = pl.multiple_of(ci * RC, RC)
            vbuf = vs[pl.ds(r0, KWP + RC), :]
            dbuf = dvs[pl.ds(r0, KWP + RC), :]
            dcur = dbuf[0:RC, :]
            dv = jnp.zeros((RC, CBC), F32)
            for rr in range(8):
                vroll = vbuf if rr == 0 else pltpu.roll(vbuf, rr, 0)
                droll = dbuf if rr == 0 else pltpu.roll(dbuf, KWP + RC - rr, 0)
                for q in range(4):
                    s = 8 * q + rr
                    if s > KW - 1:
                        continue
                    k = KW - 1 - s
                    dv = dv + droll[8 * q:8 * q + RC, :] * w_ref[k:k + 1, :]
                    dw_ref[8 * k:8 * k + 8, :] += _fold_rows(dcur * vroll[KWP - 8 * q:KWP - 8 * q + RC, :])
            u1 = u1_ref[pl.ds(r0, RC), :]
            sg = _sig(u2_ref[pl.ds(r0, RC), :])
            du1 = dv * sg
            du2 = dv * u1 * (sg * (1.0 - sg))
            du_ref[0, pl.ds(r0, RC), :] = du1.astype(BF16)
            du_ref[1, pl.ds(r0, RC), :] = du2.astype(BF16)
            vecs_ref[8:16, :] += _fold_rows(du1)
            vecs_ref[16:24, :] += _fold_rows(du2)
            return carry
        lax.fori_loop(0, TP // RC, conv_chunk, 0)

    blk = pl.BlockSpec((TP, CBC), lambda j: (0, j))
    return pl.pallas_call(
        body, name="conf_bwd_conv",
        grid=(NCBC,),
        in_specs=[blk, pl.BlockSpec((TP, CBC), lambda j: (0, 2 * NCBC + j)),
                  pl.BlockSpec((TP, CBC), lambda j: (0, 3 * NCBC + j)), pl.BlockSpec((KWP, CBC), lambda j: (0, j)),
                  _AFTER],
        out_specs=[pl.BlockSpec((2, TP, CBC), lambda j: (0, 0, j)), pl.BlockSpec((8 * KWP, CBC), lambda j: (0, j)),
                   pl.BlockSpec((24, CBC), lambda j: (0, j))],
        out_shape=[jax.ShapeDtypeStruct((2, TP, DC), BF16),
                   jax.ShapeDtypeStruct((8 * KWP, DC), F32), jax.ShapeDtypeStruct((24, DC), F32)],
        scratch_shapes=[pltpu.VMEM((TP + KWP, CBC), F32), pltpu.VMEM((TP + KWP, CBC), F32)],
        compiler_params=_cparams(),
    )(dvc, z, z, dw_w, after)


def _lru_bwd(dycat, z, xc, hs, conv_w, wa_g, b_a, wx_g, b_x, lam, after):
    NV = 6

    def body(dy_ref, x_ref, g_ref, xc_ref, hs_ref, cw_ref, wa_ref, ba_ref, wx_ref, bx_ref, lam_ref, after_ref,
             dzl_ref, dwa_ref, dwx_ref, dcw_ref, vecs_ref, ga_s, gx_s, dxc_s):
        vecs_ref[...] = jnp.zeros_like(vecs_ref)
        dcw_ref[...] = jnp.zeros_like(dcw_ref)
        dxc_s[pl.ds(TP, 8), :] = jnp.zeros((8, CB), F32)

        def gate_chunk(ci, carry):
            r0 = pl.multiple_of(ci * TM, TM)
            xb = xc_ref[pl.ds(r0, TM), :].astype(BF16)
            ga_s[pl.ds(r0, TM), :] = jnp.dot(xb, wa_ref[...], preferred_element_type=F32) + ba_ref[...]
            gx_s[pl.ds(r0, TM), :] = jnp.dot(xb, wx_ref[...], preferred_element_type=F32) + bx_ref[...]
            return carry
        lax.fori_loop(0, TP // TM, gate_chunk, 0)

        sp8 = LRU_C * _softplus(-lam_ref[...])
        row = _row_iota((R, CB))
        nchunk = TP // R

        def scan_chunk(cj, carry):
            a_next, lam_next = carry
            ci = nchunk - 1 - cj
            r0 = pl.multiple_of(ci * R, R)
            dyv = dy_ref[pl.ds(r0, R), :]
            g = g_ref[pl.ds(r0, R), :]
            hv = hs_ref[pl.ds(r0, R), :]
            xc = xc_ref[pl.ds(r0, R), :]
            sg = _sig(g)
            dgl = dyv * hv * (sg * (1.0 + g * (1.0 - sg)))
            dzl_ref[1, pl.ds(r0, R), :] = dgl.astype(BF16)
            vecs_ref[0:8, :] += _fold8(dgl)
            dhs = dyv * (g * sg)
            r, i, a, mult = _gate_values(ga_s[pl.ds(r0, R), :], gx_s[pl.ds(r0, R), :], xc, sp8)
            b = jnp.where(row == R - 1, a_next, pltpu.roll(a, R - 1, 0))
            lv = dhs
            k = 1
            while k < R:
                m = row < R - k
                lv = jnp.where(m, lv + b * pltpu.roll(lv, R - k, 0), lv)
                b = jnp.where(m, b * pltpu.roll(b, R - k, 0), b)
                k *= 2
            lv = lv + b * lam_next
            p0 = pl.multiple_of(jnp.maximum(r0 - 8, 0), 8)
            hprev8 = jnp.where(ci > 0, hs_ref[pl.ds(p0, 8), :], 0.0)
            hprev = pltpu.roll(jnp.concatenate([hprev8, hv], axis=0), 1, 0)[8:8 + R, :]
            da = lv * hprev
            ixc = i * xc
            dmult = lv * ixc
            di = lv * mult * xc
            dxc_s[pl.ds(r0, R), :] = lv * mult * i
            a2 = a * a
            dlog_a = da * a - dmult * a2 / mult
            vecs_ref[32:40, :] += _fold8(dlog_a * r)
            dga = -(dlog_a * sp8) * r * (1.0 - r)
            dgx = di * i * (1.0 - i)
            ga_s[pl.ds(r0, R), :] = dga
            gx_s[pl.ds(r0, R), :] = dgx
            vecs_ref[16:24, :] += _fold8(dga)
            vecs_ref[24:32, :] += _fold8(dgx)
            a_first = jnp.sum(jnp.where(row == 0, a, 0.0), axis=0, keepdims=True)
            l_first = jnp.sum(jnp.where(row == 0, lv, 0.0), axis=0, keepdims=True)
            return a_first, l_first
        lax.fori_loop(0, nchunk // 2, lambda i, cr: scan_chunk(2 * i + 1, scan_chunk(2 * i, cr)),
                      (jnp.zeros((1, CB), F32), jnp.zeros((1, CB), F32)))

        dwa_ref[...] = jnp.zeros_like(dwa_ref)
        dwx_ref[...] = jnp.zeros_like(dwx_ref)

        def mm_chunk(ci, carry):
            r0 = pl.multiple_of(ci * TM, TM)
            xb = xc_ref[pl.ds(r0, TM), :].astype(BF16)
            dgab = ga_s[pl.ds(r0, TM), :].astype(BF16)
            dgxb = gx_s[pl.ds(r0, TM), :].astype(BF16)
            dxc_s[pl.ds(r0, TM), :] += (lax.dot_general(dgab, wa_ref[...], _NT, preferred_element_type=F32)
                                        + lax.dot_general(dgxb, wx_ref[...], _NT, preferred_element_type=F32))
            dwa_ref[...] += lax.dot_general(xb, dgab, _TN, preferred_element_type=F32)
            dwx_ref[...] += lax.dot_general(xb, dgxb, _TN, preferred_element_type=F32)
            return carry
        lax.fori_loop(0, TP // TM, mm_chunk, 0)

        taps = [cw_ref[k:k + 1, :] for k in range(LW)]

        def conv_chunk(ci, carry):
            r0 = pl.multiple_of(ci * R, R)
            dbuf = dxc_s[pl.ds(r0, R + 8), :]
            dcur = dbuf[0:R, :]
            p0 = pl.multiple_of(jnp.maximum(r0 - 8, 0), 8)
            xprev = jnp.where(ci > 0, x_ref[pl.ds(p0, 8), :], 0.0)
            xbuf = jnp.concatenate([xprev, x_ref[pl.ds(r0, R), :]], axis=0)
            dxl = dcur * taps[LW - 1]
            dcw_ref[8 * (LW - 1):8 * LW, :] += _fold8(dcur * xbuf[8:8 + R, :])
            for s in range(1, LW):
                k = LW - 1 - s
                dxl = dxl + pltpu.roll(dbuf, R + 8 - s, 0)[0:R, :] * taps[k]
                dcw_ref[8 * k:8 * k + 8, :] += _fold8(dcur * pltpu.roll(xbuf, s, 0)[8:8 + R, :])
            dzl_ref[0, pl.ds(r0, R), :] = dxl.astype(BF16)
            vecs_ref[8:16, :] += _fold8(dxl)
            vecs_ref[40:48, :] += _fold8(dcur)
            return carry
        lax.fori_loop(0, TP // R, conv_chunk, 0)
        vecs_ref[32:40, :] = vecs_ref[32:40, :] * (LRU_C * _sig(-lam_ref[...]))

    col = lambda off: pl.BlockSpec((TP, CB), lambda j: (0, off + j))
    vec = pl.BlockSpec((1, CB), lambda j: (0, j))
    wsp = pl.BlockSpec((None, CB, CB), lambda j: (j, 0, 0))
    return pl.pallas_call(
        body, name="lru_bwd",
        grid=(NCB,),
        in_specs=[col(0), col(0), col(NCB), col(0), col(0), pl.BlockSpec((LW, CB), lambda j: (0, j)),
                  wsp, vec, wsp, vec, vec, _AFTER],
        out_specs=[pl.BlockSpec((2, TP, CB), lambda j: (0, 0, j)), wsp, wsp,
                   pl.BlockSpec((8 * LW, CB), lambda j: (0, j)), pl.BlockSpec((8 * NV, CB), lambda j: (0, j))],
        out_shape=[jax.ShapeDtypeStruct((2, TP, DL), BF16),
                   jax.ShapeDtypeStruct((NCB, CB, CB), F32), jax.ShapeDtypeStruct((NCB, CB, CB), F32),
                   jax.ShapeDtypeStruct((8 * LW, DL), F32), jax.ShapeDtypeStruct((8 * NV, DL), F32)],
        scratch_shapes=[pltpu.VMEM((TP, CB), F32), pltpu.VMEM((TP, CB), F32), pltpu.VMEM((TP + 8, CB), F32)],
        compiler_params=_cparams(),
    )(dycat, z, z, xc, hs, conv_w, wa_g, b_a, wx_g, b_x, lam, after)


def _dz_section(sec, dzl_ref, dzc_ref, dgc_ref, use):
    @pl.when(sec < 2)
    def _():
        use(dzl_ref)

    @pl.when(jnp.logical_and(sec >= 2, sec < 4))
    def _():
        use(dzc_ref)

    @pl.when(sec == 4)
    def _():
        use(dgc_ref)


def _dz_specs(rows, index):
    return [pl.BlockSpec((None, rows, 1024), lambda a, b: (jnp.minimum(index(a, b)[1], 1), index(a, b)[0], 0)),
            pl.BlockSpec((None, rows, 1024), lambda a, b: (jnp.clip(index(a, b)[1] - 2, 0, 1), index(a, b)[0], 0)),
            pl.BlockSpec((rows, 1024), lambda a, b: (index(a, b)[0], 0))]


def _inproj_wgrad(name, hn, dzs, after):
    KB = 512
    nsec = dzs.shape[0]

    def body(hn_ref, dz_ref, after_ref, dw_ref):
        dw_ref[...] = lax.dot_general(hn_ref[...], dz_ref[...], _TN, preferred_element_type=F32).astype(BF16)

    return pl.pallas_call(
        body, name=name,
        grid=(nsec, D // KB),
        in_specs=[pl.BlockSpec((TP, KB), lambda n, kb: (0, kb)),
                  pl.BlockSpec((None, TP, 1024), lambda n, kb: (n, 0, 0)), _AFTER],
        out_specs=pl.BlockSpec((KB, 1024), lambda n, kb: (kb, n)),
        out_shape=jax.ShapeDtypeStruct((D, nsec * 1024), BF16),
        compiler_params=_cparams(),
    )(hn, dzs, after)


def _sum_win_parts(parts_a, parts_b, parts_c):
    RB = 64

    def body(a_ref, b_ref, c_ref, o_ref):
        def chunk(ci, carry):
            r0 = pl.multiple_of(ci * R, R)
            for ref, base, ncol in ((a_ref, 0, 2048), (b_ref, 2048, 2048), (c_ref, 4096, 1024)):
                for c0 in range(0, ncol, 512):
                    acc = ref[0, pl.ds(r0, R), c0:c0 + 512].astype(F32)
                    for sidx in range(1, NDEV):
                        acc = acc + ref[sidx, pl.ds(r0, R), c0:c0 + 512].astype(F32)
                    o_ref[pl.ds(r0, R), base + c0:base + c0 + 512] = acc.astype(BF16)
            return carry
        lax.fori_loop(0, RB // R, chunk, 0)

    spec = lambda ncol: pl.BlockSpec((NDEV, RB, ncol), lambda i: (0, i, 0))
    return pl.pallas_call(
        body, name="sum_win_parts",
        grid=(D // NDEV // RB,),
        in_specs=[spec(2048), spec(2048), spec(1024)],
        out_specs=pl.BlockSpec((RB, NIN), lambda i: (i, 0)),
        out_shape=jax.ShapeDtypeStruct((D // NDEV, NIN), BF16),
        compiler_params=_cparams(),
    )(parts_a, parts_b, parts_c)


def _inproj_bwd(dzl, dzc, dgc, w_in, h, dout, pre_w, after):
    nsec = NIN // 1024

    def body(dzl_ref, dzc_ref, dgc_ref, w_ref, h_ref, dout_ref, pw_ref, after_ref, gx_hbm, dmeta_ref, dpw_ref,
             acc_s, dh_s, sem):
        i = pl.program_id(0)
        s = pl.program_id(1)

        def gx_copy(t):
            lo, n, off = _tile_rows(t)
            return pltpu.make_async_copy(dh_s.at[pl.ds(off, n)], gx_hbm.at[pl.ds(lo, n)], sem)

        @pl.when(s == 0)
        def _():
            acc_s[...] = jnp.zeros_like(acc_s)

        def use(dz_ref):
            acc_s[...] += lax.dot_general(dz_ref[...], w_ref[...], _NT, preferred_element_type=F32)
        _dz_section(s, dzl_ref, dzc_ref, dgc_ref, use)

        @pl.when(jnp.logical_and(i == 0, s == nsec - 1))
        def _():
            dpw_ref[...] = jnp.zeros_like(dpw_ref)

        @pl.when(s == nsec - 1)
        def _():
            _for_tile(i - 1, lambda t: gx_copy(t).wait())
            pw = pw_ref[...]

            def chunk(ci, carry):
                r0 = pl.multiple_of(ci * 8, 8)
                hv = h_ref[pl.ds(r0, 8), :]
                dhn = acc_s[pl.ds(r0, 8), :]
                rs = lax.rsqrt(jnp.mean(hv * hv, axis=-1, keepdims=True) + EPS)
                dpw_ref[...] += dhn * (hv * rs)
                gw = dhn * pw
                dot = jnp.mean(gw * hv, axis=-1, keepdims=True)
                dh_s[pl.ds(r0, 8), :] = rs * gw - hv * (rs * rs * rs * dot) + dout_ref[pl.ds(r0, 8), :]
                return carry
            lax.fori_loop(0, TM // 8, chunk, 0, unroll=4)
            _for_tile(i, lambda t: gx_copy(t).start())

            @pl.when(i == 0)
            def _():
                dmeta_ref[...] = dh_s[0:NMETA, :]

            @pl.when(i == NTILE - 1)
            def _():
                gx_copy(NTILE - 1).wait()

    row = pl.BlockSpec((TM, D), lambda i, s: (i, 0))
    return pl.pallas_call(
        body, name="inproj_bwd",
        grid=(TP // TM, nsec),
        in_specs=_dz_specs(TM, lambda i, s: (i, s)) + [
            pl.BlockSpec((D, 1024), lambda i, s: (0, s)), row, row, pl.BlockSpec((1, D), lambda i, s: (0, 0)),
            _AFTER],
        out_specs=[pl.BlockSpec(memory_space=pl.ANY), pl.BlockSpec((NMETA, D), lambda i, s: (0, 0)),
                   pl.BlockSpec((8, D), lambda i, s: (0, 0))],
        out_shape=[jax.ShapeDtypeStruct((SEQ, D), F32), jax.ShapeDtypeStruct((NMETA, D), F32),
                   jax.ShapeDtypeStruct((8, D), F32)],
        scratch_shapes=[pltpu.VMEM((TM, D), F32), pltpu.VMEM((TM, D), F32), pltpu.SemaphoreType.DMA(())],
        compiler_params=_cparams(),
    )(dzl, dzc, dgc, w_in, h, dout, pre_w, after)


def _adamw(name, parts, w, m, v, block_rows):
    rows, cols = w.shape
    nparts = parts.shape[0]
    cw = cols if cols <= 640 else 512

    def body(p_ref, w_ref, m_ref, v_ref, g_ref, d_ref, nm_ref, nv_ref):
        def chunk(ci, carry):
            r0 = pl.multiple_of(ci * R, R)
            for c0 in range(0, cols, cw):
                at = (pl.ds(r0, R), slice(c0, c0 + cw))
                g = p_ref[(0,) + at].astype(F32)
                for sidx in range(1, nparts):
                    g = g + p_ref[(sidx,) + at].astype(F32)
                delta, mv, vv = _adam_math(g, w_ref[at], m_ref[at], v_ref[at])
                g_ref[at] = g
                nm_ref[at] = mv
                nv_ref[at] = vv
                d_ref[at] = delta
            return carry
        lax.fori_loop(0, block_rows // R, chunk, 0)

    blk = pl.BlockSpec((block_rows, cols), lambda i: (i, 0))
    shp = jax.ShapeDtypeStruct((rows, cols), F32)
    return pl.pallas_call(
        body, name=name,
        grid=(rows // block_rows,),
        in_specs=[pl.BlockSpec((nparts, block_rows, cols), lambda i: (0, i, 0)), blk, blk, blk],
        out_specs=[blk, blk, blk, blk],
        out_shape=[shp, shp, shp, shp],
        compiler_params=_cparams(),
    )(parts, w, m, v)


def _adam_math(g, w, m, v):
    c1 = 1.0 / (1.0 - ADAM_B1 ** ADAM_STEP)
    c2 = 1.0 / (1.0 - ADAM_B2 ** ADAM_STEP)
    mv = ADAM_B1 * m + (1.0 - ADAM_B1) * g
    vv = ADAM_B2 * v + (1.0 - ADAM_B2) * (g * g)
    upd = (mv * c1) / (jnp.sqrt(vv * c2) + ADAM_EPS) + ADAM_WD * w
    return -ADAM_LR * upd, mv, vv


_VEC = [("pre_norm_w", 2), ("post_norm_w", 2), ("b_in", 5), ("lru_conv_b", 1), ("b_gate_a", 1), ("b_gate_x", 1),
        ("lru_lambda", 1), ("conf_dw_b", 1), ("conf_ln_w", 1), ("conf_ln_b", 1), ("conf_pw_b", 1)]
_VEC_ROWS = 24
_LOSS_ROW = 17
_SM_ROWS = 64


def _pack_grads(dprew_acc, dpostw_acc, cvecs, kvecs, lvecs, dcw_acc, ddw_acc, dh, loss_acc):
    def body(pre_ref, post_ref, c_ref, k_ref, l_ref, dcw_ref, ddw_ref, dh_ref, loss_ref, vec_ref, small_ref, tmp):
        s8 = lambda ref, r: jnp.sum(ref[8 * r:8 * r + 8, :], axis=0, keepdims=True)
        vec_ref[...] = jnp.zeros_like(vec_ref)
        pre, post = s8(pre_ref, 0), s8(post_ref, 0)
        rows = [pre[:, 0:1024], pre[:, 1024:2048], post[:, 0:1024], post[:, 1024:2048],
                s8(l_ref, 1), s8(l_ref, 0), s8(k_ref, 1), s8(k_ref, 2), s8(c_ref, 1),
                s8(l_ref, 5), s8(l_ref, 2), s8(l_ref, 3), s8(l_ref, 4),
                s8(k_ref, 0), s8(c_ref, 2), s8(c_ref, 3), s8(c_ref, 0)]
        for r, val in enumerate(rows):
            vec_ref[r:r + 1, :] = val
        vec_ref[_LOSS_ROW:_LOSS_ROW + 1, :] = jnp.zeros((1, 1024), F32) + (0.5 / D) * jnp.sum(loss_ref[...])

        small_ref[...] = jnp.zeros_like(small_ref)
        for k in range(LW):
            tmp[k:k + 1, :] = s8(dcw_ref, k)
        for k in range(KW):
            tmp[8 + k:9 + k, :] = s8(ddw_ref, k)
        for d in range(NDEV):
            small_ref[d, 0:LW, 0:128] = tmp[0:LW, 128 * d:128 * d + 128]
            small_ref[d, 8:8 + KW, 0:128] = tmp[8:8 + KW, 128 * d:128 * d + 128]
            small_ref[d, 40:56, :] = dh_ref[:, 256 * d:256 * d + 256]

    full = lambda a: pl.BlockSpec(a.shape, lambda i: (0,) * a.ndim)
    ins = [dprew_acc, dpostw_acc, cvecs, kvecs, lvecs, dcw_acc, ddw_acc]
    return pl.pallas_call(
        body, name="pack_grads",
        grid=(1,),
        in_specs=[full(a) for a in ins] + [full(dh), full(loss_acc)],
        out_specs=[pl.BlockSpec((_VEC_ROWS, 1024), lambda i: (0, 0)),
                   pl.BlockSpec((NDEV, _SM_ROWS, 256), lambda i: (0, 0, 0))],
        out_shape=[jax.ShapeDtypeStruct((_VEC_ROWS, 1024), F32), jax.ShapeDtypeStruct((NDEV, _SM_ROWS, 256), F32)],
        scratch_shapes=[pltpu.VMEM((40, 1024), F32)],
        compiler_params=_cparams(),
    )(*ins, dh, loss_acc)


def _adamw_vec(parts, W, M, V):
    nv = len(_VEC)

    def body(*refs):
        p_ref = refs[0]
        w_refs, m_refs, v_refs = refs[1:1 + nv], refs[1 + nv:1 + 2 * nv], refs[1 + 2 * nv:1 + 3 * nv]
        outs = refs[1 + 3 * nv:]

        def total(r):
            acc = p_ref[0, r:r + 1, :]
            for sidx in range(1, NDEV):
                acc = acc + p_ref[sidx, r:r + 1, :]
            return acc

        row = 0
        for idx, (_, nrows) in enumerate(_VEC):
            for part in range(nrows):
                cols = slice(1024 * part, 1024 * part + 1024)
                g = total(row + part)
                delta, mv, vv = _adam_math(g, w_refs[idx][:, cols], m_refs[idx][:, cols], v_refs[idx][:, cols])
                for o, val in zip(outs[4 * idx:4 * idx + 4], (g, delta, mv, vv)):
                    o[:, cols] = val
            row += nrows
        outs[-1][...] = total(_LOSS_ROW)[:, 0:128]

    names = [n for n, _ in _VEC]
    flat = lambda d: [d[n].reshape(1, -1) for n in names]
    ws, ms, vs = flat(W), flat(M), flat(V)
    res = pl.pallas_call(
        body, name="adamw_vec",
        out_shape=[jax.ShapeDtypeStruct(w.shape, F32) for w in ws for _ in range(4)]
        + [jax.ShapeDtypeStruct((1, 128), F32)],
        compiler_params=_cparams(),
    )(parts, *ws, *ms, *vs)
    return {n: tuple(res[4 * i:4 * i + 4]) for i, n in enumerate(names)}, res[-1]


def _adamw_small(parts, W, M, V):
    where = {"lru_conv_w": (slice(0, LW), slice(0, 128)), "conf_dw_w": (slice(8, 8 + KW), slice(0, 128)),
             "meta_tokens": (slice(40, 56), slice(0, 256))}
    names = list(where)

    def body(*refs):
        p_ref = refs[0]
        outs = refs[10:]
        for idx, n in enumerate(names):
            rs, cs = where[n]
            g = p_ref[0, rs, cs]
            for sidx in range(1, NDEV):
                g = g + p_ref[sidx, rs, cs]
            delta, mv, vv = _adam_math(g, refs[1 + idx][...], refs[4 + idx][...], refs[7 + idx][...])
            for o, val in zip(outs[4 * idx:4 * idx + 4], (g, delta, mv, vv)):
                o[...] = val

    two_d = lambda a: a.reshape(a.shape[-2:])
    ws, ms, vs = ([two_d(d[n]) for n in names] for d in (W, M, V))
    res = pl.pallas_call(
        body, name="adamw_small",
        out_shape=[jax.ShapeDtypeStruct(w.shape, F32) for w in ws for _ in range(4)],
        compiler_params=_cparams(),
    )(parts, *ws, *ms, *vs)
    return {n: tuple(res[4 * i:4 * i + 4]) for i, n in enumerate(names)}


def _pack_small(lru_cw, dw_w, meta):
    buf = jnp.zeros((_SM_ROWS, 256), F32)
    buf = buf.at[0:LW, 0:128].set(lru_cw)
    buf = buf.at[8:8 + dw_w.shape[0], 0:128].set(dw_w)
    return buf.at[40:56, :].set(meta)


def _block_diag4(w):
    w4 = w.reshape(NCB, 4, 64, 64)
    eye = jnp.eye(4, dtype=w.dtype)
    return jnp.einsum("ghij,hk->ghikj", w4, eye).reshape(NCB, CB, CB)


def _diag_blocks(g):
    g5 = g.reshape(NCB, 4, 64, 4, 64)
    return jnp.stack([g5[:, hh, :, hh, :] for hh in range(4)], axis=1).reshape(16, 64, 64)


def _gate_mats(W):
    return _block_diag4(W["w_gate_a"][0]).astype(BF16), _block_diag4(W["w_gate_x"][0]).astype(BF16)


def _local_step(x, target, meta_full, inproj, out_weights, lru_cw_full, dw_w_full, W, gate_mats, send):
    wa_g, wx_g = gate_mats

    h, hn = _prenorm(x, meta_full, W["pre_norm_w"])
    z, win_full = inproj(hn)
    ylru, xc, hs = _lru_fwd(z, lru_cw_full, W["lru_conv_b"], wa_g, W["b_gate_a"], wx_g, W["b_gate_x"],
                            W["lru_lambda"])
    vc = _conf_fwd_conv(z, dw_w_full, W["conf_dw_b"])
    wout_full, pw_full = out_weights(vc)
    yconf, p = _conf_fwd_proj(vc, z, W["conf_ln_w"], W["conf_ln_b"], pw_full, W["conf_pw_b"])
    dout, dy, loss_acc, dpostw_acc = _outproj_loss(ylru, yconf, wout_full, h, target, W["post_norm_w"])

    dycat, dwout_part = _outproj_bwd(dy, ylru, yconf, wout_full)
    tok = send("w_out", ("w_out", dwout_part))
    dvc, dgc, dpw_part, cvecs = _conf_bwd_proj(dycat, p, z, vc, W["conf_ln_w"], W["conf_ln_b"], pw_full, tok)
    tok = send("w_in_c", ("conf_pw_w", dpw_part), ("w_in_c", _inproj_wgrad("inproj_wgrad_c", hn, dgc[None], dgc)))
    dzc, ddw_acc, kvecs = _conf_bwd_conv(dvc, z, dw_w_full, tok)
    tok = send("w_in_b", ("w_in_b", _inproj_wgrad("inproj_wgrad_b", hn, dzc, dzc)))
    dzl, dwa_g, dwx_g, dcw_acc, lvecs = _lru_bwd(dycat, z, xc, hs, lru_cw_full, wa_g, W["b_gate_a"], wx_g,
                                                 W["b_gate_x"], W["lru_lambda"], tok)
    tok = send("w_gates", ("w_gate_a", _diag_blocks(dwa_g).reshape(16 * 64, 64)),
               ("w_gate_x", _diag_blocks(dwx_g).reshape(16 * 64, 64)))
    tok = send("w_in_a", ("w_in_a", _inproj_wgrad("inproj_wgrad_a", hn, dzl, tok)))
    grad_x, dmeta, dprew_acc = _inproj_bwd(dzl, dzc, dgc, win_full, h, dout, W["pre_norm_w"], tok)

    vec_pack, small_part = _pack_grads(dprew_acc, dpostw_acc, cvecs, kvecs, lvecs, dcw_acc, ddw_acc, dmeta, loss_acc)
    return grad_x, vec_pack, small_part


def kernel(x, meta_tokens, pre_norm_w, post_norm_w, w_in, b_in, lru_conv_w, lru_conv_b, w_gate_a, b_gate_a, w_gate_x, b_gate_x, lru_lambda, conf_dw_w, conf_dw_b, conf_ln_w, conf_ln_b, conf_pw_w, conf_pw_b, w_out, loss_target, m_meta_tokens, m_pre_norm_w, m_post_norm_w, m_w_in, m_b_in, m_lru_conv_w, m_lru_conv_b, m_w_gate_a, m_b_gate_a, m_w_gate_x, m_b_gate_x, m_lru_lambda, m_conf_dw_w, m_conf_dw_b, m_conf_ln_w, m_conf_ln_b, m_conf_pw_w, m_conf_pw_b, m_w_out, v_meta_tokens, v_pre_norm_w, v_post_norm_w, v_w_in, v_b_in, v_lru_conv_w, v_lru_conv_b, v_w_gate_a, v_b_gate_a, v_w_gate_x, v_b_gate_x, v_lru_lambda, v_conf_dw_w, v_conf_dw_b, v_conf_ln_w, v_conf_ln_b, v_conf_pw_w, v_conf_pw_b, v_w_out):
    W = dict(meta_tokens=meta_tokens, pre_norm_w=pre_norm_w, post_norm_w=post_norm_w, w_in=w_in, b_in=b_in,
             lru_conv_w=lru_conv_w, lru_conv_b=lru_conv_b, w_gate_a=w_gate_a, b_gate_a=b_gate_a,
             w_gate_x=w_gate_x, b_gate_x=b_gate_x, lru_lambda=lru_lambda, conf_dw_w=conf_dw_w,
             conf_dw_b=conf_dw_b, conf_ln_w=conf_ln_w, conf_ln_b=conf_ln_b, conf_pw_w=conf_pw_w,
             conf_pw_b=conf_pw_b, w_out=w_out)
    M = dict(meta_tokens=m_meta_tokens, pre_norm_w=m_pre_norm_w, post_norm_w=m_post_norm_w, w_in=m_w_in,
             b_in=m_b_in, lru_conv_w=m_lru_conv_w, lru_conv_b=m_lru_conv_b, w_gate_a=m_w_gate_a,
             b_gate_a=m_b_gate_a, w_gate_x=m_w_gate_x, b_gate_x=m_b_gate_x, lru_lambda=m_lru_lambda,
             conf_dw_w=m_conf_dw_w, conf_dw_b=m_conf_dw_b, conf_ln_w=m_conf_ln_w, conf_ln_b=m_conf_ln_b,
             conf_pw_w=m_conf_pw_w, conf_pw_b=m_conf_pw_b, w_out=m_w_out)
    V = dict(meta_tokens=v_meta_tokens, pre_norm_w=v_pre_norm_w, post_norm_w=v_post_norm_w, w_in=v_w_in,
             b_in=v_b_in, lru_conv_w=v_lru_conv_w, lru_conv_b=v_lru_conv_b, w_gate_a=v_w_gate_a,
             b_gate_a=v_b_gate_a, w_gate_x=v_w_gate_x, b_gate_x=v_b_gate_x, lru_lambda=v_lru_lambda,
             conf_dw_w=v_conf_dw_w, conf_dw_b=v_conf_dw_b, conf_ln_w=v_conf_ln_w, conf_ln_b=v_conf_ln_b,
             conf_pw_w=v_conf_pw_w, conf_pw_b=v_conf_pw_b, w_out=v_w_out)
    names = list(W.keys())
    shapes = {n: W[n].shape for n in names}

    small = _pack_small(lru_conv_w[0], conf_dw_w[0], meta_tokens)
    (small_flight,), tok = _exchange_start("gather_small_start", [
        (small, jax.ShapeDtypeStruct((NDEV, _SM_ROWS, 256), F32), _whole, _slot)])
    win_flight, tok = _win_gather_start(w_in[0].astype(BF16) + tok[0, 0].astype(BF16))
    gate_mats = _gate_mats(W)
    wout_shard = w_out[0].astype(BF16) + tok[0, 0].astype(BF16)
    pw_shard = conf_pw_w[0].astype(BF16)
    cast_done = (gate_mats[0][0, 0:8, 0:128] + gate_mats[1][0, 0:8, 0:128]
                 + wout_shard[0:8, 0:128] + pw_shard[0:8, 0:128])
    win_flight, tok = _win_gather_links(win_flight, cast_done)
    gathered, tok = _exchange_start("gather_out_start", [
        (wout_shard + tok[0, 0].astype(BF16), jax.ShapeDtypeStruct((D, D), BF16), _whole, _rows(D // NDEV)),
        (pw_shard, jax.ShapeDtypeStruct((DC, DC), BF16), _whole, _rows(DC // NDEV)),
    ])
    (small_all,) = _exchange_wait("gather_small_wait", [small_flight], tok)
    unshard = lambda a: jnp.transpose(a, (1, 0, 2)).reshape(a.shape[1], -1)
    lru_cw_full = unshard(small_all[:, 0:LW, 0:128])
    dw_w_full = unshard(small_all[:, 8:8 + KWP, 0:128])
    meta_full = unshard(small_all[:, 40:56, :])

    def out_weights(after):
        return _exchange_wait("gather_out_wait", gathered, after)

    def inproj(hn):
        xi, yi, ci = lax.axis_index("x"), lax.axis_index("y"), lax.axis_index("c")
        shard = lambda px, py, pc: (4 * px + 2 * py + pc).astype(jnp.int32)
        over_links = jnp.stack([shard(1 - xi, yi, ci), shard(xi, 1 - yi, ci), shard(1 - xi, 1 - yi, ci)])
        z, src = _inproj_cols("inproj_own", jnp.stack([shard(xi, yi, ci)]), hn, win_flight["src"], b_in, None)
        flight = _win_gather_early(dict(win_flight, src=src))
        z, land = _inproj_cols("inproj_here", jnp.stack([shard(xi, yi, 1 - ci)]), hn, flight["land"], b_in, z)
        flight = _win_gather_forward(dict(flight, land=land), z)
        z, land = _inproj_cols("inproj_links", over_links, hn, flight["land"], b_in, z)
        land = _win_gather_wait(dict(flight, land=land))
        return _inproj_cols("inproj_sibling", over_links + 1 - 2 * ci, hn, land, b_in, z)

    row_stage = lambda ncol: (jax.ShapeDtypeStruct((NDEV, D // NDEV, ncol), BF16), _rows(D // NDEV))
    piece = {"w_in_a": row_stage(2048), "w_in_b": row_stage(2048), "w_in_c": row_stage(1024),
             "w_out": row_stage(D),
             "conf_pw_w": (jax.ShapeDtypeStruct((NDEV, DC // NDEV, DC), BF16), _rows(DC // NDEV)),
             "w_gate_a": (jax.ShapeDtypeStruct((NDEV, 16 * 64, 64), BF16), _whole),
             "w_gate_x": (jax.ShapeDtypeStruct((NDEV, 16 * 64, 64), BF16), _whole)}
    sent = {}

    def send(call, *named_parts):
        handles, token = _exchange_start(
            "scatter_" + call + "_start",
            [(part.astype(BF16), piece[name][0], piece[name][1], _slot) for name, part in named_parts])
        for (name, _), handle in zip(named_parts, handles):
            sent[name] = [handle]
        return token

    grad_x, vec_pack, small_part = _local_step(
        x[0], loss_target[0], meta_full, inproj, out_weights, lru_cw_full, dw_w_full, W, gate_mats, send)
    grad_x = grad_x[None]

    rest, tok = _exchange_start("scatter_rest_start", [
        (small_part, jax.ShapeDtypeStruct((NDEV, _SM_ROWS, 256), F32), _slot, _slot),
        (vec_pack, jax.ShapeDtypeStruct((NDEV, _VEC_ROWS, 1024), F32), _whole, _slot),
    ])
    (parts_c,) = _exchange_wait("scatter_w_in_c_wait", sent["w_in_c"], tok)
    (parts_b,) = _exchange_wait("scatter_w_in_b_wait", sent["w_in_b"], parts_c)
    (parts_a,) = _exchange_wait("scatter_w_in_a_wait", sent["w_in_a"], parts_b)
    win_rows = _sum_win_parts(parts_a, parts_b, parts_c)
    win_stage2, tok = _exchange_start("scatter_w_in_stage2_start", [
        (win_rows, jax.ShapeDtypeStruct((NDEV, D // NDEV, NIN // NDEV), BF16), _cols(NIN // NDEV), _slot)])

    G, DW, NM, NV = {}, {}, {}, {}
    (wout_parts,) = _exchange_wait("scatter_w_out_wait", sent["w_out"], tok)
    G["w_out"], DW["w_out"], NM["w_out"], NV["w_out"] = _adamw("adamw_w_out", wout_parts, w_out[0], m_w_out[0], v_w_out[0], 64)
    (pw_parts,) = _exchange_wait("scatter_conf_pw_w_wait", sent["conf_pw_w"], G["w_out"])
    G["conf_pw_w"], DW["conf_pw_w"], NM["conf_pw_w"], NV["conf_pw_w"] = _adamw(
        "adamw_pw", pw_parts, conf_pw_w[0], m_conf_pw_w[0], v_conf_pw_w[0], 128)
    res = {}
    wa_parts, wx_parts = _exchange_wait("scatter_w_gates_wait", sent["w_gate_a"] + sent["w_gate_x"], G["conf_pw_w"])
    for n, parts in (("w_gate_a", wa_parts), ("w_gate_x", wx_parts)):
        res[n] = _adamw("adamw_" + n, parts, *[d[n].reshape(16 * 64, 64) for d in (W, M, V)], 16 * 64)
    small_parts, vec_parts = _exchange_wait("scatter_rest_wait", rest, res["w_gate_x"][0])
    res.update(_adamw_small(small_parts, W, M, V))
    vec_res, loss_row = _adamw_vec(vec_parts, W, M, V)
    res.update(vec_res)
    (win_sum,) = _exchange_wait("scatter_w_in_stage2_wait", win_stage2, loss_row)
    res["w_in"] = _adamw("adamw_w_in", win_sum.reshape(1, D, NIN // NDEV), w_in[0], m_w_in[0], v_w_in[0], 256)
    for n, vals in res.items():
        for dst, val in zip((G, DW, NM, NV), vals):
            dst[n] = val
    for dst in (G, DW, NM, NV):
        for n in names:
            dst[n] = dst[n].reshape(shapes[n])
    loss = loss_row[0, 0]

    return (loss, grad_x, *[G[n] for n in names], *[DW[n] for n in names],
            *[NM[n] for n in names], *[NV[n] for n in names])
```

```python
import functools

import jax
import jax.numpy as jnp
from jax import lax
from jax.experimental import pallas as pl
from jax.experimental.pallas import tpu as pltpu

F32 = jnp.float32
BF16 = jnp.bfloat16

D = 2048
DL = 1024
DC = 1024
NIN = 5120
NMETA = 16
SEQ = 2048
T = NMETA + SEQ
TP = 2176
TM = 544
CB = 256
NCB = DL // CB
R = 16
KW = 31
KWP = 32
LW = 4
LRU_C = 8.0
EPS = 1e-6
NDEV = 8

ADAM_LR = 0.001
ADAM_B1 = 0.9
ADAM_B2 = 0.999
ADAM_EPS = 1e-08
ADAM_WD = 0.01
ADAM_STEP = 10

VMEM_LIMIT = 56 * 1024 * 1024


def _cparams():
    return pltpu.CompilerParams(vmem_limit_bytes=VMEM_LIMIT)


def _sig(x):
    return 1.0 / (1.0 + jnp.exp(-x))


def _expm1_neg(y):
    poly = y * (1.0 + y * (0.5 + y * (1.0 / 6.0 + y * (1.0 / 24.0 + y * (1.0 / 120.0)))))
    return jnp.where(y > -0.1, poly, jnp.exp(y) - 1.0)


def _softplus(x):
    e = jnp.exp(-jnp.abs(x))
    w = 1.0 + e
    l1p = jnp.where(w == 1.0, e, jnp.log(w) * e / (w - 1.0))
    return jnp.maximum(x, 0.0) + l1p


def _row_iota(shape):
    return lax.broadcasted_iota(jnp.int32, shape, 0)


def _fold8(v):
    return v[0:8, :] + v[8:16, :]


_FLIPS = [(k >> 2 & 1, k >> 1 & 1, k & 1) for k in range(1, NDEV)]
_HBM = pl.BlockSpec(memory_space=pltpu.HBM)
_SEM = pl.BlockSpec(memory_space=pltpu.SEMAPHORE)


def _peers():
    x, y, c = lax.axis_index("x"), lax.axis_index("y"), lax.axis_index("c")
    out = []
    for dx, dy, dc in _FLIPS:
        px = 1 - x if dx else x
        py = 1 - y if dy else y
        pc = 1 - c if dc else c
        out.append(((px, py, pc), 4 * px + 2 * py + pc))
    return 4 * x + 2 * y + c, out


def _exchange_start(name, items):
    n = len(items)

    def body(*refs):
        srcs, lands = refs[:n], refs[n:2 * n]
        outs = refs[2 * n:]
        send_sems, recv_sems, local_sems = outs[:n], outs[n:2 * n], outs[2 * n:3 * n]
        token = outs[-1]
        me, peers = _peers()
        for a in range(n):
            src_at, dst_at = items[a][2], items[a][3]
            pltpu.make_async_copy(src_at(srcs[a], me), dst_at(lands[a], me), local_sems[a]).start()
        for a in range(n):
            src_at, dst_at = items[a][2], items[a][3]
            for k, (pos, peer) in enumerate(peers):
                pltpu.make_async_remote_copy(
                    src_ref=src_at(srcs[a], peer), dst_ref=dst_at(lands[a], me),
                    send_sem=send_sems[a].at[k], recv_sem=recv_sems[a].at[k],
                    device_id=pos, device_id_type=pl.DeviceIdType.MESH).start()
        token[...] = jnp.zeros_like(token)

    srcs = [pltpu.with_memory_space_constraint(it[0], pltpu.HBM) for it in items]
    lands = [pltpu.with_memory_space_constraint(lax.empty(it[1].shape, it[1].dtype), pltpu.HBM) for it in items]
    sem7 = pltpu.SemaphoreType.DMA((NDEV - 1,))
    res = pl.pallas_call(
        body, name=name,
        out_shape=([sem7] * (2 * n) + [pltpu.SemaphoreType.DMA(())] * n
                   + [pltpu.HBM(a.shape, a.dtype) for a in srcs] + [pltpu.HBM(a.shape, a.dtype) for a in lands]
                   + [jax.ShapeDtypeStruct((8, 128), F32)]),
        in_specs=[_HBM] * (2 * n),
        out_specs=[_SEM] * (3 * n) + [_HBM] * (2 * n) + [pl.BlockSpec(memory_space=pltpu.VMEM)],
        input_output_aliases={i: 3 * n + i for i in range(2 * n)},
        compiler_params=pltpu.CompilerParams(has_side_effects=pltpu.SideEffectType.DATAFLOW_SIDE_EFFECTING),
    )(*srcs, *lands)
    handles = [dict(send=res[a], recv=res[n + a], local=res[2 * n + a], src=res[3 * n + a], land=res[4 * n + a],
                    src_at=items[a][2], dst_at=items[a][3]) for a in range(n)]
    return handles, res[-1]


def _wait_bytes(piece, sem):
    pltpu.make_async_copy(piece, piece, sem).wait()


def _exchange_wait(name, handles, after):
    n = len(handles)

    def body(*refs):
        srcs, lands = refs[:n], refs[n:2 * n]
        send_sems, recv_sems, local_sems = refs[2 * n:3 * n], refs[3 * n:4 * n], refs[4 * n:5 * n]
        me, peers = _peers()
        for a in range(n):
            src_at, dst_at = handles[a]["src_at"], handles[a]["dst_at"]
            for k, (pos, peer) in enumerate(peers):
                _wait_bytes(src_at(srcs[a], peer), send_sems[a].at[k])
                _wait_bytes(dst_at(lands[a], peer), recv_sems[a].at[k])
            pltpu.make_async_copy(src_at(srcs[a], me), dst_at(lands[a], me), local_sems[a]).wait()

    srcs = [hd["src"] for hd in handles]
    lands = [hd["land"] for hd in handles]
    res = pl.pallas_call(
        body, name=name,
        out_shape=[pltpu.HBM(a.shape, a.dtype) for a in srcs] + [pltpu.HBM(a.shape, a.dtype) for a in lands],
        in_specs=[_HBM] * (2 * n) + [_SEM] * (3 * n) + [pl.BlockSpec(memory_space=pl.ANY)],
        out_specs=[_HBM] * (2 * n),
        input_output_aliases={i: i for i in range(2 * n)},
        compiler_params=pltpu.CompilerParams(has_side_effects=pltpu.SideEffectType.DATAFLOW_SIDE_EFFECTING),
    )(*srcs, *lands, *[hd["send"] for hd in handles], *[hd["recv"] for hd in handles],
      *[hd["local"] for hd in handles], after)
    return list(res[n:])


_SIDE = pltpu.SideEffectType.DATAFLOW_SIDE_EFFECTING
_WCOLS = NIN // NDEV


def _win_cols(ref, l):
    return ref.at[:, pl.ds(pl.multiple_of(l * _WCOLS, 128), _WCOLS)]


def _win_routes():
    x, y, c = lax.axis_index("x"), lax.axis_index("y"), lax.axis_index("c")
    pos = [(x, y, 1 - c), (1 - x, y, c), (x, 1 - y, c), (1 - x, 1 - y, c)]
    return 4 * x + 2 * y + c, [(p, 4 * p[0] + 2 * p[1] + p[2]) for p in pos]


def _win_gather_start(shard):
    def body(src, land, send_sem, recv_sem, local_sem, src_thru, land_thru, token):
        me, routes = _win_routes()
        pltpu.make_async_copy(src, _win_cols(land, me), local_sem).start()
        pltpu.make_async_remote_copy(src_ref=src, dst_ref=_win_cols(land, me), send_sem=send_sem, recv_sem=recv_sem,
                                     device_id=routes[0][0], device_id_type=pl.DeviceIdType.MESH).start()
        token[...] = jnp.zeros_like(token)

    src = pltpu.with_memory_space_constraint(shard, pltpu.HBM)
    land = pltpu.with_memory_space_constraint(lax.empty((D, NIN), BF16), pltpu.HBM)
    sem = pltpu.SemaphoreType.DMA(())
    res = pl.pallas_call(
        body, name="win_gather_start",
        out_shape=[sem, sem, sem, pltpu.HBM(src.shape, BF16), pltpu.HBM(land.shape, BF16),
                   jax.ShapeDtypeStruct((8, 128), F32)],
        in_specs=[_HBM, _HBM],
        out_specs=[_SEM, _SEM, _SEM, _HBM, _HBM, pl.BlockSpec(memory_space=pltpu.VMEM)],
        input_output_aliases={0: 3, 1: 4},
        compiler_params=pltpu.CompilerParams(has_side_effects=_SIDE),
    )(src, land)
    return dict(send0=res[0], recv0=res[1], local=res[2], src=res[3], land=res[4]), res[5]


def _win_gather_links(hd, after):
    def body(src, land, after_ref, send_sems, recv_sems, src_thru, land_thru, token):
        me, routes = _win_routes()
        for k in (1, 2, 3):
            pltpu.make_async_remote_copy(src_ref=src, dst_ref=_win_cols(land, me), send_sem=send_sems.at[k - 1],
                                         recv_sem=recv_sems.at[k - 1], device_id=routes[k][0],
                                         device_id_type=pl.DeviceIdType.MESH).start()
        token[...] = jnp.zeros_like(token)

    sem3 = pltpu.SemaphoreType.DMA((3,))
    res = pl.pallas_call(
        body, name="win_gather_links",
        out_shape=[sem3, sem3, pltpu.HBM(hd["src"].shape, BF16), pltpu.HBM(hd["land"].shape, BF16),
                   jax.ShapeDtypeStruct((8, 128), F32)],
        in_specs=[_HBM, _HBM, pl.BlockSpec(memory_space=pl.ANY)],
        out_specs=[_SEM, _SEM, _HBM, _HBM, pl.BlockSpec(memory_space=pltpu.VMEM)],
        input_output_aliases={0: 2, 1: 3},
        compiler_params=pltpu.CompilerParams(has_side_effects=_SIDE),
    )(hd["src"], hd["land"], after)
    return dict(hd, send=res[0], recv=res[1], src=res[2], land=res[3]), res[4]


def _win_gather_forward(hd, after):
    def body(land, recv_sems, after_ref, land_thru, fsend_sems, frecv_sems, token):
        token[...] = jnp.zeros_like(token)
        me, routes = _win_routes()
        sibling = routes[0][0]
        for k in (1, 2, 3):
            pos, peer = routes[k]
            piece = _win_cols(land, peer)
            pltpu.make_async_remote_copy(src_ref=piece, dst_ref=piece, send_sem=fsend_sems.at[k - 1],
                                         recv_sem=recv_sems.at[k - 1], device_id=pos,
                                         device_id_type=pl.DeviceIdType.MESH).wait_recv()
            pltpu.make_async_remote_copy(src_ref=piece, dst_ref=piece, send_sem=fsend_sems.at[k - 1],
                                         recv_sem=frecv_sems.at[k - 1], device_id=sibling,
                                         device_id_type=pl.DeviceIdType.MESH).start()

    sem3 = pltpu.SemaphoreType.DMA((3,))
    res = pl.pallas_call(
        body, name="win_gather_forward",
        out_shape=[pltpu.HBM(hd["land"].shape, BF16), sem3, sem3, jax.ShapeDtypeStruct((8, 128), F32)],
        in_specs=[_HBM, _SEM, pl.BlockSpec(memory_space=pl.ANY)],
        out_specs=[_HBM, _SEM, _SEM, pl.BlockSpec(memory_space=pltpu.VMEM)],
        input_output_aliases={0: 0},
        compiler_params=pltpu.CompilerParams(has_side_effects=_SIDE),
    )(hd["land"], hd["recv"], after)
    return dict(hd, land=res[0], fsend=res[1], frecv=res[2]), res[3]


def _win_gather_early(hd):
    def body(src, land, recv_sem, local_sem, src_thru, land_thru):
        me, routes = _win_routes()
        _wait_bytes(_win_cols(land, routes[0][1]), recv_sem)
        pltpu.make_async_copy(src, _win_cols(land, me), local_sem).wait()

    res = pl.pallas_call(
        body, name="win_gather_early",
        out_shape=[pltpu.HBM(hd["src"].shape, BF16), pltpu.HBM(hd["land"].shape, BF16)],
        in_specs=[_HBM, _HBM, _SEM, _SEM],
        out_specs=[_HBM, _HBM],
        input_output_aliases={0: 0, 1: 1},
        compiler_params=pltpu.CompilerParams(has_side_effects=_SIDE),
    )(hd["src"], hd["land"], hd["recv0"], hd["local"])
    return dict(hd, src=res[0], land=res[1])


def _win_gather_wait(hd):
    def body(src, land, send0_sem, send_sems, fsend_sems, frecv_sems, src_thru, land_thru, token):
        token[...] = jnp.zeros_like(token)
        me, routes = _win_routes()
        sib_pos, sibling = routes[0]
        for k in range(4):
            _wait_bytes(src, send0_sem if k == 0 else send_sems.at[k - 1])
        for k in (1, 2, 3):
            _wait_bytes(_win_cols(land, routes[k][1]), fsend_sems.at[k - 1])
            _wait_bytes(_win_cols(land, 4 * routes[k][0][0] + 2 * routes[k][0][1] + sib_pos[2]), frecv_sems.at[k - 1])

    res = pl.pallas_call(
        body, name="win_gather_wait",
        out_shape=[pltpu.HBM(hd["src"].shape, BF16), pltpu.HBM(hd["land"].shape, BF16),
                   jax.ShapeDtypeStruct((8, 128), F32)],
        in_specs=[_HBM, _HBM] + [_SEM] * 4,
        out_specs=[_HBM, _HBM, pl.BlockSpec(memory_space=pltpu.VMEM)],
        input_output_aliases={0: 0, 1: 1},
        compiler_params=pltpu.CompilerParams(has_side_effects=_SIDE),
    )(hd["src"], hd["land"], hd["send0"], hd["send"], hd["fsend"], hd["frecv"])
    return res[1], res[2]


def _whole(ref, l):
    return ref


def _slot(ref, l):
    return ref.at[l]


def _cols(width):
    def at(ref, l):
        return ref.at[:, pl.ds(pl.multiple_of(l * width, 128), width)]
    return at


def _rows(height):
    def at(ref, l):
        return ref.at[pl.ds(pl.multiple_of(l * height, 8), height), :]
    return at


NTILE = TP // TM


def _tile_rows(t):
    lo = max(t * TM - NMETA, 0)
    hi = min((t + 1) * TM - NMETA, SEQ)
    return lo, hi - lo, lo + NMETA - t * TM


def _for_tile(t, fn):
    for static_t in range(NTILE):
        pl.when(t == static_t)(functools.partial(fn, static_t))


def _token_tile_copy(hbm_ref, buf, sem, t):
    lo, n, off = _tile_rows(t)
    return pltpu.make_async_copy(hbm_ref.at[pl.ds(lo, n)], buf.at[pl.ds(off, n)], sem)


def _prenorm(x, meta_full, pre_w):
    def body(x_ref, meta_ref, pw_ref, h_ref, hn_ref, xbuf, sems):
        i = pl.program_id(0)
        slot = i % 2

        def start(t):
            _token_tile_copy(x_ref, xbuf.at[t % 2], sems.at[t % 2], t).start()

        @pl.when(i == 0)
        def _():
            start(0)
        _for_tile(i + 1, start)
        _for_tile(i, lambda t: _token_tile_copy(x_ref, xbuf.at[t % 2], sems.at[t % 2], t).wait())

        @pl.when(i == 0)
        def _():
            xbuf[0, 0:NMETA, :] = meta_ref[...]

        @pl.when(i == NTILE - 1)
        def _():
            last = _tile_rows(NTILE - 1)[1]
            xbuf[(NTILE - 1) % 2, last:TM, :] = jnp.zeros((TM - last, D), F32)

        pw = pw_ref[...]

        def chunk(ci, carry):
            r0 = pl.multiple_of(ci * R, R)
            xv = xbuf[slot, pl.ds(r0, R), :]
            h_ref[pl.ds(r0, R), :] = xv
            ms = jnp.mean(xv * xv, axis=-1, keepdims=True)
            hn_ref[pl.ds(r0, R), :] = (xv * lax.rsqrt(ms + EPS) * pw).astype(BF16)
            return carry
        lax.fori_loop(0, TM // R, chunk, 0, unroll=2)

    row = pl.BlockSpec((TM, D), lambda i: (i, 0))
    return pl.pallas_call(
        body, name="prenorm",
        grid=(NTILE,),
        in_specs=[pl.BlockSpec(memory_space=pl.ANY), pl.BlockSpec((NMETA, D), lambda i: (0, 0)),
                  pl.BlockSpec((1, D), lambda i: (0, 0))],
        out_specs=[row, row],
        out_shape=[jax.ShapeDtypeStruct((TP, D), F32), jax.ShapeDtypeStruct((TP, D), BF16)],
        scratch_shapes=[pltpu.VMEM((2, TM, D), F32), pltpu.SemaphoreType.DMA((2,))],
        compiler_params=_cparams(),
    )(x, meta_full, pre_w)


def _inproj_cols(name, shards, hn, w_land, b_in, z_prev):
    nsh = shards.shape[0]
    one_shard = w_land.shape[1] == _WCOLS

    def body(idx_ref, hn_ref, w_ref, b_ref, *rest):
        z_ref = rest[-2]
        z_ref[...] = jnp.dot(hn_ref[...], w_ref[...], preferred_element_type=F32) + b_ref[...]

    any_spec = pl.BlockSpec(memory_space=pl.ANY)
    in_specs = [pl.BlockSpec((TM, D), lambda j, i, idx: (i, 0)),
                pl.BlockSpec((D, _WCOLS), lambda j, i, idx: (0, 0 if one_shard else idx[j])),
                pl.BlockSpec((1, _WCOLS), lambda j, i, idx: (0, idx[j]))]
    operands = [hn, w_land, b_in]
    aliases = {2: 1}
    if z_prev is not None:
        in_specs.append(any_spec)
        operands.append(z_prev)
        aliases[4] = 0
    return pl.pallas_call(
        body, name=name,
        grid_spec=pltpu.PrefetchScalarGridSpec(
            num_scalar_prefetch=1, grid=(nsh, TP // TM), in_specs=in_specs,
            out_specs=[pl.BlockSpec((TM, _WCOLS), lambda j, i, idx: (i, idx[j])), any_spec]),
        out_shape=[jax.ShapeDtypeStruct((TP, NIN), F32), jax.ShapeDtypeStruct(w_land.shape, w_land.dtype)],
        input_output_aliases=aliases,
        compiler_params=_cparams(),
    )(shards, *operands)


def _gate_values(ga, gx, xc, sp8):
    r = _sig(ga)
    i = _sig(gx)
    log_a = -(r * sp8)
    a = jnp.exp(log_a)
    mult = jnp.sqrt(-_expm1_neg(2.0 * log_a))
    return r, i, a, mult


def _lru_fwd(z, conv_w, conv_b, wa_g, b_a, wx_g, b_x, lam):
    def body(x_ref, g_ref, cw_ref, cb_ref, wa_ref, ba_ref, wx_ref, bx_ref, lam_ref,
             y_ref, xc_ref, hs_ref, ga_s, gx_s):
        taps = [cw_ref[k:k + 1, :] for k in range(LW)]
        cb = cb_ref[...]

        def conv_chunk(ci, carry):
            r0 = pl.multiple_of(ci * R, R)
            cur = x_ref[pl.ds(r0, R), :]
            p0 = pl.multiple_of(jnp.maximum(r0 - 8, 0), 8)
            prev = jnp.where(ci > 0, x_ref[pl.ds(p0, 8), :], 0.0)
            buf = jnp.concatenate([prev, cur], axis=0)
            acc = cur * taps[LW - 1] + cb
            for s in range(1, LW):
                acc = acc + pltpu.roll(buf, s, 0)[8:8 + R, :] * taps[LW - 1 - s]
            xc_ref[pl.ds(r0, R), :] = acc
            return carry
        lax.fori_loop(0, TP // R, conv_chunk, 0)

        def gate_chunk(ci, carry):
            r0 = pl.multiple_of(ci * TM, TM)
            xb = xc_ref[pl.ds(r0, TM), :].astype(BF16)
            ga_s[pl.ds(r0, TM), :] = jnp.dot(xb, wa_ref[...], preferred_element_type=F32) + ba_ref[...]
            gx_s[pl.ds(r0, TM), :] = jnp.dot(xb, wx_ref[...], preferred_element_type=F32) + bx_ref[...]
            return carry
        lax.fori_loop(0, TP // TM, gate_chunk, 0)

        sp8 = LRU_C * _softplus(-lam_ref[...])
        row = _row_iota((R, CB))

        def scan_chunk(ci, hprev):
            r0 = pl.multiple_of(ci * R, R)
            xc = xc_ref[pl.ds(r0, R), :]
            _, i, a, mult = _gate_values(ga_s[pl.ds(r0, R), :], gx_s[pl.ds(r0, R), :], xc, sp8)
            u = mult * (i * xc)
            k = 1
            while k < R:
                m = row >= k
                u = jnp.where(m, a * pltpu.roll(u, k, 0) + u, u)
                a = jnp.where(m, a * pltpu.roll(a, k, 0), a)
                k *= 2
            hv = u + a * hprev
            hs_ref[pl.ds(r0, R), :] = hv
            g = g_ref[pl.ds(r0, R), :]
            y_ref[pl.ds(r0, R), :] = (hv * (g * _sig(g))).astype(BF16)
            return jnp.sum(jnp.where(row == R - 1, hv, 0.0), axis=0, keepdims=True)
        lax.fori_loop(0, TP // R // 2, lambda i, hp: scan_chunk(2 * i + 1, scan_chunk(2 * i, hp)),
                      jnp.zeros((1, CB), F32))

    col = lambda off: pl.BlockSpec((TP, CB), lambda j: (0, off + j))
    vec = pl.BlockSpec((1, CB), lambda j: (0, j))
    wsp = pl.BlockSpec((None, CB, CB), lambda j: (j, 0, 0))
    return pl.pallas_call(
        body, name="lru_fwd",
        grid=(NCB,),
        in_specs=[col(0), col(NCB), pl.BlockSpec((LW, CB), lambda j: (0, j)), vec, wsp, vec, wsp, vec, vec],
        out_specs=[col(0), col(0), col(0)],
        out_shape=[jax.ShapeDtypeStruct((TP, DL), BF16), jax.ShapeDtypeStruct((TP, DL), F32),
                   jax.ShapeDtypeStruct((TP, DL), F32)],
        scratch_shapes=[pltpu.VMEM((TP, CB), F32), pltpu.VMEM((TP, CB), F32)],
        compiler_params=_cparams(),
    )(z, z, conv_w, conv_b, wa_g, b_a, wx_g, b_x, lam)


CBC = 128
NCBC = DC // CBC
RC = 64


def _fold_rows(v):
    acc = v[0:8, :]
    for r in range(8, v.shape[0], 8):
        acc = acc + v[r:r + 8, :]
    return acc


def _conf_fwd_conv(z, dw_w, dw_b):
    def body(u1_ref, u2_ref, w_ref, b_ref, vc_ref, vs):
        vs[pl.ds(0, KWP), :] = jnp.zeros((KWP, CBC), F32)

        def glu_chunk(ci, carry):
            r0 = pl.multiple_of(ci * RC, RC)
            vs[pl.ds(KWP + r0, RC), :] = u1_ref[pl.ds(r0, RC), :] * _sig(u2_ref[pl.ds(r0, RC), :])
            return carry
        lax.fori_loop(0, TP // RC, glu_chunk, 0)

        bias = b_ref[...]

        def conv_chunk(ci, carry):
            r0 = pl.multiple_of(ci * RC, RC)
            buf = vs[pl.ds(r0, KWP + RC), :]
            acc = jnp.zeros((RC, CBC), F32) + bias
            for rr in range(8):
                rolled = buf if rr == 0 else pltpu.roll(buf, rr, 0)
                for q in range(4):
                    s = 8 * q + rr
                    if s > KW - 1:
                        continue
                    k = KW - 1 - s
                    acc = acc + rolled[KWP - 8 * q:KWP - 8 * q + RC, :] * w_ref[k:k + 1, :]
            vc_ref[pl.ds(r0, RC), :] = acc
            return carry
        lax.fori_loop(0, TP // RC, conv_chunk, 0)

    return pl.pallas_call(
        body, name="conf_fwd_conv",
        grid=(NCBC,),
        in_specs=[pl.BlockSpec((TP, CBC), lambda j: (0, 2 * NCBC + j)),
                  pl.BlockSpec((TP, CBC), lambda j: (0, 3 * NCBC + j)),
                  pl.BlockSpec((KWP, CBC), lambda j: (0, j)),
                  pl.BlockSpec((1, CBC), lambda j: (0, j))],
        out_specs=pl.BlockSpec((TP, CBC), lambda j: (0, j)),
        out_shape=jax.ShapeDtypeStruct((TP, DC), F32),
        scratch_shapes=[pltpu.VMEM((TP + KWP, CBC), F32)],
        compiler_params=_cparams(),
    )(z, z, dw_w, dw_b)


def _ln_chunk(vc, lw, lb):
    mu = jnp.mean(vc, axis=-1, keepdims=True)
    xm = vc - mu
    var = jnp.mean(xm * xm, axis=-1, keepdims=True)
    rstd = lax.rsqrt(var + EPS)
    xhat = xm * rstd
    return xhat, rstd, xhat * lw + lb


def _conf_fwd_proj(vc, z, ln_w, ln_b, pw_w, pw_b):
    def body(vc_ref, g_ref, lw_ref, lb_ref, w_ref, b_ref, y_ref, p_ref, s_s):
        lw, lb = lw_ref[...], lb_ref[...]

        def ln_chunk(ci, carry):
            r0 = pl.multiple_of(ci * R, R)
            for half in range(2):
                rr = r0 + 8 * half
                _, _, ln = _ln_chunk(vc_ref[pl.ds(rr, 8), :], lw, lb)
                p_ref[pl.ds(rr, 8), :] = ln * _sig(ln)
            s_s[pl.ds(r0, R), :] = p_ref[pl.ds(r0, R), :].astype(BF16)
            return carry
        lax.fori_loop(0, TM // R, ln_chunk, 0, unroll=2)

        p_ref[...] = jnp.dot(s_s[...], w_ref[...], preferred_element_type=F32) + b_ref[...]

        def out_chunk(ci, carry):
            r0 = pl.multiple_of(ci * R, R)
            g = g_ref[pl.ds(r0, R), :]
            y_ref[pl.ds(r0, R), :] = (p_ref[pl.ds(r0, R), :] * (g * _sig(g))).astype(BF16)
            return carry
        lax.fori_loop(0, TM // R, out_chunk, 0)

    row = pl.BlockSpec((TM, DC), lambda i: (i, 0))
    vec = pl.BlockSpec((1, DC), lambda i: (0, 0))
    return pl.pallas_call(
        body, name="conf_fwd_proj",
        grid=(TP // TM,),
        in_specs=[row, pl.BlockSpec((TM, DC), lambda i: (i, 4)), vec, vec,
                  pl.BlockSpec((DC, DC), lambda i: (0, 0)), vec],
        out_specs=[row, row],
        out_shape=[jax.ShapeDtypeStruct((TP, DC), BF16), jax.ShapeDtypeStruct((TP, DC), F32)],
        scratch_shapes=[pltpu.VMEM((TM, DC), BF16)],
        compiler_params=_cparams(),
    )(vc, z, ln_w, ln_b, pw_w, pw_b)


def _outproj_loss(ylru, yconf, w_out, h, target, post_w):
    def body(yl_ref, yc_ref, w_ref, h_ref, tgt_hbm, pw_ref, dout_ref, dy_ref, loss_ref, dpw_ref, y_s, t_ref, sem):
        i = pl.program_id(0)
        k = pl.program_id(1)

        @pl.when(k == 0)
        def _():
            _for_tile(i, lambda t: _token_tile_copy(tgt_hbm, t_ref, sem, t).start())
            y_s[...] = jnp.dot(yl_ref[...], w_ref[...], preferred_element_type=F32)

        @pl.when(k == 1)
        def _():
            y_s[...] += jnp.dot(yc_ref[...], w_ref[...], preferred_element_type=F32)

        @pl.when(jnp.logical_and(i == 0, k == 1))
        def _():
            loss_ref[...] = jnp.zeros_like(loss_ref)
            dpw_ref[...] = jnp.zeros_like(dpw_ref)

        @pl.when(k == 1)
        def _():
            _for_tile(i, lambda t: _token_tile_copy(tgt_hbm, t_ref, sem, t).wait())

            @pl.when(i == 0)
            def _():
                t_ref[0:NMETA, :] = jnp.zeros((NMETA, D), F32)

            @pl.when(i == NTILE - 1)
            def _():
                last = _tile_rows(NTILE - 1)[1]
                t_ref[last:TM, :] = jnp.zeros((TM - last, D), F32)

            pw = pw_ref[...]
            row = _row_iota((8, D))

            def chunk(ci, carry):
                r0 = pl.multiple_of(ci * 8, 8)
                yv = y_s[pl.ds(r0, 8), :]
                rs = lax.rsqrt(jnp.mean(yv * yv, axis=-1, keepdims=True) + EPS)
                grow = row + (i * TM + r0)
                valid = jnp.logical_and(grow >= NMETA, grow < T)
                yn = yv * rs
                err = jnp.where(valid, h_ref[pl.ds(r0, 8), :] + yn * pw - t_ref[pl.ds(r0, 8), :], 0.0)
                loss_ref[...] += err * err
                d_rn = err * (1.0 / D)
                dout_ref[pl.ds(r0, 8), :] = d_rn
                dpw_ref[...] += d_rn * yn
                gw = d_rn * pw
                dot = jnp.mean(gw * yv, axis=-1, keepdims=True)
                dy_ref[pl.ds(r0, 8), :] = (rs * gw - yv * (rs * rs * rs * dot)).astype(BF16)
                return carry
            lax.fori_loop(0, TM // 8, chunk, 0, unroll=4)

    row = pl.BlockSpec((TM, D), lambda i, k: (i, 0))
    half = pl.BlockSpec((TM, DL), lambda i, k: (i, 0))
    acc = pl.BlockSpec((8, D), lambda i, k: (0, 0))
    return pl.pallas_call(
        body, name="outproj_loss",
        grid=(TP // TM, 2),
        in_specs=[half, half, pl.BlockSpec((DL, D), lambda i, k: (k, 0)), row, pl.BlockSpec(memory_space=pl.ANY),
                  pl.BlockSpec((1, D), lambda i, k: (0, 0))],
        out_specs=[row, row, acc, acc],
        out_shape=[jax.ShapeDtypeStruct((TP, D), F32), jax.ShapeDtypeStruct((TP, D), BF16),
                   jax.ShapeDtypeStruct((8, D), F32), jax.ShapeDtypeStruct((8, D), F32)],
        scratch_shapes=[pltpu.VMEM((TM, D), F32), pltpu.VMEM((TM, D), F32), pltpu.SemaphoreType.DMA(())],
        compiler_params=_cparams(),
    )(ylru, yconf, w_out, h, target, post_w)


_NT = (((1,), (1,)), ((), ()))
_TN = (((0,), (0,)), ((), ()))


def _outproj_bwd(dy, ylru, yconf, w_out):
    def body(dy_ref, yl_ref, yc_ref, w_ref, dycat_ref, dw_ref):
        j = pl.program_id(0)
        dyv = dy_ref[...]
        dycat_ref[...] = lax.dot_general(dyv, w_ref[...], _NT, preferred_element_type=F32)

        @pl.when(j < NCB)
        def _():
            dw_ref[...] = lax.dot_general(yl_ref[...], dyv, _TN, preferred_element_type=F32).astype(BF16)

        @pl.when(j >= NCB)
        def _():
            dw_ref[...] = lax.dot_general(yc_ref[...], dyv, _TN, preferred_element_type=F32).astype(BF16)

    return pl.pallas_call(
        body, name="outproj_bwd",
        grid=(2 * NCB,),
        in_specs=[pl.BlockSpec((TP, D), lambda j: (0, 0)),
                  pl.BlockSpec((TP, CB), lambda j: (0, jnp.minimum(j, NCB - 1))),
                  pl.BlockSpec((TP, CB), lambda j: (0, jnp.maximum(j - NCB, 0))),
                  pl.BlockSpec((CB, D), lambda j: (j, 0))],
        out_specs=[pl.BlockSpec((TP, CB), lambda j: (0, j)), pl.BlockSpec((CB, D), lambda j: (j, 0))],
        out_shape=[jax.ShapeDtypeStruct((TP, D), F32), jax.ShapeDtypeStruct((D, D), BF16)],
        compiler_params=_cparams(),
    )(dy, ylru, yconf, w_out)


_AFTER = pl.BlockSpec(memory_space=pl.ANY)


def _conf_bwd_proj(dycat, p, z, vc, ln_w, ln_b, pw_w, after):
    def body(dy_ref, p_ref, g_ref, vc_ref, lw_ref, lb_ref, w_ref, after_ref,
             dvc_ref, dgc_ref, dpw_ref, vecs_ref, dp_s, s_s, ds_s):
        i = pl.program_id(0)
        lw, lb = lw_ref[...], lb_ref[...]

        @pl.when(i == 0)
        def _():
            dpw_ref[...] = jnp.zeros_like(dpw_ref)
            vecs_ref[...] = jnp.zeros_like(vecs_ref)

        def pre_chunk(ci, carry):
            r0 = pl.multiple_of(ci * R, R)
            for half in range(2):
                rr = r0 + 8 * half
                dyv = dy_ref[pl.ds(rr, 8), :]
                g = g_ref[pl.ds(rr, 8), :]
                sg = _sig(g)
                dp = dyv * (g * sg)
                dg = dyv * p_ref[pl.ds(rr, 8), :] * (sg * (1.0 + g * (1.0 - sg)))
                vecs_ref[0:8, :] += dp
                vecs_ref[8:16, :] += dg
                ds_s[pl.ds(rr, 8), :] = dp
                dvc_ref[pl.ds(rr, 8), :] = dg
            dp_s[pl.ds(r0, R), :] = ds_s[pl.ds(r0, R), :].astype(BF16)
            dgc_ref[pl.ds(r0, R), :] = dvc_ref[pl.ds(r0, R), :].astype(BF16)
            for half in range(2):
                rr = r0 + 8 * half
                _, _, ln = _ln_chunk(vc_ref[pl.ds(rr, 8), :], lw, lb)
                ds_s[pl.ds(rr, 8), :] = ln * _sig(ln)
            s_s[pl.ds(r0, R), :] = ds_s[pl.ds(r0, R), :].astype(BF16)
            return carry
        lax.fori_loop(0, TM // R, pre_chunk, 0, unroll=2)

        dpb = dp_s[...]
        ds_s[...] = lax.dot_general(dpb, w_ref[...], _NT, preferred_element_type=F32)
        dpw_ref[...] += lax.dot_general(s_s[...], dpb, _TN, preferred_element_type=F32)

        def post_chunk(ci, carry):
            r0 = pl.multiple_of(ci * 8, 8)
            xhat, rstd, ln = _ln_chunk(vc_ref[pl.ds(r0, 8), :], lw, lb)
            sl = _sig(ln)
            dln = ds_s[pl.ds(r0, 8), :] * (sl * (1.0 + ln * (1.0 - sl)))
            vecs_ref[16:24, :] += dln * xhat
            vecs_ref[24:32, :] += dln
            dxh = dln * lw
            m1 = jnp.mean(dxh, axis=-1, keepdims=True)
            m2 = jnp.mean(dxh * xhat, axis=-1, keepdims=True)
            dvc_ref[pl.ds(r0, 8), :] = rstd * (dxh - m1 - xhat * m2)
            return carry
        lax.fori_loop(0, TM // 8, post_chunk, 0, unroll=4)

    row = pl.BlockSpec((TM, DC), lambda i: (i, 0))
    vec = pl.BlockSpec((1, DC), lambda i: (0, 0))
    return pl.pallas_call(
        body, name="conf_bwd_proj",
        grid=(TP // TM,),
        in_specs=[pl.BlockSpec((TM, DC), lambda i: (i, 1)), row, pl.BlockSpec((TM, DC), lambda i: (i, 4)), row,
                  vec, vec, pl.BlockSpec((DC, DC), lambda i: (0, 0)), _AFTER],
        out_specs=[row, row, pl.BlockSpec((DC, DC), lambda i: (0, 0)), pl.BlockSpec((32, DC), lambda i: (0, 0))],
        out_shape=[jax.ShapeDtypeStruct((TP, DC), F32), jax.ShapeDtypeStruct((TP, DC), BF16),
                   jax.ShapeDtypeStruct((DC, DC), F32), jax.ShapeDtypeStruct((32, DC), F32)],
        scratch_shapes=[pltpu.VMEM((TM, DC), BF16), pltpu.VMEM((TM, DC), BF16), pltpu.VMEM((TM, DC), F32)],
        compiler_params=_cparams(),
    )(dycat, p, z, vc, ln_w, ln_b, pw_w, after)


def _conf_bwd_conv(dvc, z, dw_w, after):
    def body(dvc_ref, u1_ref, u2_ref, w_ref, after_ref, du_ref, dw_ref, vecs_ref, vs, dvs):
        vs[pl.ds(0, KWP), :] = jnp.zeros((KWP, CBC), F32)
        dvs[pl.ds(TP, KWP), :] = jnp.zeros((KWP, CBC), F32)
        dw_ref[...] = jnp.zeros_like(dw_ref)
        vecs_ref[...] = jnp.zeros_like(vecs_ref)

        def fill_chunk(ci, carry):
            r0 = pl.multiple_of(ci * RC, RC)
            vs[pl.ds(KWP + r0, RC), :] = u1_ref[pl.ds(r0, RC), :] * _sig(u2_ref[pl.ds(r0, RC), :])
            dv = dvc_ref[pl.ds(r0, RC), :]
            dvs[pl.ds(r0, RC), :] = dv
            vecs_ref[0:8, :] += _fold_rows(dv)
            return carry
        lax.fori_loop(0, TP // RC, fill_chunk, 0)

        def conv_chunk(ci, carry):
            r0 = pl.multiple_of(ci * RC, RC)
            vbuf = vs[pl.ds(r0, KWP + RC), :]
            dbuf = dvs[pl.ds(r0, KWP + RC), :]
            dcur = dbuf[0:RC, :]
            dv = jnp.zeros((RC, CBC), F32)
            for rr in range(8):
                vroll = vbuf if rr == 0 else pltpu.roll(vbuf, rr, 0)
                droll = dbuf if rr == 0 else pltpu.roll(dbuf, KWP + RC - rr, 0)
                for q in range(4):
                    s = 8 * q + rr
                    if s > KW - 1:
                        continue
                    k = KW - 1 - s
                    dv = dv + droll[8 * q:8 * q + RC, :] * w_ref[k:k + 1, :]
                    dw_ref[8 * k:8 * k + 8, :] += _fold_rows(dcur * vroll[KWP - 8 * q:KWP - 8 * q + RC, :])
            u1 = u1_ref[pl.ds(r0, RC), :]
            sg = _sig(u2_ref[pl.ds(r0, RC), :])
            du1 = dv * sg
            du2 = dv * u1 * (sg * (1.0 - sg))
            du_ref[0, pl.ds(r0, RC), :] = du1.astype(BF16)
            du_ref[1, pl.ds(r0, RC), :] = du2.astype(BF16)
            vecs_ref[8:16, :] += _fold_rows(du1)
            vecs_ref[16:24, :] += _fold_rows(du2)
            return carry
        lax.fori_loop(0, TP // RC, conv_chunk, 0)

    blk = pl.BlockSpec((TP, CBC), lambda j: (0, j))
    return pl.pallas_call(
        body, name="conf_bwd_conv",
        grid=(NCBC,),
        in_specs=[blk, pl.BlockSpec((TP, CBC), lambda j: (0, 2 * NCBC + j)),
                  pl.BlockSpec((TP, CBC), lambda j: (0, 3 * NCBC + j)), pl.BlockSpec((KWP, CBC), lambda j: (0, j)),
                  _AFTER],
        out_specs=[pl.BlockSpec((2, TP, CBC), lambda j: (0, 0, j)), pl.BlockSpec((8 * KWP, CBC), lambda j: (0, j)),
                   pl.BlockSpec((24, CBC), lambda j: (0, j))],
        out_shape=[jax.ShapeDtypeStruct((2, TP, DC), BF16),
                   jax.ShapeDtypeStruct((8 * KWP, DC), F32), jax.ShapeDtypeStruct((24, DC), F32)],
        scratch_shapes=[pltpu.VMEM((TP + KWP, CBC), F32), pltpu.VMEM((TP + KWP, CBC), F32)],
        compiler_params=_cparams(),
    )(dvc, z, z, dw_w, after)


def _lru_bwd(dycat, z, xc, hs, conv_w, wa_g, b_a, wx_g, b_x, lam, after):
    NV = 6

    def body(dy_ref, x_ref, g_ref, xc_ref, hs_ref, cw_ref, wa_ref, ba_ref, wx_ref, bx_ref, lam_ref, after_ref,
             dzl_ref, dwa_ref, dwx_ref, dcw_ref, vecs_ref, ga_s, gx_s, dxc_s):
        vecs_ref[...] = jnp.zeros_like(vecs_ref)
        dcw_ref[...] = jnp.zeros_like(dcw_ref)
        dxc_s[pl.ds(TP, 8), :] = jnp.zeros((8, CB), F32)

        def gate_chunk(ci, carry):
            r0 = pl.multiple_of(ci * TM, TM)
            xb = xc_ref[pl.ds(r0, TM), :].astype(BF16)
            ga_s[pl.ds(r0, TM), :] = jnp.dot(xb, wa_ref[...], preferred_element_type=F32) + ba_ref[...]
            gx_s[pl.ds(r0, TM), :] = jnp.dot(xb, wx_ref[...], preferred_element_type=F32) + bx_ref[...]
            return carry
        lax.fori_loop(0, TP // TM, gate_chunk, 0)

        sp8 = LRU_C * _softplus(-lam_ref[...])
        row = _row_iota((R, CB))
        nchunk = TP // R

        def scan_chunk(cj, carry):
            a_next, lam_next = carry
            ci = nchunk - 1 - cj
            r0 = pl.multiple_of(ci * R, R)
            dyv = dy_ref[pl.ds(r0, R), :]
            g = g_ref[pl.ds(r0, R), :]
            hv = hs_ref[pl.ds(r0, R), :]
            xc = xc_ref[pl.ds(r0, R), :]
            sg = _sig(g)
            dgl = dyv * hv * (sg * (1.0 + g * (1.0 - sg)))
            dzl_ref[1, pl.ds(r0, R), :] = dgl.astype(BF16)
            vecs_ref[0:8, :] += _fold8(dgl)
            dhs = dyv * (g * sg)
            r, i, a, mult = _gate_values(ga_s[pl.ds(r0, R), :], gx_s[pl.ds(r0, R), :], xc, sp8)
            b = jnp.where(row == R - 1, a_next, pltpu.roll(a, R - 1, 0))
            lv = dhs
            k = 1
            while k < R:
                m = row < R - k
                lv = jnp.where(m, lv + b * pltpu.roll(lv, R - k, 0), lv)
                b = jnp.where(m, b * pltpu.roll(b, R - k, 0), b)
                k *= 2
            lv = lv + b * lam_next
            p0 = pl.multiple_of(jnp.maximum(r0 - 8, 0), 8)
            hprev8 = jnp.where(ci > 0, hs_ref[pl.ds(p0, 8), :], 0.0)
            hprev = pltpu.roll(jnp.concatenate([hprev8, hv], axis=0), 1, 0)[8:8 + R, :]
            da = lv * hprev
            ixc = i * xc
            dmult = lv * ixc
            di = lv * mult * xc
            dxc_s[pl.ds(r0, R), :] = lv * mult * i
            a2 = a * a
            dlog_a = da * a - dmult * a2 / mult
            vecs_ref[32:40, :] += _fold8(dlog_a * r)
            dga = -(dlog_a * sp8) * r * (1.0 - r)
            dgx = di * i * (1.0 - i)
            ga_s[pl.ds(r0, R), :] = dga
            gx_s[pl.ds(r0, R), :] = dgx
            vecs_ref[16:24, :] += _fold8(dga)
            vecs_ref[24:32, :] += _fold8(dgx)
            a_first = jnp.sum(jnp.where(row == 0, a, 0.0), axis=0, keepdims=True)
            l_first = jnp.sum(jnp.where(row == 0, lv, 0.0), axis=0, keepdims=True)
            return a_first, l_first
        lax.fori_loop(0, nchunk // 2, lambda i, cr: scan_chunk(2 * i + 1, scan_chunk(2 * i, cr)),
                      (jnp.zeros((1, CB), F32), jnp.zeros((1, CB), F32)))

        dwa_ref[...] = jnp.zeros_like(dwa_ref)
        dwx_ref[...] = jnp.zeros_like(dwx_ref)

        def mm_chunk(ci, carry):
            r0 = pl.multiple_of(ci * TM, TM)
            xb = xc_ref[pl.ds(r0, TM), :].astype(BF16)
            dgab = ga_s[pl.ds(r0, TM), :].astype(BF16)
            dgxb = gx_s[pl.ds(r0, TM), :].astype(BF16)
            dxc_s[pl.ds(r0, TM), :] += (lax.dot_general(dgab, wa_ref[...], _NT, preferred_element_type=F32)
                                        + lax.dot_general(dgxb, wx_ref[...], _NT, preferred_element_type=F32))
            dwa_ref[...] += lax.dot_general(xb, dgab, _TN, preferred_element_type=F32)
            dwx_ref[...] += lax.dot_general(xb, dgxb, _TN, preferred_element_type=F32)
            return carry
        lax.fori_loop(0, TP // TM, mm_chunk, 0)

        taps = [cw_ref[k:k + 1, :] for k in range(LW)]

        def conv_chunk(ci, carry):
            r0 = pl.multiple_of(ci * R, R)
            dbuf = dxc_s[pl.ds(r0, R + 8), :]
            dcur = dbuf[0:R, :]
            p0 = pl.multiple_of(jnp.maximum(r0 - 8, 0), 8)
            xprev = jnp.where(ci > 0, x_ref[pl.ds(p0, 8), :], 0.0)
            xbuf = jnp.concatenate([xprev, x_ref[pl.ds(r0, R), :]], axis=0)
            dxl = dcur * taps[LW - 1]
            dcw_ref[8 * (LW - 1):8 * LW, :] += _fold8(dcur * xbuf[8:8 + R, :])
            for s in range(1, LW):
                k = LW - 1 - s
                dxl = dxl + pltpu.roll(dbuf, R + 8 - s, 0)[0:R, :] * taps[k]
                dcw_ref[8 * k:8 * k + 8, :] += _fold8(dcur * pltpu.roll(xbuf, s, 0)[8:8 + R, :])
            dzl_ref[0, pl.ds(r0, R), :] = dxl.astype(BF16)
            vecs_ref[8:16, :] += _fold8(dxl)
            vecs_ref[40:48, :] += _fold8(dcur)
            return carry
        lax.fori_loop(0, TP // R, conv_chunk, 0)
        vecs_ref[32:40, :] = vecs_ref[32:40, :] * (LRU_C * _sig(-lam_ref[...]))

    col = lambda off: pl.BlockSpec((TP, CB), lambda j: (0, off + j))
    vec = pl.BlockSpec((1, CB), lambda j: (0, j))
    wsp = pl.BlockSpec((None, CB, CB), lambda j: (j, 0, 0))
    return pl.pallas_call(
        body, name="lru_bwd",
        grid=(NCB,),
        in_specs=[col(0), col(0), col(NCB), col(0), col(0), pl.BlockSpec((LW, CB), lambda j: (0, j)),
                  wsp, vec, wsp, vec, vec, _AFTER],
        out_specs=[pl.BlockSpec((2, TP, CB), lambda j: (0, 0, j)), wsp, wsp,
                   pl.BlockSpec((8 * LW, CB), lambda j: (0, j)), pl.BlockSpec((8 * NV, CB), lambda j: (0, j))],
        out_shape=[jax.ShapeDtypeStruct((2, TP, DL), BF16),
                   jax.ShapeDtypeStruct((NCB, CB, CB), F32), jax.ShapeDtypeStruct((NCB, CB, CB), F32),
                   jax.ShapeDtypeStruct((8 * LW, DL), F32), jax.ShapeDtypeStruct((8 * NV, DL), F32)],
        scratch_shapes=[pltpu.VMEM((TP, CB), F32), pltpu.VMEM((TP, CB), F32), pltpu.VMEM((TP + 8, CB), F32)],
        compiler_params=_cparams(),
    )(dycat, z, z, xc, hs, conv_w, wa_g, b_a, wx_g, b_x, lam, after)


def _dz_section(sec, dzl_ref, dzc_ref, dgc_ref, use):
    @pl.when(sec < 2)
    def _():
        use(dzl_ref)

    @pl.when(jnp.logical_and(sec >= 2, sec < 4))
    def _():
        use(dzc_ref)

    @pl.when(sec == 4)
    def _():
        use(dgc_ref)


def _dz_specs(rows, index):
    return [pl.BlockSpec((None, rows, 1024), lambda a, b: (jnp.minimum(index(a, b)[1], 1), index(a, b)[0], 0)),
            pl.BlockSpec((None, rows, 1024), lambda a, b: (jnp.clip(index(a, b)[1] - 2, 0, 1), index(a, b)[0], 0)),
            pl.BlockSpec((rows, 1024), lambda a, b: (index(a, b)[0], 0))]


def _inproj_wgrad(name, hn, dzs, after):
    KB = 512
    nsec = dzs.shape[0]

    def body(hn_ref, dz_ref, after_ref, dw_ref):
        dw_ref[...] = lax.dot_general(hn_ref[...], dz_ref[...], _TN, preferred_element_type=F32).astype(BF16)

    return pl.pallas_call(
        body, name=name,
        grid=(nsec, D // KB),
        in_specs=[pl.BlockSpec((TP, KB), lambda n, kb: (0, kb)),
                  pl.BlockSpec((None, TP, 1024), lambda n, kb: (n, 0, 0)), _AFTER],
        out_specs=pl.BlockSpec((KB, 1024), lambda n, kb: (kb, n)),
        out_shape=jax.ShapeDtypeStruct((D, nsec * 1024), BF16),
        compiler_params=_cparams(),
    )(hn, dzs, after)


def _sum_win_parts(parts_a, parts_b, parts_c):
    RB = 64

    def body(a_ref, b_ref, c_ref, o_ref):
        def chunk(ci, carry):
            r0 = pl.multiple_of(ci * R, R)
            for ref, base, ncol in ((a_ref, 0, 2048), (b_ref, 2048, 2048), (c_ref, 4096, 1024)):
                for c0 in range(0, ncol, 512):
                    acc = ref[0, pl.ds(r0, R), c0:c0 + 512].astype(F32)
                    for sidx in range(1, NDEV):
                        acc = acc + ref[sidx, pl.ds(r0, R), c0:c0 + 512].astype(F32)
                    o_ref[pl.ds(r0, R), base + c0:base + c0 + 512] = acc.astype(BF16)
            return carry
        lax.fori_loop(0, RB // R, chunk, 0)

    spec = lambda ncol: pl.BlockSpec((NDEV, RB, ncol), lambda i: (0, i, 0))
    return pl.pallas_call(
        body, name="sum_win_parts",
        grid=(D // NDEV // RB,),
        in_specs=[spec(2048), spec(2048), spec(1024)],
        out_specs=pl.BlockSpec((RB, NIN), lambda i: (i, 0)),
        out_shape=jax.ShapeDtypeStruct((D // NDEV, NIN), BF16),
        compiler_params=_cparams(),
    )(parts_a, parts_b, parts_c)


def _inproj_bwd(dzl, dzc, dgc, w_in, h, dout, pre_w, after):
    nsec = NIN // 1024

    def body(dzl_ref, dzc_ref, dgc_ref, w_ref, h_ref, dout_ref, pw_ref, after_ref, gx_hbm, dmeta_ref, dpw_ref,
             acc_s, dh_s, sem):
        i = pl.program_id(0)
        s = pl.program_id(1)

        def gx_copy(t):
            lo, n, off = _tile_rows(t)
            return pltpu.make_async_copy(dh_s.at[pl.ds(off, n)], gx_hbm.at[pl.ds(lo, n)], sem)

        @pl.when(s == 0)
        def _():
            acc_s[...] = jnp.zeros_like(acc_s)

        def use(dz_ref):
            acc_s[...] += lax.dot_general(dz_ref[...], w_ref[...], _NT, preferred_element_type=F32)
        _dz_section(s, dzl_ref, dzc_ref, dgc_ref, use)

        @pl.when(jnp.logical_and(i == 0, s == nsec - 1))
        def _():
            dpw_ref[...] = jnp.zeros_like(dpw_ref)

        @pl.when(s == nsec - 1)
        def _():
            _for_tile(i - 1, lambda t: gx_copy(t).wait())
            pw = pw_ref[...]

            def chunk(ci, carry):
                r0 = pl.multiple_of(ci * 8, 8)
                hv = h_ref[pl.ds(r0, 8), :]
                dhn = acc_s[pl.ds(r0, 8), :]
                rs = lax.rsqrt(jnp.mean(hv * hv, axis=-1, keepdims=True) + EPS)
                dpw_ref[...] += dhn * (hv * rs)
                gw = dhn * pw
                dot = jnp.mean(gw * hv, axis=-1, keepdims=True)
                dh_s[pl.ds(r0, 8), :] = rs * gw - hv * (rs * rs * rs * dot) + dout_ref[pl.ds(r0, 8), :]
                return carry
            lax.fori_loop(0, TM // 8, chunk, 0, unroll=4)
            _for_tile(i, lambda t: gx_copy(t).start())

            @pl.when(i == 0)
            def _():
                dmeta_ref[...] = dh_s[0:NMETA, :]

            @pl.when(i == NTILE - 1)
            def _():
                gx_copy(NTILE - 1).wait()

    row = pl.BlockSpec((TM, D), lambda i, s: (i, 0))
    return pl.pallas_call(
        body, name="inproj_bwd",
        grid=(TP // TM, nsec),
        in_specs=_dz_specs(TM, lambda i, s: (i, s)) + [
            pl.BlockSpec((D, 1024), lambda i, s: (0, s)), row, row, pl.BlockSpec((1, D), lambda i, s: (0, 0)),
            _AFTER],
        out_specs=[pl.BlockSpec(memory_space=pl.ANY), pl.BlockSpec((NMETA, D), lambda i, s: (0, 0)),
                   pl.BlockSpec((8, D), lambda i, s: (0, 0))],
        out_shape=[jax.ShapeDtypeStruct((SEQ, D), F32), jax.ShapeDtypeStruct((NMETA, D), F32),
                   jax.ShapeDtypeStruct((8, D), F32)],
        scratch_shapes=[pltpu.VMEM((TM, D), F32), pltpu.VMEM((TM, D), F32), pltpu.SemaphoreType.DMA(())],
        compiler_params=_cparams(),
    )(dzl, dzc, dgc, w_in, h, dout, pre_w, after)


def _adamw(name, parts, w, m, v, block_rows):
    rows, cols = w.shape
    nparts = parts.shape[0]
    cw = cols if cols <= 640 else 512

    def body(p_ref, w_ref, m_ref, v_ref, g_ref, d_ref, nm_ref, nv_ref):
        def chunk(ci, carry):
            r0 = pl.multiple_of(ci * R, R)
            for c0 in range(0, cols, cw):
                at = (pl.ds(r0, R), slice(c0, c0 + cw))
                g = p_ref[(0,) + at].astype(F32)
                for sidx in range(1, nparts):
                    g = g + p_ref[(sidx,) + at].astype(F32)
                delta, mv, vv = _adam_math(g, w_ref[at], m_ref[at], v_ref[at])
                g_ref[at] = g
                nm_ref[at] = mv
                nv_ref[at] = vv
                d_ref[at] = delta
            return carry
        lax.fori_loop(0, block_rows // R, chunk, 0)

    blk = pl.BlockSpec((block_rows, cols), lambda i: (i, 0))
    shp = jax.ShapeDtypeStruct((rows, cols), F32)
    return pl.pallas_call(
        body, name=name,
        grid=(rows // block_rows,),
        in_specs=[pl.BlockSpec((nparts, block_rows, cols), lambda i: (0, i, 0)), blk, blk, blk],
        out_specs=[blk, blk, blk, blk],
        out_shape=[shp, shp, shp, shp],
        compiler_params=_cparams(),
    )(parts, w, m, v)


def _adam_math(g, w, m, v):
    c1 = 1.0 / (1.0 - ADAM_B1 ** ADAM_STEP)
    c2 = 1.0 / (1.0 - ADAM_B2 ** ADAM_STEP)
    mv = ADAM_B1 * m + (1.0 - ADAM_B1) * g
    vv = ADAM_B2 * v + (1.0 - ADAM_B2) * (g * g)
    upd = (mv * c1) / (jnp.sqrt(vv * c2) + ADAM_EPS) + ADAM_WD * w
    return -ADAM_LR * upd, mv, vv


_VEC = [("pre_norm_w", 2), ("post_norm_w", 2), ("b_in", 5), ("lru_conv_b", 1), ("b_gate_a", 1), ("b_gate_x", 1),
        ("lru_lambda", 1), ("conf_dw_b", 1), ("conf_ln_w", 1), ("conf_ln_b", 1), ("conf_pw_b", 1)]
_VEC_ROWS = 24
_LOSS_ROW = 17
_SM_ROWS = 64


def _pack_grads(dprew_acc, dpostw_acc, cvecs, kvecs, lvecs, dcw_acc, ddw_acc, dh, loss_acc):
    def body(pre_ref, post_ref, c_ref, k_ref, l_ref, dcw_ref, ddw_ref, dh_ref, loss_ref, vec_ref, small_ref, tmp):
        s8 = lambda ref, r: jnp.sum(ref[8 * r:8 * r + 8, :], axis=0, keepdims=True)
        vec_ref[...] = jnp.zeros_like(vec_ref)
        pre, post = s8(pre_ref, 0), s8(post_ref, 0)
        rows = [pre[:, 0:1024], pre[:, 1024:2048], post[:, 0:1024], post[:, 1024:2048],
                s8(l_ref, 1), s8(l_ref, 0), s8(k_ref, 1), s8(k_ref, 2), s8(c_ref, 1),
                s8(l_ref, 5), s8(l_ref, 2), s8(l_ref, 3), s8(l_ref, 4),
                s8(k_ref, 0), s8(c_ref, 2), s8(c_ref, 3), s8(c_ref, 0)]
        for r, val in enumerate(rows):
            vec_ref[r:r + 1, :] = val
        vec_ref[_LOSS_ROW:_LOSS_ROW + 1, :] = jnp.zeros((1, 1024), F32) + (0.5 / D) * jnp.sum(loss_ref[...])

        small_ref[...] = jnp.zeros_like(small_ref)
        for k in range(LW):
            tmp[k:k + 1, :] = s8(dcw_ref, k)
        for k in range(KW):
            tmp[8 + k:9 + k, :] = s8(ddw_ref, k)
        for d in range(NDEV):
            small_ref[d, 0:LW, 0:128] = tmp[0:LW, 128 * d:128 * d + 128]
            small_ref[d, 8:8 + KW, 0:128] = tmp[8:8 + KW, 128 * d:128 * d + 128]
            small_ref[d, 40:56, :] = dh_ref[:, 256 * d:256 * d + 256]

    full = lambda a: pl.BlockSpec(a.shape, lambda i: (0,) * a.ndim)
    ins = [dprew_acc, dpostw_acc, cvecs, kvecs, lvecs, dcw_acc, ddw_acc]
    return pl.pallas_call(
        body, name="pack_grads",
        grid=(1,),
        in_specs=[full(a) for a in ins] + [full(dh), full(loss_acc)],
        out_specs=[pl.BlockSpec((_VEC_ROWS, 1024), lambda i: (0, 0)),
                   pl.BlockSpec((NDEV, _SM_ROWS, 256), lambda i: (0, 0, 0))],
        out_shape=[jax.ShapeDtypeStruct((_VEC_ROWS, 1024), F32), jax.ShapeDtypeStruct((NDEV, _SM_ROWS, 256), F32)],
        scratch_shapes=[pltpu.VMEM((40, 1024), F32)],
        compiler_params=_cparams(),
    )(*ins, dh, loss_acc)


def _adamw_vec(parts, W, M, V):
    nv = len(_VEC)

    def body(*refs):
        p_ref = refs[0]
        w_refs, m_refs, v_refs = refs[1:1 + nv], refs[1 + nv:1 + 2 * nv], refs[1 + 2 * nv:1 + 3 * nv]
        outs = refs[1 + 3 * nv:]

        def total(r):
            acc = p_ref[0, r:r + 1, :]
            for sidx in range(1, NDEV):
                acc = acc + p_ref[sidx, r:r + 1, :]
            return acc

        row = 0
        for idx, (_, nrows) in enumerate(_VEC):
            for part in range(nrows):
                cols = slice(1024 * part, 1024 * part + 1024)
                g = total(row + part)
                delta, mv, vv = _adam_math(g, w_refs[idx][:, cols], m_refs[idx][:, cols], v_refs[idx][:, cols])
                for o, val in zip(outs[4 * idx:4 * idx + 4], (g, delta, mv, vv)):
                    o[:, cols] = val
            row += nrows
        outs[-1][...] = total(_LOSS_ROW)[:, 0:128]

    names = [n for n, _ in _VEC]
    flat = lambda d: [d[n].reshape(1, -1) for n in names]
    ws, ms, vs = flat(W), flat(M), flat(V)
    res = pl.pallas_call(
        body, name="adamw_vec",
        out_shape=[jax.ShapeDtypeStruct(w.shape, F32) for w in ws for _ in range(4)]
        + [jax.ShapeDtypeStruct((1, 128), F32)],
        compiler_params=_cparams(),
    )(parts, *ws, *ms, *vs)
    return {n: tuple(res[4 * i:4 * i + 4]) for i, n in enumerate(names)}, res[-1]


def _adamw_small(parts, W, M, V):
    where = {"lru_conv_w": (slice(0, LW), slice(0, 128)), "conf_dw_w": (slice(8, 8 + KW), slice(0, 128)),
             "meta_tokens": (slice(40, 56), slice(0, 256))}
    names = list(where)

    def body(*refs):
        p_ref = refs[0]
        outs = refs[10:]
        for idx, n in enumerate(names):
            rs, cs = where[n]
            g = p_ref[0, rs, cs]
            for sidx in range(1, NDEV):
                g = g + p_ref[sidx, rs, cs]
            delta, mv, vv = _adam_math(g, refs[1 + idx][...], refs[4 + idx][...], refs[7 + idx][...])
            for o, val in zip(outs[4 * idx:4 * idx + 4], (g, delta, mv, vv)):
                o[...] = val

    two_d = lambda a: a.reshape(a.shape[-2:])
    ws, ms, vs = ([two_d(d[n]) for n in names] for d in (W, M, V))
    res = pl.pallas_call(
        body, name="adamw_small",
        out_shape=[jax.ShapeDtypeStruct(w.shape, F32) for w in ws for _ in range(4)],
        compiler_params=_cparams(),
    )(parts, *ws, *ms, *vs)
    return {n: tuple(res[4 * i:4 * i + 4]) for i, n in enumerate(names)}


def _pack_small(lru_cw, dw_w, meta):
    buf = jnp.zeros((_SM_ROWS, 256), F32)
    buf = buf.at[0:LW, 0:128].set(lru_cw)
    buf = buf.at[8:8 + dw_w.shape[0], 0:128].set(dw_w)
    return buf.at[40:56, :].set(meta)


def _block_diag4(w):
    w4 = w.reshape(NCB, 4, 64, 64)
    eye = jnp.eye(4, dtype=w.dtype)
    return jnp.einsum("ghij,hk->ghikj", w4, eye).reshape(NCB, CB, CB)


def _diag_blocks(g):
    g5 = g.reshape(NCB, 4, 64, 4, 64)
    return jnp.stack([g5[:, hh, :, hh, :] for hh in range(4)], axis=1).reshape(16, 64, 64)


def _gate_mats(W):
    return _block_diag4(W["w_gate_a"][0]).astype(BF16), _block_diag4(W["w_gate_x"][0]).astype(BF16)


def _local_step(x, target, meta_full, inproj, out_weights, lru_cw_full, dw_w_full, W, gate_mats, send):
    wa_g, wx_g = gate_mats

    h, hn = _prenorm(x, meta_full, W["pre_norm_w"])
    z, win_full = inproj(hn)
    ylru, xc, hs = _lru_fwd(z, lru_cw_full, W["lru_conv_b"], wa_g, W["b_gate_a"], wx_g, W["b_gate_x"],
                            W["lru_lambda"])
    vc = _conf_fwd_conv(z, dw_w_full, W["conf_dw_b"])
    wout_full, pw_full = out_weights(vc)
    yconf, p = _conf_fwd_proj(vc, z, W["conf_ln_w"], W["conf_ln_b"], pw_full, W["conf_pw_b"])
    dout, dy, loss_acc, dpostw_acc = _outproj_loss(ylru, yconf, wout_full, h, target, W["post_norm_w"])

    dycat, dwout_part = _outproj_bwd(dy, ylru, yconf, wout_full)
    tok = send("w_out", ("w_out", dwout_part))
    dvc, dgc, dpw_part, cvecs = _conf_bwd_proj(dycat, p, z, vc, W["conf_ln_w"], W["conf_ln_b"], pw_full, tok)
    tok = send("w_in_c", ("conf_pw_w", dpw_part), ("w_in_c", _inproj_wgrad("inproj_wgrad_c", hn, dgc[None], dgc)))
    dzc, ddw_acc, kvecs = _conf_bwd_conv(dvc, z, dw_w_full, tok)
    tok = send("w_in_b", ("w_in_b", _inproj_wgrad("inproj_wgrad_b", hn, dzc, dzc)))
    dzl, dwa_g, dwx_g, dcw_acc, lvecs = _lru_bwd(dycat, z, xc, hs, lru_cw_full, wa_g, W["b_gate_a"], wx_g,
                                                 W["b_gate_x"], W["lru_lambda"], tok)
    tok = send("w_gates", ("w_gate_a", _diag_blocks(dwa_g).reshape(16 * 64, 64)),
               ("w_gate_x", _diag_blocks(dwx_g).reshape(16 * 64, 64)))
    tok = send("w_in_a", ("w_in_a", _inproj_wgrad("inproj_wgrad_a", hn, dzl, tok)))
    grad_x, dmeta, dprew_acc = _inproj_bwd(dzl, dzc, dgc, win_full, h, dout, W["pre_norm_w"], tok)

    vec_pack, small_part = _pack_grads(dprew_acc, dpostw_acc, cvecs, kvecs, lvecs, dcw_acc, ddw_acc, dmeta, loss_acc)
    return grad_x, vec_pack, small_part


def kernel(x, meta_tokens, pre_norm_w, post_norm_w, w_in, b_in, lru_conv_w, lru_conv_b, w_gate_a, b_gate_a, w_gate_x, b_gate_x, lru_lambda, conf_dw_w, conf_dw_b, conf_ln_w, conf_ln_b, conf_pw_w, conf_pw_b, w_out, loss_target, m_meta_tokens, m_pre_norm_w, m_post_norm_w, m_w_in, m_b_in, m_lru_conv_w, m_lru_conv_b, m_w_gate_a, m_b_gate_a, m_w_gate_x, m_b_gate_x, m_lru_lambda, m_conf_dw_w, m_conf_dw_b, m_conf_ln_w, m_conf_ln_b, m_conf_pw_w, m_conf_pw_b, m_w_out, v_meta_tokens, v_pre_norm_w, v_post_norm_w, v_w_in, v_b_in, v_lru_conv_w, v_lru_conv_b, v_w_gate_a, v_b_gate_a, v_w_gate_x, v_b_gate_x, v_lru_lambda, v_conf_dw_w, v_conf_dw_b, v_conf_ln_w, v_conf_ln_b, v_conf_pw_w, v_conf_pw_b, v_w_out):
    W = dict(meta_tokens=meta_tokens, pre_norm_w=pre_norm_w, post_norm_w=post_norm_w, w_in=w_in, b_in=b_in,
             lru_conv_w=lru_conv_w, lru_conv_b=lru_conv_b, w_gate_a=w_gate_a, b_gate_a=b_gate_a,
             w_gate_x=w_gate_x, b_gate_x=b_gate_x, lru_lambda=lru_lambda, conf_dw_w=conf_dw_w,
             conf_dw_b=conf_dw_b, conf_ln_w=conf_ln_w, conf_ln_b=conf_ln_b, conf_pw_w=conf_pw_w,
             conf_pw_b=conf_pw_b, w_out=w_out)
    M = dict(meta_tokens=m_meta_tokens, pre_norm_w=m_pre_norm_w, post_norm_w=m_post_norm_w, w_in=m_w_in,
             b_in=m_b_in, lru_conv_w=m_lru_conv_w, lru_conv_b=m_lru_conv_b, w_gate_a=m_w_gate_a,
             b_gate_a=m_b_gate_a, w_gate_x=m_w_gate_x, b_gate_x=m_b_gate_x, lru_lambda=m_lru_lambda,
             conf_dw_w=m_conf_dw_w, conf_dw_b=m_conf_dw_b, conf_ln_w=m_conf_ln_w, conf_ln_b=m_conf_ln_b,
             conf_pw_w=m_conf_pw_w, conf_pw_b=m_conf_pw_b, w_out=m_w_out)
    V = dict(meta_tokens=v_meta_tokens, pre_norm_w=v_pre_norm_w, post_norm_w=v_post_norm_w, w_in=v_w_in,
             b_in=v_b_in, lru_conv_w=v_lru_conv_w, lru_conv_b=v_lru_conv_b, w_gate_a=v_w_gate_a,
             b_gate_a=v_b_gate_a, w_gate_x=v_w_gate_x, b_gate_x=v_b_gate_x, lru_lambda=v_lru_lambda,
             conf_dw_w=v_conf_dw_w, conf_dw_b=v_conf_dw_b, conf_ln_w=v_conf_ln_w, conf_ln_b=v_conf_ln_b,
             conf_pw_w=v_conf_pw_w, conf_pw_b=v_conf_pw_b, w_out=v_w_out)
    names = list(W.keys())
    shapes = {n: W[n].shape for n in names}

    small = _pack_small(lru_conv_w[0], conf_dw_w[0], meta_tokens)
    (small_flight,), tok = _exchange_start("gather_small_start", [
        (small, jax.ShapeDtypeStruct((NDEV, _SM_ROWS, 256), F32), _whole, _slot)])
    win_flight, tok = _win_gather_start(w_in[0].astype(BF16) + tok[0, 0].astype(BF16))
    gate_mats = _gate_mats(W)
    wout_shard = w_out[0].astype(BF16) + tok[0, 0].astype(BF16)
    pw_shard = conf_pw_w[0].astype(BF16)
    cast_done = (gate_mats[0][0, 0:8, 0:128] + gate_mats[1][0, 0:8, 0:128]
                 + wout_shard[0:8, 0:128] + pw_shard[0:8, 0:128])
    win_flight, tok = _win_gather_links(win_flight, cast_done)
    (small_all,) = _exchange_wait("gather_small_wait", [small_flight], tok)
    out_flight = {}
    unshard = lambda a: jnp.transpose(a, (1, 0, 2)).reshape(a.shape[1], -1)
    lru_cw_full = unshard(small_all[:, 0:LW, 0:128])
    dw_w_full = unshard(small_all[:, 8:8 + KWP, 0:128])
    meta_full = unshard(small_all[:, 40:56, :])

    def out_weights(after):
        pw_whole, wout_whole = _exchange_wait("gather_out_wait", out_flight["handles"], after)
        return wout_whole, pw_whole

    def inproj(hn):
        xi, yi, ci = lax.axis_index("x"), lax.axis_index("y"), lax.axis_index("c")
        shard = lambda px, py, pc: (4 * px + 2 * py + pc).astype(jnp.int32)
        over_links = jnp.stack([shard(1 - xi, yi, ci), shard(xi, 1 - yi, ci), shard(1 - xi, 1 - yi, ci)])
        z, src = _inproj_cols("inproj_own", jnp.stack([shard(xi, yi, ci)]), hn, win_flight["src"], b_in, None)
        flight = _win_gather_early(dict(win_flight, src=src))
        z, land = _inproj_cols("inproj_here", jnp.stack([shard(xi, yi, 1 - ci)]), hn, flight["land"], b_in, z)
        flight, _ = _win_gather_forward(dict(flight, land=land), z)
        z, land = _inproj_cols("inproj_links", over_links, hn, flight["land"], b_in, z)
        land, tok = _win_gather_wait(dict(flight, land=land))
        out_flight["handles"], _ = _exchange_start("gather_out_start", [
            (pw_shard + tok[0, 0].astype(BF16), jax.ShapeDtypeStruct((DC, DC), BF16), _whole, _rows(DC // NDEV)),
            (wout_shard, jax.ShapeDtypeStruct((D, D), BF16), _whole, _rows(D // NDEV)),
        ])
        return _inproj_cols("inproj_sibling", over_links + 1 - 2 * ci, hn, land, b_in, z)

    row_stage = lambda ncol: (jax.ShapeDtypeStruct((NDEV, D // NDEV, ncol), BF16), _rows(D // NDEV))
    piece = {"w_in_a": row_stage(2048), "w_in_b": row_stage(2048), "w_in_c": row_stage(1024),
             "w_out": row_stage(D),
             "conf_pw_w": (jax.ShapeDtypeStruct((NDEV, DC // NDEV, DC), BF16), _rows(DC // NDEV)),
             "w_gate_a": (jax.ShapeDtypeStruct((NDEV, 16 * 64, 64), BF16), _whole),
             "w_gate_x": (jax.ShapeDtypeStruct((NDEV, 16 * 64, 64), BF16), _whole)}
    sent = {}

    def send(call, *named_parts):
        handles, token = _exchange_start(
            "scatter_" + call + "_start",
            [(part.astype(BF16), piece[name][0], piece[name][1], _slot) for name, part in named_parts])
        for (name, _), handle in zip(named_parts, handles):
            sent[name] = [handle]
        return token

    grad_x, vec_pack, small_part = _local_step(
        x[0], loss_target[0], meta_full, inproj, out_weights, lru_cw_full, dw_w_full, W, gate_mats, send)
    grad_x = grad_x[None]

    rest, tok = _exchange_start("scatter_rest_start", [
        (small_part, jax.ShapeDtypeStruct((NDEV, _SM_ROWS, 256), F32), _slot, _slot),
        (vec_pack, jax.ShapeDtypeStruct((NDEV, _VEC_ROWS, 1024), F32), _whole, _slot),
    ])
    (parts_c,) = _exchange_wait("scatter_w_in_c_wait", sent["w_in_c"], tok)
    (parts_b,) = _exchange_wait("scatter_w_in_b_wait", sent["w_in_b"], parts_c)
    (parts_a,) = _exchange_wait("scatter_w_in_a_wait", sent["w_in_a"], parts_b)
    win_rows = _sum_win_parts(parts_a, parts_b, parts_c)
    win_stage2, tok = _exchange_start("scatter_w_in_stage2_start", [
        (win_rows, jax.ShapeDtypeStruct((NDEV, D // NDEV, NIN // NDEV), BF16), _cols(NIN // NDEV), _slot)])

    G, DW, NM, NV = {}, {}, {}, {}
    (wout_parts,) = _exchange_wait("scatter_w_out_wait", sent["w_out"], tok)
    G["w_out"], DW["w_out"], NM["w_out"], NV["w_out"] = _adamw("adamw_w_out", wout_parts, w_out[0], m_w_out[0], v_w_out[0], 64)
    (pw_parts,) = _exchange_wait("scatter_conf_pw_w_wait", sent["conf_pw_w"], G["w_out"])
    G["conf_pw_w"], DW["conf_pw_w"], NM["conf_pw_w"], NV["conf_pw_w"] = _adamw(
        "adamw_pw", pw_parts, conf_pw_w[0], m_conf_pw_w[0], v_conf_pw_w[0], 128)
    res = {}
    wa_parts, wx_parts = _exchange_wait("scatter_w_gates_wait", sent["w_gate_a"] + sent["w_gate_x"], G["conf_pw_w"])
    for n, parts in (("w_gate_a", wa_parts), ("w_gate_x", wx_parts)):
        res[n] = _adamw("adamw_" + n, parts, *[d[n].reshape(16 * 64, 64) for d in (W, M, V)], 16 * 64)
    small_parts, vec_parts = _exchange_wait("scatter_rest_wait", rest, res["w_gate_x"][0])
    res.update(_adamw_small(small_parts, W, M, V))
    vec_res, loss_row = _adamw_vec(vec_parts, W, M, V)
    res.update(vec_res)
    (win_sum,) = _exchange_wait("scatter_w_in_stage2_wait", win_stage2, loss_row)
    res["w_in"] = _adamw("adamw_w_in", win_sum.reshape(1, D, NIN // NDEV), w_in[0], m_w_in[0], v_w_in[0], 256)
    for n, vals in res.items():
        for dst, val in zip((G, DW, NM, NV), vals):
            dst[n] = val
    for dst in (G, DW, NM, NV):
        for n in names:
            dst[n] = dst[n].reshape(shapes[n])
    loss = loss_row[0, 0]

    return (loss, grad_x, *[G[n] for n in names], *[DW[n] for n in names],
            *[NM[n] for n in names], *[NV[n] for n in names])
```

```python
import functools

import jax
import jax.numpy as jnp
from jax import lax
from jax.experimental import pallas as pl
from jax.experimental.pallas import tpu as pltpu

F32 = jnp.float32
BF16 = jnp.bfloat16

D = 2048
DL = 1024
DC = 1024
NIN = 5120
NMETA = 16
SEQ = 2048
T = NMETA + SEQ
TP = 2176
TM = 544
CB = 256
NCB = DL // CB
R = 16
KW = 31
KWP = 32
LW = 4
LRU_C = 8.0
EPS = 1e-6
NDEV = 8

ADAM_LR = 0.001
ADAM_B1 = 0.9
ADAM_B2 = 0.999
ADAM_EPS = 1e-08
ADAM_WD = 0.01
ADAM_STEP = 10

VMEM_LIMIT = 56 * 1024 * 1024


def _cparams():
    return pltpu.CompilerParams(vmem_limit_bytes=VMEM_LIMIT)


def _sig(x):
    return 1.0 / (1.0 + jnp.exp(-x))


def _expm1_neg(y):
    poly = y * (1.0 + y * (0.5 + y * (1.0 / 6.0 + y * (1.0 / 24.0 + y * (1.0 / 120.0)))))
    return jnp.where(y > -0.1, poly, jnp.exp(y) - 1.0)


def _softplus(x):
    e = jnp.exp(-jnp.abs(x))
    w = 1.0 + e
    l1p = jnp.where(w == 1.0, e, jnp.log(w) * e / (w - 1.0))
    return jnp.maximum(x, 0.0) + l1p


def _row_iota(shape):
    return lax.broadcasted_iota(jnp.int32, shape, 0)


def _fold8(v):
    return v[0:8, :] + v[8:16, :]


_FLIPS = [(k >> 2 & 1, k >> 1 & 1, k & 1) for k in range(1, NDEV)]
_HBM = pl.BlockSpec(memory_space=pltpu.HBM)
_SEM = pl.BlockSpec(memory_space=pltpu.SEMAPHORE)


def _peers():
    x, y, c = lax.axis_index("x"), lax.axis_index("y"), lax.axis_index("c")
    out = []
    for dx, dy, dc in _FLIPS:
        px = 1 - x if dx else x
        py = 1 - y if dy else y
        pc = 1 - c if dc else c
        out.append(((px, py, pc), 4 * px + 2 * py + pc))
    return 4 * x + 2 * y + c, out


def _exchange_start(name, items):
    n = len(items)

    def body(*refs):
        srcs, lands = refs[:n], refs[n:2 * n]
        outs = refs[2 * n:]
        send_sems, recv_sems, local_sems = outs[:n], outs[n:2 * n], outs[2 * n:3 * n]
        token = outs[-1]
        me, peers = _peers()
        for a in range(n):
            src_at, dst_at = items[a][2], items[a][3]
            pltpu.make_async_copy(src_at(srcs[a], me), dst_at(lands[a], me), local_sems[a]).start()
        for a in range(n):
            src_at, dst_at = items[a][2], items[a][3]
            for k, (pos, peer) in enumerate(peers):
                pltpu.make_async_remote_copy(
                    src_ref=src_at(srcs[a], peer), dst_ref=dst_at(lands[a], me),
                    send_sem=send_sems[a].at[k], recv_sem=recv_sems[a].at[k],
                    device_id=pos, device_id_type=pl.DeviceIdType.MESH).start()
        token[...] = jnp.zeros_like(token)

    srcs = [pltpu.with_memory_space_constraint(it[0], pltpu.HBM) for it in items]
    lands = [pltpu.with_memory_space_constraint(lax.empty(it[1].shape, it[1].dtype), pltpu.HBM) for it in items]
    sem7 = pltpu.SemaphoreType.DMA((NDEV - 1,))
    res = pl.pallas_call(
        body, name=name,
        out_shape=([sem7] * (2 * n) + [pltpu.SemaphoreType.DMA(())] * n
                   + [pltpu.HBM(a.shape, a.dtype) for a in srcs] + [pltpu.HBM(a.shape, a.dtype) for a in lands]
                   + [jax.ShapeDtypeStruct((8, 128), F32)]),
        in_specs=[_HBM] * (2 * n),
        out_specs=[_SEM] * (3 * n) + [_HBM] * (2 * n) + [pl.BlockSpec(memory_space=pltpu.VMEM)],
        input_output_aliases={i: 3 * n + i for i in range(2 * n)},
        compiler_params=pltpu.CompilerParams(has_side_effects=pltpu.SideEffectType.DATAFLOW_SIDE_EFFECTING),
    )(*srcs, *lands)
    handles = [dict(send=res[a], recv=res[n + a], local=res[2 * n + a], src=res[3 * n + a], land=res[4 * n + a],
                    src_at=items[a][2], dst_at=items[a][3]) for a in range(n)]
    return handles, res[-1]


def _wait_bytes(piece, sem):
    pltpu.make_async_copy(piece, piece, sem).wait()


def _exchange_wait(name, handles, after):
    n = len(handles)

    def body(*refs):
        srcs, lands = refs[:n], refs[n:2 * n]
        send_sems, recv_sems, local_sems = refs[2 * n:3 * n], refs[3 * n:4 * n], refs[4 * n:5 * n]
        me, peers = _peers()
        for a in range(n):
            src_at, dst_at = handles[a]["src_at"], handles[a]["dst_at"]
            for k, (pos, peer) in enumerate(peers):
                _wait_bytes(src_at(srcs[a], peer), send_sems[a].at[k])
                _wait_bytes(dst_at(lands[a], peer), recv_sems[a].at[k])
            pltpu.make_async_copy(src_at(srcs[a], me), dst_at(lands[a], me), local_sems[a]).wait()

    srcs = [hd["src"] for hd in handles]
    lands = [hd["land"] for hd in handles]
    res = pl.pallas_call(
        body, name=name,
        out_shape=[pltpu.HBM(a.shape, a.dtype) for a in srcs] + [pltpu.HBM(a.shape, a.dtype) for a in lands],
        in_specs=[_HBM] * (2 * n) + [_SEM] * (3 * n) + [pl.BlockSpec(memory_space=pl.ANY)],
        out_specs=[_HBM] * (2 * n),
        input_output_aliases={i: i for i in range(2 * n)},
        compiler_params=pltpu.CompilerParams(has_side_effects=pltpu.SideEffectType.DATAFLOW_SIDE_EFFECTING),
    )(*srcs, *lands, *[hd["send"] for hd in handles], *[hd["recv"] for hd in handles],
      *[hd["local"] for hd in handles], after)
    return list(res[n:])


_SIDE = pltpu.SideEffectType.DATAFLOW_SIDE_EFFECTING
_WCOLS = NIN // NDEV


def _win_cols(ref, l):
    return ref.at[:, pl.ds(pl.multiple_of(l * _WCOLS, 128), _WCOLS)]


def _win_routes():
    x, y, c = lax.axis_index("x"), lax.axis_index("y"), lax.axis_index("c")
    pos = [(x, y, 1 - c), (1 - x, y, c), (x, 1 - y, c), (1 - x, 1 - y, c)]
    return 4 * x + 2 * y + c, [(p, 4 * p[0] + 2 * p[1] + p[2]) for p in pos]


def _win_gather_start(shard):
    def body(src, land, send_sem, recv_sem, local_sem, src_thru, land_thru, token):
        me, routes = _win_routes()
        pltpu.make_async_copy(src, _win_cols(land, me), local_sem).start()
        pltpu.make_async_remote_copy(src_ref=src, dst_ref=_win_cols(land, me), send_sem=send_sem, recv_sem=recv_sem,
                                     device_id=routes[0][0], device_id_type=pl.DeviceIdType.MESH).start()
        token[...] = jnp.zeros_like(token)

    src = pltpu.with_memory_space_constraint(shard, pltpu.HBM)
    land = pltpu.with_memory_space_constraint(lax.empty((D, NIN), BF16), pltpu.HBM)
    sem = pltpu.SemaphoreType.DMA(())
    res = pl.pallas_call(
        body, name="win_gather_start",
        out_shape=[sem, sem, sem, pltpu.HBM(src.shape, BF16), pltpu.HBM(land.shape, BF16),
                   jax.ShapeDtypeStruct((8, 128), F32)],
        in_specs=[_HBM, _HBM],
        out_specs=[_SEM, _SEM, _SEM, _HBM, _HBM, pl.BlockSpec(memory_space=pltpu.VMEM)],
        input_output_aliases={0: 3, 1: 4},
        compiler_params=pltpu.CompilerParams(has_side_effects=_SIDE),
    )(src, land)
    return dict(send0=res[0], recv0=res[1], local=res[2], src=res[3], land=res[4]), res[5]


def _win_gather_links(hd, after):
    def body(src, land, after_ref, send_sems, recv_sems, src_thru, land_thru, token):
        me, routes = _win_routes()
        for k in (1, 2, 3):
            pltpu.make_async_remote_copy(src_ref=src, dst_ref=_win_cols(land, me), send_sem=send_sems.at[k - 1],
                                         recv_sem=recv_sems.at[k - 1], device_id=routes[k][0],
                                         device_id_type=pl.DeviceIdType.MESH).start()
        token[...] = jnp.zeros_like(token)

    sem3 = pltpu.SemaphoreType.DMA((3,))
    res = pl.pallas_call(
        body, name="win_gather_links",
        out_shape=[sem3, sem3, pltpu.HBM(hd["src"].shape, BF16), pltpu.HBM(hd["land"].shape, BF16),
                   jax.ShapeDtypeStruct((8, 128), F32)],
        in_specs=[_HBM, _HBM, pl.BlockSpec(memory_space=pl.ANY)],
        out_specs=[_SEM, _SEM, _HBM, _HBM, pl.BlockSpec(memory_space=pltpu.VMEM)],
        input_output_aliases={0: 2, 1: 3},
        compiler_params=pltpu.CompilerParams(has_side_effects=_SIDE),
    )(hd["src"], hd["land"], after)
    return dict(hd, send=res[0], recv=res[1], src=res[2], land=res[3]), res[4]


def _win_gather_forward(name, hd, ks, after):
    def body(land, recv_sems, after_ref, land_thru, fsend_sems, frecv_sems):
        me, routes = _win_routes()
        sibling = routes[0][0]
        for n, k in enumerate(ks):
            pos, peer = routes[k]
            piece = _win_cols(land, peer)
            pltpu.make_async_remote_copy(src_ref=piece, dst_ref=piece, send_sem=fsend_sems.at[n],
                                         recv_sem=recv_sems.at[k - 1], device_id=pos,
                                         device_id_type=pl.DeviceIdType.MESH).wait_recv()
            pltpu.make_async_remote_copy(src_ref=piece, dst_ref=piece, send_sem=fsend_sems.at[n],
                                         recv_sem=frecv_sems.at[n], device_id=sibling,
                                         device_id_type=pl.DeviceIdType.MESH).start()

    sems = pltpu.SemaphoreType.DMA((len(ks),))
    res = pl.pallas_call(
        body, name="win_gather_forward_" + name,
        out_shape=[pltpu.HBM(hd["land"].shape, BF16), sems, sems],
        in_specs=[_HBM, _SEM, pl.BlockSpec(memory_space=pl.ANY)],
        out_specs=[_HBM, _SEM, _SEM],
        input_output_aliases={0: 0},
        compiler_params=pltpu.CompilerParams(has_side_effects=_SIDE),
    )(hd["land"], hd["recv"], after)
    return dict(hd, land=res[0], **{"fsend" + name: res[1], "frecv" + name: res[2]})


def _win_gather_forwarded(name, hd, ks):
    def body(land, fsend_sems, frecv_sems, land_thru):
        me, routes = _win_routes()
        sib_c = routes[0][0][2]
        for n, k in enumerate(ks):
            _wait_bytes(_win_cols(land, routes[k][1]), fsend_sems.at[n])
            _wait_bytes(_win_cols(land, 4 * routes[k][0][0] + 2 * routes[k][0][1] + sib_c), frecv_sems.at[n])

    res = pl.pallas_call(
        body, name="win_gather_forwarded_" + name,
        out_shape=[pltpu.HBM(hd["land"].shape, BF16)],
        in_specs=[_HBM, _SEM, _SEM],
        out_specs=[_HBM],
        input_output_aliases={0: 0},
        compiler_params=pltpu.CompilerParams(has_side_effects=_SIDE),
    )(hd["land"], hd["fsend" + name], hd["frecv" + name])
    return dict(hd, land=res[0])


def _win_gather_early(hd):
    def body(src, land, recv_sem, local_sem, src_thru, land_thru):
        me, routes = _win_routes()
        _wait_bytes(_win_cols(land, routes[0][1]), recv_sem)
        pltpu.make_async_copy(src, _win_cols(land, me), local_sem).wait()

    res = pl.pallas_call(
        body, name="win_gather_early",
        out_shape=[pltpu.HBM(hd["src"].shape, BF16), pltpu.HBM(hd["land"].shape, BF16)],
        in_specs=[_HBM, _HBM, _SEM, _SEM],
        out_specs=[_HBM, _HBM],
        input_output_aliases={0: 0, 1: 1},
        compiler_params=pltpu.CompilerParams(has_side_effects=_SIDE),
    )(hd["src"], hd["land"], hd["recv0"], hd["local"])
    return dict(hd, src=res[0], land=res[1])


def _win_gather_wait(hd):
    def body(src, land, send0_sem, send_sems, src_thru, land_thru):
        for k in range(4):
            _wait_bytes(src, send0_sem if k == 0 else send_sems.at[k - 1])

    res = pl.pallas_call(
        body, name="win_gather_wait",
        out_shape=[pltpu.HBM(hd["src"].shape, BF16), pltpu.HBM(hd["land"].shape, BF16)],
        in_specs=[_HBM, _HBM, _SEM, _SEM],
        out_specs=[_HBM, _HBM],
        input_output_aliases={0: 0, 1: 1},
        compiler_params=pltpu.CompilerParams(has_side_effects=_SIDE),
    )(hd["src"], hd["land"], hd["send0"], hd["send"])
    return res[1]


def _whole(ref, l):
    return ref


def _slot(ref, l):
    return ref.at[l]


def _cols(width):
    def at(ref, l):
        return ref.at[:, pl.ds(pl.multiple_of(l * width, 128), width)]
    return at


def _rows(height):
    def at(ref, l):
        return ref.at[pl.ds(pl.multiple_of(l * height, 8), height), :]
    return at


NTILE = TP // TM


def _tile_rows(t):
    lo = max(t * TM - NMETA, 0)
    hi = min((t + 1) * TM - NMETA, SEQ)
    return lo, hi - lo, lo + NMETA - t * TM


def _for_tile(t, fn):
    for static_t in range(NTILE):
        pl.when(t == static_t)(functools.partial(fn, static_t))


def _token_tile_copy(hbm_ref, buf, sem, t):
    lo, n, off = _tile_rows(t)
    return pltpu.make_async_copy(hbm_ref.at[pl.ds(lo, n)], buf.at[pl.ds(off, n)], sem)


def _prenorm(x, meta_full, pre_w):
    def body(x_ref, meta_ref, pw_ref, h_ref, hn_ref, xbuf, sems):
        i = pl.program_id(0)
        slot = i % 2

        def start(t):
            _token_tile_copy(x_ref, xbuf.at[t % 2], sems.at[t % 2], t).start()

        @pl.when(i == 0)
        def _():
            start(0)
        _for_tile(i + 1, start)
        _for_tile(i, lambda t: _token_tile_copy(x_ref, xbuf.at[t % 2], sems.at[t % 2], t).wait())

        @pl.when(i == 0)
        def _():
            xbuf[0, 0:NMETA, :] = meta_ref[...]

        @pl.when(i == NTILE - 1)
        def _():
            last = _tile_rows(NTILE - 1)[1]
            xbuf[(NTILE - 1) % 2, last:TM, :] = jnp.zeros((TM - last, D), F32)

        pw = pw_ref[...]

        def chunk(ci, carry):
            r0 = pl.multiple_of(ci * R, R)
            xv = xbuf[slot, pl.ds(r0, R), :]
            h_ref[pl.ds(r0, R), :] = xv
            ms = jnp.mean(xv * xv, axis=-1, keepdims=True)
            hn_ref[pl.ds(r0, R), :] = (xv * lax.rsqrt(ms + EPS) * pw).astype(BF16)
            return carry
        lax.fori_loop(0, TM // R, chunk, 0, unroll=2)

    row = pl.BlockSpec((TM, D), lambda i: (i, 0))
    return pl.pallas_call(
        body, name="prenorm",
        grid=(NTILE,),
        in_specs=[pl.BlockSpec(memory_space=pl.ANY), pl.BlockSpec((NMETA, D), lambda i: (0, 0)),
                  pl.BlockSpec((1, D), lambda i: (0, 0))],
        out_specs=[row, row],
        out_shape=[jax.ShapeDtypeStruct((TP, D), F32), jax.ShapeDtypeStruct((TP, D), BF16)],
        scratch_shapes=[pltpu.VMEM((2, TM, D), F32), pltpu.SemaphoreType.DMA((2,))],
        compiler_params=_cparams(),
    )(x, meta_full, pre_w)


def _inproj_cols(name, shards, hn, w_land, b_in, z_prev):
    nsh = shards.shape[0]
    one_shard = w_land.shape[1] == _WCOLS

    def body(idx_ref, hn_ref, w_ref, b_ref, *rest):
        z_ref = rest[-2]
        z_ref[...] = jnp.dot(hn_ref[...], w_ref[...], preferred_element_type=F32) + b_ref[...]

    any_spec = pl.BlockSpec(memory_space=pl.ANY)
    in_specs = [pl.BlockSpec((TM, D), lambda j, i, idx: (i, 0)),
                pl.BlockSpec((D, _WCOLS), lambda j, i, idx: (0, 0 if one_shard else idx[j])),
                pl.BlockSpec((1, _WCOLS), lambda j, i, idx: (0, idx[j]))]
    operands = [hn, w_land, b_in]
    aliases = {2: 1}
    if z_prev is not None:
        in_specs.append(any_spec)
        operands.append(z_prev)
        aliases[4] = 0
    return pl.pallas_call(
        body, name=name,
        grid_spec=pltpu.PrefetchScalarGridSpec(
            num_scalar_prefetch=1, grid=(nsh, TP // TM), in_specs=in_specs,
            out_specs=[pl.BlockSpec((TM, _WCOLS), lambda j, i, idx: (i, idx[j])), any_spec]),
        out_shape=[jax.ShapeDtypeStruct((TP, NIN), F32), jax.ShapeDtypeStruct(w_land.shape, w_land.dtype)],
        input_output_aliases=aliases,
        compiler_params=_cparams(),
    )(shards, *operands)


def _gate_values(ga, gx, xc, sp8):
    r = _sig(ga)
    i = _sig(gx)
    log_a = -(r * sp8)
    a = jnp.exp(log_a)
    mult = jnp.sqrt(-_expm1_neg(2.0 * log_a))
    return r, i, a, mult


def _lru_fwd(z, conv_w, conv_b, wa_g, b_a, wx_g, b_x, lam):
    def body(x_ref, g_ref, cw_ref, cb_ref, wa_ref, ba_ref, wx_ref, bx_ref, lam_ref,
             y_ref, xc_ref, hs_ref, ga_s, gx_s):
        taps = [cw_ref[k:k + 1, :] for k in range(LW)]
        cb = cb_ref[...]

        def conv_chunk(ci, carry):
            r0 = pl.multiple_of(ci * R, R)
            cur = x_ref[pl.ds(r0, R), :]
            p0 = pl.multiple_of(jnp.maximum(r0 - 8, 0), 8)
            prev = jnp.where(ci > 0, x_ref[pl.ds(p0, 8), :], 0.0)
            buf = jnp.concatenate([prev, cur], axis=0)
            acc = cur * taps[LW - 1] + cb
            for s in range(1, LW):
                acc = acc + pltpu.roll(buf, s, 0)[8:8 + R, :] * taps[LW - 1 - s]
            xc_ref[pl.ds(r0, R), :] = acc
            return carry
        lax.fori_loop(0, TP // R, conv_chunk, 0)

        def gate_chunk(ci, carry):
            r0 = pl.multiple_of(ci * TM, TM)
            xb = xc_ref[pl.ds(r0, TM), :].astype(BF16)
            ga_s[pl.ds(r0, TM), :] = jnp.dot(xb, wa_ref[...], preferred_element_type=F32) + ba_ref[...]
            gx_s[pl.ds(r0, TM), :] = jnp.dot(xb, wx_ref[...], preferred_element_type=F32) + bx_ref[...]
            return carry
        lax.fori_loop(0, TP // TM, gate_chunk, 0)

        sp8 = LRU_C * _softplus(-lam_ref[...])
        row = _row_iota((R, CB))

        def scan_chunk(ci, hprev):
            r0 = pl.multiple_of(ci * R, R)
            xc = xc_ref[pl.ds(r0, R), :]
            _, i, a, mult = _gate_values(ga_s[pl.ds(r0, R), :], gx_s[pl.ds(r0, R), :], xc, sp8)
            u = mult * (i * xc)
            k = 1
            while k < R:
                m = row >= k
                u = jnp.where(m, a * pltpu.roll(u, k, 0) + u, u)
                a = jnp.where(m, a * pltpu.roll(a, k, 0), a)
                k *= 2
            hv = u + a * hprev
            hs_ref[pl.ds(r0, R), :] = hv
            g = g_ref[pl.ds(r0, R), :]
            y_ref[pl.ds(r0, R), :] = (hv * (g * _sig(g))).astype(BF16)
            return jnp.sum(jnp.where(row == R - 1, hv, 0.0), axis=0, keepdims=True)
        def scan_pass(i, hp):
            for sub in range(4):
                hp = scan_chunk(4 * i + sub, hp)
            return hp
        lax.fori_loop(0, TP // R // 4, scan_pass, jnp.zeros((1, CB), F32))

    col = lambda off: pl.BlockSpec((TP, CB), lambda j: (0, off + j))
    vec = pl.BlockSpec((1, CB), lambda j: (0, j))
    wsp = pl.BlockSpec((None, CB, CB), lambda j: (j, 0, 0))
    return pl.pallas_call(
        body, name="lru_fwd",
        grid=(NCB,),
        in_specs=[col(0), col(NCB), pl.BlockSpec((LW, CB), lambda j: (0, j)), vec, wsp, vec, wsp, vec, vec],
        out_specs=[col(0), col(0), col(0)],
        out_shape=[jax.ShapeDtypeStruct((TP, DL), BF16), jax.ShapeDtypeStruct((TP, DL), F32),
                   jax.ShapeDtypeStruct((TP, DL), F32)],
        scratch_shapes=[pltpu.VMEM((TP, CB), F32), pltpu.VMEM((TP, CB), F32)],
        compiler_params=_cparams(),
    )(z, z, conv_w, conv_b, wa_g, b_a, wx_g, b_x, lam)


CBC = 128
NCBC = DC // CBC
RC = 64


def _fold_rows(v):
    acc = v[0:8, :]
    for r in range(8, v.shape[0], 8):
        acc = acc + v[r:r + 8, :]
    return acc


def _conf_fwd_conv(z, dw_w, dw_b):
    def body(u1_ref, u2_ref, w_ref, b_ref, vc_ref, vs):
        vs[pl.ds(0, KWP), :] = jnp.zeros((KWP, CBC), F32)

        def glu_chunk(ci, carry):
            r0 = pl.multiple_of(ci * RC, RC)
            vs[pl.ds(KWP + r0, RC), :] = u1_ref[pl.ds(r0, RC), :] * _sig(u2_ref[pl.ds(r0, RC), :])
            return carry
        lax.fori_loop(0, TP // RC, glu_chunk, 0)

        bias = b_ref[...]

        def conv_chunk(ci, carry):
            r0 = pl.multiple_of(ci * RC, RC)
            buf = vs[pl.ds(r0, KWP + RC), :]
            acc = jnp.zeros((RC, CBC), F32) + bias
            for rr in range(8):
                rolled = buf if rr == 0 else pltpu.roll(buf, rr, 0)
                for q in range(4):
                    s = 8 * q + rr
                    if s > KW - 1:
                        continue
                    k = KW - 1 - s
                    acc = acc + rolled[KWP - 8 * q:KWP - 8 * q + RC, :] * w_ref[k:k + 1, :]
            vc_ref[pl.ds(r0, RC), :] = acc
            return carry
        lax.fori_loop(0, TP // RC, conv_chunk, 0)

    return pl.pallas_call(
        body, name="conf_fwd_conv",
        grid=(NCBC,),
        in_specs=[pl.BlockSpec((TP, CBC), lambda j: (0, 2 * NCBC + j)),
                  pl.BlockSpec((TP, CBC), lambda j: (0, 3 * NCBC + j)),
                  pl.BlockSpec((KWP, CBC), lambda j: (0, j)),
                  pl.BlockSpec((1, CBC), lambda j: (0, j))],
        out_specs=pl.BlockSpec((TP, CBC), lambda j: (0, j)),
        out_shape=jax.ShapeDtypeStruct((TP, DC), F32),
        scratch_shapes=[pltpu.VMEM((TP + KWP, CBC), F32)],
        compiler_params=_cparams(),
    )(z, z, dw_w, dw_b)


def _ln_chunk(vc, lw, lb):
    mu = jnp.mean(vc, axis=-1, keepdims=True)
    xm = vc - mu
    var = jnp.mean(xm * xm, axis=-1, keepdims=True)
    rstd = lax.rsqrt(var + EPS)
    xhat = xm * rstd
    return xhat, rstd, xhat * lw + lb


def _conf_fwd_proj(vc, z, ln_w, ln_b, pw_w, pw_b):
    def body(vc_ref, g_ref, lw_ref, lb_ref, w_ref, b_ref, y_ref, p_ref, s_s):
        lw, lb = lw_ref[...], lb_ref[...]

        def ln_chunk(ci, carry):
            r0 = pl.multiple_of(ci * R, R)
            for half in range(2):
                rr = r0 + 8 * half
                _, _, ln = _ln_chunk(vc_ref[pl.ds(rr, 8), :], lw, lb)
                p_ref[pl.ds(rr, 8), :] = ln * _sig(ln)
            s_s[pl.ds(r0, R), :] = p_ref[pl.ds(r0, R), :].astype(BF16)
            return carry
        lax.fori_loop(0, TM // R, ln_chunk, 0, unroll=2)

        p_ref[...] = jnp.dot(s_s[...], w_ref[...], preferred_element_type=F32) + b_ref[...]

        def out_chunk(ci, carry):
            r0 = pl.multiple_of(ci * R, R)
            g = g_ref[pl.ds(r0, R), :]
            y_ref[pl.ds(r0, R), :] = (p_ref[pl.ds(r0, R), :] * (g * _sig(g))).astype(BF16)
            return carry
        lax.fori_loop(0, TM // R, out_chunk, 0)

    row = pl.BlockSpec((TM, DC), lambda i: (i, 0))
    vec = pl.BlockSpec((1, DC), lambda i: (0, 0))
    return pl.pallas_call(
        body, name="conf_fwd_proj",
        grid=(TP // TM,),
        in_specs=[row, pl.BlockSpec((TM, DC), lambda i: (i, 4)), vec, vec,
                  pl.BlockSpec((DC, DC), lambda i: (0, 0)), vec],
        out_specs=[row, row],
        out_shape=[jax.ShapeDtypeStruct((TP, DC), BF16), jax.ShapeDtypeStruct((TP, DC), F32)],
        scratch_shapes=[pltpu.VMEM((TM, DC), BF16)],
        compiler_params=_cparams(),
    )(vc, z, ln_w, ln_b, pw_w, pw_b)


def _outproj_loss(ylru, yconf, w_out, h, target, post_w):
    def body(yl_ref, yc_ref, w_ref, h_ref, tgt_hbm, pw_ref, dout_ref, dy_ref, loss_ref, dpw_ref, y_s, t_ref, sem):
        i = pl.program_id(0)
        k = pl.program_id(1)

        @pl.when(k == 0)
        def _():
            _for_tile(i, lambda t: _token_tile_copy(tgt_hbm, t_ref, sem, t).start())
            y_s[...] = jnp.dot(yl_ref[...], w_ref[...], preferred_element_type=F32)

        @pl.when(k == 1)
        def _():
            y_s[...] += jnp.dot(yc_ref[...], w_ref[...], preferred_element_type=F32)

        @pl.when(jnp.logical_and(i == 0, k == 1))
        def _():
            loss_ref[...] = jnp.zeros_like(loss_ref)
            dpw_ref[...] = jnp.zeros_like(dpw_ref)

        @pl.when(k == 1)
        def _():
            _for_tile(i, lambda t: _token_tile_copy(tgt_hbm, t_ref, sem, t).wait())

            @pl.when(i == 0)
            def _():
                t_ref[0:NMETA, :] = jnp.zeros((NMETA, D), F32)

            @pl.when(i == NTILE - 1)
            def _():
                last = _tile_rows(NTILE - 1)[1]
                t_ref[last:TM, :] = jnp.zeros((TM - last, D), F32)

            pw = pw_ref[...]
            row = _row_iota((8, D))

            def chunk(ci, carry):
                r0 = pl.multiple_of(ci * 8, 8)
                yv = y_s[pl.ds(r0, 8), :]
                rs = lax.rsqrt(jnp.mean(yv * yv, axis=-1, keepdims=True) + EPS)
                grow = row + (i * TM + r0)
                valid = jnp.logical_and(grow >= NMETA, grow < T)
                yn = yv * rs
                err = jnp.where(valid, h_ref[pl.ds(r0, 8), :] + yn * pw - t_ref[pl.ds(r0, 8), :], 0.0)
                loss_ref[...] += err * err
                d_rn = err * (1.0 / D)
                dout_ref[pl.ds(r0, 8), :] = d_rn
                dpw_ref[...] += d_rn * yn
                gw = d_rn * pw
                dot = jnp.mean(gw * yv, axis=-1, keepdims=True)
                dy_ref[pl.ds(r0, 8), :] = (rs * gw - yv * (rs * rs * rs * dot)).astype(BF16)
                return carry
            lax.fori_loop(0, TM // 8, chunk, 0, unroll=4)

    row = pl.BlockSpec((TM, D), lambda i, k: (i, 0))
    half = pl.BlockSpec((TM, DL), lambda i, k: (i, 0))
    acc = pl.BlockSpec((8, D), lambda i, k: (0, 0))
    return pl.pallas_call(
        body, name="outproj_loss",
        grid=(TP // TM, 2),
        in_specs=[half, half, pl.BlockSpec((DL, D), lambda i, k: (k, 0)), row, pl.BlockSpec(memory_space=pl.ANY),
                  pl.BlockSpec((1, D), lambda i, k: (0, 0))],
        out_specs=[row, row, acc, acc],
        out_shape=[jax.ShapeDtypeStruct((TP, D), F32), jax.ShapeDtypeStruct((TP, D), BF16),
                   jax.ShapeDtypeStruct((8, D), F32), jax.ShapeDtypeStruct((8, D), F32)],
        scratch_shapes=[pltpu.VMEM((TM, D), F32), pltpu.VMEM((TM, D), F32), pltpu.SemaphoreType.DMA(())],
        compiler_params=_cparams(),
    )(ylru, yconf, w_out, h, target, post_w)


_NT = (((1,), (1,)), ((), ()))
_TN = (((0,), (0,)), ((), ()))


def _outproj_bwd(dy, ylru, yconf, w_out):
    def body(dy_ref, yl_ref, yc_ref, w_ref, dycat_ref, dw_ref):
        j = pl.program_id(0)
        dyv = dy_ref[...]
        dycat_ref[...] = lax.dot_general(dyv, w_ref[...], _NT, preferred_element_type=F32)

        @pl.when(j < NCB)
        def _():
            dw_ref[...] = lax.dot_general(yl_ref[...], dyv, _TN, preferred_element_type=F32).astype(BF16)

        @pl.when(j >= NCB)
        def _():
            dw_ref[...] = lax.dot_general(yc_ref[...], dyv, _TN, preferred_element_type=F32).astype(BF16)

    return pl.pallas_call(
        body, name="outproj_bwd",
        grid=(2 * NCB,),
        in_specs=[pl.BlockSpec((TP, D), lambda j: (0, 0)),
                  pl.BlockSpec((TP, CB), lambda j: (0, jnp.minimum(j, NCB - 1))),
                  pl.BlockSpec((TP, CB), lambda j: (0, jnp.maximum(j - NCB, 0))),
                  pl.BlockSpec((CB, D), lambda j: (j, 0))],
        out_specs=[pl.BlockSpec((TP, CB), lambda j: (0, j)), pl.BlockSpec((CB, D), lambda j: (j, 0))],
        out_shape=[jax.ShapeDtypeStruct((TP, D), F32), jax.ShapeDtypeStruct((D, D), BF16)],
        compiler_params=_cparams(),
    )(dy, ylru, yconf, w_out)


_AFTER = pl.BlockSpec(memory_space=pl.ANY)


def _conf_bwd_proj(dycat, p, z, vc, ln_w, ln_b, pw_w, after):
    def body(dy_ref, p_ref, g_ref, vc_ref, lw_ref, lb_ref, w_ref, after_ref,
             dvc_ref, dgc_ref, dpw_ref, vecs_ref, dp_s, s_s, ds_s):
        i = pl.program_id(0)
        lw, lb = lw_ref[...], lb_ref[...]

        @pl.when(i == 0)
        def _():
            dpw_ref[...] = jnp.zeros_like(dpw_ref)
            vecs_ref[...] = jnp.zeros_like(vecs_ref)

        def pre_chunk(ci, carry):
            r0 = pl.multiple_of(ci * R, R)
            for half in range(2):
                rr = r0 + 8 * half
                dyv = dy_ref[pl.ds(rr, 8), :]
                g = g_ref[pl.ds(rr, 8), :]
                sg = _sig(g)
                dp = dyv * (g * sg)
                dg = dyv * p_ref[pl.ds(rr, 8), :] * (sg * (1.0 + g * (1.0 - sg)))
                vecs_ref[0:8, :] += dp
                vecs_ref[8:16, :] += dg
                ds_s[pl.ds(rr, 8), :] = dp
                dvc_ref[pl.ds(rr, 8), :] = dg
            dp_s[pl.ds(r0, R), :] = ds_s[pl.ds(r0, R), :].astype(BF16)
            dgc_ref[pl.ds(r0, R), :] = dvc_ref[pl.ds(r0, R), :].astype(BF16)
            for half in range(2):
                rr = r0 + 8 * half
                _, _, ln = _ln_chunk(vc_ref[pl.ds(rr, 8), :], lw, lb)
                ds_s[pl.ds(rr, 8), :] = ln * _sig(ln)
            s_s[pl.ds(r0, R), :] = ds_s[pl.ds(r0, R), :].astype(BF16)
            return carry
        lax.fori_loop(0, TM // R, pre_chunk, 0, unroll=2)

        dpb = dp_s[...]
        ds_s[...] = lax.dot_general(dpb, w_ref[...], _NT, preferred_element_type=F32)
        dpw_ref[...] += lax.dot_general(s_s[...], dpb, _TN, preferred_element_type=F32)

        def post_chunk(ci, carry):
            r0 = pl.multiple_of(ci * 8, 8)
            xhat, rstd, ln = _ln_chunk(vc_ref[pl.ds(r0, 8), :], lw, lb)
            sl = _sig(ln)
            dln = ds_s[pl.ds(r0, 8), :] * (sl * (1.0 + ln * (1.0 - sl)))
            vecs_ref[16:24, :] += dln * xhat
            vecs_ref[24:32, :] += dln
            dxh = dln * lw
            m1 = jnp.mean(dxh, axis=-1, keepdims=True)
            m2 = jnp.mean(dxh * xhat, axis=-1, keepdims=True)
            dvc_ref[pl.ds(r0, 8), :] = rstd * (dxh - m1 - xhat * m2)
            return carry
        lax.fori_loop(0, TM // 8, post_chunk, 0, unroll=4)

    row = pl.BlockSpec((TM, DC), lambda i: (i, 0))
    vec = pl.BlockSpec((1, DC), lambda i: (0, 0))
    return pl.pallas_call(
        body, name="conf_bwd_proj",
        grid=(TP // TM,),
        in_specs=[pl.BlockSpec((TM, DC), lambda i: (i, 1)), row, pl.BlockSpec((TM, DC), lambda i: (i, 4)), row,
                  vec, vec, pl.BlockSpec((DC, DC), lambda i: (0, 0)), _AFTER],
        out_specs=[row, row, pl.BlockSpec((DC, DC), lambda i: (0, 0)), pl.BlockSpec((32, DC), lambda i: (0, 0))],
        out_shape=[jax.ShapeDtypeStruct((TP, DC), F32), jax.ShapeDtypeStruct((TP, DC), BF16),
                   jax.ShapeDtypeStruct((DC, DC), F32), jax.ShapeDtypeStruct((32, DC), F32)],
        scratch_shapes=[pltpu.VMEM((TM, DC), BF16), pltpu.VMEM((TM, DC), BF16), pltpu.VMEM((TM, DC), F32)],
        compiler_params=_cparams(),
    )(dycat, p, z, vc, ln_w, ln_b, pw_w, after)


def _conf_bwd_conv(dvc, z, dw_w, after):
    def body(dvc_ref, u1_ref, u2_ref, w_ref, after_ref, du_ref, dw_ref, vecs_ref, vs, dvs):
        vs[pl.ds(0, KWP), :] = jnp.zeros((KWP, CBC), F32)
        dvs[pl.ds(TP, KWP), :] = jnp.zeros((KWP, CBC), F32)
        dw_ref[...] = jnp.zeros_like(dw_ref)
        vecs_ref[...] = jnp.zeros_like(vecs_ref)

        def fill_chunk(ci, carry):
            r0 = pl.multiple_of(ci * RC, RC)
            vs[pl.ds(KWP + r0, RC), :] = u1_ref[pl.ds(r0, RC), :] * _sig(u2_ref[pl.ds(r0, RC), :])
            dv = dvc_ref[pl.ds(r0, RC), :]
            dvs[pl.ds(r0, RC), :] = dv
            vecs_ref[0:8, :] += _fold_rows(dv)
            return carry
        lax.fori_loop(0, TP // RC, fill_chunk, 0)

        def conv_chunk(ci, carry):
            r0 = pl.multiple_of(ci * RC, RC)
            vbuf = vs[pl.ds(r0, KWP + RC), :]
            dbuf = dvs[pl.ds(r0, KWP + RC), :]
            dcur = dbuf[0:RC, :]
            dv = jnp.zeros((RC, CBC), F32)
            for rr in range(8):
                vroll = vbuf if rr == 0 else pltpu.roll(vbuf, rr, 0)
                droll = dbuf if rr == 0 else pltpu.roll(dbuf, KWP + RC - rr, 0)
                for q in range(4):
                    s = 8 * q + rr
                    if s > KW - 1:
                        continue
                    k = KW - 1 - s
                    dv = dv + droll[8 * q:8 * q + RC, :] * w_ref[k:k + 1, :]
                    dw_ref[8 * k:8 * k + 8, :] += _fold_rows(dcur * vroll[KWP - 8 * q:KWP - 8 * q + RC, :])
            u1 = u1_ref[pl.ds(r0, RC), :]
            sg = _sig(u2_ref[pl.ds(r0, RC), :])
            du1 = dv * sg
            du2 = dv * u1 * (sg * (1.0 - sg))
            du_ref[0, pl.ds(r0, RC), :] = du1.astype(BF16)
            du_ref[1, pl.ds(r0, RC), :] = du2.astype(BF16)
            vecs_ref[8:16, :] += _fold_rows(du1)
            vecs_ref[16:24, :] += _fold_rows(du2)
            return carry
        lax.fori_loop(0, TP // RC, conv_chunk, 0)

    blk = pl.BlockSpec((TP, CBC), lambda j: (0, j))
    return pl.pallas_call(
        body, name="conf_bwd_conv",
        grid=(NCBC,),
        in_specs=[blk, pl.BlockSpec((TP, CBC), lambda j: (0, 2 * NCBC + j)),
                  pl.BlockSpec((TP, CBC), lambda j: (0, 3 * NCBC + j)), pl.BlockSpec((KWP, CBC), lambda j: (0, j)),
                  _AFTER],
        out_specs=[pl.BlockSpec((2, TP, CBC), lambda j: (0, 0, j)), pl.BlockSpec((8 * KWP, CBC), lambda j: (0, j)),
                   pl.BlockSpec((24, CBC), lambda j: (0, j))],
        out_shape=[jax.ShapeDtypeStruct((2, TP, DC), BF16),
                   jax.ShapeDtypeStruct((8 * KWP, DC), F32), jax.ShapeDtypeStruct((24, DC), F32)],
        scratch_shapes=[pltpu.VMEM((TP + KWP, CBC), F32), pltpu.VMEM((TP + KWP, CBC), F32)],
        compiler_params=_cparams(),
    )(dvc, z, z, dw_w, after)


def _lru_bwd(dycat, z, xc, hs, conv_w, wa_g, b_a, wx_g, b_x, lam, after):
    NV = 6

    def body(dy_ref, x_ref, g_ref, xc_ref, hs_ref, cw_ref, wa_ref, ba_ref, wx_ref, bx_ref, lam_ref, after_ref,
             dzl_ref, dwa_ref, dwx_ref, dcw_ref, vecs_ref, ga_s, gx_s, dxc_s):
        vecs_ref[...] = jnp.zeros_like(vecs_ref)
        dcw_ref[...] = jnp.zeros_like(dcw_ref)
        dxc_s[pl.ds(TP, 8), :] = jnp.zeros((8, CB), F32)

        def gate_chunk(ci, carry):
            r0 = pl.multiple_of(ci * TM, TM)
            xb = xc_ref[pl.ds(r0, TM), :].astype(BF16)
            ga_s[pl.ds(r0, TM), :] = jnp.dot(xb, wa_ref[...], preferred_element_type=F32) + ba_ref[...]
            gx_s[pl.ds(r0, TM), :] = jnp.dot(xb, wx_ref[...], preferred_element_type=F32) + bx_ref[...]
            return carry
        lax.fori_loop(0, TP // TM, gate_chunk, 0)

        sp8 = LRU_C * _softplus(-lam_ref[...])
        row = _row_iota((R, CB))
        nchunk = TP // R

        def scan_chunk(cj, carry):
            a_next, lam_next = carry
            ci = nchunk - 1 - cj
            r0 = pl.multiple_of(ci * R, R)
            dyv = dy_ref[pl.ds(r0, R), :]
            g = g_ref[pl.ds(r0, R), :]
            hv = hs_ref[pl.ds(r0, R), :]
            xc = xc_ref[pl.ds(r0, R), :]
            sg = _sig(g)
            dgl = dyv * hv * (sg * (1.0 + g * (1.0 - sg)))
            dzl_ref[1, pl.ds(r0, R), :] = dgl.astype(BF16)
            vecs_ref[0:8, :] += _fold8(dgl)
            dhs = dyv * (g * sg)
            r, i, a, mult = _gate_values(ga_s[pl.ds(r0, R), :], gx_s[pl.ds(r0, R), :], xc, sp8)
            b = jnp.where(row == R - 1, a_next, pltpu.roll(a, R - 1, 0))
            lv = dhs
            k = 1
            while k < R:
                m = row < R - k
                lv = jnp.where(m, lv + b * pltpu.roll(lv, R - k, 0), lv)
                b = jnp.where(m, b * pltpu.roll(b, R - k, 0), b)
                k *= 2
            lv = lv + b * lam_next
            p0 = pl.multiple_of(jnp.maximum(r0 - 8, 0), 8)
            hprev8 = jnp.where(ci > 0, hs_ref[pl.ds(p0, 8), :], 0.0)
            hprev = pltpu.roll(jnp.concatenate([hprev8, hv], axis=0), 1, 0)[8:8 + R, :]
            da = lv * hprev
            ixc = i * xc
            dmult = lv * ixc
            di = lv * mult * xc
            dxc_s[pl.ds(r0, R), :] = lv * mult * i
            a2 = a * a
            dlog_a = da * a - dmult * a2 / mult
            vecs_ref[32:40, :] += _fold8(dlog_a * r)
            dga = -(dlog_a * sp8) * r * (1.0 - r)
            dgx = di * i * (1.0 - i)
            ga_s[pl.ds(r0, R), :] = dga
            gx_s[pl.ds(r0, R), :] = dgx
            vecs_ref[16:24, :] += _fold8(dga)
            vecs_ref[24:32, :] += _fold8(dgx)
            a_first = jnp.sum(jnp.where(row == 0, a, 0.0), axis=0, keepdims=True)
            l_first = jnp.sum(jnp.where(row == 0, lv, 0.0), axis=0, keepdims=True)
            return a_first, l_first
        lax.fori_loop(0, nchunk // 2, lambda i, cr: scan_chunk(2 * i + 1, scan_chunk(2 * i, cr)),
                      (jnp.zeros((1, CB), F32), jnp.zeros((1, CB), F32)))

        dwa_ref[...] = jnp.zeros_like(dwa_ref)
        dwx_ref[...] = jnp.zeros_like(dwx_ref)

        def mm_chunk(ci, carry):
            r0 = pl.multiple_of(ci * TM, TM)
            xb = xc_ref[pl.ds(r0, TM), :].astype(BF16)
            dgab = ga_s[pl.ds(r0, TM), :].astype(BF16)
            dgxb = gx_s[pl.ds(r0, TM), :].astype(BF16)
            dxc_s[pl.ds(r0, TM), :] += (lax.dot_general(dgab, wa_ref[...], _NT, preferred_element_type=F32)
                                        + lax.dot_general(dgxb, wx_ref[...], _NT, preferred_element_type=F32))
            dwa_ref[...] += lax.dot_general(xb, dgab, _TN, preferred_element_type=F32)
            dwx_ref[...] += lax.dot_general(xb, dgxb, _TN, preferred_element_type=F32)
            return carry
        lax.fori_loop(0, TP // TM, mm_chunk, 0)

        taps = [cw_ref[k:k + 1, :] for k in range(LW)]

        def conv_chunk(ci, carry):
            r0 = pl.multiple_of(ci * R, R)
            dbuf = dxc_s[pl.ds(r0, R + 8), :]
            dcur = dbuf[0:R, :]
            p0 = pl.multiple_of(jnp.maximum(r0 - 8, 0), 8)
            xprev = jnp.where(ci > 0, x_ref[pl.ds(p0, 8), :], 0.0)
            xbuf = jnp.concatenate([xprev, x_ref[pl.ds(r0, R), :]], axis=0)
            dxl = dcur * taps[LW - 1]
            dcw_ref[8 * (LW - 1):8 * LW, :] += _fold8(dcur * xbuf[8:8 + R, :])
            for s in range(1, LW):
                k = LW - 1 - s
                dxl = dxl + pltpu.roll(dbuf, R + 8 - s, 0)[0:R, :] * taps[k]
                dcw_ref[8 * k:8 * k + 8, :] += _fold8(dcur * pltpu.roll(xbuf, s, 0)[8:8 + R, :])
            dzl_ref[0, pl.ds(r0, R), :] = dxl.astype(BF16)
            vecs_ref[8:16, :] += _fold8(dxl)
            vecs_ref[40:48, :] += _fold8(dcur)
            return carry
        lax.fori_loop(0, TP // R, conv_chunk, 0)
        vecs_ref[32:40, :] = vecs_ref[32:40, :] * (LRU_C * _sig(-lam_ref[...]))

    col = lambda off: pl.BlockSpec((TP, CB), lambda j: (0, off + j))
    vec = pl.BlockSpec((1, CB), lambda j: (0, j))
    wsp = pl.BlockSpec((None, CB, CB), lambda j: (j, 0, 0))
    return pl.pallas_call(
        body, name="lru_bwd",
        grid=(NCB,),
        in_specs=[col(0), col(0), col(NCB), col(0), col(0), pl.BlockSpec((LW, CB), lambda j: (0, j)),
                  wsp, vec, wsp, vec, vec, _AFTER],
        out_specs=[pl.BlockSpec((2, TP, CB), lambda j: (0, 0, j)), wsp, wsp,
                   pl.BlockSpec((8 * LW, CB), lambda j: (0, j)), pl.BlockSpec((8 * NV, CB), lambda j: (0, j))],
        out_shape=[jax.ShapeDtypeStruct((2, TP, DL), BF16),
                   jax.ShapeDtypeStruct((NCB, CB, CB), F32), jax.ShapeDtypeStruct((NCB, CB, CB), F32),
                   jax.ShapeDtypeStruct((8 * LW, DL), F32), jax.ShapeDtypeStruct((8 * NV, DL), F32)],
        scratch_shapes=[pltpu.VMEM((TP, CB), F32), pltpu.VMEM((TP, CB), F32), pltpu.VMEM((TP + 8, CB), F32)],
        compiler_params=_cparams(),
    )(dycat, z, z, xc, hs, conv_w, wa_g, b_a, wx_g, b_x, lam, after)


def _dz_section(sec, dzl_ref, dzc_ref, dgc_ref, use):
    @pl.when(sec < 2)
    def _():
        use(dzl_ref)

    @pl.when(jnp.logical_and(sec >= 2, sec < 4))
    def _():
        use(dzc_ref)

    @pl.when(sec == 4)
    def _():
        use(dgc_ref)


def _dz_specs(rows, index):
    return [pl.BlockSpec((None, rows, 1024), lambda a, b: (jnp.minimum(index(a, b)[1], 1), index(a, b)[0], 0)),
            pl.BlockSpec((None, rows, 1024), lambda a, b: (jnp.clip(index(a, b)[1] - 2, 0, 1), index(a, b)[0], 0)),
            pl.BlockSpec((rows, 1024), lambda a, b: (index(a, b)[0], 0))]


def _inproj_wgrad(name, hn, dzs, after):
    KB = 512
    nsec = dzs.shape[0]

    def body(hn_ref, dz_ref, after_ref, dw_ref):
        dw_ref[...] = lax.dot_general(hn_ref[...], dz_ref[...], _TN, preferred_element_type=F32).astype(BF16)

    return pl.pallas_call(
        body, name=name,
        grid=(nsec, D // KB),
        in_specs=[pl.BlockSpec((TP, KB), lambda n, kb: (0, kb)),
                  pl.BlockSpec((None, TP, 1024), lambda n, kb: (n, 0, 0)), _AFTER],
        out_specs=pl.BlockSpec((KB, 1024), lambda n, kb: (kb, n)),
        out_shape=jax.ShapeDtypeStruct((D, nsec * 1024), BF16),
        compiler_params=_cparams(),
    )(hn, dzs, after)


def _sum_win_parts(parts_a, parts_b, parts_c):
    RB = 64

    def body(a_ref, b_ref, c_ref, o_ref):
        def chunk(ci, carry):
            r0 = pl.multiple_of(ci * R, R)
            for ref, base, ncol in ((a_ref, 0, 2048), (b_ref, 2048, 2048), (c_ref, 4096, 1024)):
                for c0 in range(0, ncol, 512):
                    acc = ref[0, pl.ds(r0, R), c0:c0 + 512].astype(F32)
                    for sidx in range(1, NDEV):
                        acc = acc + ref[sidx, pl.ds(r0, R), c0:c0 + 512].astype(F32)
                    o_ref[pl.ds(r0, R), base + c0:base + c0 + 512] = acc.astype(BF16)
            return carry
        lax.fori_loop(0, RB // R, chunk, 0)

    spec = lambda ncol: pl.BlockSpec((NDEV, RB, ncol), lambda i: (0, i, 0))
    return pl.pallas_call(
        body, name="sum_win_parts",
        grid=(D // NDEV // RB,),
        in_specs=[spec(2048), spec(2048), spec(1024)],
        out_specs=pl.BlockSpec((RB, NIN), lambda i: (i, 0)),
        out_shape=jax.ShapeDtypeStruct((D // NDEV, NIN), BF16),
        compiler_params=_cparams(),
    )(parts_a, parts_b, parts_c)


def _inproj_bwd(dzl, dzc, dgc, w_in, h, dout, pre_w, after):
    nsec = NIN // 1024

    def body(dzl_ref, dzc_ref, dgc_ref, w_ref, h_ref, dout_ref, pw_ref, after_ref, gx_hbm, dmeta_ref, dpw_ref,
             acc_s, dh_s, sem):
        i = pl.program_id(0)
        s = pl.program_id(1)

        def gx_copy(t):
            lo, n, off = _tile_rows(t)
            return pltpu.make_async_copy(dh_s.at[pl.ds(off, n)], gx_hbm.at[pl.ds(lo, n)], sem)

        @pl.when(s == 0)
        def _():
            acc_s[...] = jnp.zeros_like(acc_s)

        def use(dz_ref):
            acc_s[...] += lax.dot_general(dz_ref[...], w_ref[...], _NT, preferred_element_type=F32)
        _dz_section(s, dzl_ref, dzc_ref, dgc_ref, use)

        @pl.when(jnp.logical_and(i == 0, s == nsec - 1))
        def _():
            dpw_ref[...] = jnp.zeros_like(dpw_ref)

        @pl.when(s == nsec - 1)
        def _():
            _for_tile(i - 1, lambda t: gx_copy(t).wait())
            pw = pw_ref[...]

            def chunk(ci, carry):
                r0 = pl.multiple_of(ci * 8, 8)
                hv = h_ref[pl.ds(r0, 8), :]
                dhn = acc_s[pl.ds(r0, 8), :]
                rs = lax.rsqrt(jnp.mean(hv * hv, axis=-1, keepdims=True) + EPS)
                dpw_ref[...] += dhn * (hv * rs)
                gw = dhn * pw
                dot = jnp.mean(gw * hv, axis=-1, keepdims=True)
                dh_s[pl.ds(r0, 8), :] = rs * gw - hv * (rs * rs * rs * dot) + dout_ref[pl.ds(r0, 8), :]
                return carry
            lax.fori_loop(0, TM // 8, chunk, 0, unroll=4)
            _for_tile(i, lambda t: gx_copy(t).start())

            @pl.when(i == 0)
            def _():
                dmeta_ref[...] = dh_s[0:NMETA, :]

            @pl.when(i == NTILE - 1)
            def _():
                gx_copy(NTILE - 1).wait()

    row = pl.BlockSpec((TM, D), lambda i, s: (i, 0))
    return pl.pallas_call(
        body, name="inproj_bwd",
        grid=(TP // TM, nsec),
        in_specs=_dz_specs(TM, lambda i, s: (i, s)) + [
            pl.BlockSpec((D, 1024), lambda i, s: (0, s)), row, row, pl.BlockSpec((1, D), lambda i, s: (0, 0)),
            _AFTER],
        out_specs=[pl.BlockSpec(memory_space=pl.ANY), pl.BlockSpec((NMETA, D), lambda i, s: (0, 0)),
                   pl.BlockSpec((8, D), lambda i, s: (0, 0))],
        out_shape=[jax.ShapeDtypeStruct((SEQ, D), F32), jax.ShapeDtypeStruct((NMETA, D), F32),
                   jax.ShapeDtypeStruct((8, D), F32)],
        scratch_shapes=[pltpu.VMEM((TM, D), F32), pltpu.VMEM((TM, D), F32), pltpu.SemaphoreType.DMA(())],
        compiler_params=_cparams(),
    )(dzl, dzc, dgc, w_in, h, dout, pre_w, after)


def _adamw(name, parts, w, m, v, block_rows):
    rows, cols = w.shape
    nparts = parts.shape[0]
    cw = cols if cols <= 640 else 512

    def body(p_ref, w_ref, m_ref, v_ref, g_ref, d_ref, nm_ref, nv_ref):
        def chunk(ci, carry):
            r0 = pl.multiple_of(ci * R, R)
            for c0 in range(0, cols, cw):
                at = (pl.ds(r0, R), slice(c0, c0 + cw))
                g = p_ref[(0,) + at].astype(F32)
                for sidx in range(1, nparts):
                    g = g + p_ref[(sidx,) + at].astype(F32)
                delta, mv, vv = _adam_math(g, w_ref[at], m_ref[at], v_ref[at])
                g_ref[at] = g
                nm_ref[at] = mv
                nv_ref[at] = vv
                d_ref[at] = delta
            return carry
        lax.fori_loop(0, block_rows // R, chunk, 0)

    blk = pl.BlockSpec((block_rows, cols), lambda i: (i, 0))
    shp = jax.ShapeDtypeStruct((rows, cols), F32)
    return pl.pallas_call(
        body, name=name,
        grid=(rows // block_rows,),
        in_specs=[pl.BlockSpec((nparts, block_rows, cols), lambda i: (0, i, 0)), blk, blk, blk],
        out_specs=[blk, blk, blk, blk],
        out_shape=[shp, shp, shp, shp],
        compiler_params=_cparams(),
    )(parts, w, m, v)


def _adam_math(g, w, m, v):
    c1 = 1.0 / (1.0 - ADAM_B1 ** ADAM_STEP)
    c2 = 1.0 / (1.0 - ADAM_B2 ** ADAM_STEP)
    mv = ADAM_B1 * m + (1.0 - ADAM_B1) * g
    vv = ADAM_B2 * v + (1.0 - ADAM_B2) * (g * g)
    upd = (mv * c1) / (jnp.sqrt(vv * c2) + ADAM_EPS) + ADAM_WD * w
    return -ADAM_LR * upd, mv, vv


_VEC = [("pre_norm_w", 2), ("post_norm_w", 2), ("b_in", 5), ("lru_conv_b", 1), ("b_gate_a", 1), ("b_gate_x", 1),
        ("lru_lambda", 1), ("conf_dw_b", 1), ("conf_ln_w", 1), ("conf_ln_b", 1), ("conf_pw_b", 1)]
_VEC_ROWS = 24
_LOSS_ROW = 17
_SM_ROWS = 64


def _pack_grads(dprew_acc, dpostw_acc, cvecs, kvecs, lvecs, dcw_acc, ddw_acc, dh, loss_acc):
    def body(pre_ref, post_ref, c_ref, k_ref, l_ref, dcw_ref, ddw_ref, dh_ref, loss_ref, vec_ref, small_ref, tmp):
        s8 = lambda ref, r: jnp.sum(ref[8 * r:8 * r + 8, :], axis=0, keepdims=True)
        vec_ref[...] = jnp.zeros_like(vec_ref)
        pre, post = s8(pre_ref, 0), s8(post_ref, 0)
        rows = [pre[:, 0:1024], pre[:, 1024:2048], post[:, 0:1024], post[:, 1024:2048],
                s8(l_ref, 1), s8(l_ref, 0), s8(k_ref, 1), s8(k_ref, 2), s8(c_ref, 1),
                s8(l_ref, 5), s8(l_ref, 2), s8(l_ref, 3), s8(l_ref, 4),
                s8(k_ref, 0), s8(c_ref, 2), s8(c_ref, 3), s8(c_ref, 0)]
        for r, val in enumerate(rows):
            vec_ref[r:r + 1, :] = val
        vec_ref[_LOSS_ROW:_LOSS_ROW + 1, :] = jnp.zeros((1, 1024), F32) + (0.5 / D) * jnp.sum(loss_ref[...])

        small_ref[...] = jnp.zeros_like(small_ref)
        for k in range(LW):
            tmp[k:k + 1, :] = s8(dcw_ref, k)
        for k in range(KW):
            tmp[8 + k:9 + k, :] = s8(ddw_ref, k)
        for d in range(NDEV):
            small_ref[d, 0:LW, 0:128] = tmp[0:LW, 128 * d:128 * d + 128]
            small_ref[d, 8:8 + KW, 0:128] = tmp[8:8 + KW, 128 * d:128 * d + 128]
            small_ref[d, 40:56, :] = dh_ref[:, 256 * d:256 * d + 256]

    full = lambda a: pl.BlockSpec(a.shape, lambda i: (0,) * a.ndim)
    ins = [dprew_acc, dpostw_acc, cvecs, kvecs, lvecs, dcw_acc, ddw_acc]
    return pl.pallas_call(
        body, name="pack_grads",
        grid=(1,),
        in_specs=[full(a) for a in ins] + [full(dh), full(loss_acc)],
        out_specs=[pl.BlockSpec((_VEC_ROWS, 1024), lambda i: (0, 0)),
                   pl.BlockSpec((NDEV, _SM_ROWS, 256), lambda i: (0, 0, 0))],
        out_shape=[jax.ShapeDtypeStruct((_VEC_ROWS, 1024), F32), jax.ShapeDtypeStruct((NDEV, _SM_ROWS, 256), F32)],
        scratch_shapes=[pltpu.VMEM((40, 1024), F32)],
        compiler_params=_cparams(),
    )(*ins, dh, loss_acc)


def _adamw_vec(parts, W, M, V):
    nv = len(_VEC)

    def body(*refs):
        p_ref = refs[0]
        w_refs, m_refs, v_refs = refs[1:1 + nv], refs[1 + nv:1 + 2 * nv], refs[1 + 2 * nv:1 + 3 * nv]
        outs = refs[1 + 3 * nv:]

        def total(r):
            acc = p_ref[0, r:r + 1, :]
            for sidx in range(1, NDEV):
                acc = acc + p_ref[sidx, r:r + 1, :]
            return acc

        row = 0
        for idx, (_, nrows) in enumerate(_VEC):
            for part in range(nrows):
                cols = slice(1024 * part, 1024 * part + 1024)
                g = total(row + part)
                delta, mv, vv = _adam_math(g, w_refs[idx][:, cols], m_refs[idx][:, cols], v_refs[idx][:, cols])
                for o, val in zip(outs[4 * idx:4 * idx + 4], (g, delta, mv, vv)):
                    o[:, cols] = val
            row += nrows
        outs[-1][...] = total(_LOSS_ROW)[:, 0:128]

    names = [n for n, _ in _VEC]
    flat = lambda d: [d[n].reshape(1, -1) for n in names]
    ws, ms, vs = flat(W), flat(M), flat(V)
    res = pl.pallas_call(
        body, name="adamw_vec",
        out_shape=[jax.ShapeDtypeStruct(w.shape, F32) for w in ws for _ in range(4)]
        + [jax.ShapeDtypeStruct((1, 128), F32)],
        compiler_params=_cparams(),
    )(parts, *ws, *ms, *vs)
    return {n: tuple(res[4 * i:4 * i + 4]) for i, n in enumerate(names)}, res[-1]


def _adamw_small(parts, W, M, V):
    where = {"lru_conv_w": (slice(0, LW), slice(0, 128)), "conf_dw_w": (slice(8, 8 + KW), slice(0, 128)),
             "meta_tokens": (slice(40, 56), slice(0, 256))}
    names = list(where)

    def body(*refs):
        p_ref = refs[0]
        outs = refs[10:]
        for idx, n in enumerate(names):
            rs, cs = where[n]
            g = p_ref[0, rs, cs]
            for sidx in range(1, NDEV):
                g = g + p_ref[sidx, rs, cs]
            delta, mv, vv = _adam_math(g, refs[1 + idx][...], refs[4 + idx][...], refs[7 + idx][...])
            for o, val in zip(outs[4 * idx:4 * idx + 4], (g, delta, mv, vv)):
                o[...] = val

    two_d = lambda a: a.reshape(a.shape[-2:])
    ws, ms, vs = ([two_d(d[n]) for n in names] for d in (W, M, V))
    res = pl.pallas_call(
        body, name="adamw_small",
        out_shape=[jax.ShapeDtypeStruct(w.shape, F32) for w in ws for _ in range(4)],
        compiler_params=_cparams(),
    )(parts, *ws, *ms, *vs)
    return {n: tuple(res[4 * i:4 * i + 4]) for i, n in enumerate(names)}


def _pack_small(lru_cw, dw_w, meta):
    buf = jnp.zeros((_SM_ROWS, 256), F32)
    buf = buf.at[0:LW, 0:128].set(lru_cw)
    buf = buf.at[8:8 + dw_w.shape[0], 0:128].set(dw_w)
    return buf.at[40:56, :].set(meta)


def _block_diag4(w):
    w4 = w.reshape(NCB, 4, 64, 64)
    eye = jnp.eye(4, dtype=w.dtype)
    return jnp.einsum("ghij,hk->ghikj", w4, eye).reshape(NCB, CB, CB)


def _diag_blocks(g):
    g5 = g.reshape(NCB, 4, 64, 4, 64)
    return jnp.stack([g5[:, hh, :, hh, :] for hh in range(4)], axis=1).reshape(16, 64, 64)


def _gate_mats(W):
    return _block_diag4(W["w_gate_a"][0]).astype(BF16), _block_diag4(W["w_gate_x"][0]).astype(BF16)


def _local_step(x, target, meta_full, inproj, out_weights, lru_cw_full, dw_w_full, W, gate_mats, send):
    wa_g, wx_g = gate_mats

    h, hn = _prenorm(x, meta_full, W["pre_norm_w"])
    z, win_full = inproj(hn)
    ylru, xc, hs = _lru_fwd(z, lru_cw_full, W["lru_conv_b"], wa_g, W["b_gate_a"], wx_g, W["b_gate_x"],
                            W["lru_lambda"])
    vc = _conf_fwd_conv(z, dw_w_full, W["conf_dw_b"])
    wout_full, pw_full = out_weights(vc)
    yconf, p = _conf_fwd_proj(vc, z, W["conf_ln_w"], W["conf_ln_b"], pw_full, W["conf_pw_b"])
    dout, dy, loss_acc, dpostw_acc = _outproj_loss(ylru, yconf, wout_full, h, target, W["post_norm_w"])

    dycat, dwout_part = _outproj_bwd(dy, ylru, yconf, wout_full)
    tok = send("w_out", ("w_out", dwout_part))
    dvc, dgc, dpw_part, cvecs = _conf_bwd_proj(dycat, p, z, vc, W["conf_ln_w"], W["conf_ln_b"], pw_full, tok)
    tok = send("w_in_c", ("conf_pw_w", dpw_part), ("w_in_c", _inproj_wgrad("inproj_wgrad_c", hn, dgc[None], dgc)))
    dzc, ddw_acc, kvecs = _conf_bwd_conv(dvc, z, dw_w_full, tok)
    tok = send("w_in_b", ("w_in_b", _inproj_wgrad("inproj_wgrad_b", hn, dzc, dzc)))
    dzl, dwa_g, dwx_g, dcw_acc, lvecs = _lru_bwd(dycat, z, xc, hs, lru_cw_full, wa_g, W["b_gate_a"], wx_g,
                                                 W["b_gate_x"], W["lru_lambda"], tok)
    tok = send("w_gates", ("w_gate_a", _diag_blocks(dwa_g).reshape(16 * 64, 64)),
               ("w_gate_x", _diag_blocks(dwx_g).reshape(16 * 64, 64)))
    tok = send("w_in_a", ("w_in_a", _inproj_wgrad("inproj_wgrad_a", hn, dzl, tok)))
    grad_x, dmeta, dprew_acc = _inproj_bwd(dzl, dzc, dgc, win_full, h, dout, W["pre_norm_w"], tok)

    vec_pack, small_part = _pack_grads(dprew_acc, dpostw_acc, cvecs, kvecs, lvecs, dcw_acc, ddw_acc, dmeta, loss_acc)
    return grad_x, vec_pack, small_part


def kernel(x, meta_tokens, pre_norm_w, post_norm_w, w_in, b_in, lru_conv_w, lru_conv_b, w_gate_a, b_gate_a, w_gate_x, b_gate_x, lru_lambda, conf_dw_w, conf_dw_b, conf_ln_w, conf_ln_b, conf_pw_w, conf_pw_b, w_out, loss_target, m_meta_tokens, m_pre_norm_w, m_post_norm_w, m_w_in, m_b_in, m_lru_conv_w, m_lru_conv_b, m_w_gate_a, m_b_gate_a, m_w_gate_x, m_b_gate_x, m_lru_lambda, m_conf_dw_w, m_conf_dw_b, m_conf_ln_w, m_conf_ln_b, m_conf_pw_w, m_conf_pw_b, m_w_out, v_meta_tokens, v_pre_norm_w, v_post_norm_w, v_w_in, v_b_in, v_lru_conv_w, v_lru_conv_b, v_w_gate_a, v_b_gate_a, v_w_gate_x, v_b_gate_x, v_lru_lambda, v_conf_dw_w, v_conf_dw_b, v_conf_ln_w, v_conf_ln_b, v_conf_pw_w, v_conf_pw_b, v_w_out):
    W = dict(meta_tokens=meta_tokens, pre_norm_w=pre_norm_w, post_norm_w=post_norm_w, w_in=w_in, b_in=b_in,
             lru_conv_w=lru_conv_w, lru_conv_b=lru_conv_b, w_gate_a=w_gate_a, b_gate_a=b_gate_a,
             w_gate_x=w_gate_x, b_gate_x=b_gate_x, lru_lambda=lru_lambda, conf_dw_w=conf_dw_w,
             conf_dw_b=conf_dw_b, conf_ln_w=conf_ln_w, conf_ln_b=conf_ln_b, conf_pw_w=conf_pw_w,
             conf_pw_b=conf_pw_b, w_out=w_out)
    M = dict(meta_tokens=m_meta_tokens, pre_norm_w=m_pre_norm_w, post_norm_w=m_post_norm_w, w_in=m_w_in,
             b_in=m_b_in, lru_conv_w=m_lru_conv_w, lru_conv_b=m_lru_conv_b, w_gate_a=m_w_gate_a,
             b_gate_a=m_b_gate_a, w_gate_x=m_w_gate_x, b_gate_x=m_b_gate_x, lru_lambda=m_lru_lambda,
             conf_dw_w=m_conf_dw_w, conf_dw_b=m_conf_dw_b, conf_ln_w=m_conf_ln_w, conf_ln_b=m_conf_ln_b,
             conf_pw_w=m_conf_pw_w, conf_pw_b=m_conf_pw_b, w_out=m_w_out)
    V = dict(meta_tokens=v_meta_tokens, pre_norm_w=v_pre_norm_w, post_norm_w=v_post_norm_w, w_in=v_w_in,
             b_in=v_b_in, lru_conv_w=v_lru_conv_w, lru_conv_b=v_lru_conv_b, w_gate_a=v_w_gate_a,
             b_gate_a=v_b_gate_a, w_gate_x=v_w_gate_x, b_gate_x=v_b_gate_x, lru_lambda=v_lru_lambda,
             conf_dw_w=v_conf_dw_w, conf_dw_b=v_conf_dw_b, conf_ln_w=v_conf_ln_w, conf_ln_b=v_conf_ln_b,
             conf_pw_w=v_conf_pw_w, conf_pw_b=v_conf_pw_b, w_out=v_w_out)
    names = list(W.keys())
    shapes = {n: W[n].shape for n in names}

    small = _pack_small(lru_conv_w[0], conf_dw_w[0], meta_tokens)
    (small_flight,), tok = _exchange_start("gather_small_start", [
        (small, jax.ShapeDtypeStruct((NDEV, _SM_ROWS, 256), F32), _whole, _slot)])
    win_flight, tok = _win_gather_start(w_in[0].astype(BF16) + tok[0, 0].astype(BF16))
    gate_mats = _gate_mats(W)
    wout_shard = w_out[0].astype(BF16) + tok[0, 0].astype(BF16)
    pw_shard = conf_pw_w[0].astype(BF16)
    cast_done = (gate_mats[0][0, 0:8, 0:128] + gate_mats[1][0, 0:8, 0:128]
                 + wout_shard[0:8, 0:128] + pw_shard[0:8, 0:128])
    win_flight, tok = _win_gather_links(win_flight, cast_done)
    gathered, tok = _exchange_start("gather_out_start", [
        (wout_shard + tok[0, 0].astype(BF16), jax.ShapeDtypeStruct((D, D), BF16), _whole, _rows(D // NDEV)),
        (pw_shard, jax.ShapeDtypeStruct((DC, DC), BF16), _whole, _rows(DC // NDEV)),
    ])
    (small_all,) = _exchange_wait("gather_small_wait", [small_flight], tok)
    unshard = lambda a: jnp.transpose(a, (1, 0, 2)).reshape(a.shape[1], -1)
    lru_cw_full = unshard(small_all[:, 0:LW, 0:128])
    dw_w_full = unshard(small_all[:, 8:8 + KWP, 0:128])
    meta_full = unshard(small_all[:, 40:56, :])

    def out_weights(after):
        return _exchange_wait("gather_out_wait", gathered, after)

    def inproj(hn):
        xi, yi, ci = lax.axis_index("x"), lax.axis_index("y"), lax.axis_index("c")
        shard = lambda px, py, pc: (4 * px + 2 * py + pc).astype(jnp.int32)
        beside = jnp.stack([shard(1 - xi, yi, ci), shard(xi, 1 - yi, ci)])
        across = jnp.stack([shard(1 - xi, 1 - yi, ci)])
        other_core = 1 - 2 * ci
        z, src = _inproj_cols("inproj_own", jnp.stack([shard(xi, yi, ci)]), hn, win_flight["src"], b_in, None)
        flight = _win_gather_early(dict(win_flight, src=src))
        z, land = _inproj_cols("inproj_here", jnp.stack([shard(xi, yi, 1 - ci)]), hn, flight["land"], b_in, z)
        flight = _win_gather_forward("n", dict(flight, land=land), (1, 2), z)
        z, land = _inproj_cols("inproj_beside", beside, hn, flight["land"], b_in, z)
        flight = _win_gather_forward("d", dict(flight, land=land), (3,), z)
        z, land = _inproj_cols("inproj_across", across, hn, flight["land"], b_in, z)
        flight = _win_gather_forwarded("n", dict(flight, land=land), (1, 2))
        z, land = _inproj_cols("inproj_beside_sibling", beside + other_core, hn, flight["land"], b_in, z)
        flight = _win_gather_forwarded("d", dict(flight, land=land), (3,))
        z, land = _inproj_cols("inproj_across_sibling", across + other_core, hn, flight["land"], b_in, z)
        return z, _win_gather_wait(dict(flight, land=land))

    row_stage = lambda ncol: (jax.ShapeDtypeStruct((NDEV, D // NDEV, ncol), BF16), _rows(D // NDEV))
    piece = {"w_in_a": row_stage(2048), "w_in_b": row_stage(2048), "w_in_c": row_stage(1024),
             "w_out": row_stage(D),
             "conf_pw_w": (jax.ShapeDtypeStruct((NDEV, DC // NDEV, DC), BF16), _rows(DC // NDEV)),
             "w_gate_a": (jax.ShapeDtypeStruct((NDEV, 16 * 64, 64), BF16), _whole),
             "w_gate_x": (jax.ShapeDtypeStruct((NDEV, 16 * 64, 64), BF16), _whole)}
    sent = {}

    def send(call, *named_parts):
        handles, token = _exchange_start(
            "scatter_" + call + "_start",
            [(part.astype(BF16), piece[name][0], piece[name][1], _slot) for name, part in named_parts])
        for (name, _), handle in zip(named_parts, handles):
            sent[name] = [handle]
        return token

    grad_x, vec_pack, small_part = _local_step(
        x[0], loss_target[0], meta_full, inproj, out_weights, lru_cw_full, dw_w_full, W, gate_mats, send)
    grad_x = grad_x[None]

    rest, tok = _exchange_start("scatter_rest_start", [
        (small_part, jax.ShapeDtypeStruct((NDEV, _SM_ROWS, 256), F32), _slot, _slot),
        (vec_pack, jax.ShapeDtypeStruct((NDEV, _VEC_ROWS, 1024), F32), _whole, _slot),
    ])
    (parts_c,) = _exchange_wait("scatter_w_in_c_wait", sent["w_in_c"], tok)
    (parts_b,) = _exchange_wait("scatter_w_in_b_wait", sent["w_in_b"], parts_c)
    (parts_a,) = _exchange_wait("scatter_w_in_a_wait", sent["w_in_a"], parts_b)
    win_rows = _sum_win_parts(parts_a, parts_b, parts_c)
    win_stage2, tok = _exchange_start("scatter_w_in_stage2_start", [
        (win_rows, jax.ShapeDtypeStruct((NDEV, D // NDEV, NIN // NDEV), BF16), _cols(NIN // NDEV), _slot)])

    G, DW, NM, NV = {}, {}, {}, {}
    (wout_parts,) = _exchange_wait("scatter_w_out_wait", sent["w_out"], tok)
    G["w_out"], DW["w_out"], NM["w_out"], NV["w_out"] = _adamw("adamw_w_out", wout_parts, w_out[0], m_w_out[0], v_w_out[0], 64)
    (pw_parts,) = _exchange_wait("scatter_conf_pw_w_wait", sent["conf_pw_w"], G["w_out"])
    G["conf_pw_w"], DW["conf_pw_w"], NM["conf_pw_w"], NV["conf_pw_w"] = _adamw(
        "adamw_pw", pw_parts, conf_pw_w[0], m_conf_pw_w[0], v_conf_pw_w[0], 128)
    res = {}
    wa_parts, wx_parts = _exchange_wait("scatter_w_gates_wait", sent["w_gate_a"] + sent["w_gate_x"], G["conf_pw_w"])
    for n, parts in (("w_gate_a", wa_parts), ("w_gate_x", wx_parts)):
        res[n] = _adamw("adamw_" + n, parts, *[d[n].reshape(16 * 64, 64) for d in (W, M, V)], 16 * 64)
    small_parts, vec_parts = _exchange_wait("scatter_rest_wait", rest, res["w_gate_x"][0])
    res.update(_adamw_small(small_parts, W, M, V))
    vec_res, loss_row = _adamw_vec(vec_parts, W, M, V)
    res.update(vec_res)
    (win_sum,) = _exchange_wait("scatter_w_in_stage2_wait", win_stage2, loss_row)
    res["w_in"] = _adamw("adamw_w_in", win_sum.reshape(1, D, NIN // NDEV), w_in[0], m_w_in[0], v_w_in[0], 256)
    for n, vals in res.items():
        for dst, val in zip((G, DW, NM, NV), vals):
            dst[n] = val
    for dst in (G, DW, NM, NV):
        for n in names:
            dst[n] = dst[n].reshape(shapes[n])
    loss = loss_row[0, 0]

    return (loss, grad_x, *[G[n] for n in names], *[DW[n] for n in names],
            *[NM[n] for n in names], *[NV[n] for n in names])
```

```python
import functools

import jax
import jax.numpy as jnp
from jax import lax
from jax.experimental import pallas as pl
from jax.experimental.pallas import tpu as pltpu

F32 = jnp.float32
BF16 = jnp.bfloat16

D = 2048
DL = 1024
DC = 1024
NIN = 5120
NMETA = 16
SEQ = 2048
T = NMETA + SEQ
TP = 2176
TM = 544
CB = 256
NCB = DL // CB
R = 16
KW = 31
KWP = 32
LW = 4
LRU_C = 8.0
EPS = 1e-6
NDEV = 8

ADAM_LR = 0.001
ADAM_B1 = 0.9
ADAM_B2 = 0.999
ADAM_EPS = 1e-08
ADAM_WD = 0.01
ADAM_STEP = 10

VMEM_LIMIT = 56 * 1024 * 1024


def _cparams():
    return pltpu.CompilerParams(vmem_limit_bytes=VMEM_LIMIT)


def _sig(x):
    return 1.0 / (1.0 + jnp.exp(-x))


def _expm1_neg(y):
    poly = y * (1.0 + y * (0.5 + y * (1.0 / 6.0 + y * (1.0 / 24.0 + y * (1.0 / 120.0)))))
    return jnp.where(y > -0.1, poly, jnp.exp(y) - 1.0)


def _softplus(x):
    e = jnp.exp(-jnp.abs(x))
    w = 1.0 + e
    l1p = jnp.where(w == 1.0, e, jnp.log(w) * e / (w - 1.0))
    return jnp.maximum(x, 0.0) + l1p


def _row_iota(shape):
    return lax.broadcasted_iota(jnp.int32, shape, 0)


def _fold8(v):
    return v[0:8, :] + v[8:16, :]


_FLIPS = [(k >> 2 & 1, k >> 1 & 1, k & 1) for k in range(1, NDEV)]
_HBM = pl.BlockSpec(memory_space=pltpu.HBM)
_SEM = pl.BlockSpec(memory_space=pltpu.SEMAPHORE)


def _peers():
    x, y, c = lax.axis_index("x"), lax.axis_index("y"), lax.axis_index("c")
    out = []
    for dx, dy, dc in _FLIPS:
        px = 1 - x if dx else x
        py = 1 - y if dy else y
        pc = 1 - c if dc else c
        out.append(((px, py, pc), 4 * px + 2 * py + pc))
    return 4 * x + 2 * y + c, out


def _exchange_start(name, items):
    n = len(items)

    def body(*refs):
        srcs, lands = refs[:n], refs[n:2 * n]
        outs = refs[2 * n:]
        send_sems, recv_sems, local_sems = outs[:n], outs[n:2 * n], outs[2 * n:3 * n]
        token = outs[-1]
        me, peers = _peers()
        for a in range(n):
            src_at, dst_at = items[a][2], items[a][3]
            pltpu.make_async_copy(src_at(srcs[a], me), dst_at(lands[a], me), local_sems[a]).start()
        for a in range(n):
            src_at, dst_at = items[a][2], items[a][3]
            for k, (pos, peer) in enumerate(peers):
                pltpu.make_async_remote_copy(
                    src_ref=src_at(srcs[a], peer), dst_ref=dst_at(lands[a], me),
                    send_sem=send_sems[a].at[k], recv_sem=recv_sems[a].at[k],
                    device_id=pos, device_id_type=pl.DeviceIdType.MESH).start()
        token[...] = jnp.zeros_like(token)

    srcs = [pltpu.with_memory_space_constraint(it[0], pltpu.HBM) for it in items]
    lands = [pltpu.with_memory_space_constraint(lax.empty(it[1].shape, it[1].dtype), pltpu.HBM) for it in items]
    sem7 = pltpu.SemaphoreType.DMA((NDEV - 1,))
    res = pl.pallas_call(
        body, name=name,
        out_shape=([sem7] * (2 * n) + [pltpu.SemaphoreType.DMA(())] * n
                   + [pltpu.HBM(a.shape, a.dtype) for a in srcs] + [pltpu.HBM(a.shape, a.dtype) for a in lands]
                   + [jax.ShapeDtypeStruct((8, 128), F32)]),
        in_specs=[_HBM] * (2 * n),
        out_specs=[_SEM] * (3 * n) + [_HBM] * (2 * n) + [pl.BlockSpec(memory_space=pltpu.VMEM)],
        input_output_aliases={i: 3 * n + i for i in range(2 * n)},
        compiler_params=pltpu.CompilerParams(has_side_effects=pltpu.SideEffectType.DATAFLOW_SIDE_EFFECTING),
    )(*srcs, *lands)
    handles = [dict(send=res[a], recv=res[n + a], local=res[2 * n + a], src=res[3 * n + a], land=res[4 * n + a],
                    src_at=items[a][2], dst_at=items[a][3]) for a in range(n)]
    return handles, res[-1]


def _wait_bytes(piece, sem):
    pltpu.make_async_copy(piece, piece, sem).wait()


def _exchange_wait(name, handles, after):
    n = len(handles)

    def body(*refs):
        srcs, lands = refs[:n], refs[n:2 * n]
        send_sems, recv_sems, local_sems = refs[2 * n:3 * n], refs[3 * n:4 * n], refs[4 * n:5 * n]
        me, peers = _peers()
        for a in range(n):
            src_at, dst_at = handles[a]["src_at"], handles[a]["dst_at"]
            for k, (pos, peer) in enumerate(peers):
                _wait_bytes(src_at(srcs[a], peer), send_sems[a].at[k])
                _wait_bytes(dst_at(lands[a], peer), recv_sems[a].at[k])
            pltpu.make_async_copy(src_at(srcs[a], me), dst_at(lands[a], me), local_sems[a]).wait()

    srcs = [hd["src"] for hd in handles]
    lands = [hd["land"] for hd in handles]
    res = pl.pallas_call(
        body, name=name,
        out_shape=[pltpu.HBM(a.shape, a.dtype) for a in srcs] + [pltpu.HBM(a.shape, a.dtype) for a in lands],
        in_specs=[_HBM] * (2 * n) + [_SEM] * (3 * n) + [pl.BlockSpec(memory_space=pl.ANY)],
        out_specs=[_HBM] * (2 * n),
        input_output_aliases={i: i for i in range(2 * n)},
        compiler_params=pltpu.CompilerParams(has_side_effects=pltpu.SideEffectType.DATAFLOW_SIDE_EFFECTING),
    )(*srcs, *lands, *[hd["send"] for hd in handles], *[hd["recv"] for hd in handles],
      *[hd["local"] for hd in handles], after)
    return list(res[n:])


_SIDE = pltpu.SideEffectType.DATAFLOW_SIDE_EFFECTING
_WCOLS = NIN // NDEV


def _win_cols(ref, l):
    return ref.at[:, pl.ds(pl.multiple_of(l * _WCOLS, 128), _WCOLS)]


def _win_routes():
    x, y, c = lax.axis_index("x"), lax.axis_index("y"), lax.axis_index("c")
    pos = [(x, y, 1 - c), (1 - x, y, c), (x, 1 - y, c), (1 - x, 1 - y, c)]
    return 4 * x + 2 * y + c, [(p, 4 * p[0] + 2 * p[1] + p[2]) for p in pos]


def _win_gather_start(shard):
    def body(src, land, send_sem, recv_sem, local_sem, src_thru, land_thru, token):
        me, routes = _win_routes()
        pltpu.make_async_copy(src, _win_cols(land, me), local_sem).start()
        pltpu.make_async_remote_copy(src_ref=src, dst_ref=_win_cols(land, me), send_sem=send_sem, recv_sem=recv_sem,
                                     device_id=routes[0][0], device_id_type=pl.DeviceIdType.MESH).start()
        token[...] = jnp.zeros_like(token)

    src = pltpu.with_memory_space_constraint(shard, pltpu.HBM)
    land = pltpu.with_memory_space_constraint(lax.empty((D, NIN), BF16), pltpu.HBM)
    sem = pltpu.SemaphoreType.DMA(())
    res = pl.pallas_call(
        body, name="win_gather_start",
        out_shape=[sem, sem, sem, pltpu.HBM(src.shape, BF16), pltpu.HBM(land.shape, BF16),
                   jax.ShapeDtypeStruct((8, 128), F32)],
        in_specs=[_HBM, _HBM],
        out_specs=[_SEM, _SEM, _SEM, _HBM, _HBM, pl.BlockSpec(memory_space=pltpu.VMEM)],
        input_output_aliases={0: 3, 1: 4},
        compiler_params=pltpu.CompilerParams(has_side_effects=_SIDE),
    )(src, land)
    return dict(send0=res[0], recv0=res[1], local=res[2], src=res[3], land=res[4]), res[5]


def _win_gather_links(hd, after):
    def body(src, land, after_ref, send_sems, recv_sems, src_thru, land_thru, token):
        me, routes = _win_routes()
        for k in (1, 2, 3):
            pltpu.make_async_remote_copy(src_ref=src, dst_ref=_win_cols(land, me), send_sem=send_sems.at[k - 1],
                                         recv_sem=recv_sems.at[k - 1], device_id=routes[k][0],
                                         device_id_type=pl.DeviceIdType.MESH).start()
        token[...] = jnp.zeros_like(token)

    sem3 = pltpu.SemaphoreType.DMA((3,))
    res = pl.pallas_call(
        body, name="win_gather_links",
        out_shape=[sem3, sem3, pltpu.HBM(hd["src"].shape, BF16), pltpu.HBM(hd["land"].shape, BF16),
                   jax.ShapeDtypeStruct((8, 128), F32)],
        in_specs=[_HBM, _HBM, pl.BlockSpec(memory_space=pl.ANY)],
        out_specs=[_SEM, _SEM, _HBM, _HBM, pl.BlockSpec(memory_space=pltpu.VMEM)],
        input_output_aliases={0: 2, 1: 3},
        compiler_params=pltpu.CompilerParams(has_side_effects=_SIDE),
    )(hd["src"], hd["land"], after)
    return dict(hd, send=res[0], recv=res[1], src=res[2], land=res[3]), res[4]


def _win_gather_forward(name, hd, ks, after):
    def body(land, recv_sems, after_ref, land_thru, fsend_sems, frecv_sems):
        me, routes = _win_routes()
        sibling = routes[0][0]
        for n, k in enumerate(ks):
            pos, peer = routes[k]
            piece = _win_cols(land, peer)
            pltpu.make_async_remote_copy(src_ref=piece, dst_ref=piece, send_sem=fsend_sems.at[n],
                                         recv_sem=recv_sems.at[k - 1], device_id=pos,
                                         device_id_type=pl.DeviceIdType.MESH).wait_recv()
            pltpu.make_async_remote_copy(src_ref=piece, dst_ref=piece, send_sem=fsend_sems.at[n],
                                         recv_sem=frecv_sems.at[n], device_id=sibling,
                                         device_id_type=pl.DeviceIdType.MESH).start()

    sems = pltpu.SemaphoreType.DMA((len(ks),))
    res = pl.pallas_call(
        body, name="win_gather_forward_" + name,
        out_shape=[pltpu.HBM(hd["land"].shape, BF16), sems, sems],
        in_specs=[_HBM, _SEM, pl.BlockSpec(memory_space=pl.ANY)],
        out_specs=[_HBM, _SEM, _SEM],
        input_output_aliases={0: 0},
        compiler_params=pltpu.CompilerParams(has_side_effects=_SIDE),
    )(hd["land"], hd["recv"], after)
    return dict(hd, land=res[0], **{"fsend" + name: res[1], "frecv" + name: res[2]})


def _win_gather_forwarded(name, hd, ks):
    def body(land, fsend_sems, frecv_sems, land_thru):
        me, routes = _win_routes()
        sib_c = routes[0][0][2]
        for n, k in enumerate(ks):
            _wait_bytes(_win_cols(land, routes[k][1]), fsend_sems.at[n])
            _wait_bytes(_win_cols(land, 4 * routes[k][0][0] + 2 * routes[k][0][1] + sib_c), frecv_sems.at[n])

    res = pl.pallas_call(
        body, name="win_gather_forwarded_" + name,
        out_shape=[pltpu.HBM(hd["land"].shape, BF16)],
        in_specs=[_HBM, _SEM, _SEM],
        out_specs=[_HBM],
        input_output_aliases={0: 0},
        compiler_params=pltpu.CompilerParams(has_side_effects=_SIDE),
    )(hd["land"], hd["fsend" + name], hd["frecv" + name])
    return dict(hd, land=res[0])


def _win_gather_early(hd):
    def body(src, land, recv_sem, local_sem, src_thru, land_thru):
        me, routes = _win_routes()
        _wait_bytes(_win_cols(land, routes[0][1]), recv_sem)
        pltpu.make_async_copy(src, _win_cols(land, me), local_sem).wait()

    res = pl.pallas_call(
        body, name="win_gather_early",
        out_shape=[pltpu.HBM(hd["src"].shape, BF16), pltpu.HBM(hd["land"].shape, BF16)],
        in_specs=[_HBM, _HBM, _SEM, _SEM],
        out_specs=[_HBM, _HBM],
        input_output_aliases={0: 0, 1: 1},
        compiler_params=pltpu.CompilerParams(has_side_effects=_SIDE),
    )(hd["src"], hd["land"], hd["recv0"], hd["local"])
    return dict(hd, src=res[0], land=res[1])


def _win_gather_wait(hd):
    def body(src, land, send0_sem, send_sems, src_thru, land_thru):
        for k in range(4):
            _wait_bytes(src, send0_sem if k == 0 else send_sems.at[k - 1])

    res = pl.pallas_call(
        body, name="win_gather_wait",
        out_shape=[pltpu.HBM(hd["src"].shape, BF16), pltpu.HBM(hd["land"].shape, BF16)],
        in_specs=[_HBM, _HBM, _SEM, _SEM],
        out_specs=[_HBM, _HBM],
        input_output_aliases={0: 0, 1: 1},
        compiler_params=pltpu.CompilerParams(has_side_effects=_SIDE),
    )(hd["src"], hd["land"], hd["send0"], hd["send"])
    return res[1]


def _whole(ref, l):
    return ref


def _slot(ref, l):
    return ref.at[l]


def _cols(width):
    def at(ref, l):
        return ref.at[:, pl.ds(pl.multiple_of(l * width, 128), width)]
    return at


def _rows(height):
    def at(ref, l):
        return ref.at[pl.ds(pl.multiple_of(l * height, 8), height), :]
    return at


NTILE = TP // TM


def _tile_rows(t):
    lo = max(t * TM - NMETA, 0)
    hi = min((t + 1) * TM - NMETA, SEQ)
    return lo, hi - lo, lo + NMETA - t * TM


def _for_tile(t, fn):
    for static_t in range(NTILE):
        pl.when(t == static_t)(functools.partial(fn, static_t))


def _token_tile_copy(hbm_ref, buf, sem, t):
    lo, n, off = _tile_rows(t)
    return pltpu.make_async_copy(hbm_ref.at[pl.ds(lo, n)], buf.at[pl.ds(off, n)], sem)


def _prenorm(x, meta_full, pre_w):
    def body(x_ref, meta_ref, pw_ref, h_ref, hn_ref, xbuf, sems):
        i = pl.program_id(0)
        slot = i % 2

        def start(t):
            _token_tile_copy(x_ref, xbuf.at[t % 2], sems.at[t % 2], t).start()

        @pl.when(i == 0)
        def _():
            start(0)
        _for_tile(i + 1, start)
        _for_tile(i, lambda t: _token_tile_copy(x_ref, xbuf.at[t % 2], sems.at[t % 2], t).wait())

        @pl.when(i == 0)
        def _():
            xbuf[0, 0:NMETA, :] = meta_ref[...]

        @pl.when(i == NTILE - 1)
        def _():
            last = _tile_rows(NTILE - 1)[1]
            xbuf[(NTILE - 1) % 2, last:TM, :] = jnp.zeros((TM - last, D), F32)

        pw = pw_ref[...]

        def chunk(ci, carry):
            r0 = pl.multiple_of(ci * R, R)
            xv = xbuf[slot, pl.ds(r0, R), :]
            h_ref[pl.ds(r0, R), :] = xv
            ms = jnp.mean(xv * xv, axis=-1, keepdims=True)
            hn_ref[pl.ds(r0, R), :] = (xv * lax.rsqrt(ms + EPS) * pw).astype(BF16)
            return carry
        lax.fori_loop(0, TM // R, chunk, 0, unroll=2)

    row = pl.BlockSpec((TM, D), lambda i: (i, 0))
    return pl.pallas_call(
        body, name="prenorm",
        grid=(NTILE,),
        in_specs=[pl.BlockSpec(memory_space=pl.ANY), pl.BlockSpec((NMETA, D), lambda i: (0, 0)),
                  pl.BlockSpec((1, D), lambda i: (0, 0))],
        out_specs=[row, row],
        out_shape=[jax.ShapeDtypeStruct((TP, D), F32), jax.ShapeDtypeStruct((TP, D), BF16)],
        scratch_shapes=[pltpu.VMEM((2, TM, D), F32), pltpu.SemaphoreType.DMA((2,))],
        compiler_params=_cparams(),
    )(x, meta_full, pre_w)


def _inproj_cols(name, shards, hn, w_land, b_in, z_prev):
    nsh = shards.shape[0]
    one_shard = w_land.shape[1] == _WCOLS

    def body(idx_ref, hn_ref, w_ref, b_ref, *rest):
        z_ref = rest[-2]
        z_ref[...] = jnp.dot(hn_ref[...], w_ref[...], preferred_element_type=F32) + b_ref[...]

    any_spec = pl.BlockSpec(memory_space=pl.ANY)
    in_specs = [pl.BlockSpec((TM, D), lambda j, i, idx: (i, 0)),
                pl.BlockSpec((D, _WCOLS), lambda j, i, idx: (0, 0 if one_shard else idx[j])),
                pl.BlockSpec((1, _WCOLS), lambda j, i, idx: (0, idx[j]))]
    operands = [hn, w_land, b_in]
    aliases = {2: 1}
    if z_prev is not None:
        in_specs.append(any_spec)
        operands.append(z_prev)
        aliases[4] = 0
    return pl.pallas_call(
        body, name=name,
        grid_spec=pltpu.PrefetchScalarGridSpec(
            num_scalar_prefetch=1, grid=(nsh, TP // TM), in_specs=in_specs,
            out_specs=[pl.BlockSpec((TM, _WCOLS), lambda j, i, idx: (i, idx[j])), any_spec]),
        out_shape=[jax.ShapeDtypeStruct((TP, NIN), F32), jax.ShapeDtypeStruct(w_land.shape, w_land.dtype)],
        input_output_aliases=aliases,
        compiler_params=_cparams(),
    )(shards, *operands)


def _gate_values(ga, gx, xc, sp8):
    r = _sig(ga)
    i = _sig(gx)
    log_a = -(r * sp8)
    a = jnp.exp(log_a)
    mult = jnp.sqrt(-_expm1_neg(2.0 * log_a))
    return r, i, a, mult


def _lru_fwd(z, conv_w, conv_b, wa_g, b_a, wx_g, b_x, lam):
    def body(x_ref, g_ref, cw_ref, cb_ref, wa_ref, ba_ref, wx_ref, bx_ref, lam_ref,
             y_ref, xc_ref, hs_ref, ga_s, gx_s):
        taps = [cw_ref[k:k + 1, :] for k in range(LW)]
        cb = cb_ref[...]

        def conv_chunk(ci, carry):
            r0 = pl.multiple_of(ci * R, R)
            cur = x_ref[pl.ds(r0, R), :]
            p0 = pl.multiple_of(jnp.maximum(r0 - 8, 0), 8)
            prev = jnp.where(ci > 0, x_ref[pl.ds(p0, 8), :], 0.0)
            buf = jnp.concatenate([prev, cur], axis=0)
            acc = cur * taps[LW - 1] + cb
            for s in range(1, LW):
                acc = acc + pltpu.roll(buf, s, 0)[8:8 + R, :] * taps[LW - 1 - s]
            xc_ref[pl.ds(r0, R), :] = acc
            return carry
        lax.fori_loop(0, TP // R, conv_chunk, 0)

        def gate_chunk(ci, carry):
            r0 = pl.multiple_of(ci * TM, TM)
            xb = xc_ref[pl.ds(r0, TM), :].astype(BF16)
            ga_s[pl.ds(r0, TM), :] = jnp.dot(xb, wa_ref[...], preferred_element_type=F32) + ba_ref[...]
            gx_s[pl.ds(r0, TM), :] = jnp.dot(xb, wx_ref[...], preferred_element_type=F32) + bx_ref[...]
            return carry
        lax.fori_loop(0, TP // TM, gate_chunk, 0)

        sp8 = LRU_C * _softplus(-lam_ref[...])
        row = _row_iota((R, CB))

        def scan_chunk(ci, hprev):
            r0 = pl.multiple_of(ci * R, R)
            xc = xc_ref[pl.ds(r0, R), :]
            _, i, a, mult = _gate_values(ga_s[pl.ds(r0, R), :], gx_s[pl.ds(r0, R), :], xc, sp8)
            u = mult * (i * xc)
            k = 1
            while k < R:
                m = row >= k
                u = jnp.where(m, a * pltpu.roll(u, k, 0) + u, u)
                a = jnp.where(m, a * pltpu.roll(a, k, 0), a)
                k *= 2
            hv = u + a * hprev
            hs_ref[pl.ds(r0, R), :] = hv
            g = g_ref[pl.ds(r0, R), :]
            y_ref[pl.ds(r0, R), :] = (hv * (g * _sig(g))).astype(BF16)
            return jnp.sum(jnp.where(row == R - 1, hv, 0.0), axis=0, keepdims=True)
        def scan_pass(i, hp):
            for sub in range(4):
                hp = scan_chunk(4 * i + sub, hp)
            return hp
        lax.fori_loop(0, TP // R // 4, scan_pass, jnp.zeros((1, CB), F32))

    col = lambda off: pl.BlockSpec((TP, CB), lambda j: (0, off + j))
    vec = pl.BlockSpec((1, CB), lambda j: (0, j))
    wsp = pl.BlockSpec((None, CB, CB), lambda j: (j, 0, 0))
    return pl.pallas_call(
        body, name="lru_fwd",
        grid=(NCB,),
        in_specs=[col(0), col(NCB), pl.BlockSpec((LW, CB), lambda j: (0, j)), vec, wsp, vec, wsp, vec, vec],
        out_specs=[col(0), col(0), col(0)],
        out_shape=[jax.ShapeDtypeStruct((TP, DL), BF16), jax.ShapeDtypeStruct((TP, DL), F32),
                   jax.ShapeDtypeStruct((TP, DL), F32)],
        scratch_shapes=[pltpu.VMEM((TP, CB), F32), pltpu.VMEM((TP, CB), F32)],
        compiler_params=_cparams(),
    )(z, z, conv_w, conv_b, wa_g, b_a, wx_g, b_x, lam)


CBC = 128
NCBC = DC // CBC
RC = 64


def _fold_rows(v):
    acc = v[0:8, :]
    for r in range(8, v.shape[0], 8):
        acc = acc + v[r:r + 8, :]
    return acc


def _conf_fwd_conv(z, dw_w, dw_b):
    def body(u1_ref, u2_ref, w_ref, b_ref, vc_ref, vs):
        vs[pl.ds(0, KWP), :] = jnp.zeros((KWP, CBC), F32)

        def glu_chunk(ci, carry):
            r0 = pl.multiple_of(ci * RC, RC)
            vs[pl.ds(KWP + r0, RC), :] = u1_ref[pl.ds(r0, RC), :] * _sig(u2_ref[pl.ds(r0, RC), :])
            return carry
        lax.fori_loop(0, TP // RC, glu_chunk, 0)

        bias = b_ref[...]

        def conv_chunk(ci, carry):
            r0 = pl.multiple_of(ci * RC, RC)
            buf = vs[pl.ds(r0, KWP + RC), :]
            acc = jnp.zeros((RC, CBC), F32) + bias
            for rr in range(8):
                rolled = buf if rr == 0 else pltpu.roll(buf, rr, 0)
                for q in range(4):
                    s = 8 * q + rr
                    if s > KW - 1:
                        continue
                    k = KW - 1 - s
                    acc = acc + rolled[KWP - 8 * q:KWP - 8 * q + RC, :] * w_ref[k:k + 1, :]
            vc_ref[pl.ds(r0, RC), :] = acc
            return carry
        lax.fori_loop(0, TP // RC, conv_chunk, 0)

    return pl.pallas_call(
        body, name="conf_fwd_conv",
        grid=(NCBC,),
        in_specs=[pl.BlockSpec((TP, CBC), lambda j: (0, 2 * NCBC + j)),
                  pl.BlockSpec((TP, CBC), lambda j: (0, 3 * NCBC + j)),
                  pl.BlockSpec((KWP, CBC), lambda j: (0, j)),
                  pl.BlockSpec((1, CBC), lambda j: (0, j))],
        out_specs=pl.BlockSpec((TP, CBC), lambda j: (0, j)),
        out_shape=jax.ShapeDtypeStruct((TP, DC), F32),
        scratch_shapes=[pltpu.VMEM((TP + KWP, CBC), F32)],
        compiler_params=_cparams(),
    )(z, z, dw_w, dw_b)


def _ln_chunk(vc, lw, lb):
    mu = jnp.mean(vc, axis=-1, keepdims=True)
    xm = vc - mu
    var = jnp.mean(xm * xm, axis=-1, keepdims=True)
    rstd = lax.rsqrt(var + EPS)
    xhat = xm * rstd
    return xhat, rstd, xhat * lw + lb


def _conf_fwd_proj(vc, z, ln_w, ln_b, pw_w, pw_b):
    def body(vc_ref, g_ref, lw_ref, lb_ref, w_ref, b_ref, y_ref, p_ref, s_s):
        lw, lb = lw_ref[...], lb_ref[...]

        def ln_chunk(ci, carry):
            r0 = pl.multiple_of(ci * R, R)
            for half in range(2):
                rr = r0 + 8 * half
                _, _, ln = _ln_chunk(vc_ref[pl.ds(rr, 8), :], lw, lb)
                p_ref[pl.ds(rr, 8), :] = ln * _sig(ln)
            s_s[pl.ds(r0, R), :] = p_ref[pl.ds(r0, R), :].astype(BF16)
            return carry
        lax.fori_loop(0, TM // R, ln_chunk, 0, unroll=2)

        p_ref[...] = jnp.dot(s_s[...], w_ref[...], preferred_element_type=F32) + b_ref[...]

        def out_chunk(ci, carry):
            r0 = pl.multiple_of(ci * R, R)
            g = g_ref[pl.ds(r0, R), :]
            y_ref[pl.ds(r0, R), :] = (p_ref[pl.ds(r0, R), :] * (g * _sig(g))).astype(BF16)
            return carry
        lax.fori_loop(0, TM // R, out_chunk, 0)

    row = pl.BlockSpec((TM, DC), lambda i: (i, 0))
    vec = pl.BlockSpec((1, DC), lambda i: (0, 0))
    return pl.pallas_call(
        body, name="conf_fwd_proj",
        grid=(TP // TM,),
        in_specs=[row, pl.BlockSpec((TM, DC), lambda i: (i, 4)), vec, vec,
                  pl.BlockSpec((DC, DC), lambda i: (0, 0)), vec],
        out_specs=[row, row],
        out_shape=[jax.ShapeDtypeStruct((TP, DC), BF16), jax.ShapeDtypeStruct((TP, DC), F32)],
        scratch_shapes=[pltpu.VMEM((TM, DC), BF16)],
        compiler_params=_cparams(),
    )(vc, z, ln_w, ln_b, pw_w, pw_b)


def _outproj_loss(ylru, yconf, w_out, h, target, post_w):
    def body(yl_ref, yc_ref, w_ref, h_ref, tgt_hbm, pw_ref, dout_ref, dy_ref, loss_ref, dpw_ref, y_s, t_ref, sem):
        i = pl.program_id(0)
        k = pl.program_id(1)

        @pl.when(k == 0)
        def _():
            _for_tile(i, lambda t: _token_tile_copy(tgt_hbm, t_ref, sem, t).start())
            y_s[...] = jnp.dot(yl_ref[...], w_ref[...], preferred_element_type=F32)

        @pl.when(k == 1)
        def _():
            y_s[...] += jnp.dot(yc_ref[...], w_ref[...], preferred_element_type=F32)

        @pl.when(jnp.logical_and(i == 0, k == 1))
        def _():
            loss_ref[...] = jnp.zeros_like(loss_ref)
            dpw_ref[...] = jnp.zeros_like(dpw_ref)

        @pl.when(k == 1)
        def _():
            _for_tile(i, lambda t: _token_tile_copy(tgt_hbm, t_ref, sem, t).wait())

            @pl.when(i == 0)
            def _():
                t_ref[0:NMETA, :] = jnp.zeros((NMETA, D), F32)

            @pl.when(i == NTILE - 1)
            def _():
                last = _tile_rows(NTILE - 1)[1]
                t_ref[last:TM, :] = jnp.zeros((TM - last, D), F32)

            pw = pw_ref[...]
            row = _row_iota((8, D))

            def chunk(ci, carry):
                r0 = pl.multiple_of(ci * 8, 8)
                yv = y_s[pl.ds(r0, 8), :]
                rs = lax.rsqrt(jnp.mean(yv * yv, axis=-1, keepdims=True) + EPS)
                grow = row + (i * TM + r0)
                valid = jnp.logical_and(grow >= NMETA, grow < T)
                yn = yv * rs
                err = jnp.where(valid, h_ref[pl.ds(r0, 8), :] + yn * pw - t_ref[pl.ds(r0, 8), :], 0.0)
                loss_ref[...] += err * err
                d_rn = err * (1.0 / D)
                dout_ref[pl.ds(r0, 8), :] = d_rn
                dpw_ref[...] += d_rn * yn
                gw = d_rn * pw
                dot = jnp.mean(gw * yv, axis=-1, keepdims=True)
                dy_ref[pl.ds(r0, 8), :] = (rs * gw - yv * (rs * rs * rs * dot)).astype(BF16)
                return carry
            lax.fori_loop(0, TM // 8, chunk, 0, unroll=4)

    row = pl.BlockSpec((TM, D), lambda i, k: (i, 0))
    half = pl.BlockSpec((TM, DL), lambda i, k: (i, 0))
    acc = pl.BlockSpec((8, D), lambda i, k: (0, 0))
    return pl.pallas_call(
        body, name="outproj_loss",
        grid=(TP // TM, 2),
        in_specs=[half, half, pl.BlockSpec((DL, D), lambda i, k: (k, 0)), row, pl.BlockSpec(memory_space=pl.ANY),
                  pl.BlockSpec((1, D), lambda i, k: (0, 0))],
        out_specs=[row, row, acc, acc],
        out_shape=[jax.ShapeDtypeStruct((TP, D), F32), jax.ShapeDtypeStruct((TP, D), BF16),
                   jax.ShapeDtypeStruct((8, D), F32), jax.ShapeDtypeStruct((8, D), F32)],
        scratch_shapes=[pltpu.VMEM((TM, D), F32), pltpu.VMEM((TM, D), F32), pltpu.SemaphoreType.DMA(())],
        compiler_params=_cparams(),
    )(ylru, yconf, w_out, h, target, post_w)


_NT = (((1,), (1,)), ((), ()))
_TN = (((0,), (0,)), ((), ()))


def _outproj_bwd(dy, ylru, yconf, w_out):
    def body(dy_ref, yl_ref, yc_ref, w_ref, dycat_ref, dw_ref):
        j = pl.program_id(0)
        dyv = dy_ref[...]
        dycat_ref[...] = lax.dot_general(dyv, w_ref[...], _NT, preferred_element_type=F32)

        @pl.when(j < NCB)
        def _():
            dw_ref[...] = lax.dot_general(yl_ref[...], dyv, _TN, preferred_element_type=F32).astype(BF16)

        @pl.when(j >= NCB)
        def _():
            dw_ref[...] = lax.dot_general(yc_ref[...], dyv, _TN, preferred_element_type=F32).astype(BF16)

    return pl.pallas_call(
        body, name="outproj_bwd",
        grid=(2 * NCB,),
        in_specs=[pl.BlockSpec((TP, D), lambda j: (0, 0)),
                  pl.BlockSpec((TP, CB), lambda j: (0, jnp.minimum(j, NCB - 1))),
                  pl.BlockSpec((TP, CB), lambda j: (0, jnp.maximum(j - NCB, 0))),
                  pl.BlockSpec((CB, D), lambda j: (j, 0))],
        out_specs=[pl.BlockSpec((TP, CB), lambda j: (0, j)), pl.BlockSpec((CB, D), lambda j: (j, 0))],
        out_shape=[jax.ShapeDtypeStruct((TP, D), F32), jax.ShapeDtypeStruct((D, D), BF16)],
        compiler_params=_cparams(),
    )(dy, ylru, yconf, w_out)


_AFTER = pl.BlockSpec(memory_space=pl.ANY)


def _conf_bwd_proj(dycat, p, z, vc, ln_w, ln_b, pw_w, after):
    def body(dy_ref, p_ref, g_ref, vc_ref, lw_ref, lb_ref, w_ref, after_ref,
             dvc_ref, dgc_ref, dpw_ref, vecs_ref, dp_s, s_s, ds_s):
        i = pl.program_id(0)
        lw, lb = lw_ref[...], lb_ref[...]

        @pl.when(i == 0)
        def _():
            dpw_ref[...] = jnp.zeros_like(dpw_ref)
            vecs_ref[...] = jnp.zeros_like(vecs_ref)

        def pre_chunk(ci, carry):
            r0 = pl.multiple_of(ci * R, R)
            for half in range(2):
                rr = r0 + 8 * half
                dyv = dy_ref[pl.ds(rr, 8), :]
                g = g_ref[pl.ds(rr, 8), :]
                sg = _sig(g)
                dp = dyv * (g * sg)
                dg = dyv * p_ref[pl.ds(rr, 8), :] * (sg * (1.0 + g * (1.0 - sg)))
                vecs_ref[0:8, :] += dp
                vecs_ref[8:16, :] += dg
                ds_s[pl.ds(rr, 8), :] = dp
                dvc_ref[pl.ds(rr, 8), :] = dg
            dp_s[pl.ds(r0, R), :] = ds_s[pl.ds(r0, R), :].astype(BF16)
            dgc_ref[pl.ds(r0, R), :] = dvc_ref[pl.ds(r0, R), :].astype(BF16)
            for half in range(2):
                rr = r0 + 8 * half
                _, _, ln = _ln_chunk(vc_ref[pl.ds(rr, 8), :], lw, lb)
                ds_s[pl.ds(rr, 8), :] = ln * _sig(ln)
            s_s[pl.ds(r0, R), :] = ds_s[pl.ds(r0, R), :].astype(BF16)
            return carry
        lax.fori_loop(0, TM // R, pre_chunk, 0, unroll=2)

        dpb = dp_s[...]
        ds_s[...] = lax.dot_general(dpb, w_ref[...], _NT, preferred_element_type=F32)
        dpw_ref[...] += lax.dot_general(s_s[...], dpb, _TN, preferred_element_type=F32)

        def post_chunk(ci, carry):
            r0 = pl.multiple_of(ci * 8, 8)
            xhat, rstd, ln = _ln_chunk(vc_ref[pl.ds(r0, 8), :], lw, lb)
            sl = _sig(ln)
            dln = ds_s[pl.ds(r0, 8), :] * (sl * (1.0 + ln * (1.0 - sl)))
            vecs_ref[16:24, :] += dln * xhat
            vecs_ref[24:32, :] += dln
            dxh = dln * lw
            m1 = jnp.mean(dxh, axis=-1, keepdims=True)
            m2 = jnp.mean(dxh * xhat, axis=-1, keepdims=True)
            dvc_ref[pl.ds(r0, 8), :] = rstd * (dxh - m1 - xhat * m2)
            return carry
        lax.fori_loop(0, TM // 8, post_chunk, 0, unroll=4)

    row = pl.BlockSpec((TM, DC), lambda i: (i, 0))
    vec = pl.BlockSpec((1, DC), lambda i: (0, 0))
    return pl.pallas_call(
        body, name="conf_bwd_proj",
        grid=(TP // TM,),
        in_specs=[pl.BlockSpec((TM, DC), lambda i: (i, 1)), row, pl.BlockSpec((TM, DC), lambda i: (i, 4)), row,
                  vec, vec, pl.BlockSpec((DC, DC), lambda i: (0, 0)), _AFTER],
        out_specs=[row, row, pl.BlockSpec((DC, DC), lambda i: (0, 0)), pl.BlockSpec((32, DC), lambda i: (0, 0))],
        out_shape=[jax.ShapeDtypeStruct((TP, DC), F32), jax.ShapeDtypeStruct((TP, DC), BF16),
                   jax.ShapeDtypeStruct((DC, DC), F32), jax.ShapeDtypeStruct((32, DC), F32)],
        scratch_shapes=[pltpu.VMEM((TM, DC), BF16), pltpu.VMEM((TM, DC), BF16), pltpu.VMEM((TM, DC), F32)],
        compiler_params=_cparams(),
    )(dycat, p, z, vc, ln_w, ln_b, pw_w, after)


def _conf_bwd_conv(dvc, z, dw_w, after):
    def body(dvc_ref, u1_ref, u2_ref, w_ref, after_ref, du_ref, dw_ref, vecs_ref, vs, dvs):
        vs[pl.ds(0, KWP), :] = jnp.zeros((KWP, CBC), F32)
        dvs[pl.ds(TP, KWP), :] = jnp.zeros((KWP, CBC), F32)
        dw_ref[...] = jnp.zeros_like(dw_ref)
        vecs_ref[...] = jnp.zeros_like(vecs_ref)

        def fill_chunk(ci, carry):
            r0 = pl.multiple_of(ci * RC, RC)
            vs[pl.ds(KWP + r0, RC), :] = u1_ref[pl.ds(r0, RC), :] * _sig(u2_ref[pl.ds(r0, RC), :])
            dv = dvc_ref[pl.ds(r0, RC), :]
            dvs[pl.ds(r0, RC), :] = dv
            vecs_ref[0:8, :] += _fold_rows(dv)
            return carry
        lax.fori_loop(0, TP // RC, fill_chunk, 0)

        def conv_chunk(ci, carry):
            r0 = pl.multiple_of(ci * RC, RC)
            vbuf = vs[pl.ds(r0, KWP + RC), :]
            dbuf = dvs[pl.ds(r0, KWP + RC), :]
            dcur = dbuf[0:RC, :]
            dv = jnp.zeros((RC, CBC), F32)
            for rr in range(8):
                vroll = vbuf if rr == 0 else pltpu.roll(vbuf, rr, 0)
                droll = dbuf if rr == 0 else pltpu.roll(dbuf, KWP + RC - rr, 0)
                for q in range(4):
                    s = 8 * q + rr
                    if s > KW - 1:
                        continue
                    k = KW - 1 - s
                    dv = dv + droll[8 * q:8 * q + RC, :] * w_ref[k:k + 1, :]
                    dw_ref[8 * k:8 * k + 8, :] += _fold_rows(dcur * vroll[KWP - 8 * q:KWP - 8 * q + RC, :])
            u1 = u1_ref[pl.ds(r0, RC), :]
            sg = _sig(u2_ref[pl.ds(r0, RC), :])
            du1 = dv * sg
            du2 = dv * u1 * (sg * (1.0 - sg))
            du_ref[0, pl.ds(r0, RC), :] = du1.astype(BF16)
            du_ref[1, pl.ds(r0, RC), :] = du2.astype(BF16)
            vecs_ref[8:16, :] += _fold_rows(du1)
            vecs_ref[16:24, :] += _fold_rows(du2)
            return carry
        lax.fori_loop(0, TP // RC, conv_chunk, 0)

    blk = pl.BlockSpec((TP, CBC), lambda j: (0, j))
    return pl.pallas_call(
        body, name="conf_bwd_conv",
        grid=(NCBC,),
        in_specs=[blk, pl.BlockSpec((TP, CBC), lambda j: (0, 2 * NCBC + j)),
                  pl.BlockSpec((TP, CBC), lambda j: (0, 3 * NCBC + j)), pl.BlockSpec((KWP, CBC), lambda j: (0, j)),
                  _AFTER],
        out_specs=[pl.BlockSpec((2, TP, CBC), lambda j: (0, 0, j)), pl.BlockSpec((8 * KWP, CBC), lambda j: (0, j)),
                   pl.BlockSpec((24, CBC), lambda j: (0, j))],
        out_shape=[jax.ShapeDtypeStruct((2, TP, DC), BF16),
                   jax.ShapeDtypeStruct((8 * KWP, DC), F32), jax.ShapeDtypeStruct((24, DC), F32)],
        scratch_shapes=[pltpu.VMEM((TP + KWP, CBC), F32), pltpu.VMEM((TP + KWP, CBC), F32)],
        compiler_params=_cparams(),
    )(dvc, z, z, dw_w, after)


def _lru_bwd(dycat, z, xc, hs, conv_w, wa_g, b_a, wx_g, b_x, lam, after):
    NV = 6

    def body(dy_ref, x_ref, g_ref, xc_ref, hs_ref, cw_ref, wa_ref, ba_ref, wx_ref, bx_ref, lam_ref, after_ref,
             dzl_ref, dwa_ref, dwx_ref, dcw_ref, vecs_ref, ga_s, gx_s, dxc_s):
        vecs_ref[...] = jnp.zeros_like(vecs_ref)
        dcw_ref[...] = jnp.zeros_like(dcw_ref)
        dxc_s[pl.ds(TP, 8), :] = jnp.zeros((8, CB), F32)

        def gate_chunk(ci, carry):
            r0 = pl.multiple_of(ci * TM, TM)
            xb = xc_ref[pl.ds(r0, TM), :].astype(BF16)
            ga_s[pl.ds(r0, TM), :] = jnp.dot(xb, wa_ref[...], preferred_element_type=F32) + ba_ref[...]
            gx_s[pl.ds(r0, TM), :] = jnp.dot(xb, wx_ref[...], preferred_element_type=F32) + bx_ref[...]
            return carry
        lax.fori_loop(0, TP // TM, gate_chunk, 0)

        sp8 = LRU_C * _softplus(-lam_ref[...])
        row = _row_iota((R, CB))
        nchunk = TP // R

        def scan_chunk(cj, carry):
            a_next, lam_next = carry
            ci = nchunk - 1 - cj
            r0 = pl.multiple_of(ci * R, R)
            dyv = dy_ref[pl.ds(r0, R), :]
            g = g_ref[pl.ds(r0, R), :]
            hv = hs_ref[pl.ds(r0, R), :]
            xc = xc_ref[pl.ds(r0, R), :]
            sg = _sig(g)
            dgl = dyv * hv * (sg * (1.0 + g * (1.0 - sg)))
            dzl_ref[1, pl.ds(r0, R), :] = dgl.astype(BF16)
            vecs_ref[0:8, :] += _fold8(dgl)
            dhs = dyv * (g * sg)
            r, i, a, mult = _gate_values(ga_s[pl.ds(r0, R), :], gx_s[pl.ds(r0, R), :], xc, sp8)
            b = jnp.where(row == R - 1, a_next, pltpu.roll(a, R - 1, 0))
            lv = dhs
            k = 1
            while k < R:
                m = row < R - k
                lv = jnp.where(m, lv + b * pltpu.roll(lv, R - k, 0), lv)
                b = jnp.where(m, b * pltpu.roll(b, R - k, 0), b)
                k *= 2
            lv = lv + b * lam_next
            p0 = pl.multiple_of(jnp.maximum(r0 - 8, 0), 8)
            hprev8 = jnp.where(ci > 0, hs_ref[pl.ds(p0, 8), :], 0.0)
            hprev = pltpu.roll(jnp.concatenate([hprev8, hv], axis=0), 1, 0)[8:8 + R, :]
            da = lv * hprev
            ixc = i * xc
            dmult = lv * ixc
            di = lv * mult * xc
            dxc_s[pl.ds(r0, R), :] = lv * mult * i
            a2 = a * a
            dlog_a = da * a - dmult * a2 / mult
            vecs_ref[32:40, :] += _fold8(dlog_a * r)
            dga = -(dlog_a * sp8) * r * (1.0 - r)
            dgx = di * i * (1.0 - i)
            ga_s[pl.ds(r0, R), :] = dga
            gx_s[pl.ds(r0, R), :] = dgx
            vecs_ref[16:24, :] += _fold8(dga)
            vecs_ref[24:32, :] += _fold8(dgx)
            a_first = jnp.sum(jnp.where(row == 0, a, 0.0), axis=0, keepdims=True)
            l_first = jnp.sum(jnp.where(row == 0, lv, 0.0), axis=0, keepdims=True)
            return a_first, l_first
        lax.fori_loop(0, nchunk // 2, lambda i, cr: scan_chunk(2 * i + 1, scan_chunk(2 * i, cr)),
                      (jnp.zeros((1, CB), F32), jnp.zeros((1, CB), F32)))

        dwa_ref[...] = jnp.zeros_like(dwa_ref)
        dwx_ref[...] = jnp.zeros_like(dwx_ref)

        def mm_chunk(ci, carry):
            r0 = pl.multiple_of(ci * TM, TM)
            xb = xc_ref[pl.ds(r0, TM), :].astype(BF16)
            dgab = ga_s[pl.ds(r0, TM), :].astype(BF16)
            dgxb = gx_s[pl.ds(r0, TM), :].astype(BF16)
            dxc_s[pl.ds(r0, TM), :] += (lax.dot_general(dgab, wa_ref[...], _NT, preferred_element_type=F32)
                                        + lax.dot_general(dgxb, wx_ref[...], _NT, preferred_element_type=F32))
            dwa_ref[...] += lax.dot_general(xb, dgab, _TN, preferred_element_type=F32)
            dwx_ref[...] += lax.dot_general(xb, dgxb, _TN, preferred_element_type=F32)
            return carry
        lax.fori_loop(0, TP // TM, mm_chunk, 0)

        taps = [cw_ref[k:k + 1, :] for k in range(LW)]

        def conv_chunk(ci, carry):
            r0 = pl.multiple_of(ci * R, R)
            dbuf = dxc_s[pl.ds(r0, R + 8), :]
            dcur = dbuf[0:R, :]
            p0 = pl.multiple_of(jnp.maximum(r0 - 8, 0), 8)
            xprev = jnp.where(ci > 0, x_ref[pl.ds(p0, 8), :], 0.0)
            xbuf = jnp.concatenate([xprev, x_ref[pl.ds(r0, R), :]], axis=0)
            dxl = dcur * taps[LW - 1]
            dcw_ref[8 * (LW - 1):8 * LW, :] += _fold8(dcur * xbuf[8:8 + R, :])
            for s in range(1, LW):
                k = LW - 1 - s
                dxl = dxl + pltpu.roll(dbuf, R + 8 - s, 0)[0:R, :] * taps[k]
                dcw_ref[8 * k:8 * k + 8, :] += _fold8(dcur * pltpu.roll(xbuf, s, 0)[8:8 + R, :])
            dzl_ref[0, pl.ds(r0, R), :] = dxl.astype(BF16)
            vecs_ref[8:16, :] += _fold8(dxl)
            vecs_ref[40:48, :] += _fold8(dcur)
            return carry
        lax.fori_loop(0, TP // R, conv_chunk, 0)
        vecs_ref[32:40, :] = vecs_ref[32:40, :] * (LRU_C * _sig(-lam_ref[...]))

    col = lambda off: pl.BlockSpec((TP, CB), lambda j: (0, off + j))
    vec = pl.BlockSpec((1, CB), lambda j: (0, j))
    wsp = pl.BlockSpec((None, CB, CB), lambda j: (j, 0, 0))
    return pl.pallas_call(
        body, name="lru_bwd",
        grid=(NCB,),
        in_specs=[col(0), col(0), col(NCB), col(0), col(0), pl.BlockSpec((LW, CB), lambda j: (0, j)),
                  wsp, vec, wsp, vec, vec, _AFTER],
        out_specs=[pl.BlockSpec((2, TP, CB), lambda j: (0, 0, j)), wsp, wsp,
                   pl.BlockSpec((8 * LW, CB), lambda j: (0, j)), pl.BlockSpec((8 * NV, CB), lambda j: (0, j))],
        out_shape=[jax.ShapeDtypeStruct((2, TP, DL), BF16),
                   jax.ShapeDtypeStruct((NCB, CB, CB), F32), jax.ShapeDtypeStruct((NCB, CB, CB), F32),
                   jax.ShapeDtypeStruct((8 * LW, DL), F32), jax.ShapeDtypeStruct((8 * NV, DL), F32)],
        scratch_shapes=[pltpu.VMEM((TP, CB), F32), pltpu.VMEM((TP, CB), F32), pltpu.VMEM((TP + 8, CB), F32)],
        compiler_params=_cparams(),
    )(dycat, z, z, xc, hs, conv_w, wa_g, b_a, wx_g, b_x, lam, after)


def _dz_section(sec, dzl_ref, dzc_ref, dgc_ref, use):
    @pl.when(sec < 2)
    def _():
        use(dzl_ref)

    @pl.when(jnp.logical_and(sec >= 2, sec < 4))
    def _():
        use(dzc_ref)

    @pl.when(sec == 4)
    def _():
        use(dgc_ref)


def _dz_specs(rows, index):
    return [pl.BlockSpec((None, rows, 1024), lambda a, b: (jnp.minimum(index(a, b)[1], 1), index(a, b)[0], 0)),
            pl.BlockSpec((None, rows, 1024), lambda a, b: (jnp.clip(index(a, b)[1] - 2, 0, 1), index(a, b)[0], 0)),
            pl.BlockSpec((rows, 1024), lambda a, b: (index(a, b)[0], 0))]


def _inproj_wgrad(name, hn, dzs, after):
    KB = 512
    nsec = dzs.shape[0]

    def body(hn_ref, dz_ref, after_ref, dw_ref):
        dw_ref[...] = lax.dot_general(hn_ref[...], dz_ref[...], _TN, preferred_element_type=F32).astype(BF16)

    return pl.pallas_call(
        body, name=name,
        grid=(nsec, D // KB),
        in_specs=[pl.BlockSpec((TP, KB), lambda n, kb: (0, kb)),
                  pl.BlockSpec((None, TP, 1024), lambda n, kb: (n, 0, 0)), _AFTER],
        out_specs=pl.BlockSpec((KB, 1024), lambda n, kb: (kb, n)),
        out_shape=jax.ShapeDtypeStruct((D, nsec * 1024), BF16),
        compiler_params=_cparams(),
    )(hn, dzs, after)


def _sum_win_parts(parts_a, parts_b, parts_c):
    RB = 64

    def body(a_ref, b_ref, c_ref, o_ref):
        def chunk(ci, carry):
            r0 = pl.multiple_of(ci * R, R)
            for ref, base, ncol in ((a_ref, 0, 2048), (b_ref, 2048, 2048), (c_ref, 4096, 1024)):
                for c0 in range(0, ncol, 512):
                    acc = ref[0, pl.ds(r0, R), c0:c0 + 512].astype(F32)
                    for sidx in range(1, NDEV):
                        acc = acc + ref[sidx, pl.ds(r0, R), c0:c0 + 512].astype(F32)
                    o_ref[pl.ds(r0, R), base + c0:base + c0 + 512] = acc.astype(BF16)
            return carry
        lax.fori_loop(0, RB // R, chunk, 0)

    spec = lambda ncol: pl.BlockSpec((NDEV, RB, ncol), lambda i: (0, i, 0))
    return pl.pallas_call(
        body, name="sum_win_parts",
        grid=(D // NDEV // RB,),
        in_specs=[spec(2048), spec(2048), spec(1024)],
        out_specs=pl.BlockSpec((RB, NIN), lambda i: (i, 0)),
        out_shape=jax.ShapeDtypeStruct((D // NDEV, NIN), BF16),
        compiler_params=_cparams(),
    )(parts_a, parts_b, parts_c)


def _inproj_bwd(dzl, dzc, dgc, w_in, h, dout, pre_w, after):
    nsec = NIN // 1024

    def body(dzl_ref, dzc_ref, dgc_ref, w_ref, h_ref, dout_ref, pw_ref, after_ref, gx_hbm, dmeta_ref, dpw_ref,
             acc_s, dh_s, sem):
        i = pl.program_id(0)
        s = pl.program_id(1)

        def gx_copy(t):
            lo, n, off = _tile_rows(t)
            return pltpu.make_async_copy(dh_s.at[pl.ds(off, n)], gx_hbm.at[pl.ds(lo, n)], sem)

        @pl.when(s == 0)
        def _():
            acc_s[...] = jnp.zeros_like(acc_s)

        def use(dz_ref):
            acc_s[...] += lax.dot_general(dz_ref[...], w_ref[...], _NT, preferred_element_type=F32)
        _dz_section(s, dzl_ref, dzc_ref, dgc_ref, use)

        @pl.when(jnp.logical_and(i == 0, s == nsec - 1))
        def _():
            dpw_ref[...] = jnp.zeros_like(dpw_ref)

        @pl.when(s == nsec - 1)
        def _():
            _for_tile(i - 1, lambda t: gx_copy(t).wait())
            pw = pw_ref[...]

            def chunk(ci, carry):
                r0 = pl.multiple_of(ci * 8, 8)
                hv = h_ref[pl.ds(r0, 8), :]
                dhn = acc_s[pl.ds(r0, 8), :]
                rs = lax.rsqrt(jnp.mean(hv * hv, axis=-1, keepdims=True) + EPS)
                dpw_ref[...] += dhn * (hv * rs)
                gw = dhn * pw
                dot = jnp.mean(gw * hv, axis=-1, keepdims=True)
                dh_s[pl.ds(r0, 8), :] = rs * gw - hv * (rs * rs * rs * dot) + dout_ref[pl.ds(r0, 8), :]
                return carry
            lax.fori_loop(0, TM // 8, chunk, 0, unroll=4)
            _for_tile(i, lambda t: gx_copy(t).start())

            @pl.when(i == 0)
            def _():
                dmeta_ref[...] = dh_s[0:NMETA, :]

            @pl.when(i == NTILE - 1)
            def _():
                gx_copy(NTILE - 1).wait()

    row = pl.BlockSpec((TM, D), lambda i, s: (i, 0))
    return pl.pallas_call(
        body, name="inproj_bwd",
        grid=(TP // TM, nsec),
        in_specs=_dz_specs(TM, lambda i, s: (i, s)) + [
            pl.BlockSpec((D, 1024), lambda i, s: (0, s)), row, row, pl.BlockSpec((1, D), lambda i, s: (0, 0)),
            _AFTER],
        out_specs=[pl.BlockSpec(memory_space=pl.ANY), pl.BlockSpec((NMETA, D), lambda i, s: (0, 0)),
                   pl.BlockSpec((8, D), lambda i, s: (0, 0))],
        out_shape=[jax.ShapeDtypeStruct((SEQ, D), F32), jax.ShapeDtypeStruct((NMETA, D), F32),
                   jax.ShapeDtypeStruct((8, D), F32)],
        scratch_shapes=[pltpu.VMEM((TM, D), F32), pltpu.VMEM((TM, D), F32), pltpu.SemaphoreType.DMA(())],
        compiler_params=_cparams(),
    )(dzl, dzc, dgc, w_in, h, dout, pre_w, after)


def _adamw(name, parts, w, m, v, block_rows):
    rows, cols = w.shape
    nparts = parts.shape[0]
    cw = cols if cols <= 640 else 512

    def body(p_ref, w_ref, m_ref, v_ref, g_ref, d_ref, nm_ref, nv_ref):
        def chunk(ci, carry):
            r0 = pl.multiple_of(ci * R, R)
            for c0 in range(0, cols, cw):
                at = (pl.ds(r0, R), slice(c0, c0 + cw))
                g = p_ref[(0,) + at].astype(F32)
                for sidx in range(1, nparts):
                    g = g + p_ref[(sidx,) + at].astype(F32)
                delta, mv, vv = _adam_math(g, w_ref[at], m_ref[at], v_ref[at])
                g_ref[at] = g
                nm_ref[at] = mv
                nv_ref[at] = vv
                d_ref[at] = delta
            return carry
        lax.fori_loop(0, block_rows // R, chunk, 0)

    blk = pl.BlockSpec((block_rows, cols), lambda i: (i, 0))
    shp = jax.ShapeDtypeStruct((rows, cols), F32)
    return pl.pallas_call(
        body, name=name,
        grid=(rows // block_rows,),
        in_specs=[pl.BlockSpec((nparts, block_rows, cols), lambda i: (0, i, 0)), blk, blk, blk],
        out_specs=[blk, blk, blk, blk],
        out_shape=[shp, shp, shp, shp],
        compiler_params=_cparams(),
    )(parts, w, m, v)


def _adam_math(g, w, m, v):
    c1 = 1.0 / (1.0 - ADAM_B1 ** ADAM_STEP)
    c2 = 1.0 / (1.0 - ADAM_B2 ** ADAM_STEP)
    mv = ADAM_B1 * m + (1.0 - ADAM_B1) * g
    vv = ADAM_B2 * v + (1.0 - ADAM_B2) * (g * g)
    upd = (mv * c1) / (jnp.sqrt(vv * c2) + ADAM_EPS) + ADAM_WD * w
    return -ADAM_LR * upd, mv, vv


_VEC = [("pre_norm_w", 2), ("post_norm_w", 2), ("b_in", 5), ("lru_conv_b", 1), ("b_gate_a", 1), ("b_gate_x", 1),
        ("lru_lambda", 1), ("conf_dw_b", 1), ("conf_ln_w", 1), ("conf_ln_b", 1), ("conf_pw_b", 1)]
_VEC_ROWS = 24
_LOSS_ROW = 17
_SM_ROWS = 64


def _pack_grads(dprew_acc, dpostw_acc, cvecs, kvecs, lvecs, dcw_acc, ddw_acc, dh, loss_acc):
    def body(pre_ref, post_ref, c_ref, k_ref, l_ref, dcw_ref, ddw_ref, dh_ref, loss_ref, vec_ref, small_ref, tmp):
        s8 = lambda ref, r: jnp.sum(ref[8 * r:8 * r + 8, :], axis=0, keepdims=True)
        vec_ref[...] = jnp.zeros_like(vec_ref)
        pre, post = s8(pre_ref, 0), s8(post_ref, 0)
        rows = [pre[:, 0:1024], pre[:, 1024:2048], post[:, 0:1024], post[:, 1024:2048],
                s8(l_ref, 1), s8(l_ref, 0), s8(k_ref, 1), s8(k_ref, 2), s8(c_ref, 1),
                s8(l_ref, 5), s8(l_ref, 2), s8(l_ref, 3), s8(l_ref, 4),
                s8(k_ref, 0), s8(c_ref, 2), s8(c_ref, 3), s8(c_ref, 0)]
        for r, val in enumerate(rows):
            vec_ref[r:r + 1, :] = val
        vec_ref[_LOSS_ROW:_LOSS_ROW + 1, :] = jnp.zeros((1, 1024), F32) + (0.5 / D) * jnp.sum(loss_ref[...])

        small_ref[...] = jnp.zeros_like(small_ref)
        for k in range(LW):
            tmp[k:k + 1, :] = s8(dcw_ref, k)
        for k in range(KW):
            tmp[8 + k:9 + k, :] = s8(ddw_ref, k)
        for d in range(NDEV):
            small_ref[d, 0:LW, 0:128] = tmp[0:LW, 128 * d:128 * d + 128]
            small_ref[d, 8:8 + KW, 0:128] = tmp[8:8 + KW, 128 * d:128 * d + 128]
            small_ref[d, 40:56, :] = dh_ref[:, 256 * d:256 * d + 256]

    full = lambda a: pl.BlockSpec(a.shape, lambda i: (0,) * a.ndim)
    ins = [dprew_acc, dpostw_acc, cvecs, kvecs, lvecs, dcw_acc, ddw_acc]
    return pl.pallas_call(
        body, name="pack_grads",
        grid=(1,),
        in_specs=[full(a) for a in ins] + [full(dh), full(loss_acc)],
        out_specs=[pl.BlockSpec((_VEC_ROWS, 1024), lambda i: (0, 0)),
                   pl.BlockSpec((NDEV, _SM_ROWS, 256), lambda i: (0, 0, 0))],
        out_shape=[jax.ShapeDtypeStruct((_VEC_ROWS, 1024), F32), jax.ShapeDtypeStruct((NDEV, _SM_ROWS, 256), F32)],
        scratch_shapes=[pltpu.VMEM((40, 1024), F32)],
        compiler_params=_cparams(),
    )(*ins, dh, loss_acc)


def _adamw_vec(parts, W, M, V):
    nv = len(_VEC)

    def body(*refs):
        p_ref = refs[0]
        w_refs, m_refs, v_refs = refs[1:1 + nv], refs[1 + nv:1 + 2 * nv], refs[1 + 2 * nv:1 + 3 * nv]
        outs = refs[1 + 3 * nv:]

        def total(r):
            acc = p_ref[0, r:r + 1, :]
            for sidx in range(1, NDEV):
                acc = acc + p_ref[sidx, r:r + 1, :]
            return acc

        row = 0
        for idx, (_, nrows) in enumerate(_VEC):
            for part in range(nrows):
                cols = slice(1024 * part, 1024 * part + 1024)
                g = total(row + part)
                delta, mv, vv = _adam_math(g, w_refs[idx][:, cols], m_refs[idx][:, cols], v_refs[idx][:, cols])
                for o, val in zip(outs[4 * idx:4 * idx + 4], (g, delta, mv, vv)):
                    o[:, cols] = val
            row += nrows
        outs[-1][...] = total(_LOSS_ROW)[:, 0:128]

    names = [n for n, _ in _VEC]
    flat = lambda d: [d[n].reshape(1, -1) for n in names]
    ws, ms, vs = flat(W), flat(M), flat(V)
    res = pl.pallas_call(
        body, name="adamw_vec",
        out_shape=[jax.ShapeDtypeStruct(w.shape, F32) for w in ws for _ in range(4)]
        + [jax.ShapeDtypeStruct((1, 128), F32)],
        compiler_params=_cparams(),
    )(parts, *ws, *ms, *vs)
    return {n: tuple(res[4 * i:4 * i + 4]) for i, n in enumerate(names)}, res[-1]


def _adamw_small(parts, W, M, V):
    where = {"lru_conv_w": (slice(0, LW), slice(0, 128)), "conf_dw_w": (slice(8, 8 + KW), slice(0, 128)),
             "meta_tokens": (slice(40, 56), slice(0, 256))}
    names = list(where)

    def body(*refs):
        p_ref = refs[0]
        outs = refs[10:]
        for idx, n in enumerate(names):
            rs, cs = where[n]
            g = p_ref[0, rs, cs]
            for sidx in range(1, NDEV):
                g = g + p_ref[sidx, rs, cs]
            delta, mv, vv = _adam_math(g, refs[1 + idx][...], refs[4 + idx][...], refs[7 + idx][...])
            for o, val in zip(outs[4 * idx:4 * idx + 4], (g, delta, mv, vv)):
                o[...] = val

    two_d = lambda a: a.reshape(a.shape[-2:])
    ws, ms, vs = ([two_d(d[n]) for n in names] for d in (W, M, V))
    res = pl.pallas_call(
        body, name="adamw_small",
        out_shape=[jax.ShapeDtypeStruct(w.shape, F32) for w in ws for _ in range(4)],
        compiler_params=_cparams(),
    )(parts, *ws, *ms, *vs)
    return {n: tuple(res[4 * i:4 * i + 4]) for i, n in enumerate(names)}


def _pack_small(lru_cw, dw_w, meta):
    buf = jnp.zeros((_SM_ROWS, 256), F32)
    buf = buf.at[0:LW, 0:128].set(lru_cw)
    buf = buf.at[8:8 + dw_w.shape[0], 0:128].set(dw_w)
    return buf.at[40:56, :].set(meta)


def _block_diag4(w):
    w4 = w.reshape(NCB, 4, 64, 64)
    eye = jnp.eye(4, dtype=w.dtype)
    return jnp.einsum("ghij,hk->ghikj", w4, eye).reshape(NCB, CB, CB)


def _diag_blocks(g):
    g5 = g.reshape(NCB, 4, 64, 4, 64)
    return jnp.stack([g5[:, hh, :, hh, :] for hh in range(4)], axis=1).reshape(16, 64, 64)


def _gate_mats(W):
    return _block_diag4(W["w_gate_a"][0]).astype(BF16), _block_diag4(W["w_gate_x"][0]).astype(BF16)


def _local_step(x, target, meta_full, inproj, out_weights, lru_cw_full, dw_w_full, W, gate_mats, send):
    wa_g, wx_g = gate_mats

    h, hn = _prenorm(x, meta_full, W["pre_norm_w"])
    z, win_full = inproj(hn)
    ylru, xc, hs = _lru_fwd(z, lru_cw_full, W["lru_conv_b"], wa_g, W["b_gate_a"], wx_g, W["b_gate_x"],
                            W["lru_lambda"])
    vc = _conf_fwd_conv(z, dw_w_full, W["conf_dw_b"])
    wout_full, pw_full = out_weights(vc)
    yconf, p = _conf_fwd_proj(vc, z, W["conf_ln_w"], W["conf_ln_b"], pw_full, W["conf_pw_b"])
    dout, dy, loss_acc, dpostw_acc = _outproj_loss(ylru, yconf, wout_full, h, target, W["post_norm_w"])

    dycat, dwout_part = _outproj_bwd(dy, ylru, yconf, wout_full)
    tok = send("w_out", ("w_out", dwout_part))
    dvc, dgc, dpw_part, cvecs = _conf_bwd_proj(dycat, p, z, vc, W["conf_ln_w"], W["conf_ln_b"], pw_full, tok)
    tok = send("w_in_c", ("conf_pw_w", dpw_part), ("w_in_c", _inproj_wgrad("inproj_wgrad_c", hn, dgc[None], dgc)))
    dzc, ddw_acc, kvecs = _conf_bwd_conv(dvc, z, dw_w_full, tok)
    tok = send("w_in_b", ("w_in_b", _inproj_wgrad("inproj_wgrad_b", hn, dzc, dzc)))
    dzl, dwa_g, dwx_g, dcw_acc, lvecs = _lru_bwd(dycat, z, xc, hs, lru_cw_full, wa_g, W["b_gate_a"], wx_g,
                                                 W["b_gate_x"], W["lru_lambda"], tok)
    tok = send("w_gates", ("w_gate_a", _diag_blocks(dwa_g).reshape(16 * 64, 64)),
               ("w_gate_x", _diag_blocks(dwx_g).reshape(16 * 64, 64)))
    tok = send("w_in_a", ("w_in_a", _inproj_wgrad("inproj_wgrad_a", hn, dzl, tok)))
    grad_x, dmeta, dprew_acc = _inproj_bwd(dzl, dzc, dgc, win_full, h, dout, W["pre_norm_w"], tok)

    vec_pack, small_part = _pack_grads(dprew_acc, dpostw_acc, cvecs, kvecs, lvecs, dcw_acc, ddw_acc, dmeta, loss_acc)
    return grad_x, vec_pack, small_part


def kernel(x, meta_tokens, pre_norm_w, post_norm_w, w_in, b_in, lru_conv_w, lru_conv_b, w_gate_a, b_gate_a, w_gate_x, b_gate_x, lru_lambda, conf_dw_w, conf_dw_b, conf_ln_w, conf_ln_b, conf_pw_w, conf_pw_b, w_out, loss_target, m_meta_tokens, m_pre_norm_w, m_post_norm_w, m_w_in, m_b_in, m_lru_conv_w, m_lru_conv_b, m_w_gate_a, m_b_gate_a, m_w_gate_x, m_b_gate_x, m_lru_lambda, m_conf_dw_w, m_conf_dw_b, m_conf_ln_w, m_conf_ln_b, m_conf_pw_w, m_conf_pw_b, m_w_out, v_meta_tokens, v_pre_norm_w, v_post_norm_w, v_w_in, v_b_in, v_lru_conv_w, v_lru_conv_b, v_w_gate_a, v_b_gate_a, v_w_gate_x, v_b_gate_x, v_lru_lambda, v_conf_dw_w, v_conf_dw_b, v_conf_ln_w, v_conf_ln_b, v_conf_pw_w, v_conf_pw_b, v_w_out):
    W = dict(meta_tokens=meta_tokens, pre_norm_w=pre_norm_w, post_norm_w=post_norm_w, w_in=w_in, b_in=b_in,
             lru_conv_w=lru_conv_w, lru_conv_b=lru_conv_b, w_gate_a=w_gate_a, b_gate_a=b_gate_a,
             w_gate_x=w_gate_x, b_gate_x=b_gate_x, lru_lambda=lru_lambda, conf_dw_w=conf_dw_w,
             conf_dw_b=conf_dw_b, conf_ln_w=conf_ln_w, conf_ln_b=conf_ln_b, conf_pw_w=conf_pw_w,
             conf_pw_b=conf_pw_b, w_out=w_out)
    M = dict(meta_tokens=m_meta_tokens, pre_norm_w=m_pre_norm_w, post_norm_w=m_post_norm_w, w_in=m_w_in,
             b_in=m_b_in, lru_conv_w=m_lru_conv_w, lru_conv_b=m_lru_conv_b, w_gate_a=m_w_gate_a,
             b_gate_a=m_b_gate_a, w_gate_x=m_w_gate_x, b_gate_x=m_b_gate_x, lru_lambda=m_lru_lambda,
             conf_dw_w=m_conf_dw_w, conf_dw_b=m_conf_dw_b, conf_ln_w=m_conf_ln_w, conf_ln_b=m_conf_ln_b,
             conf_pw_w=m_conf_pw_w, conf_pw_b=m_conf_pw_b, w_out=m_w_out)
    V = dict(meta_tokens=v_meta_tokens, pre_norm_w=v_pre_norm_w, post_norm_w=v_post_norm_w, w_in=v_w_in,
             b_in=v_b_in, lru_conv_w=v_lru_conv_w, lru_conv_b=v_lru_conv_b, w_gate_a=v_w_gate_a,
             b_gate_a=v_b_gate_a, w_gate_x=v_w_gate_x, b_gate_x=v_b_gate_x, lru_lambda=v_lru_lambda,
             conf_dw_w=v_conf_dw_w, conf_dw_b=v_conf_dw_b, conf_ln_w=v_conf_ln_w, conf_ln_b=v_conf_ln_b,
             conf_pw_w=v_conf_pw_w, conf_pw_b=v_conf_pw_b, w_out=v_w_out)
    names = list(W.keys())
    shapes = {n: W[n].shape for n in names}

    small = _pack_small(lru_conv_w[0], conf_dw_w[0], meta_tokens)
    (small_flight,), tok = _exchange_start("gather_small_start", [
        (small, jax.ShapeDtypeStruct((NDEV, _SM_ROWS, 256), F32), _whole, _slot)])
    win_flight, tok = _win_gather_start(w_in[0].astype(BF16) + tok[0, 0].astype(BF16))
    gate_mats = _gate_mats(W)
    wout_shard = w_out[0].astype(BF16) + tok[0, 0].astype(BF16)
    pw_shard = conf_pw_w[0].astype(BF16)
    cast_done = (gate_mats[0][0, 0:8, 0:128] + gate_mats[1][0, 0:8, 0:128]
                 + wout_shard[0:8, 0:128] + pw_shard[0:8, 0:128])
    win_flight, tok = _win_gather_links(win_flight, cast_done)
    gathered, tok = _exchange_start("gather_out_start", [
        (wout_shard + tok[0, 0].astype(BF16), jax.ShapeDtypeStruct((D, D), BF16), _whole, _rows(D // NDEV)),
        (pw_shard, jax.ShapeDtypeStruct((DC, DC), BF16), _whole, _rows(DC // NDEV)),
    ])
    (small_all,) = _exchange_wait("gather_small_wait", [small_flight], tok)
    unshard = lambda a: jnp.transpose(a, (1, 0, 2)).reshape(a.shape[1], -1)
    lru_cw_full = unshard(small_all[:, 0:LW, 0:128])
    dw_w_full = unshard(small_all[:, 8:8 + KWP, 0:128])
    meta_full = unshard(small_all[:, 40:56, :])

    def out_weights(after):
        return _exchange_wait("gather_out_wait", gathered, after)

    def inproj(hn):
        xi, yi, ci = lax.axis_index("x"), lax.axis_index("y"), lax.axis_index("c")
        shard = lambda px, py, pc: (4 * px + 2 * py + pc).astype(jnp.int32)
        over_links = jnp.stack([shard(1 - xi, yi, ci), shard(xi, 1 - yi, ci), shard(1 - xi, 1 - yi, ci)])
        z, src = _inproj_cols("inproj_own", jnp.stack([shard(xi, yi, ci)]), hn, win_flight["src"], b_in, None)
        flight = _win_gather_early(dict(win_flight, src=src))
        z, land = _inproj_cols("inproj_here", jnp.stack([shard(xi, yi, 1 - ci)]), hn, flight["land"], b_in, z)
        flight = _win_gather_forward("all", dict(flight, land=land), (1, 2, 3), z)
        z, land = _inproj_cols("inproj_links", over_links, hn, flight["land"], b_in, z)
        flight = _win_gather_forwarded("all", dict(flight, land=land), (1, 2, 3))
        z, land = _inproj_cols("inproj_sibling", over_links + 1 - 2 * ci, hn, flight["land"], b_in, z)
        return z, _win_gather_wait(dict(flight, land=land))

    row_stage = lambda ncol: (jax.ShapeDtypeStruct((NDEV, D // NDEV, ncol), BF16), _rows(D // NDEV))
    piece = {"w_in_a": row_stage(2048), "w_in_b": row_stage(2048), "w_in_c": row_stage(1024),
             "w_out": row_stage(D),
             "conf_pw_w": (jax.ShapeDtypeStruct((NDEV, DC // NDEV, DC), BF16), _rows(DC // NDEV)),
             "w_gate_a": (jax.ShapeDtypeStruct((NDEV, 16 * 64, 64), BF16), _whole),
             "w_gate_x": (jax.ShapeDtypeStruct((NDEV, 16 * 64, 64), BF16), _whole)}
    sent = {}

    def send(call, *named_parts):
        handles, token = _exchange_start(
            "scatter_" + call + "_start",
            [(part.astype(BF16), piece[name][0], piece[name][1], _slot) for name, part in named_parts])
        for (name, _), handle in zip(named_parts, handles):
            sent[name] = [handle]
        return token

    grad_x, vec_pack, small_part = _local_step(
        x[0], loss_target[0], meta_full, inproj, out_weights, lru_cw_full, dw_w_full, W, gate_mats, send)
    grad_x = grad_x[None]

    rest, tok = _exchange_start("scatter_rest_start", [
        (small_part, jax.ShapeDtypeStruct((NDEV, _SM_ROWS, 256), F32), _slot, _slot),
        (vec_pack, jax.ShapeDtypeStruct((NDEV, _VEC_ROWS, 1024), F32), _whole, _slot),
    ])
    (parts_c,) = _exchange_wait("scatter_w_in_c_wait", sent["w_in_c"], tok)
    (parts_b,) = _exchange_wait("scatter_w_in_b_wait", sent["w_in_b"], parts_c)
    (parts_a,) = _exchange_wait("scatter_w_in_a_wait", sent["w_in_a"], parts_b)
    win_rows = _sum_win_parts(parts_a, parts_b, parts_c)
    win_stage2, tok = _exchange_start("scatter_w_in_stage2_start", [
        (win_rows, jax.ShapeDtypeStruct((NDEV, D // NDEV, NIN // NDEV), BF16), _cols(NIN // NDEV), _slot)])

    G, DW, NM, NV = {}, {}, {}, {}
    (wout_parts,) = _exchange_wait("scatter_w_out_wait", sent["w_out"], tok)
    G["w_out"], DW["w_out"], NM["w_out"], NV["w_out"] = _adamw("adamw_w_out", wout_parts, w_out[0], m_w_out[0], v_w_out[0], 64)
    (pw_parts,) = _exchange_wait("scatter_conf_pw_w_wait", sent["conf_pw_w"], G["w_out"])
    G["conf_pw_w"], DW["conf_pw_w"], NM["conf_pw_w"], NV["conf_pw_w"] = _adamw(
        "adamw_pw", pw_parts, conf_pw_w[0], m_conf_pw_w[0], v_conf_pw_w[0], 128)
    res = {}
    wa_parts, wx_parts = _exchange_wait("scatter_w_gates_wait", sent["w_gate_a"] + sent["w_gate_x"], G["conf_pw_w"])
    for n, parts in (("w_gate_a", wa_parts), ("w_gate_x", wx_parts)):
        res[n] = _adamw("adamw_" + n, parts, *[d[n].reshape(16 * 64, 64) for d in (W, M, V)], 16 * 64)
    small_parts, vec_parts = _exchange_wait("scatter_rest_wait", rest, res["w_gate_x"][0])
    res.update(_adamw_small(small_parts, W, M, V))
    vec_res, loss_row = _adamw_vec(vec_parts, W, M, V)
    res.update(vec_res)
    (win_sum,) = _exchange_wait("scatter_w_in_stage2_wait", win_stage2, loss_row)
    res["w_in"] = _adamw("adamw_w_in", win_sum.reshape(1, D, NIN // NDEV), w_in[0], m_w_in[0], v_w_in[0], 256)
    for n, vals in res.items():
        for dst, val in zip((G, DW, NM, NV), vals):
            dst[n] = val
    for dst in (G, DW, NM, NV):
        for n in names:
            dst[n] = dst[n].reshape(shapes[n])
    loss = loss_row[0, 0]

    return (loss, grad_x, *[G[n] for n in names], *[DW[n] for n in names],
            *[NM[n] for n in names], *[NV[n] for n in names])
```

```python
import functools

import jax
import jax.numpy as jnp
from jax import lax
from jax.experimental import pallas as pl
from jax.experimental.pallas import tpu as pltpu

F32 = jnp.float32
BF16 = jnp.bfloat16

D = 2048
DL = 1024
DC = 1024
NIN = 5120
NMETA = 16
SEQ = 2048
T = NMETA + SEQ
TP = 2176
TM = 544
CB = 256
NCB = DL // CB
R = 16
KW = 31
KWP = 32
LW = 4
LRU_C = 8.0
EPS = 1e-6
NDEV = 8

ADAM_LR = 0.001
ADAM_B1 = 0.9
ADAM_B2 = 0.999
ADAM_EPS = 1e-08
ADAM_WD = 0.01
ADAM_STEP = 10

VMEM_LIMIT = 56 * 1024 * 1024


def _cparams():
    return pltpu.CompilerParams(vmem_limit_bytes=VMEM_LIMIT)


def _sig(x):
    return 1.0 / (1.0 + jnp.exp(-x))


def _expm1_neg(y):
    poly = y * (1.0 + y * (0.5 + y * (1.0 / 6.0 + y * (1.0 / 24.0 + y * (1.0 / 120.0)))))
    return jnp.where(y > -0.1, poly, jnp.exp(y) - 1.0)


def _softplus(x):
    e = jnp.exp(-jnp.abs(x))
    w = 1.0 + e
    l1p = jnp.where(w == 1.0, e, jnp.log(w) * e / (w - 1.0))
    return jnp.maximum(x, 0.0) + l1p


def _row_iota(shape):
    return lax.broadcasted_iota(jnp.int32, shape, 0)


def _fold8(v):
    return v[0:8, :] + v[8:16, :]


_FLIPS = [(k >> 2 & 1, k >> 1 & 1, k & 1) for k in range(1, NDEV)]
_HBM = pl.BlockSpec(memory_space=pltpu.HBM)
_SEM = pl.BlockSpec(memory_space=pltpu.SEMAPHORE)


def _peers():
    x, y, c = lax.axis_index("x"), lax.axis_index("y"), lax.axis_index("c")
    out = []
    for dx, dy, dc in _FLIPS:
        px = 1 - x if dx else x
        py = 1 - y if dy else y
        pc = 1 - c if dc else c
        out.append(((px, py, pc), 4 * px + 2 * py + pc))
    return 4 * x + 2 * y + c, out


def _exchange_start(name, items):
    n = len(items)

    def body(*refs):
        srcs, lands = refs[:n], refs[n:2 * n]
        outs = refs[2 * n:]
        send_sems, recv_sems, local_sems = outs[:n], outs[n:2 * n], outs[2 * n:3 * n]
        token = outs[-1]
        me, peers = _peers()
        for a in range(n):
            src_at, dst_at = items[a][2], items[a][3]
            pltpu.make_async_copy(src_at(srcs[a], me), dst_at(lands[a], me), local_sems[a]).start()
        for a in range(n):
            src_at, dst_at = items[a][2], items[a][3]
            for k, (pos, peer) in enumerate(peers):
                pltpu.make_async_remote_copy(
                    src_ref=src_at(srcs[a], peer), dst_ref=dst_at(lands[a], me),
                    send_sem=send_sems[a].at[k], recv_sem=recv_sems[a].at[k],
                    device_id=pos, device_id_type=pl.DeviceIdType.MESH).start()
        token[...] = jnp.zeros_like(token)

    srcs = [pltpu.with_memory_space_constraint(it[0], pltpu.HBM) for it in items]
    lands = [pltpu.with_memory_space_constraint(lax.empty(it[1].shape, it[1].dtype), pltpu.HBM) for it in items]
    sem7 = pltpu.SemaphoreType.DMA((NDEV - 1,))
    res = pl.pallas_call(
        body, name=name,
        out_shape=([sem7] * (2 * n) + [pltpu.SemaphoreType.DMA(())] * n
                   + [pltpu.HBM(a.shape, a.dtype) for a in srcs] + [pltpu.HBM(a.shape, a.dtype) for a in lands]
                   + [jax.ShapeDtypeStruct((8, 128), F32)]),
        in_specs=[_HBM] * (2 * n),
        out_specs=[_SEM] * (3 * n) + [_HBM] * (2 * n) + [pl.BlockSpec(memory_space=pltpu.VMEM)],
        input_output_aliases={i: 3 * n + i for i in range(2 * n)},
        compiler_params=pltpu.CompilerParams(has_side_effects=pltpu.SideEffectType.DATAFLOW_SIDE_EFFECTING),
    )(*srcs, *lands)
    handles = [dict(send=res[a], recv=res[n + a], local=res[2 * n + a], src=res[3 * n + a], land=res[4 * n + a],
                    src_at=items[a][2], dst_at=items[a][3]) for a in range(n)]
    return handles, res[-1]


def _wait_bytes(piece, sem):
    pltpu.make_async_copy(piece, piece, sem).wait()


def _exchange_wait(name, handles, after):
    n = len(handles)

    def body(*refs):
        srcs, lands = refs[:n], refs[n:2 * n]
        send_sems, recv_sems, local_sems = refs[2 * n:3 * n], refs[3 * n:4 * n], refs[4 * n:5 * n]
        me, peers = _peers()
        for a in range(n):
            src_at, dst_at = handles[a]["src_at"], handles[a]["dst_at"]
            for k, (pos, peer) in enumerate(peers):
                _wait_bytes(src_at(srcs[a], peer), send_sems[a].at[k])
                _wait_bytes(dst_at(lands[a], peer), recv_sems[a].at[k])
            pltpu.make_async_copy(src_at(srcs[a], me), dst_at(lands[a], me), local_sems[a]).wait()

    srcs = [hd["src"] for hd in handles]
    lands = [hd["land"] for hd in handles]
    res = pl.pallas_call(
        body, name=name,
        out_shape=[pltpu.HBM(a.shape, a.dtype) for a in srcs] + [pltpu.HBM(a.shape, a.dtype) for a in lands],
        in_specs=[_HBM] * (2 * n) + [_SEM] * (3 * n) + [pl.BlockSpec(memory_space=pl.ANY)],
        out_specs=[_HBM] * (2 * n),
        input_output_aliases={i: i for i in range(2 * n)},
        compiler_params=pltpu.CompilerParams(has_side_effects=pltpu.SideEffectType.DATAFLOW_SIDE_EFFECTING),
    )(*srcs, *lands, *[hd["send"] for hd in handles], *[hd["recv"] for hd in handles],
      *[hd["local"] for hd in handles], after)
    return list(res[n:])


_SIDE = pltpu.SideEffectType.DATAFLOW_SIDE_EFFECTING
_WCOLS = NIN // NDEV


def _win_cols(ref, l):
    return ref.at[:, pl.ds(pl.multiple_of(l * _WCOLS, 128), _WCOLS)]


def _win_routes():
    x, y, c = lax.axis_index("x"), lax.axis_index("y"), lax.axis_index("c")
    pos = [(x, y, 1 - c), (1 - x, y, c), (x, 1 - y, c), (1 - x, 1 - y, c)]
    return 4 * x + 2 * y + c, [(p, 4 * p[0] + 2 * p[1] + p[2]) for p in pos]


def _win_gather_start(shard):
    def body(src, land, send_sem, recv_sem, local_sem, src_thru, land_thru, token):
        me, routes = _win_routes()
        pltpu.make_async_copy(src, _win_cols(land, me), local_sem).start()
        pltpu.make_async_remote_copy(src_ref=src, dst_ref=_win_cols(land, me), send_sem=send_sem, recv_sem=recv_sem,
                                     device_id=routes[0][0], device_id_type=pl.DeviceIdType.MESH).start()
        token[...] = jnp.zeros_like(token)

    src = pltpu.with_memory_space_constraint(shard, pltpu.HBM)
    land = pltpu.with_memory_space_constraint(lax.empty((D, NIN), BF16), pltpu.HBM)
    sem = pltpu.SemaphoreType.DMA(())
    res = pl.pallas_call(
        body, name="win_gather_start",
        out_shape=[sem, sem, sem, pltpu.HBM(src.shape, BF16), pltpu.HBM(land.shape, BF16),
                   jax.ShapeDtypeStruct((8, 128), F32)],
        in_specs=[_HBM, _HBM],
        out_specs=[_SEM, _SEM, _SEM, _HBM, _HBM, pl.BlockSpec(memory_space=pltpu.VMEM)],
        input_output_aliases={0: 3, 1: 4},
        compiler_params=pltpu.CompilerParams(has_side_effects=_SIDE),
    )(src, land)
    return dict(send0=res[0], recv0=res[1], local=res[2], src=res[3], land=res[4]), res[5]


def _win_gather_links(hd, after):
    def body(src, land, after_ref, send_sems, recv_sems, src_thru, land_thru, token):
        me, routes = _win_routes()
        for k in (1, 2, 3):
            pltpu.make_async_remote_copy(src_ref=src, dst_ref=_win_cols(land, me), send_sem=send_sems.at[k - 1],
                                         recv_sem=recv_sems.at[k - 1], device_id=routes[k][0],
                                         device_id_type=pl.DeviceIdType.MESH).start()
        token[...] = jnp.zeros_like(token)

    sem3 = pltpu.SemaphoreType.DMA((3,))
    res = pl.pallas_call(
        body, name="win_gather_links",
        out_shape=[sem3, sem3, pltpu.HBM(hd["src"].shape, BF16), pltpu.HBM(hd["land"].shape, BF16),
                   jax.ShapeDtypeStruct((8, 128), F32)],
        in_specs=[_HBM, _HBM, pl.BlockSpec(memory_space=pl.ANY)],
        out_specs=[_SEM, _SEM, _HBM, _HBM, pl.BlockSpec(memory_space=pltpu.VMEM)],
        input_output_aliases={0: 2, 1: 3},
        compiler_params=pltpu.CompilerParams(has_side_effects=_SIDE),
    )(hd["src"], hd["land"], after)
    return dict(hd, send=res[0], recv=res[1], src=res[2], land=res[3]), res[4]


def _win_gather_forward(name, hd, ks, after):
    def body(land, recv_sems, after_ref, land_thru, fsend_sems, frecv_sems):
        me, routes = _win_routes()
        sibling = routes[0][0]
        for n, k in enumerate(ks):
            pos, peer = routes[k]
            piece = _win_cols(land, peer)
            pltpu.make_async_remote_copy(src_ref=piece, dst_ref=piece, send_sem=fsend_sems.at[n],
                                         recv_sem=recv_sems.at[k - 1], device_id=pos,
                                         device_id_type=pl.DeviceIdType.MESH).wait_recv()
            pltpu.make_async_remote_copy(src_ref=piece, dst_ref=piece, send_sem=fsend_sems.at[n],
                                         recv_sem=frecv_sems.at[n], device_id=sibling,
                                         device_id_type=pl.DeviceIdType.MESH).start()

    sems = pltpu.SemaphoreType.DMA((len(ks),))
    res = pl.pallas_call(
        body, name="win_gather_forward_" + name,
        out_shape=[pltpu.HBM(hd["land"].shape, BF16), sems, sems],
        in_specs=[_HBM, _SEM, pl.BlockSpec(memory_space=pl.ANY)],
        out_specs=[_HBM, _SEM, _SEM],
        input_output_aliases={0: 0},
        compiler_params=pltpu.CompilerParams(has_side_effects=_SIDE),
    )(hd["land"], hd["recv"], after)
    return dict(hd, land=res[0], **{"fsend" + name: res[1], "frecv" + name: res[2]})


def _win_gather_forwarded(name, hd, ks):
    def body(land, fsend_sems, frecv_sems, land_thru):
        me, routes = _win_routes()
        sib_c = routes[0][0][2]
        for n, k in enumerate(ks):
            _wait_bytes(_win_cols(land, routes[k][1]), fsend_sems.at[n])
            _wait_bytes(_win_cols(land, 4 * routes[k][0][0] + 2 * routes[k][0][1] + sib_c), frecv_sems.at[n])

    res = pl.pallas_call(
        body, name="win_gather_forwarded_" + name,
        out_shape=[pltpu.HBM(hd["land"].shape, BF16)],
        in_specs=[_HBM, _SEM, _SEM],
        out_specs=[_HBM],
        input_output_aliases={0: 0},
        compiler_params=pltpu.CompilerParams(has_side_effects=_SIDE),
    )(hd["land"], hd["fsend" + name], hd["frecv" + name])
    return dict(hd, land=res[0])


def _win_gather_early(hd):
    def body(src, land, recv_sem, local_sem, src_thru, land_thru):
        me, routes = _win_routes()
        _wait_bytes(_win_cols(land, routes[0][1]), recv_sem)
        pltpu.make_async_copy(src, _win_cols(land, me), local_sem).wait()

    res = pl.pallas_call(
        body, name="win_gather_early",
        out_shape=[pltpu.HBM(hd["src"].shape, BF16), pltpu.HBM(hd["land"].shape, BF16)],
        in_specs=[_HBM, _HBM, _SEM, _SEM],
        out_specs=[_HBM, _HBM],
        input_output_aliases={0: 0, 1: 1},
        compiler_params=pltpu.CompilerParams(has_side_effects=_SIDE),
    )(hd["src"], hd["land"], hd["recv0"], hd["local"])
    return dict(hd, src=res[0], land=res[1])


def _win_gather_wait(hd):
    def body(src, land, send0_sem, send_sems, src_thru, land_thru):
        for k in range(4):
            _wait_bytes(src, send0_sem if k == 0 else send_sems.at[k - 1])

    res = pl.pallas_call(
        body, name="win_gather_wait",
        out_shape=[pltpu.HBM(hd["src"].shape, BF16), pltpu.HBM(hd["land"].shape, BF16)],
        in_specs=[_HBM, _HBM, _SEM, _SEM],
        out_specs=[_HBM, _HBM],
        input_output_aliases={0: 0, 1: 1},
        compiler_params=pltpu.CompilerParams(has_side_effects=_SIDE),
    )(hd["src"], hd["land"], hd["send0"], hd["send"])
    return res[1]


def _whole(ref, l):
    return ref


def _slot(ref, l):
    return ref.at[l]


def _cols(width):
    def at(ref, l):
        return ref.at[:, pl.ds(pl.multiple_of(l * width, 128), width)]
    return at


def _rows(height):
    def at(ref, l):
        return ref.at[pl.ds(pl.multiple_of(l * height, 8), height), :]
    return at


NTILE = TP // TM


def _tile_rows(t):
    lo = max(t * TM - NMETA, 0)
    hi = min((t + 1) * TM - NMETA, SEQ)
    return lo, hi - lo, lo + NMETA - t * TM


def _for_tile(t, fn):
    for static_t in range(NTILE):
        pl.when(t == static_t)(functools.partial(fn, static_t))


def _token_tile_copy(hbm_ref, buf, sem, t):
    lo, n, off = _tile_rows(t)
    return pltpu.make_async_copy(hbm_ref.at[pl.ds(lo, n)], buf.at[pl.ds(off, n)], sem)


def _prenorm(x, meta_full, pre_w):
    def body(x_ref, meta_ref, pw_ref, h_ref, hn_ref, xbuf, sems):
        i = pl.program_id(0)
        slot = i % 2

        def start(t):
            _token_tile_copy(x_ref, xbuf.at[t % 2], sems.at[t % 2], t).start()

        @pl.when(i == 0)
        def _():
            start(0)
        _for_tile(i + 1, start)
        _for_tile(i, lambda t: _token_tile_copy(x_ref, xbuf.at[t % 2], sems.at[t % 2], t).wait())

        @pl.when(i == 0)
        def _():
            xbuf[0, 0:NMETA, :] = meta_ref[...]

        @pl.when(i == NTILE - 1)
        def _():
            last = _tile_rows(NTILE - 1)[1]
            xbuf[(NTILE - 1) % 2, last:TM, :] = jnp.zeros((TM - last, D), F32)

        pw = pw_ref[...]

        def chunk(ci, carry):
            r0 = pl.multiple_of(ci * R, R)
            xv = xbuf[slot, pl.ds(r0, R), :]
            h_ref[pl.ds(r0, R), :] = xv
            ms = jnp.mean(xv * xv, axis=-1, keepdims=True)
            hn_ref[pl.ds(r0, R), :] = (xv * lax.rsqrt(ms + EPS) * pw).astype(BF16)
            return carry
        lax.fori_loop(0, TM // R, chunk, 0, unroll=2)

    row = pl.BlockSpec((TM, D), lambda i: (i, 0))
    return pl.pallas_call(
        body, name="prenorm",
        grid=(NTILE,),
        in_specs=[pl.BlockSpec(memory_space=pl.ANY), pl.BlockSpec((NMETA, D), lambda i: (0, 0)),
                  pl.BlockSpec((1, D), lambda i: (0, 0))],
        out_specs=[row, row],
        out_shape=[jax.ShapeDtypeStruct((TP, D), F32), jax.ShapeDtypeStruct((TP, D), BF16)],
        scratch_shapes=[pltpu.VMEM((2, TM, D), F32), pltpu.SemaphoreType.DMA((2,))],
        compiler_params=_cparams(),
    )(x, meta_full, pre_w)


def _inproj_cols(name, shards, hn, w_land, b_in, z_prev):
    nsh = shards.shape[0]
    one_shard = w_land.shape[1] == _WCOLS

    def body(idx_ref, hn_ref, w_ref, b_ref, *rest):
        z_ref = rest[-2]
        z_ref[...] = jnp.dot(hn_ref[...], w_ref[...], preferred_element_type=F32) + b_ref[...]

    any_spec = pl.BlockSpec(memory_space=pl.ANY)
    in_specs = [pl.BlockSpec((TM, D), lambda j, i, idx: (i, 0)),
                pl.BlockSpec((D, _WCOLS), lambda j, i, idx: (0, 0 if one_shard else idx[j])),
                pl.BlockSpec((1, _WCOLS), lambda j, i, idx: (0, idx[j]))]
    operands = [hn, w_land, b_in]
    aliases = {2: 1}
    if z_prev is not None:
        in_specs.append(any_spec)
        operands.append(z_prev)
        aliases[4] = 0
    return pl.pallas_call(
        body, name=name,
        grid_spec=pltpu.PrefetchScalarGridSpec(
            num_scalar_prefetch=1, grid=(nsh, TP // TM), in_specs=in_specs,
            out_specs=[pl.BlockSpec((TM, _WCOLS), lambda j, i, idx: (i, idx[j])), any_spec]),
        out_shape=[jax.ShapeDtypeStruct((TP, NIN), F32), jax.ShapeDtypeStruct(w_land.shape, w_land.dtype)],
        input_output_aliases=aliases,
        compiler_params=_cparams(),
    )(shards, *operands)


def _gate_values(ga, gx, xc, sp8):
    r = _sig(ga)
    i = _sig(gx)
    log_a = -(r * sp8)
    a = jnp.exp(log_a)
    mult = jnp.sqrt(-_expm1_neg(2.0 * log_a))
    return r, i, a, mult


def _lru_fwd(z, conv_w, conv_b, wa_g, b_a, wx_g, b_x, lam):
    def body(x_ref, g_ref, cw_ref, cb_ref, wa_ref, ba_ref, wx_ref, bx_ref, lam_ref,
             y_ref, xc_ref, hs_ref, ga_s, gx_s):
        taps = [cw_ref[k:k + 1, :] for k in range(LW)]
        cb = cb_ref[...]

        def conv_chunk(ci, carry):
            r0 = pl.multiple_of(ci * R, R)
            cur = x_ref[pl.ds(r0, R), :]
            p0 = pl.multiple_of(jnp.maximum(r0 - 8, 0), 8)
            prev = jnp.where(ci > 0, x_ref[pl.ds(p0, 8), :], 0.0)
            buf = jnp.concatenate([prev, cur], axis=0)
            acc = cur * taps[LW - 1] + cb
            for s in range(1, LW):
                acc = acc + pltpu.roll(buf, s, 0)[8:8 + R, :] * taps[LW - 1 - s]
            xc_ref[pl.ds(r0, R), :] = acc
            return carry
        lax.fori_loop(0, TP // R, conv_chunk, 0)

        def gate_chunk(ci, carry):
            r0 = pl.multiple_of(ci * TM, TM)
            xb = xc_ref[pl.ds(r0, TM), :].astype(BF16)
            ga_s[pl.ds(r0, TM), :] = jnp.dot(xb, wa_ref[...], preferred_element_type=F32) + ba_ref[...]
            gx_s[pl.ds(r0, TM), :] = jnp.dot(xb, wx_ref[...], preferred_element_type=F32) + bx_ref[...]
            return carry
        lax.fori_loop(0, TP // TM, gate_chunk, 0)

        sp8 = LRU_C * _softplus(-lam_ref[...])
        row = _row_iota((R, CB))

        def scan_chunk(ci, hprev):
            r0 = pl.multiple_of(ci * R, R)
            xc = xc_ref[pl.ds(r0, R), :]
            _, i, a, mult = _gate_values(ga_s[pl.ds(r0, R), :], gx_s[pl.ds(r0, R), :], xc, sp8)
            u = mult * (i * xc)
            k = 1
            while k < R:
                m = row >= k
                u = jnp.where(m, a * pltpu.roll(u, k, 0) + u, u)
                a = jnp.where(m, a * pltpu.roll(a, k, 0), a)
                k *= 2
            hv = u + a * hprev
            hs_ref[pl.ds(r0, R), :] = hv
            g = g_ref[pl.ds(r0, R), :]
            y_ref[pl.ds(r0, R), :] = (hv * (g * _sig(g))).astype(BF16)
            return jnp.sum(jnp.where(row == R - 1, hv, 0.0), axis=0, keepdims=True)
        def scan_pass(i, hp):
            for sub in range(4):
                hp = scan_chunk(4 * i + sub, hp)
            return hp
        lax.fori_loop(0, TP // R // 4, scan_pass, jnp.zeros((1, CB), F32))

    col = lambda off: pl.BlockSpec((TP, CB), lambda j: (0, off + j))
    vec = pl.BlockSpec((1, CB), lambda j: (0, j))
    wsp = pl.BlockSpec((None, CB, CB), lambda j: (j, 0, 0))
    return pl.pallas_call(
        body, name="lru_fwd",
        grid=(NCB,),
        in_specs=[col(0), col(NCB), pl.BlockSpec((LW, CB), lambda j: (0, j)), vec, wsp, vec, wsp, vec, vec],
        out_specs=[col(0), col(0), col(0)],
        out_shape=[jax.ShapeDtypeStruct((TP, DL), BF16), jax.ShapeDtypeStruct((TP, DL), F32),
                   jax.ShapeDtypeStruct((TP, DL), F32)],
        scratch_shapes=[pltpu.VMEM((TP, CB), F32), pltpu.VMEM((TP, CB), F32)],
        compiler_params=_cparams(),
    )(z, z, conv_w, conv_b, wa_g, b_a, wx_g, b_x, lam)


CBC = 128
NCBC = DC // CBC
RC = 64


def _fold_rows(v):
    acc = v[0:8, :]
    for r in range(8, v.shape[0], 8):
        acc = acc + v[r:r + 8, :]
    return acc


def _conf_fwd_conv(z, dw_w, dw_b):
    def body(u1_ref, u2_ref, w_ref, b_ref, vc_ref, vs):
        vs[pl.ds(0, KWP), :] = jnp.zeros((KWP, CBC), F32)

        def glu_chunk(ci, carry):
            r0 = pl.multiple_of(ci * RC, RC)
            vs[pl.ds(KWP + r0, RC), :] = u1_ref[pl.ds(r0, RC), :] * _sig(u2_ref[pl.ds(r0, RC), :])
            return carry
        lax.fori_loop(0, TP // RC, glu_chunk, 0)

        bias = b_ref[...]

        def conv_chunk(ci, carry):
            r0 = pl.multiple_of(ci * RC, RC)
            buf = vs[pl.ds(r0, KWP + RC), :]
            acc = jnp.zeros((RC, CBC), F32) + bias
            for rr in range(8):
                rolled = buf if rr == 0 else pltpu.roll(buf, rr, 0)
                for q in range(4):
                    s = 8 * q + rr
                    if s > KW - 1:
                        continue
                    k = KW - 1 - s
                    acc = acc + rolled[KWP - 8 * q:KWP - 8 * q + RC, :] * w_ref[k:k + 1, :]
            vc_ref[pl.ds(r0, RC), :] = acc
            return carry
        lax.fori_loop(0, TP // RC, conv_chunk, 0)

    return pl.pallas_call(
        body, name="conf_fwd_conv",
        grid=(NCBC,),
        in_specs=[pl.BlockSpec((TP, CBC), lambda j: (0, 2 * NCBC + j)),
                  pl.BlockSpec((TP, CBC), lambda j: (0, 3 * NCBC + j)),
                  pl.BlockSpec((KWP, CBC), lambda j: (0, j)),
                  pl.BlockSpec((1, CBC), lambda j: (0, j))],
        out_specs=pl.BlockSpec((TP, CBC), lambda j: (0, j)),
        out_shape=jax.ShapeDtypeStruct((TP, DC), F32),
        scratch_shapes=[pltpu.VMEM((TP + KWP, CBC), F32)],
        compiler_params=_cparams(),
    )(z, z, dw_w, dw_b)


def _ln_chunk(vc, lw, lb):
    mu = jnp.mean(vc, axis=-1, keepdims=True)
    xm = vc - mu
    var = jnp.mean(xm * xm, axis=-1, keepdims=True)
    rstd = lax.rsqrt(var + EPS)
    xhat = xm * rstd
    return xhat, rstd, xhat * lw + lb


def _conf_fwd_proj(vc, z, ln_w, ln_b, pw_w, pw_b):
    def body(vc_ref, g_ref, lw_ref, lb_ref, w_ref, b_ref, y_ref, p_ref, s_s):
        lw, lb = lw_ref[...], lb_ref[...]

        def ln_chunk(ci, carry):
            r0 = pl.multiple_of(ci * R, R)
            for half in range(2):
                rr = r0 + 8 * half
                _, _, ln = _ln_chunk(vc_ref[pl.ds(rr, 8), :], lw, lb)
                p_ref[pl.ds(rr, 8), :] = ln * _sig(ln)
            s_s[pl.ds(r0, R), :] = p_ref[pl.ds(r0, R), :].astype(BF16)
            return carry
        lax.fori_loop(0, TM // R, ln_chunk, 0, unroll=2)

        p_ref[...] = jnp.dot(s_s[...], w_ref[...], preferred_element_type=F32) + b_ref[...]

        def out_chunk(ci, carry):
            r0 = pl.multiple_of(ci * R, R)
            g = g_ref[pl.ds(r0, R), :]
            y_ref[pl.ds(r0, R), :] = (p_ref[pl.ds(r0, R), :] * (g * _sig(g))).astype(BF16)
            return carry
        lax.fori_loop(0, TM // R, out_chunk, 0)

    row = pl.BlockSpec((TM, DC), lambda i: (i, 0))
    vec = pl.BlockSpec((1, DC), lambda i: (0, 0))
    return pl.pallas_call(
        body, name="conf_fwd_proj",
        grid=(TP // TM,),
        in_specs=[row, pl.BlockSpec((TM, DC), lambda i: (i, 4)), vec, vec,
                  pl.BlockSpec((DC, DC), lambda i: (0, 0)), vec],
        out_specs=[row, row],
        out_shape=[jax.ShapeDtypeStruct((TP, DC), BF16), jax.ShapeDtypeStruct((TP, DC), F32)],
        scratch_shapes=[pltpu.VMEM((TM, DC), BF16)],
        compiler_params=_cparams(),
    )(vc, z, ln_w, ln_b, pw_w, pw_b)


def _outproj_loss(ylru, yconf, w_out, h, target, post_w):
    def body(yl_ref, yc_ref, w_ref, h_ref, tgt_hbm, pw_ref, dout_ref, dy_ref, loss_ref, dpw_ref, y_s, t_ref, sem):
        i = pl.program_id(0)
        k = pl.program_id(1)

        @pl.when(k == 0)
        def _():
            _for_tile(i, lambda t: _token_tile_copy(tgt_hbm, t_ref, sem, t).start())
            y_s[...] = jnp.dot(yl_ref[...], w_ref[...], preferred_element_type=F32)

        @pl.when(k == 1)
        def _():
            y_s[...] += jnp.dot(yc_ref[...], w_ref[...], preferred_element_type=F32)

        @pl.when(jnp.logical_and(i == 0, k == 1))
        def _():
            loss_ref[...] = jnp.zeros_like(loss_ref)
            dpw_ref[...] = jnp.zeros_like(dpw_ref)

        @pl.when(k == 1)
        def _():
            _for_tile(i, lambda t: _token_tile_copy(tgt_hbm, t_ref, sem, t).wait())

            @pl.when(i == 0)
            def _():
                t_ref[0:NMETA, :] = jnp.zeros((NMETA, D), F32)

            @pl.when(i == NTILE - 1)
            def _():
                last = _tile_rows(NTILE - 1)[1]
                t_ref[last:TM, :] = jnp.zeros((TM - last, D), F32)

            pw = pw_ref[...]
            row = _row_iota((8, D))

            def chunk(ci, carry):
                r0 = pl.multiple_of(ci * 8, 8)
                yv = y_s[pl.ds(r0, 8), :]
                rs = lax.rsqrt(jnp.mean(yv * yv, axis=-1, keepdims=True) + EPS)
                grow = row + (i * TM + r0)
                valid = jnp.logical_and(grow >= NMETA, grow < T)
                yn = yv * rs
                err = jnp.where(valid, h_ref[pl.ds(r0, 8), :] + yn * pw - t_ref[pl.ds(r0, 8), :], 0.0)
                loss_ref[...] += err * err
                d_rn = err * (1.0 / D)
                dout_ref[pl.ds(r0, 8), :] = d_rn
                dpw_ref[...] += d_rn * yn
                gw = d_rn * pw
                dot = jnp.mean(gw * yv, axis=-1, keepdims=True)
                dy_ref[pl.ds(r0, 8), :] = (rs * gw - yv * (rs * rs * rs * dot)).astype(BF16)
                return carry
            lax.fori_loop(0, TM // 8, chunk, 0, unroll=4)

    row = pl.BlockSpec((TM, D), lambda i, k: (i, 0))
    half = pl.BlockSpec((TM, DL), lambda i, k: (i, 0))
    acc = pl.BlockSpec((8, D), lambda i, k: (0, 0))
    return pl.pallas_call(
        body, name="outproj_loss",
        grid=(TP // TM, 2),
        in_specs=[half, half, pl.BlockSpec((DL, D), lambda i, k: (k, 0)), row, pl.BlockSpec(memory_space=pl.ANY),
                  pl.BlockSpec((1, D), lambda i, k: (0, 0))],
        out_specs=[row, row, acc, acc],
        out_shape=[jax.ShapeDtypeStruct((TP, D), F32), jax.ShapeDtypeStruct((TP, D), BF16),
                   jax.ShapeDtypeStruct((8, D), F32), jax.ShapeDtypeStruct((8, D), F32)],
        scratch_shapes=[pltpu.VMEM((TM, D), F32), pltpu.VMEM((TM, D), F32), pltpu.SemaphoreType.DMA(())],
        compiler_params=_cparams(),
    )(ylru, yconf, w_out, h, target, post_w)


_NT = (((1,), (1,)), ((), ()))
_TN = (((0,), (0,)), ((), ()))


def _outproj_bwd(dy, ylru, yconf, w_out):
    def body(dy_ref, yl_ref, yc_ref, w_ref, dycat_ref, dw_ref):
        j = pl.program_id(0)
        dyv = dy_ref[...]
        dycat_ref[...] = lax.dot_general(dyv, w_ref[...], _NT, preferred_element_type=F32)

        @pl.when(j < NCB)
        def _():
            dw_ref[...] = lax.dot_general(yl_ref[...], dyv, _TN, preferred_element_type=F32).astype(BF16)

        @pl.when(j >= NCB)
        def _():
            dw_ref[...] = lax.dot_general(yc_ref[...], dyv, _TN, preferred_element_type=F32).astype(BF16)

    return pl.pallas_call(
        body, name="outproj_bwd",
        grid=(2 * NCB,),
        in_specs=[pl.BlockSpec((TP, D), lambda j: (0, 0)),
                  pl.BlockSpec((TP, CB), lambda j: (0, jnp.minimum(j, NCB - 1))),
                  pl.BlockSpec((TP, CB), lambda j: (0, jnp.maximum(j - NCB, 0))),
                  pl.BlockSpec((CB, D), lambda j: (j, 0))],
        out_specs=[pl.BlockSpec((TP, CB), lambda j: (0, j)), pl.BlockSpec((CB, D), lambda j: (j, 0))],
        out_shape=[jax.ShapeDtypeStruct((TP, D), F32), jax.ShapeDtypeStruct((D, D), BF16)],
        compiler_params=_cparams(),
    )(dy, ylru, yconf, w_out)


_AFTER = pl.BlockSpec(memory_space=pl.ANY)


def _conf_bwd_proj(dycat, p, z, vc, hs, ln_w, ln_b, pw_w, after):
    def body(dy_ref, p_ref, g_ref, vc_ref, dyl_ref, hs_ref, gl_ref, lw_ref, lb_ref, w_ref, after_ref,
             dvc_ref, dz_ref, dpw_ref, vecs_ref, dp_s, s_s, ds_s):
        i = pl.program_id(0)
        lw, lb = lw_ref[...], lb_ref[...]

        @pl.when(i == 0)
        def _():
            dpw_ref[...] = jnp.zeros_like(dpw_ref)
            vecs_ref[...] = jnp.zeros_like(vecs_ref)

        def pre_chunk(ci, carry):
            r0 = pl.multiple_of(ci * R, R)
            for half in range(2):
                rr = r0 + 8 * half
                dyv = dy_ref[pl.ds(rr, 8), :]
                g = g_ref[pl.ds(rr, 8), :]
                sg = _sig(g)
                dp = dyv * (g * sg)
                dg = dyv * p_ref[pl.ds(rr, 8), :] * (sg * (1.0 + g * (1.0 - sg)))
                vecs_ref[0:8, :] += dp
                vecs_ref[8:16, :] += dg
                ds_s[pl.ds(rr, 8), :] = dp
                dvc_ref[pl.ds(rr, 8), :] = dg
            dp_s[pl.ds(r0, R), :] = ds_s[pl.ds(r0, R), :].astype(BF16)
            dz_ref[0, pl.ds(r0, R), :] = dvc_ref[pl.ds(r0, R), :].astype(BF16)
            for half in range(2):
                rr = r0 + 8 * half
                gl = gl_ref[pl.ds(rr, 8), :]
                sgl = _sig(gl)
                dgl = dyl_ref[pl.ds(rr, 8), :] * hs_ref[pl.ds(rr, 8), :] * (sgl * (1.0 + gl * (1.0 - sgl)))
                vecs_ref[32:40, :] += dgl
                dvc_ref[pl.ds(rr, 8), :] = dgl
            dz_ref[1, pl.ds(r0, R), :] = dvc_ref[pl.ds(r0, R), :].astype(BF16)
            for half in range(2):
                rr = r0 + 8 * half
                _, _, ln = _ln_chunk(vc_ref[pl.ds(rr, 8), :], lw, lb)
                ds_s[pl.ds(rr, 8), :] = ln * _sig(ln)
            s_s[pl.ds(r0, R), :] = ds_s[pl.ds(r0, R), :].astype(BF16)
            return carry
        lax.fori_loop(0, TM // R, pre_chunk, 0, unroll=2)

        dpb = dp_s[...]
        ds_s[...] = lax.dot_general(dpb, w_ref[...], _NT, preferred_element_type=F32)
        dpw_ref[...] += lax.dot_general(s_s[...], dpb, _TN, preferred_element_type=F32)

        def post_chunk(ci, carry):
            r0 = pl.multiple_of(ci * 8, 8)
            xhat, rstd, ln = _ln_chunk(vc_ref[pl.ds(r0, 8), :], lw, lb)
            sl = _sig(ln)
            dln = ds_s[pl.ds(r0, 8), :] * (sl * (1.0 + ln * (1.0 - sl)))
            vecs_ref[16:24, :] += dln * xhat
            vecs_ref[24:32, :] += dln
            dxh = dln * lw
            m1 = jnp.mean(dxh, axis=-1, keepdims=True)
            m2 = jnp.mean(dxh * xhat, axis=-1, keepdims=True)
            dvc_ref[pl.ds(r0, 8), :] = rstd * (dxh - m1 - xhat * m2)
            return carry
        lax.fori_loop(0, TM // 8, post_chunk, 0, unroll=4)

    row = pl.BlockSpec((TM, DC), lambda i: (i, 0))
    vec = pl.BlockSpec((1, DC), lambda i: (0, 0))
    return pl.pallas_call(
        body, name="conf_bwd_proj",
        grid=(TP // TM,),
        in_specs=[pl.BlockSpec((TM, DC), lambda i: (i, 1)), row, pl.BlockSpec((TM, DC), lambda i: (i, 4)), row,
                  pl.BlockSpec((TM, DL), lambda i: (i, 0)), row, pl.BlockSpec((TM, DL), lambda i: (i, 1)),
                  vec, vec, pl.BlockSpec((DC, DC), lambda i: (0, 0)), _AFTER],
        out_specs=[row, pl.BlockSpec((2, TM, DC), lambda i: (0, i, 0)), pl.BlockSpec((DC, DC), lambda i: (0, 0)),
                   pl.BlockSpec((40, DC), lambda i: (0, 0))],
        out_shape=[jax.ShapeDtypeStruct((TP, DC), F32), jax.ShapeDtypeStruct((2, TP, DC), BF16),
                   jax.ShapeDtypeStruct((DC, DC), F32), jax.ShapeDtypeStruct((40, DC), F32)],
        scratch_shapes=[pltpu.VMEM((TM, DC), BF16), pltpu.VMEM((TM, DC), BF16), pltpu.VMEM((TM, DC), F32)],
        compiler_params=_cparams(),
    )(dycat, p, z, vc, dycat, hs, z, ln_w, ln_b, pw_w, after)


def _conf_bwd_conv(dvc, z, dw_w, after):
    def body(dvc_ref, u1_ref, u2_ref, w_ref, after_ref, du_ref, dw_ref, vecs_ref, vs, dvs):
        vs[pl.ds(0, KWP), :] = jnp.zeros((KWP, CBC), F32)
        dvs[pl.ds(TP, KWP), :] = jnp.zeros((KWP, CBC), F32)
        dw_ref[...] = jnp.zeros_like(dw_ref)
        vecs_ref[...] = jnp.zeros_like(vecs_ref)

        def fill_chunk(ci, carry):
            r0 = pl.multiple_of(ci * RC, RC)
            vs[pl.ds(KWP + r0, RC), :] = u1_ref[pl.ds(r0, RC), :] * _sig(u2_ref[pl.ds(r0, RC), :])
            dv = dvc_ref[pl.ds(r0, RC), :]
            dvs[pl.ds(r0, RC), :] = dv
            vecs_ref[0:8, :] += _fold_rows(dv)
            return carry
        lax.fori_loop(0, TP // RC, fill_chunk, 0)

        def conv_chunk(ci, carry):
            r0 = pl.multiple_of(ci * RC, RC)
            vbuf = vs[pl.ds(r0, KWP + RC), :]
            dbuf = dvs[pl.ds(r0, KWP + RC), :]
            dcur = dbuf[0:RC, :]
            dv = jnp.zeros((RC, CBC), F32)
            for rr in range(8):
                vroll = vbuf if rr == 0 else pltpu.roll(vbuf, rr, 0)
                droll = dbuf if rr == 0 else pltpu.roll(dbuf, KWP + RC - rr, 0)
                for q in range(4):
                    s = 8 * q + rr
                    if s > KW - 1:
                        continue
                    k = KW - 1 - s
                    dv = dv + droll[8 * q:8 * q + RC, :] * w_ref[k:k + 1, :]
                    dw_ref[8 * k:8 * k + 8, :] += _fold_rows(dcur * vroll[KWP - 8 * q:KWP - 8 * q + RC, :])
            u1 = u1_ref[pl.ds(r0, RC), :]
            sg = _sig(u2_ref[pl.ds(r0, RC), :])
            du1 = dv * sg
            du2 = dv * u1 * (sg * (1.0 - sg))
            du_ref[0, pl.ds(r0, RC), :] = du1.astype(BF16)
            du_ref[1, pl.ds(r0, RC), :] = du2.astype(BF16)
            vecs_ref[8:16, :] += _fold_rows(du1)
            vecs_ref[16:24, :] += _fold_rows(du2)
            return carry
        lax.fori_loop(0, TP // RC, conv_chunk, 0)

    blk = pl.BlockSpec((TP, CBC), lambda j: (0, j))
    return pl.pallas_call(
        body, name="conf_bwd_conv",
        grid=(NCBC,),
        in_specs=[blk, pl.BlockSpec((TP, CBC), lambda j: (0, 2 * NCBC + j)),
                  pl.BlockSpec((TP, CBC), lambda j: (0, 3 * NCBC + j)), pl.BlockSpec((KWP, CBC), lambda j: (0, j)),
                  _AFTER],
        out_specs=[pl.BlockSpec((2, TP, CBC), lambda j: (0, 0, j)), pl.BlockSpec((8 * KWP, CBC), lambda j: (0, j)),
                   pl.BlockSpec((24, CBC), lambda j: (0, j))],
        out_shape=[jax.ShapeDtypeStruct((2, TP, DC), BF16),
                   jax.ShapeDtypeStruct((8 * KWP, DC), F32), jax.ShapeDtypeStruct((24, DC), F32)],
        scratch_shapes=[pltpu.VMEM((TP + KWP, CBC), F32), pltpu.VMEM((TP + KWP, CBC), F32)],
        compiler_params=_cparams(),
    )(dvc, z, z, dw_w, after)


def _lru_bwd(dycat, z, xc, hs, conv_w, wa_g, b_a, wx_g, b_x, lam, after):
    NV = 6

    def body(dy_ref, x_ref, g_ref, xc_ref, hs_ref, cw_ref, wa_ref, ba_ref, wx_ref, bx_ref, lam_ref, after_ref,
             dzl_ref, dwa_ref, dwx_ref, dcw_ref, vecs_ref, ga_s, gx_s, dxc_s):
        vecs_ref[...] = jnp.zeros_like(vecs_ref)
        dcw_ref[...] = jnp.zeros_like(dcw_ref)
        dxc_s[pl.ds(TP, 8), :] = jnp.zeros((8, CB), F32)

        def gate_chunk(ci, carry):
            r0 = pl.multiple_of(ci * TM, TM)
            xb = xc_ref[pl.ds(r0, TM), :].astype(BF16)
            ga_s[pl.ds(r0, TM), :] = jnp.dot(xb, wa_ref[...], preferred_element_type=F32) + ba_ref[...]
            gx_s[pl.ds(r0, TM), :] = jnp.dot(xb, wx_ref[...], preferred_element_type=F32) + bx_ref[...]
            return carry
        lax.fori_loop(0, TP // TM, gate_chunk, 0)

        sp8 = LRU_C * _softplus(-lam_ref[...])
        row = _row_iota((R, CB))
        nchunk = TP // R

        def scan_chunk(cj, carry):
            a_next, lam_next = carry
            ci = nchunk - 1 - cj
            r0 = pl.multiple_of(ci * R, R)
            dyv = dy_ref[pl.ds(r0, R), :]
            g = g_ref[pl.ds(r0, R), :]
            hv = hs_ref[pl.ds(r0, R), :]
            xc = xc_ref[pl.ds(r0, R), :]
            sg = _sig(g)
            dhs = dyv * (g * sg)
            r, i, a, mult = _gate_values(ga_s[pl.ds(r0, R), :], gx_s[pl.ds(r0, R), :], xc, sp8)
            b = jnp.where(row == R - 1, a_next, pltpu.roll(a, R - 1, 0))
            lv = dhs
            k = 1
            while k < R:
                m = row < R - k
                lv = jnp.where(m, lv + b * pltpu.roll(lv, R - k, 0), lv)
                b = jnp.where(m, b * pltpu.roll(b, R - k, 0), b)
                k *= 2
            lv = lv + b * lam_next
            p0 = pl.multiple_of(jnp.maximum(r0 - 8, 0), 8)
            hprev8 = jnp.where(ci > 0, hs_ref[pl.ds(p0, 8), :], 0.0)
            hprev = pltpu.roll(jnp.concatenate([hprev8, hv], axis=0), 1, 0)[8:8 + R, :]
            da = lv * hprev
            ixc = i * xc
            dmult = lv * ixc
            di = lv * mult * xc
            dxc_s[pl.ds(r0, R), :] = lv * mult * i
            a2 = a * a
            dlog_a = da * a - dmult * a2 / mult
            vecs_ref[32:40, :] += _fold8(dlog_a * r)
            dga = -(dlog_a * sp8) * r * (1.0 - r)
            dgx = di * i * (1.0 - i)
            ga_s[pl.ds(r0, R), :] = dga
            gx_s[pl.ds(r0, R), :] = dgx
            vecs_ref[16:24, :] += _fold8(dga)
            vecs_ref[24:32, :] += _fold8(dgx)
            a_first = jnp.sum(jnp.where(row == 0, a, 0.0), axis=0, keepdims=True)
            l_first = jnp.sum(jnp.where(row == 0, lv, 0.0), axis=0, keepdims=True)
            return a_first, l_first
        lax.fori_loop(0, nchunk // 2, lambda i, cr: scan_chunk(2 * i + 1, scan_chunk(2 * i, cr)),
                      (jnp.zeros((1, CB), F32), jnp.zeros((1, CB), F32)))

        dwa_ref[...] = jnp.zeros_like(dwa_ref)
        dwx_ref[...] = jnp.zeros_like(dwx_ref)

        def mm_chunk(ci, carry):
            r0 = pl.multiple_of(ci * TM, TM)
            xb = xc_ref[pl.ds(r0, TM), :].astype(BF16)
            dgab = ga_s[pl.ds(r0, TM), :].astype(BF16)
            dgxb = gx_s[pl.ds(r0, TM), :].astype(BF16)
            dxc_s[pl.ds(r0, TM), :] += (lax.dot_general(dgab, wa_ref[...], _NT, preferred_element_type=F32)
                                        + lax.dot_general(dgxb, wx_ref[...], _NT, preferred_element_type=F32))
            dwa_ref[...] += lax.dot_general(xb, dgab, _TN, preferred_element_type=F32)
            dwx_ref[...] += lax.dot_general(xb, dgxb, _TN, preferred_element_type=F32)
            return carry
        lax.fori_loop(0, TP // TM, mm_chunk, 0)

        taps = [cw_ref[k:k + 1, :] for k in range(LW)]

        def conv_chunk(ci, carry):
            r0 = pl.multiple_of(ci * R, R)
            dbuf = dxc_s[pl.ds(r0, R + 8), :]
            dcur = dbuf[0:R, :]
            p0 = pl.multiple_of(jnp.maximum(r0 - 8, 0), 8)
            xprev = jnp.where(ci > 0, x_ref[pl.ds(p0, 8), :], 0.0)
            xbuf = jnp.concatenate([xprev, x_ref[pl.ds(r0, R), :]], axis=0)
            dxl = dcur * taps[LW - 1]
            dcw_ref[8 * (LW - 1):8 * LW, :] += _fold8(dcur * xbuf[8:8 + R, :])
            for s in range(1, LW):
                k = LW - 1 - s
                dxl = dxl + pltpu.roll(dbuf, R + 8 - s, 0)[0:R, :] * taps[k]
                dcw_ref[8 * k:8 * k + 8, :] += _fold8(dcur * pltpu.roll(xbuf, s, 0)[8:8 + R, :])
            dzl_ref[0, pl.ds(r0, R), :] = dxl.astype(BF16)
            vecs_ref[8:16, :] += _fold8(dxl)
            vecs_ref[40:48, :] += _fold8(dcur)
            return carry
        lax.fori_loop(0, TP // R, conv_chunk, 0)
        vecs_ref[32:40, :] = vecs_ref[32:40, :] * (LRU_C * _sig(-lam_ref[...]))

    col = lambda off: pl.BlockSpec((TP, CB), lambda j: (0, off + j))
    vec = pl.BlockSpec((1, CB), lambda j: (0, j))
    wsp = pl.BlockSpec((None, CB, CB), lambda j: (j, 0, 0))
    return pl.pallas_call(
        body, name="lru_bwd",
        grid=(NCB,),
        in_specs=[col(0), col(0), col(NCB), col(0), col(0), pl.BlockSpec((LW, CB), lambda j: (0, j)),
                  wsp, vec, wsp, vec, vec, _AFTER],
        out_specs=[pl.BlockSpec((1, TP, CB), lambda j: (0, 0, j)), wsp, wsp,
                   pl.BlockSpec((8 * LW, CB), lambda j: (0, j)), pl.BlockSpec((8 * NV, CB), lambda j: (0, j))],
        out_shape=[jax.ShapeDtypeStruct((1, TP, DL), BF16),
                   jax.ShapeDtypeStruct((NCB, CB, CB), F32), jax.ShapeDtypeStruct((NCB, CB, CB), F32),
                   jax.ShapeDtypeStruct((8 * LW, DL), F32), jax.ShapeDtypeStruct((8 * NV, DL), F32)],
        scratch_shapes=[pltpu.VMEM((TP, CB), F32), pltpu.VMEM((TP, CB), F32), pltpu.VMEM((TP + 8, CB), F32)],
        compiler_params=_cparams(),
    )(dycat, z, z, xc, hs, conv_w, wa_g, b_a, wx_g, b_x, lam, after)


def _dz_section(sec, dzl_ref, dz41_ref, dzc_ref, use):
    @pl.when(sec == 0)
    def _():
        use(dzl_ref)

    @pl.when(jnp.logical_or(sec == 1, sec == 4))
    def _():
        use(dz41_ref)

    @pl.when(jnp.logical_or(sec == 2, sec == 3))
    def _():
        use(dzc_ref)


def _dz_specs(rows, index):
    return [pl.BlockSpec((None, rows, 1024), lambda a, b: (0, index(a, b)[0], 0)),
            pl.BlockSpec((None, rows, 1024), lambda a, b: (jnp.where(index(a, b)[1] == 1, 1, 0), index(a, b)[0], 0)),
            pl.BlockSpec((None, rows, 1024), lambda a, b: (jnp.clip(index(a, b)[1] - 2, 0, 1), index(a, b)[0], 0))]


def _inproj_wgrad(name, hn, dzs, after):
    KB = 512
    nsec = dzs.shape[0]

    def body(hn_ref, dz_ref, after_ref, dw_ref):
        dw_ref[...] = lax.dot_general(hn_ref[...], dz_ref[...], _TN, preferred_element_type=F32).astype(BF16)

    return pl.pallas_call(
        body, name=name,
        grid=(nsec, D // KB),
        in_specs=[pl.BlockSpec((TP, KB), lambda n, kb: (0, kb)),
                  pl.BlockSpec((None, TP, 1024), lambda n, kb: (n, 0, 0)), _AFTER],
        out_specs=pl.BlockSpec((KB, 1024), lambda n, kb: (kb, n)),
        out_shape=jax.ShapeDtypeStruct((D, nsec * 1024), BF16),
        compiler_params=_cparams(),
    )(hn, dzs, after)


def _sum_win_parts(parts_a, parts_b, parts_c):
    RB = 64

    def body(a_ref, b_ref, c_ref, o_ref):
        def chunk(ci, carry):
            r0 = pl.multiple_of(ci * R, R)
            for ref, src, base, ncol in ((a_ref, 0, 0, 1024), (c_ref, 1024, 1024, 1024), (b_ref, 0, 2048, 2048),
                                         (c_ref, 0, 4096, 1024)):
                for c0 in range(0, ncol, 512):
                    acc = ref[0, pl.ds(r0, R), src + c0:src + c0 + 512].astype(F32)
                    for sidx in range(1, NDEV):
                        acc = acc + ref[sidx, pl.ds(r0, R), src + c0:src + c0 + 512].astype(F32)
                    o_ref[pl.ds(r0, R), base + c0:base + c0 + 512] = acc.astype(BF16)
            return carry
        lax.fori_loop(0, RB // R, chunk, 0)

    spec = lambda ncol: pl.BlockSpec((NDEV, RB, ncol), lambda i: (0, i, 0))
    return pl.pallas_call(
        body, name="sum_win_parts",
        grid=(D // NDEV // RB,),
        in_specs=[spec(1024), spec(2048), spec(2048)],
        out_specs=pl.BlockSpec((RB, NIN), lambda i: (i, 0)),
        out_shape=jax.ShapeDtypeStruct((D // NDEV, NIN), BF16),
        compiler_params=_cparams(),
    )(parts_a, parts_b, parts_c)


def _inproj_bwd(dzl, dz41, dzc, w_in, h, dout, pre_w, after):
    nsec = NIN // 1024

    def body(dzl_ref, dz41_ref, dzc_ref, w_ref, h_ref, dout_ref, pw_ref, after_ref, gx_hbm, dmeta_ref, dpw_ref,
             acc_s, dh_s, sem):
        i = pl.program_id(0)
        s = pl.program_id(1)

        def gx_copy(t):
            lo, n, off = _tile_rows(t)
            return pltpu.make_async_copy(dh_s.at[pl.ds(off, n)], gx_hbm.at[pl.ds(lo, n)], sem)

        @pl.when(s == 0)
        def _():
            acc_s[...] = jnp.zeros_like(acc_s)

        def use(dz_ref):
            acc_s[...] += lax.dot_general(dz_ref[...], w_ref[...], _NT, preferred_element_type=F32)
        _dz_section(s, dzl_ref, dz41_ref, dzc_ref, use)

        @pl.when(jnp.logical_and(i == 0, s == nsec - 1))
        def _():
            dpw_ref[...] = jnp.zeros_like(dpw_ref)

        @pl.when(s == nsec - 1)
        def _():
            _for_tile(i - 1, lambda t: gx_copy(t).wait())
            pw = pw_ref[...]

            def chunk(ci, carry):
                r0 = pl.multiple_of(ci * 8, 8)
                hv = h_ref[pl.ds(r0, 8), :]
                dhn = acc_s[pl.ds(r0, 8), :]
                rs = lax.rsqrt(jnp.mean(hv * hv, axis=-1, keepdims=True) + EPS)
                dpw_ref[...] += dhn * (hv * rs)
                gw = dhn * pw
                dot = jnp.mean(gw * hv, axis=-1, keepdims=True)
                dh_s[pl.ds(r0, 8), :] = rs * gw - hv * (rs * rs * rs * dot) + dout_ref[pl.ds(r0, 8), :]
                return carry
            lax.fori_loop(0, TM // 8, chunk, 0, unroll=4)
            _for_tile(i, lambda t: gx_copy(t).start())

            @pl.when(i == 0)
            def _():
                dmeta_ref[...] = dh_s[0:NMETA, :]

            @pl.when(i == NTILE - 1)
            def _():
                gx_copy(NTILE - 1).wait()

    row = pl.BlockSpec((TM, D), lambda i, s: (i, 0))
    return pl.pallas_call(
        body, name="inproj_bwd",
        grid=(TP // TM, nsec),
        in_specs=_dz_specs(TM, lambda i, s: (i, s)) + [
            pl.BlockSpec((D, 1024), lambda i, s: (0, s)), row, row, pl.BlockSpec((1, D), lambda i, s: (0, 0)),
            _AFTER],
        out_specs=[pl.BlockSpec(memory_space=pl.ANY), pl.BlockSpec((NMETA, D), lambda i, s: (0, 0)),
                   pl.BlockSpec((8, D), lambda i, s: (0, 0))],
        out_shape=[jax.ShapeDtypeStruct((SEQ, D), F32), jax.ShapeDtypeStruct((NMETA, D), F32),
                   jax.ShapeDtypeStruct((8, D), F32)],
        scratch_shapes=[pltpu.VMEM((TM, D), F32), pltpu.VMEM((TM, D), F32), pltpu.SemaphoreType.DMA(())],
        compiler_params=_cparams(),
    )(dzl, dz41, dzc, w_in, h, dout, pre_w, after)


def _adamw(name, parts, w, m, v, block_rows):
    rows, cols = w.shape
    nparts = parts.shape[0]
    cw = cols if cols <= 640 else 512

    def body(p_ref, w_ref, m_ref, v_ref, g_ref, d_ref, nm_ref, nv_ref):
        def chunk(ci, carry):
            r0 = pl.multiple_of(ci * R, R)
            for c0 in range(0, cols, cw):
                at = (pl.ds(r0, R), slice(c0, c0 + cw))
                g = p_ref[(0,) + at].astype(F32)
                for sidx in range(1, nparts):
                    g = g + p_ref[(sidx,) + at].astype(F32)
                delta, mv, vv = _adam_math(g, w_ref[at], m_ref[at], v_ref[at])
                g_ref[at] = g
                nm_ref[at] = mv
                nv_ref[at] = vv
                d_ref[at] = delta
            return carry
        lax.fori_loop(0, block_rows // R, chunk, 0)

    blk = pl.BlockSpec((block_rows, cols), lambda i: (i, 0))
    shp = jax.ShapeDtypeStruct((rows, cols), F32)
    return pl.pallas_call(
        body, name=name,
        grid=(rows // block_rows,),
        in_specs=[pl.BlockSpec((nparts, block_rows, cols), lambda i: (0, i, 0)), blk, blk, blk],
        out_specs=[blk, blk, blk, blk],
        out_shape=[shp, shp, shp, shp],
        compiler_params=_cparams(),
    )(parts, w, m, v)


def _adam_math(g, w, m, v):
    c1 = 1.0 / (1.0 - ADAM_B1 ** ADAM_STEP)
    c2 = 1.0 / (1.0 - ADAM_B2 ** ADAM_STEP)
    mv = ADAM_B1 * m + (1.0 - ADAM_B1) * g
    vv = ADAM_B2 * v + (1.0 - ADAM_B2) * (g * g)
    upd = (mv * c1) / (jnp.sqrt(vv * c2) + ADAM_EPS) + ADAM_WD * w
    return -ADAM_LR * upd, mv, vv


_VEC = [("pre_norm_w", 2), ("post_norm_w", 2), ("b_in", 5), ("lru_conv_b", 1), ("b_gate_a", 1), ("b_gate_x", 1),
        ("lru_lambda", 1), ("conf_dw_b", 1), ("conf_ln_w", 1), ("conf_ln_b", 1), ("conf_pw_b", 1)]
_VEC_ROWS = 24
_LOSS_ROW = 17
_SM_ROWS = 64


def _pack_grads(dprew_acc, dpostw_acc, cvecs, kvecs, lvecs, dcw_acc, ddw_acc, dh, loss_acc):
    def body(pre_ref, post_ref, c_ref, k_ref, l_ref, dcw_ref, ddw_ref, dh_ref, loss_ref, vec_ref, small_ref, tmp):
        s8 = lambda ref, r: jnp.sum(ref[8 * r:8 * r + 8, :], axis=0, keepdims=True)
        vec_ref[...] = jnp.zeros_like(vec_ref)
        pre, post = s8(pre_ref, 0), s8(post_ref, 0)
        rows = [pre[:, 0:1024], pre[:, 1024:2048], post[:, 0:1024], post[:, 1024:2048],
                s8(l_ref, 1), s8(c_ref, 4), s8(k_ref, 1), s8(k_ref, 2), s8(c_ref, 1),
                s8(l_ref, 5), s8(l_ref, 2), s8(l_ref, 3), s8(l_ref, 4),
                s8(k_ref, 0), s8(c_ref, 2), s8(c_ref, 3), s8(c_ref, 0)]
        for r, val in enumerate(rows):
            vec_ref[r:r + 1, :] = val
        vec_ref[_LOSS_ROW:_LOSS_ROW + 1, :] = jnp.zeros((1, 1024), F32) + (0.5 / D) * jnp.sum(loss_ref[...])

        small_ref[...] = jnp.zeros_like(small_ref)
        for k in range(LW):
            tmp[k:k + 1, :] = s8(dcw_ref, k)
        for k in range(KW):
            tmp[8 + k:9 + k, :] = s8(ddw_ref, k)
        for d in range(NDEV):
            small_ref[d, 0:LW, 0:128] = tmp[0:LW, 128 * d:128 * d + 128]
            small_ref[d, 8:8 + KW, 0:128] = tmp[8:8 + KW, 128 * d:128 * d + 128]
            small_ref[d, 40:56, :] = dh_ref[:, 256 * d:256 * d + 256]

    full = lambda a: pl.BlockSpec(a.shape, lambda i: (0,) * a.ndim)
    ins = [dprew_acc, dpostw_acc, cvecs, kvecs, lvecs, dcw_acc, ddw_acc]
    return pl.pallas_call(
        body, name="pack_grads",
        grid=(1,),
        in_specs=[full(a) for a in ins] + [full(dh), full(loss_acc)],
        out_specs=[pl.BlockSpec((_VEC_ROWS, 1024), lambda i: (0, 0)),
                   pl.BlockSpec((NDEV, _SM_ROWS, 256), lambda i: (0, 0, 0))],
        out_shape=[jax.ShapeDtypeStruct((_VEC_ROWS, 1024), F32), jax.ShapeDtypeStruct((NDEV, _SM_ROWS, 256), F32)],
        scratch_shapes=[pltpu.VMEM((40, 1024), F32)],
        compiler_params=_cparams(),
    )(*ins, dh, loss_acc)


def _adamw_vec(parts, W, M, V):
    nv = len(_VEC)

    def body(*refs):
        p_ref = refs[0]
        w_refs, m_refs, v_refs = refs[1:1 + nv], refs[1 + nv:1 + 2 * nv], refs[1 + 2 * nv:1 + 3 * nv]
        outs = refs[1 + 3 * nv:]

        def total(r):
            acc = p_ref[0, r:r + 1, :]
            for sidx in range(1, NDEV):
                acc = acc + p_ref[sidx, r:r + 1, :]
            return acc

        row = 0
        for idx, (_, nrows) in enumerate(_VEC):
            for part in range(nrows):
                cols = slice(1024 * part, 1024 * part + 1024)
                g = total(row + part)
                delta, mv, vv = _adam_math(g, w_refs[idx][:, cols], m_refs[idx][:, cols], v_refs[idx][:, cols])
                for o, val in zip(outs[4 * idx:4 * idx + 4], (g, delta, mv, vv)):
                    o[:, cols] = val
            row += nrows
        outs[-1][...] = total(_LOSS_ROW)[:, 0:128]

    names = [n for n, _ in _VEC]
    flat = lambda d: [d[n].reshape(1, -1) for n in names]
    ws, ms, vs = flat(W), flat(M), flat(V)
    res = pl.pallas_call(
        body, name="adamw_vec",
        out_shape=[jax.ShapeDtypeStruct(w.shape, F32) for w in ws for _ in range(4)]
        + [jax.ShapeDtypeStruct((1, 128), F32)],
        compiler_params=_cparams(),
    )(parts, *ws, *ms, *vs)
    return {n: tuple(res[4 * i:4 * i + 4]) for i, n in enumerate(names)}, res[-1]


def _adamw_small(parts, W, M, V):
    where = {"lru_conv_w": (slice(0, LW), slice(0, 128)), "conf_dw_w": (slice(8, 8 + KW), slice(0, 128)),
             "meta_tokens": (slice(40, 56), slice(0, 256))}
    names = list(where)

    def body(*refs):
        p_ref = refs[0]
        outs = refs[10:]
        for idx, n in enumerate(names):
            rs, cs = where[n]
            g = p_ref[0, rs, cs]
            for sidx in range(1, NDEV):
                g = g + p_ref[sidx, rs, cs]
            delta, mv, vv = _adam_math(g, refs[1 + idx][...], refs[4 + idx][...], refs[7 + idx][...])
            for o, val in zip(outs[4 * idx:4 * idx + 4], (g, delta, mv, vv)):
                o[...] = val

    two_d = lambda a: a.reshape(a.shape[-2:])
    ws, ms, vs = ([two_d(d[n]) for n in names] for d in (W, M, V))
    res = pl.pallas_call(
        body, name="adamw_small",
        out_shape=[jax.ShapeDtypeStruct(w.shape, F32) for w in ws for _ in range(4)],
        compiler_params=_cparams(),
    )(parts, *ws, *ms, *vs)
    return {n: tuple(res[4 * i:4 * i + 4]) for i, n in enumerate(names)}


def _pack_small(lru_cw, dw_w, meta):
    buf = jnp.zeros((_SM_ROWS, 256), F32)
    buf = buf.at[0:LW, 0:128].set(lru_cw)
    buf = buf.at[8:8 + dw_w.shape[0], 0:128].set(dw_w)
    return buf.at[40:56, :].set(meta)


def _block_diag4(w):
    w4 = w.reshape(NCB, 4, 64, 64)
    eye = jnp.eye(4, dtype=w.dtype)
    return jnp.einsum("ghij,hk->ghikj", w4, eye).reshape(NCB, CB, CB)


def _diag_blocks(g):
    g5 = g.reshape(NCB, 4, 64, 4, 64)
    return jnp.stack([g5[:, hh, :, hh, :] for hh in range(4)], axis=1).reshape(16, 64, 64)


def _gate_mats(W):
    return _block_diag4(W["w_gate_a"][0]).astype(BF16), _block_diag4(W["w_gate_x"][0]).astype(BF16)


def _local_step(x, target, meta_full, inproj, out_weights, lru_cw_full, dw_w_full, W, gate_mats, send):
    wa_g, wx_g = gate_mats

    h, hn = _prenorm(x, meta_full, W["pre_norm_w"])
    z, win_full = inproj(hn)
    ylru, xc, hs = _lru_fwd(z, lru_cw_full, W["lru_conv_b"], wa_g, W["b_gate_a"], wx_g, W["b_gate_x"],
                            W["lru_lambda"])
    vc = _conf_fwd_conv(z, dw_w_full, W["conf_dw_b"])
    wout_full, pw_full = out_weights(vc)
    yconf, p = _conf_fwd_proj(vc, z, W["conf_ln_w"], W["conf_ln_b"], pw_full, W["conf_pw_b"])
    dout, dy, loss_acc, dpostw_acc = _outproj_loss(ylru, yconf, wout_full, h, target, W["post_norm_w"])

    dycat, dwout_part = _outproj_bwd(dy, ylru, yconf, wout_full)
    tok = send("w_out", ("w_out", dwout_part))
    dvc, dz41, dpw_part, cvecs = _conf_bwd_proj(dycat, p, z, vc, hs, W["conf_ln_w"], W["conf_ln_b"], pw_full, tok)
    tok = send("w_in_c", ("conf_pw_w", dpw_part), ("w_in_c", _inproj_wgrad("inproj_wgrad_c", hn, dz41, dz41)))
    dzc, ddw_acc, kvecs = _conf_bwd_conv(dvc, z, dw_w_full, tok)
    tok = send("w_in_b", ("w_in_b", _inproj_wgrad("inproj_wgrad_b", hn, dzc, dzc)))
    dzl, dwa_g, dwx_g, dcw_acc, lvecs = _lru_bwd(dycat, z, xc, hs, lru_cw_full, wa_g, W["b_gate_a"], wx_g,
                                                 W["b_gate_x"], W["lru_lambda"], tok)
    tok = send("w_gates", ("w_gate_a", _diag_blocks(dwa_g).reshape(16 * 64, 64)),
               ("w_gate_x", _diag_blocks(dwx_g).reshape(16 * 64, 64)))
    tok = send("w_in_a", ("w_in_a", _inproj_wgrad("inproj_wgrad_a", hn, dzl, tok)))
    grad_x, dmeta, dprew_acc = _inproj_bwd(dzl, dz41, dzc, win_full, h, dout, W["pre_norm_w"], tok)

    vec_pack, small_part = _pack_grads(dprew_acc, dpostw_acc, cvecs, kvecs, lvecs, dcw_acc, ddw_acc, dmeta, loss_acc)
    return grad_x, vec_pack, small_part


def kernel(x, meta_tokens, pre_norm_w, post_norm_w, w_in, b_in, lru_conv_w, lru_conv_b, w_gate_a, b_gate_a, w_gate_x, b_gate_x, lru_lambda, conf_dw_w, conf_dw_b, conf_ln_w, conf_ln_b, conf_pw_w, conf_pw_b, w_out, loss_target, m_meta_tokens, m_pre_norm_w, m_post_norm_w, m_w_in, m_b_in, m_lru_conv_w, m_lru_conv_b, m_w_gate_a, m_b_gate_a, m_w_gate_x, m_b_gate_x, m_lru_lambda, m_conf_dw_w, m_conf_dw_b, m_conf_ln_w, m_conf_ln_b, m_conf_pw_w, m_conf_pw_b, m_w_out, v_meta_tokens, v_pre_norm_w, v_post_norm_w, v_w_in, v_b_in, v_lru_conv_w, v_lru_conv_b, v_w_gate_a, v_b_gate_a, v_w_gate_x, v_b_gate_x, v_lru_lambda, v_conf_dw_w, v_conf_dw_b, v_conf_ln_w, v_conf_ln_b, v_conf_pw_w, v_conf_pw_b, v_w_out):
    W = dict(meta_tokens=meta_tokens, pre_norm_w=pre_norm_w, post_norm_w=post_norm_w, w_in=w_in, b_in=b_in,
             lru_conv_w=lru_conv_w, lru_conv_b=lru_conv_b, w_gate_a=w_gate_a, b_gate_a=b_gate_a,
             w_gate_x=w_gate_x, b_gate_x=b_gate_x, lru_lambda=lru_lambda, conf_dw_w=conf_dw_w,
             conf_dw_b=conf_dw_b, conf_ln_w=conf_ln_w, conf_ln_b=conf_ln_b, conf_pw_w=conf_pw_w,
             conf_pw_b=conf_pw_b, w_out=w_out)
    M = dict(meta_tokens=m_meta_tokens, pre_norm_w=m_pre_norm_w, post_norm_w=m_post_norm_w, w_in=m_w_in,
             b_in=m_b_in, lru_conv_w=m_lru_conv_w, lru_conv_b=m_lru_conv_b, w_gate_a=m_w_gate_a,
             b_gate_a=m_b_gate_a, w_gate_x=m_w_gate_x, b_gate_x=m_b_gate_x, lru_lambda=m_lru_lambda,
             conf_dw_w=m_conf_dw_w, conf_dw_b=m_conf_dw_b, conf_ln_w=m_conf_ln_w, conf_ln_b=m_conf_ln_b,
             conf_pw_w=m_conf_pw_w, conf_pw_b=m_conf_pw_b, w_out=m_w_out)
    V = dict(meta_tokens=v_meta_tokens, pre_norm_w=v_pre_norm_w, post_norm_w=v_post_norm_w, w_in=v_w_in,
             b_in=v_b_in, lru_conv_w=v_lru_conv_w, lru_conv_b=v_lru_conv_b, w_gate_a=v_w_gate_a,
             b_gate_a=v_b_gate_a, w_gate_x=v_w_gate_x, b_gate_x=v_b_gate_x, lru_lambda=v_lru_lambda,
             conf_dw_w=v_conf_dw_w, conf_dw_b=v_conf_dw_b, conf_ln_w=v_conf_ln_w, conf_ln_b=v_conf_ln_b,
             conf_pw_w=v_conf_pw_w, conf_pw_b=v_conf_pw_b, w_out=v_w_out)
    names = list(W.keys())
    shapes = {n: W[n].shape for n in names}

    small = _pack_small(lru_conv_w[0], conf_dw_w[0], meta_tokens)
    (small_flight,), tok = _exchange_start("gather_small_start", [
        (small, jax.ShapeDtypeStruct((NDEV, _SM_ROWS, 256), F32), _whole, _slot)])
    win_flight, tok = _win_gather_start(w_in[0].astype(BF16) + tok[0, 0].astype(BF16))
    gate_mats = _gate_mats(W)
    wout_shard = w_out[0].astype(BF16) + tok[0, 0].astype(BF16)
    pw_shard = conf_pw_w[0].astype(BF16)
    cast_done = (gate_mats[0][0, 0:8, 0:128] + gate_mats[1][0, 0:8, 0:128]
                 + wout_shard[0:8, 0:128] + pw_shard[0:8, 0:128])
    win_flight, tok = _win_gather_links(win_flight, cast_done)
    gathered, tok = _exchange_start("gather_out_start", [
        (wout_shard + tok[0, 0].astype(BF16), jax.ShapeDtypeStruct((D, D), BF16), _whole, _rows(D // NDEV)),
        (pw_shard, jax.ShapeDtypeStruct((DC, DC), BF16), _whole, _rows(DC // NDEV)),
    ])
    (small_all,) = _exchange_wait("gather_small_wait", [small_flight], tok)
    unshard = lambda a: jnp.transpose(a, (1, 0, 2)).reshape(a.shape[1], -1)
    lru_cw_full = unshard(small_all[:, 0:LW, 0:128])
    dw_w_full = unshard(small_all[:, 8:8 + KWP, 0:128])
    meta_full = unshard(small_all[:, 40:56, :])

    def out_weights(after):
        return _exchange_wait("gather_out_wait", gathered, after)

    def inproj(hn):
        xi, yi, ci = lax.axis_index("x"), lax.axis_index("y"), lax.axis_index("c")
        shard = lambda px, py, pc: (4 * px + 2 * py + pc).astype(jnp.int32)
        over_links = jnp.stack([shard(1 - xi, yi, ci), shard(xi, 1 - yi, ci), shard(1 - xi, 1 - yi, ci)])
        z, src = _inproj_cols("inproj_own", jnp.stack([shard(xi, yi, ci)]), hn, win_flight["src"], b_in, None)
        flight = _win_gather_early(dict(win_flight, src=src))
        z, land = _inproj_cols("inproj_here", jnp.stack([shard(xi, yi, 1 - ci)]), hn, flight["land"], b_in, z)
        flight = _win_gather_forward("all", dict(flight, land=land), (1, 2, 3), z)
        z, land = _inproj_cols("inproj_links", over_links, hn, flight["land"], b_in, z)
        flight = _win_gather_forwarded("all", dict(flight, land=land), (1, 2, 3))
        z, land = _inproj_cols("inproj_sibling", over_links + 1 - 2 * ci, hn, flight["land"], b_in, z)
        return z, _win_gather_wait(dict(flight, land=land))

    row_stage = lambda ncol: (jax.ShapeDtypeStruct((NDEV, D // NDEV, ncol), BF16), _rows(D // NDEV))
    piece = {"w_in_a": row_stage(1024), "w_in_b": row_stage(2048), "w_in_c": row_stage(2048),
             "w_out": row_stage(D),
             "conf_pw_w": (jax.ShapeDtypeStruct((NDEV, DC // NDEV, DC), BF16), _rows(DC // NDEV)),
             "w_gate_a": (jax.ShapeDtypeStruct((NDEV, 16 * 64, 64), BF16), _whole),
             "w_gate_x": (jax.ShapeDtypeStruct((NDEV, 16 * 64, 64), BF16), _whole)}
    sent = {}

    def send(call, *named_parts):
        handles, token = _exchange_start(
            "scatter_" + call + "_start",
            [(part.astype(BF16), piece[name][0], piece[name][1], _slot) for name, part in named_parts])
        for (name, _), handle in zip(named_parts, handles):
            sent[name] = [handle]
        return token

    grad_x, vec_pack, small_part = _local_step(
        x[0], loss_target[0], meta_full, inproj, out_weights, lru_cw_full, dw_w_full, W, gate_mats, send)
    grad_x = grad_x[None]

    rest, tok = _exchange_start("scatter_rest_start", [
        (small_part, jax.ShapeDtypeStruct((NDEV, _SM_ROWS, 256), F32), _slot, _slot),
        (vec_pack, jax.ShapeDtypeStruct((NDEV, _VEC_ROWS, 1024), F32), _whole, _slot),
    ])
    (parts_c,) = _exchange_wait("scatter_w_in_c_wait", sent["w_in_c"], tok)
    (parts_b,) = _exchange_wait("scatter_w_in_b_wait", sent["w_in_b"], parts_c)
    (parts_a,) = _exchange_wait("scatter_w_in_a_wait", sent["w_in_a"], parts_b)
    win_rows = _sum_win_parts(parts_a, parts_b, parts_c)
    win_stage2, tok = _exchange_start("scatter_w_in_stage2_start", [
        (win_rows, jax.ShapeDtypeStruct((NDEV, D // NDEV, NIN // NDEV), BF16), _cols(NIN // NDEV), _slot)])

    G, DW, NM, NV = {}, {}, {}, {}
    (wout_parts,) = _exchange_wait("scatter_w_out_wait", sent["w_out"], tok)
    G["w_out"], DW["w_out"], NM["w_out"], NV["w_out"] = _adamw("adamw_w_out", wout_parts, w_out[0], m_w_out[0], v_w_out[0], 64)
    (pw_parts,) = _exchange_wait("scatter_conf_pw_w_wait", sent["conf_pw_w"], G["w_out"])
    G["conf_pw_w"], DW["conf_pw_w"], NM["conf_pw_w"], NV["conf_pw_w"] = _adamw(
        "adamw_pw", pw_parts, conf_pw_w[0], m_conf_pw_w[0], v_conf_pw_w[0], 128)
    res = {}
    wa_parts, wx_parts = _exchange_wait("scatter_w_gates_wait", sent["w_gate_a"] + sent["w_gate_x"], G["conf_pw_w"])
    for n, parts in (("w_gate_a", wa_parts), ("w_gate_x", wx_parts)):
        res[n] = _adamw("adamw_" + n, parts, *[d[n].reshape(16 * 64, 64) for d in (W, M, V)], 16 * 64)
    small_parts, vec_parts = _exchange_wait("scatter_rest_wait", rest, res["w_gate_x"][0])
    res.update(_adamw_small(small_parts, W, M, V))
    vec_res, loss_row = _adamw_vec(vec_parts, W, M, V)
    res.update(vec_res)
    (win_sum,) = _exchange_wait("scatter_w_in_stage2_wait", win_stage2, loss_row)
    res["w_in"] = _adamw("adamw_w_in", win_sum.reshape(1, D, NIN // NDEV), w_in[0], m_w_in[0], v_w_in[0], 256)
    for n, vals in res.items():
        for dst, val in zip((G, DW, NM, NV), vals):
            dst[n] = val
    for dst in (G, DW, NM, NV):
        for n in names:
            dst[n] = dst[n].reshape(shapes[n])
    loss = loss_row[0, 0]

    return (loss, grad_x, *[G[n] for n in names], *[DW[n] for n in names],
            *[NM[n] for n in names], *[NV[n] for n in names])
```

```python
import functools

import jax
import jax.numpy as jnp
from jax import lax
from jax.experimental import pallas as pl
from jax.experimental.pallas import tpu as pltpu

F32 = jnp.float32
BF16 = jnp.bfloat16

D = 2048
DL = 1024
DC = 1024
NIN = 5120
NMETA = 16
SEQ = 2048
T = NMETA + SEQ
TP = 2176
TM = 544
CB = 256
NCB = DL // CB
R = 16
KW = 31
KWP = 32
LW = 4
LRU_C = 8.0
EPS = 1e-6
NDEV = 8

ADAM_LR = 0.001
ADAM_B1 = 0.9
ADAM_B2 = 0.999
ADAM_EPS = 1e-08
ADAM_WD = 0.01
ADAM_STEP = 10

VMEM_LIMIT = 56 * 1024 * 1024


def _cparams():
    return pltpu.CompilerParams(vmem_limit_bytes=VMEM_LIMIT)


def _sig(x):
    return 1.0 / (1.0 + jnp.exp(-x))


def _expm1_neg(y):
    poly = y * (1.0 + y * (0.5 + y * (1.0 / 6.0 + y * (1.0 / 24.0 + y * (1.0 / 120.0)))))
    return jnp.where(y > -0.1, poly, jnp.exp(y) - 1.0)


def _softplus(x):
    e = jnp.exp(-jnp.abs(x))
    w = 1.0 + e
    l1p = jnp.where(w == 1.0, e, jnp.log(w) * e / (w - 1.0))
    return jnp.maximum(x, 0.0) + l1p


def _row_iota(shape):
    return lax.broadcasted_iota(jnp.int32, shape, 0)


def _fold8(v):
    return v[0:8, :] + v[8:16, :]


_FLIPS = [(k >> 2 & 1, k >> 1 & 1, k & 1) for k in range(1, NDEV)]
_HBM = pl.BlockSpec(memory_space=pltpu.HBM)
_SEM = pl.BlockSpec(memory_space=pltpu.SEMAPHORE)


def _peers():
    x, y, c = lax.axis_index("x"), lax.axis_index("y"), lax.axis_index("c")
    out = []
    for dx, dy, dc in _FLIPS:
        px = 1 - x if dx else x
        py = 1 - y if dy else y
        pc = 1 - c if dc else c
        out.append(((px, py, pc), 4 * px + 2 * py + pc))
    return 4 * x + 2 * y + c, out


def _exchange_start(name, items):
    n = len(items)

    def body(*refs):
        srcs, lands = refs[:n], refs[n:2 * n]
        outs = refs[2 * n:]
        send_sems, recv_sems, local_sems = outs[:n], outs[n:2 * n], outs[2 * n:3 * n]
        token = outs[-1]
        me, peers = _peers()
        for a in range(n):
            src_at, dst_at = items[a][2], items[a][3]
            pltpu.make_async_copy(src_at(srcs[a], me), dst_at(lands[a], me), local_sems[a]).start()
        for a in range(n):
            src_at, dst_at = items[a][2], items[a][3]
            for k, (pos, peer) in enumerate(peers):
                pltpu.make_async_remote_copy(
                    src_ref=src_at(srcs[a], peer), dst_ref=dst_at(lands[a], me),
                    send_sem=send_sems[a].at[k], recv_sem=recv_sems[a].at[k],
                    device_id=pos, device_id_type=pl.DeviceIdType.MESH).start()
        token[...] = jnp.zeros_like(token)

    srcs = [pltpu.with_memory_space_constraint(it[0], pltpu.HBM) for it in items]
    lands = [pltpu.with_memory_space_constraint(lax.empty(it[1].shape, it[1].dtype), pltpu.HBM) for it in items]
    sem7 = pltpu.SemaphoreType.DMA((NDEV - 1,))
    res = pl.pallas_call(
        body, name=name,
        out_shape=([sem7] * (2 * n) + [pltpu.SemaphoreType.DMA(())] * n
                   + [pltpu.HBM(a.shape, a.dtype) for a in srcs] + [pltpu.HBM(a.shape, a.dtype) for a in lands]
                   + [jax.ShapeDtypeStruct((8, 128), F32)]),
        in_specs=[_HBM] * (2 * n),
        out_specs=[_SEM] * (3 * n) + [_HBM] * (2 * n) + [pl.BlockSpec(memory_space=pltpu.VMEM)],
        input_output_aliases={i: 3 * n + i for i in range(2 * n)},
        compiler_params=pltpu.CompilerParams(has_side_effects=pltpu.SideEffectType.DATAFLOW_SIDE_EFFECTING),
    )(*srcs, *lands)
    handles = [dict(send=res[a], recv=res[n + a], local=res[2 * n + a], src=res[3 * n + a], land=res[4 * n + a],
                    src_at=items[a][2], dst_at=items[a][3]) for a in range(n)]
    return handles, res[-1]


def _wait_bytes(piece, sem):
    pltpu.make_async_copy(piece, piece, sem).wait()


def _exchange_wait(name, handles, after):
    n = len(handles)

    def body(*refs):
        srcs, lands = refs[:n], refs[n:2 * n]
        send_sems, recv_sems, local_sems = refs[2 * n:3 * n], refs[3 * n:4 * n], refs[4 * n:5 * n]
        me, peers = _peers()
        for a in range(n):
            src_at, dst_at = handles[a]["src_at"], handles[a]["dst_at"]
            for k, (pos, peer) in enumerate(peers):
                _wait_bytes(src_at(srcs[a], peer), send_sems[a].at[k])
                _wait_bytes(dst_at(lands[a], peer), recv_sems[a].at[k])
            pltpu.make_async_copy(src_at(srcs[a], me), dst_at(lands[a], me), local_sems[a]).wait()

    srcs = [hd["src"] for hd in handles]
    lands = [hd["land"] for hd in handles]
    res = pl.pallas_call(
        body, name=name,
        out_shape=[pltpu.HBM(a.shape, a.dtype) for a in srcs] + [pltpu.HBM(a.shape, a.dtype) for a in lands],
        in_specs=[_HBM] * (2 * n) + [_SEM] * (3 * n) + [pl.BlockSpec(memory_space=pl.ANY)],
        out_specs=[_HBM] * (2 * n),
        input_output_aliases={i: i for i in range(2 * n)},
        compiler_params=pltpu.CompilerParams(has_side_effects=pltpu.SideEffectType.DATAFLOW_SIDE_EFFECTING),
    )(*srcs, *lands, *[hd["send"] for hd in handles], *[hd["recv"] for hd in handles],
      *[hd["local"] for hd in handles], after)
    return list(res[n:])


_SIDE = pltpu.SideEffectType.DATAFLOW_SIDE_EFFECTING
_WCOLS = NIN // NDEV


def _win_cols(ref, l):
    return ref.at[:, pl.ds(pl.multiple_of(l * _WCOLS, 128), _WCOLS)]


def _win_routes():
    x, y, c = lax.axis_index("x"), lax.axis_index("y"), lax.axis_index("c")
    pos = [(x, y, 1 - c), (1 - x, y, c), (x, 1 - y, c), (1 - x, 1 - y, c)]
    return 4 * x + 2 * y + c, [(p, 4 * p[0] + 2 * p[1] + p[2]) for p in pos]


def _win_gather_start(shard):
    def body(src, land, send_sem, recv_sem, local_sem, src_thru, land_thru, token):
        me, routes = _win_routes()
        pltpu.make_async_copy(src, _win_cols(land, me), local_sem).start()
        pltpu.make_async_remote_copy(src_ref=src, dst_ref=_win_cols(land, me), send_sem=send_sem, recv_sem=recv_sem,
                                     device_id=routes[0][0], device_id_type=pl.DeviceIdType.MESH).start()
        token[...] = jnp.zeros_like(token)

    src = pltpu.with_memory_space_constraint(shard, pltpu.HBM)
    land = pltpu.with_memory_space_constraint(lax.empty((D, NIN), BF16), pltpu.HBM)
    sem = pltpu.SemaphoreType.DMA(())
    res = pl.pallas_call(
        body, name="win_gather_start",
        out_shape=[sem, sem, sem, pltpu.HBM(src.shape, BF16), pltpu.HBM(land.shape, BF16),
                   jax.ShapeDtypeStruct((8, 128), F32)],
        in_specs=[_HBM, _HBM],
        out_specs=[_SEM, _SEM, _SEM, _HBM, _HBM, pl.BlockSpec(memory_space=pltpu.VMEM)],
        input_output_aliases={0: 3, 1: 4},
        compiler_params=pltpu.CompilerParams(has_side_effects=_SIDE),
    )(src, land)
    return dict(send0=res[0], recv0=res[1], local=res[2], src=res[3], land=res[4]), res[5]


def _win_gather_links(hd, after):
    def body(src, land, after_ref, send_sems, recv_sems, src_thru, land_thru, token):
        me, routes = _win_routes()
        for k in (1, 2, 3):
            pltpu.make_async_remote_copy(src_ref=src, dst_ref=_win_cols(land, me), send_sem=send_sems.at[k - 1],
                                         recv_sem=recv_sems.at[k - 1], device_id=routes[k][0],
                                         device_id_type=pl.DeviceIdType.MESH).start()
        token[...] = jnp.zeros_like(token)

    sem3 = pltpu.SemaphoreType.DMA((3,))
    res = pl.pallas_call(
        body, name="win_gather_links",
        out_shape=[sem3, sem3, pltpu.HBM(hd["src"].shape, BF16), pltpu.HBM(hd["land"].shape, BF16),
                   jax.ShapeDtypeStruct((8, 128), F32)],
        in_specs=[_HBM, _HBM, pl.BlockSpec(memory_space=pl.ANY)],
        out_specs=[_SEM, _SEM, _HBM, _HBM, pl.BlockSpec(memory_space=pltpu.VMEM)],
        input_output_aliases={0: 2, 1: 3},
        compiler_params=pltpu.CompilerParams(has_side_effects=_SIDE),
    )(hd["src"], hd["land"], after)
    return dict(hd, send=res[0], recv=res[1], src=res[2], land=res[3]), res[4]


def _win_gather_forward(name, hd, ks, after):
    def body(land, recv_sems, after_ref, land_thru, fsend_sems, frecv_sems):
        me, routes = _win_routes()
        sibling = routes[0][0]
        for n, k in enumerate(ks):
            pos, peer = routes[k]
            piece = _win_cols(land, peer)
            pltpu.make_async_remote_copy(src_ref=piece, dst_ref=piece, send_sem=fsend_sems.at[n],
                                         recv_sem=recv_sems.at[k - 1], device_id=pos,
                                         device_id_type=pl.DeviceIdType.MESH).wait_recv()
            pltpu.make_async_remote_copy(src_ref=piece, dst_ref=piece, send_sem=fsend_sems.at[n],
                                         recv_sem=frecv_sems.at[n], device_id=sibling,
                                         device_id_type=pl.DeviceIdType.MESH).start()

    sems = pltpu.SemaphoreType.DMA((len(ks),))
    res = pl.pallas_call(
        body, name="win_gather_forward_" + name,
        out_shape=[pltpu.HBM(hd["land"].shape, BF16), sems, sems],
        in_specs=[_HBM, _SEM, pl.BlockSpec(memory_space=pl.ANY)],
        out_specs=[_HBM, _SEM, _SEM],
        input_output_aliases={0: 0},
        compiler_params=pltpu.CompilerParams(has_side_effects=_SIDE),
    )(hd["land"], hd["recv"], after)
    return dict(hd, land=res[0], **{"fsend" + name: res[1], "frecv" + name: res[2]})


def _win_gather_forwarded(name, hd, ks):
    def body(land, fsend_sems, frecv_sems, land_thru):
        me, routes = _win_routes()
        sib_c = routes[0][0][2]
        for n, k in enumerate(ks):
            _wait_bytes(_win_cols(land, routes[k][1]), fsend_sems.at[n])
            _wait_bytes(_win_cols(land, 4 * routes[k][0][0] + 2 * routes[k][0][1] + sib_c), frecv_sems.at[n])

    res = pl.pallas_call(
        body, name="win_gather_forwarded_" + name,
        out_shape=[pltpu.HBM(hd["land"].shape, BF16)],
        in_specs=[_HBM, _SEM, _SEM],
        out_specs=[_HBM],
        input_output_aliases={0: 0},
        compiler_params=pltpu.CompilerParams(has_side_effects=_SIDE),
    )(hd["land"], hd["fsend" + name], hd["frecv" + name])
    return dict(hd, land=res[0])


def _win_gather_early(hd):
    def body(src, land, recv_sem, local_sem, src_thru, land_thru):
        me, routes = _win_routes()
        _wait_bytes(_win_cols(land, routes[0][1]), recv_sem)
        pltpu.make_async_copy(src, _win_cols(land, me), local_sem).wait()

    res = pl.pallas_call(
        body, name="win_gather_early",
        out_shape=[pltpu.HBM(hd["src"].shape, BF16), pltpu.HBM(hd["land"].shape, BF16)],
        in_specs=[_HBM, _HBM, _SEM, _SEM],
        out_specs=[_HBM, _HBM],
        input_output_aliases={0: 0, 1: 1},
        compiler_params=pltpu.CompilerParams(has_side_effects=_SIDE),
    )(hd["src"], hd["land"], hd["recv0"], hd["local"])
    return dict(hd, src=res[0], land=res[1])


def _win_gather_wait(hd):
    def body(src, land, send0_sem, send_sems, src_thru, land_thru):
        for k in range(4):
            _wait_bytes(src, send0_sem if k == 0 else send_sems.at[k - 1])

    res = pl.pallas_call(
        body, name="win_gather_wait",
        out_shape=[pltpu.HBM(hd["src"].shape, BF16), pltpu.HBM(hd["land"].shape, BF16)],
        in_specs=[_HBM, _HBM, _SEM, _SEM],
        out_specs=[_HBM, _HBM],
        input_output_aliases={0: 0, 1: 1},
        compiler_params=pltpu.CompilerParams(has_side_effects=_SIDE),
    )(hd["src"], hd["land"], hd["send0"], hd["send"])
    return res[1]


def _whole(ref, l):
    return ref


def _slot(ref, l):
    return ref.at[l]


def _cols(width):
    def at(ref, l):
        return ref.at[:, pl.ds(pl.multiple_of(l * width, 128), width)]
    return at


def _rows(height):
    def at(ref, l):
        return ref.at[pl.ds(pl.multiple_of(l * height, 8), height), :]
    return at


NTILE = TP // TM


def _tile_rows(t):
    lo = max(t * TM - NMETA, 0)
    hi = min((t + 1) * TM - NMETA, SEQ)
    return lo, hi - lo, lo + NMETA - t * TM


def _for_tile(t, fn):
    for static_t in range(NTILE):
        pl.when(t == static_t)(functools.partial(fn, static_t))


def _token_tile_copy(hbm_ref, buf, sem, t):
    lo, n, off = _tile_rows(t)
    return pltpu.make_async_copy(hbm_ref.at[pl.ds(lo, n)], buf.at[pl.ds(off, n)], sem)


def _prenorm(x, meta_full, pre_w):
    def body(x_ref, meta_ref, pw_ref, h_ref, hn_ref, xbuf, sems):
        i = pl.program_id(0)
        slot = i % 2

        def start(t):
            _token_tile_copy(x_ref, xbuf.at[t % 2], sems.at[t % 2], t).start()

        @pl.when(i == 0)
        def _():
            start(0)
        _for_tile(i + 1, start)
        _for_tile(i, lambda t: _token_tile_copy(x_ref, xbuf.at[t % 2], sems.at[t % 2], t).wait())

        @pl.when(i == 0)
        def _():
            xbuf[0, 0:NMETA, :] = meta_ref[...]

        @pl.when(i == NTILE - 1)
        def _():
            last = _tile_rows(NTILE - 1)[1]
            xbuf[(NTILE - 1) % 2, last:TM, :] = jnp.zeros((TM - last, D), F32)

        pw = pw_ref[...]

        def chunk(ci, carry):
            r0 = pl.multiple_of(ci * R, R)
            xv = xbuf[slot, pl.ds(r0, R), :]
            h_ref[pl.ds(r0, R), :] = xv
            ms = jnp.mean(xv * xv, axis=-1, keepdims=True)
            hn_ref[pl.ds(r0, R), :] = (xv * lax.rsqrt(ms + EPS) * pw).astype(BF16)
            return carry
        lax.fori_loop(0, TM // R, chunk, 0, unroll=2)

    row = pl.BlockSpec((TM, D), lambda i: (i, 0))
    return pl.pallas_call(
        body, name="prenorm",
        grid=(NTILE,),
        in_specs=[pl.BlockSpec(memory_space=pl.ANY), pl.BlockSpec((NMETA, D), lambda i: (0, 0)),
                  pl.BlockSpec((1, D), lambda i: (0, 0))],
        out_specs=[row, row],
        out_shape=[jax.ShapeDtypeStruct((TP, D), F32), jax.ShapeDtypeStruct((TP, D), BF16)],
        scratch_shapes=[pltpu.VMEM((2, TM, D), F32), pltpu.SemaphoreType.DMA((2,))],
        compiler_params=_cparams(),
    )(x, meta_full, pre_w)


def _inproj_cols(name, shards, hn, w_land, b_in, z_prev):
    nsh = shards.shape[0]
    one_shard = w_land.shape[1] == _WCOLS

    def body(idx_ref, hn_ref, w_ref, b_ref, *rest):
        z_ref = rest[-2]
        z_ref[...] = jnp.dot(hn_ref[...], w_ref[...], preferred_element_type=F32) + b_ref[...]

    any_spec = pl.BlockSpec(memory_space=pl.ANY)
    in_specs = [pl.BlockSpec((TM, D), lambda j, i, idx: (i, 0)),
                pl.BlockSpec((D, _WCOLS), lambda j, i, idx: (0, 0 if one_shard else idx[j])),
                pl.BlockSpec((1, _WCOLS), lambda j, i, idx: (0, idx[j]))]
    operands = [hn, w_land, b_in]
    aliases = {2: 1}
    if z_prev is not None:
        in_specs.append(any_spec)
        operands.append(z_prev)
        aliases[4] = 0
    return pl.pallas_call(
        body, name=name,
        grid_spec=pltpu.PrefetchScalarGridSpec(
            num_scalar_prefetch=1, grid=(nsh, TP // TM), in_specs=in_specs,
            out_specs=[pl.BlockSpec((TM, _WCOLS), lambda j, i, idx: (i, idx[j])), any_spec]),
        out_shape=[jax.ShapeDtypeStruct((TP, NIN), F32), jax.ShapeDtypeStruct(w_land.shape, w_land.dtype)],
        input_output_aliases=aliases,
        compiler_params=_cparams(),
    )(shards, *operands)


def _gate_values(ga, gx, xc, sp8):
    r = _sig(ga)
    i = _sig(gx)
    log_a = -(r * sp8)
    a = jnp.exp(log_a)
    mult = jnp.sqrt(-_expm1_neg(2.0 * log_a))
    return r, i, a, mult


def _lru_fwd(z, conv_w, conv_b, wa_g, b_a, wx_g, b_x, lam):
    def body(x_ref, g_ref, cw_ref, cb_ref, wa_ref, ba_ref, wx_ref, bx_ref, lam_ref,
             y_ref, xc_ref, hs_ref, ga_s, gx_s):
        taps = [cw_ref[k:k + 1, :] for k in range(LW)]
        cb = cb_ref[...]

        def conv_chunk(ci, carry):
            r0 = pl.multiple_of(ci * R, R)
            cur = x_ref[pl.ds(r0, R), :]
            p0 = pl.multiple_of(jnp.maximum(r0 - 8, 0), 8)
            prev = jnp.where(ci > 0, x_ref[pl.ds(p0, 8), :], 0.0)
            buf = jnp.concatenate([prev, cur], axis=0)
            acc = cur * taps[LW - 1] + cb
            for s in range(1, LW):
                acc = acc + pltpu.roll(buf, s, 0)[8:8 + R, :] * taps[LW - 1 - s]
            xc_ref[pl.ds(r0, R), :] = acc
            return carry
        lax.fori_loop(0, TP // R, conv_chunk, 0)

        def gate_chunk(ci, carry):
            r0 = pl.multiple_of(ci * TM, TM)
            xb = xc_ref[pl.ds(r0, TM), :].astype(BF16)
            ga_s[pl.ds(r0, TM), :] = jnp.dot(xb, wa_ref[...], preferred_element_type=F32) + ba_ref[...]
            gx_s[pl.ds(r0, TM), :] = jnp.dot(xb, wx_ref[...], preferred_element_type=F32) + bx_ref[...]
            return carry
        lax.fori_loop(0, TP // TM, gate_chunk, 0)

        sp8 = LRU_C * _softplus(-lam_ref[...])
        row = _row_iota((R, CB))

        def scan_chunk(ci, hprev):
            r0 = pl.multiple_of(ci * R, R)
            xc = xc_ref[pl.ds(r0, R), :]
            _, i, a, mult = _gate_values(ga_s[pl.ds(r0, R), :], gx_s[pl.ds(r0, R), :], xc, sp8)
            u = mult * (i * xc)
            k = 1
            while k < R:
                m = row >= k
                u = jnp.where(m, a * pltpu.roll(u, k, 0) + u, u)
                a = jnp.where(m, a * pltpu.roll(a, k, 0), a)
                k *= 2
            hv = u + a * hprev
            hs_ref[pl.ds(r0, R), :] = hv
            g = g_ref[pl.ds(r0, R), :]
            y_ref[pl.ds(r0, R), :] = (hv * (g * _sig(g))).astype(BF16)
            return jnp.sum(jnp.where(row == R - 1, hv, 0.0), axis=0, keepdims=True)
        def scan_pass(i, hp):
            for sub in range(4):
                hp = scan_chunk(4 * i + sub, hp)
            return hp
        lax.fori_loop(0, TP // R // 4, scan_pass, jnp.zeros((1, CB), F32))

    col = lambda off: pl.BlockSpec((TP, CB), lambda j: (0, off + j))
    vec = pl.BlockSpec((1, CB), lambda j: (0, j))
    wsp = pl.BlockSpec((None, CB, CB), lambda j: (j, 0, 0))
    return pl.pallas_call(
        body, name="lru_fwd",
        grid=(NCB,),
        in_specs=[col(0), col(NCB), pl.BlockSpec((LW, CB), lambda j: (0, j)), vec, wsp, vec, wsp, vec, vec],
        out_specs=[col(0), col(0), col(0)],
        out_shape=[jax.ShapeDtypeStruct((TP, DL), BF16), jax.ShapeDtypeStruct((TP, DL), F32),
                   jax.ShapeDtypeStruct((TP, DL), F32)],
        scratch_shapes=[pltpu.VMEM((TP, CB), F32), pltpu.VMEM((TP, CB), F32)],
        compiler_params=_cparams(),
    )(z, z, conv_w, conv_b, wa_g, b_a, wx_g, b_x, lam)


CBC = 128
NCBC = DC // CBC
RC = 64


def _fold_rows(v):
    acc = v[0:8, :]
    for r in range(8, v.shape[0], 8):
        acc = acc + v[r:r + 8, :]
    return acc


def _conf_fwd_conv(z, dw_w, dw_b):
    def body(u1_ref, u2_ref, w_ref, b_ref, vc_ref, vs):
        vs[pl.ds(0, KWP), :] = jnp.zeros((KWP, CBC), F32)

        def glu_chunk(ci, carry):
            r0 = pl.multiple_of(ci * RC, RC)
            vs[pl.ds(KWP + r0, RC), :] = u1_ref[pl.ds(r0, RC), :] * _sig(u2_ref[pl.ds(r0, RC), :])
            return carry
        lax.fori_loop(0, TP // RC, glu_chunk, 0)

        bias = b_ref[...]

        def conv_chunk(ci, carry):
            r0 = pl.multiple_of(ci * RC, RC)
            buf = vs[pl.ds(r0, KWP + RC), :]
            acc = jnp.zeros((RC, CBC), F32) + bias
            for rr in range(8):
                rolled = buf if rr == 0 else pltpu.roll(buf, rr, 0)
                for q in range(4):
                    s = 8 * q + rr
                    if s > KW - 1:
                        continue
                    k = KW - 1 - s
                    acc = acc + rolled[KWP - 8 * q:KWP - 8 * q + RC, :] * w_ref[k:k + 1, :]
            vc_ref[pl.ds(r0, RC), :] = acc
            return carry
        lax.fori_loop(0, TP // RC, conv_chunk, 0)

    return pl.pallas_call(
        body, name="conf_fwd_conv",
        grid=(NCBC,),
        in_specs=[pl.BlockSpec((TP, CBC), lambda j: (0, 2 * NCBC + j)),
                  pl.BlockSpec((TP, CBC), lambda j: (0, 3 * NCBC + j)),
                  pl.BlockSpec((KWP, CBC), lambda j: (0, j)),
                  pl.BlockSpec((1, CBC), lambda j: (0, j))],
        out_specs=pl.BlockSpec((TP, CBC), lambda j: (0, j)),
        out_shape=jax.ShapeDtypeStruct((TP, DC), F32),
        scratch_shapes=[pltpu.VMEM((TP + KWP, CBC), F32)],
        compiler_params=_cparams(),
    )(z, z, dw_w, dw_b)


def _ln_chunk(vc, lw, lb):
    mu = jnp.mean(vc, axis=-1, keepdims=True)
    xm = vc - mu
    var = jnp.mean(xm * xm, axis=-1, keepdims=True)
    rstd = lax.rsqrt(var + EPS)
    xhat = xm * rstd
    return xhat, rstd, xhat * lw + lb


def _conf_fwd_proj(vc, z, ln_w, ln_b, pw_w, pw_b):
    def body(vc_ref, g_ref, lw_ref, lb_ref, w_ref, b_ref, y_ref, p_ref, xhat_ref, rstd_ref, s_s):
        lw, lb = lw_ref[...], lb_ref[...]

        def ln_chunk(ci, carry):
            r0 = pl.multiple_of(ci * R, R)
            for half in range(2):
                rr = r0 + 8 * half
                xhat, rstd, ln = _ln_chunk(vc_ref[pl.ds(rr, 8), :], lw, lb)
                xhat_ref[pl.ds(rr, 8), :] = xhat
                rstd_ref[pl.ds(rr, 8), :] = jnp.broadcast_to(rstd, (8, 128))
                p_ref[pl.ds(rr, 8), :] = ln * _sig(ln)
            s_s[pl.ds(r0, R), :] = p_ref[pl.ds(r0, R), :].astype(BF16)
            return carry
        lax.fori_loop(0, TM // R, ln_chunk, 0, unroll=2)

        p_ref[...] = jnp.dot(s_s[...], w_ref[...], preferred_element_type=F32) + b_ref[...]

        def out_chunk(ci, carry):
            r0 = pl.multiple_of(ci * R, R)
            g = g_ref[pl.ds(r0, R), :]
            y_ref[pl.ds(r0, R), :] = (p_ref[pl.ds(r0, R), :] * (g * _sig(g))).astype(BF16)
            return carry
        lax.fori_loop(0, TM // R, out_chunk, 0)

    row = pl.BlockSpec((TM, DC), lambda i: (i, 0))
    vec = pl.BlockSpec((1, DC), lambda i: (0, 0))
    return pl.pallas_call(
        body, name="conf_fwd_proj",
        grid=(TP // TM,),
        in_specs=[row, pl.BlockSpec((TM, DC), lambda i: (i, 4)), vec, vec,
                  pl.BlockSpec((DC, DC), lambda i: (0, 0)), vec],
        out_specs=[row, row, row, pl.BlockSpec((TM, 128), lambda i: (i, 0))],
        out_shape=[jax.ShapeDtypeStruct((TP, DC), BF16), jax.ShapeDtypeStruct((TP, DC), F32),
                   jax.ShapeDtypeStruct((TP, DC), F32), jax.ShapeDtypeStruct((TP, 128), F32)],
        scratch_shapes=[pltpu.VMEM((TM, DC), BF16)],
        compiler_params=_cparams(),
    )(vc, z, ln_w, ln_b, pw_w, pw_b)


def _outproj_loss(ylru, yconf, w_out, h, target, post_w):
    def body(yl_ref, yc_ref, w_ref, h_ref, tgt_hbm, pw_ref, dout_ref, dy_ref, loss_ref, dpw_ref, y_s, t_ref, sem):
        i = pl.program_id(0)
        k = pl.program_id(1)

        @pl.when(k == 0)
        def _():
            _for_tile(i, lambda t: _token_tile_copy(tgt_hbm, t_ref, sem, t).start())
            y_s[...] = jnp.dot(yl_ref[...], w_ref[...], preferred_element_type=F32)

        @pl.when(k == 1)
        def _():
            y_s[...] += jnp.dot(yc_ref[...], w_ref[...], preferred_element_type=F32)

        @pl.when(jnp.logical_and(i == 0, k == 1))
        def _():
            loss_ref[...] = jnp.zeros_like(loss_ref)
            dpw_ref[...] = jnp.zeros_like(dpw_ref)

        @pl.when(k == 1)
        def _():
            _for_tile(i, lambda t: _token_tile_copy(tgt_hbm, t_ref, sem, t).wait())

            @pl.when(i == 0)
            def _():
                t_ref[0:NMETA, :] = jnp.zeros((NMETA, D), F32)

            @pl.when(i == NTILE - 1)
            def _():
                last = _tile_rows(NTILE - 1)[1]
                t_ref[last:TM, :] = jnp.zeros((TM - last, D), F32)

            pw = pw_ref[...]
            row = _row_iota((8, D))

            def chunk(ci, carry):
                r0 = pl.multiple_of(ci * 8, 8)
                yv = y_s[pl.ds(r0, 8), :]
                rs = lax.rsqrt(jnp.mean(yv * yv, axis=-1, keepdims=True) + EPS)
                grow = row + (i * TM + r0)
                valid = jnp.logical_and(grow >= NMETA, grow < T)
                yn = yv * rs
                err = jnp.where(valid, h_ref[pl.ds(r0, 8), :] + yn * pw - t_ref[pl.ds(r0, 8), :], 0.0)
                loss_ref[...] += err * err
                d_rn = err * (1.0 / D)
                dout_ref[pl.ds(r0, 8), :] = d_rn
                dpw_ref[...] += d_rn * yn
                gw = d_rn * pw
                dot = jnp.mean(gw * yv, axis=-1, keepdims=True)
                dy_ref[pl.ds(r0, 8), :] = (rs * gw - yv * (rs * rs * rs * dot)).astype(BF16)
                return carry
            lax.fori_loop(0, TM // 8, chunk, 0, unroll=4)

    row = pl.BlockSpec((TM, D), lambda i, k: (i, 0))
    half = pl.BlockSpec((TM, DL), lambda i, k: (i, 0))
    acc = pl.BlockSpec((8, D), lambda i, k: (0, 0))
    return pl.pallas_call(
        body, name="outproj_loss",
        grid=(TP // TM, 2),
        in_specs=[half, half, pl.BlockSpec((DL, D), lambda i, k: (k, 0)), row, pl.BlockSpec(memory_space=pl.ANY),
                  pl.BlockSpec((1, D), lambda i, k: (0, 0))],
        out_specs=[row, row, acc, acc],
        out_shape=[jax.ShapeDtypeStruct((TP, D), F32), jax.ShapeDtypeStruct((TP, D), BF16),
                   jax.ShapeDtypeStruct((8, D), F32), jax.ShapeDtypeStruct((8, D), F32)],
        scratch_shapes=[pltpu.VMEM((TM, D), F32), pltpu.VMEM((TM, D), F32), pltpu.SemaphoreType.DMA(())],
        compiler_params=_cparams(),
    )(ylru, yconf, w_out, h, target, post_w)


_NT = (((1,), (1,)), ((), ()))
_TN = (((0,), (0,)), ((), ()))


def _outproj_bwd(dy, ylru, yconf, w_out):
    def body(dy_ref, yl_ref, yc_ref, w_ref, dycat_ref, dw_ref):
        j = pl.program_id(0)
        dyv = dy_ref[...]
        dycat_ref[...] = lax.dot_general(dyv, w_ref[...], _NT, preferred_element_type=F32)

        @pl.when(j < NCB)
        def _():
            dw_ref[...] = lax.dot_general(yl_ref[...], dyv, _TN, preferred_element_type=F32).astype(BF16)

        @pl.when(j >= NCB)
        def _():
            dw_ref[...] = lax.dot_general(yc_ref[...], dyv, _TN, preferred_element_type=F32).astype(BF16)

    return pl.pallas_call(
        body, name="outproj_bwd",
        grid=(2 * NCB,),
        in_specs=[pl.BlockSpec((TP, D), lambda j: (0, 0)),
                  pl.BlockSpec((TP, CB), lambda j: (0, jnp.minimum(j, NCB - 1))),
                  pl.BlockSpec((TP, CB), lambda j: (0, jnp.maximum(j - NCB, 0))),
                  pl.BlockSpec((CB, D), lambda j: (j, 0))],
        out_specs=[pl.BlockSpec((TP, CB), lambda j: (0, j)), pl.BlockSpec((CB, D), lambda j: (j, 0))],
        out_shape=[jax.ShapeDtypeStruct((TP, D), F32), jax.ShapeDtypeStruct((D, D), BF16)],
        compiler_params=_cparams(),
    )(dy, ylru, yconf, w_out)


_AFTER = pl.BlockSpec(memory_space=pl.ANY)


def _conf_bwd_proj(dycat, p, z, xhat, rstd, hs, ln_w, ln_b, pw_w, after):
    def body(dy_ref, p_ref, g_ref, xhat_ref, rstd_ref, dyl_ref, hs_ref, gl_ref, lw_ref, lb_ref, w_ref, after_ref,
             dvc_ref, dz_ref, dpw_ref, vecs_ref, dp_s, s_s, ds_s):
        i = pl.program_id(0)
        lw, lb = lw_ref[...], lb_ref[...]

        @pl.when(i == 0)
        def _():
            dpw_ref[...] = jnp.zeros_like(dpw_ref)
            vecs_ref[...] = jnp.zeros_like(vecs_ref)

        def pre_chunk(ci, carry):
            r0 = pl.multiple_of(ci * R, R)
            for half in range(2):
                rr = r0 + 8 * half
                dyv = dy_ref[pl.ds(rr, 8), :]
                g = g_ref[pl.ds(rr, 8), :]
                sg = _sig(g)
                dp = dyv * (g * sg)
                dg = dyv * p_ref[pl.ds(rr, 8), :] * (sg * (1.0 + g * (1.0 - sg)))
                vecs_ref[0:8, :] += dp
                vecs_ref[8:16, :] += dg
                ds_s[pl.ds(rr, 8), :] = dp
                dvc_ref[pl.ds(rr, 8), :] = dg
            dp_s[pl.ds(r0, R), :] = ds_s[pl.ds(r0, R), :].astype(BF16)
            dz_ref[0, pl.ds(r0, R), :] = dvc_ref[pl.ds(r0, R), :].astype(BF16)
            for half in range(2):
                rr = r0 + 8 * half
                gl = gl_ref[pl.ds(rr, 8), :]
                sgl = _sig(gl)
                dgl = dyl_ref[pl.ds(rr, 8), :] * hs_ref[pl.ds(rr, 8), :] * (sgl * (1.0 + gl * (1.0 - sgl)))
                vecs_ref[32:40, :] += dgl
                dvc_ref[pl.ds(rr, 8), :] = dgl
            dz_ref[1, pl.ds(r0, R), :] = dvc_ref[pl.ds(r0, R), :].astype(BF16)
            for half in range(2):
                rr = r0 + 8 * half
                ln = xhat_ref[pl.ds(rr, 8), :] * lw + lb
                ds_s[pl.ds(rr, 8), :] = ln * _sig(ln)
            s_s[pl.ds(r0, R), :] = ds_s[pl.ds(r0, R), :].astype(BF16)
            return carry
        lax.fori_loop(0, TM // R, pre_chunk, 0, unroll=2)

        dpb = dp_s[...]
        ds_s[...] = lax.dot_general(dpb, w_ref[...], _NT, preferred_element_type=F32)
        dpw_ref[...] += lax.dot_general(s_s[...], dpb, _TN, preferred_element_type=F32)

        def post_chunk(ci, carry):
            r0 = pl.multiple_of(ci * 8, 8)
            xhat = xhat_ref[pl.ds(r0, 8), :]
            rstd = jnp.tile(rstd_ref[pl.ds(r0, 8), :], (1, DC // 128))
            ln = xhat * lw + lb
            sl = _sig(ln)
            dln = ds_s[pl.ds(r0, 8), :] * (sl * (1.0 + ln * (1.0 - sl)))
            vecs_ref[16:24, :] += dln * xhat
            vecs_ref[24:32, :] += dln
            dxh = dln * lw
            m1 = jnp.mean(dxh, axis=-1, keepdims=True)
            m2 = jnp.mean(dxh * xhat, axis=-1, keepdims=True)
            dvc_ref[pl.ds(r0, 8), :] = rstd * (dxh - m1 - xhat * m2)
            return carry
        lax.fori_loop(0, TM // 8, post_chunk, 0, unroll=4)

    row = pl.BlockSpec((TM, DC), lambda i: (i, 0))
    vec = pl.BlockSpec((1, DC), lambda i: (0, 0))
    return pl.pallas_call(
        body, name="conf_bwd_proj",
        grid=(TP // TM,),
        in_specs=[pl.BlockSpec((TM, DC), lambda i: (i, 1)), row, pl.BlockSpec((TM, DC), lambda i: (i, 4)), row,
                  pl.BlockSpec((TM, 128), lambda i: (i, 0)),
                  pl.BlockSpec((TM, DL), lambda i: (i, 0)), row, pl.BlockSpec((TM, DL), lambda i: (i, 1)),
                  vec, vec, pl.BlockSpec((DC, DC), lambda i: (0, 0)), _AFTER],
        out_specs=[row, pl.BlockSpec((2, TM, DC), lambda i: (0, i, 0)), pl.BlockSpec((DC, DC), lambda i: (0, 0)),
                   pl.BlockSpec((40, DC), lambda i: (0, 0))],
        out_shape=[jax.ShapeDtypeStruct((TP, DC), F32), jax.ShapeDtypeStruct((2, TP, DC), BF16),
                   jax.ShapeDtypeStruct((DC, DC), F32), jax.ShapeDtypeStruct((40, DC), F32)],
        scratch_shapes=[pltpu.VMEM((TM, DC), BF16), pltpu.VMEM((TM, DC), BF16), pltpu.VMEM((TM, DC), F32)],
        compiler_params=_cparams(),
    )(dycat, p, z, xhat, rstd, dycat, hs, z, ln_w, ln_b, pw_w, after)


def _conf_bwd_conv(dvc, z, dw_w, after):
    def body(dvc_ref, u1_ref, u2_ref, w_ref, after_ref, du_ref, dw_ref, vecs_ref, vs, dvs):
        vs[pl.ds(0, KWP), :] = jnp.zeros((KWP, CBC), F32)
        dvs[pl.ds(TP, KWP), :] = jnp.zeros((KWP, CBC), F32)
        dw_ref[...] = jnp.zeros_like(dw_ref)
        vecs_ref[...] = jnp.zeros_like(vecs_ref)

        def fill_chunk(ci, carry):
            r0 = pl.multiple_of(ci * RC, RC)
            vs[pl.ds(KWP + r0, RC), :] = u1_ref[pl.ds(r0, RC), :] * _sig(u2_ref[pl.ds(r0, RC), :])
            dv = dvc_ref[pl.ds(r0, RC), :]
            dvs[pl.ds(r0, RC), :] = dv
            vecs_ref[0:8, :] += _fold_rows(dv)
            return carry
        lax.fori_loop(0, TP // RC, fill_chunk, 0)

        def conv_chunk(ci, carry):
            r0 = pl.multiple_of(ci * RC, RC)
            vbuf = vs[pl.ds(r0, KWP + RC), :]
            dbuf = dvs[pl.ds(r0, KWP + RC), :]
            dcur = dbuf[0:RC, :]
            dv = jnp.zeros((RC, CBC), F32)
            for rr in range(8):
                vroll = vbuf if rr == 0 else pltpu.roll(vbuf, rr, 0)
                droll = dbuf if rr == 0 else pltpu.roll(dbuf, KWP + RC - rr, 0)
                for q in range(4):
                    s = 8 * q + rr
                    if s > KW - 1:
                        continue
                    k = KW - 1 - s
                    dv = dv + droll[8 * q:8 * q + RC, :] * w_ref[k:k + 1, :]
                    dw_ref[8 * k:8 * k + 8, :] += _fold_rows(dcur * vroll[KWP - 8 * q:KWP - 8 * q + RC, :])
            u1 = u1_ref[pl.ds(r0, RC), :]
            sg = _sig(u2_ref[pl.ds(r0, RC), :])
            du1 = dv * sg
            du2 = dv * u1 * (sg * (1.0 - sg))
            du_ref[0, pl.ds(r0, RC), :] = du1.astype(BF16)
            du_ref[1, pl.ds(r0, RC), :] = du2.astype(BF16)
            vecs_ref[8:16, :] += _fold_rows(du1)
            vecs_ref[16:24, :] += _fold_rows(du2)
            return carry
        lax.fori_loop(0, TP // RC, conv_chunk, 0)

    blk = pl.BlockSpec((TP, CBC), lambda j: (0, j))
    return pl.pallas_call(
        body, name="conf_bwd_conv",
        grid=(NCBC,),
        in_specs=[blk, pl.BlockSpec((TP, CBC), lambda j: (0, 2 * NCBC + j)),
                  pl.BlockSpec((TP, CBC), lambda j: (0, 3 * NCBC + j)), pl.BlockSpec((KWP, CBC), lambda j: (0, j)),
                  _AFTER],
        out_specs=[pl.BlockSpec((2, TP, CBC), lambda j: (0, 0, j)), pl.BlockSpec((8 * KWP, CBC), lambda j: (0, j)),
                   pl.BlockSpec((24, CBC), lambda j: (0, j))],
        out_shape=[jax.ShapeDtypeStruct((2, TP, DC), BF16),
                   jax.ShapeDtypeStruct((8 * KWP, DC), F32), jax.ShapeDtypeStruct((24, DC), F32)],
        scratch_shapes=[pltpu.VMEM((TP + KWP, CBC), F32), pltpu.VMEM((TP + KWP, CBC), F32)],
        compiler_params=_cparams(),
    )(dvc, z, z, dw_w, after)


def _lru_bwd(dycat, z, xc, hs, conv_w, wa_g, b_a, wx_g, b_x, lam, after):
    NV = 6

    def body(dy_ref, x_ref, g_ref, xc_ref, hs_ref, cw_ref, wa_ref, ba_ref, wx_ref, bx_ref, lam_ref, after_ref,
             dzl_ref, dwa_ref, dwx_ref, dcw_ref, vecs_ref, ga_s, gx_s, dxc_s):
        vecs_ref[...] = jnp.zeros_like(vecs_ref)
        dcw_ref[...] = jnp.zeros_like(dcw_ref)
        dxc_s[pl.ds(TP, 8), :] = jnp.zeros((8, CB), F32)

        def gate_chunk(ci, carry):
            r0 = pl.multiple_of(ci * TM, TM)
            xb = xc_ref[pl.ds(r0, TM), :].astype(BF16)
            ga_s[pl.ds(r0, TM), :] = jnp.dot(xb, wa_ref[...], preferred_element_type=F32) + ba_ref[...]
            gx_s[pl.ds(r0, TM), :] = jnp.dot(xb, wx_ref[...], preferred_element_type=F32) + bx_ref[...]
            return carry
        lax.fori_loop(0, TP // TM, gate_chunk, 0)

        sp8 = LRU_C * _softplus(-lam_ref[...])
        row = _row_iota((R, CB))
        nchunk = TP // R

        def scan_chunk(cj, carry):
            a_next, lam_next = carry
            ci = nchunk - 1 - cj
            r0 = pl.multiple_of(ci * R, R)
            dyv = dy_ref[pl.ds(r0, R), :]
            g = g_ref[pl.ds(r0, R), :]
            hv = hs_ref[pl.ds(r0, R), :]
            xc = xc_ref[pl.ds(r0, R), :]
            sg = _sig(g)
            dhs = dyv * (g * sg)
            r, i, a, mult = _gate_values(ga_s[pl.ds(r0, R), :], gx_s[pl.ds(r0, R), :], xc, sp8)
            b = jnp.where(row == R - 1, a_next, pltpu.roll(a, R - 1, 0))
            lv = dhs
            k = 1
            while k < R:
                m = row < R - k
                lv = jnp.where(m, lv + b * pltpu.roll(lv, R - k, 0), lv)
                b = jnp.where(m, b * pltpu.roll(b, R - k, 0), b)
                k *= 2
            lv = lv + b * lam_next
            p0 = pl.multiple_of(jnp.maximum(r0 - 8, 0), 8)
            hprev8 = jnp.where(ci > 0, hs_ref[pl.ds(p0, 8), :], 0.0)
            hprev = pltpu.roll(jnp.concatenate([hprev8, hv], axis=0), 1, 0)[8:8 + R, :]
            da = lv * hprev
            ixc = i * xc
            dmult = lv * ixc
            di = lv * mult * xc
            dxc_s[pl.ds(r0, R), :] = lv * mult * i
            a2 = a * a
            dlog_a = da * a - dmult * a2 / mult
            vecs_ref[32:40, :] += _fold8(dlog_a * r)
            dga = -(dlog_a * sp8) * r * (1.0 - r)
            dgx = di * i * (1.0 - i)
            ga_s[pl.ds(r0, R), :] = dga
            gx_s[pl.ds(r0, R), :] = dgx
            vecs_ref[16:24, :] += _fold8(dga)
            vecs_ref[24:32, :] += _fold8(dgx)
            a_first = jnp.sum(jnp.where(row == 0, a, 0.0), axis=0, keepdims=True)
            l_first = jnp.sum(jnp.where(row == 0, lv, 0.0), axis=0, keepdims=True)
            return a_first, l_first
        lax.fori_loop(0, nchunk // 2, lambda i, cr: scan_chunk(2 * i + 1, scan_chunk(2 * i, cr)),
                      (jnp.zeros((1, CB), F32), jnp.zeros((1, CB), F32)))

        dwa_ref[...] = jnp.zeros_like(dwa_ref)
        dwx_ref[...] = jnp.zeros_like(dwx_ref)

        def mm_chunk(ci, carry):
            r0 = pl.multiple_of(ci * TM, TM)
            xb = xc_ref[pl.ds(r0, TM), :].astype(BF16)
            dgab = ga_s[pl.ds(r0, TM), :].astype(BF16)
            dgxb = gx_s[pl.ds(r0, TM), :].astype(BF16)
            dxc_s[pl.ds(r0, TM), :] += (lax.dot_general(dgab, wa_ref[...], _NT, preferred_element_type=F32)
                                        + lax.dot_general(dgxb, wx_ref[...], _NT, preferred_element_type=F32))
            dwa_ref[...] += lax.dot_general(xb, dgab, _TN, preferred_element_type=F32)
            dwx_ref[...] += lax.dot_general(xb, dgxb, _TN, preferred_element_type=F32)
            return carry
        lax.fori_loop(0, TP // TM, mm_chunk, 0)

        taps = [cw_ref[k:k + 1, :] for k in range(LW)]

        def conv_chunk(ci, carry):
            r0 = pl.multiple_of(ci * R, R)
            dbuf = dxc_s[pl.ds(r0, R + 8), :]
            dcur = dbuf[0:R, :]
            p0 = pl.multiple_of(jnp.maximum(r0 - 8, 0), 8)
            xprev = jnp.where(ci > 0, x_ref[pl.ds(p0, 8), :], 0.0)
            xbuf = jnp.concatenate([xprev, x_ref[pl.ds(r0, R), :]], axis=0)
            dxl = dcur * taps[LW - 1]
            dcw_ref[8 * (LW - 1):8 * LW, :] += _fold8(dcur * xbuf[8:8 + R, :])
            for s in range(1, LW):
                k = LW - 1 - s
                dxl = dxl + pltpu.roll(dbuf, R + 8 - s, 0)[0:R, :] * taps[k]
                dcw_ref[8 * k:8 * k + 8, :] += _fold8(dcur * pltpu.roll(xbuf, s, 0)[8:8 + R, :])
            dzl_ref[0, pl.ds(r0, R), :] = dxl.astype(BF16)
            vecs_ref[8:16, :] += _fold8(dxl)
            vecs_ref[40:48, :] += _fold8(dcur)
            return carry
        lax.fori_loop(0, TP // R, conv_chunk, 0)
        vecs_ref[32:40, :] = vecs_ref[32:40, :] * (LRU_C * _sig(-lam_ref[...]))

    col = lambda off: pl.BlockSpec((TP, CB), lambda j: (0, off + j))
    vec = pl.BlockSpec((1, CB), lambda j: (0, j))
    wsp = pl.BlockSpec((None, CB, CB), lambda j: (j, 0, 0))
    return pl.pallas_call(
        body, name="lru_bwd",
        grid=(NCB,),
        in_specs=[col(0), col(0), col(NCB), col(0), col(0), pl.BlockSpec((LW, CB), lambda j: (0, j)),
                  wsp, vec, wsp, vec, vec, _AFTER],
        out_specs=[pl.BlockSpec((1, TP, CB), lambda j: (0, 0, j)), wsp, wsp,
                   pl.BlockSpec((8 * LW, CB), lambda j: (0, j)), pl.BlockSpec((8 * NV, CB), lambda j: (0, j))],
        out_shape=[jax.ShapeDtypeStruct((1, TP, DL), BF16),
                   jax.ShapeDtypeStruct((NCB, CB, CB), F32), jax.ShapeDtypeStruct((NCB, CB, CB), F32),
                   jax.ShapeDtypeStruct((8 * LW, DL), F32), jax.ShapeDtypeStruct((8 * NV, DL), F32)],
        scratch_shapes=[pltpu.VMEM((TP, CB), F32), pltpu.VMEM((TP, CB), F32), pltpu.VMEM((TP + 8, CB), F32)],
        compiler_params=_cparams(),
    )(dycat, z, z, xc, hs, conv_w, wa_g, b_a, wx_g, b_x, lam, after)


def _dz_section(sec, dzl_ref, dz41_ref, dzc_ref, use):
    @pl.when(sec == 0)
    def _():
        use(dzl_ref)

    @pl.when(jnp.logical_or(sec == 1, sec == 4))
    def _():
        use(dz41_ref)

    @pl.when(jnp.logical_or(sec == 2, sec == 3))
    def _():
        use(dzc_ref)


def _dz_specs(rows, index):
    return [pl.BlockSpec((None, rows, 1024), lambda a, b: (0, index(a, b)[0], 0)),
            pl.BlockSpec((None, rows, 1024), lambda a, b: (jnp.where(index(a, b)[1] == 1, 1, 0), index(a, b)[0], 0)),
            pl.BlockSpec((None, rows, 1024), lambda a, b: (jnp.clip(index(a, b)[1] - 2, 0, 1), index(a, b)[0], 0))]


def _inproj_wgrad(name, hn, dzs, after):
    KB = 512
    nsec = dzs.shape[0]

    def body(hn_ref, dz_ref, after_ref, dw_ref):
        dw_ref[...] = lax.dot_general(hn_ref[...], dz_ref[...], _TN, preferred_element_type=F32).astype(BF16)

    return pl.pallas_call(
        body, name=name,
        grid=(nsec, D // KB),
        in_specs=[pl.BlockSpec((TP, KB), lambda n, kb: (0, kb)),
                  pl.BlockSpec((None, TP, 1024), lambda n, kb: (n, 0, 0)), _AFTER],
        out_specs=pl.BlockSpec((KB, 1024), lambda n, kb: (kb, n)),
        out_shape=jax.ShapeDtypeStruct((D, nsec * 1024), BF16),
        compiler_params=_cparams(),
    )(hn, dzs, after)


def _sum_win_parts(parts_a, parts_b, parts_c):
    RB = 64

    def body(a_ref, b_ref, c_ref, o_ref):
        def chunk(ci, carry):
            r0 = pl.multiple_of(ci * R, R)
            for ref, src, base, ncol in ((a_ref, 0, 0, 1024), (c_ref, 1024, 1024, 1024), (b_ref, 0, 2048, 2048),
                                         (c_ref, 0, 4096, 1024)):
                for c0 in range(0, ncol, 512):
                    acc = ref[0, pl.ds(r0, R), src + c0:src + c0 + 512].astype(F32)
                    for sidx in range(1, NDEV):
                        acc = acc + ref[sidx, pl.ds(r0, R), src + c0:src + c0 + 512].astype(F32)
                    o_ref[pl.ds(r0, R), base + c0:base + c0 + 512] = acc.astype(BF16)
            return carry
        lax.fori_loop(0, RB // R, chunk, 0)

    spec = lambda ncol: pl.BlockSpec((NDEV, RB, ncol), lambda i: (0, i, 0))
    return pl.pallas_call(
        body, name="sum_win_parts",
        grid=(D // NDEV // RB,),
        in_specs=[spec(1024), spec(2048), spec(2048)],
        out_specs=pl.BlockSpec((RB, NIN), lambda i: (i, 0)),
        out_shape=jax.ShapeDtypeStruct((D // NDEV, NIN), BF16),
        compiler_params=_cparams(),
    )(parts_a, parts_b, parts_c)


def _inproj_bwd(dzl, dz41, dzc, w_in, h, dout, pre_w, after):
    nsec = NIN // 1024

    def body(dzl_ref, dz41_ref, dzc_ref, w_ref, h_ref, dout_ref, pw_ref, after_ref, gx_hbm, dmeta_ref, dpw_ref,
             acc_s, dh_s, sem):
        i = pl.program_id(0)
        s = pl.program_id(1)

        def gx_copy(t):
            lo, n, off = _tile_rows(t)
            return pltpu.make_async_copy(dh_s.at[pl.ds(off, n)], gx_hbm.at[pl.ds(lo, n)], sem)

        @pl.when(s == 0)
        def _():
            acc_s[...] = jnp.zeros_like(acc_s)

        def use(dz_ref):
            acc_s[...] += lax.dot_general(dz_ref[...], w_ref[...], _NT, preferred_element_type=F32)
        _dz_section(s, dzl_ref, dz41_ref, dzc_ref, use)

        @pl.when(jnp.logical_and(i == 0, s == nsec - 1))
        def _():
            dpw_ref[...] = jnp.zeros_like(dpw_ref)

        @pl.when(s == nsec - 1)
        def _():
            _for_tile(i - 1, lambda t: gx_copy(t).wait())
            pw = pw_ref[...]

            def chunk(ci, carry):
                r0 = pl.multiple_of(ci * 8, 8)
                hv = h_ref[pl.ds(r0, 8), :]
                dhn = acc_s[pl.ds(r0, 8), :]
                rs = lax.rsqrt(jnp.mean(hv * hv, axis=-1, keepdims=True) + EPS)
                dpw_ref[...] += dhn * (hv * rs)
                gw = dhn * pw
                dot = jnp.mean(gw * hv, axis=-1, keepdims=True)
                dh_s[pl.ds(r0, 8), :] = rs * gw - hv * (rs * rs * rs * dot) + dout_ref[pl.ds(r0, 8), :]
                return carry
            lax.fori_loop(0, TM // 8, chunk, 0, unroll=4)
            _for_tile(i, lambda t: gx_copy(t).start())

            @pl.when(i == 0)
            def _():
                dmeta_ref[...] = dh_s[0:NMETA, :]

            @pl.when(i == NTILE - 1)
            def _():
                gx_copy(NTILE - 1).wait()

    row = pl.BlockSpec((TM, D), lambda i, s: (i, 0))
    return pl.pallas_call(
        body, name="inproj_bwd",
        grid=(TP // TM, nsec),
        in_specs=_dz_specs(TM, lambda i, s: (i, s)) + [
            pl.BlockSpec((D, 1024), lambda i, s: (0, s)), row, row, pl.BlockSpec((1, D), lambda i, s: (0, 0)),
            _AFTER],
        out_specs=[pl.BlockSpec(memory_space=pl.ANY), pl.BlockSpec((NMETA, D), lambda i, s: (0, 0)),
                   pl.BlockSpec((8, D), lambda i, s: (0, 0))],
        out_shape=[jax.ShapeDtypeStruct((SEQ, D), F32), jax.ShapeDtypeStruct((NMETA, D), F32),
                   jax.ShapeDtypeStruct((8, D), F32)],
        scratch_shapes=[pltpu.VMEM((TM, D), F32), pltpu.VMEM((TM, D), F32), pltpu.SemaphoreType.DMA(())],
        compiler_params=_cparams(),
    )(dzl, dz41, dzc, w_in, h, dout, pre_w, after)


def _adamw(name, parts, w, m, v, block_rows):
    rows, cols = w.shape
    nparts = parts.shape[0]
    cw = cols if cols <= 640 else 512

    def body(p_ref, w_ref, m_ref, v_ref, g_ref, d_ref, nm_ref, nv_ref):
        def chunk(ci, carry):
            r0 = pl.multiple_of(ci * R, R)
            for c0 in range(0, cols, cw):
                at = (pl.ds(r0, R), slice(c0, c0 + cw))
                g = p_ref[(0,) + at].astype(F32)
                for sidx in range(1, nparts):
                    g = g + p_ref[(sidx,) + at].astype(F32)
                delta, mv, vv = _adam_math(g, w_ref[at], m_ref[at], v_ref[at])
                g_ref[at] = g
                nm_ref[at] = mv
                nv_ref[at] = vv
                d_ref[at] = delta
            return carry
        lax.fori_loop(0, block_rows // R, chunk, 0)

    blk = pl.BlockSpec((block_rows, cols), lambda i: (i, 0))
    shp = jax.ShapeDtypeStruct((rows, cols), F32)
    return pl.pallas_call(
        body, name=name,
        grid=(rows // block_rows,),
        in_specs=[pl.BlockSpec((nparts, block_rows, cols), lambda i: (0, i, 0)), blk, blk, blk],
        out_specs=[blk, blk, blk, blk],
        out_shape=[shp, shp, shp, shp],
        compiler_params=_cparams(),
    )(parts, w, m, v)


def _adam_math(g, w, m, v):
    c1 = 1.0 / (1.0 - ADAM_B1 ** ADAM_STEP)
    c2 = 1.0 / (1.0 - ADAM_B2 ** ADAM_STEP)
    mv = ADAM_B1 * m + (1.0 - ADAM_B1) * g
    vv = ADAM_B2 * v + (1.0 - ADAM_B2) * (g * g)
    upd = (mv * c1) / (jnp.sqrt(vv * c2) + ADAM_EPS) + ADAM_WD * w
    return -ADAM_LR * upd, mv, vv


_VEC = [("pre_norm_w", 2), ("post_norm_w", 2), ("b_in", 5), ("lru_conv_b", 1), ("b_gate_a", 1), ("b_gate_x", 1),
        ("lru_lambda", 1), ("conf_dw_b", 1), ("conf_ln_w", 1), ("conf_ln_b", 1), ("conf_pw_b", 1)]
_VEC_ROWS = 24
_LOSS_ROW = 17
_SM_ROWS = 64


def _pack_grads(dprew_acc, dpostw_acc, cvecs, kvecs, lvecs, dcw_acc, ddw_acc, dh, loss_acc):
    def body(pre_ref, post_ref, c_ref, k_ref, l_ref, dcw_ref, ddw_ref, dh_ref, loss_ref, vec_ref, small_ref, tmp):
        s8 = lambda ref, r: jnp.sum(ref[8 * r:8 * r + 8, :], axis=0, keepdims=True)
        vec_ref[...] = jnp.zeros_like(vec_ref)
        pre, post = s8(pre_ref, 0), s8(post_ref, 0)
        rows = [pre[:, 0:1024], pre[:, 1024:2048], post[:, 0:1024], post[:, 1024:2048],
                s8(l_ref, 1), s8(c_ref, 4), s8(k_ref, 1), s8(k_ref, 2), s8(c_ref, 1),
                s8(l_ref, 5), s8(l_ref, 2), s8(l_ref, 3), s8(l_ref, 4),
                s8(k_ref, 0), s8(c_ref, 2), s8(c_ref, 3), s8(c_ref, 0)]
        for r, val in enumerate(rows):
            vec_ref[r:r + 1, :] = val
        vec_ref[_LOSS_ROW:_LOSS_ROW + 1, :] = jnp.zeros((1, 1024), F32) + (0.5 / D) * jnp.sum(loss_ref[...])

        small_ref[...] = jnp.zeros_like(small_ref)
        for k in range(LW):
            tmp[k:k + 1, :] = s8(dcw_ref, k)
        for k in range(KW):
            tmp[8 + k:9 + k, :] = s8(ddw_ref, k)
        for d in range(NDEV):
            small_ref[d, 0:LW, 0:128] = tmp[0:LW, 128 * d:128 * d + 128]
            small_ref[d, 8:8 + KW, 0:128] = tmp[8:8 + KW, 128 * d:128 * d + 128]
            small_ref[d, 40:56, :] = dh_ref[:, 256 * d:256 * d + 256]

    full = lambda a: pl.BlockSpec(a.shape, lambda i: (0,) * a.ndim)
    ins = [dprew_acc, dpostw_acc, cvecs, kvecs, lvecs, dcw_acc, ddw_acc]
    return pl.pallas_call(
        body, name="pack_grads",
        grid=(1,),
        in_specs=[full(a) for a in ins] + [full(dh), full(loss_acc)],
        out_specs=[pl.BlockSpec((_VEC_ROWS, 1024), lambda i: (0, 0)),
                   pl.BlockSpec((NDEV, _SM_ROWS, 256), lambda i: (0, 0, 0))],
        out_shape=[jax.ShapeDtypeStruct((_VEC_ROWS, 1024), F32), jax.ShapeDtypeStruct((NDEV, _SM_ROWS, 256), F32)],
        scratch_shapes=[pltpu.VMEM((40, 1024), F32)],
        compiler_params=_cparams(),
    )(*ins, dh, loss_acc)


def _adamw_vec(parts, W, M, V):
    nv = len(_VEC)

    def body(*refs):
        p_ref = refs[0]
        w_refs, m_refs, v_refs = refs[1:1 + nv], refs[1 + nv:1 + 2 * nv], refs[1 + 2 * nv:1 + 3 * nv]
        outs = refs[1 + 3 * nv:]

        def total(r):
            acc = p_ref[0, r:r + 1, :]
            for sidx in range(1, NDEV):
                acc = acc + p_ref[sidx, r:r + 1, :]
            return acc

        row = 0
        for idx, (_, nrows) in enumerate(_VEC):
            for part in range(nrows):
                cols = slice(1024 * part, 1024 * part + 1024)
                g = total(row + part)
                delta, mv, vv = _adam_math(g, w_refs[idx][:, cols], m_refs[idx][:, cols], v_refs[idx][:, cols])
                for o, val in zip(outs[4 * idx:4 * idx + 4], (g, delta, mv, vv)):
                    o[:, cols] = val
            row += nrows
        outs[-1][...] = total(_LOSS_ROW)[:, 0:128]

    names = [n for n, _ in _VEC]
    flat = lambda d: [d[n].reshape(1, -1) for n in names]
    ws, ms, vs = flat(W), flat(M), flat(V)
    res = pl.pallas_call(
        body, name="adamw_vec",
        out_shape=[jax.ShapeDtypeStruct(w.shape, F32) for w in ws for _ in range(4)]
        + [jax.ShapeDtypeStruct((1, 128), F32)],
        compiler_params=_cparams(),
    )(parts, *ws, *ms, *vs)
    return {n: tuple(res[4 * i:4 * i + 4]) for i, n in enumerate(names)}, res[-1]


def _adamw_small(parts, W, M, V):
    where = {"lru_conv_w": (slice(0, LW), slice(0, 128)), "conf_dw_w": (slice(8, 8 + KW), slice(0, 128)),
             "meta_tokens": (slice(40, 56), slice(0, 256))}
    names = list(where)

    def body(*refs):
        p_ref = refs[0]
        outs = refs[10:]
        for idx, n in enumerate(names):
            rs, cs = where[n]
            g = p_ref[0, rs, cs]
            for sidx in range(1, NDEV):
                g = g + p_ref[sidx, rs, cs]
            delta, mv, vv = _adam_math(g, refs[1 + idx][...], refs[4 + idx][...], refs[7 + idx][...])
            for o, val in zip(outs[4 * idx:4 * idx + 4], (g, delta, mv, vv)):
                o[...] = val

    two_d = lambda a: a.reshape(a.shape[-2:])
    ws, ms, vs = ([two_d(d[n]) for n in names] for d in (W, M, V))
    res = pl.pallas_call(
        body, name="adamw_small",
        out_shape=[jax.ShapeDtypeStruct(w.shape, F32) for w in ws for _ in range(4)],
        compiler_params=_cparams(),
    )(parts, *ws, *ms, *vs)
    return {n: tuple(res[4 * i:4 * i + 4]) for i, n in enumerate(names)}


def _pack_small(lru_cw, dw_w, meta):
    buf = jnp.zeros((_SM_ROWS, 256), F32)
    buf = buf.at[0:LW, 0:128].set(lru_cw)
    buf = buf.at[8:8 + dw_w.shape[0], 0:128].set(dw_w)
    return buf.at[40:56, :].set(meta)


def _block_diag4(w):
    w4 = w.reshape(NCB, 4, 64, 64)
    eye = jnp.eye(4, dtype=w.dtype)
    return jnp.einsum("ghij,hk->ghikj", w4, eye).reshape(NCB, CB, CB)


def _diag_blocks(g):
    g5 = g.reshape(NCB, 4, 64, 4, 64)
    return jnp.stack([g5[:, hh, :, hh, :] for hh in range(4)], axis=1).reshape(16, 64, 64)


def _gate_mats(W):
    return _block_diag4(W["w_gate_a"][0]).astype(BF16), _block_diag4(W["w_gate_x"][0]).astype(BF16)


def _local_step(x, target, meta_full, inproj, out_weights, lru_cw_full, dw_w_full, W, gate_mats, send):
    wa_g, wx_g = gate_mats

    h, hn = _prenorm(x, meta_full, W["pre_norm_w"])
    z, win_full = inproj(hn)
    ylru, xc, hs = _lru_fwd(z, lru_cw_full, W["lru_conv_b"], wa_g, W["b_gate_a"], wx_g, W["b_gate_x"],
                            W["lru_lambda"])
    vc = _conf_fwd_conv(z, dw_w_full, W["conf_dw_b"])
    wout_full, pw_full = out_weights(vc)
    yconf, p, xhat, rstd = _conf_fwd_proj(vc, z, W["conf_ln_w"], W["conf_ln_b"], pw_full, W["conf_pw_b"])
    dout, dy, loss_acc, dpostw_acc = _outproj_loss(ylru, yconf, wout_full, h, target, W["post_norm_w"])

    dycat, dwout_part = _outproj_bwd(dy, ylru, yconf, wout_full)
    tok = send("w_out", ("w_out", dwout_part))
    dvc, dz41, dpw_part, cvecs = _conf_bwd_proj(dycat, p, z, xhat, rstd, hs, W["conf_ln_w"], W["conf_ln_b"], pw_full, tok)
    tok = send("w_in_c", ("conf_pw_w", dpw_part), ("w_in_c", _inproj_wgrad("inproj_wgrad_c", hn, dz41, dz41)))
    dzc, ddw_acc, kvecs = _conf_bwd_conv(dvc, z, dw_w_full, tok)
    tok = send("w_in_b", ("w_in_b", _inproj_wgrad("inproj_wgrad_b", hn, dzc, dzc)))
    dzl, dwa_g, dwx_g, dcw_acc, lvecs = _lru_bwd(dycat, z, xc, hs, lru_cw_full, wa_g, W["b_gate_a"], wx_g,
                                                 W["b_gate_x"], W["lru_lambda"], tok)
    tok = send("w_gates", ("w_gate_a", _diag_blocks(dwa_g).reshape(16 * 64, 64)),
               ("w_gate_x", _diag_blocks(dwx_g).reshape(16 * 64, 64)))
    tok = send("w_in_a", ("w_in_a", _inproj_wgrad("inproj_wgrad_a", hn, dzl, tok)))
    grad_x, dmeta, dprew_acc = _inproj_bwd(dzl, dz41, dzc, win_full, h, dout, W["pre_norm_w"], tok)

    vec_pack, small_part = _pack_grads(dprew_acc, dpostw_acc, cvecs, kvecs, lvecs, dcw_acc, ddw_acc, dmeta, loss_acc)
    return grad_x, vec_pack, small_part


def kernel(x, meta_tokens, pre_norm_w, post_norm_w, w_in, b_in, lru_conv_w, lru_conv_b, w_gate_a, b_gate_a, w_gate_x, b_gate_x, lru_lambda, conf_dw_w, conf_dw_b, conf_ln_w, conf_ln_b, conf_pw_w, conf_pw_b, w_out, loss_target, m_meta_tokens, m_pre_norm_w, m_post_norm_w, m_w_in, m_b_in, m_lru_conv_w, m_lru_conv_b, m_w_gate_a, m_b_gate_a, m_w_gate_x, m_b_gate_x, m_lru_lambda, m_conf_dw_w, m_conf_dw_b, m_conf_ln_w, m_conf_ln_b, m_conf_pw_w, m_conf_pw_b, m_w_out, v_meta_tokens, v_pre_norm_w, v_post_norm_w, v_w_in, v_b_in, v_lru_conv_w, v_lru_conv_b, v_w_gate_a, v_b_gate_a, v_w_gate_x, v_b_gate_x, v_lru_lambda, v_conf_dw_w, v_conf_dw_b, v_conf_ln_w, v_conf_ln_b, v_conf_pw_w, v_conf_pw_b, v_w_out):
    W = dict(meta_tokens=meta_tokens, pre_norm_w=pre_norm_w, post_norm_w=post_norm_w, w_in=w_in, b_in=b_in,
             lru_conv_w=lru_conv_w, lru_conv_b=lru_conv_b, w_gate_a=w_gate_a, b_gate_a=b_gate_a,
             w_gate_x=w_gate_x, b_gate_x=b_gate_x, lru_lambda=lru_lambda, conf_dw_w=conf_dw_w,
             conf_dw_b=conf_dw_b, conf_ln_w=conf_ln_w, conf_ln_b=conf_ln_b, conf_pw_w=conf_pw_w,
             conf_pw_b=conf_pw_b, w_out=w_out)
    M = dict(meta_tokens=m_meta_tokens, pre_norm_w=m_pre_norm_w, post_norm_w=m_post_norm_w, w_in=m_w_in,
             b_in=m_b_in, lru_conv_w=m_lru_conv_w, lru_conv_b=m_lru_conv_b, w_gate_a=m_w_gate_a,
             b_gate_a=m_b_gate_a, w_gate_x=m_w_gate_x, b_gate_x=m_b_gate_x, lru_lambda=m_lru_lambda,
             conf_dw_w=m_conf_dw_w, conf_dw_b=m_conf_dw_b, conf_ln_w=m_conf_ln_w, conf_ln_b=m_conf_ln_b,
             conf_pw_w=m_conf_pw_w, conf_pw_b=m_conf_pw_b, w_out=m_w_out)
    V = dict(meta_tokens=v_meta_tokens, pre_norm_w=v_pre_norm_w, post_norm_w=v_post_norm_w, w_in=v_w_in,
             b_in=v_b_in, lru_conv_w=v_lru_conv_w, lru_conv_b=v_lru_conv_b, w_gate_a=v_w_gate_a,
             b_gate_a=v_b_gate_a, w_gate_x=v_w_gate_x, b_gate_x=v_b_gate_x, lru_lambda=v_lru_lambda,
             conf_dw_w=v_conf_dw_w, conf_dw_b=v_conf_dw_b, conf_ln_w=v_conf_ln_w, conf_ln_b=v_conf_ln_b,
             conf_pw_w=v_conf_pw_w, conf_pw_b=v_conf_pw_b, w_out=v_w_out)
    names = list(W.keys())
    shapes = {n: W[n].shape for n in names}

    small = _pack_small(lru_conv_w[0], conf_dw_w[0], meta_tokens)
    (small_flight,), tok = _exchange_start("gather_small_start", [
        (small, jax.ShapeDtypeStruct((NDEV, _SM_ROWS, 256), F32), _whole, _slot)])
    win_flight, tok = _win_gather_start(w_in[0].astype(BF16) + tok[0, 0].astype(BF16))
    gate_mats = _gate_mats(W)
    wout_shard = w_out[0].astype(BF16) + tok[0, 0].astype(BF16)
    pw_shard = conf_pw_w[0].astype(BF16)
    cast_done = (gate_mats[0][0, 0:8, 0:128] + gate_mats[1][0, 0:8, 0:128]
                 + wout_shard[0:8, 0:128] + pw_shard[0:8, 0:128])
    win_flight, tok = _win_gather_links(win_flight, cast_done)
    gathered, tok = _exchange_start("gather_out_start", [
        (wout_shard + tok[0, 0].astype(BF16), jax.ShapeDtypeStruct((D, D), BF16), _whole, _rows(D // NDEV)),
        (pw_shard, jax.ShapeDtypeStruct((DC, DC), BF16), _whole, _rows(DC // NDEV)),
    ])
    (small_all,) = _exchange_wait("gather_small_wait", [small_flight], tok)
    unshard = lambda a: jnp.transpose(a, (1, 0, 2)).reshape(a.shape[1], -1)
    lru_cw_full = unshard(small_all[:, 0:LW, 0:128])
    dw_w_full = unshard(small_all[:, 8:8 + KWP, 0:128])
    meta_full = unshard(small_all[:, 40:56, :])

    def out_weights(after):
        return _exchange_wait("gather_out_wait", gathered, after)

    def inproj(hn):
        xi, yi, ci = lax.axis_index("x"), lax.axis_index("y"), lax.axis_index("c")
        shard = lambda px, py, pc: (4 * px + 2 * py + pc).astype(jnp.int32)
        over_links = jnp.stack([shard(1 - xi, yi, ci), shard(xi, 1 - yi, ci), shard(1 - xi, 1 - yi, ci)])
        z, src = _inproj_cols("inproj_own", jnp.stack([shard(xi, yi, ci)]), hn, win_flight["src"], b_in, None)
        flight = _win_gather_early(dict(win_flight, src=src))
        z, land = _inproj_cols("inproj_here", jnp.stack([shard(xi, yi, 1 - ci)]), hn, flight["land"], b_in, z)
        flight = _win_gather_forward("all", dict(flight, land=land), (1, 2, 3), z)
        z, land = _inproj_cols("inproj_links", over_links, hn, flight["land"], b_in, z)
        flight = _win_gather_forwarded("all", dict(flight, land=land), (1, 2, 3))
        z, land = _inproj_cols("inproj_sibling", over_links + 1 - 2 * ci, hn, flight["land"], b_in, z)
        return z, _win_gather_wait(dict(flight, land=land))

    row_stage = lambda ncol: (jax.ShapeDtypeStruct((NDEV, D // NDEV, ncol), BF16), _rows(D // NDEV))
    piece = {"w_in_a": row_stage(1024), "w_in_b": row_stage(2048), "w_in_c": row_stage(2048),
             "w_out": row_stage(D),
             "conf_pw_w": (jax.ShapeDtypeStruct((NDEV, DC // NDEV, DC), BF16), _rows(DC // NDEV)),
             "w_gate_a": (jax.ShapeDtypeStruct((NDEV, 16 * 64, 64), BF16), _whole),
             "w_gate_x": (jax.ShapeDtypeStruct((NDEV, 16 * 64, 64), BF16), _whole)}
    sent = {}

    def send(call, *named_parts):
        handles, token = _exchange_start(
            "scatter_" + call + "_start",
            [(part.astype(BF16), piece[name][0], piece[name][1], _slot) for name, part in named_parts])
        for (name, _), handle in zip(named_parts, handles):
            sent[name] = [handle]
        return token

    grad_x, vec_pack, small_part = _local_step(
        x[0], loss_target[0], meta_full, inproj, out_weights, lru_cw_full, dw_w_full, W, gate_mats, send)
    grad_x = grad_x[None]

    rest, tok = _exchange_start("scatter_rest_start", [
        (small_part, jax.ShapeDtypeStruct((NDEV, _SM_ROWS, 256), F32), _slot, _slot),
        (vec_pack, jax.ShapeDtypeStruct((NDEV, _VEC_ROWS, 1024), F32), _whole, _slot),
    ])
    (parts_c,) = _exchange_wait("scatter_w_in_c_wait", sent["w_in_c"], tok)
    (parts_b,) = _exchange_wait("scatter_w_in_b_wait", sent["w_in_b"], parts_c)
    (parts_a,) = _exchange_wait("scatter_w_in_a_wait", sent["w_in_a"], parts_b)
    win_rows = _sum_win_parts(parts_a, parts_b, parts_c)
    win_stage2, tok = _exchange_start("scatter_w_in_stage2_start", [
        (win_rows, jax.ShapeDtypeStruct((NDEV, D // NDEV, NIN // NDEV), BF16), _cols(NIN // NDEV), _slot)])

    G, DW, NM, NV = {}, {}, {}, {}
    (wout_parts,) = _exchange_wait("scatter_w_out_wait", sent["w_out"], tok)
    G["w_out"], DW["w_out"], NM["w_out"], NV["w_out"] = _adamw("adamw_w_out", wout_parts, w_out[0], m_w_out[0], v_w_out[0], 64)
    (pw_parts,) = _exchange_wait("scatter_conf_pw_w_wait", sent["conf_pw_w"], G["w_out"])
    G["conf_pw_w"], DW["conf_pw_w"], NM["conf_pw_w"], NV["conf_pw_w"] = _adamw(
        "adamw_pw", pw_parts, conf_pw_w[0], m_conf_pw_w[0], v_conf_pw_w[0], 128)
    res = {}
    wa_parts, wx_parts = _exchange_wait("scatter_w_gates_wait", sent["w_gate_a"] + sent["w_gate_x"], G["conf_pw_w"])
    for n, parts in (("w_gate_a", wa_parts), ("w_gate_x", wx_parts)):
        res[n] = _adamw("adamw_" + n, parts, *[d[n].reshape(16 * 64, 64) for d in (W, M, V)], 16 * 64)
    small_parts, vec_parts = _exchange_wait("scatter_rest_wait", rest, res["w_gate_x"][0])
    res.update(_adamw_small(small_parts, W, M, V))
    vec_res, loss_row = _adamw_vec(vec_parts, W, M, V)
    res.update(vec_res)
    (win_sum,) = _exchange_wait("scatter_w_in_stage2_wait", win_stage2, loss_row)
    res["w_in"] = _adamw("adamw_w_in", win_sum.reshape(1, D, NIN // NDEV), w_in[0], m_w_in[0], v_w_in[0], 256)
    for n, vals in res.items():
        for dst, val in zip((G, DW, NM, NV), vals):
            dst[n] = val
    for dst in (G, DW, NM, NV):
        for n in names:
            dst[n] = dst[n].reshape(shapes[n])
    loss = loss_row[0, 0]

    return (loss, grad_x, *[G[n] for n in names], *[DW[n] for n in names],
            *[NM[n] for n in names], *[NV[n] for n in names])
```

```python
import functools

import jax
import jax.numpy as jnp
from jax import lax
from jax.experimental import pallas as pl
from jax.experimental.pallas import tpu as pltpu

F32 = jnp.float32
BF16 = jnp.bfloat16

D = 2048
DL = 1024
DC = 1024
NIN = 5120
NMETA = 16
SEQ = 2048
T = NMETA + SEQ
TP = 2176
TM = 544
CB = 256
NCB = DL // CB
R = 16
KW = 31
KWP = 32
LW = 4
LRU_C = 8.0
EPS = 1e-6
NDEV = 8

ADAM_LR = 0.001
ADAM_B1 = 0.9
ADAM_B2 = 0.999
ADAM_EPS = 1e-08
ADAM_WD = 0.01
ADAM_STEP = 10

VMEM_LIMIT = 56 * 1024 * 1024


def _cparams():
    return pltpu.CompilerParams(vmem_limit_bytes=VMEM_LIMIT)


def _sig(x):
    return 1.0 / (1.0 + jnp.exp(-x))


def _expm1_neg(y):
    poly = y * (1.0 + y * (0.5 + y * (1.0 / 6.0 + y * (1.0 / 24.0 + y * (1.0 / 120.0)))))
    return jnp.where(y > -0.1, poly, jnp.exp(y) - 1.0)


def _softplus(x):
    e = jnp.exp(-jnp.abs(x))
    w = 1.0 + e
    l1p = jnp.where(w == 1.0, e, jnp.log(w) * e / (w - 1.0))
    return jnp.maximum(x, 0.0) + l1p


def _row_iota(shape):
    return lax.broadcasted_iota(jnp.int32, shape, 0)


def _fold8(v):
    return v[0:8, :] + v[8:16, :]


_FLIPS = [(k >> 2 & 1, k >> 1 & 1, k & 1) for k in range(1, NDEV)]
_HBM = pl.BlockSpec(memory_space=pltpu.HBM)
_SEM = pl.BlockSpec(memory_space=pltpu.SEMAPHORE)


def _peers():
    x, y, c = lax.axis_index("x"), lax.axis_index("y"), lax.axis_index("c")
    out = []
    for dx, dy, dc in _FLIPS:
        px = 1 - x if dx else x
        py = 1 - y if dy else y
        pc = 1 - c if dc else c
        out.append(((px, py, pc), 4 * px + 2 * py + pc))
    return 4 * x + 2 * y + c, out


def _exchange_start(name, items):
    n = len(items)

    def body(*refs):
        srcs, lands = refs[:n], refs[n:2 * n]
        outs = refs[2 * n:]
        send_sems, recv_sems, local_sems = outs[:n], outs[n:2 * n], outs[2 * n:3 * n]
        token = outs[-1]
        me, peers = _peers()
        for a in range(n):
            src_at, dst_at = items[a][2], items[a][3]
            pltpu.make_async_copy(src_at(srcs[a], me), dst_at(lands[a], me), local_sems[a]).start()
        for a in range(n):
            src_at, dst_at = items[a][2], items[a][3]
            for k, (pos, peer) in enumerate(peers):
                pltpu.make_async_remote_copy(
                    src_ref=src_at(srcs[a], peer), dst_ref=dst_at(lands[a], me),
                    send_sem=send_sems[a].at[k], recv_sem=recv_sems[a].at[k],
                    device_id=pos, device_id_type=pl.DeviceIdType.MESH).start()
        token[...] = jnp.zeros_like(token)

    srcs = [pltpu.with_memory_space_constraint(it[0], pltpu.HBM) for it in items]
    lands = [pltpu.with_memory_space_constraint(lax.empty(it[1].shape, it[1].dtype), pltpu.HBM) for it in items]
    sem7 = pltpu.SemaphoreType.DMA((NDEV - 1,))
    res = pl.pallas_call(
        body, name=name,
        out_shape=([sem7] * (2 * n) + [pltpu.SemaphoreType.DMA(())] * n
                   + [pltpu.HBM(a.shape, a.dtype) for a in srcs] + [pltpu.HBM(a.shape, a.dtype) for a in lands]
                   + [jax.ShapeDtypeStruct((8, 128), F32)]),
        in_specs=[_HBM] * (2 * n),
        out_specs=[_SEM] * (3 * n) + [_HBM] * (2 * n) + [pl.BlockSpec(memory_space=pltpu.VMEM)],
        input_output_aliases={i: 3 * n + i for i in range(2 * n)},
        compiler_params=pltpu.CompilerParams(has_side_effects=pltpu.SideEffectType.DATAFLOW_SIDE_EFFECTING),
    )(*srcs, *lands)
    handles = [dict(send=res[a], recv=res[n + a], local=res[2 * n + a], src=res[3 * n + a], land=res[4 * n + a],
                    src_at=items[a][2], dst_at=items[a][3]) for a in range(n)]
    return handles, res[-1]


def _wait_bytes(piece, sem):
    pltpu.make_async_copy(piece, piece, sem).wait()


def _exchange_wait(name, handles, after):
    n = len(handles)

    def body(*refs):
        srcs, lands = refs[:n], refs[n:2 * n]
        send_sems, recv_sems, local_sems = refs[2 * n:3 * n], refs[3 * n:4 * n], refs[4 * n:5 * n]
        me, peers = _peers()
        for a in range(n):
            src_at, dst_at = handles[a]["src_at"], handles[a]["dst_at"]
            for k, (pos, peer) in enumerate(peers):
                _wait_bytes(src_at(srcs[a], peer), send_sems[a].at[k])
                _wait_bytes(dst_at(lands[a], peer), recv_sems[a].at[k])
            pltpu.make_async_copy(src_at(srcs[a], me), dst_at(lands[a], me), local_sems[a]).wait()

    srcs = [hd["src"] for hd in handles]
    lands = [hd["land"] for hd in handles]
    res = pl.pallas_call(
        body, name=name,
        out_shape=[pltpu.HBM(a.shape, a.dtype) for a in srcs] + [pltpu.HBM(a.shape, a.dtype) for a in lands],
        in_specs=[_HBM] * (2 * n) + [_SEM] * (3 * n) + [pl.BlockSpec(memory_space=pl.ANY)],
        out_specs=[_HBM] * (2 * n),
        input_output_aliases={i: i for i in range(2 * n)},
        compiler_params=pltpu.CompilerParams(has_side_effects=pltpu.SideEffectType.DATAFLOW_SIDE_EFFECTING),
    )(*srcs, *lands, *[hd["send"] for hd in handles], *[hd["recv"] for hd in handles],
      *[hd["local"] for hd in handles], after)
    return list(res[n:])


_SIDE = pltpu.SideEffectType.DATAFLOW_SIDE_EFFECTING
_WCOLS = NIN // NDEV


def _win_cols(ref, l):
    return ref.at[:, pl.ds(pl.multiple_of(l * _WCOLS, 128), _WCOLS)]


def _win_routes():
    x, y, c = lax.axis_index("x"), lax.axis_index("y"), lax.axis_index("c")
    pos = [(x, y, 1 - c), (1 - x, y, c), (x, 1 - y, c), (1 - x, 1 - y, c)]
    return 4 * x + 2 * y + c, [(p, 4 * p[0] + 2 * p[1] + p[2]) for p in pos]


def _win_gather_start(shard):
    def body(src, land, send_sem, recv_sem, local_sem, src_thru, land_thru, token):
        me, routes = _win_routes()
        pltpu.make_async_copy(src, _win_cols(land, me), local_sem).start()
        pltpu.make_async_remote_copy(src_ref=src, dst_ref=_win_cols(land, me), send_sem=send_sem, recv_sem=recv_sem,
                                     device_id=routes[0][0], device_id_type=pl.DeviceIdType.MESH).start()
        token[...] = jnp.zeros_like(token)

    src = pltpu.with_memory_space_constraint(shard, pltpu.HBM)
    land = pltpu.with_memory_space_constraint(lax.empty((D, NIN), BF16), pltpu.HBM)
    sem = pltpu.SemaphoreType.DMA(())
    res = pl.pallas_call(
        body, name="win_gather_start",
        out_shape=[sem, sem, sem, pltpu.HBM(src.shape, BF16), pltpu.HBM(land.shape, BF16),
                   jax.ShapeDtypeStruct((8, 128), F32)],
        in_specs=[_HBM, _HBM],
        out_specs=[_SEM, _SEM, _SEM, _HBM, _HBM, pl.BlockSpec(memory_space=pltpu.VMEM)],
        input_output_aliases={0: 3, 1: 4},
        compiler_params=pltpu.CompilerParams(has_side_effects=_SIDE),
    )(src, land)
    return dict(send0=res[0], recv0=res[1], local=res[2], src=res[3], land=res[4]), res[5]


def _win_gather_links(hd, after):
    def body(src, land, after_ref, send_sems, recv_sems, src_thru, land_thru, token):
        me, routes = _win_routes()
        for k in (1, 2, 3):
            pltpu.make_async_remote_copy(src_ref=src, dst_ref=_win_cols(land, me), send_sem=send_sems.at[k - 1],
                                         recv_sem=recv_sems.at[k - 1], device_id=routes[k][0],
                                         device_id_type=pl.DeviceIdType.MESH).start()
        token[...] = jnp.zeros_like(token)

    sem3 = pltpu.SemaphoreType.DMA((3,))
    res = pl.pallas_call(
        body, name="win_gather_links",
        out_shape=[sem3, sem3, pltpu.HBM(hd["src"].shape, BF16), pltpu.HBM(hd["land"].shape, BF16),
                   jax.ShapeDtypeStruct((8, 128), F32)],
        in_specs=[_HBM, _HBM, pl.BlockSpec(memory_space=pl.ANY)],
        out_specs=[_SEM, _SEM, _HBM, _HBM, pl.BlockSpec(memory_space=pltpu.VMEM)],
        input_output_aliases={0: 2, 1: 3},
        compiler_params=pltpu.CompilerParams(has_side_effects=_SIDE),
    )(hd["src"], hd["land"], after)
    return dict(hd, send=res[0], recv=res[1], src=res[2], land=res[3]), res[4]


def _win_gather_forward(name, hd, ks, after):
    def body(land, recv_sems, after_ref, land_thru, fsend_sems, frecv_sems):
        me, routes = _win_routes()
        sibling = routes[0][0]
        for n, k in enumerate(ks):
            pos, peer = routes[k]
            piece = _win_cols(land, peer)
            pltpu.make_async_remote_copy(src_ref=piece, dst_ref=piece, send_sem=fsend_sems.at[n],
                                         recv_sem=recv_sems.at[k - 1], device_id=pos,
                                         device_id_type=pl.DeviceIdType.MESH).wait_recv()
            pltpu.make_async_remote_copy(src_ref=piece, dst_ref=piece, send_sem=fsend_sems.at[n],
                                         recv_sem=frecv_sems.at[n], device_id=sibling,
                                         device_id_type=pl.DeviceIdType.MESH).start()

    sems = pltpu.SemaphoreType.DMA((len(ks),))
    res = pl.pallas_call(
        body, name="win_gather_forward_" + name,
        out_shape=[pltpu.HBM(hd["land"].shape, BF16), sems, sems],
        in_specs=[_HBM, _SEM, pl.BlockSpec(memory_space=pl.ANY)],
        out_specs=[_HBM, _SEM, _SEM],
        input_output_aliases={0: 0},
        compiler_params=pltpu.CompilerParams(has_side_effects=_SIDE),
    )(hd["land"], hd["recv"], after)
    return dict(hd, land=res[0], **{"fsend" + name: res[1], "frecv" + name: res[2]})


def _win_gather_forwarded(name, hd, ks):
    def body(land, fsend_sems, frecv_sems, land_thru):
        me, routes = _win_routes()
        sib_c = routes[0][0][2]
        for n, k in enumerate(ks):
            _wait_bytes(_win_cols(land, routes[k][1]), fsend_sems.at[n])
            _wait_bytes(_win_cols(land, 4 * routes[k][0][0] + 2 * routes[k][0][1] + sib_c), frecv_sems.at[n])

    res = pl.pallas_call(
        body, name="win_gather_forwarded_" + name,
        out_shape=[pltpu.HBM(hd["land"].shape, BF16)],
        in_specs=[_HBM, _SEM, _SEM],
        out_specs=[_HBM],
        input_output_aliases={0: 0},
        compiler_params=pltpu.CompilerParams(has_side_effects=_SIDE),
    )(hd["land"], hd["fsend" + name], hd["frecv" + name])
    return dict(hd, land=res[0])


def _win_gather_early(hd):
    def body(src, land, recv_sem, local_sem, src_thru, land_thru):
        me, routes = _win_routes()
        _wait_bytes(_win_cols(land, routes[0][1]), recv_sem)
        pltpu.make_async_copy(src, _win_cols(land, me), local_sem).wait()

    res = pl.pallas_call(
        body, name="win_gather_early",
        out_shape=[pltpu.HBM(hd["src"].shape, BF16), pltpu.HBM(hd["land"].shape, BF16)],
        in_specs=[_HBM, _HBM, _SEM, _SEM],
        out_specs=[_HBM, _HBM],
        input_output_aliases={0: 0, 1: 1},
        compiler_params=pltpu.CompilerParams(has_side_effects=_SIDE),
    )(hd["src"], hd["land"], hd["recv0"], hd["local"])
    return dict(hd, src=res[0], land=res[1])


def _win_gather_wait(hd):
    def body(src, land, send0_sem, send_sems, src_thru, land_thru):
        for k in range(4):
            _wait_bytes(src, send0_sem if k == 0 else send_sems.at[k - 1])

    res = pl.pallas_call(
        body, name="win_gather_wait",
        out_shape=[pltpu.HBM(hd["src"].shape, BF16), pltpu.HBM(hd["land"].shape, BF16)],
        in_specs=[_HBM, _HBM, _SEM, _SEM],
        out_specs=[_HBM, _HBM],
        input_output_aliases={0: 0, 1: 1},
        compiler_params=pltpu.CompilerParams(has_side_effects=_SIDE),
    )(hd["src"], hd["land"], hd["send0"], hd["send"])
    return res[1]


def _whole(ref, l):
    return ref


def _slot(ref, l):
    return ref.at[l]


def _cols(width):
    def at(ref, l):
        return ref.at[:, pl.ds(pl.multiple_of(l * width, 128), width)]
    return at


def _rows(height):
    def at(ref, l):
        return ref.at[pl.ds(pl.multiple_of(l * height, 8), height), :]
    return at


NTILE = TP // TM


def _tile_rows(t):
    lo = max(t * TM - NMETA, 0)
    hi = min((t + 1) * TM - NMETA, SEQ)
    return lo, hi - lo, lo + NMETA - t * TM


def _for_tile(t, fn):
    for static_t in range(NTILE):
        pl.when(t == static_t)(functools.partial(fn, static_t))


def _token_tile_copy(hbm_ref, buf, sem, t):
    lo, n, off = _tile_rows(t)
    return pltpu.make_async_copy(hbm_ref.at[pl.ds(lo, n)], buf.at[pl.ds(off, n)], sem)


def _prenorm(x, meta_full, pre_w):
    def body(x_ref, meta_ref, pw_ref, h_ref, hn_ref, xbuf, sems):
        i = pl.program_id(0)
        slot = i % 2

        def start(t):
            _token_tile_copy(x_ref, xbuf.at[t % 2], sems.at[t % 2], t).start()

        @pl.when(i == 0)
        def _():
            start(0)
        _for_tile(i + 1, start)
        _for_tile(i, lambda t: _token_tile_copy(x_ref, xbuf.at[t % 2], sems.at[t % 2], t).wait())

        @pl.when(i == 0)
        def _():
            xbuf[0, 0:NMETA, :] = meta_ref[...]

        @pl.when(i == NTILE - 1)
        def _():
            last = _tile_rows(NTILE - 1)[1]
            xbuf[(NTILE - 1) % 2, last:TM, :] = jnp.zeros((TM - last, D), F32)

        pw = pw_ref[...]

        def chunk(ci, carry):
            r0 = pl.multiple_of(ci * R, R)
            xv = xbuf[slot, pl.ds(r0, R), :]
            h_ref[pl.ds(r0, R), :] = xv
            ms = jnp.mean(xv * xv, axis=-1, keepdims=True)
            hn_ref[pl.ds(r0, R), :] = (xv * lax.rsqrt(ms + EPS) * pw).astype(BF16)
            return carry
        lax.fori_loop(0, TM // R, chunk, 0, unroll=2)

    row = pl.BlockSpec((TM, D), lambda i: (i, 0))
    return pl.pallas_call(
        body, name="prenorm",
        grid=(NTILE,),
        in_specs=[pl.BlockSpec(memory_space=pl.ANY), pl.BlockSpec((NMETA, D), lambda i: (0, 0)),
                  pl.BlockSpec((1, D), lambda i: (0, 0))],
        out_specs=[row, row],
        out_shape=[jax.ShapeDtypeStruct((TP, D), F32), jax.ShapeDtypeStruct((TP, D), BF16)],
        scratch_shapes=[pltpu.VMEM((2, TM, D), F32), pltpu.SemaphoreType.DMA((2,))],
        compiler_params=_cparams(),
    )(x, meta_full, pre_w)


def _inproj_cols(name, shards, hn, w_land, b_in, z_prev):
    nsh = shards.shape[0]
    one_shard = w_land.shape[1] == _WCOLS

    def body(idx_ref, hn_ref, w_ref, b_ref, *rest):
        z_ref = rest[-2]
        z_ref[...] = jnp.dot(hn_ref[...], w_ref[...], preferred_element_type=F32) + b_ref[...]

    any_spec = pl.BlockSpec(memory_space=pl.ANY)
    in_specs = [pl.BlockSpec((TM, D), lambda j, i, idx: (i, 0)),
                pl.BlockSpec((D, _WCOLS), lambda j, i, idx: (0, 0 if one_shard else idx[j])),
                pl.BlockSpec((1, _WCOLS), lambda j, i, idx: (0, idx[j]))]
    operands = [hn, w_land, b_in]
    aliases = {2: 1}
    if z_prev is not None:
        in_specs.append(any_spec)
        operands.append(z_prev)
        aliases[4] = 0
    return pl.pallas_call(
        body, name=name,
        grid_spec=pltpu.PrefetchScalarGridSpec(
            num_scalar_prefetch=1, grid=(nsh, TP // TM), in_specs=in_specs,
            out_specs=[pl.BlockSpec((TM, _WCOLS), lambda j, i, idx: (i, idx[j])), any_spec]),
        out_shape=[jax.ShapeDtypeStruct((TP, NIN), F32), jax.ShapeDtypeStruct(w_land.shape, w_land.dtype)],
        input_output_aliases=aliases,
        compiler_params=_cparams(),
    )(shards, *operands)


def _gate_values(ga, gx, xc, sp8):
    r = _sig(ga)
    i = _sig(gx)
    log_a = -(r * sp8)
    a = jnp.exp(log_a)
    mult = jnp.sqrt(-_expm1_neg(2.0 * log_a))
    return r, i, a, mult


def _lru_fwd(z, conv_w, conv_b, wa_g, b_a, wx_g, b_x, lam):
    def body(x_ref, g_ref, cw_ref, cb_ref, wa_ref, ba_ref, wx_ref, bx_ref, lam_ref,
             y_ref, xc_ref, hs_ref, ga_s, gx_s):
        taps = [cw_ref[k:k + 1, :] for k in range(LW)]
        cb = cb_ref[...]

        def conv_chunk(ci, carry):
            r0 = pl.multiple_of(ci * R, R)
            cur = x_ref[pl.ds(r0, R), :]
            p0 = pl.multiple_of(jnp.maximum(r0 - 8, 0), 8)
            prev = jnp.where(ci > 0, x_ref[pl.ds(p0, 8), :], 0.0)
            buf = jnp.concatenate([prev, cur], axis=0)
            acc = cur * taps[LW - 1] + cb
            for s in range(1, LW):
                acc = acc + pltpu.roll(buf, s, 0)[8:8 + R, :] * taps[LW - 1 - s]
            xc_ref[pl.ds(r0, R), :] = acc
            return carry
        lax.fori_loop(0, TP // R, conv_chunk, 0)

        def gate_chunk(ci, carry):
            r0 = pl.multiple_of(ci * TM, TM)
            xb = xc_ref[pl.ds(r0, TM), :].astype(BF16)
            ga_s[pl.ds(r0, TM), :] = jnp.dot(xb, wa_ref[...], preferred_element_type=F32) + ba_ref[...]
            gx_s[pl.ds(r0, TM), :] = jnp.dot(xb, wx_ref[...], preferred_element_type=F32) + bx_ref[...]
            return carry
        lax.fori_loop(0, TP // TM, gate_chunk, 0)

        sp8 = LRU_C * _softplus(-lam_ref[...])
        row = _row_iota((R, CB))

        def scan_chunk(ci, hprev):
            r0 = pl.multiple_of(ci * R, R)
            xc = xc_ref[pl.ds(r0, R), :]
            _, i, a, mult = _gate_values(ga_s[pl.ds(r0, R), :], gx_s[pl.ds(r0, R), :], xc, sp8)
            u = mult * (i * xc)
            k = 1
            while k < R:
                m = row >= k
                u = jnp.where(m, a * pltpu.roll(u, k, 0) + u, u)
                a = jnp.where(m, a * pltpu.roll(a, k, 0), a)
                k *= 2
            hv = u + a * hprev
            hs_ref[pl.ds(r0, R), :] = hv
            g = g_ref[pl.ds(r0, R), :]
            y_ref[pl.ds(r0, R), :] = (hv * (g * _sig(g))).astype(BF16)
            return jnp.sum(jnp.where(row == R - 1, hv, 0.0), axis=0, keepdims=True)
        def scan_pass(i, hp):
            for sub in range(4):
                hp = scan_chunk(4 * i + sub, hp)
            return hp
        lax.fori_loop(0, TP // R // 4, scan_pass, jnp.zeros((1, CB), F32))

    col = lambda off: pl.BlockSpec((TP, CB), lambda j: (0, off + j))
    vec = pl.BlockSpec((1, CB), lambda j: (0, j))
    wsp = pl.BlockSpec((None, CB, CB), lambda j: (j, 0, 0))
    return pl.pallas_call(
        body, name="lru_fwd",
        grid=(NCB,),
        in_specs=[col(0), col(NCB), pl.BlockSpec((LW, CB), lambda j: (0, j)), vec, wsp, vec, wsp, vec, vec],
        out_specs=[col(0), col(0), col(0)],
        out_shape=[jax.ShapeDtypeStruct((TP, DL), BF16), jax.ShapeDtypeStruct((TP, DL), F32),
                   jax.ShapeDtypeStruct((TP, DL), F32)],
        scratch_shapes=[pltpu.VMEM((TP, CB), F32), pltpu.VMEM((TP, CB), F32)],
        compiler_params=_cparams(),
    )(z, z, conv_w, conv_b, wa_g, b_a, wx_g, b_x, lam)


CBC = 128
NCBC = DC // CBC
RC = 64


def _fold_rows(v):
    acc = v[0:8, :]
    for r in range(8, v.shape[0], 8):
        acc = acc + v[r:r + 8, :]
    return acc


def _conf_fwd_conv(z, dw_w, dw_b):
    def body(u1_ref, u2_ref, w_ref, b_ref, vc_ref, vs):
        vs[pl.ds(0, KWP), :] = jnp.zeros((KWP, CBC), F32)

        def glu_chunk(ci, carry):
            r0 = pl.multiple_of(ci * RC, RC)
            vs[pl.ds(KWP + r0, RC), :] = u1_ref[pl.ds(r0, RC), :] * _sig(u2_ref[pl.ds(r0, RC), :])
            return carry
        lax.fori_loop(0, TP // RC, glu_chunk, 0)

        bias = b_ref[...]

        def conv_chunk(ci, carry):
            r0 = pl.multiple_of(ci * RC, RC)
            buf = vs[pl.ds(r0, KWP + RC), :]
            acc = jnp.zeros((RC, CBC), F32) + bias
            for rr in range(8):
                rolled = buf if rr == 0 else pltpu.roll(buf, rr, 0)
                for q in range(4):
                    s = 8 * q + rr
                    if s > KW - 1:
                        continue
                    k = KW - 1 - s
                    acc = acc + rolled[KWP - 8 * q:KWP - 8 * q + RC, :] * w_ref[k:k + 1, :]
            vc_ref[pl.ds(r0, RC), :] = acc
            return carry
        lax.fori_loop(0, TP // RC, conv_chunk, 0)

    return pl.pallas_call(
        body, name="conf_fwd_conv",
        grid=(NCBC,),
        in_specs=[pl.BlockSpec((TP, CBC), lambda j: (0, 2 * NCBC + j)),
                  pl.BlockSpec((TP, CBC), lambda j: (0, 3 * NCBC + j)),
                  pl.BlockSpec((KWP, CBC), lambda j: (0, j)),
                  pl.BlockSpec((1, CBC), lambda j: (0, j))],
        out_specs=pl.BlockSpec((TP, CBC), lambda j: (0, j)),
        out_shape=jax.ShapeDtypeStruct((TP, DC), F32),
        scratch_shapes=[pltpu.VMEM((TP + KWP, CBC), F32)],
        compiler_params=_cparams(),
    )(z, z, dw_w, dw_b)


def _ln_chunk(vc, lw, lb):
    mu = jnp.mean(vc, axis=-1, keepdims=True)
    xm = vc - mu
    var = jnp.mean(xm * xm, axis=-1, keepdims=True)
    rstd = lax.rsqrt(var + EPS)
    xhat = xm * rstd
    return xhat, rstd, xhat * lw + lb


def _conf_fwd_proj(vc, z, ln_w, ln_b, pw_w, pw_b):
    def body(vc_ref, g_ref, lw_ref, lb_ref, w_ref, b_ref, y_ref, p_ref, xhat_ref, rstd_ref, s_s):
        lw, lb = lw_ref[...], lb_ref[...]

        def ln_chunk(ci, carry):
            r0 = pl.multiple_of(ci * R, R)
            for half in range(2):
                rr = r0 + 8 * half
                xhat, rstd, ln = _ln_chunk(vc_ref[pl.ds(rr, 8), :], lw, lb)
                xhat_ref[pl.ds(rr, 8), :] = xhat
                rstd_ref[pl.ds(rr, 8), :] = jnp.broadcast_to(rstd, (8, 128))
                p_ref[pl.ds(rr, 8), :] = ln * _sig(ln)
            s_s[pl.ds(r0, R), :] = p_ref[pl.ds(r0, R), :].astype(BF16)
            return carry
        lax.fori_loop(0, TM // R, ln_chunk, 0, unroll=2)

        p_ref[...] = jnp.dot(s_s[...], w_ref[...], preferred_element_type=F32) + b_ref[...]

        def out_chunk(ci, carry):
            r0 = pl.multiple_of(ci * R, R)
            g = g_ref[pl.ds(r0, R), :]
            y_ref[pl.ds(r0, R), :] = (p_ref[pl.ds(r0, R), :] * (g * _sig(g))).astype(BF16)
            return carry
        lax.fori_loop(0, TM // R, out_chunk, 0)

    row = pl.BlockSpec((TM, DC), lambda i: (i, 0))
    vec = pl.BlockSpec((1, DC), lambda i: (0, 0))
    return pl.pallas_call(
        body, name="conf_fwd_proj",
        grid=(TP // TM,),
        in_specs=[row, pl.BlockSpec((TM, DC), lambda i: (i, 4)), vec, vec,
                  pl.BlockSpec((DC, DC), lambda i: (0, 0)), vec],
        out_specs=[row, row, row, pl.BlockSpec((TM, 128), lambda i: (i, 0))],
        out_shape=[jax.ShapeDtypeStruct((TP, DC), BF16), jax.ShapeDtypeStruct((TP, DC), F32),
                   jax.ShapeDtypeStruct((TP, DC), F32), jax.ShapeDtypeStruct((TP, 128), F32)],
        scratch_shapes=[pltpu.VMEM((TM, DC), BF16)],
        compiler_params=_cparams(),
    )(vc, z, ln_w, ln_b, pw_w, pw_b)


def _outproj_loss(ylru, yconf, w_out, h, target, post_w):
    def body(yl_ref, yc_ref, w_ref, h_ref, tgt_hbm, pw_ref, dout_ref, dy_ref, loss_ref, dpw_ref, y_s, t_ref, sem):
        i = pl.program_id(0)
        k = pl.program_id(1)

        @pl.when(k == 0)
        def _():
            _for_tile(i, lambda t: _token_tile_copy(tgt_hbm, t_ref, sem, t).start())
            y_s[...] = jnp.dot(yl_ref[...], w_ref[...], preferred_element_type=F32)

        @pl.when(k == 1)
        def _():
            y_s[...] += jnp.dot(yc_ref[...], w_ref[...], preferred_element_type=F32)

        @pl.when(jnp.logical_and(i == 0, k == 1))
        def _():
            loss_ref[...] = jnp.zeros_like(loss_ref)
            dpw_ref[...] = jnp.zeros_like(dpw_ref)

        @pl.when(k == 1)
        def _():
            _for_tile(i, lambda t: _token_tile_copy(tgt_hbm, t_ref, sem, t).wait())

            @pl.when(i == 0)
            def _():
                t_ref[0:NMETA, :] = jnp.zeros((NMETA, D), F32)

            @pl.when(i == NTILE - 1)
            def _():
                last = _tile_rows(NTILE - 1)[1]
                t_ref[last:TM, :] = jnp.zeros((TM - last, D), F32)

            pw = pw_ref[...]
            row = _row_iota((8, D))

            def chunk(ci, carry):
                r0 = pl.multiple_of(ci * 8, 8)
                yv = y_s[pl.ds(r0, 8), :]
                rs = lax.rsqrt(jnp.mean(yv * yv, axis=-1, keepdims=True) + EPS)
                grow = row + (i * TM + r0)
                valid = jnp.logical_and(grow >= NMETA, grow < T)
                yn = yv * rs
                err = jnp.where(valid, h_ref[pl.ds(r0, 8), :] + yn * pw - t_ref[pl.ds(r0, 8), :], 0.0)
                loss_ref[...] += err * err
                d_rn = err * (1.0 / D)
                dout_ref[pl.ds(r0, 8), :] = d_rn
                dpw_ref[...] += d_rn * yn
                gw = d_rn * pw
                dot = jnp.mean(gw * yv, axis=-1, keepdims=True)
                dy_ref[pl.ds(r0, 8), :] = (rs * gw - yv * (rs * rs * rs * dot)).astype(BF16)
                return carry
            lax.fori_loop(0, TM // 8, chunk, 0, unroll=4)

    row = pl.BlockSpec((TM, D), lambda i, k: (i, 0))
    half = pl.BlockSpec((TM, DL), lambda i, k: (i, 0))
    acc = pl.BlockSpec((8, D), lambda i, k: (0, 0))
    return pl.pallas_call(
        body, name="outproj_loss",
        grid=(TP // TM, 2),
        in_specs=[half, half, pl.BlockSpec((DL, D), lambda i, k: (k, 0)), row, pl.BlockSpec(memory_space=pl.ANY),
                  pl.BlockSpec((1, D), lambda i, k: (0, 0))],
        out_specs=[row, row, acc, acc],
        out_shape=[jax.ShapeDtypeStruct((TP, D), F32), jax.ShapeDtypeStruct((TP, D), BF16),
                   jax.ShapeDtypeStruct((8, D), F32), jax.ShapeDtypeStruct((8, D), F32)],
        scratch_shapes=[pltpu.VMEM((TM, D), F32), pltpu.VMEM((TM, D), F32), pltpu.SemaphoreType.DMA(())],
        compiler_params=_cparams(),
    )(ylru, yconf, w_out, h, target, post_w)


_NT = (((1,), (1,)), ((), ()))
_TN = (((0,), (0,)), ((), ()))


def _outproj_bwd(dy, ylru, yconf, w_out):
    def body(dy_ref, yl_ref, yc_ref, w_ref, dycat_ref, dw_ref):
        j = pl.program_id(0)
        dyv = dy_ref[...]
        dycat_ref[...] = lax.dot_general(dyv, w_ref[...], _NT, preferred_element_type=F32)

        @pl.when(j < NCB)
        def _():
            dw_ref[...] = lax.dot_general(yl_ref[...], dyv, _TN, preferred_element_type=F32).astype(BF16)

        @pl.when(j >= NCB)
        def _():
            dw_ref[...] = lax.dot_general(yc_ref[...], dyv, _TN, preferred_element_type=F32).astype(BF16)

    return pl.pallas_call(
        body, name="outproj_bwd",
        grid=(2 * NCB,),
        in_specs=[pl.BlockSpec((TP, D), lambda j: (0, 0)),
                  pl.BlockSpec((TP, CB), lambda j: (0, jnp.minimum(j, NCB - 1))),
                  pl.BlockSpec((TP, CB), lambda j: (0, jnp.maximum(j - NCB, 0))),
                  pl.BlockSpec((CB, D), lambda j: (j, 0))],
        out_specs=[pl.BlockSpec((TP, CB), lambda j: (0, j)), pl.BlockSpec((CB, D), lambda j: (j, 0))],
        out_shape=[jax.ShapeDtypeStruct((TP, D), F32), jax.ShapeDtypeStruct((D, D), BF16)],
        compiler_params=_cparams(),
    )(dy, ylru, yconf, w_out)


_AFTER = pl.BlockSpec(memory_space=pl.ANY)


def _conf_bwd_proj(dycat, p, z, xhat, rstd, hs, ln_w, ln_b, pw_w, after):
    def body(dy_ref, p_ref, g_ref, xhat_ref, rstd_ref, dyl_ref, hs_ref, gl_ref, lw_ref, lb_ref, w_ref, after_ref,
             dvc_ref, dz_ref, dpw_ref, vecs_ref, dp_s, s_s, ds_s):
        i = pl.program_id(0)
        lw, lb = lw_ref[...], lb_ref[...]

        @pl.when(i == 0)
        def _():
            dpw_ref[...] = jnp.zeros_like(dpw_ref)
            vecs_ref[...] = jnp.zeros_like(vecs_ref)

        def pre_chunk(ci, carry):
            r0 = pl.multiple_of(ci * R, R)
            for half in range(2):
                rr = r0 + 8 * half
                dyv = dy_ref[pl.ds(rr, 8), :]
                g = g_ref[pl.ds(rr, 8), :]
                sg = _sig(g)
                dp = dyv * (g * sg)
                dg = dyv * p_ref[pl.ds(rr, 8), :] * (sg * (1.0 + g * (1.0 - sg)))
                vecs_ref[0:8, :] += dp
                vecs_ref[8:16, :] += dg
                ds_s[pl.ds(rr, 8), :] = dp
                dvc_ref[pl.ds(rr, 8), :] = dg
            dp_s[pl.ds(r0, R), :] = ds_s[pl.ds(r0, R), :].astype(BF16)
            dz_ref[0, pl.ds(r0, R), :] = dvc_ref[pl.ds(r0, R), :].astype(BF16)
            for half in range(2):
                rr = r0 + 8 * half
                gl = gl_ref[pl.ds(rr, 8), :]
                sgl = _sig(gl)
                dgl = dyl_ref[pl.ds(rr, 8), :] * hs_ref[pl.ds(rr, 8), :] * (sgl * (1.0 + gl * (1.0 - sgl)))
                vecs_ref[32:40, :] += dgl
                dvc_ref[pl.ds(rr, 8), :] = dgl
            dz_ref[1, pl.ds(r0, R), :] = dvc_ref[pl.ds(r0, R), :].astype(BF16)
            for half in range(2):
                rr = r0 + 8 * half
                ln = xhat_ref[pl.ds(rr, 8), :] * lw + lb
                ds_s[pl.ds(rr, 8), :] = ln * _sig(ln)
            s_s[pl.ds(r0, R), :] = ds_s[pl.ds(r0, R), :].astype(BF16)
            return carry
        lax.fori_loop(0, TM // R, pre_chunk, 0, unroll=2)

        dpb = dp_s[...]
        ds_s[...] = lax.dot_general(dpb, w_ref[...], _NT, preferred_element_type=F32)
        dpw_ref[...] += lax.dot_general(s_s[...], dpb, _TN, preferred_element_type=F32)

        def post_chunk(ci, carry):
            r0 = pl.multiple_of(ci * 8, 8)
            xhat = xhat_ref[pl.ds(r0, 8), :]
            rstd = jnp.tile(rstd_ref[pl.ds(r0, 8), :], (1, DC // 128))
            ln = xhat * lw + lb
            sl = _sig(ln)
            dln = ds_s[pl.ds(r0, 8), :] * (sl * (1.0 + ln * (1.0 - sl)))
            vecs_ref[16:24, :] += dln * xhat
            vecs_ref[24:32, :] += dln
            dxh = dln * lw
            m1 = jnp.mean(dxh, axis=-1, keepdims=True)
            m2 = jnp.mean(dxh * xhat, axis=-1, keepdims=True)
            dvc_ref[pl.ds(r0, 8), :] = rstd * (dxh - m1 - xhat * m2)
            return carry
        lax.fori_loop(0, TM // 8, post_chunk, 0, unroll=4)

    row = pl.BlockSpec((TM, DC), lambda i: (i, 0))
    vec = pl.BlockSpec((1, DC), lambda i: (0, 0))
    return pl.pallas_call(
        body, name="conf_bwd_proj",
        grid=(TP // TM,),
        in_specs=[pl.BlockSpec((TM, DC), lambda i: (i, 1)), row, pl.BlockSpec((TM, DC), lambda i: (i, 4)), row,
                  pl.BlockSpec((TM, 128), lambda i: (i, 0)),
                  pl.BlockSpec((TM, DL), lambda i: (i, 0)), row, pl.BlockSpec((TM, DL), lambda i: (i, 1)),
                  vec, vec, pl.BlockSpec((DC, DC), lambda i: (0, 0)), _AFTER],
        out_specs=[row, pl.BlockSpec((2, TM, DC), lambda i: (0, i, 0)), pl.BlockSpec((DC, DC), lambda i: (0, 0)),
                   pl.BlockSpec((40, DC), lambda i: (0, 0))],
        out_shape=[jax.ShapeDtypeStruct((TP, DC), F32), jax.ShapeDtypeStruct((2, TP, DC), BF16),
                   jax.ShapeDtypeStruct((DC, DC), F32), jax.ShapeDtypeStruct((40, DC), F32)],
        scratch_shapes=[pltpu.VMEM((TM, DC), BF16), pltpu.VMEM((TM, DC), BF16), pltpu.VMEM((TM, DC), F32)],
        compiler_params=_cparams(),
    )(dycat, p, z, xhat, rstd, dycat, hs, z, ln_w, ln_b, pw_w, after)


def _conf_bwd_conv(dvc, z, dw_w, after):
    def body(dvc_ref, u1_ref, u2_ref, w_ref, after_ref, du_ref, dw_ref, vecs_ref, vs, dvs):
        vs[pl.ds(0, KWP), :] = jnp.zeros((KWP, CBC), F32)
        dvs[pl.ds(TP, KWP), :] = jnp.zeros((KWP, CBC), F32)
        dw_ref[...] = jnp.zeros_like(dw_ref)
        vecs_ref[...] = jnp.zeros_like(vecs_ref)

        def fill_chunk(ci, carry):
            r0 = pl.multiple_of(ci * RC, RC)
            vs[pl.ds(KWP + r0, RC), :] = u1_ref[pl.ds(r0, RC), :] * _sig(u2_ref[pl.ds(r0, RC), :])
            dv = dvc_ref[pl.ds(r0, RC), :]
            dvs[pl.ds(r0, RC), :] = dv
            vecs_ref[0:8, :] += _fold_rows(dv)
            return carry
        lax.fori_loop(0, TP // RC, fill_chunk, 0)

        def conv_chunk(ci, carry):
            r0 = pl.multiple_of(ci * RC, RC)
            vbuf = vs[pl.ds(r0, KWP + RC), :]
            dbuf = dvs[pl.ds(r0, KWP + RC), :]
            dcur = dbuf[0:RC, :]
            dv = jnp.zeros((RC, CBC), F32)
            for rr in range(8):
                vroll = vbuf if rr == 0 else pltpu.roll(vbuf, rr, 0)
                droll = dbuf if rr == 0 else pltpu.roll(dbuf, KWP + RC - rr, 0)
                for q in range(4):
                    s = 8 * q + rr
                    if s > KW - 1:
                        continue
                    k = KW - 1 - s
                    dv = dv + droll[8 * q:8 * q + RC, :] * w_ref[k:k + 1, :]
                    dw_ref[8 * k:8 * k + 8, :] += _fold_rows(dcur * vroll[KWP - 8 * q:KWP - 8 * q + RC, :])
            u1 = u1_ref[pl.ds(r0, RC), :]
            sg = _sig(u2_ref[pl.ds(r0, RC), :])
            du1 = dv * sg
            du2 = dv * u1 * (sg * (1.0 - sg))
            du_ref[0, pl.ds(r0, RC), :] = du1.astype(BF16)
            du_ref[1, pl.ds(r0, RC), :] = du2.astype(BF16)
            vecs_ref[8:16, :] += _fold_rows(du1)
            vecs_ref[16:24, :] += _fold_rows(du2)
            return carry
        lax.fori_loop(0, TP // RC, conv_chunk, 0)

    blk = pl.BlockSpec((TP, CBC), lambda j: (0, j))
    return pl.pallas_call(
        body, name="conf_bwd_conv",
        grid=(NCBC,),
        in_specs=[blk, pl.BlockSpec((TP, CBC), lambda j: (0, 2 * NCBC + j)),
                  pl.BlockSpec((TP, CBC), lambda j: (0, 3 * NCBC + j)), pl.BlockSpec((KWP, CBC), lambda j: (0, j)),
                  _AFTER],
        out_specs=[pl.BlockSpec((2, TP, CBC), lambda j: (0, 0, j)), pl.BlockSpec((8 * KWP, CBC), lambda j: (0, j)),
                   pl.BlockSpec((24, CBC), lambda j: (0, j))],
        out_shape=[jax.ShapeDtypeStruct((2, TP, DC), BF16),
                   jax.ShapeDtypeStruct((8 * KWP, DC), F32), jax.ShapeDtypeStruct((24, DC), F32)],
        scratch_shapes=[pltpu.VMEM((TP + KWP, CBC), F32), pltpu.VMEM((TP + KWP, CBC), F32)],
        compiler_params=_cparams(),
    )(dvc, z, z, dw_w, after)


def _lru_bwd(dycat, z, xc, hs, conv_w, wa_g, b_a, wx_g, b_x, lam, after):
    NV = 6

    def body(dy_ref, x_ref, g_ref, xc_ref, hs_ref, cw_ref, wa_ref, ba_ref, wx_ref, bx_ref, lam_ref, after_ref,
             dzl_ref, dwa_ref, dwx_ref, dcw_ref, vecs_ref, ga_s, gx_s, dxc_s):
        vecs_ref[...] = jnp.zeros_like(vecs_ref)
        dcw_ref[...] = jnp.zeros_like(dcw_ref)
        dxc_s[pl.ds(TP, 8), :] = jnp.zeros((8, CB), F32)

        def gate_chunk(ci, carry):
            r0 = pl.multiple_of(ci * TM, TM)
            xb = xc_ref[pl.ds(r0, TM), :].astype(BF16)
            ga_s[pl.ds(r0, TM), :] = jnp.dot(xb, wa_ref[...], preferred_element_type=F32) + ba_ref[...]
            gx_s[pl.ds(r0, TM), :] = jnp.dot(xb, wx_ref[...], preferred_element_type=F32) + bx_ref[...]
            return carry
        lax.fori_loop(0, TP // TM, gate_chunk, 0)

        sp8 = LRU_C * _softplus(-lam_ref[...])
        row = _row_iota((R, CB))
        nchunk = TP // R

        def scan_chunk(cj, carry):
            a_next, lam_next = carry
            ci = nchunk - 1 - cj
            r0 = pl.multiple_of(ci * R, R)
            dyv = dy_ref[pl.ds(r0, R), :]
            g = g_ref[pl.ds(r0, R), :]
            hv = hs_ref[pl.ds(r0, R), :]
            xc = xc_ref[pl.ds(r0, R), :]
            sg = _sig(g)
            dhs = dyv * (g * sg)
            r, i, a, mult = _gate_values(ga_s[pl.ds(r0, R), :], gx_s[pl.ds(r0, R), :], xc, sp8)
            b = jnp.where(row == R - 1, a_next, pltpu.roll(a, R - 1, 0))
            lv = dhs
            k = 1
            while k < R:
                m = row < R - k
                lv = jnp.where(m, lv + b * pltpu.roll(lv, R - k, 0), lv)
                b = jnp.where(m, b * pltpu.roll(b, R - k, 0), b)
                k *= 2
            lv = lv + b * lam_next
            p0 = pl.multiple_of(jnp.maximum(r0 - 8, 0), 8)
            hprev8 = jnp.where(ci > 0, hs_ref[pl.ds(p0, 8), :], 0.0)
            hprev = pltpu.roll(jnp.concatenate([hprev8, hv], axis=0), 1, 0)[8:8 + R, :]
            da = lv * hprev
            ixc = i * xc
            dmult = lv * ixc
            di = lv * mult * xc
            dxc_s[pl.ds(r0, R), :] = lv * mult * i
            a2 = a * a
            dlog_a = da * a - dmult * a2 / mult
            vecs_ref[32:40, :] += _fold8(dlog_a * r)
            dga = -(dlog_a * sp8) * r * (1.0 - r)
            dgx = di * i * (1.0 - i)
            ga_s[pl.ds(r0, R), :] = dga
            gx_s[pl.ds(r0, R), :] = dgx
            vecs_ref[16:24, :] += _fold8(dga)
            vecs_ref[24:32, :] += _fold8(dgx)
            a_first = jnp.sum(jnp.where(row == 0, a, 0.0), axis=0, keepdims=True)
            l_first = jnp.sum(jnp.where(row == 0, lv, 0.0), axis=0, keepdims=True)
            return a_first, l_first
        lax.fori_loop(0, nchunk // 2, lambda i, cr: scan_chunk(2 * i + 1, scan_chunk(2 * i, cr)),
                      (jnp.zeros((1, CB), F32), jnp.zeros((1, CB), F32)))

        dwa_ref[...] = jnp.zeros_like(dwa_ref)
        dwx_ref[...] = jnp.zeros_like(dwx_ref)

        def mm_chunk(ci, carry):
            r0 = pl.multiple_of(ci * TM, TM)
            xb = xc_ref[pl.ds(r0, TM), :].astype(BF16)
            dgab = ga_s[pl.ds(r0, TM), :].astype(BF16)
            dgxb = gx_s[pl.ds(r0, TM), :].astype(BF16)
            dxc_s[pl.ds(r0, TM), :] += (lax.dot_general(dgab, wa_ref[...], _NT, preferred_element_type=F32)
                                        + lax.dot_general(dgxb, wx_ref[...], _NT, preferred_element_type=F32))
            dwa_ref[...] += lax.dot_general(xb, dgab, _TN, preferred_element_type=F32)
            dwx_ref[...] += lax.dot_general(xb, dgxb, _TN, preferred_element_type=F32)
            return carry
        lax.fori_loop(0, TP // TM, mm_chunk, 0)

        taps = [cw_ref[k:k + 1, :] for k in range(LW)]

        def conv_chunk(ci, carry):
            r0 = pl.multiple_of(ci * R, R)
            dbuf = dxc_s[pl.ds(r0, R + 8), :]
            dcur = dbuf[0:R, :]
            p0 = pl.multiple_of(jnp.maximum(r0 - 8, 0), 8)
            xprev = jnp.where(ci > 0, x_ref[pl.ds(p0, 8), :], 0.0)
            xbuf = jnp.concatenate([xprev, x_ref[pl.ds(r0, R), :]], axis=0)
            dxl = dcur * taps[LW - 1]
            dcw_ref[8 * (LW - 1):8 * LW, :] += _fold8(dcur * xbuf[8:8 + R, :])
            for s in range(1, LW):
                k = LW - 1 - s
                dxl = dxl + pltpu.roll(dbuf, R + 8 - s, 0)[0:R, :] * taps[k]
                dcw_ref[8 * k:8 * k + 8, :] += _fold8(dcur * pltpu.roll(xbuf, s, 0)[8:8 + R, :])
            dzl_ref[0, pl.ds(r0, R), :] = dxl.astype(BF16)
            vecs_ref[8:16, :] += _fold8(dxl)
            vecs_ref[40:48, :] += _fold8(dcur)
            return carry
        lax.fori_loop(0, TP // R, conv_chunk, 0)
        vecs_ref[32:40, :] = vecs_ref[32:40, :] * (LRU_C * _sig(-lam_ref[...]))

    col = lambda off: pl.BlockSpec((TP, CB), lambda j: (0, off + j))
    vec = pl.BlockSpec((1, CB), lambda j: (0, j))
    wsp = pl.BlockSpec((None, CB, CB), lambda j: (j, 0, 0))
    return pl.pallas_call(
        body, name="lru_bwd",
        grid=(NCB,),
        in_specs=[col(0), col(0), col(NCB), col(0), col(0), pl.BlockSpec((LW, CB), lambda j: (0, j)),
                  wsp, vec, wsp, vec, vec, _AFTER],
        out_specs=[pl.BlockSpec((1, TP, CB), lambda j: (0, 0, j)), wsp, wsp,
                   pl.BlockSpec((8 * LW, CB), lambda j: (0, j)), pl.BlockSpec((8 * NV, CB), lambda j: (0, j))],
        out_shape=[jax.ShapeDtypeStruct((1, TP, DL), BF16),
                   jax.ShapeDtypeStruct((NCB, CB, CB), F32), jax.ShapeDtypeStruct((NCB, CB, CB), F32),
                   jax.ShapeDtypeStruct((8 * LW, DL), F32), jax.ShapeDtypeStruct((8 * NV, DL), F32)],
        scratch_shapes=[pltpu.VMEM((TP, CB), F32), pltpu.VMEM((TP, CB), F32), pltpu.VMEM((TP + 8, CB), F32)],
        compiler_params=_cparams(),
    )(dycat, z, z, xc, hs, conv_w, wa_g, b_a, wx_g, b_x, lam, after)


def _dz_section(sec, dzl_ref, dz41_ref, dzc_ref, use):
    @pl.when(sec == 0)
    def _():
        use(dzl_ref)

    @pl.when(jnp.logical_or(sec == 1, sec == 4))
    def _():
        use(dz41_ref)

    @pl.when(jnp.logical_or(sec == 2, sec == 3))
    def _():
        use(dzc_ref)


def _dz_specs(rows, index):
    return [pl.BlockSpec((None, rows, 1024), lambda a, b: (0, index(a, b)[0], 0)),
            pl.BlockSpec((None, rows, 1024), lambda a, b: (jnp.where(index(a, b)[1] == 1, 1, 0), index(a, b)[0], 0)),
            pl.BlockSpec((None, rows, 1024), lambda a, b: (jnp.clip(index(a, b)[1] - 2, 0, 1), index(a, b)[0], 0))]


def _inproj_wgrad(name, hn, dzs, after):
    KB = 512
    nsec = dzs.shape[0]

    def body(hn_ref, dz_ref, after_ref, dw_ref):
        dw_ref[...] = lax.dot_general(hn_ref[...], dz_ref[...], _TN, preferred_element_type=F32).astype(BF16)

    return pl.pallas_call(
        body, name=name,
        grid=(nsec, D // KB),
        in_specs=[pl.BlockSpec((TP, KB), lambda n, kb: (0, kb)),
                  pl.BlockSpec((None, TP, 1024), lambda n, kb: (n, 0, 0)), _AFTER],
        out_specs=pl.BlockSpec((KB, 1024), lambda n, kb: (kb, n)),
        out_shape=jax.ShapeDtypeStruct((D, nsec * 1024), BF16),
        compiler_params=_cparams(),
    )(hn, dzs, after)


_SUM_RB = 64


def _sum_sources(ref, o_ref):
    def chunk(ci, carry):
        r0 = pl.multiple_of(ci * R, R)
        for c0 in range(0, 1024, 512):
            acc = ref[0, pl.ds(r0, R), c0:c0 + 512].astype(F32)
            for sidx in range(1, NDEV):
                acc = acc + ref[sidx, pl.ds(r0, R), c0:c0 + 512].astype(F32)
            o_ref[pl.ds(r0, R), c0:c0 + 512] = acc.astype(BF16)
        return carry
    lax.fori_loop(0, _SUM_RB // R, chunk, 0)


def _sum_win_late_sections(parts_b, parts_c):
    def body(b_ref, c_ref, o_ref):
        j = pl.program_id(1)

        @pl.when(jnp.logical_or(j == 0, j == 3))
        def _():
            _sum_sources(c_ref, o_ref)

        @pl.when(jnp.logical_or(j == 1, j == 2))
        def _():
            _sum_sources(b_ref, o_ref)

    return pl.pallas_call(
        body, name="sum_win_late_sections",
        grid=(D // NDEV // _SUM_RB, 4),
        in_specs=[pl.BlockSpec((NDEV, _SUM_RB, 1024), lambda i, j: (0, i, jnp.clip(j - 1, 0, 1))),
                  pl.BlockSpec((NDEV, _SUM_RB, 1024), lambda i, j: (0, i, jnp.where(j == 0, 1, 0)))],
        out_specs=pl.BlockSpec((_SUM_RB, 1024), lambda i, j: (i, j + 1)),
        out_shape=jax.ShapeDtypeStruct((D // NDEV, NIN), BF16),
        compiler_params=_cparams(),
    )(parts_b, parts_c)


def _sum_win_first_section(parts_a, rows):
    def body(a_ref, rows_ref, o_ref):
        _sum_sources(a_ref, o_ref)

    return pl.pallas_call(
        body, name="sum_win_first_section",
        grid=(D // NDEV // _SUM_RB,),
        in_specs=[pl.BlockSpec((NDEV, _SUM_RB, 1024), lambda i: (0, i, 0)), pl.BlockSpec(memory_space=pl.ANY)],
        out_specs=pl.BlockSpec((_SUM_RB, 1024), lambda i: (i, 0)),
        out_shape=jax.ShapeDtypeStruct((D // NDEV, NIN), BF16),
        input_output_aliases={1: 0},
        compiler_params=_cparams(),
    )(parts_a, rows)


def _inproj_bwd(dzl, dz41, dzc, w_in, h, dout, pre_w, after):
    nsec = NIN // 1024

    def body(dzl_ref, dz41_ref, dzc_ref, w_ref, h_ref, dout_ref, pw_ref, after_ref, gx_hbm, dmeta_ref, dpw_ref,
             acc_s, dh_s, sem):
        i = pl.program_id(0)
        s = pl.program_id(1)

        def gx_copy(t):
            lo, n, off = _tile_rows(t)
            return pltpu.make_async_copy(dh_s.at[pl.ds(off, n)], gx_hbm.at[pl.ds(lo, n)], sem)

        @pl.when(s == 0)
        def _():
            acc_s[...] = jnp.zeros_like(acc_s)

        def use(dz_ref):
            acc_s[...] += lax.dot_general(dz_ref[...], w_ref[...], _NT, preferred_element_type=F32)
        _dz_section(s, dzl_ref, dz41_ref, dzc_ref, use)

        @pl.when(jnp.logical_and(i == 0, s == nsec - 1))
        def _():
            dpw_ref[...] = jnp.zeros_like(dpw_ref)

        @pl.when(s == nsec - 1)
        def _():
            _for_tile(i - 1, lambda t: gx_copy(t).wait())
            pw = pw_ref[...]

            def chunk(ci, carry):
                r0 = pl.multiple_of(ci * 8, 8)
                hv = h_ref[pl.ds(r0, 8), :]
                dhn = acc_s[pl.ds(r0, 8), :]
                rs = lax.rsqrt(jnp.mean(hv * hv, axis=-1, keepdims=True) + EPS)
                dpw_ref[...] += dhn * (hv * rs)
                gw = dhn * pw
                dot = jnp.mean(gw * hv, axis=-1, keepdims=True)
                dh_s[pl.ds(r0, 8), :] = rs * gw - hv * (rs * rs * rs * dot) + dout_ref[pl.ds(r0, 8), :]
                return carry
            lax.fori_loop(0, TM // 8, chunk, 0, unroll=4)
            _for_tile(i, lambda t: gx_copy(t).start())

            @pl.when(i == 0)
            def _():
                dmeta_ref[...] = dh_s[0:NMETA, :]

            @pl.when(i == NTILE - 1)
            def _():
                gx_copy(NTILE - 1).wait()

    row = pl.BlockSpec((TM, D), lambda i, s: (i, 0))
    return pl.pallas_call(
        body, name="inproj_bwd",
        grid=(TP // TM, nsec),
        in_specs=_dz_specs(TM, lambda i, s: (i, s)) + [
            pl.BlockSpec((D, 1024), lambda i, s: (0, s)), row, row, pl.BlockSpec((1, D), lambda i, s: (0, 0)),
            _AFTER],
        out_specs=[pl.BlockSpec(memory_space=pl.ANY), pl.BlockSpec((NMETA, D), lambda i, s: (0, 0)),
                   pl.BlockSpec((8, D), lambda i, s: (0, 0))],
        out_shape=[jax.ShapeDtypeStruct((SEQ, D), F32), jax.ShapeDtypeStruct((NMETA, D), F32),
                   jax.ShapeDtypeStruct((8, D), F32)],
        scratch_shapes=[pltpu.VMEM((TM, D), F32), pltpu.VMEM((TM, D), F32), pltpu.SemaphoreType.DMA(())],
        compiler_params=_cparams(),
    )(dzl, dz41, dzc, w_in, h, dout, pre_w, after)


def _adamw(name, parts, w, m, v, block_rows):
    rows, cols = w.shape
    nparts = parts.shape[0]
    cw = cols if cols <= 640 else 512

    def body(p_ref, w_ref, m_ref, v_ref, g_ref, d_ref, nm_ref, nv_ref):
        def chunk(ci, carry):
            r0 = pl.multiple_of(ci * R, R)
            for c0 in range(0, cols, cw):
                at = (pl.ds(r0, R), slice(c0, c0 + cw))
                g = p_ref[(0,) + at].astype(F32)
                for sidx in range(1, nparts):
                    g = g + p_ref[(sidx,) + at].astype(F32)
                delta, mv, vv = _adam_math(g, w_ref[at], m_ref[at], v_ref[at])
                g_ref[at] = g
                nm_ref[at] = mv
                nv_ref[at] = vv
                d_ref[at] = delta
            return carry
        lax.fori_loop(0, block_rows // R, chunk, 0)

    blk = pl.BlockSpec((block_rows, cols), lambda i: (i, 0))
    shp = jax.ShapeDtypeStruct((rows, cols), F32)
    return pl.pallas_call(
        body, name=name,
        grid=(rows // block_rows,),
        in_specs=[pl.BlockSpec((nparts, block_rows, cols), lambda i: (0, i, 0)), blk, blk, blk],
        out_specs=[blk, blk, blk, blk],
        out_shape=[shp, shp, shp, shp],
        compiler_params=_cparams(),
    )(parts, w, m, v)


def _adam_math(g, w, m, v):
    c1 = 1.0 / (1.0 - ADAM_B1 ** ADAM_STEP)
    c2 = 1.0 / (1.0 - ADAM_B2 ** ADAM_STEP)
    mv = ADAM_B1 * m + (1.0 - ADAM_B1) * g
    vv = ADAM_B2 * v + (1.0 - ADAM_B2) * (g * g)
    upd = (mv * c1) / (jnp.sqrt(vv * c2) + ADAM_EPS) + ADAM_WD * w
    return -ADAM_LR * upd, mv, vv


_VEC = [("pre_norm_w", 2), ("post_norm_w", 2), ("b_in", 5), ("lru_conv_b", 1), ("b_gate_a", 1), ("b_gate_x", 1),
        ("lru_lambda", 1), ("conf_dw_b", 1), ("conf_ln_w", 1), ("conf_ln_b", 1), ("conf_pw_b", 1)]
_VEC_ROWS = 24
_LOSS_ROW = 17
_SM_ROWS = 64


def _pack_grads(dprew_acc, dpostw_acc, cvecs, kvecs, lvecs, dcw_acc, ddw_acc, dh, loss_acc):
    def body(pre_ref, post_ref, c_ref, k_ref, l_ref, dcw_ref, ddw_ref, dh_ref, loss_ref, vec_ref, small_ref, tmp):
        s8 = lambda ref, r: jnp.sum(ref[8 * r:8 * r + 8, :], axis=0, keepdims=True)
        vec_ref[...] = jnp.zeros_like(vec_ref)
        pre, post = s8(pre_ref, 0), s8(post_ref, 0)
        rows = [pre[:, 0:1024], pre[:, 1024:2048], post[:, 0:1024], post[:, 1024:2048],
                s8(l_ref, 1), s8(c_ref, 4), s8(k_ref, 1), s8(k_ref, 2), s8(c_ref, 1),
                s8(l_ref, 5), s8(l_ref, 2), s8(l_ref, 3), s8(l_ref, 4),
                s8(k_ref, 0), s8(c_ref, 2), s8(c_ref, 3), s8(c_ref, 0)]
        for r, val in enumerate(rows):
            vec_ref[r:r + 1, :] = val
        vec_ref[_LOSS_ROW:_LOSS_ROW + 1, :] = jnp.zeros((1, 1024), F32) + (0.5 / D) * jnp.sum(loss_ref[...])

        small_ref[...] = jnp.zeros_like(small_ref)
        for k in range(LW):
            tmp[k:k + 1, :] = s8(dcw_ref, k)
        for k in range(KW):
            tmp[8 + k:9 + k, :] = s8(ddw_ref, k)
        for d in range(NDEV):
            small_ref[d, 0:LW, 0:128] = tmp[0:LW, 128 * d:128 * d + 128]
            small_ref[d, 8:8 + KW, 0:128] = tmp[8:8 + KW, 128 * d:128 * d + 128]
            small_ref[d, 40:56, :] = dh_ref[:, 256 * d:256 * d + 256]

    full = lambda a: pl.BlockSpec(a.shape, lambda i: (0,) * a.ndim)
    ins = [dprew_acc, dpostw_acc, cvecs, kvecs, lvecs, dcw_acc, ddw_acc]
    return pl.pallas_call(
        body, name="pack_grads",
        grid=(1,),
        in_specs=[full(a) for a in ins] + [full(dh), full(loss_acc)],
        out_specs=[pl.BlockSpec((_VEC_ROWS, 1024), lambda i: (0, 0)),
                   pl.BlockSpec((NDEV, _SM_ROWS, 256), lambda i: (0, 0, 0))],
        out_shape=[jax.ShapeDtypeStruct((_VEC_ROWS, 1024), F32), jax.ShapeDtypeStruct((NDEV, _SM_ROWS, 256), F32)],
        scratch_shapes=[pltpu.VMEM((40, 1024), F32)],
        compiler_params=_cparams(),
    )(*ins, dh, loss_acc)


def _adamw_vec(parts, W, M, V):
    nv = len(_VEC)

    def body(*refs):
        p_ref = refs[0]
        w_refs, m_refs, v_refs = refs[1:1 + nv], refs[1 + nv:1 + 2 * nv], refs[1 + 2 * nv:1 + 3 * nv]
        outs = refs[1 + 3 * nv:]

        def total(r):
            acc = p_ref[0, r:r + 1, :]
            for sidx in range(1, NDEV):
                acc = acc + p_ref[sidx, r:r + 1, :]
            return acc

        row = 0
        for idx, (_, nrows) in enumerate(_VEC):
            for part in range(nrows):
                cols = slice(1024 * part, 1024 * part + 1024)
                g = total(row + part)
                delta, mv, vv = _adam_math(g, w_refs[idx][:, cols], m_refs[idx][:, cols], v_refs[idx][:, cols])
                for o, val in zip(outs[4 * idx:4 * idx + 4], (g, delta, mv, vv)):
                    o[:, cols] = val
            row += nrows
        outs[-1][...] = total(_LOSS_ROW)[:, 0:128]

    names = [n for n, _ in _VEC]
    flat = lambda d: [d[n].reshape(1, -1) for n in names]
    ws, ms, vs = flat(W), flat(M), flat(V)
    res = pl.pallas_call(
        body, name="adamw_vec",
        out_shape=[jax.ShapeDtypeStruct(w.shape, F32) for w in ws for _ in range(4)]
        + [jax.ShapeDtypeStruct((1, 128), F32)],
        compiler_params=_cparams(),
    )(parts, *ws, *ms, *vs)
    return {n: tuple(res[4 * i:4 * i + 4]) for i, n in enumerate(names)}, res[-1]


def _adamw_small(parts, W, M, V):
    where = {"lru_conv_w": (slice(0, LW), slice(0, 128)), "conf_dw_w": (slice(8, 8 + KW), slice(0, 128)),
             "meta_tokens": (slice(40, 56), slice(0, 256))}
    names = list(where)

    def body(*refs):
        p_ref = refs[0]
        outs = refs[10:]
        for idx, n in enumerate(names):
            rs, cs = where[n]
            g = p_ref[0, rs, cs]
            for sidx in range(1, NDEV):
                g = g + p_ref[sidx, rs, cs]
            delta, mv, vv = _adam_math(g, refs[1 + idx][...], refs[4 + idx][...], refs[7 + idx][...])
            for o, val in zip(outs[4 * idx:4 * idx + 4], (g, delta, mv, vv)):
                o[...] = val

    two_d = lambda a: a.reshape(a.shape[-2:])
    ws, ms, vs = ([two_d(d[n]) for n in names] for d in (W, M, V))
    res = pl.pallas_call(
        body, name="adamw_small",
        out_shape=[jax.ShapeDtypeStruct(w.shape, F32) for w in ws for _ in range(4)],
        compiler_params=_cparams(),
    )(parts, *ws, *ms, *vs)
    return {n: tuple(res[4 * i:4 * i + 4]) for i, n in enumerate(names)}


def _pack_small(lru_cw, dw_w, meta):
    buf = jnp.zeros((_SM_ROWS, 256), F32)
    buf = buf.at[0:LW, 0:128].set(lru_cw)
    buf = buf.at[8:8 + dw_w.shape[0], 0:128].set(dw_w)
    return buf.at[40:56, :].set(meta)


def _block_diag4(w):
    w4 = w.reshape(NCB, 4, 64, 64)
    eye = jnp.eye(4, dtype=w.dtype)
    return jnp.einsum("ghij,hk->ghikj", w4, eye).reshape(NCB, CB, CB)


def _diag_blocks(g):
    g5 = g.reshape(NCB, 4, 64, 4, 64)
    return jnp.stack([g5[:, hh, :, hh, :] for hh in range(4)], axis=1).reshape(16, 64, 64)


def _gate_mats(W):
    return _block_diag4(W["w_gate_a"][0]).astype(BF16), _block_diag4(W["w_gate_x"][0]).astype(BF16)


def _local_step(x, target, meta_full, inproj, out_weights, lru_cw_full, dw_w_full, W, gate_mats, send, before_last):
    wa_g, wx_g = gate_mats

    h, hn = _prenorm(x, meta_full, W["pre_norm_w"])
    z, win_full = inproj(hn)
    ylru, xc, hs = _lru_fwd(z, lru_cw_full, W["lru_conv_b"], wa_g, W["b_gate_a"], wx_g, W["b_gate_x"],
                            W["lru_lambda"])
    vc = _conf_fwd_conv(z, dw_w_full, W["conf_dw_b"])
    wout_full, pw_full = out_weights(vc)
    yconf, p, xhat, rstd = _conf_fwd_proj(vc, z, W["conf_ln_w"], W["conf_ln_b"], pw_full, W["conf_pw_b"])
    dout, dy, loss_acc, dpostw_acc = _outproj_loss(ylru, yconf, wout_full, h, target, W["post_norm_w"])

    dycat, dwout_part = _outproj_bwd(dy, ylru, yconf, wout_full)
    tok = send("w_out", ("w_out", dwout_part))
    dvc, dz41, dpw_part, cvecs = _conf_bwd_proj(dycat, p, z, xhat, rstd, hs, W["conf_ln_w"], W["conf_ln_b"], pw_full, tok)
    tok = send("w_in_c", ("conf_pw_w", dpw_part), ("w_in_c", _inproj_wgrad("inproj_wgrad_c", hn, dz41, dz41)))
    dzc, ddw_acc, kvecs = _conf_bwd_conv(dvc, z, dw_w_full, tok)
    tok = send("w_in_b", ("w_in_b", _inproj_wgrad("inproj_wgrad_b", hn, dzc, dzc)))
    dzl, dwa_g, dwx_g, dcw_acc, lvecs = _lru_bwd(dycat, z, xc, hs, lru_cw_full, wa_g, W["b_gate_a"], wx_g,
                                                 W["b_gate_x"], W["lru_lambda"], tok)
    tok = send("w_gates", ("w_gate_a", _diag_blocks(dwa_g).reshape(16 * 64, 64)),
               ("w_gate_x", _diag_blocks(dwx_g).reshape(16 * 64, 64)))
    tok = send("w_in_a", ("w_in_a", _inproj_wgrad("inproj_wgrad_a", hn, dzl, tok)))
    grad_x, dmeta, dprew_acc = _inproj_bwd(dzl, dz41, dzc, win_full, h, dout, W["pre_norm_w"], before_last(tok))

    vec_pack, small_part = _pack_grads(dprew_acc, dpostw_acc, cvecs, kvecs, lvecs, dcw_acc, ddw_acc, dmeta, loss_acc)
    return grad_x, vec_pack, small_part


def kernel(x, meta_tokens, pre_norm_w, post_norm_w, w_in, b_in, lru_conv_w, lru_conv_b, w_gate_a, b_gate_a, w_gate_x, b_gate_x, lru_lambda, conf_dw_w, conf_dw_b, conf_ln_w, conf_ln_b, conf_pw_w, conf_pw_b, w_out, loss_target, m_meta_tokens, m_pre_norm_w, m_post_norm_w, m_w_in, m_b_in, m_lru_conv_w, m_lru_conv_b, m_w_gate_a, m_b_gate_a, m_w_gate_x, m_b_gate_x, m_lru_lambda, m_conf_dw_w, m_conf_dw_b, m_conf_ln_w, m_conf_ln_b, m_conf_pw_w, m_conf_pw_b, m_w_out, v_meta_tokens, v_pre_norm_w, v_post_norm_w, v_w_in, v_b_in, v_lru_conv_w, v_lru_conv_b, v_w_gate_a, v_b_gate_a, v_w_gate_x, v_b_gate_x, v_lru_lambda, v_conf_dw_w, v_conf_dw_b, v_conf_ln_w, v_conf_ln_b, v_conf_pw_w, v_conf_pw_b, v_w_out):
    W = dict(meta_tokens=meta_tokens, pre_norm_w=pre_norm_w, post_norm_w=post_norm_w, w_in=w_in, b_in=b_in,
             lru_conv_w=lru_conv_w, lru_conv_b=lru_conv_b, w_gate_a=w_gate_a, b_gate_a=b_gate_a,
             w_gate_x=w_gate_x, b_gate_x=b_gate_x, lru_lambda=lru_lambda, conf_dw_w=conf_dw_w,
             conf_dw_b=conf_dw_b, conf_ln_w=conf_ln_w, conf_ln_b=conf_ln_b, conf_pw_w=conf_pw_w,
             conf_pw_b=conf_pw_b, w_out=w_out)
    M = dict(meta_tokens=m_meta_tokens, pre_norm_w=m_pre_norm_w, post_norm_w=m_post_norm_w, w_in=m_w_in,
             b_in=m_b_in, lru_conv_w=m_lru_conv_w, lru_conv_b=m_lru_conv_b, w_gate_a=m_w_gate_a,
             b_gate_a=m_b_gate_a, w_gate_x=m_w_gate_x, b_gate_x=m_b_gate_x, lru_lambda=m_lru_lambda,
             conf_dw_w=m_conf_dw_w, conf_dw_b=m_conf_dw_b, conf_ln_w=m_conf_ln_w, conf_ln_b=m_conf_ln_b,
             conf_pw_w=m_conf_pw_w, conf_pw_b=m_conf_pw_b, w_out=m_w_out)
    V = dict(meta_tokens=v_meta_tokens, pre_norm_w=v_pre_norm_w, post_norm_w=v_post_norm_w, w_in=v_w_in,
             b_in=v_b_in, lru_conv_w=v_lru_conv_w, lru_conv_b=v_lru_conv_b, w_gate_a=v_w_gate_a,
             b_gate_a=v_b_gate_a, w_gate_x=v_w_gate_x, b_gate_x=v_b_gate_x, lru_lambda=v_lru_lambda,
             conf_dw_w=v_conf_dw_w, conf_dw_b=v_conf_dw_b, conf_ln_w=v_conf_ln_w, conf_ln_b=v_conf_ln_b,
             conf_pw_w=v_conf_pw_w, conf_pw_b=v_conf_pw_b, w_out=v_w_out)
    names = list(W.keys())
    shapes = {n: W[n].shape for n in names}

    small = _pack_small(lru_conv_w[0], conf_dw_w[0], meta_tokens)
    (small_flight,), tok = _exchange_start("gather_small_start", [
        (small, jax.ShapeDtypeStruct((NDEV, _SM_ROWS, 256), F32), _whole, _slot)])
    win_flight, tok = _win_gather_start(w_in[0].astype(BF16) + tok[0, 0].astype(BF16))
    gate_mats = _gate_mats(W)
    wout_shard = w_out[0].astype(BF16) + tok[0, 0].astype(BF16)
    pw_shard = conf_pw_w[0].astype(BF16)
    cast_done = (gate_mats[0][0, 0:8, 0:128] + gate_mats[1][0, 0:8, 0:128]
                 + wout_shard[0:8, 0:128] + pw_shard[0:8, 0:128])
    win_flight, tok = _win_gather_links(win_flight, cast_done)
    gathered, tok = _exchange_start("gather_out_start", [
        (wout_shard + tok[0, 0].astype(BF16), jax.ShapeDtypeStruct((D, D), BF16), _whole, _rows(D // NDEV)),
        (pw_shard, jax.ShapeDtypeStruct((DC, DC), BF16), _whole, _rows(DC // NDEV)),
    ])
    (small_all,) = _exchange_wait("gather_small_wait", [small_flight], tok)
    unshard = lambda a: jnp.transpose(a, (1, 0, 2)).reshape(a.shape[1], -1)
    lru_cw_full = unshard(small_all[:, 0:LW, 0:128])
    dw_w_full = unshard(small_all[:, 8:8 + KWP, 0:128])
    meta_full = unshard(small_all[:, 40:56, :])

    def out_weights(after):
        return _exchange_wait("gather_out_wait", gathered, after)

    def inproj(hn):
        xi, yi, ci = lax.axis_index("x"), lax.axis_index("y"), lax.axis_index("c")
        shard = lambda px, py, pc: (4 * px + 2 * py + pc).astype(jnp.int32)
        over_links = jnp.stack([shard(1 - xi, yi, ci), shard(xi, 1 - yi, ci), shard(1 - xi, 1 - yi, ci)])
        z, src = _inproj_cols("inproj_own", jnp.stack([shard(xi, yi, ci)]), hn, win_flight["src"], b_in, None)
        flight = _win_gather_early(dict(win_flight, src=src))
        z, land = _inproj_cols("inproj_here", jnp.stack([shard(xi, yi, 1 - ci)]), hn, flight["land"], b_in, z)
        flight = _win_gather_forward("all", dict(flight, land=land), (1, 2, 3), z)
        z, land = _inproj_cols("inproj_links", over_links, hn, flight["land"], b_in, z)
        flight = _win_gather_forwarded("all", dict(flight, land=land), (1, 2, 3))
        z, land = _inproj_cols("inproj_sibling", over_links + 1 - 2 * ci, hn, flight["land"], b_in, z)
        return z, _win_gather_wait(dict(flight, land=land))

    row_stage = lambda ncol: (jax.ShapeDtypeStruct((NDEV, D // NDEV, ncol), BF16), _rows(D // NDEV))
    piece = {"w_in_a": row_stage(1024), "w_in_b": row_stage(2048), "w_in_c": row_stage(2048),
             "w_out": row_stage(D),
             "conf_pw_w": (jax.ShapeDtypeStruct((NDEV, DC // NDEV, DC), BF16), _rows(DC // NDEV)),
             "w_gate_a": (jax.ShapeDtypeStruct((NDEV, 16 * 64, 64), BF16), _whole),
             "w_gate_x": (jax.ShapeDtypeStruct((NDEV, 16 * 64, 64), BF16), _whole)}
    sent = {}

    def send(call, *named_parts):
        handles, token = _exchange_start(
            "scatter_" + call + "_start",
            [(part.astype(BF16), piece[name][0], piece[name][1], _slot) for name, part in named_parts])
        for (name, _), handle in zip(named_parts, handles):
            sent[name] = [handle]
        return token

    late = {}

    def before_last(tok):
        (parts_c,) = _exchange_wait("scatter_w_in_c_wait", sent["w_in_c"], tok)
        (parts_b,) = _exchange_wait("scatter_w_in_b_wait", sent["w_in_b"], parts_c)
        late["rows"] = _sum_win_late_sections(parts_b, parts_c)
        return late["rows"]

    grad_x, vec_pack, small_part = _local_step(
        x[0], loss_target[0], meta_full, inproj, out_weights, lru_cw_full, dw_w_full, W, gate_mats, send, before_last)
    grad_x = grad_x[None]

    rest, tok = _exchange_start("scatter_rest_start", [
        (small_part, jax.ShapeDtypeStruct((NDEV, _SM_ROWS, 256), F32), _slot, _slot),
        (vec_pack, jax.ShapeDtypeStruct((NDEV, _VEC_ROWS, 1024), F32), _whole, _slot),
    ])
    (parts_a,) = _exchange_wait("scatter_w_in_a_wait", sent["w_in_a"], tok)
    win_rows = _sum_win_first_section(parts_a, late["rows"])
    win_stage2, tok = _exchange_start("scatter_w_in_stage2_start", [
        (win_rows, jax.ShapeDtypeStruct((NDEV, D // NDEV, NIN // NDEV), BF16), _cols(NIN // NDEV), _slot)])

    G, DW, NM, NV = {}, {}, {}, {}
    (wout_parts,) = _exchange_wait("scatter_w_out_wait", sent["w_out"], tok)
    G["w_out"], DW["w_out"], NM["w_out"], NV["w_out"] = _adamw("adamw_w_out", wout_parts, w_out[0], m_w_out[0], v_w_out[0], 64)
    (pw_parts,) = _exchange_wait("scatter_conf_pw_w_wait", sent["conf_pw_w"], G["w_out"])
    G["conf_pw_w"], DW["conf_pw_w"], NM["conf_pw_w"], NV["conf_pw_w"] = _adamw(
        "adamw_pw", pw_parts, conf_pw_w[0], m_conf_pw_w[0], v_conf_pw_w[0], 128)
    res = {}
    wa_parts, wx_parts = _exchange_wait("scatter_w_gates_wait", sent["w_gate_a"] + sent["w_gate_x"], G["conf_pw_w"])
    for n, parts in (("w_gate_a", wa_parts), ("w_gate_x", wx_parts)):
        res[n] = _adamw("adamw_" + n, parts, *[d[n].reshape(16 * 64, 64) for d in (W, M, V)], 16 * 64)
    small_parts, vec_parts = _exchange_wait("scatter_rest_wait", rest, res["w_gate_x"][0])
    res.update(_adamw_small(small_parts, W, M, V))
    vec_res, loss_row = _adamw_vec(vec_parts, W, M, V)
    res.update(vec_res)
    (win_sum,) = _exchange_wait("scatter_w_in_stage2_wait", win_stage2, loss_row)
    res["w_in"] = _adamw("adamw_w_in", win_sum.reshape(1, D, NIN // NDEV), w_in[0], m_w_in[0], v_w_in[0], 256)
    for n, vals in res.items():
        for dst, val in zip((G, DW, NM, NV), vals):
            dst[n] = val
    for dst in (G, DW, NM, NV):
        for n in names:
            dst[n] = dst[n].reshape(shapes[n])
    loss = loss_row[0, 0]

    return (loss, grad_x, *[G[n] for n in names], *[DW[n] for n in names],
            *[NM[n] for n in names], *[NV[n] for n in names])
```

```python
import functools

import jax
import jax.numpy as jnp
from jax import lax
from jax.experimental import pallas as pl
from jax.experimental.pallas import tpu as pltpu

F32 = jnp.float32
BF16 = jnp.bfloat16

D = 2048
DL = 1024
DC = 1024
NIN = 5120
NMETA = 16
SEQ = 2048
T = NMETA + SEQ
TP = 2176
TM = 544
CB = 256
NCB = DL // CB
R = 16
KW = 31
KWP = 32
LW = 4
LRU_C = 8.0
EPS = 1e-6
NDEV = 8

ADAM_LR = 0.001
ADAM_B1 = 0.9
ADAM_B2 = 0.999
ADAM_EPS = 1e-08
ADAM_WD = 0.01
ADAM_STEP = 10

VMEM_LIMIT = 56 * 1024 * 1024


def _cparams():
    return pltpu.CompilerParams(vmem_limit_bytes=VMEM_LIMIT)


def _sig(x):
    return 1.0 / (1.0 + jnp.exp(-x))


def _expm1_neg(y):
    poly = y * (1.0 + y * (0.5 + y * (1.0 / 6.0 + y * (1.0 / 24.0 + y * (1.0 / 120.0)))))
    return jnp.where(y > -0.1, poly, jnp.exp(y) - 1.0)


def _softplus(x):
    e = jnp.exp(-jnp.abs(x))
    w = 1.0 + e
    l1p = jnp.where(w == 1.0, e, jnp.log(w) * e / (w - 1.0))
    return jnp.maximum(x, 0.0) + l1p


def _row_iota(shape):
    return lax.broadcasted_iota(jnp.int32, shape, 0)


def _fold8(v):
    return v[0:8, :] + v[8:16, :]


_FLIPS = [(k >> 2 & 1, k >> 1 & 1, k & 1) for k in range(1, NDEV)]
_HBM = pl.BlockSpec(memory_space=pltpu.HBM)
_SEM = pl.BlockSpec(memory_space=pltpu.SEMAPHORE)


def _peers():
    x, y, c = lax.axis_index("x"), lax.axis_index("y"), lax.axis_index("c")
    out = []
    for dx, dy, dc in _FLIPS:
        px = 1 - x if dx else x
        py = 1 - y if dy else y
        pc = 1 - c if dc else c
        out.append(((px, py, pc), 4 * px + 2 * py + pc))
    return 4 * x + 2 * y + c, out


def _exchange_start(name, items):
    n = len(items)

    def body(*refs):
        srcs, lands = refs[:n], refs[n:2 * n]
        outs = refs[2 * n:]
        send_sems, recv_sems, local_sems = outs[:n], outs[n:2 * n], outs[2 * n:3 * n]
        token = outs[-1]
        me, peers = _peers()
        for a in range(n):
            src_at, dst_at = items[a][2], items[a][3]
            pltpu.make_async_copy(src_at(srcs[a], me), dst_at(lands[a], me), local_sems[a]).start()
        for a in range(n):
            src_at, dst_at = items[a][2], items[a][3]
            for k, (pos, peer) in enumerate(peers):
                pltpu.make_async_remote_copy(
                    src_ref=src_at(srcs[a], peer), dst_ref=dst_at(lands[a], me),
                    send_sem=send_sems[a].at[k], recv_sem=recv_sems[a].at[k],
                    device_id=pos, device_id_type=pl.DeviceIdType.MESH).start()
        token[...] = jnp.zeros_like(token)

    srcs = [pltpu.with_memory_space_constraint(it[0], pltpu.HBM) for it in items]
    lands = [pltpu.with_memory_space_constraint(lax.empty(it[1].shape, it[1].dtype), pltpu.HBM) for it in items]
    sem7 = pltpu.SemaphoreType.DMA((NDEV - 1,))
    res = pl.pallas_call(
        body, name=name,
        out_shape=([sem7] * (2 * n) + [pltpu.SemaphoreType.DMA(())] * n
                   + [pltpu.HBM(a.shape, a.dtype) for a in srcs] + [pltpu.HBM(a.shape, a.dtype) for a in lands]
                   + [jax.ShapeDtypeStruct((8, 128), F32)]),
        in_specs=[_HBM] * (2 * n),
        out_specs=[_SEM] * (3 * n) + [_HBM] * (2 * n) + [pl.BlockSpec(memory_space=pltpu.VMEM)],
        input_output_aliases={i: 3 * n + i for i in range(2 * n)},
        compiler_params=pltpu.CompilerParams(has_side_effects=pltpu.SideEffectType.DATAFLOW_SIDE_EFFECTING),
    )(*srcs, *lands)
    handles = [dict(send=res[a], recv=res[n + a], local=res[2 * n + a], src=res[3 * n + a], land=res[4 * n + a],
                    src_at=items[a][2], dst_at=items[a][3]) for a in range(n)]
    return handles, res[-1]


def _wait_bytes(piece, sem):
    pltpu.make_async_copy(piece, piece, sem).wait()


def _exchange_wait(name, handles, after):
    n = len(handles)

    def body(*refs):
        srcs, lands = refs[:n], refs[n:2 * n]
        send_sems, recv_sems, local_sems = refs[2 * n:3 * n], refs[3 * n:4 * n], refs[4 * n:5 * n]
        me, peers = _peers()
        for a in range(n):
            src_at, dst_at = handles[a]["src_at"], handles[a]["dst_at"]
            for k, (pos, peer) in enumerate(peers):
                _wait_bytes(src_at(srcs[a], peer), send_sems[a].at[k])
                _wait_bytes(dst_at(lands[a], peer), recv_sems[a].at[k])
            pltpu.make_async_copy(src_at(srcs[a], me), dst_at(lands[a], me), local_sems[a]).wait()

    srcs = [hd["src"] for hd in handles]
    lands = [hd["land"] for hd in handles]
    res = pl.pallas_call(
        body, name=name,
        out_shape=[pltpu.HBM(a.shape, a.dtype) for a in srcs] + [pltpu.HBM(a.shape, a.dtype) for a in lands],
        in_specs=[_HBM] * (2 * n) + [_SEM] * (3 * n) + [pl.BlockSpec(memory_space=pl.ANY)],
        out_specs=[_HBM] * (2 * n),
        input_output_aliases={i: i for i in range(2 * n)},
        compiler_params=pltpu.CompilerParams(has_side_effects=pltpu.SideEffectType.DATAFLOW_SIDE_EFFECTING),
    )(*srcs, *lands, *[hd["send"] for hd in handles], *[hd["recv"] for hd in handles],
      *[hd["local"] for hd in handles], after)
    return list(res[n:])


_SIDE = pltpu.SideEffectType.DATAFLOW_SIDE_EFFECTING
_WCOLS = NIN // NDEV


def _win_cols(ref, l):
    return ref.at[:, pl.ds(pl.multiple_of(l * _WCOLS, 128), _WCOLS)]


def _win_routes():
    x, y, c = lax.axis_index("x"), lax.axis_index("y"), lax.axis_index("c")
    pos = [(x, y, 1 - c), (1 - x, y, c), (x, 1 - y, c), (1 - x, 1 - y, c)]
    return 4 * x + 2 * y + c, [(p, 4 * p[0] + 2 * p[1] + p[2]) for p in pos]


def _win_gather_start(shard):
    def body(src, land, send_sem, recv_sem, local_sem, src_thru, land_thru, token):
        me, routes = _win_routes()
        pltpu.make_async_copy(src, _win_cols(land, me), local_sem).start()
        pltpu.make_async_remote_copy(src_ref=src, dst_ref=_win_cols(land, me), send_sem=send_sem, recv_sem=recv_sem,
                                     device_id=routes[0][0], device_id_type=pl.DeviceIdType.MESH).start()
        token[...] = jnp.zeros_like(token)

    src = pltpu.with_memory_space_constraint(shard, pltpu.HBM)
    land = pltpu.with_memory_space_constraint(lax.empty((D, NIN), BF16), pltpu.HBM)
    sem = pltpu.SemaphoreType.DMA(())
    res = pl.pallas_call(
        body, name="win_gather_start",
        out_shape=[sem, sem, sem, pltpu.HBM(src.shape, BF16), pltpu.HBM(land.shape, BF16),
                   jax.ShapeDtypeStruct((8, 128), F32)],
        in_specs=[_HBM, _HBM],
        out_specs=[_SEM, _SEM, _SEM, _HBM, _HBM, pl.BlockSpec(memory_space=pltpu.VMEM)],
        input_output_aliases={0: 3, 1: 4},
        compiler_params=pltpu.CompilerParams(has_side_effects=_SIDE),
    )(src, land)
    return dict(send0=res[0], recv0=res[1], local=res[2], src=res[3], land=res[4]), res[5]


def _win_gather_links(hd, after):
    def body(src, land, after_ref, send_sems, recv_sems, src_thru, land_thru, token):
        me, routes = _win_routes()
        for k in (1, 2, 3):
            pltpu.make_async_remote_copy(src_ref=src, dst_ref=_win_cols(land, me), send_sem=send_sems.at[k - 1],
                                         recv_sem=recv_sems.at[k - 1], device_id=routes[k][0],
                                         device_id_type=pl.DeviceIdType.MESH).start()
        token[...] = jnp.zeros_like(token)

    sem3 = pltpu.SemaphoreType.DMA((3,))
    res = pl.pallas_call(
        body, name="win_gather_links",
        out_shape=[sem3, sem3, pltpu.HBM(hd["src"].shape, BF16), pltpu.HBM(hd["land"].shape, BF16),
                   jax.ShapeDtypeStruct((8, 128), F32)],
        in_specs=[_HBM, _HBM, pl.BlockSpec(memory_space=pl.ANY)],
        out_specs=[_SEM, _SEM, _HBM, _HBM, pl.BlockSpec(memory_space=pltpu.VMEM)],
        input_output_aliases={0: 2, 1: 3},
        compiler_params=pltpu.CompilerParams(has_side_effects=_SIDE),
    )(hd["src"], hd["land"], after)
    return dict(hd, send=res[0], recv=res[1], src=res[2], land=res[3]), res[4]


def _win_gather_forward(name, hd, ks, after):
    def body(land, recv_sems, after_ref, land_thru, fsend_sems, frecv_sems):
        me, routes = _win_routes()
        sibling = routes[0][0]
        for n, k in enumerate(ks):
            pos, peer = routes[k]
            piece = _win_cols(land, peer)
            pltpu.make_async_remote_copy(src_ref=piece, dst_ref=piece, send_sem=fsend_sems.at[n],
                                         recv_sem=recv_sems.at[k - 1], device_id=pos,
                                         device_id_type=pl.DeviceIdType.MESH).wait_recv()
            pltpu.make_async_remote_copy(src_ref=piece, dst_ref=piece, send_sem=fsend_sems.at[n],
                                         recv_sem=frecv_sems.at[n], device_id=sibling,
                                         device_id_type=pl.DeviceIdType.MESH).start()

    sems = pltpu.SemaphoreType.DMA((len(ks),))
    res = pl.pallas_call(
        body, name="win_gather_forward_" + name,
        out_shape=[pltpu.HBM(hd["land"].shape, BF16), sems, sems],
        in_specs=[_HBM, _SEM, pl.BlockSpec(memory_space=pl.ANY)],
        out_specs=[_HBM, _SEM, _SEM],
        input_output_aliases={0: 0},
        compiler_params=pltpu.CompilerParams(has_side_effects=_SIDE),
    )(hd["land"], hd["recv"], after)
    return dict(hd, land=res[0], **{"fsend" + name: res[1], "frecv" + name: res[2]})


def _win_gather_forwarded(name, hd, ks):
    def body(land, fsend_sems, frecv_sems, land_thru):
        me, routes = _win_routes()
        sib_c = routes[0][0][2]
        for n, k in enumerate(ks):
            _wait_bytes(_win_cols(land, routes[k][1]), fsend_sems.at[n])
            _wait_bytes(_win_cols(land, 4 * routes[k][0][0] + 2 * routes[k][0][1] + sib_c), frecv_sems.at[n])

    res = pl.pallas_call(
        body, name="win_gather_forwarded_" + name,
        out_shape=[pltpu.HBM(hd["land"].shape, BF16)],
        in_specs=[_HBM, _SEM, _SEM],
        out_specs=[_HBM],
        input_output_aliases={0: 0},
        compiler_params=pltpu.CompilerParams(has_side_effects=_SIDE),
    )(hd["land"], hd["fsend" + name], hd["frecv" + name])
    return dict(hd, land=res[0])


def _win_gather_early(hd):
    def body(src, land, recv_sem, local_sem, src_thru, land_thru):
        me, routes = _win_routes()
        _wait_bytes(_win_cols(land, routes[0][1]), recv_sem)
        pltpu.make_async_copy(src, _win_cols(land, me), local_sem).wait()

    res = pl.pallas_call(
        body, name="win_gather_early",
        out_shape=[pltpu.HBM(hd["src"].shape, BF16), pltpu.HBM(hd["land"].shape, BF16)],
        in_specs=[_HBM, _HBM, _SEM, _SEM],
        out_specs=[_HBM, _HBM],
        input_output_aliases={0: 0, 1: 1},
        compiler_params=pltpu.CompilerParams(has_side_effects=_SIDE),
    )(hd["src"], hd["land"], hd["recv0"], hd["local"])
    return dict(hd, src=res[0], land=res[1])


def _win_gather_wait(hd):
    def body(src, land, send0_sem, send_sems, src_thru, land_thru):
        for k in range(4):
            _wait_bytes(src, send0_sem if k == 0 else send_sems.at[k - 1])

    res = pl.pallas_call(
        body, name="win_gather_wait",
        out_shape=[pltpu.HBM(hd["src"].shape, BF16), pltpu.HBM(hd["land"].shape, BF16)],
        in_specs=[_HBM, _HBM, _SEM, _SEM],
        out_specs=[_HBM, _HBM],
        input_output_aliases={0: 0, 1: 1},
        compiler_params=pltpu.CompilerParams(has_side_effects=_SIDE),
    )(hd["src"], hd["land"], hd["send0"], hd["send"])
    return res[1]


def _whole(ref, l):
    return ref


def _slot(ref, l):
    return ref.at[l]


def _cols(width):
    def at(ref, l):
        return ref.at[:, pl.ds(pl.multiple_of(l * width, 128), width)]
    return at


def _rows(height):
    def at(ref, l):
        return ref.at[pl.ds(pl.multiple_of(l * height, 8), height), :]
    return at


NTILE = TP // TM


def _tile_rows(t):
    lo = max(t * TM - NMETA, 0)
    hi = min((t + 1) * TM - NMETA, SEQ)
    return lo, hi - lo, lo + NMETA - t * TM


def _for_tile(t, fn):
    for static_t in range(NTILE):
        pl.when(t == static_t)(functools.partial(fn, static_t))


def _token_tile_copy(hbm_ref, buf, sem, t):
    lo, n, off = _tile_rows(t)
    return pltpu.make_async_copy(hbm_ref.at[pl.ds(lo, n)], buf.at[pl.ds(off, n)], sem)


def _prenorm(x, meta_full, pre_w):
    def body(x_ref, meta_ref, pw_ref, h_ref, hn_ref, xbuf, sems):
        i = pl.program_id(0)
        slot = i % 2

        def start(t):
            _token_tile_copy(x_ref, xbuf.at[t % 2], sems.at[t % 2], t).start()

        @pl.when(i == 0)
        def _():
            start(0)
        _for_tile(i + 1, start)
        _for_tile(i, lambda t: _token_tile_copy(x_ref, xbuf.at[t % 2], sems.at[t % 2], t).wait())

        @pl.when(i == 0)
        def _():
            xbuf[0, 0:NMETA, :] = meta_ref[...]

        @pl.when(i == NTILE - 1)
        def _():
            last = _tile_rows(NTILE - 1)[1]
            xbuf[(NTILE - 1) % 2, last:TM, :] = jnp.zeros((TM - last, D), F32)

        pw = pw_ref[...]

        def chunk(ci, carry):
            r0 = pl.multiple_of(ci * R, R)
            xv = xbuf[slot, pl.ds(r0, R), :]
            h_ref[pl.ds(r0, R), :] = xv
            ms = jnp.mean(xv * xv, axis=-1, keepdims=True)
            hn_ref[pl.ds(r0, R), :] = (xv * lax.rsqrt(ms + EPS) * pw).astype(BF16)
            return carry
        lax.fori_loop(0, TM // R, chunk, 0, unroll=2)

    row = pl.BlockSpec((TM, D), lambda i: (i, 0))
    return pl.pallas_call(
        body, name="prenorm",
        grid=(NTILE,),
        in_specs=[pl.BlockSpec(memory_space=pl.ANY), pl.BlockSpec((NMETA, D), lambda i: (0, 0)),
                  pl.BlockSpec((1, D), lambda i: (0, 0))],
        out_specs=[row, row],
        out_shape=[jax.ShapeDtypeStruct((TP, D), F32), jax.ShapeDtypeStruct((TP, D), BF16)],
        scratch_shapes=[pltpu.VMEM((2, TM, D), F32), pltpu.SemaphoreType.DMA((2,))],
        compiler_params=_cparams(),
    )(x, meta_full, pre_w)


def _inproj_cols(name, shards, hn, w_land, b_in, z_prev):
    nsh = shards.shape[0]
    one_shard = w_land.shape[1] == _WCOLS

    def body(idx_ref, hn_ref, w_ref, b_ref, *rest):
        z_ref = rest[-2]
        z_ref[...] = jnp.dot(hn_ref[...], w_ref[...], preferred_element_type=F32) + b_ref[...]

    any_spec = pl.BlockSpec(memory_space=pl.ANY)
    in_specs = [pl.BlockSpec((TM, D), lambda j, i, idx: (i, 0)),
                pl.BlockSpec((D, _WCOLS), lambda j, i, idx: (0, 0 if one_shard else idx[j])),
                pl.BlockSpec((1, _WCOLS), lambda j, i, idx: (0, idx[j]))]
    operands = [hn, w_land, b_in]
    aliases = {2: 1}
    if z_prev is not None:
        in_specs.append(any_spec)
        operands.append(z_prev)
        aliases[4] = 0
    return pl.pallas_call(
        body, name=name,
        grid_spec=pltpu.PrefetchScalarGridSpec(
            num_scalar_prefetch=1, grid=(nsh, TP // TM), in_specs=in_specs,
            out_specs=[pl.BlockSpec((TM, _WCOLS), lambda j, i, idx: (i, idx[j])), any_spec]),
        out_shape=[jax.ShapeDtypeStruct((TP, NIN), F32), jax.ShapeDtypeStruct(w_land.shape, w_land.dtype)],
        input_output_aliases=aliases,
        compiler_params=_cparams(),
    )(shards, *operands)


def _gate_values(ga, gx, xc, sp8):
    r = _sig(ga)
    i = _sig(gx)
    log_a = -(r * sp8)
    a = jnp.exp(log_a)
    mult = jnp.sqrt(-_expm1_neg(2.0 * log_a))
    return r, i, a, mult


def _lru_fwd(z, conv_w, conv_b, wa_g, b_a, wx_g, b_x, lam):
    def body(x_ref, g_ref, cw_ref, cb_ref, wa_ref, ba_ref, wx_ref, bx_ref, lam_ref,
             y_ref, xc_ref, hs_ref, ga_s, gx_s):
        taps = [cw_ref[k:k + 1, :] for k in range(LW)]
        cb = cb_ref[...]

        def conv_chunk(ci, carry):
            r0 = pl.multiple_of(ci * R, R)
            cur = x_ref[pl.ds(r0, R), :]
            p0 = pl.multiple_of(jnp.maximum(r0 - 8, 0), 8)
            prev = jnp.where(ci > 0, x_ref[pl.ds(p0, 8), :], 0.0)
            buf = jnp.concatenate([prev, cur], axis=0)
            acc = cur * taps[LW - 1] + cb
            for s in range(1, LW):
                acc = acc + pltpu.roll(buf, s, 0)[8:8 + R, :] * taps[LW - 1 - s]
            xc_ref[pl.ds(r0, R), :] = acc
            return carry
        lax.fori_loop(0, TP // R, conv_chunk, 0)

        def gate_chunk(ci, carry):
            r0 = pl.multiple_of(ci * TM, TM)
            xb = xc_ref[pl.ds(r0, TM), :].astype(BF16)
            ga_s[pl.ds(r0, TM), :] = jnp.dot(xb, wa_ref[...], preferred_element_type=F32) + ba_ref[...]
            gx_s[pl.ds(r0, TM), :] = jnp.dot(xb, wx_ref[...], preferred_element_type=F32) + bx_ref[...]
            return carry
        lax.fori_loop(0, TP // TM, gate_chunk, 0)

        sp8 = LRU_C * _softplus(-lam_ref[...])
        row = _row_iota((R, CB))

        def scan_chunk(ci, hprev):
            r0 = pl.multiple_of(ci * R, R)
            xc = xc_ref[pl.ds(r0, R), :]
            _, i, a, mult = _gate_values(ga_s[pl.ds(r0, R), :], gx_s[pl.ds(r0, R), :], xc, sp8)
            u = mult * (i * xc)
            k = 1
            while k < R:
                m = row >= k
                u = jnp.where(m, a * pltpu.roll(u, k, 0) + u, u)
                a = jnp.where(m, a * pltpu.roll(a, k, 0), a)
                k *= 2
            hv = u + a * hprev
            hs_ref[pl.ds(r0, R), :] = hv
            g = g_ref[pl.ds(r0, R), :]
            y_ref[pl.ds(r0, R), :] = (hv * (g * _sig(g))).astype(BF16)
            return jnp.sum(jnp.where(row == R - 1, hv, 0.0), axis=0, keepdims=True)
        def scan_pass(i, hp):
            for sub in range(4):
                hp = scan_chunk(4 * i + sub, hp)
            return hp
        lax.fori_loop(0, TP // R // 4, scan_pass, jnp.zeros((1, CB), F32))

    col = lambda off: pl.BlockSpec((TP, CB), lambda j: (0, off + j))
    vec = pl.BlockSpec((1, CB), lambda j: (0, j))
    wsp = pl.BlockSpec((None, CB, CB), lambda j: (j, 0, 0))
    return pl.pallas_call(
        body, name="lru_fwd",
        grid=(NCB,),
        in_specs=[col(0), col(NCB), pl.BlockSpec((LW, CB), lambda j: (0, j)), vec, wsp, vec, wsp, vec, vec],
        out_specs=[col(0), col(0), col(0)],
        out_shape=[jax.ShapeDtypeStruct((TP, DL), BF16), jax.ShapeDtypeStruct((TP, DL), F32),
                   jax.ShapeDtypeStruct((TP, DL), F32)],
        scratch_shapes=[pltpu.VMEM((TP, CB), F32), pltpu.VMEM((TP, CB), F32)],
        compiler_params=_cparams(),
    )(z, z, conv_w, conv_b, wa_g, b_a, wx_g, b_x, lam)


CBC = 128
NCBC = DC // CBC
RC = 64


def _fold_rows(v):
    acc = v[0:8, :]
    for r in range(8, v.shape[0], 8):
        acc = acc + v[r:r + 8, :]
    return acc


def _conf_fwd_conv(z, dw_w, dw_b):
    def body(u1_ref, u2_ref, w_ref, b_ref, vc_ref, vs):
        vs[pl.ds(0, KWP), :] = jnp.zeros((KWP, CBC), F32)

        def glu_chunk(ci, carry):
            r0 = pl.multiple_of(ci * RC, RC)
            vs[pl.ds(KWP + r0, RC), :] = u1_ref[pl.ds(r0, RC), :] * _sig(u2_ref[pl.ds(r0, RC), :])
            return carry
        lax.fori_loop(0, TP // RC, glu_chunk, 0)

        bias = b_ref[...]

        def conv_chunk(ci, carry):
            r0 = pl.multiple_of(ci * RC, RC)
            buf = vs[pl.ds(r0, KWP + RC), :]
            acc = jnp.zeros((RC, CBC), F32) + bias
            for rr in range(8):
                rolled = buf if rr == 0 else pltpu.roll(buf, rr, 0)
                for q in range(4):
                    s = 8 * q + rr
                    if s > KW - 1:
                        continue
                    k = KW - 1 - s
                    acc = acc + rolled[KWP - 8 * q:KWP - 8 * q + RC, :] * w_ref[k:k + 1, :]
            vc_ref[pl.ds(r0, RC), :] = acc
            return carry
        lax.fori_loop(0, TP // RC, conv_chunk, 0)

    return pl.pallas_call(
        body, name="conf_fwd_conv",
        grid=(NCBC,),
        in_specs=[pl.BlockSpec((TP, CBC), lambda j: (0, 2 * NCBC + j)),
                  pl.BlockSpec((TP, CBC), lambda j: (0, 3 * NCBC + j)),
                  pl.BlockSpec((KWP, CBC), lambda j: (0, j)),
                  pl.BlockSpec((1, CBC), lambda j: (0, j))],
        out_specs=pl.BlockSpec((TP, CBC), lambda j: (0, j)),
        out_shape=jax.ShapeDtypeStruct((TP, DC), F32),
        scratch_shapes=[pltpu.VMEM((TP + KWP, CBC), F32)],
        compiler_params=_cparams(),
    )(z, z, dw_w, dw_b)


def _ln_chunk(vc, lw, lb):
    mu = jnp.mean(vc, axis=-1, keepdims=True)
    xm = vc - mu
    var = jnp.mean(xm * xm, axis=-1, keepdims=True)
    rstd = lax.rsqrt(var + EPS)
    xhat = xm * rstd
    return xhat, rstd, xhat * lw + lb


def _conf_fwd_proj(vc, z, ln_w, ln_b, pw_w, pw_b):
    def body(vc_ref, g_ref, lw_ref, lb_ref, w_ref, b_ref, y_ref, p_ref, xhat_ref, rstd_ref, s_s):
        lw, lb = lw_ref[...], lb_ref[...]

        def ln_chunk(ci, carry):
            r0 = pl.multiple_of(ci * R, R)
            for half in range(2):
                rr = r0 + 8 * half
                xhat, rstd, ln = _ln_chunk(vc_ref[pl.ds(rr, 8), :], lw, lb)
                xhat_ref[pl.ds(rr, 8), :] = xhat
                rstd_ref[pl.ds(rr, 8), :] = jnp.broadcast_to(rstd, (8, 128))
                p_ref[pl.ds(rr, 8), :] = ln * _sig(ln)
            s_s[pl.ds(r0, R), :] = p_ref[pl.ds(r0, R), :].astype(BF16)
            return carry
        lax.fori_loop(0, TM // R, ln_chunk, 0, unroll=2)

        p_ref[...] = jnp.dot(s_s[...], w_ref[...], preferred_element_type=F32) + b_ref[...]

        def out_chunk(ci, carry):
            r0 = pl.multiple_of(ci * R, R)
            g = g_ref[pl.ds(r0, R), :]
            y_ref[pl.ds(r0, R), :] = (p_ref[pl.ds(r0, R), :] * (g * _sig(g))).astype(BF16)
            return carry
        lax.fori_loop(0, TM // R, out_chunk, 0)

    row = pl.BlockSpec((TM, DC), lambda i: (i, 0))
    vec = pl.BlockSpec((1, DC), lambda i: (0, 0))
    return pl.pallas_call(
        body, name="conf_fwd_proj",
        grid=(TP // TM,),
        in_specs=[row, pl.BlockSpec((TM, DC), lambda i: (i, 4)), vec, vec,
                  pl.BlockSpec((DC, DC), lambda i: (0, 0)), vec],
        out_specs=[row, row, row, pl.BlockSpec((TM, 128), lambda i: (i, 0))],
        out_shape=[jax.ShapeDtypeStruct((TP, DC), BF16), jax.ShapeDtypeStruct((TP, DC), F32),
                   jax.ShapeDtypeStruct((TP, DC), F32), jax.ShapeDtypeStruct((TP, 128), F32)],
        scratch_shapes=[pltpu.VMEM((TM, DC), BF16)],
        compiler_params=_cparams(),
    )(vc, z, ln_w, ln_b, pw_w, pw_b)


def _outproj_loss(ylru, yconf, w_out, h, target, post_w):
    def body(yl_ref, yc_ref, w_ref, h_ref, tgt_hbm, pw_ref, dout_ref, dy_ref, loss_ref, dpw_ref, y_s, t_ref, sem):
        i = pl.program_id(0)
        k = pl.program_id(1)

        @pl.when(k == 0)
        def _():
            _for_tile(i, lambda t: _token_tile_copy(tgt_hbm, t_ref, sem, t).start())
            y_s[...] = jnp.dot(yl_ref[...], w_ref[...], preferred_element_type=F32)

        @pl.when(k == 1)
        def _():
            y_s[...] += jnp.dot(yc_ref[...], w_ref[...], preferred_element_type=F32)

        @pl.when(jnp.logical_and(i == 0, k == 1))
        def _():
            loss_ref[...] = jnp.zeros_like(loss_ref)
            dpw_ref[...] = jnp.zeros_like(dpw_ref)

        @pl.when(k == 1)
        def _():
            _for_tile(i, lambda t: _token_tile_copy(tgt_hbm, t_ref, sem, t).wait())

            @pl.when(i == 0)
            def _():
                t_ref[0:NMETA, :] = jnp.zeros((NMETA, D), F32)

            @pl.when(i == NTILE - 1)
            def _():
                last = _tile_rows(NTILE - 1)[1]
                t_ref[last:TM, :] = jnp.zeros((TM - last, D), F32)

            pw = pw_ref[...]
            row = _row_iota((8, D))

            def chunk(ci, carry):
                r0 = pl.multiple_of(ci * 8, 8)
                yv = y_s[pl.ds(r0, 8), :]
                rs = lax.rsqrt(jnp.mean(yv * yv, axis=-1, keepdims=True) + EPS)
                grow = row + (i * TM + r0)
                valid = jnp.logical_and(grow >= NMETA, grow < T)
                yn = yv * rs
                err = jnp.where(valid, h_ref[pl.ds(r0, 8), :] + yn * pw - t_ref[pl.ds(r0, 8), :], 0.0)
                loss_ref[...] += err * err
                d_rn = err * (1.0 / D)
                dout_ref[pl.ds(r0, 8), :] = d_rn
                dpw_ref[...] += d_rn * yn
                gw = d_rn * pw
                dot = jnp.mean(gw * yv, axis=-1, keepdims=True)
                dy_ref[pl.ds(r0, 8), :] = (rs * gw - yv * (rs * rs * rs * dot)).astype(BF16)
                return carry
            lax.fori_loop(0, TM // 8, chunk, 0, unroll=4)

    row = pl.BlockSpec((TM, D), lambda i, k: (i, 0))
    half = pl.BlockSpec((TM, DL), lambda i, k: (i, 0))
    acc = pl.BlockSpec((8, D), lambda i, k: (0, 0))
    return pl.pallas_call(
        body, name="outproj_loss",
        grid=(TP // TM, 2),
        in_specs=[half, half, pl.BlockSpec((DL, D), lambda i, k: (k, 0)), row, pl.BlockSpec(memory_space=pl.ANY),
                  pl.BlockSpec((1, D), lambda i, k: (0, 0))],
        out_specs=[row, row, acc, acc],
        out_shape=[jax.ShapeDtypeStruct((TP, D), F32), jax.ShapeDtypeStruct((TP, D), BF16),
                   jax.ShapeDtypeStruct((8, D), F32), jax.ShapeDtypeStruct((8, D), F32)],
        scratch_shapes=[pltpu.VMEM((TM, D), F32), pltpu.VMEM((TM, D), F32), pltpu.SemaphoreType.DMA(())],
        compiler_params=_cparams(),
    )(ylru, yconf, w_out, h, target, post_w)


_NT = (((1,), (1,)), ((), ()))
_TN = (((0,), (0,)), ((), ()))


def _outproj_bwd(dy, ylru, yconf, w_out):
    def body(dy_ref, yl_ref, yc_ref, w_ref, dycat_ref, dw_ref):
        j = pl.program_id(0)
        dyv = dy_ref[...]
        dycat_ref[...] = lax.dot_general(dyv, w_ref[...], _NT, preferred_element_type=F32)

        @pl.when(j < NCB)
        def _():
            dw_ref[...] = lax.dot_general(yl_ref[...], dyv, _TN, preferred_element_type=F32).astype(BF16)

        @pl.when(j >= NCB)
        def _():
            dw_ref[...] = lax.dot_general(yc_ref[...], dyv, _TN, preferred_element_type=F32).astype(BF16)

    return pl.pallas_call(
        body, name="outproj_bwd",
        grid=(2 * NCB,),
        in_specs=[pl.BlockSpec((TP, D), lambda j: (0, 0)),
                  pl.BlockSpec((TP, CB), lambda j: (0, jnp.minimum(j, NCB - 1))),
                  pl.BlockSpec((TP, CB), lambda j: (0, jnp.maximum(j - NCB, 0))),
                  pl.BlockSpec((CB, D), lambda j: (j, 0))],
        out_specs=[pl.BlockSpec((TP, CB), lambda j: (0, j)), pl.BlockSpec((CB, D), lambda j: (j, 0))],
        out_shape=[jax.ShapeDtypeStruct((TP, D), F32), jax.ShapeDtypeStruct((D, D), BF16)],
        compiler_params=_cparams(),
    )(dy, ylru, yconf, w_out)


_AFTER = pl.BlockSpec(memory_space=pl.ANY)


def _conf_bwd_proj(dycat, p, z, xhat, rstd, hs, ln_w, ln_b, pw_w, after):
    def body(dy_ref, p_ref, g_ref, xhat_ref, rstd_ref, dyl_ref, hs_ref, gl_ref, lw_ref, lb_ref, w_ref, after_ref,
             dvc_ref, dz_ref, dpw_ref, vecs_ref, dp_s, s_s, ds_s):
        i = pl.program_id(0)
        lw, lb = lw_ref[...], lb_ref[...]

        @pl.when(i == 0)
        def _():
            dpw_ref[...] = jnp.zeros_like(dpw_ref)
            vecs_ref[...] = jnp.zeros_like(vecs_ref)

        def pre_chunk(ci, carry):
            r0 = pl.multiple_of(ci * R, R)
            for half in range(2):
                rr = r0 + 8 * half
                dyv = dy_ref[pl.ds(rr, 8), :]
                g = g_ref[pl.ds(rr, 8), :]
                sg = _sig(g)
                dp = dyv * (g * sg)
                dg = dyv * p_ref[pl.ds(rr, 8), :] * (sg * (1.0 + g * (1.0 - sg)))
                vecs_ref[0:8, :] += dp
                vecs_ref[8:16, :] += dg
                ds_s[pl.ds(rr, 8), :] = dp
                dvc_ref[pl.ds(rr, 8), :] = dg
            dp_s[pl.ds(r0, R), :] = ds_s[pl.ds(r0, R), :].astype(BF16)
            dz_ref[0, pl.ds(r0, R), :] = dvc_ref[pl.ds(r0, R), :].astype(BF16)
            for half in range(2):
                rr = r0 + 8 * half
                gl = gl_ref[pl.ds(rr, 8), :]
                sgl = _sig(gl)
                dgl = dyl_ref[pl.ds(rr, 8), :] * hs_ref[pl.ds(rr, 8), :] * (sgl * (1.0 + gl * (1.0 - sgl)))
                vecs_ref[32:40, :] += dgl
                dvc_ref[pl.ds(rr, 8), :] = dgl
            dz_ref[1, pl.ds(r0, R), :] = dvc_ref[pl.ds(r0, R), :].astype(BF16)
            for half in range(2):
                rr = r0 + 8 * half
                ln = xhat_ref[pl.ds(rr, 8), :] * lw + lb
                ds_s[pl.ds(rr, 8), :] = ln * _sig(ln)
            s_s[pl.ds(r0, R), :] = ds_s[pl.ds(r0, R), :].astype(BF16)
            return carry
        lax.fori_loop(0, TM // R, pre_chunk, 0, unroll=2)

        dpb = dp_s[...]
        ds_s[...] = lax.dot_general(dpb, w_ref[...], _NT, preferred_element_type=F32)
        dpw_ref[...] += lax.dot_general(s_s[...], dpb, _TN, preferred_element_type=F32)

        def post_chunk(ci, carry):
            r0 = pl.multiple_of(ci * 8, 8)
            xhat = xhat_ref[pl.ds(r0, 8), :]
            rstd = jnp.tile(rstd_ref[pl.ds(r0, 8), :], (1, DC // 128))
            ln = xhat * lw + lb
            sl = _sig(ln)
            dln = ds_s[pl.ds(r0, 8), :] * (sl * (1.0 + ln * (1.0 - sl)))
            vecs_ref[16:24, :] += dln * xhat
            vecs_ref[24:32, :] += dln
            dxh = dln * lw
            m1 = jnp.mean(dxh, axis=-1, keepdims=True)
            m2 = jnp.mean(dxh * xhat, axis=-1, keepdims=True)
            dvc_ref[pl.ds(r0, 8), :] = rstd * (dxh - m1 - xhat * m2)
            return carry
        lax.fori_loop(0, TM // 8, post_chunk, 0, unroll=4)

    row = pl.BlockSpec((TM, DC), lambda i: (i, 0))
    vec = pl.BlockSpec((1, DC), lambda i: (0, 0))
    return pl.pallas_call(
        body, name="conf_bwd_proj",
        grid=(TP // TM,),
        in_specs=[pl.BlockSpec((TM, DC), lambda i: (i, 1)), row, pl.BlockSpec((TM, DC), lambda i: (i, 4)), row,
                  pl.BlockSpec((TM, 128), lambda i: (i, 0)),
                  pl.BlockSpec((TM, DL), lambda i: (i, 0)), row, pl.BlockSpec((TM, DL), lambda i: (i, 1)),
                  vec, vec, pl.BlockSpec((DC, DC), lambda i: (0, 0)), _AFTER],
        out_specs=[row, pl.BlockSpec((2, TM, DC), lambda i: (0, i, 0)), pl.BlockSpec((DC, DC), lambda i: (0, 0)),
                   pl.BlockSpec((40, DC), lambda i: (0, 0))],
        out_shape=[jax.ShapeDtypeStruct((TP, DC), F32), jax.ShapeDtypeStruct((2, TP, DC), BF16),
                   jax.ShapeDtypeStruct((DC, DC), F32), jax.ShapeDtypeStruct((40, DC), F32)],
        scratch_shapes=[pltpu.VMEM((TM, DC), BF16), pltpu.VMEM((TM, DC), BF16), pltpu.VMEM((TM, DC), F32)],
        compiler_params=_cparams(),
    )(dycat, p, z, xhat, rstd, dycat, hs, z, ln_w, ln_b, pw_w, after)


def _conf_bwd_conv(dvc, z, dw_w, after):
    def body(dvc_ref, u1_ref, u2_ref, w_ref, after_ref, du_ref, dw_ref, vecs_ref, vs, dvs):
        vs[pl.ds(0, KWP), :] = jnp.zeros((KWP, CBC), F32)
        dvs[pl.ds(TP, KWP), :] = jnp.zeros((KWP, CBC), F32)
        dw_ref[...] = jnp.zeros_like(dw_ref)
        vecs_ref[...] = jnp.zeros_like(vecs_ref)

        def fill_chunk(ci, carry):
            r0 = pl.multiple_of(ci * RC, RC)
            vs[pl.ds(KWP + r0, RC), :] = u1_ref[pl.ds(r0, RC), :] * _sig(u2_ref[pl.ds(r0, RC), :])
            dv = dvc_ref[pl.ds(r0, RC), :]
            dvs[pl.ds(r0, RC), :] = dv
            vecs_ref[0:8, :] += _fold_rows(dv)
            return carry
        lax.fori_loop(0, TP // RC, fill_chunk, 0)

        def conv_chunk(ci, carry):
            r0 = pl.multiple_of(ci * RC, RC)
            vbuf = vs[pl.ds(r0, KWP + RC), :]
            dbuf = dvs[pl.ds(r0, KWP + RC), :]
            dcur = dbuf[0:RC, :]
            dv = jnp.zeros((RC, CBC), F32)
            for rr in range(8):
                vroll = vbuf if rr == 0 else pltpu.roll(vbuf, rr, 0)
                droll = dbuf if rr == 0 else pltpu.roll(dbuf, KWP + RC - rr, 0)
                for q in range(4):
                    s = 8 * q + rr
                    if s > KW - 1:
                        continue
                    k = KW - 1 - s
                    dv = dv + droll[8 * q:8 * q + RC, :] * w_ref[k:k + 1, :]
                    dw_ref[8 * k:8 * k + 8, :] += _fold_rows(dcur * vroll[KWP - 8 * q:KWP - 8 * q + RC, :])
            u1 = u1_ref[pl.ds(r0, RC), :]
            sg = _sig(u2_ref[pl.ds(r0, RC), :])
            du1 = dv * sg
            du2 = dv * u1 * (sg * (1.0 - sg))
            du_ref[0, pl.ds(r0, RC), :] = du1.astype(BF16)
            du_ref[1, pl.ds(r0, RC), :] = du2.astype(BF16)
            vecs_ref[8:16, :] += _fold_rows(du1)
            vecs_ref[16:24, :] += _fold_rows(du2)
            return carry
        lax.fori_loop(0, TP // RC, conv_chunk, 0)

    blk = pl.BlockSpec((TP, CBC), lambda j: (0, j))
    return pl.pallas_call(
        body, name="conf_bwd_conv",
        grid=(NCBC,),
        in_specs=[blk, pl.BlockSpec((TP, CBC), lambda j: (0, 2 * NCBC + j)),
                  pl.BlockSpec((TP, CBC), lambda j: (0, 3 * NCBC + j)), pl.BlockSpec((KWP, CBC), lambda j: (0, j)),
                  _AFTER],
        out_specs=[pl.BlockSpec((2, TP, CBC), lambda j: (0, 0, j)), pl.BlockSpec((8 * KWP, CBC), lambda j: (0, j)),
                   pl.BlockSpec((24, CBC), lambda j: (0, j))],
        out_shape=[jax.ShapeDtypeStruct((2, TP, DC), BF16),
                   jax.ShapeDtypeStruct((8 * KWP, DC), F32), jax.ShapeDtypeStruct((24, DC), F32)],
        scratch_shapes=[pltpu.VMEM((TP + KWP, CBC), F32), pltpu.VMEM((TP + KWP, CBC), F32)],
        compiler_params=_cparams(),
    )(dvc, z, z, dw_w, after)


def _lru_bwd(dycat, z, xc, hs, conv_w, wa_g, b_a, wx_g, b_x, lam, after):
    NV = 6

    def body(dy_ref, x_ref, g_ref, xc_ref, hs_ref, cw_ref, wa_ref, ba_ref, wx_ref, bx_ref, lam_ref, after_ref,
             dzl_ref, dwa_ref, dwx_ref, dcw_ref, vecs_ref, ga_s, gx_s, dxc_s):
        vecs_ref[...] = jnp.zeros_like(vecs_ref)
        dcw_ref[...] = jnp.zeros_like(dcw_ref)
        dxc_s[pl.ds(TP, 8), :] = jnp.zeros((8, CB), F32)

        def gate_chunk(ci, carry):
            r0 = pl.multiple_of(ci * TM, TM)
            xb = xc_ref[pl.ds(r0, TM), :].astype(BF16)
            ga_s[pl.ds(r0, TM), :] = jnp.dot(xb, wa_ref[...], preferred_element_type=F32) + ba_ref[...]
            gx_s[pl.ds(r0, TM), :] = jnp.dot(xb, wx_ref[...], preferred_element_type=F32) + bx_ref[...]
            return carry
        lax.fori_loop(0, TP // TM, gate_chunk, 0)

        sp8 = LRU_C * _softplus(-lam_ref[...])
        row = _row_iota((R, CB))
        nchunk = TP // R

        def scan_chunk(cj, carry):
            a_next, lam_next = carry
            ci = nchunk - 1 - cj
            r0 = pl.multiple_of(ci * R, R)
            dyv = dy_ref[pl.ds(r0, R), :]
            g = g_ref[pl.ds(r0, R), :]
            hv = hs_ref[pl.ds(r0, R), :]
            xc = xc_ref[pl.ds(r0, R), :]
            sg = _sig(g)
            dhs = dyv * (g * sg)
            r, i, a, mult = _gate_values(ga_s[pl.ds(r0, R), :], gx_s[pl.ds(r0, R), :], xc, sp8)
            b = jnp.where(row == R - 1, a_next, pltpu.roll(a, R - 1, 0))
            lv = dhs
            k = 1
            while k < R:
                m = row < R - k
                lv = jnp.where(m, lv + b * pltpu.roll(lv, R - k, 0), lv)
                b = jnp.where(m, b * pltpu.roll(b, R - k, 0), b)
                k *= 2
            lv = lv + b * lam_next
            p0 = pl.multiple_of(jnp.maximum(r0 - 8, 0), 8)
            hprev8 = jnp.where(ci > 0, hs_ref[pl.ds(p0, 8), :], 0.0)
            hprev = pltpu.roll(jnp.concatenate([hprev8, hv], axis=0), 1, 0)[8:8 + R, :]
            da = lv * hprev
            ixc = i * xc
            dmult = lv * ixc
            di = lv * mult * xc
            dxc_s[pl.ds(r0, R), :] = lv * mult * i
            a2 = a * a
            dlog_a = da * a - dmult * a2 / mult
            vecs_ref[32:40, :] += _fold8(dlog_a * r)
            dga = -(dlog_a * sp8) * r * (1.0 - r)
            dgx = di * i * (1.0 - i)
            ga_s[pl.ds(r0, R), :] = dga
            gx_s[pl.ds(r0, R), :] = dgx
            vecs_ref[16:24, :] += _fold8(dga)
            vecs_ref[24:32, :] += _fold8(dgx)
            a_first = jnp.sum(jnp.where(row == 0, a, 0.0), axis=0, keepdims=True)
            l_first = jnp.sum(jnp.where(row == 0, lv, 0.0), axis=0, keepdims=True)
            return a_first, l_first
        lax.fori_loop(0, nchunk // 2, lambda i, cr: scan_chunk(2 * i + 1, scan_chunk(2 * i, cr)),
                      (jnp.zeros((1, CB), F32), jnp.zeros((1, CB), F32)))

        dwa_ref[...] = jnp.zeros_like(dwa_ref)
        dwx_ref[...] = jnp.zeros_like(dwx_ref)

        def mm_chunk(ci, carry):
            r0 = pl.multiple_of(ci * TM, TM)
            xb = xc_ref[pl.ds(r0, TM), :].astype(BF16)
            dgab = ga_s[pl.ds(r0, TM), :].astype(BF16)
            dgxb = gx_s[pl.ds(r0, TM), :].astype(BF16)
            dxc_s[pl.ds(r0, TM), :] += (lax.dot_general(dgab, wa_ref[...], _NT, preferred_element_type=F32)
                                        + lax.dot_general(dgxb, wx_ref[...], _NT, preferred_element_type=F32))
            dwa_ref[...] += lax.dot_general(xb, dgab, _TN, preferred_element_type=F32)
            dwx_ref[...] += lax.dot_general(xb, dgxb, _TN, preferred_element_type=F32)
            return carry
        lax.fori_loop(0, TP // TM, mm_chunk, 0)

        taps = [cw_ref[k:k + 1, :] for k in range(LW)]

        def conv_chunk(ci, carry):
            r0 = pl.multiple_of(ci * R, R)
            dbuf = dxc_s[pl.ds(r0, R + 8), :]
            dcur = dbuf[0:R, :]
            p0 = pl.multiple_of(jnp.maximum(r0 - 8, 0), 8)
            xprev = jnp.where(ci > 0, x_ref[pl.ds(p0, 8), :], 0.0)
            xbuf = jnp.concatenate([xprev, x_ref[pl.ds(r0, R), :]], axis=0)
            dxl = dcur * taps[LW - 1]
            dcw_ref[8 * (LW - 1):8 * LW, :] += _fold8(dcur * xbuf[8:8 + R, :])
            for s in range(1, LW):
                k = LW - 1 - s
                dxl = dxl + pltpu.roll(dbuf, R + 8 - s, 0)[0:R, :] * taps[k]
                dcw_ref[8 * k:8 * k + 8, :] += _fold8(dcur * pltpu.roll(xbuf, s, 0)[8:8 + R, :])
            dzl_ref[0, pl.ds(r0, R), :] = dxl.astype(BF16)
            vecs_ref[8:16, :] += _fold8(dxl)
            vecs_ref[40:48, :] += _fold8(dcur)
            return carry
        lax.fori_loop(0, TP // R, conv_chunk, 0)
        vecs_ref[32:40, :] = vecs_ref[32:40, :] * (LRU_C * _sig(-lam_ref[...]))

    col = lambda off: pl.BlockSpec((TP, CB), lambda j: (0, off + j))
    vec = pl.BlockSpec((1, CB), lambda j: (0, j))
    wsp = pl.BlockSpec((None, CB, CB), lambda j: (j, 0, 0))
    return pl.pallas_call(
        body, name="lru_bwd",
        grid=(NCB,),
        in_specs=[col(0), col(0), col(NCB), col(0), col(0), pl.BlockSpec((LW, CB), lambda j: (0, j)),
                  wsp, vec, wsp, vec, vec, _AFTER],
        out_specs=[pl.BlockSpec((1, TP, CB), lambda j: (0, 0, j)), wsp, wsp,
                   pl.BlockSpec((8 * LW, CB), lambda j: (0, j)), pl.BlockSpec((8 * NV, CB), lambda j: (0, j))],
        out_shape=[jax.ShapeDtypeStruct((1, TP, DL), BF16),
                   jax.ShapeDtypeStruct((NCB, CB, CB), F32), jax.ShapeDtypeStruct((NCB, CB, CB), F32),
                   jax.ShapeDtypeStruct((8 * LW, DL), F32), jax.ShapeDtypeStruct((8 * NV, DL), F32)],
        scratch_shapes=[pltpu.VMEM((TP, CB), F32), pltpu.VMEM((TP, CB), F32), pltpu.VMEM((TP + 8, CB), F32)],
        compiler_params=_cparams(),
    )(dycat, z, z, xc, hs, conv_w, wa_g, b_a, wx_g, b_x, lam, after)


def _dz_section(sec, dzl_ref, dz41_ref, dzc_ref, use):
    @pl.when(sec == 0)
    def _():
        use(dzl_ref)

    @pl.when(jnp.logical_or(sec == 1, sec == 4))
    def _():
        use(dz41_ref)

    @pl.when(jnp.logical_or(sec == 2, sec == 3))
    def _():
        use(dzc_ref)


def _dz_specs(rows, index):
    return [pl.BlockSpec((None, rows, 1024), lambda a, b: (0, index(a, b)[0], 0)),
            pl.BlockSpec((None, rows, 1024), lambda a, b: (jnp.where(index(a, b)[1] == 1, 1, 0), index(a, b)[0], 0)),
            pl.BlockSpec((None, rows, 1024), lambda a, b: (jnp.clip(index(a, b)[1] - 2, 0, 1), index(a, b)[0], 0))]


def _inproj_wgrad(name, hn, dzs, after):
    KB = 512
    nsec = dzs.shape[0]

    def body(hn_ref, dz_ref, after_ref, dw_ref):
        dw_ref[...] = lax.dot_general(hn_ref[...], dz_ref[...], _TN, preferred_element_type=F32).astype(BF16)

    return pl.pallas_call(
        body, name=name,
        grid=(nsec, D // KB),
        in_specs=[pl.BlockSpec((TP, KB), lambda n, kb: (0, kb)),
                  pl.BlockSpec((None, TP, 1024), lambda n, kb: (n, 0, 0)), _AFTER],
        out_specs=pl.BlockSpec((KB, 1024), lambda n, kb: (kb, n)),
        out_shape=jax.ShapeDtypeStruct((D, nsec * 1024), BF16),
        compiler_params=_cparams(),
    )(hn, dzs, after)


def _sum_win_parts(parts_a, parts_b, parts_c):
    RB = 64

    def body(a_ref, b_ref, c_ref, o_ref):
        def chunk(ci, carry):
            r0 = pl.multiple_of(ci * R, R)
            for ref, src, base, ncol in ((a_ref, 0, 0, 1024), (c_ref, 1024, 1024, 1024), (b_ref, 0, 2048, 2048),
                                         (c_ref, 0, 4096, 1024)):
                for c0 in range(0, ncol, 512):
                    acc = ref[0, pl.ds(r0, R), src + c0:src + c0 + 512].astype(F32)
                    for sidx in range(1, NDEV):
                        acc = acc + ref[sidx, pl.ds(r0, R), src + c0:src + c0 + 512].astype(F32)
                    o_ref[pl.ds(r0, R), base + c0:base + c0 + 512] = acc.astype(BF16)
            return carry
        lax.fori_loop(0, RB // R, chunk, 0)

    spec = lambda ncol: pl.BlockSpec((NDEV, RB, ncol), lambda i: (0, i, 0))
    return pl.pallas_call(
        body, name="sum_win_parts",
        grid=(D // NDEV // RB,),
        in_specs=[spec(1024), spec(2048), spec(2048)],
        out_specs=pl.BlockSpec((RB, NIN), lambda i: (i, 0)),
        out_shape=jax.ShapeDtypeStruct((D // NDEV, NIN), BF16),
        compiler_params=_cparams(),
    )(parts_a, parts_b, parts_c)


def _inproj_bwd(dzl, dz41, dzc, w_in, h, dout, pre_w, after):
    nsec = NIN // 1024

    def body(dzl_ref, dz41_ref, dzc_ref, w_ref, h_ref, dout_ref, pw_ref, after_ref, gx_hbm, dmeta_ref, dpw_ref,
             acc_s, dh_s, sem):
        i = pl.program_id(0)
        s = pl.program_id(1)

        def gx_copy(t):
            lo, n, off = _tile_rows(t)
            return pltpu.make_async_copy(dh_s.at[pl.ds(off, n)], gx_hbm.at[pl.ds(lo, n)], sem)

        @pl.when(s == 0)
        def _():
            acc_s[...] = jnp.zeros_like(acc_s)

        def use(dz_ref):
            acc_s[...] += lax.dot_general(dz_ref[...], w_ref[...], _NT, preferred_element_type=F32)
        _dz_section(s, dzl_ref, dz41_ref, dzc_ref, use)

        @pl.when(jnp.logical_and(i == 0, s == nsec - 1))
        def _():
            dpw_ref[...] = jnp.zeros_like(dpw_ref)

        @pl.when(s == nsec - 1)
        def _():
            _for_tile(i - 1, lambda t: gx_copy(t).wait())
            pw = pw_ref[...]

            def chunk(ci, carry):
                r0 = pl.multiple_of(ci * 8, 8)
                hv = h_ref[pl.ds(r0, 8), :]
                dhn = acc_s[pl.ds(r0, 8), :]
                rs = lax.rsqrt(jnp.mean(hv * hv, axis=-1, keepdims=True) + EPS)
                dpw_ref[...] += dhn * (hv * rs)
                gw = dhn * pw
                dot = jnp.mean(gw * hv, axis=-1, keepdims=True)
                dh_s[pl.ds(r0, 8), :] = rs * gw - hv * (rs * rs * rs * dot) + dout_ref[pl.ds(r0, 8), :]
                return carry
            lax.fori_loop(0, TM // 8, chunk, 0, unroll=4)
            _for_tile(i, lambda t: gx_copy(t).start())

            @pl.when(i == 0)
            def _():
                dmeta_ref[...] = dh_s[0:NMETA, :]

            @pl.when(i == NTILE - 1)
            def _():
                gx_copy(NTILE - 1).wait()

    row = pl.BlockSpec((TM, D), lambda i, s: (i, 0))
    return pl.pallas_call(
        body, name="inproj_bwd",
        grid=(TP // TM, nsec),
        in_specs=_dz_specs(TM, lambda i, s: (i, s)) + [
            pl.BlockSpec((D, 1024), lambda i, s: (0, s)), row, row, pl.BlockSpec((1, D), lambda i, s: (0, 0)),
            _AFTER],
        out_specs=[pl.BlockSpec(memory_space=pl.ANY), pl.BlockSpec((NMETA, D), lambda i, s: (0, 0)),
                   pl.BlockSpec((8, D), lambda i, s: (0, 0))],
        out_shape=[jax.ShapeDtypeStruct((SEQ, D), F32), jax.ShapeDtypeStruct((NMETA, D), F32),
                   jax.ShapeDtypeStruct((8, D), F32)],
        scratch_shapes=[pltpu.VMEM((TM, D), F32), pltpu.VMEM((TM, D), F32), pltpu.SemaphoreType.DMA(())],
        compiler_params=_cparams(),
    )(dzl, dz41, dzc, w_in, h, dout, pre_w, after)


def _adamw(name, parts, w, m, v, block_rows):
    rows, cols = w.shape
    nparts = parts.shape[0]
    cw = cols if cols <= 640 else 512

    def body(p_ref, w_ref, m_ref, v_ref, g_ref, d_ref, nm_ref, nv_ref):
        def chunk(ci, carry):
            r0 = pl.multiple_of(ci * R, R)
            for c0 in range(0, cols, cw):
                at = (pl.ds(r0, R), slice(c0, c0 + cw))
                g = p_ref[(0,) + at].astype(F32)
                for sidx in range(1, nparts):
                    g = g + p_ref[(sidx,) + at].astype(F32)
                delta, mv, vv = _adam_math(g, w_ref[at], m_ref[at], v_ref[at])
                g_ref[at] = g
                nm_ref[at] = mv
                nv_ref[at] = vv
                d_ref[at] = delta
            return carry
        lax.fori_loop(0, block_rows // R, chunk, 0)

    blk = pl.BlockSpec((block_rows, cols), lambda i: (i, 0))
    shp = jax.ShapeDtypeStruct((rows, cols), F32)
    return pl.pallas_call(
        body, name=name,
        grid=(rows // block_rows,),
        in_specs=[pl.BlockSpec((nparts, block_rows, cols), lambda i: (0, i, 0)), blk, blk, blk],
        out_specs=[blk, blk, blk, blk],
        out_shape=[shp, shp, shp, shp],
        compiler_params=_cparams(),
    )(parts, w, m, v)


def _adam_math(g, w, m, v):
    c1 = 1.0 / (1.0 - ADAM_B1 ** ADAM_STEP)
    c2 = 1.0 / (1.0 - ADAM_B2 ** ADAM_STEP)
    mv = ADAM_B1 * m + (1.0 - ADAM_B1) * g
    vv = ADAM_B2 * v + (1.0 - ADAM_B2) * (g * g)
    upd = (mv * c1) / (jnp.sqrt(vv * c2) + ADAM_EPS) + ADAM_WD * w
    return -ADAM_LR * upd, mv, vv


_VEC = [("pre_norm_w", 2), ("post_norm_w", 2), ("b_in", 5), ("lru_conv_b", 1), ("b_gate_a", 1), ("b_gate_x", 1),
        ("lru_lambda", 1), ("conf_dw_b", 1), ("conf_ln_w", 1), ("conf_ln_b", 1), ("conf_pw_b", 1)]
_VEC_ROWS = 24
_LOSS_ROW = 17
_SM_ROWS = 64


def _pack_grads(dprew_acc, dpostw_acc, cvecs, kvecs, lvecs, dcw_acc, ddw_acc, dh, loss_acc):
    def body(pre_ref, post_ref, c_ref, k_ref, l_ref, dcw_ref, ddw_ref, dh_ref, loss_ref, vec_ref, small_ref, tmp):
        s8 = lambda ref, r: jnp.sum(ref[8 * r:8 * r + 8, :], axis=0, keepdims=True)
        vec_ref[...] = jnp.zeros_like(vec_ref)
        pre, post = s8(pre_ref, 0), s8(post_ref, 0)
        rows = [pre[:, 0:1024], pre[:, 1024:2048], post[:, 0:1024], post[:, 1024:2048],
                s8(l_ref, 1), s8(c_ref, 4), s8(k_ref, 1), s8(k_ref, 2), s8(c_ref, 1),
                s8(l_ref, 5), s8(l_ref, 2), s8(l_ref, 3), s8(l_ref, 4),
                s8(k_ref, 0), s8(c_ref, 2), s8(c_ref, 3), s8(c_ref, 0)]
        for r, val in enumerate(rows):
            vec_ref[r:r + 1, :] = val
        vec_ref[_LOSS_ROW:_LOSS_ROW + 1, :] = jnp.zeros((1, 1024), F32) + (0.5 / D) * jnp.sum(loss_ref[...])

        small_ref[...] = jnp.zeros_like(small_ref)
        for k in range(LW):
            tmp[k:k + 1, :] = s8(dcw_ref, k)
        for k in range(KW):
            tmp[8 + k:9 + k, :] = s8(ddw_ref, k)
        for d in range(NDEV):
            small_ref[d, 0:LW, 0:128] = tmp[0:LW, 128 * d:128 * d + 128]
            small_ref[d, 8:8 + KW, 0:128] = tmp[8:8 + KW, 128 * d:128 * d + 128]
            small_ref[d, 40:56, :] = dh_ref[:, 256 * d:256 * d + 256]

    full = lambda a: pl.BlockSpec(a.shape, lambda i: (0,) * a.ndim)
    ins = [dprew_acc, dpostw_acc, cvecs, kvecs, lvecs, dcw_acc, ddw_acc]
    return pl.pallas_call(
        body, name="pack_grads",
        grid=(1,),
        in_specs=[full(a) for a in ins] + [full(dh), full(loss_acc)],
        out_specs=[pl.BlockSpec((_VEC_ROWS, 1024), lambda i: (0, 0)),
                   pl.BlockSpec((NDEV, _SM_ROWS, 256), lambda i: (0, 0, 0))],
        out_shape=[jax.ShapeDtypeStruct((_VEC_ROWS, 1024), F32), jax.ShapeDtypeStruct((NDEV, _SM_ROWS, 256), F32)],
        scratch_shapes=[pltpu.VMEM((40, 1024), F32)],
        compiler_params=_cparams(),
    )(*ins, dh, loss_acc)


def _adamw_vec(parts, W, M, V):
    nv = len(_VEC)

    def body(*refs):
        p_ref = refs[0]
        w_refs, m_refs, v_refs = refs[1:1 + nv], refs[1 + nv:1 + 2 * nv], refs[1 + 2 * nv:1 + 3 * nv]
        outs = refs[1 + 3 * nv:]

        def total(r):
            acc = p_ref[0, r:r + 1, :]
            for sidx in range(1, NDEV):
                acc = acc + p_ref[sidx, r:r + 1, :]
            return acc

        row = 0
        for idx, (_, nrows) in enumerate(_VEC):
            for part in range(nrows):
                cols = slice(1024 * part, 1024 * part + 1024)
                g = total(row + part)
                delta, mv, vv = _adam_math(g, w_refs[idx][:, cols], m_refs[idx][:, cols], v_refs[idx][:, cols])
                for o, val in zip(outs[4 * idx:4 * idx + 4], (g, delta, mv, vv)):
                    o[:, cols] = val
            row += nrows
        outs[-1][...] = total(_LOSS_ROW)[:, 0:128]

    names = [n for n, _ in _VEC]
    flat = lambda d: [d[n].reshape(1, -1) for n in names]
    ws, ms, vs = flat(W), flat(M), flat(V)
    res = pl.pallas_call(
        body, name="adamw_vec",
        out_shape=[jax.ShapeDtypeStruct(w.shape, F32) for w in ws for _ in range(4)]
        + [jax.ShapeDtypeStruct((1, 128), F32)],
        compiler_params=_cparams(),
    )(parts, *ws, *ms, *vs)
    return {n: tuple(res[4 * i:4 * i + 4]) for i, n in enumerate(names)}, res[-1]


def _adamw_small(parts, W, M, V):
    where = {"lru_conv_w": (slice(0, LW), slice(0, 128)), "conf_dw_w": (slice(8, 8 + KW), slice(0, 128)),
             "meta_tokens": (slice(40, 56), slice(0, 256))}
    names = list(where)

    def body(*refs):
        p_ref = refs[0]
        outs = refs[10:]
        for idx, n in enumerate(names):
            rs, cs = where[n]
            g = p_ref[0, rs, cs]
            for sidx in range(1, NDEV):
                g = g + p_ref[sidx, rs, cs]
            delta, mv, vv = _adam_math(g, refs[1 + idx][...], refs[4 + idx][...], refs[7 + idx][...])
            for o, val in zip(outs[4 * idx:4 * idx + 4], (g, delta, mv, vv)):
                o[...] = val

    two_d = lambda a: a.reshape(a.shape[-2:])
    ws, ms, vs = ([two_d(d[n]) for n in names] for d in (W, M, V))
    res = pl.pallas_call(
        body, name="adamw_small",
        out_shape=[jax.ShapeDtypeStruct(w.shape, F32) for w in ws for _ in range(4)],
        compiler_params=_cparams(),
    )(parts, *ws, *ms, *vs)
    return {n: tuple(res[4 * i:4 * i + 4]) for i, n in enumerate(names)}


def _pack_small(lru_cw, dw_w, meta):
    buf = jnp.zeros((_SM_ROWS, 256), F32)
    buf = buf.at[0:LW, 0:128].set(lru_cw)
    buf = buf.at[8:8 + dw_w.shape[0], 0:128].set(dw_w)
    return buf.at[40:56, :].set(meta)


def _block_diag4(w):
    w4 = w.reshape(NCB, 4, 64, 64)
    eye = jnp.eye(4, dtype=w.dtype)
    return jnp.einsum("ghij,hk->ghikj", w4, eye).reshape(NCB, CB, CB)


def _diag_blocks(g):
    g5 = g.reshape(NCB, 4, 64, 4, 64)
    return jnp.stack([g5[:, hh, :, hh, :] for hh in range(4)], axis=1).reshape(16, 64, 64)


def _gate_mats(W):
    return _block_diag4(W["w_gate_a"][0]).astype(BF16), _block_diag4(W["w_gate_x"][0]).astype(BF16)


def _local_step(x, target, meta_full, inproj, out_weights, lru_cw_full, dw_w_full, W, gate_mats, send):
    wa_g, wx_g = gate_mats

    h, hn = _prenorm(x, meta_full, W["pre_norm_w"])
    z, win_full = inproj(hn)
    ylru, xc, hs = _lru_fwd(z, lru_cw_full, W["lru_conv_b"], wa_g, W["b_gate_a"], wx_g, W["b_gate_x"],
                            W["lru_lambda"])
    vc = _conf_fwd_conv(z, dw_w_full, W["conf_dw_b"])
    wout_full, pw_full = out_weights(vc)
    yconf, p, xhat, rstd = _conf_fwd_proj(vc, z, W["conf_ln_w"], W["conf_ln_b"], pw_full, W["conf_pw_b"])
    dout, dy, loss_acc, dpostw_acc = _outproj_loss(ylru, yconf, wout_full, h, target, W["post_norm_w"])

    dycat, dwout_part = _outproj_bwd(dy, ylru, yconf, wout_full)
    tok = send("w_out", ("w_out", dwout_part))
    dvc, dz41, dpw_part, cvecs = _conf_bwd_proj(dycat, p, z, xhat, rstd, hs, W["conf_ln_w"], W["conf_ln_b"], pw_full, tok)
    tok = send("w_in_c", ("conf_pw_w", dpw_part), ("w_in_c", _inproj_wgrad("inproj_wgrad_c", hn, dz41, dz41)))
    dzc, ddw_acc, kvecs = _conf_bwd_conv(dvc, z, dw_w_full, tok)
    tok = send("w_in_b", ("w_in_b", _inproj_wgrad("inproj_wgrad_b", hn, dzc, dzc)))
    dzl, dwa_g, dwx_g, dcw_acc, lvecs = _lru_bwd(dycat, z, xc, hs, lru_cw_full, wa_g, W["b_gate_a"], wx_g,
                                                 W["b_gate_x"], W["lru_lambda"], tok)
    tok = send("w_gates", ("w_gate_a", _diag_blocks(dwa_g).reshape(16 * 64, 64)),
               ("w_gate_x", _diag_blocks(dwx_g).reshape(16 * 64, 64)))
    tok = send("w_in_a", ("w_in_a", _inproj_wgrad("inproj_wgrad_a", hn, dzl, tok)))
    grad_x, dmeta, dprew_acc = _inproj_bwd(dzl, dz41, dzc, win_full, h, dout, W["pre_norm_w"], tok)

    vec_pack, small_part = _pack_grads(dprew_acc, dpostw_acc, cvecs, kvecs, lvecs, dcw_acc, ddw_acc, dmeta, loss_acc)
    return grad_x, vec_pack, small_part


def kernel(x, meta_tokens, pre_norm_w, post_norm_w, w_in, b_in, lru_conv_w, lru_conv_b, w_gate_a, b_gate_a, w_gate_x, b_gate_x, lru_lambda, conf_dw_w, conf_dw_b, conf_ln_w, conf_ln_b, conf_pw_w, conf_pw_b, w_out, loss_target, m_meta_tokens, m_pre_norm_w, m_post_norm_w, m_w_in, m_b_in, m_lru_conv_w, m_lru_conv_b, m_w_gate_a, m_b_gate_a, m_w_gate_x, m_b_gate_x, m_lru_lambda, m_conf_dw_w, m_conf_dw_b, m_conf_ln_w, m_conf_ln_b, m_conf_pw_w, m_conf_pw_b, m_w_out, v_meta_tokens, v_pre_norm_w, v_post_norm_w, v_w_in, v_b_in, v_lru_conv_w, v_lru_conv_b, v_w_gate_a, v_b_gate_a, v_w_gate_x, v_b_gate_x, v_lru_lambda, v_conf_dw_w, v_conf_dw_b, v_conf_ln_w, v_conf_ln_b, v_conf_pw_w, v_conf_pw_b, v_w_out):
    W = dict(meta_tokens=meta_tokens, pre_norm_w=pre_norm_w, post_norm_w=post_norm_w, w_in=w_in, b_in=b_in,
             lru_conv_w=lru_conv_w, lru_conv_b=lru_conv_b, w_gate_a=w_gate_a, b_gate_a=b_gate_a,
             w_gate_x=w_gate_x, b_gate_x=b_gate_x, lru_lambda=lru_lambda, conf_dw_w=conf_dw_w,
             conf_dw_b=conf_dw_b, conf_ln_w=conf_ln_w, conf_ln_b=conf_ln_b, conf_pw_w=conf_pw_w,
             conf_pw_b=conf_pw_b, w_out=w_out)
    M = dict(meta_tokens=m_meta_tokens, pre_norm_w=m_pre_norm_w, post_norm_w=m_post_norm_w, w_in=m_w_in,
             b_in=m_b_in, lru_conv_w=m_lru_conv_w, lru_conv_b=m_lru_conv_b, w_gate_a=m_w_gate_a,
             b_gate_a=m_b_gate_a, w_gate_x=m_w_gate_x, b_gate_x=m_b_gate_x, lru_lambda=m_lru_lambda,
             conf_dw_w=m_conf_dw_w, conf_dw_b=m_conf_dw_b, conf_ln_w=m_conf_ln_w, conf_ln_b=m_conf_ln_b,
             conf_pw_w=m_conf_pw_w, conf_pw_b=m_conf_pw_b, w_out=m_w_out)
    V = dict(meta_tokens=v_meta_tokens, pre_norm_w=v_pre_norm_w, post_norm_w=v_post_norm_w, w_in=v_w_in,
             b_in=v_b_in, lru_conv_w=v_lru_conv_w, lru_conv_b=v_lru_conv_b, w_gate_a=v_w_gate_a,
             b_gate_a=v_b_gate_a, w_gate_x=v_w_gate_x, b_gate_x=v_b_gate_x, lru_lambda=v_lru_lambda,
             conf_dw_w=v_conf_dw_w, conf_dw_b=v_conf_dw_b, conf_ln_w=v_conf_ln_w, conf_ln_b=v_conf_ln_b,
             conf_pw_w=v_conf_pw_w, conf_pw_b=v_conf_pw_b, w_out=v_w_out)
    names = list(W.keys())
    shapes = {n: W[n].shape for n in names}

    small = _pack_small(lru_conv_w[0], conf_dw_w[0], meta_tokens)
    (small_flight,), tok = _exchange_start("gather_small_start", [
        (small, jax.ShapeDtypeStruct((NDEV, _SM_ROWS, 256), F32), _whole, _slot)])
    win_flight, tok = _win_gather_start(w_in[0].astype(BF16) + tok[0, 0].astype(BF16))
    win_flight, tok = _win_gather_links(win_flight, tok)
    gate_mats = _gate_mats(W)
    gathered, tok = _exchange_start("gather_out_start", [
        (w_out[0].astype(BF16) + tok[0, 0].astype(BF16), jax.ShapeDtypeStruct((D, D), BF16), _whole,
         _rows(D // NDEV)),
        (conf_pw_w[0].astype(BF16), jax.ShapeDtypeStruct((DC, DC), BF16), _whole, _rows(DC // NDEV)),
    ])
    (small_all,) = _exchange_wait("gather_small_wait", [small_flight], tok)
    unshard = lambda a: jnp.transpose(a, (1, 0, 2)).reshape(a.shape[1], -1)
    lru_cw_full = unshard(small_all[:, 0:LW, 0:128])
    dw_w_full = unshard(small_all[:, 8:8 + KWP, 0:128])
    meta_full = unshard(small_all[:, 40:56, :])

    def out_weights(after):
        return _exchange_wait("gather_out_wait", gathered, after)

    def inproj(hn):
        xi, yi, ci = lax.axis_index("x"), lax.axis_index("y"), lax.axis_index("c")
        shard = lambda px, py, pc: (4 * px + 2 * py + pc).astype(jnp.int32)
        over_links = jnp.stack([shard(1 - xi, yi, ci), shard(xi, 1 - yi, ci), shard(1 - xi, 1 - yi, ci)])
        z, src = _inproj_cols("inproj_own", jnp.stack([shard(xi, yi, ci)]), hn, win_flight["src"], b_in, None)
        flight = _win_gather_early(dict(win_flight, src=src))
        z, land = _inproj_cols("inproj_here", jnp.stack([shard(xi, yi, 1 - ci)]), hn, flight["land"], b_in, z)
        flight = _win_gather_forward("all", dict(flight, land=land), (1, 2, 3), z)
        z, land = _inproj_cols("inproj_links", over_links, hn, flight["land"], b_in, z)
        flight = _win_gather_forwarded("all", dict(flight, land=land), (1, 2, 3))
        z, land = _inproj_cols("inproj_sibling", over_links + 1 - 2 * ci, hn, flight["land"], b_in, z)
        return z, _win_gather_wait(dict(flight, land=land))

    row_stage = lambda ncol: (jax.ShapeDtypeStruct((NDEV, D // NDEV, ncol), BF16), _rows(D // NDEV))
    piece = {"w_in_a": row_stage(1024), "w_in_b": row_stage(2048), "w_in_c": row_stage(2048),
             "w_out": row_stage(D),
             "conf_pw_w": (jax.ShapeDtypeStruct((NDEV, DC // NDEV, DC), BF16), _rows(DC // NDEV)),
             "w_gate_a": (jax.ShapeDtypeStruct((NDEV, 16 * 64, 64), BF16), _whole),
             "w_gate_x": (jax.ShapeDtypeStruct((NDEV, 16 * 64, 64), BF16), _whole)}
    sent = {}

    def send(call, *named_parts):
        handles, token = _exchange_start(
            "scatter_" + call + "_start",
            [(part.astype(BF16), piece[name][0], piece[name][1], _slot) for name, part in named_parts])
        for (name, _), handle in zip(named_parts, handles):
            sent[name] = [handle]
        return token

    grad_x, vec_pack, small_part = _local_step(
        x[0], loss_target[0], meta_full, inproj, out_weights, lru_cw_full, dw_w_full, W, gate_mats, send)
    grad_x = grad_x[None]

    rest, tok = _exchange_start("scatter_rest_start", [
        (small_part, jax.ShapeDtypeStruct((NDEV, _SM_ROWS, 256), F32), _slot, _slot),
        (vec_pack, jax.ShapeDtypeStruct((NDEV, _VEC_ROWS, 1024), F32), _whole, _slot),
    ])
    (parts_c,) = _exchange_wait("scatter_w_in_c_wait", sent["w_in_c"], tok)
    (parts_b,) = _exchange_wait("scatter_w_in_b_wait", sent["w_in_b"], parts_c)
    (parts_a,) = _exchange_wait("scatter_w_in_a_wait", sent["w_in_a"], parts_b)
    win_rows = _sum_win_parts(parts_a, parts_b, parts_c)
    win_stage2, tok = _exchange_start("scatter_w_in_stage2_start", [
        (win_rows, jax.ShapeDtypeStruct((NDEV, D // NDEV, NIN // NDEV), BF16), _cols(NIN // NDEV), _slot)])

    G, DW, NM, NV = {}, {}, {}, {}
    (wout_parts,) = _exchange_wait("scatter_w_out_wait", sent["w_out"], tok)
    G["w_out"], DW["w_out"], NM["w_out"], NV["w_out"] = _adamw("adamw_w_out", wout_parts, w_out[0], m_w_out[0], v_w_out[0], 64)
    (pw_parts,) = _exchange_wait("scatter_conf_pw_w_wait", sent["conf_pw_w"], G["w_out"])
    G["conf_pw_w"], DW["conf_pw_w"], NM["conf_pw_w"], NV["conf_pw_w"] = _adamw(
        "adamw_pw", pw_parts, conf_pw_w[0], m_conf_pw_w[0], v_conf_pw_w[0], 128)
    res = {}
    wa_parts, wx_parts = _exchange_wait("scatter_w_gates_wait", sent["w_gate_a"] + sent["w_gate_x"], G["conf_pw_w"])
    for n, parts in (("w_gate_a", wa_parts), ("w_gate_x", wx_parts)):
        res[n] = _adamw("adamw_" + n, parts, *[d[n].reshape(16 * 64, 64) for d in (W, M, V)], 16 * 64)
    small_parts, vec_parts = _exchange_wait("scatter_rest_wait", rest, res["w_gate_x"][0])
    res.update(_adamw_small(small_parts, W, M, V))
    vec_res, loss_row = _adamw_vec(vec_parts, W, M, V)
    res.update(vec_res)
    (win_sum,) = _exchange_wait("scatter_w_in_stage2_wait", win_stage2, loss_row)
    res["w_in"] = _adamw("adamw_w_in", win_sum.reshape(1, D, NIN // NDEV), w_in[0], m_w_in[0], v_w_in[0], 256)
    for n, vals in res.items():
        for dst, val in zip((G, DW, NM, NV), vals):
            dst[n] = val
    for dst in (G, DW, NM, NV):
        for n in names:
            dst[n] = dst[n].reshape(shapes[n])
    loss = loss_row[0, 0]

    return (loss, grad_x, *[G[n] for n in names], *[DW[n] for n in names],
            *[NM[n] for n in names], *[NV[n] for n in names])
```

```python
import functools

import jax
import jax.numpy as jnp
from jax import lax
from jax.experimental import pallas as pl
from jax.experimental.pallas import tpu as pltpu

F32 = jnp.float32
BF16 = jnp.bfloat16

D = 2048
DL = 1024
DC = 1024
NIN = 5120
NMETA = 16
SEQ = 2048
T = NMETA + SEQ
TP = 2176
TM = 544
CB = 256
NCB = DL // CB
R = 16
KW = 31
KWP = 32
LW = 4
LRU_C = 8.0
EPS = 1e-6
NDEV = 8

ADAM_LR = 0.001
ADAM_B1 = 0.9
ADAM_B2 = 0.999
ADAM_EPS = 1e-08
ADAM_WD = 0.01
ADAM_STEP = 10

VMEM_LIMIT = 56 * 1024 * 1024


def _cparams():
    return pltpu.CompilerParams(vmem_limit_bytes=VMEM_LIMIT)


def _sig(x):
    return 1.0 / (1.0 + jnp.exp(-x))


def _expm1_neg(y):
    poly = y * (1.0 + y * (0.5 + y * (1.0 / 6.0 + y * (1.0 / 24.0 + y * (1.0 / 120.0)))))
    return jnp.where(y > -0.1, poly, jnp.exp(y) - 1.0)


def _softplus(x):
    e = jnp.exp(-jnp.abs(x))
    w = 1.0 + e
    l1p = jnp.where(w == 1.0, e, jnp.log(w) * e / (w - 1.0))
    return jnp.maximum(x, 0.0) + l1p


def _row_iota(shape):
    return lax.broadcasted_iota(jnp.int32, shape, 0)


def _fold8(v):
    return v[0:8, :] + v[8:16, :]


_FLIPS = [(k >> 2 & 1, k >> 1 & 1, k & 1) for k in range(1, NDEV)]
_HBM = pl.BlockSpec(memory_space=pltpu.HBM)
_SEM = pl.BlockSpec(memory_space=pltpu.SEMAPHORE)


def _peers():
    x, y, c = lax.axis_index("x"), lax.axis_index("y"), lax.axis_index("c")
    out = []
    for dx, dy, dc in _FLIPS:
        px = 1 - x if dx else x
        py = 1 - y if dy else y
        pc = 1 - c if dc else c
        out.append(((px, py, pc), 4 * px + 2 * py + pc))
    return 4 * x + 2 * y + c, out


def _exchange_start(name, items):
    n = len(items)

    def body(*refs):
        srcs, lands = refs[:n], refs[n:2 * n]
        outs = refs[2 * n:]
        send_sems, recv_sems, local_sems = outs[:n], outs[n:2 * n], outs[2 * n:3 * n]
        token = outs[-1]
        me, peers = _peers()
        for a in range(n):
            src_at, dst_at = items[a][2], items[a][3]
            pltpu.make_async_copy(src_at(srcs[a], me), dst_at(lands[a], me), local_sems[a]).start()
        for a in range(n):
            src_at, dst_at = items[a][2], items[a][3]
            for k, (pos, peer) in enumerate(peers):
                pltpu.make_async_remote_copy(
                    src_ref=src_at(srcs[a], peer), dst_ref=dst_at(lands[a], me),
                    send_sem=send_sems[a].at[k], recv_sem=recv_sems[a].at[k],
                    device_id=pos, device_id_type=pl.DeviceIdType.MESH).start()
        token[...] = jnp.zeros_like(token)

    srcs = [pltpu.with_memory_space_constraint(it[0], pltpu.HBM) for it in items]
    lands = [pltpu.with_memory_space_constraint(lax.empty(it[1].shape, it[1].dtype), pltpu.HBM) for it in items]
    sem7 = pltpu.SemaphoreType.DMA((NDEV - 1,))
    res = pl.pallas_call(
        body, name=name,
        out_shape=([sem7] * (2 * n) + [pltpu.SemaphoreType.DMA(())] * n
                   + [pltpu.HBM(a.shape, a.dtype) for a in srcs] + [pltpu.HBM(a.shape, a.dtype) for a in lands]
                   + [jax.ShapeDtypeStruct((8, 128), F32)]),
        in_specs=[_HBM] * (2 * n),
        out_specs=[_SEM] * (3 * n) + [_HBM] * (2 * n) + [pl.BlockSpec(memory_space=pltpu.VMEM)],
        input_output_aliases={i: 3 * n + i for i in range(2 * n)},
        compiler_params=pltpu.CompilerParams(has_side_effects=pltpu.SideEffectType.DATAFLOW_SIDE_EFFECTING),
    )(*srcs, *lands)
    handles = [dict(send=res[a], recv=res[n + a], local=res[2 * n + a], src=res[3 * n + a], land=res[4 * n + a],
                    src_at=items[a][2], dst_at=items[a][3]) for a in range(n)]
    return handles, res[-1]


def _wait_bytes(piece, sem):
    pltpu.make_async_copy(piece, piece, sem).wait()


def _exchange_wait(name, handles, after):
    n = len(handles)

    def body(*refs):
        srcs, lands = refs[:n], refs[n:2 * n]
        send_sems, recv_sems, local_sems = refs[2 * n:3 * n], refs[3 * n:4 * n], refs[4 * n:5 * n]
        me, peers = _peers()
        for a in range(n):
            src_at, dst_at = handles[a]["src_at"], handles[a]["dst_at"]
            for k, (pos, peer) in enumerate(peers):
                _wait_bytes(src_at(srcs[a], peer), send_sems[a].at[k])
                _wait_bytes(dst_at(lands[a], peer), recv_sems[a].at[k])
            pltpu.make_async_copy(src_at(srcs[a], me), dst_at(lands[a], me), local_sems[a]).wait()

    srcs = [hd["src"] for hd in handles]
    lands = [hd["land"] for hd in handles]
    res = pl.pallas_call(
        body, name=name,
        out_shape=[pltpu.HBM(a.shape, a.dtype) for a in srcs] + [pltpu.HBM(a.shape, a.dtype) for a in lands],
        in_specs=[_HBM] * (2 * n) + [_SEM] * (3 * n) + [pl.BlockSpec(memory_space=pl.ANY)],
        out_specs=[_HBM] * (2 * n),
        input_output_aliases={i: i for i in range(2 * n)},
        compiler_params=pltpu.CompilerParams(has_side_effects=pltpu.SideEffectType.DATAFLOW_SIDE_EFFECTING),
    )(*srcs, *lands, *[hd["send"] for hd in handles], *[hd["recv"] for hd in handles],
      *[hd["local"] for hd in handles], after)
    return list(res[n:])


_SIDE = pltpu.SideEffectType.DATAFLOW_SIDE_EFFECTING
_WCOLS = NIN // NDEV


def _win_cols(ref, l):
    return ref.at[:, pl.ds(pl.multiple_of(l * _WCOLS, 128), _WCOLS)]


def _win_routes():
    x, y, c = lax.axis_index("x"), lax.axis_index("y"), lax.axis_index("c")
    pos = [(x, y, 1 - c), (1 - x, y, c), (x, 1 - y, c), (1 - x, 1 - y, c)]
    return 4 * x + 2 * y + c, [(p, 4 * p[0] + 2 * p[1] + p[2]) for p in pos]


def _win_gather_start(shard):
    def body(src, land, send_sem, recv_sem, local_sem, src_thru, land_thru, token):
        me, routes = _win_routes()
        pltpu.make_async_copy(src, _win_cols(land, me), local_sem).start()
        pltpu.make_async_remote_copy(src_ref=src, dst_ref=_win_cols(land, me), send_sem=send_sem, recv_sem=recv_sem,
                                     device_id=routes[0][0], device_id_type=pl.DeviceIdType.MESH).start()
        token[...] = jnp.zeros_like(token)

    src = pltpu.with_memory_space_constraint(shard, pltpu.HBM)
    land = pltpu.with_memory_space_constraint(lax.empty((D, NIN), BF16), pltpu.HBM)
    sem = pltpu.SemaphoreType.DMA(())
    res = pl.pallas_call(
        body, name="win_gather_start",
        out_shape=[sem, sem, sem, pltpu.HBM(src.shape, BF16), pltpu.HBM(land.shape, BF16),
                   jax.ShapeDtypeStruct((8, 128), F32)],
        in_specs=[_HBM, _HBM],
        out_specs=[_SEM, _SEM, _SEM, _HBM, _HBM, pl.BlockSpec(memory_space=pltpu.VMEM)],
        input_output_aliases={0: 3, 1: 4},
        compiler_params=pltpu.CompilerParams(has_side_effects=_SIDE),
    )(src, land)
    return dict(send0=res[0], recv0=res[1], local=res[2], src=res[3], land=res[4]), res[5]


def _win_gather_links(hd, after):
    def body(src, land, after_ref, send_sems, recv_sems, src_thru, land_thru, token):
        me, routes = _win_routes()
        for k in (1, 2, 3):
            pltpu.make_async_remote_copy(src_ref=src, dst_ref=_win_cols(land, me), send_sem=send_sems.at[k - 1],
                                         recv_sem=recv_sems.at[k - 1], device_id=routes[k][0],
                                         device_id_type=pl.DeviceIdType.MESH).start()
        token[...] = jnp.zeros_like(token)

    sem3 = pltpu.SemaphoreType.DMA((3,))
    res = pl.pallas_call(
        body, name="win_gather_links",
        out_shape=[sem3, sem3, pltpu.HBM(hd["src"].shape, BF16), pltpu.HBM(hd["land"].shape, BF16),
                   jax.ShapeDtypeStruct((8, 128), F32)],
        in_specs=[_HBM, _HBM, pl.BlockSpec(memory_space=pl.ANY)],
        out_specs=[_SEM, _SEM, _HBM, _HBM, pl.BlockSpec(memory_space=pltpu.VMEM)],
        input_output_aliases={0: 2, 1: 3},
        compiler_params=pltpu.CompilerParams(has_side_effects=_SIDE),
    )(hd["src"], hd["land"], after)
    return dict(hd, send=res[0], recv=res[1], src=res[2], land=res[3]), res[4]


def _win_gather_forward(name, hd, ks, after):
    def body(land, recv_sems, after_ref, land_thru, fsend_sems, frecv_sems):
        me, routes = _win_routes()
        sibling = routes[0][0]
        for n, k in enumerate(ks):
            pos, peer = routes[k]
            piece = _win_cols(land, peer)
            pltpu.make_async_remote_copy(src_ref=piece, dst_ref=piece, send_sem=fsend_sems.at[n],
                                         recv_sem=recv_sems.at[k - 1], device_id=pos,
                                         device_id_type=pl.DeviceIdType.MESH).wait_recv()
            pltpu.make_async_remote_copy(src_ref=piece, dst_ref=piece, send_sem=fsend_sems.at[n],
                                         recv_sem=frecv_sems.at[n], device_id=sibling,
                                         device_id_type=pl.DeviceIdType.MESH).start()

    sems = pltpu.SemaphoreType.DMA((len(ks),))
    res = pl.pallas_call(
        body, name="win_gather_forward_" + name,
        out_shape=[pltpu.HBM(hd["land"].shape, BF16), sems, sems],
        in_specs=[_HBM, _SEM, pl.BlockSpec(memory_space=pl.ANY)],
        out_specs=[_HBM, _SEM, _SEM],
        input_output_aliases={0: 0},
        compiler_params=pltpu.CompilerParams(has_side_effects=_SIDE),
    )(hd["land"], hd["recv"], after)
    return dict(hd, land=res[0], **{"fsend" + name: res[1], "frecv" + name: res[2]})


def _win_gather_forwarded(name, hd, ks):
    def body(land, fsend_sems, frecv_sems, land_thru):
        me, routes = _win_routes()
        sib_c = routes[0][0][2]
        for n, k in enumerate(ks):
            _wait_bytes(_win_cols(land, routes[k][1]), fsend_sems.at[n])
            _wait_bytes(_win_cols(land, 4 * routes[k][0][0] + 2 * routes[k][0][1] + sib_c), frecv_sems.at[n])

    res = pl.pallas_call(
        body, name="win_gather_forwarded_" + name,
        out_shape=[pltpu.HBM(hd["land"].shape, BF16)],
        in_specs=[_HBM, _SEM, _SEM],
        out_specs=[_HBM],
        input_output_aliases={0: 0},
        compiler_params=pltpu.CompilerParams(has_side_effects=_SIDE),
    )(hd["land"], hd["fsend" + name], hd["frecv" + name])
    return dict(hd, land=res[0])


def _win_gather_early(hd):
    def body(src, land, recv_sem, local_sem, src_thru, land_thru):
        me, routes = _win_routes()
        _wait_bytes(_win_cols(land, routes[0][1]), recv_sem)
        pltpu.make_async_copy(src, _win_cols(land, me), local_sem).wait()

    res = pl.pallas_call(
        body, name="win_gather_early",
        out_shape=[pltpu.HBM(hd["src"].shape, BF16), pltpu.HBM(hd["land"].shape, BF16)],
        in_specs=[_HBM, _HBM, _SEM, _SEM],
        out_specs=[_HBM, _HBM],
        input_output_aliases={0: 0, 1: 1},
        compiler_params=pltpu.CompilerParams(has_side_effects=_SIDE),
    )(hd["src"], hd["land"], hd["recv0"], hd["local"])
    return dict(hd, src=res[0], land=res[1])


def _win_gather_wait(hd):
    def body(src, land, send0_sem, send_sems, src_thru, land_thru):
        for k in range(4):
            _wait_bytes(src, send0_sem if k == 0 else send_sems.at[k - 1])

    res = pl.pallas_call(
        body, name="win_gather_wait",
        out_shape=[pltpu.HBM(hd["src"].shape, BF16), pltpu.HBM(hd["land"].shape, BF16)],
        in_specs=[_HBM, _HBM, _SEM, _SEM],
        out_specs=[_HBM, _HBM],
        input_output_aliases={0: 0, 1: 1},
        compiler_params=pltpu.CompilerParams(has_side_effects=_SIDE),
    )(hd["src"], hd["land"], hd["send0"], hd["send"])
    return res[1]


def _whole(ref, l):
    return ref


def _slot(ref, l):
    return ref.at[l]


def _cols(width):
    def at(ref, l):
        return ref.at[:, pl.ds(pl.multiple_of(l * width, 128), width)]
    return at


def _rows(height):
    def at(ref, l):
        return ref.at[pl.ds(pl.multiple_of(l * height, 8), height), :]
    return at


NTILE = TP // TM


def _tile_rows(t):
    lo = max(t * TM - NMETA, 0)
    hi = min((t + 1) * TM - NMETA, SEQ)
    return lo, hi - lo, lo + NMETA - t * TM


def _for_tile(t, fn):
    for static_t in range(NTILE):
        pl.when(t == static_t)(functools.partial(fn, static_t))


def _token_tile_copy(hbm_ref, buf, sem, t):
    lo, n, off = _tile_rows(t)
    return pltpu.make_async_copy(hbm_ref.at[pl.ds(lo, n)], buf.at[pl.ds(off, n)], sem)


def _prenorm(x, meta_full, pre_w):
    def body(x_ref, meta_ref, pw_ref, h_ref, hn_ref, xbuf, sems):
        i = pl.program_id(0)
        slot = i % 2

        def start(t):
            _token_tile_copy(x_ref, xbuf.at[t % 2], sems.at[t % 2], t).start()

        @pl.when(i == 0)
        def _():
            start(0)
        _for_tile(i + 1, start)
        _for_tile(i, lambda t: _token_tile_copy(x_ref, xbuf.at[t % 2], sems.at[t % 2], t).wait())

        @pl.when(i == 0)
        def _():
            xbuf[0, 0:NMETA, :] = meta_ref[...]

        @pl.when(i == NTILE - 1)
        def _():
            last = _tile_rows(NTILE - 1)[1]
            xbuf[(NTILE - 1) % 2, last:TM, :] = jnp.zeros((TM - last, D), F32)

        pw = pw_ref[...]

        def chunk(ci, carry):
            r0 = pl.multiple_of(ci * R, R)
            xv = xbuf[slot, pl.ds(r0, R), :]
            h_ref[pl.ds(r0, R), :] = xv
            ms = jnp.mean(xv * xv, axis=-1, keepdims=True)
            hn_ref[pl.ds(r0, R), :] = (xv * lax.rsqrt(ms + EPS) * pw).astype(BF16)
            return carry
        lax.fori_loop(0, TM // R, chunk, 0, unroll=2)

    row = pl.BlockSpec((TM, D), lambda i: (i, 0))
    return pl.pallas_call(
        body, name="prenorm",
        grid=(NTILE,),
        in_specs=[pl.BlockSpec(memory_space=pl.ANY), pl.BlockSpec((NMETA, D), lambda i: (0, 0)),
                  pl.BlockSpec((1, D), lambda i: (0, 0))],
        out_specs=[row, row],
        out_shape=[jax.ShapeDtypeStruct((TP, D), F32), jax.ShapeDtypeStruct((TP, D), BF16)],
        scratch_shapes=[pltpu.VMEM((2, TM, D), F32), pltpu.SemaphoreType.DMA((2,))],
        compiler_params=_cparams(),
    )(x, meta_full, pre_w)


def _inproj_cols(name, shards, hn, w_land, b_in, z_prev):
    nsh = shards.shape[0]
    one_shard = w_land.shape[1] == _WCOLS

    def body(idx_ref, hn_ref, w_ref, b_ref, *rest):
        z_ref = rest[-2]
        z_ref[...] = jnp.dot(hn_ref[...], w_ref[...], preferred_element_type=F32) + b_ref[...]

    any_spec = pl.BlockSpec(memory_space=pl.ANY)
    in_specs = [pl.BlockSpec((TM, D), lambda j, i, idx: (i, 0)),
                pl.BlockSpec((D, _WCOLS), lambda j, i, idx: (0, 0 if one_shard else idx[j])),
                pl.BlockSpec((1, _WCOLS), lambda j, i, idx: (0, idx[j]))]
    operands = [hn, w_land, b_in]
    aliases = {2: 1}
    if z_prev is not None:
        in_specs.append(any_spec)
        operands.append(z_prev)
        aliases[4] = 0
    return pl.pallas_call(
        body, name=name,
        grid_spec=pltpu.PrefetchScalarGridSpec(
            num_scalar_prefetch=1, grid=(nsh, TP // TM), in_specs=in_specs,
            out_specs=[pl.BlockSpec((TM, _WCOLS), lambda j, i, idx: (i, idx[j])), any_spec]),
        out_shape=[jax.ShapeDtypeStruct((TP, NIN), F32), jax.ShapeDtypeStruct(w_land.shape, w_land.dtype)],
        input_output_aliases=aliases,
        compiler_params=_cparams(),
    )(shards, *operands)


def _gate_values(ga, gx, xc, sp8):
    r = _sig(ga)
    i = _sig(gx)
    log_a = -(r * sp8)
    a = jnp.exp(log_a)
    mult = jnp.sqrt(-_expm1_neg(2.0 * log_a))
    return r, i, a, mult


def _lru_fwd(z, conv_w, conv_b, wa_g, b_a, wx_g, b_x, lam):
    def body(x_ref, g_ref, cw_ref, cb_ref, wa_ref, ba_ref, wx_ref, bx_ref, lam_ref,
             y_ref, xc_ref, hs_ref, ga_s, gx_s):
        taps = [cw_ref[k:k + 1, :] for k in range(LW)]
        cb = cb_ref[...]

        def conv_chunk(ci, carry):
            r0 = pl.multiple_of(ci * R, R)
            cur = x_ref[pl.ds(r0, R), :]
            p0 = pl.multiple_of(jnp.maximum(r0 - 8, 0), 8)
            prev = jnp.where(ci > 0, x_ref[pl.ds(p0, 8), :], 0.0)
            buf = jnp.concatenate([prev, cur], axis=0)
            acc = cur * taps[LW - 1] + cb
            for s in range(1, LW):
                acc = acc + pltpu.roll(buf, s, 0)[8:8 + R, :] * taps[LW - 1 - s]
            xc_ref[pl.ds(r0, R), :] = acc
            return carry
        lax.fori_loop(0, TP // R, conv_chunk, 0)

        def gate_chunk(ci, carry):
            r0 = pl.multiple_of(ci * TM, TM)
            xb = xc_ref[pl.ds(r0, TM), :].astype(BF16)
            ga_s[pl.ds(r0, TM), :] = jnp.dot(xb, wa_ref[...], preferred_element_type=F32) + ba_ref[...]
            gx_s[pl.ds(r0, TM), :] = jnp.dot(xb, wx_ref[...], preferred_element_type=F32) + bx_ref[...]
            return carry
        lax.fori_loop(0, TP // TM, gate_chunk, 0)

        sp8 = LRU_C * _softplus(-lam_ref[...])
        row = _row_iota((R, CB))

        def scan_chunk(ci, hprev):
            r0 = pl.multiple_of(ci * R, R)
            xc = xc_ref[pl.ds(r0, R), :]
            _, i, a, mult = _gate_values(ga_s[pl.ds(r0, R), :], gx_s[pl.ds(r0, R), :], xc, sp8)
            u = mult * (i * xc)
            k = 1
            while k < R:
                m = row >= k
                u = jnp.where(m, a * pltpu.roll(u, k, 0) + u, u)
                a = jnp.where(m, a * pltpu.roll(a, k, 0), a)
                k *= 2
            hv = u + a * hprev
            hs_ref[pl.ds(r0, R), :] = hv
            g = g_ref[pl.ds(r0, R), :]
            y_ref[pl.ds(r0, R), :] = (hv * (g * _sig(g))).astype(BF16)
            return jnp.sum(jnp.where(row == R - 1, hv, 0.0), axis=0, keepdims=True)
        def scan_pass(i, hp):
            for sub in range(4):
                hp = scan_chunk(4 * i + sub, hp)
            return hp
        lax.fori_loop(0, TP // R // 4, scan_pass, jnp.zeros((1, CB), F32))

    col = lambda off: pl.BlockSpec((TP, CB), lambda j: (0, off + j))
    vec = pl.BlockSpec((1, CB), lambda j: (0, j))
    wsp = pl.BlockSpec((None, CB, CB), lambda j: (j, 0, 0))
    return pl.pallas_call(
        body, name="lru_fwd",
        grid=(NCB,),
        in_specs=[col(0), col(NCB), pl.BlockSpec((LW, CB), lambda j: (0, j)), vec, wsp, vec, wsp, vec, vec],
        out_specs=[col(0), col(0), col(0)],
        out_shape=[jax.ShapeDtypeStruct((TP, DL), BF16), jax.ShapeDtypeStruct((TP, DL), F32),
                   jax.ShapeDtypeStruct((TP, DL), F32)],
        scratch_shapes=[pltpu.VMEM((TP, CB), F32), pltpu.VMEM((TP, CB), F32)],
        compiler_params=_cparams(),
    )(z, z, conv_w, conv_b, wa_g, b_a, wx_g, b_x, lam)


CBC = 128
NCBC = DC // CBC
RC = 128


def _fold_rows(v):
    acc = v[0:8, :]
    for r in range(8, v.shape[0], 8):
        acc = acc + v[r:r + 8, :]
    return acc


def _conf_fwd_conv(z, dw_w, dw_b):
    def body(u1_ref, u2_ref, w_ref, b_ref, vc_ref, vs):
        vs[pl.ds(0, KWP), :] = jnp.zeros((KWP, CBC), F32)

        def glu_chunk(ci, carry):
            r0 = pl.multiple_of(ci * RC, RC)
            vs[pl.ds(KWP + r0, RC), :] = u1_ref[pl.ds(r0, RC), :] * _sig(u2_ref[pl.ds(r0, RC), :])
            return carry
        lax.fori_loop(0, TP // RC, glu_chunk, 0)

        bias = b_ref[...]

        def conv_chunk(ci, carry):
            r0 = pl.multiple_of(ci * RC, RC)
            buf = vs[pl.ds(r0, KWP + RC), :]
            acc = jnp.zeros((RC, CBC), F32) + bias
            for rr in range(8):
                rolled = buf if rr == 0 else pltpu.roll(buf, rr, 0)
                for q in range(4):
                    s = 8 * q + rr
                    if s > KW - 1:
                        continue
                    k = KW - 1 - s
                    acc = acc + rolled[KWP - 8 * q:KWP - 8 * q + RC, :] * w_ref[k:k + 1, :]
            vc_ref[pl.ds(r0, RC), :] = acc
            return carry
        lax.fori_loop(0, TP // RC, conv_chunk, 0)

    return pl.pallas_call(
        body, name="conf_fwd_conv",
        grid=(NCBC,),
        in_specs=[pl.BlockSpec((TP, CBC), lambda j: (0, 2 * NCBC + j)),
                  pl.BlockSpec((TP, CBC), lambda j: (0, 3 * NCBC + j)),
                  pl.BlockSpec((KWP, CBC), lambda j: (0, j)),
                  pl.BlockSpec((1, CBC), lambda j: (0, j))],
        out_specs=pl.BlockSpec((TP, CBC), lambda j: (0, j)),
        out_shape=jax.ShapeDtypeStruct((TP, DC), F32),
        scratch_shapes=[pltpu.VMEM((TP + KWP, CBC), F32)],
        compiler_params=_cparams(),
    )(z, z, dw_w, dw_b)


def _ln_chunk(vc, lw, lb):
    mu = jnp.mean(vc, axis=-1, keepdims=True)
    xm = vc - mu
    var = jnp.mean(xm * xm, axis=-1, keepdims=True)
    rstd = lax.rsqrt(var + EPS)
    xhat = xm * rstd
    return xhat, rstd, xhat * lw + lb


def _conf_fwd_proj(vc, z, ln_w, ln_b, pw_w, pw_b):
    def body(vc_ref, g_ref, lw_ref, lb_ref, w_ref, b_ref, y_ref, p_ref, xhat_ref, rstd_ref, s_s):
        lw, lb = lw_ref[...], lb_ref[...]

        def ln_chunk(ci, carry):
            r0 = pl.multiple_of(ci * R, R)
            for half in range(2):
                rr = r0 + 8 * half
                xhat, rstd, ln = _ln_chunk(vc_ref[pl.ds(rr, 8), :], lw, lb)
                xhat_ref[pl.ds(rr, 8), :] = xhat
                rstd_ref[pl.ds(rr, 8), :] = jnp.broadcast_to(rstd, (8, 128))
                p_ref[pl.ds(rr, 8), :] = ln * _sig(ln)
            s_s[pl.ds(r0, R), :] = p_ref[pl.ds(r0, R), :].astype(BF16)
            return carry
        lax.fori_loop(0, TM // R, ln_chunk, 0, unroll=2)

        p_ref[...] = jnp.dot(s_s[...], w_ref[...], preferred_element_type=F32) + b_ref[...]

        def out_chunk(ci, carry):
            r0 = pl.multiple_of(ci * R, R)
            g = g_ref[pl.ds(r0, R), :]
            y_ref[pl.ds(r0, R), :] = (p_ref[pl.ds(r0, R), :] * (g * _sig(g))).astype(BF16)
            return carry
        lax.fori_loop(0, TM // R, out_chunk, 0)

    row = pl.BlockSpec((TM, DC), lambda i: (i, 0))
    vec = pl.BlockSpec((1, DC), lambda i: (0, 0))
    return pl.pallas_call(
        body, name="conf_fwd_proj",
        grid=(TP // TM,),
        in_specs=[row, pl.BlockSpec((TM, DC), lambda i: (i, 4)), vec, vec,
                  pl.BlockSpec((DC, DC), lambda i: (0, 0)), vec],
        out_specs=[row, row, row, pl.BlockSpec((TM, 128), lambda i: (i, 0))],
        out_shape=[jax.ShapeDtypeStruct((TP, DC), BF16), jax.ShapeDtypeStruct((TP, DC), F32),
                   jax.ShapeDtypeStruct((TP, DC), F32), jax.ShapeDtypeStruct((TP, 128), F32)],
        scratch_shapes=[pltpu.VMEM((TM, DC), BF16)],
        compiler_params=_cparams(),
    )(vc, z, ln_w, ln_b, pw_w, pw_b)


def _outproj_loss(ylru, yconf, w_out, h, target, post_w):
    def body(yl_ref, yc_ref, w_ref, h_ref, tgt_hbm, pw_ref, dout_ref, dy_ref, loss_ref, dpw_ref, y_s, t_ref, sem):
        i = pl.program_id(0)
        k = pl.program_id(1)

        @pl.when(k == 0)
        def _():
            _for_tile(i, lambda t: _token_tile_copy(tgt_hbm, t_ref, sem, t).start())
            y_s[...] = jnp.dot(yl_ref[...], w_ref[...], preferred_element_type=F32)

        @pl.when(k == 1)
        def _():
            y_s[...] += jnp.dot(yc_ref[...], w_ref[...], preferred_element_type=F32)

        @pl.when(jnp.logical_and(i == 0, k == 1))
        def _():
            loss_ref[...] = jnp.zeros_like(loss_ref)
            dpw_ref[...] = jnp.zeros_like(dpw_ref)

        @pl.when(k == 1)
        def _():
            _for_tile(i, lambda t: _token_tile_copy(tgt_hbm, t_ref, sem, t).wait())

            @pl.when(i == 0)
            def _():
                t_ref[0:NMETA, :] = jnp.zeros((NMETA, D), F32)

            @pl.when(i == NTILE - 1)
            def _():
                last = _tile_rows(NTILE - 1)[1]
                t_ref[last:TM, :] = jnp.zeros((TM - last, D), F32)

            pw = pw_ref[...]
            row = _row_iota((8, D))

            def chunk(ci, carry):
                r0 = pl.multiple_of(ci * 8, 8)
                yv = y_s[pl.ds(r0, 8), :]
                rs = lax.rsqrt(jnp.mean(yv * yv, axis=-1, keepdims=True) + EPS)
                grow = row + (i * TM + r0)
                valid = jnp.logical_and(grow >= NMETA, grow < T)
                yn = yv * rs
                err = jnp.where(valid, h_ref[pl.ds(r0, 8), :] + yn * pw - t_ref[pl.ds(r0, 8), :], 0.0)
                loss_ref[...] += err * err
                d_rn = err * (1.0 / D)
                dout_ref[pl.ds(r0, 8), :] = d_rn
                dpw_ref[...] += d_rn * yn
                gw = d_rn * pw
                dot = jnp.mean(gw * yv, axis=-1, keepdims=True)
                dy_ref[pl.ds(r0, 8), :] = (rs * gw - yv * (rs * rs * rs * dot)).astype(BF16)
                return carry
            lax.fori_loop(0, TM // 8, chunk, 0, unroll=4)

    row = pl.BlockSpec((TM, D), lambda i, k: (i, 0))
    half = pl.BlockSpec((TM, DL), lambda i, k: (i, 0))
    acc = pl.BlockSpec((8, D), lambda i, k: (0, 0))
    return pl.pallas_call(
        body, name="outproj_loss",
        grid=(TP // TM, 2),
        in_specs=[half, half, pl.BlockSpec((DL, D), lambda i, k: (k, 0)), row, pl.BlockSpec(memory_space=pl.ANY),
                  pl.BlockSpec((1, D), lambda i, k: (0, 0))],
        out_specs=[row, row, acc, acc],
        out_shape=[jax.ShapeDtypeStruct((TP, D), F32), jax.ShapeDtypeStruct((TP, D), BF16),
                   jax.ShapeDtypeStruct((8, D), F32), jax.ShapeDtypeStruct((8, D), F32)],
        scratch_shapes=[pltpu.VMEM((TM, D), F32), pltpu.VMEM((TM, D), F32), pltpu.SemaphoreType.DMA(())],
        compiler_params=_cparams(),
    )(ylru, yconf, w_out, h, target, post_w)


_NT = (((1,), (1,)), ((), ()))
_TN = (((0,), (0,)), ((), ()))


def _outproj_bwd(dy, ylru, yconf, w_out):
    def body(dy_ref, yl_ref, yc_ref, w_ref, dycat_ref, dw_ref):
        j = pl.program_id(0)
        dyv = dy_ref[...]
        dycat_ref[...] = lax.dot_general(dyv, w_ref[...], _NT, preferred_element_type=F32)

        @pl.when(j < NCB)
        def _():
            dw_ref[...] = lax.dot_general(yl_ref[...], dyv, _TN, preferred_element_type=F32).astype(BF16)

        @pl.when(j >= NCB)
        def _():
            dw_ref[...] = lax.dot_general(yc_ref[...], dyv, _TN, preferred_element_type=F32).astype(BF16)

    return pl.pallas_call(
        body, name="outproj_bwd",
        grid=(2 * NCB,),
        in_specs=[pl.BlockSpec((TP, D), lambda j: (0, 0)),
                  pl.BlockSpec((TP, CB), lambda j: (0, jnp.minimum(j, NCB - 1))),
                  pl.BlockSpec((TP, CB), lambda j: (0, jnp.maximum(j - NCB, 0))),
                  pl.BlockSpec((CB, D), lambda j: (j, 0))],
        out_specs=[pl.BlockSpec((TP, CB), lambda j: (0, j)), pl.BlockSpec((CB, D), lambda j: (j, 0))],
        out_shape=[jax.ShapeDtypeStruct((TP, D), F32), jax.ShapeDtypeStruct((D, D), BF16)],
        compiler_params=_cparams(),
    )(dy, ylru, yconf, w_out)


_AFTER = pl.BlockSpec(memory_space=pl.ANY)


def _conf_bwd_proj(dycat, p, z, xhat, rstd, hs, ln_w, ln_b, pw_w, after):
    def body(dy_ref, p_ref, g_ref, xhat_ref, rstd_ref, dyl_ref, hs_ref, gl_ref, lw_ref, lb_ref, w_ref, after_ref,
             dvc_ref, dz_ref, dpw_ref, vecs_ref, dp_s, s_s, ds_s):
        i = pl.program_id(0)
        lw, lb = lw_ref[...], lb_ref[...]

        @pl.when(i == 0)
        def _():
            dpw_ref[...] = jnp.zeros_like(dpw_ref)
            vecs_ref[...] = jnp.zeros_like(vecs_ref)

        def pre_chunk(ci, carry):
            r0 = pl.multiple_of(ci * R, R)
            for half in range(2):
                rr = r0 + 8 * half
                dyv = dy_ref[pl.ds(rr, 8), :]
                g = g_ref[pl.ds(rr, 8), :]
                sg = _sig(g)
                dp = dyv * (g * sg)
                dg = dyv * p_ref[pl.ds(rr, 8), :] * (sg * (1.0 + g * (1.0 - sg)))
                vecs_ref[0:8, :] += dp
                vecs_ref[8:16, :] += dg
                ds_s[pl.ds(rr, 8), :] = dp
                dvc_ref[pl.ds(rr, 8), :] = dg
            dp_s[pl.ds(r0, R), :] = ds_s[pl.ds(r0, R), :].astype(BF16)
            dz_ref[0, pl.ds(r0, R), :] = dvc_ref[pl.ds(r0, R), :].astype(BF16)
            for half in range(2):
                rr = r0 + 8 * half
                gl = gl_ref[pl.ds(rr, 8), :]
                sgl = _sig(gl)
                dgl = dyl_ref[pl.ds(rr, 8), :] * hs_ref[pl.ds(rr, 8), :] * (sgl * (1.0 + gl * (1.0 - sgl)))
                vecs_ref[32:40, :] += dgl
                dvc_ref[pl.ds(rr, 8), :] = dgl
            dz_ref[1, pl.ds(r0, R), :] = dvc_ref[pl.ds(r0, R), :].astype(BF16)
            for half in range(2):
                rr = r0 + 8 * half
                ln = xhat_ref[pl.ds(rr, 8), :] * lw + lb
                ds_s[pl.ds(rr, 8), :] = ln * _sig(ln)
            s_s[pl.ds(r0, R), :] = ds_s[pl.ds(r0, R), :].astype(BF16)
            return carry
        lax.fori_loop(0, TM // R, pre_chunk, 0, unroll=2)

        dpb = dp_s[...]
        ds_s[...] = lax.dot_general(dpb, w_ref[...], _NT, preferred_element_type=F32)
        dpw_ref[...] += lax.dot_general(s_s[...], dpb, _TN, preferred_element_type=F32)

        def post_chunk(ci, carry):
            r0 = pl.multiple_of(ci * 8, 8)
            xhat = xhat_ref[pl.ds(r0, 8), :]
            rstd = jnp.tile(rstd_ref[pl.ds(r0, 8), :], (1, DC // 128))
            ln = xhat * lw + lb
            sl = _sig(ln)
            dln = ds_s[pl.ds(r0, 8), :] * (sl * (1.0 + ln * (1.0 - sl)))
            vecs_ref[16:24, :] += dln * xhat
            vecs_ref[24:32, :] += dln
            dxh = dln * lw
            m1 = jnp.mean(dxh, axis=-1, keepdims=True)
            m2 = jnp.mean(dxh * xhat, axis=-1, keepdims=True)
            dvc_ref[pl.ds(r0, 8), :] = rstd * (dxh - m1 - xhat * m2)
            return carry
        lax.fori_loop(0, TM // 8, post_chunk, 0, unroll=4)

    row = pl.BlockSpec((TM, DC), lambda i: (i, 0))
    vec = pl.BlockSpec((1, DC), lambda i: (0, 0))
    return pl.pallas_call(
        body, name="conf_bwd_proj",
        grid=(TP // TM,),
        in_specs=[pl.BlockSpec((TM, DC), lambda i: (i, 1)), row, pl.BlockSpec((TM, DC), lambda i: (i, 4)), row,
                  pl.BlockSpec((TM, 128), lambda i: (i, 0)),
                  pl.BlockSpec((TM, DL), lambda i: (i, 0)), row, pl.BlockSpec((TM, DL), lambda i: (i, 1)),
                  vec, vec, pl.BlockSpec((DC, DC), lambda i: (0, 0)), _AFTER],
        out_specs=[row, pl.BlockSpec((2, TM, DC), lambda i: (0, i, 0)), pl.BlockSpec((DC, DC), lambda i: (0, 0)),
                   pl.BlockSpec((40, DC), lambda i: (0, 0))],
        out_shape=[jax.ShapeDtypeStruct((TP, DC), F32), jax.ShapeDtypeStruct((2, TP, DC), BF16),
                   jax.ShapeDtypeStruct((DC, DC), F32), jax.ShapeDtypeStruct((40, DC), F32)],
        scratch_shapes=[pltpu.VMEM((TM, DC), BF16), pltpu.VMEM((TM, DC), BF16), pltpu.VMEM((TM, DC), F32)],
        compiler_params=_cparams(),
    )(dycat, p, z, xhat, rstd, dycat, hs, z, ln_w, ln_b, pw_w, after)


def _conf_bwd_conv(dvc, z, dw_w, after):
    def body(dvc_ref, u1_ref, u2_ref, w_ref, after_ref, du_ref, dw_ref, vecs_ref, vs, dvs):
        vs[pl.ds(0, KWP), :] = jnp.zeros((KWP, CBC), F32)
        dvs[pl.ds(TP, KWP), :] = jnp.zeros((KWP, CBC), F32)
        dw_ref[...] = jnp.zeros_like(dw_ref)
        vecs_ref[...] = jnp.zeros_like(vecs_ref)

        def fill_chunk(ci, carry):
            r0 = pl.multiple_of(ci * RC, RC)
            vs[pl.ds(KWP + r0, RC), :] = u1_ref[pl.ds(r0, RC), :] * _sig(u2_ref[pl.ds(r0, RC), :])
            dv = dvc_ref[pl.ds(r0, RC), :]
            dvs[pl.ds(r0, RC), :] = dv
            vecs_ref[0:8, :] += _fold_rows(dv)
            return carry
        lax.fori_loop(0, TP // RC, fill_chunk, 0)

        def conv_chunk(ci, carry):
            r0 = pl.multiple_of(ci * RC, RC)
            vbuf = vs[pl.ds(r0, KWP + RC), :]
            dbuf = dvs[pl.ds(r0, KWP + RC), :]
            dcur = dbuf[0:RC, :]
            dv = jnp.zeros((RC, CBC), F32)
            for rr in range(8):
                vroll = vbuf if rr == 0 else pltpu.roll(vbuf, rr, 0)
                droll = dbuf if rr == 0 else pltpu.roll(dbuf, KWP + RC - rr, 0)
                for q in range(4):
                    s = 8 * q + rr
                    if s > KW - 1:
                        continue
                    k = KW - 1 - s
                    dv = dv + droll[8 * q:8 * q + RC, :] * w_ref[k:k + 1, :]
                    dw_ref[8 * k:8 * k + 8, :] += _fold_rows(dcur * vroll[KWP - 8 * q:KWP - 8 * q + RC, :])
            u1 = u1_ref[pl.ds(r0, RC), :]
            sg = _sig(u2_ref[pl.ds(r0, RC), :])
            du1 = dv * sg
            du2 = dv * u1 * (sg * (1.0 - sg))
            du_ref[0, pl.ds(r0, RC), :] = du1.astype(BF16)
            du_ref[1, pl.ds(r0, RC), :] = du2.astype(BF16)
            vecs_ref[8:16, :] += _fold_rows(du1)
            vecs_ref[16:24, :] += _fold_rows(du2)
            return carry
        lax.fori_loop(0, TP // RC, conv_chunk, 0)

    blk = pl.BlockSpec((TP, CBC), lambda j: (0, j))
    return pl.pallas_call(
        body, name="conf_bwd_conv",
        grid=(NCBC,),
        in_specs=[blk, pl.BlockSpec((TP, CBC), lambda j: (0, 2 * NCBC + j)),
                  pl.BlockSpec((TP, CBC), lambda j: (0, 3 * NCBC + j)), pl.BlockSpec((KWP, CBC), lambda j: (0, j)),
                  _AFTER],
        out_specs=[pl.BlockSpec((2, TP, CBC), lambda j: (0, 0, j)), pl.BlockSpec((8 * KWP, CBC), lambda j: (0, j)),
                   pl.BlockSpec((24, CBC), lambda j: (0, j))],
        out_shape=[jax.ShapeDtypeStruct((2, TP, DC), BF16),
                   jax.ShapeDtypeStruct((8 * KWP, DC), F32), jax.ShapeDtypeStruct((24, DC), F32)],
        scratch_shapes=[pltpu.VMEM((TP + KWP, CBC), F32), pltpu.VMEM((TP + KWP, CBC), F32)],
        compiler_params=_cparams(),
    )(dvc, z, z, dw_w, after)


def _lru_bwd(dycat, z, xc, hs, conv_w, wa_g, b_a, wx_g, b_x, lam, after):
    NV = 6

    def body(dy_ref, x_ref, g_ref, xc_ref, hs_ref, cw_ref, wa_ref, ba_ref, wx_ref, bx_ref, lam_ref, after_ref,
             dzl_ref, dwa_ref, dwx_ref, dcw_ref, vecs_ref, ga_s, gx_s, dxc_s):
        vecs_ref[...] = jnp.zeros_like(vecs_ref)
        dcw_ref[...] = jnp.zeros_like(dcw_ref)
        dxc_s[pl.ds(TP, 8), :] = jnp.zeros((8, CB), F32)

        def gate_chunk(ci, carry):
            r0 = pl.multiple_of(ci * TM, TM)
            xb = xc_ref[pl.ds(r0, TM), :].astype(BF16)
            ga_s[pl.ds(r0, TM), :] = jnp.dot(xb, wa_ref[...], preferred_element_type=F32) + ba_ref[...]
            gx_s[pl.ds(r0, TM), :] = jnp.dot(xb, wx_ref[...], preferred_element_type=F32) + bx_ref[...]
            return carry
        lax.fori_loop(0, TP // TM, gate_chunk, 0)

        sp8 = LRU_C * _softplus(-lam_ref[...])
        row = _row_iota((R, CB))
        nchunk = TP // R

        def scan_chunk(cj, carry):
            a_next, lam_next = carry
            ci = nchunk - 1 - cj
            r0 = pl.multiple_of(ci * R, R)
            dyv = dy_ref[pl.ds(r0, R), :]
            g = g_ref[pl.ds(r0, R), :]
            hv = hs_ref[pl.ds(r0, R), :]
            xc = xc_ref[pl.ds(r0, R), :]
            sg = _sig(g)
            dhs = dyv * (g * sg)
            r, i, a, mult = _gate_values(ga_s[pl.ds(r0, R), :], gx_s[pl.ds(r0, R), :], xc, sp8)
            b = jnp.where(row == R - 1, a_next, pltpu.roll(a, R - 1, 0))
            lv = dhs
            k = 1
            while k < R:
                m = row < R - k
                lv = jnp.where(m, lv + b * pltpu.roll(lv, R - k, 0), lv)
                b = jnp.where(m, b * pltpu.roll(b, R - k, 0), b)
                k *= 2
            lv = lv + b * lam_next
            p0 = pl.multiple_of(jnp.maximum(r0 - 8, 0), 8)
            hprev8 = jnp.where(ci > 0, hs_ref[pl.ds(p0, 8), :], 0.0)
            hprev = pltpu.roll(jnp.concatenate([hprev8, hv], axis=0), 1, 0)[8:8 + R, :]
            da = lv * hprev
            ixc = i * xc
            dmult = lv * ixc
            di = lv * mult * xc
            dxc_s[pl.ds(r0, R), :] = lv * mult * i
            a2 = a * a
            dlog_a = da * a - dmult * a2 / mult
            vecs_ref[32:40, :] += _fold8(dlog_a * r)
            dga = -(dlog_a * sp8) * r * (1.0 - r)
            dgx = di * i * (1.0 - i)
            ga_s[pl.ds(r0, R), :] = dga
            gx_s[pl.ds(r0, R), :] = dgx
            vecs_ref[16:24, :] += _fold8(dga)
            vecs_ref[24:32, :] += _fold8(dgx)
            a_first = jnp.sum(jnp.where(row == 0, a, 0.0), axis=0, keepdims=True)
            l_first = jnp.sum(jnp.where(row == 0, lv, 0.0), axis=0, keepdims=True)
            return a_first, l_first
        def scan_pass(i, cr):
            for sub in range(4):
                cr = scan_chunk(4 * i + sub, cr)
            return cr
        lax.fori_loop(0, nchunk // 4, scan_pass, (jnp.zeros((1, CB), F32), jnp.zeros((1, CB), F32)))

        dwa_ref[...] = jnp.zeros_like(dwa_ref)
        dwx_ref[...] = jnp.zeros_like(dwx_ref)

        def mm_chunk(ci, carry):
            r0 = pl.multiple_of(ci * TM, TM)
            xb = xc_ref[pl.ds(r0, TM), :].astype(BF16)
            dgab = ga_s[pl.ds(r0, TM), :].astype(BF16)
            dgxb = gx_s[pl.ds(r0, TM), :].astype(BF16)
            dxc_s[pl.ds(r0, TM), :] += (lax.dot_general(dgab, wa_ref[...], _NT, preferred_element_type=F32)
                                        + lax.dot_general(dgxb, wx_ref[...], _NT, preferred_element_type=F32))
            dwa_ref[...] += lax.dot_general(xb, dgab, _TN, preferred_element_type=F32)
            dwx_ref[...] += lax.dot_general(xb, dgxb, _TN, preferred_element_type=F32)
            return carry
        lax.fori_loop(0, TP // TM, mm_chunk, 0)

        taps = [cw_ref[k:k + 1, :] for k in range(LW)]

        def conv_chunk(ci, carry):
            r0 = pl.multiple_of(ci * R, R)
            dbuf = dxc_s[pl.ds(r0, R + 8), :]
            dcur = dbuf[0:R, :]
            p0 = pl.multiple_of(jnp.maximum(r0 - 8, 0), 8)
            xprev = jnp.where(ci > 0, x_ref[pl.ds(p0, 8), :], 0.0)
            xbuf = jnp.concatenate([xprev, x_ref[pl.ds(r0, R), :]], axis=0)
            dxl = dcur * taps[LW - 1]
            dcw_ref[8 * (LW - 1):8 * LW, :] += _fold8(dcur * xbuf[8:8 + R, :])
            for s in range(1, LW):
                k = LW - 1 - s
                dxl = dxl + pltpu.roll(dbuf, R + 8 - s, 0)[0:R, :] * taps[k]
                dcw_ref[8 * k:8 * k + 8, :] += _fold8(dcur * pltpu.roll(xbuf, s, 0)[8:8 + R, :])
            dzl_ref[0, pl.ds(r0, R), :] = dxl.astype(BF16)
            vecs_ref[8:16, :] += _fold8(dxl)
            vecs_ref[40:48, :] += _fold8(dcur)
            return carry
        lax.fori_loop(0, TP // R, conv_chunk, 0)
        vecs_ref[32:40, :] = vecs_ref[32:40, :] * (LRU_C * _sig(-lam_ref[...]))

    col = lambda off: pl.BlockSpec((TP, CB), lambda j: (0, off + j))
    vec = pl.BlockSpec((1, CB), lambda j: (0, j))
    wsp = pl.BlockSpec((None, CB, CB), lambda j: (j, 0, 0))
    return pl.pallas_call(
        body, name="lru_bwd",
        grid=(NCB,),
        in_specs=[col(0), col(0), col(NCB), col(0), col(0), pl.BlockSpec((LW, CB), lambda j: (0, j)),
                  wsp, vec, wsp, vec, vec, _AFTER],
        out_specs=[pl.BlockSpec((1, TP, CB), lambda j: (0, 0, j)), wsp, wsp,
                   pl.BlockSpec((8 * LW, CB), lambda j: (0, j)), pl.BlockSpec((8 * NV, CB), lambda j: (0, j))],
        out_shape=[jax.ShapeDtypeStruct((1, TP, DL), BF16),
                   jax.ShapeDtypeStruct((NCB, CB, CB), F32), jax.ShapeDtypeStruct((NCB, CB, CB), F32),
                   jax.ShapeDtypeStruct((8 * LW, DL), F32), jax.ShapeDtypeStruct((8 * NV, DL), F32)],
        scratch_shapes=[pltpu.VMEM((TP, CB), F32), pltpu.VMEM((TP, CB), F32), pltpu.VMEM((TP + 8, CB), F32)],
        compiler_params=_cparams(),
    )(dycat, z, z, xc, hs, conv_w, wa_g, b_a, wx_g, b_x, lam, after)


def _dz_section(sec, dzl_ref, dz41_ref, dzc_ref, use):
    @pl.when(sec == 0)
    def _():
        use(dzl_ref)

    @pl.when(jnp.logical_or(sec == 1, sec == 4))
    def _():
        use(dz41_ref)

    @pl.when(jnp.logical_or(sec == 2, sec == 3))
    def _():
        use(dzc_ref)


def _dz_specs(rows, index):
    return [pl.BlockSpec((None, rows, 1024), lambda a, b: (0, index(a, b)[0], 0)),
            pl.BlockSpec((None, rows, 1024), lambda a, b: (jnp.where(index(a, b)[1] == 1, 1, 0), index(a, b)[0], 0)),
            pl.BlockSpec((None, rows, 1024), lambda a, b: (jnp.clip(index(a, b)[1] - 2, 0, 1), index(a, b)[0], 0))]


def _inproj_wgrad(name, hn, dzs, after):
    KB = 512
    nsec = dzs.shape[0]

    def body(hn_ref, dz_ref, after_ref, dw_ref):
        dw_ref[...] = lax.dot_general(hn_ref[...], dz_ref[...], _TN, preferred_element_type=F32).astype(BF16)

    return pl.pallas_call(
        body, name=name,
        grid=(nsec, D // KB),
        in_specs=[pl.BlockSpec((TP, KB), lambda n, kb: (0, kb)),
                  pl.BlockSpec((None, TP, 1024), lambda n, kb: (n, 0, 0)), _AFTER],
        out_specs=pl.BlockSpec((KB, 1024), lambda n, kb: (kb, n)),
        out_shape=jax.ShapeDtypeStruct((D, nsec * 1024), BF16),
        compiler_params=_cparams(),
    )(hn, dzs, after)


def _sum_win_parts(parts_a, parts_b, parts_c):
    RB = 64

    def body(a_ref, b_ref, c_ref, o_ref):
        def chunk(ci, carry):
            r0 = pl.multiple_of(ci * R, R)
            for ref, src, base, ncol in ((a_ref, 0, 0, 1024), (c_ref, 1024, 1024, 1024), (b_ref, 0, 2048, 2048),
                                         (c_ref, 0, 4096, 1024)):
                for c0 in range(0, ncol, 512):
                    acc = ref[0, pl.ds(r0, R), src + c0:src + c0 + 512].astype(F32)
                    for sidx in range(1, NDEV):
                        acc = acc + ref[sidx, pl.ds(r0, R), src + c0:src + c0 + 512].astype(F32)
                    o_ref[pl.ds(r0, R), base + c0:base + c0 + 512] = acc.astype(BF16)
            return carry
        lax.fori_loop(0, RB // R, chunk, 0)

    spec = lambda ncol: pl.BlockSpec((NDEV, RB, ncol), lambda i: (0, i, 0))
    return pl.pallas_call(
        body, name="sum_win_parts",
        grid=(D // NDEV // RB,),
        in_specs=[spec(1024), spec(2048), spec(2048)],
        out_specs=pl.BlockSpec((RB, NIN), lambda i: (i, 0)),
        out_shape=jax.ShapeDtypeStruct((D // NDEV, NIN), BF16),
        compiler_params=_cparams(),
    )(parts_a, parts_b, parts_c)


def _inproj_bwd(dzl, dz41, dzc, w_in, h, dout, pre_w, after):
    nsec = NIN // 1024

    def body(dzl_ref, dz41_ref, dzc_ref, w_ref, h_ref, dout_ref, pw_ref, after_ref, gx_hbm, dmeta_ref, dpw_ref,
             acc_s, dh_s, sem):
        i = pl.program_id(0)
        s = pl.program_id(1)

        def gx_copy(t):
            lo, n, off = _tile_rows(t)
            return pltpu.make_async_copy(dh_s.at[pl.ds(off, n)], gx_hbm.at[pl.ds(lo, n)], sem)

        @pl.when(s == 0)
        def _():
            acc_s[...] = jnp.zeros_like(acc_s)

        def use(dz_ref):
            acc_s[...] += lax.dot_general(dz_ref[...], w_ref[...], _NT, preferred_element_type=F32)
        _dz_section(s, dzl_ref, dz41_ref, dzc_ref, use)

        @pl.when(jnp.logical_and(i == 0, s == nsec - 1))
        def _():
            dpw_ref[...] = jnp.zeros_like(dpw_ref)

        @pl.when(s == nsec - 1)
        def _():
            _for_tile(i - 1, lambda t: gx_copy(t).wait())
            pw = pw_ref[...]

            def chunk(ci, carry):
                r0 = pl.multiple_of(ci * 8, 8)
                hv = h_ref[pl.ds(r0, 8), :]
                dhn = acc_s[pl.ds(r0, 8), :]
                rs = lax.rsqrt(jnp.mean(hv * hv, axis=-1, keepdims=True) + EPS)
                dpw_ref[...] += dhn * (hv * rs)
                gw = dhn * pw
                dot = jnp.mean(gw * hv, axis=-1, keepdims=True)
                dh_s[pl.ds(r0, 8), :] = rs * gw - hv * (rs * rs * rs * dot) + dout_ref[pl.ds(r0, 8), :]
                return carry
            lax.fori_loop(0, TM // 8, chunk, 0, unroll=4)
            _for_tile(i, lambda t: gx_copy(t).start())

            @pl.when(i == 0)
            def _():
                dmeta_ref[...] = dh_s[0:NMETA, :]

            @pl.when(i == NTILE - 1)
            def _():
                gx_copy(NTILE - 1).wait()

    row = pl.BlockSpec((TM, D), lambda i, s: (i, 0))
    return pl.pallas_call(
        body, name="inproj_bwd",
        grid=(TP // TM, nsec),
        in_specs=_dz_specs(TM, lambda i, s: (i, s)) + [
            pl.BlockSpec((D, 1024), lambda i, s: (0, s)), row, row, pl.BlockSpec((1, D), lambda i, s: (0, 0)),
            _AFTER],
        out_specs=[pl.BlockSpec(memory_space=pl.ANY), pl.BlockSpec((NMETA, D), lambda i, s: (0, 0)),
                   pl.BlockSpec((8, D), lambda i, s: (0, 0))],
        out_shape=[jax.ShapeDtypeStruct((SEQ, D), F32), jax.ShapeDtypeStruct((NMETA, D), F32),
                   jax.ShapeDtypeStruct((8, D), F32)],
        scratch_shapes=[pltpu.VMEM((TM, D), F32), pltpu.VMEM((TM, D), F32), pltpu.SemaphoreType.DMA(())],
        compiler_params=_cparams(),
    )(dzl, dz41, dzc, w_in, h, dout, pre_w, after)


def _adamw(name, parts, w, m, v, block_rows):
    rows, cols = w.shape
    nparts = parts.shape[0]
    cw = cols if cols <= 640 else 512

    def body(p_ref, w_ref, m_ref, v_ref, g_ref, d_ref, nm_ref, nv_ref):
        def chunk(ci, carry):
            r0 = pl.multiple_of(ci * R, R)
            for c0 in range(0, cols, cw):
                at = (pl.ds(r0, R), slice(c0, c0 + cw))
                g = p_ref[(0,) + at].astype(F32)
                for sidx in range(1, nparts):
                    g = g + p_ref[(sidx,) + at].astype(F32)
                delta, mv, vv = _adam_math(g, w_ref[at], m_ref[at], v_ref[at])
                g_ref[at] = g
                nm_ref[at] = mv
                nv_ref[at] = vv
                d_ref[at] = delta
            return carry
        lax.fori_loop(0, block_rows // R, chunk, 0)

    blk = pl.BlockSpec((block_rows, cols), lambda i: (i, 0))
    shp = jax.ShapeDtypeStruct((rows, cols), F32)
    return pl.pallas_call(
        body, name=name,
        grid=(rows // block_rows,),
        in_specs=[pl.BlockSpec((nparts, block_rows, cols), lambda i: (0, i, 0)), blk, blk, blk],
        out_specs=[blk, blk, blk, blk],
        out_shape=[shp, shp, shp, shp],
        compiler_params=_cparams(),
    )(parts, w, m, v)


def _adam_math(g, w, m, v):
    c1 = 1.0 / (1.0 - ADAM_B1 ** ADAM_STEP)
    c2 = 1.0 / (1.0 - ADAM_B2 ** ADAM_STEP)
    mv = ADAM_B1 * m + (1.0 - ADAM_B1) * g
    vv = ADAM_B2 * v + (1.0 - ADAM_B2) * (g * g)
    upd = (mv * c1) / (jnp.sqrt(vv * c2) + ADAM_EPS) + ADAM_WD * w
    return -ADAM_LR * upd, mv, vv


_VEC = [("pre_norm_w", 2), ("post_norm_w", 2), ("b_in", 5), ("lru_conv_b", 1), ("b_gate_a", 1), ("b_gate_x", 1),
        ("lru_lambda", 1), ("conf_dw_b", 1), ("conf_ln_w", 1), ("conf_ln_b", 1), ("conf_pw_b", 1)]
_VEC_ROWS = 24
_LOSS_ROW = 17
_SM_ROWS = 64


def _pack_grads(dprew_acc, dpostw_acc, cvecs, kvecs, lvecs, dcw_acc, ddw_acc, dh, loss_acc):
    def body(pre_ref, post_ref, c_ref, k_ref, l_ref, dcw_ref, ddw_ref, dh_ref, loss_ref, vec_ref, small_ref, tmp):
        s8 = lambda ref, r: jnp.sum(ref[8 * r:8 * r + 8, :], axis=0, keepdims=True)
        vec_ref[...] = jnp.zeros_like(vec_ref)
        pre, post = s8(pre_ref, 0), s8(post_ref, 0)
        rows = [pre[:, 0:1024], pre[:, 1024:2048], post[:, 0:1024], post[:, 1024:2048],
                s8(l_ref, 1), s8(c_ref, 4), s8(k_ref, 1), s8(k_ref, 2), s8(c_ref, 1),
                s8(l_ref, 5), s8(l_ref, 2), s8(l_ref, 3), s8(l_ref, 4),
                s8(k_ref, 0), s8(c_ref, 2), s8(c_ref, 3), s8(c_ref, 0)]
        for r, val in enumerate(rows):
            vec_ref[r:r + 1, :] = val
        vec_ref[_LOSS_ROW:_LOSS_ROW + 1, :] = jnp.zeros((1, 1024), F32) + (0.5 / D) * jnp.sum(loss_ref[...])

        small_ref[...] = jnp.zeros_like(small_ref)
        for k in range(LW):
            tmp[k:k + 1, :] = s8(dcw_ref, k)
        for k in range(KW):
            tmp[8 + k:9 + k, :] = s8(ddw_ref, k)
        for d in range(NDEV):
            small_ref[d, 0:LW, 0:128] = tmp[0:LW, 128 * d:128 * d + 128]
            small_ref[d, 8:8 + KW, 0:128] = tmp[8:8 + KW, 128 * d:128 * d + 128]
            small_ref[d, 40:56, :] = dh_ref[:, 256 * d:256 * d + 256]

    full = lambda a: pl.BlockSpec(a.shape, lambda i: (0,) * a.ndim)
    ins = [dprew_acc, dpostw_acc, cvecs, kvecs, lvecs, dcw_acc, ddw_acc]
    return pl.pallas_call(
        body, name="pack_grads",
        grid=(1,),
        in_specs=[full(a) for a in ins] + [full(dh), full(loss_acc)],
        out_specs=[pl.BlockSpec((_VEC_ROWS, 1024), lambda i: (0, 0)),
                   pl.BlockSpec((NDEV, _SM_ROWS, 256), lambda i: (0, 0, 0))],
        out_shape=[jax.ShapeDtypeStruct((_VEC_ROWS, 1024), F32), jax.ShapeDtypeStruct((NDEV, _SM_ROWS, 256), F32)],
        scratch_shapes=[pltpu.VMEM((40, 1024), F32)],
        compiler_params=_cparams(),
    )(*ins, dh, loss_acc)


def _adamw_vec(parts, W, M, V):
    nv = len(_VEC)

    def body(*refs):
        p_ref = refs[0]
        w_refs, m_refs, v_refs = refs[1:1 + nv], refs[1 + nv:1 + 2 * nv], refs[1 + 2 * nv:1 + 3 * nv]
        outs = refs[1 + 3 * nv:]

        def total(r):
            acc = p_ref[0, r:r + 1, :]
            for sidx in range(1, NDEV):
                acc = acc + p_ref[sidx, r:r + 1, :]
            return acc

        row = 0
        for idx, (_, nrows) in enumerate(_VEC):
            for part in range(nrows):
                cols = slice(1024 * part, 1024 * part + 1024)
                g = total(row + part)
                delta, mv, vv = _adam_math(g, w_refs[idx][:, cols], m_refs[idx][:, cols], v_refs[idx][:, cols])
                for o, val in zip(outs[4 * idx:4 * idx + 4], (g, delta, mv, vv)):
                    o[:, cols] = val
            row += nrows
        outs[-1][...] = total(_LOSS_ROW)[:, 0:128]

    names = [n for n, _ in _VEC]
    flat = lambda d: [d[n].reshape(1, -1) for n in names]
    ws, ms, vs = flat(W), flat(M), flat(V)
    res = pl.pallas_call(
        body, name="adamw_vec",
        out_shape=[jax.ShapeDtypeStruct(w.shape, F32) for w in ws for _ in range(4)]
        + [jax.ShapeDtypeStruct((1, 128), F32)],
        compiler_params=_cparams(),
    )(parts, *ws, *ms, *vs)
    return {n: tuple(res[4 * i:4 * i + 4]) for i, n in enumerate(names)}, res[-1]


def _adamw_small(parts, W, M, V):
    where = {"lru_conv_w": (slice(0, LW), slice(0, 128)), "conf_dw_w": (slice(8, 8 + KW), slice(0, 128)),
             "meta_tokens": (slice(40, 56), slice(0, 256))}
    names = list(where)

    def body(*refs):
        p_ref = refs[0]
        outs = refs[10:]
        for idx, n in enumerate(names):
            rs, cs = where[n]
            g = p_ref[0, rs, cs]
            for sidx in range(1, NDEV):
                g = g + p_ref[sidx, rs, cs]
            delta, mv, vv = _adam_math(g, refs[1 + idx][...], refs[4 + idx][...], refs[7 + idx][...])
            for o, val in zip(outs[4 * idx:4 * idx + 4], (g, delta, mv, vv)):
                o[...] = val

    two_d = lambda a: a.reshape(a.shape[-2:])
    ws, ms, vs = ([two_d(d[n]) for n in names] for d in (W, M, V))
    res = pl.pallas_call(
        body, name="adamw_small",
        out_shape=[jax.ShapeDtypeStruct(w.shape, F32) for w in ws for _ in range(4)],
        compiler_params=_cparams(),
    )(parts, *ws, *ms, *vs)
    return {n: tuple(res[4 * i:4 * i + 4]) for i, n in enumerate(names)}


def _pack_small(lru_cw, dw_w, meta):
    buf = jnp.zeros((_SM_ROWS, 256), F32)
    buf = buf.at[0:LW, 0:128].set(lru_cw)
    buf = buf.at[8:8 + dw_w.shape[0], 0:128].set(dw_w)
    return buf.at[40:56, :].set(meta)


def _block_diag4(w):
    w4 = w.reshape(NCB, 4, 64, 64)
    eye = jnp.eye(4, dtype=w.dtype)
    return jnp.einsum("ghij,hk->ghikj", w4, eye).reshape(NCB, CB, CB)


def _diag_blocks(g):
    g5 = g.reshape(NCB, 4, 64, 4, 64)
    return jnp.stack([g5[:, hh, :, hh, :] for hh in range(4)], axis=1).reshape(16, 64, 64)


def _gate_mats(W):
    return _block_diag4(W["w_gate_a"][0]).astype(BF16), _block_diag4(W["w_gate_x"][0]).astype(BF16)


def _local_step(x, target, meta_full, inproj, out_weights, lru_cw_full, dw_w_full, W, gate_mats, send):
    wa_g, wx_g = gate_mats

    h, hn = _prenorm(x, meta_full, W["pre_norm_w"])
    z, win_full = inproj(hn)
    ylru, xc, hs = _lru_fwd(z, lru_cw_full, W["lru_conv_b"], wa_g, W["b_gate_a"], wx_g, W["b_gate_x"],
                            W["lru_lambda"])
    vc = _conf_fwd_conv(z, dw_w_full, W["conf_dw_b"])
    wout_full, pw_full = out_weights(vc)
    yconf, p, xhat, rstd = _conf_fwd_proj(vc, z, W["conf_ln_w"], W["conf_ln_b"], pw_full, W["conf_pw_b"])
    dout, dy, loss_acc, dpostw_acc = _outproj_loss(ylru, yconf, wout_full, h, target, W["post_norm_w"])

    dycat, dwout_part = _outproj_bwd(dy, ylru, yconf, wout_full)
    tok = send("w_out", ("w_out", dwout_part))
    dvc, dz41, dpw_part, cvecs = _conf_bwd_proj(dycat, p, z, xhat, rstd, hs, W["conf_ln_w"], W["conf_ln_b"], pw_full, tok)
    tok = send("w_in_c", ("conf_pw_w", dpw_part), ("w_in_c", _inproj_wgrad("inproj_wgrad_c", hn, dz41, dz41)))
    dzc, ddw_acc, kvecs = _conf_bwd_conv(dvc, z, dw_w_full, tok)
    tok = send("w_in_b", ("w_in_b", _inproj_wgrad("inproj_wgrad_b", hn, dzc, dzc)))
    dzl, dwa_g, dwx_g, dcw_acc, lvecs = _lru_bwd(dycat, z, xc, hs, lru_cw_full, wa_g, W["b_gate_a"], wx_g,
                                                 W["b_gate_x"], W["lru_lambda"], tok)
    tok = send("w_gates", ("w_gate_a", _diag_blocks(dwa_g).reshape(16 * 64, 64)),
               ("w_gate_x", _diag_blocks(dwx_g).reshape(16 * 64, 64)))
    tok = send("w_in_a", ("w_in_a", _inproj_wgrad("inproj_wgrad_a", hn, dzl, tok)))
    grad_x, dmeta, dprew_acc = _inproj_bwd(dzl, dz41, dzc, win_full, h, dout, W["pre_norm_w"], tok)

    vec_pack, small_part = _pack_grads(dprew_acc, dpostw_acc, cvecs, kvecs, lvecs, dcw_acc, ddw_acc, dmeta, loss_acc)
    return grad_x, vec_pack, small_part


def kernel(x, meta_tokens, pre_norm_w, post_norm_w, w_in, b_in, lru_conv_w, lru_conv_b, w_gate_a, b_gate_a, w_gate_x, b_gate_x, lru_lambda, conf_dw_w, conf_dw_b, conf_ln_w, conf_ln_b, conf_pw_w, conf_pw_b, w_out, loss_target, m_meta_tokens, m_pre_norm_w, m_post_norm_w, m_w_in, m_b_in, m_lru_conv_w, m_lru_conv_b, m_w_gate_a, m_b_gate_a, m_w_gate_x, m_b_gate_x, m_lru_lambda, m_conf_dw_w, m_conf_dw_b, m_conf_ln_w, m_conf_ln_b, m_conf_pw_w, m_conf_pw_b, m_w_out, v_meta_tokens, v_pre_norm_w, v_post_norm_w, v_w_in, v_b_in, v_lru_conv_w, v_lru_conv_b, v_w_gate_a, v_b_gate_a, v_w_gate_x, v_b_gate_x, v_lru_lambda, v_conf_dw_w, v_conf_dw_b, v_conf_ln_w, v_conf_ln_b, v_conf_pw_w, v_conf_pw_b, v_w_out):
    W = dict(meta_tokens=meta_tokens, pre_norm_w=pre_norm_w, post_norm_w=post_norm_w, w_in=w_in, b_in=b_in,
             lru_conv_w=lru_conv_w, lru_conv_b=lru_conv_b, w_gate_a=w_gate_a, b_gate_a=b_gate_a,
             w_gate_x=w_gate_x, b_gate_x=b_gate_x, lru_lambda=lru_lambda, conf_dw_w=conf_dw_w,
             conf_dw_b=conf_dw_b, conf_ln_w=conf_ln_w, conf_ln_b=conf_ln_b, conf_pw_w=conf_pw_w,
             conf_pw_b=conf_pw_b, w_out=w_out)
    M = dict(meta_tokens=m_meta_tokens, pre_norm_w=m_pre_norm_w, post_norm_w=m_post_norm_w, w_in=m_w_in,
             b_in=m_b_in, lru_conv_w=m_lru_conv_w, lru_conv_b=m_lru_conv_b, w_gate_a=m_w_gate_a,
             b_gate_a=m_b_gate_a, w_gate_x=m_w_gate_x, b_gate_x=m_b_gate_x, lru_lambda=m_lru_lambda,
             conf_dw_w=m_conf_dw_w, conf_dw_b=m_conf_dw_b, conf_ln_w=m_conf_ln_w, conf_ln_b=m_conf_ln_b,
             conf_pw_w=m_conf_pw_w, conf_pw_b=m_conf_pw_b, w_out=m_w_out)
    V = dict(meta_tokens=v_meta_tokens, pre_norm_w=v_pre_norm_w, post_norm_w=v_post_norm_w, w_in=v_w_in,
             b_in=v_b_in, lru_conv_w=v_lru_conv_w, lru_conv_b=v_lru_conv_b, w_gate_a=v_w_gate_a,
             b_gate_a=v_b_gate_a, w_gate_x=v_w_gate_x, b_gate_x=v_b_gate_x, lru_lambda=v_lru_lambda,
             conf_dw_w=v_conf_dw_w, conf_dw_b=v_conf_dw_b, conf_ln_w=v_conf_ln_w, conf_ln_b=v_conf_ln_b,
             conf_pw_w=v_conf_pw_w, conf_pw_b=v_conf_pw_b, w_out=v_w_out)
    names = list(W.keys())
    shapes = {n: W[n].shape for n in names}

    small = _pack_small(lru_conv_w[0], conf_dw_w[0], meta_tokens)
    (small_flight,), tok = _exchange_start("gather_small_start", [
        (small, jax.ShapeDtypeStruct((NDEV, _SM_ROWS, 256), F32), _whole, _slot)])
    win_flight, tok = _win_gather_start(w_in[0].astype(BF16) + tok[0, 0].astype(BF16))
    win_flight, tok = _win_gather_links(win_flight, tok)
    gate_mats = _gate_mats(W)
    gathered, tok = _exchange_start("gather_out_start", [
        (w_out[0].astype(BF16) + tok[0, 0].astype(BF16), jax.ShapeDtypeStruct((D, D), BF16), _whole,
         _rows(D // NDEV)),
        (conf_pw_w[0].astype(BF16), jax.ShapeDtypeStruct((DC, DC), BF16), _whole, _rows(DC // NDEV)),
    ])
    (small_all,) = _exchange_wait("gather_small_wait", [small_flight], tok)
    unshard = lambda a: jnp.transpose(a, (1, 0, 2)).reshape(a.shape[1], -1)
    lru_cw_full = unshard(small_all[:, 0:LW, 0:128])
    dw_w_full = unshard(small_all[:, 8:8 + KWP, 0:128])
    meta_full = unshard(small_all[:, 40:56, :])

    def out_weights(after):
        return _exchange_wait("gather_out_wait", gathered, after)

    def inproj(hn):
        xi, yi, ci = lax.axis_index("x"), lax.axis_index("y"), lax.axis_index("c")
        shard = lambda px, py, pc: (4 * px + 2 * py + pc).astype(jnp.int32)
        over_links = jnp.stack([shard(1 - xi, yi, ci), shard(xi, 1 - yi, ci), shard(1 - xi, 1 - yi, ci)])
        z, src = _inproj_cols("inproj_own", jnp.stack([shard(xi, yi, ci)]), hn, win_flight["src"], b_in, None)
        flight = _win_gather_early(dict(win_flight, src=src))
        z, land = _inproj_cols("inproj_here", jnp.stack([shard(xi, yi, 1 - ci)]), hn, flight["land"], b_in, z)
        flight = _win_gather_forward("all", dict(flight, land=land), (1, 2, 3), z)
        z, land = _inproj_cols("inproj_links", over_links, hn, flight["land"], b_in, z)
        flight = _win_gather_forwarded("all", dict(flight, land=land), (1, 2, 3))
        z, land = _inproj_cols("inproj_sibling", over_links + 1 - 2 * ci, hn, flight["land"], b_in, z)
        return z, _win_gather_wait(dict(flight, land=land))

    row_stage = lambda ncol: (jax.ShapeDtypeStruct((NDEV, D // NDEV, ncol), BF16), _rows(D // NDEV))
    piece = {"w_in_a": row_stage(1024), "w_in_b": row_stage(2048), "w_in_c": row_stage(2048),
             "w_out": row_stage(D),
             "conf_pw_w": (jax.ShapeDtypeStruct((NDEV, DC // NDEV, DC), BF16), _rows(DC // NDEV)),
             "w_gate_a": (jax.ShapeDtypeStruct((NDEV, 16 * 64, 64), BF16), _whole),
             "w_gate_x": (jax.ShapeDtypeStruct((NDEV, 16 * 64, 64), BF16), _whole)}
    sent = {}

    def send(call, *named_parts):
        handles, token = _exchange_start(
            "scatter_" + call + "_start",
            [(part.astype(BF16), piece[name][0], piece[name][1], _slot) for name, part in named_parts])
        for (name, _), handle in zip(named_parts, handles):
            sent[name] = [handle]
        return token

    grad_x, vec_pack, small_part = _local_step(
        x[0], loss_target[0], meta_full, inproj, out_weights, lru_cw_full, dw_w_full, W, gate_mats, send)
    grad_x = grad_x[None]

    rest, tok = _exchange_start("scatter_rest_start", [
        (small_part, jax.ShapeDtypeStruct((NDEV, _SM_ROWS, 256), F32), _slot, _slot),
        (vec_pack, jax.ShapeDtypeStruct((NDEV, _VEC_ROWS, 1024), F32), _whole, _slot),
    ])
    (parts_c,) = _exchange_wait("scatter_w_in_c_wait", sent["w_in_c"], tok)
    (parts_b,) = _exchange_wait("scatter_w_in_b_wait", sent["w_in_b"], parts_c)
    (parts_a,) = _exchange_wait("scatter_w_in_a_wait", sent["w_in_a"], parts_b)
    win_rows = _sum_win_parts(parts_a, parts_b, parts_c)
    win_stage2, tok = _exchange_start("scatter_w_in_stage2_start", [
        (win_rows, jax.ShapeDtypeStruct((NDEV, D // NDEV, NIN // NDEV), BF16), _cols(NIN // NDEV), _slot)])

    G, DW, NM, NV = {}, {}, {}, {}
    (wout_parts,) = _exchange_wait("scatter_w_out_wait", sent["w_out"], tok)
    G["w_out"], DW["w_out"], NM["w_out"], NV["w_out"] = _adamw("adamw_w_out", wout_parts, w_out[0], m_w_out[0], v_w_out[0], 64)
    (pw_parts,) = _exchange_wait("scatter_conf_pw_w_wait", sent["conf_pw_w"], G["w_out"])
    G["conf_pw_w"], DW["conf_pw_w"], NM["conf_pw_w"], NV["conf_pw_w"] = _adamw(
        "adamw_pw", pw_parts, conf_pw_w[0], m_conf_pw_w[0], v_conf_pw_w[0], 128)
    res = {}
    wa_parts, wx_parts = _exchange_wait("scatter_w_gates_wait", sent["w_gate_a"] + sent["w_gate_x"], G["conf_pw_w"])
    for n, parts in (("w_gate_a", wa_parts), ("w_gate_x", wx_parts)):
        res[n] = _adamw("adamw_" + n, parts, *[d[n].reshape(16 * 64, 64) for d in (W, M, V)], 16 * 64)
    small_parts, vec_parts = _exchange_wait("scatter_rest_wait", rest, res["w_gate_x"][0])
    res.update(_adamw_small(small_parts, W, M, V))
    vec_res, loss_row = _adamw_vec(vec_parts, W, M, V)
    res.update(vec_res)
    (win_sum,) = _exchange_wait("scatter_w_in_stage2_wait", win_stage2, loss_row)
    res["w_in"] = _adamw("adamw_w_in", win_sum.reshape(1, D, NIN // NDEV), w_in[0], m_w_in[0], v_w_in[0], 256)
    for n, vals in res.items():
        for dst, val in zip((G, DW, NM, NV), vals):
            dst[n] = val
    for dst in (G, DW, NM, NV):
        for n in names:
            dst[n] = dst[n].reshape(shapes[n])
    loss = loss_row[0, 0]

    return (loss, grad_x, *[G[n] for n in names], *[DW[n] for n in names],
            *[NM[n] for n in names], *[NV[n] for n in names])
```

```python
import functools

import jax
import jax.numpy as jnp
from jax import lax
from jax.experimental import pallas as pl
from jax.experimental.pallas import tpu as pltpu

F32 = jnp.float32
BF16 = jnp.bfloat16

D = 2048
DL = 1024
DC = 1024
NIN = 5120
NMETA = 16
SEQ = 2048
T = NMETA + SEQ
TP = 2176
TM = 544
CB = 256
NCB = DL // CB
R = 16
RL = 32
KW = 31
KWP = 32
LW = 4
LRU_C = 8.0
EPS = 1e-6
NDEV = 8

ADAM_LR = 0.001
ADAM_B1 = 0.9
ADAM_B2 = 0.999
ADAM_EPS = 1e-08
ADAM_WD = 0.01
ADAM_STEP = 10

VMEM_LIMIT = 56 * 1024 * 1024


def _cparams():
    return pltpu.CompilerParams(vmem_limit_bytes=VMEM_LIMIT)


def _sig(x):
    return 1.0 / (1.0 + jnp.exp(-x))


def _expm1_neg(y):
    poly = y * (1.0 + y * (0.5 + y * (1.0 / 6.0 + y * (1.0 / 24.0 + y * (1.0 / 120.0)))))
    return jnp.where(y > -0.1, poly, jnp.exp(y) - 1.0)


def _softplus(x):
    e = jnp.exp(-jnp.abs(x))
    w = 1.0 + e
    l1p = jnp.where(w == 1.0, e, jnp.log(w) * e / (w - 1.0))
    return jnp.maximum(x, 0.0) + l1p


def _row_iota(shape):
    return lax.broadcasted_iota(jnp.int32, shape, 0)


def _fold8(v):
    return v[0:8, :] + v[8:16, :]


_FLIPS = [(k >> 2 & 1, k >> 1 & 1, k & 1) for k in range(1, NDEV)]
_HBM = pl.BlockSpec(memory_space=pltpu.HBM)
_SEM = pl.BlockSpec(memory_space=pltpu.SEMAPHORE)


def _peers():
    x, y, c = lax.axis_index("x"), lax.axis_index("y"), lax.axis_index("c")
    out = []
    for dx, dy, dc in _FLIPS:
        px = 1 - x if dx else x
        py = 1 - y if dy else y
        pc = 1 - c if dc else c
        out.append(((px, py, pc), 4 * px + 2 * py + pc))
    return 4 * x + 2 * y + c, out


def _exchange_start(name, items):
    n = len(items)

    def body(*refs):
        srcs, lands = refs[:n], refs[n:2 * n]
        outs = refs[2 * n:]
        send_sems, recv_sems, local_sems = outs[:n], outs[n:2 * n], outs[2 * n:3 * n]
        token = outs[-1]
        me, peers = _peers()
        for a in range(n):
            src_at, dst_at = items[a][2], items[a][3]
            pltpu.make_async_copy(src_at(srcs[a], me), dst_at(lands[a], me), local_sems[a]).start()
        for a in range(n):
            src_at, dst_at = items[a][2], items[a][3]
            for k, (pos, peer) in enumerate(peers):
                pltpu.make_async_remote_copy(
                    src_ref=src_at(srcs[a], peer), dst_ref=dst_at(lands[a], me),
                    send_sem=send_sems[a].at[k], recv_sem=recv_sems[a].at[k],
                    device_id=pos, device_id_type=pl.DeviceIdType.MESH).start()
        token[...] = jnp.zeros_like(token)

    srcs = [pltpu.with_memory_space_constraint(it[0], pltpu.HBM) for it in items]
    lands = [pltpu.with_memory_space_constraint(lax.empty(it[1].shape, it[1].dtype), pltpu.HBM) for it in items]
    sem7 = pltpu.SemaphoreType.DMA((NDEV - 1,))
    res = pl.pallas_call(
        body, name=name,
        out_shape=([sem7] * (2 * n) + [pltpu.SemaphoreType.DMA(())] * n
                   + [pltpu.HBM(a.shape, a.dtype) for a in srcs] + [pltpu.HBM(a.shape, a.dtype) for a in lands]
                   + [jax.ShapeDtypeStruct((8, 128), F32)]),
        in_specs=[_HBM] * (2 * n),
        out_specs=[_SEM] * (3 * n) + [_HBM] * (2 * n) + [pl.BlockSpec(memory_space=pltpu.VMEM)],
        input_output_aliases={i: 3 * n + i for i in range(2 * n)},
        compiler_params=pltpu.CompilerParams(has_side_effects=pltpu.SideEffectType.DATAFLOW_SIDE_EFFECTING),
    )(*srcs, *lands)
    handles = [dict(send=res[a], recv=res[n + a], local=res[2 * n + a], src=res[3 * n + a], land=res[4 * n + a],
                    src_at=items[a][2], dst_at=items[a][3]) for a in range(n)]
    return handles, res[-1]


def _wait_bytes(piece, sem):
    pltpu.make_async_copy(piece, piece, sem).wait()


def _exchange_wait(name, handles, after):
    n = len(handles)

    def body(*refs):
        srcs, lands = refs[:n], refs[n:2 * n]
        send_sems, recv_sems, local_sems = refs[2 * n:3 * n], refs[3 * n:4 * n], refs[4 * n:5 * n]
        me, peers = _peers()
        for a in range(n):
            src_at, dst_at = handles[a]["src_at"], handles[a]["dst_at"]
            for k, (pos, peer) in enumerate(peers):
                _wait_bytes(src_at(srcs[a], peer), send_sems[a].at[k])
                _wait_bytes(dst_at(lands[a], peer), recv_sems[a].at[k])
            pltpu.make_async_copy(src_at(srcs[a], me), dst_at(lands[a], me), local_sems[a]).wait()

    srcs = [hd["src"] for hd in handles]
    lands = [hd["land"] for hd in handles]
    res = pl.pallas_call(
        body, name=name,
        out_shape=[pltpu.HBM(a.shape, a.dtype) for a in srcs] + [pltpu.HBM(a.shape, a.dtype) for a in lands],
        in_specs=[_HBM] * (2 * n) + [_SEM] * (3 * n) + [pl.BlockSpec(memory_space=pl.ANY)],
        out_specs=[_HBM] * (2 * n),
        input_output_aliases={i: i for i in range(2 * n)},
        compiler_params=pltpu.CompilerParams(has_side_effects=pltpu.SideEffectType.DATAFLOW_SIDE_EFFECTING),
    )(*srcs, *lands, *[hd["send"] for hd in handles], *[hd["recv"] for hd in handles],
      *[hd["local"] for hd in handles], after)
    return list(res[n:])


_SIDE = pltpu.SideEffectType.DATAFLOW_SIDE_EFFECTING
_WCOLS = NIN // NDEV


def _win_cols(ref, l):
    return ref.at[:, pl.ds(pl.multiple_of(l * _WCOLS, 128), _WCOLS)]


def _win_routes():
    x, y, c = lax.axis_index("x"), lax.axis_index("y"), lax.axis_index("c")
    pos = [(x, y, 1 - c), (1 - x, y, c), (x, 1 - y, c), (1 - x, 1 - y, c)]
    return 4 * x + 2 * y + c, [(p, 4 * p[0] + 2 * p[1] + p[2]) for p in pos]


def _win_gather_start(shard):
    def body(src, land, send_sem, recv_sem, local_sem, src_thru, land_thru, token):
        me, routes = _win_routes()
        pltpu.make_async_copy(src, _win_cols(land, me), local_sem).start()
        pltpu.make_async_remote_copy(src_ref=src, dst_ref=_win_cols(land, me), send_sem=send_sem, recv_sem=recv_sem,
                                     device_id=routes[0][0], device_id_type=pl.DeviceIdType.MESH).start()
        token[...] = jnp.zeros_like(token)

    src = pltpu.with_memory_space_constraint(shard, pltpu.HBM)
    land = pltpu.with_memory_space_constraint(lax.empty((D, NIN), BF16), pltpu.HBM)
    sem = pltpu.SemaphoreType.DMA(())
    res = pl.pallas_call(
        body, name="win_gather_start",
        out_shape=[sem, sem, sem, pltpu.HBM(src.shape, BF16), pltpu.HBM(land.shape, BF16),
                   jax.ShapeDtypeStruct((8, 128), F32)],
        in_specs=[_HBM, _HBM],
        out_specs=[_SEM, _SEM, _SEM, _HBM, _HBM, pl.BlockSpec(memory_space=pltpu.VMEM)],
        input_output_aliases={0: 3, 1: 4},
        compiler_params=pltpu.CompilerParams(has_side_effects=_SIDE),
    )(src, land)
    return dict(send0=res[0], recv0=res[1], local=res[2], src=res[3], land=res[4]), res[5]


def _win_gather_links(hd, after):
    def body(src, land, after_ref, send_sems, recv_sems, src_thru, land_thru, token):
        me, routes = _win_routes()
        for k in (1, 2, 3):
            pltpu.make_async_remote_copy(src_ref=src, dst_ref=_win_cols(land, me), send_sem=send_sems.at[k - 1],
                                         recv_sem=recv_sems.at[k - 1], device_id=routes[k][0],
                                         device_id_type=pl.DeviceIdType.MESH).start()
        token[...] = jnp.zeros_like(token)

    sem3 = pltpu.SemaphoreType.DMA((3,))
    res = pl.pallas_call(
        body, name="win_gather_links",
        out_shape=[sem3, sem3, pltpu.HBM(hd["src"].shape, BF16), pltpu.HBM(hd["land"].shape, BF16),
                   jax.ShapeDtypeStruct((8, 128), F32)],
        in_specs=[_HBM, _HBM, pl.BlockSpec(memory_space=pl.ANY)],
        out_specs=[_SEM, _SEM, _HBM, _HBM, pl.BlockSpec(memory_space=pltpu.VMEM)],
        input_output_aliases={0: 2, 1: 3},
        compiler_params=pltpu.CompilerParams(has_side_effects=_SIDE),
    )(hd["src"], hd["land"], after)
    return dict(hd, send=res[0], recv=res[1], src=res[2], land=res[3]), res[4]


def _win_gather_forward(name, hd, ks, after):
    def body(land, recv_sems, after_ref, land_thru, fsend_sems, frecv_sems):
        me, routes = _win_routes()
        sibling = routes[0][0]
        for n, k in enumerate(ks):
            pos, peer = routes[k]
            piece = _win_cols(land, peer)
            pltpu.make_async_remote_copy(src_ref=piece, dst_ref=piece, send_sem=fsend_sems.at[n],
                                         recv_sem=recv_sems.at[k - 1], device_id=pos,
                                         device_id_type=pl.DeviceIdType.MESH).wait_recv()
            pltpu.make_async_remote_copy(src_ref=piece, dst_ref=piece, send_sem=fsend_sems.at[n],
                                         recv_sem=frecv_sems.at[n], device_id=sibling,
                                         device_id_type=pl.DeviceIdType.MESH).start()

    sems = pltpu.SemaphoreType.DMA((len(ks),))
    res = pl.pallas_call(
        body, name="win_gather_forward_" + name,
        out_shape=[pltpu.HBM(hd["land"].shape, BF16), sems, sems],
        in_specs=[_HBM, _SEM, pl.BlockSpec(memory_space=pl.ANY)],
        out_specs=[_HBM, _SEM, _SEM],
        input_output_aliases={0: 0},
        compiler_params=pltpu.CompilerParams(has_side_effects=_SIDE),
    )(hd["land"], hd["recv"], after)
    return dict(hd, land=res[0], **{"fsend" + name: res[1], "frecv" + name: res[2]})


def _win_gather_forwarded(name, hd, ks):
    def body(land, fsend_sems, frecv_sems, land_thru):
        me, routes = _win_routes()
        sib_c = routes[0][0][2]
        for n, k in enumerate(ks):
            _wait_bytes(_win_cols(land, routes[k][1]), fsend_sems.at[n])
            _wait_bytes(_win_cols(land, 4 * routes[k][0][0] + 2 * routes[k][0][1] + sib_c), frecv_sems.at[n])

    res = pl.pallas_call(
        body, name="win_gather_forwarded_" + name,
        out_shape=[pltpu.HBM(hd["land"].shape, BF16)],
        in_specs=[_HBM, _SEM, _SEM],
        out_specs=[_HBM],
        input_output_aliases={0: 0},
        compiler_params=pltpu.CompilerParams(has_side_effects=_SIDE),
    )(hd["land"], hd["fsend" + name], hd["frecv" + name])
    return dict(hd, land=res[0])


def _win_gather_early(hd):
    def body(src, land, recv_sem, local_sem, src_thru, land_thru):
        me, routes = _win_routes()
        _wait_bytes(_win_cols(land, routes[0][1]), recv_sem)
        pltpu.make_async_copy(src, _win_cols(land, me), local_sem).wait()

    res = pl.pallas_call(
        body, name="win_gather_early",
        out_shape=[pltpu.HBM(hd["src"].shape, BF16), pltpu.HBM(hd["land"].shape, BF16)],
        in_specs=[_HBM, _HBM, _SEM, _SEM],
        out_specs=[_HBM, _HBM],
        input_output_aliases={0: 0, 1: 1},
        compiler_params=pltpu.CompilerParams(has_side_effects=_SIDE),
    )(hd["src"], hd["land"], hd["recv0"], hd["local"])
    return dict(hd, src=res[0], land=res[1])


def _win_gather_wait(hd):
    def body(src, land, send0_sem, send_sems, src_thru, land_thru):
        for k in range(4):
            _wait_bytes(src, send0_sem if k == 0 else send_sems.at[k - 1])

    res = pl.pallas_call(
        body, name="win_gather_wait",
        out_shape=[pltpu.HBM(hd["src"].shape, BF16), pltpu.HBM(hd["land"].shape, BF16)],
        in_specs=[_HBM, _HBM, _SEM, _SEM],
        out_specs=[_HBM, _HBM],
        input_output_aliases={0: 0, 1: 1},
        compiler_params=pltpu.CompilerParams(has_side_effects=_SIDE),
    )(hd["src"], hd["land"], hd["send0"], hd["send"])
    return res[1]


def _whole(ref, l):
    return ref


def _slot(ref, l):
    return ref.at[l]


def _cols(width):
    def at(ref, l):
        return ref.at[:, pl.ds(pl.multiple_of(l * width, 128), width)]
    return at


def _rows(height):
    def at(ref, l):
        return ref.at[pl.ds(pl.multiple_of(l * height, 8), height), :]
    return at


NTILE = TP // TM


def _tile_rows(t):
    lo = max(t * TM - NMETA, 0)
    hi = min((t + 1) * TM - NMETA, SEQ)
    return lo, hi - lo, lo + NMETA - t * TM


def _for_tile(t, fn):
    for static_t in range(NTILE):
        pl.when(t == static_t)(functools.partial(fn, static_t))


def _token_tile_copy(hbm_ref, buf, sem, t):
    lo, n, off = _tile_rows(t)
    return pltpu.make_async_copy(hbm_ref.at[pl.ds(lo, n)], buf.at[pl.ds(off, n)], sem)


def _prenorm(x, meta_full, pre_w):
    def body(x_ref, meta_ref, pw_ref, h_ref, hn_ref, xbuf, sems):
        i = pl.program_id(0)
        slot = i % 2

        def start(t):
            _token_tile_copy(x_ref, xbuf.at[t % 2], sems.at[t % 2], t).start()

        @pl.when(i == 0)
        def _():
            start(0)
        _for_tile(i + 1, start)
        _for_tile(i, lambda t: _token_tile_copy(x_ref, xbuf.at[t % 2], sems.at[t % 2], t).wait())

        @pl.when(i == 0)
        def _():
            xbuf[0, 0:NMETA, :] = meta_ref[...]

        @pl.when(i == NTILE - 1)
        def _():
            last = _tile_rows(NTILE - 1)[1]
            xbuf[(NTILE - 1) % 2, last:TM, :] = jnp.zeros((TM - last, D), F32)

        pw = pw_ref[...]

        def chunk(ci, carry):
            r0 = pl.multiple_of(ci * R, R)
            xv = xbuf[slot, pl.ds(r0, R), :]
            h_ref[pl.ds(r0, R), :] = xv
            ms = jnp.mean(xv * xv, axis=-1, keepdims=True)
            hn_ref[pl.ds(r0, R), :] = (xv * lax.rsqrt(ms + EPS) * pw).astype(BF16)
            return carry
        lax.fori_loop(0, TM // R, chunk, 0, unroll=2)

    row = pl.BlockSpec((TM, D), lambda i: (i, 0))
    return pl.pallas_call(
        body, name="prenorm",
        grid=(NTILE,),
        in_specs=[pl.BlockSpec(memory_space=pl.ANY), pl.BlockSpec((NMETA, D), lambda i: (0, 0)),
                  pl.BlockSpec((1, D), lambda i: (0, 0))],
        out_specs=[row, row],
        out_shape=[jax.ShapeDtypeStruct((TP, D), F32), jax.ShapeDtypeStruct((TP, D), BF16)],
        scratch_shapes=[pltpu.VMEM((2, TM, D), F32), pltpu.SemaphoreType.DMA((2,))],
        compiler_params=_cparams(),
    )(x, meta_full, pre_w)


def _inproj_cols(name, shards, hn, w_land, b_in, z_prev):
    nsh = shards.shape[0]
    one_shard = w_land.shape[1] == _WCOLS

    def body(idx_ref, hn_ref, w_ref, b_ref, *rest):
        z_ref = rest[-2]
        z_ref[...] = jnp.dot(hn_ref[...], w_ref[...], preferred_element_type=F32) + b_ref[...]

    any_spec = pl.BlockSpec(memory_space=pl.ANY)
    in_specs = [pl.BlockSpec((TM, D), lambda j, i, idx: (i, 0)),
                pl.BlockSpec((D, _WCOLS), lambda j, i, idx: (0, 0 if one_shard else idx[j])),
                pl.BlockSpec((1, _WCOLS), lambda j, i, idx: (0, idx[j]))]
    operands = [hn, w_land, b_in]
    aliases = {2: 1}
    if z_prev is not None:
        in_specs.append(any_spec)
        operands.append(z_prev)
        aliases[4] = 0
    return pl.pallas_call(
        body, name=name,
        grid_spec=pltpu.PrefetchScalarGridSpec(
            num_scalar_prefetch=1, grid=(nsh, TP // TM), in_specs=in_specs,
            out_specs=[pl.BlockSpec((TM, _WCOLS), lambda j, i, idx: (i, idx[j])), any_spec]),
        out_shape=[jax.ShapeDtypeStruct((TP, NIN), F32), jax.ShapeDtypeStruct(w_land.shape, w_land.dtype)],
        input_output_aliases=aliases,
        compiler_params=_cparams(),
    )(shards, *operands)


def _gate_values(ga, gx, xc, sp8):
    r = _sig(ga)
    i = _sig(gx)
    log_a = -(r * sp8)
    a = jnp.exp(log_a)
    mult = jnp.sqrt(-_expm1_neg(2.0 * log_a))
    return r, i, a, mult


def _lru_fwd(z, conv_w, conv_b, wa_g, b_a, wx_g, b_x, lam):
    def body(x_ref, g_ref, cw_ref, cb_ref, wa_ref, ba_ref, wx_ref, bx_ref, lam_ref,
             y_ref, xc_ref, hs_ref, ga_s, gx_s):
        taps = [cw_ref[k:k + 1, :] for k in range(LW)]
        cb = cb_ref[...]

        def conv_chunk(ci, carry):
            r0 = pl.multiple_of(ci * RL, RL)
            cur = x_ref[pl.ds(r0, RL), :]
            p0 = pl.multiple_of(jnp.maximum(r0 - 8, 0), 8)
            prev = jnp.where(ci > 0, x_ref[pl.ds(p0, 8), :], 0.0)
            buf = jnp.concatenate([prev, cur], axis=0)
            acc = cur * taps[LW - 1] + cb
            for s in range(1, LW):
                acc = acc + pltpu.roll(buf, s, 0)[8:8 + RL, :] * taps[LW - 1 - s]
            xc_ref[pl.ds(r0, RL), :] = acc
            return carry
        lax.fori_loop(0, TP // RL, conv_chunk, 0)

        def gate_chunk(ci, carry):
            r0 = pl.multiple_of(ci * TM, TM)
            xb = xc_ref[pl.ds(r0, TM), :].astype(BF16)
            ga_s[pl.ds(r0, TM), :] = jnp.dot(xb, wa_ref[...], preferred_element_type=F32) + ba_ref[...]
            gx_s[pl.ds(r0, TM), :] = jnp.dot(xb, wx_ref[...], preferred_element_type=F32) + bx_ref[...]
            return carry
        lax.fori_loop(0, TP // TM, gate_chunk, 0)

        sp8 = LRU_C * _softplus(-lam_ref[...])
        row = _row_iota((R, CB))

        def scan_chunk(ci, hprev):
            r0 = pl.multiple_of(ci * R, R)
            xc = xc_ref[pl.ds(r0, R), :]
            _, i, a, mult = _gate_values(ga_s[pl.ds(r0, R), :], gx_s[pl.ds(r0, R), :], xc, sp8)
            u = mult * (i * xc)
            k = 1
            while k < R:
                m = row >= k
                u = jnp.where(m, a * pltpu.roll(u, k, 0) + u, u)
                a = jnp.where(m, a * pltpu.roll(a, k, 0), a)
                k *= 2
            hv = u + a * hprev
            hs_ref[pl.ds(r0, R), :] = hv
            g = g_ref[pl.ds(r0, R), :]
            y_ref[pl.ds(r0, R), :] = (hv * (g * _sig(g))).astype(BF16)
            return jnp.sum(jnp.where(row == R - 1, hv, 0.0), axis=0, keepdims=True)
        def scan_pass(i, hp):
            for sub in range(4):
                hp = scan_chunk(4 * i + sub, hp)
            return hp
        lax.fori_loop(0, TP // R // 4, scan_pass, jnp.zeros((1, CB), F32))

    col = lambda off: pl.BlockSpec((TP, CB), lambda j: (0, off + j))
    vec = pl.BlockSpec((1, CB), lambda j: (0, j))
    wsp = pl.BlockSpec((None, CB, CB), lambda j: (j, 0, 0))
    return pl.pallas_call(
        body, name="lru_fwd",
        grid=(NCB,),
        in_specs=[col(0), col(NCB), pl.BlockSpec((LW, CB), lambda j: (0, j)), vec, wsp, vec, wsp, vec, vec],
        out_specs=[col(0), col(0), col(0)],
        out_shape=[jax.ShapeDtypeStruct((TP, DL), BF16), jax.ShapeDtypeStruct((TP, DL), F32),
                   jax.ShapeDtypeStruct((TP, DL), F32)],
        scratch_shapes=[pltpu.VMEM((TP, CB), F32), pltpu.VMEM((TP, CB), F32)],
        compiler_params=_cparams(),
    )(z, z, conv_w, conv_b, wa_g, b_a, wx_g, b_x, lam)


CBC = 128
NCBC = DC // CBC
RC = 128


def _fold_rows(v):
    acc = v[0:8, :]
    for r in range(8, v.shape[0], 8):
        acc = acc + v[r:r + 8, :]
    return acc


def _conf_fwd_conv(z, dw_w, dw_b):
    def body(u1_ref, u2_ref, w_ref, b_ref, vc_ref, vs):
        vs[pl.ds(0, KWP), :] = jnp.zeros((KWP, CBC), F32)

        def glu_chunk(ci, carry):
            r0 = pl.multiple_of(ci * RC, RC)
            vs[pl.ds(KWP + r0, RC), :] = u1_ref[pl.ds(r0, RC), :] * _sig(u2_ref[pl.ds(r0, RC), :])
            return carry
        lax.fori_loop(0, TP // RC, glu_chunk, 0)

        bias = b_ref[...]

        def conv_chunk(ci, carry):
            r0 = pl.multiple_of(ci * RC, RC)
            buf = vs[pl.ds(r0, KWP + RC), :]
            acc = jnp.zeros((RC, CBC), F32) + bias
            for rr in range(8):
                rolled = buf if rr == 0 else pltpu.roll(buf, rr, 0)
                for q in range(4):
                    s = 8 * q + rr
                    if s > KW - 1:
                        continue
                    k = KW - 1 - s
                    acc = acc + rolled[KWP - 8 * q:KWP - 8 * q + RC, :] * w_ref[k:k + 1, :]
            vc_ref[pl.ds(r0, RC), :] = acc
            return carry
        lax.fori_loop(0, TP // RC, conv_chunk, 0)

    return pl.pallas_call(
        body, name="conf_fwd_conv",
        grid=(NCBC,),
        in_specs=[pl.BlockSpec((TP, CBC), lambda j: (0, 2 * NCBC + j)),
                  pl.BlockSpec((TP, CBC), lambda j: (0, 3 * NCBC + j)),
                  pl.BlockSpec((KWP, CBC), lambda j: (0, j)),
                  pl.BlockSpec((1, CBC), lambda j: (0, j))],
        out_specs=pl.BlockSpec((TP, CBC), lambda j: (0, j)),
        out_shape=jax.ShapeDtypeStruct((TP, DC), F32),
        scratch_shapes=[pltpu.VMEM((TP + KWP, CBC), F32)],
        compiler_params=_cparams(),
    )(z, z, dw_w, dw_b)


def _ln_chunk(vc, lw, lb):
    mu = jnp.mean(vc, axis=-1, keepdims=True)
    xm = vc - mu
    var = jnp.mean(xm * xm, axis=-1, keepdims=True)
    rstd = lax.rsqrt(var + EPS)
    xhat = xm * rstd
    return xhat, rstd, xhat * lw + lb


def _conf_fwd_proj(vc, z, ln_w, ln_b, pw_w, pw_b):
    def body(vc_ref, g_ref, lw_ref, lb_ref, w_ref, b_ref, y_ref, p_ref, xhat_ref, rstd_ref, s_s):
        lw, lb = lw_ref[...], lb_ref[...]

        def ln_chunk(ci, carry):
            r0 = pl.multiple_of(ci * R, R)
            for half in range(2):
                rr = r0 + 8 * half
                xhat, rstd, ln = _ln_chunk(vc_ref[pl.ds(rr, 8), :], lw, lb)
                xhat_ref[pl.ds(rr, 8), :] = xhat
                rstd_ref[pl.ds(rr, 8), :] = jnp.broadcast_to(rstd, (8, 128))
                p_ref[pl.ds(rr, 8), :] = ln * _sig(ln)
            s_s[pl.ds(r0, R), :] = p_ref[pl.ds(r0, R), :].astype(BF16)
            return carry
        lax.fori_loop(0, TM // R, ln_chunk, 0, unroll=2)

        p_ref[...] = jnp.dot(s_s[...], w_ref[...], preferred_element_type=F32) + b_ref[...]

        def out_chunk(ci, carry):
            r0 = pl.multiple_of(ci * R, R)
            g = g_ref[pl.ds(r0, R), :]
            y_ref[pl.ds(r0, R), :] = (p_ref[pl.ds(r0, R), :] * (g * _sig(g))).astype(BF16)
            return carry
        lax.fori_loop(0, TM // R, out_chunk, 0)

    row = pl.BlockSpec((TM, DC), lambda i: (i, 0))
    vec = pl.BlockSpec((1, DC), lambda i: (0, 0))
    return pl.pallas_call(
        body, name="conf_fwd_proj",
        grid=(TP // TM,),
        in_specs=[row, pl.BlockSpec((TM, DC), lambda i: (i, 4)), vec, vec,
                  pl.BlockSpec((DC, DC), lambda i: (0, 0)), vec],
        out_specs=[row, row, row, pl.BlockSpec((TM, 128), lambda i: (i, 0))],
        out_shape=[jax.ShapeDtypeStruct((TP, DC), BF16), jax.ShapeDtypeStruct((TP, DC), F32),
                   jax.ShapeDtypeStruct((TP, DC), F32), jax.ShapeDtypeStruct((TP, 128), F32)],
        scratch_shapes=[pltpu.VMEM((TM, DC), BF16)],
        compiler_params=_cparams(),
    )(vc, z, ln_w, ln_b, pw_w, pw_b)


def _outproj_loss(ylru, yconf, w_out, h, target, post_w):
    def body(yl_ref, yc_ref, w_ref, h_ref, tgt_hbm, pw_ref, dout_ref, dy_ref, loss_ref, dpw_ref, y_s, t_ref, sem):
        i = pl.program_id(0)
        k = pl.program_id(1)

        @pl.when(k == 0)
        def _():
            _for_tile(i, lambda t: _token_tile_copy(tgt_hbm, t_ref, sem, t).start())
            y_s[...] = jnp.dot(yl_ref[...], w_ref[...], preferred_element_type=F32)

        @pl.when(k == 1)
        def _():
            y_s[...] += jnp.dot(yc_ref[...], w_ref[...], preferred_element_type=F32)

        @pl.when(jnp.logical_and(i == 0, k == 1))
        def _():
            loss_ref[...] = jnp.zeros_like(loss_ref)
            dpw_ref[...] = jnp.zeros_like(dpw_ref)

        @pl.when(k == 1)
        def _():
            _for_tile(i, lambda t: _token_tile_copy(tgt_hbm, t_ref, sem, t).wait())

            @pl.when(i == 0)
            def _():
                t_ref[0:NMETA, :] = jnp.zeros((NMETA, D), F32)

            @pl.when(i == NTILE - 1)
            def _():
                last = _tile_rows(NTILE - 1)[1]
                t_ref[last:TM, :] = jnp.zeros((TM - last, D), F32)

            pw = pw_ref[...]
            row = _row_iota((8, D))

            def chunk(ci, carry):
                r0 = pl.multiple_of(ci * 8, 8)
                yv = y_s[pl.ds(r0, 8), :]
                rs = lax.rsqrt(jnp.mean(yv * yv, axis=-1, keepdims=True) + EPS)
                grow = row + (i * TM + r0)
                valid = jnp.logical_and(grow >= NMETA, grow < T)
                yn = yv * rs
                err = jnp.where(valid, h_ref[pl.ds(r0, 8), :] + yn * pw - t_ref[pl.ds(r0, 8), :], 0.0)
                loss_ref[...] += err * err
                d_rn = err * (1.0 / D)
                dout_ref[pl.ds(r0, 8), :] = d_rn
                dpw_ref[...] += d_rn * yn
                gw = d_rn * pw
                dot = jnp.mean(gw * yv, axis=-1, keepdims=True)
                dy_ref[pl.ds(r0, 8), :] = (rs * gw - yv * (rs * rs * rs * dot)).astype(BF16)
                return carry
            lax.fori_loop(0, TM // 8, chunk, 0, unroll=4)

    row = pl.BlockSpec((TM, D), lambda i, k: (i, 0))
    half = pl.BlockSpec((TM, DL), lambda i, k: (i, 0))
    acc = pl.BlockSpec((8, D), lambda i, k: (0, 0))
    return pl.pallas_call(
        body, name="outproj_loss",
        grid=(TP // TM, 2),
        in_specs=[half, half, pl.BlockSpec((DL, D), lambda i, k: (k, 0)), row, pl.BlockSpec(memory_space=pl.ANY),
                  pl.BlockSpec((1, D), lambda i, k: (0, 0))],
        out_specs=[row, row, acc, acc],
        out_shape=[jax.ShapeDtypeStruct((TP, D), F32), jax.ShapeDtypeStruct((TP, D), BF16),
                   jax.ShapeDtypeStruct((8, D), F32), jax.ShapeDtypeStruct((8, D), F32)],
        scratch_shapes=[pltpu.VMEM((TM, D), F32), pltpu.VMEM((TM, D), F32), pltpu.SemaphoreType.DMA(())],
        compiler_params=_cparams(),
    )(ylru, yconf, w_out, h, target, post_w)


_NT = (((1,), (1,)), ((), ()))
_TN = (((0,), (0,)), ((), ()))


def _outproj_bwd(dy, ylru, yconf, w_out):
    def body(dy_ref, yl_ref, yc_ref, w_ref, dycat_ref, dw_ref):
        j = pl.program_id(0)
        dyv = dy_ref[...]
        dycat_ref[...] = lax.dot_general(dyv, w_ref[...], _NT, preferred_element_type=F32)

        @pl.when(j < NCB)
        def _():
            dw_ref[...] = lax.dot_general(yl_ref[...], dyv, _TN, preferred_element_type=F32).astype(BF16)

        @pl.when(j >= NCB)
        def _():
            dw_ref[...] = lax.dot_general(yc_ref[...], dyv, _TN, preferred_element_type=F32).astype(BF16)

    return pl.pallas_call(
        body, name="outproj_bwd",
        grid=(2 * NCB,),
        in_specs=[pl.BlockSpec((TP, D), lambda j: (0, 0)),
                  pl.BlockSpec((TP, CB), lambda j: (0, jnp.minimum(j, NCB - 1))),
                  pl.BlockSpec((TP, CB), lambda j: (0, jnp.maximum(j - NCB, 0))),
                  pl.BlockSpec((CB, D), lambda j: (j, 0))],
        out_specs=[pl.BlockSpec((TP, CB), lambda j: (0, j)), pl.BlockSpec((CB, D), lambda j: (j, 0))],
        out_shape=[jax.ShapeDtypeStruct((TP, D), F32), jax.ShapeDtypeStruct((D, D), BF16)],
        compiler_params=_cparams(),
    )(dy, ylru, yconf, w_out)


_AFTER = pl.BlockSpec(memory_space=pl.ANY)


def _conf_bwd_proj(dycat, p, z, xhat, rstd, hs, ln_w, ln_b, pw_w, after):
    def body(dy_ref, p_ref, g_ref, xhat_ref, rstd_ref, dyl_ref, hs_ref, gl_ref, lw_ref, lb_ref, w_ref, after_ref,
             dvc_ref, dz_ref, dpw_ref, vecs_ref, dp_s, s_s, ds_s):
        i = pl.program_id(0)
        lw, lb = lw_ref[...], lb_ref[...]

        @pl.when(i == 0)
        def _():
            dpw_ref[...] = jnp.zeros_like(dpw_ref)
            vecs_ref[...] = jnp.zeros_like(vecs_ref)

        def pre_chunk(ci, carry):
            r0 = pl.multiple_of(ci * R, R)
            for half in range(2):
                rr = r0 + 8 * half
                dyv = dy_ref[pl.ds(rr, 8), :]
                g = g_ref[pl.ds(rr, 8), :]
                sg = _sig(g)
                dp = dyv * (g * sg)
                dg = dyv * p_ref[pl.ds(rr, 8), :] * (sg * (1.0 + g * (1.0 - sg)))
                vecs_ref[0:8, :] += dp
                vecs_ref[8:16, :] += dg
                ds_s[pl.ds(rr, 8), :] = dp
                dvc_ref[pl.ds(rr, 8), :] = dg
            dp_s[pl.ds(r0, R), :] = ds_s[pl.ds(r0, R), :].astype(BF16)
            dz_ref[0, pl.ds(r0, R), :] = dvc_ref[pl.ds(r0, R), :].astype(BF16)
            for half in range(2):
                rr = r0 + 8 * half
                gl = gl_ref[pl.ds(rr, 8), :]
                sgl = _sig(gl)
                dgl = dyl_ref[pl.ds(rr, 8), :] * hs_ref[pl.ds(rr, 8), :] * (sgl * (1.0 + gl * (1.0 - sgl)))
                vecs_ref[32:40, :] += dgl
                dvc_ref[pl.ds(rr, 8), :] = dgl
            dz_ref[1, pl.ds(r0, R), :] = dvc_ref[pl.ds(r0, R), :].astype(BF16)
            for half in range(2):
                rr = r0 + 8 * half
                ln = xhat_ref[pl.ds(rr, 8), :] * lw + lb
                ds_s[pl.ds(rr, 8), :] = ln * _sig(ln)
            s_s[pl.ds(r0, R), :] = ds_s[pl.ds(r0, R), :].astype(BF16)
            return carry
        lax.fori_loop(0, TM // R, pre_chunk, 0, unroll=2)

        dpb = dp_s[...]
        ds_s[...] = lax.dot_general(dpb, w_ref[...], _NT, preferred_element_type=F32)
        dpw_ref[...] += lax.dot_general(s_s[...], dpb, _TN, preferred_element_type=F32)

        def post_chunk(ci, carry):
            r0 = pl.multiple_of(ci * 8, 8)
            xhat = xhat_ref[pl.ds(r0, 8), :]
            rstd = jnp.tile(rstd_ref[pl.ds(r0, 8), :], (1, DC // 128))
            ln = xhat * lw + lb
            sl = _sig(ln)
            dln = ds_s[pl.ds(r0, 8), :] * (sl * (1.0 + ln * (1.0 - sl)))
            vecs_ref[16:24, :] += dln * xhat
            vecs_ref[24:32, :] += dln
            dxh = dln * lw
            m1 = jnp.mean(dxh, axis=-1, keepdims=True)
            m2 = jnp.mean(dxh * xhat, axis=-1, keepdims=True)
            dvc_ref[pl.ds(r0, 8), :] = rstd * (dxh - m1 - xhat * m2)
            return carry
        lax.fori_loop(0, TM // 8, post_chunk, 0, unroll=4)

    row = pl.BlockSpec((TM, DC), lambda i: (i, 0))
    vec = pl.BlockSpec((1, DC), lambda i: (0, 0))
    return pl.pallas_call(
        body, name="conf_bwd_proj",
        grid=(TP // TM,),
        in_specs=[pl.BlockSpec((TM, DC), lambda i: (i, 1)), row, pl.BlockSpec((TM, DC), lambda i: (i, 4)), row,
                  pl.BlockSpec((TM, 128), lambda i: (i, 0)),
                  pl.BlockSpec((TM, DL), lambda i: (i, 0)), row, pl.BlockSpec((TM, DL), lambda i: (i, 1)),
                  vec, vec, pl.BlockSpec((DC, DC), lambda i: (0, 0)), _AFTER],
        out_specs=[row, pl.BlockSpec((2, TM, DC), lambda i: (0, i, 0)), pl.BlockSpec((DC, DC), lambda i: (0, 0)),
                   pl.BlockSpec((40, DC), lambda i: (0, 0))],
        out_shape=[jax.ShapeDtypeStruct((TP, DC), F32), jax.ShapeDtypeStruct((2, TP, DC), BF16),
                   jax.ShapeDtypeStruct((DC, DC), F32), jax.ShapeDtypeStruct((40, DC), F32)],
        scratch_shapes=[pltpu.VMEM((TM, DC), BF16), pltpu.VMEM((TM, DC), BF16), pltpu.VMEM((TM, DC), F32)],
        compiler_params=_cparams(),
    )(dycat, p, z, xhat, rstd, dycat, hs, z, ln_w, ln_b, pw_w, after)


def _conf_bwd_conv(dvc, z, dw_w, after):
    def body(dvc_ref, u1_ref, u2_ref, w_ref, after_ref, du_ref, dw_ref, vecs_ref, vs, dvs):
        vs[pl.ds(0, KWP), :] = jnp.zeros((KWP, CBC), F32)
        dvs[pl.ds(TP, KWP), :] = jnp.zeros((KWP, CBC), F32)
        dw_ref[...] = jnp.zeros_like(dw_ref)
        vecs_ref[...] = jnp.zeros_like(vecs_ref)

        def fill_chunk(ci, carry):
            r0 = pl.multiple_of(ci * RC, RC)
            vs[pl.ds(KWP + r0, RC), :] = u1_ref[pl.ds(r0, RC), :] * _sig(u2_ref[pl.ds(r0, RC), :])
            dv = dvc_ref[pl.ds(r0, RC), :]
            dvs[pl.ds(r0, RC), :] = dv
            vecs_ref[0:8, :] += _fold_rows(dv)
            return carry
        lax.fori_loop(0, TP // RC, fill_chunk, 0)

        def conv_chunk(ci, carry):
            r0 = pl.multiple_of(ci * RC, RC)
            vbuf = vs[pl.ds(r0, KWP + RC), :]
            dbuf = dvs[pl.ds(r0, KWP + RC), :]
            dcur = dbuf[0:RC, :]
            dv = jnp.zeros((RC, CBC), F32)
            for rr in range(8):
                vroll = vbuf if rr == 0 else pltpu.roll(vbuf, rr, 0)
                droll = dbuf if rr == 0 else pltpu.roll(dbuf, KWP + RC - rr, 0)
                for q in range(4):
                    s = 8 * q + rr
                    if s > KW - 1:
                        continue
                    k = KW - 1 - s
                    dv = dv + droll[8 * q:8 * q + RC, :] * w_ref[k:k + 1, :]
                    dw_ref[8 * k:8 * k + 8, :] += _fold_rows(dcur * vroll[KWP - 8 * q:KWP - 8 * q + RC, :])
            u1 = u1_ref[pl.ds(r0, RC), :]
            sg = _sig(u2_ref[pl.ds(r0, RC), :])
            du1 = dv * sg
            du2 = dv * u1 * (sg * (1.0 - sg))
            du_ref[0, pl.ds(r0, RC), :] = du1.astype(BF16)
            du_ref[1, pl.ds(r0, RC), :] = du2.astype(BF16)
            vecs_ref[8:16, :] += _fold_rows(du1)
            vecs_ref[16:24, :] += _fold_rows(du2)
            return carry
        lax.fori_loop(0, TP // RC, conv_chunk, 0)

    blk = pl.BlockSpec((TP, CBC), lambda j: (0, j))
    return pl.pallas_call(
        body, name="conf_bwd_conv",
        grid=(NCBC,),
        in_specs=[blk, pl.BlockSpec((TP, CBC), lambda j: (0, 2 * NCBC + j)),
                  pl.BlockSpec((TP, CBC), lambda j: (0, 3 * NCBC + j)), pl.BlockSpec((KWP, CBC), lambda j: (0, j)),
                  _AFTER],
        out_specs=[pl.BlockSpec((2, TP, CBC), lambda j: (0, 0, j)), pl.BlockSpec((8 * KWP, CBC), lambda j: (0, j)),
                   pl.BlockSpec((24, CBC), lambda j: (0, j))],
        out_shape=[jax.ShapeDtypeStruct((2, TP, DC), BF16),
                   jax.ShapeDtypeStruct((8 * KWP, DC), F32), jax.ShapeDtypeStruct((24, DC), F32)],
        scratch_shapes=[pltpu.VMEM((TP + KWP, CBC), F32), pltpu.VMEM((TP + KWP, CBC), F32)],
        compiler_params=_cparams(),
    )(dvc, z, z, dw_w, after)


def _lru_bwd(dycat, z, xc, hs, conv_w, wa_g, b_a, wx_g, b_x, lam, after):
    NV = 6

    def body(dy_ref, x_ref, g_ref, xc_ref, hs_ref, cw_ref, wa_ref, ba_ref, wx_ref, bx_ref, lam_ref, after_ref,
             dzl_ref, dwa_ref, dwx_ref, dcw_ref, vecs_ref, ga_s, gx_s, dxc_s):
        vecs_ref[...] = jnp.zeros_like(vecs_ref)
        dcw_ref[...] = jnp.zeros_like(dcw_ref)
        dxc_s[pl.ds(TP, 8), :] = jnp.zeros((8, CB), F32)

        def gate_chunk(ci, carry):
            r0 = pl.multiple_of(ci * TM, TM)
            xb = xc_ref[pl.ds(r0, TM), :].astype(BF16)
            ga_s[pl.ds(r0, TM), :] = jnp.dot(xb, wa_ref[...], preferred_element_type=F32) + ba_ref[...]
            gx_s[pl.ds(r0, TM), :] = jnp.dot(xb, wx_ref[...], preferred_element_type=F32) + bx_ref[...]
            return carry
        lax.fori_loop(0, TP // TM, gate_chunk, 0)

        sp8 = LRU_C * _softplus(-lam_ref[...])
        row = _row_iota((R, CB))
        nchunk = TP // R

        def scan_chunk(cj, carry):
            a_next, lam_next = carry
            ci = nchunk - 1 - cj
            r0 = pl.multiple_of(ci * R, R)
            dyv = dy_ref[pl.ds(r0, R), :]
            g = g_ref[pl.ds(r0, R), :]
            hv = hs_ref[pl.ds(r0, R), :]
            xc = xc_ref[pl.ds(r0, R), :]
            sg = _sig(g)
            dhs = dyv * (g * sg)
            r, i, a, mult = _gate_values(ga_s[pl.ds(r0, R), :], gx_s[pl.ds(r0, R), :], xc, sp8)
            b = jnp.where(row == R - 1, a_next, pltpu.roll(a, R - 1, 0))
            lv = dhs
            k = 1
            while k < R:
                m = row < R - k
                lv = jnp.where(m, lv + b * pltpu.roll(lv, R - k, 0), lv)
                b = jnp.where(m, b * pltpu.roll(b, R - k, 0), b)
                k *= 2
            lv = lv + b * lam_next
            p0 = pl.multiple_of(jnp.maximum(r0 - 8, 0), 8)
            hprev8 = jnp.where(ci > 0, hs_ref[pl.ds(p0, 8), :], 0.0)
            hprev = pltpu.roll(jnp.concatenate([hprev8, hv], axis=0), 1, 0)[8:8 + R, :]
            da = lv * hprev
            ixc = i * xc
            dmult = lv * ixc
            di = lv * mult * xc
            dxc_s[pl.ds(r0, R), :] = lv * mult * i
            a2 = a * a
            dlog_a = da * a - dmult * a2 / mult
            vecs_ref[32:40, :] += _fold8(dlog_a * r)
            dga = -(dlog_a * sp8) * r * (1.0 - r)
            dgx = di * i * (1.0 - i)
            ga_s[pl.ds(r0, R), :] = dga
            gx_s[pl.ds(r0, R), :] = dgx
            vecs_ref[16:24, :] += _fold8(dga)
            vecs_ref[24:32, :] += _fold8(dgx)
            a_first = jnp.sum(jnp.where(row == 0, a, 0.0), axis=0, keepdims=True)
            l_first = jnp.sum(jnp.where(row == 0, lv, 0.0), axis=0, keepdims=True)
            return a_first, l_first
        def scan_pass(i, cr):
            for sub in range(4):
                cr = scan_chunk(4 * i + sub, cr)
            return cr
        lax.fori_loop(0, nchunk // 4, scan_pass, (jnp.zeros((1, CB), F32), jnp.zeros((1, CB), F32)))

        dwa_ref[...] = jnp.zeros_like(dwa_ref)
        dwx_ref[...] = jnp.zeros_like(dwx_ref)

        def mm_chunk(ci, carry):
            r0 = pl.multiple_of(ci * TM, TM)
            xb = xc_ref[pl.ds(r0, TM), :].astype(BF16)
            dgab = ga_s[pl.ds(r0, TM), :].astype(BF16)
            dgxb = gx_s[pl.ds(r0, TM), :].astype(BF16)
            dxc_s[pl.ds(r0, TM), :] += (lax.dot_general(dgab, wa_ref[...], _NT, preferred_element_type=F32)
                                        + lax.dot_general(dgxb, wx_ref[...], _NT, preferred_element_type=F32))
            dwa_ref[...] += lax.dot_general(xb, dgab, _TN, preferred_element_type=F32)
            dwx_ref[...] += lax.dot_general(xb, dgxb, _TN, preferred_element_type=F32)
            return carry
        lax.fori_loop(0, TP // TM, mm_chunk, 0)

        taps = [cw_ref[k:k + 1, :] for k in range(LW)]

        def conv_chunk(ci, carry):
            r0 = pl.multiple_of(ci * RL, RL)
            dbuf = dxc_s[pl.ds(r0, RL + 8), :]
            dcur = dbuf[0:RL, :]
            p0 = pl.multiple_of(jnp.maximum(r0 - 8, 0), 8)
            xprev = jnp.where(ci > 0, x_ref[pl.ds(p0, 8), :], 0.0)
            xbuf = jnp.concatenate([xprev, x_ref[pl.ds(r0, RL), :]], axis=0)
            dxl = dcur * taps[LW - 1]
            dcw_ref[8 * (LW - 1):8 * LW, :] += _fold_rows(dcur * xbuf[8:8 + RL, :])
            for s in range(1, LW):
                k = LW - 1 - s
                dxl = dxl + pltpu.roll(dbuf, RL + 8 - s, 0)[0:RL, :] * taps[k]
                dcw_ref[8 * k:8 * k + 8, :] += _fold_rows(dcur * pltpu.roll(xbuf, s, 0)[8:8 + RL, :])
            dzl_ref[0, pl.ds(r0, RL), :] = dxl.astype(BF16)
            vecs_ref[8:16, :] += _fold_rows(dxl)
            vecs_ref[40:48, :] += _fold_rows(dcur)
            return carry
        lax.fori_loop(0, TP // RL, conv_chunk, 0)
        vecs_ref[32:40, :] = vecs_ref[32:40, :] * (LRU_C * _sig(-lam_ref[...]))

    col = lambda off: pl.BlockSpec((TP, CB), lambda j: (0, off + j))
    vec = pl.BlockSpec((1, CB), lambda j: (0, j))
    wsp = pl.BlockSpec((None, CB, CB), lambda j: (j, 0, 0))
    return pl.pallas_call(
        body, name="lru_bwd",
        grid=(NCB,),
        in_specs=[col(0), col(0), col(NCB), col(0), col(0), pl.BlockSpec((LW, CB), lambda j: (0, j)),
                  wsp, vec, wsp, vec, vec, _AFTER],
        out_specs=[pl.BlockSpec((1, TP, CB), lambda j: (0, 0, j)), wsp, wsp,
                   pl.BlockSpec((8 * LW, CB), lambda j: (0, j)), pl.BlockSpec((8 * NV, CB), lambda j: (0, j))],
        out_shape=[jax.ShapeDtypeStruct((1, TP, DL), BF16),
                   jax.ShapeDtypeStruct((NCB, CB, CB), F32), jax.ShapeDtypeStruct((NCB, CB, CB), F32),
                   jax.ShapeDtypeStruct((8 * LW, DL), F32), jax.ShapeDtypeStruct((8 * NV, DL), F32)],
        scratch_shapes=[pltpu.VMEM((TP, CB), F32), pltpu.VMEM((TP, CB), F32), pltpu.VMEM((TP + 8, CB), F32)],
        compiler_params=_cparams(),
    )(dycat, z, z, xc, hs, conv_w, wa_g, b_a, wx_g, b_x, lam, after)


def _dz_section(sec, dzl_ref, dz41_ref, dzc_ref, use):
    @pl.when(sec == 0)
    def _():
        use(dzl_ref)

    @pl.when(jnp.logical_or(sec == 1, sec == 4))
    def _():
        use(dz41_ref)

    @pl.when(jnp.logical_or(sec == 2, sec == 3))
    def _():
        use(dzc_ref)


def _dz_specs(rows, index):
    return [pl.BlockSpec((None, rows, 1024), lambda a, b: (0, index(a, b)[0], 0)),
            pl.BlockSpec((None, rows, 1024), lambda a, b: (jnp.where(index(a, b)[1] == 1, 1, 0), index(a, b)[0], 0)),
            pl.BlockSpec((None, rows, 1024), lambda a, b: (jnp.clip(index(a, b)[1] - 2, 0, 1), index(a, b)[0], 0))]


def _inproj_wgrad(name, hn, dzs, after):
    KB = 512
    nsec = dzs.shape[0]

    def body(hn_ref, dz_ref, after_ref, dw_ref):
        dw_ref[...] = lax.dot_general(hn_ref[...], dz_ref[...], _TN, preferred_element_type=F32).astype(BF16)

    return pl.pallas_call(
        body, name=name,
        grid=(nsec, D // KB),
        in_specs=[pl.BlockSpec((TP, KB), lambda n, kb: (0, kb)),
                  pl.BlockSpec((None, TP, 1024), lambda n, kb: (n, 0, 0)), _AFTER],
        out_specs=pl.BlockSpec((KB, 1024), lambda n, kb: (kb, n)),
        out_shape=jax.ShapeDtypeStruct((D, nsec * 1024), BF16),
        compiler_params=_cparams(),
    )(hn, dzs, after)


def _sum_win_parts(parts_a, parts_b, parts_c):
    RB = 64

    def body(a_ref, b_ref, c_ref, o_ref):
        def chunk(ci, carry):
            r0 = pl.multiple_of(ci * R, R)
            for ref, src, base, ncol in ((a_ref, 0, 0, 1024), (c_ref, 1024, 1024, 1024), (b_ref, 0, 2048, 2048),
                                         (c_ref, 0, 4096, 1024)):
                for c0 in range(0, ncol, 512):
                    acc = ref[0, pl.ds(r0, R), src + c0:src + c0 + 512].astype(F32)
                    for sidx in range(1, NDEV):
                        acc = acc + ref[sidx, pl.ds(r0, R), src + c0:src + c0 + 512].astype(F32)
                    o_ref[pl.ds(r0, R), base + c0:base + c0 + 512] = acc.astype(BF16)
            return carry
        lax.fori_loop(0, RB // R, chunk, 0)

    spec = lambda ncol: pl.BlockSpec((NDEV, RB, ncol), lambda i: (0, i, 0))
    return pl.pallas_call(
        body, name="sum_win_parts",
        grid=(D // NDEV // RB,),
        in_specs=[spec(1024), spec(2048), spec(2048)],
        out_specs=pl.BlockSpec((RB, NIN), lambda i: (i, 0)),
        out_shape=jax.ShapeDtypeStruct((D // NDEV, NIN), BF16),
        compiler_params=_cparams(),
    )(parts_a, parts_b, parts_c)


def _inproj_bwd(dzl, dz41, dzc, w_in, h, dout, pre_w, after):
    nsec = NIN // 1024

    def body(dzl_ref, dz41_ref, dzc_ref, w_ref, h_ref, dout_ref, pw_ref, after_ref, gx_hbm, dmeta_ref, dpw_ref,
             acc_s, dh_s, sem):
        i = pl.program_id(0)
        s = pl.program_id(1)

        def gx_copy(t):
            lo, n, off = _tile_rows(t)
            return pltpu.make_async_copy(dh_s.at[pl.ds(off, n)], gx_hbm.at[pl.ds(lo, n)], sem)

        @pl.when(s == 0)
        def _():
            acc_s[...] = jnp.zeros_like(acc_s)

        def use(dz_ref):
            acc_s[...] += lax.dot_general(dz_ref[...], w_ref[...], _NT, preferred_element_type=F32)
        _dz_section(s, dzl_ref, dz41_ref, dzc_ref, use)

        @pl.when(jnp.logical_and(i == 0, s == nsec - 1))
        def _():
            dpw_ref[...] = jnp.zeros_like(dpw_ref)

        @pl.when(s == nsec - 1)
        def _():
            _for_tile(i - 1, lambda t: gx_copy(t).wait())
            pw = pw_ref[...]

            def chunk(ci, carry):
                r0 = pl.multiple_of(ci * 8, 8)
                hv = h_ref[pl.ds(r0, 8), :]
                dhn = acc_s[pl.ds(r0, 8), :]
                rs = lax.rsqrt(jnp.mean(hv * hv, axis=-1, keepdims=True) + EPS)
                dpw_ref[...] += dhn * (hv * rs)
                gw = dhn * pw
                dot = jnp.mean(gw * hv, axis=-1, keepdims=True)
                dh_s[pl.ds(r0, 8), :] = rs * gw - hv * (rs * rs * rs * dot) + dout_ref[pl.ds(r0, 8), :]
                return carry
            lax.fori_loop(0, TM // 8, chunk, 0, unroll=4)
            _for_tile(i, lambda t: gx_copy(t).start())

            @pl.when(i == 0)
            def _():
                dmeta_ref[...] = dh_s[0:NMETA, :]

            @pl.when(i == NTILE - 1)
            def _():
                gx_copy(NTILE - 1).wait()

    row = pl.BlockSpec((TM, D), lambda i, s: (i, 0))
    return pl.pallas_call(
        body, name="inproj_bwd",
        grid=(TP // TM, nsec),
        in_specs=_dz_specs(TM, lambda i, s: (i, s)) + [
            pl.BlockSpec((D, 1024), lambda i, s: (0, s)), row, row, pl.BlockSpec((1, D), lambda i, s: (0, 0)),
            _AFTER],
        out_specs=[pl.BlockSpec(memory_space=pl.ANY), pl.BlockSpec((NMETA, D), lambda i, s: (0, 0)),
                   pl.BlockSpec((8, D), lambda i, s: (0, 0))],
        out_shape=[jax.ShapeDtypeStruct((SEQ, D), F32), jax.ShapeDtypeStruct((NMETA, D), F32),
                   jax.ShapeDtypeStruct((8, D), F32)],
        scratch_shapes=[pltpu.VMEM((TM, D), F32), pltpu.VMEM((TM, D), F32), pltpu.SemaphoreType.DMA(())],
        compiler_params=_cparams(),
    )(dzl, dz41, dzc, w_in, h, dout, pre_w, after)


def _adamw(name, parts, w, m, v, block_rows):
    rows, cols = w.shape
    nparts = parts.shape[0]
    cw = cols if cols <= 640 else 512

    def body(p_ref, w_ref, m_ref, v_ref, g_ref, d_ref, nm_ref, nv_ref):
        def chunk(ci, carry):
            r0 = pl.multiple_of(ci * R, R)
            for c0 in range(0, cols, cw):
                at = (pl.ds(r0, R), slice(c0, c0 + cw))
                g = p_ref[(0,) + at].astype(F32)
                for sidx in range(1, nparts):
                    g = g + p_ref[(sidx,) + at].astype(F32)
                delta, mv, vv = _adam_math(g, w_ref[at], m_ref[at], v_ref[at])
                g_ref[at] = g
                nm_ref[at] = mv
                nv_ref[at] = vv
                d_ref[at] = delta
            return carry
        lax.fori_loop(0, block_rows // R, chunk, 0)

    blk = pl.BlockSpec((block_rows, cols), lambda i: (i, 0))
    shp = jax.ShapeDtypeStruct((rows, cols), F32)
    return pl.pallas_call(
        body, name=name,
        grid=(rows // block_rows,),
        in_specs=[pl.BlockSpec((nparts, block_rows, cols), lambda i: (0, i, 0)), blk, blk, blk],
        out_specs=[blk, blk, blk, blk],
        out_shape=[shp, shp, shp, shp],
        compiler_params=_cparams(),
    )(parts, w, m, v)


def _adam_math(g, w, m, v):
    c1 = 1.0 / (1.0 - ADAM_B1 ** ADAM_STEP)
    c2 = 1.0 / (1.0 - ADAM_B2 ** ADAM_STEP)
    mv = ADAM_B1 * m + (1.0 - ADAM_B1) * g
    vv = ADAM_B2 * v + (1.0 - ADAM_B2) * (g * g)
    upd = (mv * c1) / (jnp.sqrt(vv * c2) + ADAM_EPS) + ADAM_WD * w
    return -ADAM_LR * upd, mv, vv


_VEC = [("pre_norm_w", 2), ("post_norm_w", 2), ("b_in", 5), ("lru_conv_b", 1), ("b_gate_a", 1), ("b_gate_x", 1),
        ("lru_lambda", 1), ("conf_dw_b", 1), ("conf_ln_w", 1), ("conf_ln_b", 1), ("conf_pw_b", 1)]
_VEC_ROWS = 24
_LOSS_ROW = 17
_SM_ROWS = 64


def _pack_grads(dprew_acc, dpostw_acc, cvecs, kvecs, lvecs, dcw_acc, ddw_acc, dh, loss_acc):
    def body(pre_ref, post_ref, c_ref, k_ref, l_ref, dcw_ref, ddw_ref, dh_ref, loss_ref, vec_ref, small_ref, tmp):
        s8 = lambda ref, r: jnp.sum(ref[8 * r:8 * r + 8, :], axis=0, keepdims=True)
        vec_ref[...] = jnp.zeros_like(vec_ref)
        pre, post = s8(pre_ref, 0), s8(post_ref, 0)
        rows = [pre[:, 0:1024], pre[:, 1024:2048], post[:, 0:1024], post[:, 1024:2048],
                s8(l_ref, 1), s8(c_ref, 4), s8(k_ref, 1), s8(k_ref, 2), s8(c_ref, 1),
                s8(l_ref, 5), s8(l_ref, 2), s8(l_ref, 3), s8(l_ref, 4),
                s8(k_ref, 0), s8(c_ref, 2), s8(c_ref, 3), s8(c_ref, 0)]
        for r, val in enumerate(rows):
            vec_ref[r:r + 1, :] = val
        vec_ref[_LOSS_ROW:_LOSS_ROW + 1, :] = jnp.zeros((1, 1024), F32) + (0.5 / D) * jnp.sum(loss_ref[...])

        small_ref[...] = jnp.zeros_like(small_ref)
        for k in range(LW):
            tmp[k:k + 1, :] = s8(dcw_ref, k)
        for k in range(KW):
            tmp[8 + k:9 + k, :] = s8(ddw_ref, k)
        for d in range(NDEV):
            small_ref[d, 0:LW, 0:128] = tmp[0:LW, 128 * d:128 * d + 128]
            small_ref[d, 8:8 + KW, 0:128] = tmp[8:8 + KW, 128 * d:128 * d + 128]
            small_ref[d, 40:56, :] = dh_ref[:, 256 * d:256 * d + 256]

    full = lambda a: pl.BlockSpec(a.shape, lambda i: (0,) * a.ndim)
    ins = [dprew_acc, dpostw_acc, cvecs, kvecs, lvecs, dcw_acc, ddw_acc]
    return pl.pallas_call(
        body, name="pack_grads",
        grid=(1,),
        in_specs=[full(a) for a in ins] + [full(dh), full(loss_acc)],
        out_specs=[pl.BlockSpec((_VEC_ROWS, 1024), lambda i: (0, 0)),
                   pl.BlockSpec((NDEV, _SM_ROWS, 256), lambda i: (0, 0, 0))],
        out_shape=[jax.ShapeDtypeStruct((_VEC_ROWS, 1024), F32), jax.ShapeDtypeStruct((NDEV, _SM_ROWS, 256), F32)],
        scratch_shapes=[pltpu.VMEM((40, 1024), F32)],
        compiler_params=_cparams(),
    )(*ins, dh, loss_acc)


def _adamw_vec(parts, W, M, V):
    nv = len(_VEC)

    def body(*refs):
        p_ref = refs[0]
        w_refs, m_refs, v_refs = refs[1:1 + nv], refs[1 + nv:1 + 2 * nv], refs[1 + 2 * nv:1 + 3 * nv]
        outs = refs[1 + 3 * nv:]

        def total(r):
            acc = p_ref[0, r:r + 1, :]
            for sidx in range(1, NDEV):
                acc = acc + p_ref[sidx, r:r + 1, :]
            return acc

        row = 0
        for idx, (_, nrows) in enumerate(_VEC):
            for part in range(nrows):
                cols = slice(1024 * part, 1024 * part + 1024)
                g = total(row + part)
                delta, mv, vv = _adam_math(g, w_refs[idx][:, cols], m_refs[idx][:, cols], v_refs[idx][:, cols])
                for o, val in zip(outs[4 * idx:4 * idx + 4], (g, delta, mv, vv)):
                    o[:, cols] = val
            row += nrows
        outs[-1][...] = total(_LOSS_ROW)[:, 0:128]

    names = [n for n, _ in _VEC]
    flat = lambda d: [d[n].reshape(1, -1) for n in names]
    ws, ms, vs = flat(W), flat(M), flat(V)
    res = pl.pallas_call(
        body, name="adamw_vec",
        out_shape=[jax.ShapeDtypeStruct(w.shape, F32) for w in ws for _ in range(4)]
        + [jax.ShapeDtypeStruct((1, 128), F32)],
        compiler_params=_cparams(),
    )(parts, *ws, *ms, *vs)
    return {n: tuple(res[4 * i:4 * i + 4]) for i, n in enumerate(names)}, res[-1]


def _adamw_small(parts, W, M, V):
    where = {"lru_conv_w": (slice(0, LW), slice(0, 128)), "conf_dw_w": (slice(8, 8 + KW), slice(0, 128)),
             "meta_tokens": (slice(40, 56), slice(0, 256))}
    names = list(where)

    def body(*refs):
        p_ref = refs[0]
        outs = refs[10:]
        for idx, n in enumerate(names):
            rs, cs = where[n]
            g = p_ref[0, rs, cs]
            for sidx in range(1, NDEV):
                g = g + p_ref[sidx, rs, cs]
            delta, mv, vv = _adam_math(g, refs[1 + idx][...], refs[4 + idx][...], refs[7 + idx][...])
            for o, val in zip(outs[4 * idx:4 * idx + 4], (g, delta, mv, vv)):
                o[...] = val

    two_d = lambda a: a.reshape(a.shape[-2:])
    ws, ms, vs = ([two_d(d[n]) for n in names] for d in (W, M, V))
    res = pl.pallas_call(
        body, name="adamw_small",
        out_shape=[jax.ShapeDtypeStruct(w.shape, F32) for w in ws for _ in range(4)],
        compiler_params=_cparams(),
    )(parts, *ws, *ms, *vs)
    return {n: tuple(res[4 * i:4 * i + 4]) for i, n in enumerate(names)}


def _pack_small(lru_cw, dw_w, meta):
    buf = jnp.zeros((_SM_ROWS, 256), F32)
    buf = buf.at[0:LW, 0:128].set(lru_cw)
    buf = buf.at[8:8 + dw_w.shape[0], 0:128].set(dw_w)
    return buf.at[40:56, :].set(meta)


def _block_diag4(w):
    w4 = w.reshape(NCB, 4, 64, 64)
    eye = jnp.eye(4, dtype=w.dtype)
    return jnp.einsum("ghij,hk->ghikj", w4, eye).reshape(NCB, CB, CB)


def _diag_blocks(g):
    g5 = g.reshape(NCB, 4, 64, 4, 64)
    return jnp.stack([g5[:, hh, :, hh, :] for hh in range(4)], axis=1).reshape(16, 64, 64)


def _gate_mats(W):
    return _block_diag4(W["w_gate_a"][0]).astype(BF16), _block_diag4(W["w_gate_x"][0]).astype(BF16)


def _local_step(x, target, meta_full, inproj, out_weights, lru_cw_full, dw_w_full, W, gate_mats, send):
    wa_g, wx_g = gate_mats

    h, hn = _prenorm(x, meta_full, W["pre_norm_w"])
    z, win_full = inproj(hn)
    ylru, xc, hs = _lru_fwd(z, lru_cw_full, W["lru_conv_b"], wa_g, W["b_gate_a"], wx_g, W["b_gate_x"],
                            W["lru_lambda"])
    vc = _conf_fwd_conv(z, dw_w_full, W["conf_dw_b"])
    wout_full, pw_full = out_weights(vc)
    yconf, p, xhat, rstd = _conf_fwd_proj(vc, z, W["conf_ln_w"], W["conf_ln_b"], pw_full, W["conf_pw_b"])
    dout, dy, loss_acc, dpostw_acc = _outproj_loss(ylru, yconf, wout_full, h, target, W["post_norm_w"])

    dycat, dwout_part = _outproj_bwd(dy, ylru, yconf, wout_full)
    tok = send("w_out", ("w_out", dwout_part))
    dvc, dz41, dpw_part, cvecs = _conf_bwd_proj(dycat, p, z, xhat, rstd, hs, W["conf_ln_w"], W["conf_ln_b"], pw_full, tok)
    tok = send("w_in_c", ("conf_pw_w", dpw_part), ("w_in_c", _inproj_wgrad("inproj_wgrad_c", hn, dz41, dz41)))
    dzc, ddw_acc, kvecs = _conf_bwd_conv(dvc, z, dw_w_full, tok)
    tok = send("w_in_b", ("w_in_b", _inproj_wgrad("inproj_wgrad_b", hn, dzc, dzc)))
    dzl, dwa_g, dwx_g, dcw_acc, lvecs = _lru_bwd(dycat, z, xc, hs, lru_cw_full, wa_g, W["b_gate_a"], wx_g,
                                                 W["b_gate_x"], W["lru_lambda"], tok)
    tok = send("w_gates", ("w_gate_a", _diag_blocks(dwa_g).reshape(16 * 64, 64)),
               ("w_gate_x", _diag_blocks(dwx_g).reshape(16 * 64, 64)))
    tok = send("w_in_a", ("w_in_a", _inproj_wgrad("inproj_wgrad_a", hn, dzl, tok)))
    grad_x, dmeta, dprew_acc = _inproj_bwd(dzl, dz41, dzc, win_full, h, dout, W["pre_norm_w"], tok)

    vec_pack, small_part = _pack_grads(dprew_acc, dpostw_acc, cvecs, kvecs, lvecs, dcw_acc, ddw_acc, dmeta, loss_acc)
    return grad_x, vec_pack, small_part


def kernel(x, meta_tokens, pre_norm_w, post_norm_w, w_in, b_in, lru_conv_w, lru_conv_b, w_gate_a, b_gate_a, w_gate_x, b_gate_x, lru_lambda, conf_dw_w, conf_dw_b, conf_ln_w, conf_ln_b, conf_pw_w, conf_pw_b, w_out, loss_target, m_meta_tokens, m_pre_norm_w, m_post_norm_w, m_w_in, m_b_in, m_lru_conv_w, m_lru_conv_b, m_w_gate_a, m_b_gate_a, m_w_gate_x, m_b_gate_x, m_lru_lambda, m_conf_dw_w, m_conf_dw_b, m_conf_ln_w, m_conf_ln_b, m_conf_pw_w, m_conf_pw_b, m_w_out, v_meta_tokens, v_pre_norm_w, v_post_norm_w, v_w_in, v_b_in, v_lru_conv_w, v_lru_conv_b, v_w_gate_a, v_b_gate_a, v_w_gate_x, v_b_gate_x, v_lru_lambda, v_conf_dw_w, v_conf_dw_b, v_conf_ln_w, v_conf_ln_b, v_conf_pw_w, v_conf_pw_b, v_w_out):
    W = dict(meta_tokens=meta_tokens, pre_norm_w=pre_norm_w, post_norm_w=post_norm_w, w_in=w_in, b_in=b_in,
             lru_conv_w=lru_conv_w, lru_conv_b=lru_conv_b, w_gate_a=w_gate_a, b_gate_a=b_gate_a,
             w_gate_x=w_gate_x, b_gate_x=b_gate_x, lru_lambda=lru_lambda, conf_dw_w=conf_dw_w,
             conf_dw_b=conf_dw_b, conf_ln_w=conf_ln_w, conf_ln_b=conf_ln_b, conf_pw_w=conf_pw_w,
             conf_pw_b=conf_pw_b, w_out=w_out)
    M = dict(meta_tokens=m_meta_tokens, pre_norm_w=m_pre_norm_w, post_norm_w=m_post_norm_w, w_in=m_w_in,
             b_in=m_b_in, lru_conv_w=m_lru_conv_w, lru_conv_b=m_lru_conv_b, w_gate_a=m_w_gate_a,
             b_gate_a=m_b_gate_a, w_gate_x=m_w_gate_x, b_gate_x=m_b_gate_x, lru_lambda=m_lru_lambda,
             conf_dw_w=m_conf_dw_w, conf_dw_b=m_conf_dw_b, conf_ln_w=m_conf_ln_w, conf_ln_b=m_conf_ln_b,
             conf_pw_w=m_conf_pw_w, conf_pw_b=m_conf_pw_b, w_out=m_w_out)
    V = dict(meta_tokens=v_meta_tokens, pre_norm_w=v_pre_norm_w, post_norm_w=v_post_norm_w, w_in=v_w_in,
             b_in=v_b_in, lru_conv_w=v_lru_conv_w, lru_conv_b=v_lru_conv_b, w_gate_a=v_w_gate_a,
             b_gate_a=v_b_gate_a, w_gate_x=v_w_gate_x, b_gate_x=v_b_gate_x, lru_lambda=v_lru_lambda,
             conf_dw_w=v_conf_dw_w, conf_dw_b=v_conf_dw_b, conf_ln_w=v_conf_ln_w, conf_ln_b=v_conf_ln_b,
             conf_pw_w=v_conf_pw_w, conf_pw_b=v_conf_pw_b, w_out=v_w_out)
    names = list(W.keys())
    shapes = {n: W[n].shape for n in names}

    small = _pack_small(lru_conv_w[0], conf_dw_w[0], meta_tokens)
    (small_flight,), tok = _exchange_start("gather_small_start", [
        (small, jax.ShapeDtypeStruct((NDEV, _SM_ROWS, 256), F32), _whole, _slot)])
    win_flight, tok = _win_gather_start(w_in[0].astype(BF16) + tok[0, 0].astype(BF16))
    win_flight, tok = _win_gather_links(win_flight, tok)
    gate_mats = _gate_mats(W)
    gathered, tok = _exchange_start("gather_out_start", [
        (w_out[0].astype(BF16) + tok[0, 0].astype(BF16), jax.ShapeDtypeStruct((D, D), BF16), _whole,
         _rows(D // NDEV)),
        (conf_pw_w[0].astype(BF16), jax.ShapeDtypeStruct((DC, DC), BF16), _whole, _rows(DC // NDEV)),
    ])
    (small_all,) = _exchange_wait("gather_small_wait", [small_flight], tok)
    unshard = lambda a: jnp.transpose(a, (1, 0, 2)).reshape(a.shape[1], -1)
    lru_cw_full = unshard(small_all[:, 0:LW, 0:128])
    dw_w_full = unshard(small_all[:, 8:8 + KWP, 0:128])
    meta_full = unshard(small_all[:, 40:56, :])

    def out_weights(after):
        return _exchange_wait("gather_out_wait", gathered, after)

    def inproj(hn):
        xi, yi, ci = lax.axis_index("x"), lax.axis_index("y"), lax.axis_index("c")
        shard = lambda px, py, pc: (4 * px + 2 * py + pc).astype(jnp.int32)
        over_links = jnp.stack([shard(1 - xi, yi, ci), shard(xi, 1 - yi, ci), shard(1 - xi, 1 - yi, ci)])
        z, src = _inproj_cols("inproj_own", jnp.stack([shard(xi, yi, ci)]), hn, win_flight["src"], b_in, None)
        flight = _win_gather_early(dict(win_flight, src=src))
        z, land = _inproj_cols("inproj_here", jnp.stack([shard(xi, yi, 1 - ci)]), hn, flight["land"], b_in, z)
        flight = _win_gather_forward("all", dict(flight, land=land), (1, 2, 3), z)
        z, land = _inproj_cols("inproj_links", over_links, hn, flight["land"], b_in, z)
        flight = _win_gather_forwarded("all", dict(flight, land=land), (1, 2, 3))
        z, land = _inproj_cols("inproj_sibling", over_links + 1 - 2 * ci, hn, flight["land"], b_in, z)
        return z, _win_gather_wait(dict(flight, land=land))

    row_stage = lambda ncol: (jax.ShapeDtypeStruct((NDEV, D // NDEV, ncol), BF16), _rows(D // NDEV))
    piece = {"w_in_a": row_stage(1024), "w_in_b": row_stage(2048), "w_in_c": row_stage(2048),
             "w_out": row_stage(D),
             "conf_pw_w": (jax.ShapeDtypeStruct((NDEV, DC // NDEV, DC), BF16), _rows(DC // NDEV)),
             "w_gate_a": (jax.ShapeDtypeStruct((NDEV, 16 * 64, 64), BF16), _whole),
             "w_gate_x": (jax.ShapeDtypeStruct((NDEV, 16 * 64, 64), BF16), _whole)}
    sent = {}

    def send(call, *named_parts):
        handles, token = _exchange_start(
            "scatter_" + call + "_start",
            [(part.astype(BF16), piece[name][0], piece[name][1], _slot) for name, part in named_parts])
        for (name, _), handle in zip(named_parts, handles):
            sent[name] = [handle]
        return token

    grad_x, vec_pack, small_part = _local_step(
        x[0], loss_target[0], meta_full, inproj, out_weights, lru_cw_full, dw_w_full, W, gate_mats, send)
    grad_x = grad_x[None]

    rest, tok = _exchange_start("scatter_rest_start", [
        (small_part, jax.ShapeDtypeStruct((NDEV, _SM_ROWS, 256), F32), _slot, _slot),
        (vec_pack, jax.ShapeDtypeStruct((NDEV, _VEC_ROWS, 1024), F32), _whole, _slot),
    ])
    (parts_c,) = _exchange_wait("scatter_w_in_c_wait", sent["w_in_c"], tok)
    (parts_b,) = _exchange_wait("scatter_w_in_b_wait", sent["w_in_b"], parts_c)
    (parts_a,) = _exchange_wait("scatter_w_in_a_wait", sent["w_in_a"], parts_b)
    win_rows = _sum_win_parts(parts_a, parts_b, parts_c)
    win_stage2, tok = _exchange_start("scatter_w_in_stage2_start", [
        (win_rows, jax.ShapeDtypeStruct((NDEV, D // NDEV, NIN // NDEV), BF16), _cols(NIN // NDEV), _slot)])

    G, DW, NM, NV = {}, {}, {}, {}
    (wout_parts,) = _exchange_wait("scatter_w_out_wait", sent["w_out"], tok)
    G["w_out"], DW["w_out"], NM["w_out"], NV["w_out"] = _adamw("adamw_w_out", wout_parts, w_out[0], m_w_out[0], v_w_out[0], 64)
    (pw_parts,) = _exchange_wait("scatter_conf_pw_w_wait", sent["conf_pw_w"], G["w_out"])
    G["conf_pw_w"], DW["conf_pw_w"], NM["conf_pw_w"], NV["conf_pw_w"] = _adamw(
        "adamw_pw", pw_parts, conf_pw_w[0], m_conf_pw_w[0], v_conf_pw_w[0], 128)
    res = {}
    wa_parts, wx_parts = _exchange_wait("scatter_w_gates_wait", sent["w_gate_a"] + sent["w_gate_x"], G["conf_pw_w"])
    for n, parts in (("w_gate_a", wa_parts), ("w_gate_x", wx_parts)):
        res[n] = _adamw("adamw_" + n, parts, *[d[n].reshape(16 * 64, 64) for d in (W, M, V)], 16 * 64)
    small_parts, vec_parts = _exchange_wait("scatter_rest_wait", rest, res["w_gate_x"][0])
    res.update(_adamw_small(small_parts, W, M, V))
    vec_res, loss_row = _adamw_vec(vec_parts, W, M, V)
    res.update(vec_res)
    (win_sum,) = _exchange_wait("scatter_w_in_stage2_wait", win_stage2, loss_row)
    res["w_in"] = _adamw("adamw_w_in", win_sum.reshape(1, D, NIN // NDEV), w_in[0], m_w_in[0], v_w_in[0], 256)
    for n, vals in res.items():
        for dst, val in zip((G, DW, NM, NV), vals):
            dst[n] = val
    for dst in (G, DW, NM, NV):
        for n in names:
            dst[n] = dst[n].reshape(shapes[n])
    loss = loss_row[0, 0]

    return (loss, grad_x, *[G[n] for n in names], *[DW[n] for n in names],
            *[NM[n] for n in names], *[NV[n] for n in names])
```

```python
import functools

import jax
import jax.numpy as jnp
from jax import lax
from jax.experimental import pallas as pl
from jax.experimental.pallas import tpu as pltpu

F32 = jnp.float32
BF16 = jnp.bfloat16

D = 2048
DL = 1024
DC = 1024
NIN = 5120
NMETA = 16
SEQ = 2048
T = NMETA + SEQ
TP = 2176
TM = 544
CB = 256
NCB = DL // CB
R = 16
RL = 64
KW = 31
KWP = 32
LW = 4
LRU_C = 8.0
EPS = 1e-6
NDEV = 8

ADAM_LR = 0.001
ADAM_B1 = 0.9
ADAM_B2 = 0.999
ADAM_EPS = 1e-08
ADAM_WD = 0.01
ADAM_STEP = 10

VMEM_LIMIT = 56 * 1024 * 1024


def _cparams():
    return pltpu.CompilerParams(vmem_limit_bytes=VMEM_LIMIT)


def _sig(x):
    return 1.0 / (1.0 + jnp.exp(-x))


def _expm1_neg(y):
    poly = y * (1.0 + y * (0.5 + y * (1.0 / 6.0 + y * (1.0 / 24.0 + y * (1.0 / 120.0)))))
    return jnp.where(y > -0.1, poly, jnp.exp(y) - 1.0)


def _softplus(x):
    e = jnp.exp(-jnp.abs(x))
    w = 1.0 + e
    l1p = jnp.where(w == 1.0, e, jnp.log(w) * e / (w - 1.0))
    return jnp.maximum(x, 0.0) + l1p


def _row_iota(shape):
    return lax.broadcasted_iota(jnp.int32, shape, 0)


def _fold8(v):
    return v[0:8, :] + v[8:16, :]


_FLIPS = [(k >> 2 & 1, k >> 1 & 1, k & 1) for k in range(1, NDEV)]
_HBM = pl.BlockSpec(memory_space=pltpu.HBM)
_SEM = pl.BlockSpec(memory_space=pltpu.SEMAPHORE)


def _peers():
    x, y, c = lax.axis_index("x"), lax.axis_index("y"), lax.axis_index("c")
    out = []
    for dx, dy, dc in _FLIPS:
        px = 1 - x if dx else x
        py = 1 - y if dy else y
        pc = 1 - c if dc else c
        out.append(((px, py, pc), 4 * px + 2 * py + pc))
    return 4 * x + 2 * y + c, out


def _exchange_start(name, items):
    n = len(items)

    def body(*refs):
        srcs, lands = refs[:n], refs[n:2 * n]
        outs = refs[2 * n:]
        send_sems, recv_sems, local_sems = outs[:n], outs[n:2 * n], outs[2 * n:3 * n]
        token = outs[-1]
        me, peers = _peers()
        for a in range(n):
            src_at, dst_at = items[a][2], items[a][3]
            pltpu.make_async_copy(src_at(srcs[a], me), dst_at(lands[a], me), local_sems[a]).start()
        for a in range(n):
            src_at, dst_at = items[a][2], items[a][3]
            for k, (pos, peer) in enumerate(peers):
                pltpu.make_async_remote_copy(
                    src_ref=src_at(srcs[a], peer), dst_ref=dst_at(lands[a], me),
                    send_sem=send_sems[a].at[k], recv_sem=recv_sems[a].at[k],
                    device_id=pos, device_id_type=pl.DeviceIdType.MESH).start()
        token[...] = jnp.zeros_like(token)

    srcs = [pltpu.with_memory_space_constraint(it[0], pltpu.HBM) for it in items]
    lands = [pltpu.with_memory_space_constraint(lax.empty(it[1].shape, it[1].dtype), pltpu.HBM) for it in items]
    sem7 = pltpu.SemaphoreType.DMA((NDEV - 1,))
    res = pl.pallas_call(
        body, name=name,
        out_shape=([sem7] * (2 * n) + [pltpu.SemaphoreType.DMA(())] * n
                   + [pltpu.HBM(a.shape, a.dtype) for a in srcs] + [pltpu.HBM(a.shape, a.dtype) for a in lands]
                   + [jax.ShapeDtypeStruct((8, 128), F32)]),
        in_specs=[_HBM] * (2 * n),
        out_specs=[_SEM] * (3 * n) + [_HBM] * (2 * n) + [pl.BlockSpec(memory_space=pltpu.VMEM)],
        input_output_aliases={i: 3 * n + i for i in range(2 * n)},
        compiler_params=pltpu.CompilerParams(has_side_effects=pltpu.SideEffectType.DATAFLOW_SIDE_EFFECTING),
    )(*srcs, *lands)
    handles = [dict(send=res[a], recv=res[n + a], local=res[2 * n + a], src=res[3 * n + a], land=res[4 * n + a],
                    src_at=items[a][2], dst_at=items[a][3]) for a in range(n)]
    return handles, res[-1]


def _wait_bytes(piece, sem):
    pltpu.make_async_copy(piece, piece, sem).wait()


def _exchange_wait(name, handles, after):
    n = len(handles)

    def body(*refs):
        srcs, lands = refs[:n], refs[n:2 * n]
        send_sems, recv_sems, local_sems = refs[2 * n:3 * n], refs[3 * n:4 * n], refs[4 * n:5 * n]
        me, peers = _peers()
        for a in range(n):
            src_at, dst_at = handles[a]["src_at"], handles[a]["dst_at"]
            for k, (pos, peer) in enumerate(peers):
                _wait_bytes(src_at(srcs[a], peer), send_sems[a].at[k])
                _wait_bytes(dst_at(lands[a], peer), recv_sems[a].at[k])
            pltpu.make_async_copy(src_at(srcs[a], me), dst_at(lands[a], me), local_sems[a]).wait()

    srcs = [hd["src"] for hd in handles]
    lands = [hd["land"] for hd in handles]
    res = pl.pallas_call(
        body, name=name,
        out_shape=[pltpu.HBM(a.shape, a.dtype) for a in srcs] + [pltpu.HBM(a.shape, a.dtype) for a in lands],
        in_specs=[_HBM] * (2 * n) + [_SEM] * (3 * n) + [pl.BlockSpec(memory_space=pl.ANY)],
        out_specs=[_HBM] * (2 * n),
        input_output_aliases={i: i for i in range(2 * n)},
        compiler_params=pltpu.CompilerParams(has_side_effects=pltpu.SideEffectType.DATAFLOW_SIDE_EFFECTING),
    )(*srcs, *lands, *[hd["send"] for hd in handles], *[hd["recv"] for hd in handles],
      *[hd["local"] for hd in handles], after)
    return list(res[n:])


_SIDE = pltpu.SideEffectType.DATAFLOW_SIDE_EFFECTING
_WCOLS = NIN // NDEV


def _win_cols(ref, l):
    return ref.at[:, pl.ds(pl.multiple_of(l * _WCOLS, 128), _WCOLS)]


def _win_routes():
    x, y, c = lax.axis_index("x"), lax.axis_index("y"), lax.axis_index("c")
    pos = [(x, y, 1 - c), (1 - x, y, c), (x, 1 - y, c), (1 - x, 1 - y, c)]
    return 4 * x + 2 * y + c, [(p, 4 * p[0] + 2 * p[1] + p[2]) for p in pos]


def _win_gather_start(shard):
    def body(src, land, send_sem, recv_sem, local_sem, src_thru, land_thru, token):
        me, routes = _win_routes()
        pltpu.make_async_copy(src, _win_cols(land, me), local_sem).start()
        pltpu.make_async_remote_copy(src_ref=src, dst_ref=_win_cols(land, me), send_sem=send_sem, recv_sem=recv_sem,
                                     device_id=routes[0][0], device_id_type=pl.DeviceIdType.MESH).start()
        token[...] = jnp.zeros_like(token)

    src = pltpu.with_memory_space_constraint(shard, pltpu.HBM)
    land = pltpu.with_memory_space_constraint(lax.empty((D, NIN), BF16), pltpu.HBM)
    sem = pltpu.SemaphoreType.DMA(())
    res = pl.pallas_call(
        body, name="win_gather_start",
        out_shape=[sem, sem, sem, pltpu.HBM(src.shape, BF16), pltpu.HBM(land.shape, BF16),
                   jax.ShapeDtypeStruct((8, 128), F32)],
        in_specs=[_HBM, _HBM],
        out_specs=[_SEM, _SEM, _SEM, _HBM, _HBM, pl.BlockSpec(memory_space=pltpu.VMEM)],
        input_output_aliases={0: 3, 1: 4},
        compiler_params=pltpu.CompilerParams(has_side_effects=_SIDE),
    )(src, land)
    return dict(send0=res[0], recv0=res[1], local=res[2], src=res[3], land=res[4]), res[5]


def _win_gather_links(hd, after):
    def body(src, land, after_ref, send_sems, recv_sems, src_thru, land_thru, token):
        me, routes = _win_routes()
        for k in (1, 2, 3):
            pltpu.make_async_remote_copy(src_ref=src, dst_ref=_win_cols(land, me), send_sem=send_sems.at[k - 1],
                                         recv_sem=recv_sems.at[k - 1], device_id=routes[k][0],
                                         device_id_type=pl.DeviceIdType.MESH).start()
        token[...] = jnp.zeros_like(token)

    sem3 = pltpu.SemaphoreType.DMA((3,))
    res = pl.pallas_call(
        body, name="win_gather_links",
        out_shape=[sem3, sem3, pltpu.HBM(hd["src"].shape, BF16), pltpu.HBM(hd["land"].shape, BF16),
                   jax.ShapeDtypeStruct((8, 128), F32)],
        in_specs=[_HBM, _HBM, pl.BlockSpec(memory_space=pl.ANY)],
        out_specs=[_SEM, _SEM, _HBM, _HBM, pl.BlockSpec(memory_space=pltpu.VMEM)],
        input_output_aliases={0: 2, 1: 3},
        compiler_params=pltpu.CompilerParams(has_side_effects=_SIDE),
    )(hd["src"], hd["land"], after)
    return dict(hd, send=res[0], recv=res[1], src=res[2], land=res[3]), res[4]


def _win_gather_forward(name, hd, ks, after):
    def body(land, recv_sems, after_ref, land_thru, fsend_sems, frecv_sems):
        me, routes = _win_routes()
        sibling = routes[0][0]
        for n, k in enumerate(ks):
            pos, peer = routes[k]
            piece = _win_cols(land, peer)
            pltpu.make_async_remote_copy(src_ref=piece, dst_ref=piece, send_sem=fsend_sems.at[n],
                                         recv_sem=recv_sems.at[k - 1], device_id=pos,
                                         device_id_type=pl.DeviceIdType.MESH).wait_recv()
            pltpu.make_async_remote_copy(src_ref=piece, dst_ref=piece, send_sem=fsend_sems.at[n],
                                         recv_sem=frecv_sems.at[n], device_id=sibling,
                                         device_id_type=pl.DeviceIdType.MESH).start()

    sems = pltpu.SemaphoreType.DMA((len(ks),))
    res = pl.pallas_call(
        body, name="win_gather_forward_" + name,
        out_shape=[pltpu.HBM(hd["land"].shape, BF16), sems, sems],
        in_specs=[_HBM, _SEM, pl.BlockSpec(memory_space=pl.ANY)],
        out_specs=[_HBM, _SEM, _SEM],
        input_output_aliases={0: 0},
        compiler_params=pltpu.CompilerParams(has_side_effects=_SIDE),
    )(hd["land"], hd["recv"], after)
    return dict(hd, land=res[0], **{"fsend" + name: res[1], "frecv" + name: res[2]})


def _win_gather_forwarded(name, hd, ks):
    def body(land, fsend_sems, frecv_sems, land_thru):
        me, routes = _win_routes()
        sib_c = routes[0][0][2]
        for n, k in enumerate(ks):
            _wait_bytes(_win_cols(land, routes[k][1]), fsend_sems.at[n])
            _wait_bytes(_win_cols(land, 4 * routes[k][0][0] + 2 * routes[k][0][1] + sib_c), frecv_sems.at[n])

    res = pl.pallas_call(
        body, name="win_gather_forwarded_" + name,
        out_shape=[pltpu.HBM(hd["land"].shape, BF16)],
        in_specs=[_HBM, _SEM, _SEM],
        out_specs=[_HBM],
        input_output_aliases={0: 0},
        compiler_params=pltpu.CompilerParams(has_side_effects=_SIDE),
    )(hd["land"], hd["fsend" + name], hd["frecv" + name])
    return dict(hd, land=res[0])


def _win_gather_early(hd):
    def body(src, land, recv_sem, local_sem, src_thru, land_thru):
        me, routes = _win_routes()
        _wait_bytes(_win_cols(land, routes[0][1]), recv_sem)
        pltpu.make_async_copy(src, _win_cols(land, me), local_sem).wait()

    res = pl.pallas_call(
        body, name="win_gather_early",
        out_shape=[pltpu.HBM(hd["src"].shape, BF16), pltpu.HBM(hd["land"].shape, BF16)],
        in_specs=[_HBM, _HBM, _SEM, _SEM],
        out_specs=[_HBM, _HBM],
        input_output_aliases={0: 0, 1: 1},
        compiler_params=pltpu.CompilerParams(has_side_effects=_SIDE),
    )(hd["src"], hd["land"], hd["recv0"], hd["local"])
    return dict(hd, src=res[0], land=res[1])


def _win_gather_wait(hd):
    def body(src, land, send0_sem, send_sems, src_thru, land_thru):
        for k in range(4):
            _wait_bytes(src, send0_sem if k == 0 else send_sems.at[k - 1])

    res = pl.pallas_call(
        body, name="win_gather_wait",
        out_shape=[pltpu.HBM(hd["src"].shape, BF16), pltpu.HBM(hd["land"].shape, BF16)],
        in_specs=[_HBM, _HBM, _SEM, _SEM],
        out_specs=[_HBM, _HBM],
        input_output_aliases={0: 0, 1: 1},
        compiler_params=pltpu.CompilerParams(has_side_effects=_SIDE),
    )(hd["src"], hd["land"], hd["send0"], hd["send"])
    return res[1]


def _whole(ref, l):
    return ref


def _slot(ref, l):
    return ref.at[l]


def _cols(width):
    def at(ref, l):
        return ref.at[:, pl.ds(pl.multiple_of(l * width, 128), width)]
    return at


def _rows(height):
    def at(ref, l):
        return ref.at[pl.ds(pl.multiple_of(l * height, 8), height), :]
    return at


NTILE = TP // TM


def _tile_rows(t):
    lo = max(t * TM - NMETA, 0)
    hi = min((t + 1) * TM - NMETA, SEQ)
    return lo, hi - lo, lo + NMETA - t * TM


def _for_tile(t, fn):
    for static_t in range(NTILE):
        pl.when(t == static_t)(functools.partial(fn, static_t))


def _token_tile_copy(hbm_ref, buf, sem, t):
    lo, n, off = _tile_rows(t)
    return pltpu.make_async_copy(hbm_ref.at[pl.ds(lo, n)], buf.at[pl.ds(off, n)], sem)


def _prenorm(x, meta_full, pre_w):
    def body(x_ref, meta_ref, pw_ref, h_ref, hn_ref, xbuf, sems):
        i = pl.program_id(0)
        slot = i % 2

        def start(t):
            _token_tile_copy(x_ref, xbuf.at[t % 2], sems.at[t % 2], t).start()

        @pl.when(i == 0)
        def _():
            start(0)
        _for_tile(i + 1, start)
        _for_tile(i, lambda t: _token_tile_copy(x_ref, xbuf.at[t % 2], sems.at[t % 2], t).wait())

        @pl.when(i == 0)
        def _():
            xbuf[0, 0:NMETA, :] = meta_ref[...]

        @pl.when(i == NTILE - 1)
        def _():
            last = _tile_rows(NTILE - 1)[1]
            xbuf[(NTILE - 1) % 2, last:TM, :] = jnp.zeros((TM - last, D), F32)

        pw = pw_ref[...]

        def chunk(ci, carry):
            r0 = pl.multiple_of(ci * R, R)
            xv = xbuf[slot, pl.ds(r0, R), :]
            h_ref[pl.ds(r0, R), :] = xv
            ms = jnp.mean(xv * xv, axis=-1, keepdims=True)
            hn_ref[pl.ds(r0, R), :] = (xv * lax.rsqrt(ms + EPS) * pw).astype(BF16)
            return carry
        lax.fori_loop(0, TM // R, chunk, 0, unroll=2)

    row = pl.BlockSpec((TM, D), lambda i: (i, 0))
    return pl.pallas_call(
        body, name="prenorm",
        grid=(NTILE,),
        in_specs=[pl.BlockSpec(memory_space=pl.ANY), pl.BlockSpec((NMETA, D), lambda i: (0, 0)),
                  pl.BlockSpec((1, D), lambda i: (0, 0))],
        out_specs=[row, row],
        out_shape=[jax.ShapeDtypeStruct((TP, D), F32), jax.ShapeDtypeStruct((TP, D), BF16)],
        scratch_shapes=[pltpu.VMEM((2, TM, D), F32), pltpu.SemaphoreType.DMA((2,))],
        compiler_params=_cparams(),
    )(x, meta_full, pre_w)


def _inproj_cols(name, shards, hn, w_land, b_in, z_prev):
    nsh = shards.shape[0]
    one_shard = w_land.shape[1] == _WCOLS

    def body(idx_ref, hn_ref, w_ref, b_ref, *rest):
        z_ref = rest[-2]
        z_ref[...] = jnp.dot(hn_ref[...], w_ref[...], preferred_element_type=F32) + b_ref[...]

    any_spec = pl.BlockSpec(memory_space=pl.ANY)
    in_specs = [pl.BlockSpec((TM, D), lambda j, i, idx: (i, 0)),
                pl.BlockSpec((D, _WCOLS), lambda j, i, idx: (0, 0 if one_shard else idx[j])),
                pl.BlockSpec((1, _WCOLS), lambda j, i, idx: (0, idx[j]))]
    operands = [hn, w_land, b_in]
    aliases = {2: 1}
    if z_prev is not None:
        in_specs.append(any_spec)
        operands.append(z_prev)
        aliases[4] = 0
    return pl.pallas_call(
        body, name=name,
        grid_spec=pltpu.PrefetchScalarGridSpec(
            num_scalar_prefetch=1, grid=(nsh, TP // TM), in_specs=in_specs,
            out_specs=[pl.BlockSpec((TM, _WCOLS), lambda j, i, idx: (i, idx[j])), any_spec]),
        out_shape=[jax.ShapeDtypeStruct((TP, NIN), F32), jax.ShapeDtypeStruct(w_land.shape, w_land.dtype)],
        input_output_aliases=aliases,
        compiler_params=_cparams(),
    )(shards, *operands)


def _gate_values(ga, gx, xc, sp8):
    r = _sig(ga)
    i = _sig(gx)
    log_a = -(r * sp8)
    a = jnp.exp(log_a)
    mult = jnp.sqrt(-_expm1_neg(2.0 * log_a))
    return r, i, a, mult


def _lru_fwd(z, conv_w, conv_b, wa_g, b_a, wx_g, b_x, lam):
    def body(x_ref, g_ref, cw_ref, cb_ref, wa_ref, ba_ref, wx_ref, bx_ref, lam_ref,
             y_ref, xc_ref, hs_ref, ga_s, gx_s):
        taps = [cw_ref[k:k + 1, :] for k in range(LW)]
        cb = cb_ref[...]

        def conv_chunk(ci, carry):
            r0 = pl.multiple_of(ci * RL, RL)
            cur = x_ref[pl.ds(r0, RL), :]
            p0 = pl.multiple_of(jnp.maximum(r0 - 8, 0), 8)
            prev = jnp.where(ci > 0, x_ref[pl.ds(p0, 8), :], 0.0)
            buf = jnp.concatenate([prev, cur], axis=0)
            acc = cur * taps[LW - 1] + cb
            for s in range(1, LW):
                acc = acc + pltpu.roll(buf, s, 0)[8:8 + RL, :] * taps[LW - 1 - s]
            xc_ref[pl.ds(r0, RL), :] = acc
            return carry
        lax.fori_loop(0, TP // RL, conv_chunk, 0)

        def gate_chunk(ci, carry):
            r0 = pl.multiple_of(ci * TM, TM)
            xb = xc_ref[pl.ds(r0, TM), :].astype(BF16)
            ga_s[pl.ds(r0, TM), :] = jnp.dot(xb, wa_ref[...], preferred_element_type=F32) + ba_ref[...]
            gx_s[pl.ds(r0, TM), :] = jnp.dot(xb, wx_ref[...], preferred_element_type=F32) + bx_ref[...]
            return carry
        lax.fori_loop(0, TP // TM, gate_chunk, 0)

        sp8 = LRU_C * _softplus(-lam_ref[...])
        row = _row_iota((R, CB))

        def scan_chunk(ci, hprev):
            r0 = pl.multiple_of(ci * R, R)
            xc = xc_ref[pl.ds(r0, R), :]
            _, i, a, mult = _gate_values(ga_s[pl.ds(r0, R), :], gx_s[pl.ds(r0, R), :], xc, sp8)
            u = mult * (i * xc)
            k = 1
            while k < R:
                m = row >= k
                u = jnp.where(m, a * pltpu.roll(u, k, 0) + u, u)
                a = jnp.where(m, a * pltpu.roll(a, k, 0), a)
                k *= 2
            hv = u + a * hprev
            hs_ref[pl.ds(r0, R), :] = hv
            g = g_ref[pl.ds(r0, R), :]
            y_ref[pl.ds(r0, R), :] = (hv * (g * _sig(g))).astype(BF16)
            return jnp.sum(jnp.where(row == R - 1, hv, 0.0), axis=0, keepdims=True)
        def scan_pass(i, hp):
            for sub in range(4):
                hp = scan_chunk(4 * i + sub, hp)
            return hp
        lax.fori_loop(0, TP // R // 4, scan_pass, jnp.zeros((1, CB), F32))

    col = lambda off: pl.BlockSpec((TP, CB), lambda j: (0, off + j))
    vec = pl.BlockSpec((1, CB), lambda j: (0, j))
    wsp = pl.BlockSpec((None, CB, CB), lambda j: (j, 0, 0))
    return pl.pallas_call(
        body, name="lru_fwd",
        grid=(NCB,),
        in_specs=[col(0), col(NCB), pl.BlockSpec((LW, CB), lambda j: (0, j)), vec, wsp, vec, wsp, vec, vec],
        out_specs=[col(0), col(0), col(0)],
        out_shape=[jax.ShapeDtypeStruct((TP, DL), BF16), jax.ShapeDtypeStruct((TP, DL), F32),
                   jax.ShapeDtypeStruct((TP, DL), F32)],
        scratch_shapes=[pltpu.VMEM((TP, CB), F32), pltpu.VMEM((TP, CB), F32)],
        compiler_params=_cparams(),
    )(z, z, conv_w, conv_b, wa_g, b_a, wx_g, b_x, lam)


CBC = 128
NCBC = DC // CBC
RC = 128


def _fold_rows(v):
    acc = v[0:8, :]
    for r in range(8, v.shape[0], 8):
        acc = acc + v[r:r + 8, :]
    return acc


def _conf_fwd_conv(z, dw_w, dw_b):
    def body(u1_ref, u2_ref, w_ref, b_ref, vc_ref, vs):
        vs[pl.ds(0, KWP), :] = jnp.zeros((KWP, CBC), F32)

        def glu_chunk(ci, carry):
            r0 = pl.multiple_of(ci * RC, RC)
            vs[pl.ds(KWP + r0, RC), :] = u1_ref[pl.ds(r0, RC), :] * _sig(u2_ref[pl.ds(r0, RC), :])
            return carry
        lax.fori_loop(0, TP // RC, glu_chunk, 0)

        bias = b_ref[...]

        def conv_chunk(ci, carry):
            r0 = pl.multiple_of(ci * RC, RC)
            buf = vs[pl.ds(r0, KWP + RC), :]
            acc = jnp.zeros((RC, CBC), F32) + bias
            for rr in range(8):
                rolled = buf if rr == 0 else pltpu.roll(buf, rr, 0)
                for q in range(4):
                    s = 8 * q + rr
                    if s > KW - 1:
                        continue
                    k = KW - 1 - s
                    acc = acc + rolled[KWP - 8 * q:KWP - 8 * q + RC, :] * w_ref[k:k + 1, :]
            vc_ref[pl.ds(r0, RC), :] = acc
            return carry
        lax.fori_loop(0, TP // RC, conv_chunk, 0)

    return pl.pallas_call(
        body, name="conf_fwd_conv",
        grid=(NCBC,),
        in_specs=[pl.BlockSpec((TP, CBC), lambda j: (0, 2 * NCBC + j)),
                  pl.BlockSpec((TP, CBC), lambda j: (0, 3 * NCBC + j)),
                  pl.BlockSpec((KWP, CBC), lambda j: (0, j)),
                  pl.BlockSpec((1, CBC), lambda j: (0, j))],
        out_specs=pl.BlockSpec((TP, CBC), lambda j: (0, j)),
        out_shape=jax.ShapeDtypeStruct((TP, DC), F32),
        scratch_shapes=[pltpu.VMEM((TP + KWP, CBC), F32)],
        compiler_params=_cparams(),
    )(z, z, dw_w, dw_b)


def _ln_chunk(vc, lw, lb):
    mu = jnp.mean(vc, axis=-1, keepdims=True)
    xm = vc - mu
    var = jnp.mean(xm * xm, axis=-1, keepdims=True)
    rstd = lax.rsqrt(var + EPS)
    xhat = xm * rstd
    return xhat, rstd, xhat * lw + lb


def _conf_fwd_proj(vc, z, ln_w, ln_b, pw_w, pw_b):
    def body(vc_ref, g_ref, lw_ref, lb_ref, w_ref, b_ref, y_ref, p_ref, xhat_ref, rstd_ref, s_s):
        lw, lb = lw_ref[...], lb_ref[...]

        def ln_chunk(ci, carry):
            r0 = pl.multiple_of(ci * R, R)
            for half in range(2):
                rr = r0 + 8 * half
                xhat, rstd, ln = _ln_chunk(vc_ref[pl.ds(rr, 8), :], lw, lb)
                xhat_ref[pl.ds(rr, 8), :] = xhat
                rstd_ref[pl.ds(rr, 8), :] = jnp.broadcast_to(rstd, (8, 128))
                p_ref[pl.ds(rr, 8), :] = ln * _sig(ln)
            s_s[pl.ds(r0, R), :] = p_ref[pl.ds(r0, R), :].astype(BF16)
            return carry
        lax.fori_loop(0, TM // R, ln_chunk, 0, unroll=2)

        p_ref[...] = jnp.dot(s_s[...], w_ref[...], preferred_element_type=F32) + b_ref[...]

        def out_chunk(ci, carry):
            r0 = pl.multiple_of(ci * R, R)
            g = g_ref[pl.ds(r0, R), :]
            y_ref[pl.ds(r0, R), :] = (p_ref[pl.ds(r0, R), :] * (g * _sig(g))).astype(BF16)
            return carry
        lax.fori_loop(0, TM // R, out_chunk, 0)

    row = pl.BlockSpec((TM, DC), lambda i: (i, 0))
    vec = pl.BlockSpec((1, DC), lambda i: (0, 0))
    return pl.pallas_call(
        body, name="conf_fwd_proj",
        grid=(TP // TM,),
        in_specs=[row, pl.BlockSpec((TM, DC), lambda i: (i, 4)), vec, vec,
                  pl.BlockSpec((DC, DC), lambda i: (0, 0)), vec],
        out_specs=[row, row, row, pl.BlockSpec((TM, 128), lambda i: (i, 0))],
        out_shape=[jax.ShapeDtypeStruct((TP, DC), BF16), jax.ShapeDtypeStruct((TP, DC), F32),
                   jax.ShapeDtypeStruct((TP, DC), F32), jax.ShapeDtypeStruct((TP, 128), F32)],
        scratch_shapes=[pltpu.VMEM((TM, DC), BF16)],
        compiler_params=_cparams(),
    )(vc, z, ln_w, ln_b, pw_w, pw_b)


def _outproj_loss(ylru, yconf, w_out, h, target, post_w):
    def body(yl_ref, yc_ref, w_ref, h_ref, tgt_hbm, pw_ref, dout_ref, dy_ref, loss_ref, dpw_ref, y_s, t_ref, sem):
        i = pl.program_id(0)
        k = pl.program_id(1)

        @pl.when(k == 0)
        def _():
            _for_tile(i, lambda t: _token_tile_copy(tgt_hbm, t_ref, sem, t).start())
            y_s[...] = jnp.dot(yl_ref[...], w_ref[...], preferred_element_type=F32)

        @pl.when(k == 1)
        def _():
            y_s[...] += jnp.dot(yc_ref[...], w_ref[...], preferred_element_type=F32)

        @pl.when(jnp.logical_and(i == 0, k == 1))
        def _():
            loss_ref[...] = jnp.zeros_like(loss_ref)
            dpw_ref[...] = jnp.zeros_like(dpw_ref)

        @pl.when(k == 1)
        def _():
            _for_tile(i, lambda t: _token_tile_copy(tgt_hbm, t_ref, sem, t).wait())

            @pl.when(i == 0)
            def _():
                t_ref[0:NMETA, :] = jnp.zeros((NMETA, D), F32)

            @pl.when(i == NTILE - 1)
            def _():
                last = _tile_rows(NTILE - 1)[1]
                t_ref[last:TM, :] = jnp.zeros((TM - last, D), F32)

            pw = pw_ref[...]
            row = _row_iota((8, D))

            def chunk(ci, carry):
                r0 = pl.multiple_of(ci * 8, 8)
                yv = y_s[pl.ds(r0, 8), :]
                rs = lax.rsqrt(jnp.mean(yv * yv, axis=-1, keepdims=True) + EPS)
                grow = row + (i * TM + r0)
                valid = jnp.logical_and(grow >= NMETA, grow < T)
                yn = yv * rs
                err = jnp.where(valid, h_ref[pl.ds(r0, 8), :] + yn * pw - t_ref[pl.ds(r0, 8), :], 0.0)
                loss_ref[...] += err * err
                d_rn = err * (1.0 / D)
                dout_ref[pl.ds(r0, 8), :] = d_rn
                dpw_ref[...] += d_rn * yn
                gw = d_rn * pw
                dot = jnp.mean(gw * yv, axis=-1, keepdims=True)
                dy_ref[pl.ds(r0, 8), :] = (rs * gw - yv * (rs * rs * rs * dot)).astype(BF16)
                return carry
            lax.fori_loop(0, TM // 8, chunk, 0, unroll=4)

    row = pl.BlockSpec((TM, D), lambda i, k: (i, 0))
    half = pl.BlockSpec((TM, DL), lambda i, k: (i, 0))
    acc = pl.BlockSpec((8, D), lambda i, k: (0, 0))
    return pl.pallas_call(
        body, name="outproj_loss",
        grid=(TP // TM, 2),
        in_specs=[half, half, pl.BlockSpec((DL, D), lambda i, k: (k, 0)), row, pl.BlockSpec(memory_space=pl.ANY),
                  pl.BlockSpec((1, D), lambda i, k: (0, 0))],
        out_specs=[row, row, acc, acc],
        out_shape=[jax.ShapeDtypeStruct((TP, D), F32), jax.ShapeDtypeStruct((TP, D), BF16),
                   jax.ShapeDtypeStruct((8, D), F32), jax.ShapeDtypeStruct((8, D), F32)],
        scratch_shapes=[pltpu.VMEM((TM, D), F32), pltpu.VMEM((TM, D), F32), pltpu.SemaphoreType.DMA(())],
        compiler_params=_cparams(),
    )(ylru, yconf, w_out, h, target, post_w)


_NT = (((1,), (1,)), ((), ()))
_TN = (((0,), (0,)), ((), ()))


def _outproj_bwd(dy, ylru, yconf, w_out):
    def body(dy_ref, yl_ref, yc_ref, w_ref, dycat_ref, dw_ref):
        j = pl.program_id(0)
        dyv = dy_ref[...]
        dycat_ref[...] = lax.dot_general(dyv, w_ref[...], _NT, preferred_element_type=F32)

        @pl.when(j < NCB)
        def _():
            dw_ref[...] = lax.dot_general(yl_ref[...], dyv, _TN, preferred_element_type=F32).astype(BF16)

        @pl.when(j >= NCB)
        def _():
            dw_ref[...] = lax.dot_general(yc_ref[...], dyv, _TN, preferred_element_type=F32).astype(BF16)

    return pl.pallas_call(
        body, name="outproj_bwd",
        grid=(2 * NCB,),
        in_specs=[pl.BlockSpec((TP, D), lambda j: (0, 0)),
                  pl.BlockSpec((TP, CB), lambda j: (0, jnp.minimum(j, NCB - 1))),
                  pl.BlockSpec((TP, CB), lambda j: (0, jnp.maximum(j - NCB, 0))),
                  pl.BlockSpec((CB, D), lambda j: (j, 0))],
        out_specs=[pl.BlockSpec((TP, CB), lambda j: (0, j)), pl.BlockSpec((CB, D), lambda j: (j, 0))],
        out_shape=[jax.ShapeDtypeStruct((TP, D), F32), jax.ShapeDtypeStruct((D, D), BF16)],
        compiler_params=_cparams(),
    )(dy, ylru, yconf, w_out)


_AFTER = pl.BlockSpec(memory_space=pl.ANY)


def _conf_bwd_proj(dycat, p, z, xhat, rstd, hs, ln_w, ln_b, pw_w, after):
    def body(dy_ref, p_ref, g_ref, xhat_ref, rstd_ref, dyl_ref, hs_ref, gl_ref, lw_ref, lb_ref, w_ref, after_ref,
             dvc_ref, dz_ref, dpw_ref, vecs_ref, dp_s, s_s, ds_s):
        i = pl.program_id(0)
        lw, lb = lw_ref[...], lb_ref[...]

        @pl.when(i == 0)
        def _():
            dpw_ref[...] = jnp.zeros_like(dpw_ref)
            vecs_ref[...] = jnp.zeros_like(vecs_ref)

        def pre_chunk(ci, carry):
            r0 = pl.multiple_of(ci * R, R)
            for half in range(2):
                rr = r0 + 8 * half
                dyv = dy_ref[pl.ds(rr, 8), :]
                g = g_ref[pl.ds(rr, 8), :]
                sg = _sig(g)
                dp = dyv * (g * sg)
                dg = dyv * p_ref[pl.ds(rr, 8), :] * (sg * (1.0 + g * (1.0 - sg)))
                vecs_ref[0:8, :] += dp
                vecs_ref[8:16, :] += dg
                ds_s[pl.ds(rr, 8), :] = dp
                dvc_ref[pl.ds(rr, 8), :] = dg
            dp_s[pl.ds(r0, R), :] = ds_s[pl.ds(r0, R), :].astype(BF16)
            dz_ref[0, pl.ds(r0, R), :] = dvc_ref[pl.ds(r0, R), :].astype(BF16)
            for half in range(2):
                rr = r0 + 8 * half
                gl = gl_ref[pl.ds(rr, 8), :]
                sgl = _sig(gl)
                dgl = dyl_ref[pl.ds(rr, 8), :] * hs_ref[pl.ds(rr, 8), :] * (sgl * (1.0 + gl * (1.0 - sgl)))
                vecs_ref[32:40, :] += dgl
                dvc_ref[pl.ds(rr, 8), :] = dgl
            dz_ref[1, pl.ds(r0, R), :] = dvc_ref[pl.ds(r0, R), :].astype(BF16)
            for half in range(2):
                rr = r0 + 8 * half
                ln = xhat_ref[pl.ds(rr, 8), :] * lw + lb
                ds_s[pl.ds(rr, 8), :] = ln * _sig(ln)
            s_s[pl.ds(r0, R), :] = ds_s[pl.ds(r0, R), :].astype(BF16)
            return carry
        lax.fori_loop(0, TM // R, pre_chunk, 0, unroll=2)

        dpb = dp_s[...]
        ds_s[...] = lax.dot_general(dpb, w_ref[...], _NT, preferred_element_type=F32)
        dpw_ref[...] += lax.dot_general(s_s[...], dpb, _TN, preferred_element_type=F32)

        def post_chunk(ci, carry):
            r0 = pl.multiple_of(ci * 8, 8)
            xhat = xhat_ref[pl.ds(r0, 8), :]
            rstd = jnp.tile(rstd_ref[pl.ds(r0, 8), :], (1, DC // 128))
            ln = xhat * lw + lb
            sl = _sig(ln)
            dln = ds_s[pl.ds(r0, 8), :] * (sl * (1.0 + ln * (1.0 - sl)))
            vecs_ref[16:24, :] += dln * xhat
            vecs_ref[24:32, :] += dln
            dxh = dln * lw
            m1 = jnp.mean(dxh, axis=-1, keepdims=True)
            m2 = jnp.mean(dxh * xhat, axis=-1, keepdims=True)
            dvc_ref[pl.ds(r0, 8), :] = rstd * (dxh - m1 - xhat * m2)
            return carry
        lax.fori_loop(0, TM // 8, post_chunk, 0, unroll=4)

    row = pl.BlockSpec((TM, DC), lambda i: (i, 0))
    vec = pl.BlockSpec((1, DC), lambda i: (0, 0))
    return pl.pallas_call(
        body, name="conf_bwd_proj",
        grid=(TP // TM,),
        in_specs=[pl.BlockSpec((TM, DC), lambda i: (i, 1)), row, pl.BlockSpec((TM, DC), lambda i: (i, 4)), row,
                  pl.BlockSpec((TM, 128), lambda i: (i, 0)),
                  pl.BlockSpec((TM, DL), lambda i: (i, 0)), row, pl.BlockSpec((TM, DL), lambda i: (i, 1)),
                  vec, vec, pl.BlockSpec((DC, DC), lambda i: (0, 0)), _AFTER],
        out_specs=[row, pl.BlockSpec((2, TM, DC), lambda i: (0, i, 0)), pl.BlockSpec((DC, DC), lambda i: (0, 0)),
                   pl.BlockSpec((40, DC), lambda i: (0, 0))],
        out_shape=[jax.ShapeDtypeStruct((TP, DC), F32), jax.ShapeDtypeStruct((2, TP, DC), BF16),
                   jax.ShapeDtypeStruct((DC, DC), F32), jax.ShapeDtypeStruct((40, DC), F32)],
        scratch_shapes=[pltpu.VMEM((TM, DC), BF16), pltpu.VMEM((TM, DC), BF16), pltpu.VMEM((TM, DC), F32)],
        compiler_params=_cparams(),
    )(dycat, p, z, xhat, rstd, dycat, hs, z, ln_w, ln_b, pw_w, after)


def _conf_bwd_conv(dvc, z, dw_w, after):
    def body(dvc_ref, u1_ref, u2_ref, w_ref, after_ref, du_ref, dw_ref, vecs_ref, vs, dvs):
        vs[pl.ds(0, KWP), :] = jnp.zeros((KWP, CBC), F32)
        dvs[pl.ds(TP, KWP), :] = jnp.zeros((KWP, CBC), F32)
        dw_ref[...] = jnp.zeros_like(dw_ref)
        vecs_ref[...] = jnp.zeros_like(vecs_ref)

        def fill_chunk(ci, carry):
            r0 = pl.multiple_of(ci * RC, RC)
            vs[pl.ds(KWP + r0, RC), :] = u1_ref[pl.ds(r0, RC), :] * _sig(u2_ref[pl.ds(r0, RC), :])
            dv = dvc_ref[pl.ds(r0, RC), :]
            dvs[pl.ds(r0, RC), :] = dv
            vecs_ref[0:8, :] += _fold_rows(dv)
            return carry
        lax.fori_loop(0, TP // RC, fill_chunk, 0)

        def conv_chunk(ci, carry):
            r0 = pl.multiple_of(ci * RC, RC)
            vbuf = vs[pl.ds(r0, KWP + RC), :]
            dbuf = dvs[pl.ds(r0, KWP + RC), :]
            dcur = dbuf[0:RC, :]
            dv = jnp.zeros((RC, CBC), F32)
            for rr in range(8):
                vroll = vbuf if rr == 0 else pltpu.roll(vbuf, rr, 0)
                droll = dbuf if rr == 0 else pltpu.roll(dbuf, KWP + RC - rr, 0)
                for q in range(4):
                    s = 8 * q + rr
                    if s > KW - 1:
                        continue
                    k = KW - 1 - s
                    dv = dv + droll[8 * q:8 * q + RC, :] * w_ref[k:k + 1, :]
                    dw_ref[8 * k:8 * k + 8, :] += _fold_rows(dcur * vroll[KWP - 8 * q:KWP - 8 * q + RC, :])
            u1 = u1_ref[pl.ds(r0, RC), :]
            sg = _sig(u2_ref[pl.ds(r0, RC), :])
            du1 = dv * sg
            du2 = dv * u1 * (sg * (1.0 - sg))
            du_ref[0, pl.ds(r0, RC), :] = du1.astype(BF16)
            du_ref[1, pl.ds(r0, RC), :] = du2.astype(BF16)
            vecs_ref[8:16, :] += _fold_rows(du1)
            vecs_ref[16:24, :] += _fold_rows(du2)
            return carry
        lax.fori_loop(0, TP // RC, conv_chunk, 0)

    blk = pl.BlockSpec((TP, CBC), lambda j: (0, j))
    return pl.pallas_call(
        body, name="conf_bwd_conv",
        grid=(NCBC,),
        in_specs=[blk, pl.BlockSpec((TP, CBC), lambda j: (0, 2 * NCBC + j)),
                  pl.BlockSpec((TP, CBC), lambda j: (0, 3 * NCBC + j)), pl.BlockSpec((KWP, CBC), lambda j: (0, j)),
                  _AFTER],
        out_specs=[pl.BlockSpec((2, TP, CBC), lambda j: (0, 0, j)), pl.BlockSpec((8 * KWP, CBC), lambda j: (0, j)),
                   pl.BlockSpec((24, CBC), lambda j: (0, j))],
        out_shape=[jax.ShapeDtypeStruct((2, TP, DC), BF16),
                   jax.ShapeDtypeStruct((8 * KWP, DC), F32), jax.ShapeDtypeStruct((24, DC), F32)],
        scratch_shapes=[pltpu.VMEM((TP + KWP, CBC), F32), pltpu.VMEM((TP + KWP, CBC), F32)],
        compiler_params=_cparams(),
    )(dvc, z, z, dw_w, after)


def _lru_bwd(dycat, z, xc, hs, conv_w, wa_g, b_a, wx_g, b_x, lam, after):
    NV = 6

    def body(dy_ref, x_ref, g_ref, xc_ref, hs_ref, cw_ref, wa_ref, ba_ref, wx_ref, bx_ref, lam_ref, after_ref,
             dzl_ref, dwa_ref, dwx_ref, dcw_ref, vecs_ref, ga_s, gx_s, dxc_s):
        vecs_ref[...] = jnp.zeros_like(vecs_ref)
        dcw_ref[...] = jnp.zeros_like(dcw_ref)
        dxc_s[pl.ds(TP, 8), :] = jnp.zeros((8, CB), F32)

        def gate_chunk(ci, carry):
            r0 = pl.multiple_of(ci * TM, TM)
            xb = xc_ref[pl.ds(r0, TM), :].astype(BF16)
            ga_s[pl.ds(r0, TM), :] = jnp.dot(xb, wa_ref[...], preferred_element_type=F32) + ba_ref[...]
            gx_s[pl.ds(r0, TM), :] = jnp.dot(xb, wx_ref[...], preferred_element_type=F32) + bx_ref[...]
            return carry
        lax.fori_loop(0, TP // TM, gate_chunk, 0)

        sp8 = LRU_C * _softplus(-lam_ref[...])
        row = _row_iota((R, CB))
        nchunk = TP // R

        def scan_chunk(cj, carry):
            a_next, lam_next = carry
            ci = nchunk - 1 - cj
            r0 = pl.multiple_of(ci * R, R)
            dyv = dy_ref[pl.ds(r0, R), :]
            g = g_ref[pl.ds(r0, R), :]
            hv = hs_ref[pl.ds(r0, R), :]
            xc = xc_ref[pl.ds(r0, R), :]
            sg = _sig(g)
            dhs = dyv * (g * sg)
            r, i, a, mult = _gate_values(ga_s[pl.ds(r0, R), :], gx_s[pl.ds(r0, R), :], xc, sp8)
            b = jnp.where(row == R - 1, a_next, pltpu.roll(a, R - 1, 0))
            lv = dhs
            k = 1
            while k < R:
                m = row < R - k
                lv = jnp.where(m, lv + b * pltpu.roll(lv, R - k, 0), lv)
                b = jnp.where(m, b * pltpu.roll(b, R - k, 0), b)
                k *= 2
            lv = lv + b * lam_next
            p0 = pl.multiple_of(jnp.maximum(r0 - 8, 0), 8)
            hprev8 = jnp.where(ci > 0, hs_ref[pl.ds(p0, 8), :], 0.0)
            hprev = pltpu.roll(jnp.concatenate([hprev8, hv], axis=0), 1, 0)[8:8 + R, :]
            da = lv * hprev
            ixc = i * xc
            dmult = lv * ixc
            di = lv * mult * xc
            dxc_s[pl.ds(r0, R), :] = lv * mult * i
            a2 = a * a
            dlog_a = da * a - dmult * a2 / mult
            vecs_ref[32:40, :] += _fold8(dlog_a * r)
            dga = -(dlog_a * sp8) * r * (1.0 - r)
            dgx = di * i * (1.0 - i)
            ga_s[pl.ds(r0, R), :] = dga
            gx_s[pl.ds(r0, R), :] = dgx
            vecs_ref[16:24, :] += _fold8(dga)
            vecs_ref[24:32, :] += _fold8(dgx)
            a_first = jnp.sum(jnp.where(row == 0, a, 0.0), axis=0, keepdims=True)
            l_first = jnp.sum(jnp.where(row == 0, lv, 0.0), axis=0, keepdims=True)
            return a_first, l_first
        def scan_pass(i, cr):
            for sub in range(4):
                cr = scan_chunk(4 * i + sub, cr)
            return cr
        lax.fori_loop(0, nchunk // 4, scan_pass, (jnp.zeros((1, CB), F32), jnp.zeros((1, CB), F32)))

        dwa_ref[...] = jnp.zeros_like(dwa_ref)
        dwx_ref[...] = jnp.zeros_like(dwx_ref)

        def mm_chunk(ci, carry):
            r0 = pl.multiple_of(ci * TM, TM)
            xb = xc_ref[pl.ds(r0, TM), :].astype(BF16)
            dgab = ga_s[pl.ds(r0, TM), :].astype(BF16)
            dgxb = gx_s[pl.ds(r0, TM), :].astype(BF16)
            dxc_s[pl.ds(r0, TM), :] += (lax.dot_general(dgab, wa_ref[...], _NT, preferred_element_type=F32)
                                        + lax.dot_general(dgxb, wx_ref[...], _NT, preferred_element_type=F32))
            dwa_ref[...] += lax.dot_general(xb, dgab, _TN, preferred_element_type=F32)
            dwx_ref[...] += lax.dot_general(xb, dgxb, _TN, preferred_element_type=F32)
            return carry
        lax.fori_loop(0, TP // TM, mm_chunk, 0)

        taps = [cw_ref[k:k + 1, :] for k in range(LW)]

        def conv_chunk(ci, carry):
            r0 = pl.multiple_of(ci * RL, RL)
            dbuf = dxc_s[pl.ds(r0, RL + 8), :]
            dcur = dbuf[0:RL, :]
            p0 = pl.multiple_of(jnp.maximum(r0 - 8, 0), 8)
            xprev = jnp.where(ci > 0, x_ref[pl.ds(p0, 8), :], 0.0)
            xbuf = jnp.concatenate([xprev, x_ref[pl.ds(r0, RL), :]], axis=0)
            dxl = dcur * taps[LW - 1]
            dcw_ref[8 * (LW - 1):8 * LW, :] += _fold_rows(dcur * xbuf[8:8 + RL, :])
            for s in range(1, LW):
                k = LW - 1 - s
                dxl = dxl + pltpu.roll(dbuf, RL + 8 - s, 0)[0:RL, :] * taps[k]
                dcw_ref[8 * k:8 * k + 8, :] += _fold_rows(dcur * pltpu.roll(xbuf, s, 0)[8:8 + RL, :])
            dzl_ref[0, pl.ds(r0, RL), :] = dxl.astype(BF16)
            vecs_ref[8:16, :] += _fold_rows(dxl)
            vecs_ref[40:48, :] += _fold_rows(dcur)
            return carry
        lax.fori_loop(0, TP // RL, conv_chunk, 0)
        vecs_ref[32:40, :] = vecs_ref[32:40, :] * (LRU_C * _sig(-lam_ref[...]))

    col = lambda off: pl.BlockSpec((TP, CB), lambda j: (0, off + j))
    vec = pl.BlockSpec((1, CB), lambda j: (0, j))
    wsp = pl.BlockSpec((None, CB, CB), lambda j: (j, 0, 0))
    return pl.pallas_call(
        body, name="lru_bwd",
        grid=(NCB,),
        in_specs=[col(0), col(0), col(NCB), col(0), col(0), pl.BlockSpec((LW, CB), lambda j: (0, j)),
                  wsp, vec, wsp, vec, vec, _AFTER],
        out_specs=[pl.BlockSpec((1, TP, CB), lambda j: (0, 0, j)), wsp, wsp,
                   pl.BlockSpec((8 * LW, CB), lambda j: (0, j)), pl.BlockSpec((8 * NV, CB), lambda j: (0, j))],
        out_shape=[jax.ShapeDtypeStruct((1, TP, DL), BF16),
                   jax.ShapeDtypeStruct((NCB, CB, CB), F32), jax.ShapeDtypeStruct((NCB, CB, CB), F32),
                   jax.ShapeDtypeStruct((8 * LW, DL), F32), jax.ShapeDtypeStruct((8 * NV, DL), F32)],
        scratch_shapes=[pltpu.VMEM((TP, CB), F32), pltpu.VMEM((TP, CB), F32), pltpu.VMEM((TP + 8, CB), F32)],
        compiler_params=_cparams(),
    )(dycat, z, z, xc, hs, conv_w, wa_g, b_a, wx_g, b_x, lam, after)


def _dz_section(sec, dzl_ref, dz41_ref, dzc_ref, use):
    @pl.when(sec == 0)
    def _():
        use(dzl_ref)

    @pl.when(jnp.logical_or(sec == 1, sec == 4))
    def _():
        use(dz41_ref)

    @pl.when(jnp.logical_or(sec == 2, sec == 3))
    def _():
        use(dzc_ref)


def _dz_specs(rows, index):
    return [pl.BlockSpec((None, rows, 1024), lambda a, b: (0, index(a, b)[0], 0)),
            pl.BlockSpec((None, rows, 1024), lambda a, b: (jnp.where(index(a, b)[1] == 1, 1, 0), index(a, b)[0], 0)),
            pl.BlockSpec((None, rows, 1024), lambda a, b: (jnp.clip(index(a, b)[1] - 2, 0, 1), index(a, b)[0], 0))]


def _inproj_wgrad(name, hn, dzs, after):
    KB = 512
    nsec = dzs.shape[0]

    def body(hn_ref, dz_ref, after_ref, dw_ref):
        dw_ref[...] = lax.dot_general(hn_ref[...], dz_ref[...], _TN, preferred_element_type=F32).astype(BF16)

    return pl.pallas_call(
        body, name=name,
        grid=(nsec, D // KB),
        in_specs=[pl.BlockSpec((TP, KB), lambda n, kb: (0, kb)),
                  pl.BlockSpec((None, TP, 1024), lambda n, kb: (n, 0, 0)), _AFTER],
        out_specs=pl.BlockSpec((KB, 1024), lambda n, kb: (kb, n)),
        out_shape=jax.ShapeDtypeStruct((D, nsec * 1024), BF16),
        compiler_params=_cparams(),
    )(hn, dzs, after)


def _sum_win_parts(parts_a, parts_b, parts_c):
    RB = 64

    def body(a_ref, b_ref, c_ref, o_ref):
        def chunk(ci, carry):
            r0 = pl.multiple_of(ci * R, R)
            for ref, src, base, ncol in ((a_ref, 0, 0, 1024), (c_ref, 1024, 1024, 1024), (b_ref, 0, 2048, 2048),
                                         (c_ref, 0, 4096, 1024)):
                for c0 in range(0, ncol, 512):
                    acc = ref[0, pl.ds(r0, R), src + c0:src + c0 + 512].astype(F32)
                    for sidx in range(1, NDEV):
                        acc = acc + ref[sidx, pl.ds(r0, R), src + c0:src + c0 + 512].astype(F32)
                    o_ref[pl.ds(r0, R), base + c0:base + c0 + 512] = acc.astype(BF16)
            return carry
        lax.fori_loop(0, RB // R, chunk, 0)

    spec = lambda ncol: pl.BlockSpec((NDEV, RB, ncol), lambda i: (0, i, 0))
    return pl.pallas_call(
        body, name="sum_win_parts",
        grid=(D // NDEV // RB,),
        in_specs=[spec(1024), spec(2048), spec(2048)],
        out_specs=pl.BlockSpec((RB, NIN), lambda i: (i, 0)),
        out_shape=jax.ShapeDtypeStruct((D // NDEV, NIN), BF16),
        compiler_params=_cparams(),
    )(parts_a, parts_b, parts_c)


def _inproj_bwd(dzl, dz41, dzc, w_in, h, dout, pre_w, after):
    nsec = NIN // 1024

    def body(dzl_ref, dz41_ref, dzc_ref, w_ref, h_ref, dout_ref, pw_ref, after_ref, gx_hbm, dmeta_ref, dpw_ref,
             acc_s, dh_s, sem):
        i = pl.program_id(0)
        s = pl.program_id(1)

        def gx_copy(t):
            lo, n, off = _tile_rows(t)
            return pltpu.make_async_copy(dh_s.at[pl.ds(off, n)], gx_hbm.at[pl.ds(lo, n)], sem)

        @pl.when(s == 0)
        def _():
            acc_s[...] = jnp.zeros_like(acc_s)

        def use(dz_ref):
            acc_s[...] += lax.dot_general(dz_ref[...], w_ref[...], _NT, preferred_element_type=F32)
        _dz_section(s, dzl_ref, dz41_ref, dzc_ref, use)

        @pl.when(jnp.logical_and(i == 0, s == nsec - 1))
        def _():
            dpw_ref[...] = jnp.zeros_like(dpw_ref)

        @pl.when(s == nsec - 1)
        def _():
            _for_tile(i - 1, lambda t: gx_copy(t).wait())
            pw = pw_ref[...]

            def chunk(ci, carry):
                r0 = pl.multiple_of(ci * 8, 8)
                hv = h_ref[pl.ds(r0, 8), :]
                dhn = acc_s[pl.ds(r0, 8), :]
                rs = lax.rsqrt(jnp.mean(hv * hv, axis=-1, keepdims=True) + EPS)
                dpw_ref[...] += dhn * (hv * rs)
                gw = dhn * pw
                dot = jnp.mean(gw * hv, axis=-1, keepdims=True)
                dh_s[pl.ds(r0, 8), :] = rs * gw - hv * (rs * rs * rs * dot) + dout_ref[pl.ds(r0, 8), :]
                return carry
            lax.fori_loop(0, TM // 8, chunk, 0, unroll=4)
            _for_tile(i, lambda t: gx_copy(t).start())

            @pl.when(i == 0)
            def _():
                dmeta_ref[...] = dh_s[0:NMETA, :]

            @pl.when(i == NTILE - 1)
            def _():
                gx_copy(NTILE - 1).wait()

    row = pl.BlockSpec((TM, D), lambda i, s: (i, 0))
    return pl.pallas_call(
        body, name="inproj_bwd",
        grid=(TP // TM, nsec),
        in_specs=_dz_specs(TM, lambda i, s: (i, s)) + [
            pl.BlockSpec((D, 1024), lambda i, s: (0, s)), row, row, pl.BlockSpec((1, D), lambda i, s: (0, 0)),
            _AFTER],
        out_specs=[pl.BlockSpec(memory_space=pl.ANY), pl.BlockSpec((NMETA, D), lambda i, s: (0, 0)),
                   pl.BlockSpec((8, D), lambda i, s: (0, 0))],
        out_shape=[jax.ShapeDtypeStruct((SEQ, D), F32), jax.ShapeDtypeStruct((NMETA, D), F32),
                   jax.ShapeDtypeStruct((8, D), F32)],
        scratch_shapes=[pltpu.VMEM((TM, D), F32), pltpu.VMEM((TM, D), F32), pltpu.SemaphoreType.DMA(())],
        compiler_params=_cparams(),
    )(dzl, dz41, dzc, w_in, h, dout, pre_w, after)


def _adamw(name, parts, w, m, v, block_rows):
    rows, cols = w.shape
    nparts = parts.shape[0]
    cw = cols if cols <= 640 else 512

    def body(p_ref, w_ref, m_ref, v_ref, g_ref, d_ref, nm_ref, nv_ref):
        def chunk(ci, carry):
            r0 = pl.multiple_of(ci * R, R)
            for c0 in range(0, cols, cw):
                at = (pl.ds(r0, R), slice(c0, c0 + cw))
                g = p_ref[(0,) + at].astype(F32)
                for sidx in range(1, nparts):
                    g = g + p_ref[(sidx,) + at].astype(F32)
                delta, mv, vv = _adam_math(g, w_ref[at], m_ref[at], v_ref[at])
                g_ref[at] = g
                nm_ref[at] = mv
                nv_ref[at] = vv
                d_ref[at] = delta
            return carry
        lax.fori_loop(0, block_rows // R, chunk, 0)

    blk = pl.BlockSpec((block_rows, cols), lambda i: (i, 0))
    shp = jax.ShapeDtypeStruct((rows, cols), F32)
    return pl.pallas_call(
        body, name=name,
        grid=(rows // block_rows,),
        in_specs=[pl.BlockSpec((nparts, block_rows, cols), lambda i: (0, i, 0)), blk, blk, blk],
        out_specs=[blk, blk, blk, blk],
        out_shape=[shp, shp, shp, shp],
        compiler_params=_cparams(),
    )(parts, w, m, v)


def _adam_math(g, w, m, v):
    c1 = 1.0 / (1.0 - ADAM_B1 ** ADAM_STEP)
    c2 = 1.0 / (1.0 - ADAM_B2 ** ADAM_STEP)
    mv = ADAM_B1 * m + (1.0 - ADAM_B1) * g
    vv = ADAM_B2 * v + (1.0 - ADAM_B2) * (g * g)
    upd = (mv * c1) / (jnp.sqrt(vv * c2) + ADAM_EPS) + ADAM_WD * w
    return -ADAM_LR * upd, mv, vv


_VEC = [("pre_norm_w", 2), ("post_norm_w", 2), ("b_in", 5), ("lru_conv_b", 1), ("b_gate_a", 1), ("b_gate_x", 1),
        ("lru_lambda", 1), ("conf_dw_b", 1), ("conf_ln_w", 1), ("conf_ln_b", 1), ("conf_pw_b", 1)]
_VEC_ROWS = 24
_LOSS_ROW = 17
_SM_ROWS = 64


def _pack_grads(dprew_acc, dpostw_acc, cvecs, kvecs, lvecs, dcw_acc, ddw_acc, dh, loss_acc):
    def body(pre_ref, post_ref, c_ref, k_ref, l_ref, dcw_ref, ddw_ref, dh_ref, loss_ref, vec_ref, small_ref, tmp):
        s8 = lambda ref, r: jnp.sum(ref[8 * r:8 * r + 8, :], axis=0, keepdims=True)
        vec_ref[...] = jnp.zeros_like(vec_ref)
        pre, post = s8(pre_ref, 0), s8(post_ref, 0)
        rows = [pre[:, 0:1024], pre[:, 1024:2048], post[:, 0:1024], post[:, 1024:2048],
                s8(l_ref, 1), s8(c_ref, 4), s8(k_ref, 1), s8(k_ref, 2), s8(c_ref, 1),
                s8(l_ref, 5), s8(l_ref, 2), s8(l_ref, 3), s8(l_ref, 4),
                s8(k_ref, 0), s8(c_ref, 2), s8(c_ref, 3), s8(c_ref, 0)]
        for r, val in enumerate(rows):
            vec_ref[r:r + 1, :] = val
        vec_ref[_LOSS_ROW:_LOSS_ROW + 1, :] = jnp.zeros((1, 1024), F32) + (0.5 / D) * jnp.sum(loss_ref[...])

        small_ref[...] = jnp.zeros_like(small_ref)
        for k in range(LW):
            tmp[k:k + 1, :] = s8(dcw_ref, k)
        for k in range(KW):
            tmp[8 + k:9 + k, :] = s8(ddw_ref, k)
        for d in range(NDEV):
            small_ref[d, 0:LW, 0:128] = tmp[0:LW, 128 * d:128 * d + 128]
            small_ref[d, 8:8 + KW, 0:128] = tmp[8:8 + KW, 128 * d:128 * d + 128]
            small_ref[d, 40:56, :] = dh_ref[:, 256 * d:256 * d + 256]

    full = lambda a: pl.BlockSpec(a.shape, lambda i: (0,) * a.ndim)
    ins = [dprew_acc, dpostw_acc, cvecs, kvecs, lvecs, dcw_acc, ddw_acc]
    return pl.pallas_call(
        body, name="pack_grads",
        grid=(1,),
        in_specs=[full(a) for a in ins] + [full(dh), full(loss_acc)],
        out_specs=[pl.BlockSpec((_VEC_ROWS, 1024), lambda i: (0, 0)),
                   pl.BlockSpec((NDEV, _SM_ROWS, 256), lambda i: (0, 0, 0))],
        out_shape=[jax.ShapeDtypeStruct((_VEC_ROWS, 1024), F32), jax.ShapeDtypeStruct((NDEV, _SM_ROWS, 256), F32)],
        scratch_shapes=[pltpu.VMEM((40, 1024), F32)],
        compiler_params=_cparams(),
    )(*ins, dh, loss_acc)


def _adamw_vec(parts, W, M, V):
    nv = len(_VEC)

    def body(*refs):
        p_ref = refs[0]
        w_refs, m_refs, v_refs = refs[1:1 + nv], refs[1 + nv:1 + 2 * nv], refs[1 + 2 * nv:1 + 3 * nv]
        outs = refs[1 + 3 * nv:]

        def total(r):
            acc = p_ref[0, r:r + 1, :]
            for sidx in range(1, NDEV):
                acc = acc + p_ref[sidx, r:r + 1, :]
            return acc

        row = 0
        for idx, (_, nrows) in enumerate(_VEC):
            for part in range(nrows):
                cols = slice(1024 * part, 1024 * part + 1024)
                g = total(row + part)
                delta, mv, vv = _adam_math(g, w_refs[idx][:, cols], m_refs[idx][:, cols], v_refs[idx][:, cols])
                for o, val in zip(outs[4 * idx:4 * idx + 4], (g, delta, mv, vv)):
                    o[:, cols] = val
            row += nrows
        outs[-1][...] = total(_LOSS_ROW)[:, 0:128]

    names = [n for n, _ in _VEC]
    flat = lambda d: [d[n].reshape(1, -1) for n in names]
    ws, ms, vs = flat(W), flat(M), flat(V)
    res = pl.pallas_call(
        body, name="adamw_vec",
        out_shape=[jax.ShapeDtypeStruct(w.shape, F32) for w in ws for _ in range(4)]
        + [jax.ShapeDtypeStruct((1, 128), F32)],
        compiler_params=_cparams(),
    )(parts, *ws, *ms, *vs)
    return {n: tuple(res[4 * i:4 * i + 4]) for i, n in enumerate(names)}, res[-1]


def _adamw_small(parts, W, M, V):
    where = {"lru_conv_w": (slice(0, LW), slice(0, 128)), "conf_dw_w": (slice(8, 8 + KW), slice(0, 128)),
             "meta_tokens": (slice(40, 56), slice(0, 256))}
    names = list(where)

    def body(*refs):
        p_ref = refs[0]
        outs = refs[10:]
        for idx, n in enumerate(names):
            rs, cs = where[n]
            g = p_ref[0, rs, cs]
            for sidx in range(1, NDEV):
                g = g + p_ref[sidx, rs, cs]
            delta, mv, vv = _adam_math(g, refs[1 + idx][...], refs[4 + idx][...], refs[7 + idx][...])
            for o, val in zip(outs[4 * idx:4 * idx + 4], (g, delta, mv, vv)):
                o[...] = val

    two_d = lambda a: a.reshape(a.shape[-2:])
    ws, ms, vs = ([two_d(d[n]) for n in names] for d in (W, M, V))
    res = pl.pallas_call(
        body, name="adamw_small",
        out_shape=[jax.ShapeDtypeStruct(w.shape, F32) for w in ws for _ in range(4)],
        compiler_params=_cparams(),
    )(parts, *ws, *ms, *vs)
    return {n: tuple(res[4 * i:4 * i + 4]) for i, n in enumerate(names)}


def _pack_small(lru_cw, dw_w, meta):
    buf = jnp.zeros((_SM_ROWS, 256), F32)
    buf = buf.at[0:LW, 0:128].set(lru_cw)
    buf = buf.at[8:8 + dw_w.shape[0], 0:128].set(dw_w)
    return buf.at[40:56, :].set(meta)


def _block_diag4(w):
    w4 = w.reshape(NCB, 4, 64, 64)
    eye = jnp.eye(4, dtype=w.dtype)
    return jnp.einsum("ghij,hk->ghikj", w4, eye).reshape(NCB, CB, CB)


def _diag_blocks(g):
    g5 = g.reshape(NCB, 4, 64, 4, 64)
    return jnp.stack([g5[:, hh, :, hh, :] for hh in range(4)], axis=1).reshape(16, 64, 64)


def _gate_mats(W):
    return _block_diag4(W["w_gate_a"][0]).astype(BF16), _block_diag4(W["w_gate_x"][0]).astype(BF16)


def _local_step(x, target, meta_full, inproj, out_weights, lru_cw_full, dw_w_full, W, gate_mats, send):
    wa_g, wx_g = gate_mats

    h, hn = _prenorm(x, meta_full, W["pre_norm_w"])
    z, win_full = inproj(hn)
    ylru, xc, hs = _lru_fwd(z, lru_cw_full, W["lru_conv_b"], wa_g, W["b_gate_a"], wx_g, W["b_gate_x"],
                            W["lru_lambda"])
    vc = _conf_fwd_conv(z, dw_w_full, W["conf_dw_b"])
    wout_full, pw_full = out_weights(vc)
    yconf, p, xhat, rstd = _conf_fwd_proj(vc, z, W["conf_ln_w"], W["conf_ln_b"], pw_full, W["conf_pw_b"])
    dout, dy, loss_acc, dpostw_acc = _outproj_loss(ylru, yconf, wout_full, h, target, W["post_norm_w"])

    dycat, dwout_part = _outproj_bwd(dy, ylru, yconf, wout_full)
    tok = send("w_out", ("w_out", dwout_part))
    dvc, dz41, dpw_part, cvecs = _conf_bwd_proj(dycat, p, z, xhat, rstd, hs, W["conf_ln_w"], W["conf_ln_b"], pw_full, tok)
    tok = send("w_in_c", ("conf_pw_w", dpw_part), ("w_in_c", _inproj_wgrad("inproj_wgrad_c", hn, dz41, dz41)))
    dzc, ddw_acc, kvecs = _conf_bwd_conv(dvc, z, dw_w_full, tok)
    tok = send("w_in_b", ("w_in_b", _inproj_wgrad("inproj_wgrad_b", hn, dzc, dzc)))
    dzl, dwa_g, dwx_g, dcw_acc, lvecs = _lru_bwd(dycat, z, xc, hs, lru_cw_full, wa_g, W["b_gate_a"], wx_g,
                                                 W["b_gate_x"], W["lru_lambda"], tok)
    tok = send("w_gates", ("w_gate_a", _diag_blocks(dwa_g).reshape(16 * 64, 64)),
               ("w_gate_x", _diag_blocks(dwx_g).reshape(16 * 64, 64)))
    tok = send("w_in_a", ("w_in_a", _inproj_wgrad("inproj_wgrad_a", hn, dzl, tok)))
    grad_x, dmeta, dprew_acc = _inproj_bwd(dzl, dz41, dzc, win_full, h, dout, W["pre_norm_w"], tok)

    vec_pack, small_part = _pack_grads(dprew_acc, dpostw_acc, cvecs, kvecs, lvecs, dcw_acc, ddw_acc, dmeta, loss_acc)
    return grad_x, vec_pack, small_part


def kernel(x, meta_tokens, pre_norm_w, post_norm_w, w_in, b_in, lru_conv_w, lru_conv_b, w_gate_a, b_gate_a, w_gate_x, b_gate_x, lru_lambda, conf_dw_w, conf_dw_b, conf_ln_w, conf_ln_b, conf_pw_w, conf_pw_b, w_out, loss_target, m_meta_tokens, m_pre_norm_w, m_post_norm_w, m_w_in, m_b_in, m_lru_conv_w, m_lru_conv_b, m_w_gate_a, m_b_gate_a, m_w_gate_x, m_b_gate_x, m_lru_lambda, m_conf_dw_w, m_conf_dw_b, m_conf_ln_w, m_conf_ln_b, m_conf_pw_w, m_conf_pw_b, m_w_out, v_meta_tokens, v_pre_norm_w, v_post_norm_w, v_w_in, v_b_in, v_lru_conv_w, v_lru_conv_b, v_w_gate_a, v_b_gate_a, v_w_gate_x, v_b_gate_x, v_lru_lambda, v_conf_dw_w, v_conf_dw_b, v_conf_ln_w, v_conf_ln_b, v_conf_pw_w, v_conf_pw_b, v_w_out):
    W = dict(meta_tokens=meta_tokens, pre_norm_w=pre_norm_w, post_norm_w=post_norm_w, w_in=w_in, b_in=b_in,
             lru_conv_w=lru_conv_w, lru_conv_b=lru_conv_b, w_gate_a=w_gate_a, b_gate_a=b_gate_a,
             w_gate_x=w_gate_x, b_gate_x=b_gate_x, lru_lambda=lru_lambda, conf_dw_w=conf_dw_w,
             conf_dw_b=conf_dw_b, conf_ln_w=conf_ln_w, conf_ln_b=conf_ln_b, conf_pw_w=conf_pw_w,
             conf_pw_b=conf_pw_b, w_out=w_out)
    M = dict(meta_tokens=m_meta_tokens, pre_norm_w=m_pre_norm_w, post_norm_w=m_post_norm_w, w_in=m_w_in,
             b_in=m_b_in, lru_conv_w=m_lru_conv_w, lru_conv_b=m_lru_conv_b, w_gate_a=m_w_gate_a,
             b_gate_a=m_b_gate_a, w_gate_x=m_w_gate_x, b_gate_x=m_b_gate_x, lru_lambda=m_lru_lambda,
             conf_dw_w=m_conf_dw_w, conf_dw_b=m_conf_dw_b, conf_ln_w=m_conf_ln_w, conf_ln_b=m_conf_ln_b,
             conf_pw_w=m_conf_pw_w, conf_pw_b=m_conf_pw_b, w_out=m_w_out)
    V = dict(meta_tokens=v_meta_tokens, pre_norm_w=v_pre_norm_w, post_norm_w=v_post_norm_w, w_in=v_w_in,
             b_in=v_b_in, lru_conv_w=v_lru_conv_w, lru_conv_b=v_lru_conv_b, w_gate_a=v_w_gate_a,
             b_gate_a=v_b_gate_a, w_gate_x=v_w_gate_x, b_gate_x=v_b_gate_x, lru_lambda=v_lru_lambda,
             conf_dw_w=v_conf_dw_w, conf_dw_b=v_conf_dw_b, conf_ln_w=v_conf_ln_w, conf_ln_b=v_conf_ln_b,
             conf_pw_w=v_conf_pw_w, conf_pw_b=v_conf_pw_b, w_out=v_w_out)
    names = list(W.keys())
    shapes = {n: W[n].shape for n in names}

    small = _pack_small(lru_conv_w[0], conf_dw_w[0], meta_tokens)
    (small_flight,), tok = _exchange_start("gather_small_start", [
        (small, jax.ShapeDtypeStruct((NDEV, _SM_ROWS, 256), F32), _whole, _slot)])
    win_flight, tok = _win_gather_start(w_in[0].astype(BF16) + tok[0, 0].astype(BF16))
    win_flight, tok = _win_gather_links(win_flight, tok)
    gate_mats = _gate_mats(W)
    gathered, tok = _exchange_start("gather_out_start", [
        (w_out[0].astype(BF16) + tok[0, 0].astype(BF16), jax.ShapeDtypeStruct((D, D), BF16), _whole,
         _rows(D // NDEV)),
        (conf_pw_w[0].astype(BF16), jax.ShapeDtypeStruct((DC, DC), BF16), _whole, _rows(DC // NDEV)),
    ])
    (small_all,) = _exchange_wait("gather_small_wait", [small_flight], tok)
    unshard = lambda a: jnp.transpose(a, (1, 0, 2)).reshape(a.shape[1], -1)
    lru_cw_full = unshard(small_all[:, 0:LW, 0:128])
    dw_w_full = unshard(small_all[:, 8:8 + KWP, 0:128])
    meta_full = unshard(small_all[:, 40:56, :])

    def out_weights(after):
        return _exchange_wait("gather_out_wait", gathered, after)

    def inproj(hn):
        xi, yi, ci = lax.axis_index("x"), lax.axis_index("y"), lax.axis_index("c")
        shard = lambda px, py, pc: (4 * px + 2 * py + pc).astype(jnp.int32)
        over_links = jnp.stack([shard(1 - xi, yi, ci), shard(xi, 1 - yi, ci), shard(1 - xi, 1 - yi, ci)])
        z, src = _inproj_cols("inproj_own", jnp.stack([shard(xi, yi, ci)]), hn, win_flight["src"], b_in, None)
        flight = _win_gather_early(dict(win_flight, src=src))
        z, land = _inproj_cols("inproj_here", jnp.stack([shard(xi, yi, 1 - ci)]), hn, flight["land"], b_in, z)
        flight = _win_gather_forward("all", dict(flight, land=land), (1, 2, 3), z)
        z, land = _inproj_cols("inproj_links", over_links, hn, flight["land"], b_in, z)
        flight = _win_gather_forwarded("all", dict(flight, land=land), (1, 2, 3))
        z, land = _inproj_cols("inproj_sibling", over_links + 1 - 2 * ci, hn, flight["land"], b_in, z)
        return z, _win_gather_wait(dict(flight, land=land))

    row_stage = lambda ncol: (jax.ShapeDtypeStruct((NDEV, D // NDEV, ncol), BF16), _rows(D // NDEV))
    piece = {"w_in_a": row_stage(1024), "w_in_b": row_stage(2048), "w_in_c": row_stage(2048),
             "w_out": row_stage(D),
             "conf_pw_w": (jax.ShapeDtypeStruct((NDEV, DC // NDEV, DC), BF16), _rows(DC // NDEV)),
             "w_gate_a": (jax.ShapeDtypeStruct((NDEV, 16 * 64, 64), BF16), _whole),
             "w_gate_x": (jax.ShapeDtypeStruct((NDEV, 16 * 64, 64), BF16), _whole)}
    sent = {}

    def send(call, *named_parts):
        handles, token = _exchange_start(
            "scatter_" + call + "_start",
            [(part.astype(BF16), piece[name][0], piece[name][1], _slot) for name, part in named_parts])
        for (name, _), handle in zip(named_parts, handles):
            sent[name] = [handle]
        return token

    grad_x, vec_pack, small_part = _local_step(
        x[0], loss_target[0], meta_full, inproj, out_weights, lru_cw_full, dw_w_full, W, gate_mats, send)
    grad_x = grad_x[None]

    rest, tok = _exchange_start("scatter_rest_start", [
        (small_part, jax.ShapeDtypeStruct((NDEV, _SM_ROWS, 256), F32), _slot, _slot),
        (vec_pack, jax.ShapeDtypeStruct((NDEV, _VEC_ROWS, 1024), F32), _whole, _slot),
    ])
    (parts_c,) = _exchange_wait("scatter_w_in_c_wait", sent["w_in_c"], tok)
    (parts_b,) = _exchange_wait("scatter_w_in_b_wait", sent["w_in_b"], parts_c)
    (parts_a,) = _exchange_wait("scatter_w_in_a_wait", sent["w_in_a"], parts_b)
    win_rows = _sum_win_parts(parts_a, parts_b, parts_c)
    win_stage2, tok = _exchange_start("scatter_w_in_stage2_start", [
        (win_rows, jax.ShapeDtypeStruct((NDEV, D // NDEV, NIN // NDEV), BF16), _cols(NIN // NDEV), _slot)])

    G, DW, NM, NV = {}, {}, {}, {}
    (wout_parts,) = _exchange_wait("scatter_w_out_wait", sent["w_out"], tok)
    G["w_out"], DW["w_out"], NM["w_out"], NV["w_out"] = _adamw("adamw_w_out", wout_parts, w_out[0], m_w_out[0], v_w_out[0], 64)
    (pw_parts,) = _exchange_wait("scatter_conf_pw_w_wait", sent["conf_pw_w"], G["w_out"])
    G["conf_pw_w"], DW["conf_pw_w"], NM["conf_pw_w"], NV["conf_pw_w"] = _adamw(
        "adamw_pw", pw_parts, conf_pw_w[0], m_conf_pw_w[0], v_conf_pw_w[0], 128)
    res = {}
    wa_parts, wx_parts = _exchange_wait("scatter_w_gates_wait", sent["w_gate_a"] + sent["w_gate_x"], G["conf_pw_w"])
    for n, parts in (("w_gate_a", wa_parts), ("w_gate_x", wx_parts)):
        res[n] = _adamw("adamw_" + n, parts, *[d[n].reshape(16 * 64, 64) for d in (W, M, V)], 16 * 64)
    small_parts, vec_parts = _exchange_wait("scatter_rest_wait", rest, res["w_gate_x"][0])
    res.update(_adamw_small(small_parts, W, M, V))
    vec_res, loss_row = _adamw_vec(vec_parts, W, M, V)
    res.update(vec_res)
    (win_sum,) = _exchange_wait("scatter_w_in_stage2_wait", win_stage2, loss_row)
    res["w_in"] = _adamw("adamw_w_in", win_sum.reshape(1, D, NIN // NDEV), w_in[0], m_w_in[0], v_w_in[0], 256)
    for n, vals in res.items():
        for dst, val in zip((G, DW, NM, NV), vals):
            dst[n] = val
    for dst in (G, DW, NM, NV):
        for n in names:
            dst[n] = dst[n].reshape(shapes[n])
    loss = loss_row[0, 0]

    return (loss, grad_x, *[G[n] for n in names], *[DW[n] for n in names],
            *[NM[n] for n in names], *[NV[n] for n in names])
```

```python
import functools

import jax
import jax.numpy as jnp
from jax import lax
from jax.experimental import pallas as pl
from jax.experimental.pallas import tpu as pltpu

F32 = jnp.float32
BF16 = jnp.bfloat16

D = 2048
DL = 1024
DC = 1024
NIN = 5120
NMETA = 16
SEQ = 2048
T = NMETA + SEQ
TP = 2176
TM = 544
CB = 256
NCB = DL // CB
R = 16
RL = 64
KW = 31
KWP = 32
LW = 4
LRU_C = 8.0
EPS = 1e-6
NDEV = 8

ADAM_LR = 0.001
ADAM_B1 = 0.9
ADAM_B2 = 0.999
ADAM_EPS = 1e-08
ADAM_WD = 0.01
ADAM_STEP = 10

VMEM_LIMIT = 56 * 1024 * 1024


def _cparams():
    return pltpu.CompilerParams(vmem_limit_bytes=VMEM_LIMIT)


def _sig(x):
    return 1.0 / (1.0 + jnp.exp(-x))


def _expm1_neg(y):
    poly = y * (1.0 + y * (0.5 + y * (1.0 / 6.0 + y * (1.0 / 24.0 + y * (1.0 / 120.0)))))
    return jnp.where(y > -0.1, poly, jnp.exp(y) - 1.0)


def _softplus(x):
    e = jnp.exp(-jnp.abs(x))
    w = 1.0 + e
    l1p = jnp.where(w == 1.0, e, jnp.log(w) * e / (w - 1.0))
    return jnp.maximum(x, 0.0) + l1p


def _row_iota(shape):
    return lax.broadcasted_iota(jnp.int32, shape, 0)


def _fold8(v):
    return v[0:8, :] + v[8:16, :]


_FLIPS = [(k >> 2 & 1, k >> 1 & 1, k & 1) for k in range(1, NDEV)]
_HBM = pl.BlockSpec(memory_space=pltpu.HBM)
_SEM = pl.BlockSpec(memory_space=pltpu.SEMAPHORE)


def _peers():
    x, y, c = lax.axis_index("x"), lax.axis_index("y"), lax.axis_index("c")
    out = []
    for dx, dy, dc in _FLIPS:
        px = 1 - x if dx else x
        py = 1 - y if dy else y
        pc = 1 - c if dc else c
        out.append(((px, py, pc), 4 * px + 2 * py + pc))
    return 4 * x + 2 * y + c, out


def _exchange_start(name, items):
    n = len(items)

    def body(*refs):
        srcs, lands = refs[:n], refs[n:2 * n]
        outs = refs[2 * n:]
        send_sems, recv_sems, local_sems = outs[:n], outs[n:2 * n], outs[2 * n:3 * n]
        token = outs[-1]
        me, peers = _peers()
        for a in range(n):
            src_at, dst_at = items[a][2], items[a][3]
            pltpu.make_async_copy(src_at(srcs[a], me), dst_at(lands[a], me), local_sems[a]).start()
        for a in range(n):
            src_at, dst_at = items[a][2], items[a][3]
            for k, (pos, peer) in enumerate(peers):
                pltpu.make_async_remote_copy(
                    src_ref=src_at(srcs[a], peer), dst_ref=dst_at(lands[a], me),
                    send_sem=send_sems[a].at[k], recv_sem=recv_sems[a].at[k],
                    device_id=pos, device_id_type=pl.DeviceIdType.MESH).start()
        token[...] = jnp.zeros_like(token)

    srcs = [pltpu.with_memory_space_constraint(it[0], pltpu.HBM) for it in items]
    lands = [pltpu.with_memory_space_constraint(lax.empty(it[1].shape, it[1].dtype), pltpu.HBM) for it in items]
    sem7 = pltpu.SemaphoreType.DMA((NDEV - 1,))
    res = pl.pallas_call(
        body, name=name,
        out_shape=([sem7] * (2 * n) + [pltpu.SemaphoreType.DMA(())] * n
                   + [pltpu.HBM(a.shape, a.dtype) for a in srcs] + [pltpu.HBM(a.shape, a.dtype) for a in lands]
                   + [jax.ShapeDtypeStruct((8, 128), F32)]),
        in_specs=[_HBM] * (2 * n),
        out_specs=[_SEM] * (3 * n) + [_HBM] * (2 * n) + [pl.BlockSpec(memory_space=pltpu.VMEM)],
        input_output_aliases={i: 3 * n + i for i in range(2 * n)},
        compiler_params=pltpu.CompilerParams(has_side_effects=pltpu.SideEffectType.DATAFLOW_SIDE_EFFECTING),
    )(*srcs, *lands)
    handles = [dict(send=res[a], recv=res[n + a], local=res[2 * n + a], src=res[3 * n + a], land=res[4 * n + a],
                    src_at=items[a][2], dst_at=items[a][3]) for a in range(n)]
    return handles, res[-1]


def _wait_bytes(piece, sem):
    pltpu.make_async_copy(piece, piece, sem).wait()


def _exchange_wait(name, handles, after):
    n = len(handles)

    def body(*refs):
        srcs, lands = refs[:n], refs[n:2 * n]
        send_sems, recv_sems, local_sems = refs[2 * n:3 * n], refs[3 * n:4 * n], refs[4 * n:5 * n]
        me, peers = _peers()
        for a in range(n):
            src_at, dst_at = handles[a]["src_at"], handles[a]["dst_at"]
            for k, (pos, peer) in enumerate(peers):
                _wait_bytes(src_at(srcs[a], peer), send_sems[a].at[k])
                _wait_bytes(dst_at(lands[a], peer), recv_sems[a].at[k])
            pltpu.make_async_copy(src_at(srcs[a], me), dst_at(lands[a], me), local_sems[a]).wait()

    srcs = [hd["src"] for hd in handles]
    lands = [hd["land"] for hd in handles]
    res = pl.pallas_call(
        body, name=name,
        out_shape=[pltpu.HBM(a.shape, a.dtype) for a in srcs] + [pltpu.HBM(a.shape, a.dtype) for a in lands],
        in_specs=[_HBM] * (2 * n) + [_SEM] * (3 * n) + [pl.BlockSpec(memory_space=pl.ANY)],
        out_specs=[_HBM] * (2 * n),
        input_output_aliases={i: i for i in range(2 * n)},
        compiler_params=pltpu.CompilerParams(has_side_effects=pltpu.SideEffectType.DATAFLOW_SIDE_EFFECTING),
    )(*srcs, *lands, *[hd["send"] for hd in handles], *[hd["recv"] for hd in handles],
      *[hd["local"] for hd in handles], after)
    return list(res[n:])


_SIDE = pltpu.SideEffectType.DATAFLOW_SIDE_EFFECTING
_WCOLS = NIN // NDEV


def _win_cols(ref, l):
    return ref.at[:, pl.ds(pl.multiple_of(l * _WCOLS, 128), _WCOLS)]


def _win_routes():
    x, y, c = lax.axis_index("x"), lax.axis_index("y"), lax.axis_index("c")
    pos = [(x, y, 1 - c), (1 - x, y, c), (x, 1 - y, c), (1 - x, 1 - y, c)]
    return 4 * x + 2 * y + c, [(p, 4 * p[0] + 2 * p[1] + p[2]) for p in pos]


def _win_gather_start(shard):
    def body(src, land, send_sem, recv_sem, local_sem, src_thru, land_thru, token):
        me, routes = _win_routes()
        pltpu.make_async_copy(src, _win_cols(land, me), local_sem).start()
        pltpu.make_async_remote_copy(src_ref=src, dst_ref=_win_cols(land, me), send_sem=send_sem, recv_sem=recv_sem,
                                     device_id=routes[0][0], device_id_type=pl.DeviceIdType.MESH).start()
        token[...] = jnp.zeros_like(token)

    src = pltpu.with_memory_space_constraint(shard, pltpu.HBM)
    land = pltpu.with_memory_space_constraint(lax.empty((D, NIN), BF16), pltpu.HBM)
    sem = pltpu.SemaphoreType.DMA(())
    res = pl.pallas_call(
        body, name="win_gather_start",
        out_shape=[sem, sem, sem, pltpu.HBM(src.shape, BF16), pltpu.HBM(land.shape, BF16),
                   jax.ShapeDtypeStruct((8, 128), F32)],
        in_specs=[_HBM, _HBM],
        out_specs=[_SEM, _SEM, _SEM, _HBM, _HBM, pl.BlockSpec(memory_space=pltpu.VMEM)],
        input_output_aliases={0: 3, 1: 4},
        compiler_params=pltpu.CompilerParams(has_side_effects=_SIDE),
    )(src, land)
    return dict(send0=res[0], recv0=res[1], local=res[2], src=res[3], land=res[4]), res[5]


def _win_gather_links(hd, after):
    def body(src, land, after_ref, send_sems, recv_sems, src_thru, land_thru, token):
        me, routes = _win_routes()
        for k in (1, 2, 3):
            pltpu.make_async_remote_copy(src_ref=src, dst_ref=_win_cols(land, me), send_sem=send_sems.at[k - 1],
                                         recv_sem=recv_sems.at[k - 1], device_id=routes[k][0],
                                         device_id_type=pl.DeviceIdType.MESH).start()
        token[...] = jnp.zeros_like(token)

    sem3 = pltpu.SemaphoreType.DMA((3,))
    res = pl.pallas_call(
        body, name="win_gather_links",
        out_shape=[sem3, sem3, pltpu.HBM(hd["src"].shape, BF16), pltpu.HBM(hd["land"].shape, BF16),
                   jax.ShapeDtypeStruct((8, 128), F32)],
        in_specs=[_HBM, _HBM, pl.BlockSpec(memory_space=pl.ANY)],
        out_specs=[_SEM, _SEM, _HBM, _HBM, pl.BlockSpec(memory_space=pltpu.VMEM)],
        input_output_aliases={0: 2, 1: 3},
        compiler_params=pltpu.CompilerParams(has_side_effects=_SIDE),
    )(hd["src"], hd["land"], after)
    return dict(hd, send=res[0], recv=res[1], src=res[2], land=res[3]), res[4]


def _win_gather_forward(name, hd, ks, after):
    def body(land, recv_sems, after_ref, land_thru, fsend_sems, frecv_sems):
        me, routes = _win_routes()
        sibling = routes[0][0]
        for n, k in enumerate(ks):
            pos, peer = routes[k]
            piece = _win_cols(land, peer)
            pltpu.make_async_remote_copy(src_ref=piece, dst_ref=piece, send_sem=fsend_sems.at[n],
                                         recv_sem=recv_sems.at[k - 1], device_id=pos,
                                         device_id_type=pl.DeviceIdType.MESH).wait_recv()
            pltpu.make_async_remote_copy(src_ref=piece, dst_ref=piece, send_sem=fsend_sems.at[n],
                                         recv_sem=frecv_sems.at[n], device_id=sibling,
                                         device_id_type=pl.DeviceIdType.MESH).start()

    sems = pltpu.SemaphoreType.DMA((len(ks),))
    res = pl.pallas_call(
        body, name="win_gather_forward_" + name,
        out_shape=[pltpu.HBM(hd["land"].shape, BF16), sems, sems],
        in_specs=[_HBM, _SEM, pl.BlockSpec(memory_space=pl.ANY)],
        out_specs=[_HBM, _SEM, _SEM],
        input_output_aliases={0: 0},
        compiler_params=pltpu.CompilerParams(has_side_effects=_SIDE),
    )(hd["land"], hd["recv"], after)
    return dict(hd, land=res[0], **{"fsend" + name: res[1], "frecv" + name: res[2]})


def _win_gather_forwarded(name, hd, ks):
    def body(land, fsend_sems, frecv_sems, land_thru):
        me, routes = _win_routes()
        sib_c = routes[0][0][2]
        for n, k in enumerate(ks):
            _wait_bytes(_win_cols(land, routes[k][1]), fsend_sems.at[n])
            _wait_bytes(_win_cols(land, 4 * routes[k][0][0] + 2 * routes[k][0][1] + sib_c), frecv_sems.at[n])

    res = pl.pallas_call(
        body, name="win_gather_forwarded_" + name,
        out_shape=[pltpu.HBM(hd["land"].shape, BF16)],
        in_specs=[_HBM, _SEM, _SEM],
        out_specs=[_HBM],
        input_output_aliases={0: 0},
        compiler_params=pltpu.CompilerParams(has_side_effects=_SIDE),
    )(hd["land"], hd["fsend" + name], hd["frecv" + name])
    return dict(hd, land=res[0])


def _win_gather_early(hd):
    def body(src, land, recv_sem, local_sem, src_thru, land_thru):
        me, routes = _win_routes()
        _wait_bytes(_win_cols(land, routes[0][1]), recv_sem)
        pltpu.make_async_copy(src, _win_cols(land, me), local_sem).wait()

    res = pl.pallas_call(
        body, name="win_gather_early",
        out_shape=[pltpu.HBM(hd["src"].shape, BF16), pltpu.HBM(hd["land"].shape, BF16)],
        in_specs=[_HBM, _HBM, _SEM, _SEM],
        out_specs=[_HBM, _HBM],
        input_output_aliases={0: 0, 1: 1},
        compiler_params=pltpu.CompilerParams(has_side_effects=_SIDE),
    )(hd["src"], hd["land"], hd["recv0"], hd["local"])
    return dict(hd, src=res[0], land=res[1])


def _win_gather_wait(hd):
    def body(src, land, send0_sem, send_sems, src_thru, land_thru):
        for k in range(4):
            _wait_bytes(src, send0_sem if k == 0 else send_sems.at[k - 1])

    res = pl.pallas_call(
        body, name="win_gather_wait",
        out_shape=[pltpu.HBM(hd["src"].shape, BF16), pltpu.HBM(hd["land"].shape, BF16)],
        in_specs=[_HBM, _HBM, _SEM, _SEM],
        out_specs=[_HBM, _HBM],
        input_output_aliases={0: 0, 1: 1},
        compiler_params=pltpu.CompilerParams(has_side_effects=_SIDE),
    )(hd["src"], hd["land"], hd["send0"], hd["send"])
    return res[1]


def _whole(ref, l):
    return ref


def _slot(ref, l):
    return ref.at[l]


def _cols(width):
    def at(ref, l):
        return ref.at[:, pl.ds(pl.multiple_of(l * width, 128), width)]
    return at


def _rows(height):
    def at(ref, l):
        return ref.at[pl.ds(pl.multiple_of(l * height, 8), height), :]
    return at


NTILE = TP // TM


def _tile_rows(t):
    lo = max(t * TM - NMETA, 0)
    hi = min((t + 1) * TM - NMETA, SEQ)
    return lo, hi - lo, lo + NMETA - t * TM


def _for_tile(t, fn):
    for static_t in range(NTILE):
        pl.when(t == static_t)(functools.partial(fn, static_t))


def _token_tile_copy(hbm_ref, buf, sem, t):
    lo, n, off = _tile_rows(t)
    return pltpu.make_async_copy(hbm_ref.at[pl.ds(lo, n)], buf.at[pl.ds(off, n)], sem)


def _prenorm(x, meta_full, pre_w):
    def body(x_ref, meta_ref, pw_ref, h_ref, hn_ref, xbuf, sems):
        i = pl.program_id(0)
        slot = i % 2

        def start(t):
            _token_tile_copy(x_ref, xbuf.at[t % 2], sems.at[t % 2], t).start()

        @pl.when(i == 0)
        def _():
            start(0)
        _for_tile(i + 1, start)
        _for_tile(i, lambda t: _token_tile_copy(x_ref, xbuf.at[t % 2], sems.at[t % 2], t).wait())

        @pl.when(i == 0)
        def _():
            xbuf[0, 0:NMETA, :] = meta_ref[...]

        @pl.when(i == NTILE - 1)
        def _():
            last = _tile_rows(NTILE - 1)[1]
            xbuf[(NTILE - 1) % 2, last:TM, :] = jnp.zeros((TM - last, D), F32)

        pw = pw_ref[...]

        def chunk(ci, carry):
            r0 = pl.multiple_of(ci * R, R)
            xv = xbuf[slot, pl.ds(r0, R), :]
            h_ref[pl.ds(r0, R), :] = xv
            ms = jnp.mean(xv * xv, axis=-1, keepdims=True)
            hn_ref[pl.ds(r0, R), :] = (xv * lax.rsqrt(ms + EPS) * pw).astype(BF16)
            return carry
        lax.fori_loop(0, TM // R, chunk, 0, unroll=2)

    row = pl.BlockSpec((TM, D), lambda i: (i, 0))
    return pl.pallas_call(
        body, name="prenorm",
        grid=(NTILE,),
        in_specs=[pl.BlockSpec(memory_space=pl.ANY), pl.BlockSpec((NMETA, D), lambda i: (0, 0)),
                  pl.BlockSpec((1, D), lambda i: (0, 0))],
        out_specs=[row, row],
        out_shape=[jax.ShapeDtypeStruct((TP, D), F32), jax.ShapeDtypeStruct((TP, D), BF16)],
        scratch_shapes=[pltpu.VMEM((2, TM, D), F32), pltpu.SemaphoreType.DMA((2,))],
        compiler_params=_cparams(),
    )(x, meta_full, pre_w)


def _inproj_cols(name, shards, hn, w_land, b_in, z_prev):
    nsh = shards.shape[0]
    one_shard = w_land.shape[1] == _WCOLS

    def body(idx_ref, hn_ref, w_ref, b_ref, *rest):
        z_ref = rest[-2]
        z_ref[...] = jnp.dot(hn_ref[...], w_ref[...], preferred_element_type=F32) + b_ref[...]

    any_spec = pl.BlockSpec(memory_space=pl.ANY)
    in_specs = [pl.BlockSpec((TM, D), lambda j, i, idx: (i, 0)),
                pl.BlockSpec((D, _WCOLS), lambda j, i, idx: (0, 0 if one_shard else idx[j])),
                pl.BlockSpec((1, _WCOLS), lambda j, i, idx: (0, idx[j]))]
    operands = [hn, w_land, b_in]
    aliases = {2: 1}
    if z_prev is not None:
        in_specs.append(any_spec)
        operands.append(z_prev)
        aliases[4] = 0
    return pl.pallas_call(
        body, name=name,
        grid_spec=pltpu.PrefetchScalarGridSpec(
            num_scalar_prefetch=1, grid=(nsh, TP // TM), in_specs=in_specs,
            out_specs=[pl.BlockSpec((TM, _WCOLS), lambda j, i, idx: (i, idx[j])), any_spec]),
        out_shape=[jax.ShapeDtypeStruct((TP, NIN), F32), jax.ShapeDtypeStruct(w_land.shape, w_land.dtype)],
        input_output_aliases=aliases,
        compiler_params=_cparams(),
    )(shards, *operands)


def _gate_values(ga, gx, xc, sp8):
    r = _sig(ga)
    i = _sig(gx)
    log_a = -(r * sp8)
    a = jnp.exp(log_a)
    mult = jnp.sqrt(-_expm1_neg(2.0 * log_a))
    return r, i, a, mult


def _lru_fwd(z, conv_w, conv_b, wa_g, b_a, wx_g, b_x, lam):
    def body(x_ref, g_ref, cw_ref, cb_ref, wa_ref, ba_ref, wx_ref, bx_ref, lam_ref,
             y_ref, xc_ref, hs_ref, ga_s, gx_s):
        taps = [cw_ref[k:k + 1, :] for k in range(LW)]
        cb = cb_ref[...]

        def conv_chunk(ci, carry):
            r0 = pl.multiple_of(ci * RL, RL)
            cur = x_ref[pl.ds(r0, RL), :]
            p0 = pl.multiple_of(jnp.maximum(r0 - 8, 0), 8)
            prev = jnp.where(ci > 0, x_ref[pl.ds(p0, 8), :], 0.0)
            buf = jnp.concatenate([prev, cur], axis=0)
            acc = cur * taps[LW - 1] + cb
            for s in range(1, LW):
                acc = acc + pltpu.roll(buf, s, 0)[8:8 + RL, :] * taps[LW - 1 - s]
            xc_ref[pl.ds(r0, RL), :] = acc
            return carry
        lax.fori_loop(0, TP // RL, conv_chunk, 0)

        def gate_chunk(ci, carry):
            r0 = pl.multiple_of(ci * TM, TM)
            xb = xc_ref[pl.ds(r0, TM), :].astype(BF16)
            ga_s[pl.ds(r0, TM), :] = jnp.dot(xb, wa_ref[...], preferred_element_type=F32) + ba_ref[...]
            gx_s[pl.ds(r0, TM), :] = jnp.dot(xb, wx_ref[...], preferred_element_type=F32) + bx_ref[...]
            return carry
        lax.fori_loop(0, TP // TM, gate_chunk, 0)

        sp8 = LRU_C * _softplus(-lam_ref[...])
        row = _row_iota((R, CB))

        def scan_chunk(ci, hprev):
            r0 = pl.multiple_of(ci * R, R)
            xc = xc_ref[pl.ds(r0, R), :]
            _, i, a, mult = _gate_values(ga_s[pl.ds(r0, R), :], gx_s[pl.ds(r0, R), :], xc, sp8)
            u = mult * (i * xc)
            k = 1
            while k < R:
                m = row >= k
                u = jnp.where(m, a * pltpu.roll(u, k, 0) + u, u)
                a = jnp.where(m, a * pltpu.roll(a, k, 0), a)
                k *= 2
            hv = u + a * hprev
            hs_ref[pl.ds(r0, R), :] = hv
            g = g_ref[pl.ds(r0, R), :]
            y_ref[pl.ds(r0, R), :] = (hv * (g * _sig(g))).astype(BF16)
            return jnp.sum(jnp.where(row == R - 1, hv, 0.0), axis=0, keepdims=True)
        def scan_pass(i, hp):
            for sub in range(4):
                hp = scan_chunk(4 * i + sub, hp)
            return hp
        lax.fori_loop(0, TP // R // 4, scan_pass, jnp.zeros((1, CB), F32))

    col = lambda off: pl.BlockSpec((TP, CB), lambda j: (0, off + j))
    vec = pl.BlockSpec((1, CB), lambda j: (0, j))
    wsp = pl.BlockSpec((None, CB, CB), lambda j: (j, 0, 0))
    return pl.pallas_call(
        body, name="lru_fwd",
        grid=(NCB,),
        in_specs=[col(0), col(NCB), pl.BlockSpec((LW, CB), lambda j: (0, j)), vec, wsp, vec, wsp, vec, vec],
        out_specs=[col(0), col(0), col(0)],
        out_shape=[jax.ShapeDtypeStruct((TP, DL), BF16), jax.ShapeDtypeStruct((TP, DL), F32),
                   jax.ShapeDtypeStruct((TP, DL), F32)],
        scratch_shapes=[pltpu.VMEM((TP, CB), F32), pltpu.VMEM((TP, CB), F32)],
        compiler_params=_cparams(),
    )(z, z, conv_w, conv_b, wa_g, b_a, wx_g, b_x, lam)


CBC = 128
NCBC = DC // CBC
RC = 128


def _fold_rows(v):
    acc = v[0:8, :]
    for r in range(8, v.shape[0], 8):
        acc = acc + v[r:r + 8, :]
    return acc


def _conf_fwd_conv(z, dw_w, dw_b):
    def body(u1_ref, u2_ref, w_ref, b_ref, vc_ref, vs):
        vs[pl.ds(0, KWP), :] = jnp.zeros((KWP, CBC), F32)

        def glu_chunk(ci, carry):
            r0 = pl.multiple_of(ci * RC, RC)
            vs[pl.ds(KWP + r0, RC), :] = u1_ref[pl.ds(r0, RC), :] * _sig(u2_ref[pl.ds(r0, RC), :])
            return carry
        lax.fori_loop(0, TP // RC, glu_chunk, 0)

        bias = b_ref[...]

        def conv_chunk(ci, carry):
            r0 = pl.multiple_of(ci * RC, RC)
            buf = vs[pl.ds(r0, KWP + RC), :]
            acc = jnp.zeros((RC, CBC), F32) + bias
            for rr in range(8):
                rolled = buf if rr == 0 else pltpu.roll(buf, rr, 0)
                for q in range(4):
                    s = 8 * q + rr
                    if s > KW - 1:
                        continue
                    k = KW - 1 - s
                    acc = acc + rolled[KWP - 8 * q:KWP - 8 * q + RC, :] * w_ref[k:k + 1, :]
            vc_ref[pl.ds(r0, RC), :] = acc
            return carry
        lax.fori_loop(0, TP // RC, conv_chunk, 0)

    return pl.pallas_call(
        body, name="conf_fwd_conv",
        grid=(NCBC,),
        in_specs=[pl.BlockSpec((TP, CBC), lambda j: (0, 2 * NCBC + j)),
                  pl.BlockSpec((TP, CBC), lambda j: (0, 3 * NCBC + j)),
                  pl.BlockSpec((KWP, CBC), lambda j: (0, j)),
                  pl.BlockSpec((1, CBC), lambda j: (0, j))],
        out_specs=pl.BlockSpec((TP, CBC), lambda j: (0, j)),
        out_shape=jax.ShapeDtypeStruct((TP, DC), F32),
        scratch_shapes=[pltpu.VMEM((TP + KWP, CBC), F32)],
        compiler_params=_cparams(),
    )(z, z, dw_w, dw_b)


def _ln_chunk(vc, lw, lb):
    mu = jnp.mean(vc, axis=-1, keepdims=True)
    xm = vc - mu
    var = jnp.mean(xm * xm, axis=-1, keepdims=True)
    rstd = lax.rsqrt(var + EPS)
    xhat = xm * rstd
    return xhat, rstd, xhat * lw + lb


def _conf_fwd_proj(vc, z, ln_w, ln_b, pw_w, pw_b):
    def body(vc_ref, g_ref, lw_ref, lb_ref, w_ref, b_ref, y_ref, p_ref, xhat_ref, rstd_ref, s_s):
        lw, lb = lw_ref[...], lb_ref[...]

        def ln_chunk(ci, carry):
            r0 = pl.multiple_of(ci * R, R)
            for half in range(2):
                rr = r0 + 8 * half
                xhat, rstd, ln = _ln_chunk(vc_ref[pl.ds(rr, 8), :], lw, lb)
                xhat_ref[pl.ds(rr, 8), :] = xhat
                rstd_ref[pl.ds(rr, 8), :] = jnp.broadcast_to(rstd, (8, 128))
                p_ref[pl.ds(rr, 8), :] = ln * _sig(ln)
            s_s[pl.ds(r0, R), :] = p_ref[pl.ds(r0, R), :].astype(BF16)
            return carry
        lax.fori_loop(0, TM // R, ln_chunk, 0, unroll=2)

        p_ref[...] = jnp.dot(s_s[...], w_ref[...], preferred_element_type=F32) + b_ref[...]

        def out_chunk(ci, carry):
            r0 = pl.multiple_of(ci * R, R)
            g = g_ref[pl.ds(r0, R), :]
            y_ref[pl.ds(r0, R), :] = (p_ref[pl.ds(r0, R), :] * (g * _sig(g))).astype(BF16)
            return carry
        lax.fori_loop(0, TM // R, out_chunk, 0)

    row = pl.BlockSpec((TM, DC), lambda i: (i, 0))
    vec = pl.BlockSpec((1, DC), lambda i: (0, 0))
    return pl.pallas_call(
        body, name="conf_fwd_proj",
        grid=(TP // TM,),
        in_specs=[row, pl.BlockSpec((TM, DC), lambda i: (i, 4)), vec, vec,
                  pl.BlockSpec((DC, DC), lambda i: (0, 0)), vec],
        out_specs=[row, row, row, pl.BlockSpec((TM, 128), lambda i: (i, 0))],
        out_shape=[jax.ShapeDtypeStruct((TP, DC), BF16), jax.ShapeDtypeStruct((TP, DC), F32),
                   jax.ShapeDtypeStruct((TP, DC), F32), jax.ShapeDtypeStruct((TP, 128), F32)],
        scratch_shapes=[pltpu.VMEM((TM, DC), BF16)],
        compiler_params=_cparams(),
    )(vc, z, ln_w, ln_b, pw_w, pw_b)


def _outproj_loss(ylru, yconf, w_out, h, target, post_w):
    def body(yl_ref, yc_ref, w_ref, h_ref, tgt_hbm, pw_ref, dout_ref, dy_ref, loss_ref, dpw_ref, y_s, t_ref, sem):
        i = pl.program_id(0)
        k = pl.program_id(1)

        @pl.when(k == 0)
        def _():
            _for_tile(i, lambda t: _token_tile_copy(tgt_hbm, t_ref, sem, t).start())
            y_s[...] = jnp.dot(yl_ref[...], w_ref[...], preferred_element_type=F32)

        @pl.when(k == 1)
        def _():
            y_s[...] += jnp.dot(yc_ref[...], w_ref[...], preferred_element_type=F32)

        @pl.when(jnp.logical_and(i == 0, k == 1))
        def _():
            loss_ref[...] = jnp.zeros_like(loss_ref)
            dpw_ref[...] = jnp.zeros_like(dpw_ref)

        @pl.when(k == 1)
        def _():
            _for_tile(i, lambda t: _token_tile_copy(tgt_hbm, t_ref, sem, t).wait())

            @pl.when(i == 0)
            def _():
                t_ref[0:NMETA, :] = jnp.zeros((NMETA, D), F32)

            @pl.when(i == NTILE - 1)
            def _():
                last = _tile_rows(NTILE - 1)[1]
                t_ref[last:TM, :] = jnp.zeros((TM - last, D), F32)

            pw = pw_ref[...]
            row = _row_iota((8, D))

            def chunk(ci, carry):
                r0 = pl.multiple_of(ci * 8, 8)
                yv = y_s[pl.ds(r0, 8), :]
                rs = lax.rsqrt(jnp.mean(yv * yv, axis=-1, keepdims=True) + EPS)
                grow = row + (i * TM + r0)
                valid = jnp.logical_and(grow >= NMETA, grow < T)
                yn = yv * rs
                err = jnp.where(valid, h_ref[pl.ds(r0, 8), :] + yn * pw - t_ref[pl.ds(r0, 8), :], 0.0)
                loss_ref[...] += err * err
                d_rn = err * (1.0 / D)
                dout_ref[pl.ds(r0, 8), :] = d_rn
                dpw_ref[...] += d_rn * yn
                gw = d_rn * pw
                dot = jnp.mean(gw * yv, axis=-1, keepdims=True)
                dy_ref[pl.ds(r0, 8), :] = (rs * gw - yv * (rs * rs * rs * dot)).astype(BF16)
                return carry
            lax.fori_loop(0, TM // 8, chunk, 0, unroll=4)

    row = pl.BlockSpec((TM, D), lambda i, k: (i, 0))
    half = pl.BlockSpec((TM, DL), lambda i, k: (i, 0))
    acc = pl.BlockSpec((8, D), lambda i, k: (0, 0))
    return pl.pallas_call(
        body, name="outproj_loss",
        grid=(TP // TM, 2),
        in_specs=[half, half, pl.BlockSpec((DL, D), lambda i, k: (k, 0)), row, pl.BlockSpec(memory_space=pl.ANY),
                  pl.BlockSpec((1, D), lambda i, k: (0, 0))],
        out_specs=[row, row, acc, acc],
        out_shape=[jax.ShapeDtypeStruct((TP, D), F32), jax.ShapeDtypeStruct((TP, D), BF16),
                   jax.ShapeDtypeStruct((8, D), F32), jax.ShapeDtypeStruct((8, D), F32)],
        scratch_shapes=[pltpu.VMEM((TM, D), F32), pltpu.VMEM((TM, D), F32), pltpu.SemaphoreType.DMA(())],
        compiler_params=_cparams(),
    )(ylru, yconf, w_out, h, target, post_w)


_NT = (((1,), (1,)), ((), ()))
_TN = (((0,), (0,)), ((), ()))


def _outproj_bwd(dy, ylru, yconf, w_out):
    def body(dy_ref, yl_ref, yc_ref, w_ref, dycat_ref, dw_ref):
        j = pl.program_id(0)
        dyv = dy_ref[...]
        dycat_ref[...] = lax.dot_general(dyv, w_ref[...], _NT, preferred_element_type=F32)

        @pl.when(j < NCB)
        def _():
            dw_ref[...] = lax.dot_general(yl_ref[...], dyv, _TN, preferred_element_type=F32).astype(BF16)

        @pl.when(j >= NCB)
        def _():
            dw_ref[...] = lax.dot_general(yc_ref[...], dyv, _TN, preferred_element_type=F32).astype(BF16)

    return pl.pallas_call(
        body, name="outproj_bwd",
        grid=(2 * NCB,),
        in_specs=[pl.BlockSpec((TP, D), lambda j: (0, 0)),
                  pl.BlockSpec((TP, CB), lambda j: (0, jnp.minimum(j, NCB - 1))),
                  pl.BlockSpec((TP, CB), lambda j: (0, jnp.maximum(j - NCB, 0))),
                  pl.BlockSpec((CB, D), lambda j: (j, 0))],
        out_specs=[pl.BlockSpec((TP, CB), lambda j: (0, j)), pl.BlockSpec((CB, D), lambda j: (j, 0))],
        out_shape=[jax.ShapeDtypeStruct((TP, D), F32), jax.ShapeDtypeStruct((D, D), BF16)],
        compiler_params=_cparams(),
    )(dy, ylru, yconf, w_out)


_AFTER = pl.BlockSpec(memory_space=pl.ANY)


def _conf_bwd_proj(dycat, p, z, xhat, rstd, hs, ln_w, ln_b, pw_w, after):
    def body(dy_ref, p_ref, g_ref, xhat_ref, rstd_ref, dyl_ref, hs_ref, gl_ref, lw_ref, lb_ref, w_ref, after_ref,
             dvc_ref, dz_ref, dpw_ref, vecs_ref, dp_s, s_s, ds_s):
        i = pl.program_id(0)
        lw, lb = lw_ref[...], lb_ref[...]

        @pl.when(i == 0)
        def _():
            dpw_ref[...] = jnp.zeros_like(dpw_ref)
            vecs_ref[...] = jnp.zeros_like(vecs_ref)

        def pre_chunk(ci, carry):
            r0 = pl.multiple_of(ci * R, R)
            for half in range(2):
                rr = r0 + 8 * half
                dyv = dy_ref[pl.ds(rr, 8), :]
                g = g_ref[pl.ds(rr, 8), :]
                sg = _sig(g)
                dp = dyv * (g * sg)
                dg = dyv * p_ref[pl.ds(rr, 8), :] * (sg * (1.0 + g * (1.0 - sg)))
                vecs_ref[0:8, :] += dp
                vecs_ref[8:16, :] += dg
                ds_s[pl.ds(rr, 8), :] = dp
                dvc_ref[pl.ds(rr, 8), :] = dg
            dp_s[pl.ds(r0, R), :] = ds_s[pl.ds(r0, R), :].astype(BF16)
            dz_ref[0, pl.ds(r0, R), :] = dvc_ref[pl.ds(r0, R), :].astype(BF16)
            for half in range(2):
                rr = r0 + 8 * half
                gl = gl_ref[pl.ds(rr, 8), :]
                sgl = _sig(gl)
                dgl = dyl_ref[pl.ds(rr, 8), :] * hs_ref[pl.ds(rr, 8), :] * (sgl * (1.0 + gl * (1.0 - sgl)))
                vecs_ref[32:40, :] += dgl
                dvc_ref[pl.ds(rr, 8), :] = dgl
            dz_ref[1, pl.ds(r0, R), :] = dvc_ref[pl.ds(r0, R), :].astype(BF16)
            for half in range(2):
                rr = r0 + 8 * half
                ln = xhat_ref[pl.ds(rr, 8), :] * lw + lb
                ds_s[pl.ds(rr, 8), :] = ln * _sig(ln)
            s_s[pl.ds(r0, R), :] = ds_s[pl.ds(r0, R), :].astype(BF16)
            return carry
        lax.fori_loop(0, TM // R, pre_chunk, 0, unroll=2)

        dpb = dp_s[...]
        ds_s[...] = lax.dot_general(dpb, w_ref[...], _NT, preferred_element_type=F32)
        dpw_ref[...] += lax.dot_general(s_s[...], dpb, _TN, preferred_element_type=F32)

        def post_chunk(ci, carry):
            r0 = pl.multiple_of(ci * 8, 8)
            xhat = xhat_ref[pl.ds(r0, 8), :]
            rstd = jnp.tile(rstd_ref[pl.ds(r0, 8), :], (1, DC // 128))
            ln = xhat * lw + lb
            sl = _sig(ln)
            dln = ds_s[pl.ds(r0, 8), :] * (sl * (1.0 + ln * (1.0 - sl)))
            vecs_ref[16:24, :] += dln * xhat
            vecs_ref[24:32, :] += dln
            dxh = dln * lw
            m1 = jnp.mean(dxh, axis=-1, keepdims=True)
            m2 = jnp.mean(dxh * xhat, axis=-1, keepdims=True)
            dvc_ref[pl.ds(r0, 8), :] = rstd * (dxh - m1 - xhat * m2)
            return carry
        lax.fori_loop(0, TM // 8, post_chunk, 0, unroll=4)

    row = pl.BlockSpec((TM, DC), lambda i: (i, 0))
    vec = pl.BlockSpec((1, DC), lambda i: (0, 0))
    return pl.pallas_call(
        body, name="conf_bwd_proj",
        grid=(TP // TM,),
        in_specs=[pl.BlockSpec((TM, DC), lambda i: (i, 1)), row, pl.BlockSpec((TM, DC), lambda i: (i, 4)), row,
                  pl.BlockSpec((TM, 128), lambda i: (i, 0)),
                  pl.BlockSpec((TM, DL), lambda i: (i, 0)), row, pl.BlockSpec((TM, DL), lambda i: (i, 1)),
                  vec, vec, pl.BlockSpec((DC, DC), lambda i: (0, 0)), _AFTER],
        out_specs=[row, pl.BlockSpec((2, TM, DC), lambda i: (0, i, 0)), pl.BlockSpec((DC, DC), lambda i: (0, 0)),
                   pl.BlockSpec((40, DC), lambda i: (0, 0))],
        out_shape=[jax.ShapeDtypeStruct((TP, DC), F32), jax.ShapeDtypeStruct((2, TP, DC), BF16),
                   jax.ShapeDtypeStruct((DC, DC), F32), jax.ShapeDtypeStruct((40, DC), F32)],
        scratch_shapes=[pltpu.VMEM((TM, DC), BF16), pltpu.VMEM((TM, DC), BF16), pltpu.VMEM((TM, DC), F32)],
        compiler_params=_cparams(),
    )(dycat, p, z, xhat, rstd, dycat, hs, z, ln_w, ln_b, pw_w, after)


def _conf_bwd_conv(dvc, z, dw_w, after):
    def body(dvc_ref, u1_ref, u2_ref, w_ref, after_ref, du_ref, dw_ref, vecs_ref, vs, dvs):
        vs[pl.ds(0, KWP), :] = jnp.zeros((KWP, CBC), F32)
        dvs[pl.ds(TP, KWP), :] = jnp.zeros((KWP, CBC), F32)
        dw_ref[...] = jnp.zeros_like(dw_ref)
        vecs_ref[...] = jnp.zeros_like(vecs_ref)

        def fill_chunk(ci, carry):
            r0 = pl.multiple_of(ci * RC, RC)
            vs[pl.ds(KWP + r0, RC), :] = u1_ref[pl.ds(r0, RC), :] * _sig(u2_ref[pl.ds(r0, RC), :])
            dv = dvc_ref[pl.ds(r0, RC), :]
            dvs[pl.ds(r0, RC), :] = dv
            vecs_ref[0:8, :] += _fold_rows(dv)
            return carry
        lax.fori_loop(0, TP // RC, fill_chunk, 0)

        def conv_chunk(ci, carry):
            r0 = pl.multiple_of(ci * RC, RC)
            vbuf = vs[pl.ds(r0, KWP + RC), :]
            dbuf = dvs[pl.ds(r0, KWP + RC), :]
            dcur = dbuf[0:RC, :]
            dv = jnp.zeros((RC, CBC), F32)
            for rr in range(8):
                vroll = vbuf if rr == 0 else pltpu.roll(vbuf, rr, 0)
                droll = dbuf if rr == 0 else pltpu.roll(dbuf, KWP + RC - rr, 0)
                for q in range(4):
                    s = 8 * q + rr
                    if s > KW - 1:
                        continue
                    k = KW - 1 - s
                    dv = dv + droll[8 * q:8 * q + RC, :] * w_ref[k:k + 1, :]
                    dw_ref[8 * k:8 * k + 8, :] += _fold_rows(dcur * vroll[KWP - 8 * q:KWP - 8 * q + RC, :])
            u1 = u1_ref[pl.ds(r0, RC), :]
            sg = _sig(u2_ref[pl.ds(r0, RC), :])
            du1 = dv * sg
            du2 = dv * u1 * (sg * (1.0 - sg))
            du_ref[0, pl.ds(r0, RC), :] = du1.astype(BF16)
            du_ref[1, pl.ds(r0, RC), :] = du2.astype(BF16)
            vecs_ref[8:16, :] += _fold_rows(du1)
            vecs_ref[16:24, :] += _fold_rows(du2)
            return carry
        lax.fori_loop(0, TP // RC, conv_chunk, 0)

    blk = pl.BlockSpec((TP, CBC), lambda j: (0, j))
    return pl.pallas_call(
        body, name="conf_bwd_conv",
        grid=(NCBC,),
        in_specs=[blk, pl.BlockSpec((TP, CBC), lambda j: (0, 2 * NCBC + j)),
                  pl.BlockSpec((TP, CBC), lambda j: (0, 3 * NCBC + j)), pl.BlockSpec((KWP, CBC), lambda j: (0, j)),
                  _AFTER],
        out_specs=[pl.BlockSpec((2, TP, CBC), lambda j: (0, 0, j)), pl.BlockSpec((8 * KWP, CBC), lambda j: (0, j)),
                   pl.BlockSpec((24, CBC), lambda j: (0, j))],
        out_shape=[jax.ShapeDtypeStruct((2, TP, DC), BF16),
                   jax.ShapeDtypeStruct((8 * KWP, DC), F32), jax.ShapeDtypeStruct((24, DC), F32)],
        scratch_shapes=[pltpu.VMEM((TP + KWP, CBC), F32), pltpu.VMEM((TP + KWP, CBC), F32)],
        compiler_params=_cparams(),
    )(dvc, z, z, dw_w, after)


def _lru_bwd(dycat, z, xc, hs, conv_w, wa_g, b_a, wx_g, b_x, lam, after):
    NV = 6

    def body(dy_ref, x_ref, g_ref, xc_ref, hs_ref, cw_ref, wa_ref, ba_ref, wx_ref, bx_ref, lam_ref, after_ref,
             dzl_ref, dwa_ref, dwx_ref, dcw_ref, vecs_ref, ga_s, gx_s, dxc_s):
        vecs_ref[...] = jnp.zeros_like(vecs_ref)
        dcw_ref[...] = jnp.zeros_like(dcw_ref)
        dxc_s[pl.ds(TP, 8), :] = jnp.zeros((8, CB), F32)

        def gate_chunk(ci, carry):
            r0 = pl.multiple_of(ci * TM, TM)
            xb = xc_ref[pl.ds(r0, TM), :].astype(BF16)
            ga_s[pl.ds(r0, TM), :] = jnp.dot(xb, wa_ref[...], preferred_element_type=F32) + ba_ref[...]
            gx_s[pl.ds(r0, TM), :] = jnp.dot(xb, wx_ref[...], preferred_element_type=F32) + bx_ref[...]
            return carry
        lax.fori_loop(0, TP // TM, gate_chunk, 0)

        sp8 = LRU_C * _softplus(-lam_ref[...])
        row = _row_iota((R, CB))
        nchunk = TP // R

        def scan_chunk(cj, carry):
            a_next, lam_next = carry
            ci = nchunk - 1 - cj
            r0 = pl.multiple_of(ci * R, R)
            dyv = dy_ref[pl.ds(r0, R), :]
            g = g_ref[pl.ds(r0, R), :]
            hv = hs_ref[pl.ds(r0, R), :]
            xc = xc_ref[pl.ds(r0, R), :]
            sg = _sig(g)
            dhs = dyv * (g * sg)
            r, i, a, mult = _gate_values(ga_s[pl.ds(r0, R), :], gx_s[pl.ds(r0, R), :], xc, sp8)
            b = jnp.where(row == R - 1, a_next, pltpu.roll(a, R - 1, 0))
            lv = dhs
            k = 1
            while k < R:
                m = row < R - k
                lv = jnp.where(m, lv + b * pltpu.roll(lv, R - k, 0), lv)
                b = jnp.where(m, b * pltpu.roll(b, R - k, 0), b)
                k *= 2
            lv = lv + b * lam_next
            p0 = pl.multiple_of(jnp.maximum(r0 - 8, 0), 8)
            hprev8 = jnp.where(ci > 0, hs_ref[pl.ds(p0, 8), :], 0.0)
            hprev = pltpu.roll(jnp.concatenate([hprev8, hv], axis=0), 1, 0)[8:8 + R, :]
            da = lv * hprev
            ixc = i * xc
            dmult = lv * ixc
            di = lv * mult * xc
            dxc_s[pl.ds(r0, R), :] = lv * mult * i
            a2 = a * a
            dlog_a = da * a - dmult * a2 / mult
            vecs_ref[32:40, :] += _fold8(dlog_a * r)
            dga = -(dlog_a * sp8) * r * (1.0 - r)
            dgx = di * i * (1.0 - i)
            ga_s[pl.ds(r0, R), :] = dga
            gx_s[pl.ds(r0, R), :] = dgx
            vecs_ref[16:24, :] += _fold8(dga)
            vecs_ref[24:32, :] += _fold8(dgx)
            a_first = jnp.sum(jnp.where(row == 0, a, 0.0), axis=0, keepdims=True)
            l_first = jnp.sum(jnp.where(row == 0, lv, 0.0), axis=0, keepdims=True)
            return a_first, l_first
        def scan_pass(i, cr):
            for sub in range(4):
                cr = scan_chunk(4 * i + sub, cr)
            return cr
        lax.fori_loop(0, nchunk // 4, scan_pass, (jnp.zeros((1, CB), F32), jnp.zeros((1, CB), F32)))

        dwa_ref[...] = jnp.zeros_like(dwa_ref)
        dwx_ref[...] = jnp.zeros_like(dwx_ref)

        def mm_chunk(ci, carry):
            r0 = pl.multiple_of(ci * TM, TM)
            xb = xc_ref[pl.ds(r0, TM), :].astype(BF16)
            dgab = ga_s[pl.ds(r0, TM), :].astype(BF16)
            dgxb = gx_s[pl.ds(r0, TM), :].astype(BF16)
            dxc_s[pl.ds(r0, TM), :] += (lax.dot_general(dgab, wa_ref[...], _NT, preferred_element_type=F32)
                                        + lax.dot_general(dgxb, wx_ref[...], _NT, preferred_element_type=F32))
            dwa_ref[...] += lax.dot_general(xb, dgab, _TN, preferred_element_type=F32)
            dwx_ref[...] += lax.dot_general(xb, dgxb, _TN, preferred_element_type=F32)
            return carry
        lax.fori_loop(0, TP // TM, mm_chunk, 0)

        taps = [cw_ref[k:k + 1, :] for k in range(LW)]

        def conv_chunk(ci, carry):
            r0 = pl.multiple_of(ci * RL, RL)
            dbuf = dxc_s[pl.ds(r0, RL + 8), :]
            dcur = dbuf[0:RL, :]
            p0 = pl.multiple_of(jnp.maximum(r0 - 8, 0), 8)
            xprev = jnp.where(ci > 0, x_ref[pl.ds(p0, 8), :], 0.0)
            xbuf = jnp.concatenate([xprev, x_ref[pl.ds(r0, RL), :]], axis=0)
            dxl = dcur * taps[LW - 1]
            dcw_ref[8 * (LW - 1):8 * LW, :] += _fold_rows(dcur * xbuf[8:8 + RL, :])
            for s in range(1, LW):
                k = LW - 1 - s
                dxl = dxl + pltpu.roll(dbuf, RL + 8 - s, 0)[0:RL, :] * taps[k]
                dcw_ref[8 * k:8 * k + 8, :] += _fold_rows(dcur * pltpu.roll(xbuf, s, 0)[8:8 + RL, :])
            dzl_ref[0, pl.ds(r0, RL), :] = dxl.astype(BF16)
            vecs_ref[8:16, :] += _fold_rows(dxl)
            vecs_ref[40:48, :] += _fold_rows(dcur)
            return carry
        lax.fori_loop(0, TP // RL, conv_chunk, 0)
        vecs_ref[32:40, :] = vecs_ref[32:40, :] * (LRU_C * _sig(-lam_ref[...]))

    col = lambda off: pl.BlockSpec((TP, CB), lambda j: (0, off + j))
    vec = pl.BlockSpec((1, CB), lambda j: (0, j))
    wsp = pl.BlockSpec((None, CB, CB), lambda j: (j, 0, 0))
    return pl.pallas_call(
        body, name="lru_bwd",
        grid=(NCB,),
        in_specs=[col(0), col(0), col(NCB), col(0), col(0), pl.BlockSpec((LW, CB), lambda j: (0, j)),
                  wsp, vec, wsp, vec, vec, _AFTER],
        out_specs=[pl.BlockSpec((1, TP, CB), lambda j: (0, 0, j)), wsp, wsp,
                   pl.BlockSpec((8 * LW, CB), lambda j: (0, j)), pl.BlockSpec((8 * NV, CB), lambda j: (0, j))],
        out_shape=[jax.ShapeDtypeStruct((1, TP, DL), BF16),
                   jax.ShapeDtypeStruct((NCB, CB, CB), F32), jax.ShapeDtypeStruct((NCB, CB, CB), F32),
                   jax.ShapeDtypeStruct((8 * LW, DL), F32), jax.ShapeDtypeStruct((8 * NV, DL), F32)],
        scratch_shapes=[pltpu.VMEM((TP, CB), F32), pltpu.VMEM((TP, CB), F32), pltpu.VMEM((TP + 8, CB), F32)],
        compiler_params=_cparams(),
    )(dycat, z, z, xc, hs, conv_w, wa_g, b_a, wx_g, b_x, lam, after)


def _dz_section(sec, dzl_ref, dz41_ref, dzc_ref, use):
    @pl.when(sec == 0)
    def _():
        use(dzl_ref)

    @pl.when(jnp.logical_or(sec == 1, sec == 4))
    def _():
        use(dz41_ref)

    @pl.when(jnp.logical_or(sec == 2, sec == 3))
    def _():
        use(dzc_ref)


def _dz_specs(rows, index):
    return [pl.BlockSpec((None, rows, 1024), lambda a, b: (0, index(a, b)[0], 0)),
            pl.BlockSpec((None, rows, 1024), lambda a, b: (jnp.where(index(a, b)[1] == 1, 1, 0), index(a, b)[0], 0)),
            pl.BlockSpec((None, rows, 1024), lambda a, b: (jnp.clip(index(a, b)[1] - 2, 0, 1), index(a, b)[0], 0))]


def _inproj_wgrad(name, hn, dzs, after):
    KB = 1024
    nsec = dzs.shape[0]

    def body(hn_ref, dz_ref, after_ref, dw_ref):
        dw_ref[...] = lax.dot_general(hn_ref[...], dz_ref[...], _TN, preferred_element_type=F32).astype(BF16)

    return pl.pallas_call(
        body, name=name,
        grid=(nsec, D // KB),
        in_specs=[pl.BlockSpec((TP, KB), lambda n, kb: (0, kb)),
                  pl.BlockSpec((None, TP, 1024), lambda n, kb: (n, 0, 0)), _AFTER],
        out_specs=pl.BlockSpec((KB, 1024), lambda n, kb: (kb, n)),
        out_shape=jax.ShapeDtypeStruct((D, nsec * 1024), BF16),
        compiler_params=_cparams(),
    )(hn, dzs, after)


def _sum_win_parts(parts_a, parts_b, parts_c):
    RB = 64

    def body(a_ref, b_ref, c_ref, o_ref):
        def chunk(ci, carry):
            r0 = pl.multiple_of(ci * R, R)
            for ref, src, base, ncol in ((a_ref, 0, 0, 1024), (c_ref, 1024, 1024, 1024), (b_ref, 0, 2048, 2048),
                                         (c_ref, 0, 4096, 1024)):
                for c0 in range(0, ncol, 512):
                    acc = ref[0, pl.ds(r0, R), src + c0:src + c0 + 512].astype(F32)
                    for sidx in range(1, NDEV):
                        acc = acc + ref[sidx, pl.ds(r0, R), src + c0:src + c0 + 512].astype(F32)
                    o_ref[pl.ds(r0, R), base + c0:base + c0 + 512] = acc.astype(BF16)
            return carry
        lax.fori_loop(0, RB // R, chunk, 0)

    spec = lambda ncol: pl.BlockSpec((NDEV, RB, ncol), lambda i: (0, i, 0))
    return pl.pallas_call(
        body, name="sum_win_parts",
        grid=(D // NDEV // RB,),
        in_specs=[spec(1024), spec(2048), spec(2048)],
        out_specs=pl.BlockSpec((RB, NIN), lambda i: (i, 0)),
        out_shape=jax.ShapeDtypeStruct((D // NDEV, NIN), BF16),
        compiler_params=_cparams(),
    )(parts_a, parts_b, parts_c)


def _inproj_bwd(dzl, dz41, dzc, w_in, h, dout, pre_w, after):
    nsec = NIN // 1024

    def body(dzl_ref, dz41_ref, dzc_ref, w_ref, h_ref, dout_ref, pw_ref, after_ref, gx_hbm, dmeta_ref, dpw_ref,
             acc_s, dh_s, sem):
        i = pl.program_id(0)
        s = pl.program_id(1)

        def gx_copy(t):
            lo, n, off = _tile_rows(t)
            return pltpu.make_async_copy(dh_s.at[pl.ds(off, n)], gx_hbm.at[pl.ds(lo, n)], sem)

        @pl.when(s == 0)
        def _():
            acc_s[...] = jnp.zeros_like(acc_s)

        def use(dz_ref):
            acc_s[...] += lax.dot_general(dz_ref[...], w_ref[...], _NT, preferred_element_type=F32)
        _dz_section(s, dzl_ref, dz41_ref, dzc_ref, use)

        @pl.when(jnp.logical_and(i == 0, s == nsec - 1))
        def _():
            dpw_ref[...] = jnp.zeros_like(dpw_ref)

        @pl.when(s == nsec - 1)
        def _():
            _for_tile(i - 1, lambda t: gx_copy(t).wait())
            pw = pw_ref[...]

            def chunk(ci, carry):
                r0 = pl.multiple_of(ci * 8, 8)
                hv = h_ref[pl.ds(r0, 8), :]
                dhn = acc_s[pl.ds(r0, 8), :]
                rs = lax.rsqrt(jnp.mean(hv * hv, axis=-1, keepdims=True) + EPS)
                dpw_ref[...] += dhn * (hv * rs)
                gw = dhn * pw
                dot = jnp.mean(gw * hv, axis=-1, keepdims=True)
                dh_s[pl.ds(r0, 8), :] = rs * gw - hv * (rs * rs * rs * dot) + dout_ref[pl.ds(r0, 8), :]
                return carry
            lax.fori_loop(0, TM // 8, chunk, 0, unroll=4)
            _for_tile(i, lambda t: gx_copy(t).start())

            @pl.when(i == 0)
            def _():
                dmeta_ref[...] = dh_s[0:NMETA, :]

            @pl.when(i == NTILE - 1)
            def _():
                gx_copy(NTILE - 1).wait()

    row = pl.BlockSpec((TM, D), lambda i, s: (i, 0))
    return pl.pallas_call(
        body, name="inproj_bwd",
        grid=(TP // TM, nsec),
        in_specs=_dz_specs(TM, lambda i, s: (i, s)) + [
            pl.BlockSpec((D, 1024), lambda i, s: (0, s)), row, row, pl.BlockSpec((1, D), lambda i, s: (0, 0)),
            _AFTER],
        out_specs=[pl.BlockSpec(memory_space=pl.ANY), pl.BlockSpec((NMETA, D), lambda i, s: (0, 0)),
                   pl.BlockSpec((8, D), lambda i, s: (0, 0))],
        out_shape=[jax.ShapeDtypeStruct((SEQ, D), F32), jax.ShapeDtypeStruct((NMETA, D), F32),
                   jax.ShapeDtypeStruct((8, D), F32)],
        scratch_shapes=[pltpu.VMEM((TM, D), F32), pltpu.VMEM((TM, D), F32), pltpu.SemaphoreType.DMA(())],
        compiler_params=_cparams(),
    )(dzl, dz41, dzc, w_in, h, dout, pre_w, after)


def _adamw(name, parts, w, m, v, block_rows):
    rows, cols = w.shape
    nparts = parts.shape[0]
    cw = cols if cols <= 640 else 512

    def body(p_ref, w_ref, m_ref, v_ref, g_ref, d_ref, nm_ref, nv_ref):
        def chunk(ci, carry):
            r0 = pl.multiple_of(ci * R, R)
            for c0 in range(0, cols, cw):
                at = (pl.ds(r0, R), slice(c0, c0 + cw))
                g = p_ref[(0,) + at].astype(F32)
                for sidx in range(1, nparts):
                    g = g + p_ref[(sidx,) + at].astype(F32)
                delta, mv, vv = _adam_math(g, w_ref[at], m_ref[at], v_ref[at])
                g_ref[at] = g
                nm_ref[at] = mv
                nv_ref[at] = vv
                d_ref[at] = delta
            return carry
        lax.fori_loop(0, block_rows // R, chunk, 0)

    blk = pl.BlockSpec((block_rows, cols), lambda i: (i, 0))
    shp = jax.ShapeDtypeStruct((rows, cols), F32)
    return pl.pallas_call(
        body, name=name,
        grid=(rows // block_rows,),
        in_specs=[pl.BlockSpec((nparts, block_rows, cols), lambda i: (0, i, 0)), blk, blk, blk],
        out_specs=[blk, blk, blk, blk],
        out_shape=[shp, shp, shp, shp],
        compiler_params=_cparams(),
    )(parts, w, m, v)


def _adam_math(g, w, m, v):
    c1 = 1.0 / (1.0 - ADAM_B1 ** ADAM_STEP)
    c2 = 1.0 / (1.0 - ADAM_B2 ** ADAM_STEP)
    mv = ADAM_B1 * m + (1.0 - ADAM_B1) * g
    vv = ADAM_B2 * v + (1.0 - ADAM_B2) * (g * g)
    upd = (mv * c1) / (jnp.sqrt(vv * c2) + ADAM_EPS) + ADAM_WD * w
    return -ADAM_LR * upd, mv, vv


_VEC = [("pre_norm_w", 2), ("post_norm_w", 2), ("b_in", 5), ("lru_conv_b", 1), ("b_gate_a", 1), ("b_gate_x", 1),
        ("lru_lambda", 1), ("conf_dw_b", 1), ("conf_ln_w", 1), ("conf_ln_b", 1), ("conf_pw_b", 1)]
_VEC_ROWS = 24
_LOSS_ROW = 17
_SM_ROWS = 64


def _pack_grads(dprew_acc, dpostw_acc, cvecs, kvecs, lvecs, dcw_acc, ddw_acc, dh, loss_acc):
    def body(pre_ref, post_ref, c_ref, k_ref, l_ref, dcw_ref, ddw_ref, dh_ref, loss_ref, vec_ref, small_ref, tmp):
        s8 = lambda ref, r: jnp.sum(ref[8 * r:8 * r + 8, :], axis=0, keepdims=True)
        vec_ref[...] = jnp.zeros_like(vec_ref)
        pre, post = s8(pre_ref, 0), s8(post_ref, 0)
        rows = [pre[:, 0:1024], pre[:, 1024:2048], post[:, 0:1024], post[:, 1024:2048],
                s8(l_ref, 1), s8(c_ref, 4), s8(k_ref, 1), s8(k_ref, 2), s8(c_ref, 1),
                s8(l_ref, 5), s8(l_ref, 2), s8(l_ref, 3), s8(l_ref, 4),
                s8(k_ref, 0), s8(c_ref, 2), s8(c_ref, 3), s8(c_ref, 0)]
        for r, val in enumerate(rows):
            vec_ref[r:r + 1, :] = val
        vec_ref[_LOSS_ROW:_LOSS_ROW + 1, :] = jnp.zeros((1, 1024), F32) + (0.5 / D) * jnp.sum(loss_ref[...])

        small_ref[...] = jnp.zeros_like(small_ref)
        for k in range(LW):
            tmp[k:k + 1, :] = s8(dcw_ref, k)
        for k in range(KW):
            tmp[8 + k:9 + k, :] = s8(ddw_ref, k)
        for d in range(NDEV):
            small_ref[d, 0:LW, 0:128] = tmp[0:LW, 128 * d:128 * d + 128]
            small_ref[d, 8:8 + KW, 0:128] = tmp[8:8 + KW, 128 * d:128 * d + 128]
            small_ref[d, 40:56, :] = dh_ref[:, 256 * d:256 * d + 256]

    full = lambda a: pl.BlockSpec(a.shape, lambda i: (0,) * a.ndim)
    ins = [dprew_acc, dpostw_acc, cvecs, kvecs, lvecs, dcw_acc, ddw_acc]
    return pl.pallas_call(
        body, name="pack_grads",
        grid=(1,),
        in_specs=[full(a) for a in ins] + [full(dh), full(loss_acc)],
        out_specs=[pl.BlockSpec((_VEC_ROWS, 1024), lambda i: (0, 0)),
                   pl.BlockSpec((NDEV, _SM_ROWS, 256), lambda i: (0, 0, 0))],
        out_shape=[jax.ShapeDtypeStruct((_VEC_ROWS, 1024), F32), jax.ShapeDtypeStruct((NDEV, _SM_ROWS, 256), F32)],
        scratch_shapes=[pltpu.VMEM((40, 1024), F32)],
        compiler_params=_cparams(),
    )(*ins, dh, loss_acc)


def _adamw_vec(parts, W, M, V):
    nv = len(_VEC)

    def body(*refs):
        p_ref = refs[0]
        w_refs, m_refs, v_refs = refs[1:1 + nv], refs[1 + nv:1 + 2 * nv], refs[1 + 2 * nv:1 + 3 * nv]
        outs = refs[1 + 3 * nv:]

        def total(r):
            acc = p_ref[0, r:r + 1, :]
            for sidx in range(1, NDEV):
                acc = acc + p_ref[sidx, r:r + 1, :]
            return acc

        row = 0
        for idx, (_, nrows) in enumerate(_VEC):
            for part in range(nrows):
                cols = slice(1024 * part, 1024 * part + 1024)
                g = total(row + part)
                delta, mv, vv = _adam_math(g, w_refs[idx][:, cols], m_refs[idx][:, cols], v_refs[idx][:, cols])
                for o, val in zip(outs[4 * idx:4 * idx + 4], (g, delta, mv, vv)):
                    o[:, cols] = val
            row += nrows
        outs[-1][...] = total(_LOSS_ROW)[:, 0:128]

    names = [n for n, _ in _VEC]
    flat = lambda d: [d[n].reshape(1, -1) for n in names]
    ws, ms, vs = flat(W), flat(M), flat(V)
    res = pl.pallas_call(
        body, name="adamw_vec",
        out_shape=[jax.ShapeDtypeStruct(w.shape, F32) for w in ws for _ in range(4)]
        + [jax.ShapeDtypeStruct((1, 128), F32)],
        compiler_params=_cparams(),
    )(parts, *ws, *ms, *vs)
    return {n: tuple(res[4 * i:4 * i + 4]) for i, n in enumerate(names)}, res[-1]


def _adamw_small(parts, W, M, V):
    where = {"lru_conv_w": (slice(0, LW), slice(0, 128)), "conf_dw_w": (slice(8, 8 + KW), slice(0, 128)),
             "meta_tokens": (slice(40, 56), slice(0, 256))}
    names = list(where)

    def body(*refs):
        p_ref = refs[0]
        outs = refs[10:]
        for idx, n in enumerate(names):
            rs, cs = where[n]
            g = p_ref[0, rs, cs]
            for sidx in range(1, NDEV):
                g = g + p_ref[sidx, rs, cs]
            delta, mv, vv = _adam_math(g, refs[1 + idx][...], refs[4 + idx][...], refs[7 + idx][...])
            for o, val in zip(outs[4 * idx:4 * idx + 4], (g, delta, mv, vv)):
                o[...] = val

    two_d = lambda a: a.reshape(a.shape[-2:])
    ws, ms, vs = ([two_d(d[n]) for n in names] for d in (W, M, V))
    res = pl.pallas_call(
        body, name="adamw_small",
        out_shape=[jax.ShapeDtypeStruct(w.shape, F32) for w in ws for _ in range(4)],
        compiler_params=_cparams(),
    )(parts, *ws, *ms, *vs)
    return {n: tuple(res[4 * i:4 * i + 4]) for i, n in enumerate(names)}


def _pack_small(lru_cw, dw_w, meta):
    buf = jnp.zeros((_SM_ROWS, 256), F32)
    buf = buf.at[0:LW, 0:128].set(lru_cw)
    buf = buf.at[8:8 + dw_w.shape[0], 0:128].set(dw_w)
    return buf.at[40:56, :].set(meta)


def _block_diag4(w):
    w4 = w.reshape(NCB, 4, 64, 64)
    eye = jnp.eye(4, dtype=w.dtype)
    return jnp.einsum("ghij,hk->ghikj", w4, eye).reshape(NCB, CB, CB)


def _diag_blocks(g):
    g5 = g.reshape(NCB, 4, 64, 4, 64)
    return jnp.stack([g5[:, hh, :, hh, :] for hh in range(4)], axis=1).reshape(16, 64, 64)


def _gate_mats(W):
    return _block_diag4(W["w_gate_a"][0]).astype(BF16), _block_diag4(W["w_gate_x"][0]).astype(BF16)


def _local_step(x, target, meta_full, inproj, out_weights, lru_cw_full, dw_w_full, W, gate_mats, send):
    wa_g, wx_g = gate_mats

    h, hn = _prenorm(x, meta_full, W["pre_norm_w"])
    z, win_full = inproj(hn)
    ylru, xc, hs = _lru_fwd(z, lru_cw_full, W["lru_conv_b"], wa_g, W["b_gate_a"], wx_g, W["b_gate_x"],
                            W["lru_lambda"])
    vc = _conf_fwd_conv(z, dw_w_full, W["conf_dw_b"])
    wout_full, pw_full = out_weights(vc)
    yconf, p, xhat, rstd = _conf_fwd_proj(vc, z, W["conf_ln_w"], W["conf_ln_b"], pw_full, W["conf_pw_b"])
    dout, dy, loss_acc, dpostw_acc = _outproj_loss(ylru, yconf, wout_full, h, target, W["post_norm_w"])

    dycat, dwout_part = _outproj_bwd(dy, ylru, yconf, wout_full)
    tok = send("w_out", ("w_out", dwout_part))
    dvc, dz41, dpw_part, cvecs = _conf_bwd_proj(dycat, p, z, xhat, rstd, hs, W["conf_ln_w"], W["conf_ln_b"], pw_full, tok)
    tok = send("w_in_c", ("conf_pw_w", dpw_part), ("w_in_c", _inproj_wgrad("inproj_wgrad_c", hn, dz41, dz41)))
    dzc, ddw_acc, kvecs = _conf_bwd_conv(dvc, z, dw_w_full, tok)
    tok = send("w_in_b", ("w_in_b", _inproj_wgrad("inproj_wgrad_b", hn, dzc, dzc)))
    dzl, dwa_g, dwx_g, dcw_acc, lvecs = _lru_bwd(dycat, z, xc, hs, lru_cw_full, wa_g, W["b_gate_a"], wx_g,
                                                 W["b_gate_x"], W["lru_lambda"], tok)
    tok = send("w_gates", ("w_gate_a", _diag_blocks(dwa_g).reshape(16 * 64, 64)),
               ("w_gate_x", _diag_blocks(dwx_g).reshape(16 * 64, 64)))
    tok = send("w_in_a", ("w_in_a", _inproj_wgrad("inproj_wgrad_a", hn, dzl, tok)))
    grad_x, dmeta, dprew_acc = _inproj_bwd(dzl, dz41, dzc, win_full, h, dout, W["pre_norm_w"], tok)

    vec_pack, small_part = _pack_grads(dprew_acc, dpostw_acc, cvecs, kvecs, lvecs, dcw_acc, ddw_acc, dmeta, loss_acc)
    return grad_x, vec_pack, small_part


def kernel(x, meta_tokens, pre_norm_w, post_norm_w, w_in, b_in, lru_conv_w, lru_conv_b, w_gate_a, b_gate_a, w_gate_x, b_gate_x, lru_lambda, conf_dw_w, conf_dw_b, conf_ln_w, conf_ln_b, conf_pw_w, conf_pw_b, w_out, loss_target, m_meta_tokens, m_pre_norm_w, m_post_norm_w, m_w_in, m_b_in, m_lru_conv_w, m_lru_conv_b, m_w_gate_a, m_b_gate_a, m_w_gate_x, m_b_gate_x, m_lru_lambda, m_conf_dw_w, m_conf_dw_b, m_conf_ln_w, m_conf_ln_b, m_conf_pw_w, m_conf_pw_b, m_w_out, v_meta_tokens, v_pre_norm_w, v_post_norm_w, v_w_in, v_b_in, v_lru_conv_w, v_lru_conv_b, v_w_gate_a, v_b_gate_a, v_w_gate_x, v_b_gate_x, v_lru_lambda, v_conf_dw_w, v_conf_dw_b, v_conf_ln_w, v_conf_ln_b, v_conf_pw_w, v_conf_pw_b, v_w_out):
    W = dict(meta_tokens=meta_tokens, pre_norm_w=pre_norm_w, post_norm_w=post_norm_w, w_in=w_in, b_in=b_in,
             lru_conv_w=lru_conv_w, lru_conv_b=lru_conv_b, w_gate_a=w_gate_a, b_gate_a=b_gate_a,
             w_gate_x=w_gate_x, b_gate_x=b_gate_x, lru_lambda=lru_lambda, conf_dw_w=conf_dw_w,
             conf_dw_b=conf_dw_b, conf_ln_w=conf_ln_w, conf_ln_b=conf_ln_b, conf_pw_w=conf_pw_w,
             conf_pw_b=conf_pw_b, w_out=w_out)
    M = dict(meta_tokens=m_meta_tokens, pre_norm_w=m_pre_norm_w, post_norm_w=m_post_norm_w, w_in=m_w_in,
             b_in=m_b_in, lru_conv_w=m_lru_conv_w, lru_conv_b=m_lru_conv_b, w_gate_a=m_w_gate_a,
             b_gate_a=m_b_gate_a, w_gate_x=m_w_gate_x, b_gate_x=m_b_gate_x, lru_lambda=m_lru_lambda,
             conf_dw_w=m_conf_dw_w, conf_dw_b=m_conf_dw_b, conf_ln_w=m_conf_ln_w, conf_ln_b=m_conf_ln_b,
             conf_pw_w=m_conf_pw_w, conf_pw_b=m_conf_pw_b, w_out=m_w_out)
    V = dict(meta_tokens=v_meta_tokens, pre_norm_w=v_pre_norm_w, post_norm_w=v_post_norm_w, w_in=v_w_in,
             b_in=v_b_in, lru_conv_w=v_lru_conv_w, lru_conv_b=v_lru_conv_b, w_gate_a=v_w_gate_a,
             b_gate_a=v_b_gate_a, w_gate_x=v_w_gate_x, b_gate_x=v_b_gate_x, lru_lambda=v_lru_lambda,
             conf_dw_w=v_conf_dw_w, conf_dw_b=v_conf_dw_b, conf_ln_w=v_conf_ln_w, conf_ln_b=v_conf_ln_b,
             conf_pw_w=v_conf_pw_w, conf_pw_b=v_conf_pw_b, w_out=v_w_out)
    names = list(W.keys())
    shapes = {n: W[n].shape for n in names}

    small = _pack_small(lru_conv_w[0], conf_dw_w[0], meta_tokens)
    (small_flight,), tok = _exchange_start("gather_small_start", [
        (small, jax.ShapeDtypeStruct((NDEV, _SM_ROWS, 256), F32), _whole, _slot)])
    win_flight, tok = _win_gather_start(w_in[0].astype(BF16) + tok[0, 0].astype(BF16))
    win_flight, tok = _win_gather_links(win_flight, tok)
    gate_mats = _gate_mats(W)
    gathered, tok = _exchange_start("gather_out_start", [
        (w_out[0].astype(BF16) + tok[0, 0].astype(BF16), jax.ShapeDtypeStruct((D, D), BF16), _whole,
         _rows(D // NDEV)),
        (conf_pw_w[0].astype(BF16), jax.ShapeDtypeStruct((DC, DC), BF16), _whole, _rows(DC // NDEV)),
    ])
    (small_all,) = _exchange_wait("gather_small_wait", [small_flight], tok)
    unshard = lambda a: jnp.transpose(a, (1, 0, 2)).reshape(a.shape[1], -1)
    lru_cw_full = unshard(small_all[:, 0:LW, 0:128])
    dw_w_full = unshard(small_all[:, 8:8 + KWP, 0:128])
    meta_full = unshard(small_all[:, 40:56, :])

    def out_weights(after):
        return _exchange_wait("gather_out_wait", gathered, after)

    def inproj(hn):
        xi, yi, ci = lax.axis_index("x"), lax.axis_index("y"), lax.axis_index("c")
        shard = lambda px, py, pc: (4 * px + 2 * py + pc).astype(jnp.int32)
        over_links = jnp.stack([shard(1 - xi, yi, ci), shard(xi, 1 - yi, ci), shard(1 - xi, 1 - yi, ci)])
        z, src = _inproj_cols("inproj_own", jnp.stack([shard(xi, yi, ci)]), hn, win_flight["src"], b_in, None)
        flight = _win_gather_early(dict(win_flight, src=src))
        z, land = _inproj_cols("inproj_here", jnp.stack([shard(xi, yi, 1 - ci)]), hn, flight["land"], b_in, z)
        flight = _win_gather_forward("all", dict(flight, land=land), (1, 2, 3), z)
        z, land = _inproj_cols("inproj_links", over_links, hn, flight["land"], b_in, z)
        flight = _win_gather_forwarded("all", dict(flight, land=land), (1, 2, 3))
        z, land = _inproj_cols("inproj_sibling", over_links + 1 - 2 * ci, hn, flight["land"], b_in, z)
        return z, _win_gather_wait(dict(flight, land=land))

    row_stage = lambda ncol: (jax.ShapeDtypeStruct((NDEV, D // NDEV, ncol), BF16), _rows(D // NDEV))
    piece = {"w_in_a": row_stage(1024), "w_in_b": row_stage(2048), "w_in_c": row_stage(2048),
             "w_out": row_stage(D),
             "conf_pw_w": (jax.ShapeDtypeStruct((NDEV, DC // NDEV, DC), BF16), _rows(DC // NDEV)),
             "w_gate_a": (jax.ShapeDtypeStruct((NDEV, 16 * 64, 64), BF16), _whole),
             "w_gate_x": (jax.ShapeDtypeStruct((NDEV, 16 * 64, 64), BF16), _whole)}
    sent = {}

    def send(call, *named_parts):
        handles, token = _exchange_start(
            "scatter_" + call + "_start",
            [(part.astype(BF16), piece[name][0], piece[name][1], _slot) for name, part in named_parts])
        for (name, _), handle in zip(named_parts, handles):
            sent[name] = [handle]
        return token

    grad_x, vec_pack, small_part = _local_step(
        x[0], loss_target[0], meta_full, inproj, out_weights, lru_cw_full, dw_w_full, W, gate_mats, send)
    grad_x = grad_x[None]

    rest, tok = _exchange_start("scatter_rest_start", [
        (small_part, jax.ShapeDtypeStruct((NDEV, _SM_ROWS, 256), F32), _slot, _slot),
        (vec_pack, jax.ShapeDtypeStruct((NDEV, _VEC_ROWS, 1024), F32), _whole, _slot),
    ])
    (parts_c,) = _exchange_wait("scatter_w_in_c_wait", sent["w_in_c"], tok)
    (parts_b,) = _exchange_wait("scatter_w_in_b_wait", sent["w_in_b"], parts_c)
    (parts_a,) = _exchange_wait("scatter_w_in_a_wait", sent["w_in_a"], parts_b)
    win_rows = _sum_win_parts(parts_a, parts_b, parts_c)
    win_stage2, tok = _exchange_start("scatter_w_in_stage2_start", [
        (win_rows, jax.ShapeDtypeStruct((NDEV, D // NDEV, NIN // NDEV), BF16), _cols(NIN // NDEV), _slot)])

    G, DW, NM, NV = {}, {}, {}, {}
    (wout_parts,) = _exchange_wait("scatter_w_out_wait", sent["w_out"], tok)
    G["w_out"], DW["w_out"], NM["w_out"], NV["w_out"] = _adamw("adamw_w_out", wout_parts, w_out[0], m_w_out[0], v_w_out[0], 64)
    (pw_parts,) = _exchange_wait("scatter_conf_pw_w_wait", sent["conf_pw_w"], G["w_out"])
    G["conf_pw_w"], DW["conf_pw_w"], NM["conf_pw_w"], NV["conf_pw_w"] = _adamw(
        "adamw_pw", pw_parts, conf_pw_w[0], m_conf_pw_w[0], v_conf_pw_w[0], 128)
    res = {}
    wa_parts, wx_parts = _exchange_wait("scatter_w_gates_wait", sent["w_gate_a"] + sent["w_gate_x"], G["conf_pw_w"])
    for n, parts in (("w_gate_a", wa_parts), ("w_gate_x", wx_parts)):
        res[n] = _adamw("adamw_" + n, parts, *[d[n].reshape(16 * 64, 64) for d in (W, M, V)], 16 * 64)
    small_parts, vec_parts = _exchange_wait("scatter_rest_wait", rest, res["w_gate_x"][0])
    res.update(_adamw_small(small_parts, W, M, V))
    vec_res, loss_row = _adamw_vec(vec_parts, W, M, V)
    res.update(vec_res)
    (win_sum,) = _exchange_wait("scatter_w_in_stage2_wait", win_stage2, loss_row)
    res["w_in"] = _adamw("adamw_w_in", win_sum.reshape(1, D, NIN // NDEV), w_in[0], m_w_in[0], v_w_in[0], 256)
    for n, vals in res.items():
        for dst, val in zip((G, DW, NM, NV), vals):
            dst[n] = val
    for dst in (G, DW, NM, NV):
        for n in names:
            dst[n] = dst[n].reshape(shapes[n])
    loss = loss_row[0, 0]

    return (loss, grad_x, *[G[n] for n in names], *[DW[n] for n in names],
            *[NM[n] for n in names], *[NV[n] for n in names])
```
